```python
import math
import jax, jax.numpy as jnp
from jax import lax
import numpy as np


D_MODEL = 1024
BATCH = 8
SEQ = 4096
DEPTH = 1

CTX_LEN = 256
GRID_W = 64
RET_WIDTH = 512
LRU_WIDTH = 512
MIX_WIDTH = RET_WIDTH + LRU_WIDTH
RET_HEADS = 4
RET_HEAD_DIM = RET_WIDTH // RET_HEADS
RET_CHUNK = 128
LRU_BLOCKS = 8
LRU_BLOCK_DIM = LRU_WIDTH // LRU_BLOCKS
LRU_C = 8.0
CONV_WIDTH = 4
MLP_HIDDEN = 4 * D_MODEL
ROPE_BASE = 10000.0
NORM_EPS = 1e-6
N_MOD = 6
IN_COLS = 4 * RET_WIDTH + 2 * LRU_WIDTH
IN_SPLITS = (RET_WIDTH, 2 * RET_WIDTH, 3 * RET_WIDTH, 4 * RET_WIDTH, 4 * RET_WIDTH + LRU_WIDTH)

kernel_name = "hybrid_retention_rglru_dit_block"


def rmsnorm(x, g):
    xf = x.astype(jnp.float32)
    y = xf * lax.rsqrt(jnp.mean(xf * xf, axis=-1, keepdims=True) + NORM_EPS)
    return (y * g.astype(jnp.float32)).astype(x.dtype)


def modulate(h, shift, scale):
    return h * (1.0 + scale) + shift


def to_heads(a):
    b, t, _ = a.shape
    return a.reshape(b, t, RET_HEADS, RET_HEAD_DIM).astype(jnp.float32)


def head_groupnorm(y):
    mu = jnp.mean(y, axis=-1, keepdims=True)
    yc = y - mu
    var = jnp.mean(yc * yc, axis=-1, keepdims=True)
    return yc * lax.rsqrt(var + NORM_EPS)


def axial_rotary_tables(t_len):
    rows = t_len // GRID_W
    row = jnp.repeat(jnp.arange(rows, dtype=jnp.float32), GRID_W)
    col = jnp.tile(jnp.arange(GRID_W, dtype=jnp.float32), rows)
    n_freq = RET_HEAD_DIM // 4
    inv = ROPE_BASE ** (-jnp.arange(n_freq, dtype=jnp.float32) / n_freq)
    ang = jnp.concatenate([row[:, None] * inv, col[:, None] * inv], axis=-1)
    return jnp.cos(ang), jnp.sin(ang)


def apply_rotary(x, cos, sin):
    half = RET_HEAD_DIM // 2
    x1, x2 = x[..., :half], x[..., half:]
    c = cos[None, :, None, :]
    s = sin[None, :, None, :]
    return jnp.concatenate([x1 * c - x2 * s, x2 * c + x1 * s], axis=-1)


def retention_scan(q, k, v, log_g, s0):
    b, t, h, dh = q.shape
    n = t // RET_CHUNK

    def chunks(a):
        return a.reshape(b, n, RET_CHUNK, h, dh).transpose(0, 3, 1, 2, 4)

    qc, kc, vc = chunks(q), chunks(k), chunks(v)
    pos = jnp.arange(RET_CHUNK, dtype=jnp.float32)
    lg = log_g[:, None]
    rel = pos[:, None] - pos[None, :]
    decay = jnp.where(rel >= 0, jnp.exp(lg[:, :, None] * jnp.maximum(rel, 0.0)), 0.0)
    scores = jnp.einsum('bhncd,bhnmd->bhncm', qc, kc) * decay[None, :, None]
    intra = jnp.einsum('bhncm,bhnme->bhnce', scores, vc)
    w_state = jnp.exp(lg * (RET_CHUNK - 1.0 - pos))
    u = jnp.einsum('bhncd,bhnce->bhnde', kc * w_state[None, :, None, :, None], vc)
    g_chunk = jnp.exp(lg * RET_CHUNK)[None, :, :, None]

    def step(s, u_n):
        return g_chunk * s + u_n, s

    _, s_prev = lax.scan(step, s0, u.transpose(2, 0, 1, 3, 4))
    s_prev = s_prev.transpose(1, 2, 0, 3, 4)
    w_query = jnp.exp(lg * (pos + 1.0))
    cross = jnp.einsum('bhncd,bhnde->bhnce', qc * w_query[None, :, None, :, None], s_prev)
    return (intra + cross).transpose(0, 2, 3, 1, 4).reshape(b, t, h, dh)


def retention_bidir(q, k, v, log_g_f, log_g_b, s0_f, s0_b):
    o_f = retention_scan(q, k, v, log_g_f, s0_f)
    o_b = retention_scan(q[:, ::-1], k[:, ::-1], v[:, ::-1], log_g_b, s0_b)[:, ::-1]
    return o_f + o_b


def retention_final_state(k, v, log_g, reverse):
    l_len = k.shape[1]
    pos = jnp.arange(l_len, dtype=jnp.float32)
    steps = pos if reverse else (l_len - 1.0 - pos)
    w = jnp.exp(log_g[:, None] * steps[None, :])
    return jnp.einsum('blhd,blhe->bhde', k * w.T[None, :, :, None], v)


def centred_depthwise_conv(x, w, bias):
    t_len = x.shape[1]
    left = (CONV_WIDTH - 1) // 2
    right = CONV_WIDTH - 1 - left
    xp = jnp.pad(x, ((0, 0), (left, right), (0, 0)))
    out = bias + xp[:, 0:t_len] * w[0]
    for j in range(1, CONV_WIDTH):
        out = out + xp[:, j:j + t_len] * w[j]
    return out


def linear_scan(a, b, h0, reverse):
    if reverse:
        a, b = a[:, ::-1], b[:, ::-1]
    b = b.at[:, 0].add(a[:, 0] * h0)

    def combine(lhs, rhs):
        return lhs[0] * rhs[0], rhs[0] * lhs[1] + rhs[1]

    _, h = lax.associative_scan(combine, (a, b), axis=1)
    final = h[:, -1]
    if reverse:
        h = h[:, ::-1]
    return h, final


def rglru_direction(xc, w_a, b_a, w_x, b_x, lam, h0, reverse):
    b, t, _ = xc.shape
    xf = xc.astype(jnp.float32)
    xb = xf.reshape(b, t, LRU_BLOCKS, LRU_BLOCK_DIM)
    r = jax.nn.sigmoid(jnp.einsum('btnd,nde->btne', xb, w_a.astype(jnp.float32)).reshape(b, t, LRU_WIDTH) + b_a.astype(jnp.float32))
    i = jax.nn.sigmoid(jnp.einsum('btnd,nde->btne', xb, w_x.astype(jnp.float32)).reshape(b, t, LRU_WIDTH) + b_x.astype(jnp.float32))
    log_a = -LRU_C * r * jax.nn.softplus(-lam.astype(jnp.float32))
    a = jnp.exp(log_a)
    inp = jnp.sqrt(-jnp.expm1(2.0 * log_a)) * (i * xf)
    return linear_scan(a, inp, h0, reverse)


def mix_output(o_ret, g, h_lru, gate, w_out):
    b, t = g.shape[0], g.shape[1]
    ret = jax.nn.silu(g.astype(jnp.float32)) * head_groupnorm(o_ret).reshape(b, t, RET_WIDTH)
    lru = h_lru * jax.nn.gelu(gate.astype(jnp.float32))
    return jnp.concatenate([ret, lru], axis=-1).astype(w_out.dtype) @ w_out


def squared_relu_mlp(h, w1, w2):
    return jnp.square(jax.nn.relu(h @ w1)) @ w2


def _fwd_setup_inputs(seed: int = 0) -> dict:
    key = jax.random.key(seed)
    ks = jax.random.split(key, 24)
    f32 = jnp.float32

    def nrm(k, shape, scale):
        return jax.random.normal(k, shape, f32) * scale

    x = nrm(ks[0], (BATCH, SEQ, D_MODEL), 1.0)
    c = nrm(ks[1], (BATCH, D_MODEL), 1.0)
    ctx = nrm(ks[2], (BATCH, CTX_LEN, D_MODEL), 1.0)
    c_ctx = nrm(ks[3], (D_MODEL,), 1.0)
    w_ada = nrm(ks[4], (DEPTH, D_MODEL, N_MOD * D_MODEL), 0.5 * D_MODEL ** -0.5)
    b_ada = nrm(ks[5], (DEPTH, N_MOD * D_MODEL), 0.01)
    norm1_g = 1.0 + nrm(ks[6], (DEPTH, D_MODEL), 0.02)
    norm2_g = 1.0 + nrm(ks[7], (DEPTH, D_MODEL), 0.02)
    w_in = nrm(ks[8], (DEPTH, D_MODEL, IN_COLS), D_MODEL ** -0.5)
    gamma = 1.0 - 2.0 ** (-5.0 - jnp.arange(RET_HEADS, dtype=f32))
    ret_decay = jnp.log(gamma) - jnp.log1p(-gamma) + nrm(ks[9], (DEPTH, 2, RET_HEADS), 0.05)
    conv_w = nrm(ks[10], (DEPTH, CONV_WIDTH, LRU_WIDTH), CONV_WIDTH ** -0.5)
    conv_b = nrm(ks[11], (DEPTH, LRU_WIDTH), 0.01)
    lru_wa = nrm(ks[12], (DEPTH, 2, LRU_BLOCKS, LRU_BLOCK_DIM, LRU_BLOCK_DIM), LRU_BLOCK_DIM ** -0.5)
    lru_ba = nrm(ks[13], (DEPTH, 2, LRU_WIDTH), 0.01)
    lru_wx = nrm(ks[14], (DEPTH, 2, LRU_BLOCKS, LRU_BLOCK_DIM, LRU_BLOCK_DIM), LRU_BLOCK_DIM ** -0.5)
    lru_bx = nrm(ks[15], (DEPTH, 2, LRU_WIDTH), 0.01)
    u = jax.random.uniform(ks[16], (DEPTH, 2, LRU_WIDTH), f32, 0.9, 0.999)
    a0 = u ** (1.0 / LRU_C)
    lru_lambda = jnp.log(a0) - jnp.log1p(-a0)
    w_out = nrm(ks[17], (DEPTH, MIX_WIDTH, D_MODEL), MIX_WIDTH ** -0.5)
    w_mlp1 = nrm(ks[18], (DEPTH, D_MODEL, MLP_HIDDEN), D_MODEL ** -0.5)
    w_mlp2 = nrm(ks[19], (DEPTH, MLP_HIDDEN, D_MODEL), MLP_HIDDEN ** -0.5)
    final_g = 1.0 + nrm(ks[20], (D_MODEL,), 0.02)
    return dict(x=x, c=c, ctx=ctx, c_ctx=c_ctx, w_ada=w_ada, b_ada=b_ada, norm1_g=norm1_g,
                norm2_g=norm2_g, w_in=w_in, ret_decay=ret_decay, conv_w=conv_w, conv_b=conv_b,
                lru_wa=lru_wa, lru_ba=lru_ba, lru_wx=lru_wx, lru_bx=lru_bx, lru_lambda=lru_lambda,
                w_out=w_out, w_mlp1=w_mlp1, w_mlp2=w_mlp2, final_g=final_g)


def _fwd_reference(x, c, ctx, c_ctx, w_ada, b_ada, norm1_g, norm2_g, w_in, ret_decay, conv_w, conv_b,
              lru_wa, lru_ba, lru_wx, lru_bx, lru_lambda, w_out, w_mlp1, w_mlp2, final_g):
    b, t_len, _ = x.shape
    cos, sin = axial_rotary_tables(t_len)
    k_scale = RET_HEAD_DIM ** -0.5
    silu_c = jax.nn.silu(c)
    silu_cc = jax.nn.silu(c_ctx)
    for l in range(DEPTH):
        last = l == DEPTH - 1
        mod = silu_c @ w_ada[l] + b_ada[l]
        mod_c = silu_cc @ w_ada[l] + b_ada[l]
        sh1, sc1, g1, sh2, sc2, g2 = [m[:, None] for m in jnp.split(mod, N_MOD, axis=-1)]
        csh1, csc1, cg1, csh2, csc2, cg2 = jnp.split(mod_c, N_MOD, axis=-1)
        w_q, w_k, w_v, w_g, w_x, w_gate = jnp.split(w_in[l], IN_SPLITS, axis=1)
        lg_f = jax.nn.log_sigmoid(ret_decay[l, 0].astype(jnp.float32))
        lg_b = jax.nn.log_sigmoid(ret_decay[l, 1].astype(jnp.float32))
        lru_f = (lru_wa[l, 0], lru_ba[l, 0], lru_wx[l, 0], lru_bx[l, 0], lru_lambda[l, 0])
        lru_b = (lru_wa[l, 1], lru_ba[l, 1], lru_wx[l, 1], lru_bx[l, 1], lru_lambda[l, 1])

        hc = modulate(rmsnorm(ctx, norm1_g[l]), csh1, csc1)
        kc = to_heads(hc @ w_k) * k_scale
        vc = to_heads(hc @ w_v)
        s_f = retention_final_state(kc, vc, lg_f, False)
        s_b = retention_final_state(kc, vc, lg_b, True)
        xcc = centred_depthwise_conv(hc @ w_x, conv_w[l], conv_b[l])
        zero_h = jnp.zeros((b, LRU_WIDTH), jnp.float32)
        hcf, lru_sf = rglru_direction(xcc, *lru_f, zero_h, False)
        hcb, lru_sb = rglru_direction(xcc, *lru_b, zero_h, True)
        if not last:
            qc = to_heads(hc @ w_q)
            zero_s = jnp.zeros((b, RET_HEADS, RET_HEAD_DIM, RET_HEAD_DIM), jnp.float32)
            o_c = retention_bidir(qc, kc, vc, lg_f, lg_b, zero_s, zero_s)
            yc = mix_output(o_c, hc @ w_g, hcf + hcb, hc @ w_gate, w_out[l])
            ctx_next = ctx + cg1 * yc
            hc2 = modulate(rmsnorm(ctx_next, norm2_g[l]), csh2, csc2)
            ctx_next = ctx_next + cg2 * squared_relu_mlp(hc2, w_mlp1[l], w_mlp2[l])

        h = modulate(rmsnorm(x, norm1_g[l]), sh1, sc1)
        q, k, v, g, xr, gate = jnp.split(h @ w_in[l], IN_SPLITS, axis=-1)
        q = apply_rotary(to_heads(q), cos, sin)
        k = apply_rotary(to_heads(k), cos, sin) * k_scale
        o = retention_bidir(q, k, to_heads(v), lg_f, lg_b, s_f, s_b)
        xcl = centred_depthwise_conv(xr, conv_w[l], conv_b[l])
        hf, _ = rglru_direction(xcl, *lru_f, lru_sf, False)
        hb, _ = rglru_direction(xcl, *lru_b, lru_sb, True)
        y = mix_output(o, g, hf + hb, gate, w_out[l])
        x = x + g1 * y
        h2 = modulate(rmsnorm(x, norm2_g[l]), sh2, sc2)
        x = x + g2 * squared_relu_mlp(h2, w_mlp1[l], w_mlp2[l])
        if not last:
            ctx = ctx_next
    return rmsnorm(x, final_g)


import jax as _jax
import jax.numpy as _jnp

TWIN_FORMAT = 'train_step'
FWD_PARAMS = ['x', 'c', 'ctx', 'c_ctx', 'w_ada', 'b_ada', 'norm1_g', 'norm2_g', 'w_in', 'ret_decay', 'conv_w', 'conv_b', 'lru_wa', 'lru_ba', 'lru_wx', 'lru_bx', 'lru_lambda', 'w_out', 'w_mlp1', 'w_mlp2', 'final_g']
TWIN_WEIGHTS = ['c_ctx', 'w_ada', 'b_ada', 'norm1_g', 'norm2_g', 'w_in', 'ret_decay', 'conv_w', 'conv_b', 'lru_wa', 'lru_ba', 'lru_wx', 'lru_bx', 'lru_lambda', 'w_out', 'w_mlp1', 'w_mlp2', 'final_g']
TWIN_DIFF_INPUT = 'x'
TWIN_INPUTS = ['x', 'c', 'ctx', 'c_ctx', 'w_ada', 'b_ada', 'norm1_g', 'norm2_g', 'w_in', 'ret_decay', 'conv_w', 'conv_b', 'lru_wa', 'lru_ba', 'lru_wx', 'lru_bx', 'lru_lambda', 'w_out', 'w_mlp1', 'w_mlp2', 'final_g', 'loss_target', 'm_c_ctx', 'm_w_ada', 'm_b_ada', 'm_norm1_g', 'm_norm2_g', 'm_w_in', 'm_ret_decay', 'm_conv_w', 'm_conv_b', 'm_lru_wa', 'm_lru_ba', 'm_lru_wx', 'm_lru_bx', 'm_lru_lambda', 'm_w_out', 'm_w_mlp1', 'm_w_mlp2', 'm_final_g', 'v_c_ctx', 'v_w_ada', 'v_b_ada', 'v_norm1_g', 'v_norm2_g', 'v_w_in', 'v_ret_decay', 'v_conv_w', 'v_conv_b', 'v_lru_wa', 'v_lru_ba', 'v_lru_wx', 'v_lru_bx', 'v_lru_lambda', 'v_w_out', 'v_w_mlp1', 'v_w_mlp2', 'v_final_g']
TWIN_OUTPUTS = ['loss', 'grad_x', 'grad_c_ctx', 'grad_w_ada', 'grad_b_ada', 'grad_norm1_g', 'grad_norm2_g', 'grad_w_in', 'grad_ret_decay', 'grad_conv_w', 'grad_conv_b', 'grad_lru_wa', 'grad_lru_ba', 'grad_lru_wx', 'grad_lru_bx', 'grad_lru_lambda', 'grad_w_out', 'grad_w_mlp1', 'grad_w_mlp2', 'grad_final_g', 'delta_c_ctx', 'delta_w_ada', 'delta_b_ada', 'delta_norm1_g', 'delta_norm2_g', 'delta_w_in', 'delta_ret_decay', 'delta_conv_w', 'delta_conv_b', 'delta_lru_wa', 'delta_lru_ba', 'delta_lru_wx', 'delta_lru_bx', 'delta_lru_lambda', 'delta_w_out', 'delta_w_mlp1', 'delta_w_mlp2', 'delta_final_g', 'new_m_c_ctx', 'new_m_w_ada', 'new_m_b_ada', 'new_m_norm1_g', 'new_m_norm2_g', 'new_m_w_in', 'new_m_ret_decay', 'new_m_conv_w', 'new_m_conv_b', 'new_m_lru_wa', 'new_m_lru_ba', 'new_m_lru_wx', 'new_m_lru_bx', 'new_m_lru_lambda', 'new_m_w_out', 'new_m_w_mlp1', 'new_m_w_mlp2', 'new_m_final_g', 'new_v_c_ctx', 'new_v_w_ada', 'new_v_b_ada', 'new_v_norm1_g', 'new_v_norm2_g', 'new_v_w_in', 'new_v_ret_decay', 'new_v_conv_w', 'new_v_conv_b', 'new_v_lru_wa', 'new_v_lru_ba', 'new_v_lru_wx', 'new_v_lru_bx', 'new_v_lru_lambda', 'new_v_w_out', 'new_v_w_mlp1', 'new_v_w_mlp2', 'new_v_final_g']
TWIN_LEAF_KINDS = {'loss': 'loss', 'grad_x': 'grad_x', 'grad_c_ctx': 'grad_w', 'grad_w_ada': 'grad_w', 'grad_b_ada': 'grad_w', 'grad_norm1_g': 'grad_w', 'grad_norm2_g': 'grad_w', 'grad_w_in': 'grad_w', 'grad_ret_decay': 'grad_w', 'grad_conv_w': 'grad_w', 'grad_conv_b': 'grad_w', 'grad_lru_wa': 'grad_w', 'grad_lru_ba': 'grad_w', 'grad_lru_wx': 'grad_w', 'grad_lru_bx': 'grad_w', 'grad_lru_lambda': 'grad_w', 'grad_w_out': 'grad_w', 'grad_w_mlp1': 'grad_w', 'grad_w_mlp2': 'grad_w', 'grad_final_g': 'grad_w', 'delta_c_ctx': 'delta_w', 'delta_w_ada': 'delta_w', 'delta_b_ada': 'delta_w', 'delta_norm1_g': 'delta_w', 'delta_norm2_g': 'delta_w', 'delta_w_in': 'delta_w', 'delta_ret_decay': 'delta_w', 'delta_conv_w': 'delta_w', 'delta_conv_b': 'delta_w', 'delta_lru_wa': 'delta_w', 'delta_lru_ba': 'delta_w', 'delta_lru_wx': 'delta_w', 'delta_lru_bx': 'delta_w', 'delta_lru_lambda': 'delta_w', 'delta_w_out': 'delta_w', 'delta_w_mlp1': 'delta_w', 'delta_w_mlp2': 'delta_w', 'delta_final_g': 'delta_w', 'new_m_c_ctx': 'new_m', 'new_m_w_ada': 'new_m', 'new_m_b_ada': 'new_m', 'new_m_norm1_g': 'new_m', 'new_m_norm2_g': 'new_m', 'new_m_w_in': 'new_m', 'new_m_ret_decay': 'new_m', 'new_m_conv_w': 'new_m', 'new_m_conv_b': 'new_m', 'new_m_lru_wa': 'new_m', 'new_m_lru_ba': 'new_m', 'new_m_lru_wx': 'new_m', 'new_m_lru_bx': 'new_m', 'new_m_lru_lambda': 'new_m', 'new_m_w_out': 'new_m', 'new_m_w_mlp1': 'new_m', 'new_m_w_mlp2': 'new_m', 'new_m_final_g': 'new_m', 'new_v_c_ctx': 'new_v', 'new_v_w_ada': 'new_v', 'new_v_b_ada': 'new_v', 'new_v_norm1_g': 'new_v', 'new_v_norm2_g': 'new_v', 'new_v_w_in': 'new_v', 'new_v_ret_decay': 'new_v', 'new_v_conv_w': 'new_v', 'new_v_conv_b': 'new_v', 'new_v_lru_wa': 'new_v', 'new_v_lru_ba': 'new_v', 'new_v_lru_wx': 'new_v', 'new_v_lru_bx': 'new_v', 'new_v_lru_lambda': 'new_v', 'new_v_w_out': 'new_v', 'new_v_w_mlp1': 'new_v', 'new_v_w_mlp2': 'new_v', 'new_v_final_g': 'new_v'}


def _forward(args):
    return _fwd_reference(*[args[k] for k in FWD_PARAMS])


def _output_shape():
    out = _jax.eval_shape(lambda: _forward(_fwd_setup_inputs(0)))
    return out.shape, out.dtype

N_MICROBATCH = 1
ADAM_LR = 0.001
ADAM_B1 = 0.9
ADAM_B2 = 0.999
ADAM_EPS = 1e-08
ADAM_WD = 0.01
ADAM_STEP = 10
PER_EXAMPLE_BATCH_AXIS = {'x': 0, 'c': 0, 'ctx': 0, 'loss_target': 0}
SHARED_INPUTS = []
_WEIGHT_DTYPES = {'c_ctx': _jnp.float32, 'w_ada': _jnp.float32, 'b_ada': _jnp.float32, 'norm1_g': _jnp.float32, 'norm2_g': _jnp.float32, 'w_in': _jnp.float32, 'ret_decay': _jnp.float32, 'conv_w': _jnp.float32, 'conv_b': _jnp.float32, 'lru_wa': _jnp.float32, 'lru_ba': _jnp.float32, 'lru_wx': _jnp.float32, 'lru_bx': _jnp.float32, 'lru_lambda': _jnp.float32, 'w_out': _jnp.float32, 'w_mlp1': _jnp.float32, 'w_mlp2': _jnp.float32, 'final_g': _jnp.float32}
MOMENT_SCALE = {'c_ctx': 2.988717e-02, 'w_ada': 1.496809e-01, 'b_ada': 2.373945e-01, 'norm1_g': 1.077351e-01, 'norm2_g': 7.669787e-02, 'w_in': 8.283675e-02, 'ret_decay': 1.962062e-01, 'conv_w': 1.226331e-01, 'conv_b': 3.936333e-01, 'lru_wa': 1.094648e-02, 'lru_ba': 1.302133e-02, 'lru_wx': 2.008509e-02, 'lru_bx': 2.499132e-02, 'lru_lambda': 2.682136e-02, 'w_out': 1.023166e-01, 'w_mlp1': 3.975745e-02, 'w_mlp2': 6.947790e-02, 'final_g': 3.249799e+01}


def _to_microbatches(a, axis):
    t = _jnp.moveaxis(a, axis, 0)
    t = t.reshape((N_MICROBATCH, t.shape[0] // N_MICROBATCH) + t.shape[1:])
    return _jnp.moveaxis(t, 1, axis + 1)


def setup_inputs(seed: int = 0) -> dict:
    inp = _fwd_setup_inputs(seed)
    key = _jax.random.fold_in(_jax.random.key(seed), 7919)
    shape, _ = _output_shape()
    out = dict(inp)
    out["loss_target"] = _jax.random.normal(_jax.random.fold_in(key, 0), shape, _jnp.float32)
    for i, name in enumerate(TWIN_WEIGHTS):
        w = inp[name].astype(_jnp.float32)
        if MOMENT_SCALE is None:
            s = _jnp.sqrt(_jnp.mean(_jnp.square(w)) + 1e-30)
        else:
            s = MOMENT_SCALE[name]
        km, kv = _jax.random.split(_jax.random.fold_in(key, i + 1))
        out[name] = w
        out["m_" + name] = s * _jax.random.normal(km, w.shape, _jnp.float32)
        out["v_" + name] = (s * s) * _jax.random.uniform(kv, w.shape, _jnp.float32, 0.5, 1.5)
    if N_MICROBATCH > 1:
        for name, axis in PER_EXAMPLE_BATCH_AXIS.items():
            out[name] = _to_microbatches(out[name], axis)
    return {'x': out['x'], 'c': out['c'], 'ctx': out['ctx'], 'c_ctx': out['c_ctx'], 'w_ada': out['w_ada'], 'b_ada': out['b_ada'], 'norm1_g': out['norm1_g'], 'norm2_g': out['norm2_g'], 'w_in': out['w_in'], 'ret_decay': out['ret_decay'], 'conv_w': out['conv_w'], 'conv_b': out['conv_b'], 'lru_wa': out['lru_wa'], 'lru_ba': out['lru_ba'], 'lru_wx': out['lru_wx'], 'lru_bx': out['lru_bx'], 'lru_lambda': out['lru_lambda'], 'w_out': out['w_out'], 'w_mlp1': out['w_mlp1'], 'w_mlp2': out['w_mlp2'], 'final_g': out['final_g'], 'loss_target': out['loss_target'], 'm_c_ctx': out['m_c_ctx'], 'm_w_ada': out['m_w_ada'], 'm_b_ada': out['m_b_ada'], 'm_norm1_g': out['m_norm1_g'], 'm_norm2_g': out['m_norm2_g'], 'm_w_in': out['m_w_in'], 'm_ret_decay': out['m_ret_decay'], 'm_conv_w': out['m_conv_w'], 'm_conv_b': out['m_conv_b'], 'm_lru_wa': out['m_lru_wa'], 'm_lru_ba': out['m_lru_ba'], 'm_lru_wx': out['m_lru_wx'], 'm_lru_bx': out['m_lru_bx'], 'm_lru_lambda': out['m_lru_lambda'], 'm_w_out': out['m_w_out'], 'm_w_mlp1': out['m_w_mlp1'], 'm_w_mlp2': out['m_w_mlp2'], 'm_final_g': out['m_final_g'], 'v_c_ctx': out['v_c_ctx'], 'v_w_ada': out['v_w_ada'], 'v_b_ada': out['v_b_ada'], 'v_norm1_g': out['v_norm1_g'], 'v_norm2_g': out['v_norm2_g'], 'v_w_in': out['v_w_in'], 'v_ret_decay': out['v_ret_decay'], 'v_conv_w': out['v_conv_w'], 'v_conv_b': out['v_conv_b'], 'v_lru_wa': out['v_lru_wa'], 'v_lru_ba': out['v_lru_ba'], 'v_lru_wx': out['v_lru_wx'], 'v_lru_bx': out['v_lru_bx'], 'v_lru_lambda': out['v_lru_lambda'], 'v_w_out': out['v_w_out'], 'v_w_mlp1': out['v_w_mlp1'], 'v_w_mlp2': out['v_w_mlp2'], 'v_final_g': out['v_final_g']}


def _loss(weights, diff, rest, loss_target):
    with _jax.named_scope("forward"):
        args = {**rest, TWIN_DIFF_INPUT: diff, **{k: w.astype(_WEIGHT_DTYPES[k]) for k, w in weights.items()}}
        y = _forward(args)
    with _jax.named_scope("loss_head"):
        err = _jnp.square(y.astype(_jnp.float32) - loss_target)
        return 0.5 * _jnp.sum(_jnp.mean(err, axis=-1)) if err.ndim else 0.5 * err


def _adamw(w, g, m, v):
    m = ADAM_B1 * m + (1.0 - ADAM_B1) * g
    v = ADAM_B2 * v + (1.0 - ADAM_B2) * _jnp.square(g)
    m_hat = m / (1.0 - ADAM_B1 ** ADAM_STEP)
    v_hat = v / (1.0 - ADAM_B2 ** ADAM_STEP)
    delta = -ADAM_LR * (m_hat / (_jnp.sqrt(v_hat) + ADAM_EPS) + ADAM_WD * w)
    return delta, m, v


def reference(x, c, ctx, c_ctx, w_ada, b_ada, norm1_g, norm2_g, w_in, ret_decay, conv_w, conv_b, lru_wa, lru_ba, lru_wx, lru_bx, lru_lambda, w_out, w_mlp1, w_mlp2, final_g, loss_target, m_c_ctx, m_w_ada, m_b_ada, m_norm1_g, m_norm2_g, m_w_in, m_ret_decay, m_conv_w, m_conv_b, m_lru_wa, m_lru_ba, m_lru_wx, m_lru_bx, m_lru_lambda, m_w_out, m_w_mlp1, m_w_mlp2, m_final_g, v_c_ctx, v_w_ada, v_b_ada, v_norm1_g, v_norm2_g, v_w_in, v_ret_decay, v_conv_w, v_conv_b, v_lru_wa, v_lru_ba, v_lru_wx, v_lru_bx, v_lru_lambda, v_w_out, v_w_mlp1, v_w_mlp2, v_final_g):
    given = dict(x=x, c=c, ctx=ctx, c_ctx=c_ctx, w_ada=w_ada, b_ada=b_ada, norm1_g=norm1_g, norm2_g=norm2_g, w_in=w_in, ret_decay=ret_decay, conv_w=conv_w, conv_b=conv_b, lru_wa=lru_wa, lru_ba=lru_ba, lru_wx=lru_wx, lru_bx=lru_bx, lru_lambda=lru_lambda, w_out=w_out, w_mlp1=w_mlp1, w_mlp2=w_mlp2, final_g=final_g, loss_target=loss_target, m_c_ctx=m_c_ctx, m_w_ada=m_w_ada, m_b_ada=m_b_ada, m_norm1_g=m_norm1_g, m_norm2_g=m_norm2_g, m_w_in=m_w_in, m_ret_decay=m_ret_decay, m_conv_w=m_conv_w, m_conv_b=m_conv_b, m_lru_wa=m_lru_wa, m_lru_ba=m_lru_ba, m_lru_wx=m_lru_wx, m_lru_bx=m_lru_bx, m_lru_lambda=m_lru_lambda, m_w_out=m_w_out, m_w_mlp1=m_w_mlp1, m_w_mlp2=m_w_mlp2, m_final_g=m_final_g, v_c_ctx=v_c_ctx, v_w_ada=v_w_ada, v_b_ada=v_b_ada, v_norm1_g=v_norm1_g, v_norm2_g=v_norm2_g, v_w_in=v_w_in, v_ret_decay=v_ret_decay, v_conv_w=v_conv_w, v_conv_b=v_conv_b, v_lru_wa=v_lru_wa, v_lru_ba=v_lru_ba, v_lru_wx=v_lru_wx, v_lru_bx=v_lru_bx, v_lru_lambda=v_lru_lambda, v_w_out=v_w_out, v_w_mlp1=v_w_mlp1, v_w_mlp2=v_w_mlp2, v_final_g=v_final_g)
    weights = {n: given[n] for n in TWIN_WEIGHTS}
    shared = {n: given[n] for n in SHARED_INPUTS}
    per_example = {n: given[n] for n in ['x', 'c', 'ctx']}
    grad_fn = _jax.value_and_grad(_loss, argnums=(0, 1))

    def one_microbatch(ex, loss_target):
        ex = dict(ex)
        diff = ex.pop(TWIN_DIFF_INPUT)
        return grad_fn(weights, diff, {**shared, **ex}, loss_target)

    if N_MICROBATCH == 1:
        loss, (grad_w, grad_x) = one_microbatch(per_example, given["loss_target"])
    else:
        def body(carry, xs):
            loss_sum, grad_sum = carry
            l_k, (gw_k, gx_k) = one_microbatch(xs[0], xs[1])
            with _jax.named_scope("update"):
                return (loss_sum + l_k, _jax.tree.map(_jnp.add, grad_sum, gw_k)), gx_k

        init = (_jnp.zeros((), _jnp.float32), _jax.tree.map(_jnp.zeros_like, weights))
        (loss, grad_w), grad_x = _jax.lax.scan(body, init, (per_example, given["loss_target"]))
    with _jax.named_scope("update"):
        delta_w, new_m, new_v = {}, {}, {}
        for n in TWIN_WEIGHTS:
            delta_w[n], new_m[n], new_v[n] = _adamw(weights[n], grad_w[n], given["m_" + n], given["v_" + n])
    return (loss, grad_x, *[grad_w[n] for n in TWIN_WEIGHTS], *[delta_w[n] for n in TWIN_WEIGHTS],
            *[new_m[n] for n in TWIN_WEIGHTS], *[new_v[n] for n in TWIN_WEIGHTS])
```

```python
import math

import jax
import jax.numpy as jnp
from jax import lax
from jax.experimental import pallas as pl
from jax.experimental.pallas import tpu as pltpu

F32 = jnp.float32
BF16 = jnp.bfloat16

D_MODEL = 1024
HEADS = 4
DH = 128
CHUNK = 128
RET_W = HEADS * DH
LRU_W = 512
LRU_BLOCKS = 8
LRU_BD = LRU_W // LRU_BLOCKS
LRU_C = 8.0
IN_COLS = 4 * RET_W + 2 * LRU_W
MLP_H = 4 * D_MODEL
N_MOD = 6
GRID_W = 64
ROPE_BASE = 10000.0
K_SCALE = DH ** -0.5
EPS = 1e-6
GELU_K = math.sqrt(2.0 / math.pi)
GELU_C = 0.044715

ADAM_LR = 0.001
ADAM_B1 = 0.9
ADAM_B2 = 0.999
ADAM_EPS = 1e-08
ADAM_WD = 0.01
ADAM_STEP = 10

N_DEV = 8
N_CHIP = 4
SUBLANES = 8
LANES = 128
VMEM_LIMIT_V7X = 56 * 1024 * 1024
MESH = pl.DeviceIdType.MESH
ANY = pl.BlockSpec(memory_space=pl.ANY)


def _pc(body, **kw):
    return pl.pallas_call(body, **kw)


def _params(*sem):
    return pltpu.CompilerParams(dimension_semantics=sem if sem else None, vmem_limit_bytes=VMEM_LIMIT_V7X)


def _tile(t):
    return 256 if t >= 256 else t


def _sds(shape, dtype=F32):
    return jax.ShapeDtypeStruct(tuple(shape), dtype)


def _full(shape):
    nd = len(shape)
    return pl.BlockSpec(tuple(shape), lambda *_: (0,) * nd)


def _sigmoid(x):
    return 1.0 / (1.0 + jnp.exp(-x))


def _log1p_pos(y):
    s = y * (1.0 - y * (0.5 - y * (1.0 / 3.0 - y * (0.25 - y * (0.2 - y / 6.0)))))
    return jnp.where(y < 0.03, s, jnp.log(1.0 + y))


def _softplus(z):
    return jnp.maximum(z, 0.0) + _log1p_pos(jnp.exp(-jnp.abs(z)))


def _neg_expm1(x):
    t = x * (1.0 + x * (0.5 + x * (1.0 / 6.0 + x * (1.0 / 24.0 + x * (1.0 / 120.0 + x * (1.0 / 720.0 + x / 5040.0))))))
    return -jnp.where(x > -0.25, t, jnp.exp(x) - 1.0)


def _rms(x):
    r = lax.rsqrt(jnp.mean(x * x, axis=-1, keepdims=True) + EPS)
    return x * r, r


def _dot(a, b):
    return jnp.dot(a, b, preferred_element_type=F32)


def _dot_nt(a, b):
    return lax.dot_general(a, b, (((1,), (1,)), ((), ())), preferred_element_type=F32)


def _dot_tn(a, b):
    return lax.dot_general(a, b, (((0,), (0,)), ((), ())), preferred_element_type=F32)


def _sum0(x):
    return jnp.sum(x, axis=0, keepdims=True)


def _norm_mod_bwd(x, g, sc, dh):
    xh, r = _rms(x)
    hn = xh * g
    dhn = dh * (1.0 + sc)
    dxh = dhn * g
    dx = r * (dxh - xh * jnp.mean(dxh * xh, axis=-1, keepdims=True))
    return dx, _sum0(dhn * xh), _sum0(dh), _sum0(dh * hn)


def _dev_index(p):
    return 4 * p[0] + 2 * p[1] + p[2]


def _all_gather(arrs, name):
    n = len(arrs)

    def body(*refs):
        srcs = refs[:n]
        outs = refs[n:2 * n]
        send_sems, recv_sems, local_sems = refs[2 * n:]
        x, y, c = lax.axis_index("x"), lax.axis_index("y"), lax.axis_index("c")
        me, sib = (x, y, c), (x, y, 1 - c)
        chips = [(1 - x, y), (x, 1 - y), (1 - x, 1 - y)]

        def copy(t, k, block, to, src=None):
            dst = outs[t].at[_dev_index(block)]
            return pltpu.make_async_remote_copy(
                src_ref=dst if src is None else src, dst_ref=dst,
                send_sem=send_sems.at[7 * t + k], recv_sem=recv_sems.at[7 * t + k],
                device_id=to, device_id_type=MESH)

        mine = [pltpu.make_async_copy(srcs[t], outs[t].at[_dev_index(me)], local_sems.at[t]) for t in range(n)]
        for cp in mine:
            cp.start()
        first = []
        for t in range(n):
            first.append(copy(t, 0, me, sib, src=srcs[t]))
            for j, ch in enumerate(chips):
                first.append(copy(t, 1 + j, me, (*ch, c), src=srcs[t]))
        for cp in first:
            cp.start()
        passed = []
        for j, ch in enumerate(chips):
            for t in range(n):
                copy(t, 1 + j, (*ch, c), me).wait_recv()
                p = copy(t, 4 + j, (*ch, c), sib)
                p.start()
                passed.append(p)
        for t in range(n):
            copy(t, 0, sib, me).wait_recv()
            for j, ch in enumerate(chips):
                copy(t, 4 + j, (*ch, 1 - c), me).wait_recv()
        for cp in first + passed:
            cp.wait_send()
        for cp in mine:
            cp.wait()

    outs = _pc(
        body, name=name,
        out_shape=[_sds((N_DEV,) + a.shape, a.dtype) for a in arrs],
        in_specs=[ANY] * n, out_specs=[ANY] * n,
        scratch_shapes=[pltpu.SemaphoreType.DMA((7 * n,)), pltpu.SemaphoreType.DMA((7 * n,)),
                        pltpu.SemaphoreType.DMA((n,))],
    )(*arrs)
    return list(outs)


def _pair_exchange(grads, name):
    n = len(grads)

    def body(*refs):
        srcs = refs[:n]
        outs = refs[n:2 * n]
        send_sems, recv_sems = refs[2 * n:]
        x, y, c = lax.axis_index("x"), lax.axis_index("y"), lax.axis_index("c")
        sib = (x, y, 1 - c)
        cps = []
        for t in range(n):
            for j in range(N_CHIP):
                cps.append(pltpu.make_async_remote_copy(
                    src_ref=srcs[t].at[2 * j + (1 - c)], dst_ref=outs[t].at[j],
                    send_sem=send_sems.at[4 * t + j], recv_sem=recv_sems.at[4 * t + j],
                    device_id=sib, device_id_type=MESH))
        for cp in cps:
            cp.start()
        for cp in cps:
            cp.wait()

    outs = _pc(
        body, name=name,
        out_shape=[_sds((N_CHIP,) + g.shape[1:], g.dtype) for g in grads],
        in_specs=[ANY] * n, out_specs=[ANY] * n,
        scratch_shapes=[pltpu.SemaphoreType.DMA((4 * n,)), pltpu.SemaphoreType.DMA((4 * n,))],
    )(*grads)
    return list(outs)


def _chip_exchange(parts, name):
    n = len(parts)

    def body(*refs):
        srcs = refs[:n]
        outs = refs[n:2 * n]
        send_sems, recv_sems = refs[2 * n:]
        x, y, c = lax.axis_index("x"), lax.axis_index("y"), lax.axis_index("c")
        chips = [(1 - x, y), (x, 1 - y), (1 - x, 1 - y)]
        cps = []
        for t in range(n):
            for k, ch in enumerate(chips):
                cps.append(pltpu.make_async_remote_copy(
                    src_ref=srcs[t].at[2 * ch[0] + ch[1]], dst_ref=outs[t].at[k],
                    send_sem=send_sems.at[3 * t + k], recv_sem=recv_sems.at[3 * t + k],
                    device_id=(*ch, c), device_id_type=MESH))
        for cp in cps:
            cp.start()
        for cp in cps:
            cp.wait()

    outs = _pc(
        body, name=name,
        out_shape=[_sds((3,) + p.shape[1:], p.dtype) for p in parts],
        in_specs=[ANY] * n, out_specs=[ANY] * n,
        scratch_shapes=[pltpu.SemaphoreType.DMA((3 * n,)), pltpu.SemaphoreType.DMA((3 * n,))],
    )(*parts)
    return list(outs)


def _pair_gather(halves, name):
    n = len(halves)

    def body(*refs):
        srcs = refs[:n]
        outs = refs[n:2 * n]
        send_sems, recv_sems, local_sems = refs[2 * n:]
        x, y, c = lax.axis_index("x"), lax.axis_index("y"), lax.axis_index("c")
        sib = (x, y, 1 - c)
        loc, snd, rcv = [], [], []
        for t in range(n):
            loc.append(pltpu.make_async_copy(srcs[t], outs[t].at[c], local_sems.at[t]))
            snd.append(pltpu.make_async_remote_copy(
                src_ref=srcs[t], dst_ref=outs[t].at[c], send_sem=send_sems.at[t], recv_sem=recv_sems.at[t],
                device_id=sib, device_id_type=MESH))
            rcv.append(pltpu.make_async_remote_copy(
                src_ref=srcs[t], dst_ref=outs[t].at[1 - c], send_sem=send_sems.at[t], recv_sem=recv_sems.at[t],
                device_id=sib, device_id_type=MESH))
        for cp in loc + snd:
            cp.start()
        for cp in rcv:
            cp.wait_recv()
        for cp in snd:
            cp.wait_send()
        for cp in loc:
            cp.wait()

    outs = _pc(
        body, name=name,
        out_shape=[_sds((2,) + h.shape, h.dtype) for h in halves],
        in_specs=[ANY] * n, out_specs=[ANY] * n,
        scratch_shapes=[pltpu.SemaphoreType.DMA((n,)), pltpu.SemaphoreType.DMA((n,)), pltpu.SemaphoreType.DMA((n,))],
    )(*halves)
    return list(outs)


def _row_block(r):
    for b in (512, 256, 128, 64, 32, 16, 8):
        if r % b == 0:
            return b
    return r


def _pair_add(g, recv, c_idx, name):
    _, r, cc = g.shape
    br = _row_block(r)

    def body(c_ref, g_ref, r_ref, p_ref, pb_ref):
        s = g_ref[...] + r_ref[...]
        p_ref[...] = s
        pb_ref[...] = s.astype(BF16)

    grid_spec = pltpu.PrefetchScalarGridSpec(
        num_scalar_prefetch=1, grid=(N_CHIP, r // br),
        in_specs=[pl.BlockSpec((1, br, cc), lambda j, i, c_ref: (2 * j + c_ref[0], i, 0)),
                  pl.BlockSpec((1, br, cc), lambda j, i, c_ref: (j, i, 0))],
        out_specs=[pl.BlockSpec((1, br, cc), lambda j, i, c_ref: (j, i, 0)),
                   pl.BlockSpec((1, br, cc), lambda j, i, c_ref: (j, i, 0))])
    return _pc(body, name=name, grid_spec=grid_spec,
               out_shape=[_sds((N_CHIP, r, cc)), _sds((N_CHIP, r, cc), BF16)],
               compiler_params=_params("arbitrary", "arbitrary"))(c_idx, g, recv)


def _chip_add(p, q, j_idx, name):
    _, r, cc = p.shape
    br = _row_block(r)

    def body(j_ref, p_ref, q_ref, o_ref):
        o_ref[...] = ((p_ref[0] + q_ref[0].astype(F32)) + q_ref[1].astype(F32)) + q_ref[2].astype(F32)

    grid_spec = pltpu.PrefetchScalarGridSpec(
        num_scalar_prefetch=1, grid=(r // br,),
        in_specs=[pl.BlockSpec((1, br, cc), lambda i, j_ref: (j_ref[0], i, 0)),
                  pl.BlockSpec((3, br, cc), lambda i, j_ref: (0, i, 0))],
        out_specs=pl.BlockSpec((br, cc), lambda i, j_ref: (i, 0)))
    return _pc(body, name=name, grid_spec=grid_spec, out_shape=_sds((r, cc)),
               compiler_params=_params("arbitrary"))(j_idx, p, q)


def _reduce_scatter(grads, c_idx, j_idx):
    names = ["w_in", "w_out", "w_mlp1", "w_mlp2"]
    recv = _pair_exchange(grads, "rs_pair_exchange")
    sums, sums_b = [], []
    for t, g in enumerate(grads):
        s, sb = _pair_add(g, recv[t], c_idx, "rs_pair_add_" + names[t])
        sums.append(s)
        sums_b.append(sb)
    others = _chip_exchange(sums_b, "rs_chip_exchange")
    halves = [_chip_add(sums[t], others[t], j_idx, "rs_chip_add_" + names[t]) for t in range(len(grads))]
    both = _pair_gather(halves, "rs_pair_gather")
    return [b.reshape((2 * b.shape[1],) + b.shape[2:]) for b in both]


def _sum_devices(g, name):
    _, r, cc = g.shape

    def body(g_ref, o_ref):
        acc = g_ref[0]
        for d in range(1, N_DEV):
            acc = acc + g_ref[d]
        o_ref[...] = acc

    return _pc(body, name=name, out_shape=_sds((r, cc)), in_specs=[_full(g.shape)], out_specs=_full((r, cc)),
               compiler_params=_params())(g)


def _adamw(w, g, m, v, name):
    r, cc = w.shape
    br = _row_block(r)
    if br * cc * 4 > (1 << 20) and br > 8:
        br = max(8, (1 << 20) // (cc * 4) // 8 * 8)
        while r % br:
            br -= 8
    c1 = 1.0 - ADAM_B1 ** ADAM_STEP
    c2 = 1.0 - ADAM_B2 ** ADAM_STEP

    def body(w_ref, g_ref, m_ref, v_ref, d_ref, mo_ref, vo_ref):
        gg = g_ref[...]
        mn = ADAM_B1 * m_ref[...] + (1.0 - ADAM_B1) * gg
        vn = ADAM_B2 * v_ref[...] + (1.0 - ADAM_B2) * (gg * gg)
        mh = mn / c1
        vh = vn / c2
        d_ref[...] = -ADAM_LR * (mh / (jnp.sqrt(vh) + ADAM_EPS) + ADAM_WD * w_ref[...])
        mo_ref[...] = mn
        vo_ref[...] = vn

    spec = pl.BlockSpec((br, cc), lambda i: (i, 0))
    return _pc(body, name=name, grid=(r // br,), in_specs=[spec] * 4, out_specs=[spec] * 3,
               out_shape=[_sds((r, cc))] * 3, compiler_params=_params("arbitrary"))(w, g, m, v)


def _prep(c_all, c_ctx, ret_decay):
    def body(c_ref, cc_ref, rd_ref, a_ref, lg_ref, sg_ref):
        ca = c_ref[...]
        cc = cc_ref[...]
        a_ref[...] = jnp.zeros_like(a_ref)
        a_ref[0:8, :] = ca * _sigmoid(ca)
        a_ref[8:9, :] = cc * _sigmoid(cc)
        rd = rd_ref[...]
        lg_ref[...] = -_softplus(-rd)
        sg_ref[...] = _sigmoid(-rd)

    rd = jnp.broadcast_to(ret_decay.reshape(2, HEADS).T[:, :, None], (HEADS, 2, LANES))
    return _pc(body, name="prep",
               out_shape=[_sds((16, D_MODEL)), _sds((HEADS, 2, LANES)), _sds((HEADS, 2, LANES))],
               in_specs=[_full((8, D_MODEL)), _full((1, D_MODEL)), _full((HEADS, 2, LANES))],
               out_specs=[_full((16, D_MODEL)), _full((HEADS, 2, LANES)), _full((HEADS, 2, LANES))],
               compiler_params=_params())(c_all, c_ctx.reshape(1, D_MODEL), rd)


def _mod_fwd(a16, w_ada, b_shard):
    n = w_ada.shape[1]
    bn = 512

    def body(a_ref, w_ref, b_ref, o_ref):
        o_ref[...] = jnp.dot(a_ref[...], w_ref[...], preferred_element_type=F32,
                             precision=lax.Precision.HIGHEST) + b_ref[...]

    return _pc(body, name="mod_fwd", grid=(n // bn,),
               in_specs=[_full((16, D_MODEL)), pl.BlockSpec((D_MODEL, bn), lambda i: (0, i)),
                         pl.BlockSpec((1, bn), lambda i: (0, i))],
               out_specs=pl.BlockSpec((16, bn), lambda i: (0, i)), out_shape=_sds((16, n)),
               compiler_params=_params("arbitrary"))(a16, w_ada, b_shard)


def _ada_grad(at, b):
    n = b.shape[1]
    bn = 512

    def body(a_ref, b_ref, o_ref):
        o_ref[...] = jnp.dot(a_ref[...], b_ref[...], preferred_element_type=F32, precision=lax.Precision.HIGHEST)

    return _pc(body, name="ada_grad", grid=(n // bn,),
               in_specs=[_full((D_MODEL, LANES)), pl.BlockSpec((LANES, bn), lambda i: (0, i))],
               out_specs=pl.BlockSpec((D_MODEL, bn), lambda i: (0, i)), out_shape=_sds((D_MODEL, n)),
               compiler_params=_params("arbitrary"))(at, b)


def _cctx_partial(dmc8, w_ada):
    n = w_ada.shape[1]
    bn = 512

    def body(d_ref, w_ref, o_ref):
        @pl.when(pl.program_id(0) == 0)
        def _():
            o_ref[...] = jnp.zeros_like(o_ref)
        o_ref[...] += lax.dot_general(d_ref[...], w_ref[...], (((1,), (1,)), ((), ())),
                                      preferred_element_type=F32, precision=lax.Precision.HIGHEST)

    return _pc(body, name="cctx_partial", grid=(n // bn,),
               in_specs=[pl.BlockSpec((8, bn), lambda i: (0, i)), pl.BlockSpec((D_MODEL, bn), lambda i: (0, i))],
               out_specs=_full((8, D_MODEL)), out_shape=_sds((8, D_MODEL)),
               compiler_params=_params("arbitrary"))(dmc8, w_ada)


def _cctx_final(parts, c_ctx, m, v):
    c1 = 1.0 - ADAM_B1 ** ADAM_STEP
    c2 = 1.0 - ADAM_B2 ** ADAM_STEP

    def body(p_ref, c_ref, m_ref, v_ref, g_ref, d_ref, mo_ref, vo_ref):
        s = ((p_ref[0, 0:1, :] + p_ref[2, 0:1, :]) + p_ref[4, 0:1, :]) + p_ref[6, 0:1, :]
        z = c_ref[...]
        sg = _sigmoid(z)
        gg = s * (sg * (1.0 + z * (1.0 - sg)))
        g_ref[...] = gg
        mn = ADAM_B1 * m_ref[...] + (1.0 - ADAM_B1) * gg
        vn = ADAM_B2 * v_ref[...] + (1.0 - ADAM_B2) * (gg * gg)
        d_ref[...] = -ADAM_LR * ((mn / c1) / (jnp.sqrt(vn / c2) + ADAM_EPS) + ADAM_WD * z)
        mo_ref[...] = mn
        vo_ref[...] = vn

    row = _full((1, D_MODEL))
    return _pc(body, name="cctx_final", out_shape=[_sds((1, D_MODEL))] * 4,
               in_specs=[_full(parts.shape), row, row, row], out_specs=[row] * 4,
               compiler_params=_params())(parts, c_ctx.reshape(1, D_MODEL), m.reshape(1, D_MODEL), v.reshape(1, D_MODEL))


def _rotary_tables(t_len):
    rows = t_len // GRID_W
    row = jnp.repeat(jnp.arange(rows, dtype=F32), GRID_W)
    col = jnp.tile(jnp.arange(GRID_W, dtype=F32), rows)
    n_freq = DH // 4
    inv = ROPE_BASE ** (-jnp.arange(n_freq, dtype=F32) / n_freq)
    ang = jnp.concatenate([row[:, None] * inv, col[:, None] * inv], axis=-1)
    cos, sin = jnp.cos(ang), jnp.sin(ang)
    return jnp.concatenate([cos, cos], axis=-1), jnp.concatenate([-sin, sin], axis=-1)


def _inproj_fwd(x, gn, sh, sc, w4, cos2, sin2, name):
    t = x.shape[0]
    tm = _tile(t)
    nc = IN_COLS // N_CHIP

    def body(x_ref, gn_ref, sh_ref, sc_ref, w_ref, c_ref, s_ref, p_ref, hb_ref):
        xh, _ = _rms(x_ref[...])
        h = xh * gn_ref[...] * (1.0 + sc_ref[...]) + sh_ref[...]
        hb = h.astype(BF16)
        hb_ref[...] = hb
        for j in range(N_CHIP):
            p_ref[:, nc * j:nc * (j + 1)] = _dot(hb, w_ref[j])
        cc = c_ref[...]
        ss = s_ref[...]
        for hh in range(2 * HEADS):
            blk = p_ref[:, DH * hh:DH * (hh + 1)]
            rot = blk * cc + pltpu.roll(blk, DH // 2, 1) * ss
            if hh >= HEADS:
                rot = rot * K_SCALE
            p_ref[:, DH * hh:DH * (hh + 1)] = rot

    row = _full((1, D_MODEL))
    return _pc(body, name=name, grid=(t // tm,),
               in_specs=[pl.BlockSpec((tm, D_MODEL), lambda i: (i, 0)), row, row, row, _full(w4.shape),
                         pl.BlockSpec((tm, DH), lambda i: (i, 0)), pl.BlockSpec((tm, DH), lambda i: (i, 0))],
               out_specs=[pl.BlockSpec((tm, IN_COLS), lambda i: (i, 0)), pl.BlockSpec((tm, D_MODEL), lambda i: (i, 0))],
               out_shape=[_sds((t, IN_COLS)), _sds((t, D_MODEL), BF16)],
               compiler_params=_params("arbitrary"))(x, gn, sh, sc, w4, cos2, sin2)


def _inproj_bwd(x, gn, sh, sc, w4, cos2, sin2, pieces, dres, name):
    t = x.shape[0]
    tm = _tile(t)
    nc = IN_COLS // N_CHIP

    def body(x_ref, gn_ref, sh_ref, sc_ref, w_ref, c_ref, s_ref, dqf, dqb, dkf, dkb, dvf, dvb, dg, dxr, dgt, dres_ref,
             dx_ref, dpb_ref, dgn_ref, dsh_ref, dsc_ref):
        cc = c_ref[...]
        ss = s_ref[...]
        dq = dqf[...] + dqb[...]
        dk = dkf[...] + dkb[...]
        for hh in range(HEADS):
            sl = slice(DH * hh, DH * (hh + 1))
            b = dq[:, sl]
            dpb_ref[:, sl] = (b * cc + pltpu.roll(b * ss, DH // 2, 1)).astype(BF16)
            b = dk[:, sl]
            dpb_ref[:, RET_W + DH * hh:RET_W + DH * (hh + 1)] = (
                (b * cc + pltpu.roll(b * ss, DH // 2, 1)) * K_SCALE).astype(BF16)
        dpb_ref[:, 2 * RET_W:3 * RET_W] = (dvf[...] + dvb[...]).astype(BF16)
        dpb_ref[:, 3 * RET_W:4 * RET_W] = dg[...].astype(BF16)
        dpb_ref[:, 4 * RET_W:4 * RET_W + LRU_W] = dxr[...].astype(BF16)
        dpb_ref[:, 4 * RET_W + LRU_W:IN_COLS] = dgt[...].astype(BF16)
        dh = _dot_nt(dpb_ref[:, 0:nc], w_ref[0])
        for j in range(1, N_CHIP):
            dh = dh + _dot_nt(dpb_ref[:, nc * j:nc * (j + 1)], w_ref[j])
        dx, dgn_t, dsh_t, dsc_t = _norm_mod_bwd(x_ref[...], gn_ref[...], sc_ref[...], dh)
        dx_ref[...] = dres_ref[...] + dx

        @pl.when(pl.program_id(0) == 0)
        def _():
            dgn_ref[...] = jnp.zeros_like(dgn_ref)
            dsh_ref[...] = jnp.zeros_like(dsh_ref)
            dsc_ref[...] = jnp.zeros_like(dsc_ref)
        dgn_ref[...] += dgn_t
        dsh_ref[...] += dsh_t
        dsc_ref[...] += dsc_t

    row = _full((1, D_MODEL))
    pc = pl.BlockSpec((tm, RET_W), lambda i: (i, 0))
    big = pl.BlockSpec((tm, D_MODEL), lambda i: (i, 0))
    return _pc(body, name=name, grid=(t // tm,),
               in_specs=[big, row, row, row, _full(w4.shape),
                         pl.BlockSpec((tm, DH), lambda i: (i, 0)), pl.BlockSpec((tm, DH), lambda i: (i, 0))]
               + [pc] * 9 + [big],
               out_specs=[big, pl.BlockSpec((tm, IN_COLS), lambda i: (i, 0)), row, row, row],
               out_shape=[_sds((t, D_MODEL)), _sds((t, IN_COLS), BF16), _sds((1, D_MODEL)), _sds((1, D_MODEL)),
                          _sds((1, D_MODEL))],
               compiler_params=_params("arbitrary"))(x, gn, sh, sc, w4, cos2, sin2, *pieces, dres)


XR_BLOCK = (4 * RET_W) // LRU_W


def _halo_specs(t, tm, col):
    n8 = tm // SUBLANES
    last8 = t // SUBLANES - 1
    prev = pl.BlockSpec((SUBLANES, LRU_W), lambda i: (jnp.maximum(i * n8 - 1, 0), col))
    main = pl.BlockSpec((tm, LRU_W), lambda i: (i, col))
    nxt = pl.BlockSpec((SUBLANES, LRU_W), lambda i: (jnp.minimum((i + 1) * n8, last8), col))
    return prev, main, nxt


def _with_halo(prev_ref, main_ref, next_ref, i, nt):
    prev = jnp.where(i > 0, prev_ref[...], 0.0)
    nxt = jnp.where(i < nt - 1, next_ref[...], 0.0)
    return jnp.concatenate([prev, main_ref[...], nxt], axis=0)


def _conv_fwd(proj, cw, cb, name):
    t = proj.shape[0]
    tm = _tile(t)
    nt = t // tm
    n = tm + 2 * SUBLANES
    mid = slice(SUBLANES, SUBLANES + tm)

    def body(p_ref, m_ref, n_ref, w_ref, b_ref, o_ref):
        xp = _with_halo(p_ref, m_ref, n_ref, pl.program_id(0), nt)
        acc = b_ref[...] + pltpu.roll(xp, 1, 0)[mid] * w_ref[0:1, :]
        acc = acc + xp[mid] * w_ref[1:2, :]
        acc = acc + pltpu.roll(xp, n - 1, 0)[mid] * w_ref[2:3, :]
        acc = acc + pltpu.roll(xp, n - 2, 0)[mid] * w_ref[3:4, :]
        o_ref[...] = acc

    return _pc(body, name=name, grid=(nt,),
               in_specs=[*_halo_specs(t, tm, XR_BLOCK), _full((4, LRU_W)), _full((1, LRU_W))],
               out_specs=pl.BlockSpec((tm, LRU_W), lambda i: (i, 0)), out_shape=_sds((t, LRU_W)),
               compiler_params=_params("arbitrary"))(proj, proj, proj, cw, cb)


def _conv_bwd(dxc, proj, cw, name):
    t = proj.shape[0]
    tm = _tile(t)
    nt = t // tm
    n = tm + 2 * SUBLANES
    mid = slice(SUBLANES, SUBLANES + tm)

    def body(dp_ref, dm_ref, dn_ref, xp_ref, xm_ref, xn_ref, w_ref, dx_ref, dw_ref, db_ref):
        i = pl.program_id(0)
        dp = _with_halo(dp_ref, dm_ref, dn_ref, i, nt)
        xp = _with_halo(xp_ref, xm_ref, xn_ref, i, nt)
        dx = pltpu.roll(dp, n - 1, 0)[mid] * w_ref[0:1, :]
        dx = dx + dp[mid] * w_ref[1:2, :]
        dx = dx + pltpu.roll(dp, 1, 0)[mid] * w_ref[2:3, :]
        dx = dx + pltpu.roll(dp, 2, 0)[mid] * w_ref[3:4, :]
        dx_ref[...] = dx
        d = dm_ref[...]

        @pl.when(i == 0)
        def _():
            dw_ref[...] = jnp.zeros_like(dw_ref)
            db_ref[...] = jnp.zeros_like(db_ref)
        dw_ref[0:1, :] += _sum0(d * pltpu.roll(xp, 1, 0)[mid])
        dw_ref[1:2, :] += _sum0(d * xp[mid])
        dw_ref[2:3, :] += _sum0(d * pltpu.roll(xp, n - 1, 0)[mid])
        dw_ref[3:4, :] += _sum0(d * pltpu.roll(xp, n - 2, 0)[mid])
        db_ref[...] += _sum0(d)

    return _pc(body, name=name, grid=(nt,),
               in_specs=[*_halo_specs(t, tm, 0), *_halo_specs(t, tm, XR_BLOCK), _full((4, LRU_W))],
               out_specs=[pl.BlockSpec((tm, LRU_W), lambda i: (i, 0)), _full((4, LRU_W)), _full((1, LRU_W))],
               out_shape=[_sds((t, LRU_W)), _sds((4, LRU_W)), _sds((1, LRU_W))],
               compiler_params=_params("arbitrary"))(dxc, dxc, dxc, proj, proj, proj, cw)


def _local_scan(a, b, reverse):
    n = a.shape[0]
    row = lax.broadcasted_iota(jnp.int32, a.shape, 0) & (SUBLANES - 1)
    for s in (1, 2, 4):
        if reverse:
            a_s, b_s, ok = pltpu.roll(a, n - s, 0), pltpu.roll(b, n - s, 0), row < SUBLANES - s
        else:
            a_s, b_s, ok = pltpu.roll(a, s, 0), pltpu.roll(b, s, 0), row >= s
        b = a * jnp.where(ok, b_s, 0.0) + b
        a = a * jnp.where(ok, a_s, 1.0)
    return a, b


def _carry_scan(a_s, b_s, out_ref, carry, reverse):
    ng = a_s.shape[0] // SUBLANES
    shape = carry.shape

    def step(g, cr):
        gg = (ng - 1 - g) if reverse else g
        off = pl.multiple_of(gg * SUBLANES, SUBLANES)
        h = a_s[pl.ds(off, SUBLANES), :] * cr + b_s[pl.ds(off, SUBLANES), :]
        out_ref[pl.ds(off, SUBLANES), :] = h
        edge = h[0:1, :] if reverse else h[SUBLANES - 1:SUBLANES, :]
        return jnp.broadcast_to(edge, shape)

    return lax.fori_loop(0, ng, step, carry)


def _lru_gates(xc, wa_ref, wx_ref, ba, bx, lam):
    xb = xc.astype(BF16)
    r = _sigmoid(_dot(xb, wa_ref[...]) + ba)
    ig = _sigmoid(_dot(xb, wx_ref[...]) + bx)
    sp = _softplus(-lam)
    la = -LRU_C * r * sp
    a = jnp.exp(la)
    mult = jnp.sqrt(_neg_expm1(2.0 * la))
    return r, ig, sp, a, mult


def _lru_fwd(xc, wa, wx, ba, bx, lam, h0, reverse, name):
    t = xc.shape[0]
    tm = _tile(t)
    nt = t // tm
    tidx = (lambda i: (nt - 1 - i, 0)) if reverse else (lambda i: (i, 0))

    def body(x_ref, wa_ref, wx_ref, ba_ref, bx_ref, lam_ref, h0_ref, h_ref, a_s, b_s, c_s):
        @pl.when(pl.program_id(0) == 0)
        def _():
            c_s[...] = jnp.broadcast_to(h0_ref[...], c_s.shape)
        xv = x_ref[...]
        _, ig, _, a, mult = _lru_gates(xv, wa_ref, wx_ref, ba_ref[...], bx_ref[...], lam_ref[...])
        al, bl = _local_scan(a, mult * (ig * xv), reverse)
        a_s[...] = al
        b_s[...] = bl
        c_s[...] = _carry_scan(a_s, b_s, h_ref, c_s[...], reverse)

    vec = _full((1, LRU_W))
    mat = _full((LRU_W, LRU_W))
    return _pc(body, name=name, grid=(nt,),
               in_specs=[pl.BlockSpec((tm, LRU_W), tidx), mat, mat, vec, vec, vec, vec],
               out_specs=pl.BlockSpec((tm, LRU_W), tidx), out_shape=_sds((t, LRU_W)),
               scratch_shapes=[pltpu.VMEM((tm, LRU_W), F32), pltpu.VMEM((tm, LRU_W), F32),
                               pltpu.VMEM((SUBLANES, LRU_W), F32)],
               compiler_params=_params("arbitrary"))(xc, wa, wx, ba, bx, lam, h0)


def _lru_bwd(xc, wa, wx, ba, bx, lam, h, h0, dh, reverse, name):
    t = xc.shape[0]
    tm = _tile(t)
    nt = t // tm
    n8 = tm // SUBLANES
    last8 = t // SUBLANES - 1
    tidx = (lambda i: (i, 0)) if reverse else (lambda i: (nt - 1 - i, 0))
    if reverse:
        halo = pl.BlockSpec((SUBLANES, LRU_W), lambda i: (jnp.minimum((i + 1) * n8, last8), 0))
    else:
        halo = pl.BlockSpec((SUBLANES, LRU_W), lambda i: (jnp.maximum((nt - 1 - i) * n8 - 1, 0), 0))

    def body(x_ref, wa_ref, wx_ref, ba_ref, bx_ref, lam_ref, h_ref, halo_ref, h0_ref, dh_ref,
             dx_ref, dpre_ref, dba_ref, dbx_ref, dlam_ref, dh0_ref, a_s, b_s, l_s, c_s, e_s):
        i = pl.program_id(0)

        @pl.when(i == 0)
        def _():
            c_s[...] = jnp.zeros_like(c_s)
            e_s[...] = jnp.zeros_like(e_s)
            dba_ref[...] = jnp.zeros_like(dba_ref)
            dbx_ref[...] = jnp.zeros_like(dbx_ref)
            dlam_ref[...] = jnp.zeros_like(dlam_ref)
        xv = x_ref[...]
        lam = lam_ref[...]
        r, ig, sp, a, mult = _lru_gates(xv, wa_ref, wx_ref, ba_ref[...], bx_ref[...], lam)
        hv = h_ref[...]
        rowi = lax.broadcasted_iota(jnp.int32, (tm, LRU_W), 0)
        edge_a = jnp.broadcast_to(e_s[0:1, :], (tm, LRU_W))
        h0b = jnp.broadcast_to(h0_ref[...], (tm, LRU_W))
        if reverse:
            a_sh = jnp.where(rowi == 0, edge_a, pltpu.roll(a, 1, 0))
            hin_edge = jnp.where(i == nt - 1, h0b, jnp.broadcast_to(halo_ref[0:1, :], (tm, LRU_W)))
            h_in = jnp.where(rowi == tm - 1, hin_edge, pltpu.roll(hv, tm - 1, 0))
        else:
            a_sh = jnp.where(rowi == tm - 1, edge_a, pltpu.roll(a, tm - 1, 0))
            hin_edge = jnp.where(i == nt - 1, h0b, jnp.broadcast_to(halo_ref[SUBLANES - 1:SUBLANES, :], (tm, LRU_W)))
            h_in = jnp.where(rowi == 0, hin_edge, pltpu.roll(hv, 1, 0))
        al, bl = _local_scan(a_sh, dh_ref[...], not reverse)
        a_s[...] = al
        b_s[...] = bl
        c_s[...] = _carry_scan(a_s, b_s, l_s, c_s[...], not reverse)
        e_s[...] = jnp.broadcast_to(a[tm - 1:tm, :] if reverse else a[0:1, :], e_s.shape)
        lmb = l_s[...]
        da = lmb * h_in
        ixc = ig * xv
        dmult = lmb * ixc
        dixc = lmb * mult
        dla = da * a - dmult * (a * a) / mult
        dpr = dla * (-LRU_C * sp) * r * (1.0 - r)
        dpi = dixc * xv * ig * (1.0 - ig)
        dprb = dpr.astype(BF16)
        dpib = dpi.astype(BF16)
        dpre_ref[:, 0:LRU_W] = dprb
        dpre_ref[:, LRU_W:2 * LRU_W] = dpib
        dx_ref[...] = dixc * ig + _dot_nt(dprb, wa_ref[...]) + _dot_nt(dpib, wx_ref[...])
        dba_ref[...] += _sum0(dpr)
        dbx_ref[...] += _sum0(dpi)
        dlam_ref[...] += _sum0(dla * (-LRU_C * r)) * (-_sigmoid(-lam))

        @pl.when(i == nt - 1)
        def _():
            al0 = a * lmb
            dh0_ref[...] = al0[tm - 1:tm, :] if reverse else al0[0:1, :]

    vec = _full((1, LRU_W))
    mat = _full((LRU_W, LRU_W))
    tile = pl.BlockSpec((tm, LRU_W), tidx)
    return _pc(body, name=name, grid=(nt,),
               in_specs=[tile, mat, mat, vec, vec, vec, tile, halo, vec, tile],
               out_specs=[tile, pl.BlockSpec((tm, 2 * LRU_W), tidx), vec, vec, vec, vec],
               out_shape=[_sds((t, LRU_W)), _sds((t, 2 * LRU_W), BF16), _sds((1, LRU_W)), _sds((1, LRU_W)),
                          _sds((1, LRU_W)), _sds((1, LRU_W))],
               scratch_shapes=[pltpu.VMEM((tm, LRU_W), F32), pltpu.VMEM((tm, LRU_W), F32), pltpu.VMEM((tm, LRU_W), F32),
                               pltpu.VMEM((SUBLANES, LRU_W), F32), pltpu.VMEM((SUBLANES, LRU_W), F32)],
               compiler_params=_params("arbitrary"))(xc, wa, wx, ba, bx, lam, h, h, h0, dh)


def _decay_tables(lg, reverse):
    ci = lax.broadcasted_iota(jnp.int32, (CHUNK, CHUNK), 0).astype(F32)
    mi = lax.broadcasted_iota(jnp.int32, (CHUNK, CHUNK), 1).astype(F32)
    if reverse:
        rel, pq, ps = mi - ci, CHUNK - ci, ci
    else:
        rel, pq, ps = ci - mi, ci + 1.0, CHUNK - 1.0 - ci
    relc = jnp.maximum(rel, 0.0)
    dm = jnp.where(rel >= 0, jnp.exp(lg * relc), 0.0)
    return relc, dm, jnp.exp(lg * pq), jnp.exp(lg * ps), jnp.exp(lg * float(CHUNK)), pq, ps


def _ret_fwd(proj, lgv, s0f, s0b):
    t = proj.shape[0]
    n = t // CHUNK

    def one(q_ref, k_ref, v_ref, lg, s_s, o_ref, sp_ref, reverse):
        _, dm, wq, ws, g, _, _ = _decay_tables(lg, reverse)
        q, k = q_ref[...], k_ref[...]
        vb = v_ref[...].astype(BF16)
        p = _dot_nt(q.astype(BF16), k.astype(BF16)) * dm
        s = s_s[...]
        sp_ref[0, 0] = s
        o_ref[...] = _dot(p.astype(BF16), vb) + _dot((q * wq).astype(BF16), s.astype(BF16))
        s_s[...] = g * s + _dot_tn((k * ws).astype(BF16), vb)

    def body(qf, kf, vf, qb, kb, vb, lg_ref, s0f_ref, s0b_ref, of_ref, ob_ref, spf_ref, spb_ref, sf_s, sb_s):
        @pl.when(pl.program_id(1) == 0)
        def _():
            sf_s[...] = s0f_ref[0]
            sb_s[...] = s0b_ref[0]
        one(qf, kf, vf, lg_ref[0, 0:1, :], sf_s, of_ref, spf_ref, False)
        one(qb, kb, vb, lg_ref[0, 1:2, :], sb_s, ob_ref, spb_ref, True)

    blk = (CHUNK, DH)
    fw = [pl.BlockSpec(blk, lambda h, i, o=o: (i, o + h)) for o in (0, HEADS, 2 * HEADS)]
    bw = [pl.BlockSpec(blk, lambda h, i, o=o: (n - 1 - i, o + h)) for o in (0, HEADS, 2 * HEADS)]
    st = pl.BlockSpec((1, DH, DH), lambda h, i: (h, 0, 0))
    return _pc(body, name="ret_fwd", grid=(HEADS, n),
               in_specs=fw + bw + [pl.BlockSpec((1, 2, LANES), lambda h, i: (h, 0, 0)), st, st],
               out_specs=[pl.BlockSpec(blk, lambda h, i: (i, h)), pl.BlockSpec(blk, lambda h, i: (n - 1 - i, h)),
                          pl.BlockSpec((1, 1, DH, DH), lambda h, i: (h, i, 0, 0)),
                          pl.BlockSpec((1, 1, DH, DH), lambda h, i: (h, n - 1 - i, 0, 0))],
               out_shape=[_sds((t, RET_W)), _sds((t, RET_W)), _sds((HEADS, n, DH, DH)), _sds((HEADS, n, DH, DH))],
               scratch_shapes=[pltpu.VMEM((DH, DH), F32), pltpu.VMEM((DH, DH), F32)],
               compiler_params=_params("arbitrary", "arbitrary"))(proj, proj, proj, proj, proj, proj, lgv, s0f, s0b)


def _ret_bwd(proj, lgv, sgv, sprev, do, reverse, name):
    t = proj.shape[0]
    n = t // CHUNK
    d = 1 if reverse else 0
    cidx = (lambda i: i) if reverse else (lambda i: n - 1 - i)

    def body(q_ref, k_ref, v_ref, lg_ref, sg_ref, s_ref, do_ref, dq_ref, dk_ref, dv_ref, ds0_ref, drd_ref, ds_s, acc_s):
        i = pl.program_id(1)

        @pl.when(i == 0)
        def _():
            ds_s[...] = jnp.zeros_like(ds_s)
            acc_s[...] = jnp.zeros_like(acc_s)
        relc, dm, wq, ws, g, pq, ps = _decay_tables(lg_ref[0, d:d + 1, :], reverse)
        q, k = q_ref[...], k_ref[...]
        qb, kb, vb = q.astype(BF16), k.astype(BF16), v_ref[...].astype(BF16)
        p = _dot_nt(qb, kb) * dm
        s = s_ref[0, 0]
        dob = do_ref[...].astype(BF16)
        dsn = ds_s[...]
        dsb = dsn.astype(BF16)
        dv_ref[...] = _dot_tn(p.astype(BF16), dob) + _dot((k * ws).astype(BF16), dsb)
        dp = _dot_nt(dob, vb)
        dab = (dp * dm).astype(BF16)
        xq = _dot_nt(dob, s.astype(BF16))
        yk = _dot_nt(vb, dsb)
        dq_ref[...] = _dot(dab, kb) + xq * wq
        dk_ref[...] = _dot_tn(dab, qb) + yk * ws
        ds_new = g * dsn + _dot_tn((q * wq).astype(BF16), dob)
        ds_s[...] = ds_new
        part = (_sum0(dp * p * relc) + _sum0(xq * q * wq * pq) + _sum0(yk * k * ws * ps)
                + _sum0(dsn * s) * g * float(CHUNK))
        acc_s[...] += jnp.broadcast_to(part, acc_s.shape)

        @pl.when(i == n - 1)
        def _():
            ds0_ref[0] = ds_new
            tot = jnp.sum(acc_s[0:1, :], axis=1, keepdims=True)
            drd_ref[0] = jnp.broadcast_to(tot, (SUBLANES, LANES)) * sg_ref[0, d:d + 1, :]

    blk = (CHUNK, DH)
    qkv = [pl.BlockSpec(blk, lambda h, i, o=o: (cidx(i), o + h)) for o in (0, HEADS, 2 * HEADS)]
    hc = pl.BlockSpec(blk, lambda h, i: (cidx(i), h))
    lane = pl.BlockSpec((1, 2, LANES), lambda h, i: (h, 0, 0))
    return _pc(body, name=name, grid=(HEADS, n),
               in_specs=qkv + [lane, lane, pl.BlockSpec((1, 1, DH, DH), lambda h, i: (h, cidx(i), 0, 0)), hc],
               out_specs=[hc, hc, hc, pl.BlockSpec((1, DH, DH), lambda h, i: (h, 0, 0)),
                          pl.BlockSpec((1, SUBLANES, LANES), lambda h, i: (h, 0, 0))],
               out_shape=[_sds((t, RET_W))] * 3 + [_sds((HEADS, DH, DH)), _sds((HEADS, SUBLANES, LANES))],
               scratch_shapes=[pltpu.VMEM((DH, DH), F32), pltpu.VMEM((SUBLANES, LANES), F32)],
               compiler_params=_params("arbitrary", "arbitrary"))(proj, proj, proj, lgv, sgv, sprev, do)


def _ctx_weights(lg, l_len, reverse):
    pos = lax.broadcasted_iota(jnp.int32, (l_len, DH), 0).astype(F32)
    steps = pos if reverse else (l_len - 1.0 - pos)
    return jnp.exp(lg * steps), steps


def _ctx_state_fwd(projc, lgv):
    l_len = projc.shape[0]

    def body(k_ref, v_ref, lg_ref, sf_ref, sb_ref):
        k = k_ref[...]
        vb = v_ref[...].astype(BF16)
        for d, o_ref in ((0, sf_ref), (1, sb_ref)):
            w, _ = _ctx_weights(lg_ref[0, d:d + 1, :], l_len, d == 1)
            o_ref[0] = _dot_tn((k * w).astype(BF16), vb)

    st = pl.BlockSpec((1, DH, DH), lambda h: (h, 0, 0))
    return _pc(body, name="ctx_state_fwd", grid=(HEADS,),
               in_specs=[pl.BlockSpec((l_len, DH), lambda h: (0, HEADS + h)),
                         pl.BlockSpec((l_len, DH), lambda h: (0, 2 * HEADS + h)),
                         pl.BlockSpec((1, 2, LANES), lambda h: (h, 0, 0))],
               out_specs=[st, st], out_shape=[_sds((HEADS, DH, DH))] * 2,
               compiler_params=_params("arbitrary"))(projc, projc, lgv)


def _ctx_state_bwd(projc, lgv, sgv, dsf, dsb):
    l_len = projc.shape[0]

    def body(k_ref, v_ref, lg_ref, sg_ref, dsf_ref, dsb_ref, dk_ref, dv_ref, drd_ref):
        k = k_ref[...]
        vb = v_ref[...].astype(BF16)
        dk = jnp.zeros((l_len, DH), F32)
        dv = jnp.zeros((l_len, DH), F32)
        rows = []
        for d, ds_ref in ((0, dsf_ref), (1, dsb_ref)):
            w, steps = _ctx_weights(lg_ref[0, d:d + 1, :], l_len, d == 1)
            dsb16 = ds_ref[0].astype(BF16)
            dkw = _dot_nt(vb, dsb16)
            dk = dk + dkw * w
            dv = dv + _dot((k * w).astype(BF16), dsb16)
            tot = jnp.sum(_sum0(dkw * k * w * steps), axis=1, keepdims=True)
            rows.append(jnp.broadcast_to(tot, (1, LANES)) * sg_ref[0, d:d + 1, :])
        dk_ref[...] = dk
        dv_ref[...] = dv
        rid = lax.broadcasted_iota(jnp.int32, (SUBLANES, LANES), 0)
        drd_ref[0] = jnp.where(rid == 0, rows[0], jnp.where(rid == 1, rows[1], 0.0))

    st = pl.BlockSpec((1, DH, DH), lambda h: (h, 0, 0))
    lane = pl.BlockSpec((1, 2, LANES), lambda h: (h, 0, 0))
    hc = pl.BlockSpec((l_len, DH), lambda h: (0, h))
    return _pc(body, name="ctx_state_bwd", grid=(HEADS,),
               in_specs=[pl.BlockSpec((l_len, DH), lambda h: (0, HEADS + h)),
                         pl.BlockSpec((l_len, DH), lambda h: (0, 2 * HEADS + h)), lane, lane, st, st],
               out_specs=[hc, hc, pl.BlockSpec((1, SUBLANES, LANES), lambda h: (h, 0, 0))],
               out_shape=[_sds((l_len, RET_W)), _sds((l_len, RET_W)), _sds((HEADS, SUBLANES, LANES))],
               compiler_params=_params("arbitrary"))(projc, projc, lgv, sgv, dsf, dsb)


G_BLOCK = (3 * RET_W) // RET_W
GATE_BLOCK = (4 * RET_W + LRU_W) // LRU_W


def _head_norm(y):
    yc = y - jnp.mean(y, axis=-1, keepdims=True)
    rs = lax.rsqrt(jnp.mean(yc * yc, axis=-1, keepdims=True) + EPS)
    return yc * rs, rs


def _gelu_parts(z):
    th = jnp.tanh(GELU_K * (z + GELU_C * z * z * z))
    return 0.5 * z * (1.0 + th), th


def _mix_fwd(o_f, o_b, proj, hf, hb, w_out, x, g1):
    t = x.shape[0]
    tm = _tile(t)

    def body(of_ref, ob_ref, g_ref, gt_ref, hf_ref, hb_ref, w_ref, x_ref, g1_ref, x1_ref, cat_ref):
        o = of_ref[...] + ob_ref[...]
        g = g_ref[...]
        for hh in range(HEADS):
            sl = slice(DH * hh, DH * (hh + 1))
            nrm, _ = _head_norm(o[:, sl])
            gh = g[:, sl]
            cat_ref[:, sl] = (gh * _sigmoid(gh) * nrm).astype(BF16)
        gel, _ = _gelu_parts(gt_ref[...])
        cat_ref[:, RET_W:] = ((hf_ref[...] + hb_ref[...]) * gel).astype(BF16)
        x1_ref[...] = x_ref[...] + g1_ref[...] * _dot(cat_ref[...], w_ref[...])

    half = pl.BlockSpec((tm, RET_W), lambda i: (i, 0))
    big = pl.BlockSpec((tm, D_MODEL), lambda i: (i, 0))
    return _pc(body, name="mix_fwd", grid=(t // tm,),
               in_specs=[half, half, pl.BlockSpec((tm, RET_W), lambda i: (i, G_BLOCK)),
                         pl.BlockSpec((tm, LRU_W), lambda i: (i, GATE_BLOCK)), half, half,
                         _full((D_MODEL, D_MODEL)), big, _full((1, D_MODEL))],
               out_specs=[big, big], out_shape=[_sds((t, D_MODEL)), _sds((t, D_MODEL), BF16)],
               compiler_params=_params("arbitrary"))(o_f, o_b, proj, proj, hf, hb, w_out, x, g1)


def _mix_bwd(o_f, o_b, proj, hf, hb, w_out, cat, dx1, g1):
    t = dx1.shape[0]
    tm = _tile(t)

    def body(of_ref, ob_ref, g_ref, gt_ref, hf_ref, hb_ref, w_ref, cat_ref, dx1_ref, g1_ref,
             do_ref, dhs_ref, dg_ref, dgt_ref, dyb_ref, dg1_ref):
        dx1v = dx1_ref[...]
        y = _dot(cat_ref[...], w_ref[...])

        @pl.when(pl.program_id(0) == 0)
        def _():
            dg1_ref[...] = jnp.zeros_like(dg1_ref)
        dg1_ref[...] += _sum0(dx1v * y)
        dyb = (g1_ref[...] * dx1v).astype(BF16)
        dyb_ref[...] = dyb
        dcat = _dot_nt(dyb, w_ref[...])
        o = of_ref[...] + ob_ref[...]
        g = g_ref[...]
        for hh in range(HEADS):
            sl = slice(DH * hh, DH * (hh + 1))
            nrm, rs = _head_norm(o[:, sl])
            gh = g[:, sl]
            sg = _sigmoid(gh)
            dret = dcat[:, sl]
            dg_ref[:, sl] = dret * nrm * (sg * (1.0 + gh * (1.0 - sg)))
            dn = dret * (gh * sg)
            dyc = rs * (dn - nrm * jnp.mean(dn * nrm, axis=-1, keepdims=True))
            do_ref[:, sl] = dyc - jnp.mean(dyc, axis=-1, keepdims=True)
        z = gt_ref[...]
        gel, th = _gelu_parts(z)
        dlru = dcat[:, RET_W:]
        dhs_ref[...] = dlru * gel
        dgel = 0.5 * (1.0 + th) + 0.5 * z * (1.0 - th * th) * GELU_K * (1.0 + 3.0 * GELU_C * z * z)
        dgt_ref[...] = dlru * (hf_ref[...] + hb_ref[...]) * dgel

    half = pl.BlockSpec((tm, RET_W), lambda i: (i, 0))
    big = pl.BlockSpec((tm, D_MODEL), lambda i: (i, 0))
    return _pc(body, name="mix_bwd", grid=(t // tm,),
               in_specs=[half, half, pl.BlockSpec((tm, RET_W), lambda i: (i, G_BLOCK)),
                         pl.BlockSpec((tm, LRU_W), lambda i: (i, GATE_BLOCK)), half, half,
                         _full((D_MODEL, D_MODEL)), big, big, _full((1, D_MODEL))],
               out_specs=[half, half, half, half, big, _full((1, D_MODEL))],
               out_shape=[_sds((t, RET_W))] * 4 + [_sds((t, D_MODEL), BF16), _sds((1, D_MODEL))],
               compiler_params=_params("arbitrary"))(o_f, o_b, proj, proj, hf, hb, w_out, cat, dx1, g1)


def _mlp(x1, n2g, sh2, sc2, g2, fg, w1, w2, tgt):
    t = x1.shape[0]
    tm = _tile(t)
    hb_ = MLP_H // N_CHIP

    def body(x1_ref, n2g_ref, sh2_ref, sc2_ref, g2_ref, fg_ref, w1_hbm, w2_hbm, tgt_ref,
             dx1_ref, h2b_ref, ab_ref, dub_ref, dmb_ref, dsc_ref, dsh_ref, dg2_ref, dn2_ref, dfg_ref, loss_ref,
             w1_s, w2_s, r_s, sems):
        @pl.when(pl.program_id(0) == 0)
        def _():
            c1 = pltpu.make_async_copy(w1_hbm, w1_s, sems.at[0])
            c2 = pltpu.make_async_copy(w2_hbm, w2_s, sems.at[1])
            c1.start()
            c2.start()
            for r in (dsc_ref, dsh_ref, dg2_ref, dn2_ref, dfg_ref, loss_ref):
                r[...] = jnp.zeros_like(r)
            c1.wait()
            c2.wait()
        x1v = x1_ref[...]
        n2g, sc2, g2, fg = n2g_ref[...], sc2_ref[...], g2_ref[...], fg_ref[...]
        xh, _ = _rms(x1v)
        h2b = (xh * n2g * (1.0 + sc2) + sh2_ref[...]).astype(BF16)
        h2b_ref[...] = h2b
        m = jnp.zeros((tm, D_MODEL), F32)
        for j in range(N_CHIP):
            sl = slice(hb_ * j, hb_ * (j + 1))
            r = jnp.maximum(_dot(h2b, w1_s[j]), 0.0)
            r_s[:, sl] = r
            ab = (r * r).astype(BF16)
            ab_ref[:, sl] = ab
            m = m + _dot(ab, w2_s[j])
        x2 = x1v + g2 * m
        x2h, r2 = _rms(x2)
        err = x2h * fg - tgt_ref[...]
        loss_ref[...] += _sum0(err * err)
        dout = err * (1.0 / D_MODEL)
        dfg_ref[...] += _sum0(dout * x2h)
        dxh = dout * fg
        dx2 = r2 * (dxh - x2h * jnp.mean(dxh * x2h, axis=-1, keepdims=True))
        dg2_ref[...] += _sum0(dx2 * m)
        dmb = (g2 * dx2).astype(BF16)
        dmb_ref[...] = dmb
        dh2 = jnp.zeros((tm, D_MODEL), F32)
        for j in range(N_CHIP):
            sl = slice(hb_ * j, hb_ * (j + 1))
            dub = (_dot_nt(dmb, w2_s[j]) * (2.0 * r_s[:, sl])).astype(BF16)
            dub_ref[:, sl] = dub
            dh2 = dh2 + _dot_nt(dub, w1_s[j])
        dx, dn2_t, dsh_t, dsc_t = _norm_mod_bwd(x1v, n2g, sc2, dh2)
        dx1_ref[...] = dx2 + dx
        dn2_ref[...] += dn2_t
        dsh_ref[...] += dsh_t
        dsc_ref[...] += dsc_t

        @pl.when(pl.program_id(0) == t // tm - 1)
        def _():
            tot = jnp.sum(loss_ref[...], axis=1, keepdims=True) * (0.5 / D_MODEL)
            loss_ref[...] = jnp.broadcast_to(tot, loss_ref.shape)

    row = _full((1, D_MODEL))
    big = pl.BlockSpec((tm, D_MODEL), lambda i: (i, 0))
    wide = pl.BlockSpec((tm, MLP_H), lambda i: (i, 0))
    return _pc(body, name="mlp", grid=(t // tm,),
               in_specs=[big, row, row, row, row, row, ANY, ANY, big],
               out_specs=[big, big, wide, wide, big, row, row, row, row, row, row],
               out_shape=[_sds((t, D_MODEL)), _sds((t, D_MODEL), BF16), _sds((t, MLP_H), BF16), _sds((t, MLP_H), BF16),
                          _sds((t, D_MODEL), BF16)] + [_sds((1, D_MODEL))] * 6,
               scratch_shapes=[pltpu.VMEM(w1.shape, BF16), pltpu.VMEM(w2.shape, BF16), pltpu.VMEM((tm, MLP_H), F32),
                               pltpu.SemaphoreType.DMA((2,))],
               compiler_params=_params("arbitrary"))(x1, n2g, sh2, sc2, g2, fg, w1, w2, tgt)


def _tn(a, b, nj, a_blocked, b_blocked, name, extra=None):
    t = a.shape[0]
    m = a.shape[1] // (nj if a_blocked else 1)
    n = b.shape[1] // (nj if b_blocked else 1)
    bk = 512 if t % 512 == 0 else t
    nk = t // bk
    a_map = (lambda j, k: (k, j)) if a_blocked else (lambda j, k: (k, 0))
    b_map = (lambda j, k: (k, j)) if b_blocked else (lambda j, k: (k, 0))
    in_specs = [pl.BlockSpec((bk, m), a_map), pl.BlockSpec((bk, n), b_map)]
    args = [a, b]
    if extra is not None:
        a2, b2 = extra
        t2 = a2.shape[0]
        in_specs += [pl.BlockSpec((t2, m), (lambda j, k: (0, j)) if a_blocked else (lambda j, k: (0, 0))),
                     pl.BlockSpec((t2, n), (lambda j, k: (0, j)) if b_blocked else (lambda j, k: (0, 0)))]
        args += [a2, b2]

    def body(*refs):
        a_ref, b_ref = refs[0], refs[1]
        o_ref, acc = refs[-2], refs[-1]
        k = pl.program_id(1)

        @pl.when(k == 0)
        def _():
            acc[...] = jnp.zeros_like(acc)
        acc[...] += _dot_tn(a_ref[...].astype(BF16), b_ref[...].astype(BF16))

        @pl.when(k == nk - 1)
        def _():
            if extra is not None:
                acc[...] += _dot_tn(refs[2][...].astype(BF16), refs[3][...].astype(BF16))
            o_ref[0] = acc[...]

    return _pc(body, name=name, grid=(nj, nk), in_specs=in_specs,
               out_specs=pl.BlockSpec((1, m, n), lambda j, k: (j, 0, 0)), out_shape=_sds((nj, m, n)),
               scratch_shapes=[pltpu.VMEM((m, n), F32)],
               compiler_params=_params("arbitrary", "arbitrary"))(*args)


def _block_diag(w):
    out = jnp.zeros((LRU_W, LRU_W), F32)
    for n in range(LRU_BLOCKS):
        out = lax.dynamic_update_slice(out, w[n], (LRU_BD * n, LRU_BD * n))
    return out.astype(BF16)


def _diag_blocks(mat):
    return jnp.stack([mat[LRU_BD * n:LRU_BD * (n + 1), LRU_BD * n:LRU_BD * (n + 1)] for n in range(LRU_BLOCKS)])


def _pad_rows(v, width=LANES):
    flat = v.reshape(-1).astype(F32)
    tile = SUBLANES * width
    n = -(-flat.shape[0] // tile) * tile
    return jnp.pad(flat, (0, n - flat.shape[0])).reshape(n // width, width)


class _Slab:
    def __init__(self):
        self.parts, self.meta, self.rows = [], {}, 0

    def add(self, name, v):
        p = _pad_rows(v)
        self.meta[name] = (self.rows, p.shape[0], v.shape)
        self.parts.append(p)
        self.rows += p.shape[0]

    def build(self):
        return jnp.concatenate(self.parts, axis=0)

    def take(self, slab, name, shape=None):
        r0, nr, shp = self.meta[name]
        shp = shp if shape is None else shape
        return slab[r0:r0 + nr].reshape(-1)[:math.prod(shp)].reshape(shp)


def _lane_rep(v8):
    return jnp.broadcast_to(v8.reshape(SUBLANES, 1), (SUBLANES, LANES))


def kernel(x, c, ctx, c_ctx, w_ada, b_ada, norm1_g, norm2_g, w_in, ret_decay, conv_w, conv_b, lru_wa, lru_ba, lru_wx, lru_bx, lru_lambda, w_out, w_mlp1, w_mlp2, final_g, loss_target, m_c_ctx, m_w_ada, m_b_ada, m_norm1_g, m_norm2_g, m_w_in, m_ret_decay, m_conv_w, m_conv_b, m_lru_wa, m_lru_ba, m_lru_wx, m_lru_bx, m_lru_lambda, m_w_out, m_w_mlp1, m_w_mlp2, m_final_g, v_c_ctx, v_w_ada, v_b_ada, v_norm1_g, v_norm2_g, v_w_in, v_ret_decay, v_conv_w, v_conv_b, v_lru_wa, v_lru_ba, v_lru_wx, v_lru_bx, v_lru_lambda, v_w_out, v_w_mlp1, v_w_mlp2, v_final_g):
    ax, ay, ac = lax.axis_index("x"), lax.axis_index("y"), lax.axis_index("c")
    chip = 2 * ax + ay
    dev = 4 * ax + 2 * ay + ac
    c_idx = ac.reshape(1).astype(jnp.int32)
    j_idx = chip.reshape(1).astype(jnp.int32)

    xt = x[0]
    t_len = xt.shape[0]
    ctxt = ctx[0]
    l_len = ctxt.shape[0]
    tgt = loss_target[0]
    ada_n = w_ada.shape[2]

    def my_half(w2d):
        r = w2d.shape[0] // 2
        return lax.dynamic_slice_in_dim(w2d, ac * r, r, axis=0).astype(BF16)

    gw_in, gw_out, gw_1, gw_2 = _all_gather(
        [my_half(w_in[0]), my_half(w_out[0]), my_half(w_mlp1[0]), my_half(w_mlp2[0])], "gather_weights")
    w4 = gw_in.reshape(N_CHIP, D_MODEL, IN_COLS // N_CHIP)
    wo = gw_out.reshape(D_MODEL, D_MODEL)
    w1 = gw_1.reshape(N_CHIP, D_MODEL, MLP_H // N_CHIP)
    w2 = gw_2.reshape(N_CHIP, MLP_H // N_CHIP, D_MODEL)

    (c_all,) = _all_gather([jnp.pad(c, ((0, SUBLANES - 1), (0, 0)))], "gather_c")
    a16, lgv, sgv = _prep(c_all[:, 0, :], c_ctx, ret_decay[0])
    b_shard = lax.dynamic_slice_in_dim(b_ada, chip * ada_n, ada_n, axis=1)
    (mod_parts,) = _all_gather([_mod_fwd(a16, w_ada[0], b_shard)], "gather_mod")
    mod_all = mod_parts[0::2].transpose(1, 0, 2).reshape(16, N_CHIP * ada_n)
    mod_me = lax.dynamic_slice_in_dim(mod_all, dev, 1, axis=0)
    sh1, sc1, g1, sh2, sc2, g2 = [mod_me[:, D_MODEL * k:D_MODEL * (k + 1)] for k in range(N_MOD)]
    csh1, csc1 = mod_all[8:9, 0:D_MODEL], mod_all[8:9, D_MODEL:2 * D_MODEL]

    cos2, sin2 = _rotary_tables(t_len)
    cos_c, sin_c = jnp.ones((l_len, DH), F32), jnp.zeros((l_len, DH), F32)
    n1g, n2g = norm1_g, norm2_g
    fg = final_g.reshape(1, D_MODEL)

    pad8 = lambda a: jnp.pad(a, ((0, SUBLANES - a.shape[0]), (0, 0)))
    small = jnp.concatenate([pad8(conv_w[0]), pad8(lru_ba[0]), pad8(lru_bx[0]), pad8(lru_lambda[0])], axis=0)
    (small_all,) = _all_gather([small], "gather_small_params")
    small_full = small_all[0::2].transpose(1, 0, 2).reshape(4 * SUBLANES, LRU_W)
    cw = small_full[0:4]
    cb = conv_b
    ba_f, ba_b = small_full[8:9], small_full[9:10]
    bx_f, bx_b = small_full[16:17], small_full[17:18]
    lam_f, lam_b = small_full[24:25], small_full[25:26]
    wa_f, wa_b = _block_diag(lru_wa[0, 0]), _block_diag(lru_wa[0, 1])
    wx_f, wx_b = _block_diag(lru_wx[0, 0]), _block_diag(lru_wx[0, 1])
    zero_h = jnp.zeros((1, LRU_W), F32)

    projc, hcb16 = _inproj_fwd(ctxt, n1g, csh1, csc1, w4, cos_c, sin_c, "inproj_fwd_ctx")
    s_f, s_b = _ctx_state_fwd(projc, lgv)
    xcc = _conv_fwd(projc, cw, cb, "conv_fwd_ctx")
    hcf = _lru_fwd(xcc, wa_f, wx_f, ba_f, bx_f, lam_f, zero_h, False, "lru_fwd_ctx_f")
    hcbk = _lru_fwd(xcc, wa_b, wx_b, ba_b, bx_b, lam_b, zero_h, True, "lru_fwd_ctx_b")
    lru_sf, lru_sb = hcf[l_len - 1:l_len], hcbk[0:1]

    proj, hb16 = _inproj_fwd(xt, n1g, sh1, sc1, w4, cos2, sin2, "inproj_fwd")
    o_f, o_b, spf, spb = _ret_fwd(proj, lgv, s_f, s_b)
    xcl = _conv_fwd(proj, cw, cb, "conv_fwd")
    hf = _lru_fwd(xcl, wa_f, wx_f, ba_f, bx_f, lam_f, lru_sf, False, "lru_fwd_f")
    hbk = _lru_fwd(xcl, wa_b, wx_b, ba_b, bx_b, lam_b, lru_sb, True, "lru_fwd_b")
    x1, cat = _mix_fwd(o_f, o_b, proj, hf, hbk, wo, xt, g1)

    (dx1, h2b, ab, dub, dmb, dsc2, dsh2, dg2, dn2g, dfg, lossv) = _mlp(x1, n2g, sh2, sc2, g2, fg, w1, w2, tgt)
    gw_mlp1 = _tn(h2b, dub, N_CHIP, False, True, "grad_w_mlp1")
    gw_mlp2 = _tn(ab, dmb, N_CHIP, True, False, "grad_w_mlp2")

    do, dhs, dg, dgate, dyb, dg1 = _mix_bwd(o_f, o_b, proj, hf, hbk, wo, cat, dx1, g1)
    gw_o = _tn(cat, dyb, 1, False, False, "grad_w_out")

    dq_f, dk_f, dv_f, ds_f, drd_f = _ret_bwd(proj, lgv, sgv, spf, do, False, "ret_bwd_f")
    dq_b, dk_b, dv_b, ds_b, drd_b = _ret_bwd(proj, lgv, sgv, spb, do, True, "ret_bwd_b")
    dxc_f, dpre_f, dba_f, dbx_f, dlam_f, dh0_f = _lru_bwd(xcl, wa_f, wx_f, ba_f, bx_f, lam_f, hf, lru_sf, dhs, False,
                                                         "lru_bwd_f")
    dxc_b, dpre_b, dba_b, dbx_b, dlam_b, dh0_b = _lru_bwd(xcl, wa_b, wx_b, ba_b, bx_b, lam_b, hbk, lru_sb, dhs, True,
                                                         "lru_bwd_b")
    dxr, dcw, dcb = _conv_bwd(dxc_f + dxc_b, proj, cw, "conv_bwd")
    grad_x, dpb, dn1g, dsh1, dsc1 = _inproj_bwd(
        xt, n1g, sh1, sc1, w4, cos2, sin2, [dq_f, dq_b, dk_f, dk_b, dv_f, dv_b, dg, dxr, dgate], dx1, "inproj_bwd")

    dkc, dvc, drd_c = _ctx_state_bwd(projc, lgv, sgv, ds_f, ds_b)
    zc = jnp.zeros((l_len, LRU_W), F32)
    dhc_f = lax.dynamic_update_slice(zc, dh0_f, (l_len - 1, 0))
    dhc_b = lax.dynamic_update_slice(zc, dh0_b, (0, 0))
    dxcc_f, dprec_f, dbac_f, dbxc_f, dlamc_f, _ = _lru_bwd(xcc, wa_f, wx_f, ba_f, bx_f, lam_f, hcf, zero_h, dhc_f, False,
                                                          "lru_bwd_ctx_f")
    dxcc_b, dprec_b, dbac_b, dbxc_b, dlamc_b, _ = _lru_bwd(xcc, wa_b, wx_b, ba_b, bx_b, lam_b, hcbk, zero_h, dhc_b, True,
                                                          "lru_bwd_ctx_b")
    dxrc, dcw_c, dcb_c = _conv_bwd(dxcc_f + dxcc_b, projc, cw, "conv_bwd_ctx")
    zr = jnp.zeros((l_len, RET_W), F32)
    _, dpbc, dn1g_c, dcsh1, dcsc1 = _inproj_bwd(
        ctxt, n1g, csh1, csc1, w4, cos_c, sin_c, [zr, zr, dkc, zr, dvc, zr, zr, dxrc, zr],
        jnp.zeros((l_len, D_MODEL), F32), "inproj_bwd_ctx")

    gw_i = _tn(hb16, dpb, N_CHIP, False, True, "grad_w_in", extra=(hcb16, dpbc))
    gwa_f = _tn(xcl, dpre_f, 2, False, True, "grad_lru_gates_f", extra=(xcc, dprec_f))
    gwa_b = _tn(xcl, dpre_b, 2, False, True, "grad_lru_gates_b", extra=(xcc, dprec_b))

    blocked = [gw_i.reshape(N_DEV, D_MODEL // 2, IN_COLS // N_CHIP), gw_o.reshape(N_DEV, D_MODEL // N_DEV, D_MODEL),
               gw_mlp1.reshape(N_DEV, D_MODEL // 2, MLP_H // N_CHIP), gw_mlp2.reshape(N_DEV, MLP_H // N_DEV, D_MODEL)]
    g_in, g_out, g_1, g_2 = _reduce_scatter(blocked, c_idx, j_idx)
    big = {}
    for nm, w, g, m, v in (("w_in", w_in, g_in, m_w_in, v_w_in), ("w_out", w_out, g_out, m_w_out, v_w_out),
                           ("w_mlp1", w_mlp1, g_1, m_w_mlp1, v_w_mlp1), ("w_mlp2", w_mlp2, g_2, m_w_mlp2, v_w_mlp2)):
        d_, mn, vn = _adamw(w[0], g, m[0], v[0], "adamw_" + nm)
        big[nm] = (g[None], d_[None], mn[None], vn[None])

    part = _Slab()
    part.add("loss", lossv)
    part.add("dmod", jnp.concatenate([dsh1, dsc1, dg1, dsh2, dsc2, dg2], axis=1))
    part.add("dmodc", jnp.concatenate([dcsh1, dcsc1], axis=1))
    part.add("norm1_g", dn1g + dn1g_c)
    part.add("norm2_g", dn2g)
    part.add("final_g", dfg)
    part.add("ret_decay", jnp.concatenate([drd_f[:, 0, :] + drd_c[:, 0, :], drd_b[:, 0, :] + drd_c[:, 1, :]], axis=0))
    part.add("conv_w", dcw + dcw_c)
    part.add("conv_b", dcb + dcb_c)
    part.add("lru_wa", jnp.stack([_diag_blocks(gwa_f[0]), _diag_blocks(gwa_b[0])]))
    part.add("lru_wx", jnp.stack([_diag_blocks(gwa_f[1]), _diag_blocks(gwa_b[1])]))
    part.add("lru_ba", jnp.concatenate([dba_f + dbac_f, dba_b + dbac_b], axis=0))
    part.add("lru_bx", jnp.concatenate([dbx_f + dbxc_f, dbx_b + dbxc_b], axis=0))
    part.add("lru_lambda", jnp.concatenate([dlam_f + dlamc_f, dlam_b + dlamc_b], axis=0))
    (parts_all,) = _all_gather([part.build()], "gather_small_grads")
    tot = _sum_devices(parts_all, "sum_small_grads")

    loss = part.take(tot, "loss")[0, 0]
    dmod_all = parts_all[:, part.meta["dmod"][0]:part.meta["dmod"][0] + part.meta["dmod"][1]].reshape(N_DEV, -1)
    dmodc_tot = jnp.pad(part.take(tot, "dmodc").reshape(1, -1), ((0, 0), (0, (N_MOD - 2) * D_MODEL)))
    dmod_tot = part.take(tot, "dmod").reshape(1, -1)
    grad_b_ada = dmod_tot + dmodc_tot

    cols = lambda a: lax.dynamic_slice_in_dim(a, chip * ada_n, ada_n, axis=1)
    b128 = jnp.pad(jnp.concatenate([cols(dmod_all), jnp.pad(cols(dmodc_tot), ((0, SUBLANES - 1), (0, 0)))], axis=0),
                   ((0, LANES - 2 * SUBLANES), (0, 0)))
    g_ada = _ada_grad(jnp.pad(a16.T, ((0, 0), (0, LANES - 16))), b128)
    d_ada, m_ada, v_ada = _adamw(w_ada[0], g_ada, m_w_ada[0], v_w_ada[0], "adamw_w_ada")

    dmc8 = jnp.pad(cols(dmodc_tot), ((0, SUBLANES - 1), (0, 0)))
    (cparts,) = _all_gather([_cctx_partial(dmc8, w_ada[0])], "gather_cctx")
    g_cc, d_cc, m_cc, v_cc = _cctx_final(cparts, c_ctx, m_c_ctx, v_c_ctx)

    def shard_cols(a, width=LANES):
        return lax.dynamic_slice_in_dim(a, chip * width, width, axis=a.ndim - 1)

    grads = {
        "b_ada": grad_b_ada,
        "norm1_g": part.take(tot, "norm1_g"),
        "norm2_g": part.take(tot, "norm2_g"),
        "ret_decay": part.take(tot, "ret_decay", (SUBLANES, LANES)),
        "conv_w": shard_cols(part.take(tot, "conv_w")),
        "conv_b": part.take(tot, "conv_b"),
        "lru_wa": part.take(tot, "lru_wa"),
        "lru_ba": shard_cols(part.take(tot, "lru_ba")),
        "lru_wx": part.take(tot, "lru_wx"),
        "lru_bx": shard_cols(part.take(tot, "lru_bx")),
        "lru_lambda": shard_cols(part.take(tot, "lru_lambda")),
        "final_g": part.take(tot, "final_g"),
    }
    params = {
        "b_ada": (b_ada, m_b_ada, v_b_ada), "norm1_g": (norm1_g, m_norm1_g, v_norm1_g),
        "norm2_g": (norm2_g, m_norm2_g, v_norm2_g), "ret_decay": (ret_decay, m_ret_decay, v_ret_decay),
        "conv_w": (conv_w, m_conv_w, v_conv_w), "conv_b": (conv_b, m_conv_b, v_conv_b),
        "lru_wa": (lru_wa, m_lru_wa, v_lru_wa), "lru_ba": (lru_ba, m_lru_ba, v_lru_ba),
        "lru_wx": (lru_wx, m_lru_wx, v_lru_wx), "lru_bx": (lru_bx, m_lru_bx, v_lru_bx),
        "lru_lambda": (lru_lambda, m_lru_lambda, v_lru_lambda), "final_g": (final_g, m_final_g, v_final_g),
    }
    slabs = [_Slab() for _ in range(4)]
    for nm, gval in grads.items():
        vals = (gval,) + params[nm]
        for s, val in zip(slabs, vals):
            if nm == "ret_decay" and val.shape != (SUBLANES, LANES):
                val = _lane_rep(val.reshape(-1))
            s.add(nm, val)
    gs, ws, ms, vs = [s.build() for s in slabs]
    ds_, mns, vns = _adamw(ws, gs, ms, vs, "adamw_small")
    small_out = {}
    for nm in grads:
        shp = params[nm][0].shape
        if nm == "ret_decay":
            unp = lambda sl: slabs[0].take(sl, nm, (SUBLANES, LANES))[:, 0].reshape(shp)
        else:
            unp = lambda sl: slabs[0].take(sl, nm, shp)
        small_out[nm] = (unp(gs), unp(ds_), unp(mns), unp(vns))
    small_out["c_ctx"] = tuple(a.reshape(D_MODEL) for a in (g_cc, d_cc, m_cc, v_cc))
    small_out["w_ada"] = (g_ada[None], d_ada[None], m_ada[None], v_ada[None])
    small_out.update(big)

    order = ["c_ctx", "w_ada", "b_ada", "norm1_g", "norm2_g", "w_in", "ret_decay", "conv_w", "conv_b", "lru_wa", "lru_ba",
             "lru_wx", "lru_bx", "lru_lambda", "w_out", "w_mlp1", "w_mlp2", "final_g"]
    outs = [loss, grad_x[None]]
    for k in range(4):
        outs += [small_out[nm][k] for nm in order]
    return tuple(outs)
```

```python
import math

import jax
import jax.numpy as jnp
from jax import lax
from jax.experimental import pallas as pl
from jax.experimental.pallas import tpu as pltpu

F32 = jnp.float32
BF16 = jnp.bfloat16

D_MODEL = 1024
HEADS = 4
DH = 128
CHUNK = 128
RET_W = HEADS * DH
LRU_W = 512
LRU_BLOCKS = 8
LRU_BD = LRU_W // LRU_BLOCKS
LRU_C = 8.0
IN_COLS = 4 * RET_W + 2 * LRU_W
MLP_H = 4 * D_MODEL
N_MOD = 6
GRID_W = 64
ROPE_BASE = 10000.0
K_SCALE = DH ** -0.5
EPS = 1e-6
GELU_K = math.sqrt(2.0 / math.pi)
GELU_C = 0.044715

ADAM_LR = 0.001
ADAM_B1 = 0.9
ADAM_B2 = 0.999
ADAM_EPS = 1e-08
ADAM_WD = 0.01
ADAM_STEP = 10

N_DEV = 8
N_CHIP = 4
SUBLANES = 8
LANES = 128
VMEM_LIMIT_V7X = 56 * 1024 * 1024
MESH = pl.DeviceIdType.MESH
ANY = pl.BlockSpec(memory_space=pl.ANY)


def _pc(body, **kw):
    return pl.pallas_call(body, **kw)


def _params(*sem):
    return pltpu.CompilerParams(dimension_semantics=sem if sem else None, vmem_limit_bytes=VMEM_LIMIT_V7X)


def _tile(t):
    return 256 if t >= 256 else t


def _sds(shape, dtype=F32):
    return jax.ShapeDtypeStruct(tuple(shape), dtype)


def _full(shape):
    nd = len(shape)
    return pl.BlockSpec(tuple(shape), lambda *_: (0,) * nd)


def _sigmoid(x):
    return 1.0 / (1.0 + jnp.exp(-x))


def _log1p_pos(y):
    s = y * (1.0 - y * (0.5 - y * (1.0 / 3.0 - y * (0.25 - y * (0.2 - y / 6.0)))))
    return jnp.where(y < 0.03, s, jnp.log(1.0 + y))


def _softplus(z):
    return jnp.maximum(z, 0.0) + _log1p_pos(jnp.exp(-jnp.abs(z)))


def _neg_expm1(x):
    t = x * (1.0 + x * (0.5 + x * (1.0 / 6.0 + x * (1.0 / 24.0 + x * (1.0 / 120.0 + x * (1.0 / 720.0 + x / 5040.0))))))
    return -jnp.where(x > -0.25, t, jnp.exp(x) - 1.0)


def _rms(x):
    r = lax.rsqrt(jnp.mean(x * x, axis=-1, keepdims=True) + EPS)
    return x * r, r


def _dot(a, b):
    return jnp.dot(a, b, preferred_element_type=F32)


def _dot_nt(a, b):
    return lax.dot_general(a, b, (((1,), (1,)), ((), ())), preferred_element_type=F32)


def _dot_tn(a, b):
    return lax.dot_general(a, b, (((0,), (0,)), ((), ())), preferred_element_type=F32)


def _sum0(x):
    return jnp.sum(x, axis=0, keepdims=True)


def _norm_mod_bwd(x, g, sc, dh):
    xh, r = _rms(x)
    hn = xh * g
    dhn = dh * (1.0 + sc)
    dxh = dhn * g
    dx = r * (dxh - xh * jnp.mean(dxh * xh, axis=-1, keepdims=True))
    return dx, _sum0(dhn * xh), _sum0(dh), _sum0(dh * hn)


def _dev_index(p):
    return 4 * p[0] + 2 * p[1] + p[2]


def _all_gather(arrs, name):
    n = len(arrs)

    def body(*refs):
        srcs = refs[:n]
        outs = refs[n:2 * n]
        stage = refs[2 * n:3 * n]
        send_sems, recv_sems, local_sems = refs[3 * n:]
        x, y, c = lax.axis_index("x"), lax.axis_index("y"), lax.axis_index("c")
        me, sib = (x, y, c), (x, y, 1 - c)
        chips = [(1 - x, y), (x, 1 - y), (1 - x, 1 - y)]

        def copy(t, k, block, to, src=None):
            dst = outs[t].at[_dev_index(block)]
            return pltpu.make_async_remote_copy(
                src_ref=dst if src is None else src, dst_ref=dst,
                send_sem=send_sems.at[7 * t + k], recv_sem=recv_sems.at[7 * t + k],
                device_id=to, device_id_type=MESH)

        stage_in = [pltpu.make_async_copy(srcs[t], stage[t], local_sems.at[t]) for t in range(n)]
        for cp in stage_in:
            cp.start()
        first = []
        for t in range(n):
            first.append(copy(t, 0, me, sib, src=srcs[t]))
            for j, ch in enumerate(chips):
                first.append(copy(t, 1 + j, me, (*ch, c), src=srcs[t]))
        for cp in first:
            cp.start()
        mine = [pltpu.make_async_copy(stage[t], outs[t].at[_dev_index(me)], local_sems.at[t]) for t in range(n)]
        for t in range(n):
            stage_in[t].wait()
            mine[t].start()
        passed = []
        for j, ch in enumerate(chips):
            for t in range(n):
                copy(t, 1 + j, (*ch, c), me).wait_recv()
                p = copy(t, 4 + j, (*ch, c), sib)
                p.start()
                passed.append(p)
        for t in range(n):
            copy(t, 0, sib, me).wait_recv()
            for j, ch in enumerate(chips):
                copy(t, 4 + j, (*ch, 1 - c), me).wait_recv()
        for cp in first + passed:
            cp.wait_send()
        for cp in mine:
            cp.wait()

    outs = _pc(
        body, name=name,
        out_shape=[_sds((N_DEV,) + a.shape, a.dtype) for a in arrs],
        in_specs=[ANY] * n, out_specs=[ANY] * n,
        scratch_shapes=[pltpu.VMEM(a.shape, a.dtype) for a in arrs]
        + [pltpu.SemaphoreType.DMA((7 * n,)), pltpu.SemaphoreType.DMA((7 * n,)), pltpu.SemaphoreType.DMA((n,))],
        compiler_params=_params(),
    )(*arrs)
    return list(outs)


def _pair_exchange(grads, name):
    n = len(grads)

    def body(*refs):
        srcs = refs[:n]
        outs = refs[n:2 * n]
        send_sems, recv_sems = refs[2 * n:]
        x, y, c = lax.axis_index("x"), lax.axis_index("y"), lax.axis_index("c")
        sib = (x, y, 1 - c)
        cps = []
        for t in range(n):
            for j in range(N_CHIP):
                cps.append(pltpu.make_async_remote_copy(
                    src_ref=srcs[t].at[2 * j + (1 - c)], dst_ref=outs[t].at[j],
                    send_sem=send_sems.at[4 * t + j], recv_sem=recv_sems.at[4 * t + j],
                    device_id=sib, device_id_type=MESH))
        for cp in cps:
            cp.start()
        for cp in cps:
            cp.wait()

    outs = _pc(
        body, name=name,
        out_shape=[_sds((N_CHIP,) + g.shape[1:], g.dtype) for g in grads],
        in_specs=[ANY] * n, out_specs=[ANY] * n,
        scratch_shapes=[pltpu.SemaphoreType.DMA((4 * n,)), pltpu.SemaphoreType.DMA((4 * n,))],
    )(*grads)
    return list(outs)


def _chip_exchange(parts, name):
    n = len(parts)

    def body(*refs):
        srcs = refs[:n]
        outs = refs[n:2 * n]
        send_sems, recv_sems = refs[2 * n:]
        x, y, c = lax.axis_index("x"), lax.axis_index("y"), lax.axis_index("c")
        chips = [(1 - x, y), (x, 1 - y), (1 - x, 1 - y)]
        cps = []
        for t in range(n):
            for k, ch in enumerate(chips):
                cps.append(pltpu.make_async_remote_copy(
                    src_ref=srcs[t].at[2 * ch[0] + ch[1]], dst_ref=outs[t].at[k],
                    send_sem=send_sems.at[3 * t + k], recv_sem=recv_sems.at[3 * t + k],
                    device_id=(*ch, c), device_id_type=MESH))
        for cp in cps:
            cp.start()
        for cp in cps:
            cp.wait()

    outs = _pc(
        body, name=name,
        out_shape=[_sds((3,) + p.shape[1:], p.dtype) for p in parts],
        in_specs=[ANY] * n, out_specs=[ANY] * n,
        scratch_shapes=[pltpu.SemaphoreType.DMA((3 * n,)), pltpu.SemaphoreType.DMA((3 * n,))],
    )(*parts)
    return list(outs)


def _pair_gather(bufs, name):
    n = len(bufs)

    def body(*refs):
        srcs = refs[:n]
        outs = refs[n:2 * n]
        send_sems, recv_sems = refs[2 * n:]
        x, y, c = lax.axis_index("x"), lax.axis_index("y"), lax.axis_index("c")
        sib = (x, y, 1 - c)
        snd, rcv = [], []
        for t in range(n):
            snd.append(pltpu.make_async_remote_copy(
                src_ref=srcs[t].at[c], dst_ref=outs[t].at[c], send_sem=send_sems.at[t], recv_sem=recv_sems.at[t],
                device_id=sib, device_id_type=MESH))
            rcv.append(pltpu.make_async_remote_copy(
                src_ref=srcs[t].at[c], dst_ref=outs[t].at[1 - c], send_sem=send_sems.at[t], recv_sem=recv_sems.at[t],
                device_id=sib, device_id_type=MESH))
        for cp in snd:
            cp.start()
        for cp in rcv:
            cp.wait_recv()
        for cp in snd:
            cp.wait_send()

    outs = _pc(
        body, name=name,
        out_shape=[_sds(b.shape, b.dtype) for b in bufs],
        in_specs=[ANY] * n, out_specs=[ANY] * n,
        input_output_aliases={t: t for t in range(n)},
        scratch_shapes=[pltpu.SemaphoreType.DMA((n,)), pltpu.SemaphoreType.DMA((n,))],
    )(*bufs)
    return list(outs)


def _row_block(r):
    for b in (512, 256, 128, 64, 32, 16, 8):
        if r % b == 0:
            return b
    return r


def _pair_add(g, recv, c_idx, name):
    _, r, cc = g.shape
    br = _row_block(r)

    def body(c_ref, g_ref, r_ref, p_ref, pb_ref):
        s = g_ref[...] + r_ref[...]
        p_ref[...] = s
        pb_ref[...] = s.astype(BF16)

    grid_spec = pltpu.PrefetchScalarGridSpec(
        num_scalar_prefetch=1, grid=(N_CHIP, r // br),
        in_specs=[pl.BlockSpec((1, br, cc), lambda j, i, c_ref: (2 * j + c_ref[0], i, 0)),
                  pl.BlockSpec((1, br, cc), lambda j, i, c_ref: (j, i, 0))],
        out_specs=[pl.BlockSpec((1, br, cc), lambda j, i, c_ref: (j, i, 0)),
                   pl.BlockSpec((1, br, cc), lambda j, i, c_ref: (j, i, 0))])
    return _pc(body, name=name, grid_spec=grid_spec,
               out_shape=[_sds((N_CHIP, r, cc)), _sds((N_CHIP, r, cc), BF16)],
               compiler_params=_params("arbitrary", "arbitrary"))(c_idx, g, recv)


def _chip_add(p, q, jc_idx, name):
    _, r, cc = p.shape
    br = _row_block(r)

    def body(jc_ref, p_ref, q_ref, o_ref):
        o_ref[0] = ((p_ref[0] + q_ref[0].astype(F32)) + q_ref[1].astype(F32)) + q_ref[2].astype(F32)

    grid_spec = pltpu.PrefetchScalarGridSpec(
        num_scalar_prefetch=1, grid=(r // br,),
        in_specs=[pl.BlockSpec((1, br, cc), lambda i, jc_ref: (jc_ref[0], i, 0)),
                  pl.BlockSpec((3, br, cc), lambda i, jc_ref: (0, i, 0))],
        out_specs=pl.BlockSpec((1, br, cc), lambda i, jc_ref: (jc_ref[1], i, 0)))
    return _pc(body, name=name, grid_spec=grid_spec, out_shape=_sds((2, r, cc)),
               compiler_params=_params("arbitrary"))(jc_idx, p, q)


def _reduce_scatter(grads, c_idx, j_idx):
    names = ["w_in", "w_out", "w_mlp1", "w_mlp2"]
    recv = _pair_exchange(grads, "rs_pair_exchange")
    sums, sums_b = [], []
    for t, g in enumerate(grads):
        s, sb = _pair_add(g, recv[t], c_idx, "rs_pair_add_" + names[t])
        sums.append(s)
        sums_b.append(sb)
    others = _chip_exchange(sums_b, "rs_chip_exchange")
    jc_idx = jnp.concatenate([j_idx, c_idx])
    halves = [_chip_add(sums[t], others[t], jc_idx, "rs_chip_add_" + names[t]) for t in range(len(grads))]
    both = _pair_gather(halves, "rs_pair_gather")
    return [b.reshape((2 * b.shape[1],) + b.shape[2:]) for b in both]


def _sum_devices(g, name):
    _, r, cc = g.shape

    def body(g_ref, o_ref):
        acc = g_ref[0]
        for d in range(1, N_DEV):
            acc = acc + g_ref[d]
        o_ref[...] = acc

    return _pc(body, name=name, out_shape=_sds((r, cc)), in_specs=[_full(g.shape)], out_specs=_full((r, cc)),
               compiler_params=_params())(g)


def _adamw(w, g, m, v, name):
    r, cc = w.shape
    br = _row_block(r)
    if r * cc * 4 <= (1 << 20):
        br = r
    elif br * cc * 4 > (1 << 20) and br > 8:
        br = max(8, (1 << 20) // (cc * 4) // 8 * 8)
        while r % br:
            br -= 8
    c1 = 1.0 - ADAM_B1 ** ADAM_STEP
    c2 = 1.0 - ADAM_B2 ** ADAM_STEP

    def body(w_ref, g_ref, m_ref, v_ref, d_ref, mo_ref, vo_ref):
        gg = g_ref[...]
        mn = ADAM_B1 * m_ref[...] + (1.0 - ADAM_B1) * gg
        vn = ADAM_B2 * v_ref[...] + (1.0 - ADAM_B2) * (gg * gg)
        mh = mn / c1
        vh = vn / c2
        d_ref[...] = -ADAM_LR * (mh / (jnp.sqrt(vh) + ADAM_EPS) + ADAM_WD * w_ref[...])
        mo_ref[...] = mn
        vo_ref[...] = vn

    spec = pl.BlockSpec((br, cc), lambda i: (i, 0))
    return _pc(body, name=name, grid=(r // br,), in_specs=[spec] * 4, out_specs=[spec] * 3,
               out_shape=[_sds((r, cc))] * 3, compiler_params=_params("arbitrary"))(w, g, m, v)


def _prep(c_all, c_ctx, ret_decay):
    def body(c_ref, cc_ref, rd_ref, a_ref, lg_ref, sg_ref):
        ca = c_ref[...]
        cc = cc_ref[...]
        a_ref[...] = jnp.zeros_like(a_ref)
        a_ref[0:8, :] = ca * _sigmoid(ca)
        a_ref[8:9, :] = cc * _sigmoid(cc)
        rd = rd_ref[...]
        lg_ref[...] = -_softplus(-rd)
        sg_ref[...] = _sigmoid(-rd)

    rd = jnp.broadcast_to(ret_decay.reshape(2, HEADS).T[:, :, None], (HEADS, 2, LANES))
    return _pc(body, name="prep",
               out_shape=[_sds((16, D_MODEL)), _sds((HEADS, 2, LANES)), _sds((HEADS, 2, LANES))],
               in_specs=[_full((8, D_MODEL)), _full((1, D_MODEL)), _full((HEADS, 2, LANES))],
               out_specs=[_full((16, D_MODEL)), _full((HEADS, 2, LANES)), _full((HEADS, 2, LANES))],
               compiler_params=_params())(c_all, c_ctx.reshape(1, D_MODEL), rd)


def _mod_fwd(a16, w_ada, b_shard):
    n = w_ada.shape[1]
    bn = 512

    def body(a_ref, w_ref, b_ref, o_ref):
        o_ref[...] = jnp.dot(a_ref[...], w_ref[...], preferred_element_type=F32,
                             precision=lax.Precision.HIGHEST) + b_ref[...]

    return _pc(body, name="mod_fwd", grid=(n // bn,),
               in_specs=[_full((16, D_MODEL)), pl.BlockSpec((D_MODEL, bn), lambda i: (0, i)),
                         pl.BlockSpec((1, bn), lambda i: (0, i))],
               out_specs=pl.BlockSpec((16, bn), lambda i: (0, i)), out_shape=_sds((16, n)),
               compiler_params=_params("arbitrary"))(a16, w_ada, b_shard)


def _ada_grad(at, b):
    n = b.shape[1]
    bn = 512

    def body(a_ref, b_ref, o_ref):
        o_ref[...] = jnp.dot(a_ref[...], b_ref[...], preferred_element_type=F32, precision=lax.Precision.HIGHEST)

    return _pc(body, name="ada_grad", grid=(n // bn,),
               in_specs=[_full((D_MODEL, LANES)), pl.BlockSpec((LANES, bn), lambda i: (0, i))],
               out_specs=pl.BlockSpec((D_MODEL, bn), lambda i: (0, i)), out_shape=_sds((D_MODEL, n)),
               compiler_params=_params("arbitrary"))(at, b)


def _cctx_partial(dmc8, w_ada):
    n = w_ada.shape[1]
    bn = 512

    def body(d_ref, w_ref, o_ref):
        @pl.when(pl.program_id(0) == 0)
        def _():
            o_ref[...] = jnp.zeros_like(o_ref)
        o_ref[...] += lax.dot_general(d_ref[...], w_ref[...], (((1,), (1,)), ((), ())),
                                      preferred_element_type=F32, precision=lax.Precision.HIGHEST)

    return _pc(body, name="cctx_partial", grid=(n // bn,),
               in_specs=[pl.BlockSpec((8, bn), lambda i: (0, i)), pl.BlockSpec((D_MODEL, bn), lambda i: (0, i))],
               out_specs=_full((8, D_MODEL)), out_shape=_sds((8, D_MODEL)),
               compiler_params=_params("arbitrary"))(dmc8, w_ada)


def _cctx_final(parts, c_ctx, m, v):
    c1 = 1.0 - ADAM_B1 ** ADAM_STEP
    c2 = 1.0 - ADAM_B2 ** ADAM_STEP

    def body(p_ref, c_ref, m_ref, v_ref, g_ref, d_ref, mo_ref, vo_ref):
        s = ((p_ref[0, 0:1, :] + p_ref[2, 0:1, :]) + p_ref[4, 0:1, :]) + p_ref[6, 0:1, :]
        z = c_ref[...]
        sg = _sigmoid(z)
        gg = s * (sg * (1.0 + z * (1.0 - sg)))
        g_ref[...] = gg
        mn = ADAM_B1 * m_ref[...] + (1.0 - ADAM_B1) * gg
        vn = ADAM_B2 * v_ref[...] + (1.0 - ADAM_B2) * (gg * gg)
        d_ref[...] = -ADAM_LR * ((mn / c1) / (jnp.sqrt(vn / c2) + ADAM_EPS) + ADAM_WD * z)
        mo_ref[...] = mn
        vo_ref[...] = vn

    row = _full((1, D_MODEL))
    return _pc(body, name="cctx_final", out_shape=[_sds((1, D_MODEL))] * 4,
               in_specs=[_full(parts.shape), row, row, row], out_specs=[row] * 4,
               compiler_params=_params())(parts, c_ctx.reshape(1, D_MODEL), m.reshape(1, D_MODEL), v.reshape(1, D_MODEL))


def _rotary_tables(t_len):
    rows = t_len // GRID_W
    row = jnp.repeat(jnp.arange(rows, dtype=F32), GRID_W)
    col = jnp.tile(jnp.arange(GRID_W, dtype=F32), rows)
    n_freq = DH // 4
    inv = ROPE_BASE ** (-jnp.arange(n_freq, dtype=F32) / n_freq)
    ang = jnp.concatenate([row[:, None] * inv, col[:, None] * inv], axis=-1)
    cos, sin = jnp.cos(ang), jnp.sin(ang)
    return jnp.concatenate([cos, cos], axis=-1), jnp.concatenate([-sin, sin], axis=-1)


def _inproj_fwd(x, gn, sh, sc, w4, cos2, sin2, name):
    t = x.shape[0]
    tm = _tile(t)
    nc = IN_COLS // N_CHIP

    def body(x_ref, gn_ref, sh_ref, sc_ref, w_ref, c_ref, s_ref, p_ref, hb_ref):
        xh, _ = _rms(x_ref[...])
        h = xh * gn_ref[...] * (1.0 + sc_ref[...]) + sh_ref[...]
        hb = h.astype(BF16)
        hb_ref[...] = hb
        for j in range(N_CHIP):
            p_ref[:, nc * j:nc * (j + 1)] = _dot(hb, w_ref[j])
        cc = c_ref[...]
        ss = s_ref[...]
        for hh in range(2 * HEADS):
            blk = p_ref[:, DH * hh:DH * (hh + 1)]
            rot = blk * cc + pltpu.roll(blk, DH // 2, 1) * ss
            if hh >= HEADS:
                rot = rot * K_SCALE
            p_ref[:, DH * hh:DH * (hh + 1)] = rot

    row = _full((1, D_MODEL))
    return _pc(body, name=name, grid=(t // tm,),
               in_specs=[pl.BlockSpec((tm, D_MODEL), lambda i: (i, 0)), row, row, row, _full(w4.shape),
                         pl.BlockSpec((tm, DH), lambda i: (i, 0)), pl.BlockSpec((tm, DH), lambda i: (i, 0))],
               out_specs=[pl.BlockSpec((tm, IN_COLS), lambda i: (i, 0)), pl.BlockSpec((tm, D_MODEL), lambda i: (i, 0))],
               out_shape=[_sds((t, IN_COLS)), _sds((t, D_MODEL), BF16)],
               compiler_params=_params("arbitrary"))(x, gn, sh, sc, w4, cos2, sin2)


def _inproj_bwd(x, gn, sh, sc, w4, cos2, sin2, pieces, dres, name):
    t = x.shape[0]
    tm = _tile(t)
    nc = IN_COLS // N_CHIP

    def body(x_ref, gn_ref, sh_ref, sc_ref, w_ref, c_ref, s_ref, dqf, dqb, dkf, dkb, dvf, dvb, dg, dxr, dgt, dres_ref,
             dx_ref, dpb_ref, dgn_ref, dsh_ref, dsc_ref):
        cc = c_ref[...]
        ss = s_ref[...]
        dq = dqf[...] + dqb[...]
        dk = dkf[...] + dkb[...]
        for hh in range(HEADS):
            sl = slice(DH * hh, DH * (hh + 1))
            b = dq[:, sl]
            dpb_ref[:, sl] = (b * cc + pltpu.roll(b * ss, DH // 2, 1)).astype(BF16)
            b = dk[:, sl]
            dpb_ref[:, RET_W + DH * hh:RET_W + DH * (hh + 1)] = (
                (b * cc + pltpu.roll(b * ss, DH // 2, 1)) * K_SCALE).astype(BF16)
        dpb_ref[:, 2 * RET_W:3 * RET_W] = (dvf[...] + dvb[...]).astype(BF16)
        dpb_ref[:, 3 * RET_W:4 * RET_W] = dg[...].astype(BF16)
        dpb_ref[:, 4 * RET_W:4 * RET_W + LRU_W] = dxr[...].astype(BF16)
        dpb_ref[:, 4 * RET_W + LRU_W:IN_COLS] = dgt[...].astype(BF16)
        dh = _dot_nt(dpb_ref[:, 0:nc], w_ref[0])
        for j in range(1, N_CHIP):
            dh = dh + _dot_nt(dpb_ref[:, nc * j:nc * (j + 1)], w_ref[j])
        dx, dgn_t, dsh_t, dsc_t = _norm_mod_bwd(x_ref[...], gn_ref[...], sc_ref[...], dh)
        dx_ref[...] = dres_ref[...] + dx

        @pl.when(pl.program_id(0) == 0)
        def _():
            dgn_ref[...] = jnp.zeros_like(dgn_ref)
            dsh_ref[...] = jnp.zeros_like(dsh_ref)
            dsc_ref[...] = jnp.zeros_like(dsc_ref)
        dgn_ref[...] += dgn_t
        dsh_ref[...] += dsh_t
        dsc_ref[...] += dsc_t

    row = _full((1, D_MODEL))
    pc = pl.BlockSpec((tm, RET_W), lambda i: (i, 0))
    big = pl.BlockSpec((tm, D_MODEL), lambda i: (i, 0))
    return _pc(body, name=name, grid=(t // tm,),
               in_specs=[big, row, row, row, _full(w4.shape),
                         pl.BlockSpec((tm, DH), lambda i: (i, 0)), pl.BlockSpec((tm, DH), lambda i: (i, 0))]
               + [pc] * 9 + [big],
               out_specs=[big, pl.BlockSpec((tm, IN_COLS), lambda i: (i, 0)), row, row, row],
               out_shape=[_sds((t, D_MODEL)), _sds((t, IN_COLS), BF16), _sds((1, D_MODEL)), _sds((1, D_MODEL)),
                          _sds((1, D_MODEL))],
               compiler_params=_params("arbitrary"))(x, gn, sh, sc, w4, cos2, sin2, *pieces, dres)


XR_BLOCK = (4 * RET_W) // LRU_W


def _halo_specs(t, tm, col):
    n8 = tm // SUBLANES
    last8 = t // SUBLANES - 1
    prev = pl.BlockSpec((SUBLANES, LRU_W), lambda i: (jnp.maximum(i * n8 - 1, 0), col))
    main = pl.BlockSpec((tm, LRU_W), lambda i: (i, col))
    nxt = pl.BlockSpec((SUBLANES, LRU_W), lambda i: (jnp.minimum((i + 1) * n8, last8), col))
    return prev, main, nxt


def _with_halo(prev_ref, main_ref, next_ref, i, nt):
    prev = jnp.where(i > 0, prev_ref[...], 0.0)
    nxt = jnp.where(i < nt - 1, next_ref[...], 0.0)
    return jnp.concatenate([prev, main_ref[...], nxt], axis=0)


def _conv_fwd(proj, cw, cb, name):
    t = proj.shape[0]
    tm = _tile(t)
    nt = t // tm
    n = tm + 2 * SUBLANES
    mid = slice(SUBLANES, SUBLANES + tm)

    def body(p_ref, m_ref, n_ref, w_ref, b_ref, o_ref):
        xp = _with_halo(p_ref, m_ref, n_ref, pl.program_id(0), nt)
        acc = b_ref[...] + pltpu.roll(xp, 1, 0)[mid] * w_ref[0:1, :]
        acc = acc + xp[mid] * w_ref[1:2, :]
        acc = acc + pltpu.roll(xp, n - 1, 0)[mid] * w_ref[2:3, :]
        acc = acc + pltpu.roll(xp, n - 2, 0)[mid] * w_ref[3:4, :]
        o_ref[...] = acc

    return _pc(body, name=name, grid=(nt,),
               in_specs=[*_halo_specs(t, tm, XR_BLOCK), _full((4, LRU_W)), _full((1, LRU_W))],
               out_specs=pl.BlockSpec((tm, LRU_W), lambda i: (i, 0)), out_shape=_sds((t, LRU_W)),
               compiler_params=_params("arbitrary"))(proj, proj, proj, cw, cb)


def _conv_bwd(dxc, proj, cw, name):
    t = proj.shape[0]
    tm = _tile(t)
    nt = t // tm
    n = tm + 2 * SUBLANES
    mid = slice(SUBLANES, SUBLANES + tm)

    def body(dp_ref, dm_ref, dn_ref, xp_ref, xm_ref, xn_ref, w_ref, dx_ref, dw_ref, db_ref):
        i = pl.program_id(0)
        dp = _with_halo(dp_ref, dm_ref, dn_ref, i, nt)
        xp = _with_halo(xp_ref, xm_ref, xn_ref, i, nt)
        dx = pltpu.roll(dp, n - 1, 0)[mid] * w_ref[0:1, :]
        dx = dx + dp[mid] * w_ref[1:2, :]
        dx = dx + pltpu.roll(dp, 1, 0)[mid] * w_ref[2:3, :]
        dx = dx + pltpu.roll(dp, 2, 0)[mid] * w_ref[3:4, :]
        dx_ref[...] = dx
        d = dm_ref[...]

        @pl.when(i == 0)
        def _():
            dw_ref[...] = jnp.zeros_like(dw_ref)
            db_ref[...] = jnp.zeros_like(db_ref)
        dw_ref[0:1, :] += _sum0(d * pltpu.roll(xp, 1, 0)[mid])
        dw_ref[1:2, :] += _sum0(d * xp[mid])
        dw_ref[2:3, :] += _sum0(d * pltpu.roll(xp, n - 1, 0)[mid])
        dw_ref[3:4, :] += _sum0(d * pltpu.roll(xp, n - 2, 0)[mid])
        db_ref[...] += _sum0(d)

    return _pc(body, name=name, grid=(nt,),
               in_specs=[*_halo_specs(t, tm, 0), *_halo_specs(t, tm, XR_BLOCK), _full((4, LRU_W))],
               out_specs=[pl.BlockSpec((tm, LRU_W), lambda i: (i, 0)), _full((4, LRU_W)), _full((1, LRU_W))],
               out_shape=[_sds((t, LRU_W)), _sds((4, LRU_W)), _sds((1, LRU_W))],
               compiler_params=_params("arbitrary"))(dxc, dxc, dxc, proj, proj, proj, cw)


def _local_scan(a, b, reverse):
    n = a.shape[0]
    row = lax.broadcasted_iota(jnp.int32, a.shape, 0) & (SUBLANES - 1)
    for s in (1, 2, 4):
        if reverse:
            a_s, b_s, ok = pltpu.roll(a, n - s, 0), pltpu.roll(b, n - s, 0), row < SUBLANES - s
        else:
            a_s, b_s, ok = pltpu.roll(a, s, 0), pltpu.roll(b, s, 0), row >= s
        b = a * jnp.where(ok, b_s, 0.0) + b
        a = a * jnp.where(ok, a_s, 1.0)
    return a, b


def _carry_scan(a_s, b_s, out_ref, carry, reverse):
    ng = a_s.shape[0] // SUBLANES
    shape = carry.shape

    def step(g, cr):
        gg = (ng - 1 - g) if reverse else g
        off = pl.multiple_of(gg * SUBLANES, SUBLANES)
        h = a_s[pl.ds(off, SUBLANES), :] * cr + b_s[pl.ds(off, SUBLANES), :]
        out_ref[pl.ds(off, SUBLANES), :] = h
        edge = h[0:1, :] if reverse else h[SUBLANES - 1:SUBLANES, :]
        return jnp.broadcast_to(edge, shape)

    return lax.fori_loop(0, ng, step, carry)


def _lru_gates(xc, wa_ref, wx_ref, ba, bx, lam):
    xb = xc.astype(BF16)
    r = _sigmoid(_dot(xb, wa_ref[...]) + ba)
    ig = _sigmoid(_dot(xb, wx_ref[...]) + bx)
    sp = _softplus(-lam)
    la = -LRU_C * r * sp
    a = jnp.exp(la)
    mult = jnp.sqrt(_neg_expm1(2.0 * la))
    return r, ig, sp, a, mult


def _lru_fwd(xc, wa, wx, ba, bx, lam, h0, reverse, name):
    t = xc.shape[0]
    tm = _tile(t)
    nt = t // tm
    tidx = (lambda i: (nt - 1 - i, 0)) if reverse else (lambda i: (i, 0))

    def body(x_ref, wa_ref, wx_ref, ba_ref, bx_ref, lam_ref, h0_ref, h_ref, a_s, b_s, c_s):
        @pl.when(pl.program_id(0) == 0)
        def _():
            c_s[...] = jnp.broadcast_to(h0_ref[...], c_s.shape)
        xv = x_ref[...]
        _, ig, _, a, mult = _lru_gates(xv, wa_ref, wx_ref, ba_ref[...], bx_ref[...], lam_ref[...])
        al, bl = _local_scan(a, mult * (ig * xv), reverse)
        a_s[...] = al
        b_s[...] = bl
        c_s[...] = _carry_scan(a_s, b_s, h_ref, c_s[...], reverse)

    vec = _full((1, LRU_W))
    mat = _full((LRU_W, LRU_W))
    return _pc(body, name=name, grid=(nt,),
               in_specs=[pl.BlockSpec((tm, LRU_W), tidx), mat, mat, vec, vec, vec, vec],
               out_specs=pl.BlockSpec((tm, LRU_W), tidx), out_shape=_sds((t, LRU_W)),
               scratch_shapes=[pltpu.VMEM((tm, LRU_W), F32), pltpu.VMEM((tm, LRU_W), F32),
                               pltpu.VMEM((SUBLANES, LRU_W), F32)],
               compiler_params=_params("arbitrary"))(xc, wa, wx, ba, bx, lam, h0)


def _lru_bwd(xc, wa, wx, ba, bx, lam, h, h0, dh, reverse, name):
    t = xc.shape[0]
    tm = _tile(t)
    nt = t // tm
    n8 = tm // SUBLANES
    last8 = t // SUBLANES - 1
    tidx = (lambda i: (i, 0)) if reverse else (lambda i: (nt - 1 - i, 0))
    if reverse:
        halo = pl.BlockSpec((SUBLANES, LRU_W), lambda i: (jnp.minimum((i + 1) * n8, last8), 0))
    else:
        halo = pl.BlockSpec((SUBLANES, LRU_W), lambda i: (jnp.maximum((nt - 1 - i) * n8 - 1, 0), 0))

    def body(x_ref, wa_ref, wx_ref, ba_ref, bx_ref, lam_ref, h_ref, halo_ref, h0_ref, dh_ref,
             dx_ref, dpre_ref, dba_ref, dbx_ref, dlam_ref, dh0_ref, a_s, b_s, l_s, c_s, e_s):
        i = pl.program_id(0)

        @pl.when(i == 0)
        def _():
            c_s[...] = jnp.zeros_like(c_s)
            e_s[...] = jnp.zeros_like(e_s)
            dba_ref[...] = jnp.zeros_like(dba_ref)
            dbx_ref[...] = jnp.zeros_like(dbx_ref)
            dlam_ref[...] = jnp.zeros_like(dlam_ref)
        xv = x_ref[...]
        lam = lam_ref[...]
        r, ig, sp, a, mult = _lru_gates(xv, wa_ref, wx_ref, ba_ref[...], bx_ref[...], lam)
        hv = h_ref[...]
        rowi = lax.broadcasted_iota(jnp.int32, (tm, LRU_W), 0)
        edge_a = jnp.broadcast_to(e_s[0:1, :], (tm, LRU_W))
        h0b = jnp.broadcast_to(h0_ref[...], (tm, LRU_W))
        if reverse:
            a_sh = jnp.where(rowi == 0, edge_a, pltpu.roll(a, 1, 0))
            hin_edge = jnp.where(i == nt - 1, h0b, jnp.broadcast_to(halo_ref[0:1, :], (tm, LRU_W)))
            h_in = jnp.where(rowi == tm - 1, hin_edge, pltpu.roll(hv, tm - 1, 0))
        else:
            a_sh = jnp.where(rowi == tm - 1, edge_a, pltpu.roll(a, tm - 1, 0))
            hin_edge = jnp.where(i == nt - 1, h0b, jnp.broadcast_to(halo_ref[SUBLANES - 1:SUBLANES, :], (tm, LRU_W)))
            h_in = jnp.where(rowi == 0, hin_edge, pltpu.roll(hv, 1, 0))
        al, bl = _local_scan(a_sh, dh_ref[...], not reverse)
        a_s[...] = al
        b_s[...] = bl
        c_s[...] = _carry_scan(a_s, b_s, l_s, c_s[...], not reverse)
        e_s[...] = jnp.broadcast_to(a[tm - 1:tm, :] if reverse else a[0:1, :], e_s.shape)
        lmb = l_s[...]
        da = lmb * h_in
        ixc = ig * xv
        dmult = lmb * ixc
        dixc = lmb * mult
        dla = da * a - dmult * (a * a) / mult
        dpr = dla * (-LRU_C * sp) * r * (1.0 - r)
        dpi = dixc * xv * ig * (1.0 - ig)
        dprb = dpr.astype(BF16)
        dpib = dpi.astype(BF16)
        dpre_ref[:, 0:LRU_W] = dprb
        dpre_ref[:, LRU_W:2 * LRU_W] = dpib
        dx_ref[...] = dixc * ig + _dot_nt(dprb, wa_ref[...]) + _dot_nt(dpib, wx_ref[...])
        dba_ref[...] += _sum0(dpr)
        dbx_ref[...] += _sum0(dpi)
        dlam_ref[...] += _sum0(dla * (-LRU_C * r)) * (-_sigmoid(-lam))

        @pl.when(i == nt - 1)
        def _():
            al0 = a * lmb
            dh0_ref[...] = al0[tm - 1:tm, :] if reverse else al0[0:1, :]

    vec = _full((1, LRU_W))
    mat = _full((LRU_W, LRU_W))
    tile = pl.BlockSpec((tm, LRU_W), tidx)
    return _pc(body, name=name, grid=(nt,),
               in_specs=[tile, mat, mat, vec, vec, vec, tile, halo, vec, tile],
               out_specs=[tile, pl.BlockSpec((tm, 2 * LRU_W), tidx), vec, vec, vec, vec],
               out_shape=[_sds((t, LRU_W)), _sds((t, 2 * LRU_W), BF16), _sds((1, LRU_W)), _sds((1, LRU_W)),
                          _sds((1, LRU_W)), _sds((1, LRU_W))],
               scratch_shapes=[pltpu.VMEM((tm, LRU_W), F32), pltpu.VMEM((tm, LRU_W), F32), pltpu.VMEM((tm, LRU_W), F32),
                               pltpu.VMEM((SUBLANES, LRU_W), F32), pltpu.VMEM((SUBLANES, LRU_W), F32)],
               compiler_params=_params("arbitrary"))(xc, wa, wx, ba, bx, lam, h, h, h0, dh)


def _decay_tables(lg, reverse):
    ci = lax.broadcasted_iota(jnp.int32, (CHUNK, CHUNK), 0).astype(F32)
    mi = lax.broadcasted_iota(jnp.int32, (CHUNK, CHUNK), 1).astype(F32)
    if reverse:
        rel, pq, ps = mi - ci, CHUNK - ci, ci
    else:
        rel, pq, ps = ci - mi, ci + 1.0, CHUNK - 1.0 - ci
    relc = jnp.maximum(rel, 0.0)
    dm = jnp.where(rel >= 0, jnp.exp(lg * relc), 0.0)
    return relc, dm, jnp.exp(lg * pq), jnp.exp(lg * ps), jnp.exp(lg * float(CHUNK)), pq, ps


def _ret_fwd(proj, lgv, s0f, s0b):
    t = proj.shape[0]
    n = t // CHUNK

    def one(q_ref, k_ref, v_ref, lg, s_s, o_ref, sp_ref, reverse):
        _, dm, wq, ws, g, _, _ = _decay_tables(lg, reverse)
        q, k = q_ref[...], k_ref[...]
        vb = v_ref[...].astype(BF16)
        p = _dot_nt(q.astype(BF16), k.astype(BF16)) * dm
        s = s_s[...]
        sp_ref[0, 0] = s
        o_ref[...] = _dot(p.astype(BF16), vb) + _dot((q * wq).astype(BF16), s.astype(BF16))
        s_s[...] = g * s + _dot_tn((k * ws).astype(BF16), vb)

    def body(qf, kf, vf, qb, kb, vb, lg_ref, s0f_ref, s0b_ref, of_ref, ob_ref, spf_ref, spb_ref, sf_s, sb_s):
        @pl.when(pl.program_id(1) == 0)
        def _():
            sf_s[...] = s0f_ref[0]
            sb_s[...] = s0b_ref[0]
        one(qf, kf, vf, lg_ref[0, 0:1, :], sf_s, of_ref, spf_ref, False)
        one(qb, kb, vb, lg_ref[0, 1:2, :], sb_s, ob_ref, spb_ref, True)

    blk = (CHUNK, DH)
    fw = [pl.BlockSpec(blk, lambda h, i, o=o: (i, o + h)) for o in (0, HEADS, 2 * HEADS)]
    bw = [pl.BlockSpec(blk, lambda h, i, o=o: (n - 1 - i, o + h)) for o in (0, HEADS, 2 * HEADS)]
    st = pl.BlockSpec((1, DH, DH), lambda h, i: (h, 0, 0))
    return _pc(body, name="ret_fwd", grid=(HEADS, n),
               in_specs=fw + bw + [pl.BlockSpec((1, 2, LANES), lambda h, i: (h, 0, 0)), st, st],
               out_specs=[pl.BlockSpec(blk, lambda h, i: (i, h)), pl.BlockSpec(blk, lambda h, i: (n - 1 - i, h)),
                          pl.BlockSpec((1, 1, DH, DH), lambda h, i: (h, i, 0, 0)),
                          pl.BlockSpec((1, 1, DH, DH), lambda h, i: (h, n - 1 - i, 0, 0))],
               out_shape=[_sds((t, RET_W)), _sds((t, RET_W)), _sds((HEADS, n, DH, DH)), _sds((HEADS, n, DH, DH))],
               scratch_shapes=[pltpu.VMEM((DH, DH), F32), pltpu.VMEM((DH, DH), F32)],
               compiler_params=_params("arbitrary", "arbitrary"))(proj, proj, proj, proj, proj, proj, lgv, s0f, s0b)


def _ret_bwd(proj, lgv, sgv, sprev, do, reverse, name):
    t = proj.shape[0]
    n = t // CHUNK
    d = 1 if reverse else 0
    cidx = (lambda i: i) if reverse else (lambda i: n - 1 - i)

    def body(q_ref, k_ref, v_ref, lg_ref, sg_ref, s_ref, do_ref, dq_ref, dk_ref, dv_ref, ds0_ref, drd_ref, ds_s, acc_s):
        i = pl.program_id(1)

        @pl.when(i == 0)
        def _():
            ds_s[...] = jnp.zeros_like(ds_s)
            acc_s[...] = jnp.zeros_like(acc_s)
        relc, dm, wq, ws, g, pq, ps = _decay_tables(lg_ref[0, d:d + 1, :], reverse)
        q, k = q_ref[...], k_ref[...]
        qb, kb, vb = q.astype(BF16), k.astype(BF16), v_ref[...].astype(BF16)
        p = _dot_nt(qb, kb) * dm
        s = s_ref[0, 0]
        dob = do_ref[...].astype(BF16)
        dsn = ds_s[...]
        dsb = dsn.astype(BF16)
        dv_ref[...] = _dot_tn(p.astype(BF16), dob) + _dot((k * ws).astype(BF16), dsb)
        dp = _dot_nt(dob, vb)
        dab = (dp * dm).astype(BF16)
        xq = _dot_nt(dob, s.astype(BF16))
        yk = _dot_nt(vb, dsb)
        dq_ref[...] = _dot(dab, kb) + xq * wq
        dk_ref[...] = _dot_tn(dab, qb) + yk * ws
        ds_new = g * dsn + _dot_tn((q * wq).astype(BF16), dob)
        ds_s[...] = ds_new
        part = (_sum0(dp * p * relc) + _sum0(xq * q * wq * pq) + _sum0(yk * k * ws * ps)
                + _sum0(dsn * s) * g * float(CHUNK))
        acc_s[...] += jnp.broadcast_to(part, acc_s.shape)

        @pl.when(i == n - 1)
        def _():
            ds0_ref[0] = ds_new
            tot = jnp.sum(acc_s[0:1, :], axis=1, keepdims=True)
            drd_ref[0] = jnp.broadcast_to(tot, (SUBLANES, LANES)) * sg_ref[0, d:d + 1, :]

    blk = (CHUNK, DH)
    qkv = [pl.BlockSpec(blk, lambda h, i, o=o: (cidx(i), o + h)) for o in (0, HEADS, 2 * HEADS)]
    hc = pl.BlockSpec(blk, lambda h, i: (cidx(i), h))
    lane = pl.BlockSpec((1, 2, LANES), lambda h, i: (h, 0, 0))
    return _pc(body, name=name, grid=(HEADS, n),
               in_specs=qkv + [lane, lane, pl.BlockSpec((1, 1, DH, DH), lambda h, i: (h, cidx(i), 0, 0)), hc],
               out_specs=[hc, hc, hc, pl.BlockSpec((1, DH, DH), lambda h, i: (h, 0, 0)),
                          pl.BlockSpec((1, SUBLANES, LANES), lambda h, i: (h, 0, 0))],
               out_shape=[_sds((t, RET_W))] * 3 + [_sds((HEADS, DH, DH)), _sds((HEADS, SUBLANES, LANES))],
               scratch_shapes=[pltpu.VMEM((DH, DH), F32), pltpu.VMEM((SUBLANES, LANES), F32)],
               compiler_params=_params("arbitrary", "arbitrary"))(proj, proj, proj, lgv, sgv, sprev, do)


def _ctx_weights(lg, l_len, reverse):
    pos = lax.broadcasted_iota(jnp.int32, (l_len, DH), 0).astype(F32)
    steps = pos if reverse else (l_len - 1.0 - pos)
    return jnp.exp(lg * steps), steps


def _ctx_state_fwd(projc, lgv):
    l_len = projc.shape[0]

    def body(k_ref, v_ref, lg_ref, sf_ref, sb_ref):
        k = k_ref[...]
        vb = v_ref[...].astype(BF16)
        for d, o_ref in ((0, sf_ref), (1, sb_ref)):
            w, _ = _ctx_weights(lg_ref[0, d:d + 1, :], l_len, d == 1)
            o_ref[0] = _dot_tn((k * w).astype(BF16), vb)

    st = pl.BlockSpec((1, DH, DH), lambda h: (h, 0, 0))
    return _pc(body, name="ctx_state_fwd", grid=(HEADS,),
               in_specs=[pl.BlockSpec((l_len, DH), lambda h: (0, HEADS + h)),
                         pl.BlockSpec((l_len, DH), lambda h: (0, 2 * HEADS + h)),
                         pl.BlockSpec((1, 2, LANES), lambda h: (h, 0, 0))],
               out_specs=[st, st], out_shape=[_sds((HEADS, DH, DH))] * 2,
               compiler_params=_params("arbitrary"))(projc, projc, lgv)


def _ctx_state_bwd(projc, lgv, sgv, dsf, dsb):
    l_len = projc.shape[0]

    def body(k_ref, v_ref, lg_ref, sg_ref, dsf_ref, dsb_ref, dk_ref, dv_ref, drd_ref):
        k = k_ref[...]
        vb = v_ref[...].astype(BF16)
        dk = jnp.zeros((l_len, DH), F32)
        dv = jnp.zeros((l_len, DH), F32)
        rows = []
        for d, ds_ref in ((0, dsf_ref), (1, dsb_ref)):
            w, steps = _ctx_weights(lg_ref[0, d:d + 1, :], l_len, d == 1)
            dsb16 = ds_ref[0].astype(BF16)
            dkw = _dot_nt(vb, dsb16)
            dk = dk + dkw * w
            dv = dv + _dot((k * w).astype(BF16), dsb16)
            tot = jnp.sum(_sum0(dkw * k * w * steps), axis=1, keepdims=True)
            rows.append(jnp.broadcast_to(tot, (1, LANES)) * sg_ref[0, d:d + 1, :])
        dk_ref[...] = dk
        dv_ref[...] = dv
        rid = lax.broadcasted_iota(jnp.int32, (SUBLANES, LANES), 0)
        drd_ref[0] = jnp.where(rid == 0, rows[0], jnp.where(rid == 1, rows[1], 0.0))

    st = pl.BlockSpec((1, DH, DH), lambda h: (h, 0, 0))
    lane = pl.BlockSpec((1, 2, LANES), lambda h: (h, 0, 0))
    hc = pl.BlockSpec((l_len, DH), lambda h: (0, h))
    return _pc(body, name="ctx_state_bwd", grid=(HEADS,),
               in_specs=[pl.BlockSpec((l_len, DH), lambda h: (0, HEADS + h)),
                         pl.BlockSpec((l_len, DH), lambda h: (0, 2 * HEADS + h)), lane, lane, st, st],
               out_specs=[hc, hc, pl.BlockSpec((1, SUBLANES, LANES), lambda h: (h, 0, 0))],
               out_shape=[_sds((l_len, RET_W)), _sds((l_len, RET_W)), _sds((HEADS, SUBLANES, LANES))],
               compiler_params=_params("arbitrary"))(projc, projc, lgv, sgv, dsf, dsb)


G_BLOCK = (3 * RET_W) // RET_W
GATE_BLOCK = (4 * RET_W + LRU_W) // LRU_W


def _head_norm(y):
    yc = y - jnp.mean(y, axis=-1, keepdims=True)
    rs = lax.rsqrt(jnp.mean(yc * yc, axis=-1, keepdims=True) + EPS)
    return yc * rs, rs


def _gelu_parts(z):
    th = jnp.tanh(GELU_K * (z + GELU_C * z * z * z))
    return 0.5 * z * (1.0 + th), th


def _mix_fwd(o_f, o_b, proj, hf, hb, w_out, x, g1):
    t = x.shape[0]
    tm = _tile(t)

    def body(of_ref, ob_ref, g_ref, gt_ref, hf_ref, hb_ref, w_ref, x_ref, g1_ref, x1_ref, cat_ref):
        o = of_ref[...] + ob_ref[...]
        g = g_ref[...]
        for hh in range(HEADS):
            sl = slice(DH * hh, DH * (hh + 1))
            nrm, _ = _head_norm(o[:, sl])
            gh = g[:, sl]
            cat_ref[:, sl] = (gh * _sigmoid(gh) * nrm).astype(BF16)
        gel, _ = _gelu_parts(gt_ref[...])
        cat_ref[:, RET_W:] = ((hf_ref[...] + hb_ref[...]) * gel).astype(BF16)
        x1_ref[...] = x_ref[...] + g1_ref[...] * _dot(cat_ref[...], w_ref[...])

    half = pl.BlockSpec((tm, RET_W), lambda i: (i, 0))
    big = pl.BlockSpec((tm, D_MODEL), lambda i: (i, 0))
    return _pc(body, name="mix_fwd", grid=(t // tm,),
               in_specs=[half, half, pl.BlockSpec((tm, RET_W), lambda i: (i, G_BLOCK)),
                         pl.BlockSpec((tm, LRU_W), lambda i: (i, GATE_BLOCK)), half, half,
                         _full((D_MODEL, D_MODEL)), big, _full((1, D_MODEL))],
               out_specs=[big, big], out_shape=[_sds((t, D_MODEL)), _sds((t, D_MODEL), BF16)],
               compiler_params=_params("arbitrary"))(o_f, o_b, proj, proj, hf, hb, w_out, x, g1)


def _mix_bwd(o_f, o_b, proj, hf, hb, w_out, cat, dx1, g1):
    t = dx1.shape[0]
    tm = _tile(t)

    def body(of_ref, ob_ref, g_ref, gt_ref, hf_ref, hb_ref, w_ref, cat_ref, dx1_ref, g1_ref,
             do_ref, dhs_ref, dg_ref, dgt_ref, dyb_ref, dg1_ref):
        dx1v = dx1_ref[...]
        y = _dot(cat_ref[...], w_ref[...])

        @pl.when(pl.program_id(0) == 0)
        def _():
            dg1_ref[...] = jnp.zeros_like(dg1_ref)
        dg1_ref[...] += _sum0(dx1v * y)
        dyb = (g1_ref[...] * dx1v).astype(BF16)
        dyb_ref[...] = dyb
        dcat = _dot_nt(dyb, w_ref[...])
        o = of_ref[...] + ob_ref[...]
        g = g_ref[...]
        for hh in range(HEADS):
            sl = slice(DH * hh, DH * (hh + 1))
            nrm, rs = _head_norm(o[:, sl])
            gh = g[:, sl]
            sg = _sigmoid(gh)
            dret = dcat[:, sl]
            dg_ref[:, sl] = dret * nrm * (sg * (1.0 + gh * (1.0 - sg)))
            dn = dret * (gh * sg)
            dyc = rs * (dn - nrm * jnp.mean(dn * nrm, axis=-1, keepdims=True))
            do_ref[:, sl] = dyc - jnp.mean(dyc, axis=-1, keepdims=True)
        z = gt_ref[...]
        gel, th = _gelu_parts(z)
        dlru = dcat[:, RET_W:]
        dhs_ref[...] = dlru * gel
        dgel = 0.5 * (1.0 + th) + 0.5 * z * (1.0 - th * th) * GELU_K * (1.0 + 3.0 * GELU_C * z * z)
        dgt_ref[...] = dlru * (hf_ref[...] + hb_ref[...]) * dgel

    half = pl.BlockSpec((tm, RET_W), lambda i: (i, 0))
    big = pl.BlockSpec((tm, D_MODEL), lambda i: (i, 0))
    return _pc(body, name="mix_bwd", grid=(t // tm,),
               in_specs=[half, half, pl.BlockSpec((tm, RET_W), lambda i: (i, G_BLOCK)),
                         pl.BlockSpec((tm, LRU_W), lambda i: (i, GATE_BLOCK)), half, half,
                         _full((D_MODEL, D_MODEL)), big, big, _full((1, D_MODEL))],
               out_specs=[half, half, half, half, big, _full((1, D_MODEL))],
               out_shape=[_sds((t, RET_W))] * 4 + [_sds((t, D_MODEL), BF16), _sds((1, D_MODEL))],
               compiler_params=_params("arbitrary"))(o_f, o_b, proj, proj, hf, hb, w_out, cat, dx1, g1)


def _mlp(x1, n2g, sh2, sc2, g2, fg, w1, w2, tgt):
    t = x1.shape[0]
    tm = _tile(t)
    hb_ = MLP_H // N_CHIP

    def body(x1_ref, n2g_ref, sh2_ref, sc2_ref, g2_ref, fg_ref, w1_hbm, w2_hbm, tgt_ref,
             dx1_ref, h2b_ref, ab_ref, dub_ref, dmb_ref, dsc_ref, dsh_ref, dg2_ref, dn2_ref, dfg_ref, loss_ref,
             w1_s, w2_s, r_s, sems):
        @pl.when(pl.program_id(0) == 0)
        def _():
            c1 = pltpu.make_async_copy(w1_hbm, w1_s, sems.at[0])
            c2 = pltpu.make_async_copy(w2_hbm, w2_s, sems.at[1])
            c1.start()
            c2.start()
            for r in (dsc_ref, dsh_ref, dg2_ref, dn2_ref, dfg_ref, loss_ref):
                r[...] = jnp.zeros_like(r)
            c1.wait()
            c2.wait()
        x1v = x1_ref[...]
        n2g, sc2, g2, fg = n2g_ref[...], sc2_ref[...], g2_ref[...], fg_ref[...]
        xh, _ = _rms(x1v)
        h2b = (xh * n2g * (1.0 + sc2) + sh2_ref[...]).astype(BF16)
        h2b_ref[...] = h2b
        m = jnp.zeros((tm, D_MODEL), F32)
        for j in range(N_CHIP):
            sl = slice(hb_ * j, hb_ * (j + 1))
            r = jnp.maximum(_dot(h2b, w1_s[j]), 0.0)
            r_s[:, sl] = r
            ab = (r * r).astype(BF16)
            ab_ref[:, sl] = ab
            m = m + _dot(ab, w2_s[j])
        x2 = x1v + g2 * m
        x2h, r2 = _rms(x2)
        err = x2h * fg - tgt_ref[...]
        loss_ref[...] += _sum0(err * err)
        dout = err * (1.0 / D_MODEL)
        dfg_ref[...] += _sum0(dout * x2h)
        dxh = dout * fg
        dx2 = r2 * (dxh - x2h * jnp.mean(dxh * x2h, axis=-1, keepdims=True))
        dg2_ref[...] += _sum0(dx2 * m)
        dmb = (g2 * dx2).astype(BF16)
        dmb_ref[...] = dmb
        dh2 = jnp.zeros((tm, D_MODEL), F32)
        for j in range(N_CHIP):
            sl = slice(hb_ * j, hb_ * (j + 1))
            dub = (_dot_nt(dmb, w2_s[j]) * (2.0 * r_s[:, sl])).astype(BF16)
            dub_ref[:, sl] = dub
            dh2 = dh2 + _dot_nt(dub, w1_s[j])
        dx, dn2_t, dsh_t, dsc_t = _norm_mod_bwd(x1v, n2g, sc2, dh2)
        dx1_ref[...] = dx2 + dx
        dn2_ref[...] += dn2_t
        dsh_ref[...] += dsh_t
        dsc_ref[...] += dsc_t

        @pl.when(pl.program_id(0) == t // tm - 1)
        def _():
            tot = jnp.sum(loss_ref[...], axis=1, keepdims=True) * (0.5 / D_MODEL)
            loss_ref[...] = jnp.broadcast_to(tot, loss_ref.shape)

    row = _full((1, D_MODEL))
    big = pl.BlockSpec((tm, D_MODEL), lambda i: (i, 0))
    wide = pl.BlockSpec((tm, MLP_H), lambda i: (i, 0))
    return _pc(body, name="mlp", grid=(t // tm,),
               in_specs=[big, row, row, row, row, row, ANY, ANY, big],
               out_specs=[big, big, wide, wide, big, row, row, row, row, row, row],
               out_shape=[_sds((t, D_MODEL)), _sds((t, D_MODEL), BF16), _sds((t, MLP_H), BF16), _sds((t, MLP_H), BF16),
                          _sds((t, D_MODEL), BF16)] + [_sds((1, D_MODEL))] * 6,
               scratch_shapes=[pltpu.VMEM(w1.shape, BF16), pltpu.VMEM(w2.shape, BF16), pltpu.VMEM((tm, MLP_H), F32),
                               pltpu.SemaphoreType.DMA((2,))],
               compiler_params=_params("arbitrary"))(x1, n2g, sh2, sc2, g2, fg, w1, w2, tgt)


def _tn(a, b, nj, a_blocked, b_blocked, name, extra=None):
    t = a.shape[0]
    m = a.shape[1] // (nj if a_blocked else 1)
    n = b.shape[1] // (nj if b_blocked else 1)
    bk = 512 if t % 512 == 0 else t
    nk = t // bk
    a_map = (lambda j, k: (k, j)) if a_blocked else (lambda j, k: (k, 0))
    b_map = (lambda j, k: (k, j)) if b_blocked else (lambda j, k: (k, 0))
    in_specs = [pl.BlockSpec((bk, m), a_map), pl.BlockSpec((bk, n), b_map)]
    args = [a, b]
    if extra is not None:
        a2, b2 = extra
        t2 = a2.shape[0]
        in_specs += [pl.BlockSpec((t2, m), (lambda j, k: (0, j)) if a_blocked else (lambda j, k: (0, 0))),
                     pl.BlockSpec((t2, n), (lambda j, k: (0, j)) if b_blocked else (lambda j, k: (0, 0)))]
        args += [a2, b2]

    def body(*refs):
        a_ref, b_ref = refs[0], refs[1]
        o_ref, acc = refs[-2], refs[-1]
        k = pl.program_id(1)

        @pl.when(k == 0)
        def _():
            acc[...] = jnp.zeros_like(acc)
        acc[...] += _dot_tn(a_ref[...].astype(BF16), b_ref[...].astype(BF16))

        @pl.when(k == nk - 1)
        def _():
            if extra is not None:
                acc[...] += _dot_tn(refs[2][...].astype(BF16), refs[3][...].astype(BF16))
            o_ref[0] = acc[...]

    return _pc(body, name=name, grid=(nj, nk), in_specs=in_specs,
               out_specs=pl.BlockSpec((1, m, n), lambda j, k: (j, 0, 0)), out_shape=_sds((nj, m, n)),
               scratch_shapes=[pltpu.VMEM((m, n), F32)],
               compiler_params=_params("arbitrary", "arbitrary"))(*args)


def _block_diag(w):
    eye = jnp.eye(LRU_BLOCKS, dtype=F32)
    return (w[:, :, None, :] * eye[:, None, :, None]).reshape(LRU_W, LRU_W).astype(BF16)


def _diag_blocks(mat):
    eye = jnp.eye(LRU_BLOCKS, dtype=jnp.bool_)
    m4 = mat.reshape(LRU_BLOCKS, LRU_BD, LRU_BLOCKS, LRU_BD)
    return jnp.sum(jnp.where(eye[:, None, :, None], m4, 0.0), axis=2)


def _pad_rows(v, width=LANES):
    flat = v.reshape(-1).astype(F32)
    tile = SUBLANES * width
    n = -(-flat.shape[0] // tile) * tile
    return jnp.pad(flat, (0, n - flat.shape[0])).reshape(n // width, width)


class _Slab:
    def __init__(self):
        self.parts, self.meta, self.rows = [], {}, 0

    def add(self, name, v):
        p = _pad_rows(v)
        self.meta[name] = (self.rows, p.shape[0], v.shape)
        self.parts.append(p)
        self.rows += p.shape[0]

    def build(self):
        return jnp.concatenate(self.parts, axis=0)

    def take(self, slab, name, shape=None):
        r0, nr, shp = self.meta[name]
        shp = shp if shape is None else shape
        return slab[r0:r0 + nr].reshape(-1)[:math.prod(shp)].reshape(shp)


def _lane_rep(v8):
    return jnp.broadcast_to(v8.reshape(SUBLANES, 1), (SUBLANES, LANES))


def kernel(x, c, ctx, c_ctx, w_ada, b_ada, norm1_g, norm2_g, w_in, ret_decay, conv_w, conv_b, lru_wa, lru_ba, lru_wx, lru_bx, lru_lambda, w_out, w_mlp1, w_mlp2, final_g, loss_target, m_c_ctx, m_w_ada, m_b_ada, m_norm1_g, m_norm2_g, m_w_in, m_ret_decay, m_conv_w, m_conv_b, m_lru_wa, m_lru_ba, m_lru_wx, m_lru_bx, m_lru_lambda, m_w_out, m_w_mlp1, m_w_mlp2, m_final_g, v_c_ctx, v_w_ada, v_b_ada, v_norm1_g, v_norm2_g, v_w_in, v_ret_decay, v_conv_w, v_conv_b, v_lru_wa, v_lru_ba, v_lru_wx, v_lru_bx, v_lru_lambda, v_w_out, v_w_mlp1, v_w_mlp2, v_final_g):
    ax, ay, ac = lax.axis_index("x"), lax.axis_index("y"), lax.axis_index("c")
    chip = 2 * ax + ay
    dev = 4 * ax + 2 * ay + ac
    c_idx = ac.reshape(1).astype(jnp.int32)
    j_idx = chip.reshape(1).astype(jnp.int32)

    xt = x[0]
    t_len = xt.shape[0]
    ctxt = ctx[0]
    l_len = ctxt.shape[0]
    tgt = loss_target[0]
    ada_n = w_ada.shape[2]

    def my_half(w2d):
        r = w2d.shape[0] // 2
        return lax.dynamic_slice_in_dim(w2d, ac * r, r, axis=0).astype(BF16)

    gw_in, gw_out, gw_1, gw_2 = _all_gather(
        [my_half(w_in[0]), my_half(w_out[0]), my_half(w_mlp1[0]), my_half(w_mlp2[0])], "gather_weights")
    w4 = gw_in.reshape(N_CHIP, D_MODEL, IN_COLS // N_CHIP)
    wo = gw_out.reshape(D_MODEL, D_MODEL)
    w1 = gw_1.reshape(N_CHIP, D_MODEL, MLP_H // N_CHIP)
    w2 = gw_2.reshape(N_CHIP, MLP_H // N_CHIP, D_MODEL)

    (c_all,) = _all_gather([jnp.pad(c, ((0, SUBLANES - 1), (0, 0)))], "gather_c")
    a16, lgv, sgv = _prep(c_all[:, 0, :], c_ctx, ret_decay[0])
    b_shard = lax.dynamic_slice_in_dim(b_ada, chip * ada_n, ada_n, axis=1)
    (mod_parts,) = _all_gather([_mod_fwd(a16, w_ada[0], b_shard)], "gather_mod")
    mod_all = mod_parts[0::2].transpose(1, 0, 2).reshape(16, N_CHIP * ada_n)
    mod_me = lax.dynamic_slice_in_dim(mod_all, dev, 1, axis=0)
    sh1, sc1, g1, sh2, sc2, g2 = [mod_me[:, D_MODEL * k:D_MODEL * (k + 1)] for k in range(N_MOD)]
    csh1, csc1 = mod_all[8:9, 0:D_MODEL], mod_all[8:9, D_MODEL:2 * D_MODEL]

    cos2, sin2 = _rotary_tables(t_len)
    cos_c, sin_c = jnp.ones((l_len, DH), F32), jnp.zeros((l_len, DH), F32)
    n1g, n2g = norm1_g, norm2_g
    fg = final_g.reshape(1, D_MODEL)

    pad8 = lambda a: jnp.pad(a, ((0, SUBLANES - a.shape[0]), (0, 0)))
    small = jnp.concatenate([pad8(conv_w[0]), pad8(lru_ba[0]), pad8(lru_bx[0]), pad8(lru_lambda[0])], axis=0)
    (small_all,) = _all_gather([small], "gather_small_params")
    small_full = small_all[0::2].transpose(1, 0, 2).reshape(4 * SUBLANES, LRU_W)
    cw = small_full[0:4]
    cb = conv_b
    ba_f, ba_b = small_full[8:9], small_full[9:10]
    bx_f, bx_b = small_full[16:17], small_full[17:18]
    lam_f, lam_b = small_full[24:25], small_full[25:26]
    wa_f, wa_b = _block_diag(lru_wa[0, 0]), _block_diag(lru_wa[0, 1])
    wx_f, wx_b = _block_diag(lru_wx[0, 0]), _block_diag(lru_wx[0, 1])
    zero_h = jnp.zeros((1, LRU_W), F32)

    projc, hcb16 = _inproj_fwd(ctxt, n1g, csh1, csc1, w4, cos_c, sin_c, "inproj_fwd_ctx")
    s_f, s_b = _ctx_state_fwd(projc, lgv)
    xcc = _conv_fwd(projc, cw, cb, "conv_fwd_ctx")
    hcf = _lru_fwd(xcc, wa_f, wx_f, ba_f, bx_f, lam_f, zero_h, False, "lru_fwd_ctx_f")
    hcbk = _lru_fwd(xcc, wa_b, wx_b, ba_b, bx_b, lam_b, zero_h, True, "lru_fwd_ctx_b")
    lru_sf, lru_sb = hcf[l_len - 1:l_len], hcbk[0:1]

    proj, hb16 = _inproj_fwd(xt, n1g, sh1, sc1, w4, cos2, sin2, "inproj_fwd")
    o_f, o_b, spf, spb = _ret_fwd(proj, lgv, s_f, s_b)
    xcl = _conv_fwd(proj, cw, cb, "conv_fwd")
    hf = _lru_fwd(xcl, wa_f, wx_f, ba_f, bx_f, lam_f, lru_sf, False, "lru_fwd_f")
    hbk = _lru_fwd(xcl, wa_b, wx_b, ba_b, bx_b, lam_b, lru_sb, True, "lru_fwd_b")
    x1, cat = _mix_fwd(o_f, o_b, proj, hf, hbk, wo, xt, g1)

    (dx1, h2b, ab, dub, dmb, dsc2, dsh2, dg2, dn2g, dfg, lossv) = _mlp(x1, n2g, sh2, sc2, g2, fg, w1, w2, tgt)
    gw_mlp1 = _tn(h2b, dub, N_CHIP, False, True, "grad_w_mlp1")
    gw_mlp2 = _tn(ab, dmb, N_CHIP, True, False, "grad_w_mlp2")

    do, dhs, dg, dgate, dyb, dg1 = _mix_bwd(o_f, o_b, proj, hf, hbk, wo, cat, dx1, g1)
    gw_o = _tn(cat, dyb, 1, False, False, "grad_w_out")

    dq_f, dk_f, dv_f, ds_f, drd_f = _ret_bwd(proj, lgv, sgv, spf, do, False, "ret_bwd_f")
    dq_b, dk_b, dv_b, ds_b, drd_b = _ret_bwd(proj, lgv, sgv, spb, do, True, "ret_bwd_b")
    dxc_f, dpre_f, dba_f, dbx_f, dlam_f, dh0_f = _lru_bwd(xcl, wa_f, wx_f, ba_f, bx_f, lam_f, hf, lru_sf, dhs, False,
                                                         "lru_bwd_f")
    dxc_b, dpre_b, dba_b, dbx_b, dlam_b, dh0_b = _lru_bwd(xcl, wa_b, wx_b, ba_b, bx_b, lam_b, hbk, lru_sb, dhs, True,
                                                         "lru_bwd_b")
    dxr, dcw, dcb = _conv_bwd(dxc_f + dxc_b, proj, cw, "conv_bwd")
    grad_x, dpb, dn1g, dsh1, dsc1 = _inproj_bwd(
        xt, n1g, sh1, sc1, w4, cos2, sin2, [dq_f, dq_b, dk_f, dk_b, dv_f, dv_b, dg, dxr, dgate], dx1, "inproj_bwd")

    dkc, dvc, drd_c = _ctx_state_bwd(projc, lgv, sgv, ds_f, ds_b)
    zc = jnp.zeros((l_len, LRU_W), F32)
    dhc_f = lax.dynamic_update_slice(zc, dh0_f, (l_len - 1, 0))
    dhc_b = lax.dynamic_update_slice(zc, dh0_b, (0, 0))
    dxcc_f, dprec_f, dbac_f, dbxc_f, dlamc_f, _ = _lru_bwd(xcc, wa_f, wx_f, ba_f, bx_f, lam_f, hcf, zero_h, dhc_f, False,
                                                          "lru_bwd_ctx_f")
    dxcc_b, dprec_b, dbac_b, dbxc_b, dlamc_b, _ = _lru_bwd(xcc, wa_b, wx_b, ba_b, bx_b, lam_b, hcbk, zero_h, dhc_b, True,
                                                          "lru_bwd_ctx_b")
    dxrc, dcw_c, dcb_c = _conv_bwd(dxcc_f + dxcc_b, projc, cw, "conv_bwd_ctx")
    zr = jnp.zeros((l_len, RET_W), F32)
    _, dpbc, dn1g_c, dcsh1, dcsc1 = _inproj_bwd(
        ctxt, n1g, csh1, csc1, w4, cos_c, sin_c, [zr, zr, dkc, zr, dvc, zr, zr, dxrc, zr],
        jnp.zeros((l_len, D_MODEL), F32), "inproj_bwd_ctx")

    gw_i = _tn(hb16, dpb, N_CHIP, False, True, "grad_w_in", extra=(hcb16, dpbc))
    gwa_f = _tn(xcl, dpre_f, 2, False, True, "grad_lru_gates_f", extra=(xcc, dprec_f))
    gwa_b = _tn(xcl, dpre_b, 2, False, True, "grad_lru_gates_b", extra=(xcc, dprec_b))

    blocked = [gw_i.reshape(N_DEV, D_MODEL // 2, IN_COLS // N_CHIP), gw_o.reshape(N_DEV, D_MODEL // N_DEV, D_MODEL),
               gw_mlp1.reshape(N_DEV, D_MODEL // 2, MLP_H // N_CHIP), gw_mlp2.reshape(N_DEV, MLP_H // N_DEV, D_MODEL)]
    g_in, g_out, g_1, g_2 = _reduce_scatter(blocked, c_idx, j_idx)
    big = {}
    for nm, w, g, m, v in (("w_in", w_in, g_in, m_w_in, v_w_in), ("w_out", w_out, g_out, m_w_out, v_w_out),
                           ("w_mlp1", w_mlp1, g_1, m_w_mlp1, v_w_mlp1), ("w_mlp2", w_mlp2, g_2, m_w_mlp2, v_w_mlp2)):
        d_, mn, vn = _adamw(w[0], g, m[0], v[0], "adamw_" + nm)
        big[nm] = (g[None], d_[None], mn[None], vn[None])

    part = _Slab()
    part.add("loss", lossv)
    part.add("dmod", jnp.concatenate([dsh1, dsc1, dg1, dsh2, dsc2, dg2], axis=1))
    part.add("dmodc", jnp.concatenate([dcsh1, dcsc1], axis=1))
    part.add("norm1_g", dn1g + dn1g_c)
    part.add("norm2_g", dn2g)
    part.add("final_g", dfg)
    part.add("ret_decay", jnp.concatenate([drd_f[:, 0, :] + drd_c[:, 0, :], drd_b[:, 0, :] + drd_c[:, 1, :]], axis=0))
    part.add("conv_w", dcw + dcw_c)
    part.add("conv_b", dcb + dcb_c)
    part.add("lru_wa", jnp.stack([_diag_blocks(gwa_f[0]), _diag_blocks(gwa_b[0])]))
    part.add("lru_wx", jnp.stack([_diag_blocks(gwa_f[1]), _diag_blocks(gwa_b[1])]))
    part.add("lru_ba", jnp.concatenate([dba_f + dbac_f, dba_b + dbac_b], axis=0))
    part.add("lru_bx", jnp.concatenate([dbx_f + dbxc_f, dbx_b + dbxc_b], axis=0))
    part.add("lru_lambda", jnp.concatenate([dlam_f + dlamc_f, dlam_b + dlamc_b], axis=0))
    (parts_all,) = _all_gather([part.build()], "gather_small_grads")
    tot = _sum_devices(parts_all, "sum_small_grads")

    loss = part.take(tot, "loss")[0, 0]
    dmod_all = parts_all[:, part.meta["dmod"][0]:part.meta["dmod"][0] + part.meta["dmod"][1]].reshape(N_DEV, -1)
    dmodc_tot = jnp.pad(part.take(tot, "dmodc").reshape(1, -1), ((0, 0), (0, (N_MOD - 2) * D_MODEL)))
    dmod_tot = part.take(tot, "dmod").reshape(1, -1)
    grad_b_ada = dmod_tot + dmodc_tot

    cols = lambda a: lax.dynamic_slice_in_dim(a, chip * ada_n, ada_n, axis=1)
    b128 = jnp.pad(jnp.concatenate([cols(dmod_all), jnp.pad(cols(dmodc_tot), ((0, SUBLANES - 1), (0, 0)))], axis=0),
                   ((0, LANES - 2 * SUBLANES), (0, 0)))
    g_ada = _ada_grad(jnp.pad(a16.T, ((0, 0), (0, LANES - 16))), b128)
    d_ada, m_ada, v_ada = _adamw(w_ada[0], g_ada, m_w_ada[0], v_w_ada[0], "adamw_w_ada")

    dmc8 = jnp.pad(cols(dmodc_tot), ((0, SUBLANES - 1), (0, 0)))
    (cparts,) = _all_gather([_cctx_partial(dmc8, w_ada[0])], "gather_cctx")
    g_cc, d_cc, m_cc, v_cc = _cctx_final(cparts, c_ctx, m_c_ctx, v_c_ctx)

    def shard_cols(a, width=LANES):
        return lax.dynamic_slice_in_dim(a, chip * width, width, axis=a.ndim - 1)

    grads = {
        "b_ada": grad_b_ada,
        "norm1_g": part.take(tot, "norm1_g"),
        "norm2_g": part.take(tot, "norm2_g"),
        "ret_decay": part.take(tot, "ret_decay", (SUBLANES, LANES)),
        "conv_w": shard_cols(part.take(tot, "conv_w")),
        "conv_b": part.take(tot, "conv_b"),
        "lru_wa": part.take(tot, "lru_wa"),
        "lru_ba": shard_cols(part.take(tot, "lru_ba")),
        "lru_wx": part.take(tot, "lru_wx"),
        "lru_bx": shard_cols(part.take(tot, "lru_bx")),
        "lru_lambda": shard_cols(part.take(tot, "lru_lambda")),
        "final_g": part.take(tot, "final_g"),
    }
    params = {
        "b_ada": (b_ada, m_b_ada, v_b_ada), "norm1_g": (norm1_g, m_norm1_g, v_norm1_g),
        "norm2_g": (norm2_g, m_norm2_g, v_norm2_g), "ret_decay": (ret_decay, m_ret_decay, v_ret_decay),
        "conv_w": (conv_w, m_conv_w, v_conv_w), "conv_b": (conv_b, m_conv_b, v_conv_b),
        "lru_wa": (lru_wa, m_lru_wa, v_lru_wa), "lru_ba": (lru_ba, m_lru_ba, v_lru_ba),
        "lru_wx": (lru_wx, m_lru_wx, v_lru_wx), "lru_bx": (lru_bx, m_lru_bx, v_lru_bx),
        "lru_lambda": (lru_lambda, m_lru_lambda, v_lru_lambda), "final_g": (final_g, m_final_g, v_final_g),
    }
    slabs = [_Slab() for _ in range(4)]
    for nm, gval in grads.items():
        vals = (gval,) + params[nm]
        for s, val in zip(slabs, vals):
            if nm == "ret_decay" and val.shape != (SUBLANES, LANES):
                val = _lane_rep(val.reshape(-1))
            s.add(nm, val)
    gs, ws, ms, vs = [s.build() for s in slabs]
    ds_, mns, vns = _adamw(ws, gs, ms, vs, "adamw_small")
    small_out = {}
    for nm in grads:
        shp = params[nm][0].shape
        if nm == "ret_decay":
            unp = lambda sl: slabs[0].take(sl, nm, (SUBLANES, LANES))[:, 0].reshape(shp)
        else:
            unp = lambda sl: slabs[0].take(sl, nm, shp)
        small_out[nm] = (unp(gs), unp(ds_), unp(mns), unp(vns))
    small_out["c_ctx"] = tuple(a.reshape(D_MODEL) for a in (g_cc, d_cc, m_cc, v_cc))
    small_out["w_ada"] = (g_ada[None], d_ada[None], m_ada[None], v_ada[None])
    small_out.update(big)

    order = ["c_ctx", "w_ada", "b_ada", "norm1_g", "norm2_g", "w_in", "ret_decay", "conv_w", "conv_b", "lru_wa", "lru_ba",
             "lru_wx", "lru_bx", "lru_lambda", "w_out", "w_mlp1", "w_mlp2", "final_g"]
    outs = [loss, grad_x[None]]
    for k in range(4):
        outs += [small_out[nm][k] for nm in order]
    return tuple(outs)
```

```python
import math

import jax
import jax.numpy as jnp
from jax import lax
from jax.experimental import pallas as pl
from jax.experimental.pallas import tpu as pltpu

F32 = jnp.float32
BF16 = jnp.bfloat16

D_MODEL = 1024
HEADS = 4
DH = 128
CHUNK = 128
RET_W = HEADS * DH
LRU_W = 512
LRU_BLOCKS = 8
LRU_BD = LRU_W // LRU_BLOCKS
LRU_C = 8.0
IN_COLS = 4 * RET_W + 2 * LRU_W
MLP_H = 4 * D_MODEL
N_MOD = 6
GRID_W = 64
ROPE_BASE = 10000.0
K_SCALE = DH ** -0.5
EPS = 1e-6
GELU_K = math.sqrt(2.0 / math.pi)
GELU_C = 0.044715

ADAM_LR = 0.001
ADAM_B1 = 0.9
ADAM_B2 = 0.999
ADAM_EPS = 1e-08
ADAM_WD = 0.01
ADAM_STEP = 10

N_DEV = 8
N_CHIP = 4
SUBLANES = 8
LANES = 128
VMEM_LIMIT_V7X = 56 * 1024 * 1024
MESH = pl.DeviceIdType.MESH
ANY = pl.BlockSpec(memory_space=pl.ANY)


def _pc(body, **kw):
    return pl.pallas_call(body, **kw)


def _params(*sem):
    return pltpu.CompilerParams(dimension_semantics=sem if sem else None, vmem_limit_bytes=VMEM_LIMIT_V7X)


def _tile(t):
    return 256 if t >= 256 else t


def _sds(shape, dtype=F32):
    return jax.ShapeDtypeStruct(tuple(shape), dtype)


def _full(shape):
    nd = len(shape)
    return pl.BlockSpec(tuple(shape), lambda *_: (0,) * nd)


def _sigmoid(x):
    return 1.0 / (1.0 + jnp.exp(-x))


def _log1p_pos(y):
    s = y * (1.0 - y * (0.5 - y * (1.0 / 3.0 - y * (0.25 - y * (0.2 - y / 6.0)))))
    return jnp.where(y < 0.03, s, jnp.log(1.0 + y))


def _softplus(z):
    return jnp.maximum(z, 0.0) + _log1p_pos(jnp.exp(-jnp.abs(z)))


def _neg_expm1(x):
    t = x * (1.0 + x * (0.5 + x * (1.0 / 6.0 + x * (1.0 / 24.0 + x * (1.0 / 120.0 + x * (1.0 / 720.0 + x / 5040.0))))))
    return -jnp.where(x > -0.25, t, jnp.exp(x) - 1.0)


def _rms(x):
    r = lax.rsqrt(jnp.mean(x * x, axis=-1, keepdims=True) + EPS)
    return x * r, r


def _dot(a, b):
    return jnp.dot(a, b, preferred_element_type=F32)


def _dot_nt(a, b):
    return lax.dot_general(a, b, (((1,), (1,)), ((), ())), preferred_element_type=F32)


def _dot_tn(a, b):
    return lax.dot_general(a, b, (((0,), (0,)), ((), ())), preferred_element_type=F32)


def _sum0(x):
    return jnp.sum(x, axis=0, keepdims=True)


def _norm_mod_bwd(x, g, sc, dh):
    xh, r = _rms(x)
    hn = xh * g
    dhn = dh * (1.0 + sc)
    dxh = dhn * g
    dx = r * (dxh - xh * jnp.mean(dxh * xh, axis=-1, keepdims=True))
    return dx, _sum0(dhn * xh), _sum0(dh), _sum0(dh * hn)


def _dev_index(p):
    return 4 * p[0] + 2 * p[1] + p[2]


def _mesh_pos():
    return lax.axis_index("x"), lax.axis_index("y"), lax.axis_index("c")


class _AllGather:
    def __init__(self, arrs):
        n = len(arrs)
        self.arrays = list(arrs)
        self.out_shapes = [_sds((N_DEV,) + a.shape, a.dtype) for a in arrs]
        self.scratch = ([pltpu.VMEM(a.shape, a.dtype) for a in arrs]
                        + [pltpu.SemaphoreType.DMA((7 * n,)), pltpu.SemaphoreType.DMA((7 * n,)),
                           pltpu.SemaphoreType.DMA((n,))])
        self.aliases = {}

    def _parts(self, ins, outs, scr):
        n = len(self.arrays)
        stage = scr[:n]
        send_sems, recv_sems, local_sems = scr[n:]
        x, y, c = _mesh_pos()
        me, sib = (x, y, c), (x, y, 1 - c)
        chips = [(1 - x, y), (x, 1 - y), (1 - x, 1 - y)]

        def copy(t, k, block, to, own=False):
            dst = outs[t].at[_dev_index(block)]
            return pltpu.make_async_remote_copy(
                src_ref=ins[t] if own else dst, dst_ref=dst,
                send_sem=send_sems.at[7 * t + k], recv_sem=recv_sems.at[7 * t + k],
                device_id=to, device_id_type=MESH)

        first = []
        for t in range(n):
            first.append(copy(t, 0, me, sib, own=True))
            for j, ch in enumerate(chips):
                first.append(copy(t, 1 + j, me, (*ch, c), own=True))
        stage_in = [pltpu.make_async_copy(ins[t], stage[t], local_sems.at[t]) for t in range(n)]
        mine = [pltpu.make_async_copy(stage[t], outs[t].at[_dev_index(me)], local_sems.at[t]) for t in range(n)]
        return n, c, me, sib, chips, copy, first, stage_in, mine

    def start(self, ins, outs, scr):
        n, _, _, _, _, _, first, stage_in, mine = self._parts(ins, outs, scr)
        for cp in stage_in:
            cp.start()
        for cp in first:
            cp.start()
        for t in range(n):
            stage_in[t].wait()
            mine[t].start()

    def finish(self, ins, outs, scr):
        n, c, me, sib, chips, copy, first, _, mine = self._parts(ins, outs, scr)
        passed = []
        for j, ch in enumerate(chips):
            for t in range(n):
                copy(t, 1 + j, (*ch, c), me).wait_recv()
                p = copy(t, 4 + j, (*ch, c), sib)
                p.start()
                passed.append(p)
        for t in range(n):
            copy(t, 0, sib, me).wait_recv()
            for j, ch in enumerate(chips):
                copy(t, 4 + j, (*ch, 1 - c), me).wait_recv()
        for cp in first + passed:
            cp.wait_send()
        for cp in mine:
            cp.wait()


class _Exchange:
    def __init__(self, arrays, out_shapes, plan, n_copies, aliases=None):
        self.arrays = list(arrays)
        self.out_shapes = list(out_shapes)
        self.plan = plan
        self.scratch = [pltpu.SemaphoreType.DMA((n_copies,)), pltpu.SemaphoreType.DMA((n_copies,))]
        self.aliases = aliases or {}

    def _copies(self, ins, outs, scr):
        send_sems, recv_sems = scr
        snd, rcv = [], []
        for i, (src, dst, peer, lands) in enumerate(self.plan(ins, outs, _mesh_pos())):
            kw = dict(send_sem=send_sems.at[i], recv_sem=recv_sems.at[i], device_id=peer, device_id_type=MESH)
            snd.append(pltpu.make_async_remote_copy(src_ref=src, dst_ref=dst, **kw))
            rcv.append(pltpu.make_async_remote_copy(src_ref=src, dst_ref=lands, **kw))
        return snd, rcv

    def start(self, ins, outs, scr):
        for cp in self._copies(ins, outs, scr)[0]:
            cp.start()

    def finish(self, ins, outs, scr):
        snd, rcv = self._copies(ins, outs, scr)
        for cp in rcv:
            cp.wait_recv()
        for cp in snd:
            cp.wait_send()


def _pair_exchange(grads):
    n = len(grads)

    def plan(ins, outs, pos):
        x, y, c = pos
        return [(ins[t].at[2 * j + (1 - c)], outs[t].at[j], (x, y, 1 - c), outs[t].at[j])
                for t in range(n) for j in range(N_CHIP)]

    return _Exchange(grads, [_sds((N_CHIP,) + g.shape[1:], g.dtype) for g in grads], plan, N_CHIP * n)


def _chip_exchange(parts):
    n = len(parts)

    def plan(ins, outs, pos):
        x, y, c = pos
        chips = [(1 - x, y), (x, 1 - y), (1 - x, 1 - y)]
        return [(ins[t].at[2 * ch[0] + ch[1]], outs[t].at[k], (*ch, c), outs[t].at[k])
                for t in range(n) for k, ch in enumerate(chips)]

    return _Exchange(parts, [_sds((3,) + p.shape[1:], p.dtype) for p in parts], plan, 3 * n)


def _pair_gather(bufs):
    n = len(bufs)

    def plan(ins, outs, pos):
        x, y, c = pos
        return [(ins[t].at[c], outs[t].at[c], (x, y, 1 - c), outs[t].at[1 - c]) for t in range(n)]

    return _Exchange(bufs, [_sds(b.shape, b.dtype) for b in bufs], plan, n, aliases={t: t for t in range(n)})


def _run_comm(comm, name):
    n_in, n_out = len(comm.arrays), len(comm.out_shapes)

    def body(*refs):
        ins, outs, scr = refs[:n_in], refs[n_in:n_in + n_out], refs[n_in + n_out:]
        comm.start(ins, outs, scr)
        comm.finish(ins, outs, scr)

    outs = _pc(body, name=name, out_shape=comm.out_shapes, in_specs=[ANY] * n_in, out_specs=[ANY] * n_out,
               input_output_aliases=dict(comm.aliases), scratch_shapes=comm.scratch,
               compiler_params=_params())(*comm.arrays)
    return list(outs)


def _all_gather(arrs, name):
    return _run_comm(_AllGather(arrs), name)


def _call(body, *, name, grid, in_specs, out_specs, out_shape, scratch_shapes, sem, args, comms=()):
    n_in, n_out, n_scr = len(in_specs), len(out_specs), len(scratch_shapes)
    c_in = [len(cm.arrays) for cm in comms]
    c_out = [len(cm.out_shapes) for cm in comms]
    c_scr = [len(cm.scratch) for cm in comms]
    aliases = {}
    for k, cm in enumerate(comms):
        for a, b in cm.aliases.items():
            aliases[n_in + sum(c_in[:k]) + a] = n_out + sum(c_out[:k]) + b

    def split(refs, counts):
        out, pos = [], 0
        for cnt in counts:
            out.append(refs[pos:pos + cnt])
            pos += cnt
        return out

    def wrapped(*refs):
        ins = refs[:n_in + sum(c_in)]
        outs = refs[len(ins):len(ins) + n_out + sum(c_out)]
        scr = refs[len(ins) + len(outs):]
        cins, couts, cscr = split(ins[n_in:], c_in), split(outs[n_out:], c_out), split(scr[n_scr:], c_scr)
        if comms:
            first = pl.program_id(0) == 0
            last = pl.program_id(0) == grid[0] - 1
            for k in range(1, len(grid)):
                first = jnp.logical_and(first, pl.program_id(k) == 0)
                last = jnp.logical_and(last, pl.program_id(k) == grid[k] - 1)

            @pl.when(first)
            def _():
                for k, cm in enumerate(comms):
                    cm.start(cins[k], couts[k], cscr[k])
        body(*ins[:n_in], *outs[:n_out], *scr[:n_scr])
        if comms:
            @pl.when(last)
            def _():
                for k, cm in enumerate(comms):
                    cm.finish(cins[k], couts[k], cscr[k])

    outs = _pc(wrapped, name=name, grid=grid,
               in_specs=list(in_specs) + [ANY] * sum(c_in), out_specs=list(out_specs) + [ANY] * sum(c_out),
               out_shape=list(out_shape) + [s for cm in comms for s in cm.out_shapes],
               scratch_shapes=list(scratch_shapes) + [s for cm in comms for s in cm.scratch],
               input_output_aliases=aliases, compiler_params=_params(*sem),
               )(*args, *[a for cm in comms for a in cm.arrays])
    outs = list(outs)
    return outs[:n_out], split(outs[n_out:], c_out)


def _row_block(r):
    for b in (512, 256, 128, 64, 32, 16, 8):
        if r % b == 0:
            return b
    return r


def _pair_add(g, recv, c_idx, name):
    _, r, cc = g.shape
    br = _row_block(r)

    def body(c_ref, g_ref, r_ref, p_ref, pb_ref):
        s = g_ref[...] + r_ref[...]
        p_ref[...] = s
        pb_ref[...] = s.astype(BF16)

    grid_spec = pltpu.PrefetchScalarGridSpec(
        num_scalar_prefetch=1, grid=(N_CHIP, r // br),
        in_specs=[pl.BlockSpec((1, br, cc), lambda j, i, c_ref: (2 * j + c_ref[0], i, 0)),
                  pl.BlockSpec((1, br, cc), lambda j, i, c_ref: (j, i, 0))],
        out_specs=[pl.BlockSpec((1, br, cc), lambda j, i, c_ref: (j, i, 0)),
                   pl.BlockSpec((1, br, cc), lambda j, i, c_ref: (j, i, 0))])
    return _pc(body, name=name, grid_spec=grid_spec,
               out_shape=[_sds((N_CHIP, r, cc)), _sds((N_CHIP, r, cc), BF16)],
               compiler_params=_params("arbitrary", "arbitrary"))(c_idx, g, recv)


def _chip_add(p, q, jc_idx, name):
    _, r, cc = p.shape
    br = _row_block(r)

    def body(jc_ref, p_ref, q_ref, o_ref):
        o_ref[0] = ((p_ref[0] + q_ref[0].astype(F32)) + q_ref[1].astype(F32)) + q_ref[2].astype(F32)

    grid_spec = pltpu.PrefetchScalarGridSpec(
        num_scalar_prefetch=1, grid=(r // br,),
        in_specs=[pl.BlockSpec((1, br, cc), lambda i, jc_ref: (jc_ref[0], i, 0)),
                  pl.BlockSpec((3, br, cc), lambda i, jc_ref: (0, i, 0))],
        out_specs=pl.BlockSpec((1, br, cc), lambda i, jc_ref: (jc_ref[1], i, 0)))
    return _pc(body, name=name, grid_spec=grid_spec, out_shape=_sds((2, r, cc)),
               compiler_params=_params("arbitrary"))(jc_idx, p, q)


def _shard_of(both):
    return both.reshape((2 * both.shape[1],) + both.shape[2:])


def _sum_devices(g, name):
    _, r, cc = g.shape

    def body(g_ref, o_ref):
        acc = g_ref[0]
        for d in range(1, N_DEV):
            acc = acc + g_ref[d]
        o_ref[...] = acc

    return _pc(body, name=name, out_shape=_sds((r, cc)), in_specs=[_full(g.shape)], out_specs=_full((r, cc)),
               compiler_params=_params())(g)


def _adamw(w, g, m, v, name):
    r, cc = w.shape
    br = _row_block(r)
    if r * cc * 4 <= (1 << 20):
        br = r
    elif br * cc * 4 > (1 << 20) and br > 8:
        br = max(8, (1 << 20) // (cc * 4) // 8 * 8)
        while r % br:
            br -= 8
    c1 = 1.0 - ADAM_B1 ** ADAM_STEP
    c2 = 1.0 - ADAM_B2 ** ADAM_STEP

    def body(w_ref, g_ref, m_ref, v_ref, d_ref, mo_ref, vo_ref):
        gg = g_ref[...]
        mn = ADAM_B1 * m_ref[...] + (1.0 - ADAM_B1) * gg
        vn = ADAM_B2 * v_ref[...] + (1.0 - ADAM_B2) * (gg * gg)
        mh = mn / c1
        vh = vn / c2
        d_ref[...] = -ADAM_LR * (mh / (jnp.sqrt(vh) + ADAM_EPS) + ADAM_WD * w_ref[...])
        mo_ref[...] = mn
        vo_ref[...] = vn

    spec = pl.BlockSpec((br, cc), lambda i: (i, 0))
    return _pc(body, name=name, grid=(r // br,), in_specs=[spec] * 4, out_specs=[spec] * 3,
               out_shape=[_sds((r, cc))] * 3, compiler_params=_params("arbitrary"))(w, g, m, v)


def _prep(c_all, c_ctx, ret_decay):
    def body(c_ref, cc_ref, rd_ref, a_ref, lg_ref, sg_ref):
        ca = c_ref[...]
        cc = cc_ref[...]
        a_ref[...] = jnp.zeros_like(a_ref)
        a_ref[0:8, :] = ca * _sigmoid(ca)
        a_ref[8:9, :] = cc * _sigmoid(cc)
        rd = rd_ref[...]
        lg_ref[...] = -_softplus(-rd)
        sg_ref[...] = _sigmoid(-rd)

    rd = jnp.broadcast_to(ret_decay.reshape(2, HEADS).T[:, :, None], (HEADS, 2, LANES))
    return _pc(body, name="prep",
               out_shape=[_sds((16, D_MODEL)), _sds((HEADS, 2, LANES)), _sds((HEADS, 2, LANES))],
               in_specs=[_full((8, D_MODEL)), _full((1, D_MODEL)), _full((HEADS, 2, LANES))],
               out_specs=[_full((16, D_MODEL)), _full((HEADS, 2, LANES)), _full((HEADS, 2, LANES))],
               compiler_params=_params())(c_all, c_ctx.reshape(1, D_MODEL), rd)


def _mod_fwd(a16, w_ada, b_shard):
    n = w_ada.shape[1]
    bn = 512

    def body(a_ref, w_ref, b_ref, o_ref):
        o_ref[...] = jnp.dot(a_ref[...], w_ref[...], preferred_element_type=F32,
                             precision=lax.Precision.HIGHEST) + b_ref[...]

    return _pc(body, name="mod_fwd", grid=(n // bn,),
               in_specs=[_full((16, D_MODEL)), pl.BlockSpec((D_MODEL, bn), lambda i: (0, i)),
                         pl.BlockSpec((1, bn), lambda i: (0, i))],
               out_specs=pl.BlockSpec((16, bn), lambda i: (0, i)), out_shape=_sds((16, n)),
               compiler_params=_params("arbitrary"))(a16, w_ada, b_shard)


def _ada_grad(at, b):
    n = b.shape[1]
    bn = 512

    def body(a_ref, b_ref, o_ref):
        o_ref[...] = jnp.dot(a_ref[...], b_ref[...], preferred_element_type=F32, precision=lax.Precision.HIGHEST)

    return _pc(body, name="ada_grad", grid=(n // bn,),
               in_specs=[_full((D_MODEL, LANES)), pl.BlockSpec((LANES, bn), lambda i: (0, i))],
               out_specs=pl.BlockSpec((D_MODEL, bn), lambda i: (0, i)), out_shape=_sds((D_MODEL, n)),
               compiler_params=_params("arbitrary"))(at, b)


def _cctx_partial(dmc8, w_ada):
    n = w_ada.shape[1]
    bn = 512

    def body(d_ref, w_ref, o_ref):
        @pl.when(pl.program_id(0) == 0)
        def _():
            o_ref[...] = jnp.zeros_like(o_ref)
        o_ref[...] += lax.dot_general(d_ref[...], w_ref[...], (((1,), (1,)), ((), ())),
                                      preferred_element_type=F32, precision=lax.Precision.HIGHEST)

    return _pc(body, name="cctx_partial", grid=(n // bn,),
               in_specs=[pl.BlockSpec((8, bn), lambda i: (0, i)), pl.BlockSpec((D_MODEL, bn), lambda i: (0, i))],
               out_specs=_full((8, D_MODEL)), out_shape=_sds((8, D_MODEL)),
               compiler_params=_params("arbitrary"))(dmc8, w_ada)


def _cctx_final(parts, c_ctx, m, v):
    c1 = 1.0 - ADAM_B1 ** ADAM_STEP
    c2 = 1.0 - ADAM_B2 ** ADAM_STEP

    def body(p_ref, c_ref, m_ref, v_ref, g_ref, d_ref, mo_ref, vo_ref):
        s = ((p_ref[0, 0:1, :] + p_ref[2, 0:1, :]) + p_ref[4, 0:1, :]) + p_ref[6, 0:1, :]
        z = c_ref[...]
        sg = _sigmoid(z)
        gg = s * (sg * (1.0 + z * (1.0 - sg)))
        g_ref[...] = gg
        mn = ADAM_B1 * m_ref[...] + (1.0 - ADAM_B1) * gg
        vn = ADAM_B2 * v_ref[...] + (1.0 - ADAM_B2) * (gg * gg)
        d_ref[...] = -ADAM_LR * ((mn / c1) / (jnp.sqrt(vn / c2) + ADAM_EPS) + ADAM_WD * z)
        mo_ref[...] = mn
        vo_ref[...] = vn

    row = _full((1, D_MODEL))
    return _pc(body, name="cctx_final", out_shape=[_sds((1, D_MODEL))] * 4,
               in_specs=[_full(parts.shape), row, row, row], out_specs=[row] * 4,
               compiler_params=_params())(parts, c_ctx.reshape(1, D_MODEL), m.reshape(1, D_MODEL), v.reshape(1, D_MODEL))


def _rotary_tables(t_len):
    rows = t_len // GRID_W
    row = jnp.repeat(jnp.arange(rows, dtype=F32), GRID_W)
    col = jnp.tile(jnp.arange(GRID_W, dtype=F32), rows)
    n_freq = DH // 4
    inv = ROPE_BASE ** (-jnp.arange(n_freq, dtype=F32) / n_freq)
    ang = jnp.concatenate([row[:, None] * inv, col[:, None] * inv], axis=-1)
    cos, sin = jnp.cos(ang), jnp.sin(ang)
    return jnp.concatenate([cos, cos], axis=-1), jnp.concatenate([-sin, sin], axis=-1)


def _inproj_fwd(x, gn, sh, sc, w4, cos2, sin2, name):
    t = x.shape[0]
    tm = _tile(t)
    nc = IN_COLS // N_CHIP

    def body(x_ref, gn_ref, sh_ref, sc_ref, w_ref, c_ref, s_ref, p_ref, hb_ref):
        xh, _ = _rms(x_ref[...])
        h = xh * gn_ref[...] * (1.0 + sc_ref[...]) + sh_ref[...]
        hb = h.astype(BF16)
        hb_ref[...] = hb
        for j in range(N_CHIP):
            p_ref[:, nc * j:nc * (j + 1)] = _dot(hb, w_ref[j])
        cc = c_ref[...]
        ss = s_ref[...]
        for hh in range(2 * HEADS):
            blk = p_ref[:, DH * hh:DH * (hh + 1)]
            rot = blk * cc + pltpu.roll(blk, DH // 2, 1) * ss
            if hh >= HEADS:
                rot = rot * K_SCALE
            p_ref[:, DH * hh:DH * (hh + 1)] = rot

    row = _full((1, D_MODEL))
    return _pc(body, name=name, grid=(t // tm,),
               in_specs=[pl.BlockSpec((tm, D_MODEL), lambda i: (i, 0)), row, row, row, _full(w4.shape),
                         pl.BlockSpec((tm, DH), lambda i: (i, 0)), pl.BlockSpec((tm, DH), lambda i: (i, 0))],
               out_specs=[pl.BlockSpec((tm, IN_COLS), lambda i: (i, 0)), pl.BlockSpec((tm, D_MODEL), lambda i: (i, 0))],
               out_shape=[_sds((t, IN_COLS)), _sds((t, D_MODEL), BF16)],
               compiler_params=_params("arbitrary"))(x, gn, sh, sc, w4, cos2, sin2)


def _inproj_bwd(x, gn, sh, sc, w4, cos2, sin2, pieces, dres, name):
    t = x.shape[0]
    tm = _tile(t)
    nc = IN_COLS // N_CHIP

    def body(x_ref, gn_ref, sh_ref, sc_ref, w_ref, c_ref, s_ref, dqf, dqb, dkf, dkb, dvf, dvb, dg, dxr, dgt, dres_ref,
             dx_ref, dpb_ref, dgn_ref, dsh_ref, dsc_ref):
        cc = c_ref[...]
        ss = s_ref[...]
        dq = dqf[...] + dqb[...]
        dk = dkf[...] + dkb[...]
        for hh in range(HEADS):
            sl = slice(DH * hh, DH * (hh + 1))
            b = dq[:, sl]
            dpb_ref[:, sl] = (b * cc + pltpu.roll(b * ss, DH // 2, 1)).astype(BF16)
            b = dk[:, sl]
            dpb_ref[:, RET_W + DH * hh:RET_W + DH * (hh + 1)] = (
                (b * cc + pltpu.roll(b * ss, DH // 2, 1)) * K_SCALE).astype(BF16)
        dpb_ref[:, 2 * RET_W:3 * RET_W] = (dvf[...] + dvb[...]).astype(BF16)
        dpb_ref[:, 3 * RET_W:4 * RET_W] = dg[...].astype(BF16)
        dpb_ref[:, 4 * RET_W:4 * RET_W + LRU_W] = dxr[...].astype(BF16)
        dpb_ref[:, 4 * RET_W + LRU_W:IN_COLS] = dgt[...].astype(BF16)
        dh = _dot_nt(dpb_ref[:, 0:nc], w_ref[0])
        for j in range(1, N_CHIP):
            dh = dh + _dot_nt(dpb_ref[:, nc * j:nc * (j + 1)], w_ref[j])
        dx, dgn_t, dsh_t, dsc_t = _norm_mod_bwd(x_ref[...], gn_ref[...], sc_ref[...], dh)
        dx_ref[...] = dres_ref[...] + dx

        @pl.when(pl.program_id(0) == 0)
        def _():
            dgn_ref[...] = jnp.zeros_like(dgn_ref)
            dsh_ref[...] = jnp.zeros_like(dsh_ref)
            dsc_ref[...] = jnp.zeros_like(dsc_ref)
        dgn_ref[...] += dgn_t
        dsh_ref[...] += dsh_t
        dsc_ref[...] += dsc_t

    row = _full((1, D_MODEL))
    pc = pl.BlockSpec((tm, RET_W), lambda i: (i, 0))
    big = pl.BlockSpec((tm, D_MODEL), lambda i: (i, 0))
    return _pc(body, name=name, grid=(t // tm,),
               in_specs=[big, row, row, row, _full(w4.shape),
                         pl.BlockSpec((tm, DH), lambda i: (i, 0)), pl.BlockSpec((tm, DH), lambda i: (i, 0))]
               + [pc] * 9 + [big],
               out_specs=[big, pl.BlockSpec((tm, IN_COLS), lambda i: (i, 0)), row, row, row],
               out_shape=[_sds((t, D_MODEL)), _sds((t, IN_COLS), BF16), _sds((1, D_MODEL)), _sds((1, D_MODEL)),
                          _sds((1, D_MODEL))],
               compiler_params=_params("arbitrary"))(x, gn, sh, sc, w4, cos2, sin2, *pieces, dres)


XR_BLOCK = (4 * RET_W) // LRU_W


def _halo_specs(t, tm, col):
    n8 = tm // SUBLANES
    last8 = t // SUBLANES - 1
    prev = pl.BlockSpec((SUBLANES, LRU_W), lambda i: (jnp.maximum(i * n8 - 1, 0), col))
    main = pl.BlockSpec((tm, LRU_W), lambda i: (i, col))
    nxt = pl.BlockSpec((SUBLANES, LRU_W), lambda i: (jnp.minimum((i + 1) * n8, last8), col))
    return prev, main, nxt


def _with_halo(prev_ref, main_ref, next_ref, i, nt):
    prev = jnp.where(i > 0, prev_ref[...], 0.0)
    nxt = jnp.where(i < nt - 1, next_ref[...], 0.0)
    return jnp.concatenate([prev, main_ref[...], nxt], axis=0)


def _conv_fwd(proj, cw, cb, name):
    t = proj.shape[0]
    tm = _tile(t)
    nt = t // tm
    n = tm + 2 * SUBLANES
    mid = slice(SUBLANES, SUBLANES + tm)

    def body(p_ref, m_ref, n_ref, w_ref, b_ref, o_ref):
        xp = _with_halo(p_ref, m_ref, n_ref, pl.program_id(0), nt)
        acc = b_ref[...] + pltpu.roll(xp, 1, 0)[mid] * w_ref[0:1, :]
        acc = acc + xp[mid] * w_ref[1:2, :]
        acc = acc + pltpu.roll(xp, n - 1, 0)[mid] * w_ref[2:3, :]
        acc = acc + pltpu.roll(xp, n - 2, 0)[mid] * w_ref[3:4, :]
        o_ref[...] = acc

    return _pc(body, name=name, grid=(nt,),
               in_specs=[*_halo_specs(t, tm, XR_BLOCK), _full((4, LRU_W)), _full((1, LRU_W))],
               out_specs=pl.BlockSpec((tm, LRU_W), lambda i: (i, 0)), out_shape=_sds((t, LRU_W)),
               compiler_params=_params("arbitrary"))(proj, proj, proj, cw, cb)


def _conv_bwd(dxc, proj, cw, name):
    t = proj.shape[0]
    tm = _tile(t)
    nt = t // tm
    n = tm + 2 * SUBLANES
    mid = slice(SUBLANES, SUBLANES + tm)

    def body(dp_ref, dm_ref, dn_ref, xp_ref, xm_ref, xn_ref, w_ref, dx_ref, dw_ref, db_ref):
        i = pl.program_id(0)
        dp = _with_halo(dp_ref, dm_ref, dn_ref, i, nt)
        xp = _with_halo(xp_ref, xm_ref, xn_ref, i, nt)
        dx = pltpu.roll(dp, n - 1, 0)[mid] * w_ref[0:1, :]
        dx = dx + dp[mid] * w_ref[1:2, :]
        dx = dx + pltpu.roll(dp, 1, 0)[mid] * w_ref[2:3, :]
        dx = dx + pltpu.roll(dp, 2, 0)[mid] * w_ref[3:4, :]
        dx_ref[...] = dx
        d = dm_ref[...]

        @pl.when(i == 0)
        def _():
            dw_ref[...] = jnp.zeros_like(dw_ref)
            db_ref[...] = jnp.zeros_like(db_ref)
        dw_ref[0:1, :] += _sum0(d * pltpu.roll(xp, 1, 0)[mid])
        dw_ref[1:2, :] += _sum0(d * xp[mid])
        dw_ref[2:3, :] += _sum0(d * pltpu.roll(xp, n - 1, 0)[mid])
        dw_ref[3:4, :] += _sum0(d * pltpu.roll(xp, n - 2, 0)[mid])
        db_ref[...] += _sum0(d)

    return _pc(body, name=name, grid=(nt,),
               in_specs=[*_halo_specs(t, tm, 0), *_halo_specs(t, tm, XR_BLOCK), _full((4, LRU_W))],
               out_specs=[pl.BlockSpec((tm, LRU_W), lambda i: (i, 0)), _full((4, LRU_W)), _full((1, LRU_W))],
               out_shape=[_sds((t, LRU_W)), _sds((4, LRU_W)), _sds((1, LRU_W))],
               compiler_params=_params("arbitrary"))(dxc, dxc, dxc, proj, proj, proj, cw)


def _local_scan(a, b, reverse):
    n = a.shape[0]
    row = lax.broadcasted_iota(jnp.int32, a.shape, 0) & (SUBLANES - 1)
    for s in (1, 2, 4):
        if reverse:
            a_s, b_s, ok = pltpu.roll(a, n - s, 0), pltpu.roll(b, n - s, 0), row < SUBLANES - s
        else:
            a_s, b_s, ok = pltpu.roll(a, s, 0), pltpu.roll(b, s, 0), row >= s
        b = a * jnp.where(ok, b_s, 0.0) + b
        a = a * jnp.where(ok, a_s, 1.0)
    return a, b


def _carry_scan(a_s, b_s, out_ref, carry, reverse):
    ng = a_s.shape[0] // SUBLANES
    shape = carry.shape

    def step(g, cr):
        gg = (ng - 1 - g) if reverse else g
        off = pl.multiple_of(gg * SUBLANES, SUBLANES)
        h = a_s[pl.ds(off, SUBLANES), :] * cr + b_s[pl.ds(off, SUBLANES), :]
        out_ref[pl.ds(off, SUBLANES), :] = h
        edge = h[0:1, :] if reverse else h[SUBLANES - 1:SUBLANES, :]
        return jnp.broadcast_to(edge, shape)

    return lax.fori_loop(0, ng, step, carry)


def _lru_gates(xc, wa_ref, wx_ref, ba, bx, lam):
    xb = xc.astype(BF16)
    r = _sigmoid(_dot(xb, wa_ref[...]) + ba)
    ig = _sigmoid(_dot(xb, wx_ref[...]) + bx)
    sp = _softplus(-lam)
    la = -LRU_C * r * sp
    a = jnp.exp(la)
    mult = jnp.sqrt(_neg_expm1(2.0 * la))
    return r, ig, sp, a, mult


def _lru_fwd(xc, wa, wx, ba, bx, lam, h0, reverse, name):
    t = xc.shape[0]
    tm = _tile(t)
    nt = t // tm
    tidx = (lambda i: (nt - 1 - i, 0)) if reverse else (lambda i: (i, 0))

    def body(x_ref, wa_ref, wx_ref, ba_ref, bx_ref, lam_ref, h0_ref, h_ref, a_s, b_s, c_s):
        @pl.when(pl.program_id(0) == 0)
        def _():
            c_s[...] = jnp.broadcast_to(h0_ref[...], c_s.shape)
        xv = x_ref[...]
        _, ig, _, a, mult = _lru_gates(xv, wa_ref, wx_ref, ba_ref[...], bx_ref[...], lam_ref[...])
        al, bl = _local_scan(a, mult * (ig * xv), reverse)
        a_s[...] = al
        b_s[...] = bl
        c_s[...] = _carry_scan(a_s, b_s, h_ref, c_s[...], reverse)

    vec = _full((1, LRU_W))
    mat = _full((LRU_W, LRU_W))
    return _pc(body, name=name, grid=(nt,),
               in_specs=[pl.BlockSpec((tm, LRU_W), tidx), mat, mat, vec, vec, vec, vec],
               out_specs=pl.BlockSpec((tm, LRU_W), tidx), out_shape=_sds((t, LRU_W)),
               scratch_shapes=[pltpu.VMEM((tm, LRU_W), F32), pltpu.VMEM((tm, LRU_W), F32),
                               pltpu.VMEM((SUBLANES, LRU_W), F32)],
               compiler_params=_params("arbitrary"))(xc, wa, wx, ba, bx, lam, h0)


def _lru_bwd(xc, wa, wx, ba, bx, lam, h, h0, dh, reverse, name, comms=()):
    t = xc.shape[0]
    tm = _tile(t)
    nt = t // tm
    n8 = tm // SUBLANES
    last8 = t // SUBLANES - 1
    tidx = (lambda i: (i, 0)) if reverse else (lambda i: (nt - 1 - i, 0))
    if reverse:
        halo = pl.BlockSpec((SUBLANES, LRU_W), lambda i: (jnp.minimum((i + 1) * n8, last8), 0))
    else:
        halo = pl.BlockSpec((SUBLANES, LRU_W), lambda i: (jnp.maximum((nt - 1 - i) * n8 - 1, 0), 0))

    def body(x_ref, wa_ref, wx_ref, ba_ref, bx_ref, lam_ref, h_ref, halo_ref, h0_ref, dh_ref,
             dx_ref, dpre_ref, dba_ref, dbx_ref, dlam_ref, dh0_ref, a_s, b_s, l_s, c_s, e_s):
        i = pl.program_id(0)

        @pl.when(i == 0)
        def _():
            c_s[...] = jnp.zeros_like(c_s)
            e_s[...] = jnp.zeros_like(e_s)
            dba_ref[...] = jnp.zeros_like(dba_ref)
            dbx_ref[...] = jnp.zeros_like(dbx_ref)
            dlam_ref[...] = jnp.zeros_like(dlam_ref)
        xv = x_ref[...]
        lam = lam_ref[...]
        r, ig, sp, a, mult = _lru_gates(xv, wa_ref, wx_ref, ba_ref[...], bx_ref[...], lam)
        hv = h_ref[...]
        rowi = lax.broadcasted_iota(jnp.int32, (tm, LRU_W), 0)
        edge_a = jnp.broadcast_to(e_s[0:1, :], (tm, LRU_W))
        h0b = jnp.broadcast_to(h0_ref[...], (tm, LRU_W))
        if reverse:
            a_sh = jnp.where(rowi == 0, edge_a, pltpu.roll(a, 1, 0))
            hin_edge = jnp.where(i == nt - 1, h0b, jnp.broadcast_to(halo_ref[0:1, :], (tm, LRU_W)))
            h_in = jnp.where(rowi == tm - 1, hin_edge, pltpu.roll(hv, tm - 1, 0))
        else:
            a_sh = jnp.where(rowi == tm - 1, edge_a, pltpu.roll(a, tm - 1, 0))
            hin_edge = jnp.where(i == nt - 1, h0b, jnp.broadcast_to(halo_ref[SUBLANES - 1:SUBLANES, :], (tm, LRU_W)))
            h_in = jnp.where(rowi == 0, hin_edge, pltpu.roll(hv, 1, 0))
        al, bl = _local_scan(a_sh, dh_ref[...], not reverse)
        a_s[...] = al
        b_s[...] = bl
        c_s[...] = _carry_scan(a_s, b_s, l_s, c_s[...], not reverse)
        e_s[...] = jnp.broadcast_to(a[tm - 1:tm, :] if reverse else a[0:1, :], e_s.shape)
        lmb = l_s[...]
        da = lmb * h_in
        ixc = ig * xv
        dmult = lmb * ixc
        dixc = lmb * mult
        dla = da * a - dmult * (a * a) / mult
        dpr = dla * (-LRU_C * sp) * r * (1.0 - r)
        dpi = dixc * xv * ig * (1.0 - ig)
        dprb = dpr.astype(BF16)
        dpib = dpi.astype(BF16)
        dpre_ref[:, 0:LRU_W] = dprb
        dpre_ref[:, LRU_W:2 * LRU_W] = dpib
        dx_ref[...] = dixc * ig + _dot_nt(dprb, wa_ref[...]) + _dot_nt(dpib, wx_ref[...])
        dba_ref[...] += _sum0(dpr)
        dbx_ref[...] += _sum0(dpi)
        dlam_ref[...] += _sum0(dla * (-LRU_C * r)) * (-_sigmoid(-lam))

        @pl.when(i == nt - 1)
        def _():
            al0 = a * lmb
            dh0_ref[...] = al0[tm - 1:tm, :] if reverse else al0[0:1, :]

    vec = _full((1, LRU_W))
    mat = _full((LRU_W, LRU_W))
    tile = pl.BlockSpec((tm, LRU_W), tidx)
    return _call(body, name=name, grid=(nt,),
                 in_specs=[tile, mat, mat, vec, vec, vec, tile, halo, vec, tile],
                 out_specs=[tile, pl.BlockSpec((tm, 2 * LRU_W), tidx), vec, vec, vec, vec],
                 out_shape=[_sds((t, LRU_W)), _sds((t, 2 * LRU_W), BF16), _sds((1, LRU_W)), _sds((1, LRU_W)),
                            _sds((1, LRU_W)), _sds((1, LRU_W))],
                 scratch_shapes=[pltpu.VMEM((tm, LRU_W), F32), pltpu.VMEM((tm, LRU_W), F32),
                                 pltpu.VMEM((tm, LRU_W), F32), pltpu.VMEM((SUBLANES, LRU_W), F32),
                                 pltpu.VMEM((SUBLANES, LRU_W), F32)],
                 sem=("arbitrary",), args=(xc, wa, wx, ba, bx, lam, h, h, h0, dh), comms=comms)


def _decay_tables(lg, reverse):
    ci = lax.broadcasted_iota(jnp.int32, (CHUNK, CHUNK), 0).astype(F32)
    mi = lax.broadcasted_iota(jnp.int32, (CHUNK, CHUNK), 1).astype(F32)
    if reverse:
        rel, pq, ps = mi - ci, CHUNK - ci, ci
    else:
        rel, pq, ps = ci - mi, ci + 1.0, CHUNK - 1.0 - ci
    relc = jnp.maximum(rel, 0.0)
    dm = jnp.where(rel >= 0, jnp.exp(lg * relc), 0.0)
    return relc, dm, jnp.exp(lg * pq), jnp.exp(lg * ps), jnp.exp(lg * float(CHUNK)), pq, ps


def _ret_fwd(proj, lgv, s0f, s0b, comms=()):
    t = proj.shape[0]
    n = t // CHUNK

    def one(q, k, v, lg, s_s, hh, o_ref, sp_ref, reverse):
        _, dm, wq, ws, g, _, _ = _decay_tables(lg, reverse)
        vb = v.astype(BF16)
        p = _dot_nt(q.astype(BF16), k.astype(BF16)) * dm
        s = s_s[hh]
        sp_ref[hh, 0] = s
        o_ref[:, DH * hh:DH * (hh + 1)] = _dot(p.astype(BF16), vb) + _dot((q * wq).astype(BF16), s.astype(BF16))
        s_s[hh] = g * s + _dot_tn((k * ws).astype(BF16), vb)

    def body(qf, kf, vf, qb, kb, vb, lg_ref, s0f_ref, s0b_ref, of_ref, ob_ref, spf_ref, spb_ref, sf_s, sb_s):
        @pl.when(pl.program_id(0) == 0)
        def _():
            sf_s[...] = s0f_ref[...]
            sb_s[...] = s0b_ref[...]
        for hh in range(HEADS):
            sl = slice(DH * hh, DH * (hh + 1))
            one(qf[:, sl], kf[:, sl], vf[:, sl], lg_ref[hh, 0:1, :], sf_s, hh, of_ref, spf_ref, False)
            one(qb[:, sl], kb[:, sl], vb[:, sl], lg_ref[hh, 1:2, :], sb_s, hh, ob_ref, spb_ref, True)

    blk = (CHUNK, RET_W)
    fw = [pl.BlockSpec(blk, lambda i, o=o: (i, o)) for o in range(3)]
    bw = [pl.BlockSpec(blk, lambda i, o=o: (n - 1 - i, o)) for o in range(3)]
    st = _full((HEADS, DH, DH))
    return _call(body, name="ret_fwd", grid=(n,),
                 in_specs=fw + bw + [_full((HEADS, 2, LANES)), st, st],
                 out_specs=[pl.BlockSpec(blk, lambda i: (i, 0)), pl.BlockSpec(blk, lambda i: (n - 1 - i, 0)),
                            pl.BlockSpec((HEADS, 1, DH, DH), lambda i: (0, i, 0, 0)),
                            pl.BlockSpec((HEADS, 1, DH, DH), lambda i: (0, n - 1 - i, 0, 0))],
                 out_shape=[_sds((t, RET_W)), _sds((t, RET_W)), _sds((HEADS, n, DH, DH)), _sds((HEADS, n, DH, DH))],
                 scratch_shapes=[pltpu.VMEM((HEADS, DH, DH), F32), pltpu.VMEM((HEADS, DH, DH), F32)],
                 sem=("arbitrary",), args=(proj, proj, proj, proj, proj, proj, lgv, s0f, s0b), comms=comms)


def _ret_bwd(proj, lgv, sgv, sprev, do, reverse, name, comms=()):
    t = proj.shape[0]
    n = t // CHUNK
    d = 1 if reverse else 0
    cidx = (lambda i: i) if reverse else (lambda i: n - 1 - i)

    def body(q_ref, k_ref, v_ref, lg_ref, sg_ref, s_ref, do_ref, dq_ref, dk_ref, dv_ref, ds0_ref, drd_ref, ds_s, acc_s):
        i = pl.program_id(0)

        @pl.when(i == 0)
        def _():
            ds_s[...] = jnp.zeros_like(ds_s)
            acc_s[...] = jnp.zeros_like(acc_s)
        for hh in range(HEADS):
            sl = slice(DH * hh, DH * (hh + 1))
            relc, dm, wq, ws, g, pq, ps = _decay_tables(lg_ref[hh, d:d + 1, :], reverse)
            q, k = q_ref[:, sl], k_ref[:, sl]
            qb, kb, vb = q.astype(BF16), k.astype(BF16), v_ref[:, sl].astype(BF16)
            p = _dot_nt(qb, kb) * dm
            s = s_ref[hh, 0]
            dob = do_ref[:, sl].astype(BF16)
            dsn = ds_s[hh]
            dsb = dsn.astype(BF16)
            dv_ref[:, sl] = _dot_tn(p.astype(BF16), dob) + _dot((k * ws).astype(BF16), dsb)
            dp = _dot_nt(dob, vb)
            dab = (dp * dm).astype(BF16)
            xq = _dot_nt(dob, s.astype(BF16))
            yk = _dot_nt(vb, dsb)
            dq_ref[:, sl] = _dot(dab, kb) + xq * wq
            dk_ref[:, sl] = _dot_tn(dab, qb) + yk * ws
            ds_s[hh] = g * dsn + _dot_tn((q * wq).astype(BF16), dob)
            part = (_sum0(dp * p * relc) + _sum0(xq * q * wq * pq) + _sum0(yk * k * ws * ps)
                    + _sum0(dsn * s) * g * float(CHUNK))
            acc_s[hh] += jnp.broadcast_to(part, (SUBLANES, LANES))

        @pl.when(i == n - 1)
        def _():
            ds0_ref[...] = ds_s[...]
            for hh in range(HEADS):
                tot = jnp.sum(acc_s[hh, 0:1, :], axis=1, keepdims=True)
                drd_ref[hh] = jnp.broadcast_to(tot, (SUBLANES, LANES)) * sg_ref[hh, d:d + 1, :]

    blk = (CHUNK, RET_W)
    qkv = [pl.BlockSpec(blk, lambda i, o=o: (cidx(i), o)) for o in range(3)]
    hc = pl.BlockSpec(blk, lambda i: (cidx(i), 0))
    lane = _full((HEADS, 2, LANES))
    return _call(body, name=name, grid=(n,),
                 in_specs=qkv + [lane, lane, pl.BlockSpec((HEADS, 1, DH, DH), lambda i: (0, cidx(i), 0, 0)), hc],
                 out_specs=[hc, hc, hc, _full((HEADS, DH, DH)), _full((HEADS, SUBLANES, LANES))],
                 out_shape=[_sds((t, RET_W))] * 3 + [_sds((HEADS, DH, DH)), _sds((HEADS, SUBLANES, LANES))],
                 scratch_shapes=[pltpu.VMEM((HEADS, DH, DH), F32), pltpu.VMEM((HEADS, SUBLANES, LANES), F32)],
                 sem=("arbitrary",), args=(proj, proj, proj, lgv, sgv, sprev, do), comms=comms)


def _ctx_weights(lg, l_len, reverse):
    pos = lax.broadcasted_iota(jnp.int32, (l_len, DH), 0).astype(F32)
    steps = pos if reverse else (l_len - 1.0 - pos)
    return jnp.exp(lg * steps), steps


def _ctx_state_fwd(projc, lgv):
    l_len = projc.shape[0]

    def body(k_ref, v_ref, lg_ref, sf_ref, sb_ref):
        k = k_ref[...]
        vb = v_ref[...].astype(BF16)
        for d, o_ref in ((0, sf_ref), (1, sb_ref)):
            w, _ = _ctx_weights(lg_ref[0, d:d + 1, :], l_len, d == 1)
            o_ref[0] = _dot_tn((k * w).astype(BF16), vb)

    st = pl.BlockSpec((1, DH, DH), lambda h: (h, 0, 0))
    return _pc(body, name="ctx_state_fwd", grid=(HEADS,),
               in_specs=[pl.BlockSpec((l_len, DH), lambda h: (0, HEADS + h)),
                         pl.BlockSpec((l_len, DH), lambda h: (0, 2 * HEADS + h)),
                         pl.BlockSpec((1, 2, LANES), lambda h: (h, 0, 0))],
               out_specs=[st, st], out_shape=[_sds((HEADS, DH, DH))] * 2,
               compiler_params=_params("arbitrary"))(projc, projc, lgv)


def _ctx_state_bwd(projc, lgv, sgv, dsf, dsb):
    l_len = projc.shape[0]

    def body(k_ref, v_ref, lg_ref, sg_ref, dsf_ref, dsb_ref, dk_ref, dv_ref, drd_ref):
        k = k_ref[...]
        vb = v_ref[...].astype(BF16)
        dk = jnp.zeros((l_len, DH), F32)
        dv = jnp.zeros((l_len, DH), F32)
        rows = []
        for d, ds_ref in ((0, dsf_ref), (1, dsb_ref)):
            w, steps = _ctx_weights(lg_ref[0, d:d + 1, :], l_len, d == 1)
            dsb16 = ds_ref[0].astype(BF16)
            dkw = _dot_nt(vb, dsb16)
            dk = dk + dkw * w
            dv = dv + _dot((k * w).astype(BF16), dsb16)
            tot = jnp.sum(_sum0(dkw * k * w * steps), axis=1, keepdims=True)
            rows.append(jnp.broadcast_to(tot, (1, LANES)) * sg_ref[0, d:d + 1, :])
        dk_ref[...] = dk
        dv_ref[...] = dv
        rid = lax.broadcasted_iota(jnp.int32, (SUBLANES, LANES), 0)
        drd_ref[0] = jnp.where(rid == 0, rows[0], jnp.where(rid == 1, rows[1], 0.0))

    st = pl.BlockSpec((1, DH, DH), lambda h: (h, 0, 0))
    lane = pl.BlockSpec((1, 2, LANES), lambda h: (h, 0, 0))
    hc = pl.BlockSpec((l_len, DH), lambda h: (0, h))
    return _pc(body, name="ctx_state_bwd", grid=(HEADS,),
               in_specs=[pl.BlockSpec((l_len, DH), lambda h: (0, HEADS + h)),
                         pl.BlockSpec((l_len, DH), lambda h: (0, 2 * HEADS + h)), lane, lane, st, st],
               out_specs=[hc, hc, pl.BlockSpec((1, SUBLANES, LANES), lambda h: (h, 0, 0))],
               out_shape=[_sds((l_len, RET_W)), _sds((l_len, RET_W)), _sds((HEADS, SUBLANES, LANES))],
               compiler_params=_params("arbitrary"))(projc, projc, lgv, sgv, dsf, dsb)


G_BLOCK = (3 * RET_W) // RET_W
GATE_BLOCK = (4 * RET_W + LRU_W) // LRU_W


def _head_norm(y):
    yc = y - jnp.mean(y, axis=-1, keepdims=True)
    rs = lax.rsqrt(jnp.mean(yc * yc, axis=-1, keepdims=True) + EPS)
    return yc * rs, rs


def _gelu_parts(z):
    th = jnp.tanh(GELU_K * (z + GELU_C * z * z * z))
    return 0.5 * z * (1.0 + th), th


def _mix_fwd(o_f, o_b, proj, hf, hb, w_out, x, g1):
    t = x.shape[0]
    tm = _tile(t)

    def body(of_ref, ob_ref, g_ref, gt_ref, hf_ref, hb_ref, w_ref, x_ref, g1_ref, x1_ref, cat_ref):
        o = of_ref[...] + ob_ref[...]
        g = g_ref[...]
        for hh in range(HEADS):
            sl = slice(DH * hh, DH * (hh + 1))
            nrm, _ = _head_norm(o[:, sl])
            gh = g[:, sl]
            cat_ref[:, sl] = (gh * _sigmoid(gh) * nrm).astype(BF16)
        gel, _ = _gelu_parts(gt_ref[...])
        cat_ref[:, RET_W:] = ((hf_ref[...] + hb_ref[...]) * gel).astype(BF16)
        x1_ref[...] = x_ref[...] + g1_ref[...] * _dot(cat_ref[...], w_ref[...])

    half = pl.BlockSpec((tm, RET_W), lambda i: (i, 0))
    big = pl.BlockSpec((tm, D_MODEL), lambda i: (i, 0))
    return _pc(body, name="mix_fwd", grid=(t // tm,),
               in_specs=[half, half, pl.BlockSpec((tm, RET_W), lambda i: (i, G_BLOCK)),
                         pl.BlockSpec((tm, LRU_W), lambda i: (i, GATE_BLOCK)), half, half,
                         _full((D_MODEL, D_MODEL)), big, _full((1, D_MODEL))],
               out_specs=[big, big], out_shape=[_sds((t, D_MODEL)), _sds((t, D_MODEL), BF16)],
               compiler_params=_params("arbitrary"))(o_f, o_b, proj, proj, hf, hb, w_out, x, g1)


def _mix_bwd(o_f, o_b, proj, hf, hb, w_out, cat, dx1, g1, comms=()):
    t = dx1.shape[0]
    tm = _tile(t)

    def body(of_ref, ob_ref, g_ref, gt_ref, hf_ref, hb_ref, w_ref, cat_ref, dx1_ref, g1_ref,
             do_ref, dhs_ref, dg_ref, dgt_ref, dyb_ref, dg1_ref):
        dx1v = dx1_ref[...]
        y = _dot(cat_ref[...], w_ref[...])

        @pl.when(pl.program_id(0) == 0)
        def _():
            dg1_ref[...] = jnp.zeros_like(dg1_ref)
        dg1_ref[...] += _sum0(dx1v * y)
        dyb = (g1_ref[...] * dx1v).astype(BF16)
        dyb_ref[...] = dyb
        dcat = _dot_nt(dyb, w_ref[...])
        o = of_ref[...] + ob_ref[...]
        g = g_ref[...]
        for hh in range(HEADS):
            sl = slice(DH * hh, DH * (hh + 1))
            nrm, rs = _head_norm(o[:, sl])
            gh = g[:, sl]
            sg = _sigmoid(gh)
            dret = dcat[:, sl]
            dg_ref[:, sl] = dret * nrm * (sg * (1.0 + gh * (1.0 - sg)))
            dn = dret * (gh * sg)
            dyc = rs * (dn - nrm * jnp.mean(dn * nrm, axis=-1, keepdims=True))
            do_ref[:, sl] = dyc - jnp.mean(dyc, axis=-1, keepdims=True)
        z = gt_ref[...]
        gel, th = _gelu_parts(z)
        dlru = dcat[:, RET_W:]
        dhs_ref[...] = dlru * gel
        dgel = 0.5 * (1.0 + th) + 0.5 * z * (1.0 - th * th) * GELU_K * (1.0 + 3.0 * GELU_C * z * z)
        dgt_ref[...] = dlru * (hf_ref[...] + hb_ref[...]) * dgel

    half = pl.BlockSpec((tm, RET_W), lambda i: (i, 0))
    big = pl.BlockSpec((tm, D_MODEL), lambda i: (i, 0))
    return _call(body, name="mix_bwd", grid=(t // tm,),
                 in_specs=[half, half, pl.BlockSpec((tm, RET_W), lambda i: (i, G_BLOCK)),
                           pl.BlockSpec((tm, LRU_W), lambda i: (i, GATE_BLOCK)), half, half,
                           _full((D_MODEL, D_MODEL)), big, big, _full((1, D_MODEL))],
                 out_specs=[half, half, half, half, big, _full((1, D_MODEL))],
                 out_shape=[_sds((t, RET_W))] * 4 + [_sds((t, D_MODEL), BF16), _sds((1, D_MODEL))],
                 scratch_shapes=[], sem=("arbitrary",), args=(o_f, o_b, proj, proj, hf, hb, w_out, cat, dx1, g1),
                 comms=comms)


def _mlp(x1, n2g, sh2, sc2, g2, fg, w1, w2, tgt):
    t = x1.shape[0]
    tm = _tile(t)
    hb_ = MLP_H // N_CHIP

    def body(x1_ref, n2g_ref, sh2_ref, sc2_ref, g2_ref, fg_ref, w1_hbm, w2_hbm, tgt_ref,
             dx1_ref, h2b_ref, ab_ref, dub_ref, dmb_ref, dsc_ref, dsh_ref, dg2_ref, dn2_ref, dfg_ref, loss_ref,
             w1_s, w2_s, r_s, sems):
        @pl.when(pl.program_id(0) == 0)
        def _():
            c1 = pltpu.make_async_copy(w1_hbm, w1_s, sems.at[0])
            c2 = pltpu.make_async_copy(w2_hbm, w2_s, sems.at[1])
            c1.start()
            c2.start()
            for r in (dsc_ref, dsh_ref, dg2_ref, dn2_ref, dfg_ref, loss_ref):
                r[...] = jnp.zeros_like(r)
            c1.wait()
            c2.wait()
        x1v = x1_ref[...]
        n2g, sc2, g2, fg = n2g_ref[...], sc2_ref[...], g2_ref[...], fg_ref[...]
        xh, _ = _rms(x1v)
        h2b = (xh * n2g * (1.0 + sc2) + sh2_ref[...]).astype(BF16)
        h2b_ref[...] = h2b
        m = jnp.zeros((tm, D_MODEL), F32)
        for j in range(N_CHIP):
            sl = slice(hb_ * j, hb_ * (j + 1))
            r = jnp.maximum(_dot(h2b, w1_s[j]), 0.0)
            r_s[:, sl] = r
            ab = (r * r).astype(BF16)
            ab_ref[:, sl] = ab
            m = m + _dot(ab, w2_s[j])
        x2 = x1v + g2 * m
        x2h, r2 = _rms(x2)
        err = x2h * fg - tgt_ref[...]
        loss_ref[...] += _sum0(err * err)
        dout = err * (1.0 / D_MODEL)
        dfg_ref[...] += _sum0(dout * x2h)
        dxh = dout * fg
        dx2 = r2 * (dxh - x2h * jnp.mean(dxh * x2h, axis=-1, keepdims=True))
        dg2_ref[...] += _sum0(dx2 * m)
        dmb = (g2 * dx2).astype(BF16)
        dmb_ref[...] = dmb
        dh2 = jnp.zeros((tm, D_MODEL), F32)
        for j in range(N_CHIP):
            sl = slice(hb_ * j, hb_ * (j + 1))
            dub = (_dot_nt(dmb, w2_s[j]) * (2.0 * r_s[:, sl])).astype(BF16)
            dub_ref[:, sl] = dub
            dh2 = dh2 + _dot_nt(dub, w1_s[j])
        dx, dn2_t, dsh_t, dsc_t = _norm_mod_bwd(x1v, n2g, sc2, dh2)
        dx1_ref[...] = dx2 + dx
        dn2_ref[...] += dn2_t
        dsh_ref[...] += dsh_t
        dsc_ref[...] += dsc_t

        @pl.when(pl.program_id(0) == t // tm - 1)
        def _():
            tot = jnp.sum(loss_ref[...], axis=1, keepdims=True) * (0.5 / D_MODEL)
            loss_ref[...] = jnp.broadcast_to(tot, loss_ref.shape)

    row = _full((1, D_MODEL))
    big = pl.BlockSpec((tm, D_MODEL), lambda i: (i, 0))
    wide = pl.BlockSpec((tm, MLP_H), lambda i: (i, 0))
    return _pc(body, name="mlp", grid=(t // tm,),
               in_specs=[big, row, row, row, row, row, ANY, ANY, big],
               out_specs=[big, big, wide, wide, big, row, row, row, row, row, row],
               out_shape=[_sds((t, D_MODEL)), _sds((t, D_MODEL), BF16), _sds((t, MLP_H), BF16), _sds((t, MLP_H), BF16),
                          _sds((t, D_MODEL), BF16)] + [_sds((1, D_MODEL))] * 6,
               scratch_shapes=[pltpu.VMEM(w1.shape, BF16), pltpu.VMEM(w2.shape, BF16), pltpu.VMEM((tm, MLP_H), F32),
                               pltpu.SemaphoreType.DMA((2,))],
               compiler_params=_params("arbitrary"))(x1, n2g, sh2, sc2, g2, fg, w1, w2, tgt)


def _tn(a, b, nj, a_blocked, b_blocked, name, extra=None):
    t = a.shape[0]
    m = a.shape[1] // (nj if a_blocked else 1)
    n = b.shape[1] // (nj if b_blocked else 1)
    bk = 512 if t % 512 == 0 else t
    nk = t // bk
    a_map = (lambda j, k: (k, j)) if a_blocked else (lambda j, k: (k, 0))
    b_map = (lambda j, k: (k, j)) if b_blocked else (lambda j, k: (k, 0))
    in_specs = [pl.BlockSpec((bk, m), a_map), pl.BlockSpec((bk, n), b_map)]
    args = [a, b]
    if extra is not None:
        a2, b2 = extra
        t2 = a2.shape[0]
        in_specs += [pl.BlockSpec((t2, m), (lambda j, k: (0, j)) if a_blocked else (lambda j, k: (0, 0))),
                     pl.BlockSpec((t2, n), (lambda j, k: (0, j)) if b_blocked else (lambda j, k: (0, 0)))]
        args += [a2, b2]

    def body(*refs):
        a_ref, b_ref = refs[0], refs[1]
        o_ref, acc = refs[-2], refs[-1]
        k = pl.program_id(1)

        @pl.when(k == 0)
        def _():
            acc[...] = jnp.zeros_like(acc)
        acc[...] += _dot_tn(a_ref[...].astype(BF16), b_ref[...].astype(BF16))

        @pl.when(k == nk - 1)
        def _():
            if extra is not None:
                acc[...] += _dot_tn(refs[2][...].astype(BF16), refs[3][...].astype(BF16))
            o_ref[0] = acc[...]

    return _pc(body, name=name, grid=(nj, nk), in_specs=in_specs,
               out_specs=pl.BlockSpec((1, m, n), lambda j, k: (j, 0, 0)), out_shape=_sds((nj, m, n)),
               scratch_shapes=[pltpu.VMEM((m, n), F32)],
               compiler_params=_params("arbitrary", "arbitrary"))(*args)


def _block_diag(w):
    eye = jnp.eye(LRU_BLOCKS, dtype=F32)
    return (w[:, :, None, :] * eye[:, None, :, None]).reshape(LRU_W, LRU_W).astype(BF16)


def _diag_blocks(mat):
    eye = jnp.eye(LRU_BLOCKS, dtype=jnp.bool_)
    m4 = mat.reshape(LRU_BLOCKS, LRU_BD, LRU_BLOCKS, LRU_BD)
    return jnp.sum(jnp.where(eye[:, None, :, None], m4, 0.0), axis=2)


def _pad_rows(v, width=LANES):
    flat = v.reshape(-1).astype(F32)
    tile = SUBLANES * width
    n = -(-flat.shape[0] // tile) * tile
    return jnp.pad(flat, (0, n - flat.shape[0])).reshape(n // width, width)


class _Slab:
    def __init__(self):
        self.parts, self.meta, self.rows = [], {}, 0

    def add(self, name, v):
        p = _pad_rows(v)
        self.meta[name] = (self.rows, p.shape[0], v.shape)
        self.parts.append(p)
        self.rows += p.shape[0]

    def build(self):
        return jnp.concatenate(self.parts, axis=0)

    def take(self, slab, name, shape=None):
        r0, nr, shp = self.meta[name]
        shp = shp if shape is None else shape
        return slab[r0:r0 + nr].reshape(-1)[:math.prod(shp)].reshape(shp)


def _lane_rep(v8):
    return jnp.broadcast_to(v8.reshape(SUBLANES, 1), (SUBLANES, LANES))


def kernel(x, c, ctx, c_ctx, w_ada, b_ada, norm1_g, norm2_g, w_in, ret_decay, conv_w, conv_b, lru_wa, lru_ba, lru_wx, lru_bx, lru_lambda, w_out, w_mlp1, w_mlp2, final_g, loss_target, m_c_ctx, m_w_ada, m_b_ada, m_norm1_g, m_norm2_g, m_w_in, m_ret_decay, m_conv_w, m_conv_b, m_lru_wa, m_lru_ba, m_lru_wx, m_lru_bx, m_lru_lambda, m_w_out, m_w_mlp1, m_w_mlp2, m_final_g, v_c_ctx, v_w_ada, v_b_ada, v_norm1_g, v_norm2_g, v_w_in, v_ret_decay, v_conv_w, v_conv_b, v_lru_wa, v_lru_ba, v_lru_wx, v_lru_bx, v_lru_lambda, v_w_out, v_w_mlp1, v_w_mlp2, v_final_g):
    ax, ay, ac = lax.axis_index("x"), lax.axis_index("y"), lax.axis_index("c")
    chip = 2 * ax + ay
    dev = 4 * ax + 2 * ay + ac
    c_idx = ac.reshape(1).astype(jnp.int32)
    j_idx = chip.reshape(1).astype(jnp.int32)

    xt = x[0]
    t_len = xt.shape[0]
    ctxt = ctx[0]
    l_len = ctxt.shape[0]
    tgt = loss_target[0]
    ada_n = w_ada.shape[2]

    def my_half(w2d):
        r = w2d.shape[0] // 2
        return lax.dynamic_slice_in_dim(w2d, ac * r, r, axis=0).astype(BF16)

    (gw_in,) = _all_gather([my_half(w_in[0])], "gather_w_in")
    w4 = gw_in.reshape(N_CHIP, D_MODEL, IN_COLS // N_CHIP)
    gather_rest = _AllGather([my_half(w_out[0]), my_half(w_mlp1[0]), my_half(w_mlp2[0])])

    (c_all,) = _all_gather([jnp.pad(c, ((0, SUBLANES - 1), (0, 0)))], "gather_c")
    a16, lgv, sgv = _prep(c_all[:, 0, :], c_ctx, ret_decay[0])
    b_shard = lax.dynamic_slice_in_dim(b_ada, chip * ada_n, ada_n, axis=1)
    (mod_parts,) = _all_gather([_mod_fwd(a16, w_ada[0], b_shard)], "gather_mod")
    mod_all = mod_parts[0::2].transpose(1, 0, 2).reshape(16, N_CHIP * ada_n)
    mod_me = lax.dynamic_slice_in_dim(mod_all, dev, 1, axis=0)
    sh1, sc1, g1, sh2, sc2, g2 = [mod_me[:, D_MODEL * k:D_MODEL * (k + 1)] for k in range(N_MOD)]
    csh1, csc1 = mod_all[8:9, 0:D_MODEL], mod_all[8:9, D_MODEL:2 * D_MODEL]

    cos2, sin2 = _rotary_tables(t_len)
    cos_c, sin_c = jnp.ones((l_len, DH), F32), jnp.zeros((l_len, DH), F32)
    n1g, n2g = norm1_g, norm2_g
    fg = final_g.reshape(1, D_MODEL)

    pad8 = lambda a: jnp.pad(a, ((0, SUBLANES - a.shape[0]), (0, 0)))
    small = jnp.concatenate([pad8(conv_w[0]), pad8(lru_ba[0]), pad8(lru_bx[0]), pad8(lru_lambda[0])], axis=0)
    (small_all,) = _all_gather([small], "gather_small_params")
    small_full = small_all[0::2].transpose(1, 0, 2).reshape(4 * SUBLANES, LRU_W)
    cw = small_full[0:4]
    cb = conv_b
    ba_f, ba_b = small_full[8:9], small_full[9:10]
    bx_f, bx_b = small_full[16:17], small_full[17:18]
    lam_f, lam_b = small_full[24:25], small_full[25:26]
    wa_f, wa_b = _block_diag(lru_wa[0, 0]), _block_diag(lru_wa[0, 1])
    wx_f, wx_b = _block_diag(lru_wx[0, 0]), _block_diag(lru_wx[0, 1])
    zero_h = jnp.zeros((1, LRU_W), F32)

    projc, hcb16 = _inproj_fwd(ctxt, n1g, csh1, csc1, w4, cos_c, sin_c, "inproj_fwd_ctx")
    s_f, s_b = _ctx_state_fwd(projc, lgv)
    xcc = _conv_fwd(projc, cw, cb, "conv_fwd_ctx")
    hcf = _lru_fwd(xcc, wa_f, wx_f, ba_f, bx_f, lam_f, zero_h, False, "lru_fwd_ctx_f")
    hcbk = _lru_fwd(xcc, wa_b, wx_b, ba_b, bx_b, lam_b, zero_h, True, "lru_fwd_ctx_b")
    lru_sf, lru_sb = hcf[l_len - 1:l_len], hcbk[0:1]

    proj, hb16 = _inproj_fwd(xt, n1g, sh1, sc1, w4, cos2, sin2, "inproj_fwd")
    (o_f, o_b, spf, spb), ((gw_out, gw_1, gw_2),) = _ret_fwd(proj, lgv, s_f, s_b, comms=(gather_rest,))
    wo = gw_out.reshape(D_MODEL, D_MODEL)
    w1 = gw_1.reshape(N_CHIP, D_MODEL, MLP_H // N_CHIP)
    w2 = gw_2.reshape(N_CHIP, MLP_H // N_CHIP, D_MODEL)
    xcl = _conv_fwd(proj, cw, cb, "conv_fwd")
    hf = _lru_fwd(xcl, wa_f, wx_f, ba_f, bx_f, lam_f, lru_sf, False, "lru_fwd_f")
    hbk = _lru_fwd(xcl, wa_b, wx_b, ba_b, bx_b, lam_b, lru_sb, True, "lru_fwd_b")
    x1, cat = _mix_fwd(o_f, o_b, proj, hf, hbk, wo, xt, g1)

    (dx1, h2b, ab, dub, dmb, dsc2, dsh2, dg2, dn2g, dfg, lossv) = _mlp(x1, n2g, sh2, sc2, g2, fg, w1, w2, tgt)
    gw_mlp1 = _tn(h2b, dub, N_CHIP, False, True, "grad_w_mlp1")
    gw_mlp2 = _tn(ab, dmb, N_CHIP, True, False, "grad_w_mlp2")

    jc_idx = jnp.concatenate([j_idx, c_idx])
    b_1 = gw_mlp1.reshape(N_DEV, D_MODEL // 2, MLP_H // N_CHIP)
    b_2 = gw_mlp2.reshape(N_DEV, MLP_H // N_DEV, D_MODEL)
    (do, dhs, dg, dgate, dyb, dg1), ((r_1, r_2),) = _mix_bwd(
        o_f, o_b, proj, hf, hbk, wo, cat, dx1, g1, comms=(_pair_exchange([b_1, b_2]),))
    gw_o = _tn(cat, dyb, 1, False, False, "grad_w_out")
    b_o = gw_o.reshape(N_DEV, D_MODEL // N_DEV, D_MODEL)
    p_1, pb_1 = _pair_add(b_1, r_1, c_idx, "rs_pair_add_w_mlp1")
    p_2, pb_2 = _pair_add(b_2, r_2, c_idx, "rs_pair_add_w_mlp2")

    (dq_f, dk_f, dv_f, ds_f, drd_f), ((q_1, q_2), (r_o,)) = _ret_bwd(
        proj, lgv, sgv, spf, do, False, "ret_bwd_f", comms=(_chip_exchange([pb_1, pb_2]), _pair_exchange([b_o])))
    p_o, pb_o = _pair_add(b_o, r_o, c_idx, "rs_pair_add_w_out")
    h_1 = _chip_add(p_1, q_1, jc_idx, "rs_chip_add_w_mlp1")
    h_2 = _chip_add(p_2, q_2, jc_idx, "rs_chip_add_w_mlp2")

    (dq_b, dk_b, dv_b, ds_b, drd_b), ((q_o,), (f_1, f_2)) = _ret_bwd(
        proj, lgv, sgv, spb, do, True, "ret_bwd_b", comms=(_chip_exchange([pb_o]), _pair_gather([h_1, h_2])))
    h_o = _chip_add(p_o, q_o, jc_idx, "rs_chip_add_w_out")

    (dxc_f, dpre_f, dba_f, dbx_f, dlam_f, dh0_f), ((f_o,),) = _lru_bwd(
        xcl, wa_f, wx_f, ba_f, bx_f, lam_f, hf, lru_sf, dhs, False, "lru_bwd_f", comms=(_pair_gather([h_o]),))
    (dxc_b, dpre_b, dba_b, dbx_b, dlam_b, dh0_b), _ = _lru_bwd(
        xcl, wa_b, wx_b, ba_b, bx_b, lam_b, hbk, lru_sb, dhs, True, "lru_bwd_b")
    dxr, dcw, dcb = _conv_bwd(dxc_f + dxc_b, proj, cw, "conv_bwd")
    grad_x, dpb, dn1g, dsh1, dsc1 = _inproj_bwd(
        xt, n1g, sh1, sc1, w4, cos2, sin2, [dq_f, dq_b, dk_f, dk_b, dv_f, dv_b, dg, dxr, dgate], dx1, "inproj_bwd")

    dkc, dvc, drd_c = _ctx_state_bwd(projc, lgv, sgv, ds_f, ds_b)
    zc = jnp.zeros((l_len, LRU_W), F32)
    dhc_f = lax.dynamic_update_slice(zc, dh0_f, (l_len - 1, 0))
    dhc_b = lax.dynamic_update_slice(zc, dh0_b, (0, 0))
    (dxcc_f, dprec_f, dbac_f, dbxc_f, dlamc_f, _), _ = _lru_bwd(
        xcc, wa_f, wx_f, ba_f, bx_f, lam_f, hcf, zero_h, dhc_f, False, "lru_bwd_ctx_f")
    (dxcc_b, dprec_b, dbac_b, dbxc_b, dlamc_b, _), _ = _lru_bwd(
        xcc, wa_b, wx_b, ba_b, bx_b, lam_b, hcbk, zero_h, dhc_b, True, "lru_bwd_ctx_b")
    dxrc, dcw_c, dcb_c = _conv_bwd(dxcc_f + dxcc_b, projc, cw, "conv_bwd_ctx")
    zr = jnp.zeros((l_len, RET_W), F32)
    _, dpbc, dn1g_c, dcsh1, dcsc1 = _inproj_bwd(
        ctxt, n1g, csh1, csc1, w4, cos_c, sin_c, [zr, zr, dkc, zr, dvc, zr, zr, dxrc, zr],
        jnp.zeros((l_len, D_MODEL), F32), "inproj_bwd_ctx")

    gw_i = _tn(hb16, dpb, N_CHIP, False, True, "grad_w_in", extra=(hcb16, dpbc))
    gwa_f = _tn(xcl, dpre_f, 2, False, True, "grad_lru_gates_f", extra=(xcc, dprec_f))
    gwa_b = _tn(xcl, dpre_b, 2, False, True, "grad_lru_gates_b", extra=(xcc, dprec_b))

    b_i = gw_i.reshape(N_DEV, D_MODEL // 2, IN_COLS // N_CHIP)
    (r_i,) = _run_comm(_pair_exchange([b_i]), "rs_pair_exchange_w_in")
    p_i, pb_i = _pair_add(b_i, r_i, c_idx, "rs_pair_add_w_in")
    (q_i,) = _run_comm(_chip_exchange([pb_i]), "rs_chip_exchange_w_in")
    (f_i,) = _run_comm(_pair_gather([_chip_add(p_i, q_i, jc_idx, "rs_chip_add_w_in")]), "rs_pair_gather_w_in")
    g_in, g_out, g_1, g_2 = _shard_of(f_i), _shard_of(f_o), _shard_of(f_1), _shard_of(f_2)
    big = {}
    for nm, w, g, m, v in (("w_in", w_in, g_in, m_w_in, v_w_in), ("w_out", w_out, g_out, m_w_out, v_w_out),
                           ("w_mlp1", w_mlp1, g_1, m_w_mlp1, v_w_mlp1), ("w_mlp2", w_mlp2, g_2, m_w_mlp2, v_w_mlp2)):
        d_, mn, vn = _adamw(w[0], g, m[0], v[0], "adamw_" + nm)
        big[nm] = (g[None], d_[None], mn[None], vn[None])

    part = _Slab()
    part.add("loss", lossv)
    part.add("dmod", jnp.concatenate([dsh1, dsc1, dg1, dsh2, dsc2, dg2], axis=1))
    part.add("dmodc", jnp.concatenate([dcsh1, dcsc1], axis=1))
    part.add("norm1_g", dn1g + dn1g_c)
    part.add("norm2_g", dn2g)
    part.add("final_g", dfg)
    part.add("ret_decay", jnp.concatenate([drd_f[:, 0, :] + drd_c[:, 0, :], drd_b[:, 0, :] + drd_c[:, 1, :]], axis=0))
    part.add("conv_w", dcw + dcw_c)
    part.add("conv_b", dcb + dcb_c)
    part.add("lru_wa", jnp.stack([_diag_blocks(gwa_f[0]), _diag_blocks(gwa_b[0])]))
    part.add("lru_wx", jnp.stack([_diag_blocks(gwa_f[1]), _diag_blocks(gwa_b[1])]))
    part.add("lru_ba", jnp.concatenate([dba_f + dbac_f, dba_b + dbac_b], axis=0))
    part.add("lru_bx", jnp.concatenate([dbx_f + dbxc_f, dbx_b + dbxc_b], axis=0))
    part.add("lru_lambda", jnp.concatenate([dlam_f + dlamc_f, dlam_b + dlamc_b], axis=0))
    (parts_all,) = _all_gather([part.build()], "gather_small_grads")
    tot = _sum_devices(parts_all, "sum_small_grads")

    loss = part.take(tot, "loss")[0, 0]
    dmod_all = parts_all[:, part.meta["dmod"][0]:part.meta["dmod"][0] + part.meta["dmod"][1]].reshape(N_DEV, -1)
    dmodc_tot = jnp.pad(part.take(tot, "dmodc").reshape(1, -1), ((0, 0), (0, (N_MOD - 2) * D_MODEL)))
    dmod_tot = part.take(tot, "dmod").reshape(1, -1)
    grad_b_ada = dmod_tot + dmodc_tot

    cols = lambda a: lax.dynamic_slice_in_dim(a, chip * ada_n, ada_n, axis=1)
    b128 = jnp.pad(jnp.concatenate([cols(dmod_all), jnp.pad(cols(dmodc_tot), ((0, SUBLANES - 1), (0, 0)))], axis=0),
                   ((0, LANES - 2 * SUBLANES), (0, 0)))
    g_ada = _ada_grad(jnp.pad(a16.T, ((0, 0), (0, LANES - 16))), b128)
    d_ada, m_ada, v_ada = _adamw(w_ada[0], g_ada, m_w_ada[0], v_w_ada[0], "adamw_w_ada")

    dmc8 = jnp.pad(cols(dmodc_tot), ((0, SUBLANES - 1), (0, 0)))
    (cparts,) = _all_gather([_cctx_partial(dmc8, w_ada[0])], "gather_cctx")
    g_cc, d_cc, m_cc, v_cc = _cctx_final(cparts, c_ctx, m_c_ctx, v_c_ctx)

    def shard_cols(a, width=LANES):
        return lax.dynamic_slice_in_dim(a, chip * width, width, axis=a.ndim - 1)

    grads = {
        "b_ada": grad_b_ada,
        "norm1_g": part.take(tot, "norm1_g"),
        "norm2_g": part.take(tot, "norm2_g"),
        "ret_decay": part.take(tot, "ret_decay", (SUBLANES, LANES)),
        "conv_w": shard_cols(part.take(tot, "conv_w")),
        "conv_b": part.take(tot, "conv_b"),
        "lru_wa": part.take(tot, "lru_wa"),
        "lru_ba": shard_cols(part.take(tot, "lru_ba")),
        "lru_wx": part.take(tot, "lru_wx"),
        "lru_bx": shard_cols(part.take(tot, "lru_bx")),
        "lru_lambda": shard_cols(part.take(tot, "lru_lambda")),
        "final_g": part.take(tot, "final_g"),
    }
    params = {
        "b_ada": (b_ada, m_b_ada, v_b_ada), "norm1_g": (norm1_g, m_norm1_g, v_norm1_g),
        "norm2_g": (norm2_g, m_norm2_g, v_norm2_g), "ret_decay": (ret_decay, m_ret_decay, v_ret_decay),
        "conv_w": (conv_w, m_conv_w, v_conv_w), "conv_b": (conv_b, m_conv_b, v_conv_b),
        "lru_wa": (lru_wa, m_lru_wa, v_lru_wa), "lru_ba": (lru_ba, m_lru_ba, v_lru_ba),
        "lru_wx": (lru_wx, m_lru_wx, v_lru_wx), "lru_bx": (lru_bx, m_lru_bx, v_lru_bx),
        "lru_lambda": (lru_lambda, m_lru_lambda, v_lru_lambda), "final_g": (final_g, m_final_g, v_final_g),
    }
    slabs = [_Slab() for _ in range(4)]
    for nm, gval in grads.items():
        vals = (gval,) + params[nm]
        for s, val in zip(slabs, vals):
            if nm == "ret_decay" and val.shape != (SUBLANES, LANES):
                val = _lane_rep(val.reshape(-1))
            s.add(nm, val)
    gs, ws, ms, vs = [s.build() for s in slabs]
    ds_, mns, vns = _adamw(ws, gs, ms, vs, "adamw_small")
    small_out = {}
    for nm in grads:
        shp = params[nm][0].shape
        if nm == "ret_decay":
            unp = lambda sl: slabs[0].take(sl, nm, (SUBLANES, LANES))[:, 0].reshape(shp)
        else:
            unp = lambda sl: slabs[0].take(sl, nm, shp)
        small_out[nm] = (unp(gs), unp(ds_), unp(mns), unp(vns))
    small_out["c_ctx"] = tuple(a.reshape(D_MODEL) for a in (g_cc, d_cc, m_cc, v_cc))
    small_out["w_ada"] = (g_ada[None], d_ada[None], m_ada[None], v_ada[None])
    small_out.update(big)

    order = ["c_ctx", "w_ada", "b_ada", "norm1_g", "norm2_g", "w_in", "ret_decay", "conv_w", "conv_b", "lru_wa", "lru_ba",
             "lru_wx", "lru_bx", "lru_lambda", "w_out", "w_mlp1", "w_mlp2", "final_g"]
    outs = [loss, grad_x[None]]
    for k in range(4):
        outs += [small_out[nm][k] for nm in order]
    return tuple(outs)
```

```python
import math

import jax
import jax.numpy as jnp
from jax import lax
from jax.experimental import pallas as pl
from jax.experimental.pallas import tpu as pltpu

F32 = jnp.float32
BF16 = jnp.bfloat16

D_MODEL = 1024
HEADS = 4
DH = 128
CHUNK = 128
RET_W = HEADS * DH
LRU_W = 512
LRU_BLOCKS = 8
LRU_BD = LRU_W // LRU_BLOCKS
LRU_C = 8.0
IN_COLS = 4 * RET_W + 2 * LRU_W
MLP_H = 4 * D_MODEL
N_MOD = 6
GRID_W = 64
ROPE_BASE = 10000.0
K_SCALE = DH ** -0.5
EPS = 1e-6
GELU_K = math.sqrt(2.0 / math.pi)
GELU_C = 0.044715

ADAM_LR = 0.001
ADAM_B1 = 0.9
ADAM_B2 = 0.999
ADAM_EPS = 1e-08
ADAM_WD = 0.01
ADAM_STEP = 10

N_DEV = 8
N_CHIP = 4
SUBLANES = 8
LANES = 128
VMEM_LIMIT_V7X = 56 * 1024 * 1024
MESH = pl.DeviceIdType.MESH
ANY = pl.BlockSpec(memory_space=pl.ANY)


def _pc(body, **kw):
    return pl.pallas_call(body, **kw)


def _params(*sem):
    return pltpu.CompilerParams(dimension_semantics=sem if sem else None, vmem_limit_bytes=VMEM_LIMIT_V7X)


def _tile(t):
    return 256 if t >= 256 else t


def _sds(shape, dtype=F32):
    return jax.ShapeDtypeStruct(tuple(shape), dtype)


def _full(shape):
    nd = len(shape)
    return pl.BlockSpec(tuple(shape), lambda *_: (0,) * nd)


def _sigmoid(x):
    return 1.0 / (1.0 + jnp.exp(-x))


def _log1p_pos(y):
    s = y * (1.0 - y * (0.5 - y * (1.0 / 3.0 - y * (0.25 - y * (0.2 - y / 6.0)))))
    return jnp.where(y < 0.03, s, jnp.log(1.0 + y))


def _softplus(z):
    return jnp.maximum(z, 0.0) + _log1p_pos(jnp.exp(-jnp.abs(z)))


def _neg_expm1(x):
    t = x * (1.0 + x * (0.5 + x * (1.0 / 6.0 + x * (1.0 / 24.0 + x * (1.0 / 120.0 + x * (1.0 / 720.0 + x / 5040.0))))))
    return -jnp.where(x > -0.25, t, jnp.exp(x) - 1.0)


def _rms(x):
    r = lax.rsqrt(jnp.mean(x * x, axis=-1, keepdims=True) + EPS)
    return x * r, r


def _dot(a, b):
    return jnp.dot(a, b, preferred_element_type=F32)


def _dot_nt(a, b):
    return lax.dot_general(a, b, (((1,), (1,)), ((), ())), preferred_element_type=F32)


def _dot_tn(a, b):
    return lax.dot_general(a, b, (((0,), (0,)), ((), ())), preferred_element_type=F32)


def _sum0(x):
    return jnp.sum(x, axis=0, keepdims=True)


def _norm_mod_bwd(x, g, sc, dh):
    xh, r = _rms(x)
    hn = xh * g
    dhn = dh * (1.0 + sc)
    dxh = dhn * g
    dx = r * (dxh - xh * jnp.mean(dxh * xh, axis=-1, keepdims=True))
    return dx, _sum0(dhn * xh), _sum0(dh), _sum0(dh * hn)


def _dev_index(p):
    return 4 * p[0] + 2 * p[1] + p[2]


def _mesh_pos():
    return lax.axis_index("x"), lax.axis_index("y"), lax.axis_index("c")


class _AllGather:
    def __init__(self, arrs):
        n = len(arrs)
        self.arrays = list(arrs)
        self.out_shapes = [_sds((N_DEV,) + a.shape, a.dtype) for a in arrs]
        self.scratch = ([pltpu.VMEM(a.shape, a.dtype) for a in arrs]
                        + [pltpu.SemaphoreType.DMA((7 * n,)), pltpu.SemaphoreType.DMA((7 * n,)),
                           pltpu.SemaphoreType.DMA((n,))])
        self.aliases = {}

    def _parts(self, ins, outs, scr):
        n = len(self.arrays)
        stage = scr[:n]
        send_sems, recv_sems, local_sems = scr[n:]
        x, y, c = _mesh_pos()
        me, sib = (x, y, c), (x, y, 1 - c)
        chips = [(1 - x, y), (x, 1 - y), (1 - x, 1 - y)]

        def copy(t, k, block, to, own=False):
            dst = outs[t].at[_dev_index(block)]
            return pltpu.make_async_remote_copy(
                src_ref=ins[t] if own else dst, dst_ref=dst,
                send_sem=send_sems.at[7 * t + k], recv_sem=recv_sems.at[7 * t + k],
                device_id=to, device_id_type=MESH)

        first = []
        for t in range(n):
            first.append(copy(t, 0, me, sib, own=True))
            for j, ch in enumerate(chips):
                first.append(copy(t, 1 + j, me, (*ch, c), own=True))
        stage_in = [pltpu.make_async_copy(ins[t], stage[t], local_sems.at[t]) for t in range(n)]
        mine = [pltpu.make_async_copy(stage[t], outs[t].at[_dev_index(me)], local_sems.at[t]) for t in range(n)]
        return n, c, me, sib, chips, copy, first, stage_in, mine

    def start(self, ins, outs, scr):
        n, _, _, _, _, _, first, stage_in, mine = self._parts(ins, outs, scr)
        for cp in stage_in:
            cp.start()
        for cp in first:
            cp.start()
        for t in range(n):
            stage_in[t].wait()
            mine[t].start()

    def finish(self, ins, outs, scr):
        n, c, me, sib, chips, copy, first, _, mine = self._parts(ins, outs, scr)
        passed = []
        for j, ch in enumerate(chips):
            for t in range(n):
                copy(t, 1 + j, (*ch, c), me).wait_recv()
                p = copy(t, 4 + j, (*ch, c), sib)
                p.start()
                passed.append(p)
        for t in range(n):
            copy(t, 0, sib, me).wait_recv()
            for j, ch in enumerate(chips):
                copy(t, 4 + j, (*ch, 1 - c), me).wait_recv()
        for cp in first + passed:
            cp.wait_send()
        for cp in mine:
            cp.wait()


class _Exchange:
    def __init__(self, arrays, out_shapes, plan, n_copies, aliases=None):
        self.arrays = list(arrays)
        self.out_shapes = list(out_shapes)
        self.plan = plan
        self.scratch = [pltpu.SemaphoreType.DMA((n_copies,)), pltpu.SemaphoreType.DMA((n_copies,))]
        self.aliases = aliases or {}

    def _copies(self, ins, outs, scr):
        send_sems, recv_sems = scr
        snd, rcv = [], []
        for i, (src, dst, peer, lands) in enumerate(self.plan(ins, outs, _mesh_pos())):
            kw = dict(send_sem=send_sems.at[i], recv_sem=recv_sems.at[i], device_id=peer, device_id_type=MESH)
            snd.append(pltpu.make_async_remote_copy(src_ref=src, dst_ref=dst, **kw))
            rcv.append(pltpu.make_async_remote_copy(src_ref=src, dst_ref=lands, **kw))
        return snd, rcv

    def start(self, ins, outs, scr):
        for cp in self._copies(ins, outs, scr)[0]:
            cp.start()

    def finish(self, ins, outs, scr):
        snd, rcv = self._copies(ins, outs, scr)
        for cp in rcv:
            cp.wait_recv()
        for cp in snd:
            cp.wait_send()


def _pair_exchange(grads):
    n = len(grads)

    def plan(ins, outs, pos):
        x, y, c = pos
        return [(ins[t].at[2 * j + (1 - c)], outs[t].at[j], (x, y, 1 - c), outs[t].at[j])
                for t in range(n) for j in range(N_CHIP)]

    return _Exchange(grads, [_sds((N_CHIP,) + g.shape[1:], g.dtype) for g in grads], plan, N_CHIP * n)


def _chip_exchange(parts):
    n = len(parts)

    def plan(ins, outs, pos):
        x, y, c = pos
        chips = [(1 - x, y), (x, 1 - y), (1 - x, 1 - y)]
        return [(ins[t].at[2 * ch[0] + ch[1]], outs[t].at[k], (*ch, c), outs[t].at[k])
                for t in range(n) for k, ch in enumerate(chips)]

    return _Exchange(parts, [_sds((3,) + p.shape[1:], p.dtype) for p in parts], plan, 3 * n)


def _pair_gather(bufs):
    n = len(bufs)

    def plan(ins, outs, pos):
        x, y, c = pos
        return [(ins[t].at[c], outs[t].at[c], (x, y, 1 - c), outs[t].at[1 - c]) for t in range(n)]

    return _Exchange(bufs, [_sds(b.shape, b.dtype) for b in bufs], plan, n, aliases={t: t for t in range(n)})


def _run_comm(comm, name):
    n_in, n_out = len(comm.arrays), len(comm.out_shapes)

    def body(*refs):
        ins, outs, scr = refs[:n_in], refs[n_in:n_in + n_out], refs[n_in + n_out:]
        comm.start(ins, outs, scr)
        comm.finish(ins, outs, scr)

    outs = _pc(body, name=name, out_shape=comm.out_shapes, in_specs=[ANY] * n_in, out_specs=[ANY] * n_out,
               input_output_aliases=dict(comm.aliases), scratch_shapes=comm.scratch,
               compiler_params=_params())(*comm.arrays)
    return list(outs)


def _all_gather(arrs, name):
    return _run_comm(_AllGather(arrs), name)


def _call(body, *, name, grid, in_specs, out_specs, out_shape, scratch_shapes, sem, args, comms=()):
    n_in, n_out, n_scr = len(in_specs), len(out_specs), len(scratch_shapes)
    c_in = [len(cm.arrays) for cm in comms]
    c_out = [len(cm.out_shapes) for cm in comms]
    c_scr = [len(cm.scratch) for cm in comms]
    aliases = {}
    for k, cm in enumerate(comms):
        for a, b in cm.aliases.items():
            aliases[n_in + sum(c_in[:k]) + a] = n_out + sum(c_out[:k]) + b

    def split(refs, counts):
        out, pos = [], 0
        for cnt in counts:
            out.append(refs[pos:pos + cnt])
            pos += cnt
        return out

    def wrapped(*refs):
        ins = refs[:n_in + sum(c_in)]
        outs = refs[len(ins):len(ins) + n_out + sum(c_out)]
        scr = refs[len(ins) + len(outs):]
        cins, couts, cscr = split(ins[n_in:], c_in), split(outs[n_out:], c_out), split(scr[n_scr:], c_scr)
        if comms:
            first = pl.program_id(0) == 0
            last = pl.program_id(0) == grid[0] - 1
            for k in range(1, len(grid)):
                first = jnp.logical_and(first, pl.program_id(k) == 0)
                last = jnp.logical_and(last, pl.program_id(k) == grid[k] - 1)

            @pl.when(first)
            def _():
                for k, cm in enumerate(comms):
                    cm.start(cins[k], couts[k], cscr[k])
        body(*ins[:n_in], *outs[:n_out], *scr[:n_scr])
        if comms:
            @pl.when(last)
            def _():
                for k, cm in enumerate(comms):
                    cm.finish(cins[k], couts[k], cscr[k])

    outs = _pc(wrapped, name=name, grid=grid,
               in_specs=list(in_specs) + [ANY] * sum(c_in), out_specs=list(out_specs) + [ANY] * sum(c_out),
               out_shape=list(out_shape) + [s for cm in comms for s in cm.out_shapes],
               scratch_shapes=list(scratch_shapes) + [s for cm in comms for s in cm.scratch],
               input_output_aliases=aliases, compiler_params=_params(*sem),
               )(*args, *[a for cm in comms for a in cm.arrays])
    outs = list(outs)
    return outs[:n_out], split(outs[n_out:], c_out)


def _row_block(r):
    for b in (512, 256, 128, 64, 32, 16, 8):
        if r % b == 0:
            return b
    return r


def _pair_add(g, recv, c_idx, name):
    _, r, cc = g.shape
    br = _row_block(r)

    def body(c_ref, g_ref, r_ref, p_ref, pb_ref):
        s = g_ref[...] + r_ref[...]
        p_ref[...] = s
        pb_ref[...] = s.astype(BF16)

    grid_spec = pltpu.PrefetchScalarGridSpec(
        num_scalar_prefetch=1, grid=(N_CHIP, r // br),
        in_specs=[pl.BlockSpec((1, br, cc), lambda j, i, c_ref: (2 * j + c_ref[0], i, 0)),
                  pl.BlockSpec((1, br, cc), lambda j, i, c_ref: (j, i, 0))],
        out_specs=[pl.BlockSpec((1, br, cc), lambda j, i, c_ref: (j, i, 0)),
                   pl.BlockSpec((1, br, cc), lambda j, i, c_ref: (j, i, 0))])
    return _pc(body, name=name, grid_spec=grid_spec,
               out_shape=[_sds((N_CHIP, r, cc)), _sds((N_CHIP, r, cc), BF16)],
               compiler_params=_params("arbitrary", "arbitrary"))(c_idx, g, recv)


def _chip_add(p, q, jc_idx, name):
    _, r, cc = p.shape
    br = _row_block(r)

    def body(jc_ref, p_ref, q_ref, o_ref):
        o_ref[0] = ((p_ref[0] + q_ref[0].astype(F32)) + q_ref[1].astype(F32)) + q_ref[2].astype(F32)

    grid_spec = pltpu.PrefetchScalarGridSpec(
        num_scalar_prefetch=1, grid=(r // br,),
        in_specs=[pl.BlockSpec((1, br, cc), lambda i, jc_ref: (jc_ref[0], i, 0)),
                  pl.BlockSpec((3, br, cc), lambda i, jc_ref: (0, i, 0))],
        out_specs=pl.BlockSpec((1, br, cc), lambda i, jc_ref: (jc_ref[1], i, 0)))
    return _pc(body, name=name, grid_spec=grid_spec, out_shape=_sds((2, r, cc)),
               compiler_params=_params("arbitrary"))(jc_idx, p, q)


def _shard_of(both):
    return both.reshape((2 * both.shape[1],) + both.shape[2:])


def _sum_devices(g, name):
    _, r, cc = g.shape

    def body(g_ref, o_ref):
        acc = g_ref[0]
        for d in range(1, N_DEV):
            acc = acc + g_ref[d]
        o_ref[...] = acc

    return _pc(body, name=name, out_shape=_sds((r, cc)), in_specs=[_full(g.shape)], out_specs=_full((r, cc)),
               compiler_params=_params())(g)


def _adamw(w, g, m, v, name, comms=()):
    r, cc = w.shape
    br = _row_block(r)
    if r * cc * 4 <= (1 << 20):
        br = r
    elif br * cc * 4 > (1 << 20) and br > 8:
        br = max(8, (1 << 20) // (cc * 4) // 8 * 8)
        while r % br:
            br -= 8
    c1 = 1.0 - ADAM_B1 ** ADAM_STEP
    c2 = 1.0 - ADAM_B2 ** ADAM_STEP

    def body(w_ref, g_ref, m_ref, v_ref, d_ref, mo_ref, vo_ref):
        gg = g_ref[...]
        mn = ADAM_B1 * m_ref[...] + (1.0 - ADAM_B1) * gg
        vn = ADAM_B2 * v_ref[...] + (1.0 - ADAM_B2) * (gg * gg)
        mh = mn / c1
        vh = vn / c2
        d_ref[...] = -ADAM_LR * (mh / (jnp.sqrt(vh) + ADAM_EPS) + ADAM_WD * w_ref[...])
        mo_ref[...] = mn
        vo_ref[...] = vn

    spec = pl.BlockSpec((br, cc), lambda i: (i, 0))
    outs, couts = _call(body, name=name, grid=(r // br,), in_specs=[spec] * 4, out_specs=[spec] * 3,
                        out_shape=[_sds((r, cc))] * 3, scratch_shapes=[], sem=("arbitrary",), args=(w, g, m, v),
                        comms=comms)
    return (outs, couts) if comms else outs


def _prep(c_all, c_ctx, ret_decay):
    def body(c_ref, cc_ref, rd_ref, a_ref, lg_ref, sg_ref):
        ca = c_ref[...]
        cc = cc_ref[...]
        a_ref[...] = jnp.zeros_like(a_ref)
        a_ref[0:8, :] = ca * _sigmoid(ca)
        a_ref[8:9, :] = cc * _sigmoid(cc)
        rd = rd_ref[...]
        lg_ref[...] = -_softplus(-rd)
        sg_ref[...] = _sigmoid(-rd)

    rd = jnp.broadcast_to(ret_decay.reshape(2, HEADS).T[:, :, None], (HEADS, 2, LANES))
    return _pc(body, name="prep",
               out_shape=[_sds((16, D_MODEL)), _sds((HEADS, 2, LANES)), _sds((HEADS, 2, LANES))],
               in_specs=[_full((8, D_MODEL)), _full((1, D_MODEL)), _full((HEADS, 2, LANES))],
               out_specs=[_full((16, D_MODEL)), _full((HEADS, 2, LANES)), _full((HEADS, 2, LANES))],
               compiler_params=_params())(c_all, c_ctx.reshape(1, D_MODEL), rd)


def _mod_fwd(a16, w_ada, b_shard):
    n = w_ada.shape[1]
    bn = 512

    def body(a_ref, w_ref, b_ref, o_ref):
        o_ref[...] = jnp.dot(a_ref[...], w_ref[...], preferred_element_type=F32,
                             precision=lax.Precision.HIGHEST) + b_ref[...]

    return _pc(body, name="mod_fwd", grid=(n // bn,),
               in_specs=[_full((16, D_MODEL)), pl.BlockSpec((D_MODEL, bn), lambda i: (0, i)),
                         pl.BlockSpec((1, bn), lambda i: (0, i))],
               out_specs=pl.BlockSpec((16, bn), lambda i: (0, i)), out_shape=_sds((16, n)),
               compiler_params=_params("arbitrary"))(a16, w_ada, b_shard)


def _ada_grad(at, b):
    n = b.shape[1]
    bn = 512

    def body(a_ref, b_ref, o_ref):
        o_ref[...] = jnp.dot(a_ref[...], b_ref[...], preferred_element_type=F32, precision=lax.Precision.HIGHEST)

    return _pc(body, name="ada_grad", grid=(n // bn,),
               in_specs=[_full((D_MODEL, LANES)), pl.BlockSpec((LANES, bn), lambda i: (0, i))],
               out_specs=pl.BlockSpec((D_MODEL, bn), lambda i: (0, i)), out_shape=_sds((D_MODEL, n)),
               compiler_params=_params("arbitrary"))(at, b)


def _cctx_partial(dmc8, w_ada):
    n = w_ada.shape[1]
    bn = 512

    def body(d_ref, w_ref, o_ref):
        @pl.when(pl.program_id(0) == 0)
        def _():
            o_ref[...] = jnp.zeros_like(o_ref)
        o_ref[...] += lax.dot_general(d_ref[...], w_ref[...], (((1,), (1,)), ((), ())),
                                      preferred_element_type=F32, precision=lax.Precision.HIGHEST)

    return _pc(body, name="cctx_partial", grid=(n // bn,),
               in_specs=[pl.BlockSpec((8, bn), lambda i: (0, i)), pl.BlockSpec((D_MODEL, bn), lambda i: (0, i))],
               out_specs=_full((8, D_MODEL)), out_shape=_sds((8, D_MODEL)),
               compiler_params=_params("arbitrary"))(dmc8, w_ada)


def _cctx_final(parts, c_ctx, m, v):
    c1 = 1.0 - ADAM_B1 ** ADAM_STEP
    c2 = 1.0 - ADAM_B2 ** ADAM_STEP

    def body(p_ref, c_ref, m_ref, v_ref, g_ref, d_ref, mo_ref, vo_ref):
        s = ((p_ref[0, 0:1, :] + p_ref[2, 0:1, :]) + p_ref[4, 0:1, :]) + p_ref[6, 0:1, :]
        z = c_ref[...]
        sg = _sigmoid(z)
        gg = s * (sg * (1.0 + z * (1.0 - sg)))
        g_ref[...] = gg
        mn = ADAM_B1 * m_ref[...] + (1.0 - ADAM_B1) * gg
        vn = ADAM_B2 * v_ref[...] + (1.0 - ADAM_B2) * (gg * gg)
        d_ref[...] = -ADAM_LR * ((mn / c1) / (jnp.sqrt(vn / c2) + ADAM_EPS) + ADAM_WD * z)
        mo_ref[...] = mn
        vo_ref[...] = vn

    row = _full((1, D_MODEL))
    return _pc(body, name="cctx_final", out_shape=[_sds((1, D_MODEL))] * 4,
               in_specs=[_full(parts.shape), row, row, row], out_specs=[row] * 4,
               compiler_params=_params())(parts, c_ctx.reshape(1, D_MODEL), m.reshape(1, D_MODEL), v.reshape(1, D_MODEL))


def _rotary_tables(t_len):
    rows = t_len // GRID_W
    row = jnp.repeat(jnp.arange(rows, dtype=F32), GRID_W)
    col = jnp.tile(jnp.arange(GRID_W, dtype=F32), rows)
    n_freq = DH // 4
    inv = ROPE_BASE ** (-jnp.arange(n_freq, dtype=F32) / n_freq)
    ang = jnp.concatenate([row[:, None] * inv, col[:, None] * inv], axis=-1)
    cos, sin = jnp.cos(ang), jnp.sin(ang)
    return jnp.concatenate([cos, cos], axis=-1), jnp.concatenate([-sin, sin], axis=-1)


def _inproj_fwd(x, gn, sh, sc, w4, cos2, sin2, name):
    t = x.shape[0]
    tm = _tile(t)
    nc = IN_COLS // N_CHIP

    def body(x_ref, gn_ref, sh_ref, sc_ref, w_ref, c_ref, s_ref, p_ref, hb_ref):
        xh, _ = _rms(x_ref[...])
        h = xh * gn_ref[...] * (1.0 + sc_ref[...]) + sh_ref[...]
        hb = h.astype(BF16)
        hb_ref[...] = hb
        for j in range(N_CHIP):
            p_ref[:, nc * j:nc * (j + 1)] = _dot(hb, w_ref[j])
        cc = c_ref[...]
        ss = s_ref[...]
        for hh in range(2 * HEADS):
            blk = p_ref[:, DH * hh:DH * (hh + 1)]
            rot = blk * cc + pltpu.roll(blk, DH // 2, 1) * ss
            if hh >= HEADS:
                rot = rot * K_SCALE
            p_ref[:, DH * hh:DH * (hh + 1)] = rot

    row = _full((1, D_MODEL))
    return _pc(body, name=name, grid=(t // tm,),
               in_specs=[pl.BlockSpec((tm, D_MODEL), lambda i: (i, 0)), row, row, row, _full(w4.shape),
                         pl.BlockSpec((tm, DH), lambda i: (i, 0)), pl.BlockSpec((tm, DH), lambda i: (i, 0))],
               out_specs=[pl.BlockSpec((tm, IN_COLS), lambda i: (i, 0)), pl.BlockSpec((tm, D_MODEL), lambda i: (i, 0))],
               out_shape=[_sds((t, IN_COLS)), _sds((t, D_MODEL), BF16)],
               compiler_params=_params("arbitrary"))(x, gn, sh, sc, w4, cos2, sin2)


def _inproj_bwd(x, gn, sh, sc, w4, cos2, sin2, pieces, dres, name):
    t = x.shape[0]
    tm = _tile(t)
    nc = IN_COLS // N_CHIP

    def body(x_ref, gn_ref, sh_ref, sc_ref, w_ref, c_ref, s_ref, dqf, dqb, dkf, dkb, dvf, dvb, dg, dxr, dgt, dres_ref,
             dx_ref, dpb_ref, dgn_ref, dsh_ref, dsc_ref):
        cc = c_ref[...]
        ss = s_ref[...]
        dq = dqf[...] + dqb[...]
        dk = dkf[...] + dkb[...]
        for hh in range(HEADS):
            sl = slice(DH * hh, DH * (hh + 1))
            b = dq[:, sl]
            dpb_ref[:, sl] = (b * cc + pltpu.roll(b * ss, DH // 2, 1)).astype(BF16)
            b = dk[:, sl]
            dpb_ref[:, RET_W + DH * hh:RET_W + DH * (hh + 1)] = (
                (b * cc + pltpu.roll(b * ss, DH // 2, 1)) * K_SCALE).astype(BF16)
        dpb_ref[:, 2 * RET_W:3 * RET_W] = (dvf[...] + dvb[...]).astype(BF16)
        dpb_ref[:, 3 * RET_W:4 * RET_W] = dg[...].astype(BF16)
        dpb_ref[:, 4 * RET_W:4 * RET_W + LRU_W] = dxr[...].astype(BF16)
        dpb_ref[:, 4 * RET_W + LRU_W:IN_COLS] = dgt[...].astype(BF16)
        dh = _dot_nt(dpb_ref[:, 0:nc], w_ref[0])
        for j in range(1, N_CHIP):
            dh = dh + _dot_nt(dpb_ref[:, nc * j:nc * (j + 1)], w_ref[j])
        dx, dgn_t, dsh_t, dsc_t = _norm_mod_bwd(x_ref[...], gn_ref[...], sc_ref[...], dh)
        dx_ref[...] = dres_ref[...] + dx

        @pl.when(pl.program_id(0) == 0)
        def _():
            dgn_ref[...] = jnp.zeros_like(dgn_ref)
            dsh_ref[...] = jnp.zeros_like(dsh_ref)
            dsc_ref[...] = jnp.zeros_like(dsc_ref)
        dgn_ref[...] += dgn_t
        dsh_ref[...] += dsh_t
        dsc_ref[...] += dsc_t

    row = _full((1, D_MODEL))
    pc = pl.BlockSpec((tm, RET_W), lambda i: (i, 0))
    big = pl.BlockSpec((tm, D_MODEL), lambda i: (i, 0))
    return _pc(body, name=name, grid=(t // tm,),
               in_specs=[big, row, row, row, _full(w4.shape),
                         pl.BlockSpec((tm, DH), lambda i: (i, 0)), pl.BlockSpec((tm, DH), lambda i: (i, 0))]
               + [pc] * 9 + [big],
               out_specs=[big, pl.BlockSpec((tm, IN_COLS), lambda i: (i, 0)), row, row, row],
               out_shape=[_sds((t, D_MODEL)), _sds((t, IN_COLS), BF16), _sds((1, D_MODEL)), _sds((1, D_MODEL)),
                          _sds((1, D_MODEL))],
               compiler_params=_params("arbitrary"))(x, gn, sh, sc, w4, cos2, sin2, *pieces, dres)


XR_BLOCK = (4 * RET_W) // LRU_W


def _halo_specs(t, tm, col):
    n8 = tm // SUBLANES
    last8 = t // SUBLANES - 1
    prev = pl.BlockSpec((SUBLANES, LRU_W), lambda i: (jnp.maximum(i * n8 - 1, 0), col))
    main = pl.BlockSpec((tm, LRU_W), lambda i: (i, col))
    nxt = pl.BlockSpec((SUBLANES, LRU_W), lambda i: (jnp.minimum((i + 1) * n8, last8), col))
    return prev, main, nxt


def _with_halo(prev_ref, main_ref, next_ref, i, nt):
    prev = jnp.where(i > 0, prev_ref[...], 0.0)
    nxt = jnp.where(i < nt - 1, next_ref[...], 0.0)
    return jnp.concatenate([prev, main_ref[...], nxt], axis=0)


def _conv_fwd(proj, cw, cb, name):
    t = proj.shape[0]
    tm = _tile(t)
    nt = t // tm
    n = tm + 2 * SUBLANES
    mid = slice(SUBLANES, SUBLANES + tm)

    def body(p_ref, m_ref, n_ref, w_ref, b_ref, o_ref):
        xp = _with_halo(p_ref, m_ref, n_ref, pl.program_id(0), nt)
        acc = b_ref[...] + pltpu.roll(xp, 1, 0)[mid] * w_ref[0:1, :]
        acc = acc + xp[mid] * w_ref[1:2, :]
        acc = acc + pltpu.roll(xp, n - 1, 0)[mid] * w_ref[2:3, :]
        acc = acc + pltpu.roll(xp, n - 2, 0)[mid] * w_ref[3:4, :]
        o_ref[...] = acc

    return _pc(body, name=name, grid=(nt,),
               in_specs=[*_halo_specs(t, tm, XR_BLOCK), _full((4, LRU_W)), _full((1, LRU_W))],
               out_specs=pl.BlockSpec((tm, LRU_W), lambda i: (i, 0)), out_shape=_sds((t, LRU_W)),
               compiler_params=_params("arbitrary"))(proj, proj, proj, cw, cb)


def _conv_bwd(dxc, proj, cw, name):
    t = proj.shape[0]
    tm = _tile(t)
    nt = t // tm
    n = tm + 2 * SUBLANES
    mid = slice(SUBLANES, SUBLANES + tm)

    def body(dp_ref, dm_ref, dn_ref, xp_ref, xm_ref, xn_ref, w_ref, dx_ref, dw_ref, db_ref):
        i = pl.program_id(0)
        dp = _with_halo(dp_ref, dm_ref, dn_ref, i, nt)
        xp = _with_halo(xp_ref, xm_ref, xn_ref, i, nt)
        dx = pltpu.roll(dp, n - 1, 0)[mid] * w_ref[0:1, :]
        dx = dx + dp[mid] * w_ref[1:2, :]
        dx = dx + pltpu.roll(dp, 1, 0)[mid] * w_ref[2:3, :]
        dx = dx + pltpu.roll(dp, 2, 0)[mid] * w_ref[3:4, :]
        dx_ref[...] = dx
        d = dm_ref[...]

        @pl.when(i == 0)
        def _():
            dw_ref[...] = jnp.zeros_like(dw_ref)
            db_ref[...] = jnp.zeros_like(db_ref)
        dw_ref[0:1, :] += _sum0(d * pltpu.roll(xp, 1, 0)[mid])
        dw_ref[1:2, :] += _sum0(d * xp[mid])
        dw_ref[2:3, :] += _sum0(d * pltpu.roll(xp, n - 1, 0)[mid])
        dw_ref[3:4, :] += _sum0(d * pltpu.roll(xp, n - 2, 0)[mid])
        db_ref[...] += _sum0(d)

    return _pc(body, name=name, grid=(nt,),
               in_specs=[*_halo_specs(t, tm, 0), *_halo_specs(t, tm, XR_BLOCK), _full((4, LRU_W))],
               out_specs=[pl.BlockSpec((tm, LRU_W), lambda i: (i, 0)), _full((4, LRU_W)), _full((1, LRU_W))],
               out_shape=[_sds((t, LRU_W)), _sds((4, LRU_W)), _sds((1, LRU_W))],
               compiler_params=_params("arbitrary"))(dxc, dxc, dxc, proj, proj, proj, cw)


def _local_scan(a, b, reverse):
    n = a.shape[0]
    row = lax.broadcasted_iota(jnp.int32, a.shape, 0) & (SUBLANES - 1)
    for s in (1, 2, 4):
        if reverse:
            a_s, b_s, ok = pltpu.roll(a, n - s, 0), pltpu.roll(b, n - s, 0), row < SUBLANES - s
        else:
            a_s, b_s, ok = pltpu.roll(a, s, 0), pltpu.roll(b, s, 0), row >= s
        b = a * jnp.where(ok, b_s, 0.0) + b
        a = a * jnp.where(ok, a_s, 1.0)
    return a, b


def _carry_scan(a_s, b_s, out_ref, carry, reverse):
    ng = a_s.shape[0] // SUBLANES
    shape = carry.shape

    def step(g, cr):
        gg = (ng - 1 - g) if reverse else g
        off = pl.multiple_of(gg * SUBLANES, SUBLANES)
        h = a_s[pl.ds(off, SUBLANES), :] * cr + b_s[pl.ds(off, SUBLANES), :]
        out_ref[pl.ds(off, SUBLANES), :] = h
        edge = h[0:1, :] if reverse else h[SUBLANES - 1:SUBLANES, :]
        return jnp.broadcast_to(edge, shape)

    return lax.fori_loop(0, ng, step, carry)


def _lru_gates(xc, wa_ref, wx_ref, ba, bx, lam):
    xb = xc.astype(BF16)
    r = _sigmoid(_dot(xb, wa_ref[...]) + ba)
    ig = _sigmoid(_dot(xb, wx_ref[...]) + bx)
    sp = _softplus(-lam)
    la = -LRU_C * r * sp
    a = jnp.exp(la)
    mult = jnp.sqrt(_neg_expm1(2.0 * la))
    return r, ig, sp, a, mult


def _lru_fwd(xc, wa, wx, ba, bx, lam, h0, reverse, name, comms=()):
    t = xc.shape[0]
    tm = _tile(t)
    nt = t // tm
    tidx = (lambda i: (nt - 1 - i, 0)) if reverse else (lambda i: (i, 0))

    def body(x_ref, wa_ref, wx_ref, ba_ref, bx_ref, lam_ref, h0_ref, h_ref, a_s, b_s, c_s):
        @pl.when(pl.program_id(0) == 0)
        def _():
            c_s[...] = jnp.broadcast_to(h0_ref[...], c_s.shape)
        xv = x_ref[...]
        _, ig, _, a, mult = _lru_gates(xv, wa_ref, wx_ref, ba_ref[...], bx_ref[...], lam_ref[...])
        al, bl = _local_scan(a, mult * (ig * xv), reverse)
        a_s[...] = al
        b_s[...] = bl
        c_s[...] = _carry_scan(a_s, b_s, h_ref, c_s[...], reverse)

    vec = _full((1, LRU_W))
    mat = _full((LRU_W, LRU_W))
    (h,), couts = _call(body, name=name, grid=(nt,),
                        in_specs=[pl.BlockSpec((tm, LRU_W), tidx), mat, mat, vec, vec, vec, vec],
                        out_specs=[pl.BlockSpec((tm, LRU_W), tidx)], out_shape=[_sds((t, LRU_W))],
                        scratch_shapes=[pltpu.VMEM((tm, LRU_W), F32), pltpu.VMEM((tm, LRU_W), F32),
                                        pltpu.VMEM((SUBLANES, LRU_W), F32)],
                        sem=("arbitrary",), args=(xc, wa, wx, ba, bx, lam, h0), comms=comms)
    return (h, couts) if comms else h


def _lru_bwd(xc, wa, wx, ba, bx, lam, h, h0, dh, reverse, name, comms=()):
    t = xc.shape[0]
    tm = _tile(t)
    nt = t // tm
    n8 = tm // SUBLANES
    last8 = t // SUBLANES - 1
    tidx = (lambda i: (i, 0)) if reverse else (lambda i: (nt - 1 - i, 0))
    if reverse:
        halo = pl.BlockSpec((SUBLANES, LRU_W), lambda i: (jnp.minimum((i + 1) * n8, last8), 0))
    else:
        halo = pl.BlockSpec((SUBLANES, LRU_W), lambda i: (jnp.maximum((nt - 1 - i) * n8 - 1, 0), 0))

    def body(x_ref, wa_ref, wx_ref, ba_ref, bx_ref, lam_ref, h_ref, halo_ref, h0_ref, dh_ref,
             dx_ref, dpre_ref, dba_ref, dbx_ref, dlam_ref, dh0_ref, a_s, b_s, l_s, c_s, e_s):
        i = pl.program_id(0)

        @pl.when(i == 0)
        def _():
            c_s[...] = jnp.zeros_like(c_s)
            e_s[...] = jnp.zeros_like(e_s)
            dba_ref[...] = jnp.zeros_like(dba_ref)
            dbx_ref[...] = jnp.zeros_like(dbx_ref)
            dlam_ref[...] = jnp.zeros_like(dlam_ref)
        xv = x_ref[...]
        lam = lam_ref[...]
        r, ig, sp, a, mult = _lru_gates(xv, wa_ref, wx_ref, ba_ref[...], bx_ref[...], lam)
        hv = h_ref[...]
        rowi = lax.broadcasted_iota(jnp.int32, (tm, LRU_W), 0)
        edge_a = jnp.broadcast_to(e_s[0:1, :], (tm, LRU_W))
        h0b = jnp.broadcast_to(h0_ref[...], (tm, LRU_W))
        if reverse:
            a_sh = jnp.where(rowi == 0, edge_a, pltpu.roll(a, 1, 0))
            hin_edge = jnp.where(i == nt - 1, h0b, jnp.broadcast_to(halo_ref[0:1, :], (tm, LRU_W)))
            h_in = jnp.where(rowi == tm - 1, hin_edge, pltpu.roll(hv, tm - 1, 0))
        else:
            a_sh = jnp.where(rowi == tm - 1, edge_a, pltpu.roll(a, tm - 1, 0))
            hin_edge = jnp.where(i == nt - 1, h0b, jnp.broadcast_to(halo_ref[SUBLANES - 1:SUBLANES, :], (tm, LRU_W)))
            h_in = jnp.where(rowi == 0, hin_edge, pltpu.roll(hv, 1, 0))
        al, bl = _local_scan(a_sh, dh_ref[...], not reverse)
        a_s[...] = al
        b_s[...] = bl
        c_s[...] = _carry_scan(a_s, b_s, l_s, c_s[...], not reverse)
        e_s[...] = jnp.broadcast_to(a[tm - 1:tm, :] if reverse else a[0:1, :], e_s.shape)
        lmb = l_s[...]
        da = lmb * h_in
        ixc = ig * xv
        dmult = lmb * ixc
        dixc = lmb * mult
        dla = da * a - dmult * (a * a) / mult
        dpr = dla * (-LRU_C * sp) * r * (1.0 - r)
        dpi = dixc * xv * ig * (1.0 - ig)
        dprb = dpr.astype(BF16)
        dpib = dpi.astype(BF16)
        dpre_ref[:, 0:LRU_W] = dprb
        dpre_ref[:, LRU_W:2 * LRU_W] = dpib
        dx_ref[...] = dixc * ig + _dot_nt(dprb, wa_ref[...]) + _dot_nt(dpib, wx_ref[...])
        dba_ref[...] += _sum0(dpr)
        dbx_ref[...] += _sum0(dpi)
        dlam_ref[...] += _sum0(dla * (-LRU_C * r)) * (-_sigmoid(-lam))

        @pl.when(i == nt - 1)
        def _():
            al0 = a * lmb
            dh0_ref[...] = al0[tm - 1:tm, :] if reverse else al0[0:1, :]

    vec = _full((1, LRU_W))
    mat = _full((LRU_W, LRU_W))
    tile = pl.BlockSpec((tm, LRU_W), tidx)
    return _call(body, name=name, grid=(nt,),
                 in_specs=[tile, mat, mat, vec, vec, vec, tile, halo, vec, tile],
                 out_specs=[tile, pl.BlockSpec((tm, 2 * LRU_W), tidx), vec, vec, vec, vec],
                 out_shape=[_sds((t, LRU_W)), _sds((t, 2 * LRU_W), BF16), _sds((1, LRU_W)), _sds((1, LRU_W)),
                            _sds((1, LRU_W)), _sds((1, LRU_W))],
                 scratch_shapes=[pltpu.VMEM((tm, LRU_W), F32), pltpu.VMEM((tm, LRU_W), F32),
                                 pltpu.VMEM((tm, LRU_W), F32), pltpu.VMEM((SUBLANES, LRU_W), F32),
                                 pltpu.VMEM((SUBLANES, LRU_W), F32)],
                 sem=("arbitrary",), args=(xc, wa, wx, ba, bx, lam, h, h, h0, dh), comms=comms)


def _decay_tables(lg, reverse):
    ci = lax.broadcasted_iota(jnp.int32, (CHUNK, CHUNK), 0).astype(F32)
    mi = lax.broadcasted_iota(jnp.int32, (CHUNK, CHUNK), 1).astype(F32)
    if reverse:
        rel, pq, ps = mi - ci, CHUNK - ci, ci
    else:
        rel, pq, ps = ci - mi, ci + 1.0, CHUNK - 1.0 - ci
    relc = jnp.maximum(rel, 0.0)
    dm = jnp.where(rel >= 0, jnp.exp(lg * relc), 0.0)
    return relc, dm, jnp.exp(lg * pq), jnp.exp(lg * ps), jnp.exp(lg * float(CHUNK)), pq, ps


def _ret_fwd(proj, lgv, s0f, s0b, comms=()):
    t = proj.shape[0]
    n = t // CHUNK

    def one(q, k, v, lg, s_s, hh, o_ref, sp_ref, reverse):
        _, dm, wq, ws, g, _, _ = _decay_tables(lg, reverse)
        vb = v.astype(BF16)
        p = _dot_nt(q.astype(BF16), k.astype(BF16)) * dm
        s = s_s[hh]
        sp_ref[hh, 0] = s
        o_ref[:, DH * hh:DH * (hh + 1)] = _dot(p.astype(BF16), vb) + _dot((q * wq).astype(BF16), s.astype(BF16))
        s_s[hh] = g * s + _dot_tn((k * ws).astype(BF16), vb)

    def body(qf, kf, vf, qb, kb, vb, lg_ref, s0f_ref, s0b_ref, of_ref, ob_ref, spf_ref, spb_ref, sf_s, sb_s):
        @pl.when(pl.program_id(0) == 0)
        def _():
            sf_s[...] = s0f_ref[...]
            sb_s[...] = s0b_ref[...]
        for hh in range(HEADS):
            sl = slice(DH * hh, DH * (hh + 1))
            one(qf[:, sl], kf[:, sl], vf[:, sl], lg_ref[hh, 0:1, :], sf_s, hh, of_ref, spf_ref, False)
            one(qb[:, sl], kb[:, sl], vb[:, sl], lg_ref[hh, 1:2, :], sb_s, hh, ob_ref, spb_ref, True)

    blk = (CHUNK, RET_W)
    fw = [pl.BlockSpec(blk, lambda i, o=o: (i, o)) for o in range(3)]
    bw = [pl.BlockSpec(blk, lambda i, o=o: (n - 1 - i, o)) for o in range(3)]
    st = _full((HEADS, DH, DH))
    return _call(body, name="ret_fwd", grid=(n,),
                 in_specs=fw + bw + [_full((HEADS, 2, LANES)), st, st],
                 out_specs=[pl.BlockSpec(blk, lambda i: (i, 0)), pl.BlockSpec(blk, lambda i: (n - 1 - i, 0)),
                            pl.BlockSpec((HEADS, 1, DH, DH), lambda i: (0, i, 0, 0)),
                            pl.BlockSpec((HEADS, 1, DH, DH), lambda i: (0, n - 1 - i, 0, 0))],
                 out_shape=[_sds((t, RET_W)), _sds((t, RET_W)), _sds((HEADS, n, DH, DH)), _sds((HEADS, n, DH, DH))],
                 scratch_shapes=[pltpu.VMEM((HEADS, DH, DH), F32), pltpu.VMEM((HEADS, DH, DH), F32)],
                 sem=("arbitrary",), args=(proj, proj, proj, proj, proj, proj, lgv, s0f, s0b), comms=comms)


def _ret_bwd(proj, lgv, sgv, sprev, do, reverse, name, comms=()):
    t = proj.shape[0]
    n = t // CHUNK
    d = 1 if reverse else 0
    cidx = (lambda i: i) if reverse else (lambda i: n - 1 - i)

    def body(q_ref, k_ref, v_ref, lg_ref, sg_ref, s_ref, do_ref, dq_ref, dk_ref, dv_ref, ds0_ref, drd_ref, ds_s, acc_s):
        i = pl.program_id(0)

        @pl.when(i == 0)
        def _():
            ds_s[...] = jnp.zeros_like(ds_s)
            acc_s[...] = jnp.zeros_like(acc_s)
        for hh in range(HEADS):
            sl = slice(DH * hh, DH * (hh + 1))
            relc, dm, wq, ws, g, pq, ps = _decay_tables(lg_ref[hh, d:d + 1, :], reverse)
            q, k = q_ref[:, sl], k_ref[:, sl]
            qb, kb, vb = q.astype(BF16), k.astype(BF16), v_ref[:, sl].astype(BF16)
            p = _dot_nt(qb, kb) * dm
            s = s_ref[hh, 0]
            dob = do_ref[:, sl].astype(BF16)
            dsn = ds_s[hh]
            dsb = dsn.astype(BF16)
            dv_ref[:, sl] = _dot_tn(p.astype(BF16), dob) + _dot((k * ws).astype(BF16), dsb)
            dp = _dot_nt(dob, vb)
            dab = (dp * dm).astype(BF16)
            xq = _dot_nt(dob, s.astype(BF16))
            yk = _dot_nt(vb, dsb)
            dq_ref[:, sl] = _dot(dab, kb) + xq * wq
            dk_ref[:, sl] = _dot_tn(dab, qb) + yk * ws
            ds_s[hh] = g * dsn + _dot_tn((q * wq).astype(BF16), dob)
            part = (_sum0(dp * p * relc) + _sum0(xq * q * wq * pq) + _sum0(yk * k * ws * ps)
                    + _sum0(dsn * s) * g * float(CHUNK))
            acc_s[hh] += jnp.broadcast_to(part, (SUBLANES, LANES))

        @pl.when(i == n - 1)
        def _():
            ds0_ref[...] = ds_s[...]
            for hh in range(HEADS):
                tot = jnp.sum(acc_s[hh, 0:1, :], axis=1, keepdims=True)
                drd_ref[hh] = jnp.broadcast_to(tot, (SUBLANES, LANES)) * sg_ref[hh, d:d + 1, :]

    blk = (CHUNK, RET_W)
    qkv = [pl.BlockSpec(blk, lambda i, o=o: (cidx(i), o)) for o in range(3)]
    hc = pl.BlockSpec(blk, lambda i: (cidx(i), 0))
    lane = _full((HEADS, 2, LANES))
    return _call(body, name=name, grid=(n,),
                 in_specs=qkv + [lane, lane, pl.BlockSpec((HEADS, 1, DH, DH), lambda i: (0, cidx(i), 0, 0)), hc],
                 out_specs=[hc, hc, hc, _full((HEADS, DH, DH)), _full((HEADS, SUBLANES, LANES))],
                 out_shape=[_sds((t, RET_W))] * 3 + [_sds((HEADS, DH, DH)), _sds((HEADS, SUBLANES, LANES))],
                 scratch_shapes=[pltpu.VMEM((HEADS, DH, DH), F32), pltpu.VMEM((HEADS, SUBLANES, LANES), F32)],
                 sem=("arbitrary",), args=(proj, proj, proj, lgv, sgv, sprev, do), comms=comms)


def _ctx_weights(lg, l_len, reverse):
    pos = lax.broadcasted_iota(jnp.int32, (l_len, DH), 0).astype(F32)
    steps = pos if reverse else (l_len - 1.0 - pos)
    return jnp.exp(lg * steps), steps


def _ctx_state_fwd(projc, lgv):
    l_len = projc.shape[0]

    def body(k_ref, v_ref, lg_ref, sf_ref, sb_ref):
        k = k_ref[...]
        vb = v_ref[...].astype(BF16)
        for d, o_ref in ((0, sf_ref), (1, sb_ref)):
            w, _ = _ctx_weights(lg_ref[0, d:d + 1, :], l_len, d == 1)
            o_ref[0] = _dot_tn((k * w).astype(BF16), vb)

    st = pl.BlockSpec((1, DH, DH), lambda h: (h, 0, 0))
    return _pc(body, name="ctx_state_fwd", grid=(HEADS,),
               in_specs=[pl.BlockSpec((l_len, DH), lambda h: (0, HEADS + h)),
                         pl.BlockSpec((l_len, DH), lambda h: (0, 2 * HEADS + h)),
                         pl.BlockSpec((1, 2, LANES), lambda h: (h, 0, 0))],
               out_specs=[st, st], out_shape=[_sds((HEADS, DH, DH))] * 2,
               compiler_params=_params("arbitrary"))(projc, projc, lgv)


def _ctx_state_bwd(projc, lgv, sgv, dsf, dsb):
    l_len = projc.shape[0]

    def body(k_ref, v_ref, lg_ref, sg_ref, dsf_ref, dsb_ref, dk_ref, dv_ref, drd_ref):
        k = k_ref[...]
        vb = v_ref[...].astype(BF16)
        dk = jnp.zeros((l_len, DH), F32)
        dv = jnp.zeros((l_len, DH), F32)
        rows = []
        for d, ds_ref in ((0, dsf_ref), (1, dsb_ref)):
            w, steps = _ctx_weights(lg_ref[0, d:d + 1, :], l_len, d == 1)
            dsb16 = ds_ref[0].astype(BF16)
            dkw = _dot_nt(vb, dsb16)
            dk = dk + dkw * w
            dv = dv + _dot((k * w).astype(BF16), dsb16)
            tot = jnp.sum(_sum0(dkw * k * w * steps), axis=1, keepdims=True)
            rows.append(jnp.broadcast_to(tot, (1, LANES)) * sg_ref[0, d:d + 1, :])
        dk_ref[...] = dk
        dv_ref[...] = dv
        rid = lax.broadcasted_iota(jnp.int32, (SUBLANES, LANES), 0)
        drd_ref[0] = jnp.where(rid == 0, rows[0], jnp.where(rid == 1, rows[1], 0.0))

    st = pl.BlockSpec((1, DH, DH), lambda h: (h, 0, 0))
    lane = pl.BlockSpec((1, 2, LANES), lambda h: (h, 0, 0))
    hc = pl.BlockSpec((l_len, DH), lambda h: (0, h))
    return _pc(body, name="ctx_state_bwd", grid=(HEADS,),
               in_specs=[pl.BlockSpec((l_len, DH), lambda h: (0, HEADS + h)),
                         pl.BlockSpec((l_len, DH), lambda h: (0, 2 * HEADS + h)), lane, lane, st, st],
               out_specs=[hc, hc, pl.BlockSpec((1, SUBLANES, LANES), lambda h: (h, 0, 0))],
               out_shape=[_sds((l_len, RET_W)), _sds((l_len, RET_W)), _sds((HEADS, SUBLANES, LANES))],
               compiler_params=_params("arbitrary"))(projc, projc, lgv, sgv, dsf, dsb)


G_BLOCK = (3 * RET_W) // RET_W
GATE_BLOCK = (4 * RET_W + LRU_W) // LRU_W


def _head_norm(y):
    yc = y - jnp.mean(y, axis=-1, keepdims=True)
    rs = lax.rsqrt(jnp.mean(yc * yc, axis=-1, keepdims=True) + EPS)
    return yc * rs, rs


def _gelu_parts(z):
    th = jnp.tanh(GELU_K * (z + GELU_C * z * z * z))
    return 0.5 * z * (1.0 + th), th


def _mix_fwd(o_f, o_b, proj, hf, hb, w_out, x, g1):
    t = x.shape[0]
    tm = _tile(t)

    def body(of_ref, ob_ref, g_ref, gt_ref, hf_ref, hb_ref, w_ref, x_ref, g1_ref, x1_ref, cat_ref):
        o = of_ref[...] + ob_ref[...]
        g = g_ref[...]
        for hh in range(HEADS):
            sl = slice(DH * hh, DH * (hh + 1))
            nrm, _ = _head_norm(o[:, sl])
            gh = g[:, sl]
            cat_ref[:, sl] = (gh * _sigmoid(gh) * nrm).astype(BF16)
        gel, _ = _gelu_parts(gt_ref[...])
        cat_ref[:, RET_W:] = ((hf_ref[...] + hb_ref[...]) * gel).astype(BF16)
        x1_ref[...] = x_ref[...] + g1_ref[...] * _dot(cat_ref[...], w_ref[...])

    half = pl.BlockSpec((tm, RET_W), lambda i: (i, 0))
    big = pl.BlockSpec((tm, D_MODEL), lambda i: (i, 0))
    return _pc(body, name="mix_fwd", grid=(t // tm,),
               in_specs=[half, half, pl.BlockSpec((tm, RET_W), lambda i: (i, G_BLOCK)),
                         pl.BlockSpec((tm, LRU_W), lambda i: (i, GATE_BLOCK)), half, half,
                         _full((D_MODEL, D_MODEL)), big, _full((1, D_MODEL))],
               out_specs=[big, big], out_shape=[_sds((t, D_MODEL)), _sds((t, D_MODEL), BF16)],
               compiler_params=_params("arbitrary"))(o_f, o_b, proj, proj, hf, hb, w_out, x, g1)


def _mix_bwd(o_f, o_b, proj, hf, hb, w_out, cat, dx1, g1, comms=()):
    t = dx1.shape[0]
    tm = _tile(t)

    def body(of_ref, ob_ref, g_ref, gt_ref, hf_ref, hb_ref, w_ref, cat_ref, dx1_ref, g1_ref,
             do_ref, dhs_ref, dg_ref, dgt_ref, dyb_ref, dg1_ref):
        dx1v = dx1_ref[...]
        y = _dot(cat_ref[...], w_ref[...])

        @pl.when(pl.program_id(0) == 0)
        def _():
            dg1_ref[...] = jnp.zeros_like(dg1_ref)
        dg1_ref[...] += _sum0(dx1v * y)
        dyb = (g1_ref[...] * dx1v).astype(BF16)
        dyb_ref[...] = dyb
        dcat = _dot_nt(dyb, w_ref[...])
        o = of_ref[...] + ob_ref[...]
        g = g_ref[...]
        for hh in range(HEADS):
            sl = slice(DH * hh, DH * (hh + 1))
            nrm, rs = _head_norm(o[:, sl])
            gh = g[:, sl]
            sg = _sigmoid(gh)
            dret = dcat[:, sl]
            dg_ref[:, sl] = dret * nrm * (sg * (1.0 + gh * (1.0 - sg)))
            dn = dret * (gh * sg)
            dyc = rs * (dn - nrm * jnp.mean(dn * nrm, axis=-1, keepdims=True))
            do_ref[:, sl] = dyc - jnp.mean(dyc, axis=-1, keepdims=True)
        z = gt_ref[...]
        gel, th = _gelu_parts(z)
        dlru = dcat[:, RET_W:]
        dhs_ref[...] = dlru * gel
        dgel = 0.5 * (1.0 + th) + 0.5 * z * (1.0 - th * th) * GELU_K * (1.0 + 3.0 * GELU_C * z * z)
        dgt_ref[...] = dlru * (hf_ref[...] + hb_ref[...]) * dgel

    half = pl.BlockSpec((tm, RET_W), lambda i: (i, 0))
    big = pl.BlockSpec((tm, D_MODEL), lambda i: (i, 0))
    return _call(body, name="mix_bwd", grid=(t // tm,),
                 in_specs=[half, half, pl.BlockSpec((tm, RET_W), lambda i: (i, G_BLOCK)),
                           pl.BlockSpec((tm, LRU_W), lambda i: (i, GATE_BLOCK)), half, half,
                           _full((D_MODEL, D_MODEL)), big, big, _full((1, D_MODEL))],
                 out_specs=[half, half, half, half, big, _full((1, D_MODEL))],
                 out_shape=[_sds((t, RET_W))] * 4 + [_sds((t, D_MODEL), BF16), _sds((1, D_MODEL))],
                 scratch_shapes=[], sem=("arbitrary",), args=(o_f, o_b, proj, proj, hf, hb, w_out, cat, dx1, g1),
                 comms=comms)


def _mlp(x1, n2g, sh2, sc2, g2, fg, w1, w2, tgt):
    t = x1.shape[0]
    tm = _tile(t)
    hb_ = MLP_H // N_CHIP

    def body(x1_ref, n2g_ref, sh2_ref, sc2_ref, g2_ref, fg_ref, w1_hbm, w2_hbm, tgt_ref,
             dx1_ref, h2b_ref, ab_ref, dub_ref, dmb_ref, dsc_ref, dsh_ref, dg2_ref, dn2_ref, dfg_ref, loss_ref,
             w1_s, w2_s, r_s, sems):
        @pl.when(pl.program_id(0) == 0)
        def _():
            c1 = pltpu.make_async_copy(w1_hbm, w1_s, sems.at[0])
            c2 = pltpu.make_async_copy(w2_hbm, w2_s, sems.at[1])
            c1.start()
            c2.start()
            for r in (dsc_ref, dsh_ref, dg2_ref, dn2_ref, dfg_ref, loss_ref):
                r[...] = jnp.zeros_like(r)
            c1.wait()
            c2.wait()
        x1v = x1_ref[...]
        n2g, sc2, g2, fg = n2g_ref[...], sc2_ref[...], g2_ref[...], fg_ref[...]
        xh, _ = _rms(x1v)
        h2b = (xh * n2g * (1.0 + sc2) + sh2_ref[...]).astype(BF16)
        h2b_ref[...] = h2b
        m = jnp.zeros((tm, D_MODEL), F32)
        for j in range(N_CHIP):
            sl = slice(hb_ * j, hb_ * (j + 1))
            r = jnp.maximum(_dot(h2b, w1_s[j]), 0.0)
            r_s[:, sl] = r
            ab = (r * r).astype(BF16)
            ab_ref[:, sl] = ab
            m = m + _dot(ab, w2_s[j])
        x2 = x1v + g2 * m
        x2h, r2 = _rms(x2)
        err = x2h * fg - tgt_ref[...]
        loss_ref[...] += _sum0(err * err)
        dout = err * (1.0 / D_MODEL)
        dfg_ref[...] += _sum0(dout * x2h)
        dxh = dout * fg
        dx2 = r2 * (dxh - x2h * jnp.mean(dxh * x2h, axis=-1, keepdims=True))
        dg2_ref[...] += _sum0(dx2 * m)
        dmb = (g2 * dx2).astype(BF16)
        dmb_ref[...] = dmb
        dh2 = jnp.zeros((tm, D_MODEL), F32)
        for j in range(N_CHIP):
            sl = slice(hb_ * j, hb_ * (j + 1))
            dub = (_dot_nt(dmb, w2_s[j]) * (2.0 * r_s[:, sl])).astype(BF16)
            dub_ref[:, sl] = dub
            dh2 = dh2 + _dot_nt(dub, w1_s[j])
        dx, dn2_t, dsh_t, dsc_t = _norm_mod_bwd(x1v, n2g, sc2, dh2)
        dx1_ref[...] = dx2 + dx
        dn2_ref[...] += dn2_t
        dsh_ref[...] += dsh_t
        dsc_ref[...] += dsc_t

        @pl.when(pl.program_id(0) == t // tm - 1)
        def _():
            tot = jnp.sum(loss_ref[...], axis=1, keepdims=True) * (0.5 / D_MODEL)
            loss_ref[...] = jnp.broadcast_to(tot, loss_ref.shape)

    row = _full((1, D_MODEL))
    big = pl.BlockSpec((tm, D_MODEL), lambda i: (i, 0))
    wide = pl.BlockSpec((tm, MLP_H), lambda i: (i, 0))
    return _pc(body, name="mlp", grid=(t // tm,),
               in_specs=[big, row, row, row, row, row, ANY, ANY, big],
               out_specs=[big, big, wide, wide, big, row, row, row, row, row, row],
               out_shape=[_sds((t, D_MODEL)), _sds((t, D_MODEL), BF16), _sds((t, MLP_H), BF16), _sds((t, MLP_H), BF16),
                          _sds((t, D_MODEL), BF16)] + [_sds((1, D_MODEL))] * 6,
               scratch_shapes=[pltpu.VMEM(w1.shape, BF16), pltpu.VMEM(w2.shape, BF16), pltpu.VMEM((tm, MLP_H), F32),
                               pltpu.SemaphoreType.DMA((2,))],
               compiler_params=_params("arbitrary"))(x1, n2g, sh2, sc2, g2, fg, w1, w2, tgt)


def _tn(a, b, nj, a_blocked, b_blocked, name, extra=None, comms=()):
    t = a.shape[0]
    m = a.shape[1] // (nj if a_blocked else 1)
    n = b.shape[1] // (nj if b_blocked else 1)
    bk = 512 if t % 512 == 0 else t
    nk = t // bk
    a_map = (lambda j, k: (k, j)) if a_blocked else (lambda j, k: (k, 0))
    b_map = (lambda j, k: (k, j)) if b_blocked else (lambda j, k: (k, 0))
    in_specs = [pl.BlockSpec((bk, m), a_map), pl.BlockSpec((bk, n), b_map)]
    args = [a, b]
    if extra is not None:
        a2, b2 = extra
        t2 = a2.shape[0]
        in_specs += [pl.BlockSpec((t2, m), (lambda j, k: (0, j)) if a_blocked else (lambda j, k: (0, 0))),
                     pl.BlockSpec((t2, n), (lambda j, k: (0, j)) if b_blocked else (lambda j, k: (0, 0)))]
        args += [a2, b2]

    def body(*refs):
        a_ref, b_ref = refs[0], refs[1]
        o_ref, acc = refs[-2], refs[-1]
        k = pl.program_id(1)

        @pl.when(k == 0)
        def _():
            acc[...] = jnp.zeros_like(acc)
        acc[...] += _dot_tn(a_ref[...].astype(BF16), b_ref[...].astype(BF16))

        @pl.when(k == nk - 1)
        def _():
            if extra is not None:
                acc[...] += _dot_tn(refs[2][...].astype(BF16), refs[3][...].astype(BF16))
            o_ref[0] = acc[...]

    (out,), couts = _call(body, name=name, grid=(nj, nk), in_specs=in_specs,
                          out_specs=[pl.BlockSpec((1, m, n), lambda j, k: (j, 0, 0))], out_shape=[_sds((nj, m, n))],
                          scratch_shapes=[pltpu.VMEM((m, n), F32)], sem=("arbitrary", "arbitrary"), args=args,
                          comms=comms)
    return (out, couts) if comms else out


def _block_diag(w):
    eye = jnp.eye(LRU_BLOCKS, dtype=F32)
    return (w[:, :, None, :] * eye[:, None, :, None]).reshape(LRU_W, LRU_W).astype(BF16)


def _diag_blocks(mat):
    eye = jnp.eye(LRU_BLOCKS, dtype=jnp.bool_)
    m4 = mat.reshape(LRU_BLOCKS, LRU_BD, LRU_BLOCKS, LRU_BD)
    return jnp.sum(jnp.where(eye[:, None, :, None], m4, 0.0), axis=2)


def _pad_rows(v, width=LANES):
    flat = v.reshape(-1).astype(F32)
    tile = SUBLANES * width
    n = -(-flat.shape[0] // tile) * tile
    return jnp.pad(flat, (0, n - flat.shape[0])).reshape(n // width, width)


class _Slab:
    def __init__(self):
        self.parts, self.meta, self.rows = [], {}, 0

    def add(self, name, v):
        p = _pad_rows(v)
        self.meta[name] = (self.rows, p.shape[0], v.shape)
        self.parts.append(p)
        self.rows += p.shape[0]

    def build(self):
        return jnp.concatenate(self.parts, axis=0)

    def take(self, slab, name, shape=None):
        r0, nr, shp = self.meta[name]
        shp = shp if shape is None else shape
        return slab[r0:r0 + nr].reshape(-1)[:math.prod(shp)].reshape(shp)


def _lane_rep(v8):
    return jnp.broadcast_to(v8.reshape(SUBLANES, 1), (SUBLANES, LANES))


def kernel(x, c, ctx, c_ctx, w_ada, b_ada, norm1_g, norm2_g, w_in, ret_decay, conv_w, conv_b, lru_wa, lru_ba, lru_wx, lru_bx, lru_lambda, w_out, w_mlp1, w_mlp2, final_g, loss_target, m_c_ctx, m_w_ada, m_b_ada, m_norm1_g, m_norm2_g, m_w_in, m_ret_decay, m_conv_w, m_conv_b, m_lru_wa, m_lru_ba, m_lru_wx, m_lru_bx, m_lru_lambda, m_w_out, m_w_mlp1, m_w_mlp2, m_final_g, v_c_ctx, v_w_ada, v_b_ada, v_norm1_g, v_norm2_g, v_w_in, v_ret_decay, v_conv_w, v_conv_b, v_lru_wa, v_lru_ba, v_lru_wx, v_lru_bx, v_lru_lambda, v_w_out, v_w_mlp1, v_w_mlp2, v_final_g):
    ax, ay, ac = lax.axis_index("x"), lax.axis_index("y"), lax.axis_index("c")
    chip = 2 * ax + ay
    dev = 4 * ax + 2 * ay + ac
    c_idx = ac.reshape(1).astype(jnp.int32)
    j_idx = chip.reshape(1).astype(jnp.int32)

    xt = x[0]
    t_len = xt.shape[0]
    ctxt = ctx[0]
    l_len = ctxt.shape[0]
    tgt = loss_target[0]
    ada_n = w_ada.shape[2]

    def my_half(w2d):
        r = w2d.shape[0] // 2
        return lax.dynamic_slice_in_dim(w2d, ac * r, r, axis=0).astype(BF16)

    pad8 = lambda a: jnp.pad(a, ((0, SUBLANES - a.shape[0]), (0, 0)))
    small = jnp.concatenate([pad8(conv_w[0]), pad8(lru_ba[0]), pad8(lru_bx[0]), pad8(lru_lambda[0])], axis=0)
    gw_in, c_all, small_all = _all_gather([my_half(w_in[0]), pad8(c), small], "gather_head")
    w4 = gw_in.reshape(N_CHIP, D_MODEL, IN_COLS // N_CHIP)

    a16, lgv, sgv = _prep(c_all[:, 0, :], c_ctx, ret_decay[0])
    b_shard = lax.dynamic_slice_in_dim(b_ada, chip * ada_n, ada_n, axis=1)
    (mod_parts,) = _all_gather([_mod_fwd(a16, w_ada[0], b_shard)], "gather_mod")
    mod_all = mod_parts[0::2].transpose(1, 0, 2).reshape(16, N_CHIP * ada_n)
    mod_me = lax.dynamic_slice_in_dim(mod_all, dev, 1, axis=0)
    sh1, sc1, g1, sh2, sc2, g2 = [mod_me[:, D_MODEL * k:D_MODEL * (k + 1)] for k in range(N_MOD)]
    csh1, csc1 = mod_all[8:9, 0:D_MODEL], mod_all[8:9, D_MODEL:2 * D_MODEL]

    cos2, sin2 = _rotary_tables(t_len)
    cos_c, sin_c = jnp.ones((l_len, DH), F32), jnp.zeros((l_len, DH), F32)
    n1g, n2g = norm1_g, norm2_g
    fg = final_g.reshape(1, D_MODEL)

    small_full = small_all[0::2].transpose(1, 0, 2).reshape(4 * SUBLANES, LRU_W)
    cw = small_full[0:4]
    cb = conv_b
    ba_f, ba_b = small_full[8:9], small_full[9:10]
    bx_f, bx_b = small_full[16:17], small_full[17:18]
    lam_f, lam_b = small_full[24:25], small_full[25:26]
    wa_f, wa_b = _block_diag(lru_wa[0, 0]), _block_diag(lru_wa[0, 1])
    wx_f, wx_b = _block_diag(lru_wx[0, 0]), _block_diag(lru_wx[0, 1])
    zero_h = jnp.zeros((1, LRU_W), F32)

    projc, hcb16 = _inproj_fwd(ctxt, n1g, csh1, csc1, w4, cos_c, sin_c, "inproj_fwd_ctx")
    s_f, s_b = _ctx_state_fwd(projc, lgv)
    xcc = _conv_fwd(projc, cw, cb, "conv_fwd_ctx")
    hcf = _lru_fwd(xcc, wa_f, wx_f, ba_f, bx_f, lam_f, zero_h, False, "lru_fwd_ctx_f")
    hcbk = _lru_fwd(xcc, wa_b, wx_b, ba_b, bx_b, lam_b, zero_h, True, "lru_fwd_ctx_b")
    lru_sf, lru_sb = hcf[l_len - 1:l_len], hcbk[0:1]

    proj, hb16 = _inproj_fwd(xt, n1g, sh1, sc1, w4, cos2, sin2, "inproj_fwd")
    (o_f, o_b, spf, spb), ((gw_1,),) = _ret_fwd(proj, lgv, s_f, s_b, comms=(_AllGather([my_half(w_mlp1[0])]),))
    xcl = _conv_fwd(proj, cw, cb, "conv_fwd")
    hf, ((gw_2,),) = _lru_fwd(xcl, wa_f, wx_f, ba_f, bx_f, lam_f, lru_sf, False, "lru_fwd_f",
                             comms=(_AllGather([my_half(w_mlp2[0])]),))
    hbk, ((gw_out,),) = _lru_fwd(xcl, wa_b, wx_b, ba_b, bx_b, lam_b, lru_sb, True, "lru_fwd_b",
                                comms=(_AllGather([my_half(w_out[0])]),))
    wo = gw_out.reshape(D_MODEL, D_MODEL)
    w1 = gw_1.reshape(N_CHIP, D_MODEL, MLP_H // N_CHIP)
    w2 = gw_2.reshape(N_CHIP, MLP_H // N_CHIP, D_MODEL)
    x1, cat = _mix_fwd(o_f, o_b, proj, hf, hbk, wo, xt, g1)

    (dx1, h2b, ab, dub, dmb, dsc2, dsh2, dg2, dn2g, dfg, lossv) = _mlp(x1, n2g, sh2, sc2, g2, fg, w1, w2, tgt)
    gw_mlp1 = _tn(h2b, dub, N_CHIP, False, True, "grad_w_mlp1")
    b_1 = gw_mlp1.reshape(N_DEV, D_MODEL // 2, MLP_H // N_CHIP)
    gw_mlp2, ((r_1,),) = _tn(ab, dmb, N_CHIP, True, False, "grad_w_mlp2", comms=(_pair_exchange([b_1]),))

    jc_idx = jnp.concatenate([j_idx, c_idx])
    b_2 = gw_mlp2.reshape(N_DEV, MLP_H // N_DEV, D_MODEL)
    (do, dhs, dg, dgate, dyb, dg1), ((r_2,),) = _mix_bwd(
        o_f, o_b, proj, hf, hbk, wo, cat, dx1, g1, comms=(_pair_exchange([b_2]),))
    gw_o = _tn(cat, dyb, 1, False, False, "grad_w_out")
    b_o = gw_o.reshape(N_DEV, D_MODEL // N_DEV, D_MODEL)
    p_1, pb_1 = _pair_add(b_1, r_1, c_idx, "rs_pair_add_w_mlp1")
    p_2, pb_2 = _pair_add(b_2, r_2, c_idx, "rs_pair_add_w_mlp2")

    (dq_f, dk_f, dv_f, ds_f, drd_f), ((q_1,), (r_o,)) = _ret_bwd(
        proj, lgv, sgv, spf, do, False, "ret_bwd_f", comms=(_chip_exchange([pb_1]), _pair_exchange([b_o])))
    p_o, pb_o = _pair_add(b_o, r_o, c_idx, "rs_pair_add_w_out")
    h_1 = _chip_add(p_1, q_1, jc_idx, "rs_chip_add_w_mlp1")

    (dq_b, dk_b, dv_b, ds_b, drd_b), ((q_2,), (f_1,)) = _ret_bwd(
        proj, lgv, sgv, spb, do, True, "ret_bwd_b", comms=(_chip_exchange([pb_2]), _pair_gather([h_1])))
    h_2 = _chip_add(p_2, q_2, jc_idx, "rs_chip_add_w_mlp2")

    (dxc_f, dpre_f, dba_f, dbx_f, dlam_f, dh0_f), ((q_o,), (f_2,)) = _lru_bwd(
        xcl, wa_f, wx_f, ba_f, bx_f, lam_f, hf, lru_sf, dhs, False, "lru_bwd_f",
        comms=(_chip_exchange([pb_o]), _pair_gather([h_2])))
    h_o = _chip_add(p_o, q_o, jc_idx, "rs_chip_add_w_out")
    (dxc_b, dpre_b, dba_b, dbx_b, dlam_b, dh0_b), ((f_o,),) = _lru_bwd(
        xcl, wa_b, wx_b, ba_b, bx_b, lam_b, hbk, lru_sb, dhs, True, "lru_bwd_b", comms=(_pair_gather([h_o]),))
    dxr, dcw, dcb = _conv_bwd(dxc_f + dxc_b, proj, cw, "conv_bwd")
    grad_x, dpb, dn1g, dsh1, dsc1 = _inproj_bwd(
        xt, n1g, sh1, sc1, w4, cos2, sin2, [dq_f, dq_b, dk_f, dk_b, dv_f, dv_b, dg, dxr, dgate], dx1, "inproj_bwd")

    dkc, dvc, drd_c = _ctx_state_bwd(projc, lgv, sgv, ds_f, ds_b)
    zc = jnp.zeros((l_len, LRU_W), F32)
    dhc_f = lax.dynamic_update_slice(zc, dh0_f, (l_len - 1, 0))
    dhc_b = lax.dynamic_update_slice(zc, dh0_b, (0, 0))
    (dxcc_f, dprec_f, dbac_f, dbxc_f, dlamc_f, _), _ = _lru_bwd(
        xcc, wa_f, wx_f, ba_f, bx_f, lam_f, hcf, zero_h, dhc_f, False, "lru_bwd_ctx_f")
    (dxcc_b, dprec_b, dbac_b, dbxc_b, dlamc_b, _), _ = _lru_bwd(
        xcc, wa_b, wx_b, ba_b, bx_b, lam_b, hcbk, zero_h, dhc_b, True, "lru_bwd_ctx_b")
    dxrc, dcw_c, dcb_c = _conv_bwd(dxcc_f + dxcc_b, projc, cw, "conv_bwd_ctx")
    zr = jnp.zeros((l_len, RET_W), F32)
    _, dpbc, dn1g_c, dcsh1, dcsc1 = _inproj_bwd(
        ctxt, n1g, csh1, csc1, w4, cos_c, sin_c, [zr, zr, dkc, zr, dvc, zr, zr, dxrc, zr],
        jnp.zeros((l_len, D_MODEL), F32), "inproj_bwd_ctx")

    gw_i = _tn(hb16, dpb, N_CHIP, False, True, "grad_w_in", extra=(hcb16, dpbc))
    b_i = gw_i.reshape(N_DEV, D_MODEL // 2, IN_COLS // N_CHIP)
    gwa_f, ((r_i,),) = _tn(xcl, dpre_f, 2, False, True, "grad_lru_gates_f", extra=(xcc, dprec_f),
                           comms=(_pair_exchange([b_i]),))
    p_i, pb_i = _pair_add(b_i, r_i, c_idx, "rs_pair_add_w_in")
    gwa_b, ((q_i,),) = _tn(xcl, dpre_b, 2, False, True, "grad_lru_gates_b", extra=(xcc, dprec_b),
                           comms=(_chip_exchange([pb_i]),))
    h_i = _chip_add(p_i, q_i, jc_idx, "rs_chip_add_w_in")
    g_out, g_1, g_2 = _shard_of(f_o), _shard_of(f_1), _shard_of(f_2)
    big = {}
    (d_, mn, vn), ((f_i,),) = _adamw(w_mlp1[0], g_1, m_w_mlp1[0], v_w_mlp1[0], "adamw_w_mlp1",
                                     comms=(_pair_gather([h_i]),))
    big["w_mlp1"] = (g_1[None], d_[None], mn[None], vn[None])
    g_in = _shard_of(f_i)
    for nm, w, g, m, v in (("w_in", w_in, g_in, m_w_in, v_w_in), ("w_out", w_out, g_out, m_w_out, v_w_out),
                           ("w_mlp2", w_mlp2, g_2, m_w_mlp2, v_w_mlp2)):
        d_, mn, vn = _adamw(w[0], g, m[0], v[0], "adamw_" + nm)
        big[nm] = (g[None], d_[None], mn[None], vn[None])

    part = _Slab()
    part.add("loss", lossv)
    part.add("dmod", jnp.concatenate([dsh1, dsc1, dg1, dsh2, dsc2, dg2], axis=1))
    part.add("dmodc", jnp.concatenate([dcsh1, dcsc1], axis=1))
    part.add("norm1_g", dn1g + dn1g_c)
    part.add("norm2_g", dn2g)
    part.add("final_g", dfg)
    part.add("ret_decay", jnp.concatenate([drd_f[:, 0, :] + drd_c[:, 0, :], drd_b[:, 0, :] + drd_c[:, 1, :]], axis=0))
    part.add("conv_w", dcw + dcw_c)
    part.add("conv_b", dcb + dcb_c)
    part.add("lru_wa", jnp.stack([_diag_blocks(gwa_f[0]), _diag_blocks(gwa_b[0])]))
    part.add("lru_wx", jnp.stack([_diag_blocks(gwa_f[1]), _diag_blocks(gwa_b[1])]))
    part.add("lru_ba", jnp.concatenate([dba_f + dbac_f, dba_b + dbac_b], axis=0))
    part.add("lru_bx", jnp.concatenate([dbx_f + dbxc_f, dbx_b + dbxc_b], axis=0))
    part.add("lru_lambda", jnp.concatenate([dlam_f + dlamc_f, dlam_b + dlamc_b], axis=0))
    (parts_all,) = _all_gather([part.build()], "gather_small_grads")
    tot = _sum_devices(parts_all, "sum_small_grads")

    loss = part.take(tot, "loss")[0, 0]
    dmod_all = parts_all[:, part.meta["dmod"][0]:part.meta["dmod"][0] + part.meta["dmod"][1]].reshape(N_DEV, -1)
    dmodc_tot = jnp.pad(part.take(tot, "dmodc").reshape(1, -1), ((0, 0), (0, (N_MOD - 2) * D_MODEL)))
    dmod_tot = part.take(tot, "dmod").reshape(1, -1)
    grad_b_ada = dmod_tot + dmodc_tot

    cols = lambda a: lax.dynamic_slice_in_dim(a, chip * ada_n, ada_n, axis=1)
    b128 = jnp.pad(jnp.concatenate([cols(dmod_all), jnp.pad(cols(dmodc_tot), ((0, SUBLANES - 1), (0, 0)))], axis=0),
                   ((0, LANES - 2 * SUBLANES), (0, 0)))
    g_ada = _ada_grad(jnp.pad(a16.T, ((0, 0), (0, LANES - 16))), b128)
    d_ada, m_ada, v_ada = _adamw(w_ada[0], g_ada, m_w_ada[0], v_w_ada[0], "adamw_w_ada")

    dmc8 = jnp.pad(cols(dmodc_tot), ((0, SUBLANES - 1), (0, 0)))
    (cparts,) = _all_gather([_cctx_partial(dmc8, w_ada[0])], "gather_cctx")
    g_cc, d_cc, m_cc, v_cc = _cctx_final(cparts, c_ctx, m_c_ctx, v_c_ctx)

    def shard_cols(a, width=LANES):
        return lax.dynamic_slice_in_dim(a, chip * width, width, axis=a.ndim - 1)

    grads = {
        "b_ada": grad_b_ada,
        "norm1_g": part.take(tot, "norm1_g"),
        "norm2_g": part.take(tot, "norm2_g"),
        "ret_decay": part.take(tot, "ret_decay", (SUBLANES, LANES)),
        "conv_w": shard_cols(part.take(tot, "conv_w")),
        "conv_b": part.take(tot, "conv_b"),
        "lru_wa": part.take(tot, "lru_wa"),
        "lru_ba": shard_cols(part.take(tot, "lru_ba")),
        "lru_wx": part.take(tot, "lru_wx"),
        "lru_bx": shard_cols(part.take(tot, "lru_bx")),
        "lru_lambda": shard_cols(part.take(tot, "lru_lambda")),
        "final_g": part.take(tot, "final_g"),
    }
    params = {
        "b_ada": (b_ada, m_b_ada, v_b_ada), "norm1_g": (norm1_g, m_norm1_g, v_norm1_g),
        "norm2_g": (norm2_g, m_norm2_g, v_norm2_g), "ret_decay": (ret_decay, m_ret_decay, v_ret_decay),
        "conv_w": (conv_w, m_conv_w, v_conv_w), "conv_b": (conv_b, m_conv_b, v_conv_b),
        "lru_wa": (lru_wa, m_lru_wa, v_lru_wa), "lru_ba": (lru_ba, m_lru_ba, v_lru_ba),
        "lru_wx": (lru_wx, m_lru_wx, v_lru_wx), "lru_bx": (lru_bx, m_lru_bx, v_lru_bx),
        "lru_lambda": (lru_lambda, m_lru_lambda, v_lru_lambda), "final_g": (final_g, m_final_g, v_final_g),
    }
    slabs = [_Slab() for _ in range(4)]
    for nm, gval in grads.items():
        vals = (gval,) + params[nm]
        for s, val in zip(slabs, vals):
            if nm == "ret_decay" and val.shape != (SUBLANES, LANES):
                val = _lane_rep(val.reshape(-1))
            s.add(nm, val)
    gs, ws, ms, vs = [s.build() for s in slabs]
    ds_, mns, vns = _adamw(ws, gs, ms, vs, "adamw_small")
    small_out = {}
    for nm in grads:
        shp = params[nm][0].shape
        if nm == "ret_decay":
            unp = lambda sl: slabs[0].take(sl, nm, (SUBLANES, LANES))[:, 0].reshape(shp)
        else:
            unp = lambda sl: slabs[0].take(sl, nm, shp)
        small_out[nm] = (unp(gs), unp(ds_), unp(mns), unp(vns))
    small_out["c_ctx"] = tuple(a.reshape(D_MODEL) for a in (g_cc, d_cc, m_cc, v_cc))
    small_out["w_ada"] = (g_ada[None], d_ada[None], m_ada[None], v_ada[None])
    small_out.update(big)

    order = ["c_ctx", "w_ada", "b_ada", "norm1_g", "norm2_g", "w_in", "ret_decay", "conv_w", "conv_b", "lru_wa", "lru_ba",
             "lru_wx", "lru_bx", "lru_lambda", "w_out", "w_mlp1", "w_mlp2", "final_g"]
    outs = [loss, grad_x[None]]
    for k in range(4):
        outs += [small_out[nm][k] for nm in order]
    return tuple(outs)
```

```python
import math

import jax
import jax.numpy as jnp
from jax import lax
from jax.experimental import pallas as pl
from jax.experimental.pallas import tpu as pltpu

F32 = jnp.float32
BF16 = jnp.bfloat16

D_MODEL = 1024
HEADS = 4
DH = 128
CHUNK = 128
RET_W = HEADS * DH
LRU_W = 512
LRU_BLOCKS = 8
LRU_BD = LRU_W // LRU_BLOCKS
LRU_C = 8.0
IN_COLS = 4 * RET_W + 2 * LRU_W
MLP_H = 4 * D_MODEL
N_MOD = 6
GRID_W = 64
ROPE_BASE = 10000.0
K_SCALE = DH ** -0.5
EPS = 1e-6
GELU_K = math.sqrt(2.0 / math.pi)
GELU_C = 0.044715

ADAM_LR = 0.001
ADAM_B1 = 0.9
ADAM_B2 = 0.999
ADAM_EPS = 1e-08
ADAM_WD = 0.01
ADAM_STEP = 10

N_DEV = 8
N_CHIP = 4
SUBLANES = 8
LANES = 128
VMEM_LIMIT_V7X = 56 * 1024 * 1024
MESH = pl.DeviceIdType.MESH
ANY = pl.BlockSpec(memory_space=pl.ANY)


def _pc(body, **kw):
    return pl.pallas_call(body, **kw)


def _params(*sem):
    return pltpu.CompilerParams(dimension_semantics=sem if sem else None, vmem_limit_bytes=VMEM_LIMIT_V7X)


def _tile(t, big=False):
    if big and t >= 1024:
        return 512
    return 256 if t >= 256 else t


def _sds(shape, dtype=F32):
    return jax.ShapeDtypeStruct(tuple(shape), dtype)


def _full(shape):
    nd = len(shape)
    return pl.BlockSpec(tuple(shape), lambda *_: (0,) * nd)


def _sigmoid(x):
    return 1.0 / (1.0 + jnp.exp(-x))


def _log1p_pos(y):
    s = y * (1.0 - y * (0.5 - y * (1.0 / 3.0 - y * (0.25 - y * (0.2 - y / 6.0)))))
    return jnp.where(y < 0.03, s, jnp.log(1.0 + y))


def _softplus(z):
    return jnp.maximum(z, 0.0) + _log1p_pos(jnp.exp(-jnp.abs(z)))


def _neg_expm1(x, exp_x):
    t = x * (1.0 + x * (0.5 + x * (1.0 / 6.0 + x * (1.0 / 24.0 + x * (1.0 / 120.0 + x * (1.0 / 720.0 + x / 5040.0))))))
    return -jnp.where(x > -0.25, t, exp_x - 1.0)


def _rms(x):
    r = lax.rsqrt(jnp.mean(x * x, axis=-1, keepdims=True) + EPS)
    return x * r, r


def _dot(a, b):
    return jnp.dot(a, b, preferred_element_type=F32)


def _dot_nt(a, b):
    return lax.dot_general(a, b, (((1,), (1,)), ((), ())), preferred_element_type=F32)


def _dot_tn(a, b):
    return lax.dot_general(a, b, (((0,), (0,)), ((), ())), preferred_element_type=F32)


def _sum0(x):
    return jnp.sum(x, axis=0, keepdims=True)


def _norm_mod_bwd(x, g, sc, dh):
    xh, r = _rms(x)
    hn = xh * g
    dhn = dh * (1.0 + sc)
    dxh = dhn * g
    dx = r * (dxh - xh * jnp.mean(dxh * xh, axis=-1, keepdims=True))
    return dx, _sum0(dhn * xh), _sum0(dh), _sum0(dh * hn)


def _dev_index(p):
    return 4 * p[0] + 2 * p[1] + p[2]


def _mesh_pos():
    return lax.axis_index("x"), lax.axis_index("y"), lax.axis_index("c")


class _AllGather:
    def __init__(self, arrs):
        n = len(arrs)
        self.arrays = list(arrs)
        self.out_shapes = [_sds((N_DEV,) + a.shape, a.dtype) for a in arrs]
        self.scratch = ([pltpu.VMEM(a.shape, a.dtype) for a in arrs]
                        + [pltpu.SemaphoreType.DMA((7 * n,)), pltpu.SemaphoreType.DMA((7 * n,)),
                           pltpu.SemaphoreType.DMA((n,))])
        self.aliases = {}

    def _parts(self, ins, outs, scr):
        n = len(self.arrays)
        stage = scr[:n]
        send_sems, recv_sems, local_sems = scr[n:]
        x, y, c = _mesh_pos()
        me, sib = (x, y, c), (x, y, 1 - c)
        chips = [(1 - x, y), (x, 1 - y), (1 - x, 1 - y)]

        def copy(t, k, block, to, own=False):
            dst = outs[t].at[_dev_index(block)]
            return pltpu.make_async_remote_copy(
                src_ref=ins[t] if own else dst, dst_ref=dst,
                send_sem=send_sems.at[7 * t + k], recv_sem=recv_sems.at[7 * t + k],
                device_id=to, device_id_type=MESH)

        first = []
        for t in range(n):
            first.append(copy(t, 0, me, sib, own=True))
            for j, ch in enumerate(chips):
                first.append(copy(t, 1 + j, me, (*ch, c), own=True))
        stage_in = [pltpu.make_async_copy(ins[t], stage[t], local_sems.at[t]) for t in range(n)]
        mine = [pltpu.make_async_copy(stage[t], outs[t].at[_dev_index(me)], local_sems.at[t]) for t in range(n)]
        return n, c, me, sib, chips, copy, first, stage_in, mine

    def start(self, ins, outs, scr):
        n, _, _, _, _, _, first, stage_in, mine = self._parts(ins, outs, scr)
        for cp in stage_in:
            cp.start()
        for cp in first:
            cp.start()
        for t in range(n):
            stage_in[t].wait()
            mine[t].start()

    def finish(self, ins, outs, scr):
        n, c, me, sib, chips, copy, first, _, mine = self._parts(ins, outs, scr)
        passed = []
        for j, ch in enumerate(chips):
            for t in range(n):
                copy(t, 1 + j, (*ch, c), me).wait_recv()
                p = copy(t, 4 + j, (*ch, c), sib)
                p.start()
                passed.append(p)
        for t in range(n):
            copy(t, 0, sib, me).wait_recv()
            for j, ch in enumerate(chips):
                copy(t, 4 + j, (*ch, 1 - c), me).wait_recv()
        for cp in first + passed:
            cp.wait_send()
        for cp in mine:
            cp.wait()


class _Exchange:
    def __init__(self, arrays, out_shapes, plan, n_copies, aliases=None):
        self.arrays = list(arrays)
        self.out_shapes = list(out_shapes)
        self.plan = plan
        self.scratch = [pltpu.SemaphoreType.DMA((n_copies,)), pltpu.SemaphoreType.DMA((n_copies,))]
        self.aliases = aliases or {}

    def _copies(self, ins, outs, scr):
        send_sems, recv_sems = scr
        snd, rcv = [], []
        for i, (src, dst, peer, lands) in enumerate(self.plan(ins, outs, _mesh_pos())):
            kw = dict(send_sem=send_sems.at[i], recv_sem=recv_sems.at[i], device_id=peer, device_id_type=MESH)
            snd.append(pltpu.make_async_remote_copy(src_ref=src, dst_ref=dst, **kw))
            rcv.append(pltpu.make_async_remote_copy(src_ref=src, dst_ref=lands, **kw))
        return snd, rcv

    def start(self, ins, outs, scr):
        for cp in self._copies(ins, outs, scr)[0]:
            cp.start()

    def finish(self, ins, outs, scr):
        snd, rcv = self._copies(ins, outs, scr)
        for cp in rcv:
            cp.wait_recv()
        for cp in snd:
            cp.wait_send()


def _pair_exchange(grads):
    n = len(grads)

    def plan(ins, outs, pos):
        x, y, c = pos
        return [(ins[t].at[2 * j + (1 - c)], outs[t].at[j], (x, y, 1 - c), outs[t].at[j])
                for t in range(n) for j in range(N_CHIP)]

    return _Exchange(grads, [_sds((N_CHIP,) + g.shape[1:], g.dtype) for g in grads], plan, N_CHIP * n)


def _chip_exchange(parts):
    n = len(parts)

    def plan(ins, outs, pos):
        x, y, c = pos
        chips = [(1 - x, y), (x, 1 - y), (1 - x, 1 - y)]
        return [(ins[t].at[2 * ch[0] + ch[1]], outs[t].at[k], (*ch, c), outs[t].at[k])
                for t in range(n) for k, ch in enumerate(chips)]

    return _Exchange(parts, [_sds((3,) + p.shape[1:], p.dtype) for p in parts], plan, 3 * n)


def _pair_gather(bufs):
    n = len(bufs)

    def plan(ins, outs, pos):
        x, y, c = pos
        return [(ins[t].at[c], outs[t].at[c], (x, y, 1 - c), outs[t].at[1 - c]) for t in range(n)]

    return _Exchange(bufs, [_sds(b.shape, b.dtype) for b in bufs], plan, n, aliases={t: t for t in range(n)})


def _run_comm(comm, name):
    n_in, n_out = len(comm.arrays), len(comm.out_shapes)

    def body(*refs):
        ins, outs, scr = refs[:n_in], refs[n_in:n_in + n_out], refs[n_in + n_out:]
        comm.start(ins, outs, scr)
        comm.finish(ins, outs, scr)

    outs = _pc(body, name=name, out_shape=comm.out_shapes, in_specs=[ANY] * n_in, out_specs=[ANY] * n_out,
               input_output_aliases=dict(comm.aliases), scratch_shapes=comm.scratch,
               compiler_params=_params())(*comm.arrays)
    return list(outs)


def _all_gather(arrs, name):
    return _run_comm(_AllGather(arrs), name)


def _call(body, *, name, grid, in_specs, out_specs, out_shape, scratch_shapes, sem, args, comms=()):
    n_in, n_out, n_scr = len(in_specs), len(out_specs), len(scratch_shapes)
    c_in = [len(cm.arrays) for cm in comms]
    c_out = [len(cm.out_shapes) for cm in comms]
    c_scr = [len(cm.scratch) for cm in comms]
    aliases = {}
    for k, cm in enumerate(comms):
        for a, b in cm.aliases.items():
            aliases[n_in + sum(c_in[:k]) + a] = n_out + sum(c_out[:k]) + b

    def split(refs, counts):
        out, pos = [], 0
        for cnt in counts:
            out.append(refs[pos:pos + cnt])
            pos += cnt
        return out

    def wrapped(*refs):
        ins = refs[:n_in + sum(c_in)]
        outs = refs[len(ins):len(ins) + n_out + sum(c_out)]
        scr = refs[len(ins) + len(outs):]
        cins, couts, cscr = split(ins[n_in:], c_in), split(outs[n_out:], c_out), split(scr[n_scr:], c_scr)
        if comms:
            first = pl.program_id(0) == 0
            last = pl.program_id(0) == grid[0] - 1
            for k in range(1, len(grid)):
                first = jnp.logical_and(first, pl.program_id(k) == 0)
                last = jnp.logical_and(last, pl.program_id(k) == grid[k] - 1)

            @pl.when(first)
            def _():
                for k, cm in enumerate(comms):
                    cm.start(cins[k], couts[k], cscr[k])
        body(*ins[:n_in], *outs[:n_out], *scr[:n_scr])
        if comms:
            @pl.when(last)
            def _():
                for k, cm in enumerate(comms):
                    cm.finish(cins[k], couts[k], cscr[k])

    outs = _pc(wrapped, name=name, grid=grid,
               in_specs=list(in_specs) + [ANY] * sum(c_in), out_specs=list(out_specs) + [ANY] * sum(c_out),
               out_shape=list(out_shape) + [s for cm in comms for s in cm.out_shapes],
               scratch_shapes=list(scratch_shapes) + [s for cm in comms for s in cm.scratch],
               input_output_aliases=aliases, compiler_params=_params(*sem),
               )(*args, *[a for cm in comms for a in cm.arrays])
    outs = list(outs)
    return outs[:n_out], split(outs[n_out:], c_out)


def _row_block(r):
    for b in (512, 256, 128, 64, 32, 16, 8):
        if r % b == 0:
            return b
    return r


def _pair_add(g, recv, c_idx, name):
    _, r, cc = g.shape
    br = _row_block(r)

    def body(c_ref, g_ref, r_ref, p_ref, pb_ref):
        s = g_ref[...] + r_ref[...]
        p_ref[...] = s
        pb_ref[...] = s.astype(BF16)

    grid_spec = pltpu.PrefetchScalarGridSpec(
        num_scalar_prefetch=1, grid=(N_CHIP, r // br),
        in_specs=[pl.BlockSpec((1, br, cc), lambda j, i, c_ref: (2 * j + c_ref[0], i, 0)),
                  pl.BlockSpec((1, br, cc), lambda j, i, c_ref: (j, i, 0))],
        out_specs=[pl.BlockSpec((1, br, cc), lambda j, i, c_ref: (j, i, 0)),
                   pl.BlockSpec((1, br, cc), lambda j, i, c_ref: (j, i, 0))])
    return _pc(body, name=name, grid_spec=grid_spec,
               out_shape=[_sds((N_CHIP, r, cc)), _sds((N_CHIP, r, cc), BF16)],
               compiler_params=_params("arbitrary", "arbitrary"))(c_idx, g, recv)


def _chip_add(p, q, jc_idx, name):
    _, r, cc = p.shape
    br = _row_block(r)

    def body(jc_ref, p_ref, q_ref, o_ref):
        o_ref[0] = ((p_ref[0] + q_ref[0].astype(F32)) + q_ref[1].astype(F32)) + q_ref[2].astype(F32)

    grid_spec = pltpu.PrefetchScalarGridSpec(
        num_scalar_prefetch=1, grid=(r // br,),
        in_specs=[pl.BlockSpec((1, br, cc), lambda i, jc_ref: (jc_ref[0], i, 0)),
                  pl.BlockSpec((3, br, cc), lambda i, jc_ref: (0, i, 0))],
        out_specs=pl.BlockSpec((1, br, cc), lambda i, jc_ref: (jc_ref[1], i, 0)))
    return _pc(body, name=name, grid_spec=grid_spec, out_shape=_sds((2, r, cc)),
               compiler_params=_params("arbitrary"))(jc_idx, p, q)


def _shard_of(both):
    return both.reshape((2 * both.shape[1],) + both.shape[2:])


def _sum_devices(g, name):
    _, r, cc = g.shape

    def body(g_ref, o_ref):
        acc = g_ref[0]
        for d in range(1, N_DEV):
            acc = acc + g_ref[d]
        o_ref[...] = acc

    return _pc(body, name=name, out_shape=_sds((r, cc)), in_specs=[_full(g.shape)], out_specs=_full((r, cc)),
               compiler_params=_params())(g)


def _adamw(w, g, m, v, name, comms=()):
    r, cc = w.shape
    br = _row_block(r)
    if r * cc * 4 <= (1 << 20):
        br = r
    elif br * cc * 4 > (1 << 20) and br > 8:
        br = max(8, (1 << 20) // (cc * 4) // 8 * 8)
        while r % br:
            br -= 8
    c1 = 1.0 - ADAM_B1 ** ADAM_STEP
    c2 = 1.0 - ADAM_B2 ** ADAM_STEP

    def body(w_ref, g_ref, m_ref, v_ref, d_ref, mo_ref, vo_ref):
        gg = g_ref[...]
        mn = ADAM_B1 * m_ref[...] + (1.0 - ADAM_B1) * gg
        vn = ADAM_B2 * v_ref[...] + (1.0 - ADAM_B2) * (gg * gg)
        mh = mn / c1
        vh = vn / c2
        d_ref[...] = -ADAM_LR * (mh / (jnp.sqrt(vh) + ADAM_EPS) + ADAM_WD * w_ref[...])
        mo_ref[...] = mn
        vo_ref[...] = vn

    spec = pl.BlockSpec((br, cc), lambda i: (i, 0))
    outs, couts = _call(body, name=name, grid=(r // br,), in_specs=[spec] * 4, out_specs=[spec] * 3,
                        out_shape=[_sds((r, cc))] * 3, scratch_shapes=[], sem=("arbitrary",), args=(w, g, m, v),
                        comms=comms)
    return (outs, couts) if comms else outs


def _prep(c_all, c_ctx, ret_decay):
    def body(c_ref, cc_ref, rd_ref, a_ref, lg_ref, sg_ref):
        ca = c_ref[...]
        cc = cc_ref[...]
        a_ref[...] = jnp.zeros_like(a_ref)
        a_ref[0:8, :] = ca * _sigmoid(ca)
        a_ref[8:9, :] = cc * _sigmoid(cc)
        rd = rd_ref[...]
        lg_ref[...] = -_softplus(-rd)
        sg_ref[...] = _sigmoid(-rd)

    rd = jnp.broadcast_to(ret_decay.reshape(2, HEADS).T[:, :, None], (HEADS, 2, LANES))
    return _pc(body, name="prep",
               out_shape=[_sds((16, D_MODEL)), _sds((HEADS, 2, LANES)), _sds((HEADS, 2, LANES))],
               in_specs=[_full((8, D_MODEL)), _full((1, D_MODEL)), _full((HEADS, 2, LANES))],
               out_specs=[_full((16, D_MODEL)), _full((HEADS, 2, LANES)), _full((HEADS, 2, LANES))],
               compiler_params=_params())(c_all, c_ctx.reshape(1, D_MODEL), rd)


def _mod_fwd(a16, w_ada, b_shard):
    n = w_ada.shape[1]
    bn = 512

    def body(a_ref, w_ref, b_ref, o_ref):
        o_ref[...] = jnp.dot(a_ref[...], w_ref[...], preferred_element_type=F32,
                             precision=lax.Precision.HIGHEST) + b_ref[...]

    return _pc(body, name="mod_fwd", grid=(n // bn,),
               in_specs=[_full((16, D_MODEL)), pl.BlockSpec((D_MODEL, bn), lambda i: (0, i)),
                         pl.BlockSpec((1, bn), lambda i: (0, i))],
               out_specs=pl.BlockSpec((16, bn), lambda i: (0, i)), out_shape=_sds((16, n)),
               compiler_params=_params("arbitrary"))(a16, w_ada, b_shard)


def _ada_grad(at, b):
    n = b.shape[1]
    bn = 512

    def body(a_ref, b_ref, o_ref):
        o_ref[...] = jnp.dot(a_ref[...], b_ref[...], preferred_element_type=F32, precision=lax.Precision.HIGHEST)

    return _pc(body, name="ada_grad", grid=(n // bn,),
               in_specs=[_full((D_MODEL, LANES)), pl.BlockSpec((LANES, bn), lambda i: (0, i))],
               out_specs=pl.BlockSpec((D_MODEL, bn), lambda i: (0, i)), out_shape=_sds((D_MODEL, n)),
               compiler_params=_params("arbitrary"))(at, b)


def _cctx_partial(dmc8, w_ada):
    n = w_ada.shape[1]
    bn = 512

    def body(d_ref, w_ref, o_ref):
        @pl.when(pl.program_id(0) == 0)
        def _():
            o_ref[...] = jnp.zeros_like(o_ref)
        o_ref[...] += lax.dot_general(d_ref[...], w_ref[...], (((1,), (1,)), ((), ())),
                                      preferred_element_type=F32, precision=lax.Precision.HIGHEST)

    return _pc(body, name="cctx_partial", grid=(n // bn,),
               in_specs=[pl.BlockSpec((8, bn), lambda i: (0, i)), pl.BlockSpec((D_MODEL, bn), lambda i: (0, i))],
               out_specs=_full((8, D_MODEL)), out_shape=_sds((8, D_MODEL)),
               compiler_params=_params("arbitrary"))(dmc8, w_ada)


def _cctx_final(parts, c_ctx, m, v):
    c1 = 1.0 - ADAM_B1 ** ADAM_STEP
    c2 = 1.0 - ADAM_B2 ** ADAM_STEP

    def body(p_ref, c_ref, m_ref, v_ref, g_ref, d_ref, mo_ref, vo_ref):
        s = ((p_ref[0, 0:1, :] + p_ref[2, 0:1, :]) + p_ref[4, 0:1, :]) + p_ref[6, 0:1, :]
        z = c_ref[...]
        sg = _sigmoid(z)
        gg = s * (sg * (1.0 + z * (1.0 - sg)))
        g_ref[...] = gg
        mn = ADAM_B1 * m_ref[...] + (1.0 - ADAM_B1) * gg
        vn = ADAM_B2 * v_ref[...] + (1.0 - ADAM_B2) * (gg * gg)
        d_ref[...] = -ADAM_LR * ((mn / c1) / (jnp.sqrt(vn / c2) + ADAM_EPS) + ADAM_WD * z)
        mo_ref[...] = mn
        vo_ref[...] = vn

    row = _full((1, D_MODEL))
    return _pc(body, name="cctx_final", out_shape=[_sds((1, D_MODEL))] * 4,
               in_specs=[_full(parts.shape), row, row, row], out_specs=[row] * 4,
               compiler_params=_params())(parts, c_ctx.reshape(1, D_MODEL), m.reshape(1, D_MODEL), v.reshape(1, D_MODEL))


def _rotary_tables(t_len):
    rows = t_len // GRID_W
    row = jnp.repeat(jnp.arange(rows, dtype=F32), GRID_W)
    col = jnp.tile(jnp.arange(GRID_W, dtype=F32), rows)
    n_freq = DH // 4
    inv = ROPE_BASE ** (-jnp.arange(n_freq, dtype=F32) / n_freq)
    ang = jnp.concatenate([row[:, None] * inv, col[:, None] * inv], axis=-1)
    cos, sin = jnp.cos(ang), jnp.sin(ang)
    return jnp.concatenate([cos, cos], axis=-1), jnp.concatenate([-sin, sin], axis=-1)


def _inproj_fwd(x, gn, sh, sc, w4, cos2, sin2, name):
    t = x.shape[0]
    tm = _tile(t, True)
    nc = IN_COLS // N_CHIP

    def body(x_ref, gn_ref, sh_ref, sc_ref, w_ref, c_ref, s_ref, p_ref, hb_ref):
        xh, _ = _rms(x_ref[...])
        h = xh * gn_ref[...] * (1.0 + sc_ref[...]) + sh_ref[...]
        hb = h.astype(BF16)
        hb_ref[...] = hb
        for j in range(N_CHIP):
            p_ref[:, nc * j:nc * (j + 1)] = _dot(hb, w_ref[j])
        cc = c_ref[...]
        ss = s_ref[...]
        for hh in range(2 * HEADS):
            blk = p_ref[:, DH * hh:DH * (hh + 1)]
            rot = blk * cc + pltpu.roll(blk, DH // 2, 1) * ss
            if hh >= HEADS:
                rot = rot * K_SCALE
            p_ref[:, DH * hh:DH * (hh + 1)] = rot

    row = _full((1, D_MODEL))
    return _pc(body, name=name, grid=(t // tm,),
               in_specs=[pl.BlockSpec((tm, D_MODEL), lambda i: (i, 0)), row, row, row, _full(w4.shape),
                         pl.BlockSpec((tm, DH), lambda i: (i, 0)), pl.BlockSpec((tm, DH), lambda i: (i, 0))],
               out_specs=[pl.BlockSpec((tm, IN_COLS), lambda i: (i, 0)), pl.BlockSpec((tm, D_MODEL), lambda i: (i, 0))],
               out_shape=[_sds((t, IN_COLS)), _sds((t, D_MODEL), BF16)],
               compiler_params=_params("arbitrary"))(x, gn, sh, sc, w4, cos2, sin2)


def _inproj_bwd(x, gn, sh, sc, w4, cos2, sin2, pieces, dres, name):
    t = x.shape[0]
    tm = _tile(t)
    nc = IN_COLS // N_CHIP

    def body(x_ref, gn_ref, sh_ref, sc_ref, w_ref, c_ref, s_ref, dqf, dqb, dkf, dkb, dvf, dvb, dg, dxr, dgt, dres_ref,
             dx_ref, dpb_ref, dgn_ref, dsh_ref, dsc_ref):
        cc = c_ref[...]
        ss = s_ref[...]
        dq = dqf[...] + dqb[...]
        dk = dkf[...] + dkb[...]
        for hh in range(HEADS):
            sl = slice(DH * hh, DH * (hh + 1))
            b = dq[:, sl]
            dpb_ref[:, sl] = (b * cc + pltpu.roll(b * ss, DH // 2, 1)).astype(BF16)
            b = dk[:, sl]
            dpb_ref[:, RET_W + DH * hh:RET_W + DH * (hh + 1)] = (
                (b * cc + pltpu.roll(b * ss, DH // 2, 1)) * K_SCALE).astype(BF16)
        dpb_ref[:, 2 * RET_W:3 * RET_W] = (dvf[...] + dvb[...]).astype(BF16)
        dpb_ref[:, 3 * RET_W:4 * RET_W] = dg[...].astype(BF16)
        dpb_ref[:, 4 * RET_W:4 * RET_W + LRU_W] = dxr[...].astype(BF16)
        dpb_ref[:, 4 * RET_W + LRU_W:IN_COLS] = dgt[...].astype(BF16)
        dh = _dot_nt(dpb_ref[:, 0:nc], w_ref[0])
        for j in range(1, N_CHIP):
            dh = dh + _dot_nt(dpb_ref[:, nc * j:nc * (j + 1)], w_ref[j])
        dx, dgn_t, dsh_t, dsc_t = _norm_mod_bwd(x_ref[...], gn_ref[...], sc_ref[...], dh)
        dx_ref[...] = dres_ref[...] + dx

        @pl.when(pl.program_id(0) == 0)
        def _():
            dgn_ref[...] = jnp.zeros_like(dgn_ref)
            dsh_ref[...] = jnp.zeros_like(dsh_ref)
            dsc_ref[...] = jnp.zeros_like(dsc_ref)
        dgn_ref[...] += dgn_t
        dsh_ref[...] += dsh_t
        dsc_ref[...] += dsc_t

    row = _full((1, D_MODEL))
    pc = pl.BlockSpec((tm, RET_W), lambda i: (i, 0))
    big = pl.BlockSpec((tm, D_MODEL), lambda i: (i, 0))
    return _pc(body, name=name, grid=(t // tm,),
               in_specs=[big, row, row, row, _full(w4.shape),
                         pl.BlockSpec((tm, DH), lambda i: (i, 0)), pl.BlockSpec((tm, DH), lambda i: (i, 0))]
               + [pc] * 9 + [big],
               out_specs=[big, pl.BlockSpec((tm, IN_COLS), lambda i: (i, 0)), row, row, row],
               out_shape=[_sds((t, D_MODEL)), _sds((t, IN_COLS), BF16), _sds((1, D_MODEL)), _sds((1, D_MODEL)),
                          _sds((1, D_MODEL))],
               compiler_params=_params("arbitrary"))(x, gn, sh, sc, w4, cos2, sin2, *pieces, dres)


XR_BLOCK = (4 * RET_W) // LRU_W


def _halo_specs(t, tm, col):
    n8 = tm // SUBLANES
    last8 = t // SUBLANES - 1
    prev = pl.BlockSpec((SUBLANES, LRU_W), lambda i: (jnp.maximum(i * n8 - 1, 0), col))
    main = pl.BlockSpec((tm, LRU_W), lambda i: (i, col))
    nxt = pl.BlockSpec((SUBLANES, LRU_W), lambda i: (jnp.minimum((i + 1) * n8, last8), col))
    return prev, main, nxt


def _with_halo(prev_ref, main_ref, next_ref, i, nt):
    prev = jnp.where(i > 0, prev_ref[...], 0.0)
    nxt = jnp.where(i < nt - 1, next_ref[...], 0.0)
    return jnp.concatenate([prev, main_ref[...], nxt], axis=0)


def _conv_fwd(proj, cw, cb, name):
    t = proj.shape[0]
    tm = _tile(t, True)
    nt = t // tm
    n = tm + 2 * SUBLANES
    mid = slice(SUBLANES, SUBLANES + tm)

    def body(p_ref, m_ref, n_ref, w_ref, b_ref, o_ref):
        xp = _with_halo(p_ref, m_ref, n_ref, pl.program_id(0), nt)
        acc = b_ref[...] + pltpu.roll(xp, 1, 0)[mid] * w_ref[0:1, :]
        acc = acc + xp[mid] * w_ref[1:2, :]
        acc = acc + pltpu.roll(xp, n - 1, 0)[mid] * w_ref[2:3, :]
        acc = acc + pltpu.roll(xp, n - 2, 0)[mid] * w_ref[3:4, :]
        o_ref[...] = acc

    return _pc(body, name=name, grid=(nt,),
               in_specs=[*_halo_specs(t, tm, XR_BLOCK), _full((4, LRU_W)), _full((1, LRU_W))],
               out_specs=pl.BlockSpec((tm, LRU_W), lambda i: (i, 0)), out_shape=_sds((t, LRU_W)),
               compiler_params=_params("arbitrary"))(proj, proj, proj, cw, cb)


def _conv_bwd(dxc_a, dxc_b, proj, cw, name):
    t = proj.shape[0]
    tm = _tile(t, True)
    nt = t // tm
    n = tm + 2 * SUBLANES
    mid = slice(SUBLANES, SUBLANES + tm)

    def body(ap_ref, am_ref, an_ref, bp_ref, bm_ref, bn_ref, xp_ref, xm_ref, xn_ref, w_ref, dx_ref, dw_ref, db_ref):
        i = pl.program_id(0)
        dp = _with_halo(ap_ref, am_ref, an_ref, i, nt) + _with_halo(bp_ref, bm_ref, bn_ref, i, nt)
        xp = _with_halo(xp_ref, xm_ref, xn_ref, i, nt)
        dx = pltpu.roll(dp, n - 1, 0)[mid] * w_ref[0:1, :]
        dx = dx + dp[mid] * w_ref[1:2, :]
        dx = dx + pltpu.roll(dp, 1, 0)[mid] * w_ref[2:3, :]
        dx = dx + pltpu.roll(dp, 2, 0)[mid] * w_ref[3:4, :]
        dx_ref[...] = dx
        d = dp[mid]

        @pl.when(i == 0)
        def _():
            dw_ref[...] = jnp.zeros_like(dw_ref)
            db_ref[...] = jnp.zeros_like(db_ref)
        dw_ref[0:1, :] += _sum0(d * pltpu.roll(xp, 1, 0)[mid])
        dw_ref[1:2, :] += _sum0(d * xp[mid])
        dw_ref[2:3, :] += _sum0(d * pltpu.roll(xp, n - 1, 0)[mid])
        dw_ref[3:4, :] += _sum0(d * pltpu.roll(xp, n - 2, 0)[mid])
        db_ref[...] += _sum0(d)

    return _pc(body, name=name, grid=(nt,),
               in_specs=[*_halo_specs(t, tm, 0), *_halo_specs(t, tm, 0), *_halo_specs(t, tm, XR_BLOCK),
                         _full((4, LRU_W))],
               out_specs=[pl.BlockSpec((tm, LRU_W), lambda i: (i, 0)), _full((4, LRU_W)), _full((1, LRU_W))],
               out_shape=[_sds((t, LRU_W)), _sds((4, LRU_W)), _sds((1, LRU_W))],
               compiler_params=_params("arbitrary"))(dxc_a, dxc_a, dxc_a, dxc_b, dxc_b, dxc_b, proj, proj, proj, cw)


def _local_scan(a, b, reverse):
    n = a.shape[0]
    row = lax.broadcasted_iota(jnp.int32, a.shape, 0) & (SUBLANES - 1)
    for s in (1, 2, 4):
        if reverse:
            a_s, b_s, ok = pltpu.roll(a, n - s, 0), pltpu.roll(b, n - s, 0), row < SUBLANES - s
        else:
            a_s, b_s, ok = pltpu.roll(a, s, 0), pltpu.roll(b, s, 0), row >= s
        b = a * jnp.where(ok, b_s, 0.0) + b
        a = a * jnp.where(ok, a_s, 1.0)
    return a, b


def _carry_scan(a_s, b_s, out_ref, carry, reverse):
    ng = a_s.shape[0] // SUBLANES
    shape = carry.shape

    def step(g, cr):
        gg = (ng - 1 - g) if reverse else g
        off = pl.multiple_of(gg * SUBLANES, SUBLANES)
        h = a_s[pl.ds(off, SUBLANES), :] * cr + b_s[pl.ds(off, SUBLANES), :]
        out_ref[pl.ds(off, SUBLANES), :] = h
        edge = h[0:1, :] if reverse else h[SUBLANES - 1:SUBLANES, :]
        return jnp.broadcast_to(edge, shape)

    return lax.fori_loop(0, ng, step, carry)


def _lru_gates(xc, wa_ref, wx_ref, ba, bx, lam):
    xb = xc.astype(BF16)
    r = _sigmoid(_dot(xb, wa_ref[...]) + ba)
    ig = _sigmoid(_dot(xb, wx_ref[...]) + bx)
    sp = _softplus(-lam)
    la = -LRU_C * r * sp
    a = jnp.exp(la)
    mult = jnp.sqrt(_neg_expm1(2.0 * la, a * a))
    return r, ig, sp, a, mult


def _lru_fwd(xc, wa, wx, ba, bx, lam, h0, reverse, name, comms=()):
    t = xc.shape[0]
    tm = _tile(t, True)
    nt = t // tm
    tidx = (lambda i: (nt - 1 - i, 0)) if reverse else (lambda i: (i, 0))

    def body(x_ref, wa_ref, wx_ref, ba_ref, bx_ref, lam_ref, h0_ref, h_ref, a_s, b_s, c_s):
        @pl.when(pl.program_id(0) == 0)
        def _():
            c_s[...] = jnp.broadcast_to(h0_ref[...], c_s.shape)
        xv = x_ref[...]
        _, ig, _, a, mult = _lru_gates(xv, wa_ref, wx_ref, ba_ref[...], bx_ref[...], lam_ref[...])
        al, bl = _local_scan(a, mult * (ig * xv), reverse)
        a_s[...] = al
        b_s[...] = bl
        c_s[...] = _carry_scan(a_s, b_s, h_ref, c_s[...], reverse)

    vec = _full((1, LRU_W))
    mat = _full((LRU_W, LRU_W))
    (h,), couts = _call(body, name=name, grid=(nt,),
                        in_specs=[pl.BlockSpec((tm, LRU_W), tidx), mat, mat, vec, vec, vec, vec],
                        out_specs=[pl.BlockSpec((tm, LRU_W), tidx)], out_shape=[_sds((t, LRU_W))],
                        scratch_shapes=[pltpu.VMEM((tm, LRU_W), F32), pltpu.VMEM((tm, LRU_W), F32),
                                        pltpu.VMEM((SUBLANES, LRU_W), F32)],
                        sem=("arbitrary",), args=(xc, wa, wx, ba, bx, lam, h0), comms=comms)
    return (h, couts) if comms else h


def _lru_bwd(xc, wa, wx, ba, bx, lam, h, h0, dh, reverse, name, comms=()):
    t = xc.shape[0]
    tm = _tile(t, True)
    nt = t // tm
    n8 = tm // SUBLANES
    last8 = t // SUBLANES - 1
    tidx = (lambda i: (i, 0)) if reverse else (lambda i: (nt - 1 - i, 0))
    if reverse:
        halo = pl.BlockSpec((SUBLANES, LRU_W), lambda i: (jnp.minimum((i + 1) * n8, last8), 0))
    else:
        halo = pl.BlockSpec((SUBLANES, LRU_W), lambda i: (jnp.maximum((nt - 1 - i) * n8 - 1, 0), 0))

    def body(x_ref, wa_ref, wx_ref, ba_ref, bx_ref, lam_ref, h_ref, halo_ref, h0_ref, dh_ref,
             dx_ref, dpre_ref, dba_ref, dbx_ref, dlam_ref, dh0_ref, a_s, b_s, l_s, c_s, e_s):
        i = pl.program_id(0)

        @pl.when(i == 0)
        def _():
            c_s[...] = jnp.zeros_like(c_s)
            e_s[...] = jnp.zeros_like(e_s)
            dba_ref[...] = jnp.zeros_like(dba_ref)
            dbx_ref[...] = jnp.zeros_like(dbx_ref)
            dlam_ref[...] = jnp.zeros_like(dlam_ref)
        xv = x_ref[...]
        lam = lam_ref[...]
        r, ig, sp, a, mult = _lru_gates(xv, wa_ref, wx_ref, ba_ref[...], bx_ref[...], lam)
        hv = h_ref[...]
        rowi = lax.broadcasted_iota(jnp.int32, (tm, LRU_W), 0)
        edge_a = jnp.broadcast_to(e_s[0:1, :], (tm, LRU_W))
        h0b = jnp.broadcast_to(h0_ref[...], (tm, LRU_W))
        if reverse:
            a_sh = jnp.where(rowi == 0, edge_a, pltpu.roll(a, 1, 0))
            hin_edge = jnp.where(i == nt - 1, h0b, jnp.broadcast_to(halo_ref[0:1, :], (tm, LRU_W)))
            h_in = jnp.where(rowi == tm - 1, hin_edge, pltpu.roll(hv, tm - 1, 0))
        else:
            a_sh = jnp.where(rowi == tm - 1, edge_a, pltpu.roll(a, tm - 1, 0))
            hin_edge = jnp.where(i == nt - 1, h0b, jnp.broadcast_to(halo_ref[SUBLANES - 1:SUBLANES, :], (tm, LRU_W)))
            h_in = jnp.where(rowi == 0, hin_edge, pltpu.roll(hv, 1, 0))
        al, bl = _local_scan(a_sh, dh_ref[...], not reverse)
        a_s[...] = al
        b_s[...] = bl
        c_s[...] = _carry_scan(a_s, b_s, l_s, c_s[...], not reverse)
        e_s[...] = jnp.broadcast_to(a[tm - 1:tm, :] if reverse else a[0:1, :], e_s.shape)
        lmb = l_s[...]
        da = lmb * h_in
        ixc = ig * xv
        dmult = lmb * ixc
        dixc = lmb * mult
        dla = da * a - dmult * (a * a) / mult
        dpr = dla * (-LRU_C * sp) * r * (1.0 - r)
        dpi = dixc * xv * ig * (1.0 - ig)
        dprb = dpr.astype(BF16)
        dpib = dpi.astype(BF16)
        dpre_ref[:, 0:LRU_W] = dprb
        dpre_ref[:, LRU_W:2 * LRU_W] = dpib
        dx_ref[...] = dixc * ig + _dot_nt(dprb, wa_ref[...]) + _dot_nt(dpib, wx_ref[...])
        dba_ref[...] += _sum0(dpr)
        dbx_ref[...] += _sum0(dpi)
        dlam_ref[...] += _sum0(dla * (-LRU_C * r)) * (-_sigmoid(-lam))

        @pl.when(i == nt - 1)
        def _():
            al0 = a * lmb
            dh0_ref[...] = al0[tm - 1:tm, :] if reverse else al0[0:1, :]

    vec = _full((1, LRU_W))
    mat = _full((LRU_W, LRU_W))
    tile = pl.BlockSpec((tm, LRU_W), tidx)
    return _call(body, name=name, grid=(nt,),
                 in_specs=[tile, mat, mat, vec, vec, vec, tile, halo, vec, tile],
                 out_specs=[tile, pl.BlockSpec((tm, 2 * LRU_W), tidx), vec, vec, vec, vec],
                 out_shape=[_sds((t, LRU_W)), _sds((t, 2 * LRU_W), BF16), _sds((1, LRU_W)), _sds((1, LRU_W)),
                            _sds((1, LRU_W)), _sds((1, LRU_W))],
                 scratch_shapes=[pltpu.VMEM((tm, LRU_W), F32), pltpu.VMEM((tm, LRU_W), F32),
                                 pltpu.VMEM((tm, LRU_W), F32), pltpu.VMEM((SUBLANES, LRU_W), F32),
                                 pltpu.VMEM((SUBLANES, LRU_W), F32)],
                 sem=("arbitrary",), args=(xc, wa, wx, ba, bx, lam, h, h, h0, dh), comms=comms)


def _decay_tables(lg, reverse):
    ci = lax.broadcasted_iota(jnp.int32, (CHUNK, CHUNK), 0).astype(F32)
    mi = lax.broadcasted_iota(jnp.int32, (CHUNK, CHUNK), 1).astype(F32)
    if reverse:
        rel, pq, ps = mi - ci, CHUNK - ci, ci
    else:
        rel, pq, ps = ci - mi, ci + 1.0, CHUNK - 1.0 - ci
    relc = jnp.maximum(rel, 0.0)
    dm = jnp.where(rel >= 0, jnp.exp(lg * relc), 0.0)
    return relc, dm, jnp.exp(lg * pq), jnp.exp(lg * ps), jnp.exp(lg * float(CHUNK)), pq, ps


def _ret_fwd(proj, lgv, s0f, s0b, comms=()):
    t = proj.shape[0]
    n = t // CHUNK

    def one(q, k, v, lg, s_s, hh, o_ref, sp_ref, reverse):
        _, dm, wq, ws, g, _, _ = _decay_tables(lg, reverse)
        vb = v.astype(BF16)
        p = _dot_nt(q.astype(BF16), k.astype(BF16)) * dm
        s = s_s[hh]
        sp_ref[hh, 0] = s
        o_ref[:, DH * hh:DH * (hh + 1)] = _dot(p.astype(BF16), vb) + _dot((q * wq).astype(BF16), s.astype(BF16))
        s_s[hh] = g * s + _dot_tn((k * ws).astype(BF16), vb)

    def body(qf, kf, vf, qb, kb, vb, lg_ref, s0f_ref, s0b_ref, of_ref, ob_ref, spf_ref, spb_ref, sf_s, sb_s):
        @pl.when(pl.program_id(0) == 0)
        def _():
            sf_s[...] = s0f_ref[...]
            sb_s[...] = s0b_ref[...]
        for hh in range(HEADS):
            sl = slice(DH * hh, DH * (hh + 1))
            one(qf[:, sl], kf[:, sl], vf[:, sl], lg_ref[hh, 0:1, :], sf_s, hh, of_ref, spf_ref, False)
            one(qb[:, sl], kb[:, sl], vb[:, sl], lg_ref[hh, 1:2, :], sb_s, hh, ob_ref, spb_ref, True)

    blk = (CHUNK, RET_W)
    fw = [pl.BlockSpec(blk, lambda i, o=o: (i, o)) for o in range(3)]
    bw = [pl.BlockSpec(blk, lambda i, o=o: (n - 1 - i, o)) for o in range(3)]
    st = _full((HEADS, DH, DH))
    return _call(body, name="ret_fwd", grid=(n,),
                 in_specs=fw + bw + [_full((HEADS, 2, LANES)), st, st],
                 out_specs=[pl.BlockSpec(blk, lambda i: (i, 0)), pl.BlockSpec(blk, lambda i: (n - 1 - i, 0)),
                            pl.BlockSpec((HEADS, 1, DH, DH), lambda i: (0, i, 0, 0)),
                            pl.BlockSpec((HEADS, 1, DH, DH), lambda i: (0, n - 1 - i, 0, 0))],
                 out_shape=[_sds((t, RET_W)), _sds((t, RET_W)), _sds((HEADS, n, DH, DH)), _sds((HEADS, n, DH, DH))],
                 scratch_shapes=[pltpu.VMEM((HEADS, DH, DH), F32), pltpu.VMEM((HEADS, DH, DH), F32)],
                 sem=("arbitrary",), args=(proj, proj, proj, proj, proj, proj, lgv, s0f, s0b), comms=comms)


def _ret_bwd(proj, lgv, sgv, sprev, do, reverse, name, comms=()):
    t = proj.shape[0]
    n = t // CHUNK
    d = 1 if reverse else 0
    cidx = (lambda i: i) if reverse else (lambda i: n - 1 - i)

    def body(q_ref, k_ref, v_ref, lg_ref, sg_ref, s_ref, do_ref, dq_ref, dk_ref, dv_ref, ds0_ref, drd_ref, ds_s, acc_s):
        i = pl.program_id(0)

        @pl.when(i == 0)
        def _():
            ds_s[...] = jnp.zeros_like(ds_s)
            acc_s[...] = jnp.zeros_like(acc_s)
        for hh in range(HEADS):
            sl = slice(DH * hh, DH * (hh + 1))
            relc, dm, wq, ws, g, pq, ps = _decay_tables(lg_ref[hh, d:d + 1, :], reverse)
            q, k = q_ref[:, sl], k_ref[:, sl]
            qb, kb, vb = q.astype(BF16), k.astype(BF16), v_ref[:, sl].astype(BF16)
            p = _dot_nt(qb, kb) * dm
            s = s_ref[hh, 0]
            dob = do_ref[:, sl].astype(BF16)
            dsn = ds_s[hh]
            dsb = dsn.astype(BF16)
            dv_ref[:, sl] = _dot_tn(p.astype(BF16), dob) + _dot((k * ws).astype(BF16), dsb)
            dp = _dot_nt(dob, vb)
            dab = (dp * dm).astype(BF16)
            xq = _dot_nt(dob, s.astype(BF16))
            yk = _dot_nt(vb, dsb)
            dq_ref[:, sl] = _dot(dab, kb) + xq * wq
            dk_ref[:, sl] = _dot_tn(dab, qb) + yk * ws
            ds_s[hh] = g * dsn + _dot_tn((q * wq).astype(BF16), dob)
            part = (_sum0(dp * p * relc) + _sum0(xq * q * wq * pq) + _sum0(yk * k * ws * ps)
                    + _sum0(dsn * s) * g * float(CHUNK))
            acc_s[hh] += jnp.broadcast_to(part, (SUBLANES, LANES))

        @pl.when(i == n - 1)
        def _():
            ds0_ref[...] = ds_s[...]
            for hh in range(HEADS):
                tot = jnp.sum(acc_s[hh, 0:1, :], axis=1, keepdims=True)
                drd_ref[hh] = jnp.broadcast_to(tot, (SUBLANES, LANES)) * sg_ref[hh, d:d + 1, :]

    blk = (CHUNK, RET_W)
    qkv = [pl.BlockSpec(blk, lambda i, o=o: (cidx(i), o)) for o in range(3)]
    hc = pl.BlockSpec(blk, lambda i: (cidx(i), 0))
    lane = _full((HEADS, 2, LANES))
    return _call(body, name=name, grid=(n,),
                 in_specs=qkv + [lane, lane, pl.BlockSpec((HEADS, 1, DH, DH), lambda i: (0, cidx(i), 0, 0)), hc],
                 out_specs=[hc, hc, hc, _full((HEADS, DH, DH)), _full((HEADS, SUBLANES, LANES))],
                 out_shape=[_sds((t, RET_W))] * 3 + [_sds((HEADS, DH, DH)), _sds((HEADS, SUBLANES, LANES))],
                 scratch_shapes=[pltpu.VMEM((HEADS, DH, DH), F32), pltpu.VMEM((HEADS, SUBLANES, LANES), F32)],
                 sem=("arbitrary",), args=(proj, proj, proj, lgv, sgv, sprev, do), comms=comms)


def _ctx_weights(lg, l_len, reverse):
    pos = lax.broadcasted_iota(jnp.int32, (l_len, DH), 0).astype(F32)
    steps = pos if reverse else (l_len - 1.0 - pos)
    return jnp.exp(lg * steps), steps


def _ctx_state_fwd(projc, lgv):
    l_len = projc.shape[0]

    def body(k_ref, v_ref, lg_ref, sf_ref, sb_ref):
        k = k_ref[...]
        vb = v_ref[...].astype(BF16)
        for d, o_ref in ((0, sf_ref), (1, sb_ref)):
            w, _ = _ctx_weights(lg_ref[0, d:d + 1, :], l_len, d == 1)
            o_ref[0] = _dot_tn((k * w).astype(BF16), vb)

    st = pl.BlockSpec((1, DH, DH), lambda h: (h, 0, 0))
    return _pc(body, name="ctx_state_fwd", grid=(HEADS,),
               in_specs=[pl.BlockSpec((l_len, DH), lambda h: (0, HEADS + h)),
                         pl.BlockSpec((l_len, DH), lambda h: (0, 2 * HEADS + h)),
                         pl.BlockSpec((1, 2, LANES), lambda h: (h, 0, 0))],
               out_specs=[st, st], out_shape=[_sds((HEADS, DH, DH))] * 2,
               compiler_params=_params("arbitrary"))(projc, projc, lgv)


def _ctx_state_bwd(projc, lgv, sgv, dsf, dsb):
    l_len = projc.shape[0]

    def body(k_ref, v_ref, lg_ref, sg_ref, dsf_ref, dsb_ref, dk_ref, dv_ref, drd_ref):
        k = k_ref[...]
        vb = v_ref[...].astype(BF16)
        dk = jnp.zeros((l_len, DH), F32)
        dv = jnp.zeros((l_len, DH), F32)
        rows = []
        for d, ds_ref in ((0, dsf_ref), (1, dsb_ref)):
            w, steps = _ctx_weights(lg_ref[0, d:d + 1, :], l_len, d == 1)
            dsb16 = ds_ref[0].astype(BF16)
            dkw = _dot_nt(vb, dsb16)
            dk = dk + dkw * w
            dv = dv + _dot((k * w).astype(BF16), dsb16)
            tot = jnp.sum(_sum0(dkw * k * w * steps), axis=1, keepdims=True)
            rows.append(jnp.broadcast_to(tot, (1, LANES)) * sg_ref[0, d:d + 1, :])
        dk_ref[...] = dk
        dv_ref[...] = dv
        rid = lax.broadcasted_iota(jnp.int32, (SUBLANES, LANES), 0)
        drd_ref[0] = jnp.where(rid == 0, rows[0], jnp.where(rid == 1, rows[1], 0.0))

    st = pl.BlockSpec((1, DH, DH), lambda h: (h, 0, 0))
    lane = pl.BlockSpec((1, 2, LANES), lambda h: (h, 0, 0))
    hc = pl.BlockSpec((l_len, DH), lambda h: (0, h))
    return _pc(body, name="ctx_state_bwd", grid=(HEADS,),
               in_specs=[pl.BlockSpec((l_len, DH), lambda h: (0, HEADS + h)),
                         pl.BlockSpec((l_len, DH), lambda h: (0, 2 * HEADS + h)), lane, lane, st, st],
               out_specs=[hc, hc, pl.BlockSpec((1, SUBLANES, LANES), lambda h: (h, 0, 0))],
               out_shape=[_sds((l_len, RET_W)), _sds((l_len, RET_W)), _sds((HEADS, SUBLANES, LANES))],
               compiler_params=_params("arbitrary"))(projc, projc, lgv, sgv, dsf, dsb)


G_BLOCK = (3 * RET_W) // RET_W
GATE_BLOCK = (4 * RET_W + LRU_W) // LRU_W


def _head_norm(y):
    yc = y - jnp.mean(y, axis=-1, keepdims=True)
    rs = lax.rsqrt(jnp.mean(yc * yc, axis=-1, keepdims=True) + EPS)
    return yc * rs, rs


def _gelu_parts(z):
    th = jnp.tanh(GELU_K * (z + GELU_C * z * z * z))
    return 0.5 * z * (1.0 + th), th


def _mix_fwd(o_f, o_b, proj, hf, hb, w_out, x, g1):
    t = x.shape[0]
    tm = _tile(t, True)

    def body(of_ref, ob_ref, g_ref, gt_ref, hf_ref, hb_ref, w_ref, x_ref, g1_ref, x1_ref, cat_ref):
        o = of_ref[...] + ob_ref[...]
        g = g_ref[...]
        for hh in range(HEADS):
            sl = slice(DH * hh, DH * (hh + 1))
            nrm, _ = _head_norm(o[:, sl])
            gh = g[:, sl]
            cat_ref[:, sl] = (gh * _sigmoid(gh) * nrm).astype(BF16)
        gel, _ = _gelu_parts(gt_ref[...])
        cat_ref[:, RET_W:] = ((hf_ref[...] + hb_ref[...]) * gel).astype(BF16)
        x1_ref[...] = x_ref[...] + g1_ref[...] * _dot(cat_ref[...], w_ref[...])

    half = pl.BlockSpec((tm, RET_W), lambda i: (i, 0))
    big = pl.BlockSpec((tm, D_MODEL), lambda i: (i, 0))
    return _pc(body, name="mix_fwd", grid=(t // tm,),
               in_specs=[half, half, pl.BlockSpec((tm, RET_W), lambda i: (i, G_BLOCK)),
                         pl.BlockSpec((tm, LRU_W), lambda i: (i, GATE_BLOCK)), half, half,
                         _full((D_MODEL, D_MODEL)), big, _full((1, D_MODEL))],
               out_specs=[big, big], out_shape=[_sds((t, D_MODEL)), _sds((t, D_MODEL), BF16)],
               compiler_params=_params("arbitrary"))(o_f, o_b, proj, proj, hf, hb, w_out, x, g1)


def _mix_bwd(o_f, o_b, proj, hf, hb, w_out, cat, dx1, g1, comms=()):
    t = dx1.shape[0]
    tm = _tile(t, True)

    def body(of_ref, ob_ref, g_ref, gt_ref, hf_ref, hb_ref, w_ref, cat_ref, dx1_ref, g1_ref,
             do_ref, dhs_ref, dg_ref, dgt_ref, dyb_ref, dg1_ref):
        dx1v = dx1_ref[...]
        y = _dot(cat_ref[...], w_ref[...])

        @pl.when(pl.program_id(0) == 0)
        def _():
            dg1_ref[...] = jnp.zeros_like(dg1_ref)
        dg1_ref[...] += _sum0(dx1v * y)
        dyb = (g1_ref[...] * dx1v).astype(BF16)
        dyb_ref[...] = dyb
        dcat = _dot_nt(dyb, w_ref[...])
        o = of_ref[...] + ob_ref[...]
        g = g_ref[...]
        for hh in range(HEADS):
            sl = slice(DH * hh, DH * (hh + 1))
            nrm, rs = _head_norm(o[:, sl])
            gh = g[:, sl]
            sg = _sigmoid(gh)
            dret = dcat[:, sl]
            dg_ref[:, sl] = dret * nrm * (sg * (1.0 + gh * (1.0 - sg)))
            dn = dret * (gh * sg)
            dyc = rs * (dn - nrm * jnp.mean(dn * nrm, axis=-1, keepdims=True))
            do_ref[:, sl] = dyc - jnp.mean(dyc, axis=-1, keepdims=True)
        z = gt_ref[...]
        gel, th = _gelu_parts(z)
        dlru = dcat[:, RET_W:]
        dhs_ref[...] = dlru * gel
        dgel = 0.5 * (1.0 + th) + 0.5 * z * (1.0 - th * th) * GELU_K * (1.0 + 3.0 * GELU_C * z * z)
        dgt_ref[...] = dlru * (hf_ref[...] + hb_ref[...]) * dgel

    half = pl.BlockSpec((tm, RET_W), lambda i: (i, 0))
    big = pl.BlockSpec((tm, D_MODEL), lambda i: (i, 0))
    return _call(body, name="mix_bwd", grid=(t // tm,),
                 in_specs=[half, half, pl.BlockSpec((tm, RET_W), lambda i: (i, G_BLOCK)),
                           pl.BlockSpec((tm, LRU_W), lambda i: (i, GATE_BLOCK)), half, half,
                           _full((D_MODEL, D_MODEL)), big, big, _full((1, D_MODEL))],
                 out_specs=[half, half, half, half, big, _full((1, D_MODEL))],
                 out_shape=[_sds((t, RET_W))] * 4 + [_sds((t, D_MODEL), BF16), _sds((1, D_MODEL))],
                 scratch_shapes=[], sem=("arbitrary",), args=(o_f, o_b, proj, proj, hf, hb, w_out, cat, dx1, g1),
                 comms=comms)


def _mlp(x1, n2g, sh2, sc2, g2, fg, w1, w2, tgt):
    t = x1.shape[0]
    tm = _tile(t)
    hb_ = MLP_H // N_CHIP

    def body(x1_ref, n2g_ref, sh2_ref, sc2_ref, g2_ref, fg_ref, w1_hbm, w2_hbm, tgt_ref,
             dx1_ref, h2b_ref, ab_ref, dub_ref, dmb_ref, dsc_ref, dsh_ref, dg2_ref, dn2_ref, dfg_ref, loss_ref,
             w1_s, w2_s, r_s, sems):
        @pl.when(pl.program_id(0) == 0)
        def _():
            c1 = pltpu.make_async_copy(w1_hbm, w1_s, sems.at[0])
            c2 = pltpu.make_async_copy(w2_hbm, w2_s, sems.at[1])
            c1.start()
            c2.start()
            for r in (dsc_ref, dsh_ref, dg2_ref, dn2_ref, dfg_ref, loss_ref):
                r[...] = jnp.zeros_like(r)
            c1.wait()
            c2.wait()
        x1v = x1_ref[...]
        n2g, sc2, g2, fg = n2g_ref[...], sc2_ref[...], g2_ref[...], fg_ref[...]
        xh, _ = _rms(x1v)
        h2b = (xh * n2g * (1.0 + sc2) + sh2_ref[...]).astype(BF16)
        h2b_ref[...] = h2b
        m = jnp.zeros((tm, D_MODEL), F32)
        for j in range(N_CHIP):
            sl = slice(hb_ * j, hb_ * (j + 1))
            r = jnp.maximum(_dot(h2b, w1_s[j]), 0.0)
            r_s[:, sl] = r
            ab = (r * r).astype(BF16)
            ab_ref[:, sl] = ab
            m = m + _dot(ab, w2_s[j])
        x2 = x1v + g2 * m
        x2h, r2 = _rms(x2)
        err = x2h * fg - tgt_ref[...]
        loss_ref[...] += _sum0(err * err)
        dout = err * (1.0 / D_MODEL)
        dfg_ref[...] += _sum0(dout * x2h)
        dxh = dout * fg
        dx2 = r2 * (dxh - x2h * jnp.mean(dxh * x2h, axis=-1, keepdims=True))
        dg2_ref[...] += _sum0(dx2 * m)
        dmb = (g2 * dx2).astype(BF16)
        dmb_ref[...] = dmb
        dh2 = jnp.zeros((tm, D_MODEL), F32)
        for j in range(N_CHIP):
            sl = slice(hb_ * j, hb_ * (j + 1))
            dub = (_dot_nt(dmb, w2_s[j]) * (2.0 * r_s[:, sl])).astype(BF16)
            dub_ref[:, sl] = dub
            dh2 = dh2 + _dot_nt(dub, w1_s[j])
        dx, dn2_t, dsh_t, dsc_t = _norm_mod_bwd(x1v, n2g, sc2, dh2)
        dx1_ref[...] = dx2 + dx
        dn2_ref[...] += dn2_t
        dsh_ref[...] += dsh_t
        dsc_ref[...] += dsc_t

        @pl.when(pl.program_id(0) == t // tm - 1)
        def _():
            tot = jnp.sum(loss_ref[...], axis=1, keepdims=True) * (0.5 / D_MODEL)
            loss_ref[...] = jnp.broadcast_to(tot, loss_ref.shape)

    row = _full((1, D_MODEL))
    big = pl.BlockSpec((tm, D_MODEL), lambda i: (i, 0))
    wide = pl.BlockSpec((tm, MLP_H), lambda i: (i, 0))
    return _pc(body, name="mlp", grid=(t // tm,),
               in_specs=[big, row, row, row, row, row, ANY, ANY, big],
               out_specs=[big, big, wide, wide, big, row, row, row, row, row, row],
               out_shape=[_sds((t, D_MODEL)), _sds((t, D_MODEL), BF16), _sds((t, MLP_H), BF16), _sds((t, MLP_H), BF16),
                          _sds((t, D_MODEL), BF16)] + [_sds((1, D_MODEL))] * 6,
               scratch_shapes=[pltpu.VMEM(w1.shape, BF16), pltpu.VMEM(w2.shape, BF16), pltpu.VMEM((tm, MLP_H), F32),
                               pltpu.SemaphoreType.DMA((2,))],
               compiler_params=_params("arbitrary"))(x1, n2g, sh2, sc2, g2, fg, w1, w2, tgt)


def _tn(a, b, nj, a_blocked, b_blocked, name, extra=None, comms=()):
    t = a.shape[0]
    m = a.shape[1] // (nj if a_blocked else 1)
    n = b.shape[1] // (nj if b_blocked else 1)
    bk = 1024 if t % 1024 == 0 else (512 if t % 512 == 0 else t)
    nk = t // bk
    a_map = (lambda j, k: (k, j)) if a_blocked else (lambda j, k: (k, 0))
    b_map = (lambda j, k: (k, j)) if b_blocked else (lambda j, k: (k, 0))
    in_specs = [pl.BlockSpec((bk, m), a_map), pl.BlockSpec((bk, n), b_map)]
    args = [a, b]
    if extra is not None:
        a2, b2 = extra
        t2 = a2.shape[0]
        in_specs += [pl.BlockSpec((t2, m), (lambda j, k: (0, j)) if a_blocked else (lambda j, k: (0, 0))),
                     pl.BlockSpec((t2, n), (lambda j, k: (0, j)) if b_blocked else (lambda j, k: (0, 0)))]
        args += [a2, b2]

    def body(*refs):
        a_ref, b_ref = refs[0], refs[1]
        o_ref, acc = refs[-2], refs[-1]
        k = pl.program_id(1)

        @pl.when(k == 0)
        def _():
            acc[...] = jnp.zeros_like(acc)
        acc[...] += _dot_tn(a_ref[...].astype(BF16), b_ref[...].astype(BF16))

        @pl.when(k == nk - 1)
        def _():
            if extra is not None:
                acc[...] += _dot_tn(refs[2][...].astype(BF16), refs[3][...].astype(BF16))
            o_ref[0] = acc[...]

    (out,), couts = _call(body, name=name, grid=(nj, nk), in_specs=in_specs,
                          out_specs=[pl.BlockSpec((1, m, n), lambda j, k: (j, 0, 0))], out_shape=[_sds((nj, m, n))],
                          scratch_shapes=[pltpu.VMEM((m, n), F32)], sem=("arbitrary", "arbitrary"), args=args,
                          comms=comms)
    return (out, couts) if comms else out


def _block_diag(w):
    eye = jnp.eye(LRU_BLOCKS, dtype=F32)
    return (w[:, :, None, :] * eye[:, None, :, None]).reshape(LRU_W, LRU_W).astype(BF16)


def _diag_blocks(mat):
    eye = jnp.eye(LRU_BLOCKS, dtype=jnp.bool_)
    m4 = mat.reshape(LRU_BLOCKS, LRU_BD, LRU_BLOCKS, LRU_BD)
    return jnp.sum(jnp.where(eye[:, None, :, None], m4, 0.0), axis=2)


def _pad_rows(v, width=LANES):
    flat = v.reshape(-1).astype(F32)
    tile = SUBLANES * width
    n = -(-flat.shape[0] // tile) * tile
    return jnp.pad(flat, (0, n - flat.shape[0])).reshape(n // width, width)


class _Slab:
    def __init__(self):
        self.parts, self.meta, self.rows = [], {}, 0

    def add(self, name, v):
        p = _pad_rows(v)
        self.meta[name] = (self.rows, p.shape[0], v.shape)
        self.parts.append(p)
        self.rows += p.shape[0]

    def build(self):
        return jnp.concatenate(self.parts, axis=0)

    def take(self, slab, name, shape=None):
        r0, nr, shp = self.meta[name]
        shp = shp if shape is None else shape
        return slab[r0:r0 + nr].reshape(-1)[:math.prod(shp)].reshape(shp)


def _lane_rep(v8):
    return jnp.broadcast_to(v8.reshape(SUBLANES, 1), (SUBLANES, LANES))


def kernel(x, c, ctx, c_ctx, w_ada, b_ada, norm1_g, norm2_g, w_in, ret_decay, conv_w, conv_b, lru_wa, lru_ba, lru_wx, lru_bx, lru_lambda, w_out, w_mlp1, w_mlp2, final_g, loss_target, m_c_ctx, m_w_ada, m_b_ada, m_norm1_g, m_norm2_g, m_w_in, m_ret_decay, m_conv_w, m_conv_b, m_lru_wa, m_lru_ba, m_lru_wx, m_lru_bx, m_lru_lambda, m_w_out, m_w_mlp1, m_w_mlp2, m_final_g, v_c_ctx, v_w_ada, v_b_ada, v_norm1_g, v_norm2_g, v_w_in, v_ret_decay, v_conv_w, v_conv_b, v_lru_wa, v_lru_ba, v_lru_wx, v_lru_bx, v_lru_lambda, v_w_out, v_w_mlp1, v_w_mlp2, v_final_g):
    ax, ay, ac = lax.axis_index("x"), lax.axis_index("y"), lax.axis_index("c")
    chip = 2 * ax + ay
    dev = 4 * ax + 2 * ay + ac
    c_idx = ac.reshape(1).astype(jnp.int32)
    j_idx = chip.reshape(1).astype(jnp.int32)

    xt = x[0]
    t_len = xt.shape[0]
    ctxt = ctx[0]
    l_len = ctxt.shape[0]
    tgt = loss_target[0]
    ada_n = w_ada.shape[2]

    def my_half(w2d):
        r = w2d.shape[0] // 2
        return lax.dynamic_slice_in_dim(w2d, ac * r, r, axis=0).astype(BF16)

    pad8 = lambda a: jnp.pad(a, ((0, SUBLANES - a.shape[0]), (0, 0)))
    small = jnp.concatenate([pad8(conv_w[0]), pad8(lru_ba[0]), pad8(lru_bx[0]), pad8(lru_lambda[0])], axis=0)
    gw_in, c_all, small_all = _all_gather([my_half(w_in[0]), pad8(c), small], "gather_head")
    w4 = gw_in.reshape(N_CHIP, D_MODEL, IN_COLS // N_CHIP)

    a16, lgv, sgv = _prep(c_all[:, 0, :], c_ctx, ret_decay[0])
    b_shard = lax.dynamic_slice_in_dim(b_ada, chip * ada_n, ada_n, axis=1)
    (mod_parts,) = _all_gather([_mod_fwd(a16, w_ada[0], b_shard)], "gather_mod")
    mod_all = mod_parts[0::2].transpose(1, 0, 2).reshape(16, N_CHIP * ada_n)
    mod_me = lax.dynamic_slice_in_dim(mod_all, dev, 1, axis=0)
    sh1, sc1, g1, sh2, sc2, g2 = [mod_me[:, D_MODEL * k:D_MODEL * (k + 1)] for k in range(N_MOD)]
    csh1, csc1 = mod_all[8:9, 0:D_MODEL], mod_all[8:9, D_MODEL:2 * D_MODEL]

    cos2, sin2 = _rotary_tables(t_len)
    cos_c, sin_c = jnp.ones((l_len, DH), F32), jnp.zeros((l_len, DH), F32)
    n1g, n2g = norm1_g, norm2_g
    fg = final_g.reshape(1, D_MODEL)

    small_full = small_all[0::2].transpose(1, 0, 2).reshape(4 * SUBLANES, LRU_W)
    cw = small_full[0:4]
    cb = conv_b
    ba_f, ba_b = small_full[8:9], small_full[9:10]
    bx_f, bx_b = small_full[16:17], small_full[17:18]
    lam_f, lam_b = small_full[24:25], small_full[25:26]
    wa_f, wa_b = _block_diag(lru_wa[0, 0]), _block_diag(lru_wa[0, 1])
    wx_f, wx_b = _block_diag(lru_wx[0, 0]), _block_diag(lru_wx[0, 1])
    zero_h = jnp.zeros((1, LRU_W), F32)

    projc, hcb16 = _inproj_fwd(ctxt, n1g, csh1, csc1, w4, cos_c, sin_c, "inproj_fwd_ctx")
    s_f, s_b = _ctx_state_fwd(projc, lgv)
    xcc = _conv_fwd(projc, cw, cb, "conv_fwd_ctx")
    hcf = _lru_fwd(xcc, wa_f, wx_f, ba_f, bx_f, lam_f, zero_h, False, "lru_fwd_ctx_f")
    hcbk = _lru_fwd(xcc, wa_b, wx_b, ba_b, bx_b, lam_b, zero_h, True, "lru_fwd_ctx_b")
    lru_sf, lru_sb = hcf[l_len - 1:l_len], hcbk[0:1]

    proj, hb16 = _inproj_fwd(xt, n1g, sh1, sc1, w4, cos2, sin2, "inproj_fwd")
    (o_f, o_b, spf, spb), ((gw_1,),) = _ret_fwd(proj, lgv, s_f, s_b, comms=(_AllGather([my_half(w_mlp1[0])]),))
    xcl = _conv_fwd(proj, cw, cb, "conv_fwd")
    hf, ((gw_2,),) = _lru_fwd(xcl, wa_f, wx_f, ba_f, bx_f, lam_f, lru_sf, False, "lru_fwd_f",
                             comms=(_AllGather([my_half(w_mlp2[0])]),))
    hbk, ((gw_out,),) = _lru_fwd(xcl, wa_b, wx_b, ba_b, bx_b, lam_b, lru_sb, True, "lru_fwd_b",
                                comms=(_AllGather([my_half(w_out[0])]),))
    wo = gw_out.reshape(D_MODEL, D_MODEL)
    w1 = gw_1.reshape(N_CHIP, D_MODEL, MLP_H // N_CHIP)
    w2 = gw_2.reshape(N_CHIP, MLP_H // N_CHIP, D_MODEL)
    x1, cat = _mix_fwd(o_f, o_b, proj, hf, hbk, wo, xt, g1)

    (dx1, h2b, ab, dub, dmb, dsc2, dsh2, dg2, dn2g, dfg, lossv) = _mlp(x1, n2g, sh2, sc2, g2, fg, w1, w2, tgt)
    gw_mlp1 = _tn(h2b, dub, N_CHIP, False, True, "grad_w_mlp1")
    b_1 = gw_mlp1.reshape(N_DEV, D_MODEL // 2, MLP_H // N_CHIP)
    gw_mlp2, ((r_1,),) = _tn(ab, dmb, N_CHIP, True, False, "grad_w_mlp2", comms=(_pair_exchange([b_1]),))

    jc_idx = jnp.concatenate([j_idx, c_idx])
    b_2 = gw_mlp2.reshape(N_DEV, MLP_H // N_DEV, D_MODEL)
    (do, dhs, dg, dgate, dyb, dg1), ((r_2,),) = _mix_bwd(
        o_f, o_b, proj, hf, hbk, wo, cat, dx1, g1, comms=(_pair_exchange([b_2]),))
    gw_o = _tn(cat, dyb, 1, False, False, "grad_w_out")
    b_o = gw_o.reshape(N_DEV, D_MODEL // N_DEV, D_MODEL)
    p_1, pb_1 = _pair_add(b_1, r_1, c_idx, "rs_pair_add_w_mlp1")
    p_2, pb_2 = _pair_add(b_2, r_2, c_idx, "rs_pair_add_w_mlp2")

    (dq_f, dk_f, dv_f, ds_f, drd_f), ((q_1,), (r_o,)) = _ret_bwd(
        proj, lgv, sgv, spf, do, False, "ret_bwd_f", comms=(_chip_exchange([pb_1]), _pair_exchange([b_o])))
    p_o, pb_o = _pair_add(b_o, r_o, c_idx, "rs_pair_add_w_out")
    h_1 = _chip_add(p_1, q_1, jc_idx, "rs_chip_add_w_mlp1")

    (dq_b, dk_b, dv_b, ds_b, drd_b), ((q_2,), (f_1,)) = _ret_bwd(
        proj, lgv, sgv, spb, do, True, "ret_bwd_b", comms=(_chip_exchange([pb_2]), _pair_gather([h_1])))
    h_2 = _chip_add(p_2, q_2, jc_idx, "rs_chip_add_w_mlp2")

    (dxc_f, dpre_f, dba_f, dbx_f, dlam_f, dh0_f), ((q_o,), (f_2,)) = _lru_bwd(
        xcl, wa_f, wx_f, ba_f, bx_f, lam_f, hf, lru_sf, dhs, False, "lru_bwd_f",
        comms=(_chip_exchange([pb_o]), _pair_gather([h_2])))
    h_o = _chip_add(p_o, q_o, jc_idx, "rs_chip_add_w_out")
    (dxc_b, dpre_b, dba_b, dbx_b, dlam_b, dh0_b), ((f_o,),) = _lru_bwd(
        xcl, wa_b, wx_b, ba_b, bx_b, lam_b, hbk, lru_sb, dhs, True, "lru_bwd_b", comms=(_pair_gather([h_o]),))
    dxr, dcw, dcb = _conv_bwd(dxc_f, dxc_b, proj, cw, "conv_bwd")
    grad_x, dpb, dn1g, dsh1, dsc1 = _inproj_bwd(
        xt, n1g, sh1, sc1, w4, cos2, sin2, [dq_f, dq_b, dk_f, dk_b, dv_f, dv_b, dg, dxr, dgate], dx1, "inproj_bwd")

    dkc, dvc, drd_c = _ctx_state_bwd(projc, lgv, sgv, ds_f, ds_b)
    zc = jnp.zeros((l_len, LRU_W), F32)
    dhc_f = lax.dynamic_update_slice(zc, dh0_f, (l_len - 1, 0))
    dhc_b = lax.dynamic_update_slice(zc, dh0_b, (0, 0))
    (dxcc_f, dprec_f, dbac_f, dbxc_f, dlamc_f, _), _ = _lru_bwd(
        xcc, wa_f, wx_f, ba_f, bx_f, lam_f, hcf, zero_h, dhc_f, False, "lru_bwd_ctx_f")
    (dxcc_b, dprec_b, dbac_b, dbxc_b, dlamc_b, _), _ = _lru_bwd(
        xcc, wa_b, wx_b, ba_b, bx_b, lam_b, hcbk, zero_h, dhc_b, True, "lru_bwd_ctx_b")
    dxrc, dcw_c, dcb_c = _conv_bwd(dxcc_f, dxcc_b, projc, cw, "conv_bwd_ctx")
    zr = jnp.zeros((l_len, RET_W), F32)
    _, dpbc, dn1g_c, dcsh1, dcsc1 = _inproj_bwd(
        ctxt, n1g, csh1, csc1, w4, cos_c, sin_c, [zr, zr, dkc, zr, dvc, zr, zr, dxrc, zr],
        jnp.zeros((l_len, D_MODEL), F32), "inproj_bwd_ctx")

    gw_i = _tn(hb16, dpb, N_CHIP, False, True, "grad_w_in", extra=(hcb16, dpbc))
    b_i = gw_i.reshape(N_DEV, D_MODEL // 2, IN_COLS // N_CHIP)
    gwa_f, ((r_i,),) = _tn(xcl, dpre_f, 2, False, True, "grad_lru_gates_f", extra=(xcc, dprec_f),
                           comms=(_pair_exchange([b_i]),))
    p_i, pb_i = _pair_add(b_i, r_i, c_idx, "rs_pair_add_w_in")
    gwa_b, ((q_i,),) = _tn(xcl, dpre_b, 2, False, True, "grad_lru_gates_b", extra=(xcc, dprec_b),
                           comms=(_chip_exchange([pb_i]),))
    h_i = _chip_add(p_i, q_i, jc_idx, "rs_chip_add_w_in")
    g_out, g_1, g_2 = _shard_of(f_o), _shard_of(f_1), _shard_of(f_2)
    big = {}
    (d_, mn, vn), ((f_i,),) = _adamw(w_mlp1[0], g_1, m_w_mlp1[0], v_w_mlp1[0], "adamw_w_mlp1",
                                     comms=(_pair_gather([h_i]),))
    big["w_mlp1"] = (g_1[None], d_[None], mn[None], vn[None])
    g_in = _shard_of(f_i)
    for nm, w, g, m, v in (("w_in", w_in, g_in, m_w_in, v_w_in), ("w_out", w_out, g_out, m_w_out, v_w_out),
                           ("w_mlp2", w_mlp2, g_2, m_w_mlp2, v_w_mlp2)):
        d_, mn, vn = _adamw(w[0], g, m[0], v[0], "adamw_" + nm)
        big[nm] = (g[None], d_[None], mn[None], vn[None])

    part = _Slab()
    part.add("loss", lossv)
    part.add("dmod", jnp.concatenate([dsh1, dsc1, dg1, dsh2, dsc2, dg2], axis=1))
    part.add("dmodc", jnp.concatenate([dcsh1, dcsc1], axis=1))
    part.add("norm1_g", dn1g + dn1g_c)
    part.add("norm2_g", dn2g)
    part.add("final_g", dfg)
    part.add("ret_decay", jnp.concatenate([drd_f[:, 0, :] + drd_c[:, 0, :], drd_b[:, 0, :] + drd_c[:, 1, :]], axis=0))
    part.add("conv_w", dcw + dcw_c)
    part.add("conv_b", dcb + dcb_c)
    part.add("lru_wa", jnp.stack([_diag_blocks(gwa_f[0]), _diag_blocks(gwa_b[0])]))
    part.add("lru_wx", jnp.stack([_diag_blocks(gwa_f[1]), _diag_blocks(gwa_b[1])]))
    part.add("lru_ba", jnp.concatenate([dba_f + dbac_f, dba_b + dbac_b], axis=0))
    part.add("lru_bx", jnp.concatenate([dbx_f + dbxc_f, dbx_b + dbxc_b], axis=0))
    part.add("lru_lambda", jnp.concatenate([dlam_f + dlamc_f, dlam_b + dlamc_b], axis=0))
    (parts_all,) = _all_gather([part.build()], "gather_small_grads")
    tot = _sum_devices(parts_all, "sum_small_grads")

    loss = part.take(tot, "loss")[0, 0]
    dmod_all = parts_all[:, part.meta["dmod"][0]:part.meta["dmod"][0] + part.meta["dmod"][1]].reshape(N_DEV, -1)
    dmodc_tot = jnp.pad(part.take(tot, "dmodc").reshape(1, -1), ((0, 0), (0, (N_MOD - 2) * D_MODEL)))
    dmod_tot = part.take(tot, "dmod").reshape(1, -1)
    grad_b_ada = dmod_tot + dmodc_tot

    cols = lambda a: lax.dynamic_slice_in_dim(a, chip * ada_n, ada_n, axis=1)
    b128 = jnp.pad(jnp.concatenate([cols(dmod_all), jnp.pad(cols(dmodc_tot), ((0, SUBLANES - 1), (0, 0)))], axis=0),
                   ((0, LANES - 2 * SUBLANES), (0, 0)))
    g_ada = _ada_grad(jnp.pad(a16.T, ((0, 0), (0, LANES - 16))), b128)
    d_ada, m_ada, v_ada = _adamw(w_ada[0], g_ada, m_w_ada[0], v_w_ada[0], "adamw_w_ada")

    dmc8 = jnp.pad(cols(dmodc_tot), ((0, SUBLANES - 1), (0, 0)))
    (cparts,) = _all_gather([_cctx_partial(dmc8, w_ada[0])], "gather_cctx")
    g_cc, d_cc, m_cc, v_cc = _cctx_final(cparts, c_ctx, m_c_ctx, v_c_ctx)

    def shard_cols(a, width=LANES):
        return lax.dynamic_slice_in_dim(a, chip * width, width, axis=a.ndim - 1)

    grads = {
        "b_ada": grad_b_ada,
        "norm1_g": part.take(tot, "norm1_g"),
        "norm2_g": part.take(tot, "norm2_g"),
        "ret_decay": part.take(tot, "ret_decay", (SUBLANES, LANES)),
        "conv_w": shard_cols(part.take(tot, "conv_w")),
        "conv_b": part.take(tot, "conv_b"),
        "lru_wa": part.take(tot, "lru_wa"),
        "lru_ba": shard_cols(part.take(tot, "lru_ba")),
        "lru_wx": part.take(tot, "lru_wx"),
        "lru_bx": shard_cols(part.take(tot, "lru_bx")),
        "lru_lambda": shard_cols(part.take(tot, "lru_lambda")),
        "final_g": part.take(tot, "final_g"),
    }
    params = {
        "b_ada": (b_ada, m_b_ada, v_b_ada), "norm1_g": (norm1_g, m_norm1_g, v_norm1_g),
        "norm2_g": (norm2_g, m_norm2_g, v_norm2_g), "ret_decay": (ret_decay, m_ret_decay, v_ret_decay),
        "conv_w": (conv_w, m_conv_w, v_conv_w), "conv_b": (conv_b, m_conv_b, v_conv_b),
        "lru_wa": (lru_wa, m_lru_wa, v_lru_wa), "lru_ba": (lru_ba, m_lru_ba, v_lru_ba),
        "lru_wx": (lru_wx, m_lru_wx, v_lru_wx), "lru_bx": (lru_bx, m_lru_bx, v_lru_bx),
        "lru_lambda": (lru_lambda, m_lru_lambda, v_lru_lambda), "final_g": (final_g, m_final_g, v_final_g),
    }
    slabs = [_Slab() for _ in range(4)]
    for nm, gval in grads.items():
        vals = (gval,) + params[nm]
        for s, val in zip(slabs, vals):
            if nm == "ret_decay" and val.shape != (SUBLANES, LANES):
                val = _lane_rep(val.reshape(-1))
            s.add(nm, val)
    gs, ws, ms, vs = [s.build() for s in slabs]
    ds_, mns, vns = _adamw(ws, gs, ms, vs, "adamw_small")
    small_out = {}
    for nm in grads:
        shp = params[nm][0].shape
        if nm == "ret_decay":
            unp = lambda sl: slabs[0].take(sl, nm, (SUBLANES, LANES))[:, 0].reshape(shp)
        else:
            unp = lambda sl: slabs[0].take(sl, nm, shp)
        small_out[nm] = (unp(gs), unp(ds_), unp(mns), unp(vns))
    small_out["c_ctx"] = tuple(a.reshape(D_MODEL) for a in (g_cc, d_cc, m_cc, v_cc))
    small_out["w_ada"] = (g_ada[None], d_ada[None], m_ada[None], v_ada[None])
    small_out.update(big)

    order = ["c_ctx", "w_ada", "b_ada", "norm1_g", "norm2_g", "w_in", "ret_decay", "conv_w", "conv_b", "lru_wa", "lru_ba",
             "lru_wx", "lru_bx", "lru_lambda", "w_out", "w_mlp1", "w_mlp2", "final_g"]
    outs = [loss, grad_x[None]]
    for k in range(4):
        outs += [small_out[nm][k] for nm in order]
    return tuple(outs)
```

```python
import math

import jax
import jax.numpy as jnp
from jax import lax
from jax.experimental import pallas as pl
from jax.experimental.pallas import tpu as pltpu

F32 = jnp.float32
BF16 = jnp.bfloat16

D_MODEL = 1024
HEADS = 4
DH = 128
CHUNK = 128
RET_W = HEADS * DH
LRU_W = 512
LRU_BLOCKS = 8
LRU_BD = LRU_W // LRU_BLOCKS
LRU_C = 8.0
IN_COLS = 4 * RET_W + 2 * LRU_W
MLP_H = 4 * D_MODEL
N_MOD = 6
GRID_W = 64
ROPE_BASE = 10000.0
K_SCALE = DH ** -0.5
EPS = 1e-6
GELU_K = math.sqrt(2.0 / math.pi)
GELU_C = 0.044715

ADAM_LR = 0.001
ADAM_B1 = 0.9
ADAM_B2 = 0.999
ADAM_EPS = 1e-08
ADAM_WD = 0.01
ADAM_STEP = 10

N_DEV = 8
N_CHIP = 4
SUBLANES = 8
LANES = 128
VMEM_LIMIT_V7X = 56 * 1024 * 1024
MESH = pl.DeviceIdType.MESH
ANY = pl.BlockSpec(memory_space=pl.ANY)


def _pc(body, **kw):
    return pl.pallas_call(body, **kw)


def _params(*sem):
    return pltpu.CompilerParams(dimension_semantics=sem if sem else None, vmem_limit_bytes=VMEM_LIMIT_V7X)


def _tile(t, big=False):
    if big and t >= 1024:
        return 512
    return 256 if t >= 256 else t


def _sds(shape, dtype=F32):
    return jax.ShapeDtypeStruct(tuple(shape), dtype)


def _full(shape):
    nd = len(shape)
    return pl.BlockSpec(tuple(shape), lambda *_: (0,) * nd)


def _sigmoid(x):
    return 1.0 / (1.0 + jnp.exp(-x))


def _log1p_pos(y):
    s = y * (1.0 - y * (0.5 - y * (1.0 / 3.0 - y * (0.25 - y * (0.2 - y / 6.0)))))
    return jnp.where(y < 0.03, s, jnp.log(1.0 + y))


def _softplus(z):
    return jnp.maximum(z, 0.0) + _log1p_pos(jnp.exp(-jnp.abs(z)))


def _neg_expm1(x, exp_x):
    t = x * (1.0 + x * (0.5 + x * (1.0 / 6.0 + x * (1.0 / 24.0 + x * (1.0 / 120.0 + x * (1.0 / 720.0 + x / 5040.0))))))
    return -jnp.where(x > -0.25, t, exp_x - 1.0)


def _rms(x):
    r = lax.rsqrt(jnp.mean(x * x, axis=-1, keepdims=True) + EPS)
    return x * r, r


def _dot(a, b):
    return jnp.dot(a, b, preferred_element_type=F32)


def _dot_nt(a, b):
    return lax.dot_general(a, b, (((1,), (1,)), ((), ())), preferred_element_type=F32)


def _dot_tn(a, b):
    return lax.dot_general(a, b, (((0,), (0,)), ((), ())), preferred_element_type=F32)


def _sum0(x):
    return jnp.sum(x, axis=0, keepdims=True)


def _norm_mod_bwd(x, g, sc, dh):
    xh, r = _rms(x)
    hn = xh * g
    dhn = dh * (1.0 + sc)
    dxh = dhn * g
    dx = r * (dxh - xh * jnp.mean(dxh * xh, axis=-1, keepdims=True))
    return dx, _sum0(dhn * xh), _sum0(dh), _sum0(dh * hn)


def _dev_index(p):
    return 4 * p[0] + 2 * p[1] + p[2]


def _mesh_pos():
    return lax.axis_index("x"), lax.axis_index("y"), lax.axis_index("c")


class _AllGather:
    def __init__(self, arrs):
        n = len(arrs)
        self.arrays = list(arrs)
        self.out_shapes = [_sds((N_DEV,) + a.shape, a.dtype) for a in arrs]
        self.scratch = ([pltpu.VMEM(a.shape, a.dtype) for a in arrs]
                        + [pltpu.SemaphoreType.DMA((7 * n,)), pltpu.SemaphoreType.DMA((7 * n,)),
                           pltpu.SemaphoreType.DMA((n,))])
        self.aliases = {}

    def _parts(self, ins, outs, scr):
        n = len(self.arrays)
        stage = scr[:n]
        send_sems, recv_sems, local_sems = scr[n:]
        x, y, c = _mesh_pos()
        me, sib = (x, y, c), (x, y, 1 - c)
        chips = [(1 - x, y), (x, 1 - y), (1 - x, 1 - y)]

        def copy(t, k, block, to, own=False):
            dst = outs[t].at[_dev_index(block)]
            return pltpu.make_async_remote_copy(
                src_ref=ins[t] if own else dst, dst_ref=dst,
                send_sem=send_sems.at[7 * t + k], recv_sem=recv_sems.at[7 * t + k],
                device_id=to, device_id_type=MESH)

        first = []
        for t in range(n):
            first.append(copy(t, 0, me, sib, own=True))
            for j, ch in enumerate(chips):
                first.append(copy(t, 1 + j, me, (*ch, c), own=True))
        stage_in = [pltpu.make_async_copy(ins[t], stage[t], local_sems.at[t]) for t in range(n)]
        mine = [pltpu.make_async_copy(stage[t], outs[t].at[_dev_index(me)], local_sems.at[t]) for t in range(n)]
        return n, c, me, sib, chips, copy, first, stage_in, mine

    def start(self, ins, outs, scr):
        n, _, _, _, _, _, first, stage_in, mine = self._parts(ins, outs, scr)
        for cp in stage_in:
            cp.start()
        for cp in first:
            cp.start()
        for t in range(n):
            stage_in[t].wait()
            mine[t].start()

    def finish(self, ins, outs, scr):
        n, c, me, sib, chips, copy, first, _, mine = self._parts(ins, outs, scr)
        passed = []
        for j, ch in enumerate(chips):
            for t in range(n):
                copy(t, 1 + j, (*ch, c), me).wait_recv()
                p = copy(t, 4 + j, (*ch, c), sib)
                p.start()
                passed.append(p)
        for t in range(n):
            copy(t, 0, sib, me).wait_recv()
            for j, ch in enumerate(chips):
                copy(t, 4 + j, (*ch, 1 - c), me).wait_recv()
        for cp in first + passed:
            cp.wait_send()
        for cp in mine:
            cp.wait()


class _Exchange:
    def __init__(self, arrays, out_shapes, plan, n_copies, aliases=None):
        self.arrays = list(arrays)
        self.out_shapes = list(out_shapes)
        self.plan = plan
        self.scratch = [pltpu.SemaphoreType.DMA((n_copies,)), pltpu.SemaphoreType.DMA((n_copies,))]
        self.aliases = aliases or {}

    def _copies(self, ins, outs, scr):
        send_sems, recv_sems = scr
        snd, rcv = [], []
        for i, (src, dst, peer, lands) in enumerate(self.plan(ins, outs, _mesh_pos())):
            kw = dict(send_sem=send_sems.at[i], recv_sem=recv_sems.at[i], device_id=peer, device_id_type=MESH)
            snd.append(pltpu.make_async_remote_copy(src_ref=src, dst_ref=dst, **kw))
            rcv.append(pltpu.make_async_remote_copy(src_ref=src, dst_ref=lands, **kw))
        return snd, rcv

    def start(self, ins, outs, scr):
        for cp in self._copies(ins, outs, scr)[0]:
            cp.start()

    def finish(self, ins, outs, scr):
        snd, rcv = self._copies(ins, outs, scr)
        for cp in rcv:
            cp.wait_recv()
        for cp in snd:
            cp.wait_send()


def _pair_exchange(grads):
    n = len(grads)

    def plan(ins, outs, pos):
        x, y, c = pos
        return [(ins[t].at[2 * j + (1 - c)], outs[t].at[j], (x, y, 1 - c), outs[t].at[j])
                for t in range(n) for j in range(N_CHIP)]

    return _Exchange(grads, [_sds((N_CHIP,) + g.shape[1:], g.dtype) for g in grads], plan, N_CHIP * n)


def _chip_exchange(parts):
    n = len(parts)

    def plan(ins, outs, pos):
        x, y, c = pos
        chips = [(1 - x, y), (x, 1 - y), (1 - x, 1 - y)]
        return [(ins[t].at[2 * ch[0] + ch[1]], outs[t].at[k], (*ch, c), outs[t].at[k])
                for t in range(n) for k, ch in enumerate(chips)]

    return _Exchange(parts, [_sds((3,) + p.shape[1:], p.dtype) for p in parts], plan, 3 * n)


def _pair_gather(bufs):
    n = len(bufs)

    def plan(ins, outs, pos):
        x, y, c = pos
        return [(ins[t].at[c], outs[t].at[c], (x, y, 1 - c), outs[t].at[1 - c]) for t in range(n)]

    return _Exchange(bufs, [_sds(b.shape, b.dtype) for b in bufs], plan, n, aliases={t: t for t in range(n)})


def _run_comm(comm, name):
    n_in, n_out = len(comm.arrays), len(comm.out_shapes)

    def body(*refs):
        ins, outs, scr = refs[:n_in], refs[n_in:n_in + n_out], refs[n_in + n_out:]
        comm.start(ins, outs, scr)
        comm.finish(ins, outs, scr)

    outs = _pc(body, name=name, out_shape=comm.out_shapes, in_specs=[ANY] * n_in, out_specs=[ANY] * n_out,
               input_output_aliases=dict(comm.aliases), scratch_shapes=comm.scratch,
               compiler_params=_params())(*comm.arrays)
    return list(outs)


def _all_gather(arrs, name):
    return _run_comm(_AllGather(arrs), name)


def _call(body, *, name, grid, in_specs, out_specs, out_shape, scratch_shapes, sem, args, comms=()):
    n_in, n_out, n_scr = len(in_specs), len(out_specs), len(scratch_shapes)
    c_in = [len(cm.arrays) for cm in comms]
    c_out = [len(cm.out_shapes) for cm in comms]
    c_scr = [len(cm.scratch) for cm in comms]
    aliases = {}
    for k, cm in enumerate(comms):
        for a, b in cm.aliases.items():
            aliases[n_in + sum(c_in[:k]) + a] = n_out + sum(c_out[:k]) + b

    def split(refs, counts):
        out, pos = [], 0
        for cnt in counts:
            out.append(refs[pos:pos + cnt])
            pos += cnt
        return out

    def wrapped(*refs):
        ins = refs[:n_in + sum(c_in)]
        outs = refs[len(ins):len(ins) + n_out + sum(c_out)]
        scr = refs[len(ins) + len(outs):]
        cins, couts, cscr = split(ins[n_in:], c_in), split(outs[n_out:], c_out), split(scr[n_scr:], c_scr)
        if comms:
            first = pl.program_id(0) == 0
            last = pl.program_id(0) == grid[0] - 1
            for k in range(1, len(grid)):
                first = jnp.logical_and(first, pl.program_id(k) == 0)
                last = jnp.logical_and(last, pl.program_id(k) == grid[k] - 1)

            @pl.when(first)
            def _():
                for k, cm in enumerate(comms):
                    cm.start(cins[k], couts[k], cscr[k])
        body(*ins[:n_in], *outs[:n_out], *scr[:n_scr])
        if comms:
            @pl.when(last)
            def _():
                for k, cm in enumerate(comms):
                    cm.finish(cins[k], couts[k], cscr[k])

    outs = _pc(wrapped, name=name, grid=grid,
               in_specs=list(in_specs) + [ANY] * sum(c_in), out_specs=list(out_specs) + [ANY] * sum(c_out),
               out_shape=list(out_shape) + [s for cm in comms for s in cm.out_shapes],
               scratch_shapes=list(scratch_shapes) + [s for cm in comms for s in cm.scratch],
               input_output_aliases=aliases, compiler_params=_params(*sem),
               )(*args, *[a for cm in comms for a in cm.arrays])
    outs = list(outs)
    return outs[:n_out], split(outs[n_out:], c_out)


def _row_block(r):
    for b in (512, 256, 128, 64, 32, 16, 8):
        if r % b == 0:
            return b
    return r


def _pair_add(g, recv, c_idx, name):
    _, r, cc = g.shape
    br = _row_block(r)

    def body(c_ref, g_ref, r_ref, p_ref, pb_ref):
        s = g_ref[...] + r_ref[...]
        p_ref[...] = s
        pb_ref[...] = s.astype(BF16)

    grid_spec = pltpu.PrefetchScalarGridSpec(
        num_scalar_prefetch=1, grid=(N_CHIP, r // br),
        in_specs=[pl.BlockSpec((1, br, cc), lambda j, i, c_ref: (2 * j + c_ref[0], i, 0)),
                  pl.BlockSpec((1, br, cc), lambda j, i, c_ref: (j, i, 0))],
        out_specs=[pl.BlockSpec((1, br, cc), lambda j, i, c_ref: (j, i, 0)),
                   pl.BlockSpec((1, br, cc), lambda j, i, c_ref: (j, i, 0))])
    return _pc(body, name=name, grid_spec=grid_spec,
               out_shape=[_sds((N_CHIP, r, cc)), _sds((N_CHIP, r, cc), BF16)],
               compiler_params=_params("arbitrary", "arbitrary"))(c_idx, g, recv)


def _chip_add(p, q, jc_idx, name):
    _, r, cc = p.shape
    br = _row_block(r)

    def body(jc_ref, p_ref, q_ref, o_ref):
        o_ref[0] = ((p_ref[0] + q_ref[0].astype(F32)) + q_ref[1].astype(F32)) + q_ref[2].astype(F32)

    grid_spec = pltpu.PrefetchScalarGridSpec(
        num_scalar_prefetch=1, grid=(r // br,),
        in_specs=[pl.BlockSpec((1, br, cc), lambda i, jc_ref: (jc_ref[0], i, 0)),
                  pl.BlockSpec((3, br, cc), lambda i, jc_ref: (0, i, 0))],
        out_specs=pl.BlockSpec((1, br, cc), lambda i, jc_ref: (jc_ref[1], i, 0)))
    return _pc(body, name=name, grid_spec=grid_spec, out_shape=_sds((2, r, cc)),
               compiler_params=_params("arbitrary"))(jc_idx, p, q)


def _shard_of(both):
    return both.reshape((2 * both.shape[1],) + both.shape[2:])


def _adamw(w, g, m, v, name, comms=()):
    r, cc = w.shape
    br = _row_block(r)
    if r * cc * 4 <= (1 << 20):
        br = r
    elif br * cc * 4 > (1 << 20) and br > 8:
        br = max(8, (1 << 20) // (cc * 4) // 8 * 8)
        while r % br:
            br -= 8
    c1 = 1.0 - ADAM_B1 ** ADAM_STEP
    c2 = 1.0 - ADAM_B2 ** ADAM_STEP

    def body(w_ref, g_ref, m_ref, v_ref, d_ref, mo_ref, vo_ref):
        gg = g_ref[...]
        mn = ADAM_B1 * m_ref[...] + (1.0 - ADAM_B1) * gg
        vn = ADAM_B2 * v_ref[...] + (1.0 - ADAM_B2) * (gg * gg)
        mh = mn / c1
        vh = vn / c2
        d_ref[...] = -ADAM_LR * (mh / (jnp.sqrt(vh) + ADAM_EPS) + ADAM_WD * w_ref[...])
        mo_ref[...] = mn
        vo_ref[...] = vn

    spec = pl.BlockSpec((br, cc), lambda i: (i, 0))
    outs, couts = _call(body, name=name, grid=(r // br,), in_specs=[spec] * 4, out_specs=[spec] * 3,
                        out_shape=[_sds((r, cc))] * 3, scratch_shapes=[], sem=("arbitrary",), args=(w, g, m, v),
                        comms=comms)
    return (outs, couts) if comms else outs


def _prep(c_all, c_ctx, ret_decay):
    def body(c_ref, cc_ref, rd_ref, a_ref, lg_ref, sg_ref):
        ca = c_ref[...]
        cc = cc_ref[...]
        a_ref[...] = jnp.zeros_like(a_ref)
        a_ref[0:8, :] = ca * _sigmoid(ca)
        a_ref[8:9, :] = cc * _sigmoid(cc)
        rd = rd_ref[...]
        lg_ref[...] = -_softplus(-rd)
        sg_ref[...] = _sigmoid(-rd)

    rd = jnp.broadcast_to(ret_decay.reshape(2, HEADS).T[:, :, None], (HEADS, 2, LANES))
    return _pc(body, name="prep",
               out_shape=[_sds((16, D_MODEL)), _sds((HEADS, 2, LANES)), _sds((HEADS, 2, LANES))],
               in_specs=[_full((8, D_MODEL)), _full((1, D_MODEL)), _full((HEADS, 2, LANES))],
               out_specs=[_full((16, D_MODEL)), _full((HEADS, 2, LANES)), _full((HEADS, 2, LANES))],
               compiler_params=_params())(c_all, c_ctx.reshape(1, D_MODEL), rd)


def _mod_fwd(a16, w_ada, b_shard):
    n = w_ada.shape[1]
    bn = 512

    def body(a_ref, w_ref, b_ref, o_ref):
        o_ref[...] = jnp.dot(a_ref[...], w_ref[...], preferred_element_type=F32,
                             precision=lax.Precision.HIGHEST) + b_ref[...]

    return _pc(body, name="mod_fwd", grid=(n // bn,),
               in_specs=[_full((16, D_MODEL)), pl.BlockSpec((D_MODEL, bn), lambda i: (0, i)),
                         pl.BlockSpec((1, bn), lambda i: (0, i))],
               out_specs=pl.BlockSpec((16, bn), lambda i: (0, i)), out_shape=_sds((16, n)),
               compiler_params=_params("arbitrary"))(a16, w_ada, b_shard)


def _ada_grad(at, b):
    n = b.shape[1]
    bn = 512

    def body(a_ref, b_ref, o_ref):
        o_ref[...] = jnp.dot(a_ref[...], b_ref[...], preferred_element_type=F32, precision=lax.Precision.HIGHEST)

    return _pc(body, name="ada_grad", grid=(n // bn,),
               in_specs=[_full((D_MODEL, LANES)), pl.BlockSpec((LANES, bn), lambda i: (0, i))],
               out_specs=pl.BlockSpec((D_MODEL, bn), lambda i: (0, i)), out_shape=_sds((D_MODEL, n)),
               compiler_params=_params("arbitrary"))(at, b)


def _cctx_partial(dmc8, w_ada):
    n = w_ada.shape[1]
    bn = 512

    def body(d_ref, w_ref, o_ref):
        @pl.when(pl.program_id(0) == 0)
        def _():
            o_ref[...] = jnp.zeros_like(o_ref)
        o_ref[...] += lax.dot_general(d_ref[...], w_ref[...], (((1,), (1,)), ((), ())),
                                      preferred_element_type=F32, precision=lax.Precision.HIGHEST)

    return _pc(body, name="cctx_partial", grid=(n // bn,),
               in_specs=[pl.BlockSpec((8, bn), lambda i: (0, i)), pl.BlockSpec((D_MODEL, bn), lambda i: (0, i))],
               out_specs=_full((8, D_MODEL)), out_shape=_sds((8, D_MODEL)),
               compiler_params=_params("arbitrary"))(dmc8, w_ada)


def _cctx_final(parts, c_ctx, m, v):
    c1 = 1.0 - ADAM_B1 ** ADAM_STEP
    c2 = 1.0 - ADAM_B2 ** ADAM_STEP

    def body(p_ref, c_ref, m_ref, v_ref, g_ref, d_ref, mo_ref, vo_ref):
        s = ((p_ref[0, 0:1, :] + p_ref[2, 0:1, :]) + p_ref[4, 0:1, :]) + p_ref[6, 0:1, :]
        z = c_ref[...]
        sg = _sigmoid(z)
        gg = s * (sg * (1.0 + z * (1.0 - sg)))
        g_ref[...] = gg
        mn = ADAM_B1 * m_ref[...] + (1.0 - ADAM_B1) * gg
        vn = ADAM_B2 * v_ref[...] + (1.0 - ADAM_B2) * (gg * gg)
        d_ref[...] = -ADAM_LR * ((mn / c1) / (jnp.sqrt(vn / c2) + ADAM_EPS) + ADAM_WD * z)
        mo_ref[...] = mn
        vo_ref[...] = vn

    row = _full((1, D_MODEL))
    return _pc(body, name="cctx_final", out_shape=[_sds((1, D_MODEL))] * 4,
               in_specs=[_full(parts.shape), row, row, row], out_specs=[row] * 4,
               compiler_params=_params())(parts, c_ctx.reshape(1, D_MODEL), m.reshape(1, D_MODEL), v.reshape(1, D_MODEL))


def _rotary_tables(t_len):
    rows = t_len // GRID_W
    row = jnp.repeat(jnp.arange(rows, dtype=F32), GRID_W)
    col = jnp.tile(jnp.arange(GRID_W, dtype=F32), rows)
    n_freq = DH // 4
    inv = ROPE_BASE ** (-jnp.arange(n_freq, dtype=F32) / n_freq)
    ang = jnp.concatenate([row[:, None] * inv, col[:, None] * inv], axis=-1)
    cos, sin = jnp.cos(ang), jnp.sin(ang)
    return jnp.concatenate([cos, cos], axis=-1), jnp.concatenate([-sin, sin], axis=-1)


def _inproj_fwd(x, gn, sh, sc, w4, cos2, sin2, name):
    t = x.shape[0]
    tm = _tile(t, True)
    nc = IN_COLS // N_CHIP

    def body(x_ref, gn_ref, sh_ref, sc_ref, w_ref, c_ref, s_ref, p_ref, hb_ref):
        xh, _ = _rms(x_ref[...])
        h = xh * gn_ref[...] * (1.0 + sc_ref[...]) + sh_ref[...]
        hb = h.astype(BF16)
        hb_ref[...] = hb
        for j in range(N_CHIP):
            p_ref[:, nc * j:nc * (j + 1)] = _dot(hb, w_ref[j])
        cc = c_ref[...]
        ss = s_ref[...]
        for hh in range(2 * HEADS):
            blk = p_ref[:, DH * hh:DH * (hh + 1)]
            rot = blk * cc + pltpu.roll(blk, DH // 2, 1) * ss
            if hh >= HEADS:
                rot = rot * K_SCALE
            p_ref[:, DH * hh:DH * (hh + 1)] = rot

    row = _full((1, D_MODEL))
    return _pc(body, name=name, grid=(t // tm,),
               in_specs=[pl.BlockSpec((tm, D_MODEL), lambda i: (i, 0)), row, row, row, _full(w4.shape),
                         pl.BlockSpec((tm, DH), lambda i: (i, 0)), pl.BlockSpec((tm, DH), lambda i: (i, 0))],
               out_specs=[pl.BlockSpec((tm, IN_COLS), lambda i: (i, 0)), pl.BlockSpec((tm, D_MODEL), lambda i: (i, 0))],
               out_shape=[_sds((t, IN_COLS)), _sds((t, D_MODEL), BF16)],
               compiler_params=_params("arbitrary"))(x, gn, sh, sc, w4, cos2, sin2)


def _inproj_bwd(x, gn, sh, sc, w4, cos2, sin2, pieces, dres, name):
    t = x.shape[0]
    tm = _tile(t)
    nc = IN_COLS // N_CHIP

    def body(x_ref, gn_ref, sh_ref, sc_ref, w_ref, c_ref, s_ref, dqf, dqb, dkf, dkb, dvf, dvb, dg, dxr, dgt, dres_ref,
             dx_ref, dpb_ref, dgn_ref, dsh_ref, dsc_ref):
        cc = c_ref[...]
        ss = s_ref[...]
        dq = dqf[...] + dqb[...]
        dk = dkf[...] + dkb[...]
        for hh in range(HEADS):
            sl = slice(DH * hh, DH * (hh + 1))
            b = dq[:, sl]
            dpb_ref[:, sl] = (b * cc + pltpu.roll(b * ss, DH // 2, 1)).astype(BF16)
            b = dk[:, sl]
            dpb_ref[:, RET_W + DH * hh:RET_W + DH * (hh + 1)] = (
                (b * cc + pltpu.roll(b * ss, DH // 2, 1)) * K_SCALE).astype(BF16)
        dpb_ref[:, 2 * RET_W:3 * RET_W] = (dvf[...] + dvb[...]).astype(BF16)
        dpb_ref[:, 3 * RET_W:4 * RET_W] = dg[...].astype(BF16)
        dpb_ref[:, 4 * RET_W:4 * RET_W + LRU_W] = dxr[...].astype(BF16)
        dpb_ref[:, 4 * RET_W + LRU_W:IN_COLS] = dgt[...].astype(BF16)
        dh = _dot_nt(dpb_ref[:, 0:nc], w_ref[0])
        for j in range(1, N_CHIP):
            dh = dh + _dot_nt(dpb_ref[:, nc * j:nc * (j + 1)], w_ref[j])
        dx, dgn_t, dsh_t, dsc_t = _norm_mod_bwd(x_ref[...], gn_ref[...], sc_ref[...], dh)
        dx_ref[...] = dres_ref[...] + dx

        @pl.when(pl.program_id(0) == 0)
        def _():
            dgn_ref[...] = jnp.zeros_like(dgn_ref)
            dsh_ref[...] = jnp.zeros_like(dsh_ref)
            dsc_ref[...] = jnp.zeros_like(dsc_ref)
        dgn_ref[...] += dgn_t
        dsh_ref[...] += dsh_t
        dsc_ref[...] += dsc_t

    row = _full((1, D_MODEL))
    pc = pl.BlockSpec((tm, RET_W), lambda i: (i, 0))
    big = pl.BlockSpec((tm, D_MODEL), lambda i: (i, 0))
    return _pc(body, name=name, grid=(t // tm,),
               in_specs=[big, row, row, row, _full(w4.shape),
                         pl.BlockSpec((tm, DH), lambda i: (i, 0)), pl.BlockSpec((tm, DH), lambda i: (i, 0))]
               + [pc] * 9 + [big],
               out_specs=[big, pl.BlockSpec((tm, IN_COLS), lambda i: (i, 0)), row, row, row],
               out_shape=[_sds((t, D_MODEL)), _sds((t, IN_COLS), BF16), _sds((1, D_MODEL)), _sds((1, D_MODEL)),
                          _sds((1, D_MODEL))],
               compiler_params=_params("arbitrary"))(x, gn, sh, sc, w4, cos2, sin2, *pieces, dres)


XR_BLOCK = (4 * RET_W) // LRU_W


def _halo_specs(t, tm, col):
    n8 = tm // SUBLANES
    last8 = t // SUBLANES - 1
    prev = pl.BlockSpec((SUBLANES, LRU_W), lambda i: (jnp.maximum(i * n8 - 1, 0), col))
    main = pl.BlockSpec((tm, LRU_W), lambda i: (i, col))
    nxt = pl.BlockSpec((SUBLANES, LRU_W), lambda i: (jnp.minimum((i + 1) * n8, last8), col))
    return prev, main, nxt


def _with_halo(prev_ref, main_ref, next_ref, i, nt):
    prev = jnp.where(i > 0, prev_ref[...], 0.0)
    nxt = jnp.where(i < nt - 1, next_ref[...], 0.0)
    return jnp.concatenate([prev, main_ref[...], nxt], axis=0)


def _conv_fwd(proj, cw, cb, name):
    t = proj.shape[0]
    tm = _tile(t, True)
    nt = t // tm
    n = tm + 2 * SUBLANES
    mid = slice(SUBLANES, SUBLANES + tm)

    def body(p_ref, m_ref, n_ref, w_ref, b_ref, o_ref):
        xp = _with_halo(p_ref, m_ref, n_ref, pl.program_id(0), nt)
        acc = b_ref[...] + pltpu.roll(xp, 1, 0)[mid] * w_ref[0:1, :]
        acc = acc + xp[mid] * w_ref[1:2, :]
        acc = acc + pltpu.roll(xp, n - 1, 0)[mid] * w_ref[2:3, :]
        acc = acc + pltpu.roll(xp, n - 2, 0)[mid] * w_ref[3:4, :]
        o_ref[...] = acc

    return _pc(body, name=name, grid=(nt,),
               in_specs=[*_halo_specs(t, tm, XR_BLOCK), _full((4, LRU_W)), _full((1, LRU_W))],
               out_specs=pl.BlockSpec((tm, LRU_W), lambda i: (i, 0)), out_shape=_sds((t, LRU_W)),
               compiler_params=_params("arbitrary"))(proj, proj, proj, cw, cb)


def _conv_bwd(dxc_a, dxc_b, proj, cw, name):
    t = proj.shape[0]
    tm = _tile(t, True)
    nt = t // tm
    n = tm + 2 * SUBLANES
    mid = slice(SUBLANES, SUBLANES + tm)

    def body(ap_ref, am_ref, an_ref, bp_ref, bm_ref, bn_ref, xp_ref, xm_ref, xn_ref, w_ref, dx_ref, dw_ref, db_ref):
        i = pl.program_id(0)
        dp = _with_halo(ap_ref, am_ref, an_ref, i, nt) + _with_halo(bp_ref, bm_ref, bn_ref, i, nt)
        xp = _with_halo(xp_ref, xm_ref, xn_ref, i, nt)
        dx = pltpu.roll(dp, n - 1, 0)[mid] * w_ref[0:1, :]
        dx = dx + dp[mid] * w_ref[1:2, :]
        dx = dx + pltpu.roll(dp, 1, 0)[mid] * w_ref[2:3, :]
        dx = dx + pltpu.roll(dp, 2, 0)[mid] * w_ref[3:4, :]
        dx_ref[...] = dx
        d = dp[mid]

        @pl.when(i == 0)
        def _():
            dw_ref[...] = jnp.zeros_like(dw_ref)
            db_ref[...] = jnp.zeros_like(db_ref)
        dw_ref[0:1, :] += _sum0(d * pltpu.roll(xp, 1, 0)[mid])
        dw_ref[1:2, :] += _sum0(d * xp[mid])
        dw_ref[2:3, :] += _sum0(d * pltpu.roll(xp, n - 1, 0)[mid])
        dw_ref[3:4, :] += _sum0(d * pltpu.roll(xp, n - 2, 0)[mid])
        db_ref[...] += _sum0(d)

    return _pc(body, name=name, grid=(nt,),
               in_specs=[*_halo_specs(t, tm, 0), *_halo_specs(t, tm, 0), *_halo_specs(t, tm, XR_BLOCK),
                         _full((4, LRU_W))],
               out_specs=[pl.BlockSpec((tm, LRU_W), lambda i: (i, 0)), _full((4, LRU_W)), _full((1, LRU_W))],
               out_shape=[_sds((t, LRU_W)), _sds((4, LRU_W)), _sds((1, LRU_W))],
               compiler_params=_params("arbitrary"))(dxc_a, dxc_a, dxc_a, dxc_b, dxc_b, dxc_b, proj, proj, proj, cw)


def _local_scan(a, b, reverse):
    n = a.shape[0]
    row = lax.broadcasted_iota(jnp.int32, a.shape, 0) & (SUBLANES - 1)
    for s in (1, 2, 4):
        if reverse:
            a_s, b_s, ok = pltpu.roll(a, n - s, 0), pltpu.roll(b, n - s, 0), row < SUBLANES - s
        else:
            a_s, b_s, ok = pltpu.roll(a, s, 0), pltpu.roll(b, s, 0), row >= s
        b = a * jnp.where(ok, b_s, 0.0) + b
        a = a * jnp.where(ok, a_s, 1.0)
    return a, b


def _carry_scan(a_s, b_s, out_ref, carry, reverse):
    ng = a_s.shape[0] // SUBLANES
    shape = carry.shape

    def step(g, cr):
        gg = (ng - 1 - g) if reverse else g
        off = pl.multiple_of(gg * SUBLANES, SUBLANES)
        h = a_s[pl.ds(off, SUBLANES), :] * cr + b_s[pl.ds(off, SUBLANES), :]
        out_ref[pl.ds(off, SUBLANES), :] = h
        edge = h[0:1, :] if reverse else h[SUBLANES - 1:SUBLANES, :]
        return jnp.broadcast_to(edge, shape)

    return lax.fori_loop(0, ng, step, carry)


def _lru_gates(xc, wa_ref, wx_ref, ba, bx, lam):
    xb = xc.astype(BF16)
    r = _sigmoid(_dot(xb, wa_ref[...]) + ba)
    ig = _sigmoid(_dot(xb, wx_ref[...]) + bx)
    sp = _softplus(-lam)
    la = -LRU_C * r * sp
    a = jnp.exp(la)
    mult = jnp.sqrt(_neg_expm1(2.0 * la, a * a))
    return r, ig, sp, a, mult


def _lru_fwd(xc, wa, wx, ba, bx, lam, h0, reverse, name, comms=()):
    t = xc.shape[0]
    tm = _tile(t, True)
    nt = t // tm
    tidx = (lambda i: (nt - 1 - i, 0)) if reverse else (lambda i: (i, 0))

    def body(x_ref, wa_ref, wx_ref, ba_ref, bx_ref, lam_ref, h0_ref, h_ref, a_s, b_s, c_s):
        @pl.when(pl.program_id(0) == 0)
        def _():
            c_s[...] = jnp.broadcast_to(h0_ref[...], c_s.shape)
        xv = x_ref[...]
        _, ig, _, a, mult = _lru_gates(xv, wa_ref, wx_ref, ba_ref[...], bx_ref[...], lam_ref[...])
        al, bl = _local_scan(a, mult * (ig * xv), reverse)
        a_s[...] = al
        b_s[...] = bl
        c_s[...] = _carry_scan(a_s, b_s, h_ref, c_s[...], reverse)

    vec = _full((1, LRU_W))
    mat = _full((LRU_W, LRU_W))
    (h,), couts = _call(body, name=name, grid=(nt,),
                        in_specs=[pl.BlockSpec((tm, LRU_W), tidx), mat, mat, vec, vec, vec, vec],
                        out_specs=[pl.BlockSpec((tm, LRU_W), tidx)], out_shape=[_sds((t, LRU_W))],
                        scratch_shapes=[pltpu.VMEM((tm, LRU_W), F32), pltpu.VMEM((tm, LRU_W), F32),
                                        pltpu.VMEM((SUBLANES, LRU_W), F32)],
                        sem=("arbitrary",), args=(xc, wa, wx, ba, bx, lam, h0), comms=comms)
    return (h, couts) if comms else h


def _lru_bwd(xc, wa, wx, ba, bx, lam, h, h0, dh, reverse, name, comms=()):
    t = xc.shape[0]
    tm = _tile(t, True)
    nt = t // tm
    n8 = tm // SUBLANES
    last8 = t // SUBLANES - 1
    tidx = (lambda i: (i, 0)) if reverse else (lambda i: (nt - 1 - i, 0))
    if reverse:
        halo = pl.BlockSpec((SUBLANES, LRU_W), lambda i: (jnp.minimum((i + 1) * n8, last8), 0))
    else:
        halo = pl.BlockSpec((SUBLANES, LRU_W), lambda i: (jnp.maximum((nt - 1 - i) * n8 - 1, 0), 0))

    def body(x_ref, wa_ref, wx_ref, ba_ref, bx_ref, lam_ref, h_ref, halo_ref, h0_ref, dh_ref,
             dx_ref, dpre_ref, dba_ref, dbx_ref, dlam_ref, dh0_ref, a_s, b_s, l_s, c_s, e_s):
        i = pl.program_id(0)

        @pl.when(i == 0)
        def _():
            c_s[...] = jnp.zeros_like(c_s)
            e_s[...] = jnp.zeros_like(e_s)
            dba_ref[...] = jnp.zeros_like(dba_ref)
            dbx_ref[...] = jnp.zeros_like(dbx_ref)
            dlam_ref[...] = jnp.zeros_like(dlam_ref)
        xv = x_ref[...]
        lam = lam_ref[...]
        r, ig, sp, a, mult = _lru_gates(xv, wa_ref, wx_ref, ba_ref[...], bx_ref[...], lam)
        hv = h_ref[...]
        rowi = lax.broadcasted_iota(jnp.int32, (tm, LRU_W), 0)
        edge_a = jnp.broadcast_to(e_s[0:1, :], (tm, LRU_W))
        h0b = jnp.broadcast_to(h0_ref[...], (tm, LRU_W))
        if reverse:
            a_sh = jnp.where(rowi == 0, edge_a, pltpu.roll(a, 1, 0))
            hin_edge = jnp.where(i == nt - 1, h0b, jnp.broadcast_to(halo_ref[0:1, :], (tm, LRU_W)))
            h_in = jnp.where(rowi == tm - 1, hin_edge, pltpu.roll(hv, tm - 1, 0))
        else:
            a_sh = jnp.where(rowi == tm - 1, edge_a, pltpu.roll(a, tm - 1, 0))
            hin_edge = jnp.where(i == nt - 1, h0b, jnp.broadcast_to(halo_ref[SUBLANES - 1:SUBLANES, :], (tm, LRU_W)))
            h_in = jnp.where(rowi == 0, hin_edge, pltpu.roll(hv, 1, 0))
        al, bl = _local_scan(a_sh, dh_ref[...], not reverse)
        a_s[...] = al
        b_s[...] = bl
        c_s[...] = _carry_scan(a_s, b_s, l_s, c_s[...], not reverse)
        e_s[...] = jnp.broadcast_to(a[tm - 1:tm, :] if reverse else a[0:1, :], e_s.shape)
        lmb = l_s[...]
        da = lmb * h_in
        ixc = ig * xv
        dmult = lmb * ixc
        dixc = lmb * mult
        dla = da * a - dmult * (a * a) / mult
        dpr = dla * (-LRU_C * sp) * r * (1.0 - r)
        dpi = dixc * xv * ig * (1.0 - ig)
        dprb = dpr.astype(BF16)
        dpib = dpi.astype(BF16)
        dpre_ref[:, 0:LRU_W] = dprb
        dpre_ref[:, LRU_W:2 * LRU_W] = dpib
        dx_ref[...] = dixc * ig + _dot_nt(dprb, wa_ref[...]) + _dot_nt(dpib, wx_ref[...])
        dba_ref[...] += _sum0(dpr)
        dbx_ref[...] += _sum0(dpi)
        dlam_ref[...] += _sum0(dla * (-LRU_C * r)) * (-_sigmoid(-lam))

        @pl.when(i == nt - 1)
        def _():
            al0 = a * lmb
            dh0_ref[...] = al0[tm - 1:tm, :] if reverse else al0[0:1, :]

    vec = _full((1, LRU_W))
    mat = _full((LRU_W, LRU_W))
    tile = pl.BlockSpec((tm, LRU_W), tidx)
    return _call(body, name=name, grid=(nt,),
                 in_specs=[tile, mat, mat, vec, vec, vec, tile, halo, vec, tile],
                 out_specs=[tile, pl.BlockSpec((tm, 2 * LRU_W), tidx), vec, vec, vec, vec],
                 out_shape=[_sds((t, LRU_W)), _sds((t, 2 * LRU_W), BF16), _sds((1, LRU_W)), _sds((1, LRU_W)),
                            _sds((1, LRU_W)), _sds((1, LRU_W))],
                 scratch_shapes=[pltpu.VMEM((tm, LRU_W), F32), pltpu.VMEM((tm, LRU_W), F32),
                                 pltpu.VMEM((tm, LRU_W), F32), pltpu.VMEM((SUBLANES, LRU_W), F32),
                                 pltpu.VMEM((SUBLANES, LRU_W), F32)],
                 sem=("arbitrary",), args=(xc, wa, wx, ba, bx, lam, h, h, h0, dh), comms=comms)


def _decay_tables(lg, reverse):
    ci = lax.broadcasted_iota(jnp.int32, (CHUNK, CHUNK), 0).astype(F32)
    mi = lax.broadcasted_iota(jnp.int32, (CHUNK, CHUNK), 1).astype(F32)
    if reverse:
        rel, pq, ps = mi - ci, CHUNK - ci, ci
    else:
        rel, pq, ps = ci - mi, ci + 1.0, CHUNK - 1.0 - ci
    relc = jnp.maximum(rel, 0.0)
    dm = jnp.where(rel >= 0, jnp.exp(lg * relc), 0.0)
    return relc, dm, jnp.exp(lg * pq), jnp.exp(lg * ps), jnp.exp(lg * float(CHUNK)), pq, ps


def _ret_fwd(proj, lgv, s0f, s0b, comms=()):
    t = proj.shape[0]
    n = t // CHUNK

    def one(q, k, v, lg, s_s, hh, o_ref, sp_ref, reverse):
        _, dm, wq, ws, g, _, _ = _decay_tables(lg, reverse)
        vb = v.astype(BF16)
        p = _dot_nt(q.astype(BF16), k.astype(BF16)) * dm
        s = s_s[hh]
        sp_ref[hh, 0] = s
        o_ref[:, DH * hh:DH * (hh + 1)] = _dot(p.astype(BF16), vb) + _dot((q * wq).astype(BF16), s.astype(BF16))
        s_s[hh] = g * s + _dot_tn((k * ws).astype(BF16), vb)

    def body(qf, kf, vf, qb, kb, vb, lg_ref, s0f_ref, s0b_ref, of_ref, ob_ref, spf_ref, spb_ref, sf_s, sb_s):
        @pl.when(pl.program_id(0) == 0)
        def _():
            sf_s[...] = s0f_ref[...]
            sb_s[...] = s0b_ref[...]
        for hh in range(HEADS):
            sl = slice(DH * hh, DH * (hh + 1))
            one(qf[:, sl], kf[:, sl], vf[:, sl], lg_ref[hh, 0:1, :], sf_s, hh, of_ref, spf_ref, False)
            one(qb[:, sl], kb[:, sl], vb[:, sl], lg_ref[hh, 1:2, :], sb_s, hh, ob_ref, spb_ref, True)

    blk = (CHUNK, RET_W)
    fw = [pl.BlockSpec(blk, lambda i, o=o: (i, o)) for o in range(3)]
    bw = [pl.BlockSpec(blk, lambda i, o=o: (n - 1 - i, o)) for o in range(3)]
    st = _full((HEADS, DH, DH))
    return _call(body, name="ret_fwd", grid=(n,),
                 in_specs=fw + bw + [_full((HEADS, 2, LANES)), st, st],
                 out_specs=[pl.BlockSpec(blk, lambda i: (i, 0)), pl.BlockSpec(blk, lambda i: (n - 1 - i, 0)),
                            pl.BlockSpec((HEADS, 1, DH, DH), lambda i: (0, i, 0, 0)),
                            pl.BlockSpec((HEADS, 1, DH, DH), lambda i: (0, n - 1 - i, 0, 0))],
                 out_shape=[_sds((t, RET_W)), _sds((t, RET_W)), _sds((HEADS, n, DH, DH)), _sds((HEADS, n, DH, DH))],
                 scratch_shapes=[pltpu.VMEM((HEADS, DH, DH), F32), pltpu.VMEM((HEADS, DH, DH), F32)],
                 sem=("arbitrary",), args=(proj, proj, proj, proj, proj, proj, lgv, s0f, s0b), comms=comms)


def _ret_bwd(proj, lgv, sgv, sprev, do, reverse, name, comms=()):
    t = proj.shape[0]
    n = t // CHUNK
    d = 1 if reverse else 0
    cidx = (lambda i: i) if reverse else (lambda i: n - 1 - i)

    def body(q_ref, k_ref, v_ref, lg_ref, sg_ref, s_ref, do_ref, dq_ref, dk_ref, dv_ref, ds0_ref, drd_ref, ds_s, acc_s):
        i = pl.program_id(0)

        @pl.when(i == 0)
        def _():
            ds_s[...] = jnp.zeros_like(ds_s)
            acc_s[...] = jnp.zeros_like(acc_s)
        for hh in range(HEADS):
            sl = slice(DH * hh, DH * (hh + 1))
            relc, dm, wq, ws, g, pq, ps = _decay_tables(lg_ref[hh, d:d + 1, :], reverse)
            q, k = q_ref[:, sl], k_ref[:, sl]
            qb, kb, vb = q.astype(BF16), k.astype(BF16), v_ref[:, sl].astype(BF16)
            p = _dot_nt(qb, kb) * dm
            s = s_ref[hh, 0]
            dob = do_ref[:, sl].astype(BF16)
            dsn = ds_s[hh]
            dsb = dsn.astype(BF16)
            dv_ref[:, sl] = _dot_tn(p.astype(BF16), dob) + _dot((k * ws).astype(BF16), dsb)
            dp = _dot_nt(dob, vb)
            dab = (dp * dm).astype(BF16)
            xq = _dot_nt(dob, s.astype(BF16))
            yk = _dot_nt(vb, dsb)
            dq_ref[:, sl] = _dot(dab, kb) + xq * wq
            dk_ref[:, sl] = _dot_tn(dab, qb) + yk * ws
            ds_s[hh] = g * dsn + _dot_tn((q * wq).astype(BF16), dob)
            part = (_sum0(dp * p * relc) + _sum0(xq * q * wq * pq) + _sum0(yk * k * ws * ps)
                    + _sum0(dsn * s) * g * float(CHUNK))
            acc_s[hh] += jnp.broadcast_to(part, (SUBLANES, LANES))

        @pl.when(i == n - 1)
        def _():
            ds0_ref[...] = ds_s[...]
            for hh in range(HEADS):
                tot = jnp.sum(acc_s[hh, 0:1, :], axis=1, keepdims=True)
                drd_ref[hh] = jnp.broadcast_to(tot, (SUBLANES, LANES)) * sg_ref[hh, d:d + 1, :]

    blk = (CHUNK, RET_W)
    qkv = [pl.BlockSpec(blk, lambda i, o=o: (cidx(i), o)) for o in range(3)]
    hc = pl.BlockSpec(blk, lambda i: (cidx(i), 0))
    lane = _full((HEADS, 2, LANES))
    return _call(body, name=name, grid=(n,),
                 in_specs=qkv + [lane, lane, pl.BlockSpec((HEADS, 1, DH, DH), lambda i: (0, cidx(i), 0, 0)), hc],
                 out_specs=[hc, hc, hc, _full((HEADS, DH, DH)), _full((HEADS, SUBLANES, LANES))],
                 out_shape=[_sds((t, RET_W))] * 3 + [_sds((HEADS, DH, DH)), _sds((HEADS, SUBLANES, LANES))],
                 scratch_shapes=[pltpu.VMEM((HEADS, DH, DH), F32), pltpu.VMEM((HEADS, SUBLANES, LANES), F32)],
                 sem=("arbitrary",), args=(proj, proj, proj, lgv, sgv, sprev, do), comms=comms)


def _ctx_weights(lg, l_len, reverse):
    pos = lax.broadcasted_iota(jnp.int32, (l_len, DH), 0).astype(F32)
    steps = pos if reverse else (l_len - 1.0 - pos)
    return jnp.exp(lg * steps), steps


def _ctx_state_fwd(projc, lgv):
    l_len = projc.shape[0]

    def body(k_ref, v_ref, lg_ref, sf_ref, sb_ref):
        k = k_ref[...]
        vb = v_ref[...].astype(BF16)
        for d, o_ref in ((0, sf_ref), (1, sb_ref)):
            w, _ = _ctx_weights(lg_ref[0, d:d + 1, :], l_len, d == 1)
            o_ref[0] = _dot_tn((k * w).astype(BF16), vb)

    st = pl.BlockSpec((1, DH, DH), lambda h: (h, 0, 0))
    return _pc(body, name="ctx_state_fwd", grid=(HEADS,),
               in_specs=[pl.BlockSpec((l_len, DH), lambda h: (0, HEADS + h)),
                         pl.BlockSpec((l_len, DH), lambda h: (0, 2 * HEADS + h)),
                         pl.BlockSpec((1, 2, LANES), lambda h: (h, 0, 0))],
               out_specs=[st, st], out_shape=[_sds((HEADS, DH, DH))] * 2,
               compiler_params=_params("arbitrary"))(projc, projc, lgv)


def _ctx_state_bwd(projc, lgv, sgv, dsf, dsb):
    l_len = projc.shape[0]

    def body(k_ref, v_ref, lg_ref, sg_ref, dsf_ref, dsb_ref, dk_ref, dv_ref, drd_ref):
        k = k_ref[...]
        vb = v_ref[...].astype(BF16)
        dk = jnp.zeros((l_len, DH), F32)
        dv = jnp.zeros((l_len, DH), F32)
        rows = []
        for d, ds_ref in ((0, dsf_ref), (1, dsb_ref)):
            w, steps = _ctx_weights(lg_ref[0, d:d + 1, :], l_len, d == 1)
            dsb16 = ds_ref[0].astype(BF16)
            dkw = _dot_nt(vb, dsb16)
            dk = dk + dkw * w
            dv = dv + _dot((k * w).astype(BF16), dsb16)
            tot = jnp.sum(_sum0(dkw * k * w * steps), axis=1, keepdims=True)
            rows.append(jnp.broadcast_to(tot, (1, LANES)) * sg_ref[0, d:d + 1, :])
        dk_ref[...] = dk
        dv_ref[...] = dv
        rid = lax.broadcasted_iota(jnp.int32, (SUBLANES, LANES), 0)
        drd_ref[0] = jnp.where(rid == 0, rows[0], jnp.where(rid == 1, rows[1], 0.0))

    st = pl.BlockSpec((1, DH, DH), lambda h: (h, 0, 0))
    lane = pl.BlockSpec((1, 2, LANES), lambda h: (h, 0, 0))
    hc = pl.BlockSpec((l_len, DH), lambda h: (0, h))
    return _pc(body, name="ctx_state_bwd", grid=(HEADS,),
               in_specs=[pl.BlockSpec((l_len, DH), lambda h: (0, HEADS + h)),
                         pl.BlockSpec((l_len, DH), lambda h: (0, 2 * HEADS + h)), lane, lane, st, st],
               out_specs=[hc, hc, pl.BlockSpec((1, SUBLANES, LANES), lambda h: (h, 0, 0))],
               out_shape=[_sds((l_len, RET_W)), _sds((l_len, RET_W)), _sds((HEADS, SUBLANES, LANES))],
               compiler_params=_params("arbitrary"))(projc, projc, lgv, sgv, dsf, dsb)


G_BLOCK = (3 * RET_W) // RET_W
GATE_BLOCK = (4 * RET_W + LRU_W) // LRU_W


def _head_norm(y):
    yc = y - jnp.mean(y, axis=-1, keepdims=True)
    rs = lax.rsqrt(jnp.mean(yc * yc, axis=-1, keepdims=True) + EPS)
    return yc * rs, rs


def _gelu_parts(z):
    th = jnp.tanh(GELU_K * (z + GELU_C * z * z * z))
    return 0.5 * z * (1.0 + th), th


def _mix_fwd(o_f, o_b, proj, hf, hb, w_out, x, g1):
    t = x.shape[0]
    tm = _tile(t, True)

    def body(of_ref, ob_ref, g_ref, gt_ref, hf_ref, hb_ref, w_ref, x_ref, g1_ref, x1_ref, cat_ref):
        o = of_ref[...] + ob_ref[...]
        g = g_ref[...]
        for hh in range(HEADS):
            sl = slice(DH * hh, DH * (hh + 1))
            nrm, _ = _head_norm(o[:, sl])
            gh = g[:, sl]
            cat_ref[:, sl] = (gh * _sigmoid(gh) * nrm).astype(BF16)
        gel, _ = _gelu_parts(gt_ref[...])
        cat_ref[:, RET_W:] = ((hf_ref[...] + hb_ref[...]) * gel).astype(BF16)
        x1_ref[...] = x_ref[...] + g1_ref[...] * _dot(cat_ref[...], w_ref[...])

    half = pl.BlockSpec((tm, RET_W), lambda i: (i, 0))
    big = pl.BlockSpec((tm, D_MODEL), lambda i: (i, 0))
    return _pc(body, name="mix_fwd", grid=(t // tm,),
               in_specs=[half, half, pl.BlockSpec((tm, RET_W), lambda i: (i, G_BLOCK)),
                         pl.BlockSpec((tm, LRU_W), lambda i: (i, GATE_BLOCK)), half, half,
                         _full((D_MODEL, D_MODEL)), big, _full((1, D_MODEL))],
               out_specs=[big, big], out_shape=[_sds((t, D_MODEL)), _sds((t, D_MODEL), BF16)],
               compiler_params=_params("arbitrary"))(o_f, o_b, proj, proj, hf, hb, w_out, x, g1)


def _mix_bwd(o_f, o_b, proj, hf, hb, w_out, cat, dx1, g1, comms=()):
    t = dx1.shape[0]
    tm = _tile(t, True)

    def body(of_ref, ob_ref, g_ref, gt_ref, hf_ref, hb_ref, w_ref, cat_ref, dx1_ref, g1_ref,
             do_ref, dhs_ref, dg_ref, dgt_ref, dyb_ref, dg1_ref):
        dx1v = dx1_ref[...]
        y = _dot(cat_ref[...], w_ref[...])

        @pl.when(pl.program_id(0) == 0)
        def _():
            dg1_ref[...] = jnp.zeros_like(dg1_ref)
        dg1_ref[...] += _sum0(dx1v * y)
        dyb = (g1_ref[...] * dx1v).astype(BF16)
        dyb_ref[...] = dyb
        dcat = _dot_nt(dyb, w_ref[...])
        o = of_ref[...] + ob_ref[...]
        g = g_ref[...]
        for hh in range(HEADS):
            sl = slice(DH * hh, DH * (hh + 1))
            nrm, rs = _head_norm(o[:, sl])
            gh = g[:, sl]
            sg = _sigmoid(gh)
            dret = dcat[:, sl]
            dg_ref[:, sl] = dret * nrm * (sg * (1.0 + gh * (1.0 - sg)))
            dn = dret * (gh * sg)
            dyc = rs * (dn - nrm * jnp.mean(dn * nrm, axis=-1, keepdims=True))
            do_ref[:, sl] = dyc - jnp.mean(dyc, axis=-1, keepdims=True)
        z = gt_ref[...]
        gel, th = _gelu_parts(z)
        dlru = dcat[:, RET_W:]
        dhs_ref[...] = dlru * gel
        dgel = 0.5 * (1.0 + th) + 0.5 * z * (1.0 - th * th) * GELU_K * (1.0 + 3.0 * GELU_C * z * z)
        dgt_ref[...] = dlru * (hf_ref[...] + hb_ref[...]) * dgel

    half = pl.BlockSpec((tm, RET_W), lambda i: (i, 0))
    big = pl.BlockSpec((tm, D_MODEL), lambda i: (i, 0))
    return _call(body, name="mix_bwd", grid=(t // tm,),
                 in_specs=[half, half, pl.BlockSpec((tm, RET_W), lambda i: (i, G_BLOCK)),
                           pl.BlockSpec((tm, LRU_W), lambda i: (i, GATE_BLOCK)), half, half,
                           _full((D_MODEL, D_MODEL)), big, big, _full((1, D_MODEL))],
                 out_specs=[half, half, half, half, big, _full((1, D_MODEL))],
                 out_shape=[_sds((t, RET_W))] * 4 + [_sds((t, D_MODEL), BF16), _sds((1, D_MODEL))],
                 scratch_shapes=[], sem=("arbitrary",), args=(o_f, o_b, proj, proj, hf, hb, w_out, cat, dx1, g1),
                 comms=comms)


def _mlp(x1, n2g, sh2, sc2, g2, fg, w1, w2, tgt):
    t = x1.shape[0]
    tm = _tile(t)
    hb_ = MLP_H // N_CHIP

    def body(x1_ref, n2g_ref, sh2_ref, sc2_ref, g2_ref, fg_ref, w1_hbm, w2_hbm, tgt_ref,
             dx1_ref, h2b_ref, ab_ref, dub_ref, dmb_ref, dsc_ref, dsh_ref, dg2_ref, dn2_ref, dfg_ref, loss_ref,
             w1_s, w2_s, r_s, sems):
        @pl.when(pl.program_id(0) == 0)
        def _():
            c1 = pltpu.make_async_copy(w1_hbm, w1_s, sems.at[0])
            c2 = pltpu.make_async_copy(w2_hbm, w2_s, sems.at[1])
            c1.start()
            c2.start()
            for r in (dsc_ref, dsh_ref, dg2_ref, dn2_ref, dfg_ref, loss_ref):
                r[...] = jnp.zeros_like(r)
            c1.wait()
            c2.wait()
        x1v = x1_ref[...]
        n2g, sc2, g2, fg = n2g_ref[...], sc2_ref[...], g2_ref[...], fg_ref[...]
        xh, _ = _rms(x1v)
        h2b = (xh * n2g * (1.0 + sc2) + sh2_ref[...]).astype(BF16)
        h2b_ref[...] = h2b
        m = jnp.zeros((tm, D_MODEL), F32)
        for j in range(N_CHIP):
            sl = slice(hb_ * j, hb_ * (j + 1))
            r = jnp.maximum(_dot(h2b, w1_s[j]), 0.0)
            r_s[:, sl] = r
            ab = (r * r).astype(BF16)
            ab_ref[:, sl] = ab
            m = m + _dot(ab, w2_s[j])
        x2 = x1v + g2 * m
        x2h, r2 = _rms(x2)
        err = x2h * fg - tgt_ref[...]
        loss_ref[...] += _sum0(err * err)
        dout = err * (1.0 / D_MODEL)
        dfg_ref[...] += _sum0(dout * x2h)
        dxh = dout * fg
        dx2 = r2 * (dxh - x2h * jnp.mean(dxh * x2h, axis=-1, keepdims=True))
        dg2_ref[...] += _sum0(dx2 * m)
        dmb = (g2 * dx2).astype(BF16)
        dmb_ref[...] = dmb
        dh2 = jnp.zeros((tm, D_MODEL), F32)
        for j in range(N_CHIP):
            sl = slice(hb_ * j, hb_ * (j + 1))
            dub = (_dot_nt(dmb, w2_s[j]) * (2.0 * r_s[:, sl])).astype(BF16)
            dub_ref[:, sl] = dub
            dh2 = dh2 + _dot_nt(dub, w1_s[j])
        dx, dn2_t, dsh_t, dsc_t = _norm_mod_bwd(x1v, n2g, sc2, dh2)
        dx1_ref[...] = dx2 + dx
        dn2_ref[...] += dn2_t
        dsh_ref[...] += dsh_t
        dsc_ref[...] += dsc_t

        @pl.when(pl.program_id(0) == t // tm - 1)
        def _():
            tot = jnp.sum(loss_ref[...], axis=1, keepdims=True) * (0.5 / D_MODEL)
            loss_ref[...] = jnp.broadcast_to(tot, loss_ref.shape)

    row = _full((1, D_MODEL))
    big = pl.BlockSpec((tm, D_MODEL), lambda i: (i, 0))
    wide = pl.BlockSpec((tm, MLP_H), lambda i: (i, 0))
    return _pc(body, name="mlp", grid=(t // tm,),
               in_specs=[big, row, row, row, row, row, ANY, ANY, big],
               out_specs=[big, big, wide, wide, big, row, row, row, row, row, row],
               out_shape=[_sds((t, D_MODEL)), _sds((t, D_MODEL), BF16), _sds((t, MLP_H), BF16), _sds((t, MLP_H), BF16),
                          _sds((t, D_MODEL), BF16)] + [_sds((1, D_MODEL))] * 6,
               scratch_shapes=[pltpu.VMEM(w1.shape, BF16), pltpu.VMEM(w2.shape, BF16), pltpu.VMEM((tm, MLP_H), F32),
                               pltpu.SemaphoreType.DMA((2,))],
               compiler_params=_params("arbitrary"))(x1, n2g, sh2, sc2, g2, fg, w1, w2, tgt)


def _tn(a, b, nj, a_blocked, b_blocked, name, extra=None, comms=()):
    t = a.shape[0]
    m = a.shape[1] // (nj if a_blocked else 1)
    n = b.shape[1] // (nj if b_blocked else 1)
    bk = 1024 if t % 1024 == 0 else (512 if t % 512 == 0 else t)
    nk = t // bk
    a_map = (lambda j, k: (k, j)) if a_blocked else (lambda j, k: (k, 0))
    b_map = (lambda j, k: (k, j)) if b_blocked else (lambda j, k: (k, 0))
    in_specs = [pl.BlockSpec((bk, m), a_map), pl.BlockSpec((bk, n), b_map)]
    args = [a, b]
    if extra is not None:
        a2, b2 = extra
        t2 = a2.shape[0]
        in_specs += [pl.BlockSpec((t2, m), (lambda j, k: (0, j)) if a_blocked else (lambda j, k: (0, 0))),
                     pl.BlockSpec((t2, n), (lambda j, k: (0, j)) if b_blocked else (lambda j, k: (0, 0)))]
        args += [a2, b2]

    def body(*refs):
        a_ref, b_ref = refs[0], refs[1]
        o_ref, acc = refs[-2], refs[-1]
        k = pl.program_id(1)

        @pl.when(k == 0)
        def _():
            acc[...] = jnp.zeros_like(acc)
        acc[...] += _dot_tn(a_ref[...].astype(BF16), b_ref[...].astype(BF16))

        @pl.when(k == nk - 1)
        def _():
            if extra is not None:
                acc[...] += _dot_tn(refs[2][...].astype(BF16), refs[3][...].astype(BF16))
            o_ref[0] = acc[...]

    (out,), couts = _call(body, name=name, grid=(nj, nk), in_specs=in_specs,
                          out_specs=[pl.BlockSpec((1, m, n), lambda j, k: (j, 0, 0))], out_shape=[_sds((nj, m, n))],
                          scratch_shapes=[pltpu.VMEM((m, n), F32)], sem=("arbitrary", "arbitrary"), args=args,
                          comms=comms)
    return (out, couts) if comms else out


ROW_LOSS = 0
ROW_DMOD = 1
ROW_DMODC = 7
ROW_N1, ROW_N2, ROW_FG, ROW_CB = 9, 10, 11, 12
ROW_BA, ROW_BX, ROW_LAM = 13, 15, 17
ROW_CW = 20
ROW_RD = 24
SLAB_ROWS = 32
SEG = D_MODEL // 2


def _pack_small(rows, drd, cw2, cb2, lru2, gates):
    n_rows, n_lru = len(rows), len(lru2)

    def body(*refs):
        r = refs[:n_rows]
        drd_f, drd_b, drd_c, cw_a, cw_b, cb_a, cb_b = refs[n_rows:n_rows + 7]
        lru = refs[n_rows + 7:n_rows + 7 + n_lru]
        gf_ref, gb_ref, slab, ga, gx = refs[n_rows + 7 + n_lru:]
        slab[...] = jnp.zeros_like(slab)
        slab[ROW_LOSS:ROW_LOSS + 1, :] = r[0][...]
        for k in range(N_MOD):
            slab[ROW_DMOD + k:ROW_DMOD + k + 1, :] = r[1 + k][...]
        slab[ROW_DMODC:ROW_DMODC + 1, :] = r[7][...]
        slab[ROW_DMODC + 1:ROW_DMODC + 2, :] = r[8][...]
        slab[ROW_N1:ROW_N1 + 1, :] = r[9][...] + r[10][...]
        slab[ROW_N2:ROW_N2 + 1, :] = r[11][...]
        slab[ROW_FG:ROW_FG + 1, :] = r[12][...]
        slab[ROW_CB:ROW_CB + 1, 0:LRU_W] = cb_a[...] + cb_b[...]
        for k, row in enumerate((ROW_BA, ROW_BA + 1, ROW_BX, ROW_BX + 1, ROW_LAM, ROW_LAM + 1)):
            slab[row:row + 1, 0:LRU_W] = lru[2 * k][...] + lru[2 * k + 1][...]
        slab[ROW_CW:ROW_CW + 4, 0:LRU_W] = cw_a[...] + cw_b[...]
        for h in range(HEADS):
            slab[ROW_RD + h:ROW_RD + h + 1, 0:LANES] = drd_f[h, 0:1, :] + drd_c[h, 0:1, :]
            slab[ROW_RD + HEADS + h:ROW_RD + HEADS + h + 1, 0:LANES] = drd_b[h, 0:1, :] + drd_c[h, 1:2, :]
        for d, g_ref in enumerate((gf_ref, gb_ref)):
            for n in range(LRU_BLOCKS):
                blk = slice(LRU_BD * n, LRU_BD * (n + 1))
                dst = slice(LRU_W * d + LRU_BD * n, LRU_W * d + LRU_BD * (n + 1))
                ga[dst, :] = g_ref[0, blk, blk]
                gx[dst, :] = g_ref[1, blk, blk]

    args = list(rows) + list(drd) + list(cw2) + list(cb2) + list(lru2) + list(gates)
    return _pc(body, name="pack_small", in_specs=[_full(a.shape) for a in args],
               out_specs=[_full((SLAB_ROWS, D_MODEL)), _full((2 * LRU_W, LRU_BD)), _full((2 * LRU_W, LRU_BD))],
               out_shape=[_sds((SLAB_ROWS, D_MODEL)), _sds((2 * LRU_W, LRU_BD)), _sds((2 * LRU_W, LRU_BD))],
               compiler_params=_params())(*args)


def _adam_math(w, g, m, v):
    mn = ADAM_B1 * m + (1.0 - ADAM_B1) * g
    vn = ADAM_B2 * v + (1.0 - ADAM_B2) * (g * g)
    mh = mn / (1.0 - ADAM_B1 ** ADAM_STEP)
    vh = vn / (1.0 - ADAM_B2 ** ADAM_STEP)
    return -ADAM_LR * (mh / (jnp.sqrt(vh) + ADAM_EPS) + ADAM_WD * w), mn, vn


SMALL_PARAMS = ("b_ada", "norm1_g", "norm2_g", "final_g", "ret_decay", "conv_w", "conv_b", "lru_wa", "lru_ba", "lru_wx",
                "lru_bx", "lru_lambda")


def _finalize_small(chip_idx, slab_all, ga_all, gx_all, wmv):
    n_p = len(SMALL_PARAMS)
    flat = [a for nm in SMALL_PARAMS for a in wmv[nm]]
    ada_n = N_MOD * D_MODEL // N_CHIP

    def body(c_ref, slab_ref, ga_ref, gx_ref, *refs):
        prm = {nm: refs[3 * k:3 * k + 3] for k, nm in enumerate(SMALL_PARAMS)}
        outs = {nm: refs[3 * n_p + 4 * k:3 * n_p + 4 * k + 4] for k, nm in enumerate(SMALL_PARAMS)}
        b128_ref, dmc_ref, loss_ref = refs[3 * n_p + 4 * n_p:]
        chip = c_ref[0]

        def pick(fn):
            acc = fn(0)
            for j in range(1, N_CHIP):
                acc = jnp.where(chip == j, fn(j), acc)
            return acc

        tot = slab_ref[0]
        for d in range(1, N_DEV):
            tot = tot + slab_ref[d]

        def update(nm, g, sl=None):
            w_ref, m_ref, v_ref = prm[nm]
            g_ref, d_ref, mo_ref, vo_ref = outs[nm]
            ix = (slice(None), slice(None)) if sl is None else (slice(None), sl)
            dl, mn, vn = _adam_math(w_ref[ix], g, m_ref[ix], v_ref[ix])
            g_ref[ix] = g
            d_ref[ix] = dl
            mo_ref[ix] = mn
            vo_ref[ix] = vn

        loss_ref[...] = jnp.broadcast_to(tot[ROW_LOSS:ROW_LOSS + 1, 0:LANES], (SUBLANES, LANES))
        for k in range(N_MOD):
            g = tot[ROW_DMOD + k:ROW_DMOD + k + 1, :]
            if k < 2:
                g = g + tot[ROW_DMODC + k:ROW_DMODC + k + 1, :]
            update("b_ada", g, slice(D_MODEL * k, D_MODEL * (k + 1)))
        update("norm1_g", tot[ROW_N1:ROW_N1 + 1, :])
        update("norm2_g", tot[ROW_N2:ROW_N2 + 1, :])
        update("final_g", tot[ROW_FG:ROW_FG + 1, :])
        update("ret_decay", tot[ROW_RD:ROW_RD + SUBLANES, 0:LANES])
        update("conv_b", tot[ROW_CB:ROW_CB + 1, 0:LRU_W])
        update("conv_w", pick(lambda j: tot[ROW_CW:ROW_CW + 4, LANES * j:LANES * (j + 1)]))
        for nm, row in (("lru_ba", ROW_BA), ("lru_bx", ROW_BX), ("lru_lambda", ROW_LAM)):
            update(nm, pick(lambda j, row=row: tot[row:row + 2, LANES * j:LANES * (j + 1)]))
        for nm, g_all in (("lru_wa", ga_ref), ("lru_wx", gx_ref)):
            g = g_all[0]
            for d in range(1, N_DEV):
                g = g + g_all[d]
            update(nm, g)

        def seg(rows6, s):
            return rows6[s // 2][:, SEG * (s % 2):SEG * (s % 2 + 1)]

        b128_ref[...] = jnp.zeros_like(b128_ref)
        dmc_ref[...] = jnp.zeros_like(dmc_ref)
        zero = jnp.zeros((1, D_MODEL), F32)
        ctx6 = [tot[ROW_DMODC:ROW_DMODC + 1, :], tot[ROW_DMODC + 1:ROW_DMODC + 2, :]] + [zero] * (N_MOD - 2)
        for q in range(ada_n // SEG):
            cols = slice(SEG * q, SEG * (q + 1))
            for d in range(N_DEV):
                rows6 = [slab_ref[d, ROW_DMOD + k:ROW_DMOD + k + 1, :] for k in range(N_MOD)]
                b128_ref[d:d + 1, cols] = pick(lambda j, rows6=rows6: seg(rows6, 3 * j + q))
            c = pick(lambda j: seg(ctx6, 3 * j + q))
            b128_ref[N_DEV:N_DEV + 1, cols] = c
            dmc_ref[0:1, cols] = c

    out_shape = []
    for nm in SMALL_PARAMS:
        out_shape += [_sds(wmv[nm][0].shape)] * 4
    out_shape += [_sds((LANES, ada_n)), _sds((SUBLANES, ada_n)), _sds((SUBLANES, LANES))]
    args = [slab_all, ga_all, gx_all] + flat
    grid_spec = pltpu.PrefetchScalarGridSpec(
        num_scalar_prefetch=1, grid=(1,), in_specs=[_full(a.shape) for a in args],
        out_specs=[_full(s.shape) for s in out_shape])
    outs = _pc(body, name="finalize_small", grid_spec=grid_spec, out_shape=out_shape,
               compiler_params=_params("arbitrary"))(chip_idx, *args)
    res = {nm: tuple(outs[4 * k:4 * k + 4]) for k, nm in enumerate(SMALL_PARAMS)}
    return res, outs[4 * n_p], outs[4 * n_p + 1], outs[4 * n_p + 2]


def _block_diag(w):
    eye = jnp.eye(LRU_BLOCKS, dtype=F32)
    return (w[:, :, None, :] * eye[:, None, :, None]).reshape(LRU_W, LRU_W).astype(BF16)


def _lane_rep(v8):
    return jnp.broadcast_to(v8.reshape(SUBLANES, 1), (SUBLANES, LANES))


def kernel(x, c, ctx, c_ctx, w_ada, b_ada, norm1_g, norm2_g, w_in, ret_decay, conv_w, conv_b, lru_wa, lru_ba, lru_wx, lru_bx, lru_lambda, w_out, w_mlp1, w_mlp2, final_g, loss_target, m_c_ctx, m_w_ada, m_b_ada, m_norm1_g, m_norm2_g, m_w_in, m_ret_decay, m_conv_w, m_conv_b, m_lru_wa, m_lru_ba, m_lru_wx, m_lru_bx, m_lru_lambda, m_w_out, m_w_mlp1, m_w_mlp2, m_final_g, v_c_ctx, v_w_ada, v_b_ada, v_norm1_g, v_norm2_g, v_w_in, v_ret_decay, v_conv_w, v_conv_b, v_lru_wa, v_lru_ba, v_lru_wx, v_lru_bx, v_lru_lambda, v_w_out, v_w_mlp1, v_w_mlp2, v_final_g):
    ax, ay, ac = lax.axis_index("x"), lax.axis_index("y"), lax.axis_index("c")
    chip = 2 * ax + ay
    dev = 4 * ax + 2 * ay + ac
    c_idx = ac.reshape(1).astype(jnp.int32)
    j_idx = chip.reshape(1).astype(jnp.int32)

    xt = x[0]
    t_len = xt.shape[0]
    ctxt = ctx[0]
    l_len = ctxt.shape[0]
    tgt = loss_target[0]
    ada_n = w_ada.shape[2]

    def my_half(w2d):
        r = w2d.shape[0] // 2
        return lax.dynamic_slice_in_dim(w2d, ac * r, r, axis=0).astype(BF16)

    pad8 = lambda a: jnp.pad(a, ((0, SUBLANES - a.shape[0]), (0, 0)))
    small = jnp.concatenate([pad8(conv_w[0]), pad8(lru_ba[0]), pad8(lru_bx[0]), pad8(lru_lambda[0])], axis=0)
    gw_in, c_all, small_all = _all_gather([my_half(w_in[0]), pad8(c), small], "gather_head")
    w4 = gw_in.reshape(N_CHIP, D_MODEL, IN_COLS // N_CHIP)

    a16, lgv, sgv = _prep(c_all[:, 0, :], c_ctx, ret_decay[0])
    b_shard = lax.dynamic_slice_in_dim(b_ada, chip * ada_n, ada_n, axis=1)
    (mod_parts,) = _all_gather([_mod_fwd(a16, w_ada[0], b_shard)], "gather_mod")
    mod_all = mod_parts[0::2].transpose(1, 0, 2).reshape(16, N_CHIP * ada_n)
    mod_me = lax.dynamic_slice_in_dim(mod_all, dev, 1, axis=0)
    sh1, sc1, g1, sh2, sc2, g2 = [mod_me[:, D_MODEL * k:D_MODEL * (k + 1)] for k in range(N_MOD)]
    csh1, csc1 = mod_all[8:9, 0:D_MODEL], mod_all[8:9, D_MODEL:2 * D_MODEL]

    cos2, sin2 = _rotary_tables(t_len)
    cos_c, sin_c = jnp.ones((l_len, DH), F32), jnp.zeros((l_len, DH), F32)
    n1g, n2g = norm1_g, norm2_g
    fg = final_g.reshape(1, D_MODEL)

    small_full = small_all[0::2].transpose(1, 0, 2).reshape(4 * SUBLANES, LRU_W)
    cw = small_full[0:4]
    cb = conv_b
    ba_f, ba_b = small_full[8:9], small_full[9:10]
    bx_f, bx_b = small_full[16:17], small_full[17:18]
    lam_f, lam_b = small_full[24:25], small_full[25:26]
    wa_f, wa_b = _block_diag(lru_wa[0, 0]), _block_diag(lru_wa[0, 1])
    wx_f, wx_b = _block_diag(lru_wx[0, 0]), _block_diag(lru_wx[0, 1])
    zero_h = jnp.zeros((1, LRU_W), F32)

    projc, hcb16 = _inproj_fwd(ctxt, n1g, csh1, csc1, w4, cos_c, sin_c, "inproj_fwd_ctx")
    s_f, s_b = _ctx_state_fwd(projc, lgv)
    xcc = _conv_fwd(projc, cw, cb, "conv_fwd_ctx")
    hcf = _lru_fwd(xcc, wa_f, wx_f, ba_f, bx_f, lam_f, zero_h, False, "lru_fwd_ctx_f")
    hcbk = _lru_fwd(xcc, wa_b, wx_b, ba_b, bx_b, lam_b, zero_h, True, "lru_fwd_ctx_b")
    lru_sf, lru_sb = hcf[l_len - 1:l_len], hcbk[0:1]

    proj, hb16 = _inproj_fwd(xt, n1g, sh1, sc1, w4, cos2, sin2, "inproj_fwd")
    (o_f, o_b, spf, spb), ((gw_1,),) = _ret_fwd(proj, lgv, s_f, s_b, comms=(_AllGather([my_half(w_mlp1[0])]),))
    xcl = _conv_fwd(proj, cw, cb, "conv_fwd")
    hf, ((gw_2,),) = _lru_fwd(xcl, wa_f, wx_f, ba_f, bx_f, lam_f, lru_sf, False, "lru_fwd_f",
                             comms=(_AllGather([my_half(w_mlp2[0])]),))
    hbk, ((gw_out,),) = _lru_fwd(xcl, wa_b, wx_b, ba_b, bx_b, lam_b, lru_sb, True, "lru_fwd_b",
                                comms=(_AllGather([my_half(w_out[0])]),))
    wo = gw_out.reshape(D_MODEL, D_MODEL)
    w1 = gw_1.reshape(N_CHIP, D_MODEL, MLP_H // N_CHIP)
    w2 = gw_2.reshape(N_CHIP, MLP_H // N_CHIP, D_MODEL)
    x1, cat = _mix_fwd(o_f, o_b, proj, hf, hbk, wo, xt, g1)

    (dx1, h2b, ab, dub, dmb, dsc2, dsh2, dg2, dn2g, dfg, lossv) = _mlp(x1, n2g, sh2, sc2, g2, fg, w1, w2, tgt)
    gw_mlp1 = _tn(h2b, dub, N_CHIP, False, True, "grad_w_mlp1")
    b_1 = gw_mlp1.reshape(N_DEV, D_MODEL // 2, MLP_H // N_CHIP)
    gw_mlp2, ((r_1,),) = _tn(ab, dmb, N_CHIP, True, False, "grad_w_mlp2", comms=(_pair_exchange([b_1]),))

    jc_idx = jnp.concatenate([j_idx, c_idx])
    b_2 = gw_mlp2.reshape(N_DEV, MLP_H // N_DEV, D_MODEL)
    (do, dhs, dg, dgate, dyb, dg1), ((r_2,),) = _mix_bwd(
        o_f, o_b, proj, hf, hbk, wo, cat, dx1, g1, comms=(_pair_exchange([b_2]),))
    gw_o = _tn(cat, dyb, 1, False, False, "grad_w_out")
    b_o = gw_o.reshape(N_DEV, D_MODEL // N_DEV, D_MODEL)
    p_1, pb_1 = _pair_add(b_1, r_1, c_idx, "rs_pair_add_w_mlp1")
    p_2, pb_2 = _pair_add(b_2, r_2, c_idx, "rs_pair_add_w_mlp2")

    (dq_f, dk_f, dv_f, ds_f, drd_f), ((q_1,), (r_o,)) = _ret_bwd(
        proj, lgv, sgv, spf, do, False, "ret_bwd_f", comms=(_chip_exchange([pb_1]), _pair_exchange([b_o])))
    p_o, pb_o = _pair_add(b_o, r_o, c_idx, "rs_pair_add_w_out")
    h_1 = _chip_add(p_1, q_1, jc_idx, "rs_chip_add_w_mlp1")

    (dq_b, dk_b, dv_b, ds_b, drd_b), ((q_2,), (f_1,)) = _ret_bwd(
        proj, lgv, sgv, spb, do, True, "ret_bwd_b", comms=(_chip_exchange([pb_2]), _pair_gather([h_1])))
    h_2 = _chip_add(p_2, q_2, jc_idx, "rs_chip_add_w_mlp2")

    (dxc_f, dpre_f, dba_f, dbx_f, dlam_f, dh0_f), ((q_o,), (f_2,)) = _lru_bwd(
        xcl, wa_f, wx_f, ba_f, bx_f, lam_f, hf, lru_sf, dhs, False, "lru_bwd_f",
        comms=(_chip_exchange([pb_o]), _pair_gather([h_2])))
    h_o = _chip_add(p_o, q_o, jc_idx, "rs_chip_add_w_out")
    (dxc_b, dpre_b, dba_b, dbx_b, dlam_b, dh0_b), ((f_o,),) = _lru_bwd(
        xcl, wa_b, wx_b, ba_b, bx_b, lam_b, hbk, lru_sb, dhs, True, "lru_bwd_b", comms=(_pair_gather([h_o]),))
    dxr, dcw, dcb = _conv_bwd(dxc_f, dxc_b, proj, cw, "conv_bwd")
    grad_x, dpb, dn1g, dsh1, dsc1 = _inproj_bwd(
        xt, n1g, sh1, sc1, w4, cos2, sin2, [dq_f, dq_b, dk_f, dk_b, dv_f, dv_b, dg, dxr, dgate], dx1, "inproj_bwd")

    dkc, dvc, drd_c = _ctx_state_bwd(projc, lgv, sgv, ds_f, ds_b)
    zc = jnp.zeros((l_len, LRU_W), F32)
    dhc_f = lax.dynamic_update_slice(zc, dh0_f, (l_len - 1, 0))
    dhc_b = lax.dynamic_update_slice(zc, dh0_b, (0, 0))
    (dxcc_f, dprec_f, dbac_f, dbxc_f, dlamc_f, _), _ = _lru_bwd(
        xcc, wa_f, wx_f, ba_f, bx_f, lam_f, hcf, zero_h, dhc_f, False, "lru_bwd_ctx_f")
    (dxcc_b, dprec_b, dbac_b, dbxc_b, dlamc_b, _), _ = _lru_bwd(
        xcc, wa_b, wx_b, ba_b, bx_b, lam_b, hcbk, zero_h, dhc_b, True, "lru_bwd_ctx_b")
    dxrc, dcw_c, dcb_c = _conv_bwd(dxcc_f, dxcc_b, projc, cw, "conv_bwd_ctx")
    zr = jnp.zeros((l_len, RET_W), F32)
    _, dpbc, dn1g_c, dcsh1, dcsc1 = _inproj_bwd(
        ctxt, n1g, csh1, csc1, w4, cos_c, sin_c, [zr, zr, dkc, zr, dvc, zr, zr, dxrc, zr],
        jnp.zeros((l_len, D_MODEL), F32), "inproj_bwd_ctx")

    gw_i = _tn(hb16, dpb, N_CHIP, False, True, "grad_w_in", extra=(hcb16, dpbc))
    b_i = gw_i.reshape(N_DEV, D_MODEL // 2, IN_COLS // N_CHIP)
    gwa_f, ((r_i,),) = _tn(xcl, dpre_f, 2, False, True, "grad_lru_gates_f", extra=(xcc, dprec_f),
                           comms=(_pair_exchange([b_i]),))
    p_i, pb_i = _pair_add(b_i, r_i, c_idx, "rs_pair_add_w_in")
    gwa_b, ((q_i,),) = _tn(xcl, dpre_b, 2, False, True, "grad_lru_gates_b", extra=(xcc, dprec_b),
                           comms=(_chip_exchange([pb_i]),))
    h_i = _chip_add(p_i, q_i, jc_idx, "rs_chip_add_w_in")
    g_out, g_1, g_2 = _shard_of(f_o), _shard_of(f_1), _shard_of(f_2)
    big = {}
    (d_, mn, vn), ((f_i,),) = _adamw(w_mlp1[0], g_1, m_w_mlp1[0], v_w_mlp1[0], "adamw_w_mlp1",
                                     comms=(_pair_gather([h_i]),))
    big["w_mlp1"] = (g_1[None], d_[None], mn[None], vn[None])
    g_in = _shard_of(f_i)
    for nm, w, g, m, v in (("w_in", w_in, g_in, m_w_in, v_w_in), ("w_out", w_out, g_out, m_w_out, v_w_out),
                           ("w_mlp2", w_mlp2, g_2, m_w_mlp2, v_w_mlp2)):
        d_, mn, vn = _adamw(w[0], g, m[0], v[0], "adamw_" + nm)
        big[nm] = (g[None], d_[None], mn[None], vn[None])

    slab, ga, gx = _pack_small(
        [lossv, dsh1, dsc1, dg1, dsh2, dsc2, dg2, dcsh1, dcsc1, dn1g, dn1g_c, dn2g, dfg],
        (drd_f, drd_b, drd_c), (dcw, dcw_c), (dcb, dcb_c),
        (dba_f, dbac_f, dba_b, dbac_b, dbx_f, dbxc_f, dbx_b, dbxc_b, dlam_f, dlamc_f, dlam_b, dlamc_b),
        (gwa_f, gwa_b))
    slab_all, ga_all, gx_all = _all_gather([slab, ga, gx], "gather_small_grads")
    params = {
        "b_ada": (b_ada, m_b_ada, v_b_ada), "norm1_g": (norm1_g, m_norm1_g, v_norm1_g),
        "norm2_g": (norm2_g, m_norm2_g, v_norm2_g), "final_g": (final_g, m_final_g, v_final_g),
        "ret_decay": (ret_decay, m_ret_decay, v_ret_decay), "conv_w": (conv_w, m_conv_w, v_conv_w),
        "conv_b": (conv_b, m_conv_b, v_conv_b), "lru_wa": (lru_wa, m_lru_wa, v_lru_wa),
        "lru_ba": (lru_ba, m_lru_ba, v_lru_ba), "lru_wx": (lru_wx, m_lru_wx, v_lru_wx),
        "lru_bx": (lru_bx, m_lru_bx, v_lru_bx), "lru_lambda": (lru_lambda, m_lru_lambda, v_lru_lambda),
    }
    as2d = {
        "b_ada": lambda a: a, "norm1_g": lambda a: a, "norm2_g": lambda a: a, "conv_b": lambda a: a,
        "final_g": lambda a: a.reshape(1, D_MODEL), "ret_decay": lambda a: _lane_rep(a.reshape(-1)),
        "conv_w": lambda a: a[0], "lru_ba": lambda a: a[0], "lru_bx": lambda a: a[0], "lru_lambda": lambda a: a[0],
        "lru_wa": lambda a: a.reshape(2 * LRU_W, LRU_BD), "lru_wx": lambda a: a.reshape(2 * LRU_W, LRU_BD),
    }
    res, b128, dmc8, loss8 = _finalize_small(
        j_idx, slab_all, ga_all, gx_all, {nm: tuple(as2d[nm](a) for a in params[nm]) for nm in SMALL_PARAMS})
    loss = loss8[0, 0]
    small_out = {}
    for nm in SMALL_PARAMS:
        shp = params[nm][0].shape
        if nm == "ret_decay":
            small_out[nm] = tuple(o[:, 0].reshape(shp) for o in res[nm])
        else:
            small_out[nm] = tuple(o.reshape(shp) for o in res[nm])

    g_ada = _ada_grad(jnp.pad(a16.T, ((0, 0), (0, LANES - 16))), b128)
    d_ada, m_ada, v_ada = _adamw(w_ada[0], g_ada, m_w_ada[0], v_w_ada[0], "adamw_w_ada")

    (cparts,) = _all_gather([_cctx_partial(dmc8, w_ada[0])], "gather_cctx")
    g_cc, d_cc, m_cc, v_cc = _cctx_final(cparts, c_ctx, m_c_ctx, v_c_ctx)
    small_out["c_ctx"] = tuple(a.reshape(D_MODEL) for a in (g_cc, d_cc, m_cc, v_cc))
    small_out["w_ada"] = (g_ada[None], d_ada[None], m_ada[None], v_ada[None])
    small_out.update(big)

    order = ["c_ctx", "w_ada", "b_ada", "norm1_g", "norm2_g", "w_in", "ret_decay", "conv_w", "conv_b", "lru_wa", "lru_ba",
             "lru_wx", "lru_bx", "lru_lambda", "w_out", "w_mlp1", "w_mlp2", "final_g"]
    outs = [loss, grad_x[None]]
    for k in range(4):
        outs += [small_out[nm][k] for nm in order]
    return tuple(outs)
```

```python
import math

import jax
import jax.numpy as jnp
from jax import lax
from jax.experimental import pallas as pl
from jax.experimental.pallas import tpu as pltpu

F32 = jnp.float32
BF16 = jnp.bfloat16

D_MODEL = 1024
HEADS = 4
DH = 128
CHUNK = 128
RET_W = HEADS * DH
LRU_W = 512
LRU_BLOCKS = 8
LRU_BD = LRU_W // LRU_BLOCKS
LRU_C = 8.0
IN_COLS = 4 * RET_W + 2 * LRU_W
MLP_H = 4 * D_MODEL
N_MOD = 6
GRID_W = 64
ROPE_BASE = 10000.0
K_SCALE = DH ** -0.5
EPS = 1e-6
GELU_K = math.sqrt(2.0 / math.pi)
GELU_C = 0.044715

ADAM_LR = 0.001
ADAM_B1 = 0.9
ADAM_B2 = 0.999
ADAM_EPS = 1e-08
ADAM_WD = 0.01
ADAM_STEP = 10

N_DEV = 8
N_CHIP = 4
SUBLANES = 8
LANES = 128
VMEM_LIMIT_V7X = 56 * 1024 * 1024
MESH = pl.DeviceIdType.MESH
ANY = pl.BlockSpec(memory_space=pl.ANY)


def _pc(body, **kw):
    return pl.pallas_call(body, **kw)


def _params(*sem):
    return pltpu.CompilerParams(dimension_semantics=sem if sem else None, vmem_limit_bytes=VMEM_LIMIT_V7X)


def _tile(t, big=False):
    if big and t >= 1024:
        return 512
    return 256 if t >= 256 else t


def _sds(shape, dtype=F32):
    return jax.ShapeDtypeStruct(tuple(shape), dtype)


def _full(shape):
    nd = len(shape)
    return pl.BlockSpec(tuple(shape), lambda *_: (0,) * nd)


def _sigmoid(x):
    return 1.0 / (1.0 + jnp.exp(-x))


def _log1p_pos(y):
    s = y * (1.0 - y * (0.5 - y * (1.0 / 3.0 - y * (0.25 - y * (0.2 - y / 6.0)))))
    return jnp.where(y < 0.03, s, jnp.log(1.0 + y))


def _softplus(z):
    return jnp.maximum(z, 0.0) + _log1p_pos(jnp.exp(-jnp.abs(z)))


def _neg_expm1(x, exp_x):
    t = x * (1.0 + x * (0.5 + x * (1.0 / 6.0 + x * (1.0 / 24.0 + x * (1.0 / 120.0 + x * (1.0 / 720.0 + x / 5040.0))))))
    return -jnp.where(x > -0.25, t, exp_x - 1.0)


def _rms(x):
    r = lax.rsqrt(jnp.mean(x * x, axis=-1, keepdims=True) + EPS)
    return x * r, r


def _dot(a, b):
    return jnp.dot(a, b, preferred_element_type=F32)


def _dot_nt(a, b):
    return lax.dot_general(a, b, (((1,), (1,)), ((), ())), preferred_element_type=F32)


def _dot_tn(a, b):
    return lax.dot_general(a, b, (((0,), (0,)), ((), ())), preferred_element_type=F32)


def _sum0(x):
    return jnp.sum(x, axis=0, keepdims=True)


def _norm_mod_bwd(x, g, sc, dh):
    xh, r = _rms(x)
    hn = xh * g
    dhn = dh * (1.0 + sc)
    dxh = dhn * g
    dx = r * (dxh - xh * jnp.mean(dxh * xh, axis=-1, keepdims=True))
    return dx, _sum0(dhn * xh), _sum0(dh), _sum0(dh * hn)


def _dev_index(p):
    return 4 * p[0] + 2 * p[1] + p[2]


def _mesh_pos():
    return lax.axis_index("x"), lax.axis_index("y"), lax.axis_index("c")


class _AllGather:
    def __init__(self, arrs):
        n = len(arrs)
        self.arrays = list(arrs)
        self.out_shapes = [_sds((N_DEV,) + a.shape, a.dtype) for a in arrs]
        self.scratch = ([pltpu.VMEM(a.shape, a.dtype) for a in arrs]
                        + [pltpu.SemaphoreType.DMA((7 * n,)), pltpu.SemaphoreType.DMA((7 * n,)),
                           pltpu.SemaphoreType.DMA((n,))])
        self.aliases = {}

    def _parts(self, ins, outs, scr):
        n = len(self.arrays)
        stage = scr[:n]
        send_sems, recv_sems, local_sems = scr[n:]
        x, y, c = _mesh_pos()
        me, sib = (x, y, c), (x, y, 1 - c)
        chips = [(1 - x, y), (x, 1 - y), (1 - x, 1 - y)]

        def copy(t, k, block, to, own=False):
            dst = outs[t].at[_dev_index(block)]
            return pltpu.make_async_remote_copy(
                src_ref=ins[t] if own else dst, dst_ref=dst,
                send_sem=send_sems.at[7 * t + k], recv_sem=recv_sems.at[7 * t + k],
                device_id=to, device_id_type=MESH)

        first = []
        for t in range(n):
            first.append(copy(t, 0, me, sib, own=True))
            for j, ch in enumerate(chips):
                first.append(copy(t, 1 + j, me, (*ch, c), own=True))
        stage_in = [pltpu.make_async_copy(ins[t], stage[t], local_sems.at[t]) for t in range(n)]
        mine = [pltpu.make_async_copy(stage[t], outs[t].at[_dev_index(me)], local_sems.at[t]) for t in range(n)]
        return n, c, me, sib, chips, copy, first, stage_in, mine

    def start(self, ins, outs, scr):
        n, _, _, _, _, _, first, stage_in, mine = self._parts(ins, outs, scr)
        for cp in stage_in:
            cp.start()
        for cp in first:
            cp.start()
        for t in range(n):
            stage_in[t].wait()
            mine[t].start()

    def finish(self, ins, outs, scr):
        n, c, me, sib, chips, copy, first, _, mine = self._parts(ins, outs, scr)
        passed = []
        for j, ch in enumerate(chips):
            for t in range(n):
                copy(t, 1 + j, (*ch, c), me).wait_recv()
                p = copy(t, 4 + j, (*ch, c), sib)
                p.start()
                passed.append(p)
        for t in range(n):
            copy(t, 0, sib, me).wait_recv()
            for j, ch in enumerate(chips):
                copy(t, 4 + j, (*ch, 1 - c), me).wait_recv()
        for cp in first + passed:
            cp.wait_send()
        for cp in mine:
            cp.wait()


class _Exchange:
    def __init__(self, arrays, out_shapes, plan, n_copies, aliases=None):
        self.arrays = list(arrays)
        self.out_shapes = list(out_shapes)
        self.plan = plan
        self.scratch = [pltpu.SemaphoreType.DMA((n_copies,)), pltpu.SemaphoreType.DMA((n_copies,))]
        self.aliases = aliases or {}

    def _copies(self, ins, outs, scr):
        send_sems, recv_sems = scr
        snd, rcv = [], []
        for i, (src, dst, peer, lands) in enumerate(self.plan(ins, outs, _mesh_pos())):
            kw = dict(send_sem=send_sems.at[i], recv_sem=recv_sems.at[i], device_id=peer, device_id_type=MESH)
            snd.append(pltpu.make_async_remote_copy(src_ref=src, dst_ref=dst, **kw))
            rcv.append(pltpu.make_async_remote_copy(src_ref=src, dst_ref=lands, **kw))
        return snd, rcv

    def start(self, ins, outs, scr):
        for cp in self._copies(ins, outs, scr)[0]:
            cp.start()

    def finish(self, ins, outs, scr):
        snd, rcv = self._copies(ins, outs, scr)
        for cp in rcv:
            cp.wait_recv()
        for cp in snd:
            cp.wait_send()


def _pair_exchange(grads):
    n = len(grads)

    def plan(ins, outs, pos):
        x, y, c = pos
        return [(ins[t].at[2 * j + (1 - c)], outs[t].at[j], (x, y, 1 - c), outs[t].at[j])
                for t in range(n) for j in range(N_CHIP)]

    return _Exchange(grads, [_sds((N_CHIP,) + g.shape[1:], g.dtype) for g in grads], plan, N_CHIP * n)


def _chip_exchange(parts):
    n = len(parts)

    def plan(ins, outs, pos):
        x, y, c = pos
        chips = [(1 - x, y), (x, 1 - y), (1 - x, 1 - y)]
        return [(ins[t].at[2 * ch[0] + ch[1]], outs[t].at[k], (*ch, c), outs[t].at[k])
                for t in range(n) for k, ch in enumerate(chips)]

    return _Exchange(parts, [_sds((3,) + p.shape[1:], p.dtype) for p in parts], plan, 3 * n)


def _pair_gather(bufs):
    n = len(bufs)

    def plan(ins, outs, pos):
        x, y, c = pos
        return [(ins[t].at[c], outs[t].at[c], (x, y, 1 - c), outs[t].at[1 - c]) for t in range(n)]

    return _Exchange(bufs, [_sds(b.shape, b.dtype) for b in bufs], plan, n, aliases={t: t for t in range(n)})


def _run_comm(comm, name):
    n_in, n_out = len(comm.arrays), len(comm.out_shapes)

    def body(*refs):
        ins, outs, scr = refs[:n_in], refs[n_in:n_in + n_out], refs[n_in + n_out:]
        comm.start(ins, outs, scr)
        comm.finish(ins, outs, scr)

    outs = _pc(body, name=name, out_shape=comm.out_shapes, in_specs=[ANY] * n_in, out_specs=[ANY] * n_out,
               input_output_aliases=dict(comm.aliases), scratch_shapes=comm.scratch,
               compiler_params=_params())(*comm.arrays)
    return list(outs)


def _all_gather(arrs, name):
    return _run_comm(_AllGather(arrs), name)


def _call(body, *, name, grid, in_specs, out_specs, out_shape, scratch_shapes, sem, args, comms=()):
    n_in, n_out, n_scr = len(in_specs), len(out_specs), len(scratch_shapes)
    c_in = [len(cm.arrays) for cm in comms]
    c_out = [len(cm.out_shapes) for cm in comms]
    c_scr = [len(cm.scratch) for cm in comms]
    aliases = {}
    for k, cm in enumerate(comms):
        for a, b in cm.aliases.items():
            aliases[n_in + sum(c_in[:k]) + a] = n_out + sum(c_out[:k]) + b

    def split(refs, counts):
        out, pos = [], 0
        for cnt in counts:
            out.append(refs[pos:pos + cnt])
            pos += cnt
        return out

    def wrapped(*refs):
        ins = refs[:n_in + sum(c_in)]
        outs = refs[len(ins):len(ins) + n_out + sum(c_out)]
        scr = refs[len(ins) + len(outs):]
        cins, couts, cscr = split(ins[n_in:], c_in), split(outs[n_out:], c_out), split(scr[n_scr:], c_scr)
        if comms:
            first = pl.program_id(0) == 0
            last = pl.program_id(0) == grid[0] - 1
            for k in range(1, len(grid)):
                first = jnp.logical_and(first, pl.program_id(k) == 0)
                last = jnp.logical_and(last, pl.program_id(k) == grid[k] - 1)

            @pl.when(first)
            def _():
                for k, cm in enumerate(comms):
                    cm.start(cins[k], couts[k], cscr[k])
        body(*ins[:n_in], *outs[:n_out], *scr[:n_scr])
        if comms:
            @pl.when(last)
            def _():
                for k, cm in enumerate(comms):
                    cm.finish(cins[k], couts[k], cscr[k])

    outs = _pc(wrapped, name=name, grid=grid,
               in_specs=list(in_specs) + [ANY] * sum(c_in), out_specs=list(out_specs) + [ANY] * sum(c_out),
               out_shape=list(out_shape) + [s for cm in comms for s in cm.out_shapes],
               scratch_shapes=list(scratch_shapes) + [s for cm in comms for s in cm.scratch],
               input_output_aliases=aliases, compiler_params=_params(*sem),
               )(*args, *[a for cm in comms for a in cm.arrays])
    outs = list(outs)
    return outs[:n_out], split(outs[n_out:], c_out)


def _row_block(r):
    for b in (512, 256, 128, 64, 32, 16, 8):
        if r % b == 0:
            return b
    return r


def _pair_add(g, recv, c_idx, name):
    _, r, cc = g.shape
    br = _row_block(r)

    def body(c_ref, g_ref, r_ref, p_ref, pb_ref):
        s = g_ref[...] + r_ref[...]
        p_ref[...] = s
        pb_ref[...] = s.astype(BF16)

    grid_spec = pltpu.PrefetchScalarGridSpec(
        num_scalar_prefetch=1, grid=(N_CHIP, r // br),
        in_specs=[pl.BlockSpec((1, br, cc), lambda j, i, c_ref: (2 * j + c_ref[0], i, 0)),
                  pl.BlockSpec((1, br, cc), lambda j, i, c_ref: (j, i, 0))],
        out_specs=[pl.BlockSpec((1, br, cc), lambda j, i, c_ref: (j, i, 0)),
                   pl.BlockSpec((1, br, cc), lambda j, i, c_ref: (j, i, 0))])
    return _pc(body, name=name, grid_spec=grid_spec,
               out_shape=[_sds((N_CHIP, r, cc)), _sds((N_CHIP, r, cc), BF16)],
               compiler_params=_params("arbitrary", "arbitrary"))(c_idx, g, recv)


def _chip_add(p, q, jc_idx, name):
    _, r, cc = p.shape
    br = _row_block(r)

    def body(jc_ref, p_ref, q_ref, o_ref):
        o_ref[0] = ((p_ref[0] + q_ref[0].astype(F32)) + q_ref[1].astype(F32)) + q_ref[2].astype(F32)

    grid_spec = pltpu.PrefetchScalarGridSpec(
        num_scalar_prefetch=1, grid=(r // br,),
        in_specs=[pl.BlockSpec((1, br, cc), lambda i, jc_ref: (jc_ref[0], i, 0)),
                  pl.BlockSpec((3, br, cc), lambda i, jc_ref: (0, i, 0))],
        out_specs=pl.BlockSpec((1, br, cc), lambda i, jc_ref: (jc_ref[1], i, 0)))
    return _pc(body, name=name, grid_spec=grid_spec, out_shape=_sds((2, r, cc)),
               compiler_params=_params("arbitrary"))(jc_idx, p, q)


def _shard_of(both):
    return both.reshape((2 * both.shape[1],) + both.shape[2:])


def _adamw(w, g, m, v, name, comms=()):
    r, cc = w.shape
    br = _row_block(r)
    if r * cc * 4 <= (1 << 20):
        br = r
    elif br * cc * 4 > (1 << 20) and br > 8:
        br = max(8, (1 << 20) // (cc * 4) // 8 * 8)
        while r % br:
            br -= 8
    c1 = 1.0 - ADAM_B1 ** ADAM_STEP
    c2 = 1.0 - ADAM_B2 ** ADAM_STEP

    def body(w_ref, g_ref, m_ref, v_ref, d_ref, mo_ref, vo_ref):
        gg = g_ref[...]
        mn = ADAM_B1 * m_ref[...] + (1.0 - ADAM_B1) * gg
        vn = ADAM_B2 * v_ref[...] + (1.0 - ADAM_B2) * (gg * gg)
        mh = mn / c1
        vh = vn / c2
        d_ref[...] = -ADAM_LR * (mh / (jnp.sqrt(vh) + ADAM_EPS) + ADAM_WD * w_ref[...])
        mo_ref[...] = mn
        vo_ref[...] = vn

    spec = pl.BlockSpec((br, cc), lambda i: (i, 0))
    outs, couts = _call(body, name=name, grid=(r // br,), in_specs=[spec] * 4, out_specs=[spec] * 3,
                        out_shape=[_sds((r, cc))] * 3, scratch_shapes=[], sem=("arbitrary",), args=(w, g, m, v),
                        comms=comms)
    return (outs, couts) if comms else outs


def _prep(c_all, c_ctx, ret_decay):
    def body(c_ref, cc_ref, rd_ref, a_ref, lg_ref, sg_ref):
        ca = c_ref[...]
        cc = cc_ref[...]
        a_ref[...] = jnp.zeros_like(a_ref)
        a_ref[0:8, :] = ca * _sigmoid(ca)
        a_ref[8:9, :] = cc * _sigmoid(cc)
        rd = rd_ref[...]
        lg_ref[...] = -_softplus(-rd)
        sg_ref[...] = _sigmoid(-rd)

    rd = jnp.broadcast_to(ret_decay.reshape(2, HEADS).T[:, :, None], (HEADS, 2, LANES))
    return _pc(body, name="prep",
               out_shape=[_sds((16, D_MODEL)), _sds((HEADS, 2, LANES)), _sds((HEADS, 2, LANES))],
               in_specs=[_full((8, D_MODEL)), _full((1, D_MODEL)), _full((HEADS, 2, LANES))],
               out_specs=[_full((16, D_MODEL)), _full((HEADS, 2, LANES)), _full((HEADS, 2, LANES))],
               compiler_params=_params())(c_all, c_ctx.reshape(1, D_MODEL), rd)


def _mod_fwd(a16, w_ada, b_shard):
    n = w_ada.shape[1]
    bn = 512

    def body(a_ref, w_ref, b_ref, o_ref):
        o_ref[...] = jnp.dot(a_ref[...], w_ref[...], preferred_element_type=F32,
                             precision=lax.Precision.HIGHEST) + b_ref[...]

    return _pc(body, name="mod_fwd", grid=(n // bn,),
               in_specs=[_full((16, D_MODEL)), pl.BlockSpec((D_MODEL, bn), lambda i: (0, i)),
                         pl.BlockSpec((1, bn), lambda i: (0, i))],
               out_specs=pl.BlockSpec((16, bn), lambda i: (0, i)), out_shape=_sds((16, n)),
               compiler_params=_params("arbitrary"))(a16, w_ada, b_shard)


def _ada_grad(at, b):
    n = b.shape[1]
    bn = 512

    def body(a_ref, b_ref, o_ref):
        o_ref[...] = jnp.dot(a_ref[...], b_ref[...], preferred_element_type=F32, precision=lax.Precision.HIGHEST)

    return _pc(body, name="ada_grad", grid=(n // bn,),
               in_specs=[_full((D_MODEL, LANES)), pl.BlockSpec((LANES, bn), lambda i: (0, i))],
               out_specs=pl.BlockSpec((D_MODEL, bn), lambda i: (0, i)), out_shape=_sds((D_MODEL, n)),
               compiler_params=_params("arbitrary"))(at, b)


def _cctx_partial(dmc8, w_ada):
    n = w_ada.shape[1]
    bn = 512

    def body(d_ref, w_ref, o_ref):
        @pl.when(pl.program_id(0) == 0)
        def _():
            o_ref[...] = jnp.zeros_like(o_ref)
        o_ref[...] += lax.dot_general(d_ref[...], w_ref[...], (((1,), (1,)), ((), ())),
                                      preferred_element_type=F32, precision=lax.Precision.HIGHEST)

    return _pc(body, name="cctx_partial", grid=(n // bn,),
               in_specs=[pl.BlockSpec((8, bn), lambda i: (0, i)), pl.BlockSpec((D_MODEL, bn), lambda i: (0, i))],
               out_specs=_full((8, D_MODEL)), out_shape=_sds((8, D_MODEL)),
               compiler_params=_params("arbitrary"))(dmc8, w_ada)


def _cctx_final(parts, c_ctx, m, v):
    c1 = 1.0 - ADAM_B1 ** ADAM_STEP
    c2 = 1.0 - ADAM_B2 ** ADAM_STEP

    def body(p_ref, c_ref, m_ref, v_ref, g_ref, d_ref, mo_ref, vo_ref):
        s = ((p_ref[0, 0:1, :] + p_ref[2, 0:1, :]) + p_ref[4, 0:1, :]) + p_ref[6, 0:1, :]
        z = c_ref[...]
        sg = _sigmoid(z)
        gg = s * (sg * (1.0 + z * (1.0 - sg)))
        g_ref[...] = gg
        mn = ADAM_B1 * m_ref[...] + (1.0 - ADAM_B1) * gg
        vn = ADAM_B2 * v_ref[...] + (1.0 - ADAM_B2) * (gg * gg)
        d_ref[...] = -ADAM_LR * ((mn / c1) / (jnp.sqrt(vn / c2) + ADAM_EPS) + ADAM_WD * z)
        mo_ref[...] = mn
        vo_ref[...] = vn

    row = _full((1, D_MODEL))
    return _pc(body, name="cctx_final", out_shape=[_sds((1, D_MODEL))] * 4,
               in_specs=[_full(parts.shape), row, row, row], out_specs=[row] * 4,
               compiler_params=_params())(parts, c_ctx.reshape(1, D_MODEL), m.reshape(1, D_MODEL), v.reshape(1, D_MODEL))


def _rotary_tables(t_len):
    rows = t_len // GRID_W
    row = jnp.repeat(jnp.arange(rows, dtype=F32), GRID_W)
    col = jnp.tile(jnp.arange(GRID_W, dtype=F32), rows)
    n_freq = DH // 4
    inv = ROPE_BASE ** (-jnp.arange(n_freq, dtype=F32) / n_freq)
    ang = jnp.concatenate([row[:, None] * inv, col[:, None] * inv], axis=-1)
    cos, sin = jnp.cos(ang), jnp.sin(ang)
    return jnp.concatenate([cos, cos], axis=-1), jnp.concatenate([-sin, sin], axis=-1)


def _inproj_fwd(x, gn, sh, sc, w4, cos2, sin2, name, comms=()):
    t = x.shape[0]
    tm = _tile(t, True)
    nc = IN_COLS // N_CHIP

    def body(x_ref, gn_ref, sh_ref, sc_ref, w_ref, c_ref, s_ref, p_ref, hb_ref):
        xh, _ = _rms(x_ref[...])
        h = xh * gn_ref[...] * (1.0 + sc_ref[...]) + sh_ref[...]
        hb = h.astype(BF16)
        hb_ref[...] = hb
        for j in range(N_CHIP):
            p_ref[:, nc * j:nc * (j + 1)] = _dot(hb, w_ref[j])
        cc = c_ref[...]
        ss = s_ref[...]
        for hh in range(2 * HEADS):
            blk = p_ref[:, DH * hh:DH * (hh + 1)]
            rot = blk * cc + pltpu.roll(blk, DH // 2, 1) * ss
            if hh >= HEADS:
                rot = rot * K_SCALE
            p_ref[:, DH * hh:DH * (hh + 1)] = rot

    row = _full((1, D_MODEL))
    outs, couts = _call(
        body, name=name, grid=(t // tm,),
        in_specs=[pl.BlockSpec((tm, D_MODEL), lambda i: (i, 0)), row, row, row, _full(w4.shape),
                  pl.BlockSpec((tm, DH), lambda i: (i, 0)), pl.BlockSpec((tm, DH), lambda i: (i, 0))],
        out_specs=[pl.BlockSpec((tm, IN_COLS), lambda i: (i, 0)), pl.BlockSpec((tm, D_MODEL), lambda i: (i, 0))],
        out_shape=[_sds((t, IN_COLS)), _sds((t, D_MODEL), BF16)], scratch_shapes=[], sem=("arbitrary",),
        args=(x, gn, sh, sc, w4, cos2, sin2), comms=comms)
    return (outs, couts) if comms else outs


def _inproj_bwd(x, gn, sh, sc, w4, cos2, sin2, pieces, dres, name):
    t = x.shape[0]
    tm = _tile(t)
    nc = IN_COLS // N_CHIP

    def body(x_ref, gn_ref, sh_ref, sc_ref, w_ref, c_ref, s_ref, dqf, dqb, dkf, dkb, dvf, dvb, dg, dxr, dgt, dres_ref,
             dx_ref, dpb_ref, dgn_ref, dsh_ref, dsc_ref):
        cc = c_ref[...]
        ss = s_ref[...]
        dq = dqf[...] + dqb[...]
        dk = dkf[...] + dkb[...]
        for hh in range(HEADS):
            sl = slice(DH * hh, DH * (hh + 1))
            b = dq[:, sl]
            dpb_ref[:, sl] = (b * cc + pltpu.roll(b * ss, DH // 2, 1)).astype(BF16)
            b = dk[:, sl]
            dpb_ref[:, RET_W + DH * hh:RET_W + DH * (hh + 1)] = (
                (b * cc + pltpu.roll(b * ss, DH // 2, 1)) * K_SCALE).astype(BF16)
        dpb_ref[:, 2 * RET_W:3 * RET_W] = (dvf[...] + dvb[...]).astype(BF16)
        dpb_ref[:, 3 * RET_W:4 * RET_W] = dg[...].astype(BF16)
        dpb_ref[:, 4 * RET_W:4 * RET_W + LRU_W] = dxr[...].astype(BF16)
        dpb_ref[:, 4 * RET_W + LRU_W:IN_COLS] = dgt[...].astype(BF16)
        dh = _dot_nt(dpb_ref[:, 0:nc], w_ref[0])
        for j in range(1, N_CHIP):
            dh = dh + _dot_nt(dpb_ref[:, nc * j:nc * (j + 1)], w_ref[j])
        dx, dgn_t, dsh_t, dsc_t = _norm_mod_bwd(x_ref[...], gn_ref[...], sc_ref[...], dh)
        dx_ref[...] = dres_ref[...] + dx

        @pl.when(pl.program_id(0) == 0)
        def _():
            dgn_ref[...] = jnp.zeros_like(dgn_ref)
            dsh_ref[...] = jnp.zeros_like(dsh_ref)
            dsc_ref[...] = jnp.zeros_like(dsc_ref)
        dgn_ref[...] += dgn_t
        dsh_ref[...] += dsh_t
        dsc_ref[...] += dsc_t

    row = _full((1, D_MODEL))
    pc = pl.BlockSpec((tm, RET_W), lambda i: (i, 0))
    big = pl.BlockSpec((tm, D_MODEL), lambda i: (i, 0))
    return _pc(body, name=name, grid=(t // tm,),
               in_specs=[big, row, row, row, _full(w4.shape),
                         pl.BlockSpec((tm, DH), lambda i: (i, 0)), pl.BlockSpec((tm, DH), lambda i: (i, 0))]
               + [pc] * 9 + [big],
               out_specs=[big, pl.BlockSpec((tm, IN_COLS), lambda i: (i, 0)), row, row, row],
               out_shape=[_sds((t, D_MODEL)), _sds((t, IN_COLS), BF16), _sds((1, D_MODEL)), _sds((1, D_MODEL)),
                          _sds((1, D_MODEL))],
               compiler_params=_params("arbitrary"))(x, gn, sh, sc, w4, cos2, sin2, *pieces, dres)


XR_BLOCK = (4 * RET_W) // LRU_W


def _halo_specs(t, tm, col):
    n8 = tm // SUBLANES
    last8 = t // SUBLANES - 1
    prev = pl.BlockSpec((SUBLANES, LRU_W), lambda i: (jnp.maximum(i * n8 - 1, 0), col))
    main = pl.BlockSpec((tm, LRU_W), lambda i: (i, col))
    nxt = pl.BlockSpec((SUBLANES, LRU_W), lambda i: (jnp.minimum((i + 1) * n8, last8), col))
    return prev, main, nxt


def _with_halo(prev_ref, main_ref, next_ref, i, nt):
    prev = jnp.where(i > 0, prev_ref[...], 0.0)
    nxt = jnp.where(i < nt - 1, next_ref[...], 0.0)
    return jnp.concatenate([prev, main_ref[...], nxt], axis=0)


def _conv_fwd(proj, cw, cb, name):
    t = proj.shape[0]
    tm = _tile(t, True)
    nt = t // tm
    n = tm + 2 * SUBLANES
    mid = slice(SUBLANES, SUBLANES + tm)

    def body(p_ref, m_ref, n_ref, w_ref, b_ref, o_ref):
        xp = _with_halo(p_ref, m_ref, n_ref, pl.program_id(0), nt)
        acc = b_ref[...] + pltpu.roll(xp, 1, 0)[mid] * w_ref[0:1, :]
        acc = acc + xp[mid] * w_ref[1:2, :]
        acc = acc + pltpu.roll(xp, n - 1, 0)[mid] * w_ref[2:3, :]
        acc = acc + pltpu.roll(xp, n - 2, 0)[mid] * w_ref[3:4, :]
        o_ref[...] = acc

    return _pc(body, name=name, grid=(nt,),
               in_specs=[*_halo_specs(t, tm, XR_BLOCK), _full((4, LRU_W)), _full((1, LRU_W))],
               out_specs=pl.BlockSpec((tm, LRU_W), lambda i: (i, 0)), out_shape=_sds((t, LRU_W)),
               compiler_params=_params("arbitrary"))(proj, proj, proj, cw, cb)


def _conv_bwd(dxc_a, dxc_b, proj, cw, name):
    t = proj.shape[0]
    tm = _tile(t, True)
    nt = t // tm
    n = tm + 2 * SUBLANES
    mid = slice(SUBLANES, SUBLANES + tm)

    def body(ap_ref, am_ref, an_ref, bp_ref, bm_ref, bn_ref, xp_ref, xm_ref, xn_ref, w_ref, dx_ref, dw_ref, db_ref):
        i = pl.program_id(0)
        dp = _with_halo(ap_ref, am_ref, an_ref, i, nt) + _with_halo(bp_ref, bm_ref, bn_ref, i, nt)
        xp = _with_halo(xp_ref, xm_ref, xn_ref, i, nt)
        dx = pltpu.roll(dp, n - 1, 0)[mid] * w_ref[0:1, :]
        dx = dx + dp[mid] * w_ref[1:2, :]
        dx = dx + pltpu.roll(dp, 1, 0)[mid] * w_ref[2:3, :]
        dx = dx + pltpu.roll(dp, 2, 0)[mid] * w_ref[3:4, :]
        dx_ref[...] = dx
        d = dp[mid]

        @pl.when(i == 0)
        def _():
            dw_ref[...] = jnp.zeros_like(dw_ref)
            db_ref[...] = jnp.zeros_like(db_ref)
        dw_ref[0:1, :] += _sum0(d * pltpu.roll(xp, 1, 0)[mid])
        dw_ref[1:2, :] += _sum0(d * xp[mid])
        dw_ref[2:3, :] += _sum0(d * pltpu.roll(xp, n - 1, 0)[mid])
        dw_ref[3:4, :] += _sum0(d * pltpu.roll(xp, n - 2, 0)[mid])
        db_ref[...] += _sum0(d)

    return _pc(body, name=name, grid=(nt,),
               in_specs=[*_halo_specs(t, tm, 0), *_halo_specs(t, tm, 0), *_halo_specs(t, tm, XR_BLOCK),
                         _full((4, LRU_W))],
               out_specs=[pl.BlockSpec((tm, LRU_W), lambda i: (i, 0)), _full((4, LRU_W)), _full((1, LRU_W))],
               out_shape=[_sds((t, LRU_W)), _sds((4, LRU_W)), _sds((1, LRU_W))],
               compiler_params=_params("arbitrary"))(dxc_a, dxc_a, dxc_a, dxc_b, dxc_b, dxc_b, proj, proj, proj, cw)


def _local_scan(a, b, reverse):
    n = a.shape[0]
    row = lax.broadcasted_iota(jnp.int32, a.shape, 0) & (SUBLANES - 1)
    for s in (1, 2, 4):
        if reverse:
            a_s, b_s, ok = pltpu.roll(a, n - s, 0), pltpu.roll(b, n - s, 0), row < SUBLANES - s
        else:
            a_s, b_s, ok = pltpu.roll(a, s, 0), pltpu.roll(b, s, 0), row >= s
        b = a * jnp.where(ok, b_s, 0.0) + b
        a = a * jnp.where(ok, a_s, 1.0)
    return a, b


def _carry_scan(a_s, b_s, out_ref, carry, reverse):
    ng = a_s.shape[0] // SUBLANES
    shape = carry.shape

    def step(g, cr):
        gg = (ng - 1 - g) if reverse else g
        off = pl.multiple_of(gg * SUBLANES, SUBLANES)
        h = a_s[pl.ds(off, SUBLANES), :] * cr + b_s[pl.ds(off, SUBLANES), :]
        out_ref[pl.ds(off, SUBLANES), :] = h
        edge = h[0:1, :] if reverse else h[SUBLANES - 1:SUBLANES, :]
        return jnp.broadcast_to(edge, shape)

    return lax.fori_loop(0, ng, step, carry)


def _lru_gates(xc, wa_ref, wx_ref, ba, bx, lam):
    xb = xc.astype(BF16)
    r = _sigmoid(_dot(xb, wa_ref[...]) + ba)
    ig = _sigmoid(_dot(xb, wx_ref[...]) + bx)
    sp = _softplus(-lam)
    la = -LRU_C * r * sp
    a = jnp.exp(la)
    mult = jnp.sqrt(_neg_expm1(2.0 * la, a * a))
    return r, ig, sp, a, mult


def _lru_fwd(xc, wa, wx, ba, bx, lam, h0, reverse, name, comms=()):
    t = xc.shape[0]
    tm = _tile(t, True)
    nt = t // tm
    tidx = (lambda i: (nt - 1 - i, 0)) if reverse else (lambda i: (i, 0))

    def body(x_ref, wa_ref, wx_ref, ba_ref, bx_ref, lam_ref, h0_ref, h_ref, a_s, b_s, c_s):
        @pl.when(pl.program_id(0) == 0)
        def _():
            c_s[...] = jnp.broadcast_to(h0_ref[...], c_s.shape)
        xv = x_ref[...]
        _, ig, _, a, mult = _lru_gates(xv, wa_ref, wx_ref, ba_ref[...], bx_ref[...], lam_ref[...])
        al, bl = _local_scan(a, mult * (ig * xv), reverse)
        a_s[...] = al
        b_s[...] = bl
        c_s[...] = _carry_scan(a_s, b_s, h_ref, c_s[...], reverse)

    vec = _full((1, LRU_W))
    mat = _full((LRU_W, LRU_W))
    (h,), couts = _call(body, name=name, grid=(nt,),
                        in_specs=[pl.BlockSpec((tm, LRU_W), tidx), mat, mat, vec, vec, vec, vec],
                        out_specs=[pl.BlockSpec((tm, LRU_W), tidx)], out_shape=[_sds((t, LRU_W))],
                        scratch_shapes=[pltpu.VMEM((tm, LRU_W), F32), pltpu.VMEM((tm, LRU_W), F32),
                                        pltpu.VMEM((SUBLANES, LRU_W), F32)],
                        sem=("arbitrary",), args=(xc, wa, wx, ba, bx, lam, h0), comms=comms)
    return (h, couts) if comms else h


def _lru_bwd(xc, wa, wx, ba, bx, lam, h, h0, dh, reverse, name, comms=()):
    t = xc.shape[0]
    tm = _tile(t, True)
    nt = t // tm
    n8 = tm // SUBLANES
    last8 = t // SUBLANES - 1
    tidx = (lambda i: (i, 0)) if reverse else (lambda i: (nt - 1 - i, 0))
    if reverse:
        halo = pl.BlockSpec((SUBLANES, LRU_W), lambda i: (jnp.minimum((i + 1) * n8, last8), 0))
    else:
        halo = pl.BlockSpec((SUBLANES, LRU_W), lambda i: (jnp.maximum((nt - 1 - i) * n8 - 1, 0), 0))

    def body(x_ref, wa_ref, wx_ref, ba_ref, bx_ref, lam_ref, h_ref, halo_ref, h0_ref, dh_ref,
             dx_ref, dpre_ref, dba_ref, dbx_ref, dlam_ref, dh0_ref, a_s, b_s, l_s, c_s, e_s):
        i = pl.program_id(0)

        @pl.when(i == 0)
        def _():
            c_s[...] = jnp.zeros_like(c_s)
            e_s[...] = jnp.zeros_like(e_s)
            dba_ref[...] = jnp.zeros_like(dba_ref)
            dbx_ref[...] = jnp.zeros_like(dbx_ref)
            dlam_ref[...] = jnp.zeros_like(dlam_ref)
        xv = x_ref[...]
        lam = lam_ref[...]
        r, ig, sp, a, mult = _lru_gates(xv, wa_ref, wx_ref, ba_ref[...], bx_ref[...], lam)
        hv = h_ref[...]
        rowi = lax.broadcasted_iota(jnp.int32, (tm, LRU_W), 0)
        edge_a = jnp.broadcast_to(e_s[0:1, :], (tm, LRU_W))
        h0b = jnp.broadcast_to(h0_ref[...], (tm, LRU_W))
        if reverse:
            a_sh = jnp.where(rowi == 0, edge_a, pltpu.roll(a, 1, 0))
            hin_edge = jnp.where(i == nt - 1, h0b, jnp.broadcast_to(halo_ref[0:1, :], (tm, LRU_W)))
            h_in = jnp.where(rowi == tm - 1, hin_edge, pltpu.roll(hv, tm - 1, 0))
        else:
            a_sh = jnp.where(rowi == tm - 1, edge_a, pltpu.roll(a, tm - 1, 0))
            hin_edge = jnp.where(i == nt - 1, h0b, jnp.broadcast_to(halo_ref[SUBLANES - 1:SUBLANES, :], (tm, LRU_W)))
            h_in = jnp.where(rowi == 0, hin_edge, pltpu.roll(hv, 1, 0))
        al, bl = _local_scan(a_sh, dh_ref[...], not reverse)
        a_s[...] = al
        b_s[...] = bl
        c_s[...] = _carry_scan(a_s, b_s, l_s, c_s[...], not reverse)
        e_s[...] = jnp.broadcast_to(a[tm - 1:tm, :] if reverse else a[0:1, :], e_s.shape)
        lmb = l_s[...]
        da = lmb * h_in
        ixc = ig * xv
        dmult = lmb * ixc
        dixc = lmb * mult
        dla = da * a - dmult * (a * a) / mult
        dpr = dla * (-LRU_C * sp) * r * (1.0 - r)
        dpi = dixc * xv * ig * (1.0 - ig)
        dprb = dpr.astype(BF16)
        dpib = dpi.astype(BF16)
        dpre_ref[:, 0:LRU_W] = dprb
        dpre_ref[:, LRU_W:2 * LRU_W] = dpib
        dx_ref[...] = dixc * ig + _dot_nt(dprb, wa_ref[...]) + _dot_nt(dpib, wx_ref[...])
        dba_ref[...] += _sum0(dpr)
        dbx_ref[...] += _sum0(dpi)
        dlam_ref[...] += _sum0(dla * (-LRU_C * r)) * (-_sigmoid(-lam))

        @pl.when(i == nt - 1)
        def _():
            al0 = a * lmb
            dh0_ref[...] = al0[tm - 1:tm, :] if reverse else al0[0:1, :]

    vec = _full((1, LRU_W))
    mat = _full((LRU_W, LRU_W))
    tile = pl.BlockSpec((tm, LRU_W), tidx)
    return _call(body, name=name, grid=(nt,),
                 in_specs=[tile, mat, mat, vec, vec, vec, tile, halo, vec, tile],
                 out_specs=[tile, pl.BlockSpec((tm, 2 * LRU_W), tidx), vec, vec, vec, vec],
                 out_shape=[_sds((t, LRU_W)), _sds((t, 2 * LRU_W), BF16), _sds((1, LRU_W)), _sds((1, LRU_W)),
                            _sds((1, LRU_W)), _sds((1, LRU_W))],
                 scratch_shapes=[pltpu.VMEM((tm, LRU_W), F32), pltpu.VMEM((tm, LRU_W), F32),
                                 pltpu.VMEM((tm, LRU_W), F32), pltpu.VMEM((SUBLANES, LRU_W), F32),
                                 pltpu.VMEM((SUBLANES, LRU_W), F32)],
                 sem=("arbitrary",), args=(xc, wa, wx, ba, bx, lam, h, h, h0, dh), comms=comms)


def _decay_tables(lg, reverse):
    ci = lax.broadcasted_iota(jnp.int32, (CHUNK, CHUNK), 0).astype(F32)
    mi = lax.broadcasted_iota(jnp.int32, (CHUNK, CHUNK), 1).astype(F32)
    if reverse:
        rel, pq, ps = mi - ci, CHUNK - ci, ci
    else:
        rel, pq, ps = ci - mi, ci + 1.0, CHUNK - 1.0 - ci
    relc = jnp.maximum(rel, 0.0)
    dm = jnp.where(rel >= 0, jnp.exp(lg * relc), 0.0)
    return relc, dm, jnp.exp(lg * pq), jnp.exp(lg * ps), jnp.exp(lg * float(CHUNK)), pq, ps


def _ret_fwd(proj, lgv, s0f, s0b, comms=()):
    t = proj.shape[0]
    n = t // CHUNK

    def one(q, k, v, lg, s_s, hh, o_ref, sp_ref, reverse):
        _, dm, wq, ws, g, _, _ = _decay_tables(lg, reverse)
        vb = v.astype(BF16)
        p = _dot_nt(q.astype(BF16), k.astype(BF16)) * dm
        s = s_s[hh]
        sp_ref[hh, 0] = s
        o_ref[:, DH * hh:DH * (hh + 1)] = _dot(p.astype(BF16), vb) + _dot((q * wq).astype(BF16), s.astype(BF16))
        s_s[hh] = g * s + _dot_tn((k * ws).astype(BF16), vb)

    def body(qf, kf, vf, qb, kb, vb, lg_ref, s0f_ref, s0b_ref, of_ref, ob_ref, spf_ref, spb_ref, sf_s, sb_s):
        @pl.when(pl.program_id(0) == 0)
        def _():
            sf_s[...] = s0f_ref[...]
            sb_s[...] = s0b_ref[...]
        for hh in range(HEADS):
            sl = slice(DH * hh, DH * (hh + 1))
            one(qf[:, sl], kf[:, sl], vf[:, sl], lg_ref[hh, 0:1, :], sf_s, hh, of_ref, spf_ref, False)
            one(qb[:, sl], kb[:, sl], vb[:, sl], lg_ref[hh, 1:2, :], sb_s, hh, ob_ref, spb_ref, True)

    blk = (CHUNK, RET_W)
    fw = [pl.BlockSpec(blk, lambda i, o=o: (i, o)) for o in range(3)]
    bw = [pl.BlockSpec(blk, lambda i, o=o: (n - 1 - i, o)) for o in range(3)]
    st = _full((HEADS, DH, DH))
    return _call(body, name="ret_fwd", grid=(n,),
                 in_specs=fw + bw + [_full((HEADS, 2, LANES)), st, st],
                 out_specs=[pl.BlockSpec(blk, lambda i: (i, 0)), pl.BlockSpec(blk, lambda i: (n - 1 - i, 0)),
                            pl.BlockSpec((HEADS, 1, DH, DH), lambda i: (0, i, 0, 0)),
                            pl.BlockSpec((HEADS, 1, DH, DH), lambda i: (0, n - 1 - i, 0, 0))],
                 out_shape=[_sds((t, RET_W)), _sds((t, RET_W)), _sds((HEADS, n, DH, DH)), _sds((HEADS, n, DH, DH))],
                 scratch_shapes=[pltpu.VMEM((HEADS, DH, DH), F32), pltpu.VMEM((HEADS, DH, DH), F32)],
                 sem=("arbitrary",), args=(proj, proj, proj, proj, proj, proj, lgv, s0f, s0b), comms=comms)


def _ret_bwd(proj, lgv, sgv, sprev, do, reverse, name, comms=()):
    t = proj.shape[0]
    n = t // CHUNK
    d = 1 if reverse else 0
    cidx = (lambda i: i) if reverse else (lambda i: n - 1 - i)

    def body(q_ref, k_ref, v_ref, lg_ref, sg_ref, s_ref, do_ref, dq_ref, dk_ref, dv_ref, ds0_ref, drd_ref, ds_s, acc_s):
        i = pl.program_id(0)

        @pl.when(i == 0)
        def _():
            ds_s[...] = jnp.zeros_like(ds_s)
            acc_s[...] = jnp.zeros_like(acc_s)
        for hh in range(HEADS):
            sl = slice(DH * hh, DH * (hh + 1))
            relc, dm, wq, ws, g, pq, ps = _decay_tables(lg_ref[hh, d:d + 1, :], reverse)
            q, k = q_ref[:, sl], k_ref[:, sl]
            qb, kb, vb = q.astype(BF16), k.astype(BF16), v_ref[:, sl].astype(BF16)
            p = _dot_nt(qb, kb) * dm
            s = s_ref[hh, 0]
            dob = do_ref[:, sl].astype(BF16)
            dsn = ds_s[hh]
            dsb = dsn.astype(BF16)
            dv_ref[:, sl] = _dot_tn(p.astype(BF16), dob) + _dot((k * ws).astype(BF16), dsb)
            dp = _dot_nt(dob, vb)
            dab = (dp * dm).astype(BF16)
            xq = _dot_nt(dob, s.astype(BF16))
            yk = _dot_nt(vb, dsb)
            dq_ref[:, sl] = _dot(dab, kb) + xq * wq
            dk_ref[:, sl] = _dot_tn(dab, qb) + yk * ws
            ds_s[hh] = g * dsn + _dot_tn((q * wq).astype(BF16), dob)
            part = (_sum0(dp * p * relc) + _sum0(xq * q * wq * pq) + _sum0(yk * k * ws * ps)
                    + _sum0(dsn * s) * g * float(CHUNK))
            acc_s[hh] += jnp.broadcast_to(part, (SUBLANES, LANES))

        @pl.when(i == n - 1)
        def _():
            ds0_ref[...] = ds_s[...]
            for hh in range(HEADS):
                tot = jnp.sum(acc_s[hh, 0:1, :], axis=1, keepdims=True)
                drd_ref[hh] = jnp.broadcast_to(tot, (SUBLANES, LANES)) * sg_ref[hh, d:d + 1, :]

    blk = (CHUNK, RET_W)
    qkv = [pl.BlockSpec(blk, lambda i, o=o: (cidx(i), o)) for o in range(3)]
    hc = pl.BlockSpec(blk, lambda i: (cidx(i), 0))
    lane = _full((HEADS, 2, LANES))
    return _call(body, name=name, grid=(n,),
                 in_specs=qkv + [lane, lane, pl.BlockSpec((HEADS, 1, DH, DH), lambda i: (0, cidx(i), 0, 0)), hc],
                 out_specs=[hc, hc, hc, _full((HEADS, DH, DH)), _full((HEADS, SUBLANES, LANES))],
                 out_shape=[_sds((t, RET_W))] * 3 + [_sds((HEADS, DH, DH)), _sds((HEADS, SUBLANES, LANES))],
                 scratch_shapes=[pltpu.VMEM((HEADS, DH, DH), F32), pltpu.VMEM((HEADS, SUBLANES, LANES), F32)],
                 sem=("arbitrary",), args=(proj, proj, proj, lgv, sgv, sprev, do), comms=comms)


def _ctx_weights(lg, l_len, reverse):
    pos = lax.broadcasted_iota(jnp.int32, (l_len, DH), 0).astype(F32)
    steps = pos if reverse else (l_len - 1.0 - pos)
    return jnp.exp(lg * steps), steps


def _ctx_state_fwd(projc, lgv):
    l_len = projc.shape[0]

    def body(k_ref, v_ref, lg_ref, sf_ref, sb_ref):
        k = k_ref[...]
        vb = v_ref[...].astype(BF16)
        for d, o_ref in ((0, sf_ref), (1, sb_ref)):
            w, _ = _ctx_weights(lg_ref[0, d:d + 1, :], l_len, d == 1)
            o_ref[0] = _dot_tn((k * w).astype(BF16), vb)

    st = pl.BlockSpec((1, DH, DH), lambda h: (h, 0, 0))
    return _pc(body, name="ctx_state_fwd", grid=(HEADS,),
               in_specs=[pl.BlockSpec((l_len, DH), lambda h: (0, HEADS + h)),
                         pl.BlockSpec((l_len, DH), lambda h: (0, 2 * HEADS + h)),
                         pl.BlockSpec((1, 2, LANES), lambda h: (h, 0, 0))],
               out_specs=[st, st], out_shape=[_sds((HEADS, DH, DH))] * 2,
               compiler_params=_params("arbitrary"))(projc, projc, lgv)


def _ctx_state_bwd(projc, lgv, sgv, dsf, dsb):
    l_len = projc.shape[0]

    def body(k_ref, v_ref, lg_ref, sg_ref, dsf_ref, dsb_ref, dk_ref, dv_ref, drd_ref):
        k = k_ref[...]
        vb = v_ref[...].astype(BF16)
        dk = jnp.zeros((l_len, DH), F32)
        dv = jnp.zeros((l_len, DH), F32)
        rows = []
        for d, ds_ref in ((0, dsf_ref), (1, dsb_ref)):
            w, steps = _ctx_weights(lg_ref[0, d:d + 1, :], l_len, d == 1)
            dsb16 = ds_ref[0].astype(BF16)
            dkw = _dot_nt(vb, dsb16)
            dk = dk + dkw * w
            dv = dv + _dot((k * w).astype(BF16), dsb16)
            tot = jnp.sum(_sum0(dkw * k * w * steps), axis=1, keepdims=True)
            rows.append(jnp.broadcast_to(tot, (1, LANES)) * sg_ref[0, d:d + 1, :])
        dk_ref[...] = dk
        dv_ref[...] = dv
        rid = lax.broadcasted_iota(jnp.int32, (SUBLANES, LANES), 0)
        drd_ref[0] = jnp.where(rid == 0, rows[0], jnp.where(rid == 1, rows[1], 0.0))

    st = pl.BlockSpec((1, DH, DH), lambda h: (h, 0, 0))
    lane = pl.BlockSpec((1, 2, LANES), lambda h: (h, 0, 0))
    hc = pl.BlockSpec((l_len, DH), lambda h: (0, h))
    return _pc(body, name="ctx_state_bwd", grid=(HEADS,),
               in_specs=[pl.BlockSpec((l_len, DH), lambda h: (0, HEADS + h)),
                         pl.BlockSpec((l_len, DH), lambda h: (0, 2 * HEADS + h)), lane, lane, st, st],
               out_specs=[hc, hc, pl.BlockSpec((1, SUBLANES, LANES), lambda h: (h, 0, 0))],
               out_shape=[_sds((l_len, RET_W)), _sds((l_len, RET_W)), _sds((HEADS, SUBLANES, LANES))],
               compiler_params=_params("arbitrary"))(projc, projc, lgv, sgv, dsf, dsb)


G_BLOCK = (3 * RET_W) // RET_W
GATE_BLOCK = (4 * RET_W + LRU_W) // LRU_W


def _head_norm(y):
    yc = y - jnp.mean(y, axis=-1, keepdims=True)
    rs = lax.rsqrt(jnp.mean(yc * yc, axis=-1, keepdims=True) + EPS)
    return yc * rs, rs


def _gelu_parts(z):
    th = jnp.tanh(GELU_K * (z + GELU_C * z * z * z))
    return 0.5 * z * (1.0 + th), th


def _mix_fwd(o_f, o_b, proj, hf, hb, w_out, x, g1):
    t = x.shape[0]
    tm = _tile(t, True)

    def body(of_ref, ob_ref, g_ref, gt_ref, hf_ref, hb_ref, w_ref, x_ref, g1_ref, x1_ref, cat_ref):
        o = of_ref[...] + ob_ref[...]
        g = g_ref[...]
        for hh in range(HEADS):
            sl = slice(DH * hh, DH * (hh + 1))
            nrm, _ = _head_norm(o[:, sl])
            gh = g[:, sl]
            cat_ref[:, sl] = (gh * _sigmoid(gh) * nrm).astype(BF16)
        gel, _ = _gelu_parts(gt_ref[...])
        cat_ref[:, RET_W:] = ((hf_ref[...] + hb_ref[...]) * gel).astype(BF16)
        x1_ref[...] = x_ref[...] + g1_ref[...] * _dot(cat_ref[...], w_ref[...])

    half = pl.BlockSpec((tm, RET_W), lambda i: (i, 0))
    big = pl.BlockSpec((tm, D_MODEL), lambda i: (i, 0))
    return _pc(body, name="mix_fwd", grid=(t // tm,),
               in_specs=[half, half, pl.BlockSpec((tm, RET_W), lambda i: (i, G_BLOCK)),
                         pl.BlockSpec((tm, LRU_W), lambda i: (i, GATE_BLOCK)), half, half,
                         _full((D_MODEL, D_MODEL)), big, _full((1, D_MODEL))],
               out_specs=[big, big], out_shape=[_sds((t, D_MODEL)), _sds((t, D_MODEL), BF16)],
               compiler_params=_params("arbitrary"))(o_f, o_b, proj, proj, hf, hb, w_out, x, g1)


def _mix_bwd(o_f, o_b, proj, hf, hb, w_out, cat, dx1, g1, comms=()):
    t = dx1.shape[0]
    tm = _tile(t, True)

    def body(of_ref, ob_ref, g_ref, gt_ref, hf_ref, hb_ref, w_ref, cat_ref, dx1_ref, g1_ref,
             do_ref, dhs_ref, dg_ref, dgt_ref, dyb_ref, dg1_ref):
        dx1v = dx1_ref[...]
        y = _dot(cat_ref[...], w_ref[...])

        @pl.when(pl.program_id(0) == 0)
        def _():
            dg1_ref[...] = jnp.zeros_like(dg1_ref)
        dg1_ref[...] += _sum0(dx1v * y)
        dyb = (g1_ref[...] * dx1v).astype(BF16)
        dyb_ref[...] = dyb
        dcat = _dot_nt(dyb, w_ref[...])
        o = of_ref[...] + ob_ref[...]
        g = g_ref[...]
        for hh in range(HEADS):
            sl = slice(DH * hh, DH * (hh + 1))
            nrm, rs = _head_norm(o[:, sl])
            gh = g[:, sl]
            sg = _sigmoid(gh)
            dret = dcat[:, sl]
            dg_ref[:, sl] = dret * nrm * (sg * (1.0 + gh * (1.0 - sg)))
            dn = dret * (gh * sg)
            dyc = rs * (dn - nrm * jnp.mean(dn * nrm, axis=-1, keepdims=True))
            do_ref[:, sl] = dyc - jnp.mean(dyc, axis=-1, keepdims=True)
        z = gt_ref[...]
        gel, th = _gelu_parts(z)
        dlru = dcat[:, RET_W:]
        dhs_ref[...] = dlru * gel
        dgel = 0.5 * (1.0 + th) + 0.5 * z * (1.0 - th * th) * GELU_K * (1.0 + 3.0 * GELU_C * z * z)
        dgt_ref[...] = dlru * (hf_ref[...] + hb_ref[...]) * dgel

    half = pl.BlockSpec((tm, RET_W), lambda i: (i, 0))
    big = pl.BlockSpec((tm, D_MODEL), lambda i: (i, 0))
    return _call(body, name="mix_bwd", grid=(t // tm,),
                 in_specs=[half, half, pl.BlockSpec((tm, RET_W), lambda i: (i, G_BLOCK)),
                           pl.BlockSpec((tm, LRU_W), lambda i: (i, GATE_BLOCK)), half, half,
                           _full((D_MODEL, D_MODEL)), big, big, _full((1, D_MODEL))],
                 out_specs=[half, half, half, half, big, _full((1, D_MODEL))],
                 out_shape=[_sds((t, RET_W))] * 4 + [_sds((t, D_MODEL), BF16), _sds((1, D_MODEL))],
                 scratch_shapes=[], sem=("arbitrary",), args=(o_f, o_b, proj, proj, hf, hb, w_out, cat, dx1, g1),
                 comms=comms)


def _mlp(x1, n2g, sh2, sc2, g2, fg, w1, w2, tgt):
    t = x1.shape[0]
    tm = _tile(t)
    hb_ = MLP_H // N_CHIP

    def body(x1_ref, n2g_ref, sh2_ref, sc2_ref, g2_ref, fg_ref, w1_hbm, w2_hbm, tgt_ref,
             dx1_ref, h2b_ref, ab_ref, dub_ref, dmb_ref, dsc_ref, dsh_ref, dg2_ref, dn2_ref, dfg_ref, loss_ref,
             w1_s, w2_s, r_s, sems):
        @pl.when(pl.program_id(0) == 0)
        def _():
            c1 = pltpu.make_async_copy(w1_hbm, w1_s, sems.at[0])
            c2 = pltpu.make_async_copy(w2_hbm, w2_s, sems.at[1])
            c1.start()
            c2.start()
            for r in (dsc_ref, dsh_ref, dg2_ref, dn2_ref, dfg_ref, loss_ref):
                r[...] = jnp.zeros_like(r)
            c1.wait()
            c2.wait()
        x1v = x1_ref[...]
        n2g, sc2, g2, fg = n2g_ref[...], sc2_ref[...], g2_ref[...], fg_ref[...]
        xh, _ = _rms(x1v)
        h2b = (xh * n2g * (1.0 + sc2) + sh2_ref[...]).astype(BF16)
        h2b_ref[...] = h2b
        m = jnp.zeros((tm, D_MODEL), F32)
        for j in range(N_CHIP):
            sl = slice(hb_ * j, hb_ * (j + 1))
            r = jnp.maximum(_dot(h2b, w1_s[j]), 0.0)
            r_s[:, sl] = r
            ab = (r * r).astype(BF16)
            ab_ref[:, sl] = ab
            m = m + _dot(ab, w2_s[j])
        x2 = x1v + g2 * m
        x2h, r2 = _rms(x2)
        err = x2h * fg - tgt_ref[...]
        loss_ref[...] += _sum0(err * err)
        dout = err * (1.0 / D_MODEL)
        dfg_ref[...] += _sum0(dout * x2h)
        dxh = dout * fg
        dx2 = r2 * (dxh - x2h * jnp.mean(dxh * x2h, axis=-1, keepdims=True))
        dg2_ref[...] += _sum0(dx2 * m)
        dmb = (g2 * dx2).astype(BF16)
        dmb_ref[...] = dmb
        dh2 = jnp.zeros((tm, D_MODEL), F32)
        for j in range(N_CHIP):
            sl = slice(hb_ * j, hb_ * (j + 1))
            dub = (_dot_nt(dmb, w2_s[j]) * (2.0 * r_s[:, sl])).astype(BF16)
            dub_ref[:, sl] = dub
            dh2 = dh2 + _dot_nt(dub, w1_s[j])
        dx, dn2_t, dsh_t, dsc_t = _norm_mod_bwd(x1v, n2g, sc2, dh2)
        dx1_ref[...] = dx2 + dx
        dn2_ref[...] += dn2_t
        dsh_ref[...] += dsh_t
        dsc_ref[...] += dsc_t

        @pl.when(pl.program_id(0) == t // tm - 1)
        def _():
            tot = jnp.sum(loss_ref[...], axis=1, keepdims=True) * (0.5 / D_MODEL)
            loss_ref[...] = jnp.broadcast_to(tot, loss_ref.shape)

    row = _full((1, D_MODEL))
    big = pl.BlockSpec((tm, D_MODEL), lambda i: (i, 0))
    wide = pl.BlockSpec((tm, MLP_H), lambda i: (i, 0))
    return _pc(body, name="mlp", grid=(t // tm,),
               in_specs=[big, row, row, row, row, row, ANY, ANY, big],
               out_specs=[big, big, wide, wide, big, row, row, row, row, row, row],
               out_shape=[_sds((t, D_MODEL)), _sds((t, D_MODEL), BF16), _sds((t, MLP_H), BF16), _sds((t, MLP_H), BF16),
                          _sds((t, D_MODEL), BF16)] + [_sds((1, D_MODEL))] * 6,
               scratch_shapes=[pltpu.VMEM(w1.shape, BF16), pltpu.VMEM(w2.shape, BF16), pltpu.VMEM((tm, MLP_H), F32),
                               pltpu.SemaphoreType.DMA((2,))],
               compiler_params=_params("arbitrary"))(x1, n2g, sh2, sc2, g2, fg, w1, w2, tgt)


def _tn(a, b, nj, a_blocked, b_blocked, name, extra=None, comms=()):
    t = a.shape[0]
    m = a.shape[1] // (nj if a_blocked else 1)
    n = b.shape[1] // (nj if b_blocked else 1)
    bk = 1024 if t % 1024 == 0 else (512 if t % 512 == 0 else t)
    nk = t // bk
    a_map = (lambda j, k: (k, j)) if a_blocked else (lambda j, k: (k, 0))
    b_map = (lambda j, k: (k, j)) if b_blocked else (lambda j, k: (k, 0))
    in_specs = [pl.BlockSpec((bk, m), a_map), pl.BlockSpec((bk, n), b_map)]
    args = [a, b]
    if extra is not None:
        a2, b2 = extra
        t2 = a2.shape[0]
        in_specs += [pl.BlockSpec((t2, m), (lambda j, k: (0, j)) if a_blocked else (lambda j, k: (0, 0))),
                     pl.BlockSpec((t2, n), (lambda j, k: (0, j)) if b_blocked else (lambda j, k: (0, 0)))]
        args += [a2, b2]

    def body(*refs):
        a_ref, b_ref = refs[0], refs[1]
        o_ref, acc = refs[-2], refs[-1]
        k = pl.program_id(1)

        @pl.when(k == 0)
        def _():
            acc[...] = jnp.zeros_like(acc)
        acc[...] += _dot_tn(a_ref[...].astype(BF16), b_ref[...].astype(BF16))

        @pl.when(k == nk - 1)
        def _():
            if extra is not None:
                acc[...] += _dot_tn(refs[2][...].astype(BF16), refs[3][...].astype(BF16))
            o_ref[0] = acc[...]

    (out,), couts = _call(body, name=name, grid=(nj, nk), in_specs=in_specs,
                          out_specs=[pl.BlockSpec((1, m, n), lambda j, k: (j, 0, 0))], out_shape=[_sds((nj, m, n))],
                          scratch_shapes=[pltpu.VMEM((m, n), F32)], sem=("arbitrary", "arbitrary"), args=args,
                          comms=comms)
    return (out, couts) if comms else out


ROW_LOSS = 0
ROW_DMOD = 1
ROW_DMODC = 7
ROW_N1, ROW_N2, ROW_FG, ROW_CB = 9, 10, 11, 12
ROW_BA, ROW_BX, ROW_LAM = 13, 15, 17
ROW_CW = 20
ROW_RD = 24
SLAB_ROWS = 32
SEG = D_MODEL // 2


def _pack_small(rows, drd, cw2, cb2, lru2, gates):
    n_rows, n_lru = len(rows), len(lru2)

    def body(*refs):
        r = refs[:n_rows]
        drd_f, drd_b, drd_c, cw_a, cw_b, cb_a, cb_b = refs[n_rows:n_rows + 7]
        lru = refs[n_rows + 7:n_rows + 7 + n_lru]
        gf_ref, gb_ref, slab, ga, gx = refs[n_rows + 7 + n_lru:]
        slab[...] = jnp.zeros_like(slab)
        slab[ROW_LOSS:ROW_LOSS + 1, :] = r[0][...]
        for k in range(N_MOD):
            slab[ROW_DMOD + k:ROW_DMOD + k + 1, :] = r[1 + k][...]
        slab[ROW_DMODC:ROW_DMODC + 1, :] = r[7][...]
        slab[ROW_DMODC + 1:ROW_DMODC + 2, :] = r[8][...]
        slab[ROW_N1:ROW_N1 + 1, :] = r[9][...] + r[10][...]
        slab[ROW_N2:ROW_N2 + 1, :] = r[11][...]
        slab[ROW_FG:ROW_FG + 1, :] = r[12][...]
        slab[ROW_CB:ROW_CB + 1, 0:LRU_W] = cb_a[...] + cb_b[...]
        for k, row in enumerate((ROW_BA, ROW_BA + 1, ROW_BX, ROW_BX + 1, ROW_LAM, ROW_LAM + 1)):
            slab[row:row + 1, 0:LRU_W] = lru[2 * k][...] + lru[2 * k + 1][...]
        slab[ROW_CW:ROW_CW + 4, 0:LRU_W] = cw_a[...] + cw_b[...]
        for h in range(HEADS):
            slab[ROW_RD + h:ROW_RD + h + 1, 0:LANES] = drd_f[h, 0:1, :] + drd_c[h, 0:1, :]
            slab[ROW_RD + HEADS + h:ROW_RD + HEADS + h + 1, 0:LANES] = drd_b[h, 0:1, :] + drd_c[h, 1:2, :]
        for d, g_ref in enumerate((gf_ref, gb_ref)):
            for n in range(LRU_BLOCKS):
                blk = slice(LRU_BD * n, LRU_BD * (n + 1))
                ga[blk, LRU_BD * d:LRU_BD * (d + 1)] = g_ref[0, blk, blk]
                gx[blk, LRU_BD * d:LRU_BD * (d + 1)] = g_ref[1, blk, blk]

    args = list(rows) + list(drd) + list(cw2) + list(cb2) + list(lru2) + list(gates)
    return _pc(body, name="pack_small", in_specs=[_full(a.shape) for a in args],
               out_specs=[_full((SLAB_ROWS, D_MODEL)), _full((LRU_W, 2 * LRU_BD)), _full((LRU_W, 2 * LRU_BD))],
               out_shape=[_sds((SLAB_ROWS, D_MODEL)), _sds((LRU_W, 2 * LRU_BD)), _sds((LRU_W, 2 * LRU_BD))],
               compiler_params=_params())(*args)


def _adam_math(w, g, m, v):
    mn = ADAM_B1 * m + (1.0 - ADAM_B1) * g
    vn = ADAM_B2 * v + (1.0 - ADAM_B2) * (g * g)
    mh = mn / (1.0 - ADAM_B1 ** ADAM_STEP)
    vh = vn / (1.0 - ADAM_B2 ** ADAM_STEP)
    return -ADAM_LR * (mh / (jnp.sqrt(vh) + ADAM_EPS) + ADAM_WD * w), mn, vn


SMALL_PARAMS = ("b_ada", "norm1_g", "norm2_g", "final_g", "ret_decay", "conv_w", "conv_b", "lru_wa", "lru_ba", "lru_wx",
                "lru_bx", "lru_lambda")


def _finalize_small(chip_idx, slab_all, ga_all, gx_all, wmv):
    n_p = len(SMALL_PARAMS)
    flat = [a for nm in SMALL_PARAMS for a in wmv[nm]]
    ada_n = N_MOD * D_MODEL // N_CHIP

    def body(c_ref, slab_ref, ga_ref, gx_ref, *refs):
        prm = {nm: refs[3 * k:3 * k + 3] for k, nm in enumerate(SMALL_PARAMS)}
        outs = {nm: refs[3 * n_p + 4 * k:3 * n_p + 4 * k + 4] for k, nm in enumerate(SMALL_PARAMS)}
        b128_ref, dmc_ref, loss_ref = refs[3 * n_p + 4 * n_p:]
        chip = c_ref[0]

        def pick(fn):
            acc = fn(0)
            for j in range(1, N_CHIP):
                acc = jnp.where(chip == j, fn(j), acc)
            return acc

        tot = slab_ref[0]
        for d in range(1, N_DEV):
            tot = tot + slab_ref[d]

        def update(nm, g, sl=None, rows=None):
            w_ref, m_ref, v_ref = prm[nm]
            g_ref, d_ref, mo_ref, vo_ref = outs[nm]
            ix = (slice(None) if rows is None else rows, slice(None) if sl is None else sl)
            dl, mn, vn = _adam_math(w_ref[ix], g, m_ref[ix], v_ref[ix])
            g_ref[ix] = g
            d_ref[ix] = dl
            mo_ref[ix] = mn
            vo_ref[ix] = vn

        loss_ref[...] = jnp.broadcast_to(tot[ROW_LOSS:ROW_LOSS + 1, 0:LANES], (SUBLANES, LANES))
        for k in range(N_MOD):
            g = tot[ROW_DMOD + k:ROW_DMOD + k + 1, :]
            if k < 2:
                g = g + tot[ROW_DMODC + k:ROW_DMODC + k + 1, :]
            update("b_ada", g, slice(D_MODEL * k, D_MODEL * (k + 1)))
        update("norm1_g", tot[ROW_N1:ROW_N1 + 1, :])
        update("norm2_g", tot[ROW_N2:ROW_N2 + 1, :])
        update("final_g", tot[ROW_FG:ROW_FG + 1, :])
        update("ret_decay", tot[ROW_RD:ROW_RD + SUBLANES, 0:LANES])
        update("conv_b", tot[ROW_CB:ROW_CB + 1, 0:LRU_W])
        update("conv_w", pick(lambda j: tot[ROW_CW:ROW_CW + 4, LANES * j:LANES * (j + 1)]))
        for nm, row in (("lru_ba", ROW_BA), ("lru_bx", ROW_BX), ("lru_lambda", ROW_LAM)):
            update(nm, pick(lambda j, row=row: tot[row:row + 2, LANES * j:LANES * (j + 1)]))
        for nm, g_all in (("lru_wa", ga_ref), ("lru_wx", gx_ref)):
            for dr in range(2):
                lanes = slice(LRU_BD * dr, LRU_BD * (dr + 1))
                g = g_all[0, :, lanes]
                for d in range(1, N_DEV):
                    g = g + g_all[d, :, lanes]
                update(nm, g, rows=slice(LRU_W * dr, LRU_W * (dr + 1)))

        def seg(rows6, s):
            return rows6[s // 2][:, SEG * (s % 2):SEG * (s % 2 + 1)]

        b128_ref[...] = jnp.zeros_like(b128_ref)
        dmc_ref[...] = jnp.zeros_like(dmc_ref)
        zero = jnp.zeros((1, D_MODEL), F32)
        ctx6 = [tot[ROW_DMODC:ROW_DMODC + 1, :], tot[ROW_DMODC + 1:ROW_DMODC + 2, :]] + [zero] * (N_MOD - 2)
        for q in range(ada_n // SEG):
            cols = slice(SEG * q, SEG * (q + 1))
            for d in range(N_DEV):
                rows6 = [slab_ref[d, ROW_DMOD + k:ROW_DMOD + k + 1, :] for k in range(N_MOD)]
                b128_ref[d:d + 1, cols] = pick(lambda j, rows6=rows6: seg(rows6, 3 * j + q))
            c = pick(lambda j: seg(ctx6, 3 * j + q))
            b128_ref[N_DEV:N_DEV + 1, cols] = c
            dmc_ref[0:1, cols] = c

    out_shape = []
    for nm in SMALL_PARAMS:
        out_shape += [_sds(wmv[nm][0].shape)] * 4
    out_shape += [_sds((LANES, ada_n)), _sds((SUBLANES, ada_n)), _sds((SUBLANES, LANES))]
    args = [slab_all, ga_all, gx_all] + flat
    grid_spec = pltpu.PrefetchScalarGridSpec(
        num_scalar_prefetch=1, grid=(1,), in_specs=[_full(a.shape) for a in args],
        out_specs=[_full(s.shape) for s in out_shape])
    outs = _pc(body, name="finalize_small", grid_spec=grid_spec, out_shape=out_shape,
               compiler_params=_params("arbitrary"))(chip_idx, *args)
    res = {nm: tuple(outs[4 * k:4 * k + 4]) for k, nm in enumerate(SMALL_PARAMS)}
    return res, outs[4 * n_p], outs[4 * n_p + 1], outs[4 * n_p + 2]


def _block_diag(w):
    eye = jnp.eye(LRU_BLOCKS, dtype=F32)
    return (w[:, :, None, :] * eye[:, None, :, None]).reshape(LRU_W, LRU_W).astype(BF16)


def _lane_rep(v8):
    return jnp.broadcast_to(v8.reshape(SUBLANES, 1), (SUBLANES, LANES))


def kernel(x, c, ctx, c_ctx, w_ada, b_ada, norm1_g, norm2_g, w_in, ret_decay, conv_w, conv_b, lru_wa, lru_ba, lru_wx, lru_bx, lru_lambda, w_out, w_mlp1, w_mlp2, final_g, loss_target, m_c_ctx, m_w_ada, m_b_ada, m_norm1_g, m_norm2_g, m_w_in, m_ret_decay, m_conv_w, m_conv_b, m_lru_wa, m_lru_ba, m_lru_wx, m_lru_bx, m_lru_lambda, m_w_out, m_w_mlp1, m_w_mlp2, m_final_g, v_c_ctx, v_w_ada, v_b_ada, v_norm1_g, v_norm2_g, v_w_in, v_ret_decay, v_conv_w, v_conv_b, v_lru_wa, v_lru_ba, v_lru_wx, v_lru_bx, v_lru_lambda, v_w_out, v_w_mlp1, v_w_mlp2, v_final_g):
    ax, ay, ac = lax.axis_index("x"), lax.axis_index("y"), lax.axis_index("c")
    chip = 2 * ax + ay
    dev = 4 * ax + 2 * ay + ac
    c_idx = ac.reshape(1).astype(jnp.int32)
    j_idx = chip.reshape(1).astype(jnp.int32)

    xt = x[0]
    t_len = xt.shape[0]
    ctxt = ctx[0]
    l_len = ctxt.shape[0]
    tgt = loss_target[0]
    ada_n = w_ada.shape[2]

    def my_half(w2d):
        r = w2d.shape[0] // 2
        return lax.dynamic_slice_in_dim(w2d, ac * r, r, axis=0).astype(BF16)

    pad8 = lambda a: jnp.pad(a, ((0, SUBLANES - a.shape[0]), (0, 0)))
    small = jnp.concatenate([pad8(conv_w[0]), pad8(lru_ba[0]), pad8(lru_bx[0]), pad8(lru_lambda[0])], axis=0)
    gw_in, c_all, small_all = _all_gather([my_half(w_in[0]), pad8(c), small], "gather_head")
    w4 = gw_in.reshape(N_CHIP, D_MODEL, IN_COLS // N_CHIP)

    a16, lgv, sgv = _prep(c_all[:, 0, :], c_ctx, ret_decay[0])
    b_shard = lax.dynamic_slice_in_dim(b_ada, chip * ada_n, ada_n, axis=1)
    (mod_parts,) = _all_gather([_mod_fwd(a16, w_ada[0], b_shard)], "gather_mod")
    mod_all = mod_parts[0::2].transpose(1, 0, 2).reshape(16, N_CHIP * ada_n)
    mod_me = lax.dynamic_slice_in_dim(mod_all, dev, 1, axis=0)
    sh1, sc1, g1, sh2, sc2, g2 = [mod_me[:, D_MODEL * k:D_MODEL * (k + 1)] for k in range(N_MOD)]
    csh1, csc1 = mod_all[8:9, 0:D_MODEL], mod_all[8:9, D_MODEL:2 * D_MODEL]

    cos2, sin2 = _rotary_tables(t_len)
    cos_c, sin_c = jnp.ones((l_len, DH), F32), jnp.zeros((l_len, DH), F32)
    n1g, n2g = norm1_g, norm2_g
    fg = final_g.reshape(1, D_MODEL)

    small_full = small_all[0::2].transpose(1, 0, 2).reshape(4 * SUBLANES, LRU_W)
    cw = small_full[0:4]
    cb = conv_b
    ba_f, ba_b = small_full[8:9], small_full[9:10]
    bx_f, bx_b = small_full[16:17], small_full[17:18]
    lam_f, lam_b = small_full[24:25], small_full[25:26]
    wa_f, wa_b = _block_diag(lru_wa[0, 0]), _block_diag(lru_wa[0, 1])
    wx_f, wx_b = _block_diag(lru_wx[0, 0]), _block_diag(lru_wx[0, 1])
    zero_h = jnp.zeros((1, LRU_W), F32)

    projc, hcb16 = _inproj_fwd(ctxt, n1g, csh1, csc1, w4, cos_c, sin_c, "inproj_fwd_ctx")
    s_f, s_b = _ctx_state_fwd(projc, lgv)
    xcc = _conv_fwd(projc, cw, cb, "conv_fwd_ctx")
    hcf = _lru_fwd(xcc, wa_f, wx_f, ba_f, bx_f, lam_f, zero_h, False, "lru_fwd_ctx_f")
    hcbk = _lru_fwd(xcc, wa_b, wx_b, ba_b, bx_b, lam_b, zero_h, True, "lru_fwd_ctx_b")
    lru_sf, lru_sb = hcf[l_len - 1:l_len], hcbk[0:1]

    (proj, hb16), ((gw_1,),) = _inproj_fwd(xt, n1g, sh1, sc1, w4, cos2, sin2, "inproj_fwd",
                                          comms=(_AllGather([my_half(w_mlp1[0])]),))
    (o_f, o_b, spf, spb), ((gw_2,),) = _ret_fwd(proj, lgv, s_f, s_b, comms=(_AllGather([my_half(w_mlp2[0])]),))
    xcl = _conv_fwd(proj, cw, cb, "conv_fwd")
    hf, ((gw_out,),) = _lru_fwd(xcl, wa_f, wx_f, ba_f, bx_f, lam_f, lru_sf, False, "lru_fwd_f",
                               comms=(_AllGather([my_half(w_out[0])]),))
    hbk = _lru_fwd(xcl, wa_b, wx_b, ba_b, bx_b, lam_b, lru_sb, True, "lru_fwd_b")
    wo = gw_out.reshape(D_MODEL, D_MODEL)
    w1 = gw_1.reshape(N_CHIP, D_MODEL, MLP_H // N_CHIP)
    w2 = gw_2.reshape(N_CHIP, MLP_H // N_CHIP, D_MODEL)
    x1, cat = _mix_fwd(o_f, o_b, proj, hf, hbk, wo, xt, g1)

    (dx1, h2b, ab, dub, dmb, dsc2, dsh2, dg2, dn2g, dfg, lossv) = _mlp(x1, n2g, sh2, sc2, g2, fg, w1, w2, tgt)
    gw_mlp1 = _tn(h2b, dub, N_CHIP, False, True, "grad_w_mlp1")
    b_1 = gw_mlp1.reshape(N_DEV, D_MODEL // 2, MLP_H // N_CHIP)
    gw_mlp2, ((r_1,),) = _tn(ab, dmb, N_CHIP, True, False, "grad_w_mlp2", comms=(_pair_exchange([b_1]),))

    jc_idx = jnp.concatenate([j_idx, c_idx])
    b_2 = gw_mlp2.reshape(N_DEV, MLP_H // N_DEV, D_MODEL)
    (do, dhs, dg, dgate, dyb, dg1), ((r_2,),) = _mix_bwd(
        o_f, o_b, proj, hf, hbk, wo, cat, dx1, g1, comms=(_pair_exchange([b_2]),))
    gw_o = _tn(cat, dyb, 1, False, False, "grad_w_out")
    b_o = gw_o.reshape(N_DEV, D_MODEL // N_DEV, D_MODEL)
    p_1, pb_1 = _pair_add(b_1, r_1, c_idx, "rs_pair_add_w_mlp1")
    p_2, pb_2 = _pair_add(b_2, r_2, c_idx, "rs_pair_add_w_mlp2")

    (dq_f, dk_f, dv_f, ds_f, drd_f), ((q_1,), (r_o,)) = _ret_bwd(
        proj, lgv, sgv, spf, do, False, "ret_bwd_f", comms=(_chip_exchange([pb_1]), _pair_exchange([b_o])))
    p_o, pb_o = _pair_add(b_o, r_o, c_idx, "rs_pair_add_w_out")
    h_1 = _chip_add(p_1, q_1, jc_idx, "rs_chip_add_w_mlp1")

    (dq_b, dk_b, dv_b, ds_b, drd_b), ((q_2,), (f_1,)) = _ret_bwd(
        proj, lgv, sgv, spb, do, True, "ret_bwd_b", comms=(_chip_exchange([pb_2]), _pair_gather([h_1])))
    h_2 = _chip_add(p_2, q_2, jc_idx, "rs_chip_add_w_mlp2")

    (dxc_f, dpre_f, dba_f, dbx_f, dlam_f, dh0_f), ((q_o,), (f_2,)) = _lru_bwd(
        xcl, wa_f, wx_f, ba_f, bx_f, lam_f, hf, lru_sf, dhs, False, "lru_bwd_f",
        comms=(_chip_exchange([pb_o]), _pair_gather([h_2])))
    h_o = _chip_add(p_o, q_o, jc_idx, "rs_chip_add_w_out")
    (dxc_b, dpre_b, dba_b, dbx_b, dlam_b, dh0_b), ((f_o,),) = _lru_bwd(
        xcl, wa_b, wx_b, ba_b, bx_b, lam_b, hbk, lru_sb, dhs, True, "lru_bwd_b", comms=(_pair_gather([h_o]),))
    dxr, dcw, dcb = _conv_bwd(dxc_f, dxc_b, proj, cw, "conv_bwd")
    grad_x, dpb, dn1g, dsh1, dsc1 = _inproj_bwd(
        xt, n1g, sh1, sc1, w4, cos2, sin2, [dq_f, dq_b, dk_f, dk_b, dv_f, dv_b, dg, dxr, dgate], dx1, "inproj_bwd")

    dkc, dvc, drd_c = _ctx_state_bwd(projc, lgv, sgv, ds_f, ds_b)
    zc = jnp.zeros((l_len, LRU_W), F32)
    dhc_f = lax.dynamic_update_slice(zc, dh0_f, (l_len - 1, 0))
    dhc_b = lax.dynamic_update_slice(zc, dh0_b, (0, 0))
    (dxcc_f, dprec_f, dbac_f, dbxc_f, dlamc_f, _), _ = _lru_bwd(
        xcc, wa_f, wx_f, ba_f, bx_f, lam_f, hcf, zero_h, dhc_f, False, "lru_bwd_ctx_f")
    (dxcc_b, dprec_b, dbac_b, dbxc_b, dlamc_b, _), _ = _lru_bwd(
        xcc, wa_b, wx_b, ba_b, bx_b, lam_b, hcbk, zero_h, dhc_b, True, "lru_bwd_ctx_b")
    dxrc, dcw_c, dcb_c = _conv_bwd(dxcc_f, dxcc_b, projc, cw, "conv_bwd_ctx")
    zr = jnp.zeros((l_len, RET_W), F32)
    _, dpbc, dn1g_c, dcsh1, dcsc1 = _inproj_bwd(
        ctxt, n1g, csh1, csc1, w4, cos_c, sin_c, [zr, zr, dkc, zr, dvc, zr, zr, dxrc, zr],
        jnp.zeros((l_len, D_MODEL), F32), "inproj_bwd_ctx")

    gw_i = _tn(hb16, dpb, N_CHIP, False, True, "grad_w_in", extra=(hcb16, dpbc))
    b_i = gw_i.reshape(N_DEV, D_MODEL // 2, IN_COLS // N_CHIP)
    gwa_f, ((r_i,),) = _tn(xcl, dpre_f, 2, False, True, "grad_lru_gates_f", extra=(xcc, dprec_f),
                           comms=(_pair_exchange([b_i]),))
    p_i, pb_i = _pair_add(b_i, r_i, c_idx, "rs_pair_add_w_in")
    gwa_b, ((q_i,),) = _tn(xcl, dpre_b, 2, False, True, "grad_lru_gates_b", extra=(xcc, dprec_b),
                           comms=(_chip_exchange([pb_i]),))
    h_i = _chip_add(p_i, q_i, jc_idx, "rs_chip_add_w_in")
    g_out, g_1, g_2 = _shard_of(f_o), _shard_of(f_1), _shard_of(f_2)
    big = {}
    (d_, mn, vn), ((f_i,),) = _adamw(w_mlp1[0], g_1, m_w_mlp1[0], v_w_mlp1[0], "adamw_w_mlp1",
                                     comms=(_pair_gather([h_i]),))
    big["w_mlp1"] = (g_1[None], d_[None], mn[None], vn[None])
    g_in = _shard_of(f_i)
    for nm, w, g, m, v in (("w_in", w_in, g_in, m_w_in, v_w_in), ("w_out", w_out, g_out, m_w_out, v_w_out),
                           ("w_mlp2", w_mlp2, g_2, m_w_mlp2, v_w_mlp2)):
        d_, mn, vn = _adamw(w[0], g, m[0], v[0], "adamw_" + nm)
        big[nm] = (g[None], d_[None], mn[None], vn[None])

    slab, ga, gx = _pack_small(
        [lossv, dsh1, dsc1, dg1, dsh2, dsc2, dg2, dcsh1, dcsc1, dn1g, dn1g_c, dn2g, dfg],
        (drd_f, drd_b, drd_c), (dcw, dcw_c), (dcb, dcb_c),
        (dba_f, dbac_f, dba_b, dbac_b, dbx_f, dbxc_f, dbx_b, dbxc_b, dlam_f, dlamc_f, dlam_b, dlamc_b),
        (gwa_f, gwa_b))
    slab_all, ga_all, gx_all = _all_gather([slab, ga, gx], "gather_small_grads")
    params = {
        "b_ada": (b_ada, m_b_ada, v_b_ada), "norm1_g": (norm1_g, m_norm1_g, v_norm1_g),
        "norm2_g": (norm2_g, m_norm2_g, v_norm2_g), "final_g": (final_g, m_final_g, v_final_g),
        "ret_decay": (ret_decay, m_ret_decay, v_ret_decay), "conv_w": (conv_w, m_conv_w, v_conv_w),
        "conv_b": (conv_b, m_conv_b, v_conv_b), "lru_wa": (lru_wa, m_lru_wa, v_lru_wa),
        "lru_ba": (lru_ba, m_lru_ba, v_lru_ba), "lru_wx": (lru_wx, m_lru_wx, v_lru_wx),
        "lru_bx": (lru_bx, m_lru_bx, v_lru_bx), "lru_lambda": (lru_lambda, m_lru_lambda, v_lru_lambda),
    }
    as2d = {
        "b_ada": lambda a: a, "norm1_g": lambda a: a, "norm2_g": lambda a: a, "conv_b": lambda a: a,
        "final_g": lambda a: a.reshape(1, D_MODEL), "ret_decay": lambda a: _lane_rep(a.reshape(-1)),
        "conv_w": lambda a: a[0], "lru_ba": lambda a: a[0], "lru_bx": lambda a: a[0], "lru_lambda": lambda a: a[0],
        "lru_wa": lambda a: a.reshape(2 * LRU_W, LRU_BD), "lru_wx": lambda a: a.reshape(2 * LRU_W, LRU_BD),
    }
    res, b128, dmc8, loss8 = _finalize_small(
        j_idx, slab_all, ga_all, gx_all, {nm: tuple(as2d[nm](a) for a in params[nm]) for nm in SMALL_PARAMS})
    loss = loss8[0, 0]
    small_out = {}
    for nm in SMALL_PARAMS:
        shp = params[nm][0].shape
        if nm == "ret_decay":
            small_out[nm] = tuple(o[:, 0].reshape(shp) for o in res[nm])
        else:
            small_out[nm] = tuple(o.reshape(shp) for o in res[nm])

    g_ada = _ada_grad(jnp.pad(a16.T, ((0, 0), (0, LANES - 16))), b128)
    d_ada, m_ada, v_ada = _adamw(w_ada[0], g_ada, m_w_ada[0], v_w_ada[0], "adamw_w_ada")

    (cparts,) = _all_gather([_cctx_partial(dmc8, w_ada[0])], "gather_cctx")
    g_cc, d_cc, m_cc, v_cc = _cctx_final(cparts, c_ctx, m_c_ctx, v_c_ctx)
    small_out["c_ctx"] = tuple(a.reshape(D_MODEL) for a in (g_cc, d_cc, m_cc, v_cc))
    small_out["w_ada"] = (g_ada[None], d_ada[None], m_ada[None], v_ada[None])
    small_out.update(big)

    order = ["c_ctx", "w_ada", "b_ada", "norm1_g", "norm2_g", "w_in", "ret_decay", "conv_w", "conv_b", "lru_wa", "lru_ba",
             "lru_wx", "lru_bx", "lru_lambda", "w_out", "w_mlp1", "w_mlp2", "final_g"]
    outs = [loss, grad_x[None]]
    for k in range(4):
        outs += [small_out[nm][k] for nm in order]
    return tuple(outs)
```

```python
import math

import jax
import jax.numpy as jnp
from jax import lax
from jax.experimental import pallas as pl
from jax.experimental.pallas import tpu as pltpu

F32 = jnp.float32
BF16 = jnp.bfloat16

D_MODEL = 1024
HEADS = 4
DH = 128
CHUNK = 128
RET_W = HEADS * DH
LRU_W = 512
LRU_BLOCKS = 8
LRU_BD = LRU_W // LRU_BLOCKS
LRU_C = 8.0
IN_COLS = 4 * RET_W + 2 * LRU_W
MLP_H = 4 * D_MODEL
N_MOD = 6
GRID_W = 64
ROPE_BASE = 10000.0
K_SCALE = DH ** -0.5
EPS = 1e-6
GELU_K = math.sqrt(2.0 / math.pi)
GELU_C = 0.044715

ADAM_LR = 0.001
ADAM_B1 = 0.9
ADAM_B2 = 0.999
ADAM_EPS = 1e-08
ADAM_WD = 0.01
ADAM_STEP = 10

N_DEV = 8
N_CHIP = 4
SUBLANES = 8
LANES = 128
VMEM_LIMIT_V7X = 56 * 1024 * 1024
MESH = pl.DeviceIdType.MESH
ANY = pl.BlockSpec(memory_space=pl.ANY)


def _pc(body, **kw):
    return pl.pallas_call(body, **kw)


def _params(*sem):
    return pltpu.CompilerParams(dimension_semantics=sem if sem else None, vmem_limit_bytes=VMEM_LIMIT_V7X)


def _tile(t, big=False):
    if big and t >= 1024:
        return 512
    return 256 if t >= 256 else t


def _sds(shape, dtype=F32):
    return jax.ShapeDtypeStruct(tuple(shape), dtype)


def _full(shape):
    nd = len(shape)
    return pl.BlockSpec(tuple(shape), lambda *_: (0,) * nd)


def _sigmoid(x):
    return 1.0 / (1.0 + jnp.exp(-x))


def _log1p_pos(y):
    s = y * (1.0 - y * (0.5 - y * (1.0 / 3.0 - y * (0.25 - y * (0.2 - y / 6.0)))))
    return jnp.where(y < 0.03, s, jnp.log(1.0 + y))


def _softplus(z):
    return jnp.maximum(z, 0.0) + _log1p_pos(jnp.exp(-jnp.abs(z)))


def _neg_expm1(x, exp_x):
    t = x * (1.0 + x * (0.5 + x * (1.0 / 6.0 + x * (1.0 / 24.0 + x * (1.0 / 120.0 + x * (1.0 / 720.0 + x / 5040.0))))))
    return -jnp.where(x > -0.25, t, exp_x - 1.0)


def _rms(x):
    r = lax.rsqrt(jnp.mean(x * x, axis=-1, keepdims=True) + EPS)
    return x * r, r


def _dot(a, b):
    return jnp.dot(a, b, preferred_element_type=F32)


def _dot_nt(a, b):
    return lax.dot_general(a, b, (((1,), (1,)), ((), ())), preferred_element_type=F32)


def _dot_tn(a, b):
    return lax.dot_general(a, b, (((0,), (0,)), ((), ())), preferred_element_type=F32)


def _sum0(x):
    return jnp.sum(x, axis=0, keepdims=True)


def _norm_mod_bwd(x, g, sc, dh):
    xh, r = _rms(x)
    hn = xh * g
    dhn = dh * (1.0 + sc)
    dxh = dhn * g
    dx = r * (dxh - xh * jnp.mean(dxh * xh, axis=-1, keepdims=True))
    return dx, _sum0(dhn * xh), _sum0(dh), _sum0(dh * hn)


def _dev_index(p):
    return 4 * p[0] + 2 * p[1] + p[2]


def _mesh_pos():
    return lax.axis_index("x"), lax.axis_index("y"), lax.axis_index("c")


class _AllGather:
    def __init__(self, arrs):
        n = len(arrs)
        self.arrays = list(arrs)
        self.out_shapes = [_sds((N_DEV,) + a.shape, a.dtype) for a in arrs]
        self.scratch = ([pltpu.VMEM(a.shape, a.dtype) for a in arrs]
                        + [pltpu.SemaphoreType.DMA((7 * n,)), pltpu.SemaphoreType.DMA((7 * n,)),
                           pltpu.SemaphoreType.DMA((n,))])
        self.aliases = {}

    def _parts(self, ins, outs, scr):
        n = len(self.arrays)
        stage = scr[:n]
        send_sems, recv_sems, local_sems = scr[n:]
        x, y, c = _mesh_pos()
        me, sib = (x, y, c), (x, y, 1 - c)
        chips = [(1 - x, y), (x, 1 - y), (1 - x, 1 - y)]

        def copy(t, k, block, to, own=False):
            dst = outs[t].at[_dev_index(block)]
            return pltpu.make_async_remote_copy(
                src_ref=ins[t] if own else dst, dst_ref=dst,
                send_sem=send_sems.at[7 * t + k], recv_sem=recv_sems.at[7 * t + k],
                device_id=to, device_id_type=MESH)

        first = []
        for t in range(n):
            first.append(copy(t, 0, me, sib, own=True))
            for j, ch in enumerate(chips):
                first.append(copy(t, 1 + j, me, (*ch, c), own=True))
        stage_in = [pltpu.make_async_copy(ins[t], stage[t], local_sems.at[t]) for t in range(n)]
        mine = [pltpu.make_async_copy(stage[t], outs[t].at[_dev_index(me)], local_sems.at[t]) for t in range(n)]
        return n, c, me, sib, chips, copy, first, stage_in, mine

    def start(self, ins, outs, scr):
        n, _, _, _, _, _, first, stage_in, mine = self._parts(ins, outs, scr)
        for cp in stage_in:
            cp.start()
        for cp in first:
            cp.start()
        for t in range(n):
            stage_in[t].wait()
            mine[t].start()

    def finish(self, ins, outs, scr):
        n, c, me, sib, chips, copy, first, _, mine = self._parts(ins, outs, scr)
        passed = []
        for j, ch in enumerate(chips):
            for t in range(n):
                copy(t, 1 + j, (*ch, c), me).wait_recv()
                p = copy(t, 4 + j, (*ch, c), sib)
                p.start()
                passed.append(p)
        for t in range(n):
            copy(t, 0, sib, me).wait_recv()
            for j, ch in enumerate(chips):
                copy(t, 4 + j, (*ch, 1 - c), me).wait_recv()
        for cp in first + passed:
            cp.wait_send()
        for cp in mine:
            cp.wait()


class _Exchange:
    def __init__(self, arrays, out_shapes, plan, n_copies, aliases=None):
        self.arrays = list(arrays)
        self.out_shapes = list(out_shapes)
        self.plan = plan
        self.scratch = [pltpu.SemaphoreType.DMA((n_copies,)), pltpu.SemaphoreType.DMA((n_copies,))]
        self.aliases = aliases or {}

    def _copies(self, ins, outs, scr):
        send_sems, recv_sems = scr
        snd, rcv = [], []
        for i, (src, dst, peer, lands) in enumerate(self.plan(ins, outs, _mesh_pos())):
            kw = dict(send_sem=send_sems.at[i], recv_sem=recv_sems.at[i], device_id=peer, device_id_type=MESH)
            snd.append(pltpu.make_async_remote_copy(src_ref=src, dst_ref=dst, **kw))
            rcv.append(pltpu.make_async_remote_copy(src_ref=src, dst_ref=lands, **kw))
        return snd, rcv

    def start(self, ins, outs, scr):
        for cp in self._copies(ins, outs, scr)[0]:
            cp.start()

    def finish(self, ins, outs, scr):
        snd, rcv = self._copies(ins, outs, scr)
        for cp in rcv:
            cp.wait_recv()
        for cp in snd:
            cp.wait_send()


def _pair_exchange(grads):
    n = len(grads)

    def plan(ins, outs, pos):
        x, y, c = pos
        return [(ins[t].at[2 * j + (1 - c)], outs[t].at[j], (x, y, 1 - c), outs[t].at[j])
                for t in range(n) for j in range(N_CHIP)]

    return _Exchange(grads, [_sds((N_CHIP,) + g.shape[1:], g.dtype) for g in grads], plan, N_CHIP * n)


def _chip_exchange(parts):
    n = len(parts)

    def plan(ins, outs, pos):
        x, y, c = pos
        chips = [(1 - x, y), (x, 1 - y), (1 - x, 1 - y)]
        return [(ins[t].at[2 * ch[0] + ch[1]], outs[t].at[k], (*ch, c), outs[t].at[k])
                for t in range(n) for k, ch in enumerate(chips)]

    return _Exchange(parts, [_sds((3,) + p.shape[1:], p.dtype) for p in parts], plan, 3 * n)


def _pair_gather(bufs):
    n = len(bufs)

    def plan(ins, outs, pos):
        x, y, c = pos
        return [(ins[t].at[c], outs[t].at[c], (x, y, 1 - c), outs[t].at[1 - c]) for t in range(n)]

    return _Exchange(bufs, [_sds(b.shape, b.dtype) for b in bufs], plan, n, aliases={t: t for t in range(n)})


def _run_comm(comm, name):
    n_in, n_out = len(comm.arrays), len(comm.out_shapes)

    def body(*refs):
        ins, outs, scr = refs[:n_in], refs[n_in:n_in + n_out], refs[n_in + n_out:]
        comm.start(ins, outs, scr)
        comm.finish(ins, outs, scr)

    outs = _pc(body, name=name, out_shape=comm.out_shapes, in_specs=[ANY] * n_in, out_specs=[ANY] * n_out,
               input_output_aliases=dict(comm.aliases), scratch_shapes=comm.scratch,
               compiler_params=_params())(*comm.arrays)
    return list(outs)


def _all_gather(arrs, name):
    return _run_comm(_AllGather(arrs), name)


def _call(body, *, name, grid, in_specs, out_specs, out_shape, scratch_shapes, sem, args, comms=()):
    n_in, n_out, n_scr = len(in_specs), len(out_specs), len(scratch_shapes)
    c_in = [len(cm.arrays) for cm in comms]
    c_out = [len(cm.out_shapes) for cm in comms]
    c_scr = [len(cm.scratch) for cm in comms]
    aliases = {}
    for k, cm in enumerate(comms):
        for a, b in cm.aliases.items():
            aliases[n_in + sum(c_in[:k]) + a] = n_out + sum(c_out[:k]) + b

    def split(refs, counts):
        out, pos = [], 0
        for cnt in counts:
            out.append(refs[pos:pos + cnt])
            pos += cnt
        return out

    def wrapped(*refs):
        ins = refs[:n_in + sum(c_in)]
        outs = refs[len(ins):len(ins) + n_out + sum(c_out)]
        scr = refs[len(ins) + len(outs):]
        cins, couts, cscr = split(ins[n_in:], c_in), split(outs[n_out:], c_out), split(scr[n_scr:], c_scr)
        if comms:
            first = pl.program_id(0) == 0
            last = pl.program_id(0) == grid[0] - 1
            for k in range(1, len(grid)):
                first = jnp.logical_and(first, pl.program_id(k) == 0)
                last = jnp.logical_and(last, pl.program_id(k) == grid[k] - 1)

            @pl.when(first)
            def _():
                for k, cm in enumerate(comms):
                    cm.start(cins[k], couts[k], cscr[k])
        body(*ins[:n_in], *outs[:n_out], *scr[:n_scr])
        if comms:
            @pl.when(last)
            def _():
                for k, cm in enumerate(comms):
                    cm.finish(cins[k], couts[k], cscr[k])

    outs = _pc(wrapped, name=name, grid=grid,
               in_specs=list(in_specs) + [ANY] * sum(c_in), out_specs=list(out_specs) + [ANY] * sum(c_out),
               out_shape=list(out_shape) + [s for cm in comms for s in cm.out_shapes],
               scratch_shapes=list(scratch_shapes) + [s for cm in comms for s in cm.scratch],
               input_output_aliases=aliases, compiler_params=_params(*sem),
               )(*args, *[a for cm in comms for a in cm.arrays])
    outs = list(outs)
    return outs[:n_out], split(outs[n_out:], c_out)


def _row_block(r):
    for b in (512, 256, 128, 64, 32, 16, 8):
        if r % b == 0:
            return b
    return r


def _pair_add(g, recv, c_idx, name):
    _, r, cc = g.shape
    br = _row_block(r)

    def body(c_ref, g_ref, r_ref, p_ref, pb_ref):
        s = g_ref[...] + r_ref[...]
        p_ref[...] = s
        pb_ref[...] = s.astype(BF16)

    grid_spec = pltpu.PrefetchScalarGridSpec(
        num_scalar_prefetch=1, grid=(N_CHIP, r // br),
        in_specs=[pl.BlockSpec((1, br, cc), lambda j, i, c_ref: (2 * j + c_ref[0], i, 0)),
                  pl.BlockSpec((1, br, cc), lambda j, i, c_ref: (j, i, 0))],
        out_specs=[pl.BlockSpec((1, br, cc), lambda j, i, c_ref: (j, i, 0)),
                   pl.BlockSpec((1, br, cc), lambda j, i, c_ref: (j, i, 0))])
    return _pc(body, name=name, grid_spec=grid_spec,
               out_shape=[_sds((N_CHIP, r, cc)), _sds((N_CHIP, r, cc), BF16)],
               compiler_params=_params("arbitrary", "arbitrary"))(c_idx, g, recv)


def _chip_add(p, q, jc_idx, name):
    _, r, cc = p.shape
    br = _row_block(r)

    def body(jc_ref, p_ref, q_ref, o_ref):
        o_ref[0] = ((p_ref[0] + q_ref[0].astype(F32)) + q_ref[1].astype(F32)) + q_ref[2].astype(F32)

    grid_spec = pltpu.PrefetchScalarGridSpec(
        num_scalar_prefetch=1, grid=(r // br,),
        in_specs=[pl.BlockSpec((1, br, cc), lambda i, jc_ref: (jc_ref[0], i, 0)),
                  pl.BlockSpec((3, br, cc), lambda i, jc_ref: (0, i, 0))],
        out_specs=pl.BlockSpec((1, br, cc), lambda i, jc_ref: (jc_ref[1], i, 0)))
    return _pc(body, name=name, grid_spec=grid_spec, out_shape=_sds((2, r, cc)),
               compiler_params=_params("arbitrary"))(jc_idx, p, q)


def _shard_of(both):
    return both.reshape((2 * both.shape[1],) + both.shape[2:])


def _adamw(w, g, m, v, name, comms=()):
    r, cc = w.shape
    br = _row_block(r)
    if r * cc * 4 <= (1 << 20):
        br = r
    elif br * cc * 4 > (1 << 20) and br > 8:
        br = max(8, (1 << 20) // (cc * 4) // 8 * 8)
        while r % br:
            br -= 8
    c1 = 1.0 - ADAM_B1 ** ADAM_STEP
    c2 = 1.0 - ADAM_B2 ** ADAM_STEP

    def body(w_ref, g_ref, m_ref, v_ref, d_ref, mo_ref, vo_ref):
        gg = g_ref[...]
        mn = ADAM_B1 * m_ref[...] + (1.0 - ADAM_B1) * gg
        vn = ADAM_B2 * v_ref[...] + (1.0 - ADAM_B2) * (gg * gg)
        mh = mn / c1
        vh = vn / c2
        d_ref[...] = -ADAM_LR * (mh / (jnp.sqrt(vh) + ADAM_EPS) + ADAM_WD * w_ref[...])
        mo_ref[...] = mn
        vo_ref[...] = vn

    spec = pl.BlockSpec((br, cc), lambda i: (i, 0))
    outs, couts = _call(body, name=name, grid=(r // br,), in_specs=[spec] * 4, out_specs=[spec] * 3,
                        out_shape=[_sds((r, cc))] * 3, scratch_shapes=[], sem=("arbitrary",), args=(w, g, m, v),
                        comms=comms)
    return (outs, couts) if comms else outs


def _prep(c_all, c_ctx, ret_decay):
    def body(c_ref, cc_ref, rd_ref, a_ref, lg_ref, sg_ref):
        ca = c_ref[...]
        cc = cc_ref[...]
        a_ref[...] = jnp.zeros_like(a_ref)
        a_ref[0:8, :] = ca * _sigmoid(ca)
        a_ref[8:9, :] = cc * _sigmoid(cc)
        rd = rd_ref[...]
        lg_ref[...] = -_softplus(-rd)
        sg_ref[...] = _sigmoid(-rd)

    rd = jnp.broadcast_to(ret_decay.reshape(2, HEADS).T[:, :, None], (HEADS, 2, LANES))
    return _pc(body, name="prep",
               out_shape=[_sds((16, D_MODEL)), _sds((HEADS, 2, LANES)), _sds((HEADS, 2, LANES))],
               in_specs=[_full((8, D_MODEL)), _full((1, D_MODEL)), _full((HEADS, 2, LANES))],
               out_specs=[_full((16, D_MODEL)), _full((HEADS, 2, LANES)), _full((HEADS, 2, LANES))],
               compiler_params=_params())(c_all, c_ctx.reshape(1, D_MODEL), rd)


def _mod_fwd(a16, w_ada, b_shard):
    n = w_ada.shape[1]
    bn = 512

    def body(a_ref, w_ref, b_ref, o_ref):
        o_ref[...] = jnp.dot(a_ref[...], w_ref[...], preferred_element_type=F32,
                             precision=lax.Precision.HIGHEST) + b_ref[...]

    return _pc(body, name="mod_fwd", grid=(n // bn,),
               in_specs=[_full((16, D_MODEL)), pl.BlockSpec((D_MODEL, bn), lambda i: (0, i)),
                         pl.BlockSpec((1, bn), lambda i: (0, i))],
               out_specs=pl.BlockSpec((16, bn), lambda i: (0, i)), out_shape=_sds((16, n)),
               compiler_params=_params("arbitrary"))(a16, w_ada, b_shard)


def _ada_grad(at, b):
    n = b.shape[1]
    bn = 512

    def body(a_ref, b_ref, o_ref):
        o_ref[...] = jnp.dot(a_ref[...], b_ref[...], preferred_element_type=F32, precision=lax.Precision.HIGHEST)

    return _pc(body, name="ada_grad", grid=(n // bn,),
               in_specs=[_full((D_MODEL, LANES)), pl.BlockSpec((LANES, bn), lambda i: (0, i))],
               out_specs=pl.BlockSpec((D_MODEL, bn), lambda i: (0, i)), out_shape=_sds((D_MODEL, n)),
               compiler_params=_params("arbitrary"))(at, b)


def _cctx_partial(dmc8, w_ada):
    n = w_ada.shape[1]
    bn = 512

    def body(d_ref, w_ref, o_ref):
        @pl.when(pl.program_id(0) == 0)
        def _():
            o_ref[...] = jnp.zeros_like(o_ref)
        o_ref[...] += lax.dot_general(d_ref[...], w_ref[...], (((1,), (1,)), ((), ())),
                                      preferred_element_type=F32, precision=lax.Precision.HIGHEST)

    return _pc(body, name="cctx_partial", grid=(n // bn,),
               in_specs=[pl.BlockSpec((8, bn), lambda i: (0, i)), pl.BlockSpec((D_MODEL, bn), lambda i: (0, i))],
               out_specs=_full((8, D_MODEL)), out_shape=_sds((8, D_MODEL)),
               compiler_params=_params("arbitrary"))(dmc8, w_ada)


def _cctx_final(parts, c_ctx, m, v):
    c1 = 1.0 - ADAM_B1 ** ADAM_STEP
    c2 = 1.0 - ADAM_B2 ** ADAM_STEP

    def body(p_ref, c_ref, m_ref, v_ref, g_ref, d_ref, mo_ref, vo_ref):
        s = ((p_ref[0, 0:1, :] + p_ref[2, 0:1, :]) + p_ref[4, 0:1, :]) + p_ref[6, 0:1, :]
        z = c_ref[...]
        sg = _sigmoid(z)
        gg = s * (sg * (1.0 + z * (1.0 - sg)))
        g_ref[...] = gg
        mn = ADAM_B1 * m_ref[...] + (1.0 - ADAM_B1) * gg
        vn = ADAM_B2 * v_ref[...] + (1.0 - ADAM_B2) * (gg * gg)
        d_ref[...] = -ADAM_LR * ((mn / c1) / (jnp.sqrt(vn / c2) + ADAM_EPS) + ADAM_WD * z)
        mo_ref[...] = mn
        vo_ref[...] = vn

    row = _full((1, D_MODEL))
    return _pc(body, name="cctx_final", out_shape=[_sds((1, D_MODEL))] * 4,
               in_specs=[_full(parts.shape), row, row, row], out_specs=[row] * 4,
               compiler_params=_params())(parts, c_ctx.reshape(1, D_MODEL), m.reshape(1, D_MODEL), v.reshape(1, D_MODEL))


def _rotary_tables(t_len):
    rows = t_len // GRID_W
    row = jnp.repeat(jnp.arange(rows, dtype=F32), GRID_W)
    col = jnp.tile(jnp.arange(GRID_W, dtype=F32), rows)
    n_freq = DH // 4
    inv = ROPE_BASE ** (-jnp.arange(n_freq, dtype=F32) / n_freq)
    ang = jnp.concatenate([row[:, None] * inv, col[:, None] * inv], axis=-1)
    cos, sin = jnp.cos(ang), jnp.sin(ang)
    return jnp.concatenate([cos, cos], axis=-1), jnp.concatenate([-sin, sin], axis=-1)


def _inproj_fwd(x, gn, sh, sc, w4, cos2, sin2, name, comms=()):
    t = x.shape[0]
    tm = _tile(t, True)
    nc = IN_COLS // N_CHIP

    def body(x_ref, gn_ref, sh_ref, sc_ref, w_ref, c_ref, s_ref, p_ref, hb_ref):
        xh, _ = _rms(x_ref[...])
        h = xh * gn_ref[...] * (1.0 + sc_ref[...]) + sh_ref[...]
        hb = h.astype(BF16)
        hb_ref[...] = hb
        for j in range(N_CHIP):
            p_ref[:, nc * j:nc * (j + 1)] = _dot(hb, w_ref[j])
        cc = c_ref[...]
        ss = s_ref[...]
        for hh in range(2 * HEADS):
            blk = p_ref[:, DH * hh:DH * (hh + 1)]
            rot = blk * cc + pltpu.roll(blk, DH // 2, 1) * ss
            if hh >= HEADS:
                rot = rot * K_SCALE
            p_ref[:, DH * hh:DH * (hh + 1)] = rot

    row = _full((1, D_MODEL))
    outs, couts = _call(
        body, name=name, grid=(t // tm,),
        in_specs=[pl.BlockSpec((tm, D_MODEL), lambda i: (i, 0)), row, row, row, _full(w4.shape),
                  pl.BlockSpec((tm, DH), lambda i: (i, 0)), pl.BlockSpec((tm, DH), lambda i: (i, 0))],
        out_specs=[pl.BlockSpec((tm, IN_COLS), lambda i: (i, 0)), pl.BlockSpec((tm, D_MODEL), lambda i: (i, 0))],
        out_shape=[_sds((t, IN_COLS)), _sds((t, D_MODEL), BF16)], scratch_shapes=[], sem=("arbitrary",),
        args=(x, gn, sh, sc, w4, cos2, sin2), comms=comms)
    return (outs, couts) if comms else outs


def _inproj_bwd(x, gn, sh, sc, w4, cos2, sin2, pieces, dres, name):
    t = x.shape[0]
    tm = _tile(t)
    nc = IN_COLS // N_CHIP

    def body(x_ref, gn_ref, sh_ref, sc_ref, w_ref, c_ref, s_ref, dqf, dqb, dkf, dkb, dvf, dvb, dg, dxr, dgt, dres_ref,
             dx_ref, dpb_ref, dgn_ref, dsh_ref, dsc_ref):
        cc = c_ref[...]
        ss = s_ref[...]
        dq = dqf[...] + dqb[...]
        dk = dkf[...] + dkb[...]
        for hh in range(HEADS):
            sl = slice(DH * hh, DH * (hh + 1))
            b = dq[:, sl]
            dpb_ref[:, sl] = (b * cc + pltpu.roll(b * ss, DH // 2, 1)).astype(BF16)
            b = dk[:, sl]
            dpb_ref[:, RET_W + DH * hh:RET_W + DH * (hh + 1)] = (
                (b * cc + pltpu.roll(b * ss, DH // 2, 1)) * K_SCALE).astype(BF16)
        dpb_ref[:, 2 * RET_W:3 * RET_W] = (dvf[...] + dvb[...]).astype(BF16)
        dpb_ref[:, 3 * RET_W:4 * RET_W] = dg[...].astype(BF16)
        dpb_ref[:, 4 * RET_W:4 * RET_W + LRU_W] = dxr[...].astype(BF16)
        dpb_ref[:, 4 * RET_W + LRU_W:IN_COLS] = dgt[...].astype(BF16)
        dh = _dot_nt(dpb_ref[:, 0:nc], w_ref[0])
        for j in range(1, N_CHIP):
            dh = dh + _dot_nt(dpb_ref[:, nc * j:nc * (j + 1)], w_ref[j])
        dx, dgn_t, dsh_t, dsc_t = _norm_mod_bwd(x_ref[...], gn_ref[...], sc_ref[...], dh)
        dx_ref[...] = dres_ref[...] + dx

        @pl.when(pl.program_id(0) == 0)
        def _():
            dgn_ref[...] = jnp.zeros_like(dgn_ref)
            dsh_ref[...] = jnp.zeros_like(dsh_ref)
            dsc_ref[...] = jnp.zeros_like(dsc_ref)
        dgn_ref[...] += dgn_t
        dsh_ref[...] += dsh_t
        dsc_ref[...] += dsc_t

    row = _full((1, D_MODEL))
    pc = pl.BlockSpec((tm, RET_W), lambda i: (i, 0))
    big = pl.BlockSpec((tm, D_MODEL), lambda i: (i, 0))
    return _pc(body, name=name, grid=(t // tm,),
               in_specs=[big, row, row, row, _full(w4.shape),
                         pl.BlockSpec((tm, DH), lambda i: (i, 0)), pl.BlockSpec((tm, DH), lambda i: (i, 0))]
               + [pc] * 9 + [big],
               out_specs=[big, pl.BlockSpec((tm, IN_COLS), lambda i: (i, 0)), row, row, row],
               out_shape=[_sds((t, D_MODEL)), _sds((t, IN_COLS), BF16), _sds((1, D_MODEL)), _sds((1, D_MODEL)),
                          _sds((1, D_MODEL))],
               compiler_params=_params("arbitrary"))(x, gn, sh, sc, w4, cos2, sin2, *pieces, dres)


XR_BLOCK = (4 * RET_W) // LRU_W


def _halo_specs(t, tm, col):
    n8 = tm // SUBLANES
    last8 = t // SUBLANES - 1
    prev = pl.BlockSpec((SUBLANES, LRU_W), lambda i: (jnp.maximum(i * n8 - 1, 0), col))
    main = pl.BlockSpec((tm, LRU_W), lambda i: (i, col))
    nxt = pl.BlockSpec((SUBLANES, LRU_W), lambda i: (jnp.minimum((i + 1) * n8, last8), col))
    return prev, main, nxt


def _with_halo(prev_ref, main_ref, next_ref, i, nt):
    prev = jnp.where(i > 0, prev_ref[...], 0.0)
    nxt = jnp.where(i < nt - 1, next_ref[...], 0.0)
    return jnp.concatenate([prev, main_ref[...], nxt], axis=0)


def _conv_fwd(proj, cw, cb, name):
    t = proj.shape[0]
    tm = _tile(t, True)
    nt = t // tm
    n = tm + 2 * SUBLANES
    mid = slice(SUBLANES, SUBLANES + tm)

    def body(p_ref, m_ref, n_ref, w_ref, b_ref, o_ref):
        xp = _with_halo(p_ref, m_ref, n_ref, pl.program_id(0), nt)
        acc = b_ref[...] + pltpu.roll(xp, 1, 0)[mid] * w_ref[0:1, :]
        acc = acc + xp[mid] * w_ref[1:2, :]
        acc = acc + pltpu.roll(xp, n - 1, 0)[mid] * w_ref[2:3, :]
        acc = acc + pltpu.roll(xp, n - 2, 0)[mid] * w_ref[3:4, :]
        o_ref[...] = acc

    return _pc(body, name=name, grid=(nt,),
               in_specs=[*_halo_specs(t, tm, XR_BLOCK), _full((4, LRU_W)), _full((1, LRU_W))],
               out_specs=pl.BlockSpec((tm, LRU_W), lambda i: (i, 0)), out_shape=_sds((t, LRU_W)),
               compiler_params=_params("arbitrary"))(proj, proj, proj, cw, cb)


def _conv_bwd(dxc_a, dxc_b, proj, cw, name):
    t = proj.shape[0]
    tm = _tile(t, True)
    nt = t // tm
    n = tm + 2 * SUBLANES
    mid = slice(SUBLANES, SUBLANES + tm)

    def body(ap_ref, am_ref, an_ref, bp_ref, bm_ref, bn_ref, xp_ref, xm_ref, xn_ref, w_ref, dx_ref, dw_ref, db_ref):
        i = pl.program_id(0)
        dp = _with_halo(ap_ref, am_ref, an_ref, i, nt) + _with_halo(bp_ref, bm_ref, bn_ref, i, nt)
        xp = _with_halo(xp_ref, xm_ref, xn_ref, i, nt)
        dx = pltpu.roll(dp, n - 1, 0)[mid] * w_ref[0:1, :]
        dx = dx + dp[mid] * w_ref[1:2, :]
        dx = dx + pltpu.roll(dp, 1, 0)[mid] * w_ref[2:3, :]
        dx = dx + pltpu.roll(dp, 2, 0)[mid] * w_ref[3:4, :]
        dx_ref[...] = dx
        d = dp[mid]

        @pl.when(i == 0)
        def _():
            dw_ref[...] = jnp.zeros_like(dw_ref)
            db_ref[...] = jnp.zeros_like(db_ref)
        dw_ref[0:1, :] += _sum0(d * pltpu.roll(xp, 1, 0)[mid])
        dw_ref[1:2, :] += _sum0(d * xp[mid])
        dw_ref[2:3, :] += _sum0(d * pltpu.roll(xp, n - 1, 0)[mid])
        dw_ref[3:4, :] += _sum0(d * pltpu.roll(xp, n - 2, 0)[mid])
        db_ref[...] += _sum0(d)

    return _pc(body, name=name, grid=(nt,),
               in_specs=[*_halo_specs(t, tm, 0), *_halo_specs(t, tm, 0), *_halo_specs(t, tm, XR_BLOCK),
                         _full((4, LRU_W))],
               out_specs=[pl.BlockSpec((tm, LRU_W), lambda i: (i, 0)), _full((4, LRU_W)), _full((1, LRU_W))],
               out_shape=[_sds((t, LRU_W)), _sds((4, LRU_W)), _sds((1, LRU_W))],
               compiler_params=_params("arbitrary"))(dxc_a, dxc_a, dxc_a, dxc_b, dxc_b, dxc_b, proj, proj, proj, cw)


def _local_scan(a, b, reverse):
    n = a.shape[0]
    row = lax.broadcasted_iota(jnp.int32, a.shape, 0) & (SUBLANES - 1)
    for s in (1, 2, 4):
        if reverse:
            a_s, b_s, ok = pltpu.roll(a, n - s, 0), pltpu.roll(b, n - s, 0), row < SUBLANES - s
        else:
            a_s, b_s, ok = pltpu.roll(a, s, 0), pltpu.roll(b, s, 0), row >= s
        b = a * jnp.where(ok, b_s, 0.0) + b
        a = a * jnp.where(ok, a_s, 1.0)
    return a, b


def _carry_scan(a_s, b_s, out_ref, carry, reverse):
    ng = a_s.shape[0] // SUBLANES
    shape = carry.shape

    def step(g, cr):
        gg = (ng - 1 - g) if reverse else g
        off = pl.multiple_of(gg * SUBLANES, SUBLANES)
        h = a_s[pl.ds(off, SUBLANES), :] * cr + b_s[pl.ds(off, SUBLANES), :]
        out_ref[pl.ds(off, SUBLANES), :] = h
        edge = h[0:1, :] if reverse else h[SUBLANES - 1:SUBLANES, :]
        return jnp.broadcast_to(edge, shape)

    return lax.fori_loop(0, ng, step, carry)


def _lru_gates(xc, wa_ref, wx_ref, ba, bx, lam):
    xb = xc.astype(BF16)
    r = _sigmoid(_dot(xb, wa_ref[...]) + ba)
    ig = _sigmoid(_dot(xb, wx_ref[...]) + bx)
    sp = _softplus(-lam)
    la = -LRU_C * r * sp
    a = jnp.exp(la)
    mult = jnp.sqrt(_neg_expm1(2.0 * la, a * a))
    return r, ig, sp, a, mult


def _lru_fwd(xc, wa, wx, ba, bx, lam, h0, reverse, name, comms=()):
    t = xc.shape[0]
    tm = _tile(t, True)
    nt = t // tm
    tidx = (lambda i: (nt - 1 - i, 0)) if reverse else (lambda i: (i, 0))

    def body(x_ref, wa_ref, wx_ref, ba_ref, bx_ref, lam_ref, h0_ref, h_ref, a_s, b_s, c_s):
        @pl.when(pl.program_id(0) == 0)
        def _():
            c_s[...] = jnp.broadcast_to(h0_ref[...], c_s.shape)
        xv = x_ref[...]
        _, ig, _, a, mult = _lru_gates(xv, wa_ref, wx_ref, ba_ref[...], bx_ref[...], lam_ref[...])
        al, bl = _local_scan(a, mult * (ig * xv), reverse)
        a_s[...] = al
        b_s[...] = bl
        c_s[...] = _carry_scan(a_s, b_s, h_ref, c_s[...], reverse)

    vec = _full((1, LRU_W))
    mat = _full((LRU_W, LRU_W))
    (h,), couts = _call(body, name=name, grid=(nt,),
                        in_specs=[pl.BlockSpec((tm, LRU_W), tidx), mat, mat, vec, vec, vec, vec],
                        out_specs=[pl.BlockSpec((tm, LRU_W), tidx)], out_shape=[_sds((t, LRU_W))],
                        scratch_shapes=[pltpu.VMEM((tm, LRU_W), F32), pltpu.VMEM((tm, LRU_W), F32),
                                        pltpu.VMEM((SUBLANES, LRU_W), F32)],
                        sem=("arbitrary",), args=(xc, wa, wx, ba, bx, lam, h0), comms=comms)
    return (h, couts) if comms else h


def _lru_bwd(xc, wa, wx, ba, bx, lam, h, h0, dh, reverse, name, comms=()):
    t = xc.shape[0]
    tm = _tile(t, True)
    nt = t // tm
    n8 = tm // SUBLANES
    last8 = t // SUBLANES - 1
    tidx = (lambda i: (i, 0)) if reverse else (lambda i: (nt - 1 - i, 0))
    if reverse:
        halo = pl.BlockSpec((SUBLANES, LRU_W), lambda i: (jnp.minimum((i + 1) * n8, last8), 0))
    else:
        halo = pl.BlockSpec((SUBLANES, LRU_W), lambda i: (jnp.maximum((nt - 1 - i) * n8 - 1, 0), 0))

    def body(x_ref, wa_ref, wx_ref, ba_ref, bx_ref, lam_ref, h_ref, halo_ref, h0_ref, dh_ref,
             dx_ref, dpre_ref, dba_ref, dbx_ref, dlam_ref, dh0_ref, a_s, b_s, l_s, c_s, e_s):
        i = pl.program_id(0)

        @pl.when(i == 0)
        def _():
            c_s[...] = jnp.zeros_like(c_s)
            e_s[...] = jnp.zeros_like(e_s)
            dba_ref[...] = jnp.zeros_like(dba_ref)
            dbx_ref[...] = jnp.zeros_like(dbx_ref)
            dlam_ref[...] = jnp.zeros_like(dlam_ref)
        xv = x_ref[...]
        lam = lam_ref[...]
        r, ig, sp, a, mult = _lru_gates(xv, wa_ref, wx_ref, ba_ref[...], bx_ref[...], lam)
        hv = h_ref[...]
        rowi = lax.broadcasted_iota(jnp.int32, (tm, LRU_W), 0)
        edge_a = jnp.broadcast_to(e_s[0:1, :], (tm, LRU_W))
        h0b = jnp.broadcast_to(h0_ref[...], (tm, LRU_W))
        if reverse:
            a_sh = jnp.where(rowi == 0, edge_a, pltpu.roll(a, 1, 0))
            hin_edge = jnp.where(i == nt - 1, h0b, jnp.broadcast_to(halo_ref[0:1, :], (tm, LRU_W)))
            h_in = jnp.where(rowi == tm - 1, hin_edge, pltpu.roll(hv, tm - 1, 0))
        else:
            a_sh = jnp.where(rowi == tm - 1, edge_a, pltpu.roll(a, tm - 1, 0))
            hin_edge = jnp.where(i == nt - 1, h0b, jnp.broadcast_to(halo_ref[SUBLANES - 1:SUBLANES, :], (tm, LRU_W)))
            h_in = jnp.where(rowi == 0, hin_edge, pltpu.roll(hv, 1, 0))
        al, bl = _local_scan(a_sh, dh_ref[...], not reverse)
        a_s[...] = al
        b_s[...] = bl
        c_s[...] = _carry_scan(a_s, b_s, l_s, c_s[...], not reverse)
        e_s[...] = jnp.broadcast_to(a[tm - 1:tm, :] if reverse else a[0:1, :], e_s.shape)
        lmb = l_s[...]
        da = lmb * h_in
        ixc = ig * xv
        dmult = lmb * ixc
        dixc = lmb * mult
        dla = da * a - dmult * (a * a) / mult
        dpr = dla * (-LRU_C * sp) * r * (1.0 - r)
        dpi = dixc * xv * ig * (1.0 - ig)
        dprb = dpr.astype(BF16)
        dpib = dpi.astype(BF16)
        dpre_ref[:, 0:LRU_W] = dprb
        dpre_ref[:, LRU_W:2 * LRU_W] = dpib
        dx_ref[...] = dixc * ig + _dot_nt(dprb, wa_ref[...]) + _dot_nt(dpib, wx_ref[...])
        dba_ref[...] += _sum0(dpr)
        dbx_ref[...] += _sum0(dpi)
        dlam_ref[...] += _sum0(dla * (-LRU_C * r)) * (-_sigmoid(-lam))

        @pl.when(i == nt - 1)
        def _():
            al0 = a * lmb
            dh0_ref[...] = al0[tm - 1:tm, :] if reverse else al0[0:1, :]

    vec = _full((1, LRU_W))
    mat = _full((LRU_W, LRU_W))
    tile = pl.BlockSpec((tm, LRU_W), tidx)
    return _call(body, name=name, grid=(nt,),
                 in_specs=[tile, mat, mat, vec, vec, vec, tile, halo, vec, tile],
                 out_specs=[tile, pl.BlockSpec((tm, 2 * LRU_W), tidx), vec, vec, vec, vec],
                 out_shape=[_sds((t, LRU_W)), _sds((t, 2 * LRU_W), BF16), _sds((1, LRU_W)), _sds((1, LRU_W)),
                            _sds((1, LRU_W)), _sds((1, LRU_W))],
                 scratch_shapes=[pltpu.VMEM((tm, LRU_W), F32), pltpu.VMEM((tm, LRU_W), F32),
                                 pltpu.VMEM((tm, LRU_W), F32), pltpu.VMEM((SUBLANES, LRU_W), F32),
                                 pltpu.VMEM((SUBLANES, LRU_W), F32)],
                 sem=("arbitrary",), args=(xc, wa, wx, ba, bx, lam, h, h, h0, dh), comms=comms)


def _decay_tables(lg, reverse):
    ci = lax.broadcasted_iota(jnp.int32, (CHUNK, CHUNK), 0).astype(F32)
    mi = lax.broadcasted_iota(jnp.int32, (CHUNK, CHUNK), 1).astype(F32)
    if reverse:
        rel, pq, ps = mi - ci, CHUNK - ci, ci
    else:
        rel, pq, ps = ci - mi, ci + 1.0, CHUNK - 1.0 - ci
    relc = jnp.maximum(rel, 0.0)
    dm = jnp.where(rel >= 0, jnp.exp(lg * relc), 0.0)
    return relc, dm, jnp.exp(lg * pq), jnp.exp(lg * ps), jnp.exp(lg * float(CHUNK)), pq, ps


def _ret_fwd(proj, lgv, s0f, s0b, comms=()):
    t = proj.shape[0]
    n = t // CHUNK

    def one(q, k, v, lg, s_s, hh, o_ref, sp_ref, reverse):
        _, dm, wq, ws, g, _, _ = _decay_tables(lg, reverse)
        vb = v.astype(BF16)
        p = _dot_nt(q.astype(BF16), k.astype(BF16)) * dm
        s = s_s[hh]
        sp_ref[hh, 0] = s
        o_ref[:, DH * hh:DH * (hh + 1)] = _dot(p.astype(BF16), vb) + _dot((q * wq).astype(BF16), s.astype(BF16))
        s_s[hh] = g * s + _dot_tn((k * ws).astype(BF16), vb)

    def body(qf, kf, vf, qb, kb, vb, lg_ref, s0f_ref, s0b_ref, of_ref, ob_ref, spf_ref, spb_ref, sf_s, sb_s):
        @pl.when(pl.program_id(0) == 0)
        def _():
            sf_s[...] = s0f_ref[...]
            sb_s[...] = s0b_ref[...]
        for hh in range(HEADS):
            sl = slice(DH * hh, DH * (hh + 1))
            one(qf[:, sl], kf[:, sl], vf[:, sl], lg_ref[hh, 0:1, :], sf_s, hh, of_ref, spf_ref, False)
            one(qb[:, sl], kb[:, sl], vb[:, sl], lg_ref[hh, 1:2, :], sb_s, hh, ob_ref, spb_ref, True)

    blk = (CHUNK, RET_W)
    fw = [pl.BlockSpec(blk, lambda i, o=o: (i, o)) for o in range(3)]
    bw = [pl.BlockSpec(blk, lambda i, o=o: (n - 1 - i, o)) for o in range(3)]
    st = _full((HEADS, DH, DH))
    return _call(body, name="ret_fwd", grid=(n,),
                 in_specs=fw + bw + [_full((HEADS, 2, LANES)), st, st],
                 out_specs=[pl.BlockSpec(blk, lambda i: (i, 0)), pl.BlockSpec(blk, lambda i: (n - 1 - i, 0)),
                            pl.BlockSpec((HEADS, 1, DH, DH), lambda i: (0, i, 0, 0)),
                            pl.BlockSpec((HEADS, 1, DH, DH), lambda i: (0, n - 1 - i, 0, 0))],
                 out_shape=[_sds((t, RET_W)), _sds((t, RET_W)), _sds((HEADS, n, DH, DH)), _sds((HEADS, n, DH, DH))],
                 scratch_shapes=[pltpu.VMEM((HEADS, DH, DH), F32), pltpu.VMEM((HEADS, DH, DH), F32)],
                 sem=("arbitrary",), args=(proj, proj, proj, proj, proj, proj, lgv, s0f, s0b), comms=comms)


def _ret_bwd(proj, lgv, sgv, sprev, do, reverse, name, comms=()):
    t = proj.shape[0]
    n = t // CHUNK
    d = 1 if reverse else 0
    cidx = (lambda i: i) if reverse else (lambda i: n - 1 - i)

    def body(q_ref, k_ref, v_ref, lg_ref, sg_ref, s_ref, do_ref, dq_ref, dk_ref, dv_ref, ds0_ref, drd_ref, ds_s, acc_s):
        i = pl.program_id(0)

        @pl.when(i == 0)
        def _():
            ds_s[...] = jnp.zeros_like(ds_s)
            acc_s[...] = jnp.zeros_like(acc_s)
        for hh in range(HEADS):
            sl = slice(DH * hh, DH * (hh + 1))
            relc, dm, wq, ws, g, pq, ps = _decay_tables(lg_ref[hh, d:d + 1, :], reverse)
            q, k = q_ref[:, sl], k_ref[:, sl]
            qb, kb, vb = q.astype(BF16), k.astype(BF16), v_ref[:, sl].astype(BF16)
            p = _dot_nt(qb, kb) * dm
            s = s_ref[hh, 0]
            dob = do_ref[:, sl].astype(BF16)
            dsn = ds_s[hh]
            dsb = dsn.astype(BF16)
            dv_ref[:, sl] = _dot_tn(p.astype(BF16), dob) + _dot((k * ws).astype(BF16), dsb)
            dp = _dot_nt(dob, vb)
            dab = (dp * dm).astype(BF16)
            xq = _dot_nt(dob, s.astype(BF16))
            yk = _dot_nt(vb, dsb)
            dq_ref[:, sl] = _dot(dab, kb) + xq * wq
            dk_ref[:, sl] = _dot_tn(dab, qb) + yk * ws
            ds_s[hh] = g * dsn + _dot_tn((q * wq).astype(BF16), dob)
            part = (_sum0(dp * p * relc) + _sum0(xq * q * wq * pq) + _sum0(yk * k * ws * ps)
                    + _sum0(dsn * s) * g * float(CHUNK))
            acc_s[hh] += jnp.broadcast_to(part, (SUBLANES, LANES))

        @pl.when(i == n - 1)
        def _():
            ds0_ref[...] = ds_s[...]
            for hh in range(HEADS):
                tot = jnp.sum(acc_s[hh, 0:1, :], axis=1, keepdims=True)
                drd_ref[hh] = jnp.broadcast_to(tot, (SUBLANES, LANES)) * sg_ref[hh, d:d + 1, :]

    blk = (CHUNK, RET_W)
    qkv = [pl.BlockSpec(blk, lambda i, o=o: (cidx(i), o)) for o in range(3)]
    hc = pl.BlockSpec(blk, lambda i: (cidx(i), 0))
    lane = _full((HEADS, 2, LANES))
    return _call(body, name=name, grid=(n,),
                 in_specs=qkv + [lane, lane, pl.BlockSpec((HEADS, 1, DH, DH), lambda i: (0, cidx(i), 0, 0)), hc],
                 out_specs=[hc, hc, hc, _full((HEADS, DH, DH)), _full((HEADS, SUBLANES, LANES))],
                 out_shape=[_sds((t, RET_W))] * 3 + [_sds((HEADS, DH, DH)), _sds((HEADS, SUBLANES, LANES))],
                 scratch_shapes=[pltpu.VMEM((HEADS, DH, DH), F32), pltpu.VMEM((HEADS, SUBLANES, LANES), F32)],
                 sem=("arbitrary",), args=(proj, proj, proj, lgv, sgv, sprev, do), comms=comms)


def _ctx_weights(lg, l_len, reverse):
    pos = lax.broadcasted_iota(jnp.int32, (l_len, DH), 0).astype(F32)
    steps = pos if reverse else (l_len - 1.0 - pos)
    return jnp.exp(lg * steps), steps


def _ctx_state_fwd(projc, lgv):
    l_len = projc.shape[0]

    def body(k_ref, v_ref, lg_ref, sf_ref, sb_ref):
        k = k_ref[...]
        vb = v_ref[...].astype(BF16)
        for d, o_ref in ((0, sf_ref), (1, sb_ref)):
            w, _ = _ctx_weights(lg_ref[0, d:d + 1, :], l_len, d == 1)
            o_ref[0] = _dot_tn((k * w).astype(BF16), vb)

    st = pl.BlockSpec((1, DH, DH), lambda h: (h, 0, 0))
    return _pc(body, name="ctx_state_fwd", grid=(HEADS,),
               in_specs=[pl.BlockSpec((l_len, DH), lambda h: (0, HEADS + h)),
                         pl.BlockSpec((l_len, DH), lambda h: (0, 2 * HEADS + h)),
                         pl.BlockSpec((1, 2, LANES), lambda h: (h, 0, 0))],
               out_specs=[st, st], out_shape=[_sds((HEADS, DH, DH))] * 2,
               compiler_params=_params("arbitrary"))(projc, projc, lgv)


def _ctx_state_bwd(projc, lgv, sgv, dsf, dsb):
    l_len = projc.shape[0]

    def body(k_ref, v_ref, lg_ref, sg_ref, dsf_ref, dsb_ref, dk_ref, dv_ref, drd_ref):
        k = k_ref[...]
        vb = v_ref[...].astype(BF16)
        dk = jnp.zeros((l_len, DH), F32)
        dv = jnp.zeros((l_len, DH), F32)
        rows = []
        for d, ds_ref in ((0, dsf_ref), (1, dsb_ref)):
            w, steps = _ctx_weights(lg_ref[0, d:d + 1, :], l_len, d == 1)
            dsb16 = ds_ref[0].astype(BF16)
            dkw = _dot_nt(vb, dsb16)
            dk = dk + dkw * w
            dv = dv + _dot((k * w).astype(BF16), dsb16)
            tot = jnp.sum(_sum0(dkw * k * w * steps), axis=1, keepdims=True)
            rows.append(jnp.broadcast_to(tot, (1, LANES)) * sg_ref[0, d:d + 1, :])
        dk_ref[...] = dk
        dv_ref[...] = dv
        rid = lax.broadcasted_iota(jnp.int32, (SUBLANES, LANES), 0)
        drd_ref[0] = jnp.where(rid == 0, rows[0], jnp.where(rid == 1, rows[1], 0.0))

    st = pl.BlockSpec((1, DH, DH), lambda h: (h, 0, 0))
    lane = pl.BlockSpec((1, 2, LANES), lambda h: (h, 0, 0))
    hc = pl.BlockSpec((l_len, DH), lambda h: (0, h))
    return _pc(body, name="ctx_state_bwd", grid=(HEADS,),
               in_specs=[pl.BlockSpec((l_len, DH), lambda h: (0, HEADS + h)),
                         pl.BlockSpec((l_len, DH), lambda h: (0, 2 * HEADS + h)), lane, lane, st, st],
               out_specs=[hc, hc, pl.BlockSpec((1, SUBLANES, LANES), lambda h: (h, 0, 0))],
               out_shape=[_sds((l_len, RET_W)), _sds((l_len, RET_W)), _sds((HEADS, SUBLANES, LANES))],
               compiler_params=_params("arbitrary"))(projc, projc, lgv, sgv, dsf, dsb)


G_BLOCK = (3 * RET_W) // RET_W
GATE_BLOCK = (4 * RET_W + LRU_W) // LRU_W


def _head_norm(y):
    yc = y - jnp.mean(y, axis=-1, keepdims=True)
    rs = lax.rsqrt(jnp.mean(yc * yc, axis=-1, keepdims=True) + EPS)
    return yc * rs, rs


def _gelu_parts(z):
    th = jnp.tanh(GELU_K * (z + GELU_C * z * z * z))
    return 0.5 * z * (1.0 + th), th


def _mix_fwd(o_f, o_b, proj, hf, hb, w_out, x, g1):
    t = x.shape[0]
    tm = _tile(t, True)

    def body(of_ref, ob_ref, g_ref, gt_ref, hf_ref, hb_ref, w_ref, x_ref, g1_ref, x1_ref, cat_ref):
        o = of_ref[...] + ob_ref[...]
        g = g_ref[...]
        for hh in range(HEADS):
            sl = slice(DH * hh, DH * (hh + 1))
            nrm, _ = _head_norm(o[:, sl])
            gh = g[:, sl]
            cat_ref[:, sl] = (gh * _sigmoid(gh) * nrm).astype(BF16)
        gel, _ = _gelu_parts(gt_ref[...])
        cat_ref[:, RET_W:] = ((hf_ref[...] + hb_ref[...]) * gel).astype(BF16)
        x1_ref[...] = x_ref[...] + g1_ref[...] * _dot(cat_ref[...], w_ref[...])

    half = pl.BlockSpec((tm, RET_W), lambda i: (i, 0))
    big = pl.BlockSpec((tm, D_MODEL), lambda i: (i, 0))
    return _pc(body, name="mix_fwd", grid=(t // tm,),
               in_specs=[half, half, pl.BlockSpec((tm, RET_W), lambda i: (i, G_BLOCK)),
                         pl.BlockSpec((tm, LRU_W), lambda i: (i, GATE_BLOCK)), half, half,
                         _full((D_MODEL, D_MODEL)), big, _full((1, D_MODEL))],
               out_specs=[big, big], out_shape=[_sds((t, D_MODEL)), _sds((t, D_MODEL), BF16)],
               compiler_params=_params("arbitrary"))(o_f, o_b, proj, proj, hf, hb, w_out, x, g1)


def _mix_bwd(o_f, o_b, proj, hf, hb, w_out, cat, dx1, g1, comms=()):
    t = dx1.shape[0]
    tm = _tile(t, True)

    def body(of_ref, ob_ref, g_ref, gt_ref, hf_ref, hb_ref, w_ref, cat_ref, dx1_ref, g1_ref,
             do_ref, dhs_ref, dg_ref, dgt_ref, dyb_ref, dg1_ref):
        dx1v = dx1_ref[...]
        y = _dot(cat_ref[...], w_ref[...])

        @pl.when(pl.program_id(0) == 0)
        def _():
            dg1_ref[...] = jnp.zeros_like(dg1_ref)
        dg1_ref[...] += _sum0(dx1v * y)
        dyb = (g1_ref[...] * dx1v).astype(BF16)
        dyb_ref[...] = dyb
        dcat = _dot_nt(dyb, w_ref[...])
        o = of_ref[...] + ob_ref[...]
        g = g_ref[...]
        for hh in range(HEADS):
            sl = slice(DH * hh, DH * (hh + 1))
            nrm, rs = _head_norm(o[:, sl])
            gh = g[:, sl]
            sg = _sigmoid(gh)
            dret = dcat[:, sl]
            dg_ref[:, sl] = dret * nrm * (sg * (1.0 + gh * (1.0 - sg)))
            dn = dret * (gh * sg)
            dyc = rs * (dn - nrm * jnp.mean(dn * nrm, axis=-1, keepdims=True))
            do_ref[:, sl] = dyc - jnp.mean(dyc, axis=-1, keepdims=True)
        z = gt_ref[...]
        gel, th = _gelu_parts(z)
        dlru = dcat[:, RET_W:]
        dhs_ref[...] = dlru * gel
        dgel = 0.5 * (1.0 + th) + 0.5 * z * (1.0 - th * th) * GELU_K * (1.0 + 3.0 * GELU_C * z * z)
        dgt_ref[...] = dlru * (hf_ref[...] + hb_ref[...]) * dgel

    half = pl.BlockSpec((tm, RET_W), lambda i: (i, 0))
    big = pl.BlockSpec((tm, D_MODEL), lambda i: (i, 0))
    return _call(body, name="mix_bwd", grid=(t // tm,),
                 in_specs=[half, half, pl.BlockSpec((tm, RET_W), lambda i: (i, G_BLOCK)),
                           pl.BlockSpec((tm, LRU_W), lambda i: (i, GATE_BLOCK)), half, half,
                           _full((D_MODEL, D_MODEL)), big, big, _full((1, D_MODEL))],
                 out_specs=[half, half, half, half, big, _full((1, D_MODEL))],
                 out_shape=[_sds((t, RET_W))] * 4 + [_sds((t, D_MODEL), BF16), _sds((1, D_MODEL))],
                 scratch_shapes=[], sem=("arbitrary",), args=(o_f, o_b, proj, proj, hf, hb, w_out, cat, dx1, g1),
                 comms=comms)


def _mlp(x1, n2g, sh2, sc2, g2, fg, w1_parts, w2_parts, tgt):
    t = x1.shape[0]
    tm = _tile(t)
    hb_ = MLP_H // N_CHIP
    q_rows = hb_ // 4
    n_cp = 4 * N_DEV

    def body(x1_ref, n2g_ref, sh2_ref, sc2_ref, g2_ref, fg_ref, w1a, w1b, w2a, w2b, tgt_ref,
             dx1_ref, h2b_ref, ab_ref, dub_ref, dmb_ref, dsc_ref, dsh_ref, dg2_ref, dn2_ref, dfg_ref, loss_ref,
             w1_s, w2_s, r_s, sems):
        @pl.when(pl.program_id(0) == 0)
        def _():
            cps = []
            for p, parts in enumerate(((w1a, w2a), (w1b, w2b))):
                for d in range(N_DEV):
                    rows = pl.ds(2 * q_rows * (d % 2) + q_rows * p, q_rows)
                    for src, dst in zip(parts, (w1_s, w2_s)):
                        cps.append(pltpu.make_async_copy(src.at[d], dst.at[d // 2, rows], sems.at[len(cps)]))
            for cp in cps:
                cp.start()
            for r in (dsc_ref, dsh_ref, dg2_ref, dn2_ref, dfg_ref, loss_ref):
                r[...] = jnp.zeros_like(r)
            for cp in cps:
                cp.wait()
        x1v = x1_ref[...]
        n2g, sc2, g2, fg = n2g_ref[...], sc2_ref[...], g2_ref[...], fg_ref[...]
        xh, _ = _rms(x1v)
        h2b = (xh * n2g * (1.0 + sc2) + sh2_ref[...]).astype(BF16)
        h2b_ref[...] = h2b
        m = jnp.zeros((tm, D_MODEL), F32)
        for j in range(N_CHIP):
            sl = slice(hb_ * j, hb_ * (j + 1))
            r = jnp.maximum(_dot(h2b, w1_s[j]), 0.0)
            r_s[:, sl] = r
            ab = (r * r).astype(BF16)
            ab_ref[:, sl] = ab
            m = m + _dot(ab, w2_s[j])
        x2 = x1v + g2 * m
        x2h, r2 = _rms(x2)
        err = x2h * fg - tgt_ref[...]
        loss_ref[...] += _sum0(err * err)
        dout = err * (1.0 / D_MODEL)
        dfg_ref[...] += _sum0(dout * x2h)
        dxh = dout * fg
        dx2 = r2 * (dxh - x2h * jnp.mean(dxh * x2h, axis=-1, keepdims=True))
        dg2_ref[...] += _sum0(dx2 * m)
        dmb = (g2 * dx2).astype(BF16)
        dmb_ref[...] = dmb
        dh2 = jnp.zeros((tm, D_MODEL), F32)
        for j in range(N_CHIP):
            sl = slice(hb_ * j, hb_ * (j + 1))
            dub = (_dot_nt(dmb, w2_s[j]) * (2.0 * r_s[:, sl])).astype(BF16)
            dub_ref[:, sl] = dub
            dh2 = dh2 + _dot_nt(dub, w1_s[j])
        dx, dn2_t, dsh_t, dsc_t = _norm_mod_bwd(x1v, n2g, sc2, dh2)
        dx1_ref[...] = dx2 + dx
        dn2_ref[...] += dn2_t
        dsh_ref[...] += dsh_t
        dsc_ref[...] += dsc_t

        @pl.when(pl.program_id(0) == t // tm - 1)
        def _():
            tot = jnp.sum(loss_ref[...], axis=1, keepdims=True) * (0.5 / D_MODEL)
            loss_ref[...] = jnp.broadcast_to(tot, loss_ref.shape)

    row = _full((1, D_MODEL))
    big = pl.BlockSpec((tm, D_MODEL), lambda i: (i, 0))
    wide = pl.BlockSpec((tm, MLP_H), lambda i: (i, 0))
    return _pc(body, name="mlp", grid=(t // tm,),
               in_specs=[big, row, row, row, row, row, ANY, ANY, ANY, ANY, big],
               out_specs=[big, big, wide, wide, big, row, row, row, row, row, row],
               out_shape=[_sds((t, D_MODEL)), _sds((t, D_MODEL), BF16), _sds((t, MLP_H), BF16), _sds((t, MLP_H), BF16),
                          _sds((t, D_MODEL), BF16)] + [_sds((1, D_MODEL))] * 6,
               scratch_shapes=[pltpu.VMEM((N_CHIP, D_MODEL, hb_), BF16), pltpu.VMEM((N_CHIP, hb_, D_MODEL), BF16),
                               pltpu.VMEM((tm, MLP_H), F32), pltpu.SemaphoreType.DMA((n_cp,))],
               compiler_params=_params("arbitrary"))(x1, n2g, sh2, sc2, g2, fg, *w1_parts, *w2_parts, tgt)


def _tn(a, b, nj, a_blocked, b_blocked, name, extra=None, comms=()):
    t = a.shape[0]
    m = a.shape[1] // (nj if a_blocked else 1)
    n = b.shape[1] // (nj if b_blocked else 1)
    bk = 1024 if t % 1024 == 0 else (512 if t % 512 == 0 else t)
    nk = t // bk
    a_map = (lambda j, k: (k, j)) if a_blocked else (lambda j, k: (k, 0))
    b_map = (lambda j, k: (k, j)) if b_blocked else (lambda j, k: (k, 0))
    in_specs = [pl.BlockSpec((bk, m), a_map), pl.BlockSpec((bk, n), b_map)]
    args = [a, b]
    if extra is not None:
        a2, b2 = extra
        t2 = a2.shape[0]
        in_specs += [pl.BlockSpec((t2, m), (lambda j, k: (0, j)) if a_blocked else (lambda j, k: (0, 0))),
                     pl.BlockSpec((t2, n), (lambda j, k: (0, j)) if b_blocked else (lambda j, k: (0, 0)))]
        args += [a2, b2]

    def body(*refs):
        a_ref, b_ref = refs[0], refs[1]
        o_ref, acc = refs[-2], refs[-1]
        k = pl.program_id(1)

        @pl.when(k == 0)
        def _():
            acc[...] = jnp.zeros_like(acc)
        acc[...] += _dot_tn(a_ref[...].astype(BF16), b_ref[...].astype(BF16))

        @pl.when(k == nk - 1)
        def _():
            if extra is not None:
                acc[...] += _dot_tn(refs[2][...].astype(BF16), refs[3][...].astype(BF16))
            o_ref[0] = acc[...]

    (out,), couts = _call(body, name=name, grid=(nj, nk), in_specs=in_specs,
                          out_specs=[pl.BlockSpec((1, m, n), lambda j, k: (j, 0, 0))], out_shape=[_sds((nj, m, n))],
                          scratch_shapes=[pltpu.VMEM((m, n), F32)], sem=("arbitrary", "arbitrary"), args=args,
                          comms=comms)
    return (out, couts) if comms else out


ROW_LOSS = 0
ROW_DMOD = 1
ROW_DMODC = 7
ROW_N1, ROW_N2, ROW_FG, ROW_CB = 9, 10, 11, 12
ROW_BA, ROW_BX, ROW_LAM = 13, 15, 17
ROW_CW = 20
ROW_RD = 24
SLAB_ROWS = 32
SEG = D_MODEL // 2


def _pack_small(rows, drd, cw2, cb2, lru2, gates):
    n_rows, n_lru = len(rows), len(lru2)

    def body(*refs):
        r = refs[:n_rows]
        drd_f, drd_b, drd_c, cw_a, cw_b, cb_a, cb_b = refs[n_rows:n_rows + 7]
        lru = refs[n_rows + 7:n_rows + 7 + n_lru]
        gf_ref, gb_ref, slab, ga, gx = refs[n_rows + 7 + n_lru:]
        slab[...] = jnp.zeros_like(slab)
        slab[ROW_LOSS:ROW_LOSS + 1, :] = r[0][...]
        for k in range(N_MOD):
            slab[ROW_DMOD + k:ROW_DMOD + k + 1, :] = r[1 + k][...]
        slab[ROW_DMODC:ROW_DMODC + 1, :] = r[7][...]
        slab[ROW_DMODC + 1:ROW_DMODC + 2, :] = r[8][...]
        slab[ROW_N1:ROW_N1 + 1, :] = r[9][...] + r[10][...]
        slab[ROW_N2:ROW_N2 + 1, :] = r[11][...]
        slab[ROW_FG:ROW_FG + 1, :] = r[12][...]
        slab[ROW_CB:ROW_CB + 1, 0:LRU_W] = cb_a[...] + cb_b[...]
        for k, row in enumerate((ROW_BA, ROW_BA + 1, ROW_BX, ROW_BX + 1, ROW_LAM, ROW_LAM + 1)):
            slab[row:row + 1, 0:LRU_W] = lru[2 * k][...] + lru[2 * k + 1][...]
        slab[ROW_CW:ROW_CW + 4, 0:LRU_W] = cw_a[...] + cw_b[...]
        for h in range(HEADS):
            slab[ROW_RD + h:ROW_RD + h + 1, 0:LANES] = drd_f[h, 0:1, :] + drd_c[h, 0:1, :]
            slab[ROW_RD + HEADS + h:ROW_RD + HEADS + h + 1, 0:LANES] = drd_b[h, 0:1, :] + drd_c[h, 1:2, :]
        for d, g_ref in enumerate((gf_ref, gb_ref)):
            for n in range(LRU_BLOCKS):
                blk = slice(LRU_BD * n, LRU_BD * (n + 1))
                ga[blk, LRU_BD * d:LRU_BD * (d + 1)] = g_ref[0, blk, blk].astype(BF16)
                gx[blk, LRU_BD * d:LRU_BD * (d + 1)] = g_ref[1, blk, blk].astype(BF16)

    args = list(rows) + list(drd) + list(cw2) + list(cb2) + list(lru2) + list(gates)
    gate_shape = (LRU_W, 2 * LRU_BD)
    return _pc(body, name="pack_small", in_specs=[_full(a.shape) for a in args],
               out_specs=[_full((SLAB_ROWS, D_MODEL)), _full(gate_shape), _full(gate_shape)],
               out_shape=[_sds((SLAB_ROWS, D_MODEL)), _sds(gate_shape, BF16), _sds(gate_shape, BF16)],
               compiler_params=_params())(*args)


def _adam_math(w, g, m, v):
    mn = ADAM_B1 * m + (1.0 - ADAM_B1) * g
    vn = ADAM_B2 * v + (1.0 - ADAM_B2) * (g * g)
    mh = mn / (1.0 - ADAM_B1 ** ADAM_STEP)
    vh = vn / (1.0 - ADAM_B2 ** ADAM_STEP)
    return -ADAM_LR * (mh / (jnp.sqrt(vh) + ADAM_EPS) + ADAM_WD * w), mn, vn


SMALL_PARAMS = ("b_ada", "norm1_g", "norm2_g", "final_g", "ret_decay", "conv_w", "conv_b", "lru_wa", "lru_ba", "lru_wx",
                "lru_bx", "lru_lambda")


def _finalize_small(chip_idx, slab_all, ga_all, gx_all, wmv):
    n_p = len(SMALL_PARAMS)
    flat = [a for nm in SMALL_PARAMS for a in wmv[nm]]
    ada_n = N_MOD * D_MODEL // N_CHIP

    def body(c_ref, slab_ref, ga_ref, gx_ref, *refs):
        prm = {nm: refs[3 * k:3 * k + 3] for k, nm in enumerate(SMALL_PARAMS)}
        outs = {nm: refs[3 * n_p + 4 * k:3 * n_p + 4 * k + 4] for k, nm in enumerate(SMALL_PARAMS)}
        b128_ref, dmc_ref, loss_ref = refs[3 * n_p + 4 * n_p:]
        chip = c_ref[0]

        def pick(fn):
            acc = fn(0)
            for j in range(1, N_CHIP):
                acc = jnp.where(chip == j, fn(j), acc)
            return acc

        tot = slab_ref[0]
        for d in range(1, N_DEV):
            tot = tot + slab_ref[d]

        def update(nm, g, sl=None, rows=None):
            w_ref, m_ref, v_ref = prm[nm]
            g_ref, d_ref, mo_ref, vo_ref = outs[nm]
            ix = (slice(None) if rows is None else rows, slice(None) if sl is None else sl)
            dl, mn, vn = _adam_math(w_ref[ix], g, m_ref[ix], v_ref[ix])
            g_ref[ix] = g
            d_ref[ix] = dl
            mo_ref[ix] = mn
            vo_ref[ix] = vn

        loss_ref[...] = jnp.broadcast_to(tot[ROW_LOSS:ROW_LOSS + 1, 0:LANES], (SUBLANES, LANES))
        for k in range(N_MOD):
            g = tot[ROW_DMOD + k:ROW_DMOD + k + 1, :]
            if k < 2:
                g = g + tot[ROW_DMODC + k:ROW_DMODC + k + 1, :]
            update("b_ada", g, slice(D_MODEL * k, D_MODEL * (k + 1)))
        update("norm1_g", tot[ROW_N1:ROW_N1 + 1, :])
        update("norm2_g", tot[ROW_N2:ROW_N2 + 1, :])
        update("final_g", tot[ROW_FG:ROW_FG + 1, :])
        update("ret_decay", tot[ROW_RD:ROW_RD + SUBLANES, 0:LANES])
        update("conv_b", tot[ROW_CB:ROW_CB + 1, 0:LRU_W])
        update("conv_w", pick(lambda j: tot[ROW_CW:ROW_CW + 4, LANES * j:LANES * (j + 1)]))
        for nm, row in (("lru_ba", ROW_BA), ("lru_bx", ROW_BX), ("lru_lambda", ROW_LAM)):
            update(nm, pick(lambda j, row=row: tot[row:row + 2, LANES * j:LANES * (j + 1)]))
        for nm, g_all in (("lru_wa", ga_ref), ("lru_wx", gx_ref)):
            for dr in range(2):
                lanes = slice(LRU_BD * dr, LRU_BD * (dr + 1))
                g = g_all[0, :, lanes].astype(F32)
                for d in range(1, N_DEV):
                    g = g + g_all[d, :, lanes].astype(F32)
                update(nm, g, rows=slice(LRU_W * dr, LRU_W * (dr + 1)))

        def seg(rows6, s):
            return rows6[s // 2][:, SEG * (s % 2):SEG * (s % 2 + 1)]

        b128_ref[...] = jnp.zeros_like(b128_ref)
        dmc_ref[...] = jnp.zeros_like(dmc_ref)
        zero = jnp.zeros((1, D_MODEL), F32)
        ctx6 = [tot[ROW_DMODC:ROW_DMODC + 1, :], tot[ROW_DMODC + 1:ROW_DMODC + 2, :]] + [zero] * (N_MOD - 2)
        for q in range(ada_n // SEG):
            cols = slice(SEG * q, SEG * (q + 1))
            for d in range(N_DEV):
                rows6 = [slab_ref[d, ROW_DMOD + k:ROW_DMOD + k + 1, :] for k in range(N_MOD)]
                b128_ref[d:d + 1, cols] = pick(lambda j, rows6=rows6: seg(rows6, 3 * j + q))
            c = pick(lambda j: seg(ctx6, 3 * j + q))
            b128_ref[N_DEV:N_DEV + 1, cols] = c
            dmc_ref[0:1, cols] = c

    out_shape = []
    for nm in SMALL_PARAMS:
        out_shape += [_sds(wmv[nm][0].shape)] * 4
    out_shape += [_sds((LANES, ada_n)), _sds((SUBLANES, ada_n)), _sds((SUBLANES, LANES))]
    args = [slab_all, ga_all, gx_all] + flat
    grid_spec = pltpu.PrefetchScalarGridSpec(
        num_scalar_prefetch=1, grid=(1,), in_specs=[_full(a.shape) for a in args],
        out_specs=[_full(s.shape) for s in out_shape])
    outs = _pc(body, name="finalize_small", grid_spec=grid_spec, out_shape=out_shape,
               compiler_params=_params("arbitrary"))(chip_idx, *args)
    res = {nm: tuple(outs[4 * k:4 * k + 4]) for k, nm in enumerate(SMALL_PARAMS)}
    return res, outs[4 * n_p], outs[4 * n_p + 1], outs[4 * n_p + 2]


def _block_diag(w):
    eye = jnp.eye(LRU_BLOCKS, dtype=F32)
    return (w[:, :, None, :] * eye[:, None, :, None]).reshape(LRU_W, LRU_W).astype(BF16)


def _lane_rep(v8):
    return jnp.broadcast_to(v8.reshape(SUBLANES, 1), (SUBLANES, LANES))


def kernel(x, c, ctx, c_ctx, w_ada, b_ada, norm1_g, norm2_g, w_in, ret_decay, conv_w, conv_b, lru_wa, lru_ba, lru_wx, lru_bx, lru_lambda, w_out, w_mlp1, w_mlp2, final_g, loss_target, m_c_ctx, m_w_ada, m_b_ada, m_norm1_g, m_norm2_g, m_w_in, m_ret_decay, m_conv_w, m_conv_b, m_lru_wa, m_lru_ba, m_lru_wx, m_lru_bx, m_lru_lambda, m_w_out, m_w_mlp1, m_w_mlp2, m_final_g, v_c_ctx, v_w_ada, v_b_ada, v_norm1_g, v_norm2_g, v_w_in, v_ret_decay, v_conv_w, v_conv_b, v_lru_wa, v_lru_ba, v_lru_wx, v_lru_bx, v_lru_lambda, v_w_out, v_w_mlp1, v_w_mlp2, v_final_g):
    ax, ay, ac = lax.axis_index("x"), lax.axis_index("y"), lax.axis_index("c")
    chip = 2 * ax + ay
    dev = 4 * ax + 2 * ay + ac
    c_idx = ac.reshape(1).astype(jnp.int32)
    j_idx = chip.reshape(1).astype(jnp.int32)

    xt = x[0]
    t_len = xt.shape[0]
    ctxt = ctx[0]
    l_len = ctxt.shape[0]
    tgt = loss_target[0]
    ada_n = w_ada.shape[2]

    def my_half(w2d):
        r = w2d.shape[0] // 2
        return lax.dynamic_slice_in_dim(w2d, ac * r, r, axis=0).astype(BF16)

    pad8 = lambda a: jnp.pad(a, ((0, SUBLANES - a.shape[0]), (0, 0)))
    small = jnp.concatenate([pad8(conv_w[0]), pad8(lru_ba[0]), pad8(lru_bx[0]), pad8(lru_lambda[0])], axis=0)
    gw_in, c_all, small_all = _all_gather([my_half(w_in[0]), pad8(c), small], "gather_head")
    w4 = gw_in.reshape(N_CHIP, D_MODEL, IN_COLS // N_CHIP)

    a16, lgv, sgv = _prep(c_all[:, 0, :], c_ctx, ret_decay[0])
    b_shard = lax.dynamic_slice_in_dim(b_ada, chip * ada_n, ada_n, axis=1)
    (mod_parts,) = _all_gather([_mod_fwd(a16, w_ada[0], b_shard)], "gather_mod")
    mod_all = mod_parts[0::2].transpose(1, 0, 2).reshape(16, N_CHIP * ada_n)
    mod_me = lax.dynamic_slice_in_dim(mod_all, dev, 1, axis=0)
    sh1, sc1, g1, sh2, sc2, g2 = [mod_me[:, D_MODEL * k:D_MODEL * (k + 1)] for k in range(N_MOD)]
    csh1, csc1 = mod_all[8:9, 0:D_MODEL], mod_all[8:9, D_MODEL:2 * D_MODEL]

    cos2, sin2 = _rotary_tables(t_len)
    cos_c, sin_c = jnp.ones((l_len, DH), F32), jnp.zeros((l_len, DH), F32)
    n1g, n2g = norm1_g, norm2_g
    fg = final_g.reshape(1, D_MODEL)

    small_full = small_all[0::2].transpose(1, 0, 2).reshape(4 * SUBLANES, LRU_W)
    cw = small_full[0:4]
    cb = conv_b
    ba_f, ba_b = small_full[8:9], small_full[9:10]
    bx_f, bx_b = small_full[16:17], small_full[17:18]
    lam_f, lam_b = small_full[24:25], small_full[25:26]
    wa_f, wa_b = _block_diag(lru_wa[0, 0]), _block_diag(lru_wa[0, 1])
    wx_f, wx_b = _block_diag(lru_wx[0, 0]), _block_diag(lru_wx[0, 1])
    zero_h = jnp.zeros((1, LRU_W), F32)

    projc, hcb16 = _inproj_fwd(ctxt, n1g, csh1, csc1, w4, cos_c, sin_c, "inproj_fwd_ctx")
    s_f, s_b = _ctx_state_fwd(projc, lgv)
    xcc = _conv_fwd(projc, cw, cb, "conv_fwd_ctx")
    hcf = _lru_fwd(xcc, wa_f, wx_f, ba_f, bx_f, lam_f, zero_h, False, "lru_fwd_ctx_f")
    hcbk = _lru_fwd(xcc, wa_b, wx_b, ba_b, bx_b, lam_b, zero_h, True, "lru_fwd_ctx_b")
    lru_sf, lru_sb = hcf[l_len - 1:l_len], hcbk[0:1]

    h1, h2 = my_half(w_mlp1[0]), my_half(w_mlp2[0])
    q = h1.shape[0] // 2
    (proj, hb16), ((gw_1a,),) = _inproj_fwd(xt, n1g, sh1, sc1, w4, cos2, sin2, "inproj_fwd",
                                           comms=(_AllGather([h1[:q]]),))
    (o_f, o_b, spf, spb), ((gw_1b, gw_out),) = _ret_fwd(proj, lgv, s_f, s_b,
                                                       comms=(_AllGather([h1[q:], my_half(w_out[0])]),))
    xcl = _conv_fwd(proj, cw, cb, "conv_fwd")
    hf, ((gw_2a,),) = _lru_fwd(xcl, wa_f, wx_f, ba_f, bx_f, lam_f, lru_sf, False, "lru_fwd_f",
                              comms=(_AllGather([h2[:q]]),))
    hbk, ((gw_2b,),) = _lru_fwd(xcl, wa_b, wx_b, ba_b, bx_b, lam_b, lru_sb, True, "lru_fwd_b",
                               comms=(_AllGather([h2[q:]]),))
    wo = gw_out.reshape(D_MODEL, D_MODEL)
    x1, cat = _mix_fwd(o_f, o_b, proj, hf, hbk, wo, xt, g1)

    (dx1, h2b, ab, dub, dmb, dsc2, dsh2, dg2, dn2g, dfg, lossv) = _mlp(
        x1, n2g, sh2, sc2, g2, fg, (gw_1a, gw_1b), (gw_2a, gw_2b), tgt)
    gw_mlp1 = _tn(h2b, dub, N_CHIP, False, True, "grad_w_mlp1")
    b_1 = gw_mlp1.reshape(N_DEV, D_MODEL // 2, MLP_H // N_CHIP)
    gw_mlp2, ((r_1,),) = _tn(ab, dmb, N_CHIP, True, False, "grad_w_mlp2", comms=(_pair_exchange([b_1]),))

    jc_idx = jnp.concatenate([j_idx, c_idx])
    b_2 = gw_mlp2.reshape(N_DEV, MLP_H // N_DEV, D_MODEL)
    (do, dhs, dg, dgate, dyb, dg1), ((r_2,),) = _mix_bwd(
        o_f, o_b, proj, hf, hbk, wo, cat, dx1, g1, comms=(_pair_exchange([b_2]),))
    gw_o = _tn(cat, dyb, 1, False, False, "grad_w_out")
    b_o = gw_o.reshape(N_DEV, D_MODEL // N_DEV, D_MODEL)
    p_1, pb_1 = _pair_add(b_1, r_1, c_idx, "rs_pair_add_w_mlp1")
    p_2, pb_2 = _pair_add(b_2, r_2, c_idx, "rs_pair_add_w_mlp2")

    (dq_f, dk_f, dv_f, ds_f, drd_f), ((q_1,), (r_o,)) = _ret_bwd(
        proj, lgv, sgv, spf, do, False, "ret_bwd_f", comms=(_chip_exchange([pb_1]), _pair_exchange([b_o])))
    p_o, pb_o = _pair_add(b_o, r_o, c_idx, "rs_pair_add_w_out")
    h_1 = _chip_add(p_1, q_1, jc_idx, "rs_chip_add_w_mlp1")

    (dq_b, dk_b, dv_b, ds_b, drd_b), ((q_2,), (f_1,)) = _ret_bwd(
        proj, lgv, sgv, spb, do, True, "ret_bwd_b", comms=(_chip_exchange([pb_2]), _pair_gather([h_1])))
    h_2 = _chip_add(p_2, q_2, jc_idx, "rs_chip_add_w_mlp2")

    (dxc_f, dpre_f, dba_f, dbx_f, dlam_f, dh0_f), ((q_o,), (f_2,)) = _lru_bwd(
        xcl, wa_f, wx_f, ba_f, bx_f, lam_f, hf, lru_sf, dhs, False, "lru_bwd_f",
        comms=(_chip_exchange([pb_o]), _pair_gather([h_2])))
    h_o = _chip_add(p_o, q_o, jc_idx, "rs_chip_add_w_out")
    (dxc_b, dpre_b, dba_b, dbx_b, dlam_b, dh0_b), ((f_o,),) = _lru_bwd(
        xcl, wa_b, wx_b, ba_b, bx_b, lam_b, hbk, lru_sb, dhs, True, "lru_bwd_b", comms=(_pair_gather([h_o]),))
    dxr, dcw, dcb = _conv_bwd(dxc_f, dxc_b, proj, cw, "conv_bwd")
    grad_x, dpb, dn1g, dsh1, dsc1 = _inproj_bwd(
        xt, n1g, sh1, sc1, w4, cos2, sin2, [dq_f, dq_b, dk_f, dk_b, dv_f, dv_b, dg, dxr, dgate], dx1, "inproj_bwd")

    dkc, dvc, drd_c = _ctx_state_bwd(projc, lgv, sgv, ds_f, ds_b)
    zc = jnp.zeros((l_len, LRU_W), F32)
    dhc_f = lax.dynamic_update_slice(zc, dh0_f, (l_len - 1, 0))
    dhc_b = lax.dynamic_update_slice(zc, dh0_b, (0, 0))
    (dxcc_f, dprec_f, dbac_f, dbxc_f, dlamc_f, _), _ = _lru_bwd(
        xcc, wa_f, wx_f, ba_f, bx_f, lam_f, hcf, zero_h, dhc_f, False, "lru_bwd_ctx_f")
    (dxcc_b, dprec_b, dbac_b, dbxc_b, dlamc_b, _), _ = _lru_bwd(
        xcc, wa_b, wx_b, ba_b, bx_b, lam_b, hcbk, zero_h, dhc_b, True, "lru_bwd_ctx_b")
    dxrc, dcw_c, dcb_c = _conv_bwd(dxcc_f, dxcc_b, projc, cw, "conv_bwd_ctx")
    zr = jnp.zeros((l_len, RET_W), F32)
    _, dpbc, dn1g_c, dcsh1, dcsc1 = _inproj_bwd(
        ctxt, n1g, csh1, csc1, w4, cos_c, sin_c, [zr, zr, dkc, zr, dvc, zr, zr, dxrc, zr],
        jnp.zeros((l_len, D_MODEL), F32), "inproj_bwd_ctx")

    gw_i = _tn(hb16, dpb, N_CHIP, False, True, "grad_w_in", extra=(hcb16, dpbc))
    b_i = gw_i.reshape(N_DEV, D_MODEL // 2, IN_COLS // N_CHIP)
    gwa_f, ((r_i,),) = _tn(xcl, dpre_f, 2, False, True, "grad_lru_gates_f", extra=(xcc, dprec_f),
                           comms=(_pair_exchange([b_i]),))
    p_i, pb_i = _pair_add(b_i, r_i, c_idx, "rs_pair_add_w_in")
    gwa_b, ((q_i,),) = _tn(xcl, dpre_b, 2, False, True, "grad_lru_gates_b", extra=(xcc, dprec_b),
                           comms=(_chip_exchange([pb_i]),))
    h_i = _chip_add(p_i, q_i, jc_idx, "rs_chip_add_w_in")
    g_out, g_1, g_2 = _shard_of(f_o), _shard_of(f_1), _shard_of(f_2)
    big = {}
    (d_, mn, vn), ((f_i,),) = _adamw(w_mlp1[0], g_1, m_w_mlp1[0], v_w_mlp1[0], "adamw_w_mlp1",
                                     comms=(_pair_gather([h_i]),))
    big["w_mlp1"] = (g_1[None], d_[None], mn[None], vn[None])
    g_in = _shard_of(f_i)
    for nm, w, g, m, v in (("w_in", w_in, g_in, m_w_in, v_w_in), ("w_out", w_out, g_out, m_w_out, v_w_out),
                           ("w_mlp2", w_mlp2, g_2, m_w_mlp2, v_w_mlp2)):
        d_, mn, vn = _adamw(w[0], g, m[0], v[0], "adamw_" + nm)
        big[nm] = (g[None], d_[None], mn[None], vn[None])

    slab, ga, gx = _pack_small(
        [lossv, dsh1, dsc1, dg1, dsh2, dsc2, dg2, dcsh1, dcsc1, dn1g, dn1g_c, dn2g, dfg],
        (drd_f, drd_b, drd_c), (dcw, dcw_c), (dcb, dcb_c),
        (dba_f, dbac_f, dba_b, dbac_b, dbx_f, dbxc_f, dbx_b, dbxc_b, dlam_f, dlamc_f, dlam_b, dlamc_b),
        (gwa_f, gwa_b))
    slab_all, ga_all, gx_all = _all_gather([slab, ga, gx], "gather_small_grads")
    params = {
        "b_ada": (b_ada, m_b_ada, v_b_ada), "norm1_g": (norm1_g, m_norm1_g, v_norm1_g),
        "norm2_g": (norm2_g, m_norm2_g, v_norm2_g), "final_g": (final_g, m_final_g, v_final_g),
        "ret_decay": (ret_decay, m_ret_decay, v_ret_decay), "conv_w": (conv_w, m_conv_w, v_conv_w),
        "conv_b": (conv_b, m_conv_b, v_conv_b), "lru_wa": (lru_wa, m_lru_wa, v_lru_wa),
        "lru_ba": (lru_ba, m_lru_ba, v_lru_ba), "lru_wx": (lru_wx, m_lru_wx, v_lru_wx),
        "lru_bx": (lru_bx, m_lru_bx, v_lru_bx), "lru_lambda": (lru_lambda, m_lru_lambda, v_lru_lambda),
    }
    as2d = {
        "b_ada": lambda a: a, "norm1_g": lambda a: a, "norm2_g": lambda a: a, "conv_b": lambda a: a,
        "final_g": lambda a: a.reshape(1, D_MODEL), "ret_decay": lambda a: _lane_rep(a.reshape(-1)),
        "conv_w": lambda a: a[0], "lru_ba": lambda a: a[0], "lru_bx": lambda a: a[0], "lru_lambda": lambda a: a[0],
        "lru_wa": lambda a: a.reshape(2 * LRU_W, LRU_BD), "lru_wx": lambda a: a.reshape(2 * LRU_W, LRU_BD),
    }
    res, b128, dmc8, loss8 = _finalize_small(
        j_idx, slab_all, ga_all, gx_all, {nm: tuple(as2d[nm](a) for a in params[nm]) for nm in SMALL_PARAMS})
    loss = loss8[0, 0]
    small_out = {}
    for nm in SMALL_PARAMS:
        shp = params[nm][0].shape
        if nm == "ret_decay":
            small_out[nm] = tuple(o[:, 0].reshape(shp) for o in res[nm])
        else:
            small_out[nm] = tuple(o.reshape(shp) for o in res[nm])

    g_ada = _ada_grad(jnp.pad(a16.T, ((0, 0), (0, LANES - 16))), b128)
    d_ada, m_ada, v_ada = _adamw(w_ada[0], g_ada, m_w_ada[0], v_w_ada[0], "adamw_w_ada")

    (cparts,) = _all_gather([_cctx_partial(dmc8, w_ada[0])], "gather_cctx")
    g_cc, d_cc, m_cc, v_cc = _cctx_final(cparts, c_ctx, m_c_ctx, v_c_ctx)
    small_out["c_ctx"] = tuple(a.reshape(D_MODEL) for a in (g_cc, d_cc, m_cc, v_cc))
    small_out["w_ada"] = (g_ada[None], d_ada[None], m_ada[None], v_ada[None])
    small_out.update(big)

    order = ["c_ctx", "w_ada", "b_ada", "norm1_g", "norm2_g", "w_in", "ret_decay", "conv_w", "conv_b", "lru_wa", "lru_ba",
             "lru_wx", "lru_bx", "lru_lambda", "w_out", "w_mlp1", "w_mlp2", "final_g"]
    outs = [loss, grad_x[None]]
    for k in range(4):
        outs += [small_out[nm][k] for nm in order]
    return tuple(outs)
```

```python
import math

import jax
import jax.numpy as jnp
from jax import lax
from jax.experimental import pallas as pl
from jax.experimental.pallas import tpu as pltpu

F32 = jnp.float32
BF16 = jnp.bfloat16

D_MODEL = 1024
HEADS = 4
DH = 128
CHUNK = 128
RET_W = HEADS * DH
LRU_W = 512
LRU_BLOCKS = 8
LRU_BD = LRU_W // LRU_BLOCKS
LRU_C = 8.0
IN_COLS = 4 * RET_W + 2 * LRU_W
MLP_H = 4 * D_MODEL
N_MOD = 6
GRID_W = 64
ROPE_BASE = 10000.0
K_SCALE = DH ** -0.5
EPS = 1e-6
GELU_K = math.sqrt(2.0 / math.pi)
GELU_C = 0.044715

ADAM_LR = 0.001
ADAM_B1 = 0.9
ADAM_B2 = 0.999
ADAM_EPS = 1e-08
ADAM_WD = 0.01
ADAM_STEP = 10

N_DEV = 8
N_CHIP = 4
SUBLANES = 8
LANES = 128
VMEM_LIMIT_V7X = 56 * 1024 * 1024
MESH = pl.DeviceIdType.MESH
ANY = pl.BlockSpec(memory_space=pl.ANY)


def _pc(body, **kw):
    return pl.pallas_call(body, **kw)


def _params(*sem):
    return pltpu.CompilerParams(dimension_semantics=sem if sem else None, vmem_limit_bytes=VMEM_LIMIT_V7X)


def _tile(t, big=False):
    if big and t >= 1024:
        return 512
    return 256 if t >= 256 else t


def _sds(shape, dtype=F32):
    return jax.ShapeDtypeStruct(tuple(shape), dtype)


def _full(shape):
    nd = len(shape)
    return pl.BlockSpec(tuple(shape), lambda *_: (0,) * nd)


def _sigmoid(x):
    return 1.0 / (1.0 + jnp.exp(-x))


def _log1p_pos(y):
    s = y * (1.0 - y * (0.5 - y * (1.0 / 3.0 - y * (0.25 - y * (0.2 - y / 6.0)))))
    return jnp.where(y < 0.03, s, jnp.log(1.0 + y))


def _softplus(z):
    return jnp.maximum(z, 0.0) + _log1p_pos(jnp.exp(-jnp.abs(z)))


def _neg_expm1(x, exp_x):
    t = x * (1.0 + x * (0.5 + x * (1.0 / 6.0 + x * (1.0 / 24.0 + x * (1.0 / 120.0 + x * (1.0 / 720.0 + x / 5040.0))))))
    return -jnp.where(x > -0.25, t, exp_x - 1.0)


def _rms(x):
    r = lax.rsqrt(jnp.mean(x * x, axis=-1, keepdims=True) + EPS)
    return x * r, r


def _dot(a, b):
    return jnp.dot(a, b, preferred_element_type=F32)


def _dot_nt(a, b):
    return lax.dot_general(a, b, (((1,), (1,)), ((), ())), preferred_element_type=F32)


def _dot_tn(a, b):
    return lax.dot_general(a, b, (((0,), (0,)), ((), ())), preferred_element_type=F32)


def _sum0(x):
    return jnp.sum(x, axis=0, keepdims=True)


def _norm_mod_bwd(x, g, sc, dh):
    xh, r = _rms(x)
    hn = xh * g
    dhn = dh * (1.0 + sc)
    dxh = dhn * g
    dx = r * (dxh - xh * jnp.mean(dxh * xh, axis=-1, keepdims=True))
    return dx, _sum0(dhn * xh), _sum0(dh), _sum0(dh * hn)


def _dev_index(p):
    return 4 * p[0] + 2 * p[1] + p[2]


def _mesh_pos():
    return lax.axis_index("x"), lax.axis_index("y"), lax.axis_index("c")


class _AllGather:
    def __init__(self, arrs):
        n = len(arrs)
        self.arrays = list(arrs)
        self.out_shapes = [_sds((N_DEV,) + a.shape, a.dtype) for a in arrs]
        self.scratch = ([pltpu.VMEM(a.shape, a.dtype) for a in arrs]
                        + [pltpu.SemaphoreType.DMA((7 * n,)), pltpu.SemaphoreType.DMA((7 * n,)),
                           pltpu.SemaphoreType.DMA((n,))])
        self.aliases = {}

    def _parts(self, ins, outs, scr):
        n = len(self.arrays)
        stage = scr[:n]
        send_sems, recv_sems, local_sems = scr[n:]
        x, y, c = _mesh_pos()
        me, sib = (x, y, c), (x, y, 1 - c)
        chips = [(1 - x, y), (x, 1 - y), (1 - x, 1 - y)]

        def copy(t, k, block, to, own=False):
            dst = outs[t].at[_dev_index(block)]
            return pltpu.make_async_remote_copy(
                src_ref=ins[t] if own else dst, dst_ref=dst,
                send_sem=send_sems.at[7 * t + k], recv_sem=recv_sems.at[7 * t + k],
                device_id=to, device_id_type=MESH)

        first = []
        for t in range(n):
            first.append(copy(t, 0, me, sib, own=True))
            for j, ch in enumerate(chips):
                first.append(copy(t, 1 + j, me, (*ch, c), own=True))
        stage_in = [pltpu.make_async_copy(ins[t], stage[t], local_sems.at[t]) for t in range(n)]
        mine = [pltpu.make_async_copy(stage[t], outs[t].at[_dev_index(me)], local_sems.at[t]) for t in range(n)]
        return n, c, me, sib, chips, copy, first, stage_in, mine

    def start(self, ins, outs, scr):
        n, _, _, _, _, _, first, stage_in, mine = self._parts(ins, outs, scr)
        for cp in stage_in:
            cp.start()
        for cp in first:
            cp.start()
        for t in range(n):
            stage_in[t].wait()
            mine[t].start()

    def finish(self, ins, outs, scr):
        n, c, me, sib, chips, copy, first, _, mine = self._parts(ins, outs, scr)
        passed = []
        for j, ch in enumerate(chips):
            for t in range(n):
                copy(t, 1 + j, (*ch, c), me).wait_recv()
                p = copy(t, 4 + j, (*ch, c), sib)
                p.start()
                passed.append(p)
        for t in range(n):
            copy(t, 0, sib, me).wait_recv()
            for j, ch in enumerate(chips):
                copy(t, 4 + j, (*ch, 1 - c), me).wait_recv()
        for cp in first + passed:
            cp.wait_send()
        for cp in mine:
            cp.wait()


class _Exchange:
    def __init__(self, arrays, out_shapes, plan, n_copies, aliases=None):
        self.arrays = list(arrays)
        self.out_shapes = list(out_shapes)
        self.plan = plan
        self.scratch = [pltpu.SemaphoreType.DMA((n_copies,)), pltpu.SemaphoreType.DMA((n_copies,))]
        self.aliases = aliases or {}

    def _copies(self, ins, outs, scr):
        send_sems, recv_sems = scr
        snd, rcv = [], []
        for i, (src, dst, peer, lands) in enumerate(self.plan(ins, outs, _mesh_pos())):
            kw = dict(send_sem=send_sems.at[i], recv_sem=recv_sems.at[i], device_id=peer, device_id_type=MESH)
            snd.append(pltpu.make_async_remote_copy(src_ref=src, dst_ref=dst, **kw))
            rcv.append(pltpu.make_async_remote_copy(src_ref=src, dst_ref=lands, **kw))
        return snd, rcv

    def start(self, ins, outs, scr):
        for cp in self._copies(ins, outs, scr)[0]:
            cp.start()

    def finish(self, ins, outs, scr):
        snd, rcv = self._copies(ins, outs, scr)
        for cp in rcv:
            cp.wait_recv()
        for cp in snd:
            cp.wait_send()


def _pair_exchange(grads):
    n = len(grads)

    def plan(ins, outs, pos):
        x, y, c = pos
        return [(ins[t].at[2 * j + (1 - c)], outs[t].at[j], (x, y, 1 - c), outs[t].at[j])
                for t in range(n) for j in range(N_CHIP)]

    return _Exchange(grads, [_sds((N_CHIP,) + g.shape[1:], g.dtype) for g in grads], plan, N_CHIP * n)


def _chip_exchange(parts):
    n = len(parts)

    def plan(ins, outs, pos):
        x, y, c = pos
        chips = [(1 - x, y), (x, 1 - y), (1 - x, 1 - y)]
        return [(ins[t].at[2 * ch[0] + ch[1]], outs[t].at[k], (*ch, c), outs[t].at[k])
                for t in range(n) for k, ch in enumerate(chips)]

    return _Exchange(parts, [_sds((3,) + p.shape[1:], p.dtype) for p in parts], plan, 3 * n)


def _pair_gather(bufs):
    n = len(bufs)

    def plan(ins, outs, pos):
        x, y, c = pos
        return [(ins[t].at[c], outs[t].at[c], (x, y, 1 - c), outs[t].at[1 - c]) for t in range(n)]

    return _Exchange(bufs, [_sds(b.shape, b.dtype) for b in bufs], plan, n, aliases={t: t for t in range(n)})


def _run_comm(comm, name):
    n_in, n_out = len(comm.arrays), len(comm.out_shapes)

    def body(*refs):
        ins, outs, scr = refs[:n_in], refs[n_in:n_in + n_out], refs[n_in + n_out:]
        comm.start(ins, outs, scr)
        comm.finish(ins, outs, scr)

    outs = _pc(body, name=name, out_shape=comm.out_shapes, in_specs=[ANY] * n_in, out_specs=[ANY] * n_out,
               input_output_aliases=dict(comm.aliases), scratch_shapes=comm.scratch,
               compiler_params=_params())(*comm.arrays)
    return list(outs)


def _all_gather(arrs, name):
    return _run_comm(_AllGather(arrs), name)


def _call(body, *, name, grid, in_specs, out_specs, out_shape, scratch_shapes, sem, args, comms=()):
    n_in, n_out, n_scr = len(in_specs), len(out_specs), len(scratch_shapes)
    c_in = [len(cm.arrays) for cm in comms]
    c_out = [len(cm.out_shapes) for cm in comms]
    c_scr = [len(cm.scratch) for cm in comms]
    aliases = {}
    for k, cm in enumerate(comms):
        for a, b in cm.aliases.items():
            aliases[n_in + sum(c_in[:k]) + a] = n_out + sum(c_out[:k]) + b

    def split(refs, counts):
        out, pos = [], 0
        for cnt in counts:
            out.append(refs[pos:pos + cnt])
            pos += cnt
        return out

    def wrapped(*refs):
        ins = refs[:n_in + sum(c_in)]
        outs = refs[len(ins):len(ins) + n_out + sum(c_out)]
        scr = refs[len(ins) + len(outs):]
        cins, couts, cscr = split(ins[n_in:], c_in), split(outs[n_out:], c_out), split(scr[n_scr:], c_scr)
        if comms:
            first = pl.program_id(0) == 0
            last = pl.program_id(0) == grid[0] - 1
            for k in range(1, len(grid)):
                first = jnp.logical_and(first, pl.program_id(k) == 0)
                last = jnp.logical_and(last, pl.program_id(k) == grid[k] - 1)

            @pl.when(first)
            def _():
                for k, cm in enumerate(comms):
                    cm.start(cins[k], couts[k], cscr[k])
        body(*ins[:n_in], *outs[:n_out], *scr[:n_scr])
        if comms:
            @pl.when(last)
            def _():
                for k, cm in enumerate(comms):
                    cm.finish(cins[k], couts[k], cscr[k])

    outs = _pc(wrapped, name=name, grid=grid,
               in_specs=list(in_specs) + [ANY] * sum(c_in), out_specs=list(out_specs) + [ANY] * sum(c_out),
               out_shape=list(out_shape) + [s for cm in comms for s in cm.out_shapes],
               scratch_shapes=list(scratch_shapes) + [s for cm in comms for s in cm.scratch],
               input_output_aliases=aliases, compiler_params=_params(*sem),
               )(*args, *[a for cm in comms for a in cm.arrays])
    outs = list(outs)
    return outs[:n_out], split(outs[n_out:], c_out)


def _row_block(r):
    for b in (512, 256, 128, 64, 32, 16, 8):
        if r % b == 0:
            return b
    return r


def _pair_add(g, recv, c_idx, name):
    _, r, cc = g.shape
    br = _row_block(r)

    def body(c_ref, g_ref, r_ref, p_ref, pb_ref):
        s = g_ref[...] + r_ref[...]
        p_ref[...] = s
        pb_ref[...] = s.astype(BF16)

    grid_spec = pltpu.PrefetchScalarGridSpec(
        num_scalar_prefetch=1, grid=(N_CHIP, r // br),
        in_specs=[pl.BlockSpec((1, br, cc), lambda j, i, c_ref: (2 * j + c_ref[0], i, 0)),
                  pl.BlockSpec((1, br, cc), lambda j, i, c_ref: (j, i, 0))],
        out_specs=[pl.BlockSpec((1, br, cc), lambda j, i, c_ref: (j, i, 0)),
                   pl.BlockSpec((1, br, cc), lambda j, i, c_ref: (j, i, 0))])
    return _pc(body, name=name, grid_spec=grid_spec,
               out_shape=[_sds((N_CHIP, r, cc)), _sds((N_CHIP, r, cc), BF16)],
               compiler_params=_params("arbitrary", "arbitrary"))(c_idx, g, recv)


def _chip_add(p, q, jc_idx, name):
    _, r, cc = p.shape
    br = _row_block(r)

    def body(jc_ref, p_ref, q_ref, o_ref):
        o_ref[0] = ((p_ref[0] + q_ref[0].astype(F32)) + q_ref[1].astype(F32)) + q_ref[2].astype(F32)

    grid_spec = pltpu.PrefetchScalarGridSpec(
        num_scalar_prefetch=1, grid=(r // br,),
        in_specs=[pl.BlockSpec((1, br, cc), lambda i, jc_ref: (jc_ref[0], i, 0)),
                  pl.BlockSpec((3, br, cc), lambda i, jc_ref: (0, i, 0))],
        out_specs=pl.BlockSpec((1, br, cc), lambda i, jc_ref: (jc_ref[1], i, 0)))
    return _pc(body, name=name, grid_spec=grid_spec, out_shape=_sds((2, r, cc)),
               compiler_params=_params("arbitrary"))(jc_idx, p, q)


def _shard_of(both):
    return both.reshape((2 * both.shape[1],) + both.shape[2:])


def _adamw(w, g, m, v, name, comms=()):
    r, cc = w.shape
    br = _row_block(r)
    if r * cc * 4 <= (1 << 20):
        br = r
    elif br * cc * 4 > (1 << 20) and br > 8:
        br = max(8, (1 << 20) // (cc * 4) // 8 * 8)
        while r % br:
            br -= 8
    c1 = 1.0 - ADAM_B1 ** ADAM_STEP
    c2 = 1.0 - ADAM_B2 ** ADAM_STEP

    def body(w_ref, g_ref, m_ref, v_ref, d_ref, mo_ref, vo_ref):
        gg = g_ref[...]
        mn = ADAM_B1 * m_ref[...] + (1.0 - ADAM_B1) * gg
        vn = ADAM_B2 * v_ref[...] + (1.0 - ADAM_B2) * (gg * gg)
        mh = mn / c1
        vh = vn / c2
        d_ref[...] = -ADAM_LR * (mh / (jnp.sqrt(vh) + ADAM_EPS) + ADAM_WD * w_ref[...])
        mo_ref[...] = mn
        vo_ref[...] = vn

    spec = pl.BlockSpec((br, cc), lambda i: (i, 0))
    outs, couts = _call(body, name=name, grid=(r // br,), in_specs=[spec] * 4, out_specs=[spec] * 3,
                        out_shape=[_sds((r, cc))] * 3, scratch_shapes=[], sem=("arbitrary",), args=(w, g, m, v),
                        comms=comms)
    return (outs, couts) if comms else outs


def _prep(c_all, c_ctx, ret_decay):
    def body(c_ref, cc_ref, rd_ref, a_ref, lg_ref, sg_ref):
        ca = c_ref[...]
        cc = cc_ref[...]
        a_ref[...] = jnp.zeros_like(a_ref)
        a_ref[0:8, :] = ca * _sigmoid(ca)
        a_ref[8:9, :] = cc * _sigmoid(cc)
        rd = rd_ref[...]
        lg_ref[...] = -_softplus(-rd)
        sg_ref[...] = _sigmoid(-rd)

    rd = jnp.broadcast_to(ret_decay.reshape(2, HEADS).T[:, :, None], (HEADS, 2, LANES))
    return _pc(body, name="prep",
               out_shape=[_sds((16, D_MODEL)), _sds((HEADS, 2, LANES)), _sds((HEADS, 2, LANES))],
               in_specs=[_full((8, D_MODEL)), _full((1, D_MODEL)), _full((HEADS, 2, LANES))],
               out_specs=[_full((16, D_MODEL)), _full((HEADS, 2, LANES)), _full((HEADS, 2, LANES))],
               compiler_params=_params())(c_all, c_ctx.reshape(1, D_MODEL), rd)


def _mod_fwd(a16, w_ada, b_shard):
    n = w_ada.shape[1]
    bn = 512

    def body(a_ref, w_ref, b_ref, o_ref):
        o_ref[...] = jnp.dot(a_ref[...], w_ref[...], preferred_element_type=F32,
                             precision=lax.Precision.HIGHEST) + b_ref[...]

    return _pc(body, name="mod_fwd", grid=(n // bn,),
               in_specs=[_full((16, D_MODEL)), pl.BlockSpec((D_MODEL, bn), lambda i: (0, i)),
                         pl.BlockSpec((1, bn), lambda i: (0, i))],
               out_specs=pl.BlockSpec((16, bn), lambda i: (0, i)), out_shape=_sds((16, n)),
               compiler_params=_params("arbitrary"))(a16, w_ada, b_shard)


def _ada_grad(at, b):
    n = b.shape[1]
    bn = 512

    def body(a_ref, b_ref, o_ref):
        o_ref[...] = jnp.dot(a_ref[...], b_ref[...], preferred_element_type=F32, precision=lax.Precision.HIGHEST)

    return _pc(body, name="ada_grad", grid=(n // bn,),
               in_specs=[_full((D_MODEL, LANES)), pl.BlockSpec((LANES, bn), lambda i: (0, i))],
               out_specs=pl.BlockSpec((D_MODEL, bn), lambda i: (0, i)), out_shape=_sds((D_MODEL, n)),
               compiler_params=_params("arbitrary"))(at, b)


def _cctx_partial(dmc8, w_ada):
    n = w_ada.shape[1]
    bn = 512

    def body(d_ref, w_ref, o_ref):
        @pl.when(pl.program_id(0) == 0)
        def _():
            o_ref[...] = jnp.zeros_like(o_ref)
        o_ref[...] += lax.dot_general(d_ref[...], w_ref[...], (((1,), (1,)), ((), ())),
                                      preferred_element_type=F32, precision=lax.Precision.HIGHEST)

    return _pc(body, name="cctx_partial", grid=(n // bn,),
               in_specs=[pl.BlockSpec((8, bn), lambda i: (0, i)), pl.BlockSpec((D_MODEL, bn), lambda i: (0, i))],
               out_specs=_full((8, D_MODEL)), out_shape=_sds((8, D_MODEL)),
               compiler_params=_params("arbitrary"))(dmc8, w_ada)


def _cctx_final(parts, c_ctx, m, v):
    c1 = 1.0 - ADAM_B1 ** ADAM_STEP
    c2 = 1.0 - ADAM_B2 ** ADAM_STEP

    def body(p_ref, c_ref, m_ref, v_ref, g_ref, d_ref, mo_ref, vo_ref):
        s = ((p_ref[0, 0:1, :] + p_ref[2, 0:1, :]) + p_ref[4, 0:1, :]) + p_ref[6, 0:1, :]
        z = c_ref[...]
        sg = _sigmoid(z)
        gg = s * (sg * (1.0 + z * (1.0 - sg)))
        g_ref[...] = gg
        mn = ADAM_B1 * m_ref[...] + (1.0 - ADAM_B1) * gg
        vn = ADAM_B2 * v_ref[...] + (1.0 - ADAM_B2) * (gg * gg)
        d_ref[...] = -ADAM_LR * ((mn / c1) / (jnp.sqrt(vn / c2) + ADAM_EPS) + ADAM_WD * z)
        mo_ref[...] = mn
        vo_ref[...] = vn

    row = _full((1, D_MODEL))
    return _pc(body, name="cctx_final", out_shape=[_sds((1, D_MODEL))] * 4,
               in_specs=[_full(parts.shape), row, row, row], out_specs=[row] * 4,
               compiler_params=_params())(parts, c_ctx.reshape(1, D_MODEL), m.reshape(1, D_MODEL), v.reshape(1, D_MODEL))


def _rotary_tables(t_len):
    rows = t_len // GRID_W
    row = jnp.repeat(jnp.arange(rows, dtype=F32), GRID_W)
    col = jnp.tile(jnp.arange(GRID_W, dtype=F32), rows)
    n_freq = DH // 4
    inv = ROPE_BASE ** (-jnp.arange(n_freq, dtype=F32) / n_freq)
    ang = jnp.concatenate([row[:, None] * inv, col[:, None] * inv], axis=-1)
    cos, sin = jnp.cos(ang), jnp.sin(ang)
    return jnp.concatenate([cos, cos], axis=-1), jnp.concatenate([-sin, sin], axis=-1)


def _inproj_fwd(x, gn, sh, sc, w4, cos2, sin2, name, comms=()):
    t = x.shape[0]
    tm = _tile(t, True)
    nc = IN_COLS // N_CHIP

    def body(x_ref, gn_ref, sh_ref, sc_ref, w_ref, c_ref, s_ref, p_ref, xr_ref, hb_ref, p_s):
        xh, _ = _rms(x_ref[...])
        h = xh * gn_ref[...] * (1.0 + sc_ref[...]) + sh_ref[...]
        hb = h.astype(BF16)
        hb_ref[...] = hb
        for j in range(N_CHIP):
            p_s[:, nc * j:nc * (j + 1)] = _dot(hb, w_ref[j])
        cc = c_ref[...]
        ss = s_ref[...]
        for hh in range(2 * HEADS):
            blk = p_s[:, DH * hh:DH * (hh + 1)]
            rot = blk * cc + pltpu.roll(blk, DH // 2, 1) * ss
            if hh >= HEADS:
                rot = rot * K_SCALE
            p_ref[:, DH * hh:DH * (hh + 1)] = rot.astype(BF16)
        p_ref[:, 2 * RET_W:] = p_s[:, 2 * RET_W:].astype(BF16)
        xr_ref[...] = p_s[:, 4 * RET_W:4 * RET_W + LRU_W]

    row = _full((1, D_MODEL))
    outs, couts = _call(
        body, name=name, grid=(t // tm,),
        in_specs=[pl.BlockSpec((tm, D_MODEL), lambda i: (i, 0)), row, row, row, _full(w4.shape),
                  pl.BlockSpec((tm, DH), lambda i: (i, 0)), pl.BlockSpec((tm, DH), lambda i: (i, 0))],
        out_specs=[pl.BlockSpec((tm, IN_COLS), lambda i: (i, 0)), pl.BlockSpec((tm, LRU_W), lambda i: (i, 0)),
                   pl.BlockSpec((tm, D_MODEL), lambda i: (i, 0))],
        out_shape=[_sds((t, IN_COLS), BF16), _sds((t, LRU_W)), _sds((t, D_MODEL), BF16)],
        scratch_shapes=[pltpu.VMEM((tm, IN_COLS), F32)], sem=("arbitrary",),
        args=(x, gn, sh, sc, w4, cos2, sin2), comms=comms)
    return (outs, couts) if comms else outs


def _inproj_bwd(x, gn, sh, sc, w4, cos2, sin2, pieces, dres, name):
    t = x.shape[0]
    tm = _tile(t)
    nc = IN_COLS // N_CHIP

    def body(x_ref, gn_ref, sh_ref, sc_ref, w_ref, c_ref, s_ref, dqf, dqb, dkf, dkb, dvf, dvb, dg, dxr, dgt, dres_ref,
             dx_ref, dpb_ref, dgn_ref, dsh_ref, dsc_ref):
        cc = c_ref[...]
        ss = s_ref[...]
        dq = dqf[...].astype(F32) + dqb[...].astype(F32)
        dk = dkf[...].astype(F32) + dkb[...].astype(F32)
        for hh in range(HEADS):
            sl = slice(DH * hh, DH * (hh + 1))
            b = dq[:, sl]
            dpb_ref[:, sl] = (b * cc + pltpu.roll(b * ss, DH // 2, 1)).astype(BF16)
            b = dk[:, sl]
            dpb_ref[:, RET_W + DH * hh:RET_W + DH * (hh + 1)] = (
                (b * cc + pltpu.roll(b * ss, DH // 2, 1)) * K_SCALE).astype(BF16)
        dpb_ref[:, 2 * RET_W:3 * RET_W] = (dvf[...].astype(F32) + dvb[...].astype(F32)).astype(BF16)
        dpb_ref[:, 3 * RET_W:4 * RET_W] = dg[...].astype(BF16)
        dpb_ref[:, 4 * RET_W:4 * RET_W + LRU_W] = dxr[...].astype(BF16)
        dpb_ref[:, 4 * RET_W + LRU_W:IN_COLS] = dgt[...].astype(BF16)
        dh = _dot_nt(dpb_ref[:, 0:nc], w_ref[0])
        for j in range(1, N_CHIP):
            dh = dh + _dot_nt(dpb_ref[:, nc * j:nc * (j + 1)], w_ref[j])
        dx, dgn_t, dsh_t, dsc_t = _norm_mod_bwd(x_ref[...], gn_ref[...], sc_ref[...], dh)
        dx_ref[...] = dres_ref[...] + dx

        @pl.when(pl.program_id(0) == 0)
        def _():
            dgn_ref[...] = jnp.zeros_like(dgn_ref)
            dsh_ref[...] = jnp.zeros_like(dsh_ref)
            dsc_ref[...] = jnp.zeros_like(dsc_ref)
        dgn_ref[...] += dgn_t
        dsh_ref[...] += dsh_t
        dsc_ref[...] += dsc_t

    row = _full((1, D_MODEL))
    pc = pl.BlockSpec((tm, RET_W), lambda i: (i, 0))
    big = pl.BlockSpec((tm, D_MODEL), lambda i: (i, 0))
    return _pc(body, name=name, grid=(t // tm,),
               in_specs=[big, row, row, row, _full(w4.shape),
                         pl.BlockSpec((tm, DH), lambda i: (i, 0)), pl.BlockSpec((tm, DH), lambda i: (i, 0))]
               + [pc] * 9 + [big],
               out_specs=[big, pl.BlockSpec((tm, IN_COLS), lambda i: (i, 0)), row, row, row],
               out_shape=[_sds((t, D_MODEL)), _sds((t, IN_COLS), BF16), _sds((1, D_MODEL)), _sds((1, D_MODEL)),
                          _sds((1, D_MODEL))],
               compiler_params=_params("arbitrary"))(x, gn, sh, sc, w4, cos2, sin2, *pieces, dres)


def _halo_specs(t, tm):
    n8 = tm // SUBLANES
    last8 = t // SUBLANES - 1
    prev = pl.BlockSpec((SUBLANES, LRU_W), lambda i: (jnp.maximum(i * n8 - 1, 0), 0))
    main = pl.BlockSpec((tm, LRU_W), lambda i: (i, 0))
    nxt = pl.BlockSpec((SUBLANES, LRU_W), lambda i: (jnp.minimum((i + 1) * n8, last8), 0))
    return prev, main, nxt


def _with_halo(prev_ref, main_ref, next_ref, i, nt):
    prev = jnp.where(i > 0, prev_ref[...], 0.0)
    nxt = jnp.where(i < nt - 1, next_ref[...], 0.0)
    return jnp.concatenate([prev, main_ref[...], nxt], axis=0)


def _conv_fwd(xr, cw, cb, name):
    t = xr.shape[0]
    tm = _tile(t, True)
    nt = t // tm
    n = tm + 2 * SUBLANES
    mid = slice(SUBLANES, SUBLANES + tm)

    def body(p_ref, m_ref, n_ref, w_ref, b_ref, o_ref):
        xp = _with_halo(p_ref, m_ref, n_ref, pl.program_id(0), nt)
        acc = b_ref[...] + pltpu.roll(xp, 1, 0)[mid] * w_ref[0:1, :]
        acc = acc + xp[mid] * w_ref[1:2, :]
        acc = acc + pltpu.roll(xp, n - 1, 0)[mid] * w_ref[2:3, :]
        acc = acc + pltpu.roll(xp, n - 2, 0)[mid] * w_ref[3:4, :]
        o_ref[...] = acc

    return _pc(body, name=name, grid=(nt,),
               in_specs=[*_halo_specs(t, tm), _full((4, LRU_W)), _full((1, LRU_W))],
               out_specs=pl.BlockSpec((tm, LRU_W), lambda i: (i, 0)), out_shape=_sds((t, LRU_W)),
               compiler_params=_params("arbitrary"))(xr, xr, xr, cw, cb)


def _conv_bwd(dxc_a, dxc_b, xr, cw, name):
    t = xr.shape[0]
    tm = _tile(t, True)
    nt = t // tm
    n = tm + 2 * SUBLANES
    mid = slice(SUBLANES, SUBLANES + tm)

    def body(ap_ref, am_ref, an_ref, bp_ref, bm_ref, bn_ref, xp_ref, xm_ref, xn_ref, w_ref, dx_ref, dw_ref, db_ref):
        i = pl.program_id(0)
        dp = _with_halo(ap_ref, am_ref, an_ref, i, nt) + _with_halo(bp_ref, bm_ref, bn_ref, i, nt)
        xp = _with_halo(xp_ref, xm_ref, xn_ref, i, nt)
        dx = pltpu.roll(dp, n - 1, 0)[mid] * w_ref[0:1, :]
        dx = dx + dp[mid] * w_ref[1:2, :]
        dx = dx + pltpu.roll(dp, 1, 0)[mid] * w_ref[2:3, :]
        dx = dx + pltpu.roll(dp, 2, 0)[mid] * w_ref[3:4, :]
        dx_ref[...] = dx.astype(BF16)
        d = dp[mid]

        @pl.when(i == 0)
        def _():
            dw_ref[...] = jnp.zeros_like(dw_ref)
            db_ref[...] = jnp.zeros_like(db_ref)
        dw_ref[0:1, :] += _sum0(d * pltpu.roll(xp, 1, 0)[mid])
        dw_ref[1:2, :] += _sum0(d * xp[mid])
        dw_ref[2:3, :] += _sum0(d * pltpu.roll(xp, n - 1, 0)[mid])
        dw_ref[3:4, :] += _sum0(d * pltpu.roll(xp, n - 2, 0)[mid])
        db_ref[...] += _sum0(d)

    return _pc(body, name=name, grid=(nt,),
               in_specs=[*_halo_specs(t, tm), *_halo_specs(t, tm), *_halo_specs(t, tm), _full((4, LRU_W))],
               out_specs=[pl.BlockSpec((tm, LRU_W), lambda i: (i, 0)), _full((4, LRU_W)), _full((1, LRU_W))],
               out_shape=[_sds((t, LRU_W), BF16), _sds((4, LRU_W)), _sds((1, LRU_W))],
               compiler_params=_params("arbitrary"))(dxc_a, dxc_a, dxc_a, dxc_b, dxc_b, dxc_b, xr, xr, xr, cw)


def _local_scan(a, b, reverse):
    n = a.shape[0]
    row = lax.broadcasted_iota(jnp.int32, a.shape, 0) & (SUBLANES - 1)
    for s in (1, 2, 4):
        if reverse:
            a_s, b_s, ok = pltpu.roll(a, n - s, 0), pltpu.roll(b, n - s, 0), row < SUBLANES - s
        else:
            a_s, b_s, ok = pltpu.roll(a, s, 0), pltpu.roll(b, s, 0), row >= s
        b = a * jnp.where(ok, b_s, 0.0) + b
        a = a * jnp.where(ok, a_s, 1.0)
    return a, b


def _carry_scan(a_s, b_s, out_ref, carry, reverse):
    ng = a_s.shape[0] // SUBLANES
    shape = carry.shape

    def step(g, cr):
        gg = (ng - 1 - g) if reverse else g
        off = pl.multiple_of(gg * SUBLANES, SUBLANES)
        h = a_s[pl.ds(off, SUBLANES), :] * cr + b_s[pl.ds(off, SUBLANES), :]
        out_ref[pl.ds(off, SUBLANES), :] = h
        edge = h[0:1, :] if reverse else h[SUBLANES - 1:SUBLANES, :]
        return jnp.broadcast_to(edge, shape)

    return lax.fori_loop(0, ng, step, carry)


def _lru_gates(xc, wa_ref, wx_ref, ba, bx, lam):
    xb = xc.astype(BF16)
    r = _sigmoid(_dot(xb, wa_ref[...]) + ba)
    ig = _sigmoid(_dot(xb, wx_ref[...]) + bx)
    sp = _softplus(-lam)
    la = -LRU_C * r * sp
    a = jnp.exp(la)
    mult = jnp.sqrt(_neg_expm1(2.0 * la, a * a))
    return r, ig, sp, a, mult


def _lru_fwd(xc, wa, wx, ba, bx, lam, h0, reverse, name, comms=()):
    t = xc.shape[0]
    tm = _tile(t, True)
    nt = t // tm
    tidx = (lambda i: (nt - 1 - i, 0)) if reverse else (lambda i: (i, 0))

    def body(x_ref, wa_ref, wx_ref, ba_ref, bx_ref, lam_ref, h0_ref, h_ref, a_s, b_s, c_s):
        @pl.when(pl.program_id(0) == 0)
        def _():
            c_s[...] = jnp.broadcast_to(h0_ref[...], c_s.shape)
        xv = x_ref[...]
        _, ig, _, a, mult = _lru_gates(xv, wa_ref, wx_ref, ba_ref[...], bx_ref[...], lam_ref[...])
        al, bl = _local_scan(a, mult * (ig * xv), reverse)
        a_s[...] = al
        b_s[...] = bl
        c_s[...] = _carry_scan(a_s, b_s, h_ref, c_s[...], reverse)

    vec = _full((1, LRU_W))
    mat = _full((LRU_W, LRU_W))
    (h,), couts = _call(body, name=name, grid=(nt,),
                        in_specs=[pl.BlockSpec((tm, LRU_W), tidx), mat, mat, vec, vec, vec, vec],
                        out_specs=[pl.BlockSpec((tm, LRU_W), tidx)], out_shape=[_sds((t, LRU_W))],
                        scratch_shapes=[pltpu.VMEM((tm, LRU_W), F32), pltpu.VMEM((tm, LRU_W), F32),
                                        pltpu.VMEM((SUBLANES, LRU_W), F32)],
                        sem=("arbitrary",), args=(xc, wa, wx, ba, bx, lam, h0), comms=comms)
    return (h, couts) if comms else h


def _lru_bwd(xc, wa, wx, ba, bx, lam, h, h0, dh, reverse, name, comms=()):
    t = xc.shape[0]
    tm = _tile(t, True)
    nt = t // tm
    n8 = tm // SUBLANES
    last8 = t // SUBLANES - 1
    tidx = (lambda i: (i, 0)) if reverse else (lambda i: (nt - 1 - i, 0))
    if reverse:
        halo = pl.BlockSpec((SUBLANES, LRU_W), lambda i: (jnp.minimum((i + 1) * n8, last8), 0))
    else:
        halo = pl.BlockSpec((SUBLANES, LRU_W), lambda i: (jnp.maximum((nt - 1 - i) * n8 - 1, 0), 0))

    def body(x_ref, wa_ref, wx_ref, ba_ref, bx_ref, lam_ref, h_ref, halo_ref, h0_ref, dh_ref,
             dx_ref, dpre_ref, dba_ref, dbx_ref, dlam_ref, dh0_ref, a_s, b_s, l_s, c_s, e_s):
        i = pl.program_id(0)

        @pl.when(i == 0)
        def _():
            c_s[...] = jnp.zeros_like(c_s)
            e_s[...] = jnp.zeros_like(e_s)
            dba_ref[...] = jnp.zeros_like(dba_ref)
            dbx_ref[...] = jnp.zeros_like(dbx_ref)
            dlam_ref[...] = jnp.zeros_like(dlam_ref)
        xv = x_ref[...]
        lam = lam_ref[...]
        r, ig, sp, a, mult = _lru_gates(xv, wa_ref, wx_ref, ba_ref[...], bx_ref[...], lam)
        hv = h_ref[...]
        rowi = lax.broadcasted_iota(jnp.int32, (tm, LRU_W), 0)
        edge_a = jnp.broadcast_to(e_s[0:1, :], (tm, LRU_W))
        h0b = jnp.broadcast_to(h0_ref[...], (tm, LRU_W))
        if reverse:
            a_sh = jnp.where(rowi == 0, edge_a, pltpu.roll(a, 1, 0))
            hin_edge = jnp.where(i == nt - 1, h0b, jnp.broadcast_to(halo_ref[0:1, :], (tm, LRU_W)))
            h_in = jnp.where(rowi == tm - 1, hin_edge, pltpu.roll(hv, tm - 1, 0))
        else:
            a_sh = jnp.where(rowi == tm - 1, edge_a, pltpu.roll(a, tm - 1, 0))
            hin_edge = jnp.where(i == nt - 1, h0b, jnp.broadcast_to(halo_ref[SUBLANES - 1:SUBLANES, :], (tm, LRU_W)))
            h_in = jnp.where(rowi == 0, hin_edge, pltpu.roll(hv, 1, 0))
        al, bl = _local_scan(a_sh, dh_ref[...], not reverse)
        a_s[...] = al
        b_s[...] = bl
        c_s[...] = _carry_scan(a_s, b_s, l_s, c_s[...], not reverse)
        e_s[...] = jnp.broadcast_to(a[tm - 1:tm, :] if reverse else a[0:1, :], e_s.shape)
        lmb = l_s[...]
        da = lmb * h_in
        ixc = ig * xv
        dmult = lmb * ixc
        dixc = lmb * mult
        dla = da * a - dmult * (a * a) / mult
        dpr = dla * (-LRU_C * sp) * r * (1.0 - r)
        dpi = dixc * xv * ig * (1.0 - ig)
        dprb = dpr.astype(BF16)
        dpib = dpi.astype(BF16)
        dpre_ref[:, 0:LRU_W] = dprb
        dpre_ref[:, LRU_W:2 * LRU_W] = dpib
        dx_ref[...] = dixc * ig + _dot_nt(dprb, wa_ref[...]) + _dot_nt(dpib, wx_ref[...])
        dba_ref[...] += _sum0(dpr)
        dbx_ref[...] += _sum0(dpi)
        dlam_ref[...] += _sum0(dla * (-LRU_C * r)) * (-_sigmoid(-lam))

        @pl.when(i == nt - 1)
        def _():
            al0 = a * lmb
            dh0_ref[...] = al0[tm - 1:tm, :] if reverse else al0[0:1, :]

    vec = _full((1, LRU_W))
    mat = _full((LRU_W, LRU_W))
    tile = pl.BlockSpec((tm, LRU_W), tidx)
    return _call(body, name=name, grid=(nt,),
                 in_specs=[tile, mat, mat, vec, vec, vec, tile, halo, vec, tile],
                 out_specs=[tile, pl.BlockSpec((tm, 2 * LRU_W), tidx), vec, vec, vec, vec],
                 out_shape=[_sds((t, LRU_W)), _sds((t, 2 * LRU_W), BF16), _sds((1, LRU_W)), _sds((1, LRU_W)),
                            _sds((1, LRU_W)), _sds((1, LRU_W))],
                 scratch_shapes=[pltpu.VMEM((tm, LRU_W), F32), pltpu.VMEM((tm, LRU_W), F32),
                                 pltpu.VMEM((tm, LRU_W), F32), pltpu.VMEM((SUBLANES, LRU_W), F32),
                                 pltpu.VMEM((SUBLANES, LRU_W), F32)],
                 sem=("arbitrary",), args=(xc, wa, wx, ba, bx, lam, h, h, h0, dh), comms=comms)


def _decay_tables(lg, reverse):
    ci = lax.broadcasted_iota(jnp.int32, (CHUNK, CHUNK), 0).astype(F32)
    mi = lax.broadcasted_iota(jnp.int32, (CHUNK, CHUNK), 1).astype(F32)
    if reverse:
        rel, pq, ps = mi - ci, CHUNK - ci, ci
    else:
        rel, pq, ps = ci - mi, ci + 1.0, CHUNK - 1.0 - ci
    relc = jnp.maximum(rel, 0.0)
    dm = jnp.where(rel >= 0, jnp.exp(lg * relc), 0.0)
    return relc, dm, jnp.exp(lg * pq), jnp.exp(lg * ps), jnp.exp(lg * float(CHUNK)), pq, ps


def _ret_fwd(proj, lgv, s0f, s0b, comms=()):
    t = proj.shape[0]
    n = t // CHUNK

    def one(q, k, v, lg, s_s, hh, o_ref, sp_ref, reverse):
        _, dm, wq, ws, g, _, _ = _decay_tables(lg, reverse)
        vb = v.astype(BF16)
        p = _dot_nt(q.astype(BF16), k.astype(BF16)) * dm
        s = s_s[hh]
        sp_ref[hh, 0] = s
        o_ref[:, DH * hh:DH * (hh + 1)] = _dot(p.astype(BF16), vb) + _dot((q * wq).astype(BF16), s.astype(BF16))
        s_s[hh] = g * s + _dot_tn((k * ws).astype(BF16), vb)

    def body(qf, kf, vf, qb, kb, vb, lg_ref, s0f_ref, s0b_ref, of_ref, ob_ref, spf_ref, spb_ref, sf_s, sb_s):
        @pl.when(pl.program_id(0) == 0)
        def _():
            sf_s[...] = s0f_ref[...]
            sb_s[...] = s0b_ref[...]
        for hh in range(HEADS):
            sl = slice(DH * hh, DH * (hh + 1))
            one(qf[:, sl].astype(F32), kf[:, sl].astype(F32), vf[:, sl], lg_ref[hh, 0:1, :], sf_s, hh, of_ref, spf_ref,
                False)
            one(qb[:, sl].astype(F32), kb[:, sl].astype(F32), vb[:, sl], lg_ref[hh, 1:2, :], sb_s, hh, ob_ref, spb_ref,
                True)

    blk = (CHUNK, RET_W)
    fw = [pl.BlockSpec(blk, lambda i, o=o: (i, o)) for o in range(3)]
    bw = [pl.BlockSpec(blk, lambda i, o=o: (n - 1 - i, o)) for o in range(3)]
    st = _full((HEADS, DH, DH))
    return _call(body, name="ret_fwd", grid=(n,),
                 in_specs=fw + bw + [_full((HEADS, 2, LANES)), st, st],
                 out_specs=[pl.BlockSpec(blk, lambda i: (i, 0)), pl.BlockSpec(blk, lambda i: (n - 1 - i, 0)),
                            pl.BlockSpec((HEADS, 1, DH, DH), lambda i: (0, i, 0, 0)),
                            pl.BlockSpec((HEADS, 1, DH, DH), lambda i: (0, n - 1 - i, 0, 0))],
                 out_shape=[_sds((t, RET_W)), _sds((t, RET_W)), _sds((HEADS, n, DH, DH)), _sds((HEADS, n, DH, DH))],
                 scratch_shapes=[pltpu.VMEM((HEADS, DH, DH), F32), pltpu.VMEM((HEADS, DH, DH), F32)],
                 sem=("arbitrary",), args=(proj, proj, proj, proj, proj, proj, lgv, s0f, s0b), comms=comms)


def _ret_bwd(proj, lgv, sgv, sprev, do, reverse, name, comms=()):
    t = proj.shape[0]
    n = t // CHUNK
    d = 1 if reverse else 0
    cidx = (lambda i: i) if reverse else (lambda i: n - 1 - i)

    def body(q_ref, k_ref, v_ref, lg_ref, sg_ref, s_ref, do_ref, dq_ref, dk_ref, dv_ref, ds0_ref, drd_ref, ds_s, acc_s):
        i = pl.program_id(0)

        @pl.when(i == 0)
        def _():
            ds_s[...] = jnp.zeros_like(ds_s)
            acc_s[...] = jnp.zeros_like(acc_s)
        for hh in range(HEADS):
            sl = slice(DH * hh, DH * (hh + 1))
            relc, dm, wq, ws, g, pq, ps = _decay_tables(lg_ref[hh, d:d + 1, :], reverse)
            qb, kb, vb = q_ref[:, sl], k_ref[:, sl], v_ref[:, sl]
            q, k = qb.astype(F32), kb.astype(F32)
            p = _dot_nt(qb, kb) * dm
            s = s_ref[hh, 0]
            dob = do_ref[:, sl].astype(BF16)
            dsn = ds_s[hh]
            dsb = dsn.astype(BF16)
            dv_ref[:, sl] = (_dot_tn(p.astype(BF16), dob) + _dot((k * ws).astype(BF16), dsb)).astype(BF16)
            dp = _dot_nt(dob, vb)
            dab = (dp * dm).astype(BF16)
            xq = _dot_nt(dob, s.astype(BF16))
            yk = _dot_nt(vb, dsb)
            dq_ref[:, sl] = (_dot(dab, kb) + xq * wq).astype(BF16)
            dk_ref[:, sl] = (_dot_tn(dab, qb) + yk * ws).astype(BF16)
            ds_s[hh] = g * dsn + _dot_tn((q * wq).astype(BF16), dob)
            part = (_sum0(dp * p * relc) + _sum0(xq * q * wq * pq) + _sum0(yk * k * ws * ps)
                    + _sum0(dsn * s) * g * float(CHUNK))
            acc_s[hh] += jnp.broadcast_to(part, (SUBLANES, LANES))

        @pl.when(i == n - 1)
        def _():
            ds0_ref[...] = ds_s[...]
            for hh in range(HEADS):
                tot = jnp.sum(acc_s[hh, 0:1, :], axis=1, keepdims=True)
                drd_ref[hh] = jnp.broadcast_to(tot, (SUBLANES, LANES)) * sg_ref[hh, d:d + 1, :]

    blk = (CHUNK, RET_W)
    qkv = [pl.BlockSpec(blk, lambda i, o=o: (cidx(i), o)) for o in range(3)]
    hc = pl.BlockSpec(blk, lambda i: (cidx(i), 0))
    lane = _full((HEADS, 2, LANES))
    return _call(body, name=name, grid=(n,),
                 in_specs=qkv + [lane, lane, pl.BlockSpec((HEADS, 1, DH, DH), lambda i: (0, cidx(i), 0, 0)), hc],
                 out_specs=[hc, hc, hc, _full((HEADS, DH, DH)), _full((HEADS, SUBLANES, LANES))],
                 out_shape=[_sds((t, RET_W), BF16)] * 3 + [_sds((HEADS, DH, DH)), _sds((HEADS, SUBLANES, LANES))],
                 scratch_shapes=[pltpu.VMEM((HEADS, DH, DH), F32), pltpu.VMEM((HEADS, SUBLANES, LANES), F32)],
                 sem=("arbitrary",), args=(proj, proj, proj, lgv, sgv, sprev, do), comms=comms)


def _ctx_weights(lg, l_len, reverse):
    pos = lax.broadcasted_iota(jnp.int32, (l_len, DH), 0).astype(F32)
    steps = pos if reverse else (l_len - 1.0 - pos)
    return jnp.exp(lg * steps), steps


def _ctx_state_fwd(projc, lgv):
    l_len = projc.shape[0]

    def body(k_ref, v_ref, lg_ref, sf_ref, sb_ref):
        k = k_ref[...]
        vb = v_ref[...].astype(BF16)
        for d, o_ref in ((0, sf_ref), (1, sb_ref)):
            w, _ = _ctx_weights(lg_ref[0, d:d + 1, :], l_len, d == 1)
            o_ref[0] = _dot_tn((k * w).astype(BF16), vb)

    st = pl.BlockSpec((1, DH, DH), lambda h: (h, 0, 0))
    return _pc(body, name="ctx_state_fwd", grid=(HEADS,),
               in_specs=[pl.BlockSpec((l_len, DH), lambda h: (0, HEADS + h)),
                         pl.BlockSpec((l_len, DH), lambda h: (0, 2 * HEADS + h)),
                         pl.BlockSpec((1, 2, LANES), lambda h: (h, 0, 0))],
               out_specs=[st, st], out_shape=[_sds((HEADS, DH, DH))] * 2,
               compiler_params=_params("arbitrary"))(projc, projc, lgv)


def _ctx_state_bwd(projc, lgv, sgv, dsf, dsb):
    l_len = projc.shape[0]

    def body(k_ref, v_ref, lg_ref, sg_ref, dsf_ref, dsb_ref, dk_ref, dv_ref, drd_ref):
        k = k_ref[...]
        vb = v_ref[...].astype(BF16)
        dk = jnp.zeros((l_len, DH), F32)
        dv = jnp.zeros((l_len, DH), F32)
        rows = []
        for d, ds_ref in ((0, dsf_ref), (1, dsb_ref)):
            w, steps = _ctx_weights(lg_ref[0, d:d + 1, :], l_len, d == 1)
            dsb16 = ds_ref[0].astype(BF16)
            dkw = _dot_nt(vb, dsb16)
            dk = dk + dkw * w
            dv = dv + _dot((k * w).astype(BF16), dsb16)
            tot = jnp.sum(_sum0(dkw * k * w * steps), axis=1, keepdims=True)
            rows.append(jnp.broadcast_to(tot, (1, LANES)) * sg_ref[0, d:d + 1, :])
        dk_ref[...] = dk.astype(BF16)
        dv_ref[...] = dv.astype(BF16)
        rid = lax.broadcasted_iota(jnp.int32, (SUBLANES, LANES), 0)
        drd_ref[0] = jnp.where(rid == 0, rows[0], jnp.where(rid == 1, rows[1], 0.0))

    st = pl.BlockSpec((1, DH, DH), lambda h: (h, 0, 0))
    lane = pl.BlockSpec((1, 2, LANES), lambda h: (h, 0, 0))
    hc = pl.BlockSpec((l_len, DH), lambda h: (0, h))
    return _pc(body, name="ctx_state_bwd", grid=(HEADS,),
               in_specs=[pl.BlockSpec((l_len, DH), lambda h: (0, HEADS + h)),
                         pl.BlockSpec((l_len, DH), lambda h: (0, 2 * HEADS + h)), lane, lane, st, st],
               out_specs=[hc, hc, pl.BlockSpec((1, SUBLANES, LANES), lambda h: (h, 0, 0))],
               out_shape=[_sds((l_len, RET_W), BF16), _sds((l_len, RET_W), BF16), _sds((HEADS, SUBLANES, LANES))],
               compiler_params=_params("arbitrary"))(projc, projc, lgv, sgv, dsf, dsb)


G_BLOCK = (3 * RET_W) // RET_W
GATE_BLOCK = (4 * RET_W + LRU_W) // LRU_W


def _head_norm(y):
    yc = y - jnp.mean(y, axis=-1, keepdims=True)
    rs = lax.rsqrt(jnp.mean(yc * yc, axis=-1, keepdims=True) + EPS)
    return yc * rs, rs


def _gelu_parts(z):
    th = jnp.tanh(GELU_K * (z + GELU_C * z * z * z))
    return 0.5 * z * (1.0 + th), th


def _mix_fwd(o_f, o_b, proj, hf, hb, w_out, x, g1):
    t = x.shape[0]
    tm = _tile(t, True)

    def body(of_ref, ob_ref, g_ref, gt_ref, hf_ref, hb_ref, w_ref, x_ref, g1_ref, x1_ref, cat_ref):
        o = of_ref[...] + ob_ref[...]
        g = g_ref[...].astype(F32)
        for hh in range(HEADS):
            sl = slice(DH * hh, DH * (hh + 1))
            nrm, _ = _head_norm(o[:, sl])
            gh = g[:, sl]
            cat_ref[:, sl] = (gh * _sigmoid(gh) * nrm).astype(BF16)
        gel, _ = _gelu_parts(gt_ref[...].astype(F32))
        cat_ref[:, RET_W:] = ((hf_ref[...] + hb_ref[...]) * gel).astype(BF16)
        x1_ref[...] = x_ref[...] + g1_ref[...] * _dot(cat_ref[...], w_ref[...])

    half = pl.BlockSpec((tm, RET_W), lambda i: (i, 0))
    big = pl.BlockSpec((tm, D_MODEL), lambda i: (i, 0))
    return _pc(body, name="mix_fwd", grid=(t // tm,),
               in_specs=[half, half, pl.BlockSpec((tm, RET_W), lambda i: (i, G_BLOCK)),
                         pl.BlockSpec((tm, LRU_W), lambda i: (i, GATE_BLOCK)), half, half,
                         _full((D_MODEL, D_MODEL)), big, _full((1, D_MODEL))],
               out_specs=[big, big], out_shape=[_sds((t, D_MODEL)), _sds((t, D_MODEL), BF16)],
               compiler_params=_params("arbitrary"))(o_f, o_b, proj, proj, hf, hb, w_out, x, g1)


def _mix_bwd(o_f, o_b, proj, hf, hb, w_out, cat, dx1, g1, comms=()):
    t = dx1.shape[0]
    tm = _tile(t, True)

    def body(of_ref, ob_ref, g_ref, gt_ref, hf_ref, hb_ref, w_ref, cat_ref, dx1_ref, g1_ref,
             do_ref, dhs_ref, dg_ref, dgt_ref, dyb_ref, dg1_ref):
        dx1v = dx1_ref[...]
        y = _dot(cat_ref[...], w_ref[...])

        @pl.when(pl.program_id(0) == 0)
        def _():
            dg1_ref[...] = jnp.zeros_like(dg1_ref)
        dg1_ref[...] += _sum0(dx1v * y)
        dyb = (g1_ref[...] * dx1v).astype(BF16)
        dyb_ref[...] = dyb
        dcat = _dot_nt(dyb, w_ref[...])
        o = of_ref[...] + ob_ref[...]
        g = g_ref[...].astype(F32)
        for hh in range(HEADS):
            sl = slice(DH * hh, DH * (hh + 1))
            nrm, rs = _head_norm(o[:, sl])
            gh = g[:, sl]
            sg = _sigmoid(gh)
            dret = dcat[:, sl]
            dg_ref[:, sl] = (dret * nrm * (sg * (1.0 + gh * (1.0 - sg)))).astype(BF16)
            dn = dret * (gh * sg)
            dyc = rs * (dn - nrm * jnp.mean(dn * nrm, axis=-1, keepdims=True))
            do_ref[:, sl] = (dyc - jnp.mean(dyc, axis=-1, keepdims=True)).astype(BF16)
        z = gt_ref[...].astype(F32)
        gel, th = _gelu_parts(z)
        dlru = dcat[:, RET_W:]
        dhs_ref[...] = dlru * gel
        dgel = 0.5 * (1.0 + th) + 0.5 * z * (1.0 - th * th) * GELU_K * (1.0 + 3.0 * GELU_C * z * z)
        dgt_ref[...] = (dlru * (hf_ref[...] + hb_ref[...]) * dgel).astype(BF16)

    half = pl.BlockSpec((tm, RET_W), lambda i: (i, 0))
    big = pl.BlockSpec((tm, D_MODEL), lambda i: (i, 0))
    return _call(body, name="mix_bwd", grid=(t // tm,),
                 in_specs=[half, half, pl.BlockSpec((tm, RET_W), lambda i: (i, G_BLOCK)),
                           pl.BlockSpec((tm, LRU_W), lambda i: (i, GATE_BLOCK)), half, half,
                           _full((D_MODEL, D_MODEL)), big, big, _full((1, D_MODEL))],
                 out_specs=[half, half, half, half, big, _full((1, D_MODEL))],
                 out_shape=[_sds((t, RET_W), BF16), _sds((t, RET_W)), _sds((t, RET_W), BF16), _sds((t, RET_W), BF16),
                            _sds((t, D_MODEL), BF16), _sds((1, D_MODEL))],
                 scratch_shapes=[], sem=("arbitrary",), args=(o_f, o_b, proj, proj, hf, hb, w_out, cat, dx1, g1),
                 comms=comms)


def _mlp(x1, n2g, sh2, sc2, g2, fg, w1_parts, w2_parts, tgt):
    t = x1.shape[0]
    tm = _tile(t)
    hb_ = MLP_H // N_CHIP
    q_rows = hb_ // 4
    n_cp = 4 * N_DEV

    def body(x1_ref, n2g_ref, sh2_ref, sc2_ref, g2_ref, fg_ref, w1a, w1b, w2a, w2b, tgt_ref,
             dx1_ref, h2b_ref, ab_ref, dub_ref, dmb_ref, dsc_ref, dsh_ref, dg2_ref, dn2_ref, dfg_ref, loss_ref,
             w1_s, w2_s, r_s, sems):
        @pl.when(pl.program_id(0) == 0)
        def _():
            cps = []
            for p, parts in enumerate(((w1a, w2a), (w1b, w2b))):
                for d in range(N_DEV):
                    rows = pl.ds(2 * q_rows * (d % 2) + q_rows * p, q_rows)
                    for src, dst in zip(parts, (w1_s, w2_s)):
                        cps.append(pltpu.make_async_copy(src.at[d], dst.at[d // 2, rows], sems.at[len(cps)]))
            for cp in cps:
                cp.start()
            for r in (dsc_ref, dsh_ref, dg2_ref, dn2_ref, dfg_ref, loss_ref):
                r[...] = jnp.zeros_like(r)
            for cp in cps:
                cp.wait()
        x1v = x1_ref[...]
        n2g, sc2, g2, fg = n2g_ref[...], sc2_ref[...], g2_ref[...], fg_ref[...]
        xh, _ = _rms(x1v)
        h2b = (xh * n2g * (1.0 + sc2) + sh2_ref[...]).astype(BF16)
        h2b_ref[...] = h2b
        m = jnp.zeros((tm, D_MODEL), F32)
        for j in range(N_CHIP):
            sl = slice(hb_ * j, hb_ * (j + 1))
            r = jnp.maximum(_dot(h2b, w1_s[j]), 0.0)
            r_s[:, sl] = r
            ab = (r * r).astype(BF16)
            ab_ref[:, sl] = ab
            m = m + _dot(ab, w2_s[j])
        x2 = x1v + g2 * m
        x2h, r2 = _rms(x2)
        err = x2h * fg - tgt_ref[...]
        loss_ref[...] += _sum0(err * err)
        dout = err * (1.0 / D_MODEL)
        dfg_ref[...] += _sum0(dout * x2h)
        dxh = dout * fg
        dx2 = r2 * (dxh - x2h * jnp.mean(dxh * x2h, axis=-1, keepdims=True))
        dg2_ref[...] += _sum0(dx2 * m)
        dmb = (g2 * dx2).astype(BF16)
        dmb_ref[...] = dmb
        dh2 = jnp.zeros((tm, D_MODEL), F32)
        for j in range(N_CHIP):
            sl = slice(hb_ * j, hb_ * (j + 1))
            dub = (_dot_nt(dmb, w2_s[j]) * (2.0 * r_s[:, sl])).astype(BF16)
            dub_ref[:, sl] = dub
            dh2 = dh2 + _dot_nt(dub, w1_s[j])
        dx, dn2_t, dsh_t, dsc_t = _norm_mod_bwd(x1v, n2g, sc2, dh2)
        dx1_ref[...] = dx2 + dx
        dn2_ref[...] += dn2_t
        dsh_ref[...] += dsh_t
        dsc_ref[...] += dsc_t

        @pl.when(pl.program_id(0) == t // tm - 1)
        def _():
            tot = jnp.sum(loss_ref[...], axis=1, keepdims=True) * (0.5 / D_MODEL)
            loss_ref[...] = jnp.broadcast_to(tot, loss_ref.shape)

    row = _full((1, D_MODEL))
    big = pl.BlockSpec((tm, D_MODEL), lambda i: (i, 0))
    wide = pl.BlockSpec((tm, MLP_H), lambda i: (i, 0))
    return _pc(body, name="mlp", grid=(t // tm,),
               in_specs=[big, row, row, row, row, row, ANY, ANY, ANY, ANY, big],
               out_specs=[big, big, wide, wide, big, row, row, row, row, row, row],
               out_shape=[_sds((t, D_MODEL)), _sds((t, D_MODEL), BF16), _sds((t, MLP_H), BF16), _sds((t, MLP_H), BF16),
                          _sds((t, D_MODEL), BF16)] + [_sds((1, D_MODEL))] * 6,
               scratch_shapes=[pltpu.VMEM((N_CHIP, D_MODEL, hb_), BF16), pltpu.VMEM((N_CHIP, hb_, D_MODEL), BF16),
                               pltpu.VMEM((tm, MLP_H), F32), pltpu.SemaphoreType.DMA((n_cp,))],
               compiler_params=_params("arbitrary"))(x1, n2g, sh2, sc2, g2, fg, *w1_parts, *w2_parts, tgt)


def _tn(a, b, nj, a_blocked, b_blocked, name, extra=None, comms=()):
    t = a.shape[0]
    m = a.shape[1] // (nj if a_blocked else 1)
    n = b.shape[1] // (nj if b_blocked else 1)
    bk = 1024 if t % 1024 == 0 else (512 if t % 512 == 0 else t)
    nk = t // bk
    a_map = (lambda j, k: (k, j)) if a_blocked else (lambda j, k: (k, 0))
    b_map = (lambda j, k: (k, j)) if b_blocked else (lambda j, k: (k, 0))
    in_specs = [pl.BlockSpec((bk, m), a_map), pl.BlockSpec((bk, n), b_map)]
    args = [a, b]
    if extra is not None:
        a2, b2 = extra
        t2 = a2.shape[0]
        in_specs += [pl.BlockSpec((t2, m), (lambda j, k: (0, j)) if a_blocked else (lambda j, k: (0, 0))),
                     pl.BlockSpec((t2, n), (lambda j, k: (0, j)) if b_blocked else (lambda j, k: (0, 0)))]
        args += [a2, b2]

    def body(*refs):
        a_ref, b_ref = refs[0], refs[1]
        o_ref, acc = refs[-2], refs[-1]
        k = pl.program_id(1)

        @pl.when(k == 0)
        def _():
            acc[...] = jnp.zeros_like(acc)
        acc[...] += _dot_tn(a_ref[...].astype(BF16), b_ref[...].astype(BF16))

        @pl.when(k == nk - 1)
        def _():
            if extra is not None:
                acc[...] += _dot_tn(refs[2][...].astype(BF16), refs[3][...].astype(BF16))
            o_ref[0] = acc[...]

    (out,), couts = _call(body, name=name, grid=(nj, nk), in_specs=in_specs,
                          out_specs=[pl.BlockSpec((1, m, n), lambda j, k: (j, 0, 0))], out_shape=[_sds((nj, m, n))],
                          scratch_shapes=[pltpu.VMEM((m, n), F32)], sem=("arbitrary", "arbitrary"), args=args,
                          comms=comms)
    return (out, couts) if comms else out


ROW_LOSS = 0
ROW_DMOD = 1
ROW_DMODC = 7
ROW_N1, ROW_N2, ROW_FG, ROW_CB = 9, 10, 11, 12
ROW_BA, ROW_BX, ROW_LAM = 13, 15, 17
ROW_CW = 20
ROW_RD = 24
SLAB_ROWS = 32
SEG = D_MODEL // 2


def _pack_small(rows, drd, cw2, cb2, lru2, gates):
    n_rows, n_lru = len(rows), len(lru2)

    def body(*refs):
        r = refs[:n_rows]
        drd_f, drd_b, drd_c, cw_a, cw_b, cb_a, cb_b = refs[n_rows:n_rows + 7]
        lru = refs[n_rows + 7:n_rows + 7 + n_lru]
        gf_ref, gb_ref, slab, ga, gx = refs[n_rows + 7 + n_lru:]
        slab[...] = jnp.zeros_like(slab)
        slab[ROW_LOSS:ROW_LOSS + 1, :] = r[0][...]
        for k in range(N_MOD):
            slab[ROW_DMOD + k:ROW_DMOD + k + 1, :] = r[1 + k][...]
        slab[ROW_DMODC:ROW_DMODC + 1, :] = r[7][...]
        slab[ROW_DMODC + 1:ROW_DMODC + 2, :] = r[8][...]
        slab[ROW_N1:ROW_N1 + 1, :] = r[9][...] + r[10][...]
        slab[ROW_N2:ROW_N2 + 1, :] = r[11][...]
        slab[ROW_FG:ROW_FG + 1, :] = r[12][...]
        slab[ROW_CB:ROW_CB + 1, 0:LRU_W] = cb_a[...] + cb_b[...]
        for k, row in enumerate((ROW_BA, ROW_BA + 1, ROW_BX, ROW_BX + 1, ROW_LAM, ROW_LAM + 1)):
            slab[row:row + 1, 0:LRU_W] = lru[2 * k][...] + lru[2 * k + 1][...]
        slab[ROW_CW:ROW_CW + 4, 0:LRU_W] = cw_a[...] + cw_b[...]
        for h in range(HEADS):
            slab[ROW_RD + h:ROW_RD + h + 1, 0:LANES] = drd_f[h, 0:1, :] + drd_c[h, 0:1, :]
            slab[ROW_RD + HEADS + h:ROW_RD + HEADS + h + 1, 0:LANES] = drd_b[h, 0:1, :] + drd_c[h, 1:2, :]
        for d, g_ref in enumerate((gf_ref, gb_ref)):
            for n in range(LRU_BLOCKS):
                blk = slice(LRU_BD * n, LRU_BD * (n + 1))
                ga[blk, LRU_BD * d:LRU_BD * (d + 1)] = g_ref[0, blk, blk].astype(BF16)
                gx[blk, LRU_BD * d:LRU_BD * (d + 1)] = g_ref[1, blk, blk].astype(BF16)

    args = list(rows) + list(drd) + list(cw2) + list(cb2) + list(lru2) + list(gates)
    gate_shape = (LRU_W, 2 * LRU_BD)
    return _pc(body, name="pack_small", in_specs=[_full(a.shape) for a in args],
               out_specs=[_full((SLAB_ROWS, D_MODEL)), _full(gate_shape), _full(gate_shape)],
               out_shape=[_sds((SLAB_ROWS, D_MODEL)), _sds(gate_shape, BF16), _sds(gate_shape, BF16)],
               compiler_params=_params())(*args)


def _adam_math(w, g, m, v):
    mn = ADAM_B1 * m + (1.0 - ADAM_B1) * g
    vn = ADAM_B2 * v + (1.0 - ADAM_B2) * (g * g)
    mh = mn / (1.0 - ADAM_B1 ** ADAM_STEP)
    vh = vn / (1.0 - ADAM_B2 ** ADAM_STEP)
    return -ADAM_LR * (mh / (jnp.sqrt(vh) + ADAM_EPS) + ADAM_WD * w), mn, vn


SMALL_PARAMS = ("b_ada", "norm1_g", "norm2_g", "final_g", "ret_decay", "conv_w", "conv_b", "lru_wa", "lru_ba", "lru_wx",
                "lru_bx", "lru_lambda")


def _finalize_small(chip_idx, slab_all, ga_all, gx_all, wmv):
    n_p = len(SMALL_PARAMS)
    flat = [a for nm in SMALL_PARAMS for a in wmv[nm]]
    ada_n = N_MOD * D_MODEL // N_CHIP

    def body(c_ref, slab_ref, ga_ref, gx_ref, *refs):
        prm = {nm: refs[3 * k:3 * k + 3] for k, nm in enumerate(SMALL_PARAMS)}
        outs = {nm: refs[3 * n_p + 4 * k:3 * n_p + 4 * k + 4] for k, nm in enumerate(SMALL_PARAMS)}
        b128_ref, dmc_ref, loss_ref = refs[3 * n_p + 4 * n_p:]
        chip = c_ref[0]

        def pick(fn):
            acc = fn(0)
            for j in range(1, N_CHIP):
                acc = jnp.where(chip == j, fn(j), acc)
            return acc

        tot = slab_ref[0]
        for d in range(1, N_DEV):
            tot = tot + slab_ref[d]

        def update(nm, g, sl=None, rows=None):
            w_ref, m_ref, v_ref = prm[nm]
            g_ref, d_ref, mo_ref, vo_ref = outs[nm]
            ix = (slice(None) if rows is None else rows, slice(None) if sl is None else sl)
            dl, mn, vn = _adam_math(w_ref[ix], g, m_ref[ix], v_ref[ix])
            g_ref[ix] = g
            d_ref[ix] = dl
            mo_ref[ix] = mn
            vo_ref[ix] = vn

        loss_ref[...] = jnp.broadcast_to(tot[ROW_LOSS:ROW_LOSS + 1, 0:LANES], (SUBLANES, LANES))
        for k in range(N_MOD):
            g = tot[ROW_DMOD + k:ROW_DMOD + k + 1, :]
            if k < 2:
                g = g + tot[ROW_DMODC + k:ROW_DMODC + k + 1, :]
            update("b_ada", g, slice(D_MODEL * k, D_MODEL * (k + 1)))
        update("norm1_g", tot[ROW_N1:ROW_N1 + 1, :])
        update("norm2_g", tot[ROW_N2:ROW_N2 + 1, :])
        update("final_g", tot[ROW_FG:ROW_FG + 1, :])
        update("ret_decay", tot[ROW_RD:ROW_RD + SUBLANES, 0:LANES])
        update("conv_b", tot[ROW_CB:ROW_CB + 1, 0:LRU_W])
        update("conv_w", pick(lambda j: tot[ROW_CW:ROW_CW + 4, LANES * j:LANES * (j + 1)]))
        for nm, row in (("lru_ba", ROW_BA), ("lru_bx", ROW_BX), ("lru_lambda", ROW_LAM)):
            update(nm, pick(lambda j, row=row: tot[row:row + 2, LANES * j:LANES * (j + 1)]))
        for nm, g_all in (("lru_wa", ga_ref), ("lru_wx", gx_ref)):
            for dr in range(2):
                lanes = slice(LRU_BD * dr, LRU_BD * (dr + 1))
                g = g_all[0, :, lanes].astype(F32)
                for d in range(1, N_DEV):
                    g = g + g_all[d, :, lanes].astype(F32)
                update(nm, g, rows=slice(LRU_W * dr, LRU_W * (dr + 1)))

        def seg(rows6, s):
            return rows6[s // 2][:, SEG * (s % 2):SEG * (s % 2 + 1)]

        b128_ref[...] = jnp.zeros_like(b128_ref)
        dmc_ref[...] = jnp.zeros_like(dmc_ref)
        zero = jnp.zeros((1, D_MODEL), F32)
        ctx6 = [tot[ROW_DMODC:ROW_DMODC + 1, :], tot[ROW_DMODC + 1:ROW_DMODC + 2, :]] + [zero] * (N_MOD - 2)
        for q in range(ada_n // SEG):
            cols = slice(SEG * q, SEG * (q + 1))
            for d in range(N_DEV):
                rows6 = [slab_ref[d, ROW_DMOD + k:ROW_DMOD + k + 1, :] for k in range(N_MOD)]
                b128_ref[d:d + 1, cols] = pick(lambda j, rows6=rows6: seg(rows6, 3 * j + q))
            c = pick(lambda j: seg(ctx6, 3 * j + q))
            b128_ref[N_DEV:N_DEV + 1, cols] = c
            dmc_ref[0:1, cols] = c

    out_shape = []
    for nm in SMALL_PARAMS:
        out_shape += [_sds(wmv[nm][0].shape)] * 4
    out_shape += [_sds((LANES, ada_n)), _sds((SUBLANES, ada_n)), _sds((SUBLANES, LANES))]
    args = [slab_all, ga_all, gx_all] + flat
    grid_spec = pltpu.PrefetchScalarGridSpec(
        num_scalar_prefetch=1, grid=(1,), in_specs=[_full(a.shape) for a in args],
        out_specs=[_full(s.shape) for s in out_shape])
    outs = _pc(body, name="finalize_small", grid_spec=grid_spec, out_shape=out_shape,
               compiler_params=_params("arbitrary"))(chip_idx, *args)
    res = {nm: tuple(outs[4 * k:4 * k + 4]) for k, nm in enumerate(SMALL_PARAMS)}
    return res, outs[4 * n_p], outs[4 * n_p + 1], outs[4 * n_p + 2]


def _block_diag(w):
    eye = jnp.eye(LRU_BLOCKS, dtype=F32)
    return (w[:, :, None, :] * eye[:, None, :, None]).reshape(LRU_W, LRU_W).astype(BF16)


def _lane_rep(v8):
    return jnp.broadcast_to(v8.reshape(SUBLANES, 1), (SUBLANES, LANES))


def kernel(x, c, ctx, c_ctx, w_ada, b_ada, norm1_g, norm2_g, w_in, ret_decay, conv_w, conv_b, lru_wa, lru_ba, lru_wx, lru_bx, lru_lambda, w_out, w_mlp1, w_mlp2, final_g, loss_target, m_c_ctx, m_w_ada, m_b_ada, m_norm1_g, m_norm2_g, m_w_in, m_ret_decay, m_conv_w, m_conv_b, m_lru_wa, m_lru_ba, m_lru_wx, m_lru_bx, m_lru_lambda, m_w_out, m_w_mlp1, m_w_mlp2, m_final_g, v_c_ctx, v_w_ada, v_b_ada, v_norm1_g, v_norm2_g, v_w_in, v_ret_decay, v_conv_w, v_conv_b, v_lru_wa, v_lru_ba, v_lru_wx, v_lru_bx, v_lru_lambda, v_w_out, v_w_mlp1, v_w_mlp2, v_final_g):
    ax, ay, ac = lax.axis_index("x"), lax.axis_index("y"), lax.axis_index("c")
    chip = 2 * ax + ay
    dev = 4 * ax + 2 * ay + ac
    c_idx = ac.reshape(1).astype(jnp.int32)
    j_idx = chip.reshape(1).astype(jnp.int32)

    xt = x[0]
    t_len = xt.shape[0]
    ctxt = ctx[0]
    l_len = ctxt.shape[0]
    tgt = loss_target[0]
    ada_n = w_ada.shape[2]

    def my_half(w2d):
        r = w2d.shape[0] // 2
        return lax.dynamic_slice_in_dim(w2d, ac * r, r, axis=0).astype(BF16)

    pad8 = lambda a: jnp.pad(a, ((0, SUBLANES - a.shape[0]), (0, 0)))
    small = jnp.concatenate([pad8(conv_w[0]), pad8(lru_ba[0]), pad8(lru_bx[0]), pad8(lru_lambda[0])], axis=0)
    gw_in, c_all, small_all = _all_gather([my_half(w_in[0]), pad8(c), small], "gather_head")
    w4 = gw_in.reshape(N_CHIP, D_MODEL, IN_COLS // N_CHIP)

    a16, lgv, sgv = _prep(c_all[:, 0, :], c_ctx, ret_decay[0])
    b_shard = lax.dynamic_slice_in_dim(b_ada, chip * ada_n, ada_n, axis=1)
    (mod_parts,) = _all_gather([_mod_fwd(a16, w_ada[0], b_shard)], "gather_mod")
    mod_all = mod_parts[0::2].transpose(1, 0, 2).reshape(16, N_CHIP * ada_n)
    mod_me = lax.dynamic_slice_in_dim(mod_all, dev, 1, axis=0)
    sh1, sc1, g1, sh2, sc2, g2 = [mod_me[:, D_MODEL * k:D_MODEL * (k + 1)] for k in range(N_MOD)]
    csh1, csc1 = mod_all[8:9, 0:D_MODEL], mod_all[8:9, D_MODEL:2 * D_MODEL]

    cos2, sin2 = _rotary_tables(t_len)
    cos_c, sin_c = jnp.ones((l_len, DH), F32), jnp.zeros((l_len, DH), F32)
    n1g, n2g = norm1_g, norm2_g
    fg = final_g.reshape(1, D_MODEL)

    small_full = small_all[0::2].transpose(1, 0, 2).reshape(4 * SUBLANES, LRU_W)
    cw = small_full[0:4]
    cb = conv_b
    ba_f, ba_b = small_full[8:9], small_full[9:10]
    bx_f, bx_b = small_full[16:17], small_full[17:18]
    lam_f, lam_b = small_full[24:25], small_full[25:26]
    wa_f, wa_b = _block_diag(lru_wa[0, 0]), _block_diag(lru_wa[0, 1])
    wx_f, wx_b = _block_diag(lru_wx[0, 0]), _block_diag(lru_wx[0, 1])
    zero_h = jnp.zeros((1, LRU_W), F32)

    projc, xrc, hcb16 = _inproj_fwd(ctxt, n1g, csh1, csc1, w4, cos_c, sin_c, "inproj_fwd_ctx")
    s_f, s_b = _ctx_state_fwd(projc, lgv)
    xcc = _conv_fwd(xrc, cw, cb, "conv_fwd_ctx")
    hcf = _lru_fwd(xcc, wa_f, wx_f, ba_f, bx_f, lam_f, zero_h, False, "lru_fwd_ctx_f")
    hcbk = _lru_fwd(xcc, wa_b, wx_b, ba_b, bx_b, lam_b, zero_h, True, "lru_fwd_ctx_b")
    lru_sf, lru_sb = hcf[l_len - 1:l_len], hcbk[0:1]

    h1, h2 = my_half(w_mlp1[0]), my_half(w_mlp2[0])
    q = h1.shape[0] // 2
    (proj, xrl, hb16), ((gw_1a,),) = _inproj_fwd(xt, n1g, sh1, sc1, w4, cos2, sin2, "inproj_fwd",
                                           comms=(_AllGather([h1[:q]]),))
    (o_f, o_b, spf, spb), ((gw_1b, gw_out),) = _ret_fwd(proj, lgv, s_f, s_b,
                                                       comms=(_AllGather([h1[q:], my_half(w_out[0])]),))
    xcl = _conv_fwd(xrl, cw, cb, "conv_fwd")
    hf, ((gw_2a,),) = _lru_fwd(xcl, wa_f, wx_f, ba_f, bx_f, lam_f, lru_sf, False, "lru_fwd_f",
                              comms=(_AllGather([h2[:q]]),))
    hbk, ((gw_2b,),) = _lru_fwd(xcl, wa_b, wx_b, ba_b, bx_b, lam_b, lru_sb, True, "lru_fwd_b",
                               comms=(_AllGather([h2[q:]]),))
    wo = gw_out.reshape(D_MODEL, D_MODEL)
    x1, cat = _mix_fwd(o_f, o_b, proj, hf, hbk, wo, xt, g1)

    (dx1, h2b, ab, dub, dmb, dsc2, dsh2, dg2, dn2g, dfg, lossv) = _mlp(
        x1, n2g, sh2, sc2, g2, fg, (gw_1a, gw_1b), (gw_2a, gw_2b), tgt)
    gw_mlp1 = _tn(h2b, dub, N_CHIP, False, True, "grad_w_mlp1")
    b_1 = gw_mlp1.reshape(N_DEV, D_MODEL // 2, MLP_H // N_CHIP)
    gw_mlp2, ((r_1,),) = _tn(ab, dmb, N_CHIP, True, False, "grad_w_mlp2", comms=(_pair_exchange([b_1]),))

    jc_idx = jnp.concatenate([j_idx, c_idx])
    b_2 = gw_mlp2.reshape(N_DEV, MLP_H // N_DEV, D_MODEL)
    (do, dhs, dg, dgate, dyb, dg1), ((r_2,),) = _mix_bwd(
        o_f, o_b, proj, hf, hbk, wo, cat, dx1, g1, comms=(_pair_exchange([b_2]),))
    gw_o = _tn(cat, dyb, 1, False, False, "grad_w_out")
    b_o = gw_o.reshape(N_DEV, D_MODEL // N_DEV, D_MODEL)
    p_1, pb_1 = _pair_add(b_1, r_1, c_idx, "rs_pair_add_w_mlp1")
    p_2, pb_2 = _pair_add(b_2, r_2, c_idx, "rs_pair_add_w_mlp2")

    (dq_f, dk_f, dv_f, ds_f, drd_f), ((q_1,), (r_o,)) = _ret_bwd(
        proj, lgv, sgv, spf, do, False, "ret_bwd_f", comms=(_chip_exchange([pb_1]), _pair_exchange([b_o])))
    p_o, pb_o = _pair_add(b_o, r_o, c_idx, "rs_pair_add_w_out")
    h_1 = _chip_add(p_1, q_1, jc_idx, "rs_chip_add_w_mlp1")

    (dq_b, dk_b, dv_b, ds_b, drd_b), ((q_2,), (f_1,)) = _ret_bwd(
        proj, lgv, sgv, spb, do, True, "ret_bwd_b", comms=(_chip_exchange([pb_2]), _pair_gather([h_1])))
    h_2 = _chip_add(p_2, q_2, jc_idx, "rs_chip_add_w_mlp2")

    (dxc_f, dpre_f, dba_f, dbx_f, dlam_f, dh0_f), ((q_o,), (f_2,)) = _lru_bwd(
        xcl, wa_f, wx_f, ba_f, bx_f, lam_f, hf, lru_sf, dhs, False, "lru_bwd_f",
        comms=(_chip_exchange([pb_o]), _pair_gather([h_2])))
    h_o = _chip_add(p_o, q_o, jc_idx, "rs_chip_add_w_out")
    (dxc_b, dpre_b, dba_b, dbx_b, dlam_b, dh0_b), ((f_o,),) = _lru_bwd(
        xcl, wa_b, wx_b, ba_b, bx_b, lam_b, hbk, lru_sb, dhs, True, "lru_bwd_b", comms=(_pair_gather([h_o]),))
    dxr, dcw, dcb = _conv_bwd(dxc_f, dxc_b, xrl, cw, "conv_bwd")
    grad_x, dpb, dn1g, dsh1, dsc1 = _inproj_bwd(
        xt, n1g, sh1, sc1, w4, cos2, sin2, [dq_f, dq_b, dk_f, dk_b, dv_f, dv_b, dg, dxr, dgate], dx1, "inproj_bwd")

    dkc, dvc, drd_c = _ctx_state_bwd(projc, lgv, sgv, ds_f, ds_b)
    zc = jnp.zeros((l_len, LRU_W), F32)
    dhc_f = lax.dynamic_update_slice(zc, dh0_f, (l_len - 1, 0))
    dhc_b = lax.dynamic_update_slice(zc, dh0_b, (0, 0))
    (dxcc_f, dprec_f, dbac_f, dbxc_f, dlamc_f, _), _ = _lru_bwd(
        xcc, wa_f, wx_f, ba_f, bx_f, lam_f, hcf, zero_h, dhc_f, False, "lru_bwd_ctx_f")
    (dxcc_b, dprec_b, dbac_b, dbxc_b, dlamc_b, _), _ = _lru_bwd(
        xcc, wa_b, wx_b, ba_b, bx_b, lam_b, hcbk, zero_h, dhc_b, True, "lru_bwd_ctx_b")
    dxrc, dcw_c, dcb_c = _conv_bwd(dxcc_f, dxcc_b, xrc, cw, "conv_bwd_ctx")
    zr = jnp.zeros((l_len, RET_W), BF16)
    _, dpbc, dn1g_c, dcsh1, dcsc1 = _inproj_bwd(
        ctxt, n1g, csh1, csc1, w4, cos_c, sin_c, [zr, zr, dkc, zr, dvc, zr, zr, dxrc, zr],
        jnp.zeros((l_len, D_MODEL), F32), "inproj_bwd_ctx")

    gw_i = _tn(hb16, dpb, N_CHIP, False, True, "grad_w_in", extra=(hcb16, dpbc))
    b_i = gw_i.reshape(N_DEV, D_MODEL // 2, IN_COLS // N_CHIP)
    gwa_f, ((r_i,),) = _tn(xcl, dpre_f, 2, False, True, "grad_lru_gates_f", extra=(xcc, dprec_f),
                           comms=(_pair_exchange([b_i]),))
    p_i, pb_i = _pair_add(b_i, r_i, c_idx, "rs_pair_add_w_in")
    gwa_b, ((q_i,),) = _tn(xcl, dpre_b, 2, False, True, "grad_lru_gates_b", extra=(xcc, dprec_b),
                           comms=(_chip_exchange([pb_i]),))
    h_i = _chip_add(p_i, q_i, jc_idx, "rs_chip_add_w_in")
    g_out, g_1, g_2 = _shard_of(f_o), _shard_of(f_1), _shard_of(f_2)
    big = {}
    (d_, mn, vn), ((f_i,),) = _adamw(w_mlp1[0], g_1, m_w_mlp1[0], v_w_mlp1[0], "adamw_w_mlp1",
                                     comms=(_pair_gather([h_i]),))
    big["w_mlp1"] = (g_1[None], d_[None], mn[None], vn[None])
    g_in = _shard_of(f_i)
    for nm, w, g, m, v in (("w_in", w_in, g_in, m_w_in, v_w_in), ("w_out", w_out, g_out, m_w_out, v_w_out),
                           ("w_mlp2", w_mlp2, g_2, m_w_mlp2, v_w_mlp2)):
        d_, mn, vn = _adamw(w[0], g, m[0], v[0], "adamw_" + nm)
        big[nm] = (g[None], d_[None], mn[None], vn[None])

    slab, ga, gx = _pack_small(
        [lossv, dsh1, dsc1, dg1, dsh2, dsc2, dg2, dcsh1, dcsc1, dn1g, dn1g_c, dn2g, dfg],
        (drd_f, drd_b, drd_c), (dcw, dcw_c), (dcb, dcb_c),
        (dba_f, dbac_f, dba_b, dbac_b, dbx_f, dbxc_f, dbx_b, dbxc_b, dlam_f, dlamc_f, dlam_b, dlamc_b),
        (gwa_f, gwa_b))
    slab_all, ga_all, gx_all = _all_gather([slab, ga, gx], "gather_small_grads")
    params = {
        "b_ada": (b_ada, m_b_ada, v_b_ada), "norm1_g": (norm1_g, m_norm1_g, v_norm1_g),
        "norm2_g": (norm2_g, m_norm2_g, v_norm2_g), "final_g": (final_g, m_final_g, v_final_g),
        "ret_decay": (ret_decay, m_ret_decay, v_ret_decay), "conv_w": (conv_w, m_conv_w, v_conv_w),
        "conv_b": (conv_b, m_conv_b, v_conv_b), "lru_wa": (lru_wa, m_lru_wa, v_lru_wa),
        "lru_ba": (lru_ba, m_lru_ba, v_lru_ba), "lru_wx": (lru_wx, m_lru_wx, v_lru_wx),
        "lru_bx": (lru_bx, m_lru_bx, v_lru_bx), "lru_lambda": (lru_lambda, m_lru_lambda, v_lru_lambda),
    }
    as2d = {
        "b_ada": lambda a: a, "norm1_g": lambda a: a, "norm2_g": lambda a: a, "conv_b": lambda a: a,
        "final_g": lambda a: a.reshape(1, D_MODEL), "ret_decay": lambda a: _lane_rep(a.reshape(-1)),
        "conv_w": lambda a: a[0], "lru_ba": lambda a: a[0], "lru_bx": lambda a: a[0], "lru_lambda": lambda a: a[0],
        "lru_wa": lambda a: a.reshape(2 * LRU_W, LRU_BD), "lru_wx": lambda a: a.reshape(2 * LRU_W, LRU_BD),
    }
    res, b128, dmc8, loss8 = _finalize_small(
        j_idx, slab_all, ga_all, gx_all, {nm: tuple(as2d[nm](a) for a in params[nm]) for nm in SMALL_PARAMS})
    loss = loss8[0, 0]
    small_out = {}
    for nm in SMALL_PARAMS:
        shp = params[nm][0].shape
        if nm == "ret_decay":
            small_out[nm] = tuple(o[:, 0].reshape(shp) for o in res[nm])
        else:
            small_out[nm] = tuple(o.reshape(shp) for o in res[nm])

    g_ada = _ada_grad(jnp.pad(a16.T, ((0, 0), (0, LANES - 16))), b128)
    d_ada, m_ada, v_ada = _adamw(w_ada[0], g_ada, m_w_ada[0], v_w_ada[0], "adamw_w_ada")

    (cparts,) = _all_gather([_cctx_partial(dmc8, w_ada[0])], "gather_cctx")
    g_cc, d_cc, m_cc, v_cc = _cctx_final(cparts, c_ctx, m_c_ctx, v_c_ctx)
    small_out["c_ctx"] = tuple(a.reshape(D_MODEL) for a in (g_cc, d_cc, m_cc, v_cc))
    small_out["w_ada"] = (g_ada[None], d_ada[None], m_ada[None], v_ada[None])
    small_out.update(big)

    order = ["c_ctx", "w_ada", "b_ada", "norm1_g", "norm2_g", "w_in", "ret_decay", "conv_w", "conv_b", "lru_wa", "lru_ba",
             "lru_wx", "lru_bx", "lru_lambda", "w_out", "w_mlp1", "w_mlp2", "final_g"]
    outs = [loss, grad_x[None]]
    for k in range(4):
        outs += [small_out[nm][k] for nm in order]
    return tuple(outs)
```

```python
import math

import jax
import jax.numpy as jnp
from jax import lax
from jax.experimental import pallas as pl
from jax.experimental.pallas import tpu as pltpu

F32 = jnp.float32
BF16 = jnp.bfloat16

D_MODEL = 1024
HEADS = 4
DH = 128
CHUNK = 256
RET_W = HEADS * DH
LRU_W = 512
LRU_BLOCKS = 8
LRU_BD = LRU_W // LRU_BLOCKS
LRU_C = 8.0
IN_COLS = 4 * RET_W + 2 * LRU_W
MLP_H = 4 * D_MODEL
N_MOD = 6
GRID_W = 64
ROPE_BASE = 10000.0
K_SCALE = DH ** -0.5
EPS = 1e-6
GELU_K = math.sqrt(2.0 / math.pi)
GELU_C = 0.044715

ADAM_LR = 0.001
ADAM_B1 = 0.9
ADAM_B2 = 0.999
ADAM_EPS = 1e-08
ADAM_WD = 0.01
ADAM_STEP = 10

N_DEV = 8
N_CHIP = 4
SUBLANES = 8
LANES = 128
VMEM_LIMIT_V7X = 56 * 1024 * 1024
MESH = pl.DeviceIdType.MESH
ANY = pl.BlockSpec(memory_space=pl.ANY)


def _pc(body, **kw):
    return pl.pallas_call(body, **kw)


def _params(*sem):
    return pltpu.CompilerParams(dimension_semantics=sem if sem else None, vmem_limit_bytes=VMEM_LIMIT_V7X)


def _tile(t, big=False):
    if big and t >= 1024:
        return 512
    return 256 if t >= 256 else t


def _sds(shape, dtype=F32):
    return jax.ShapeDtypeStruct(tuple(shape), dtype)


def _full(shape):
    nd = len(shape)
    return pl.BlockSpec(tuple(shape), lambda *_: (0,) * nd)


def _sigmoid(x):
    return 1.0 / (1.0 + jnp.exp(-x))


def _log1p_pos(y):
    s = y * (1.0 - y * (0.5 - y * (1.0 / 3.0 - y * (0.25 - y * (0.2 - y / 6.0)))))
    return jnp.where(y < 0.03, s, jnp.log(1.0 + y))


def _softplus(z):
    return jnp.maximum(z, 0.0) + _log1p_pos(jnp.exp(-jnp.abs(z)))


def _neg_expm1(x, exp_x):
    t = x * (1.0 + x * (0.5 + x * (1.0 / 6.0 + x * (1.0 / 24.0 + x * (1.0 / 120.0 + x * (1.0 / 720.0 + x / 5040.0))))))
    return -jnp.where(x > -0.25, t, exp_x - 1.0)


def _rms(x):
    r = lax.rsqrt(jnp.mean(x * x, axis=-1, keepdims=True) + EPS)
    return x * r, r


def _dot(a, b):
    return jnp.dot(a, b, preferred_element_type=F32)


def _dot_nt(a, b):
    return lax.dot_general(a, b, (((1,), (1,)), ((), ())), preferred_element_type=F32)


def _dot_tn(a, b):
    return lax.dot_general(a, b, (((0,), (0,)), ((), ())), preferred_element_type=F32)


def _sum0(x):
    return jnp.sum(x, axis=0, keepdims=True)


def _norm_mod_bwd(x, g, sc, dh):
    xh, r = _rms(x)
    hn = xh * g
    dhn = dh * (1.0 + sc)
    dxh = dhn * g
    dx = r * (dxh - xh * jnp.mean(dxh * xh, axis=-1, keepdims=True))
    return dx, _sum0(dhn * xh), _sum0(dh), _sum0(dh * hn)


def _dev_index(p):
    return 4 * p[0] + 2 * p[1] + p[2]


def _mesh_pos():
    return lax.axis_index("x"), lax.axis_index("y"), lax.axis_index("c")


class _AllGather:
    def __init__(self, arrs):
        n = len(arrs)
        self.arrays = list(arrs)
        self.out_shapes = [_sds((N_DEV,) + a.shape, a.dtype) for a in arrs]
        self.scratch = ([pltpu.VMEM(a.shape, a.dtype) for a in arrs]
                        + [pltpu.SemaphoreType.DMA((7 * n,)), pltpu.SemaphoreType.DMA((7 * n,)),
                           pltpu.SemaphoreType.DMA((n,))])
        self.aliases = {}

    def _parts(self, ins, outs, scr):
        n = len(self.arrays)
        stage = scr[:n]
        send_sems, recv_sems, local_sems = scr[n:]
        x, y, c = _mesh_pos()
        me, sib = (x, y, c), (x, y, 1 - c)
        chips = [(1 - x, y), (x, 1 - y), (1 - x, 1 - y)]

        def copy(t, k, block, to, own=False):
            dst = outs[t].at[_dev_index(block)]
            return pltpu.make_async_remote_copy(
                src_ref=ins[t] if own else dst, dst_ref=dst,
                send_sem=send_sems.at[7 * t + k], recv_sem=recv_sems.at[7 * t + k],
                device_id=to, device_id_type=MESH)

        first = []
        for t in range(n):
            first.append(copy(t, 0, me, sib, own=True))
            for j, ch in enumerate(chips):
                first.append(copy(t, 1 + j, me, (*ch, c), own=True))
        stage_in = [pltpu.make_async_copy(ins[t], stage[t], local_sems.at[t]) for t in range(n)]
        mine = [pltpu.make_async_copy(stage[t], outs[t].at[_dev_index(me)], local_sems.at[t]) for t in range(n)]
        return n, c, me, sib, chips, copy, first, stage_in, mine

    def start(self, ins, outs, scr):
        n, _, _, _, _, _, first, stage_in, mine = self._parts(ins, outs, scr)
        for cp in stage_in:
            cp.start()
        for cp in first:
            cp.start()
        for t in range(n):
            stage_in[t].wait()
            mine[t].start()

    def finish(self, ins, outs, scr):
        n, c, me, sib, chips, copy, first, _, mine = self._parts(ins, outs, scr)
        passed = []
        for j, ch in enumerate(chips):
            for t in range(n):
                copy(t, 1 + j, (*ch, c), me).wait_recv()
                p = copy(t, 4 + j, (*ch, c), sib)
                p.start()
                passed.append(p)
        for t in range(n):
            copy(t, 0, sib, me).wait_recv()
            for j, ch in enumerate(chips):
                copy(t, 4 + j, (*ch, 1 - c), me).wait_recv()
        for cp in first + passed:
            cp.wait_send()
        for cp in mine:
            cp.wait()


class _Exchange:
    def __init__(self, arrays, out_shapes, plan, n_copies, aliases=None):
        self.arrays = list(arrays)
        self.out_shapes = list(out_shapes)
        self.plan = plan
        self.scratch = [pltpu.SemaphoreType.DMA((n_copies,)), pltpu.SemaphoreType.DMA((n_copies,))]
        self.aliases = aliases or {}

    def _copies(self, ins, outs, scr):
        send_sems, recv_sems = scr
        snd, rcv = [], []
        for i, (src, dst, peer, lands) in enumerate(self.plan(ins, outs, _mesh_pos())):
            kw = dict(send_sem=send_sems.at[i], recv_sem=recv_sems.at[i], device_id=peer, device_id_type=MESH)
            snd.append(pltpu.make_async_remote_copy(src_ref=src, dst_ref=dst, **kw))
            rcv.append(pltpu.make_async_remote_copy(src_ref=src, dst_ref=lands, **kw))
        return snd, rcv

    def start(self, ins, outs, scr):
        for cp in self._copies(ins, outs, scr)[0]:
            cp.start()

    def finish(self, ins, outs, scr):
        snd, rcv = self._copies(ins, outs, scr)
        for cp in rcv:
            cp.wait_recv()
        for cp in snd:
            cp.wait_send()


def _pair_exchange(grads):
    n = len(grads)

    def plan(ins, outs, pos):
        x, y, c = pos
        return [(ins[t].at[2 * j + (1 - c)], outs[t].at[j], (x, y, 1 - c), outs[t].at[j])
                for t in range(n) for j in range(N_CHIP)]

    return _Exchange(grads, [_sds((N_CHIP,) + g.shape[1:], g.dtype) for g in grads], plan, N_CHIP * n)


def _chip_exchange(parts):
    n = len(parts)

    def plan(ins, outs, pos):
        x, y, c = pos
        chips = [(1 - x, y), (x, 1 - y), (1 - x, 1 - y)]
        return [(ins[t].at[2 * ch[0] + ch[1]], outs[t].at[k], (*ch, c), outs[t].at[k])
                for t in range(n) for k, ch in enumerate(chips)]

    return _Exchange(parts, [_sds((3,) + p.shape[1:], p.dtype) for p in parts], plan, 3 * n)


def _pair_gather(bufs):
    n = len(bufs)

    def plan(ins, outs, pos):
        x, y, c = pos
        return [(ins[t].at[c], outs[t].at[c], (x, y, 1 - c), outs[t].at[1 - c]) for t in range(n)]

    return _Exchange(bufs, [_sds(b.shape, b.dtype) for b in bufs], plan, n, aliases={t: t for t in range(n)})


def _run_comm(comm, name):
    n_in, n_out = len(comm.arrays), len(comm.out_shapes)

    def body(*refs):
        ins, outs, scr = refs[:n_in], refs[n_in:n_in + n_out], refs[n_in + n_out:]
        comm.start(ins, outs, scr)
        comm.finish(ins, outs, scr)

    outs = _pc(body, name=name, out_shape=comm.out_shapes, in_specs=[ANY] * n_in, out_specs=[ANY] * n_out,
               input_output_aliases=dict(comm.aliases), scratch_shapes=comm.scratch,
               compiler_params=_params())(*comm.arrays)
    return list(outs)


def _all_gather(arrs, name):
    return _run_comm(_AllGather(arrs), name)


def _call(body, *, name, grid, in_specs, out_specs, out_shape, scratch_shapes, sem, args, comms=()):
    n_in, n_out, n_scr = len(in_specs), len(out_specs), len(scratch_shapes)
    c_in = [len(cm.arrays) for cm in comms]
    c_out = [len(cm.out_shapes) for cm in comms]
    c_scr = [len(cm.scratch) for cm in comms]
    aliases = {}
    for k, cm in enumerate(comms):
        for a, b in cm.aliases.items():
            aliases[n_in + sum(c_in[:k]) + a] = n_out + sum(c_out[:k]) + b

    def split(refs, counts):
        out, pos = [], 0
        for cnt in counts:
            out.append(refs[pos:pos + cnt])
            pos += cnt
        return out

    def wrapped(*refs):
        ins = refs[:n_in + sum(c_in)]
        outs = refs[len(ins):len(ins) + n_out + sum(c_out)]
        scr = refs[len(ins) + len(outs):]
        cins, couts, cscr = split(ins[n_in:], c_in), split(outs[n_out:], c_out), split(scr[n_scr:], c_scr)
        if comms:
            first = pl.program_id(0) == 0
            last = pl.program_id(0) == grid[0] - 1
            for k in range(1, len(grid)):
                first = jnp.logical_and(first, pl.program_id(k) == 0)
                last = jnp.logical_and(last, pl.program_id(k) == grid[k] - 1)

            @pl.when(first)
            def _():
                for k, cm in enumerate(comms):
                    cm.start(cins[k], couts[k], cscr[k])
        body(*ins[:n_in], *outs[:n_out], *scr[:n_scr])
        if comms:
            @pl.when(last)
            def _():
                for k, cm in enumerate(comms):
                    cm.finish(cins[k], couts[k], cscr[k])

    outs = _pc(wrapped, name=name, grid=grid,
               in_specs=list(in_specs) + [ANY] * sum(c_in), out_specs=list(out_specs) + [ANY] * sum(c_out),
               out_shape=list(out_shape) + [s for cm in comms for s in cm.out_shapes],
               scratch_shapes=list(scratch_shapes) + [s for cm in comms for s in cm.scratch],
               input_output_aliases=aliases, compiler_params=_params(*sem),
               )(*args, *[a for cm in comms for a in cm.arrays])
    outs = list(outs)
    return outs[:n_out], split(outs[n_out:], c_out)


def _row_block(r):
    for b in (512, 256, 128, 64, 32, 16, 8):
        if r % b == 0:
            return b
    return r


def _pair_add(g, recv, c_idx, name):
    _, r, cc = g.shape
    br = _row_block(r)

    def body(c_ref, g_ref, r_ref, p_ref, pb_ref):
        s = g_ref[...] + r_ref[...]
        p_ref[...] = s
        pb_ref[...] = s.astype(BF16)

    grid_spec = pltpu.PrefetchScalarGridSpec(
        num_scalar_prefetch=1, grid=(N_CHIP, r // br),
        in_specs=[pl.BlockSpec((1, br, cc), lambda j, i, c_ref: (2 * j + c_ref[0], i, 0)),
                  pl.BlockSpec((1, br, cc), lambda j, i, c_ref: (j, i, 0))],
        out_specs=[pl.BlockSpec((1, br, cc), lambda j, i, c_ref: (j, i, 0)),
                   pl.BlockSpec((1, br, cc), lambda j, i, c_ref: (j, i, 0))])
    return _pc(body, name=name, grid_spec=grid_spec,
               out_shape=[_sds((N_CHIP, r, cc)), _sds((N_CHIP, r, cc), BF16)],
               compiler_params=_params("arbitrary", "arbitrary"))(c_idx, g, recv)


def _chip_add(p, q, jc_idx, name):
    _, r, cc = p.shape
    br = _row_block(r)

    def body(jc_ref, p_ref, q_ref, o_ref):
        o_ref[0] = ((p_ref[0] + q_ref[0].astype(F32)) + q_ref[1].astype(F32)) + q_ref[2].astype(F32)

    grid_spec = pltpu.PrefetchScalarGridSpec(
        num_scalar_prefetch=1, grid=(r // br,),
        in_specs=[pl.BlockSpec((1, br, cc), lambda i, jc_ref: (jc_ref[0], i, 0)),
                  pl.BlockSpec((3, br, cc), lambda i, jc_ref: (0, i, 0))],
        out_specs=pl.BlockSpec((1, br, cc), lambda i, jc_ref: (jc_ref[1], i, 0)))
    return _pc(body, name=name, grid_spec=grid_spec, out_shape=_sds((2, r, cc)),
               compiler_params=_params("arbitrary"))(jc_idx, p, q)


def _shard_of(both):
    return both.reshape((2 * both.shape[1],) + both.shape[2:])


def _adamw(w, g, m, v, name, comms=()):
    r, cc = w.shape
    br = _row_block(r)
    if r * cc * 4 <= (1 << 20):
        br = r
    elif br * cc * 4 > (1 << 20) and br > 8:
        br = max(8, (1 << 20) // (cc * 4) // 8 * 8)
        while r % br:
            br -= 8
    c1 = 1.0 - ADAM_B1 ** ADAM_STEP
    c2 = 1.0 - ADAM_B2 ** ADAM_STEP

    def body(w_ref, g_ref, m_ref, v_ref, d_ref, mo_ref, vo_ref):
        gg = g_ref[...]
        mn = ADAM_B1 * m_ref[...] + (1.0 - ADAM_B1) * gg
        vn = ADAM_B2 * v_ref[...] + (1.0 - ADAM_B2) * (gg * gg)
        mh = mn / c1
        vh = vn / c2
        d_ref[...] = -ADAM_LR * (mh / (jnp.sqrt(vh) + ADAM_EPS) + ADAM_WD * w_ref[...])
        mo_ref[...] = mn
        vo_ref[...] = vn

    spec = pl.BlockSpec((br, cc), lambda i: (i, 0))
    outs, couts = _call(body, name=name, grid=(r // br,), in_specs=[spec] * 4, out_specs=[spec] * 3,
                        out_shape=[_sds((r, cc))] * 3, scratch_shapes=[], sem=("arbitrary",), args=(w, g, m, v),
                        comms=comms)
    return (outs, couts) if comms else outs


def _prep(c_all, c_ctx, ret_decay):
    def body(c_ref, cc_ref, rd_ref, a_ref, lg_ref, sg_ref):
        ca = c_ref[...]
        cc = cc_ref[...]
        a_ref[...] = jnp.zeros_like(a_ref)
        a_ref[0:8, :] = ca * _sigmoid(ca)
        a_ref[8:9, :] = cc * _sigmoid(cc)
        rd = rd_ref[...]
        lg_ref[...] = -_softplus(-rd)
        sg_ref[...] = _sigmoid(-rd)

    rd = jnp.broadcast_to(ret_decay.reshape(2, HEADS).T[:, :, None], (HEADS, 2, LANES))
    return _pc(body, name="prep",
               out_shape=[_sds((16, D_MODEL)), _sds((HEADS, 2, LANES)), _sds((HEADS, 2, LANES))],
               in_specs=[_full((8, D_MODEL)), _full((1, D_MODEL)), _full((HEADS, 2, LANES))],
               out_specs=[_full((16, D_MODEL)), _full((HEADS, 2, LANES)), _full((HEADS, 2, LANES))],
               compiler_params=_params())(c_all, c_ctx.reshape(1, D_MODEL), rd)


def _mod_fwd(a16, w_ada, b_shard):
    n = w_ada.shape[1]
    bn = 512

    def body(a_ref, w_ref, b_ref, o_ref):
        o_ref[...] = jnp.dot(a_ref[...], w_ref[...], preferred_element_type=F32,
                             precision=lax.Precision.HIGHEST) + b_ref[...]

    return _pc(body, name="mod_fwd", grid=(n // bn,),
               in_specs=[_full((16, D_MODEL)), pl.BlockSpec((D_MODEL, bn), lambda i: (0, i)),
                         pl.BlockSpec((1, bn), lambda i: (0, i))],
               out_specs=pl.BlockSpec((16, bn), lambda i: (0, i)), out_shape=_sds((16, n)),
               compiler_params=_params("arbitrary"))(a16, w_ada, b_shard)


def _ada_grad(at, b):
    n = b.shape[1]
    bn = 512

    def body(a_ref, b_ref, o_ref):
        o_ref[...] = jnp.dot(a_ref[...], b_ref[...], preferred_element_type=F32, precision=lax.Precision.HIGHEST)

    return _pc(body, name="ada_grad", grid=(n // bn,),
               in_specs=[_full((D_MODEL, LANES)), pl.BlockSpec((LANES, bn), lambda i: (0, i))],
               out_specs=pl.BlockSpec((D_MODEL, bn), lambda i: (0, i)), out_shape=_sds((D_MODEL, n)),
               compiler_params=_params("arbitrary"))(at, b)


def _cctx_partial(dmc8, w_ada):
    n = w_ada.shape[1]
    bn = 512

    def body(d_ref, w_ref, o_ref):
        @pl.when(pl.program_id(0) == 0)
        def _():
            o_ref[...] = jnp.zeros_like(o_ref)
        o_ref[...] += lax.dot_general(d_ref[...], w_ref[...], (((1,), (1,)), ((), ())),
                                      preferred_element_type=F32, precision=lax.Precision.HIGHEST)

    return _pc(body, name="cctx_partial", grid=(n // bn,),
               in_specs=[pl.BlockSpec((8, bn), lambda i: (0, i)), pl.BlockSpec((D_MODEL, bn), lambda i: (0, i))],
               out_specs=_full((8, D_MODEL)), out_shape=_sds((8, D_MODEL)),
               compiler_params=_params("arbitrary"))(dmc8, w_ada)


def _cctx_final(parts, c_ctx, m, v):
    c1 = 1.0 - ADAM_B1 ** ADAM_STEP
    c2 = 1.0 - ADAM_B2 ** ADAM_STEP

    def body(p_ref, c_ref, m_ref, v_ref, g_ref, d_ref, mo_ref, vo_ref):
        s = ((p_ref[0, 0:1, :] + p_ref[2, 0:1, :]) + p_ref[4, 0:1, :]) + p_ref[6, 0:1, :]
        z = c_ref[...]
        sg = _sigmoid(z)
        gg = s * (sg * (1.0 + z * (1.0 - sg)))
        g_ref[...] = gg
        mn = ADAM_B1 * m_ref[...] + (1.0 - ADAM_B1) * gg
        vn = ADAM_B2 * v_ref[...] + (1.0 - ADAM_B2) * (gg * gg)
        d_ref[...] = -ADAM_LR * ((mn / c1) / (jnp.sqrt(vn / c2) + ADAM_EPS) + ADAM_WD * z)
        mo_ref[...] = mn
        vo_ref[...] = vn

    row = _full((1, D_MODEL))
    return _pc(body, name="cctx_final", out_shape=[_sds((1, D_MODEL))] * 4,
               in_specs=[_full(parts.shape), row, row, row], out_specs=[row] * 4,
               compiler_params=_params())(parts, c_ctx.reshape(1, D_MODEL), m.reshape(1, D_MODEL), v.reshape(1, D_MODEL))


def _rotary_tables(t_len):
    rows = t_len // GRID_W
    row = jnp.repeat(jnp.arange(rows, dtype=F32), GRID_W)
    col = jnp.tile(jnp.arange(GRID_W, dtype=F32), rows)
    n_freq = DH // 4
    inv = ROPE_BASE ** (-jnp.arange(n_freq, dtype=F32) / n_freq)
    ang = jnp.concatenate([row[:, None] * inv, col[:, None] * inv], axis=-1)
    cos, sin = jnp.cos(ang), jnp.sin(ang)
    return jnp.concatenate([cos, cos], axis=-1), jnp.concatenate([-sin, sin], axis=-1)


def _inproj_fwd(x, gn, sh, sc, w4, cos2, sin2, name, comms=()):
    t = x.shape[0]
    tm = _tile(t, True)
    nc = IN_COLS // N_CHIP

    def body(x_ref, gn_ref, sh_ref, sc_ref, w_ref, c_ref, s_ref, p_ref, xr_ref, hb_ref, p_s):
        xh, _ = _rms(x_ref[...])
        h = xh * gn_ref[...] * (1.0 + sc_ref[...]) + sh_ref[...]
        hb = h.astype(BF16)
        hb_ref[...] = hb
        for j in range(N_CHIP):
            p_s[:, nc * j:nc * (j + 1)] = _dot(hb, w_ref[j])
        cc = c_ref[...]
        ss = s_ref[...]
        for hh in range(2 * HEADS):
            blk = p_s[:, DH * hh:DH * (hh + 1)]
            rot = blk * cc + pltpu.roll(blk, DH // 2, 1) * ss
            if hh >= HEADS:
                rot = rot * K_SCALE
            p_ref[:, DH * hh:DH * (hh + 1)] = rot.astype(BF16)
        p_ref[:, 2 * RET_W:] = p_s[:, 2 * RET_W:].astype(BF16)
        xr_ref[...] = p_s[:, 4 * RET_W:4 * RET_W + LRU_W]

    row = _full((1, D_MODEL))
    outs, couts = _call(
        body, name=name, grid=(t // tm,),
        in_specs=[pl.BlockSpec((tm, D_MODEL), lambda i: (i, 0)), row, row, row, _full(w4.shape),
                  pl.BlockSpec((tm, DH), lambda i: (i, 0)), pl.BlockSpec((tm, DH), lambda i: (i, 0))],
        out_specs=[pl.BlockSpec((tm, IN_COLS), lambda i: (i, 0)), pl.BlockSpec((tm, LRU_W), lambda i: (i, 0)),
                   pl.BlockSpec((tm, D_MODEL), lambda i: (i, 0))],
        out_shape=[_sds((t, IN_COLS), BF16), _sds((t, LRU_W)), _sds((t, D_MODEL), BF16)],
        scratch_shapes=[pltpu.VMEM((tm, IN_COLS), F32)], sem=("arbitrary",),
        args=(x, gn, sh, sc, w4, cos2, sin2), comms=comms)
    return (outs, couts) if comms else outs


def _inproj_bwd(x, gn, sh, sc, w4, cos2, sin2, pieces, dres, name):
    t = x.shape[0]
    tm = _tile(t)
    nc = IN_COLS // N_CHIP

    def body(x_ref, gn_ref, sh_ref, sc_ref, w_ref, c_ref, s_ref, dqf, dqb, dkf, dkb, dvf, dvb, dg, dxr, dgt, dres_ref,
             dx_ref, dpb_ref, dgn_ref, dsh_ref, dsc_ref):
        cc = c_ref[...]
        ss = s_ref[...]
        dq = dqf[...].astype(F32) + dqb[...].astype(F32)
        dk = dkf[...].astype(F32) + dkb[...].astype(F32)
        for hh in range(HEADS):
            sl = slice(DH * hh, DH * (hh + 1))
            b = dq[:, sl]
            dpb_ref[:, sl] = (b * cc + pltpu.roll(b * ss, DH // 2, 1)).astype(BF16)
            b = dk[:, sl]
            dpb_ref[:, RET_W + DH * hh:RET_W + DH * (hh + 1)] = (
                (b * cc + pltpu.roll(b * ss, DH // 2, 1)) * K_SCALE).astype(BF16)
        dpb_ref[:, 2 * RET_W:3 * RET_W] = (dvf[...].astype(F32) + dvb[...].astype(F32)).astype(BF16)
        dpb_ref[:, 3 * RET_W:4 * RET_W] = dg[...].astype(BF16)
        dpb_ref[:, 4 * RET_W:4 * RET_W + LRU_W] = dxr[...].astype(BF16)
        dpb_ref[:, 4 * RET_W + LRU_W:IN_COLS] = dgt[...].astype(BF16)
        dh = _dot_nt(dpb_ref[:, 0:nc], w_ref[0])
        for j in range(1, N_CHIP):
            dh = dh + _dot_nt(dpb_ref[:, nc * j:nc * (j + 1)], w_ref[j])
        dx, dgn_t, dsh_t, dsc_t = _norm_mod_bwd(x_ref[...], gn_ref[...], sc_ref[...], dh)
        dx_ref[...] = dres_ref[...] + dx

        @pl.when(pl.program_id(0) == 0)
        def _():
            dgn_ref[...] = jnp.zeros_like(dgn_ref)
            dsh_ref[...] = jnp.zeros_like(dsh_ref)
            dsc_ref[...] = jnp.zeros_like(dsc_ref)
        dgn_ref[...] += dgn_t
        dsh_ref[...] += dsh_t
        dsc_ref[...] += dsc_t

    row = _full((1, D_MODEL))
    pc = pl.BlockSpec((tm, RET_W), lambda i: (i, 0))
    big = pl.BlockSpec((tm, D_MODEL), lambda i: (i, 0))
    return _pc(body, name=name, grid=(t // tm,),
               in_specs=[big, row, row, row, _full(w4.shape),
                         pl.BlockSpec((tm, DH), lambda i: (i, 0)), pl.BlockSpec((tm, DH), lambda i: (i, 0))]
               + [pc] * 9 + [big],
               out_specs=[big, pl.BlockSpec((tm, IN_COLS), lambda i: (i, 0)), row, row, row],
               out_shape=[_sds((t, D_MODEL)), _sds((t, IN_COLS), BF16), _sds((1, D_MODEL)), _sds((1, D_MODEL)),
                          _sds((1, D_MODEL))],
               compiler_params=_params("arbitrary"))(x, gn, sh, sc, w4, cos2, sin2, *pieces, dres)


def _halo_specs(t, tm):
    n8 = tm // SUBLANES
    last8 = t // SUBLANES - 1
    prev = pl.BlockSpec((SUBLANES, LRU_W), lambda i: (jnp.maximum(i * n8 - 1, 0), 0))
    main = pl.BlockSpec((tm, LRU_W), lambda i: (i, 0))
    nxt = pl.BlockSpec((SUBLANES, LRU_W), lambda i: (jnp.minimum((i + 1) * n8, last8), 0))
    return prev, main, nxt


def _with_halo(prev_ref, main_ref, next_ref, i, nt):
    prev = jnp.where(i > 0, prev_ref[...], 0.0)
    nxt = jnp.where(i < nt - 1, next_ref[...], 0.0)
    return jnp.concatenate([prev, main_ref[...], nxt], axis=0)


def _conv_fwd(xr, cw, cb, name):
    t = xr.shape[0]
    tm = _tile(t, True)
    nt = t // tm
    n = tm + 2 * SUBLANES
    mid = slice(SUBLANES, SUBLANES + tm)

    def body(p_ref, m_ref, n_ref, w_ref, b_ref, o_ref):
        xp = _with_halo(p_ref, m_ref, n_ref, pl.program_id(0), nt)
        acc = b_ref[...] + pltpu.roll(xp, 1, 0)[mid] * w_ref[0:1, :]
        acc = acc + xp[mid] * w_ref[1:2, :]
        acc = acc + pltpu.roll(xp, n - 1, 0)[mid] * w_ref[2:3, :]
        acc = acc + pltpu.roll(xp, n - 2, 0)[mid] * w_ref[3:4, :]
        o_ref[...] = acc

    return _pc(body, name=name, grid=(nt,),
               in_specs=[*_halo_specs(t, tm), _full((4, LRU_W)), _full((1, LRU_W))],
               out_specs=pl.BlockSpec((tm, LRU_W), lambda i: (i, 0)), out_shape=_sds((t, LRU_W)),
               compiler_params=_params("arbitrary"))(xr, xr, xr, cw, cb)


def _conv_bwd(dxc_a, dxc_b, xr, cw, name):
    t = xr.shape[0]
    tm = _tile(t, True)
    nt = t // tm
    n = tm + 2 * SUBLANES
    mid = slice(SUBLANES, SUBLANES + tm)

    def body(ap_ref, am_ref, an_ref, bp_ref, bm_ref, bn_ref, xp_ref, xm_ref, xn_ref, w_ref, dx_ref, dw_ref, db_ref):
        i = pl.program_id(0)
        dp = _with_halo(ap_ref, am_ref, an_ref, i, nt) + _with_halo(bp_ref, bm_ref, bn_ref, i, nt)
        xp = _with_halo(xp_ref, xm_ref, xn_ref, i, nt)
        dx = pltpu.roll(dp, n - 1, 0)[mid] * w_ref[0:1, :]
        dx = dx + dp[mid] * w_ref[1:2, :]
        dx = dx + pltpu.roll(dp, 1, 0)[mid] * w_ref[2:3, :]
        dx = dx + pltpu.roll(dp, 2, 0)[mid] * w_ref[3:4, :]
        dx_ref[...] = dx.astype(BF16)
        d = dp[mid]

        @pl.when(i == 0)
        def _():
            dw_ref[...] = jnp.zeros_like(dw_ref)
            db_ref[...] = jnp.zeros_like(db_ref)
        dw_ref[0:1, :] += _sum0(d * pltpu.roll(xp, 1, 0)[mid])
        dw_ref[1:2, :] += _sum0(d * xp[mid])
        dw_ref[2:3, :] += _sum0(d * pltpu.roll(xp, n - 1, 0)[mid])
        dw_ref[3:4, :] += _sum0(d * pltpu.roll(xp, n - 2, 0)[mid])
        db_ref[...] += _sum0(d)

    return _pc(body, name=name, grid=(nt,),
               in_specs=[*_halo_specs(t, tm), *_halo_specs(t, tm), *_halo_specs(t, tm), _full((4, LRU_W))],
               out_specs=[pl.BlockSpec((tm, LRU_W), lambda i: (i, 0)), _full((4, LRU_W)), _full((1, LRU_W))],
               out_shape=[_sds((t, LRU_W), BF16), _sds((4, LRU_W)), _sds((1, LRU_W))],
               compiler_params=_params("arbitrary"))(dxc_a, dxc_a, dxc_a, dxc_b, dxc_b, dxc_b, xr, xr, xr, cw)


def _local_scan(a, b, reverse):
    n = a.shape[0]
    row = lax.broadcasted_iota(jnp.int32, a.shape, 0) & (SUBLANES - 1)
    for s in (1, 2, 4):
        if reverse:
            a_s, b_s, ok = pltpu.roll(a, n - s, 0), pltpu.roll(b, n - s, 0), row < SUBLANES - s
        else:
            a_s, b_s, ok = pltpu.roll(a, s, 0), pltpu.roll(b, s, 0), row >= s
        b = a * jnp.where(ok, b_s, 0.0) + b
        a = a * jnp.where(ok, a_s, 1.0)
    return a, b


def _carry_scan(a_s, b_s, out_ref, carry, reverse):
    ng = a_s.shape[0] // SUBLANES
    shape = carry.shape

    def step(g, cr):
        gg = (ng - 1 - g) if reverse else g
        off = pl.multiple_of(gg * SUBLANES, SUBLANES)
        h = a_s[pl.ds(off, SUBLANES), :] * cr + b_s[pl.ds(off, SUBLANES), :]
        out_ref[pl.ds(off, SUBLANES), :] = h
        edge = h[0:1, :] if reverse else h[SUBLANES - 1:SUBLANES, :]
        return jnp.broadcast_to(edge, shape)

    return lax.fori_loop(0, ng, step, carry)


def _lru_gates(xc, wa_ref, wx_ref, ba, bx, lam):
    xb = xc.astype(BF16)
    r = _sigmoid(_dot(xb, wa_ref[...]) + ba)
    ig = _sigmoid(_dot(xb, wx_ref[...]) + bx)
    sp = _softplus(-lam)
    la = -LRU_C * r * sp
    a = jnp.exp(la)
    mult = jnp.sqrt(_neg_expm1(2.0 * la, a * a))
    return r, ig, sp, a, mult


def _lru_fwd(xc, wa, wx, ba, bx, lam, h0, reverse, name, comms=()):
    t = xc.shape[0]
    tm = _tile(t, True)
    nt = t // tm
    tidx = (lambda i: (nt - 1 - i, 0)) if reverse else (lambda i: (i, 0))

    def body(x_ref, wa_ref, wx_ref, ba_ref, bx_ref, lam_ref, h0_ref, h_ref, a_s, b_s, c_s):
        @pl.when(pl.program_id(0) == 0)
        def _():
            c_s[...] = jnp.broadcast_to(h0_ref[...], c_s.shape)
        xv = x_ref[...]
        _, ig, _, a, mult = _lru_gates(xv, wa_ref, wx_ref, ba_ref[...], bx_ref[...], lam_ref[...])
        al, bl = _local_scan(a, mult * (ig * xv), reverse)
        a_s[...] = al
        b_s[...] = bl
        c_s[...] = _carry_scan(a_s, b_s, h_ref, c_s[...], reverse)

    vec = _full((1, LRU_W))
    mat = _full((LRU_W, LRU_W))
    (h,), couts = _call(body, name=name, grid=(nt,),
                        in_specs=[pl.BlockSpec((tm, LRU_W), tidx), mat, mat, vec, vec, vec, vec],
                        out_specs=[pl.BlockSpec((tm, LRU_W), tidx)], out_shape=[_sds((t, LRU_W))],
                        scratch_shapes=[pltpu.VMEM((tm, LRU_W), F32), pltpu.VMEM((tm, LRU_W), F32),
                                        pltpu.VMEM((SUBLANES, LRU_W), F32)],
                        sem=("arbitrary",), args=(xc, wa, wx, ba, bx, lam, h0), comms=comms)
    return (h, couts) if comms else h


def _lru_bwd(xc, wa, wx, ba, bx, lam, h, h0, dh, reverse, name, comms=()):
    t = xc.shape[0]
    tm = _tile(t, True)
    nt = t // tm
    n8 = tm // SUBLANES
    last8 = t // SUBLANES - 1
    tidx = (lambda i: (i, 0)) if reverse else (lambda i: (nt - 1 - i, 0))
    if reverse:
        halo = pl.BlockSpec((SUBLANES, LRU_W), lambda i: (jnp.minimum((i + 1) * n8, last8), 0))
    else:
        halo = pl.BlockSpec((SUBLANES, LRU_W), lambda i: (jnp.maximum((nt - 1 - i) * n8 - 1, 0), 0))

    def body(x_ref, wa_ref, wx_ref, ba_ref, bx_ref, lam_ref, h_ref, halo_ref, h0_ref, dh_ref,
             dx_ref, dpre_ref, dba_ref, dbx_ref, dlam_ref, dh0_ref, a_s, b_s, l_s, c_s, e_s):
        i = pl.program_id(0)

        @pl.when(i == 0)
        def _():
            c_s[...] = jnp.zeros_like(c_s)
            e_s[...] = jnp.zeros_like(e_s)
            dba_ref[...] = jnp.zeros_like(dba_ref)
            dbx_ref[...] = jnp.zeros_like(dbx_ref)
            dlam_ref[...] = jnp.zeros_like(dlam_ref)
        xv = x_ref[...]
        lam = lam_ref[...]
        r, ig, sp, a, mult = _lru_gates(xv, wa_ref, wx_ref, ba_ref[...], bx_ref[...], lam)
        hv = h_ref[...]
        rowi = lax.broadcasted_iota(jnp.int32, (tm, LRU_W), 0)
        edge_a = jnp.broadcast_to(e_s[0:1, :], (tm, LRU_W))
        h0b = jnp.broadcast_to(h0_ref[...], (tm, LRU_W))
        if reverse:
            a_sh = jnp.where(rowi == 0, edge_a, pltpu.roll(a, 1, 0))
            hin_edge = jnp.where(i == nt - 1, h0b, jnp.broadcast_to(halo_ref[0:1, :], (tm, LRU_W)))
            h_in = jnp.where(rowi == tm - 1, hin_edge, pltpu.roll(hv, tm - 1, 0))
        else:
            a_sh = jnp.where(rowi == tm - 1, edge_a, pltpu.roll(a, tm - 1, 0))
            hin_edge = jnp.where(i == nt - 1, h0b, jnp.broadcast_to(halo_ref[SUBLANES - 1:SUBLANES, :], (tm, LRU_W)))
            h_in = jnp.where(rowi == 0, hin_edge, pltpu.roll(hv, 1, 0))
        al, bl = _local_scan(a_sh, dh_ref[...], not reverse)
        a_s[...] = al
        b_s[...] = bl
        c_s[...] = _carry_scan(a_s, b_s, l_s, c_s[...], not reverse)
        e_s[...] = jnp.broadcast_to(a[tm - 1:tm, :] if reverse else a[0:1, :], e_s.shape)
        lmb = l_s[...]
        da = lmb * h_in
        ixc = ig * xv
        dmult = lmb * ixc
        dixc = lmb * mult
        dla = da * a - dmult * (a * a) / mult
        dpr = dla * (-LRU_C * sp) * r * (1.0 - r)
        dpi = dixc * xv * ig * (1.0 - ig)
        dprb = dpr.astype(BF16)
        dpib = dpi.astype(BF16)
        dpre_ref[:, 0:LRU_W] = dprb
        dpre_ref[:, LRU_W:2 * LRU_W] = dpib
        dx_ref[...] = dixc * ig + _dot_nt(dprb, wa_ref[...]) + _dot_nt(dpib, wx_ref[...])
        dba_ref[...] += _sum0(dpr)
        dbx_ref[...] += _sum0(dpi)
        dlam_ref[...] += _sum0(dla * (-LRU_C * r)) * (-_sigmoid(-lam))

        @pl.when(i == nt - 1)
        def _():
            al0 = a * lmb
            dh0_ref[...] = al0[tm - 1:tm, :] if reverse else al0[0:1, :]

    vec = _full((1, LRU_W))
    mat = _full((LRU_W, LRU_W))
    tile = pl.BlockSpec((tm, LRU_W), tidx)
    return _call(body, name=name, grid=(nt,),
                 in_specs=[tile, mat, mat, vec, vec, vec, tile, halo, vec, tile],
                 out_specs=[tile, pl.BlockSpec((tm, 2 * LRU_W), tidx), vec, vec, vec, vec],
                 out_shape=[_sds((t, LRU_W)), _sds((t, 2 * LRU_W), BF16), _sds((1, LRU_W)), _sds((1, LRU_W)),
                            _sds((1, LRU_W)), _sds((1, LRU_W))],
                 scratch_shapes=[pltpu.VMEM((tm, LRU_W), F32), pltpu.VMEM((tm, LRU_W), F32),
                                 pltpu.VMEM((tm, LRU_W), F32), pltpu.VMEM((SUBLANES, LRU_W), F32),
                                 pltpu.VMEM((SUBLANES, LRU_W), F32)],
                 sem=("arbitrary",), args=(xc, wa, wx, ba, bx, lam, h, h, h0, dh), comms=comms)


def _decay_tables(lg, reverse):
    ci = lax.broadcasted_iota(jnp.int32, (CHUNK, CHUNK), 0).astype(F32)
    mi = lax.broadcasted_iota(jnp.int32, (CHUNK, CHUNK), 1).astype(F32)
    rel = (mi - ci) if reverse else (ci - mi)
    relc = jnp.maximum(rel, 0.0)
    lg_c = jnp.concatenate([lg] * (CHUNK // LANES), axis=1)
    dm = jnp.where(rel >= 0, jnp.exp(lg_c * relc), 0.0)
    cd = lax.broadcasted_iota(jnp.int32, (CHUNK, DH), 0).astype(F32)
    pq, ps = (CHUNK - cd, cd) if reverse else (cd + 1.0, CHUNK - 1.0 - cd)
    return relc, dm, jnp.exp(lg * pq), jnp.exp(lg * ps), jnp.exp(lg * float(CHUNK)), pq, ps


def _ret_fwd(proj, lgv, s0f, s0b, comms=()):
    t = proj.shape[0]
    n = t // CHUNK

    def one(q, k, v, lg, s_s, hh, o_ref, sp_ref, reverse):
        _, dm, wq, ws, g, _, _ = _decay_tables(lg, reverse)
        vb = v.astype(BF16)
        p = _dot_nt(q.astype(BF16), k.astype(BF16)) * dm
        s = s_s[hh]
        sp_ref[hh, 0] = s
        o_ref[:, DH * hh:DH * (hh + 1)] = _dot(p.astype(BF16), vb) + _dot((q * wq).astype(BF16), s.astype(BF16))
        s_s[hh] = g * s + _dot_tn((k * ws).astype(BF16), vb)

    def body(qf, kf, vf, qb, kb, vb, lg_ref, s0f_ref, s0b_ref, of_ref, ob_ref, spf_ref, spb_ref, sf_s, sb_s):
        @pl.when(pl.program_id(0) == 0)
        def _():
            sf_s[...] = s0f_ref[...]
            sb_s[...] = s0b_ref[...]
        for hh in range(HEADS):
            sl = slice(DH * hh, DH * (hh + 1))
            one(qf[:, sl].astype(F32), kf[:, sl].astype(F32), vf[:, sl], lg_ref[hh, 0:1, :], sf_s, hh, of_ref, spf_ref,
                False)
            one(qb[:, sl].astype(F32), kb[:, sl].astype(F32), vb[:, sl], lg_ref[hh, 1:2, :], sb_s, hh, ob_ref, spb_ref,
                True)

    blk = (CHUNK, RET_W)
    fw = [pl.BlockSpec(blk, lambda i, o=o: (i, o)) for o in range(3)]
    bw = [pl.BlockSpec(blk, lambda i, o=o: (n - 1 - i, o)) for o in range(3)]
    st = _full((HEADS, DH, DH))
    return _call(body, name="ret_fwd", grid=(n,),
                 in_specs=fw + bw + [_full((HEADS, 2, LANES)), st, st],
                 out_specs=[pl.BlockSpec(blk, lambda i: (i, 0)), pl.BlockSpec(blk, lambda i: (n - 1 - i, 0)),
                            pl.BlockSpec((HEADS, 1, DH, DH), lambda i: (0, i, 0, 0)),
                            pl.BlockSpec((HEADS, 1, DH, DH), lambda i: (0, n - 1 - i, 0, 0))],
                 out_shape=[_sds((t, RET_W)), _sds((t, RET_W)), _sds((HEADS, n, DH, DH)), _sds((HEADS, n, DH, DH))],
                 scratch_shapes=[pltpu.VMEM((HEADS, DH, DH), F32), pltpu.VMEM((HEADS, DH, DH), F32)],
                 sem=("arbitrary",), args=(proj, proj, proj, proj, proj, proj, lgv, s0f, s0b), comms=comms)


def _ret_bwd(proj, lgv, sgv, sprev, do, reverse, name, comms=()):
    t = proj.shape[0]
    n = t // CHUNK
    d = 1 if reverse else 0
    cidx = (lambda i: i) if reverse else (lambda i: n - 1 - i)

    def body(q_ref, k_ref, v_ref, lg_ref, sg_ref, s_ref, do_ref, dq_ref, dk_ref, dv_ref, ds0_ref, drd_ref, ds_s, acc_s):
        i = pl.program_id(0)

        @pl.when(i == 0)
        def _():
            ds_s[...] = jnp.zeros_like(ds_s)
            acc_s[...] = jnp.zeros_like(acc_s)
        for hh in range(HEADS):
            sl = slice(DH * hh, DH * (hh + 1))
            relc, dm, wq, ws, g, pq, ps = _decay_tables(lg_ref[hh, d:d + 1, :], reverse)
            qb, kb, vb = q_ref[:, sl], k_ref[:, sl], v_ref[:, sl]
            q, k = qb.astype(F32), kb.astype(F32)
            p = _dot_nt(qb, kb) * dm
            s = s_ref[hh, 0]
            dob = do_ref[:, sl].astype(BF16)
            dsn = ds_s[hh]
            dsb = dsn.astype(BF16)
            dv_ref[:, sl] = (_dot_tn(p.astype(BF16), dob) + _dot((k * ws).astype(BF16), dsb)).astype(BF16)
            dp = _dot_nt(dob, vb)
            dab = (dp * dm).astype(BF16)
            xq = _dot_nt(dob, s.astype(BF16))
            yk = _dot_nt(vb, dsb)
            dq_ref[:, sl] = (_dot(dab, kb) + xq * wq).astype(BF16)
            dk_ref[:, sl] = (_dot_tn(dab, qb) + yk * ws).astype(BF16)
            ds_s[hh] = g * dsn + _dot_tn((q * wq).astype(BF16), dob)
            s_mask = _sum0(dp * p * relc)
            part = (sum(s_mask[:, LANES * u:LANES * (u + 1)] for u in range(CHUNK // LANES))
                    + _sum0(xq * q * wq * pq) + _sum0(yk * k * ws * ps) + _sum0(dsn * s) * g * float(CHUNK))
            acc_s[hh] += jnp.broadcast_to(part, (SUBLANES, LANES))

        @pl.when(i == n - 1)
        def _():
            ds0_ref[...] = ds_s[...]
            for hh in range(HEADS):
                tot = jnp.sum(acc_s[hh, 0:1, :], axis=1, keepdims=True)
                drd_ref[hh] = jnp.broadcast_to(tot, (SUBLANES, LANES)) * sg_ref[hh, d:d + 1, :]

    blk = (CHUNK, RET_W)
    qkv = [pl.BlockSpec(blk, lambda i, o=o: (cidx(i), o)) for o in range(3)]
    hc = pl.BlockSpec(blk, lambda i: (cidx(i), 0))
    lane = _full((HEADS, 2, LANES))
    return _call(body, name=name, grid=(n,),
                 in_specs=qkv + [lane, lane, pl.BlockSpec((HEADS, 1, DH, DH), lambda i: (0, cidx(i), 0, 0)), hc],
                 out_specs=[hc, hc, hc, _full((HEADS, DH, DH)), _full((HEADS, SUBLANES, LANES))],
                 out_shape=[_sds((t, RET_W), BF16)] * 3 + [_sds((HEADS, DH, DH)), _sds((HEADS, SUBLANES, LANES))],
                 scratch_shapes=[pltpu.VMEM((HEADS, DH, DH), F32), pltpu.VMEM((HEADS, SUBLANES, LANES), F32)],
                 sem=("arbitrary",), args=(proj, proj, proj, lgv, sgv, sprev, do), comms=comms)


def _ctx_weights(lg, l_len, reverse):
    pos = lax.broadcasted_iota(jnp.int32, (l_len, DH), 0).astype(F32)
    steps = pos if reverse else (l_len - 1.0 - pos)
    return jnp.exp(lg * steps), steps


def _ctx_state_fwd(projc, lgv):
    l_len = projc.shape[0]

    def body(k_ref, v_ref, lg_ref, sf_ref, sb_ref):
        k = k_ref[...]
        vb = v_ref[...].astype(BF16)
        for d, o_ref in ((0, sf_ref), (1, sb_ref)):
            w, _ = _ctx_weights(lg_ref[0, d:d + 1, :], l_len, d == 1)
            o_ref[0] = _dot_tn((k * w).astype(BF16), vb)

    st = pl.BlockSpec((1, DH, DH), lambda h: (h, 0, 0))
    return _pc(body, name="ctx_state_fwd", grid=(HEADS,),
               in_specs=[pl.BlockSpec((l_len, DH), lambda h: (0, HEADS + h)),
                         pl.BlockSpec((l_len, DH), lambda h: (0, 2 * HEADS + h)),
                         pl.BlockSpec((1, 2, LANES), lambda h: (h, 0, 0))],
               out_specs=[st, st], out_shape=[_sds((HEADS, DH, DH))] * 2,
               compiler_params=_params("arbitrary"))(projc, projc, lgv)


def _ctx_state_bwd(projc, lgv, sgv, dsf, dsb):
    l_len = projc.shape[0]

    def body(k_ref, v_ref, lg_ref, sg_ref, dsf_ref, dsb_ref, dk_ref, dv_ref, drd_ref):
        k = k_ref[...]
        vb = v_ref[...].astype(BF16)
        dk = jnp.zeros((l_len, DH), F32)
        dv = jnp.zeros((l_len, DH), F32)
        rows = []
        for d, ds_ref in ((0, dsf_ref), (1, dsb_ref)):
            w, steps = _ctx_weights(lg_ref[0, d:d + 1, :], l_len, d == 1)
            dsb16 = ds_ref[0].astype(BF16)
            dkw = _dot_nt(vb, dsb16)
            dk = dk + dkw * w
            dv = dv + _dot((k * w).astype(BF16), dsb16)
            tot = jnp.sum(_sum0(dkw * k * w * steps), axis=1, keepdims=True)
            rows.append(jnp.broadcast_to(tot, (1, LANES)) * sg_ref[0, d:d + 1, :])
        dk_ref[...] = dk.astype(BF16)
        dv_ref[...] = dv.astype(BF16)
        rid = lax.broadcasted_iota(jnp.int32, (SUBLANES, LANES), 0)
        drd_ref[0] = jnp.where(rid == 0, rows[0], jnp.where(rid == 1, rows[1], 0.0))

    st = pl.BlockSpec((1, DH, DH), lambda h: (h, 0, 0))
    lane = pl.BlockSpec((1, 2, LANES), lambda h: (h, 0, 0))
    hc = pl.BlockSpec((l_len, DH), lambda h: (0, h))
    return _pc(body, name="ctx_state_bwd", grid=(HEADS,),
               in_specs=[pl.BlockSpec((l_len, DH), lambda h: (0, HEADS + h)),
                         pl.BlockSpec((l_len, DH), lambda h: (0, 2 * HEADS + h)), lane, lane, st, st],
               out_specs=[hc, hc, pl.BlockSpec((1, SUBLANES, LANES), lambda h: (h, 0, 0))],
               out_shape=[_sds((l_len, RET_W), BF16), _sds((l_len, RET_W), BF16), _sds((HEADS, SUBLANES, LANES))],
               compiler_params=_params("arbitrary"))(projc, projc, lgv, sgv, dsf, dsb)


G_BLOCK = (3 * RET_W) // RET_W
GATE_BLOCK = (4 * RET_W + LRU_W) // LRU_W


def _head_norm(y):
    yc = y - jnp.mean(y, axis=-1, keepdims=True)
    rs = lax.rsqrt(jnp.mean(yc * yc, axis=-1, keepdims=True) + EPS)
    return yc * rs, rs


def _gelu_parts(z):
    th = jnp.tanh(GELU_K * (z + GELU_C * z * z * z))
    return 0.5 * z * (1.0 + th), th


def _mix_fwd(o_f, o_b, proj, hf, hb, w_out, x, g1):
    t = x.shape[0]
    tm = _tile(t, True)

    def body(of_ref, ob_ref, g_ref, gt_ref, hf_ref, hb_ref, w_ref, x_ref, g1_ref, x1_ref, cat_ref):
        o = of_ref[...] + ob_ref[...]
        g = g_ref[...].astype(F32)
        for hh in range(HEADS):
            sl = slice(DH * hh, DH * (hh + 1))
            nrm, _ = _head_norm(o[:, sl])
            gh = g[:, sl]
            cat_ref[:, sl] = (gh * _sigmoid(gh) * nrm).astype(BF16)
        gel, _ = _gelu_parts(gt_ref[...].astype(F32))
        cat_ref[:, RET_W:] = ((hf_ref[...] + hb_ref[...]) * gel).astype(BF16)
        x1_ref[...] = x_ref[...] + g1_ref[...] * _dot(cat_ref[...], w_ref[...])

    half = pl.BlockSpec((tm, RET_W), lambda i: (i, 0))
    big = pl.BlockSpec((tm, D_MODEL), lambda i: (i, 0))
    return _pc(body, name="mix_fwd", grid=(t // tm,),
               in_specs=[half, half, pl.BlockSpec((tm, RET_W), lambda i: (i, G_BLOCK)),
                         pl.BlockSpec((tm, LRU_W), lambda i: (i, GATE_BLOCK)), half, half,
                         _full((D_MODEL, D_MODEL)), big, _full((1, D_MODEL))],
               out_specs=[big, big], out_shape=[_sds((t, D_MODEL)), _sds((t, D_MODEL), BF16)],
               compiler_params=_params("arbitrary"))(o_f, o_b, proj, proj, hf, hb, w_out, x, g1)


def _mix_bwd(o_f, o_b, proj, hf, hb, w_out, cat, dx1, g1, comms=()):
    t = dx1.shape[0]
    tm = _tile(t, True)

    def body(of_ref, ob_ref, g_ref, gt_ref, hf_ref, hb_ref, w_ref, cat_ref, dx1_ref, g1_ref,
             do_ref, dhs_ref, dg_ref, dgt_ref, dyb_ref, dg1_ref):
        dx1v = dx1_ref[...]
        y = _dot(cat_ref[...], w_ref[...])

        @pl.when(pl.program_id(0) == 0)
        def _():
            dg1_ref[...] = jnp.zeros_like(dg1_ref)
        dg1_ref[...] += _sum0(dx1v * y)
        dyb = (g1_ref[...] * dx1v).astype(BF16)
        dyb_ref[...] = dyb
        dcat = _dot_nt(dyb, w_ref[...])
        o = of_ref[...] + ob_ref[...]
        g = g_ref[...].astype(F32)
        for hh in range(HEADS):
            sl = slice(DH * hh, DH * (hh + 1))
            nrm, rs = _head_norm(o[:, sl])
            gh = g[:, sl]
            sg = _sigmoid(gh)
            dret = dcat[:, sl]
            dg_ref[:, sl] = (dret * nrm * (sg * (1.0 + gh * (1.0 - sg)))).astype(BF16)
            dn = dret * (gh * sg)
            dyc = rs * (dn - nrm * jnp.mean(dn * nrm, axis=-1, keepdims=True))
            do_ref[:, sl] = (dyc - jnp.mean(dyc, axis=-1, keepdims=True)).astype(BF16)
        z = gt_ref[...].astype(F32)
        gel, th = _gelu_parts(z)
        dlru = dcat[:, RET_W:]
        dhs_ref[...] = dlru * gel
        dgel = 0.5 * (1.0 + th) + 0.5 * z * (1.0 - th * th) * GELU_K * (1.0 + 3.0 * GELU_C * z * z)
        dgt_ref[...] = (dlru * (hf_ref[...] + hb_ref[...]) * dgel).astype(BF16)

    half = pl.BlockSpec((tm, RET_W), lambda i: (i, 0))
    big = pl.BlockSpec((tm, D_MODEL), lambda i: (i, 0))
    return _call(body, name="mix_bwd", grid=(t // tm,),
                 in_specs=[half, half, pl.BlockSpec((tm, RET_W), lambda i: (i, G_BLOCK)),
                           pl.BlockSpec((tm, LRU_W), lambda i: (i, GATE_BLOCK)), half, half,
                           _full((D_MODEL, D_MODEL)), big, big, _full((1, D_MODEL))],
                 out_specs=[half, half, half, half, big, _full((1, D_MODEL))],
                 out_shape=[_sds((t, RET_W), BF16), _sds((t, RET_W)), _sds((t, RET_W), BF16), _sds((t, RET_W), BF16),
                            _sds((t, D_MODEL), BF16), _sds((1, D_MODEL))],
                 scratch_shapes=[], sem=("arbitrary",), args=(o_f, o_b, proj, proj, hf, hb, w_out, cat, dx1, g1),
                 comms=comms)


def _mlp(x1, n2g, sh2, sc2, g2, fg, w1_parts, w2_parts, tgt):
    t = x1.shape[0]
    tm = _tile(t)
    hb_ = MLP_H // N_CHIP
    q_rows = hb_ // 4
    n_cp = 4 * N_DEV

    def body(x1_ref, n2g_ref, sh2_ref, sc2_ref, g2_ref, fg_ref, w1a, w1b, w2a, w2b, tgt_ref,
             dx1_ref, h2b_ref, ab_ref, dub_ref, dmb_ref, dsc_ref, dsh_ref, dg2_ref, dn2_ref, dfg_ref, loss_ref,
             w1_s, w2_s, r_s, sems):
        @pl.when(pl.program_id(0) == 0)
        def _():
            cps = []
            for p, parts in enumerate(((w1a, w2a), (w1b, w2b))):
                for d in range(N_DEV):
                    rows = pl.ds(2 * q_rows * (d % 2) + q_rows * p, q_rows)
                    for src, dst in zip(parts, (w1_s, w2_s)):
                        cps.append(pltpu.make_async_copy(src.at[d], dst.at[d // 2, rows], sems.at[len(cps)]))
            for cp in cps:
                cp.start()
            for r in (dsc_ref, dsh_ref, dg2_ref, dn2_ref, dfg_ref, loss_ref):
                r[...] = jnp.zeros_like(r)
            for cp in cps:
                cp.wait()
        x1v = x1_ref[...]
        n2g, sc2, g2, fg = n2g_ref[...], sc2_ref[...], g2_ref[...], fg_ref[...]
        xh, _ = _rms(x1v)
        h2b = (xh * n2g * (1.0 + sc2) + sh2_ref[...]).astype(BF16)
        h2b_ref[...] = h2b
        m = jnp.zeros((tm, D_MODEL), F32)
        for j in range(N_CHIP):
            sl = slice(hb_ * j, hb_ * (j + 1))
            r = jnp.maximum(_dot(h2b, w1_s[j]), 0.0)
            r_s[:, sl] = r
            ab = (r * r).astype(BF16)
            ab_ref[:, sl] = ab
            m = m + _dot(ab, w2_s[j])
        x2 = x1v + g2 * m
        x2h, r2 = _rms(x2)
        err = x2h * fg - tgt_ref[...]
        loss_ref[...] += _sum0(err * err)
        dout = err * (1.0 / D_MODEL)
        dfg_ref[...] += _sum0(dout * x2h)
        dxh = dout * fg
        dx2 = r2 * (dxh - x2h * jnp.mean(dxh * x2h, axis=-1, keepdims=True))
        dg2_ref[...] += _sum0(dx2 * m)
        dmb = (g2 * dx2).astype(BF16)
        dmb_ref[...] = dmb
        dh2 = jnp.zeros((tm, D_MODEL), F32)
        for j in range(N_CHIP):
            sl = slice(hb_ * j, hb_ * (j + 1))
            dub = (_dot_nt(dmb, w2_s[j]) * (2.0 * r_s[:, sl])).astype(BF16)
            dub_ref[:, sl] = dub
            dh2 = dh2 + _dot_nt(dub, w1_s[j])
        dx, dn2_t, dsh_t, dsc_t = _norm_mod_bwd(x1v, n2g, sc2, dh2)
        dx1_ref[...] = dx2 + dx
        dn2_ref[...] += dn2_t
        dsh_ref[...] += dsh_t
        dsc_ref[...] += dsc_t

        @pl.when(pl.program_id(0) == t // tm - 1)
        def _():
            tot = jnp.sum(loss_ref[...], axis=1, keepdims=True) * (0.5 / D_MODEL)
            loss_ref[...] = jnp.broadcast_to(tot, loss_ref.shape)

    row = _full((1, D_MODEL))
    big = pl.BlockSpec((tm, D_MODEL), lambda i: (i, 0))
    wide = pl.BlockSpec((tm, MLP_H), lambda i: (i, 0))
    return _pc(body, name="mlp", grid=(t // tm,),
               in_specs=[big, row, row, row, row, row, ANY, ANY, ANY, ANY, big],
               out_specs=[big, big, wide, wide, big, row, row, row, row, row, row],
               out_shape=[_sds((t, D_MODEL)), _sds((t, D_MODEL), BF16), _sds((t, MLP_H), BF16), _sds((t, MLP_H), BF16),
                          _sds((t, D_MODEL), BF16)] + [_sds((1, D_MODEL))] * 6,
               scratch_shapes=[pltpu.VMEM((N_CHIP, D_MODEL, hb_), BF16), pltpu.VMEM((N_CHIP, hb_, D_MODEL), BF16),
                               pltpu.VMEM((tm, MLP_H), F32), pltpu.SemaphoreType.DMA((n_cp,))],
               compiler_params=_params("arbitrary"))(x1, n2g, sh2, sc2, g2, fg, *w1_parts, *w2_parts, tgt)


def _tn(a, b, nj, a_blocked, b_blocked, name, extra=None, comms=()):
    t = a.shape[0]
    m = a.shape[1] // (nj if a_blocked else 1)
    n = b.shape[1] // (nj if b_blocked else 1)
    bk = 1024 if t % 1024 == 0 else (512 if t % 512 == 0 else t)
    nk = t // bk
    a_map = (lambda j, k: (k, j)) if a_blocked else (lambda j, k: (k, 0))
    b_map = (lambda j, k: (k, j)) if b_blocked else (lambda j, k: (k, 0))
    in_specs = [pl.BlockSpec((bk, m), a_map), pl.BlockSpec((bk, n), b_map)]
    args = [a, b]
    if extra is not None:
        a2, b2 = extra
        t2 = a2.shape[0]
        in_specs += [pl.BlockSpec((t2, m), (lambda j, k: (0, j)) if a_blocked else (lambda j, k: (0, 0))),
                     pl.BlockSpec((t2, n), (lambda j, k: (0, j)) if b_blocked else (lambda j, k: (0, 0)))]
        args += [a2, b2]

    def body(*refs):
        a_ref, b_ref = refs[0], refs[1]
        o_ref, acc = refs[-2], refs[-1]
        k = pl.program_id(1)

        @pl.when(k == 0)
        def _():
            acc[...] = jnp.zeros_like(acc)
        acc[...] += _dot_tn(a_ref[...].astype(BF16), b_ref[...].astype(BF16))

        @pl.when(k == nk - 1)
        def _():
            if extra is not None:
                acc[...] += _dot_tn(refs[2][...].astype(BF16), refs[3][...].astype(BF16))
            o_ref[0] = acc[...]

    (out,), couts = _call(body, name=name, grid=(nj, nk), in_specs=in_specs,
                          out_specs=[pl.BlockSpec((1, m, n), lambda j, k: (j, 0, 0))], out_shape=[_sds((nj, m, n))],
                          scratch_shapes=[pltpu.VMEM((m, n), F32)], sem=("arbitrary", "arbitrary"), args=args,
                          comms=comms)
    return (out, couts) if comms else out


ROW_LOSS = 0
ROW_DMOD = 1
ROW_DMODC = 7
ROW_N1, ROW_N2, ROW_FG, ROW_CB = 9, 10, 11, 12
ROW_BA, ROW_BX, ROW_LAM = 13, 15, 17
ROW_CW = 20
ROW_RD = 24
SLAB_ROWS = 32
SEG = D_MODEL // 2


def _pack_small(rows, drd, cw2, cb2, lru2, gates):
    n_rows, n_lru = len(rows), len(lru2)

    def body(*refs):
        r = refs[:n_rows]
        drd_f, drd_b, drd_c, cw_a, cw_b, cb_a, cb_b = refs[n_rows:n_rows + 7]
        lru = refs[n_rows + 7:n_rows + 7 + n_lru]
        gf_ref, gb_ref, slab, ga, gx = refs[n_rows + 7 + n_lru:]
        slab[...] = jnp.zeros_like(slab)
        slab[ROW_LOSS:ROW_LOSS + 1, :] = r[0][...]
        for k in range(N_MOD):
            slab[ROW_DMOD + k:ROW_DMOD + k + 1, :] = r[1 + k][...]
        slab[ROW_DMODC:ROW_DMODC + 1, :] = r[7][...]
        slab[ROW_DMODC + 1:ROW_DMODC + 2, :] = r[8][...]
        slab[ROW_N1:ROW_N1 + 1, :] = r[9][...] + r[10][...]
        slab[ROW_N2:ROW_N2 + 1, :] = r[11][...]
        slab[ROW_FG:ROW_FG + 1, :] = r[12][...]
        slab[ROW_CB:ROW_CB + 1, 0:LRU_W] = cb_a[...] + cb_b[...]
        for k, row in enumerate((ROW_BA, ROW_BA + 1, ROW_BX, ROW_BX + 1, ROW_LAM, ROW_LAM + 1)):
            slab[row:row + 1, 0:LRU_W] = lru[2 * k][...] + lru[2 * k + 1][...]
        slab[ROW_CW:ROW_CW + 4, 0:LRU_W] = cw_a[...] + cw_b[...]
        for h in range(HEADS):
            slab[ROW_RD + h:ROW_RD + h + 1, 0:LANES] = drd_f[h, 0:1, :] + drd_c[h, 0:1, :]
            slab[ROW_RD + HEADS + h:ROW_RD + HEADS + h + 1, 0:LANES] = drd_b[h, 0:1, :] + drd_c[h, 1:2, :]
        for d, g_ref in enumerate((gf_ref, gb_ref)):
            for n in range(LRU_BLOCKS):
                blk = slice(LRU_BD * n, LRU_BD * (n + 1))
                ga[blk, LRU_BD * d:LRU_BD * (d + 1)] = g_ref[0, blk, blk].astype(BF16)
                gx[blk, LRU_BD * d:LRU_BD * (d + 1)] = g_ref[1, blk, blk].astype(BF16)

    args = list(rows) + list(drd) + list(cw2) + list(cb2) + list(lru2) + list(gates)
    gate_shape = (LRU_W, 2 * LRU_BD)
    return _pc(body, name="pack_small", in_specs=[_full(a.shape) for a in args],
               out_specs=[_full((SLAB_ROWS, D_MODEL)), _full(gate_shape), _full(gate_shape)],
               out_shape=[_sds((SLAB_ROWS, D_MODEL)), _sds(gate_shape, BF16), _sds(gate_shape, BF16)],
               compiler_params=_params())(*args)


def _adam_math(w, g, m, v):
    mn = ADAM_B1 * m + (1.0 - ADAM_B1) * g
    vn = ADAM_B2 * v + (1.0 - ADAM_B2) * (g * g)
    mh = mn / (1.0 - ADAM_B1 ** ADAM_STEP)
    vh = vn / (1.0 - ADAM_B2 ** ADAM_STEP)
    return -ADAM_LR * (mh / (jnp.sqrt(vh) + ADAM_EPS) + ADAM_WD * w), mn, vn


SMALL_PARAMS = ("b_ada", "norm1_g", "norm2_g", "final_g", "ret_decay", "conv_w", "conv_b", "lru_wa", "lru_ba", "lru_wx",
                "lru_bx", "lru_lambda")


def _finalize_small(chip_idx, slab_all, ga_all, gx_all, wmv):
    n_p = len(SMALL_PARAMS)
    flat = [a for nm in SMALL_PARAMS for a in wmv[nm]]
    ada_n = N_MOD * D_MODEL // N_CHIP

    def body(c_ref, slab_ref, ga_ref, gx_ref, *refs):
        prm = {nm: refs[3 * k:3 * k + 3] for k, nm in enumerate(SMALL_PARAMS)}
        outs = {nm: refs[3 * n_p + 4 * k:3 * n_p + 4 * k + 4] for k, nm in enumerate(SMALL_PARAMS)}
        b128_ref, dmc_ref, loss_ref = refs[3 * n_p + 4 * n_p:]
        chip = c_ref[0]

        def pick(fn):
            acc = fn(0)
            for j in range(1, N_CHIP):
                acc = jnp.where(chip == j, fn(j), acc)
            return acc

        tot = slab_ref[0]
        for d in range(1, N_DEV):
            tot = tot + slab_ref[d]

        def update(nm, g, sl=None, rows=None):
            w_ref, m_ref, v_ref = prm[nm]
            g_ref, d_ref, mo_ref, vo_ref = outs[nm]
            ix = (slice(None) if rows is None else rows, slice(None) if sl is None else sl)
            dl, mn, vn = _adam_math(w_ref[ix], g, m_ref[ix], v_ref[ix])
            g_ref[ix] = g
            d_ref[ix] = dl
            mo_ref[ix] = mn
            vo_ref[ix] = vn

        loss_ref[...] = jnp.broadcast_to(tot[ROW_LOSS:ROW_LOSS + 1, 0:LANES], (SUBLANES, LANES))
        for k in range(N_MOD):
            g = tot[ROW_DMOD + k:ROW_DMOD + k + 1, :]
            if k < 2:
                g = g + tot[ROW_DMODC + k:ROW_DMODC + k + 1, :]
            update("b_ada", g, slice(D_MODEL * k, D_MODEL * (k + 1)))
        update("norm1_g", tot[ROW_N1:ROW_N1 + 1, :])
        update("norm2_g", tot[ROW_N2:ROW_N2 + 1, :])
        update("final_g", tot[ROW_FG:ROW_FG + 1, :])
        update("ret_decay", tot[ROW_RD:ROW_RD + SUBLANES, 0:LANES])
        update("conv_b", tot[ROW_CB:ROW_CB + 1, 0:LRU_W])
        update("conv_w", pick(lambda j: tot[ROW_CW:ROW_CW + 4, LANES * j:LANES * (j + 1)]))
        for nm, row in (("lru_ba", ROW_BA), ("lru_bx", ROW_BX), ("lru_lambda", ROW_LAM)):
            update(nm, pick(lambda j, row=row: tot[row:row + 2, LANES * j:LANES * (j + 1)]))
        for nm, g_all in (("lru_wa", ga_ref), ("lru_wx", gx_ref)):
            for dr in range(2):
                lanes = slice(LRU_BD * dr, LRU_BD * (dr + 1))
                g = g_all[0, :, lanes].astype(F32)
                for d in range(1, N_DEV):
                    g = g + g_all[d, :, lanes].astype(F32)
                update(nm, g, rows=slice(LRU_W * dr, LRU_W * (dr + 1)))

        def seg(rows6, s):
            return rows6[s // 2][:, SEG * (s % 2):SEG * (s % 2 + 1)]

        b128_ref[...] = jnp.zeros_like(b128_ref)
        dmc_ref[...] = jnp.zeros_like(dmc_ref)
        zero = jnp.zeros((1, D_MODEL), F32)
        ctx6 = [tot[ROW_DMODC:ROW_DMODC + 1, :], tot[ROW_DMODC + 1:ROW_DMODC + 2, :]] + [zero] * (N_MOD - 2)
        for q in range(ada_n // SEG):
            cols = slice(SEG * q, SEG * (q + 1))
            for d in range(N_DEV):
                rows6 = [slab_ref[d, ROW_DMOD + k:ROW_DMOD + k + 1, :] for k in range(N_MOD)]
                b128_ref[d:d + 1, cols] = pick(lambda j, rows6=rows6: seg(rows6, 3 * j + q))
            c = pick(lambda j: seg(ctx6, 3 * j + q))
            b128_ref[N_DEV:N_DEV + 1, cols] = c
            dmc_ref[0:1, cols] = c

    out_shape = []
    for nm in SMALL_PARAMS:
        out_shape += [_sds(wmv[nm][0].shape)] * 4
    out_shape += [_sds((LANES, ada_n)), _sds((SUBLANES, ada_n)), _sds((SUBLANES, LANES))]
    args = [slab_all, ga_all, gx_all] + flat
    grid_spec = pltpu.PrefetchScalarGridSpec(
        num_scalar_prefetch=1, grid=(1,), in_specs=[_full(a.shape) for a in args],
        out_specs=[_full(s.shape) for s in out_shape])
    outs = _pc(body, name="finalize_small", grid_spec=grid_spec, out_shape=out_shape,
               compiler_params=_params("arbitrary"))(chip_idx, *args)
    res = {nm: tuple(outs[4 * k:4 * k + 4]) for k, nm in enumerate(SMALL_PARAMS)}
    return res, outs[4 * n_p], outs[4 * n_p + 1], outs[4 * n_p + 2]


def _block_diag(w):
    eye = jnp.eye(LRU_BLOCKS, dtype=F32)
    return (w[:, :, None, :] * eye[:, None, :, None]).reshape(LRU_W, LRU_W).astype(BF16)


def _lane_rep(v8):
    return jnp.broadcast_to(v8.reshape(SUBLANES, 1), (SUBLANES, LANES))


def kernel(x, c, ctx, c_ctx, w_ada, b_ada, norm1_g, norm2_g, w_in, ret_decay, conv_w, conv_b, lru_wa, lru_ba, lru_wx, lru_bx, lru_lambda, w_out, w_mlp1, w_mlp2, final_g, loss_target, m_c_ctx, m_w_ada, m_b_ada, m_norm1_g, m_norm2_g, m_w_in, m_ret_decay, m_conv_w, m_conv_b, m_lru_wa, m_lru_ba, m_lru_wx, m_lru_bx, m_lru_lambda, m_w_out, m_w_mlp1, m_w_mlp2, m_final_g, v_c_ctx, v_w_ada, v_b_ada, v_norm1_g, v_norm2_g, v_w_in, v_ret_decay, v_conv_w, v_conv_b, v_lru_wa, v_lru_ba, v_lru_wx, v_lru_bx, v_lru_lambda, v_w_out, v_w_mlp1, v_w_mlp2, v_final_g):
    ax, ay, ac = lax.axis_index("x"), lax.axis_index("y"), lax.axis_index("c")
    chip = 2 * ax + ay
    dev = 4 * ax + 2 * ay + ac
    c_idx = ac.reshape(1).astype(jnp.int32)
    j_idx = chip.reshape(1).astype(jnp.int32)

    xt = x[0]
    t_len = xt.shape[0]
    ctxt = ctx[0]
    l_len = ctxt.shape[0]
    tgt = loss_target[0]
    ada_n = w_ada.shape[2]

    def my_half(w2d):
        r = w2d.shape[0] // 2
        return lax.dynamic_slice_in_dim(w2d, ac * r, r, axis=0).astype(BF16)

    pad8 = lambda a: jnp.pad(a, ((0, SUBLANES - a.shape[0]), (0, 0)))
    small = jnp.concatenate([pad8(conv_w[0]), pad8(lru_ba[0]), pad8(lru_bx[0]), pad8(lru_lambda[0])], axis=0)
    gw_in, c_all, small_all = _all_gather([my_half(w_in[0]), pad8(c), small], "gather_head")
    w4 = gw_in.reshape(N_CHIP, D_MODEL, IN_COLS // N_CHIP)

    a16, lgv, sgv = _prep(c_all[:, 0, :], c_ctx, ret_decay[0])
    b_shard = lax.dynamic_slice_in_dim(b_ada, chip * ada_n, ada_n, axis=1)
    (mod_parts,) = _all_gather([_mod_fwd(a16, w_ada[0], b_shard)], "gather_mod")
    mod_all = mod_parts[0::2].transpose(1, 0, 2).reshape(16, N_CHIP * ada_n)
    mod_me = lax.dynamic_slice_in_dim(mod_all, dev, 1, axis=0)
    sh1, sc1, g1, sh2, sc2, g2 = [mod_me[:, D_MODEL * k:D_MODEL * (k + 1)] for k in range(N_MOD)]
    csh1, csc1 = mod_all[8:9, 0:D_MODEL], mod_all[8:9, D_MODEL:2 * D_MODEL]

    cos2, sin2 = _rotary_tables(t_len)
    cos_c, sin_c = jnp.ones((l_len, DH), F32), jnp.zeros((l_len, DH), F32)
    n1g, n2g = norm1_g, norm2_g
    fg = final_g.reshape(1, D_MODEL)

    small_full = small_all[0::2].transpose(1, 0, 2).reshape(4 * SUBLANES, LRU_W)
    cw = small_full[0:4]
    cb = conv_b
    ba_f, ba_b = small_full[8:9], small_full[9:10]
    bx_f, bx_b = small_full[16:17], small_full[17:18]
    lam_f, lam_b = small_full[24:25], small_full[25:26]
    wa_f, wa_b = _block_diag(lru_wa[0, 0]), _block_diag(lru_wa[0, 1])
    wx_f, wx_b = _block_diag(lru_wx[0, 0]), _block_diag(lru_wx[0, 1])
    zero_h = jnp.zeros((1, LRU_W), F32)

    projc, xrc, hcb16 = _inproj_fwd(ctxt, n1g, csh1, csc1, w4, cos_c, sin_c, "inproj_fwd_ctx")
    s_f, s_b = _ctx_state_fwd(projc, lgv)
    xcc = _conv_fwd(xrc, cw, cb, "conv_fwd_ctx")
    hcf = _lru_fwd(xcc, wa_f, wx_f, ba_f, bx_f, lam_f, zero_h, False, "lru_fwd_ctx_f")
    hcbk = _lru_fwd(xcc, wa_b, wx_b, ba_b, bx_b, lam_b, zero_h, True, "lru_fwd_ctx_b")
    lru_sf, lru_sb = hcf[l_len - 1:l_len], hcbk[0:1]

    h1, h2 = my_half(w_mlp1[0]), my_half(w_mlp2[0])
    q = h1.shape[0] // 2
    (proj, xrl, hb16), ((gw_1a,),) = _inproj_fwd(xt, n1g, sh1, sc1, w4, cos2, sin2, "inproj_fwd",
                                           comms=(_AllGather([h1[:q]]),))
    (o_f, o_b, spf, spb), ((gw_1b, gw_out),) = _ret_fwd(proj, lgv, s_f, s_b,
                                                       comms=(_AllGather([h1[q:], my_half(w_out[0])]),))
    xcl = _conv_fwd(xrl, cw, cb, "conv_fwd")
    hf, ((gw_2a,),) = _lru_fwd(xcl, wa_f, wx_f, ba_f, bx_f, lam_f, lru_sf, False, "lru_fwd_f",
                              comms=(_AllGather([h2[:q]]),))
    hbk, ((gw_2b,),) = _lru_fwd(xcl, wa_b, wx_b, ba_b, bx_b, lam_b, lru_sb, True, "lru_fwd_b",
                               comms=(_AllGather([h2[q:]]),))
    wo = gw_out.reshape(D_MODEL, D_MODEL)
    x1, cat = _mix_fwd(o_f, o_b, proj, hf, hbk, wo, xt, g1)

    (dx1, h2b, ab, dub, dmb, dsc2, dsh2, dg2, dn2g, dfg, lossv) = _mlp(
        x1, n2g, sh2, sc2, g2, fg, (gw_1a, gw_1b), (gw_2a, gw_2b), tgt)
    gw_mlp1 = _tn(h2b, dub, N_CHIP, False, True, "grad_w_mlp1")
    b_1 = gw_mlp1.reshape(N_DEV, D_MODEL // 2, MLP_H // N_CHIP)
    gw_mlp2, ((r_1,),) = _tn(ab, dmb, N_CHIP, True, False, "grad_w_mlp2", comms=(_pair_exchange([b_1]),))

    jc_idx = jnp.concatenate([j_idx, c_idx])
    b_2 = gw_mlp2.reshape(N_DEV, MLP_H // N_DEV, D_MODEL)
    (do, dhs, dg, dgate, dyb, dg1), ((r_2,),) = _mix_bwd(
        o_f, o_b, proj, hf, hbk, wo, cat, dx1, g1, comms=(_pair_exchange([b_2]),))
    gw_o = _tn(cat, dyb, 1, False, False, "grad_w_out")
    b_o = gw_o.reshape(N_DEV, D_MODEL // N_DEV, D_MODEL)
    p_1, pb_1 = _pair_add(b_1, r_1, c_idx, "rs_pair_add_w_mlp1")
    p_2, pb_2 = _pair_add(b_2, r_2, c_idx, "rs_pair_add_w_mlp2")

    (dq_f, dk_f, dv_f, ds_f, drd_f), ((q_1,), (r_o,)) = _ret_bwd(
        proj, lgv, sgv, spf, do, False, "ret_bwd_f", comms=(_chip_exchange([pb_1]), _pair_exchange([b_o])))
    p_o, pb_o = _pair_add(b_o, r_o, c_idx, "rs_pair_add_w_out")
    h_1 = _chip_add(p_1, q_1, jc_idx, "rs_chip_add_w_mlp1")

    (dq_b, dk_b, dv_b, ds_b, drd_b), ((q_2,), (f_1,)) = _ret_bwd(
        proj, lgv, sgv, spb, do, True, "ret_bwd_b", comms=(_chip_exchange([pb_2]), _pair_gather([h_1])))
    h_2 = _chip_add(p_2, q_2, jc_idx, "rs_chip_add_w_mlp2")

    (dxc_f, dpre_f, dba_f, dbx_f, dlam_f, dh0_f), ((q_o,), (f_2,)) = _lru_bwd(
        xcl, wa_f, wx_f, ba_f, bx_f, lam_f, hf, lru_sf, dhs, False, "lru_bwd_f",
        comms=(_chip_exchange([pb_o]), _pair_gather([h_2])))
    h_o = _chip_add(p_o, q_o, jc_idx, "rs_chip_add_w_out")
    (dxc_b, dpre_b, dba_b, dbx_b, dlam_b, dh0_b), ((f_o,),) = _lru_bwd(
        xcl, wa_b, wx_b, ba_b, bx_b, lam_b, hbk, lru_sb, dhs, True, "lru_bwd_b", comms=(_pair_gather([h_o]),))
    dxr, dcw, dcb = _conv_bwd(dxc_f, dxc_b, xrl, cw, "conv_bwd")
    grad_x, dpb, dn1g, dsh1, dsc1 = _inproj_bwd(
        xt, n1g, sh1, sc1, w4, cos2, sin2, [dq_f, dq_b, dk_f, dk_b, dv_f, dv_b, dg, dxr, dgate], dx1, "inproj_bwd")

    dkc, dvc, drd_c = _ctx_state_bwd(projc, lgv, sgv, ds_f, ds_b)
    zc = jnp.zeros((l_len, LRU_W), F32)
    dhc_f = lax.dynamic_update_slice(zc, dh0_f, (l_len - 1, 0))
    dhc_b = lax.dynamic_update_slice(zc, dh0_b, (0, 0))
    (dxcc_f, dprec_f, dbac_f, dbxc_f, dlamc_f, _), _ = _lru_bwd(
        xcc, wa_f, wx_f, ba_f, bx_f, lam_f, hcf, zero_h, dhc_f, False, "lru_bwd_ctx_f")
    (dxcc_b, dprec_b, dbac_b, dbxc_b, dlamc_b, _), _ = _lru_bwd(
        xcc, wa_b, wx_b, ba_b, bx_b, lam_b, hcbk, zero_h, dhc_b, True, "lru_bwd_ctx_b")
    dxrc, dcw_c, dcb_c = _conv_bwd(dxcc_f, dxcc_b, xrc, cw, "conv_bwd_ctx")
    zr = jnp.zeros((l_len, RET_W), BF16)
    _, dpbc, dn1g_c, dcsh1, dcsc1 = _inproj_bwd(
        ctxt, n1g, csh1, csc1, w4, cos_c, sin_c, [zr, zr, dkc, zr, dvc, zr, zr, dxrc, zr],
        jnp.zeros((l_len, D_MODEL), F32), "inproj_bwd_ctx")

    gw_i = _tn(hb16, dpb, N_CHIP, False, True, "grad_w_in", extra=(hcb16, dpbc))
    b_i = gw_i.reshape(N_DEV, D_MODEL // 2, IN_COLS // N_CHIP)
    gwa_f, ((r_i,),) = _tn(xcl, dpre_f, 2, False, True, "grad_lru_gates_f", extra=(xcc, dprec_f),
                           comms=(_pair_exchange([b_i]),))
    p_i, pb_i = _pair_add(b_i, r_i, c_idx, "rs_pair_add_w_in")
    gwa_b, ((q_i,),) = _tn(xcl, dpre_b, 2, False, True, "grad_lru_gates_b", extra=(xcc, dprec_b),
                           comms=(_chip_exchange([pb_i]),))
    h_i = _chip_add(p_i, q_i, jc_idx, "rs_chip_add_w_in")
    g_out, g_1, g_2 = _shard_of(f_o), _shard_of(f_1), _shard_of(f_2)
    big = {}
    (d_, mn, vn), ((f_i,),) = _adamw(w_mlp1[0], g_1, m_w_mlp1[0], v_w_mlp1[0], "adamw_w_mlp1",
                                     comms=(_pair_gather([h_i]),))
    big["w_mlp1"] = (g_1[None], d_[None], mn[None], vn[None])
    g_in = _shard_of(f_i)
    for nm, w, g, m, v in (("w_in", w_in, g_in, m_w_in, v_w_in), ("w_out", w_out, g_out, m_w_out, v_w_out)):
        d_, mn, vn = _adamw(w[0], g, m[0], v[0], "adamw_" + nm)
        big[nm] = (g[None], d_[None], mn[None], vn[None])

    slab, ga, gx = _pack_small(
        [lossv, dsh1, dsc1, dg1, dsh2, dsc2, dg2, dcsh1, dcsc1, dn1g, dn1g_c, dn2g, dfg],
        (drd_f, drd_b, drd_c), (dcw, dcw_c), (dcb, dcb_c),
        (dba_f, dbac_f, dba_b, dbac_b, dbx_f, dbxc_f, dbx_b, dbxc_b, dlam_f, dlamc_f, dlam_b, dlamc_b),
        (gwa_f, gwa_b))
    (d_, mn, vn), ((slab_all, ga_all, gx_all),) = _adamw(
        w_mlp2[0], g_2, m_w_mlp2[0], v_w_mlp2[0], "adamw_w_mlp2", comms=(_AllGather([slab, ga, gx]),))
    big["w_mlp2"] = (g_2[None], d_[None], mn[None], vn[None])
    params = {
        "b_ada": (b_ada, m_b_ada, v_b_ada), "norm1_g": (norm1_g, m_norm1_g, v_norm1_g),
        "norm2_g": (norm2_g, m_norm2_g, v_norm2_g), "final_g": (final_g, m_final_g, v_final_g),
        "ret_decay": (ret_decay, m_ret_decay, v_ret_decay), "conv_w": (conv_w, m_conv_w, v_conv_w),
        "conv_b": (conv_b, m_conv_b, v_conv_b), "lru_wa": (lru_wa, m_lru_wa, v_lru_wa),
        "lru_ba": (lru_ba, m_lru_ba, v_lru_ba), "lru_wx": (lru_wx, m_lru_wx, v_lru_wx),
        "lru_bx": (lru_bx, m_lru_bx, v_lru_bx), "lru_lambda": (lru_lambda, m_lru_lambda, v_lru_lambda),
    }
    as2d = {
        "b_ada": lambda a: a, "norm1_g": lambda a: a, "norm2_g": lambda a: a, "conv_b": lambda a: a,
        "final_g": lambda a: a.reshape(1, D_MODEL), "ret_decay": lambda a: _lane_rep(a.reshape(-1)),
        "conv_w": lambda a: a[0], "lru_ba": lambda a: a[0], "lru_bx": lambda a: a[0], "lru_lambda": lambda a: a[0],
        "lru_wa": lambda a: a.reshape(2 * LRU_W, LRU_BD), "lru_wx": lambda a: a.reshape(2 * LRU_W, LRU_BD),
    }
    res, b128, dmc8, loss8 = _finalize_small(
        j_idx, slab_all, ga_all, gx_all, {nm: tuple(as2d[nm](a) for a in params[nm]) for nm in SMALL_PARAMS})
    loss = loss8[0, 0]
    small_out = {}
    for nm in SMALL_PARAMS:
        shp = params[nm][0].shape
        if nm == "ret_decay":
            small_out[nm] = tuple(o[:, 0].reshape(shp) for o in res[nm])
        else:
            small_out[nm] = tuple(o.reshape(shp) for o in res[nm])

    g_ada = _ada_grad(jnp.pad(a16.T, ((0, 0), (0, LANES - 16))), b128)
    (d_ada, m_ada, v_ada), ((cparts,),) = _adamw(
        w_ada[0], g_ada, m_w_ada[0], v_w_ada[0], "adamw_w_ada", comms=(_AllGather([_cctx_partial(dmc8, w_ada[0])]),))
    g_cc, d_cc, m_cc, v_cc = _cctx_final(cparts, c_ctx, m_c_ctx, v_c_ctx)
    small_out["c_ctx"] = tuple(a.reshape(D_MODEL) for a in (g_cc, d_cc, m_cc, v_cc))
    small_out["w_ada"] = (g_ada[None], d_ada[None], m_ada[None], v_ada[None])
    small_out.update(big)

    order = ["c_ctx", "w_ada", "b_ada", "norm1_g", "norm2_g", "w_in", "ret_decay", "conv_w", "conv_b", "lru_wa", "lru_ba",
             "lru_wx", "lru_bx", "lru_lambda", "w_out", "w_mlp1", "w_mlp2", "final_g"]
    outs = [loss, grad_x[None]]
    for k in range(4):
        outs += [small_out[nm][k] for nm in order]
    return tuple(outs)
```

```python
import math

import jax
import jax.numpy as jnp
from jax import lax
from jax.experimental import pallas as pl
from jax.experimental.pallas import tpu as pltpu

F32 = jnp.float32
BF16 = jnp.bfloat16

D_MODEL = 1024
HEADS = 4
DH = 128
CHUNK = 256
RET_W = HEADS * DH
LRU_W = 512
LRU_BLOCKS = 8
LRU_BD = LRU_W // LRU_BLOCKS
LRU_C = 8.0
IN_COLS = 4 * RET_W + 2 * LRU_W
MLP_H = 4 * D_MODEL
N_MOD = 6
GRID_W = 64
ROPE_BASE = 10000.0
K_SCALE = DH ** -0.5
EPS = 1e-6
GELU_K = math.sqrt(2.0 / math.pi)
GELU_C = 0.044715

ADAM_LR = 0.001
ADAM_B1 = 0.9
ADAM_B2 = 0.999
ADAM_EPS = 1e-08
ADAM_WD = 0.01
ADAM_STEP = 10

N_DEV = 8
N_CHIP = 4
SUBLANES = 8
LANES = 128
VMEM_LIMIT_V7X = 56 * 1024 * 1024
MESH = pl.DeviceIdType.MESH
ANY = pl.BlockSpec(memory_space=pl.ANY)


def _pc(body, **kw):
    return pl.pallas_call(body, **kw)


def _params(*sem):
    return pltpu.CompilerParams(dimension_semantics=sem if sem else None, vmem_limit_bytes=VMEM_LIMIT_V7X)


def _tile(t, big=False):
    if big and t >= 1024:
        return 512
    return 256 if t >= 256 else t


def _sds(shape, dtype=F32):
    return jax.ShapeDtypeStruct(tuple(shape), dtype)


def _full(shape):
    nd = len(shape)
    return pl.BlockSpec(tuple(shape), lambda *_: (0,) * nd)


def _sigmoid(x):
    return 1.0 / (1.0 + jnp.exp(-x))


def _log1p_pos(y):
    s = y * (1.0 - y * (0.5 - y * (1.0 / 3.0 - y * (0.25 - y * (0.2 - y / 6.0)))))
    return jnp.where(y < 0.03, s, jnp.log(1.0 + y))


def _softplus(z):
    return jnp.maximum(z, 0.0) + _log1p_pos(jnp.exp(-jnp.abs(z)))


def _neg_expm1(x, exp_x):
    t = x * (1.0 + x * (0.5 + x * (1.0 / 6.0 + x * (1.0 / 24.0 + x * (1.0 / 120.0 + x * (1.0 / 720.0 + x / 5040.0))))))
    return -jnp.where(x > -0.25, t, exp_x - 1.0)


def _rms(x):
    r = lax.rsqrt(jnp.mean(x * x, axis=-1, keepdims=True) + EPS)
    return x * r, r


def _dot(a, b):
    return jnp.dot(a, b, preferred_element_type=F32)


def _dot_nt(a, b):
    return lax.dot_general(a, b, (((1,), (1,)), ((), ())), preferred_element_type=F32)


def _dot_tn(a, b):
    return lax.dot_general(a, b, (((0,), (0,)), ((), ())), preferred_element_type=F32)


def _sum0(x):
    return jnp.sum(x, axis=0, keepdims=True)


def _norm_mod_bwd(x, g, sc, dh):
    xh, r = _rms(x)
    hn = xh * g
    dhn = dh * (1.0 + sc)
    dxh = dhn * g
    dx = r * (dxh - xh * jnp.mean(dxh * xh, axis=-1, keepdims=True))
    return dx, _sum0(dhn * xh), _sum0(dh), _sum0(dh * hn)


def _dev_index(p):
    return 4 * p[0] + 2 * p[1] + p[2]


def _mesh_pos():
    return lax.axis_index("x"), lax.axis_index("y"), lax.axis_index("c")


class _AllGather:
    def __init__(self, arrs):
        n = len(arrs)
        self.arrays = list(arrs)
        self.out_shapes = [_sds((N_DEV,) + a.shape, a.dtype) for a in arrs]
        self.scratch = ([pltpu.VMEM(a.shape, a.dtype) for a in arrs]
                        + [pltpu.SemaphoreType.DMA((7 * n,)), pltpu.SemaphoreType.DMA((7 * n,)),
                           pltpu.SemaphoreType.DMA((n,))])
        self.aliases = {}

    def _parts(self, ins, outs, scr):
        n = len(self.arrays)
        stage = scr[:n]
        send_sems, recv_sems, local_sems = scr[n:]
        x, y, c = _mesh_pos()
        me, sib = (x, y, c), (x, y, 1 - c)
        chips = [(1 - x, y), (x, 1 - y), (1 - x, 1 - y)]

        def copy(t, k, block, to, own=False):
            dst = outs[t].at[_dev_index(block)]
            return pltpu.make_async_remote_copy(
                src_ref=ins[t] if own else dst, dst_ref=dst,
                send_sem=send_sems.at[7 * t + k], recv_sem=recv_sems.at[7 * t + k],
                device_id=to, device_id_type=MESH)

        first = []
        for t in range(n):
            first.append(copy(t, 0, me, sib, own=True))
            for j, ch in enumerate(chips):
                first.append(copy(t, 1 + j, me, (*ch, c), own=True))
        stage_in = [pltpu.make_async_copy(ins[t], stage[t], local_sems.at[t]) for t in range(n)]
        mine = [pltpu.make_async_copy(stage[t], outs[t].at[_dev_index(me)], local_sems.at[t]) for t in range(n)]
        return n, c, me, sib, chips, copy, first, stage_in, mine

    def start(self, ins, outs, scr):
        n, _, _, _, _, _, first, stage_in, mine = self._parts(ins, outs, scr)
        for cp in stage_in:
            cp.start()
        for cp in first:
            cp.start()
        for t in range(n):
            stage_in[t].wait()
            mine[t].start()

    def finish(self, ins, outs, scr):
        n, c, me, sib, chips, copy, first, _, mine = self._parts(ins, outs, scr)
        passed = []
        for j, ch in enumerate(chips):
            for t in range(n):
                copy(t, 1 + j, (*ch, c), me).wait_recv()
                p = copy(t, 4 + j, (*ch, c), sib)
                p.start()
                passed.append(p)
        for t in range(n):
            copy(t, 0, sib, me).wait_recv()
            for j, ch in enumerate(chips):
                copy(t, 4 + j, (*ch, 1 - c), me).wait_recv()
        for cp in first + passed:
            cp.wait_send()
        for cp in mine:
            cp.wait()


class _Exchange:
    def __init__(self, arrays, out_shapes, plan, n_copies, aliases=None):
        self.arrays = list(arrays)
        self.out_shapes = list(out_shapes)
        self.plan = plan
        self.scratch = [pltpu.SemaphoreType.DMA((n_copies,)), pltpu.SemaphoreType.DMA((n_copies,))]
        self.aliases = aliases or {}

    def _copies(self, ins, outs, scr):
        send_sems, recv_sems = scr
        snd, rcv = [], []
        for i, (src, dst, peer, lands) in enumerate(self.plan(ins, outs, _mesh_pos())):
            kw = dict(send_sem=send_sems.at[i], recv_sem=recv_sems.at[i], device_id=peer, device_id_type=MESH)
            snd.append(pltpu.make_async_remote_copy(src_ref=src, dst_ref=dst, **kw))
            rcv.append(pltpu.make_async_remote_copy(src_ref=src, dst_ref=lands, **kw))
        return snd, rcv

    def start(self, ins, outs, scr):
        for cp in self._copies(ins, outs, scr)[0]:
            cp.start()

    def finish(self, ins, outs, scr):
        snd, rcv = self._copies(ins, outs, scr)
        for cp in rcv:
            cp.wait_recv()
        for cp in snd:
            cp.wait_send()


def _pair_exchange(grads):
    n = len(grads)

    def plan(ins, outs, pos):
        x, y, c = pos
        return [(ins[t].at[2 * j + (1 - c)], outs[t].at[j], (x, y, 1 - c), outs[t].at[j])
                for t in range(n) for j in range(N_CHIP)]

    return _Exchange(grads, [_sds((N_CHIP,) + g.shape[1:], g.dtype) for g in grads], plan, N_CHIP * n)


def _chip_exchange(parts):
    n = len(parts)

    def plan(ins, outs, pos):
        x, y, c = pos
        chips = [(1 - x, y), (x, 1 - y), (1 - x, 1 - y)]
        return [(ins[t].at[2 * ch[0] + ch[1]], outs[t].at[k], (*ch, c), outs[t].at[k])
                for t in range(n) for k, ch in enumerate(chips)]

    return _Exchange(parts, [_sds((3,) + p.shape[1:], p.dtype) for p in parts], plan, 3 * n)


def _pair_gather(bufs):
    n = len(bufs)

    def plan(ins, outs, pos):
        x, y, c = pos
        return [(ins[t].at[c], outs[t].at[c], (x, y, 1 - c), outs[t].at[1 - c]) for t in range(n)]

    return _Exchange(bufs, [_sds(b.shape, b.dtype) for b in bufs], plan, n, aliases={t: t for t in range(n)})


def _run_comm(comm, name):
    n_in, n_out = len(comm.arrays), len(comm.out_shapes)

    def body(*refs):
        ins, outs, scr = refs[:n_in], refs[n_in:n_in + n_out], refs[n_in + n_out:]
        comm.start(ins, outs, scr)
        comm.finish(ins, outs, scr)

    outs = _pc(body, name=name, out_shape=comm.out_shapes, in_specs=[ANY] * n_in, out_specs=[ANY] * n_out,
               input_output_aliases=dict(comm.aliases), scratch_shapes=comm.scratch,
               compiler_params=_params())(*comm.arrays)
    return list(outs)


def _all_gather(arrs, name):
    return _run_comm(_AllGather(arrs), name)


def _call(body, *, name, grid, in_specs, out_specs, out_shape, scratch_shapes, sem, args, comms=()):
    n_in, n_out, n_scr = len(in_specs), len(out_specs), len(scratch_shapes)
    c_in = [len(cm.arrays) for cm in comms]
    c_out = [len(cm.out_shapes) for cm in comms]
    c_scr = [len(cm.scratch) for cm in comms]
    aliases = {}
    for k, cm in enumerate(comms):
        for a, b in cm.aliases.items():
            aliases[n_in + sum(c_in[:k]) + a] = n_out + sum(c_out[:k]) + b

    def split(refs, counts):
        out, pos = [], 0
        for cnt in counts:
            out.append(refs[pos:pos + cnt])
            pos += cnt
        return out

    def wrapped(*refs):
        ins = refs[:n_in + sum(c_in)]
        outs = refs[len(ins):len(ins) + n_out + sum(c_out)]
        scr = refs[len(ins) + len(outs):]
        cins, couts, cscr = split(ins[n_in:], c_in), split(outs[n_out:], c_out), split(scr[n_scr:], c_scr)
        if comms:
            first = pl.program_id(0) == 0
            last = pl.program_id(0) == grid[0] - 1
            for k in range(1, len(grid)):
                first = jnp.logical_and(first, pl.program_id(k) == 0)
                last = jnp.logical_and(last, pl.program_id(k) == grid[k] - 1)

            @pl.when(first)
            def _():
                for k, cm in enumerate(comms):
                    cm.start(cins[k], couts[k], cscr[k])
        body(*ins[:n_in], *outs[:n_out], *scr[:n_scr])
        if comms:
            @pl.when(last)
            def _():
                for k, cm in enumerate(comms):
                    cm.finish(cins[k], couts[k], cscr[k])

    outs = _pc(wrapped, name=name, grid=grid,
               in_specs=list(in_specs) + [ANY] * sum(c_in), out_specs=list(out_specs) + [ANY] * sum(c_out),
               out_shape=list(out_shape) + [s for cm in comms for s in cm.out_shapes],
               scratch_shapes=list(scratch_shapes) + [s for cm in comms for s in cm.scratch],
               input_output_aliases=aliases, compiler_params=_params(*sem),
               )(*args, *[a for cm in comms for a in cm.arrays])
    outs = list(outs)
    return outs[:n_out], split(outs[n_out:], c_out)


def _row_block(r):
    for b in (512, 256, 128, 64, 32, 16, 8):
        if r % b == 0:
            return b
    return r


def _pair_add(g, recv, c_idx, name):
    _, r, cc = g.shape
    br = _row_block(r)

    def body(c_ref, g_ref, r_ref, p_ref, pb_ref):
        s = g_ref[...] + r_ref[...]
        p_ref[...] = s
        pb_ref[...] = s.astype(BF16)

    grid_spec = pltpu.PrefetchScalarGridSpec(
        num_scalar_prefetch=1, grid=(N_CHIP, r // br),
        in_specs=[pl.BlockSpec((1, br, cc), lambda j, i, c_ref: (2 * j + c_ref[0], i, 0)),
                  pl.BlockSpec((1, br, cc), lambda j, i, c_ref: (j, i, 0))],
        out_specs=[pl.BlockSpec((1, br, cc), lambda j, i, c_ref: (j, i, 0)),
                   pl.BlockSpec((1, br, cc), lambda j, i, c_ref: (j, i, 0))])
    return _pc(body, name=name, grid_spec=grid_spec,
               out_shape=[_sds((N_CHIP, r, cc)), _sds((N_CHIP, r, cc), BF16)],
               compiler_params=_params("arbitrary", "arbitrary"))(c_idx, g, recv)


def _chip_add(p, q, jc_idx, name):
    _, r, cc = p.shape
    br = _row_block(r)

    def body(jc_ref, p_ref, q_ref, o_ref):
        o_ref[0] = ((p_ref[0] + q_ref[0].astype(F32)) + q_ref[1].astype(F32)) + q_ref[2].astype(F32)

    grid_spec = pltpu.PrefetchScalarGridSpec(
        num_scalar_prefetch=1, grid=(r // br,),
        in_specs=[pl.BlockSpec((1, br, cc), lambda i, jc_ref: (jc_ref[0], i, 0)),
                  pl.BlockSpec((3, br, cc), lambda i, jc_ref: (0, i, 0))],
        out_specs=pl.BlockSpec((1, br, cc), lambda i, jc_ref: (jc_ref[1], i, 0)))
    return _pc(body, name=name, grid_spec=grid_spec, out_shape=_sds((2, r, cc)),
               compiler_params=_params("arbitrary"))(jc_idx, p, q)


def _shard_of(both):
    return both.reshape((2 * both.shape[1],) + both.shape[2:])


def _adamw(w, g, m, v, name):
    r, cc = w.shape
    br = _row_block(r)
    if r * cc * 4 <= (1 << 20):
        br = r
    elif br * cc * 4 > (1 << 20) and br > 8:
        br = max(8, (1 << 20) // (cc * 4) // 8 * 8)
        while r % br:
            br -= 8
    c1 = 1.0 - ADAM_B1 ** ADAM_STEP
    c2 = 1.0 - ADAM_B2 ** ADAM_STEP

    def body(w_ref, g_ref, m_ref, v_ref, go_ref, d_ref, mo_ref, vo_ref):
        gg = g_ref[...]
        go_ref[...] = gg
        mn = ADAM_B1 * m_ref[...] + (1.0 - ADAM_B1) * gg
        vn = ADAM_B2 * v_ref[...] + (1.0 - ADAM_B2) * (gg * gg)
        mh = mn / c1
        vh = vn / c2
        d_ref[...] = -ADAM_LR * (mh / (jnp.sqrt(vh) + ADAM_EPS) + ADAM_WD * w_ref[...])
        mo_ref[...] = mn
        vo_ref[...] = vn

    spec = pl.BlockSpec((br, cc), lambda i: (i, 0))
    return _pc(body, name=name, grid=(r // br,), in_specs=[spec] * 4, out_specs=[spec] * 4,
               out_shape=[_sds((r, cc))] * 4, compiler_params=_params("arbitrary"))(w, g, m, v)


def _prep(c_all, c_ctx, ret_decay):
    def body(c_ref, cc_ref, rd_ref, a_ref, lg_ref, sg_ref):
        ca = c_ref[...]
        cc = cc_ref[...]
        a_ref[...] = jnp.zeros_like(a_ref)
        a_ref[0:8, :] = ca * _sigmoid(ca)
        a_ref[8:9, :] = cc * _sigmoid(cc)
        rd = rd_ref[...]
        lg_ref[...] = -_softplus(-rd)
        sg_ref[...] = _sigmoid(-rd)

    rd = jnp.broadcast_to(ret_decay.reshape(2, HEADS).T[:, :, None], (HEADS, 2, LANES))
    return _pc(body, name="prep",
               out_shape=[_sds((16, D_MODEL)), _sds((HEADS, 2, LANES)), _sds((HEADS, 2, LANES))],
               in_specs=[_full((8, D_MODEL)), _full((1, D_MODEL)), _full((HEADS, 2, LANES))],
               out_specs=[_full((16, D_MODEL)), _full((HEADS, 2, LANES)), _full((HEADS, 2, LANES))],
               compiler_params=_params())(c_all, c_ctx.reshape(1, D_MODEL), rd)


def _mod_fwd(a16, w_ada, b_shard):
    n = w_ada.shape[1]
    bn = 512

    def body(a_ref, w_ref, b_ref, o_ref):
        o_ref[...] = jnp.dot(a_ref[...], w_ref[...], preferred_element_type=F32,
                             precision=lax.Precision.HIGHEST) + b_ref[...]

    return _pc(body, name="mod_fwd", grid=(n // bn,),
               in_specs=[_full((16, D_MODEL)), pl.BlockSpec((D_MODEL, bn), lambda i: (0, i)),
                         pl.BlockSpec((1, bn), lambda i: (0, i))],
               out_specs=pl.BlockSpec((16, bn), lambda i: (0, i)), out_shape=_sds((16, n)),
               compiler_params=_params("arbitrary"))(a16, w_ada, b_shard)


def _ada_grad(at, b):
    n = b.shape[1]
    bn = 512

    def body(a_ref, b_ref, o_ref):
        o_ref[...] = jnp.dot(a_ref[...], b_ref[...], preferred_element_type=F32, precision=lax.Precision.HIGHEST)

    return _pc(body, name="ada_grad", grid=(n // bn,),
               in_specs=[_full((D_MODEL, LANES)), pl.BlockSpec((LANES, bn), lambda i: (0, i))],
               out_specs=pl.BlockSpec((D_MODEL, bn), lambda i: (0, i)), out_shape=_sds((D_MODEL, n)),
               compiler_params=_params("arbitrary"))(at, b)


def _cctx_partial(dmc8, w_ada):
    n = w_ada.shape[1]
    bn = 512

    def body(d_ref, w_ref, o_ref):
        @pl.when(pl.program_id(0) == 0)
        def _():
            o_ref[...] = jnp.zeros_like(o_ref)
        o_ref[...] += lax.dot_general(d_ref[...], w_ref[...], (((1,), (1,)), ((), ())),
                                      preferred_element_type=F32, precision=lax.Precision.HIGHEST)

    return _pc(body, name="cctx_partial", grid=(n // bn,),
               in_specs=[pl.BlockSpec((8, bn), lambda i: (0, i)), pl.BlockSpec((D_MODEL, bn), lambda i: (0, i))],
               out_specs=_full((8, D_MODEL)), out_shape=_sds((8, D_MODEL)),
               compiler_params=_params("arbitrary"))(dmc8, w_ada)


def _cctx_final(parts, c_ctx, m, v):
    c1 = 1.0 - ADAM_B1 ** ADAM_STEP
    c2 = 1.0 - ADAM_B2 ** ADAM_STEP

    def body(p_ref, c_ref, m_ref, v_ref, g_ref, d_ref, mo_ref, vo_ref):
        s = ((p_ref[0, 0:1, :] + p_ref[2, 0:1, :]) + p_ref[4, 0:1, :]) + p_ref[6, 0:1, :]
        z = c_ref[...]
        sg = _sigmoid(z)
        gg = s * (sg * (1.0 + z * (1.0 - sg)))
        g_ref[...] = gg
        mn = ADAM_B1 * m_ref[...] + (1.0 - ADAM_B1) * gg
        vn = ADAM_B2 * v_ref[...] + (1.0 - ADAM_B2) * (gg * gg)
        d_ref[...] = -ADAM_LR * ((mn / c1) / (jnp.sqrt(vn / c2) + ADAM_EPS) + ADAM_WD * z)
        mo_ref[...] = mn
        vo_ref[...] = vn

    row = _full((1, D_MODEL))
    return _pc(body, name="cctx_final", out_shape=[_sds((1, D_MODEL))] * 4,
               in_specs=[_full(parts.shape), row, row, row], out_specs=[row] * 4,
               compiler_params=_params())(parts, c_ctx.reshape(1, D_MODEL), m.reshape(1, D_MODEL), v.reshape(1, D_MODEL))


def _rotary_tables(t_len):
    rows = t_len // GRID_W
    row = jnp.repeat(jnp.arange(rows, dtype=F32), GRID_W)
    col = jnp.tile(jnp.arange(GRID_W, dtype=F32), rows)
    n_freq = DH // 4
    inv = ROPE_BASE ** (-jnp.arange(n_freq, dtype=F32) / n_freq)
    ang = jnp.concatenate([row[:, None] * inv, col[:, None] * inv], axis=-1)
    cos, sin = jnp.cos(ang), jnp.sin(ang)
    return jnp.concatenate([cos, cos], axis=-1), jnp.concatenate([-sin, sin], axis=-1)


def _inproj_fwd(x, gn, sh, sc, w4, cos2, sin2, name, comms=()):
    t = x.shape[0]
    tm = _tile(t, True)
    nc = IN_COLS // N_CHIP

    def body(x_ref, gn_ref, sh_ref, sc_ref, w_ref, c_ref, s_ref, p_ref, xr_ref, hb_ref, p_s):
        xh, _ = _rms(x_ref[...])
        h = xh * gn_ref[...] * (1.0 + sc_ref[...]) + sh_ref[...]
        hb = h.astype(BF16)
        hb_ref[...] = hb
        for j in range(N_CHIP):
            p_s[:, nc * j:nc * (j + 1)] = _dot(hb, w_ref[j])
        cc = c_ref[...]
        ss = s_ref[...]
        for hh in range(2 * HEADS):
            blk = p_s[:, DH * hh:DH * (hh + 1)]
            rot = blk * cc + pltpu.roll(blk, DH // 2, 1) * ss
            if hh >= HEADS:
                rot = rot * K_SCALE
            p_ref[:, DH * hh:DH * (hh + 1)] = rot.astype(BF16)
        p_ref[:, 2 * RET_W:] = p_s[:, 2 * RET_W:].astype(BF16)
        xr_ref[...] = p_s[:, 4 * RET_W:4 * RET_W + LRU_W]

    row = _full((1, D_MODEL))
    outs, couts = _call(
        body, name=name, grid=(t // tm,),
        in_specs=[pl.BlockSpec((tm, D_MODEL), lambda i: (i, 0)), row, row, row, _full(w4.shape),
                  pl.BlockSpec((tm, DH), lambda i: (i, 0)), pl.BlockSpec((tm, DH), lambda i: (i, 0))],
        out_specs=[pl.BlockSpec((tm, IN_COLS), lambda i: (i, 0)), pl.BlockSpec((tm, LRU_W), lambda i: (i, 0)),
                   pl.BlockSpec((tm, D_MODEL), lambda i: (i, 0))],
        out_shape=[_sds((t, IN_COLS), BF16), _sds((t, LRU_W)), _sds((t, D_MODEL), BF16)],
        scratch_shapes=[pltpu.VMEM((tm, IN_COLS), F32)], sem=("arbitrary",),
        args=(x, gn, sh, sc, w4, cos2, sin2), comms=comms)
    return (outs, couts) if comms else outs


def _inproj_bwd(x, gn, sh, sc, w4, cos2, sin2, pieces, dres, name):
    t = x.shape[0]
    tm = _tile(t)
    nc = IN_COLS // N_CHIP

    def body(x_ref, gn_ref, sh_ref, sc_ref, w_ref, c_ref, s_ref, dqf, dqb, dkf, dkb, dvf, dvb, dg, dxr, dgt, dres_ref,
             dx_ref, dpb_ref, dgn_ref, dsh_ref, dsc_ref):
        cc = c_ref[...]
        ss = s_ref[...]
        dq = dqf[...].astype(F32) + dqb[...].astype(F32)
        dk = dkf[...].astype(F32) + dkb[...].astype(F32)
        for hh in range(HEADS):
            sl = slice(DH * hh, DH * (hh + 1))
            b = dq[:, sl]
            dpb_ref[:, sl] = (b * cc + pltpu.roll(b * ss, DH // 2, 1)).astype(BF16)
            b = dk[:, sl]
            dpb_ref[:, RET_W + DH * hh:RET_W + DH * (hh + 1)] = (
                (b * cc + pltpu.roll(b * ss, DH // 2, 1)) * K_SCALE).astype(BF16)
        dpb_ref[:, 2 * RET_W:3 * RET_W] = (dvf[...].astype(F32) + dvb[...].astype(F32)).astype(BF16)
        dpb_ref[:, 3 * RET_W:4 * RET_W] = dg[...].astype(BF16)
        dpb_ref[:, 4 * RET_W:4 * RET_W + LRU_W] = dxr[...].astype(BF16)
        dpb_ref[:, 4 * RET_W + LRU_W:IN_COLS] = dgt[...].astype(BF16)
        dh = _dot_nt(dpb_ref[:, 0:nc], w_ref[0])
        for j in range(1, N_CHIP):
            dh = dh + _dot_nt(dpb_ref[:, nc * j:nc * (j + 1)], w_ref[j])
        dx, dgn_t, dsh_t, dsc_t = _norm_mod_bwd(x_ref[...], gn_ref[...], sc_ref[...], dh)
        dx_ref[...] = dres_ref[...] + dx

        @pl.when(pl.program_id(0) == 0)
        def _():
            dgn_ref[...] = jnp.zeros_like(dgn_ref)
            dsh_ref[...] = jnp.zeros_like(dsh_ref)
            dsc_ref[...] = jnp.zeros_like(dsc_ref)
        dgn_ref[...] += dgn_t
        dsh_ref[...] += dsh_t
        dsc_ref[...] += dsc_t

    row = _full((1, D_MODEL))
    pc = pl.BlockSpec((tm, RET_W), lambda i: (i, 0))
    big = pl.BlockSpec((tm, D_MODEL), lambda i: (i, 0))
    return _pc(body, name=name, grid=(t // tm,),
               in_specs=[big, row, row, row, _full(w4.shape),
                         pl.BlockSpec((tm, DH), lambda i: (i, 0)), pl.BlockSpec((tm, DH), lambda i: (i, 0))]
               + [pc] * 9 + [big],
               out_specs=[big, pl.BlockSpec((tm, IN_COLS), lambda i: (i, 0)), row, row, row],
               out_shape=[_sds((t, D_MODEL)), _sds((t, IN_COLS), BF16), _sds((1, D_MODEL)), _sds((1, D_MODEL)),
                          _sds((1, D_MODEL))],
               compiler_params=_params("arbitrary"))(x, gn, sh, sc, w4, cos2, sin2, *pieces, dres)


def _halo_specs(t, tm):
    n8 = tm // SUBLANES
    last8 = t // SUBLANES - 1
    prev = pl.BlockSpec((SUBLANES, LRU_W), lambda i: (jnp.maximum(i * n8 - 1, 0), 0))
    main = pl.BlockSpec((tm, LRU_W), lambda i: (i, 0))
    nxt = pl.BlockSpec((SUBLANES, LRU_W), lambda i: (jnp.minimum((i + 1) * n8, last8), 0))
    return prev, main, nxt


def _with_halo(prev_ref, main_ref, next_ref, i, nt):
    prev = jnp.where(i > 0, prev_ref[...], 0.0)
    nxt = jnp.where(i < nt - 1, next_ref[...], 0.0)
    return jnp.concatenate([prev, main_ref[...], nxt], axis=0)


def _conv_fwd(xr, cw, cb, name):
    t = xr.shape[0]
    tm = _tile(t, True)
    nt = t // tm
    n = tm + 2 * SUBLANES
    mid = slice(SUBLANES, SUBLANES + tm)

    def body(p_ref, m_ref, n_ref, w_ref, b_ref, o_ref):
        xp = _with_halo(p_ref, m_ref, n_ref, pl.program_id(0), nt)
        acc = b_ref[...] + pltpu.roll(xp, 1, 0)[mid] * w_ref[0:1, :]
        acc = acc + xp[mid] * w_ref[1:2, :]
        acc = acc + pltpu.roll(xp, n - 1, 0)[mid] * w_ref[2:3, :]
        acc = acc + pltpu.roll(xp, n - 2, 0)[mid] * w_ref[3:4, :]
        o_ref[...] = acc

    return _pc(body, name=name, grid=(nt,),
               in_specs=[*_halo_specs(t, tm), _full((4, LRU_W)), _full((1, LRU_W))],
               out_specs=pl.BlockSpec((tm, LRU_W), lambda i: (i, 0)), out_shape=_sds((t, LRU_W)),
               compiler_params=_params("arbitrary"))(xr, xr, xr, cw, cb)


def _conv_bwd(dxc_a, dxc_b, xr, cw, name):
    t = xr.shape[0]
    tm = _tile(t, True)
    nt = t // tm
    n = tm + 2 * SUBLANES
    mid = slice(SUBLANES, SUBLANES + tm)

    def body(ap_ref, am_ref, an_ref, bp_ref, bm_ref, bn_ref, xp_ref, xm_ref, xn_ref, w_ref, dx_ref, dw_ref, db_ref):
        i = pl.program_id(0)
        dp = _with_halo(ap_ref, am_ref, an_ref, i, nt) + _with_halo(bp_ref, bm_ref, bn_ref, i, nt)
        xp = _with_halo(xp_ref, xm_ref, xn_ref, i, nt)
        dx = pltpu.roll(dp, n - 1, 0)[mid] * w_ref[0:1, :]
        dx = dx + dp[mid] * w_ref[1:2, :]
        dx = dx + pltpu.roll(dp, 1, 0)[mid] * w_ref[2:3, :]
        dx = dx + pltpu.roll(dp, 2, 0)[mid] * w_ref[3:4, :]
        dx_ref[...] = dx.astype(BF16)
        d = dp[mid]

        @pl.when(i == 0)
        def _():
            dw_ref[...] = jnp.zeros_like(dw_ref)
            db_ref[...] = jnp.zeros_like(db_ref)
        dw_ref[0:1, :] += _sum0(d * pltpu.roll(xp, 1, 0)[mid])
        dw_ref[1:2, :] += _sum0(d * xp[mid])
        dw_ref[2:3, :] += _sum0(d * pltpu.roll(xp, n - 1, 0)[mid])
        dw_ref[3:4, :] += _sum0(d * pltpu.roll(xp, n - 2, 0)[mid])
        db_ref[...] += _sum0(d)

    return _pc(body, name=name, grid=(nt,),
               in_specs=[*_halo_specs(t, tm), *_halo_specs(t, tm), *_halo_specs(t, tm), _full((4, LRU_W))],
               out_specs=[pl.BlockSpec((tm, LRU_W), lambda i: (i, 0)), _full((4, LRU_W)), _full((1, LRU_W))],
               out_shape=[_sds((t, LRU_W), BF16), _sds((4, LRU_W)), _sds((1, LRU_W))],
               compiler_params=_params("arbitrary"))(dxc_a, dxc_a, dxc_a, dxc_b, dxc_b, dxc_b, xr, xr, xr, cw)


def _local_scan(a, b, reverse):
    n = a.shape[0]
    row = lax.broadcasted_iota(jnp.int32, a.shape, 0) & (SUBLANES - 1)
    for s in (1, 2, 4):
        if reverse:
            a_s, b_s, ok = pltpu.roll(a, n - s, 0), pltpu.roll(b, n - s, 0), row < SUBLANES - s
        else:
            a_s, b_s, ok = pltpu.roll(a, s, 0), pltpu.roll(b, s, 0), row >= s
        b = a * jnp.where(ok, b_s, 0.0) + b
        a = a * jnp.where(ok, a_s, 1.0)
    return a, b


def _carry_scan(a_s, b_s, out_ref, carry, reverse):
    ng = a_s.shape[0] // SUBLANES
    shape = carry.shape

    def step(g, cr):
        gg = (ng - 1 - g) if reverse else g
        off = pl.multiple_of(gg * SUBLANES, SUBLANES)
        h = a_s[pl.ds(off, SUBLANES), :] * cr + b_s[pl.ds(off, SUBLANES), :]
        out_ref[pl.ds(off, SUBLANES), :] = h
        edge = h[0:1, :] if reverse else h[SUBLANES - 1:SUBLANES, :]
        return jnp.broadcast_to(edge, shape)

    return lax.fori_loop(0, ng, step, carry)


def _lru_gates(xc, wa_ref, wx_ref, ba, bx, lam):
    xb = xc.astype(BF16)
    r = _sigmoid(_dot(xb, wa_ref[...]) + ba)
    ig = _sigmoid(_dot(xb, wx_ref[...]) + bx)
    sp = _softplus(-lam)
    la = -LRU_C * r * sp
    a = jnp.exp(la)
    mult = jnp.sqrt(_neg_expm1(2.0 * la, a * a))
    return r, ig, sp, a, mult


def _lru_fwd(xc, wa, wx, ba, bx, lam, h0, reverse, name, comms=()):
    t = xc.shape[0]
    tm = _tile(t, True)
    nt = t // tm
    tidx = (lambda i: (nt - 1 - i, 0)) if reverse else (lambda i: (i, 0))

    def body(x_ref, wa_ref, wx_ref, ba_ref, bx_ref, lam_ref, h0_ref, h_ref, a_s, b_s, c_s):
        @pl.when(pl.program_id(0) == 0)
        def _():
            c_s[...] = jnp.broadcast_to(h0_ref[...], c_s.shape)
        xv = x_ref[...]
        _, ig, _, a, mult = _lru_gates(xv, wa_ref, wx_ref, ba_ref[...], bx_ref[...], lam_ref[...])
        al, bl = _local_scan(a, mult * (ig * xv), reverse)
        a_s[...] = al
        b_s[...] = bl
        c_s[...] = _carry_scan(a_s, b_s, h_ref, c_s[...], reverse)

    vec = _full((1, LRU_W))
    mat = _full((LRU_W, LRU_W))
    (h,), couts = _call(body, name=name, grid=(nt,),
                        in_specs=[pl.BlockSpec((tm, LRU_W), tidx), mat, mat, vec, vec, vec, vec],
                        out_specs=[pl.BlockSpec((tm, LRU_W), tidx)], out_shape=[_sds((t, LRU_W))],
                        scratch_shapes=[pltpu.VMEM((tm, LRU_W), F32), pltpu.VMEM((tm, LRU_W), F32),
                                        pltpu.VMEM((SUBLANES, LRU_W), F32)],
                        sem=("arbitrary",), args=(xc, wa, wx, ba, bx, lam, h0), comms=comms)
    return (h, couts) if comms else h


def _lru_bwd(xc, wa, wx, ba, bx, lam, h, h0, dh, reverse, name, comms=()):
    t = xc.shape[0]
    tm = _tile(t, True)
    nt = t // tm
    n8 = tm // SUBLANES
    last8 = t // SUBLANES - 1
    tidx = (lambda i: (i, 0)) if reverse else (lambda i: (nt - 1 - i, 0))
    if reverse:
        halo = pl.BlockSpec((SUBLANES, LRU_W), lambda i: (jnp.minimum((i + 1) * n8, last8), 0))
    else:
        halo = pl.BlockSpec((SUBLANES, LRU_W), lambda i: (jnp.maximum((nt - 1 - i) * n8 - 1, 0), 0))

    def body(x_ref, wa_ref, wx_ref, ba_ref, bx_ref, lam_ref, h_ref, halo_ref, h0_ref, dh_ref,
             dx_ref, dpre_ref, dba_ref, dbx_ref, dlam_ref, dh0_ref, a_s, b_s, l_s, c_s, e_s):
        i = pl.program_id(0)

        @pl.when(i == 0)
        def _():
            c_s[...] = jnp.zeros_like(c_s)
            e_s[...] = jnp.zeros_like(e_s)
            dba_ref[...] = jnp.zeros_like(dba_ref)
            dbx_ref[...] = jnp.zeros_like(dbx_ref)
            dlam_ref[...] = jnp.zeros_like(dlam_ref)
        xv = x_ref[...]
        lam = lam_ref[...]
        r, ig, sp, a, mult = _lru_gates(xv, wa_ref, wx_ref, ba_ref[...], bx_ref[...], lam)
        hv = h_ref[...]
        rowi = lax.broadcasted_iota(jnp.int32, (tm, LRU_W), 0)
        edge_a = jnp.broadcast_to(e_s[0:1, :], (tm, LRU_W))
        h0b = jnp.broadcast_to(h0_ref[...], (tm, LRU_W))
        if reverse:
            a_sh = jnp.where(rowi == 0, edge_a, pltpu.roll(a, 1, 0))
            hin_edge = jnp.where(i == nt - 1, h0b, jnp.broadcast_to(halo_ref[0:1, :], (tm, LRU_W)))
            h_in = jnp.where(rowi == tm - 1, hin_edge, pltpu.roll(hv, tm - 1, 0))
        else:
            a_sh = jnp.where(rowi == tm - 1, edge_a, pltpu.roll(a, tm - 1, 0))
            hin_edge = jnp.where(i == nt - 1, h0b, jnp.broadcast_to(halo_ref[SUBLANES - 1:SUBLANES, :], (tm, LRU_W)))
            h_in = jnp.where(rowi == 0, hin_edge, pltpu.roll(hv, 1, 0))
        al, bl = _local_scan(a_sh, dh_ref[...], not reverse)
        a_s[...] = al
        b_s[...] = bl
        c_s[...] = _carry_scan(a_s, b_s, l_s, c_s[...], not reverse)
        e_s[...] = jnp.broadcast_to(a[tm - 1:tm, :] if reverse else a[0:1, :], e_s.shape)
        lmb = l_s[...]
        da = lmb * h_in
        ixc = ig * xv
        dmult = lmb * ixc
        dixc = lmb * mult
        dla = da * a - dmult * (a * a) / mult
        dpr = dla * (-LRU_C * sp) * r * (1.0 - r)
        dpi = dixc * xv * ig * (1.0 - ig)
        dprb = dpr.astype(BF16)
        dpib = dpi.astype(BF16)
        dpre_ref[:, 0:LRU_W] = dprb
        dpre_ref[:, LRU_W:2 * LRU_W] = dpib
        dx_ref[...] = dixc * ig + _dot_nt(dprb, wa_ref[...]) + _dot_nt(dpib, wx_ref[...])
        dba_ref[...] += _sum0(dpr)
        dbx_ref[...] += _sum0(dpi)
        dlam_ref[...] += _sum0(dla * (-LRU_C * r)) * (-_sigmoid(-lam))

        @pl.when(i == nt - 1)
        def _():
            al0 = a * lmb
            dh0_ref[...] = al0[tm - 1:tm, :] if reverse else al0[0:1, :]

    vec = _full((1, LRU_W))
    mat = _full((LRU_W, LRU_W))
    tile = pl.BlockSpec((tm, LRU_W), tidx)
    return _call(body, name=name, grid=(nt,),
                 in_specs=[tile, mat, mat, vec, vec, vec, tile, halo, vec, tile],
                 out_specs=[tile, pl.BlockSpec((tm, 2 * LRU_W), tidx), vec, vec, vec, vec],
                 out_shape=[_sds((t, LRU_W)), _sds((t, 2 * LRU_W), BF16), _sds((1, LRU_W)), _sds((1, LRU_W)),
                            _sds((1, LRU_W)), _sds((1, LRU_W))],
                 scratch_shapes=[pltpu.VMEM((tm, LRU_W), F32), pltpu.VMEM((tm, LRU_W), F32),
                                 pltpu.VMEM((tm, LRU_W), F32), pltpu.VMEM((SUBLANES, LRU_W), F32),
                                 pltpu.VMEM((SUBLANES, LRU_W), F32)],
                 sem=("arbitrary",), args=(xc, wa, wx, ba, bx, lam, h, h, h0, dh), comms=comms)


def _decay_tables(lg, reverse):
    ci = lax.broadcasted_iota(jnp.int32, (CHUNK, CHUNK), 0).astype(F32)
    mi = lax.broadcasted_iota(jnp.int32, (CHUNK, CHUNK), 1).astype(F32)
    rel = (mi - ci) if reverse else (ci - mi)
    relc = jnp.maximum(rel, 0.0)
    lg_c = jnp.concatenate([lg] * (CHUNK // LANES), axis=1)
    dm = jnp.where(rel >= 0, jnp.exp(lg_c * relc), 0.0)
    cd = lax.broadcasted_iota(jnp.int32, (CHUNK, DH), 0).astype(F32)
    pq, ps = (CHUNK - cd, cd) if reverse else (cd + 1.0, CHUNK - 1.0 - cd)
    return relc, dm, jnp.exp(lg * pq), jnp.exp(lg * ps), jnp.exp(lg * float(CHUNK)), pq, ps


def _ret_fwd(proj, lgv, s0f, s0b, comms=()):
    t = proj.shape[0]
    n = t // CHUNK

    def one(q, k, v, lg, s_s, hh, o_ref, sp_ref, reverse):
        _, dm, wq, ws, g, _, _ = _decay_tables(lg, reverse)
        vb = v.astype(BF16)
        p = _dot_nt(q.astype(BF16), k.astype(BF16)) * dm
        s = s_s[hh]
        sp_ref[hh, 0] = s
        o_ref[:, DH * hh:DH * (hh + 1)] = _dot(p.astype(BF16), vb) + _dot((q * wq).astype(BF16), s.astype(BF16))
        s_s[hh] = g * s + _dot_tn((k * ws).astype(BF16), vb)

    def body(qf, kf, vf, qb, kb, vb, lg_ref, s0f_ref, s0b_ref, of_ref, ob_ref, spf_ref, spb_ref, sf_s, sb_s):
        @pl.when(pl.program_id(0) == 0)
        def _():
            sf_s[...] = s0f_ref[...]
            sb_s[...] = s0b_ref[...]
        for hh in range(HEADS):
            sl = slice(DH * hh, DH * (hh + 1))
            one(qf[:, sl].astype(F32), kf[:, sl].astype(F32), vf[:, sl], lg_ref[hh, 0:1, :], sf_s, hh, of_ref, spf_ref,
                False)
            one(qb[:, sl].astype(F32), kb[:, sl].astype(F32), vb[:, sl], lg_ref[hh, 1:2, :], sb_s, hh, ob_ref, spb_ref,
                True)

    blk = (CHUNK, RET_W)
    fw = [pl.BlockSpec(blk, lambda i, o=o: (i, o)) for o in range(3)]
    bw = [pl.BlockSpec(blk, lambda i, o=o: (n - 1 - i, o)) for o in range(3)]
    st = _full((HEADS, DH, DH))
    return _call(body, name="ret_fwd", grid=(n,),
                 in_specs=fw + bw + [_full((HEADS, 2, LANES)), st, st],
                 out_specs=[pl.BlockSpec(blk, lambda i: (i, 0)), pl.BlockSpec(blk, lambda i: (n - 1 - i, 0)),
                            pl.BlockSpec((HEADS, 1, DH, DH), lambda i: (0, i, 0, 0)),
                            pl.BlockSpec((HEADS, 1, DH, DH), lambda i: (0, n - 1 - i, 0, 0))],
                 out_shape=[_sds((t, RET_W)), _sds((t, RET_W)), _sds((HEADS, n, DH, DH)), _sds((HEADS, n, DH, DH))],
                 scratch_shapes=[pltpu.VMEM((HEADS, DH, DH), F32), pltpu.VMEM((HEADS, DH, DH), F32)],
                 sem=("arbitrary",), args=(proj, proj, proj, proj, proj, proj, lgv, s0f, s0b), comms=comms)


def _ret_bwd(proj, lgv, sgv, sprev, do, reverse, name, comms=()):
    t = proj.shape[0]
    n = t // CHUNK
    d = 1 if reverse else 0
    cidx = (lambda i: i) if reverse else (lambda i: n - 1 - i)

    def body(q_ref, k_ref, v_ref, lg_ref, sg_ref, s_ref, do_ref, dq_ref, dk_ref, dv_ref, ds0_ref, drd_ref, ds_s, acc_s):
        i = pl.program_id(0)

        @pl.when(i == 0)
        def _():
            ds_s[...] = jnp.zeros_like(ds_s)
            acc_s[...] = jnp.zeros_like(acc_s)
        for hh in range(HEADS):
            sl = slice(DH * hh, DH * (hh + 1))
            relc, dm, wq, ws, g, pq, ps = _decay_tables(lg_ref[hh, d:d + 1, :], reverse)
            qb, kb, vb = q_ref[:, sl], k_ref[:, sl], v_ref[:, sl]
            q, k = qb.astype(F32), kb.astype(F32)
            p = _dot_nt(qb, kb) * dm
            s = s_ref[hh, 0]
            dob = do_ref[:, sl].astype(BF16)
            dsn = ds_s[hh]
            dsb = dsn.astype(BF16)
            dv_ref[:, sl] = (_dot_tn(p.astype(BF16), dob) + _dot((k * ws).astype(BF16), dsb)).astype(BF16)
            dp = _dot_nt(dob, vb)
            dab = (dp * dm).astype(BF16)
            xq = _dot_nt(dob, s.astype(BF16))
            yk = _dot_nt(vb, dsb)
            dq_ref[:, sl] = (_dot(dab, kb) + xq * wq).astype(BF16)
            dk_ref[:, sl] = (_dot_tn(dab, qb) + yk * ws).astype(BF16)
            ds_s[hh] = g * dsn + _dot_tn((q * wq).astype(BF16), dob)
            s_mask = _sum0(dp * p * relc)
            part = (sum(s_mask[:, LANES * u:LANES * (u + 1)] for u in range(CHUNK // LANES))
                    + _sum0(xq * q * wq * pq) + _sum0(yk * k * ws * ps) + _sum0(dsn * s) * g * float(CHUNK))
            acc_s[hh] += jnp.broadcast_to(part, (SUBLANES, LANES))

        @pl.when(i == n - 1)
        def _():
            ds0_ref[...] = ds_s[...]
            for hh in range(HEADS):
                tot = jnp.sum(acc_s[hh, 0:1, :], axis=1, keepdims=True)
                drd_ref[hh] = jnp.broadcast_to(tot, (SUBLANES, LANES)) * sg_ref[hh, d:d + 1, :]

    blk = (CHUNK, RET_W)
    qkv = [pl.BlockSpec(blk, lambda i, o=o: (cidx(i), o)) for o in range(3)]
    hc = pl.BlockSpec(blk, lambda i: (cidx(i), 0))
    lane = _full((HEADS, 2, LANES))
    return _call(body, name=name, grid=(n,),
                 in_specs=qkv + [lane, lane, pl.BlockSpec((HEADS, 1, DH, DH), lambda i: (0, cidx(i), 0, 0)), hc],
                 out_specs=[hc, hc, hc, _full((HEADS, DH, DH)), _full((HEADS, SUBLANES, LANES))],
                 out_shape=[_sds((t, RET_W), BF16)] * 3 + [_sds((HEADS, DH, DH)), _sds((HEADS, SUBLANES, LANES))],
                 scratch_shapes=[pltpu.VMEM((HEADS, DH, DH), F32), pltpu.VMEM((HEADS, SUBLANES, LANES), F32)],
                 sem=("arbitrary",), args=(proj, proj, proj, lgv, sgv, sprev, do), comms=comms)


def _ctx_weights(lg, l_len, reverse):
    pos = lax.broadcasted_iota(jnp.int32, (l_len, DH), 0).astype(F32)
    steps = pos if reverse else (l_len - 1.0 - pos)
    return jnp.exp(lg * steps), steps


def _ctx_state_fwd(projc, lgv):
    l_len = projc.shape[0]

    def body(k_ref, v_ref, lg_ref, sf_ref, sb_ref):
        k = k_ref[...]
        vb = v_ref[...].astype(BF16)
        for d, o_ref in ((0, sf_ref), (1, sb_ref)):
            w, _ = _ctx_weights(lg_ref[0, d:d + 1, :], l_len, d == 1)
            o_ref[0] = _dot_tn((k * w).astype(BF16), vb)

    st = pl.BlockSpec((1, DH, DH), lambda h: (h, 0, 0))
    return _pc(body, name="ctx_state_fwd", grid=(HEADS,),
               in_specs=[pl.BlockSpec((l_len, DH), lambda h: (0, HEADS + h)),
                         pl.BlockSpec((l_len, DH), lambda h: (0, 2 * HEADS + h)),
                         pl.BlockSpec((1, 2, LANES), lambda h: (h, 0, 0))],
               out_specs=[st, st], out_shape=[_sds((HEADS, DH, DH))] * 2,
               compiler_params=_params("arbitrary"))(projc, projc, lgv)


def _ctx_state_bwd(projc, lgv, sgv, dsf, dsb):
    l_len = projc.shape[0]

    def body(k_ref, v_ref, lg_ref, sg_ref, dsf_ref, dsb_ref, dk_ref, dv_ref, drd_ref):
        k = k_ref[...]
        vb = v_ref[...].astype(BF16)
        dk = jnp.zeros((l_len, DH), F32)
        dv = jnp.zeros((l_len, DH), F32)
        rows = []
        for d, ds_ref in ((0, dsf_ref), (1, dsb_ref)):
            w, steps = _ctx_weights(lg_ref[0, d:d + 1, :], l_len, d == 1)
            dsb16 = ds_ref[0].astype(BF16)
            dkw = _dot_nt(vb, dsb16)
            dk = dk + dkw * w
            dv = dv + _dot((k * w).astype(BF16), dsb16)
            tot = jnp.sum(_sum0(dkw * k * w * steps), axis=1, keepdims=True)
            rows.append(jnp.broadcast_to(tot, (1, LANES)) * sg_ref[0, d:d + 1, :])
        dk_ref[...] = dk.astype(BF16)
        dv_ref[...] = dv.astype(BF16)
        rid = lax.broadcasted_iota(jnp.int32, (SUBLANES, LANES), 0)
        drd_ref[0] = jnp.where(rid == 0, rows[0], jnp.where(rid == 1, rows[1], 0.0))

    st = pl.BlockSpec((1, DH, DH), lambda h: (h, 0, 0))
    lane = pl.BlockSpec((1, 2, LANES), lambda h: (h, 0, 0))
    hc = pl.BlockSpec((l_len, DH), lambda h: (0, h))
    return _pc(body, name="ctx_state_bwd", grid=(HEADS,),
               in_specs=[pl.BlockSpec((l_len, DH), lambda h: (0, HEADS + h)),
                         pl.BlockSpec((l_len, DH), lambda h: (0, 2 * HEADS + h)), lane, lane, st, st],
               out_specs=[hc, hc, pl.BlockSpec((1, SUBLANES, LANES), lambda h: (h, 0, 0))],
               out_shape=[_sds((l_len, RET_W), BF16), _sds((l_len, RET_W), BF16), _sds((HEADS, SUBLANES, LANES))],
               compiler_params=_params("arbitrary"))(projc, projc, lgv, sgv, dsf, dsb)


G_BLOCK = (3 * RET_W) // RET_W
GATE_BLOCK = (4 * RET_W + LRU_W) // LRU_W


def _head_norm(y):
    yc = y - jnp.mean(y, axis=-1, keepdims=True)
    rs = lax.rsqrt(jnp.mean(yc * yc, axis=-1, keepdims=True) + EPS)
    return yc * rs, rs


def _gelu_parts(z):
    th = jnp.tanh(GELU_K * (z + GELU_C * z * z * z))
    return 0.5 * z * (1.0 + th), th


def _mix_fwd(o_f, o_b, proj, hf, hb, w_out, x, g1):
    t = x.shape[0]
    tm = _tile(t, True)

    def body(of_ref, ob_ref, g_ref, gt_ref, hf_ref, hb_ref, w_ref, x_ref, g1_ref, x1_ref, cat_ref):
        o = of_ref[...] + ob_ref[...]
        g = g_ref[...].astype(F32)
        for hh in range(HEADS):
            sl = slice(DH * hh, DH * (hh + 1))
            nrm, _ = _head_norm(o[:, sl])
            gh = g[:, sl]
            cat_ref[:, sl] = (gh * _sigmoid(gh) * nrm).astype(BF16)
        gel, _ = _gelu_parts(gt_ref[...].astype(F32))
        cat_ref[:, RET_W:] = ((hf_ref[...] + hb_ref[...]) * gel).astype(BF16)
        x1_ref[...] = x_ref[...] + g1_ref[...] * _dot(cat_ref[...], w_ref[...])

    half = pl.BlockSpec((tm, RET_W), lambda i: (i, 0))
    big = pl.BlockSpec((tm, D_MODEL), lambda i: (i, 0))
    return _pc(body, name="mix_fwd", grid=(t // tm,),
               in_specs=[half, half, pl.BlockSpec((tm, RET_W), lambda i: (i, G_BLOCK)),
                         pl.BlockSpec((tm, LRU_W), lambda i: (i, GATE_BLOCK)), half, half,
                         _full((D_MODEL, D_MODEL)), big, _full((1, D_MODEL))],
               out_specs=[big, big], out_shape=[_sds((t, D_MODEL)), _sds((t, D_MODEL), BF16)],
               compiler_params=_params("arbitrary"))(o_f, o_b, proj, proj, hf, hb, w_out, x, g1)


def _mix_bwd(o_f, o_b, proj, hf, hb, w_out, cat, dx1, g1, comms=()):
    t = dx1.shape[0]
    tm = _tile(t, True)

    def body(of_ref, ob_ref, g_ref, gt_ref, hf_ref, hb_ref, w_ref, cat_ref, dx1_ref, g1_ref,
             do_ref, dhs_ref, dg_ref, dgt_ref, dyb_ref, dg1_ref):
        dx1v = dx1_ref[...]
        y = _dot(cat_ref[...], w_ref[...])

        @pl.when(pl.program_id(0) == 0)
        def _():
            dg1_ref[...] = jnp.zeros_like(dg1_ref)
        dg1_ref[...] += _sum0(dx1v * y)
        dyb = (g1_ref[...] * dx1v).astype(BF16)
        dyb_ref[...] = dyb
        dcat = _dot_nt(dyb, w_ref[...])
        o = of_ref[...] + ob_ref[...]
        g = g_ref[...].astype(F32)
        for hh in range(HEADS):
            sl = slice(DH * hh, DH * (hh + 1))
            nrm, rs = _head_norm(o[:, sl])
            gh = g[:, sl]
            sg = _sigmoid(gh)
            dret = dcat[:, sl]
            dg_ref[:, sl] = (dret * nrm * (sg * (1.0 + gh * (1.0 - sg)))).astype(BF16)
            dn = dret * (gh * sg)
            dyc = rs * (dn - nrm * jnp.mean(dn * nrm, axis=-1, keepdims=True))
            do_ref[:, sl] = (dyc - jnp.mean(dyc, axis=-1, keepdims=True)).astype(BF16)
        z = gt_ref[...].astype(F32)
        gel, th = _gelu_parts(z)
        dlru = dcat[:, RET_W:]
        dhs_ref[...] = dlru * gel
        dgel = 0.5 * (1.0 + th) + 0.5 * z * (1.0 - th * th) * GELU_K * (1.0 + 3.0 * GELU_C * z * z)
        dgt_ref[...] = (dlru * (hf_ref[...] + hb_ref[...]) * dgel).astype(BF16)

    half = pl.BlockSpec((tm, RET_W), lambda i: (i, 0))
    big = pl.BlockSpec((tm, D_MODEL), lambda i: (i, 0))
    return _call(body, name="mix_bwd", grid=(t // tm,),
                 in_specs=[half, half, pl.BlockSpec((tm, RET_W), lambda i: (i, G_BLOCK)),
                           pl.BlockSpec((tm, LRU_W), lambda i: (i, GATE_BLOCK)), half, half,
                           _full((D_MODEL, D_MODEL)), big, big, _full((1, D_MODEL))],
                 out_specs=[half, half, half, half, big, _full((1, D_MODEL))],
                 out_shape=[_sds((t, RET_W), BF16), _sds((t, RET_W)), _sds((t, RET_W), BF16), _sds((t, RET_W), BF16),
                            _sds((t, D_MODEL), BF16), _sds((1, D_MODEL))],
                 scratch_shapes=[], sem=("arbitrary",), args=(o_f, o_b, proj, proj, hf, hb, w_out, cat, dx1, g1),
                 comms=comms)


def _mlp(x1, n2g, sh2, sc2, g2, fg, w1_parts, w2_parts, tgt):
    t = x1.shape[0]
    tm = _tile(t)
    hb_ = MLP_H // N_CHIP
    q_rows = hb_ // 4
    n_cp = 4 * N_DEV

    def body(x1_ref, n2g_ref, sh2_ref, sc2_ref, g2_ref, fg_ref, w1a, w1b, w2a, w2b, tgt_ref,
             dx1_ref, h2b_ref, ab_ref, dub_ref, dmb_ref, dsc_ref, dsh_ref, dg2_ref, dn2_ref, dfg_ref, loss_ref,
             w1_s, w2_s, r_s, sems):
        @pl.when(pl.program_id(0) == 0)
        def _():
            cps = []
            for p, parts in enumerate(((w1a, w2a), (w1b, w2b))):
                for d in range(N_DEV):
                    rows = pl.ds(2 * q_rows * (d % 2) + q_rows * p, q_rows)
                    for src, dst in zip(parts, (w1_s, w2_s)):
                        cps.append(pltpu.make_async_copy(src.at[d], dst.at[d // 2, rows], sems.at[len(cps)]))
            for cp in cps:
                cp.start()
            for r in (dsc_ref, dsh_ref, dg2_ref, dn2_ref, dfg_ref, loss_ref):
                r[...] = jnp.zeros_like(r)
            for cp in cps:
                cp.wait()
        x1v = x1_ref[...]
        n2g, sc2, g2, fg = n2g_ref[...], sc2_ref[...], g2_ref[...], fg_ref[...]
        xh, _ = _rms(x1v)
        h2b = (xh * n2g * (1.0 + sc2) + sh2_ref[...]).astype(BF16)
        h2b_ref[...] = h2b
        m = jnp.zeros((tm, D_MODEL), F32)
        for j in range(N_CHIP):
            sl = slice(hb_ * j, hb_ * (j + 1))
            r = jnp.maximum(_dot(h2b, w1_s[j]), 0.0)
            r_s[:, sl] = r
            ab = (r * r).astype(BF16)
            ab_ref[:, sl] = ab
            m = m + _dot(ab, w2_s[j])
        x2 = x1v + g2 * m
        x2h, r2 = _rms(x2)
        err = x2h * fg - tgt_ref[...]
        loss_ref[...] += _sum0(err * err)
        dout = err * (1.0 / D_MODEL)
        dfg_ref[...] += _sum0(dout * x2h)
        dxh = dout * fg
        dx2 = r2 * (dxh - x2h * jnp.mean(dxh * x2h, axis=-1, keepdims=True))
        dg2_ref[...] += _sum0(dx2 * m)
        dmb = (g2 * dx2).astype(BF16)
        dmb_ref[...] = dmb
        dh2 = jnp.zeros((tm, D_MODEL), F32)
        for j in range(N_CHIP):
            sl = slice(hb_ * j, hb_ * (j + 1))
            dub = (_dot_nt(dmb, w2_s[j]) * (2.0 * r_s[:, sl])).astype(BF16)
            dub_ref[:, sl] = dub
            dh2 = dh2 + _dot_nt(dub, w1_s[j])
        dx, dn2_t, dsh_t, dsc_t = _norm_mod_bwd(x1v, n2g, sc2, dh2)
        dx1_ref[...] = dx2 + dx
        dn2_ref[...] += dn2_t
        dsh_ref[...] += dsh_t
        dsc_ref[...] += dsc_t

        @pl.when(pl.program_id(0) == t // tm - 1)
        def _():
            tot = jnp.sum(loss_ref[...], axis=1, keepdims=True) * (0.5 / D_MODEL)
            loss_ref[...] = jnp.broadcast_to(tot, loss_ref.shape)

    row = _full((1, D_MODEL))
    big = pl.BlockSpec((tm, D_MODEL), lambda i: (i, 0))
    wide = pl.BlockSpec((tm, MLP_H), lambda i: (i, 0))
    return _pc(body, name="mlp", grid=(t // tm,),
               in_specs=[big, row, row, row, row, row, ANY, ANY, ANY, ANY, big],
               out_specs=[big, big, wide, wide, big, row, row, row, row, row, row],
               out_shape=[_sds((t, D_MODEL)), _sds((t, D_MODEL), BF16), _sds((t, MLP_H), BF16), _sds((t, MLP_H), BF16),
                          _sds((t, D_MODEL), BF16)] + [_sds((1, D_MODEL))] * 6,
               scratch_shapes=[pltpu.VMEM((N_CHIP, D_MODEL, hb_), BF16), pltpu.VMEM((N_CHIP, hb_, D_MODEL), BF16),
                               pltpu.VMEM((tm, MLP_H), F32), pltpu.SemaphoreType.DMA((n_cp,))],
               compiler_params=_params("arbitrary"))(x1, n2g, sh2, sc2, g2, fg, *w1_parts, *w2_parts, tgt)


def _tn(a, b, nj, a_blocked, b_blocked, name, extra=None, comms=()):
    t = a.shape[0]
    m = a.shape[1] // (nj if a_blocked else 1)
    n = b.shape[1] // (nj if b_blocked else 1)
    bk = 1024 if t % 1024 == 0 else (512 if t % 512 == 0 else t)
    nk = t // bk
    a_map = (lambda j, k: (k, j)) if a_blocked else (lambda j, k: (k, 0))
    b_map = (lambda j, k: (k, j)) if b_blocked else (lambda j, k: (k, 0))
    in_specs = [pl.BlockSpec((bk, m), a_map), pl.BlockSpec((bk, n), b_map)]
    args = [a, b]
    if extra is not None:
        a2, b2 = extra
        t2 = a2.shape[0]
        in_specs += [pl.BlockSpec((t2, m), (lambda j, k: (0, j)) if a_blocked else (lambda j, k: (0, 0))),
                     pl.BlockSpec((t2, n), (lambda j, k: (0, j)) if b_blocked else (lambda j, k: (0, 0)))]
        args += [a2, b2]

    def body(*refs):
        a_ref, b_ref = refs[0], refs[1]
        o_ref, acc = refs[-2], refs[-1]
        k = pl.program_id(1)

        @pl.when(k == 0)
        def _():
            acc[...] = jnp.zeros_like(acc)
        acc[...] += _dot_tn(a_ref[...].astype(BF16), b_ref[...].astype(BF16))

        @pl.when(k == nk - 1)
        def _():
            if extra is not None:
                acc[...] += _dot_tn(refs[2][...].astype(BF16), refs[3][...].astype(BF16))
            o_ref[0] = acc[...]

    (out,), couts = _call(body, name=name, grid=(nj, nk), in_specs=in_specs,
                          out_specs=[pl.BlockSpec((1, m, n), lambda j, k: (j, 0, 0))], out_shape=[_sds((nj, m, n))],
                          scratch_shapes=[pltpu.VMEM((m, n), F32)], sem=("arbitrary", "arbitrary"), args=args,
                          comms=comms)
    return (out, couts) if comms else out


ROW_LOSS = 0
ROW_DMOD = 1
ROW_DMODC = 7
ROW_N1, ROW_N2, ROW_FG, ROW_CB = 9, 10, 11, 12
ROW_BA, ROW_BX, ROW_LAM = 13, 15, 17
ROW_CW = 20
ROW_RD = 24
SLAB_ROWS = 32
SEG = D_MODEL // 2


def _pack_small(rows, drd, cw2, cb2, lru2, gates):
    n_rows, n_lru = len(rows), len(lru2)

    def body(*refs):
        r = refs[:n_rows]
        drd_f, drd_b, drd_c, cw_a, cw_b, cb_a, cb_b = refs[n_rows:n_rows + 7]
        lru = refs[n_rows + 7:n_rows + 7 + n_lru]
        gf_ref, gb_ref, slab, ga, gx = refs[n_rows + 7 + n_lru:]
        slab[...] = jnp.zeros_like(slab)
        slab[ROW_LOSS:ROW_LOSS + 1, :] = r[0][...]
        for k in range(N_MOD):
            slab[ROW_DMOD + k:ROW_DMOD + k + 1, :] = r[1 + k][...]
        slab[ROW_DMODC:ROW_DMODC + 1, :] = r[7][...]
        slab[ROW_DMODC + 1:ROW_DMODC + 2, :] = r[8][...]
        slab[ROW_N1:ROW_N1 + 1, :] = r[9][...] + r[10][...]
        slab[ROW_N2:ROW_N2 + 1, :] = r[11][...]
        slab[ROW_FG:ROW_FG + 1, :] = r[12][...]
        slab[ROW_CB:ROW_CB + 1, 0:LRU_W] = cb_a[...] + cb_b[...]
        for k, row in enumerate((ROW_BA, ROW_BA + 1, ROW_BX, ROW_BX + 1, ROW_LAM, ROW_LAM + 1)):
            slab[row:row + 1, 0:LRU_W] = lru[2 * k][...] + lru[2 * k + 1][...]
        slab[ROW_CW:ROW_CW + 4, 0:LRU_W] = cw_a[...] + cw_b[...]
        for h in range(HEADS):
            slab[ROW_RD + h:ROW_RD + h + 1, 0:LANES] = drd_f[h, 0:1, :] + drd_c[h, 0:1, :]
            slab[ROW_RD + HEADS + h:ROW_RD + HEADS + h + 1, 0:LANES] = drd_b[h, 0:1, :] + drd_c[h, 1:2, :]
        for d, g_ref in enumerate((gf_ref, gb_ref)):
            for n in range(LRU_BLOCKS):
                blk = slice(LRU_BD * n, LRU_BD * (n + 1))
                ga[blk, LRU_BD * d:LRU_BD * (d + 1)] = g_ref[0, blk, blk].astype(BF16)
                gx[blk, LRU_BD * d:LRU_BD * (d + 1)] = g_ref[1, blk, blk].astype(BF16)

    args = list(rows) + list(drd) + list(cw2) + list(cb2) + list(lru2) + list(gates)
    gate_shape = (LRU_W, 2 * LRU_BD)
    return _pc(body, name="pack_small", in_specs=[_full(a.shape) for a in args],
               out_specs=[_full((SLAB_ROWS, D_MODEL)), _full(gate_shape), _full(gate_shape)],
               out_shape=[_sds((SLAB_ROWS, D_MODEL)), _sds(gate_shape, BF16), _sds(gate_shape, BF16)],
               compiler_params=_params())(*args)


def _adam_math(w, g, m, v):
    mn = ADAM_B1 * m + (1.0 - ADAM_B1) * g
    vn = ADAM_B2 * v + (1.0 - ADAM_B2) * (g * g)
    mh = mn / (1.0 - ADAM_B1 ** ADAM_STEP)
    vh = vn / (1.0 - ADAM_B2 ** ADAM_STEP)
    return -ADAM_LR * (mh / (jnp.sqrt(vh) + ADAM_EPS) + ADAM_WD * w), mn, vn


SMALL_PARAMS = ("b_ada", "norm1_g", "norm2_g", "final_g", "ret_decay", "conv_w", "conv_b", "lru_wa", "lru_ba", "lru_wx",
                "lru_bx", "lru_lambda")


def _finalize_small(chip_idx, slab_all, ga_all, gx_all, wmv):
    n_p = len(SMALL_PARAMS)
    flat = [a for nm in SMALL_PARAMS for a in wmv[nm]]
    ada_n = N_MOD * D_MODEL // N_CHIP

    def body(c_ref, slab_ref, ga_ref, gx_ref, *refs):
        prm = {nm: refs[3 * k:3 * k + 3] for k, nm in enumerate(SMALL_PARAMS)}
        outs = {nm: refs[3 * n_p + 4 * k:3 * n_p + 4 * k + 4] for k, nm in enumerate(SMALL_PARAMS)}
        b128_ref, dmc_ref, loss_ref = refs[3 * n_p + 4 * n_p:]
        chip = c_ref[0]

        def pick(fn):
            acc = fn(0)
            for j in range(1, N_CHIP):
                acc = jnp.where(chip == j, fn(j), acc)
            return acc

        tot = slab_ref[0]
        for d in range(1, N_DEV):
            tot = tot + slab_ref[d]

        def update(nm, g, sl=None, rows=None):
            w_ref, m_ref, v_ref = prm[nm]
            g_ref, d_ref, mo_ref, vo_ref = outs[nm]
            ix = (slice(None) if rows is None else rows, slice(None) if sl is None else sl)
            dl, mn, vn = _adam_math(w_ref[ix], g, m_ref[ix], v_ref[ix])
            g_ref[ix] = g
            d_ref[ix] = dl
            mo_ref[ix] = mn
            vo_ref[ix] = vn

        loss_ref[...] = jnp.broadcast_to(tot[ROW_LOSS:ROW_LOSS + 1, 0:LANES], (SUBLANES, LANES))
        for k in range(N_MOD):
            g = tot[ROW_DMOD + k:ROW_DMOD + k + 1, :]
            if k < 2:
                g = g + tot[ROW_DMODC + k:ROW_DMODC + k + 1, :]
            update("b_ada", g, slice(D_MODEL * k, D_MODEL * (k + 1)))
        update("norm1_g", tot[ROW_N1:ROW_N1 + 1, :])
        update("norm2_g", tot[ROW_N2:ROW_N2 + 1, :])
        update("final_g", tot[ROW_FG:ROW_FG + 1, :])
        update("ret_decay", tot[ROW_RD:ROW_RD + SUBLANES, 0:LANES])
        update("conv_b", tot[ROW_CB:ROW_CB + 1, 0:LRU_W])
        update("conv_w", pick(lambda j: tot[ROW_CW:ROW_CW + 4, LANES * j:LANES * (j + 1)]))
        for nm, row in (("lru_ba", ROW_BA), ("lru_bx", ROW_BX), ("lru_lambda", ROW_LAM)):
            update(nm, pick(lambda j, row=row: tot[row:row + 2, LANES * j:LANES * (j + 1)]))
        for nm, g_all in (("lru_wa", ga_ref), ("lru_wx", gx_ref)):
            for dr in range(2):
                lanes = slice(LRU_BD * dr, LRU_BD * (dr + 1))
                g = g_all[0, :, lanes].astype(F32)
                for d in range(1, N_DEV):
                    g = g + g_all[d, :, lanes].astype(F32)
                update(nm, g, rows=slice(LRU_W * dr, LRU_W * (dr + 1)))

        def seg(rows6, s):
            return rows6[s // 2][:, SEG * (s % 2):SEG * (s % 2 + 1)]

        b128_ref[...] = jnp.zeros_like(b128_ref)
        dmc_ref[...] = jnp.zeros_like(dmc_ref)
        zero = jnp.zeros((1, D_MODEL), F32)
        ctx6 = [tot[ROW_DMODC:ROW_DMODC + 1, :], tot[ROW_DMODC + 1:ROW_DMODC + 2, :]] + [zero] * (N_MOD - 2)
        for q in range(ada_n // SEG):
            cols = slice(SEG * q, SEG * (q + 1))
            for d in range(N_DEV):
                rows6 = [slab_ref[d, ROW_DMOD + k:ROW_DMOD + k + 1, :] for k in range(N_MOD)]
                b128_ref[d:d + 1, cols] = pick(lambda j, rows6=rows6: seg(rows6, 3 * j + q))
            c = pick(lambda j: seg(ctx6, 3 * j + q))
            b128_ref[N_DEV:N_DEV + 1, cols] = c
            dmc_ref[0:1, cols] = c

    out_shape = []
    for nm in SMALL_PARAMS:
        out_shape += [_sds(wmv[nm][0].shape)] * 4
    out_shape += [_sds((LANES, ada_n)), _sds((SUBLANES, ada_n)), _sds((SUBLANES, LANES))]
    args = [slab_all, ga_all, gx_all] + flat
    grid_spec = pltpu.PrefetchScalarGridSpec(
        num_scalar_prefetch=1, grid=(1,), in_specs=[_full(a.shape) for a in args],
        out_specs=[_full(s.shape) for s in out_shape])
    outs = _pc(body, name="finalize_small", grid_spec=grid_spec, out_shape=out_shape,
               compiler_params=_params("arbitrary"))(chip_idx, *args)
    res = {nm: tuple(outs[4 * k:4 * k + 4]) for k, nm in enumerate(SMALL_PARAMS)}
    return res, outs[4 * n_p], outs[4 * n_p + 1], outs[4 * n_p + 2]


def _block_diag(w):
    eye = jnp.eye(LRU_BLOCKS, dtype=F32)
    return (w[:, :, None, :] * eye[:, None, :, None]).reshape(LRU_W, LRU_W).astype(BF16)


def _lane_rep(v8):
    return jnp.broadcast_to(v8.reshape(SUBLANES, 1), (SUBLANES, LANES))


def kernel(x, c, ctx, c_ctx, w_ada, b_ada, norm1_g, norm2_g, w_in, ret_decay, conv_w, conv_b, lru_wa, lru_ba, lru_wx, lru_bx, lru_lambda, w_out, w_mlp1, w_mlp2, final_g, loss_target, m_c_ctx, m_w_ada, m_b_ada, m_norm1_g, m_norm2_g, m_w_in, m_ret_decay, m_conv_w, m_conv_b, m_lru_wa, m_lru_ba, m_lru_wx, m_lru_bx, m_lru_lambda, m_w_out, m_w_mlp1, m_w_mlp2, m_final_g, v_c_ctx, v_w_ada, v_b_ada, v_norm1_g, v_norm2_g, v_w_in, v_ret_decay, v_conv_w, v_conv_b, v_lru_wa, v_lru_ba, v_lru_wx, v_lru_bx, v_lru_lambda, v_w_out, v_w_mlp1, v_w_mlp2, v_final_g):
    ax, ay, ac = lax.axis_index("x"), lax.axis_index("y"), lax.axis_index("c")
    chip = 2 * ax + ay
    dev = 4 * ax + 2 * ay + ac
    c_idx = ac.reshape(1).astype(jnp.int32)
    j_idx = chip.reshape(1).astype(jnp.int32)

    xt = x[0]
    t_len = xt.shape[0]
    ctxt = ctx[0]
    l_len = ctxt.shape[0]
    tgt = loss_target[0]
    ada_n = w_ada.shape[2]

    def my_half(w2d):
        r = w2d.shape[0] // 2
        return lax.dynamic_slice_in_dim(w2d, ac * r, r, axis=0).astype(BF16)

    pad8 = lambda a: jnp.pad(a, ((0, SUBLANES - a.shape[0]), (0, 0)))
    small = jnp.concatenate([pad8(conv_w[0]), pad8(lru_ba[0]), pad8(lru_bx[0]), pad8(lru_lambda[0])], axis=0)
    gw_in, c_all, small_all = _all_gather([my_half(w_in[0]), pad8(c), small], "gather_head")
    w4 = gw_in.reshape(N_CHIP, D_MODEL, IN_COLS // N_CHIP)

    a16, lgv, sgv = _prep(c_all[:, 0, :], c_ctx, ret_decay[0])
    b_shard = lax.dynamic_slice_in_dim(b_ada, chip * ada_n, ada_n, axis=1)
    (mod_parts,) = _all_gather([_mod_fwd(a16, w_ada[0], b_shard)], "gather_mod")
    mod_all = mod_parts[0::2].transpose(1, 0, 2).reshape(16, N_CHIP * ada_n)
    mod_me = lax.dynamic_slice_in_dim(mod_all, dev, 1, axis=0)
    sh1, sc1, g1, sh2, sc2, g2 = [mod_me[:, D_MODEL * k:D_MODEL * (k + 1)] for k in range(N_MOD)]
    csh1, csc1 = mod_all[8:9, 0:D_MODEL], mod_all[8:9, D_MODEL:2 * D_MODEL]

    cos2, sin2 = _rotary_tables(t_len)
    cos_c, sin_c = jnp.ones((l_len, DH), F32), jnp.zeros((l_len, DH), F32)
    n1g, n2g = norm1_g, norm2_g
    fg = final_g.reshape(1, D_MODEL)

    small_full = small_all[0::2].transpose(1, 0, 2).reshape(4 * SUBLANES, LRU_W)
    cw = small_full[0:4]
    cb = conv_b
    ba_f, ba_b = small_full[8:9], small_full[9:10]
    bx_f, bx_b = small_full[16:17], small_full[17:18]
    lam_f, lam_b = small_full[24:25], small_full[25:26]
    wa_f, wa_b = _block_diag(lru_wa[0, 0]), _block_diag(lru_wa[0, 1])
    wx_f, wx_b = _block_diag(lru_wx[0, 0]), _block_diag(lru_wx[0, 1])
    zero_h = jnp.zeros((1, LRU_W), F32)

    projc, xrc, hcb16 = _inproj_fwd(ctxt, n1g, csh1, csc1, w4, cos_c, sin_c, "inproj_fwd_ctx")
    s_f, s_b = _ctx_state_fwd(projc, lgv)
    xcc = _conv_fwd(xrc, cw, cb, "conv_fwd_ctx")
    hcf = _lru_fwd(xcc, wa_f, wx_f, ba_f, bx_f, lam_f, zero_h, False, "lru_fwd_ctx_f")
    hcbk = _lru_fwd(xcc, wa_b, wx_b, ba_b, bx_b, lam_b, zero_h, True, "lru_fwd_ctx_b")
    lru_sf, lru_sb = hcf[l_len - 1:l_len], hcbk[0:1]

    h1, h2 = my_half(w_mlp1[0]), my_half(w_mlp2[0])
    q = h1.shape[0] // 2
    (proj, xrl, hb16), ((gw_1a,),) = _inproj_fwd(xt, n1g, sh1, sc1, w4, cos2, sin2, "inproj_fwd",
                                           comms=(_AllGather([h1[:q]]),))
    (o_f, o_b, spf, spb), ((gw_1b, gw_out),) = _ret_fwd(proj, lgv, s_f, s_b,
                                                       comms=(_AllGather([h1[q:], my_half(w_out[0])]),))
    xcl = _conv_fwd(xrl, cw, cb, "conv_fwd")
    hf, ((gw_2a,),) = _lru_fwd(xcl, wa_f, wx_f, ba_f, bx_f, lam_f, lru_sf, False, "lru_fwd_f",
                              comms=(_AllGather([h2[:q]]),))
    hbk, ((gw_2b,),) = _lru_fwd(xcl, wa_b, wx_b, ba_b, bx_b, lam_b, lru_sb, True, "lru_fwd_b",
                               comms=(_AllGather([h2[q:]]),))
    wo = gw_out.reshape(D_MODEL, D_MODEL)
    x1, cat = _mix_fwd(o_f, o_b, proj, hf, hbk, wo, xt, g1)

    (dx1, h2b, ab, dub, dmb, dsc2, dsh2, dg2, dn2g, dfg, lossv) = _mlp(
        x1, n2g, sh2, sc2, g2, fg, (gw_1a, gw_1b), (gw_2a, gw_2b), tgt)
    gw_mlp1 = _tn(h2b, dub, N_CHIP, False, True, "grad_w_mlp1")
    b_1 = gw_mlp1.reshape(N_DEV, D_MODEL // 2, MLP_H // N_CHIP)
    gw_mlp2, ((r_1,),) = _tn(ab, dmb, N_CHIP, True, False, "grad_w_mlp2", comms=(_pair_exchange([b_1]),))

    jc_idx = jnp.concatenate([j_idx, c_idx])
    b_2 = gw_mlp2.reshape(N_DEV, MLP_H // N_DEV, D_MODEL)
    p_1, pb_1 = _pair_add(b_1, r_1, c_idx, "rs_pair_add_w_mlp1")
    (do, dhs, dg, dgate, dyb, dg1), ((q_1,), (r_2,)) = _mix_bwd(
        o_f, o_b, proj, hf, hbk, wo, cat, dx1, g1, comms=(_chip_exchange([pb_1]), _pair_exchange([b_2])))
    gw_o = _tn(cat, dyb, 1, False, False, "grad_w_out")
    b_o = gw_o.reshape(N_DEV, D_MODEL // N_DEV, D_MODEL)
    p_2, pb_2 = _pair_add(b_2, r_2, c_idx, "rs_pair_add_w_mlp2")
    h_1 = _chip_add(p_1, q_1, jc_idx, "rs_chip_add_w_mlp1")

    (dq_f, dk_f, dv_f, ds_f, drd_f), ((r_o,), (f_1,)) = _ret_bwd(
        proj, lgv, sgv, spf, do, False, "ret_bwd_f", comms=(_pair_exchange([b_o]), _pair_gather([h_1])))
    p_o, pb_o = _pair_add(b_o, r_o, c_idx, "rs_pair_add_w_out")

    (dq_b, dk_b, dv_b, ds_b, drd_b), ((q_o,),) = _ret_bwd(
        proj, lgv, sgv, spb, do, True, "ret_bwd_b", comms=(_chip_exchange([pb_o]),))
    h_o = _chip_add(p_o, q_o, jc_idx, "rs_chip_add_w_out")

    (dxc_f, dpre_f, dba_f, dbx_f, dlam_f, dh0_f), ((q_2,), (f_o,)) = _lru_bwd(
        xcl, wa_f, wx_f, ba_f, bx_f, lam_f, hf, lru_sf, dhs, False, "lru_bwd_f",
        comms=(_chip_exchange([pb_2]), _pair_gather([h_o])))
    h_2 = _chip_add(p_2, q_2, jc_idx, "rs_chip_add_w_mlp2")
    (dxc_b, dpre_b, dba_b, dbx_b, dlam_b, dh0_b), ((f_2,),) = _lru_bwd(
        xcl, wa_b, wx_b, ba_b, bx_b, lam_b, hbk, lru_sb, dhs, True, "lru_bwd_b", comms=(_pair_gather([h_2]),))
    dxr, dcw, dcb = _conv_bwd(dxc_f, dxc_b, xrl, cw, "conv_bwd")
    grad_x, dpb, dn1g, dsh1, dsc1 = _inproj_bwd(
        xt, n1g, sh1, sc1, w4, cos2, sin2, [dq_f, dq_b, dk_f, dk_b, dv_f, dv_b, dg, dxr, dgate], dx1, "inproj_bwd")

    dkc, dvc, drd_c = _ctx_state_bwd(projc, lgv, sgv, ds_f, ds_b)
    zc = jnp.zeros((l_len, LRU_W), F32)
    dhc_f = lax.dynamic_update_slice(zc, dh0_f, (l_len - 1, 0))
    dhc_b = lax.dynamic_update_slice(zc, dh0_b, (0, 0))
    (dxcc_f, dprec_f, dbac_f, dbxc_f, dlamc_f, _), _ = _lru_bwd(
        xcc, wa_f, wx_f, ba_f, bx_f, lam_f, hcf, zero_h, dhc_f, False, "lru_bwd_ctx_f")
    (dxcc_b, dprec_b, dbac_b, dbxc_b, dlamc_b, _), _ = _lru_bwd(
        xcc, wa_b, wx_b, ba_b, bx_b, lam_b, hcbk, zero_h, dhc_b, True, "lru_bwd_ctx_b")
    dxrc, dcw_c, dcb_c = _conv_bwd(dxcc_f, dxcc_b, xrc, cw, "conv_bwd_ctx")
    zr = jnp.zeros((l_len, RET_W), BF16)
    _, dpbc, dn1g_c, dcsh1, dcsc1 = _inproj_bwd(
        ctxt, n1g, csh1, csc1, w4, cos_c, sin_c, [zr, zr, dkc, zr, dvc, zr, zr, dxrc, zr],
        jnp.zeros((l_len, D_MODEL), F32), "inproj_bwd_ctx")

    gw_i = _tn(hb16, dpb, N_CHIP, False, True, "grad_w_in", extra=(hcb16, dpbc))
    b_i = gw_i.reshape(N_DEV, D_MODEL // 2, IN_COLS // N_CHIP)
    gwa_f, ((r_i,),) = _tn(xcl, dpre_f, 2, False, True, "grad_lru_gates_f", extra=(xcc, dprec_f),
                           comms=(_pair_exchange([b_i]),))
    p_i, pb_i = _pair_add(b_i, r_i, c_idx, "rs_pair_add_w_in")
    gwa_b, ((q_i,),) = _tn(xcl, dpre_b, 2, False, True, "grad_lru_gates_b", extra=(xcc, dprec_b),
                           comms=(_chip_exchange([pb_i]),))
    (f_i,) = _run_comm(_pair_gather([_chip_add(p_i, q_i, jc_idx, "rs_chip_add_w_in")]), "rs_pair_gather_w_in")
    g_in, g_out, g_1, g_2 = _shard_of(f_i), _shard_of(f_o), _shard_of(f_1), _shard_of(f_2)
    big = {}
    for nm, w, g, m, v in (("w_in", w_in, g_in, m_w_in, v_w_in), ("w_out", w_out, g_out, m_w_out, v_w_out),
                           ("w_mlp1", w_mlp1, g_1, m_w_mlp1, v_w_mlp1), ("w_mlp2", w_mlp2, g_2, m_w_mlp2, v_w_mlp2)):
        go, d_, mn, vn = _adamw(w[0], g, m[0], v[0], "adamw_" + nm)
        big[nm] = (go[None], d_[None], mn[None], vn[None])

    slab, ga, gx = _pack_small(
        [lossv, dsh1, dsc1, dg1, dsh2, dsc2, dg2, dcsh1, dcsc1, dn1g, dn1g_c, dn2g, dfg],
        (drd_f, drd_b, drd_c), (dcw, dcw_c), (dcb, dcb_c),
        (dba_f, dbac_f, dba_b, dbac_b, dbx_f, dbxc_f, dbx_b, dbxc_b, dlam_f, dlamc_f, dlam_b, dlamc_b),
        (gwa_f, gwa_b))
    slab_all, ga_all, gx_all = _all_gather([slab, ga, gx], "gather_small_grads")
    params = {
        "b_ada": (b_ada, m_b_ada, v_b_ada), "norm1_g": (norm1_g, m_norm1_g, v_norm1_g),
        "norm2_g": (norm2_g, m_norm2_g, v_norm2_g), "final_g": (final_g, m_final_g, v_final_g),
        "ret_decay": (ret_decay, m_ret_decay, v_ret_decay), "conv_w": (conv_w, m_conv_w, v_conv_w),
        "conv_b": (conv_b, m_conv_b, v_conv_b), "lru_wa": (lru_wa, m_lru_wa, v_lru_wa),
        "lru_ba": (lru_ba, m_lru_ba, v_lru_ba), "lru_wx": (lru_wx, m_lru_wx, v_lru_wx),
        "lru_bx": (lru_bx, m_lru_bx, v_lru_bx), "lru_lambda": (lru_lambda, m_lru_lambda, v_lru_lambda),
    }
    as2d = {
        "b_ada": lambda a: a, "norm1_g": lambda a: a, "norm2_g": lambda a: a, "conv_b": lambda a: a,
        "final_g": lambda a: a.reshape(1, D_MODEL), "ret_decay": lambda a: _lane_rep(a.reshape(-1)),
        "conv_w": lambda a: a[0], "lru_ba": lambda a: a[0], "lru_bx": lambda a: a[0], "lru_lambda": lambda a: a[0],
        "lru_wa": lambda a: a.reshape(2 * LRU_W, LRU_BD), "lru_wx": lambda a: a.reshape(2 * LRU_W, LRU_BD),
    }
    res, b128, dmc8, loss8 = _finalize_small(
        j_idx, slab_all, ga_all, gx_all, {nm: tuple(as2d[nm](a) for a in params[nm]) for nm in SMALL_PARAMS})
    loss = loss8[0, 0]
    small_out = {}
    for nm in SMALL_PARAMS:
        shp = params[nm][0].shape
        if nm == "ret_decay":
            small_out[nm] = tuple(o[:, 0].reshape(shp) for o in res[nm])
        else:
            small_out[nm] = tuple(o.reshape(shp) for o in res[nm])

    g_ada = _ada_grad(jnp.pad(a16.T, ((0, 0), (0, LANES - 16))), b128)
    g_ada, d_ada, m_ada, v_ada = _adamw(w_ada[0], g_ada, m_w_ada[0], v_w_ada[0], "adamw_w_ada")

    (cparts,) = _all_gather([_cctx_partial(dmc8, w_ada[0])], "gather_cctx")
    g_cc, d_cc, m_cc, v_cc = _cctx_final(cparts, c_ctx, m_c_ctx, v_c_ctx)
    small_out["c_ctx"] = tuple(a.reshape(D_MODEL) for a in (g_cc, d_cc, m_cc, v_cc))
    small_out["w_ada"] = (g_ada[None], d_ada[None], m_ada[None], v_ada[None])
    small_out.update(big)

    order = ["c_ctx", "w_ada", "b_ada", "norm1_g", "norm2_g", "w_in", "ret_decay", "conv_w", "conv_b", "lru_wa", "lru_ba",
             "lru_wx", "lru_bx", "lru_lambda", "w_out", "w_mlp1", "w_mlp2", "final_g"]
    outs = [loss, grad_x[None]]
    for k in range(4):
        outs += [small_out[nm][k] for nm in order]
    return tuple(outs)
```

```python
import math

import jax
import jax.numpy as jnp
from jax import lax
from jax.experimental import pallas as pl
from jax.experimental.pallas import tpu as pltpu

F32 = jnp.float32
BF16 = jnp.bfloat16

D_MODEL = 1024
HEADS = 4
DH = 128
CHUNK = 256
RET_W = HEADS * DH
LRU_W = 512
LRU_BLOCKS = 8
LRU_BD = LRU_W // LRU_BLOCKS
LRU_C = 8.0
IN_COLS = 4 * RET_W + 2 * LRU_W
MLP_H = 4 * D_MODEL
N_MOD = 6
GRID_W = 64
ROPE_BASE = 10000.0
K_SCALE = DH ** -0.5
EPS = 1e-6
GELU_K = math.sqrt(2.0 / math.pi)
GELU_C = 0.044715

ADAM_LR = 0.001
ADAM_B1 = 0.9
ADAM_B2 = 0.999
ADAM_EPS = 1e-08
ADAM_WD = 0.01
ADAM_STEP = 10

N_DEV = 8
N_CHIP = 4
SUBLANES = 8
LANES = 128
VMEM_LIMIT_V7X = 56 * 1024 * 1024
MESH = pl.DeviceIdType.MESH
ANY = pl.BlockSpec(memory_space=pl.ANY)


def _pc(body, **kw):
    return pl.pallas_call(body, **kw)


def _params(*sem):
    return pltpu.CompilerParams(dimension_semantics=sem if sem else None, vmem_limit_bytes=VMEM_LIMIT_V7X)


def _tile(t, big=False):
    if big and t >= 1024:
        return 512
    return 256 if t >= 256 else t


def _sds(shape, dtype=F32):
    return jax.ShapeDtypeStruct(tuple(shape), dtype)


def _full(shape):
    nd = len(shape)
    return pl.BlockSpec(tuple(shape), lambda *_: (0,) * nd)


def _sigmoid(x):
    return 1.0 / (1.0 + jnp.exp(-x))


def _log1p_pos(y):
    s = y * (1.0 - y * (0.5 - y * (1.0 / 3.0 - y * (0.25 - y * (0.2 - y / 6.0)))))
    return jnp.where(y < 0.03, s, jnp.log(1.0 + y))


def _softplus(z):
    return jnp.maximum(z, 0.0) + _log1p_pos(jnp.exp(-jnp.abs(z)))


def _one_minus_sq(la, a):
    t = la * (1.0 + la * (0.5 + la * (1.0 / 6.0 + la * (1.0 / 24.0 + la * (1.0 / 120.0)))))
    return jnp.where(la > -0.125, -t, 1.0 - a) * (1.0 + a)


def _rms(x):
    r = lax.rsqrt(jnp.mean(x * x, axis=-1, keepdims=True) + EPS)
    return x * r, r


def _dot(a, b):
    return jnp.dot(a, b, preferred_element_type=F32)


def _dot_nt(a, b):
    return lax.dot_general(a, b, (((1,), (1,)), ((), ())), preferred_element_type=F32)


def _dot_tn(a, b):
    return lax.dot_general(a, b, (((0,), (0,)), ((), ())), preferred_element_type=F32)


def _sum0(x):
    return jnp.sum(x, axis=0, keepdims=True)


def _norm_mod_bwd(x, g, sc, dh):
    xh, r = _rms(x)
    hn = xh * g
    dhn = dh * (1.0 + sc)
    dxh = dhn * g
    dx = r * (dxh - xh * jnp.mean(dxh * xh, axis=-1, keepdims=True))
    return dx, _sum0(dhn * xh), _sum0(dh), _sum0(dh * hn)


def _dev_index(p):
    return 4 * p[0] + 2 * p[1] + p[2]


def _mesh_pos():
    return lax.axis_index("x"), lax.axis_index("y"), lax.axis_index("c")


class _AllGather:
    def __init__(self, arrs):
        n = len(arrs)
        self.arrays = list(arrs)
        self.out_shapes = [_sds((N_DEV,) + a.shape, a.dtype) for a in arrs]
        self.scratch = ([pltpu.VMEM(a.shape, a.dtype) for a in arrs]
                        + [pltpu.SemaphoreType.DMA((7 * n,)), pltpu.SemaphoreType.DMA((7 * n,)),
                           pltpu.SemaphoreType.DMA((n,))])
        self.aliases = {}

    def _parts(self, ins, outs, scr):
        n = len(self.arrays)
        stage = scr[:n]
        send_sems, recv_sems, local_sems = scr[n:]
        x, y, c = _mesh_pos()
        me, sib = (x, y, c), (x, y, 1 - c)
        chips = [(1 - x, y), (x, 1 - y), (1 - x, 1 - y)]

        def copy(t, k, block, to, own=False):
            dst = outs[t].at[_dev_index(block)]
            return pltpu.make_async_remote_copy(
                src_ref=ins[t] if own else dst, dst_ref=dst,
                send_sem=send_sems.at[7 * t + k], recv_sem=recv_sems.at[7 * t + k],
                device_id=to, device_id_type=MESH)

        first = []
        for t in range(n):
            first.append(copy(t, 0, me, sib, own=True))
            for j, ch in enumerate(chips):
                first.append(copy(t, 1 + j, me, (*ch, c), own=True))
        stage_in = [pltpu.make_async_copy(ins[t], stage[t], local_sems.at[t]) for t in range(n)]
        mine = [pltpu.make_async_copy(stage[t], outs[t].at[_dev_index(me)], local_sems.at[t]) for t in range(n)]
        return n, c, me, sib, chips, copy, first, stage_in, mine

    def start(self, ins, outs, scr):
        n, _, _, _, _, _, first, stage_in, mine = self._parts(ins, outs, scr)
        for cp in stage_in:
            cp.start()
        for cp in first:
            cp.start()
        for t in range(n):
            stage_in[t].wait()
            mine[t].start()

    def finish(self, ins, outs, scr):
        n, c, me, sib, chips, copy, first, _, mine = self._parts(ins, outs, scr)
        passed = []
        for j, ch in enumerate(chips):
            for t in range(n):
                copy(t, 1 + j, (*ch, c), me).wait_recv()
                p = copy(t, 4 + j, (*ch, c), sib)
                p.start()
                passed.append(p)
        for t in range(n):
            copy(t, 0, sib, me).wait_recv()
            for j, ch in enumerate(chips):
                copy(t, 4 + j, (*ch, 1 - c), me).wait_recv()
        for cp in first + passed:
            cp.wait_send()
        for cp in mine:
            cp.wait()


class _Exchange:
    def __init__(self, arrays, out_shapes, plan, n_copies, aliases=None):
        self.arrays = list(arrays)
        self.out_shapes = list(out_shapes)
        self.plan = plan
        self.scratch = [pltpu.SemaphoreType.DMA((n_copies,)), pltpu.SemaphoreType.DMA((n_copies,))]
        self.aliases = aliases or {}

    def _copies(self, ins, outs, scr):
        send_sems, recv_sems = scr
        snd, rcv = [], []
        for i, (src, dst, peer, lands) in enumerate(self.plan(ins, outs, _mesh_pos())):
            kw = dict(send_sem=send_sems.at[i], recv_sem=recv_sems.at[i], device_id=peer, device_id_type=MESH)
            snd.append(pltpu.make_async_remote_copy(src_ref=src, dst_ref=dst, **kw))
            rcv.append(pltpu.make_async_remote_copy(src_ref=src, dst_ref=lands, **kw))
        return snd, rcv

    def start(self, ins, outs, scr):
        for cp in self._copies(ins, outs, scr)[0]:
            cp.start()

    def finish(self, ins, outs, scr):
        snd, rcv = self._copies(ins, outs, scr)
        for cp in rcv:
            cp.wait_recv()
        for cp in snd:
            cp.wait_send()


def _pair_exchange(grads):
    n = len(grads)

    def plan(ins, outs, pos):
        x, y, c = pos
        return [(ins[t].at[2 * j + (1 - c)], outs[t].at[j], (x, y, 1 - c), outs[t].at[j])
                for t in range(n) for j in range(N_CHIP)]

    return _Exchange(grads, [_sds((N_CHIP,) + g.shape[1:], g.dtype) for g in grads], plan, N_CHIP * n)


def _chip_exchange(parts):
    n = len(parts)

    def plan(ins, outs, pos):
        x, y, c = pos
        chips = [(1 - x, y), (x, 1 - y), (1 - x, 1 - y)]
        return [(ins[t].at[2 * ch[0] + ch[1]], outs[t].at[k], (*ch, c), outs[t].at[k])
                for t in range(n) for k, ch in enumerate(chips)]

    return _Exchange(parts, [_sds((3,) + p.shape[1:], p.dtype) for p in parts], plan, 3 * n)


def _pair_gather(bufs):
    n = len(bufs)

    def plan(ins, outs, pos):
        x, y, c = pos
        return [(ins[t].at[c], outs[t].at[c], (x, y, 1 - c), outs[t].at[1 - c]) for t in range(n)]

    return _Exchange(bufs, [_sds(b.shape, b.dtype) for b in bufs], plan, n, aliases={t: t for t in range(n)})


def _run_comm(comm, name):
    n_in, n_out = len(comm.arrays), len(comm.out_shapes)

    def body(*refs):
        ins, outs, scr = refs[:n_in], refs[n_in:n_in + n_out], refs[n_in + n_out:]
        comm.start(ins, outs, scr)
        comm.finish(ins, outs, scr)

    outs = _pc(body, name=name, out_shape=comm.out_shapes, in_specs=[ANY] * n_in, out_specs=[ANY] * n_out,
               input_output_aliases=dict(comm.aliases), scratch_shapes=comm.scratch,
               compiler_params=_params())(*comm.arrays)
    return list(outs)


def _all_gather(arrs, name):
    return _run_comm(_AllGather(arrs), name)


def _call(body, *, name, grid, in_specs, out_specs, out_shape, scratch_shapes, sem, args, comms=()):
    n_in, n_out, n_scr = len(in_specs), len(out_specs), len(scratch_shapes)
    c_in = [len(cm.arrays) for cm in comms]
    c_out = [len(cm.out_shapes) for cm in comms]
    c_scr = [len(cm.scratch) for cm in comms]
    aliases = {}
    for k, cm in enumerate(comms):
        for a, b in cm.aliases.items():
            aliases[n_in + sum(c_in[:k]) + a] = n_out + sum(c_out[:k]) + b

    def split(refs, counts):
        out, pos = [], 0
        for cnt in counts:
            out.append(refs[pos:pos + cnt])
            pos += cnt
        return out

    def wrapped(*refs):
        ins = refs[:n_in + sum(c_in)]
        outs = refs[len(ins):len(ins) + n_out + sum(c_out)]
        scr = refs[len(ins) + len(outs):]
        cins, couts, cscr = split(ins[n_in:], c_in), split(outs[n_out:], c_out), split(scr[n_scr:], c_scr)
        if comms:
            first = pl.program_id(0) == 0
            last = pl.program_id(0) == grid[0] - 1
            for k in range(1, len(grid)):
                first = jnp.logical_and(first, pl.program_id(k) == 0)
                last = jnp.logical_and(last, pl.program_id(k) == grid[k] - 1)

            @pl.when(first)
            def _():
                for k, cm in enumerate(comms):
                    cm.start(cins[k], couts[k], cscr[k])
        body(*ins[:n_in], *outs[:n_out], *scr[:n_scr])
        if comms:
            @pl.when(last)
            def _():
                for k, cm in enumerate(comms):
                    cm.finish(cins[k], couts[k], cscr[k])

    outs = _pc(wrapped, name=name, grid=grid,
               in_specs=list(in_specs) + [ANY] * sum(c_in), out_specs=list(out_specs) + [ANY] * sum(c_out),
               out_shape=list(out_shape) + [s for cm in comms for s in cm.out_shapes],
               scratch_shapes=list(scratch_shapes) + [s for cm in comms for s in cm.scratch],
               input_output_aliases=aliases, compiler_params=_params(*sem),
               )(*args, *[a for cm in comms for a in cm.arrays])
    outs = list(outs)
    return outs[:n_out], split(outs[n_out:], c_out)


def _row_block(r):
    for b in (512, 256, 128, 64, 32, 16, 8):
        if r % b == 0:
            return b
    return r


def _pair_add(g, recv, c_idx, name):
    _, r, cc = g.shape
    br = _row_block(r)

    def body(c_ref, g_ref, r_ref, p_ref, pb_ref):
        s = g_ref[...] + r_ref[...]
        p_ref[...] = s
        pb_ref[...] = s.astype(BF16)

    grid_spec = pltpu.PrefetchScalarGridSpec(
        num_scalar_prefetch=1, grid=(N_CHIP, r // br),
        in_specs=[pl.BlockSpec((1, br, cc), lambda j, i, c_ref: (2 * j + c_ref[0], i, 0)),
                  pl.BlockSpec((1, br, cc), lambda j, i, c_ref: (j, i, 0))],
        out_specs=[pl.BlockSpec((1, br, cc), lambda j, i, c_ref: (j, i, 0)),
                   pl.BlockSpec((1, br, cc), lambda j, i, c_ref: (j, i, 0))])
    return _pc(body, name=name, grid_spec=grid_spec,
               out_shape=[_sds((N_CHIP, r, cc)), _sds((N_CHIP, r, cc), BF16)],
               compiler_params=_params("arbitrary", "arbitrary"))(c_idx, g, recv)


def _chip_add(p, q, jc_idx, name):
    _, r, cc = p.shape
    br = _row_block(r)

    def body(jc_ref, p_ref, q_ref, o_ref):
        o_ref[0] = ((p_ref[0] + q_ref[0].astype(F32)) + q_ref[1].astype(F32)) + q_ref[2].astype(F32)

    grid_spec = pltpu.PrefetchScalarGridSpec(
        num_scalar_prefetch=1, grid=(r // br,),
        in_specs=[pl.BlockSpec((1, br, cc), lambda i, jc_ref: (jc_ref[0], i, 0)),
                  pl.BlockSpec((3, br, cc), lambda i, jc_ref: (0, i, 0))],
        out_specs=pl.BlockSpec((1, br, cc), lambda i, jc_ref: (jc_ref[1], i, 0)))
    return _pc(body, name=name, grid_spec=grid_spec, out_shape=_sds((2, r, cc)),
               compiler_params=_params("arbitrary"))(jc_idx, p, q)


def _shard_of(both):
    return both.reshape((2 * both.shape[1],) + both.shape[2:])


def _adamw(w, g, m, v, name):
    r, cc = w.shape
    br = _row_block(r)
    if r * cc * 4 <= (1 << 20):
        br = r
    elif br * cc * 4 > (1 << 20) and br > 8:
        br = max(8, (1 << 20) // (cc * 4) // 8 * 8)
        while r % br:
            br -= 8
    c1 = 1.0 - ADAM_B1 ** ADAM_STEP
    c2 = 1.0 - ADAM_B2 ** ADAM_STEP

    def body(w_ref, g_ref, m_ref, v_ref, go_ref, d_ref, mo_ref, vo_ref):
        gg = g_ref[...]
        go_ref[...] = gg
        mn = ADAM_B1 * m_ref[...] + (1.0 - ADAM_B1) * gg
        vn = ADAM_B2 * v_ref[...] + (1.0 - ADAM_B2) * (gg * gg)
        mh = mn / c1
        vh = vn / c2
        d_ref[...] = -ADAM_LR * (mh / (jnp.sqrt(vh) + ADAM_EPS) + ADAM_WD * w_ref[...])
        mo_ref[...] = mn
        vo_ref[...] = vn

    spec = pl.BlockSpec((br, cc), lambda i: (i, 0))
    return _pc(body, name=name, grid=(r // br,), in_specs=[spec] * 4, out_specs=[spec] * 4,
               out_shape=[_sds((r, cc))] * 4, compiler_params=_params("arbitrary"))(w, g, m, v)


def _head(w_half, c8, small, w_ada, b_shard, c_ctx, ret_decay):
    ada_n = w_ada.shape[1]
    mod_sds = _sds((16, ada_n))
    ag_w, ag_c, ag_m = _AllGather([w_half]), _AllGather([c8, small]), _AllGather([mod_sds])
    n_w, n_c, n_m = len(ag_w.scratch), len(ag_c.scratch), len(ag_m.scratch)

    def body(w_ref, c_ref, s_ref, wada_ref, b_ref, cc_ref, rd_ref,
             gw_ref, call_ref, sall_ref, a_ref, modp_ref, mall_ref, lg_ref, sg_ref, *scr):
        scr_w, scr_c, scr_m = scr[:n_w], scr[n_w:n_w + n_c], scr[n_w + n_c:n_w + n_c + n_m]
        c_v, w_v, m_v, sems = scr[n_w + n_c + n_m:]
        ag_w.start((w_ref,), (gw_ref,), scr_w)
        ag_c.start((c_ref, s_ref), (call_ref, sall_ref), scr_c)
        load_w = pltpu.make_async_copy(wada_ref, w_v, sems.at[0])
        load_w.start()
        rd = rd_ref[...]
        lg_ref[...] = -_softplus(-rd)
        sg_ref[...] = _sigmoid(-rd)
        ag_c.finish((c_ref, s_ref), (call_ref, sall_ref), scr_c)
        load_c = pltpu.make_async_copy(call_ref, c_v, sems.at[1])
        load_c.start()
        load_c.wait()
        a_ref[...] = jnp.zeros_like(a_ref)
        for d in range(N_DEV):
            cd = c_v[d, 0:1, :]
            a_ref[d:d + 1, :] = cd * _sigmoid(cd)
        cc = cc_ref[...]
        a_ref[N_DEV:N_DEV + 1, :] = cc * _sigmoid(cc)
        load_w.wait()
        m_v[...] = jnp.dot(a_ref[...], w_v[...], preferred_element_type=F32,
                           precision=lax.Precision.HIGHEST) + b_ref[...]
        put = pltpu.make_async_copy(m_v, modp_ref, sems.at[2])
        put.start()
        put.wait()
        ag_m.start((modp_ref,), (mall_ref,), scr_m)
        ag_m.finish((modp_ref,), (mall_ref,), scr_m)
        ag_w.finish((w_ref,), (gw_ref,), scr_w)

    rd = jnp.broadcast_to(ret_decay.reshape(2, HEADS).T[:, :, None], (HEADS, 2, LANES))
    lane = _full((HEADS, 2, LANES))
    outs = _pc(
        body, name="head",
        in_specs=[ANY, ANY, ANY, ANY, _full((1, ada_n)), _full((1, D_MODEL)), lane],
        out_specs=[ANY, ANY, ANY, _full((16, D_MODEL)), ANY, ANY, lane, lane],
        out_shape=ag_w.out_shapes + ag_c.out_shapes + [_sds((16, D_MODEL)), mod_sds] + ag_m.out_shapes
        + [_sds((HEADS, 2, LANES))] * 2,
        scratch_shapes=ag_w.scratch + ag_c.scratch + ag_m.scratch
        + [pltpu.VMEM((N_DEV,) + c8.shape, F32), pltpu.VMEM(w_ada.shape, F32), pltpu.VMEM((16, ada_n), F32),
           pltpu.SemaphoreType.DMA((3,))],
        compiler_params=_params(),
    )(w_half, c8, small, w_ada, b_shard, c_ctx.reshape(1, D_MODEL), rd)
    gw, c_all, small_all, a16, _, mod_all, lgv, sgv = outs
    return gw, c_all, small_all, a16, mod_all, lgv, sgv


def _ada_grad(at, b):
    n = b.shape[1]
    bn = 512

    def body(a_ref, b_ref, o_ref):
        o_ref[...] = jnp.dot(a_ref[...], b_ref[...], preferred_element_type=F32, precision=lax.Precision.HIGHEST)

    return _pc(body, name="ada_grad", grid=(n // bn,),
               in_specs=[_full((D_MODEL, LANES)), pl.BlockSpec((LANES, bn), lambda i: (0, i))],
               out_specs=pl.BlockSpec((D_MODEL, bn), lambda i: (0, i)), out_shape=_sds((D_MODEL, n)),
               compiler_params=_params("arbitrary"))(at, b)


def _cctx_partial(dmc8, w_ada):
    n = w_ada.shape[1]
    bn = 512

    def body(d_ref, w_ref, o_ref):
        @pl.when(pl.program_id(0) == 0)
        def _():
            o_ref[...] = jnp.zeros_like(o_ref)
        o_ref[...] += lax.dot_general(d_ref[...], w_ref[...], (((1,), (1,)), ((), ())),
                                      preferred_element_type=F32, precision=lax.Precision.HIGHEST)

    return _pc(body, name="cctx_partial", grid=(n // bn,),
               in_specs=[pl.BlockSpec((8, bn), lambda i: (0, i)), pl.BlockSpec((D_MODEL, bn), lambda i: (0, i))],
               out_specs=_full((8, D_MODEL)), out_shape=_sds((8, D_MODEL)),
               compiler_params=_params("arbitrary"))(dmc8, w_ada)


def _cctx_final(parts, c_ctx, m, v):
    c1 = 1.0 - ADAM_B1 ** ADAM_STEP
    c2 = 1.0 - ADAM_B2 ** ADAM_STEP

    def body(p_ref, c_ref, m_ref, v_ref, g_ref, d_ref, mo_ref, vo_ref):
        s = ((p_ref[0, 0:1, :] + p_ref[2, 0:1, :]) + p_ref[4, 0:1, :]) + p_ref[6, 0:1, :]
        z = c_ref[...]
        sg = _sigmoid(z)
        gg = s * (sg * (1.0 + z * (1.0 - sg)))
        g_ref[...] = gg
        mn = ADAM_B1 * m_ref[...] + (1.0 - ADAM_B1) * gg
        vn = ADAM_B2 * v_ref[...] + (1.0 - ADAM_B2) * (gg * gg)
        d_ref[...] = -ADAM_LR * ((mn / c1) / (jnp.sqrt(vn / c2) + ADAM_EPS) + ADAM_WD * z)
        mo_ref[...] = mn
        vo_ref[...] = vn

    row = _full((1, D_MODEL))
    return _pc(body, name="cctx_final", out_shape=[_sds((1, D_MODEL))] * 4,
               in_specs=[_full(parts.shape), row, row, row], out_specs=[row] * 4,
               compiler_params=_params())(parts, c_ctx.reshape(1, D_MODEL), m.reshape(1, D_MODEL), v.reshape(1, D_MODEL))


def _rotary_tables(t_len):
    rows = t_len // GRID_W
    row = jnp.repeat(jnp.arange(rows, dtype=F32), GRID_W)
    col = jnp.tile(jnp.arange(GRID_W, dtype=F32), rows)
    n_freq = DH // 4
    inv = ROPE_BASE ** (-jnp.arange(n_freq, dtype=F32) / n_freq)
    ang = jnp.concatenate([row[:, None] * inv, col[:, None] * inv], axis=-1)
    cos, sin = jnp.cos(ang), jnp.sin(ang)
    return jnp.concatenate([cos, cos], axis=-1), jnp.concatenate([-sin, sin], axis=-1)


def _inproj_fwd(x, gn, sh, sc, w4, cos2, sin2, name, comms=()):
    t = x.shape[0]
    tm = _tile(t, True)
    nc = IN_COLS // N_CHIP

    def body(x_ref, gn_ref, sh_ref, sc_ref, w_ref, c_ref, s_ref, p_ref, xr_ref, hb_ref, p_s):
        xh, _ = _rms(x_ref[...])
        h = xh * gn_ref[...] * (1.0 + sc_ref[...]) + sh_ref[...]
        hb = h.astype(BF16)
        hb_ref[...] = hb
        for j in range(N_CHIP):
            p_s[:, nc * j:nc * (j + 1)] = _dot(hb, w_ref[j])
        cc = c_ref[...]
        ss = s_ref[...]
        for hh in range(2 * HEADS):
            blk = p_s[:, DH * hh:DH * (hh + 1)]
            rot = blk * cc + pltpu.roll(blk, DH // 2, 1) * ss
            if hh >= HEADS:
                rot = rot * K_SCALE
            p_ref[:, DH * hh:DH * (hh + 1)] = rot.astype(BF16)
        p_ref[:, 2 * RET_W:] = p_s[:, 2 * RET_W:].astype(BF16)
        xr_ref[...] = p_s[:, 4 * RET_W:4 * RET_W + LRU_W]

    row = _full((1, D_MODEL))
    outs, couts = _call(
        body, name=name, grid=(t // tm,),
        in_specs=[pl.BlockSpec((tm, D_MODEL), lambda i: (i, 0)), row, row, row, _full(w4.shape),
                  pl.BlockSpec((tm, DH), lambda i: (i, 0)), pl.BlockSpec((tm, DH), lambda i: (i, 0))],
        out_specs=[pl.BlockSpec((tm, IN_COLS), lambda i: (i, 0)), pl.BlockSpec((tm, LRU_W), lambda i: (i, 0)),
                   pl.BlockSpec((tm, D_MODEL), lambda i: (i, 0))],
        out_shape=[_sds((t, IN_COLS), BF16), _sds((t, LRU_W)), _sds((t, D_MODEL), BF16)],
        scratch_shapes=[pltpu.VMEM((tm, IN_COLS), F32)], sem=("arbitrary",),
        args=(x, gn, sh, sc, w4, cos2, sin2), comms=comms)
    return (outs, couts) if comms else outs


def _inproj_bwd(x, gn, sh, sc, w4, cos2, sin2, pieces, dres, name):
    t = x.shape[0]
    tm = _tile(t)
    nc = IN_COLS // N_CHIP

    def body(x_ref, gn_ref, sh_ref, sc_ref, w_ref, c_ref, s_ref, dqf, dqb, dkf, dkb, dvf, dvb, dg, dxr, dgt, dres_ref,
             dx_ref, dpb_ref, dgn_ref, dsh_ref, dsc_ref):
        cc = c_ref[...]
        ss = s_ref[...]
        dq = dqf[...].astype(F32) + dqb[...].astype(F32)
        dk = dkf[...].astype(F32) + dkb[...].astype(F32)
        for hh in range(HEADS):
            sl = slice(DH * hh, DH * (hh + 1))
            b = dq[:, sl]
            dpb_ref[:, sl] = (b * cc + pltpu.roll(b * ss, DH // 2, 1)).astype(BF16)
            b = dk[:, sl]
            dpb_ref[:, RET_W + DH * hh:RET_W + DH * (hh + 1)] = (
                (b * cc + pltpu.roll(b * ss, DH // 2, 1)) * K_SCALE).astype(BF16)
        dpb_ref[:, 2 * RET_W:3 * RET_W] = (dvf[...].astype(F32) + dvb[...].astype(F32)).astype(BF16)
        dpb_ref[:, 3 * RET_W:4 * RET_W] = dg[...].astype(BF16)
        dpb_ref[:, 4 * RET_W:4 * RET_W + LRU_W] = dxr[...].astype(BF16)
        dpb_ref[:, 4 * RET_W + LRU_W:IN_COLS] = dgt[...].astype(BF16)
        dh = _dot_nt(dpb_ref[:, 0:nc], w_ref[0])
        for j in range(1, N_CHIP):
            dh = dh + _dot_nt(dpb_ref[:, nc * j:nc * (j + 1)], w_ref[j])
        dx, dgn_t, dsh_t, dsc_t = _norm_mod_bwd(x_ref[...], gn_ref[...], sc_ref[...], dh)
        dx_ref[...] = dres_ref[...] + dx

        @pl.when(pl.program_id(0) == 0)
        def _():
            dgn_ref[...] = jnp.zeros_like(dgn_ref)
            dsh_ref[...] = jnp.zeros_like(dsh_ref)
            dsc_ref[...] = jnp.zeros_like(dsc_ref)
        dgn_ref[...] += dgn_t
        dsh_ref[...] += dsh_t
        dsc_ref[...] += dsc_t

    row = _full((1, D_MODEL))
    pc = pl.BlockSpec((tm, RET_W), lambda i: (i, 0))
    big = pl.BlockSpec((tm, D_MODEL), lambda i: (i, 0))
    return _pc(body, name=name, grid=(t // tm,),
               in_specs=[big, row, row, row, _full(w4.shape),
                         pl.BlockSpec((tm, DH), lambda i: (i, 0)), pl.BlockSpec((tm, DH), lambda i: (i, 0))]
               + [pc] * 9 + [big],
               out_specs=[big, pl.BlockSpec((tm, IN_COLS), lambda i: (i, 0)), row, row, row],
               out_shape=[_sds((t, D_MODEL)), _sds((t, IN_COLS), BF16), _sds((1, D_MODEL)), _sds((1, D_MODEL)),
                          _sds((1, D_MODEL))],
               compiler_params=_params("arbitrary"))(x, gn, sh, sc, w4, cos2, sin2, *pieces, dres)


def _halo_specs(t, tm):
    n8 = tm // SUBLANES
    last8 = t // SUBLANES - 1
    prev = pl.BlockSpec((SUBLANES, LRU_W), lambda i: (jnp.maximum(i * n8 - 1, 0), 0))
    main = pl.BlockSpec((tm, LRU_W), lambda i: (i, 0))
    nxt = pl.BlockSpec((SUBLANES, LRU_W), lambda i: (jnp.minimum((i + 1) * n8, last8), 0))
    return prev, main, nxt


def _with_halo(prev_ref, main_ref, next_ref, i, nt):
    prev = jnp.where(i > 0, prev_ref[...], 0.0)
    nxt = jnp.where(i < nt - 1, next_ref[...], 0.0)
    return jnp.concatenate([prev, main_ref[...], nxt], axis=0)


def _conv_fwd(xr, cw, cb, name):
    t = xr.shape[0]
    tm = _tile(t, True)
    nt = t // tm
    n = tm + 2 * SUBLANES
    mid = slice(SUBLANES, SUBLANES + tm)

    def body(p_ref, m_ref, n_ref, w_ref, b_ref, o_ref):
        xp = _with_halo(p_ref, m_ref, n_ref, pl.program_id(0), nt)
        acc = b_ref[...] + pltpu.roll(xp, 1, 0)[mid] * w_ref[0:1, :]
        acc = acc + xp[mid] * w_ref[1:2, :]
        acc = acc + pltpu.roll(xp, n - 1, 0)[mid] * w_ref[2:3, :]
        acc = acc + pltpu.roll(xp, n - 2, 0)[mid] * w_ref[3:4, :]
        o_ref[...] = acc

    return _pc(body, name=name, grid=(nt,),
               in_specs=[*_halo_specs(t, tm), _full((4, LRU_W)), _full((1, LRU_W))],
               out_specs=pl.BlockSpec((tm, LRU_W), lambda i: (i, 0)), out_shape=_sds((t, LRU_W)),
               compiler_params=_params("arbitrary"))(xr, xr, xr, cw, cb)


def _conv_bwd(dxc_a, dxc_b, xr, cw, name):
    t = xr.shape[0]
    tm = _tile(t, True)
    nt = t // tm
    n = tm + 2 * SUBLANES
    mid = slice(SUBLANES, SUBLANES + tm)

    def body(ap_ref, am_ref, an_ref, bp_ref, bm_ref, bn_ref, xp_ref, xm_ref, xn_ref, w_ref, dx_ref, dw_ref, db_ref):
        i = pl.program_id(0)
        dp = _with_halo(ap_ref, am_ref, an_ref, i, nt) + _with_halo(bp_ref, bm_ref, bn_ref, i, nt)
        xp = _with_halo(xp_ref, xm_ref, xn_ref, i, nt)
        dx = pltpu.roll(dp, n - 1, 0)[mid] * w_ref[0:1, :]
        dx = dx + dp[mid] * w_ref[1:2, :]
        dx = dx + pltpu.roll(dp, 1, 0)[mid] * w_ref[2:3, :]
        dx = dx + pltpu.roll(dp, 2, 0)[mid] * w_ref[3:4, :]
        dx_ref[...] = dx.astype(BF16)
        d = dp[mid]

        @pl.when(i == 0)
        def _():
            dw_ref[...] = jnp.zeros_like(dw_ref)
            db_ref[...] = jnp.zeros_like(db_ref)
        dw_ref[0:1, :] += _sum0(d * pltpu.roll(xp, 1, 0)[mid])
        dw_ref[1:2, :] += _sum0(d * xp[mid])
        dw_ref[2:3, :] += _sum0(d * pltpu.roll(xp, n - 1, 0)[mid])
        dw_ref[3:4, :] += _sum0(d * pltpu.roll(xp, n - 2, 0)[mid])
        db_ref[...] += _sum0(d)

    return _pc(body, name=name, grid=(nt,),
               in_specs=[*_halo_specs(t, tm), *_halo_specs(t, tm), *_halo_specs(t, tm), _full((4, LRU_W))],
               out_specs=[pl.BlockSpec((tm, LRU_W), lambda i: (i, 0)), _full((4, LRU_W)), _full((1, LRU_W))],
               out_shape=[_sds((t, LRU_W), BF16), _sds((4, LRU_W)), _sds((1, LRU_W))],
               compiler_params=_params("arbitrary"))(dxc_a, dxc_a, dxc_a, dxc_b, dxc_b, dxc_b, xr, xr, xr, cw)


def _local_scan(a, b, reverse):
    n = a.shape[0]
    row = lax.broadcasted_iota(jnp.int32, a.shape, 0) & (SUBLANES - 1)
    for s in (1, 2, 4):
        if reverse:
            a_s, b_s, ok = pltpu.roll(a, n - s, 0), pltpu.roll(b, n - s, 0), row < SUBLANES - s
        else:
            a_s, b_s, ok = pltpu.roll(a, s, 0), pltpu.roll(b, s, 0), row >= s
        b = a * jnp.where(ok, b_s, 0.0) + b
        a = a * jnp.where(ok, a_s, 1.0)
    return a, b


def _carry_scan(a_s, b_s, out_ref, carry, reverse):
    ng = a_s.shape[0] // SUBLANES
    shape = carry.shape

    def step(g, cr):
        gg = (ng - 1 - g) if reverse else g
        off = pl.multiple_of(gg * SUBLANES, SUBLANES)
        h = a_s[pl.ds(off, SUBLANES), :] * cr + b_s[pl.ds(off, SUBLANES), :]
        out_ref[pl.ds(off, SUBLANES), :] = h
        edge = h[0:1, :] if reverse else h[SUBLANES - 1:SUBLANES, :]
        return jnp.broadcast_to(edge, shape)

    return lax.fori_loop(0, ng, step, carry)


def _lru_gates(xc, wa_ref, wx_ref, ba, bx, lam):
    xb = xc.astype(BF16)
    r = _sigmoid(_dot(xb, wa_ref[...]) + ba)
    ig = _sigmoid(_dot(xb, wx_ref[...]) + bx)
    sp = _softplus(-lam)
    la = -LRU_C * r * sp
    a = jnp.exp(la)
    mult = jnp.sqrt(_one_minus_sq(la, a))
    return r, ig, sp, a, mult


def _lru_fwd(xc, wa, wx, ba, bx, lam, h0, reverse, name, comms=()):
    t = xc.shape[0]
    tm = _tile(t, True)
    nt = t // tm
    tidx = (lambda i: (nt - 1 - i, 0)) if reverse else (lambda i: (i, 0))

    def body(x_ref, wa_ref, wx_ref, ba_ref, bx_ref, lam_ref, h0_ref, h_ref, a_s, b_s, c_s):
        @pl.when(pl.program_id(0) == 0)
        def _():
            c_s[...] = jnp.broadcast_to(h0_ref[...], c_s.shape)
        xv = x_ref[...]
        _, ig, _, a, mult = _lru_gates(xv, wa_ref, wx_ref, ba_ref[...], bx_ref[...], lam_ref[...])
        al, bl = _local_scan(a, mult * (ig * xv), reverse)
        a_s[...] = al
        b_s[...] = bl
        c_s[...] = _carry_scan(a_s, b_s, h_ref, c_s[...], reverse)

    vec = _full((1, LRU_W))
    mat = _full((LRU_W, LRU_W))
    (h,), couts = _call(body, name=name, grid=(nt,),
                        in_specs=[pl.BlockSpec((tm, LRU_W), tidx), mat, mat, vec, vec, vec, vec],
                        out_specs=[pl.BlockSpec((tm, LRU_W), tidx)], out_shape=[_sds((t, LRU_W))],
                        scratch_shapes=[pltpu.VMEM((tm, LRU_W), F32), pltpu.VMEM((tm, LRU_W), F32),
                                        pltpu.VMEM((SUBLANES, LRU_W), F32)],
                        sem=("arbitrary",), args=(xc, wa, wx, ba, bx, lam, h0), comms=comms)
    return (h, couts) if comms else h


def _lru_bwd(xc, wa, wx, ba, bx, lam, h, h0, dh, reverse, name, comms=()):
    t = xc.shape[0]
    tm = _tile(t, True)
    nt = t // tm
    n8 = tm // SUBLANES
    last8 = t // SUBLANES - 1
    tidx = (lambda i: (i, 0)) if reverse else (lambda i: (nt - 1 - i, 0))
    if reverse:
        halo = pl.BlockSpec((SUBLANES, LRU_W), lambda i: (jnp.minimum((i + 1) * n8, last8), 0))
    else:
        halo = pl.BlockSpec((SUBLANES, LRU_W), lambda i: (jnp.maximum((nt - 1 - i) * n8 - 1, 0), 0))

    def body(x_ref, wa_ref, wx_ref, ba_ref, bx_ref, lam_ref, h_ref, halo_ref, h0_ref, dh_ref,
             dx_ref, dpre_ref, dba_ref, dbx_ref, dlam_ref, dh0_ref, a_s, b_s, l_s, c_s, e_s):
        i = pl.program_id(0)

        @pl.when(i == 0)
        def _():
            c_s[...] = jnp.zeros_like(c_s)
            e_s[...] = jnp.zeros_like(e_s)
            dba_ref[...] = jnp.zeros_like(dba_ref)
            dbx_ref[...] = jnp.zeros_like(dbx_ref)
            dlam_ref[...] = jnp.zeros_like(dlam_ref)
        xv = x_ref[...]
        lam = lam_ref[...]
        r, ig, sp, a, mult = _lru_gates(xv, wa_ref, wx_ref, ba_ref[...], bx_ref[...], lam)
        hv = h_ref[...]
        rowi = lax.broadcasted_iota(jnp.int32, (tm, LRU_W), 0)
        edge_a = jnp.broadcast_to(e_s[0:1, :], (tm, LRU_W))
        h0b = jnp.broadcast_to(h0_ref[...], (tm, LRU_W))
        if reverse:
            a_sh = jnp.where(rowi == 0, edge_a, pltpu.roll(a, 1, 0))
            hin_edge = jnp.where(i == nt - 1, h0b, jnp.broadcast_to(halo_ref[0:1, :], (tm, LRU_W)))
            h_in = jnp.where(rowi == tm - 1, hin_edge, pltpu.roll(hv, tm - 1, 0))
        else:
            a_sh = jnp.where(rowi == tm - 1, edge_a, pltpu.roll(a, tm - 1, 0))
            hin_edge = jnp.where(i == nt - 1, h0b, jnp.broadcast_to(halo_ref[SUBLANES - 1:SUBLANES, :], (tm, LRU_W)))
            h_in = jnp.where(rowi == 0, hin_edge, pltpu.roll(hv, 1, 0))
        al, bl = _local_scan(a_sh, dh_ref[...], not reverse)
        a_s[...] = al
        b_s[...] = bl
        c_s[...] = _carry_scan(a_s, b_s, l_s, c_s[...], not reverse)
        e_s[...] = jnp.broadcast_to(a[tm - 1:tm, :] if reverse else a[0:1, :], e_s.shape)
        lmb = l_s[...]
        da = lmb * h_in
        ixc = ig * xv
        dmult = lmb * ixc
        dixc = lmb * mult
        dla = da * a - dmult * (a * a) / mult
        dpr = dla * (-LRU_C * sp) * r * (1.0 - r)
        dpi = dixc * xv * ig * (1.0 - ig)
        dprb = dpr.astype(BF16)
        dpib = dpi.astype(BF16)
        dpre_ref[:, 0:LRU_W] = dprb
        dpre_ref[:, LRU_W:2 * LRU_W] = dpib
        dx_ref[...] = dixc * ig + _dot_nt(dprb, wa_ref[...]) + _dot_nt(dpib, wx_ref[...])
        dba_ref[...] += _sum0(dpr)
        dbx_ref[...] += _sum0(dpi)
        dlam_ref[...] += _sum0(dla * (-LRU_C * r)) * (-_sigmoid(-lam))

        @pl.when(i == nt - 1)
        def _():
            al0 = a * lmb
            dh0_ref[...] = al0[tm - 1:tm, :] if reverse else al0[0:1, :]

    vec = _full((1, LRU_W))
    mat = _full((LRU_W, LRU_W))
    tile = pl.BlockSpec((tm, LRU_W), tidx)
    return _call(body, name=name, grid=(nt,),
                 in_specs=[tile, mat, mat, vec, vec, vec, tile, halo, vec, tile],
                 out_specs=[tile, pl.BlockSpec((tm, 2 * LRU_W), tidx), vec, vec, vec, vec],
                 out_shape=[_sds((t, LRU_W)), _sds((t, 2 * LRU_W), BF16), _sds((1, LRU_W)), _sds((1, LRU_W)),
                            _sds((1, LRU_W)), _sds((1, LRU_W))],
                 scratch_shapes=[pltpu.VMEM((tm, LRU_W), F32), pltpu.VMEM((tm, LRU_W), F32),
                                 pltpu.VMEM((tm, LRU_W), F32), pltpu.VMEM((SUBLANES, LRU_W), F32),
                                 pltpu.VMEM((SUBLANES, LRU_W), F32)],
                 sem=("arbitrary",), args=(xc, wa, wx, ba, bx, lam, h, h, h0, dh), comms=comms)


def _decay_tables(lg, reverse):
    ci = lax.broadcasted_iota(jnp.int32, (CHUNK, CHUNK), 0).astype(F32)
    mi = lax.broadcasted_iota(jnp.int32, (CHUNK, CHUNK), 1).astype(F32)
    rel = (mi - ci) if reverse else (ci - mi)
    relc = jnp.maximum(rel, 0.0)
    lg_c = jnp.concatenate([lg] * (CHUNK // LANES), axis=1)
    dm = jnp.where(rel >= 0, jnp.exp(lg_c * relc), 0.0)
    cd = lax.broadcasted_iota(jnp.int32, (CHUNK, DH), 0).astype(F32)
    pq, ps = (CHUNK - cd, cd) if reverse else (cd + 1.0, CHUNK - 1.0 - cd)
    return relc, dm, jnp.exp(lg * pq), jnp.exp(lg * ps), jnp.exp(lg * float(CHUNK)), pq, ps


def _ret_fwd(proj, lgv, s0f, s0b, comms=()):
    t = proj.shape[0]
    n = t // CHUNK

    def one(q, k, v, lg, s_s, hh, o_ref, sp_ref, reverse):
        _, dm, wq, ws, g, _, _ = _decay_tables(lg, reverse)
        vb = v.astype(BF16)
        p = _dot_nt(q.astype(BF16), k.astype(BF16)) * dm
        s = s_s[hh]
        sp_ref[hh, 0] = s
        o_ref[:, DH * hh:DH * (hh + 1)] = _dot(p.astype(BF16), vb) + _dot((q * wq).astype(BF16), s.astype(BF16))
        s_s[hh] = g * s + _dot_tn((k * ws).astype(BF16), vb)

    def body(qf, kf, vf, qb, kb, vb, lg_ref, s0f_ref, s0b_ref, of_ref, ob_ref, spf_ref, spb_ref, sf_s, sb_s):
        @pl.when(pl.program_id(0) == 0)
        def _():
            sf_s[...] = s0f_ref[...]
            sb_s[...] = s0b_ref[...]
        for hh in range(HEADS):
            sl = slice(DH * hh, DH * (hh + 1))
            one(qf[:, sl].astype(F32), kf[:, sl].astype(F32), vf[:, sl], lg_ref[hh, 0:1, :], sf_s, hh, of_ref, spf_ref,
                False)
            one(qb[:, sl].astype(F32), kb[:, sl].astype(F32), vb[:, sl], lg_ref[hh, 1:2, :], sb_s, hh, ob_ref, spb_ref,
                True)

    blk = (CHUNK, RET_W)
    fw = [pl.BlockSpec(blk, lambda i, o=o: (i, o)) for o in range(3)]
    bw = [pl.BlockSpec(blk, lambda i, o=o: (n - 1 - i, o)) for o in range(3)]
    st = _full((HEADS, DH, DH))
    return _call(body, name="ret_fwd", grid=(n,),
                 in_specs=fw + bw + [_full((HEADS, 2, LANES)), st, st],
                 out_specs=[pl.BlockSpec(blk, lambda i: (i, 0)), pl.BlockSpec(blk, lambda i: (n - 1 - i, 0)),
                            pl.BlockSpec((HEADS, 1, DH, DH), lambda i: (0, i, 0, 0)),
                            pl.BlockSpec((HEADS, 1, DH, DH), lambda i: (0, n - 1 - i, 0, 0))],
                 out_shape=[_sds((t, RET_W)), _sds((t, RET_W)), _sds((HEADS, n, DH, DH)), _sds((HEADS, n, DH, DH))],
                 scratch_shapes=[pltpu.VMEM((HEADS, DH, DH), F32), pltpu.VMEM((HEADS, DH, DH), F32)],
                 sem=("arbitrary",), args=(proj, proj, proj, proj, proj, proj, lgv, s0f, s0b), comms=comms)


def _ret_bwd(proj, lgv, sgv, sprev, do, reverse, name, comms=()):
    t = proj.shape[0]
    n = t // CHUNK
    d = 1 if reverse else 0
    cidx = (lambda i: i) if reverse else (lambda i: n - 1 - i)

    def body(q_ref, k_ref, v_ref, lg_ref, sg_ref, s_ref, do_ref, dq_ref, dk_ref, dv_ref, ds0_ref, drd_ref, ds_s, acc_s):
        i = pl.program_id(0)

        @pl.when(i == 0)
        def _():
            ds_s[...] = jnp.zeros_like(ds_s)
            acc_s[...] = jnp.zeros_like(acc_s)
        for hh in range(HEADS):
            sl = slice(DH * hh, DH * (hh + 1))
            relc, dm, wq, ws, g, pq, ps = _decay_tables(lg_ref[hh, d:d + 1, :], reverse)
            qb, kb, vb = q_ref[:, sl], k_ref[:, sl], v_ref[:, sl]
            q, k = qb.astype(F32), kb.astype(F32)
            p = _dot_nt(qb, kb) * dm
            s = s_ref[hh, 0]
            dob = do_ref[:, sl].astype(BF16)
            dsn = ds_s[hh]
            dsb = dsn.astype(BF16)
            dv_ref[:, sl] = (_dot_tn(p.astype(BF16), dob) + _dot((k * ws).astype(BF16), dsb)).astype(BF16)
            dp = _dot_nt(dob, vb)
            dab = (dp * dm).astype(BF16)
            xq = _dot_nt(dob, s.astype(BF16))
            yk = _dot_nt(vb, dsb)
            dq_ref[:, sl] = (_dot(dab, kb) + xq * wq).astype(BF16)
            dk_ref[:, sl] = (_dot_tn(dab, qb) + yk * ws).astype(BF16)
            ds_s[hh] = g * dsn + _dot_tn((q * wq).astype(BF16), dob)
            s_mask = _sum0(dp * p * relc)
            part = (sum(s_mask[:, LANES * u:LANES * (u + 1)] for u in range(CHUNK // LANES))
                    + _sum0(xq * q * wq * pq) + _sum0(yk * k * ws * ps) + _sum0(dsn * s) * g * float(CHUNK))
            acc_s[hh] += jnp.broadcast_to(part, (SUBLANES, LANES))

        @pl.when(i == n - 1)
        def _():
            ds0_ref[...] = ds_s[...]
            for hh in range(HEADS):
                tot = jnp.sum(acc_s[hh, 0:1, :], axis=1, keepdims=True)
                drd_ref[hh] = jnp.broadcast_to(tot, (SUBLANES, LANES)) * sg_ref[hh, d:d + 1, :]

    blk = (CHUNK, RET_W)
    qkv = [pl.BlockSpec(blk, lambda i, o=o: (cidx(i), o)) for o in range(3)]
    hc = pl.BlockSpec(blk, lambda i: (cidx(i), 0))
    lane = _full((HEADS, 2, LANES))
    return _call(body, name=name, grid=(n,),
                 in_specs=qkv + [lane, lane, pl.BlockSpec((HEADS, 1, DH, DH), lambda i: (0, cidx(i), 0, 0)), hc],
                 out_specs=[hc, hc, hc, _full((HEADS, DH, DH)), _full((HEADS, SUBLANES, LANES))],
                 out_shape=[_sds((t, RET_W), BF16)] * 3 + [_sds((HEADS, DH, DH)), _sds((HEADS, SUBLANES, LANES))],
                 scratch_shapes=[pltpu.VMEM((HEADS, DH, DH), F32), pltpu.VMEM((HEADS, SUBLANES, LANES), F32)],
                 sem=("arbitrary",), args=(proj, proj, proj, lgv, sgv, sprev, do), comms=comms)


def _ctx_weights(lg, l_len, reverse):
    pos = lax.broadcasted_iota(jnp.int32, (l_len, DH), 0).astype(F32)
    steps = pos if reverse else (l_len - 1.0 - pos)
    return jnp.exp(lg * steps), steps


def _ctx_state_fwd(projc, lgv):
    l_len = projc.shape[0]

    def body(k_ref, v_ref, lg_ref, sf_ref, sb_ref):
        k = k_ref[...]
        vb = v_ref[...].astype(BF16)
        for d, o_ref in ((0, sf_ref), (1, sb_ref)):
            w, _ = _ctx_weights(lg_ref[0, d:d + 1, :], l_len, d == 1)
            o_ref[0] = _dot_tn((k * w).astype(BF16), vb)

    st = pl.BlockSpec((1, DH, DH), lambda h: (h, 0, 0))
    return _pc(body, name="ctx_state_fwd", grid=(HEADS,),
               in_specs=[pl.BlockSpec((l_len, DH), lambda h: (0, HEADS + h)),
                         pl.BlockSpec((l_len, DH), lambda h: (0, 2 * HEADS + h)),
                         pl.BlockSpec((1, 2, LANES), lambda h: (h, 0, 0))],
               out_specs=[st, st], out_shape=[_sds((HEADS, DH, DH))] * 2,
               compiler_params=_params("arbitrary"))(projc, projc, lgv)


def _ctx_state_bwd(projc, lgv, sgv, dsf, dsb):
    l_len = projc.shape[0]

    def body(k_ref, v_ref, lg_ref, sg_ref, dsf_ref, dsb_ref, dk_ref, dv_ref, drd_ref):
        k = k_ref[...]
        vb = v_ref[...].astype(BF16)
        dk = jnp.zeros((l_len, DH), F32)
        dv = jnp.zeros((l_len, DH), F32)
        rows = []
        for d, ds_ref in ((0, dsf_ref), (1, dsb_ref)):
            w, steps = _ctx_weights(lg_ref[0, d:d + 1, :], l_len, d == 1)
            dsb16 = ds_ref[0].astype(BF16)
            dkw = _dot_nt(vb, dsb16)
            dk = dk + dkw * w
            dv = dv + _dot((k * w).astype(BF16), dsb16)
            tot = jnp.sum(_sum0(dkw * k * w * steps), axis=1, keepdims=True)
            rows.append(jnp.broadcast_to(tot, (1, LANES)) * sg_ref[0, d:d + 1, :])
        dk_ref[...] = dk.astype(BF16)
        dv_ref[...] = dv.astype(BF16)
        rid = lax.broadcasted_iota(jnp.int32, (SUBLANES, LANES), 0)
        drd_ref[0] = jnp.where(rid == 0, rows[0], jnp.where(rid == 1, rows[1], 0.0))

    st = pl.BlockSpec((1, DH, DH), lambda h: (h, 0, 0))
    lane = pl.BlockSpec((1, 2, LANES), lambda h: (h, 0, 0))
    hc = pl.BlockSpec((l_len, DH), lambda h: (0, h))
    return _pc(body, name="ctx_state_bwd", grid=(HEADS,),
               in_specs=[pl.BlockSpec((l_len, DH), lambda h: (0, HEADS + h)),
                         pl.BlockSpec((l_len, DH), lambda h: (0, 2 * HEADS + h)), lane, lane, st, st],
               out_specs=[hc, hc, pl.BlockSpec((1, SUBLANES, LANES), lambda h: (h, 0, 0))],
               out_shape=[_sds((l_len, RET_W), BF16), _sds((l_len, RET_W), BF16), _sds((HEADS, SUBLANES, LANES))],
               compiler_params=_params("arbitrary"))(projc, projc, lgv, sgv, dsf, dsb)


G_BLOCK = (3 * RET_W) // RET_W
GATE_BLOCK = (4 * RET_W + LRU_W) // LRU_W


def _head_norm(y):
    yc = y - jnp.mean(y, axis=-1, keepdims=True)
    rs = lax.rsqrt(jnp.mean(yc * yc, axis=-1, keepdims=True) + EPS)
    return yc * rs, rs


def _gelu_parts(z):
    th = jnp.tanh(GELU_K * (z + GELU_C * z * z * z))
    return 0.5 * z * (1.0 + th), th


def _mix_fwd(o_f, o_b, proj, hf, hb, w_out, x, g1):
    t = x.shape[0]
    tm = _tile(t, True)

    def body(of_ref, ob_ref, g_ref, gt_ref, hf_ref, hb_ref, w_ref, x_ref, g1_ref, x1_ref, cat_ref):
        o = of_ref[...] + ob_ref[...]
        g = g_ref[...].astype(F32)
        for hh in range(HEADS):
            sl = slice(DH * hh, DH * (hh + 1))
            nrm, _ = _head_norm(o[:, sl])
            gh = g[:, sl]
            cat_ref[:, sl] = (gh * _sigmoid(gh) * nrm).astype(BF16)
        gel, _ = _gelu_parts(gt_ref[...].astype(F32))
        cat_ref[:, RET_W:] = ((hf_ref[...] + hb_ref[...]) * gel).astype(BF16)
        x1_ref[...] = x_ref[...] + g1_ref[...] * _dot(cat_ref[...], w_ref[...])

    half = pl.BlockSpec((tm, RET_W), lambda i: (i, 0))
    big = pl.BlockSpec((tm, D_MODEL), lambda i: (i, 0))
    return _pc(body, name="mix_fwd", grid=(t // tm,),
               in_specs=[half, half, pl.BlockSpec((tm, RET_W), lambda i: (i, G_BLOCK)),
                         pl.BlockSpec((tm, LRU_W), lambda i: (i, GATE_BLOCK)), half, half,
                         _full((D_MODEL, D_MODEL)), big, _full((1, D_MODEL))],
               out_specs=[big, big], out_shape=[_sds((t, D_MODEL)), _sds((t, D_MODEL), BF16)],
               compiler_params=_params("arbitrary"))(o_f, o_b, proj, proj, hf, hb, w_out, x, g1)


def _mix_bwd(o_f, o_b, proj, hf, hb, w_out, cat, dx1, g1, comms=()):
    t = dx1.shape[0]
    tm = _tile(t, True)

    def body(of_ref, ob_ref, g_ref, gt_ref, hf_ref, hb_ref, w_ref, cat_ref, dx1_ref, g1_ref,
             do_ref, dhs_ref, dg_ref, dgt_ref, dyb_ref, dg1_ref):
        dx1v = dx1_ref[...]
        y = _dot(cat_ref[...], w_ref[...])

        @pl.when(pl.program_id(0) == 0)
        def _():
            dg1_ref[...] = jnp.zeros_like(dg1_ref)
        dg1_ref[...] += _sum0(dx1v * y)
        dyb = (g1_ref[...] * dx1v).astype(BF16)
        dyb_ref[...] = dyb
        dcat = _dot_nt(dyb, w_ref[...])
        o = of_ref[...] + ob_ref[...]
        g = g_ref[...].astype(F32)
        for hh in range(HEADS):
            sl = slice(DH * hh, DH * (hh + 1))
            nrm, rs = _head_norm(o[:, sl])
            gh = g[:, sl]
            sg = _sigmoid(gh)
            dret = dcat[:, sl]
            dg_ref[:, sl] = (dret * nrm * (sg * (1.0 + gh * (1.0 - sg)))).astype(BF16)
            dn = dret * (gh * sg)
            dyc = rs * (dn - nrm * jnp.mean(dn * nrm, axis=-1, keepdims=True))
            do_ref[:, sl] = (dyc - jnp.mean(dyc, axis=-1, keepdims=True)).astype(BF16)
        z = gt_ref[...].astype(F32)
        gel, th = _gelu_parts(z)
        dlru = dcat[:, RET_W:]
        dhs_ref[...] = dlru * gel
        dgel = 0.5 * (1.0 + th) + 0.5 * z * (1.0 - th * th) * GELU_K * (1.0 + 3.0 * GELU_C * z * z)
        dgt_ref[...] = (dlru * (hf_ref[...] + hb_ref[...]) * dgel).astype(BF16)

    half = pl.BlockSpec((tm, RET_W), lambda i: (i, 0))
    big = pl.BlockSpec((tm, D_MODEL), lambda i: (i, 0))
    return _call(body, name="mix_bwd", grid=(t // tm,),
                 in_specs=[half, half, pl.BlockSpec((tm, RET_W), lambda i: (i, G_BLOCK)),
                           pl.BlockSpec((tm, LRU_W), lambda i: (i, GATE_BLOCK)), half, half,
                           _full((D_MODEL, D_MODEL)), big, big, _full((1, D_MODEL))],
                 out_specs=[half, half, half, half, big, _full((1, D_MODEL))],
                 out_shape=[_sds((t, RET_W), BF16), _sds((t, RET_W)), _sds((t, RET_W), BF16), _sds((t, RET_W), BF16),
                            _sds((t, D_MODEL), BF16), _sds((1, D_MODEL))],
                 scratch_shapes=[], sem=("arbitrary",), args=(o_f, o_b, proj, proj, hf, hb, w_out, cat, dx1, g1),
                 comms=comms)


def _mlp(x1, n2g, sh2, sc2, g2, fg, w1_parts, w2_parts, tgt):
    t = x1.shape[0]
    tm = _tile(t)
    hb_ = MLP_H // N_CHIP
    q_rows = hb_ // 4
    n_cp = 4 * N_DEV

    def body(x1_ref, n2g_ref, sh2_ref, sc2_ref, g2_ref, fg_ref, w1a, w1b, w2a, w2b, tgt_ref,
             dx1_ref, h2b_ref, ab_ref, dub_ref, dmb_ref, dsc_ref, dsh_ref, dg2_ref, dn2_ref, dfg_ref, loss_ref,
             w1_s, w2_s, r_s, sems):
        @pl.when(pl.program_id(0) == 0)
        def _():
            cps = []
            for p, parts in enumerate(((w1a, w2a), (w1b, w2b))):
                for d in range(N_DEV):
                    rows = pl.ds(2 * q_rows * (d % 2) + q_rows * p, q_rows)
                    for src, dst in zip(parts, (w1_s, w2_s)):
                        cps.append(pltpu.make_async_copy(src.at[d], dst.at[d // 2, rows], sems.at[len(cps)]))
            for cp in cps:
                cp.start()
            for r in (dsc_ref, dsh_ref, dg2_ref, dn2_ref, dfg_ref, loss_ref):
                r[...] = jnp.zeros_like(r)
            for cp in cps:
                cp.wait()
        x1v = x1_ref[...]
        n2g, sc2, g2, fg = n2g_ref[...], sc2_ref[...], g2_ref[...], fg_ref[...]
        xh, _ = _rms(x1v)
        h2b = (xh * n2g * (1.0 + sc2) + sh2_ref[...]).astype(BF16)
        h2b_ref[...] = h2b
        m = jnp.zeros((tm, D_MODEL), F32)
        for j in range(N_CHIP):
            sl = slice(hb_ * j, hb_ * (j + 1))
            r = jnp.maximum(_dot(h2b, w1_s[j]), 0.0)
            r_s[:, sl] = r
            ab = (r * r).astype(BF16)
            ab_ref[:, sl] = ab
            m = m + _dot(ab, w2_s[j])
        x2 = x1v + g2 * m
        x2h, r2 = _rms(x2)
        err = x2h * fg - tgt_ref[...]
        loss_ref[...] += _sum0(err * err)
        dout = err * (1.0 / D_MODEL)
        dfg_ref[...] += _sum0(dout * x2h)
        dxh = dout * fg
        dx2 = r2 * (dxh - x2h * jnp.mean(dxh * x2h, axis=-1, keepdims=True))
        dg2_ref[...] += _sum0(dx2 * m)
        dmb = (g2 * dx2).astype(BF16)
        dmb_ref[...] = dmb
        dh2 = jnp.zeros((tm, D_MODEL), F32)
        for j in range(N_CHIP):
            sl = slice(hb_ * j, hb_ * (j + 1))
            dub = (_dot_nt(dmb, w2_s[j]) * (2.0 * r_s[:, sl])).astype(BF16)
            dub_ref[:, sl] = dub
            dh2 = dh2 + _dot_nt(dub, w1_s[j])
        dx, dn2_t, dsh_t, dsc_t = _norm_mod_bwd(x1v, n2g, sc2, dh2)
        dx1_ref[...] = dx2 + dx
        dn2_ref[...] += dn2_t
        dsh_ref[...] += dsh_t
        dsc_ref[...] += dsc_t

        @pl.when(pl.program_id(0) == t // tm - 1)
        def _():
            tot = jnp.sum(loss_ref[...], axis=1, keepdims=True) * (0.5 / D_MODEL)
            loss_ref[...] = jnp.broadcast_to(tot, loss_ref.shape)

    row = _full((1, D_MODEL))
    big = pl.BlockSpec((tm, D_MODEL), lambda i: (i, 0))
    wide = pl.BlockSpec((tm, MLP_H), lambda i: (i, 0))
    return _pc(body, name="mlp", grid=(t // tm,),
               in_specs=[big, row, row, row, row, row, ANY, ANY, ANY, ANY, big],
               out_specs=[big, big, wide, wide, big, row, row, row, row, row, row],
               out_shape=[_sds((t, D_MODEL)), _sds((t, D_MODEL), BF16), _sds((t, MLP_H), BF16), _sds((t, MLP_H), BF16),
                          _sds((t, D_MODEL), BF16)] + [_sds((1, D_MODEL))] * 6,
               scratch_shapes=[pltpu.VMEM((N_CHIP, D_MODEL, hb_), BF16), pltpu.VMEM((N_CHIP, hb_, D_MODEL), BF16),
                               pltpu.VMEM((tm, MLP_H), F32), pltpu.SemaphoreType.DMA((n_cp,))],
               compiler_params=_params("arbitrary"))(x1, n2g, sh2, sc2, g2, fg, *w1_parts, *w2_parts, tgt)


def _tn(a, b, nj, a_blocked, b_blocked, name, extra=None, comms=()):
    t = a.shape[0]
    m = a.shape[1] // (nj if a_blocked else 1)
    n = b.shape[1] // (nj if b_blocked else 1)
    bk = next((b for b in (2048, 1024, 512) if t % b == 0), t)
    nk = t // bk
    a_map = (lambda j, k: (k, j)) if a_blocked else (lambda j, k: (k, 0))
    b_map = (lambda j, k: (k, j)) if b_blocked else (lambda j, k: (k, 0))
    in_specs = [pl.BlockSpec((bk, m), a_map), pl.BlockSpec((bk, n), b_map)]
    args = [a, b]
    if extra is not None:
        a2, b2 = extra
        t2 = a2.shape[0]
        in_specs += [pl.BlockSpec((t2, m), (lambda j, k: (0, j)) if a_blocked else (lambda j, k: (0, 0))),
                     pl.BlockSpec((t2, n), (lambda j, k: (0, j)) if b_blocked else (lambda j, k: (0, 0)))]
        args += [a2, b2]

    def body(*refs):
        a_ref, b_ref = refs[0], refs[1]
        o_ref, acc = refs[-2], refs[-1]
        k = pl.program_id(1)

        @pl.when(k == 0)
        def _():
            acc[...] = jnp.zeros_like(acc)
        acc[...] += _dot_tn(a_ref[...].astype(BF16), b_ref[...].astype(BF16))

        @pl.when(k == nk - 1)
        def _():
            if extra is not None:
                acc[...] += _dot_tn(refs[2][...].astype(BF16), refs[3][...].astype(BF16))
            o_ref[0] = acc[...]

    (out,), couts = _call(body, name=name, grid=(nj, nk), in_specs=in_specs,
                          out_specs=[pl.BlockSpec((1, m, n), lambda j, k: (j, 0, 0))], out_shape=[_sds((nj, m, n))],
                          scratch_shapes=[pltpu.VMEM((m, n), F32)], sem=("arbitrary", "arbitrary"), args=args,
                          comms=comms)
    return (out, couts) if comms else out


ROW_LOSS = 0
ROW_DMOD = 1
ROW_DMODC = 7
ROW_N1, ROW_N2, ROW_FG, ROW_CB = 9, 10, 11, 12
ROW_BA, ROW_BX, ROW_LAM = 13, 15, 17
ROW_CW = 20
ROW_RD = 24
SLAB_ROWS = 32
SEG = D_MODEL // 2


def _pack_small(rows, drd, cw2, cb2, lru2, gates):
    n_rows, n_lru = len(rows), len(lru2)

    def body(*refs):
        r = refs[:n_rows]
        drd_f, drd_b, drd_c, cw_a, cw_b, cb_a, cb_b = refs[n_rows:n_rows + 7]
        lru = refs[n_rows + 7:n_rows + 7 + n_lru]
        gf_ref, gb_ref, slab, ga, gx = refs[n_rows + 7 + n_lru:]
        slab[...] = jnp.zeros_like(slab)
        slab[ROW_LOSS:ROW_LOSS + 1, :] = r[0][...]
        for k in range(N_MOD):
            slab[ROW_DMOD + k:ROW_DMOD + k + 1, :] = r[1 + k][...]
        slab[ROW_DMODC:ROW_DMODC + 1, :] = r[7][...]
        slab[ROW_DMODC + 1:ROW_DMODC + 2, :] = r[8][...]
        slab[ROW_N1:ROW_N1 + 1, :] = r[9][...] + r[10][...]
        slab[ROW_N2:ROW_N2 + 1, :] = r[11][...]
        slab[ROW_FG:ROW_FG + 1, :] = r[12][...]
        slab[ROW_CB:ROW_CB + 1, 0:LRU_W] = cb_a[...] + cb_b[...]
        for k, row in enumerate((ROW_BA, ROW_BA + 1, ROW_BX, ROW_BX + 1, ROW_LAM, ROW_LAM + 1)):
            slab[row:row + 1, 0:LRU_W] = lru[2 * k][...] + lru[2 * k + 1][...]
        slab[ROW_CW:ROW_CW + 4, 0:LRU_W] = cw_a[...] + cw_b[...]
        for h in range(HEADS):
            slab[ROW_RD + h:ROW_RD + h + 1, 0:LANES] = drd_f[h, 0:1, :] + drd_c[h, 0:1, :]
            slab[ROW_RD + HEADS + h:ROW_RD + HEADS + h + 1, 0:LANES] = drd_b[h, 0:1, :] + drd_c[h, 1:2, :]
        for d, g_ref in enumerate((gf_ref, gb_ref)):
            for n in range(LRU_BLOCKS):
                blk = slice(LRU_BD * n, LRU_BD * (n + 1))
                ga[blk, LRU_BD * d:LRU_BD * (d + 1)] = g_ref[0, blk, blk].astype(BF16)
                gx[blk, LRU_BD * d:LRU_BD * (d + 1)] = g_ref[1, blk, blk].astype(BF16)

    args = list(rows) + list(drd) + list(cw2) + list(cb2) + list(lru2) + list(gates)
    gate_shape = (LRU_W, 2 * LRU_BD)
    return _pc(body, name="pack_small", in_specs=[_full(a.shape) for a in args],
               out_specs=[_full((SLAB_ROWS, D_MODEL)), _full(gate_shape), _full(gate_shape)],
               out_shape=[_sds((SLAB_ROWS, D_MODEL)), _sds(gate_shape, BF16), _sds(gate_shape, BF16)],
               compiler_params=_params())(*args)


def _adam_math(w, g, m, v):
    mn = ADAM_B1 * m + (1.0 - ADAM_B1) * g
    vn = ADAM_B2 * v + (1.0 - ADAM_B2) * (g * g)
    mh = mn / (1.0 - ADAM_B1 ** ADAM_STEP)
    vh = vn / (1.0 - ADAM_B2 ** ADAM_STEP)
    return -ADAM_LR * (mh / (jnp.sqrt(vh) + ADAM_EPS) + ADAM_WD * w), mn, vn


SMALL_PARAMS = ("b_ada", "norm1_g", "norm2_g", "final_g", "ret_decay", "conv_w", "conv_b", "lru_wa", "lru_ba", "lru_wx",
                "lru_bx", "lru_lambda")


def _finalize_small(chip_idx, slab_all, ga_all, gx_all, wmv):
    n_p = len(SMALL_PARAMS)
    flat = [a for nm in SMALL_PARAMS for a in wmv[nm]]
    ada_n = N_MOD * D_MODEL // N_CHIP

    def body(c_ref, slab_ref, ga_ref, gx_ref, *refs):
        prm = {nm: refs[3 * k:3 * k + 3] for k, nm in enumerate(SMALL_PARAMS)}
        outs = {nm: refs[3 * n_p + 4 * k:3 * n_p + 4 * k + 4] for k, nm in enumerate(SMALL_PARAMS)}
        b128_ref, dmc_ref, loss_ref = refs[3 * n_p + 4 * n_p:]
        chip = c_ref[0]

        def pick(fn):
            acc = fn(0)
            for j in range(1, N_CHIP):
                acc = jnp.where(chip == j, fn(j), acc)
            return acc

        tot = slab_ref[0]
        for d in range(1, N_DEV):
            tot = tot + slab_ref[d]

        def update(nm, g, sl=None, rows=None):
            w_ref, m_ref, v_ref = prm[nm]
            g_ref, d_ref, mo_ref, vo_ref = outs[nm]
            ix = (slice(None) if rows is None else rows, slice(None) if sl is None else sl)
            dl, mn, vn = _adam_math(w_ref[ix], g, m_ref[ix], v_ref[ix])
            g_ref[ix] = g
            d_ref[ix] = dl
            mo_ref[ix] = mn
            vo_ref[ix] = vn

        loss_ref[...] = jnp.broadcast_to(tot[ROW_LOSS:ROW_LOSS + 1, 0:LANES], (SUBLANES, LANES))
        for k in range(N_MOD):
            g = tot[ROW_DMOD + k:ROW_DMOD + k + 1, :]
            if k < 2:
                g = g + tot[ROW_DMODC + k:ROW_DMODC + k + 1, :]
            update("b_ada", g, slice(D_MODEL * k, D_MODEL * (k + 1)))
        update("norm1_g", tot[ROW_N1:ROW_N1 + 1, :])
        update("norm2_g", tot[ROW_N2:ROW_N2 + 1, :])
        update("final_g", tot[ROW_FG:ROW_FG + 1, :])
        update("ret_decay", tot[ROW_RD:ROW_RD + SUBLANES, 0:LANES])
        update("conv_b", tot[ROW_CB:ROW_CB + 1, 0:LRU_W])
        update("conv_w", pick(lambda j: tot[ROW_CW:ROW_CW + 4, LANES * j:LANES * (j + 1)]))
        for nm, row in (("lru_ba", ROW_BA), ("lru_bx", ROW_BX), ("lru_lambda", ROW_LAM)):
            update(nm, pick(lambda j, row=row: tot[row:row + 2, LANES * j:LANES * (j + 1)]))
        for nm, g_all in (("lru_wa", ga_ref), ("lru_wx", gx_ref)):
            for dr in range(2):
                lanes = slice(LRU_BD * dr, LRU_BD * (dr + 1))
                g = g_all[0, :, lanes].astype(F32)
                for d in range(1, N_DEV):
                    g = g + g_all[d, :, lanes].astype(F32)
                update(nm, g, rows=slice(LRU_W * dr, LRU_W * (dr + 1)))

        def seg(rows6, s):
            return rows6[s // 2][:, SEG * (s % 2):SEG * (s % 2 + 1)]

        b128_ref[...] = jnp.zeros_like(b128_ref)
        dmc_ref[...] = jnp.zeros_like(dmc_ref)
        zero = jnp.zeros((1, D_MODEL), F32)
        ctx6 = [tot[ROW_DMODC:ROW_DMODC + 1, :], tot[ROW_DMODC + 1:ROW_DMODC + 2, :]] + [zero] * (N_MOD - 2)
        for q in range(ada_n // SEG):
            cols = slice(SEG * q, SEG * (q + 1))
            for d in range(N_DEV):
                rows6 = [slab_ref[d, ROW_DMOD + k:ROW_DMOD + k + 1, :] for k in range(N_MOD)]
                b128_ref[d:d + 1, cols] = pick(lambda j, rows6=rows6: seg(rows6, 3 * j + q))
            c = pick(lambda j: seg(ctx6, 3 * j + q))
            b128_ref[N_DEV:N_DEV + 1, cols] = c
            dmc_ref[0:1, cols] = c

    out_shape = []
    for nm in SMALL_PARAMS:
        out_shape += [_sds(wmv[nm][0].shape)] * 4
    out_shape += [_sds((LANES, ada_n)), _sds((SUBLANES, ada_n)), _sds((SUBLANES, LANES))]
    args = [slab_all, ga_all, gx_all] + flat
    grid_spec = pltpu.PrefetchScalarGridSpec(
        num_scalar_prefetch=1, grid=(1,), in_specs=[_full(a.shape) for a in args],
        out_specs=[_full(s.shape) for s in out_shape])
    outs = _pc(body, name="finalize_small", grid_spec=grid_spec, out_shape=out_shape,
               compiler_params=_params("arbitrary"))(chip_idx, *args)
    res = {nm: tuple(outs[4 * k:4 * k + 4]) for k, nm in enumerate(SMALL_PARAMS)}
    return res, outs[4 * n_p], outs[4 * n_p + 1], outs[4 * n_p + 2]


def _block_diag(w):
    eye = jnp.eye(LRU_BLOCKS, dtype=F32)
    return (w[:, :, None, :] * eye[:, None, :, None]).reshape(LRU_W, LRU_W).astype(BF16)


def _lane_rep(v8):
    return jnp.broadcast_to(v8.reshape(SUBLANES, 1), (SUBLANES, LANES))


def kernel(x, c, ctx, c_ctx, w_ada, b_ada, norm1_g, norm2_g, w_in, ret_decay, conv_w, conv_b, lru_wa, lru_ba, lru_wx, lru_bx, lru_lambda, w_out, w_mlp1, w_mlp2, final_g, loss_target, m_c_ctx, m_w_ada, m_b_ada, m_norm1_g, m_norm2_g, m_w_in, m_ret_decay, m_conv_w, m_conv_b, m_lru_wa, m_lru_ba, m_lru_wx, m_lru_bx, m_lru_lambda, m_w_out, m_w_mlp1, m_w_mlp2, m_final_g, v_c_ctx, v_w_ada, v_b_ada, v_norm1_g, v_norm2_g, v_w_in, v_ret_decay, v_conv_w, v_conv_b, v_lru_wa, v_lru_ba, v_lru_wx, v_lru_bx, v_lru_lambda, v_w_out, v_w_mlp1, v_w_mlp2, v_final_g):
    ax, ay, ac = lax.axis_index("x"), lax.axis_index("y"), lax.axis_index("c")
    chip = 2 * ax + ay
    dev = 4 * ax + 2 * ay + ac
    c_idx = ac.reshape(1).astype(jnp.int32)
    j_idx = chip.reshape(1).astype(jnp.int32)

    xt = x[0]
    t_len = xt.shape[0]
    ctxt = ctx[0]
    l_len = ctxt.shape[0]
    tgt = loss_target[0]
    ada_n = w_ada.shape[2]

    def my_half(w2d):
        r = w2d.shape[0] // 2
        return lax.dynamic_slice_in_dim(w2d, ac * r, r, axis=0).astype(BF16)

    pad8 = lambda a: jnp.pad(a, ((0, SUBLANES - a.shape[0]), (0, 0)))
    small = jnp.concatenate([pad8(conv_w[0]), pad8(lru_ba[0]), pad8(lru_bx[0]), pad8(lru_lambda[0])], axis=0)
    b_shard = lax.dynamic_slice_in_dim(b_ada, chip * ada_n, ada_n, axis=1)
    gw_in, _, small_all, a16, mod_parts, lgv, sgv = _head(
        my_half(w_in[0]), pad8(c), small, w_ada[0], b_shard, c_ctx, ret_decay[0])
    w4 = gw_in.reshape(N_CHIP, D_MODEL, IN_COLS // N_CHIP)

    mod_all = mod_parts[0::2].transpose(1, 0, 2).reshape(16, N_CHIP * ada_n)
    mod_me = lax.dynamic_slice_in_dim(mod_all, dev, 1, axis=0)
    sh1, sc1, g1, sh2, sc2, g2 = [mod_me[:, D_MODEL * k:D_MODEL * (k + 1)] for k in range(N_MOD)]
    csh1, csc1 = mod_all[8:9, 0:D_MODEL], mod_all[8:9, D_MODEL:2 * D_MODEL]

    cos2, sin2 = _rotary_tables(t_len)
    cos_c, sin_c = jnp.ones((l_len, DH), F32), jnp.zeros((l_len, DH), F32)
    n1g, n2g = norm1_g, norm2_g
    fg = final_g.reshape(1, D_MODEL)

    small_full = small_all[0::2].transpose(1, 0, 2).reshape(4 * SUBLANES, LRU_W)
    cw = small_full[0:4]
    cb = conv_b
    ba_f, ba_b = small_full[8:9], small_full[9:10]
    bx_f, bx_b = small_full[16:17], small_full[17:18]
    lam_f, lam_b = small_full[24:25], small_full[25:26]
    wa_f, wa_b = _block_diag(lru_wa[0, 0]), _block_diag(lru_wa[0, 1])
    wx_f, wx_b = _block_diag(lru_wx[0, 0]), _block_diag(lru_wx[0, 1])
    zero_h = jnp.zeros((1, LRU_W), F32)

    projc, xrc, hcb16 = _inproj_fwd(ctxt, n1g, csh1, csc1, w4, cos_c, sin_c, "inproj_fwd_ctx")
    s_f, s_b = _ctx_state_fwd(projc, lgv)
    xcc = _conv_fwd(xrc, cw, cb, "conv_fwd_ctx")
    hcf = _lru_fwd(xcc, wa_f, wx_f, ba_f, bx_f, lam_f, zero_h, False, "lru_fwd_ctx_f")
    hcbk = _lru_fwd(xcc, wa_b, wx_b, ba_b, bx_b, lam_b, zero_h, True, "lru_fwd_ctx_b")
    lru_sf, lru_sb = hcf[l_len - 1:l_len], hcbk[0:1]

    h1, h2 = my_half(w_mlp1[0]), my_half(w_mlp2[0])
    q = h1.shape[0] // 2
    (proj, xrl, hb16), ((gw_1a,),) = _inproj_fwd(xt, n1g, sh1, sc1, w4, cos2, sin2, "inproj_fwd",
                                           comms=(_AllGather([h1[:q]]),))
    (o_f, o_b, spf, spb), ((gw_1b, gw_out),) = _ret_fwd(proj, lgv, s_f, s_b,
                                                       comms=(_AllGather([h1[q:], my_half(w_out[0])]),))
    xcl = _conv_fwd(xrl, cw, cb, "conv_fwd")
    hf, ((gw_2a,),) = _lru_fwd(xcl, wa_f, wx_f, ba_f, bx_f, lam_f, lru_sf, False, "lru_fwd_f",
                              comms=(_AllGather([h2[:q]]),))
    hbk, ((gw_2b,),) = _lru_fwd(xcl, wa_b, wx_b, ba_b, bx_b, lam_b, lru_sb, True, "lru_fwd_b",
                               comms=(_AllGather([h2[q:]]),))
    wo = gw_out.reshape(D_MODEL, D_MODEL)
    x1, cat = _mix_fwd(o_f, o_b, proj, hf, hbk, wo, xt, g1)

    (dx1, h2b, ab, dub, dmb, dsc2, dsh2, dg2, dn2g, dfg, lossv) = _mlp(
        x1, n2g, sh2, sc2, g2, fg, (gw_1a, gw_1b), (gw_2a, gw_2b), tgt)
    gw_mlp1 = _tn(h2b, dub, N_CHIP, False, True, "grad_w_mlp1")
    b_1 = gw_mlp1.reshape(N_DEV, D_MODEL // 2, MLP_H // N_CHIP)
    gw_mlp2, ((r_1,),) = _tn(ab, dmb, N_CHIP, True, False, "grad_w_mlp2", comms=(_pair_exchange([b_1]),))

    jc_idx = jnp.concatenate([j_idx, c_idx])
    b_2 = gw_mlp2.reshape(N_DEV, MLP_H // N_DEV, D_MODEL)
    p_1, pb_1 = _pair_add(b_1, r_1, c_idx, "rs_pair_add_w_mlp1")
    (do, dhs, dg, dgate, dyb, dg1), ((q_1,), (r_2,)) = _mix_bwd(
        o_f, o_b, proj, hf, hbk, wo, cat, dx1, g1, comms=(_chip_exchange([pb_1]), _pair_exchange([b_2])))
    gw_o = _tn(cat, dyb, 1, False, False, "grad_w_out")
    b_o = gw_o.reshape(N_DEV, D_MODEL // N_DEV, D_MODEL)
    p_2, pb_2 = _pair_add(b_2, r_2, c_idx, "rs_pair_add_w_mlp2")
    h_1 = _chip_add(p_1, q_1, jc_idx, "rs_chip_add_w_mlp1")

    (dq_f, dk_f, dv_f, ds_f, drd_f), ((r_o,), (f_1,)) = _ret_bwd(
        proj, lgv, sgv, spf, do, False, "ret_bwd_f", comms=(_pair_exchange([b_o]), _pair_gather([h_1])))
    p_o, pb_o = _pair_add(b_o, r_o, c_idx, "rs_pair_add_w_out")

    (dq_b, dk_b, dv_b, ds_b, drd_b), ((q_o,),) = _ret_bwd(
        proj, lgv, sgv, spb, do, True, "ret_bwd_b", comms=(_chip_exchange([pb_o]),))
    h_o = _chip_add(p_o, q_o, jc_idx, "rs_chip_add_w_out")

    (dxc_f, dpre_f, dba_f, dbx_f, dlam_f, dh0_f), ((q_2,), (f_o,)) = _lru_bwd(
        xcl, wa_f, wx_f, ba_f, bx_f, lam_f, hf, lru_sf, dhs, False, "lru_bwd_f",
        comms=(_chip_exchange([pb_2]), _pair_gather([h_o])))
    h_2 = _chip_add(p_2, q_2, jc_idx, "rs_chip_add_w_mlp2")
    (dxc_b, dpre_b, dba_b, dbx_b, dlam_b, dh0_b), ((f_2,),) = _lru_bwd(
        xcl, wa_b, wx_b, ba_b, bx_b, lam_b, hbk, lru_sb, dhs, True, "lru_bwd_b", comms=(_pair_gather([h_2]),))
    dxr, dcw, dcb = _conv_bwd(dxc_f, dxc_b, xrl, cw, "conv_bwd")
    grad_x, dpb, dn1g, dsh1, dsc1 = _inproj_bwd(
        xt, n1g, sh1, sc1, w4, cos2, sin2, [dq_f, dq_b, dk_f, dk_b, dv_f, dv_b, dg, dxr, dgate], dx1, "inproj_bwd")

    dkc, dvc, drd_c = _ctx_state_bwd(projc, lgv, sgv, ds_f, ds_b)
    zc = jnp.zeros((l_len, LRU_W), F32)
    dhc_f = lax.dynamic_update_slice(zc, dh0_f, (l_len - 1, 0))
    dhc_b = lax.dynamic_update_slice(zc, dh0_b, (0, 0))
    (dxcc_f, dprec_f, dbac_f, dbxc_f, dlamc_f, _), _ = _lru_bwd(
        xcc, wa_f, wx_f, ba_f, bx_f, lam_f, hcf, zero_h, dhc_f, False, "lru_bwd_ctx_f")
    (dxcc_b, dprec_b, dbac_b, dbxc_b, dlamc_b, _), _ = _lru_bwd(
        xcc, wa_b, wx_b, ba_b, bx_b, lam_b, hcbk, zero_h, dhc_b, True, "lru_bwd_ctx_b")
    dxrc, dcw_c, dcb_c = _conv_bwd(dxcc_f, dxcc_b, xrc, cw, "conv_bwd_ctx")
    zr = jnp.zeros((l_len, RET_W), BF16)
    _, dpbc, dn1g_c, dcsh1, dcsc1 = _inproj_bwd(
        ctxt, n1g, csh1, csc1, w4, cos_c, sin_c, [zr, zr, dkc, zr, dvc, zr, zr, dxrc, zr],
        jnp.zeros((l_len, D_MODEL), F32), "inproj_bwd_ctx")

    gw_i = _tn(hb16, dpb, N_CHIP, False, True, "grad_w_in", extra=(hcb16, dpbc))
    b_i = gw_i.reshape(N_DEV, D_MODEL // 2, IN_COLS // N_CHIP)
    gwa_f, ((r_i,),) = _tn(xcl, dpre_f, 2, False, True, "grad_lru_gates_f", extra=(xcc, dprec_f),
                           comms=(_pair_exchange([b_i]),))
    p_i, pb_i = _pair_add(b_i, r_i, c_idx, "rs_pair_add_w_in")
    gwa_b, ((q_i,),) = _tn(xcl, dpre_b, 2, False, True, "grad_lru_gates_b", extra=(xcc, dprec_b),
                           comms=(_chip_exchange([pb_i]),))
    (f_i,) = _run_comm(_pair_gather([_chip_add(p_i, q_i, jc_idx, "rs_chip_add_w_in")]), "rs_pair_gather_w_in")
    g_in, g_out, g_1, g_2 = _shard_of(f_i), _shard_of(f_o), _shard_of(f_1), _shard_of(f_2)
    big = {}
    for nm, w, g, m, v in (("w_in", w_in, g_in, m_w_in, v_w_in), ("w_out", w_out, g_out, m_w_out, v_w_out),
                           ("w_mlp1", w_mlp1, g_1, m_w_mlp1, v_w_mlp1), ("w_mlp2", w_mlp2, g_2, m_w_mlp2, v_w_mlp2)):
        go, d_, mn, vn = _adamw(w[0], g, m[0], v[0], "adamw_" + nm)
        big[nm] = (go[None], d_[None], mn[None], vn[None])

    slab, ga, gx = _pack_small(
        [lossv, dsh1, dsc1, dg1, dsh2, dsc2, dg2, dcsh1, dcsc1, dn1g, dn1g_c, dn2g, dfg],
        (drd_f, drd_b, drd_c), (dcw, dcw_c), (dcb, dcb_c),
        (dba_f, dbac_f, dba_b, dbac_b, dbx_f, dbxc_f, dbx_b, dbxc_b, dlam_f, dlamc_f, dlam_b, dlamc_b),
        (gwa_f, gwa_b))
    slab_all, ga_all, gx_all = _all_gather([slab, ga, gx], "gather_small_grads")
    params = {
        "b_ada": (b_ada, m_b_ada, v_b_ada), "norm1_g": (norm1_g, m_norm1_g, v_norm1_g),
        "norm2_g": (norm2_g, m_norm2_g, v_norm2_g), "final_g": (final_g, m_final_g, v_final_g),
        "ret_decay": (ret_decay, m_ret_decay, v_ret_decay), "conv_w": (conv_w, m_conv_w, v_conv_w),
        "conv_b": (conv_b, m_conv_b, v_conv_b), "lru_wa": (lru_wa, m_lru_wa, v_lru_wa),
        "lru_ba": (lru_ba, m_lru_ba, v_lru_ba), "lru_wx": (lru_wx, m_lru_wx, v_lru_wx),
        "lru_bx": (lru_bx, m_lru_bx, v_lru_bx), "lru_lambda": (lru_lambda, m_lru_lambda, v_lru_lambda),
    }
    as2d = {
        "b_ada": lambda a: a, "norm1_g": lambda a: a, "norm2_g": lambda a: a, "conv_b": lambda a: a,
        "final_g": lambda a: a.reshape(1, D_MODEL), "ret_decay": lambda a: _lane_rep(a.reshape(-1)),
        "conv_w": lambda a: a[0], "lru_ba": lambda a: a[0], "lru_bx": lambda a: a[0], "lru_lambda": lambda a: a[0],
        "lru_wa": lambda a: a.reshape(2 * LRU_W, LRU_BD), "lru_wx": lambda a: a.reshape(2 * LRU_W, LRU_BD),
    }
    res, b128, dmc8, loss8 = _finalize_small(
        j_idx, slab_all, ga_all, gx_all, {nm: tuple(as2d[nm](a) for a in params[nm]) for nm in SMALL_PARAMS})
    loss = loss8[0, 0]
    small_out = {}
    for nm in SMALL_PARAMS:
        shp = params[nm][0].shape
        if nm == "ret_decay":
            small_out[nm] = tuple(o[:, 0].reshape(shp) for o in res[nm])
        else:
            small_out[nm] = tuple(o.reshape(shp) for o in res[nm])

    g_ada = _ada_grad(jnp.pad(a16.T, ((0, 0), (0, LANES - 16))), b128)
    g_ada, d_ada, m_ada, v_ada = _adamw(w_ada[0], g_ada, m_w_ada[0], v_w_ada[0], "adamw_w_ada")

    (cparts,) = _all_gather([_cctx_partial(dmc8, w_ada[0])], "gather_cctx")
    g_cc, d_cc, m_cc, v_cc = _cctx_final(cparts, c_ctx, m_c_ctx, v_c_ctx)
    small_out["c_ctx"] = tuple(a.reshape(D_MODEL) for a in (g_cc, d_cc, m_cc, v_cc))
    small_out["w_ada"] = (g_ada[None], d_ada[None], m_ada[None], v_ada[None])
    small_out.update(big)

    order = ["c_ctx", "w_ada", "b_ada", "norm1_g", "norm2_g", "w_in", "ret_decay", "conv_w", "conv_b", "lru_wa", "lru_ba",
             "lru_wx", "lru_bx", "lru_lambda", "w_out", "w_mlp1", "w_mlp2", "final_g"]
    outs = [loss, grad_x[None]]
    for k in range(4):
        outs += [small_out[nm][k] for nm in order]
    return tuple(outs)
```

```python
import math

import jax
import jax.numpy as jnp
from jax import lax
from jax.experimental import pallas as pl
from jax.experimental.pallas import tpu as pltpu

F32 = jnp.float32
BF16 = jnp.bfloat16

D_MODEL = 1024
HEADS = 4
DH = 128
CHUNK = 256
RET_W = HEADS * DH
LRU_W = 512
LRU_BLOCKS = 8
LRU_BD = LRU_W // LRU_BLOCKS
LRU_C = 8.0
IN_COLS = 4 * RET_W + 2 * LRU_W
MLP_H = 4 * D_MODEL
N_MOD = 6
GRID_W = 64
ROPE_BASE = 10000.0
K_SCALE = DH ** -0.5
EPS = 1e-6
GELU_K = math.sqrt(2.0 / math.pi)
GELU_C = 0.044715

ADAM_LR = 0.001
ADAM_B1 = 0.9
ADAM_B2 = 0.999
ADAM_EPS = 1e-08
ADAM_WD = 0.01
ADAM_STEP = 10

N_DEV = 8
N_CHIP = 4
SUBLANES = 8
LANES = 128
VMEM_LIMIT_V7X = 56 * 1024 * 1024
MESH = pl.DeviceIdType.MESH
ANY = pl.BlockSpec(memory_space=pl.ANY)


def _pc(body, **kw):
    return pl.pallas_call(body, **kw)


def _params(*sem):
    return pltpu.CompilerParams(dimension_semantics=sem if sem else None, vmem_limit_bytes=VMEM_LIMIT_V7X)


def _tile(t, big=False):
    if big and t >= 1024:
        return 512
    return 256 if t >= 256 else t


def _sds(shape, dtype=F32):
    return jax.ShapeDtypeStruct(tuple(shape), dtype)


def _full(shape):
    nd = len(shape)
    return pl.BlockSpec(tuple(shape), lambda *_: (0,) * nd)


def _sigmoid(x):
    return 1.0 / (1.0 + jnp.exp(-x))


def _log1p_pos(y):
    s = y * (1.0 - y * (0.5 - y * (1.0 / 3.0 - y * (0.25 - y * (0.2 - y / 6.0)))))
    return jnp.where(y < 0.03, s, jnp.log(1.0 + y))


def _softplus(z):
    return jnp.maximum(z, 0.0) + _log1p_pos(jnp.exp(-jnp.abs(z)))


def _one_minus_sq(la, a):
    t = la * (1.0 + la * (0.5 + la * (1.0 / 6.0 + la * (1.0 / 24.0 + la * (1.0 / 120.0)))))
    return jnp.where(la > -0.125, -t, 1.0 - a) * (1.0 + a)


def _rms(x):
    r = lax.rsqrt(jnp.mean(x * x, axis=-1, keepdims=True) + EPS)
    return x * r, r


def _dot(a, b):
    return jnp.dot(a, b, preferred_element_type=F32)


def _dot_nt(a, b):
    return lax.dot_general(a, b, (((1,), (1,)), ((), ())), preferred_element_type=F32)


def _dot_tn(a, b):
    return lax.dot_general(a, b, (((0,), (0,)), ((), ())), preferred_element_type=F32)


def _sum0(x):
    return jnp.sum(x, axis=0, keepdims=True)


def _norm_mod_bwd(x, g, sc, dh):
    xh, r = _rms(x)
    hn = xh * g
    dhn = dh * (1.0 + sc)
    dxh = dhn * g
    dx = r * (dxh - xh * jnp.mean(dxh * xh, axis=-1, keepdims=True))
    return dx, _sum0(dhn * xh), _sum0(dh), _sum0(dh * hn)


def _dev_index(p):
    return 4 * p[0] + 2 * p[1] + p[2]


def _mesh_pos():
    return lax.axis_index("x"), lax.axis_index("y"), lax.axis_index("c")


class _AllGather:
    def __init__(self, arrs):
        n = len(arrs)
        self.arrays = list(arrs)
        self.out_shapes = [_sds((N_DEV,) + a.shape, a.dtype) for a in arrs]
        self.scratch = ([pltpu.VMEM(a.shape, a.dtype) for a in arrs]
                        + [pltpu.SemaphoreType.DMA((7 * n,)), pltpu.SemaphoreType.DMA((7 * n,)),
                           pltpu.SemaphoreType.DMA((n,))])
        self.aliases = {}

    def _parts(self, ins, outs, scr):
        n = len(self.arrays)
        stage = scr[:n]
        send_sems, recv_sems, local_sems = scr[n:]
        x, y, c = _mesh_pos()
        me, sib = (x, y, c), (x, y, 1 - c)
        chips = [(1 - x, y), (x, 1 - y), (1 - x, 1 - y)]

        def copy(t, k, block, to, own=False):
            dst = outs[t].at[_dev_index(block)]
            return pltpu.make_async_remote_copy(
                src_ref=ins[t] if own else dst, dst_ref=dst,
                send_sem=send_sems.at[7 * t + k], recv_sem=recv_sems.at[7 * t + k],
                device_id=to, device_id_type=MESH)

        first = []
        for t in range(n):
            first.append(copy(t, 0, me, sib, own=True))
            for j, ch in enumerate(chips):
                first.append(copy(t, 1 + j, me, (*ch, c), own=True))
        stage_in = [pltpu.make_async_copy(ins[t], stage[t], local_sems.at[t]) for t in range(n)]
        mine = [pltpu.make_async_copy(stage[t], outs[t].at[_dev_index(me)], local_sems.at[t]) for t in range(n)]
        return n, c, me, sib, chips, copy, first, stage_in, mine

    def start(self, ins, outs, scr):
        n, _, _, _, _, _, first, stage_in, mine = self._parts(ins, outs, scr)
        for cp in stage_in:
            cp.start()
        for cp in first:
            cp.start()
        for t in range(n):
            stage_in[t].wait()
            mine[t].start()

    def finish(self, ins, outs, scr):
        n, c, me, sib, chips, copy, first, _, mine = self._parts(ins, outs, scr)
        passed = []
        for j, ch in enumerate(chips):
            for t in range(n):
                copy(t, 1 + j, (*ch, c), me).wait_recv()
                p = copy(t, 4 + j, (*ch, c), sib)
                p.start()
                passed.append(p)
        for t in range(n):
            copy(t, 0, sib, me).wait_recv()
            for j, ch in enumerate(chips):
                copy(t, 4 + j, (*ch, 1 - c), me).wait_recv()
        for cp in first + passed:
            cp.wait_send()
        for cp in mine:
            cp.wait()


class _Exchange:
    def __init__(self, arrays, out_shapes, plan, n_copies, aliases=None):
        self.arrays = list(arrays)
        self.out_shapes = list(out_shapes)
        self.plan = plan
        self.scratch = [pltpu.SemaphoreType.DMA((n_copies,)), pltpu.SemaphoreType.DMA((n_copies,))]
        self.aliases = aliases or {}

    def _copies(self, ins, outs, scr):
        send_sems, recv_sems = scr
        snd, rcv = [], []
        for i, (src, dst, peer, lands) in enumerate(self.plan(ins, outs, _mesh_pos())):
            kw = dict(send_sem=send_sems.at[i], recv_sem=recv_sems.at[i], device_id=peer, device_id_type=MESH)
            snd.append(pltpu.make_async_remote_copy(src_ref=src, dst_ref=dst, **kw))
            rcv.append(pltpu.make_async_remote_copy(src_ref=src, dst_ref=lands, **kw))
        return snd, rcv

    def start(self, ins, outs, scr):
        for cp in self._copies(ins, outs, scr)[0]:
            cp.start()

    def finish(self, ins, outs, scr):
        snd, rcv = self._copies(ins, outs, scr)
        for cp in rcv:
            cp.wait_recv()
        for cp in snd:
            cp.wait_send()


def _pair_exchange(grads):
    n = len(grads)

    def plan(ins, outs, pos):
        x, y, c = pos
        return [(ins[t].at[2 * j + (1 - c)], outs[t].at[j], (x, y, 1 - c), outs[t].at[j])
                for t in range(n) for j in range(N_CHIP)]

    return _Exchange(grads, [_sds((N_CHIP,) + g.shape[1:], g.dtype) for g in grads], plan, N_CHIP * n)


def _chip_exchange(parts):
    n = len(parts)

    def plan(ins, outs, pos):
        x, y, c = pos
        chips = [(1 - x, y), (x, 1 - y), (1 - x, 1 - y)]
        return [(ins[t].at[2 * ch[0] + ch[1]], outs[t].at[k], (*ch, c), outs[t].at[k])
                for t in range(n) for k, ch in enumerate(chips)]

    return _Exchange(parts, [_sds((3,) + p.shape[1:], p.dtype) for p in parts], plan, 3 * n)


def _pair_gather(bufs):
    n = len(bufs)

    def plan(ins, outs, pos):
        x, y, c = pos
        return [(ins[t].at[c], outs[t].at[c], (x, y, 1 - c), outs[t].at[1 - c]) for t in range(n)]

    return _Exchange(bufs, [_sds(b.shape, b.dtype) for b in bufs], plan, n, aliases={t: t for t in range(n)})


def _run_comm(comm, name):
    n_in, n_out = len(comm.arrays), len(comm.out_shapes)

    def body(*refs):
        ins, outs, scr = refs[:n_in], refs[n_in:n_in + n_out], refs[n_in + n_out:]
        comm.start(ins, outs, scr)
        comm.finish(ins, outs, scr)

    outs = _pc(body, name=name, out_shape=comm.out_shapes, in_specs=[ANY] * n_in, out_specs=[ANY] * n_out,
               input_output_aliases=dict(comm.aliases), scratch_shapes=comm.scratch,
               compiler_params=_params())(*comm.arrays)
    return list(outs)


def _all_gather(arrs, name):
    return _run_comm(_AllGather(arrs), name)


def _call(body, *, name, grid, in_specs, out_specs, out_shape, scratch_shapes, sem, args, comms=()):
    n_in, n_out, n_scr = len(in_specs), len(out_specs), len(scratch_shapes)
    c_in = [len(cm.arrays) for cm in comms]
    c_out = [len(cm.out_shapes) for cm in comms]
    c_scr = [len(cm.scratch) for cm in comms]
    aliases = {}
    for k, cm in enumerate(comms):
        for a, b in cm.aliases.items():
            aliases[n_in + sum(c_in[:k]) + a] = n_out + sum(c_out[:k]) + b

    def split(refs, counts):
        out, pos = [], 0
        for cnt in counts:
            out.append(refs[pos:pos + cnt])
            pos += cnt
        return out

    def wrapped(*refs):
        ins = refs[:n_in + sum(c_in)]
        outs = refs[len(ins):len(ins) + n_out + sum(c_out)]
        scr = refs[len(ins) + len(outs):]
        cins, couts, cscr = split(ins[n_in:], c_in), split(outs[n_out:], c_out), split(scr[n_scr:], c_scr)
        if comms:
            first = pl.program_id(0) == 0
            last = pl.program_id(0) == grid[0] - 1
            for k in range(1, len(grid)):
                first = jnp.logical_and(first, pl.program_id(k) == 0)
                last = jnp.logical_and(last, pl.program_id(k) == grid[k] - 1)

            @pl.when(first)
            def _():
                for k, cm in enumerate(comms):
                    cm.start(cins[k], couts[k], cscr[k])
        body(*ins[:n_in], *outs[:n_out], *scr[:n_scr])
        if comms:
            @pl.when(last)
            def _():
                for k, cm in enumerate(comms):
                    cm.finish(cins[k], couts[k], cscr[k])

    outs = _pc(wrapped, name=name, grid=grid,
               in_specs=list(in_specs) + [ANY] * sum(c_in), out_specs=list(out_specs) + [ANY] * sum(c_out),
               out_shape=list(out_shape) + [s for cm in comms for s in cm.out_shapes],
               scratch_shapes=list(scratch_shapes) + [s for cm in comms for s in cm.scratch],
               input_output_aliases=aliases, compiler_params=_params(*sem),
               )(*args, *[a for cm in comms for a in cm.arrays])
    outs = list(outs)
    return outs[:n_out], split(outs[n_out:], c_out)


def _row_block(r):
    for b in (512, 256, 128, 64, 32, 16, 8):
        if r % b == 0:
            return b
    return r


def _pair_add(g, recv, cj_idx, name):
    _, r, cc = g.shape
    br = _row_block(r)

    def body(cj_ref, g_ref, r_ref, own_ref, pb_ref):
        s = g_ref[...] + r_ref[...]
        pb_ref[...] = s.astype(BF16)

        @pl.when(pl.program_id(1) == cj_ref[1])
        def _():
            own_ref[...] = s[0]

    grid_spec = pltpu.PrefetchScalarGridSpec(
        num_scalar_prefetch=1, grid=(r // br, N_CHIP),
        in_specs=[pl.BlockSpec((1, br, cc), lambda i, j, cj_ref: (2 * j + cj_ref[0], i, 0)),
                  pl.BlockSpec((1, br, cc), lambda i, j, cj_ref: (j, i, 0))],
        out_specs=[pl.BlockSpec((br, cc), lambda i, j, cj_ref: (i, 0)),
                   pl.BlockSpec((1, br, cc), lambda i, j, cj_ref: (j, i, 0))])
    return _pc(body, name=name, grid_spec=grid_spec,
               out_shape=[_sds((r, cc)), _sds((N_CHIP, r, cc), BF16)],
               compiler_params=_params("arbitrary", "arbitrary"))(cj_idx, g, recv)


def _chip_add(p, q, cj_idx, name):
    r, cc = p.shape
    br = _row_block(r)

    def body(cj_ref, p_ref, q_ref, o_ref):
        o_ref[0] = ((p_ref[...] + q_ref[0].astype(F32)) + q_ref[1].astype(F32)) + q_ref[2].astype(F32)

    grid_spec = pltpu.PrefetchScalarGridSpec(
        num_scalar_prefetch=1, grid=(r // br,),
        in_specs=[pl.BlockSpec((br, cc), lambda i, cj_ref: (i, 0)),
                  pl.BlockSpec((3, br, cc), lambda i, cj_ref: (0, i, 0))],
        out_specs=pl.BlockSpec((1, br, cc), lambda i, cj_ref: (cj_ref[0], i, 0)))
    return _pc(body, name=name, grid_spec=grid_spec, out_shape=_sds((2, r, cc)),
               compiler_params=_params("arbitrary"))(cj_idx, p, q)


def _shard_of(both):
    return both.reshape((2 * both.shape[1],) + both.shape[2:])


def _adamw(w, g, m, v, name):
    r, cc = w.shape
    br = _row_block(r)
    if r * cc * 4 <= (1 << 20):
        br = r
    elif br * cc * 4 > (1 << 20) and br > 8:
        br = max(8, (1 << 20) // (cc * 4) // 8 * 8)
        while r % br:
            br -= 8
    c1 = 1.0 - ADAM_B1 ** ADAM_STEP
    c2 = 1.0 - ADAM_B2 ** ADAM_STEP

    def body(w_ref, g_ref, m_ref, v_ref, go_ref, d_ref, mo_ref, vo_ref):
        gg = g_ref[...]
        go_ref[...] = gg
        mn = ADAM_B1 * m_ref[...] + (1.0 - ADAM_B1) * gg
        vn = ADAM_B2 * v_ref[...] + (1.0 - ADAM_B2) * (gg * gg)
        mh = mn / c1
        vh = vn / c2
        d_ref[...] = -ADAM_LR * (mh / (jnp.sqrt(vh) + ADAM_EPS) + ADAM_WD * w_ref[...])
        mo_ref[...] = mn
        vo_ref[...] = vn

    spec = pl.BlockSpec((br, cc), lambda i: (i, 0))
    return _pc(body, name=name, grid=(r // br,), in_specs=[spec] * 4, out_specs=[spec] * 4,
               out_shape=[_sds((r, cc))] * 4, compiler_params=_params("arbitrary"))(w, g, m, v)


def _head(w_half, c8, small, w_ada, b_shard, c_ctx, ret_decay):
    ada_n = w_ada.shape[1]
    mod_sds = _sds((16, ada_n))
    ag_w, ag_c, ag_m = _AllGather([w_half]), _AllGather([c8, small]), _AllGather([mod_sds])
    n_w, n_c, n_m = len(ag_w.scratch), len(ag_c.scratch), len(ag_m.scratch)

    def body(w_ref, c_ref, s_ref, wada_ref, b_ref, cc_ref, rd_ref,
             gw_ref, call_ref, sall_ref, a_ref, modp_ref, mall_ref, lg_ref, sg_ref, *scr):
        scr_w, scr_c, scr_m = scr[:n_w], scr[n_w:n_w + n_c], scr[n_w + n_c:n_w + n_c + n_m]
        c_v, w_v, m_v, sems = scr[n_w + n_c + n_m:]
        ag_w.start((w_ref,), (gw_ref,), scr_w)
        ag_c.start((c_ref, s_ref), (call_ref, sall_ref), scr_c)
        load_w = pltpu.make_async_copy(wada_ref, w_v, sems.at[0])
        load_w.start()
        rd = rd_ref[...]
        lg_ref[...] = -_softplus(-rd)
        sg_ref[...] = _sigmoid(-rd)
        ag_c.finish((c_ref, s_ref), (call_ref, sall_ref), scr_c)
        load_c = pltpu.make_async_copy(call_ref, c_v, sems.at[1])
        load_c.start()
        load_c.wait()
        a_ref[...] = jnp.zeros_like(a_ref)
        for d in range(N_DEV):
            cd = c_v[d, 0:1, :]
            a_ref[d:d + 1, :] = cd * _sigmoid(cd)
        cc = cc_ref[...]
        a_ref[N_DEV:N_DEV + 1, :] = cc * _sigmoid(cc)
        load_w.wait()
        m_v[...] = jnp.dot(a_ref[...], w_v[...], preferred_element_type=F32,
                           precision=lax.Precision.HIGHEST) + b_ref[...]
        put = pltpu.make_async_copy(m_v, modp_ref, sems.at[2])
        put.start()
        put.wait()
        ag_m.start((modp_ref,), (mall_ref,), scr_m)
        ag_m.finish((modp_ref,), (mall_ref,), scr_m)
        ag_w.finish((w_ref,), (gw_ref,), scr_w)

    rd = jnp.broadcast_to(ret_decay.reshape(2, HEADS).T[:, :, None], (HEADS, 2, LANES))
    lane = _full((HEADS, 2, LANES))
    outs = _pc(
        body, name="head",
        in_specs=[ANY, ANY, ANY, ANY, _full((1, ada_n)), _full((1, D_MODEL)), lane],
        out_specs=[ANY, ANY, ANY, _full((16, D_MODEL)), ANY, ANY, lane, lane],
        out_shape=ag_w.out_shapes + ag_c.out_shapes + [_sds((16, D_MODEL)), mod_sds] + ag_m.out_shapes
        + [_sds((HEADS, 2, LANES))] * 2,
        scratch_shapes=ag_w.scratch + ag_c.scratch + ag_m.scratch
        + [pltpu.VMEM((N_DEV,) + c8.shape, F32), pltpu.VMEM(w_ada.shape, F32), pltpu.VMEM((16, ada_n), F32),
           pltpu.SemaphoreType.DMA((3,))],
        compiler_params=_params(),
    )(w_half, c8, small, w_ada, b_shard, c_ctx.reshape(1, D_MODEL), rd)
    gw, c_all, small_all, a16, _, mod_all, lgv, sgv = outs
    return gw, c_all, small_all, a16, mod_all, lgv, sgv


def _ada_grad(at, b):
    n = b.shape[1]
    bn = 512

    def body(a_ref, b_ref, o_ref):
        o_ref[...] = jnp.dot(a_ref[...], b_ref[...], preferred_element_type=F32, precision=lax.Precision.HIGHEST)

    return _pc(body, name="ada_grad", grid=(n // bn,),
               in_specs=[_full((D_MODEL, LANES)), pl.BlockSpec((LANES, bn), lambda i: (0, i))],
               out_specs=pl.BlockSpec((D_MODEL, bn), lambda i: (0, i)), out_shape=_sds((D_MODEL, n)),
               compiler_params=_params("arbitrary"))(at, b)


def _cctx_partial(dmc8, w_ada):
    n = w_ada.shape[1]
    bn = 512

    def body(d_ref, w_ref, o_ref):
        @pl.when(pl.program_id(0) == 0)
        def _():
            o_ref[...] = jnp.zeros_like(o_ref)
        o_ref[...] += lax.dot_general(d_ref[...], w_ref[...], (((1,), (1,)), ((), ())),
                                      preferred_element_type=F32, precision=lax.Precision.HIGHEST)

    return _pc(body, name="cctx_partial", grid=(n // bn,),
               in_specs=[pl.BlockSpec((8, bn), lambda i: (0, i)), pl.BlockSpec((D_MODEL, bn), lambda i: (0, i))],
               out_specs=_full((8, D_MODEL)), out_shape=_sds((8, D_MODEL)),
               compiler_params=_params("arbitrary"))(dmc8, w_ada)


def _cctx_final(parts, c_ctx, m, v):
    c1 = 1.0 - ADAM_B1 ** ADAM_STEP
    c2 = 1.0 - ADAM_B2 ** ADAM_STEP

    def body(p_ref, c_ref, m_ref, v_ref, g_ref, d_ref, mo_ref, vo_ref):
        s = ((p_ref[0, 0:1, :] + p_ref[2, 0:1, :]) + p_ref[4, 0:1, :]) + p_ref[6, 0:1, :]
        z = c_ref[...]
        sg = _sigmoid(z)
        gg = s * (sg * (1.0 + z * (1.0 - sg)))
        g_ref[...] = gg
        mn = ADAM_B1 * m_ref[...] + (1.0 - ADAM_B1) * gg
        vn = ADAM_B2 * v_ref[...] + (1.0 - ADAM_B2) * (gg * gg)
        d_ref[...] = -ADAM_LR * ((mn / c1) / (jnp.sqrt(vn / c2) + ADAM_EPS) + ADAM_WD * z)
        mo_ref[...] = mn
        vo_ref[...] = vn

    row = _full((1, D_MODEL))
    return _pc(body, name="cctx_final", out_shape=[_sds((1, D_MODEL))] * 4,
               in_specs=[_full(parts.shape), row, row, row], out_specs=[row] * 4,
               compiler_params=_params())(parts, c_ctx.reshape(1, D_MODEL), m.reshape(1, D_MODEL), v.reshape(1, D_MODEL))


def _rotary_tables(t_len):
    rows = t_len // GRID_W
    row = jnp.repeat(jnp.arange(rows, dtype=F32), GRID_W)
    col = jnp.tile(jnp.arange(GRID_W, dtype=F32), rows)
    n_freq = DH // 4
    inv = ROPE_BASE ** (-jnp.arange(n_freq, dtype=F32) / n_freq)
    ang = jnp.concatenate([row[:, None] * inv, col[:, None] * inv], axis=-1)
    cos, sin = jnp.cos(ang), jnp.sin(ang)
    return jnp.concatenate([cos, cos], axis=-1), jnp.concatenate([-sin, sin], axis=-1)


def _inproj_fwd(x, gn, sh, sc, w4, cos2, sin2, name, comms=()):
    t = x.shape[0]
    tm = _tile(t, True)
    nc = IN_COLS // N_CHIP

    def body(x_ref, gn_ref, sh_ref, sc_ref, w_ref, c_ref, s_ref, p_ref, xr_ref, hb_ref, p_s):
        xh, _ = _rms(x_ref[...])
        h = xh * gn_ref[...] * (1.0 + sc_ref[...]) + sh_ref[...]
        hb = h.astype(BF16)
        hb_ref[...] = hb
        for j in range(N_CHIP):
            p_s[:, nc * j:nc * (j + 1)] = _dot(hb, w_ref[j])
        cc = c_ref[...]
        ss = s_ref[...]
        for hh in range(2 * HEADS):
            blk = p_s[:, DH * hh:DH * (hh + 1)]
            rot = blk * cc + pltpu.roll(blk, DH // 2, 1) * ss
            if hh >= HEADS:
                rot = rot * K_SCALE
            p_ref[:, DH * hh:DH * (hh + 1)] = rot.astype(BF16)
        p_ref[:, 2 * RET_W:] = p_s[:, 2 * RET_W:].astype(BF16)
        xr_ref[...] = p_s[:, 4 * RET_W:4 * RET_W + LRU_W]

    row = _full((1, D_MODEL))
    outs, couts = _call(
        body, name=name, grid=(t // tm,),
        in_specs=[pl.BlockSpec((tm, D_MODEL), lambda i: (i, 0)), row, row, row, _full(w4.shape),
                  pl.BlockSpec((tm, DH), lambda i: (i, 0)), pl.BlockSpec((tm, DH), lambda i: (i, 0))],
        out_specs=[pl.BlockSpec((tm, IN_COLS), lambda i: (i, 0)), pl.BlockSpec((tm, LRU_W), lambda i: (i, 0)),
                   pl.BlockSpec((tm, D_MODEL), lambda i: (i, 0))],
        out_shape=[_sds((t, IN_COLS), BF16), _sds((t, LRU_W)), _sds((t, D_MODEL), BF16)],
        scratch_shapes=[pltpu.VMEM((tm, IN_COLS), F32)], sem=("arbitrary",),
        args=(x, gn, sh, sc, w4, cos2, sin2), comms=comms)
    return (outs, couts) if comms else outs


def _inproj_bwd(x, gn, sh, sc, w4, cos2, sin2, pieces, dres, name):
    t = x.shape[0]
    tm = _tile(t)
    nc = IN_COLS // N_CHIP

    def body(x_ref, gn_ref, sh_ref, sc_ref, w_ref, c_ref, s_ref, dqf, dqb, dkf, dkb, dvf, dvb, dg, dxr, dgt, dres_ref,
             dx_ref, dpb_ref, dgn_ref, dsh_ref, dsc_ref):
        cc = c_ref[...]
        ss = s_ref[...]
        dq = dqf[...].astype(F32) + dqb[...].astype(F32)
        dk = dkf[...].astype(F32) + dkb[...].astype(F32)
        for hh in range(HEADS):
            sl = slice(DH * hh, DH * (hh + 1))
            b = dq[:, sl]
            dpb_ref[:, sl] = (b * cc + pltpu.roll(b * ss, DH // 2, 1)).astype(BF16)
            b = dk[:, sl]
            dpb_ref[:, RET_W + DH * hh:RET_W + DH * (hh + 1)] = (
                (b * cc + pltpu.roll(b * ss, DH // 2, 1)) * K_SCALE).astype(BF16)
        dpb_ref[:, 2 * RET_W:3 * RET_W] = (dvf[...].astype(F32) + dvb[...].astype(F32)).astype(BF16)
        dpb_ref[:, 3 * RET_W:4 * RET_W] = dg[...].astype(BF16)
        dpb_ref[:, 4 * RET_W:4 * RET_W + LRU_W] = dxr[...].astype(BF16)
        dpb_ref[:, 4 * RET_W + LRU_W:IN_COLS] = dgt[...].astype(BF16)
        dh = _dot_nt(dpb_ref[:, 0:nc], w_ref[0])
        for j in range(1, N_CHIP):
            dh = dh + _dot_nt(dpb_ref[:, nc * j:nc * (j + 1)], w_ref[j])
        dx, dgn_t, dsh_t, dsc_t = _norm_mod_bwd(x_ref[...], gn_ref[...], sc_ref[...], dh)
        dx_ref[...] = dres_ref[...] + dx

        @pl.when(pl.program_id(0) == 0)
        def _():
            dgn_ref[...] = jnp.zeros_like(dgn_ref)
            dsh_ref[...] = jnp.zeros_like(dsh_ref)
            dsc_ref[...] = jnp.zeros_like(dsc_ref)
        dgn_ref[...] += dgn_t
        dsh_ref[...] += dsh_t
        dsc_ref[...] += dsc_t

    row = _full((1, D_MODEL))
    pc = pl.BlockSpec((tm, RET_W), lambda i: (i, 0))
    big = pl.BlockSpec((tm, D_MODEL), lambda i: (i, 0))
    return _pc(body, name=name, grid=(t // tm,),
               in_specs=[big, row, row, row, _full(w4.shape),
                         pl.BlockSpec((tm, DH), lambda i: (i, 0)), pl.BlockSpec((tm, DH), lambda i: (i, 0))]
               + [pc] * 9 + [big],
               out_specs=[big, pl.BlockSpec((tm, IN_COLS), lambda i: (i, 0)), row, row, row],
               out_shape=[_sds((t, D_MODEL)), _sds((t, IN_COLS), BF16), _sds((1, D_MODEL)), _sds((1, D_MODEL)),
                          _sds((1, D_MODEL))],
               compiler_params=_params("arbitrary"))(x, gn, sh, sc, w4, cos2, sin2, *pieces, dres)


def _halo_specs(t, tm):
    n8 = tm // SUBLANES
    last8 = t // SUBLANES - 1
    prev = pl.BlockSpec((SUBLANES, LRU_W), lambda i: (jnp.maximum(i * n8 - 1, 0), 0))
    main = pl.BlockSpec((tm, LRU_W), lambda i: (i, 0))
    nxt = pl.BlockSpec((SUBLANES, LRU_W), lambda i: (jnp.minimum((i + 1) * n8, last8), 0))
    return prev, main, nxt


def _with_halo(prev_ref, main_ref, next_ref, i, nt):
    prev = jnp.where(i > 0, prev_ref[...], 0.0)
    nxt = jnp.where(i < nt - 1, next_ref[...], 0.0)
    return jnp.concatenate([prev, main_ref[...], nxt], axis=0)


def _conv_fwd(xr, cw, cb, name):
    t = xr.shape[0]
    tm = _tile(t, True)
    nt = t // tm
    n = tm + 2 * SUBLANES
    mid = slice(SUBLANES, SUBLANES + tm)

    def body(p_ref, m_ref, n_ref, w_ref, b_ref, o_ref):
        xp = _with_halo(p_ref, m_ref, n_ref, pl.program_id(0), nt)
        acc = b_ref[...] + pltpu.roll(xp, 1, 0)[mid] * w_ref[0:1, :]
        acc = acc + xp[mid] * w_ref[1:2, :]
        acc = acc + pltpu.roll(xp, n - 1, 0)[mid] * w_ref[2:3, :]
        acc = acc + pltpu.roll(xp, n - 2, 0)[mid] * w_ref[3:4, :]
        o_ref[...] = acc

    return _pc(body, name=name, grid=(nt,),
               in_specs=[*_halo_specs(t, tm), _full((4, LRU_W)), _full((1, LRU_W))],
               out_specs=pl.BlockSpec((tm, LRU_W), lambda i: (i, 0)), out_shape=_sds((t, LRU_W)),
               compiler_params=_params("arbitrary"))(xr, xr, xr, cw, cb)


def _conv_bwd(dxc_a, dxc_b, xr, cw, name):
    t = xr.shape[0]
    tm = _tile(t, True)
    nt = t // tm
    n = tm + 2 * SUBLANES
    mid = slice(SUBLANES, SUBLANES + tm)

    def body(ap_ref, am_ref, an_ref, bp_ref, bm_ref, bn_ref, xp_ref, xm_ref, xn_ref, w_ref, dx_ref, dw_ref, db_ref):
        i = pl.program_id(0)
        dp = _with_halo(ap_ref, am_ref, an_ref, i, nt) + _with_halo(bp_ref, bm_ref, bn_ref, i, nt)
        xp = _with_halo(xp_ref, xm_ref, xn_ref, i, nt)
        dx = pltpu.roll(dp, n - 1, 0)[mid] * w_ref[0:1, :]
        dx = dx + dp[mid] * w_ref[1:2, :]
        dx = dx + pltpu.roll(dp, 1, 0)[mid] * w_ref[2:3, :]
        dx = dx + pltpu.roll(dp, 2, 0)[mid] * w_ref[3:4, :]
        dx_ref[...] = dx.astype(BF16)
        d = dp[mid]

        @pl.when(i == 0)
        def _():
            dw_ref[...] = jnp.zeros_like(dw_ref)
            db_ref[...] = jnp.zeros_like(db_ref)
        dw_ref[0:1, :] += _sum0(d * pltpu.roll(xp, 1, 0)[mid])
        dw_ref[1:2, :] += _sum0(d * xp[mid])
        dw_ref[2:3, :] += _sum0(d * pltpu.roll(xp, n - 1, 0)[mid])
        dw_ref[3:4, :] += _sum0(d * pltpu.roll(xp, n - 2, 0)[mid])
        db_ref[...] += _sum0(d)

    return _pc(body, name=name, grid=(nt,),
               in_specs=[*_halo_specs(t, tm), *_halo_specs(t, tm), *_halo_specs(t, tm), _full((4, LRU_W))],
               out_specs=[pl.BlockSpec((tm, LRU_W), lambda i: (i, 0)), _full((4, LRU_W)), _full((1, LRU_W))],
               out_shape=[_sds((t, LRU_W), BF16), _sds((4, LRU_W)), _sds((1, LRU_W))],
               compiler_params=_params("arbitrary"))(dxc_a, dxc_a, dxc_a, dxc_b, dxc_b, dxc_b, xr, xr, xr, cw)


def _local_scan(a, b, reverse):
    n = a.shape[0]
    row = lax.broadcasted_iota(jnp.int32, a.shape, 0) & (SUBLANES - 1)
    for s in (1, 2, 4):
        if reverse:
            a_s, b_s, ok = pltpu.roll(a, n - s, 0), pltpu.roll(b, n - s, 0), row < SUBLANES - s
        else:
            a_s, b_s, ok = pltpu.roll(a, s, 0), pltpu.roll(b, s, 0), row >= s
        b = a * jnp.where(ok, b_s, 0.0) + b
        a = a * jnp.where(ok, a_s, 1.0)
    return a, b


def _carry_scan(a_s, b_s, out_ref, carry, reverse):
    ng = a_s.shape[0] // SUBLANES
    shape = carry.shape

    def step(g, cr):
        gg = (ng - 1 - g) if reverse else g
        off = pl.multiple_of(gg * SUBLANES, SUBLANES)
        h = a_s[pl.ds(off, SUBLANES), :] * cr + b_s[pl.ds(off, SUBLANES), :]
        out_ref[pl.ds(off, SUBLANES), :] = h
        edge = h[0:1, :] if reverse else h[SUBLANES - 1:SUBLANES, :]
        return jnp.broadcast_to(edge, shape)

    return lax.fori_loop(0, ng, step, carry)


def _lru_gates(xc, wa_ref, wx_ref, ba, bx, lam):
    xb = xc.astype(BF16)
    r = _sigmoid(_dot(xb, wa_ref[...]) + ba)
    ig = _sigmoid(_dot(xb, wx_ref[...]) + bx)
    sp = _softplus(-lam)
    la = -LRU_C * r * sp
    a = jnp.exp(la)
    mult = jnp.sqrt(_one_minus_sq(la, a))
    return r, ig, sp, a, mult


def _lru_fwd(xc, wa, wx, ba, bx, lam, h0, reverse, name, comms=()):
    t = xc.shape[0]
    tm = _tile(t, True)
    nt = t // tm
    tidx = (lambda i: (nt - 1 - i, 0)) if reverse else (lambda i: (i, 0))

    def body(x_ref, wa_ref, wx_ref, ba_ref, bx_ref, lam_ref, h0_ref, h_ref, a_s, b_s, c_s):
        @pl.when(pl.program_id(0) == 0)
        def _():
            c_s[...] = jnp.broadcast_to(h0_ref[...], c_s.shape)
        xv = x_ref[...]
        _, ig, _, a, mult = _lru_gates(xv, wa_ref, wx_ref, ba_ref[...], bx_ref[...], lam_ref[...])
        al, bl = _local_scan(a, mult * (ig * xv), reverse)
        a_s[...] = al
        b_s[...] = bl
        c_s[...] = _carry_scan(a_s, b_s, h_ref, c_s[...], reverse)

    vec = _full((1, LRU_W))
    mat = _full((LRU_W, LRU_W))
    (h,), couts = _call(body, name=name, grid=(nt,),
                        in_specs=[pl.BlockSpec((tm, LRU_W), tidx), mat, mat, vec, vec, vec, vec],
                        out_specs=[pl.BlockSpec((tm, LRU_W), tidx)], out_shape=[_sds((t, LRU_W))],
                        scratch_shapes=[pltpu.VMEM((tm, LRU_W), F32), pltpu.VMEM((tm, LRU_W), F32),
                                        pltpu.VMEM((SUBLANES, LRU_W), F32)],
                        sem=("arbitrary",), args=(xc, wa, wx, ba, bx, lam, h0), comms=comms)
    return (h, couts) if comms else h


def _lru_bwd(xc, wa, wx, ba, bx, lam, h, h0, dh, reverse, name, comms=()):
    t = xc.shape[0]
    tm = _tile(t, True)
    nt = t // tm
    n8 = tm // SUBLANES
    last8 = t // SUBLANES - 1
    tidx = (lambda i: (i, 0)) if reverse else (lambda i: (nt - 1 - i, 0))
    if reverse:
        halo = pl.BlockSpec((SUBLANES, LRU_W), lambda i: (jnp.minimum((i + 1) * n8, last8), 0))
    else:
        halo = pl.BlockSpec((SUBLANES, LRU_W), lambda i: (jnp.maximum((nt - 1 - i) * n8 - 1, 0), 0))

    def body(x_ref, wa_ref, wx_ref, ba_ref, bx_ref, lam_ref, h_ref, halo_ref, h0_ref, dh_ref,
             dx_ref, dpre_ref, dba_ref, dbx_ref, dlam_ref, dh0_ref, a_s, b_s, l_s, c_s, e_s):
        i = pl.program_id(0)

        @pl.when(i == 0)
        def _():
            c_s[...] = jnp.zeros_like(c_s)
            e_s[...] = jnp.zeros_like(e_s)
            dba_ref[...] = jnp.zeros_like(dba_ref)
            dbx_ref[...] = jnp.zeros_like(dbx_ref)
            dlam_ref[...] = jnp.zeros_like(dlam_ref)
        xv = x_ref[...]
        lam = lam_ref[...]
        r, ig, sp, a, mult = _lru_gates(xv, wa_ref, wx_ref, ba_ref[...], bx_ref[...], lam)
        hv = h_ref[...]
        rowi = lax.broadcasted_iota(jnp.int32, (tm, LRU_W), 0)
        edge_a = jnp.broadcast_to(e_s[0:1, :], (tm, LRU_W))
        h0b = jnp.broadcast_to(h0_ref[...], (tm, LRU_W))
        if reverse:
            a_sh = jnp.where(rowi == 0, edge_a, pltpu.roll(a, 1, 0))
            hin_edge = jnp.where(i == nt - 1, h0b, jnp.broadcast_to(halo_ref[0:1, :], (tm, LRU_W)))
            h_in = jnp.where(rowi == tm - 1, hin_edge, pltpu.roll(hv, tm - 1, 0))
        else:
            a_sh = jnp.where(rowi == tm - 1, edge_a, pltpu.roll(a, tm - 1, 0))
            hin_edge = jnp.where(i == nt - 1, h0b, jnp.broadcast_to(halo_ref[SUBLANES - 1:SUBLANES, :], (tm, LRU_W)))
            h_in = jnp.where(rowi == 0, hin_edge, pltpu.roll(hv, 1, 0))
        al, bl = _local_scan(a_sh, dh_ref[...], not reverse)
        a_s[...] = al
        b_s[...] = bl
        c_s[...] = _carry_scan(a_s, b_s, l_s, c_s[...], not reverse)
        e_s[...] = jnp.broadcast_to(a[tm - 1:tm, :] if reverse else a[0:1, :], e_s.shape)
        lmb = l_s[...]
        da = lmb * h_in
        ixc = ig * xv
        dmult = lmb * ixc
        dixc = lmb * mult
        dla = da * a - dmult * (a * a) / mult
        dpr = dla * (-LRU_C * sp) * r * (1.0 - r)
        dpi = dixc * xv * ig * (1.0 - ig)
        dprb = dpr.astype(BF16)
        dpib = dpi.astype(BF16)
        dpre_ref[:, 0:LRU_W] = dprb
        dpre_ref[:, LRU_W:2 * LRU_W] = dpib
        dx_ref[...] = dixc * ig + _dot_nt(dprb, wa_ref[...]) + _dot_nt(dpib, wx_ref[...])
        dba_ref[...] += _sum0(dpr)
        dbx_ref[...] += _sum0(dpi)
        dlam_ref[...] += _sum0(dla * (-LRU_C * r)) * (-_sigmoid(-lam))

        @pl.when(i == nt - 1)
        def _():
            al0 = a * lmb
            dh0_ref[...] = al0[tm - 1:tm, :] if reverse else al0[0:1, :]

    vec = _full((1, LRU_W))
    mat = _full((LRU_W, LRU_W))
    tile = pl.BlockSpec((tm, LRU_W), tidx)
    return _call(body, name=name, grid=(nt,),
                 in_specs=[tile, mat, mat, vec, vec, vec, tile, halo, vec, tile],
                 out_specs=[tile, pl.BlockSpec((tm, 2 * LRU_W), tidx), vec, vec, vec, vec],
                 out_shape=[_sds((t, LRU_W)), _sds((t, 2 * LRU_W), BF16), _sds((1, LRU_W)), _sds((1, LRU_W)),
                            _sds((1, LRU_W)), _sds((1, LRU_W))],
                 scratch_shapes=[pltpu.VMEM((tm, LRU_W), F32), pltpu.VMEM((tm, LRU_W), F32),
                                 pltpu.VMEM((tm, LRU_W), F32), pltpu.VMEM((SUBLANES, LRU_W), F32),
                                 pltpu.VMEM((SUBLANES, LRU_W), F32)],
                 sem=("arbitrary",), args=(xc, wa, wx, ba, bx, lam, h, h, h0, dh), comms=comms)


def _decay_tables(lg, reverse):
    ci = lax.broadcasted_iota(jnp.int32, (CHUNK, CHUNK), 0).astype(F32)
    mi = lax.broadcasted_iota(jnp.int32, (CHUNK, CHUNK), 1).astype(F32)
    rel = (mi - ci) if reverse else (ci - mi)
    relc = jnp.maximum(rel, 0.0)
    lg_c = jnp.concatenate([lg] * (CHUNK // LANES), axis=1)
    dm = jnp.where(rel >= 0, jnp.exp(lg_c * relc), 0.0)
    cd = lax.broadcasted_iota(jnp.int32, (CHUNK, DH), 0).astype(F32)
    pq, ps = (CHUNK - cd, cd) if reverse else (cd + 1.0, CHUNK - 1.0 - cd)
    return relc, dm, jnp.exp(lg * pq), jnp.exp(lg * ps), jnp.exp(lg * float(CHUNK)), pq, ps


def _ret_fwd(proj, lgv, s0f, s0b, comms=()):
    t = proj.shape[0]
    n = t // CHUNK

    def one(q, k, v, lg, s_s, hh, o_ref, sp_ref, reverse):
        _, dm, wq, ws, g, _, _ = _decay_tables(lg, reverse)
        vb = v.astype(BF16)
        p = _dot_nt(q.astype(BF16), k.astype(BF16)) * dm
        s = s_s[hh]
        sp_ref[hh, 0] = s
        o_ref[:, DH * hh:DH * (hh + 1)] = _dot(p.astype(BF16), vb) + _dot((q * wq).astype(BF16), s.astype(BF16))
        s_s[hh] = g * s + _dot_tn((k * ws).astype(BF16), vb)

    def body(qf, kf, vf, qb, kb, vb, lg_ref, s0f_ref, s0b_ref, of_ref, ob_ref, spf_ref, spb_ref, sf_s, sb_s):
        @pl.when(pl.program_id(0) == 0)
        def _():
            sf_s[...] = s0f_ref[...]
            sb_s[...] = s0b_ref[...]
        for hh in range(HEADS):
            sl = slice(DH * hh, DH * (hh + 1))
            one(qf[:, sl].astype(F32), kf[:, sl].astype(F32), vf[:, sl], lg_ref[hh, 0:1, :], sf_s, hh, of_ref, spf_ref,
                False)
            one(qb[:, sl].astype(F32), kb[:, sl].astype(F32), vb[:, sl], lg_ref[hh, 1:2, :], sb_s, hh, ob_ref, spb_ref,
                True)

    blk = (CHUNK, RET_W)
    fw = [pl.BlockSpec(blk, lambda i, o=o: (i, o)) for o in range(3)]
    bw = [pl.BlockSpec(blk, lambda i, o=o: (n - 1 - i, o)) for o in range(3)]
    st = _full((HEADS, DH, DH))
    return _call(body, name="ret_fwd", grid=(n,),
                 in_specs=fw + bw + [_full((HEADS, 2, LANES)), st, st],
                 out_specs=[pl.BlockSpec(blk, lambda i: (i, 0)), pl.BlockSpec(blk, lambda i: (n - 1 - i, 0)),
                            pl.BlockSpec((HEADS, 1, DH, DH), lambda i: (0, i, 0, 0)),
                            pl.BlockSpec((HEADS, 1, DH, DH), lambda i: (0, n - 1 - i, 0, 0))],
                 out_shape=[_sds((t, RET_W)), _sds((t, RET_W)), _sds((HEADS, n, DH, DH)), _sds((HEADS, n, DH, DH))],
                 scratch_shapes=[pltpu.VMEM((HEADS, DH, DH), F32), pltpu.VMEM((HEADS, DH, DH), F32)],
                 sem=("arbitrary",), args=(proj, proj, proj, proj, proj, proj, lgv, s0f, s0b), comms=comms)


def _ret_bwd(proj, lgv, sgv, sprev, do, reverse, name, comms=()):
    t = proj.shape[0]
    n = t // CHUNK
    d = 1 if reverse else 0
    cidx = (lambda i: i) if reverse else (lambda i: n - 1 - i)

    def body(q_ref, k_ref, v_ref, lg_ref, sg_ref, s_ref, do_ref, dq_ref, dk_ref, dv_ref, ds0_ref, drd_ref, ds_s, acc_s):
        i = pl.program_id(0)

        @pl.when(i == 0)
        def _():
            ds_s[...] = jnp.zeros_like(ds_s)
            acc_s[...] = jnp.zeros_like(acc_s)
        for hh in range(HEADS):
            sl = slice(DH * hh, DH * (hh + 1))
            relc, dm, wq, ws, g, pq, ps = _decay_tables(lg_ref[hh, d:d + 1, :], reverse)
            qb, kb, vb = q_ref[:, sl], k_ref[:, sl], v_ref[:, sl]
            q, k = qb.astype(F32), kb.astype(F32)
            p = _dot_nt(qb, kb) * dm
            s = s_ref[hh, 0]
            dob = do_ref[:, sl].astype(BF16)
            dsn = ds_s[hh]
            dsb = dsn.astype(BF16)
            dv_ref[:, sl] = (_dot_tn(p.astype(BF16), dob) + _dot((k * ws).astype(BF16), dsb)).astype(BF16)
            dp = _dot_nt(dob, vb)
            dab = (dp * dm).astype(BF16)
            xq = _dot_nt(dob, s.astype(BF16))
            yk = _dot_nt(vb, dsb)
            dq_ref[:, sl] = (_dot(dab, kb) + xq * wq).astype(BF16)
            dk_ref[:, sl] = (_dot_tn(dab, qb) + yk * ws).astype(BF16)
            ds_s[hh] = g * dsn + _dot_tn((q * wq).astype(BF16), dob)
            s_mask = _sum0(dp * p * relc)
            part = (sum(s_mask[:, LANES * u:LANES * (u + 1)] for u in range(CHUNK // LANES))
                    + _sum0(xq * q * wq * pq) + _sum0(yk * k * ws * ps) + _sum0(dsn * s) * g * float(CHUNK))
            acc_s[hh] += jnp.broadcast_to(part, (SUBLANES, LANES))

        @pl.when(i == n - 1)
        def _():
            ds0_ref[...] = ds_s[...]
            for hh in range(HEADS):
                tot = jnp.sum(acc_s[hh, 0:1, :], axis=1, keepdims=True)
                drd_ref[hh] = jnp.broadcast_to(tot, (SUBLANES, LANES)) * sg_ref[hh, d:d + 1, :]

    blk = (CHUNK, RET_W)
    qkv = [pl.BlockSpec(blk, lambda i, o=o: (cidx(i), o)) for o in range(3)]
    hc = pl.BlockSpec(blk, lambda i: (cidx(i), 0))
    lane = _full((HEADS, 2, LANES))
    return _call(body, name=name, grid=(n,),
                 in_specs=qkv + [lane, lane, pl.BlockSpec((HEADS, 1, DH, DH), lambda i: (0, cidx(i), 0, 0)), hc],
                 out_specs=[hc, hc, hc, _full((HEADS, DH, DH)), _full((HEADS, SUBLANES, LANES))],
                 out_shape=[_sds((t, RET_W), BF16)] * 3 + [_sds((HEADS, DH, DH)), _sds((HEADS, SUBLANES, LANES))],
                 scratch_shapes=[pltpu.VMEM((HEADS, DH, DH), F32), pltpu.VMEM((HEADS, SUBLANES, LANES), F32)],
                 sem=("arbitrary",), args=(proj, proj, proj, lgv, sgv, sprev, do), comms=comms)


def _ctx_weights(lg, l_len, reverse):
    pos = lax.broadcasted_iota(jnp.int32, (l_len, DH), 0).astype(F32)
    steps = pos if reverse else (l_len - 1.0 - pos)
    return jnp.exp(lg * steps), steps


def _ctx_state_fwd(projc, lgv):
    l_len = projc.shape[0]

    def body(k_ref, v_ref, lg_ref, sf_ref, sb_ref):
        k = k_ref[...]
        vb = v_ref[...].astype(BF16)
        for d, o_ref in ((0, sf_ref), (1, sb_ref)):
            w, _ = _ctx_weights(lg_ref[0, d:d + 1, :], l_len, d == 1)
            o_ref[0] = _dot_tn((k * w).astype(BF16), vb)

    st = pl.BlockSpec((1, DH, DH), lambda h: (h, 0, 0))
    return _pc(body, name="ctx_state_fwd", grid=(HEADS,),
               in_specs=[pl.BlockSpec((l_len, DH), lambda h: (0, HEADS + h)),
                         pl.BlockSpec((l_len, DH), lambda h: (0, 2 * HEADS + h)),
                         pl.BlockSpec((1, 2, LANES), lambda h: (h, 0, 0))],
               out_specs=[st, st], out_shape=[_sds((HEADS, DH, DH))] * 2,
               compiler_params=_params("arbitrary"))(projc, projc, lgv)


def _ctx_state_bwd(projc, lgv, sgv, dsf, dsb):
    l_len = projc.shape[0]

    def body(k_ref, v_ref, lg_ref, sg_ref, dsf_ref, dsb_ref, dk_ref, dv_ref, drd_ref):
        k = k_ref[...]
        vb = v_ref[...].astype(BF16)
        dk = jnp.zeros((l_len, DH), F32)
        dv = jnp.zeros((l_len, DH), F32)
        rows = []
        for d, ds_ref in ((0, dsf_ref), (1, dsb_ref)):
            w, steps = _ctx_weights(lg_ref[0, d:d + 1, :], l_len, d == 1)
            dsb16 = ds_ref[0].astype(BF16)
            dkw = _dot_nt(vb, dsb16)
            dk = dk + dkw * w
            dv = dv + _dot((k * w).astype(BF16), dsb16)
            tot = jnp.sum(_sum0(dkw * k * w * steps), axis=1, keepdims=True)
            rows.append(jnp.broadcast_to(tot, (1, LANES)) * sg_ref[0, d:d + 1, :])
        dk_ref[...] = dk.astype(BF16)
        dv_ref[...] = dv.astype(BF16)
        rid = lax.broadcasted_iota(jnp.int32, (SUBLANES, LANES), 0)
        drd_ref[0] = jnp.where(rid == 0, rows[0], jnp.where(rid == 1, rows[1], 0.0))

    st = pl.BlockSpec((1, DH, DH), lambda h: (h, 0, 0))
    lane = pl.BlockSpec((1, 2, LANES), lambda h: (h, 0, 0))
    hc = pl.BlockSpec((l_len, DH), lambda h: (0, h))
    return _pc(body, name="ctx_state_bwd", grid=(HEADS,),
               in_specs=[pl.BlockSpec((l_len, DH), lambda h: (0, HEADS + h)),
                         pl.BlockSpec((l_len, DH), lambda h: (0, 2 * HEADS + h)), lane, lane, st, st],
               out_specs=[hc, hc, pl.BlockSpec((1, SUBLANES, LANES), lambda h: (h, 0, 0))],
               out_shape=[_sds((l_len, RET_W), BF16), _sds((l_len, RET_W), BF16), _sds((HEADS, SUBLANES, LANES))],
               compiler_params=_params("arbitrary"))(projc, projc, lgv, sgv, dsf, dsb)


G_BLOCK = (3 * RET_W) // RET_W
GATE_BLOCK = (4 * RET_W + LRU_W) // LRU_W


def _head_norm(y):
    yc = y - jnp.mean(y, axis=-1, keepdims=True)
    rs = lax.rsqrt(jnp.mean(yc * yc, axis=-1, keepdims=True) + EPS)
    return yc * rs, rs


def _gelu_parts(z):
    th = jnp.tanh(GELU_K * (z + GELU_C * z * z * z))
    return 0.5 * z * (1.0 + th), th


def _mix_fwd(o_f, o_b, proj, hf, hb, w_out, x, g1):
    t = x.shape[0]
    tm = _tile(t, True)

    def body(of_ref, ob_ref, g_ref, gt_ref, hf_ref, hb_ref, w_ref, x_ref, g1_ref, x1_ref, cat_ref):
        o = of_ref[...] + ob_ref[...]
        g = g_ref[...].astype(F32)
        for hh in range(HEADS):
            sl = slice(DH * hh, DH * (hh + 1))
            nrm, _ = _head_norm(o[:, sl])
            gh = g[:, sl]
            cat_ref[:, sl] = (gh * _sigmoid(gh) * nrm).astype(BF16)
        gel, _ = _gelu_parts(gt_ref[...].astype(F32))
        cat_ref[:, RET_W:] = ((hf_ref[...] + hb_ref[...]) * gel).astype(BF16)
        x1_ref[...] = x_ref[...] + g1_ref[...] * _dot(cat_ref[...], w_ref[...])

    half = pl.BlockSpec((tm, RET_W), lambda i: (i, 0))
    big = pl.BlockSpec((tm, D_MODEL), lambda i: (i, 0))
    return _pc(body, name="mix_fwd", grid=(t // tm,),
               in_specs=[half, half, pl.BlockSpec((tm, RET_W), lambda i: (i, G_BLOCK)),
                         pl.BlockSpec((tm, LRU_W), lambda i: (i, GATE_BLOCK)), half, half,
                         _full((D_MODEL, D_MODEL)), big, _full((1, D_MODEL))],
               out_specs=[big, big], out_shape=[_sds((t, D_MODEL)), _sds((t, D_MODEL), BF16)],
               compiler_params=_params("arbitrary"))(o_f, o_b, proj, proj, hf, hb, w_out, x, g1)


def _mix_bwd(o_f, o_b, proj, hf, hb, w_out, cat, dx1, g1, comms=()):
    t = dx1.shape[0]
    tm = _tile(t, True)

    def body(of_ref, ob_ref, g_ref, gt_ref, hf_ref, hb_ref, w_ref, cat_ref, dx1_ref, g1_ref,
             do_ref, dhs_ref, dg_ref, dgt_ref, dyb_ref, dg1_ref):
        dx1v = dx1_ref[...]
        y = _dot(cat_ref[...], w_ref[...])

        @pl.when(pl.program_id(0) == 0)
        def _():
            dg1_ref[...] = jnp.zeros_like(dg1_ref)
        dg1_ref[...] += _sum0(dx1v * y)
        dyb = (g1_ref[...] * dx1v).astype(BF16)
        dyb_ref[...] = dyb
        dcat = _dot_nt(dyb, w_ref[...])
        o = of_ref[...] + ob_ref[...]
        g = g_ref[...].astype(F32)
        for hh in range(HEADS):
            sl = slice(DH * hh, DH * (hh + 1))
            nrm, rs = _head_norm(o[:, sl])
            gh = g[:, sl]
            sg = _sigmoid(gh)
            dret = dcat[:, sl]
            dg_ref[:, sl] = (dret * nrm * (sg * (1.0 + gh * (1.0 - sg)))).astype(BF16)
            dn = dret * (gh * sg)
            dyc = rs * (dn - nrm * jnp.mean(dn * nrm, axis=-1, keepdims=True))
            do_ref[:, sl] = (dyc - jnp.mean(dyc, axis=-1, keepdims=True)).astype(BF16)
        z = gt_ref[...].astype(F32)
        gel, th = _gelu_parts(z)
        dlru = dcat[:, RET_W:]
        dhs_ref[...] = dlru * gel
        dgel = 0.5 * (1.0 + th) + 0.5 * z * (1.0 - th * th) * GELU_K * (1.0 + 3.0 * GELU_C * z * z)
        dgt_ref[...] = (dlru * (hf_ref[...] + hb_ref[...]) * dgel).astype(BF16)

    half = pl.BlockSpec((tm, RET_W), lambda i: (i, 0))
    big = pl.BlockSpec((tm, D_MODEL), lambda i: (i, 0))
    return _call(body, name="mix_bwd", grid=(t // tm,),
                 in_specs=[half, half, pl.BlockSpec((tm, RET_W), lambda i: (i, G_BLOCK)),
                           pl.BlockSpec((tm, LRU_W), lambda i: (i, GATE_BLOCK)), half, half,
                           _full((D_MODEL, D_MODEL)), big, big, _full((1, D_MODEL))],
                 out_specs=[half, half, half, half, big, _full((1, D_MODEL))],
                 out_shape=[_sds((t, RET_W), BF16), _sds((t, RET_W)), _sds((t, RET_W), BF16), _sds((t, RET_W), BF16),
                            _sds((t, D_MODEL), BF16), _sds((1, D_MODEL))],
                 scratch_shapes=[], sem=("arbitrary",), args=(o_f, o_b, proj, proj, hf, hb, w_out, cat, dx1, g1),
                 comms=comms)


def _mlp(x1, n2g, sh2, sc2, g2, fg, w1_parts, w2_parts, tgt):
    t = x1.shape[0]
    tm = _tile(t)
    hb_ = MLP_H // N_CHIP
    q_rows = hb_ // 4
    n_cp = 4 * N_DEV

    def body(x1_ref, n2g_ref, sh2_ref, sc2_ref, g2_ref, fg_ref, w1a, w1b, w2a, w2b, tgt_ref,
             dx1_ref, h2b_ref, ab_ref, dub_ref, dmb_ref, dsc_ref, dsh_ref, dg2_ref, dn2_ref, dfg_ref, loss_ref,
             w1_s, w2_s, r_s, sems):
        @pl.when(pl.program_id(0) == 0)
        def _():
            cps = []
            for p, parts in enumerate(((w1a, w2a), (w1b, w2b))):
                for d in range(N_DEV):
                    rows = pl.ds(2 * q_rows * (d % 2) + q_rows * p, q_rows)
                    for src, dst in zip(parts, (w1_s, w2_s)):
                        cps.append(pltpu.make_async_copy(src.at[d], dst.at[d // 2, rows], sems.at[len(cps)]))
            for cp in cps:
                cp.start()
            for r in (dsc_ref, dsh_ref, dg2_ref, dn2_ref, dfg_ref, loss_ref):
                r[...] = jnp.zeros_like(r)
            for cp in cps:
                cp.wait()
        x1v = x1_ref[...]
        n2g, sc2, g2, fg = n2g_ref[...], sc2_ref[...], g2_ref[...], fg_ref[...]
        xh, _ = _rms(x1v)
        h2b = (xh * n2g * (1.0 + sc2) + sh2_ref[...]).astype(BF16)
        h2b_ref[...] = h2b
        m = jnp.zeros((tm, D_MODEL), F32)
        for j in range(N_CHIP):
            sl = slice(hb_ * j, hb_ * (j + 1))
            r = jnp.maximum(_dot(h2b, w1_s[j]), 0.0)
            r_s[:, sl] = r
            ab = (r * r).astype(BF16)
            ab_ref[:, sl] = ab
            m = m + _dot(ab, w2_s[j])
        x2 = x1v + g2 * m
        x2h, r2 = _rms(x2)
        err = x2h * fg - tgt_ref[...]
        loss_ref[...] += _sum0(err * err)
        dout = err * (1.0 / D_MODEL)
        dfg_ref[...] += _sum0(dout * x2h)
        dxh = dout * fg
        dx2 = r2 * (dxh - x2h * jnp.mean(dxh * x2h, axis=-1, keepdims=True))
        dg2_ref[...] += _sum0(dx2 * m)
        dmb = (g2 * dx2).astype(BF16)
        dmb_ref[...] = dmb
        dh2 = jnp.zeros((tm, D_MODEL), F32)
        for j in range(N_CHIP):
            sl = slice(hb_ * j, hb_ * (j + 1))
            dub = (_dot_nt(dmb, w2_s[j]) * (2.0 * r_s[:, sl])).astype(BF16)
            dub_ref[:, sl] = dub
            dh2 = dh2 + _dot_nt(dub, w1_s[j])
        dx, dn2_t, dsh_t, dsc_t = _norm_mod_bwd(x1v, n2g, sc2, dh2)
        dx1_ref[...] = dx2 + dx
        dn2_ref[...] += dn2_t
        dsh_ref[...] += dsh_t
        dsc_ref[...] += dsc_t

        @pl.when(pl.program_id(0) == t // tm - 1)
        def _():
            tot = jnp.sum(loss_ref[...], axis=1, keepdims=True) * (0.5 / D_MODEL)
            loss_ref[...] = jnp.broadcast_to(tot, loss_ref.shape)

    row = _full((1, D_MODEL))
    big = pl.BlockSpec((tm, D_MODEL), lambda i: (i, 0))
    wide = pl.BlockSpec((tm, MLP_H), lambda i: (i, 0))
    return _pc(body, name="mlp", grid=(t // tm,),
               in_specs=[big, row, row, row, row, row, ANY, ANY, ANY, ANY, big],
               out_specs=[big, big, wide, wide, big, row, row, row, row, row, row],
               out_shape=[_sds((t, D_MODEL)), _sds((t, D_MODEL), BF16), _sds((t, MLP_H), BF16), _sds((t, MLP_H), BF16),
                          _sds((t, D_MODEL), BF16)] + [_sds((1, D_MODEL))] * 6,
               scratch_shapes=[pltpu.VMEM((N_CHIP, D_MODEL, hb_), BF16), pltpu.VMEM((N_CHIP, hb_, D_MODEL), BF16),
                               pltpu.VMEM((tm, MLP_H), F32), pltpu.SemaphoreType.DMA((n_cp,))],
               compiler_params=_params("arbitrary"))(x1, n2g, sh2, sc2, g2, fg, *w1_parts, *w2_parts, tgt)


def _tn(a, b, nj, a_blocked, b_blocked, name, extra=None, comms=(), b_part=(0, 1)):
    t = a.shape[0]
    part, parts = b_part
    m = a.shape[1] // (nj if a_blocked else 1)
    n = b.shape[1] // (nj if b_blocked else 1) // parts
    bk = next((b for b in (2048, 1024, 512) if t % b == 0), t)
    nk = t // bk
    a_col = (lambda j: j) if a_blocked else (lambda j: 0)
    b_col = (lambda j: parts * j + part) if b_blocked else (lambda j: part)
    in_specs = [pl.BlockSpec((bk, m), lambda j, k: (k, a_col(j))), pl.BlockSpec((bk, n), lambda j, k: (k, b_col(j)))]
    args = [a, b]
    if extra is not None:
        a2, b2 = extra
        t2 = a2.shape[0]
        in_specs += [pl.BlockSpec((t2, m), lambda j, k: (0, a_col(j))),
                     pl.BlockSpec((t2, n), lambda j, k: (0, b_col(j)))]
        args += [a2, b2]

    def body(*refs):
        a_ref, b_ref = refs[0], refs[1]
        o_ref, acc = refs[-2], refs[-1]
        k = pl.program_id(1)

        @pl.when(k == 0)
        def _():
            acc[...] = jnp.zeros_like(acc)
        acc[...] += _dot_tn(a_ref[...].astype(BF16), b_ref[...].astype(BF16))

        @pl.when(k == nk - 1)
        def _():
            if extra is not None:
                acc[...] += _dot_tn(refs[2][...].astype(BF16), refs[3][...].astype(BF16))
            o_ref[0] = acc[...]

    (out,), couts = _call(body, name=name, grid=(nj, nk), in_specs=in_specs,
                          out_specs=[pl.BlockSpec((1, m, n), lambda j, k: (j, 0, 0))], out_shape=[_sds((nj, m, n))],
                          scratch_shapes=[pltpu.VMEM((m, n), F32)], sem=("arbitrary", "arbitrary"), args=args,
                          comms=comms)
    return (out, couts) if comms else out


ROW_LOSS = 0
ROW_DMOD = 1
ROW_DMODC = 7
ROW_N1, ROW_N2, ROW_FG, ROW_CB = 9, 10, 11, 12
ROW_BA, ROW_BX, ROW_LAM = 13, 15, 17
ROW_CW = 20
ROW_RD = 24
SLAB_ROWS = 32
SEG = D_MODEL // 2


def _pack_small(rows, drd, cw2, cb2, lru2, gates):
    n_rows, n_lru = len(rows), len(lru2)

    def body(*refs):
        r = refs[:n_rows]
        drd_f, drd_b, drd_c, cw_a, cw_b, cb_a, cb_b = refs[n_rows:n_rows + 7]
        lru = refs[n_rows + 7:n_rows + 7 + n_lru]
        gf_ref, gb_ref, slab, ga, gx = refs[n_rows + 7 + n_lru:]
        slab[...] = jnp.zeros_like(slab)
        slab[ROW_LOSS:ROW_LOSS + 1, :] = r[0][...]
        for k in range(N_MOD):
            slab[ROW_DMOD + k:ROW_DMOD + k + 1, :] = r[1 + k][...]
        slab[ROW_DMODC:ROW_DMODC + 1, :] = r[7][...]
        slab[ROW_DMODC + 1:ROW_DMODC + 2, :] = r[8][...]
        slab[ROW_N1:ROW_N1 + 1, :] = r[9][...] + r[10][...]
        slab[ROW_N2:ROW_N2 + 1, :] = r[11][...]
        slab[ROW_FG:ROW_FG + 1, :] = r[12][...]
        slab[ROW_CB:ROW_CB + 1, 0:LRU_W] = cb_a[...] + cb_b[...]
        for k, row in enumerate((ROW_BA, ROW_BA + 1, ROW_BX, ROW_BX + 1, ROW_LAM, ROW_LAM + 1)):
            slab[row:row + 1, 0:LRU_W] = lru[2 * k][...] + lru[2 * k + 1][...]
        slab[ROW_CW:ROW_CW + 4, 0:LRU_W] = cw_a[...] + cw_b[...]
        for h in range(HEADS):
            slab[ROW_RD + h:ROW_RD + h + 1, 0:LANES] = drd_f[h, 0:1, :] + drd_c[h, 0:1, :]
            slab[ROW_RD + HEADS + h:ROW_RD + HEADS + h + 1, 0:LANES] = drd_b[h, 0:1, :] + drd_c[h, 1:2, :]
        for d, g_ref in enumerate((gf_ref, gb_ref)):
            for n in range(LRU_BLOCKS):
                blk = slice(LRU_BD * n, LRU_BD * (n + 1))
                ga[blk, LRU_BD * d:LRU_BD * (d + 1)] = g_ref[0, blk, blk].astype(BF16)
                gx[blk, LRU_BD * d:LRU_BD * (d + 1)] = g_ref[1, blk, blk].astype(BF16)

    args = list(rows) + list(drd) + list(cw2) + list(cb2) + list(lru2) + list(gates)
    gate_shape = (LRU_W, 2 * LRU_BD)
    return _pc(body, name="pack_small", in_specs=[_full(a.shape) for a in args],
               out_specs=[_full((SLAB_ROWS, D_MODEL)), _full(gate_shape), _full(gate_shape)],
               out_shape=[_sds((SLAB_ROWS, D_MODEL)), _sds(gate_shape, BF16), _sds(gate_shape, BF16)],
               compiler_params=_params())(*args)


def _adam_math(w, g, m, v):
    mn = ADAM_B1 * m + (1.0 - ADAM_B1) * g
    vn = ADAM_B2 * v + (1.0 - ADAM_B2) * (g * g)
    mh = mn / (1.0 - ADAM_B1 ** ADAM_STEP)
    vh = vn / (1.0 - ADAM_B2 ** ADAM_STEP)
    return -ADAM_LR * (mh / (jnp.sqrt(vh) + ADAM_EPS) + ADAM_WD * w), mn, vn


SMALL_PARAMS = ("b_ada", "norm1_g", "norm2_g", "final_g", "ret_decay", "conv_w", "conv_b", "lru_wa", "lru_ba", "lru_wx",
                "lru_bx", "lru_lambda")


def _finalize_small(chip_idx, slab_all, ga_all, gx_all, wmv):
    n_p = len(SMALL_PARAMS)
    flat = [a for nm in SMALL_PARAMS for a in wmv[nm]]
    ada_n = N_MOD * D_MODEL // N_CHIP

    def body(c_ref, slab_ref, ga_ref, gx_ref, *refs):
        prm = {nm: refs[3 * k:3 * k + 3] for k, nm in enumerate(SMALL_PARAMS)}
        outs = {nm: refs[3 * n_p + 4 * k:3 * n_p + 4 * k + 4] for k, nm in enumerate(SMALL_PARAMS)}
        b128_ref, dmc_ref, loss_ref = refs[3 * n_p + 4 * n_p:]
        chip = c_ref[0]

        def pick(fn):
            acc = fn(0)
            for j in range(1, N_CHIP):
                acc = jnp.where(chip == j, fn(j), acc)
            return acc

        tot = slab_ref[0]
        for d in range(1, N_DEV):
            tot = tot + slab_ref[d]

        def update(nm, g, sl=None, rows=None):
            w_ref, m_ref, v_ref = prm[nm]
            g_ref, d_ref, mo_ref, vo_ref = outs[nm]
            ix = (slice(None) if rows is None else rows, slice(None) if sl is None else sl)
            dl, mn, vn = _adam_math(w_ref[ix], g, m_ref[ix], v_ref[ix])
            g_ref[ix] = g
            d_ref[ix] = dl
            mo_ref[ix] = mn
            vo_ref[ix] = vn

        loss_ref[...] = jnp.broadcast_to(tot[ROW_LOSS:ROW_LOSS + 1, 0:LANES], (SUBLANES, LANES))
        for k in range(N_MOD):
            g = tot[ROW_DMOD + k:ROW_DMOD + k + 1, :]
            if k < 2:
                g = g + tot[ROW_DMODC + k:ROW_DMODC + k + 1, :]
            update("b_ada", g, slice(D_MODEL * k, D_MODEL * (k + 1)))
        update("norm1_g", tot[ROW_N1:ROW_N1 + 1, :])
        update("norm2_g", tot[ROW_N2:ROW_N2 + 1, :])
        update("final_g", tot[ROW_FG:ROW_FG + 1, :])
        update("ret_decay", tot[ROW_RD:ROW_RD + SUBLANES, 0:LANES])
        update("conv_b", tot[ROW_CB:ROW_CB + 1, 0:LRU_W])
        update("conv_w", pick(lambda j: tot[ROW_CW:ROW_CW + 4, LANES * j:LANES * (j + 1)]))
        for nm, row in (("lru_ba", ROW_BA), ("lru_bx", ROW_BX), ("lru_lambda", ROW_LAM)):
            update(nm, pick(lambda j, row=row: tot[row:row + 2, LANES * j:LANES * (j + 1)]))
        for nm, g_all in (("lru_wa", ga_ref), ("lru_wx", gx_ref)):
            for dr in range(2):
                lanes = slice(LRU_BD * dr, LRU_BD * (dr + 1))
                g = g_all[0, :, lanes].astype(F32)
                for d in range(1, N_DEV):
                    g = g + g_all[d, :, lanes].astype(F32)
                update(nm, g, rows=slice(LRU_W * dr, LRU_W * (dr + 1)))

        def seg(rows6, s):
            return rows6[s // 2][:, SEG * (s % 2):SEG * (s % 2 + 1)]

        b128_ref[...] = jnp.zeros_like(b128_ref)
        dmc_ref[...] = jnp.zeros_like(dmc_ref)
        zero = jnp.zeros((1, D_MODEL), F32)
        ctx6 = [tot[ROW_DMODC:ROW_DMODC + 1, :], tot[ROW_DMODC + 1:ROW_DMODC + 2, :]] + [zero] * (N_MOD - 2)
        for q in range(ada_n // SEG):
            cols = slice(SEG * q, SEG * (q + 1))
            for d in range(N_DEV):
                rows6 = [slab_ref[d, ROW_DMOD + k:ROW_DMOD + k + 1, :] for k in range(N_MOD)]
                b128_ref[d:d + 1, cols] = pick(lambda j, rows6=rows6: seg(rows6, 3 * j + q))
            c = pick(lambda j: seg(ctx6, 3 * j + q))
            b128_ref[N_DEV:N_DEV + 1, cols] = c
            dmc_ref[0:1, cols] = c

    out_shape = []
    for nm in SMALL_PARAMS:
        out_shape += [_sds(wmv[nm][0].shape)] * 4
    out_shape += [_sds((LANES, ada_n)), _sds((SUBLANES, ada_n)), _sds((SUBLANES, LANES))]
    args = [slab_all, ga_all, gx_all] + flat
    grid_spec = pltpu.PrefetchScalarGridSpec(
        num_scalar_prefetch=1, grid=(1,), in_specs=[_full(a.shape) for a in args],
        out_specs=[_full(s.shape) for s in out_shape])
    outs = _pc(body, name="finalize_small", grid_spec=grid_spec, out_shape=out_shape,
               compiler_params=_params("arbitrary"))(chip_idx, *args)
    res = {nm: tuple(outs[4 * k:4 * k + 4]) for k, nm in enumerate(SMALL_PARAMS)}
    return res, outs[4 * n_p], outs[4 * n_p + 1], outs[4 * n_p + 2]


def _block_diag(w):
    eye = jnp.eye(LRU_BLOCKS, dtype=F32)
    return (w[:, :, None, :] * eye[:, None, :, None]).reshape(LRU_W, LRU_W).astype(BF16)


def _lane_rep(v8):
    return jnp.broadcast_to(v8.reshape(SUBLANES, 1), (SUBLANES, LANES))


def kernel(x, c, ctx, c_ctx, w_ada, b_ada, norm1_g, norm2_g, w_in, ret_decay, conv_w, conv_b, lru_wa, lru_ba, lru_wx, lru_bx, lru_lambda, w_out, w_mlp1, w_mlp2, final_g, loss_target, m_c_ctx, m_w_ada, m_b_ada, m_norm1_g, m_norm2_g, m_w_in, m_ret_decay, m_conv_w, m_conv_b, m_lru_wa, m_lru_ba, m_lru_wx, m_lru_bx, m_lru_lambda, m_w_out, m_w_mlp1, m_w_mlp2, m_final_g, v_c_ctx, v_w_ada, v_b_ada, v_norm1_g, v_norm2_g, v_w_in, v_ret_decay, v_conv_w, v_conv_b, v_lru_wa, v_lru_ba, v_lru_wx, v_lru_bx, v_lru_lambda, v_w_out, v_w_mlp1, v_w_mlp2, v_final_g):
    ax, ay, ac = lax.axis_index("x"), lax.axis_index("y"), lax.axis_index("c")
    chip = 2 * ax + ay
    dev = 4 * ax + 2 * ay + ac
    c_idx = jnp.stack([ac, chip]).astype(jnp.int32)
    j_idx = chip.reshape(1).astype(jnp.int32)

    xt = x[0]
    t_len = xt.shape[0]
    ctxt = ctx[0]
    l_len = ctxt.shape[0]
    tgt = loss_target[0]
    ada_n = w_ada.shape[2]

    def my_half(w2d):
        r = w2d.shape[0] // 2
        return lax.dynamic_slice_in_dim(w2d, ac * r, r, axis=0).astype(BF16)

    pad8 = lambda a: jnp.pad(a, ((0, SUBLANES - a.shape[0]), (0, 0)))
    small = jnp.concatenate([pad8(conv_w[0]), pad8(lru_ba[0]), pad8(lru_bx[0]), pad8(lru_lambda[0])], axis=0)
    b_shard = lax.dynamic_slice_in_dim(b_ada, chip * ada_n, ada_n, axis=1)
    gw_in, _, small_all, a16, mod_parts, lgv, sgv = _head(
        my_half(w_in[0]), pad8(c), small, w_ada[0], b_shard, c_ctx, ret_decay[0])
    w4 = gw_in.reshape(N_CHIP, D_MODEL, IN_COLS // N_CHIP)

    mod_all = mod_parts[0::2].transpose(1, 0, 2).reshape(16, N_CHIP * ada_n)
    mod_me = lax.dynamic_slice_in_dim(mod_all, dev, 1, axis=0)
    sh1, sc1, g1, sh2, sc2, g2 = [mod_me[:, D_MODEL * k:D_MODEL * (k + 1)] for k in range(N_MOD)]
    csh1, csc1 = mod_all[8:9, 0:D_MODEL], mod_all[8:9, D_MODEL:2 * D_MODEL]

    cos2, sin2 = _rotary_tables(t_len)
    cos_c, sin_c = jnp.ones((l_len, DH), F32), jnp.zeros((l_len, DH), F32)
    n1g, n2g = norm1_g, norm2_g
    fg = final_g.reshape(1, D_MODEL)

    small_full = small_all[0::2].transpose(1, 0, 2).reshape(4 * SUBLANES, LRU_W)
    cw = small_full[0:4]
    cb = conv_b
    ba_f, ba_b = small_full[8:9], small_full[9:10]
    bx_f, bx_b = small_full[16:17], small_full[17:18]
    lam_f, lam_b = small_full[24:25], small_full[25:26]
    wa_f, wa_b = _block_diag(lru_wa[0, 0]), _block_diag(lru_wa[0, 1])
    wx_f, wx_b = _block_diag(lru_wx[0, 0]), _block_diag(lru_wx[0, 1])
    zero_h = jnp.zeros((1, LRU_W), F32)

    projc, xrc, hcb16 = _inproj_fwd(ctxt, n1g, csh1, csc1, w4, cos_c, sin_c, "inproj_fwd_ctx")
    s_f, s_b = _ctx_state_fwd(projc, lgv)
    xcc = _conv_fwd(xrc, cw, cb, "conv_fwd_ctx")
    hcf = _lru_fwd(xcc, wa_f, wx_f, ba_f, bx_f, lam_f, zero_h, False, "lru_fwd_ctx_f")
    hcbk = _lru_fwd(xcc, wa_b, wx_b, ba_b, bx_b, lam_b, zero_h, True, "lru_fwd_ctx_b")
    lru_sf, lru_sb = hcf[l_len - 1:l_len], hcbk[0:1]

    h1, h2 = my_half(w_mlp1[0]), my_half(w_mlp2[0])
    q = h1.shape[0] // 2
    (proj, xrl, hb16), ((gw_1a,),) = _inproj_fwd(xt, n1g, sh1, sc1, w4, cos2, sin2, "inproj_fwd",
                                           comms=(_AllGather([h1[:q]]),))
    (o_f, o_b, spf, spb), ((gw_1b, gw_out),) = _ret_fwd(proj, lgv, s_f, s_b,
                                                       comms=(_AllGather([h1[q:], my_half(w_out[0])]),))
    xcl = _conv_fwd(xrl, cw, cb, "conv_fwd")
    hf, ((gw_2a,),) = _lru_fwd(xcl, wa_f, wx_f, ba_f, bx_f, lam_f, lru_sf, False, "lru_fwd_f",
                              comms=(_AllGather([h2[:q]]),))
    hbk, ((gw_2b,),) = _lru_fwd(xcl, wa_b, wx_b, ba_b, bx_b, lam_b, lru_sb, True, "lru_fwd_b",
                               comms=(_AllGather([h2[q:]]),))
    wo = gw_out.reshape(D_MODEL, D_MODEL)
    x1, cat = _mix_fwd(o_f, o_b, proj, hf, hbk, wo, xt, g1)

    (dx1, h2b, ab, dub, dmb, dsc2, dsh2, dg2, dn2g, dfg, lossv) = _mlp(
        x1, n2g, sh2, sc2, g2, fg, (gw_1a, gw_1b), (gw_2a, gw_2b), tgt)
    gw_mlp1 = _tn(h2b, dub, N_CHIP, False, True, "grad_w_mlp1")
    b_1 = gw_mlp1.reshape(N_DEV, D_MODEL // 2, MLP_H // N_CHIP)
    gw_mlp2, ((r_1,),) = _tn(ab, dmb, N_CHIP, True, False, "grad_w_mlp2", comms=(_pair_exchange([b_1]),))

    jc_idx = c_idx
    b_2 = gw_mlp2.reshape(N_DEV, MLP_H // N_DEV, D_MODEL)
    p_1, pb_1 = _pair_add(b_1, r_1, c_idx, "rs_pair_add_w_mlp1")
    (do, dhs, dg, dgate, dyb, dg1), ((q_1,), (r_2,)) = _mix_bwd(
        o_f, o_b, proj, hf, hbk, wo, cat, dx1, g1, comms=(_chip_exchange([pb_1]), _pair_exchange([b_2])))
    gw_o = _tn(cat, dyb, 1, False, False, "grad_w_out")
    b_o = gw_o.reshape(N_DEV, D_MODEL // N_DEV, D_MODEL)
    p_2, pb_2 = _pair_add(b_2, r_2, c_idx, "rs_pair_add_w_mlp2")
    h_1 = _chip_add(p_1, q_1, jc_idx, "rs_chip_add_w_mlp1")

    (dq_f, dk_f, dv_f, ds_f, drd_f), ((r_o,), (f_1,)) = _ret_bwd(
        proj, lgv, sgv, spf, do, False, "ret_bwd_f", comms=(_pair_exchange([b_o]), _pair_gather([h_1])))
    p_o, pb_o = _pair_add(b_o, r_o, c_idx, "rs_pair_add_w_out")

    (dq_b, dk_b, dv_b, ds_b, drd_b), ((q_o,),) = _ret_bwd(
        proj, lgv, sgv, spb, do, True, "ret_bwd_b", comms=(_chip_exchange([pb_o]),))
    h_o = _chip_add(p_o, q_o, jc_idx, "rs_chip_add_w_out")

    (dxc_f, dpre_f, dba_f, dbx_f, dlam_f, dh0_f), ((q_2,), (f_o,)) = _lru_bwd(
        xcl, wa_f, wx_f, ba_f, bx_f, lam_f, hf, lru_sf, dhs, False, "lru_bwd_f",
        comms=(_chip_exchange([pb_2]), _pair_gather([h_o])))
    h_2 = _chip_add(p_2, q_2, jc_idx, "rs_chip_add_w_mlp2")
    (dxc_b, dpre_b, dba_b, dbx_b, dlam_b, dh0_b), ((f_2,),) = _lru_bwd(
        xcl, wa_b, wx_b, ba_b, bx_b, lam_b, hbk, lru_sb, dhs, True, "lru_bwd_b", comms=(_pair_gather([h_2]),))
    dxr, dcw, dcb = _conv_bwd(dxc_f, dxc_b, xrl, cw, "conv_bwd")
    grad_x, dpb, dn1g, dsh1, dsc1 = _inproj_bwd(
        xt, n1g, sh1, sc1, w4, cos2, sin2, [dq_f, dq_b, dk_f, dk_b, dv_f, dv_b, dg, dxr, dgate], dx1, "inproj_bwd")

    dkc, dvc, drd_c = _ctx_state_bwd(projc, lgv, sgv, ds_f, ds_b)
    zc = jnp.zeros((l_len, LRU_W), F32)
    dhc_f = lax.dynamic_update_slice(zc, dh0_f, (l_len - 1, 0))
    dhc_b = lax.dynamic_update_slice(zc, dh0_b, (0, 0))
    (dxcc_f, dprec_f, dbac_f, dbxc_f, dlamc_f, _), _ = _lru_bwd(
        xcc, wa_f, wx_f, ba_f, bx_f, lam_f, hcf, zero_h, dhc_f, False, "lru_bwd_ctx_f")
    (dxcc_b, dprec_b, dbac_b, dbxc_b, dlamc_b, _), _ = _lru_bwd(
        xcc, wa_b, wx_b, ba_b, bx_b, lam_b, hcbk, zero_h, dhc_b, True, "lru_bwd_ctx_b")
    dxrc, dcw_c, dcb_c = _conv_bwd(dxcc_f, dxcc_b, xrc, cw, "conv_bwd_ctx")
    zr = jnp.zeros((l_len, RET_W), BF16)
    _, dpbc, dn1g_c, dcsh1, dcsc1 = _inproj_bwd(
        ctxt, n1g, csh1, csc1, w4, cos_c, sin_c, [zr, zr, dkc, zr, dvc, zr, zr, dxrc, zr],
        jnp.zeros((l_len, D_MODEL), F32), "inproj_bwd_ctx")

    half_shape = (N_DEV, D_MODEL // 2, IN_COLS // N_CHIP // 2)
    b_ia = _tn(hb16, dpb, N_CHIP, False, True, "grad_w_in_a", extra=(hcb16, dpbc), b_part=(0, 2)).reshape(half_shape)
    gw_ib, ((r_ia,),) = _tn(hb16, dpb, N_CHIP, False, True, "grad_w_in_b", extra=(hcb16, dpbc), b_part=(1, 2),
                            comms=(_pair_exchange([b_ia]),))
    b_ib = gw_ib.reshape(half_shape)
    p_ia, pb_ia = _pair_add(b_ia, r_ia, c_idx, "rs_pair_add_w_in_a")
    gwa_f, ((q_ia,), (r_ib,)) = _tn(xcl, dpre_f, 2, False, True, "grad_lru_gates_f", extra=(xcc, dprec_f),
                                    comms=(_chip_exchange([pb_ia]), _pair_exchange([b_ib])))
    p_ib, pb_ib = _pair_add(b_ib, r_ib, c_idx, "rs_pair_add_w_in_b")
    h_ia = _chip_add(p_ia, q_ia, jc_idx, "rs_chip_add_w_in_a")
    gwa_b, ((q_ib,), (f_ia,)) = _tn(xcl, dpre_b, 2, False, True, "grad_lru_gates_b", extra=(xcc, dprec_b),
                                    comms=(_chip_exchange([pb_ib]), _pair_gather([h_ia])))
    (f_ib,) = _run_comm(_pair_gather([_chip_add(p_ib, q_ib, jc_idx, "rs_chip_add_w_in_b")]), "rs_pair_gather_w_in_b")
    g_in = jnp.concatenate([_shard_of(f_ia), _shard_of(f_ib)], axis=1)
    g_out, g_1, g_2 = _shard_of(f_o), _shard_of(f_1), _shard_of(f_2)
    big = {}
    for nm, w, g, m, v in (("w_in", w_in, g_in, m_w_in, v_w_in), ("w_out", w_out, g_out, m_w_out, v_w_out),
                           ("w_mlp1", w_mlp1, g_1, m_w_mlp1, v_w_mlp1), ("w_mlp2", w_mlp2, g_2, m_w_mlp2, v_w_mlp2)):
        go, d_, mn, vn = _adamw(w[0], g, m[0], v[0], "adamw_" + nm)
        big[nm] = (go[None], d_[None], mn[None], vn[None])

    slab, ga, gx = _pack_small(
        [lossv, dsh1, dsc1, dg1, dsh2, dsc2, dg2, dcsh1, dcsc1, dn1g, dn1g_c, dn2g, dfg],
        (drd_f, drd_b, drd_c), (dcw, dcw_c), (dcb, dcb_c),
        (dba_f, dbac_f, dba_b, dbac_b, dbx_f, dbxc_f, dbx_b, dbxc_b, dlam_f, dlamc_f, dlam_b, dlamc_b),
        (gwa_f, gwa_b))
    slab_all, ga_all, gx_all = _all_gather([slab, ga, gx], "gather_small_grads")
    params = {
        "b_ada": (b_ada, m_b_ada, v_b_ada), "norm1_g": (norm1_g, m_norm1_g, v_norm1_g),
        "norm2_g": (norm2_g, m_norm2_g, v_norm2_g), "final_g": (final_g, m_final_g, v_final_g),
        "ret_decay": (ret_decay, m_ret_decay, v_ret_decay), "conv_w": (conv_w, m_conv_w, v_conv_w),
        "conv_b": (conv_b, m_conv_b, v_conv_b), "lru_wa": (lru_wa, m_lru_wa, v_lru_wa),
        "lru_ba": (lru_ba, m_lru_ba, v_lru_ba), "lru_wx": (lru_wx, m_lru_wx, v_lru_wx),
        "lru_bx": (lru_bx, m_lru_bx, v_lru_bx), "lru_lambda": (lru_lambda, m_lru_lambda, v_lru_lambda),
    }
    as2d = {
        "b_ada": lambda a: a, "norm1_g": lambda a: a, "norm2_g": lambda a: a, "conv_b": lambda a: a,
        "final_g": lambda a: a.reshape(1, D_MODEL), "ret_decay": lambda a: _lane_rep(a.reshape(-1)),
        "conv_w": lambda a: a[0], "lru_ba": lambda a: a[0], "lru_bx": lambda a: a[0], "lru_lambda": lambda a: a[0],
        "lru_wa": lambda a: a.reshape(2 * LRU_W, LRU_BD), "lru_wx": lambda a: a.reshape(2 * LRU_W, LRU_BD),
    }
    res, b128, dmc8, loss8 = _finalize_small(
        j_idx, slab_all, ga_all, gx_all, {nm: tuple(as2d[nm](a) for a in params[nm]) for nm in SMALL_PARAMS})
    loss = loss8[0, 0]
    small_out = {}
    for nm in SMALL_PARAMS:
        shp = params[nm][0].shape
        if nm == "ret_decay":
            small_out[nm] = tuple(o[:, 0].reshape(shp) for o in res[nm])
        else:
            small_out[nm] = tuple(o.reshape(shp) for o in res[nm])

    g_ada = _ada_grad(jnp.pad(a16.T, ((0, 0), (0, LANES - 16))), b128)
    g_ada, d_ada, m_ada, v_ada = _adamw(w_ada[0], g_ada, m_w_ada[0], v_w_ada[0], "adamw_w_ada")

    (cparts,) = _all_gather([_cctx_partial(dmc8, w_ada[0])], "gather_cctx")
    g_cc, d_cc, m_cc, v_cc = _cctx_final(cparts, c_ctx, m_c_ctx, v_c_ctx)
    small_out["c_ctx"] = tuple(a.reshape(D_MODEL) for a in (g_cc, d_cc, m_cc, v_cc))
    small_out["w_ada"] = (g_ada[None], d_ada[None], m_ada[None], v_ada[None])
    small_out.update(big)

    order = ["c_ctx", "w_ada", "b_ada", "norm1_g", "norm2_g", "w_in", "ret_decay", "conv_w", "conv_b", "lru_wa", "lru_ba",
             "lru_wx", "lru_bx", "lru_lambda", "w_out", "w_mlp1", "w_mlp2", "final_g"]
    outs = [loss, grad_x[None]]
    for k in range(4):
        outs += [small_out[nm][k] for nm in order]
    return tuple(outs)
```

```python
import math

import jax
import jax.numpy as jnp
from jax import lax
from jax.experimental import pallas as pl
from jax.experimental.pallas import tpu as pltpu

F32 = jnp.float32
BF16 = jnp.bfloat16

D_MODEL = 1024
HEADS = 4
DH = 128
CHUNK = 256
RET_W = HEADS * DH
LRU_W = 512
LRU_BLOCKS = 8
LRU_BD = LRU_W // LRU_BLOCKS
LRU_C = 8.0
IN_COLS = 4 * RET_W + 2 * LRU_W
MLP_H = 4 * D_MODEL
N_MOD = 6
GRID_W = 64
ROPE_BASE = 10000.0
K_SCALE = DH ** -0.5
EPS = 1e-6
GELU_K = math.sqrt(2.0 / math.pi)
GELU_C = 0.044715

ADAM_LR = 0.001
ADAM_B1 = 0.9
ADAM_B2 = 0.999
ADAM_EPS = 1e-08
ADAM_WD = 0.01
ADAM_STEP = 10

N_DEV = 8
N_CHIP = 4
SUBLANES = 8
LANES = 128
VMEM_LIMIT_V7X = 56 * 1024 * 1024
MESH = pl.DeviceIdType.MESH
ANY = pl.BlockSpec(memory_space=pl.ANY)


def _pc(body, **kw):
    return pl.pallas_call(body, **kw)


def _params(*sem):
    return pltpu.CompilerParams(dimension_semantics=sem if sem else None, vmem_limit_bytes=VMEM_LIMIT_V7X)


def _tile(t, big=False):
    if big and t >= 1024:
        return 512
    return 256 if t >= 256 else t


def _sds(shape, dtype=F32):
    return jax.ShapeDtypeStruct(tuple(shape), dtype)


def _full(shape):
    nd = len(shape)
    return pl.BlockSpec(tuple(shape), lambda *_: (0,) * nd)


def _sigmoid(x):
    return 1.0 / (1.0 + jnp.exp(-x))


def _log1p_pos(y):
    s = y * (1.0 - y * (0.5 - y * (1.0 / 3.0 - y * (0.25 - y * (0.2 - y / 6.0)))))
    return jnp.where(y < 0.03, s, jnp.log(1.0 + y))


def _softplus(z):
    return jnp.maximum(z, 0.0) + _log1p_pos(jnp.exp(-jnp.abs(z)))


def _one_minus_sq(la, a):
    t = la * (1.0 + la * (0.5 + la * (1.0 / 6.0 + la * (1.0 / 24.0 + la * (1.0 / 120.0)))))
    return jnp.where(la > -0.125, -t, 1.0 - a) * (1.0 + a)


def _rms(x):
    r = lax.rsqrt(jnp.mean(x * x, axis=-1, keepdims=True) + EPS)
    return x * r, r


def _dot(a, b):
    return jnp.dot(a, b, preferred_element_type=F32)


def _dot_nt(a, b):
    return lax.dot_general(a, b, (((1,), (1,)), ((), ())), preferred_element_type=F32)


def _dot_tn(a, b):
    return lax.dot_general(a, b, (((0,), (0,)), ((), ())), preferred_element_type=F32)


def _sum0(x):
    return jnp.sum(x, axis=0, keepdims=True)


def _norm_mod_bwd(x, g, sc, dh):
    xh, r = _rms(x)
    hn = xh * g
    dhn = dh * (1.0 + sc)
    dxh = dhn * g
    dx = r * (dxh - xh * jnp.mean(dxh * xh, axis=-1, keepdims=True))
    return dx, _sum0(dhn * xh), _sum0(dh), _sum0(dh * hn)


def _dev_index(p):
    return 4 * p[0] + 2 * p[1] + p[2]


def _mesh_pos():
    return lax.axis_index("x"), lax.axis_index("y"), lax.axis_index("c")


class _AllGather:
    def __init__(self, arrs):
        n = len(arrs)
        self.arrays = list(arrs)
        self.out_shapes = [_sds((N_DEV,) + a.shape, a.dtype) for a in arrs]
        self.scratch = ([pltpu.VMEM(a.shape, a.dtype) for a in arrs]
                        + [pltpu.SemaphoreType.DMA((7 * n,)), pltpu.SemaphoreType.DMA((7 * n,)),
                           pltpu.SemaphoreType.DMA((n,))])
        self.aliases = {}

    def _parts(self, ins, outs, scr):
        n = len(self.arrays)
        stage = scr[:n]
        send_sems, recv_sems, local_sems = scr[n:]
        x, y, c = _mesh_pos()
        me, sib = (x, y, c), (x, y, 1 - c)
        chips = [(1 - x, y), (x, 1 - y), (1 - x, 1 - y)]

        def copy(t, k, block, to, own=False):
            dst = outs[t].at[_dev_index(block)]
            return pltpu.make_async_remote_copy(
                src_ref=ins[t] if own else dst, dst_ref=dst,
                send_sem=send_sems.at[7 * t + k], recv_sem=recv_sems.at[7 * t + k],
                device_id=to, device_id_type=MESH)

        first = []
        for t in range(n):
            first.append(copy(t, 0, me, sib, own=True))
            for j, ch in enumerate(chips):
                first.append(copy(t, 1 + j, me, (*ch, c), own=True))
        stage_in = [pltpu.make_async_copy(ins[t], stage[t], local_sems.at[t]) for t in range(n)]
        mine = [pltpu.make_async_copy(stage[t], outs[t].at[_dev_index(me)], local_sems.at[t]) for t in range(n)]
        return n, c, me, sib, chips, copy, first, stage_in, mine

    def start(self, ins, outs, scr):
        n, _, _, _, _, _, first, stage_in, mine = self._parts(ins, outs, scr)
        for cp in stage_in:
            cp.start()
        for cp in first:
            cp.start()
        for t in range(n):
            stage_in[t].wait()
            mine[t].start()

    def finish(self, ins, outs, scr):
        n, c, me, sib, chips, copy, first, _, mine = self._parts(ins, outs, scr)
        passed = []
        for j, ch in enumerate(chips):
            for t in range(n):
                copy(t, 1 + j, (*ch, c), me).wait_recv()
                p = copy(t, 4 + j, (*ch, c), sib)
                p.start()
                passed.append(p)
        for t in range(n):
            copy(t, 0, sib, me).wait_recv()
            for j, ch in enumerate(chips):
                copy(t, 4 + j, (*ch, 1 - c), me).wait_recv()
        for cp in first + passed:
            cp.wait_send()
        for cp in mine:
            cp.wait()


class _Exchange:
    def __init__(self, arrays, out_shapes, plan, n_copies, aliases=None):
        self.arrays = list(arrays)
        self.out_shapes = list(out_shapes)
        self.plan = plan
        self.scratch = [pltpu.SemaphoreType.DMA((n_copies,)), pltpu.SemaphoreType.DMA((n_copies,))]
        self.aliases = aliases or {}

    def _copies(self, ins, outs, scr):
        send_sems, recv_sems = scr
        snd, rcv = [], []
        for i, (src, dst, peer, lands) in enumerate(self.plan(ins, outs, _mesh_pos())):
            kw = dict(send_sem=send_sems.at[i], recv_sem=recv_sems.at[i], device_id=peer, device_id_type=MESH)
            snd.append(pltpu.make_async_remote_copy(src_ref=src, dst_ref=dst, **kw))
            rcv.append(pltpu.make_async_remote_copy(src_ref=src, dst_ref=lands, **kw))
        return snd, rcv

    def start(self, ins, outs, scr):
        for cp in self._copies(ins, outs, scr)[0]:
            cp.start()

    def finish(self, ins, outs, scr):
        snd, rcv = self._copies(ins, outs, scr)
        for cp in rcv:
            cp.wait_recv()
        for cp in snd:
            cp.wait_send()


def _pair_exchange(grads):
    n = len(grads)

    def plan(ins, outs, pos):
        x, y, c = pos
        return [(ins[t].at[2 * j + (1 - c)], outs[t].at[j], (x, y, 1 - c), outs[t].at[j])
                for t in range(n) for j in range(N_CHIP)]

    return _Exchange(grads, [_sds((N_CHIP,) + g.shape[1:], g.dtype) for g in grads], plan, N_CHIP * n)


def _chip_exchange(parts, rows=None):
    n = len(parts)

    def plan(ins, outs, pos):
        x, y, c = pos
        chips = [(1 - x, y), (x, 1 - y), (1 - x, 1 - y)]

        def src(t, ch):
            blk = ins[t].at[2 * ch[0] + ch[1]]
            return blk if rows is None else blk.at[pl.ds(rows[0], rows[1])]

        return [(src(t, ch), outs[t].at[k], (*ch, c), outs[t].at[k]) for t in range(n) for k, ch in enumerate(chips)]

    shapes = [_sds((3, p.shape[1] if rows is None else rows[1]) + p.shape[2:], p.dtype) for p in parts]
    return _Exchange(parts, shapes, plan, 3 * n)


def _pair_gather(bufs):
    n = len(bufs)

    def plan(ins, outs, pos):
        x, y, c = pos
        return [(ins[t].at[c], outs[t].at[c], (x, y, 1 - c), outs[t].at[1 - c]) for t in range(n)]

    return _Exchange(bufs, [_sds(b.shape, b.dtype) for b in bufs], plan, n, aliases={t: t for t in range(n)})


def _run_comm(comm, name):
    n_in, n_out = len(comm.arrays), len(comm.out_shapes)

    def body(*refs):
        ins, outs, scr = refs[:n_in], refs[n_in:n_in + n_out], refs[n_in + n_out:]
        comm.start(ins, outs, scr)
        comm.finish(ins, outs, scr)

    outs = _pc(body, name=name, out_shape=comm.out_shapes, in_specs=[ANY] * n_in, out_specs=[ANY] * n_out,
               input_output_aliases=dict(comm.aliases), scratch_shapes=comm.scratch,
               compiler_params=_params())(*comm.arrays)
    return list(outs)


def _all_gather(arrs, name):
    return _run_comm(_AllGather(arrs), name)


def _call(body, *, name, grid, in_specs, out_specs, out_shape, scratch_shapes, sem, args, comms=()):
    n_in, n_out, n_scr = len(in_specs), len(out_specs), len(scratch_shapes)
    c_in = [len(cm.arrays) for cm in comms]
    c_out = [len(cm.out_shapes) for cm in comms]
    c_scr = [len(cm.scratch) for cm in comms]
    aliases = {}
    for k, cm in enumerate(comms):
        for a, b in cm.aliases.items():
            aliases[n_in + sum(c_in[:k]) + a] = n_out + sum(c_out[:k]) + b

    def split(refs, counts):
        out, pos = [], 0
        for cnt in counts:
            out.append(refs[pos:pos + cnt])
            pos += cnt
        return out

    def wrapped(*refs):
        ins = refs[:n_in + sum(c_in)]
        outs = refs[len(ins):len(ins) + n_out + sum(c_out)]
        scr = refs[len(ins) + len(outs):]
        cins, couts, cscr = split(ins[n_in:], c_in), split(outs[n_out:], c_out), split(scr[n_scr:], c_scr)
        if comms:
            first = pl.program_id(0) == 0
            last = pl.program_id(0) == grid[0] - 1
            for k in range(1, len(grid)):
                first = jnp.logical_and(first, pl.program_id(k) == 0)
                last = jnp.logical_and(last, pl.program_id(k) == grid[k] - 1)

            @pl.when(first)
            def _():
                for k, cm in enumerate(comms):
                    cm.start(cins[k], couts[k], cscr[k])
        body(*ins[:n_in], *outs[:n_out], *scr[:n_scr])
        if comms:
            @pl.when(last)
            def _():
                for k, cm in enumerate(comms):
                    cm.finish(cins[k], couts[k], cscr[k])

    outs = _pc(wrapped, name=name, grid=grid,
               in_specs=list(in_specs) + [ANY] * sum(c_in), out_specs=list(out_specs) + [ANY] * sum(c_out),
               out_shape=list(out_shape) + [s for cm in comms for s in cm.out_shapes],
               scratch_shapes=list(scratch_shapes) + [s for cm in comms for s in cm.scratch],
               input_output_aliases=aliases, compiler_params=_params(*sem),
               )(*args, *[a for cm in comms for a in cm.arrays])
    outs = list(outs)
    return outs[:n_out], split(outs[n_out:], c_out)


def _row_block(r):
    for b in (512, 256, 128, 64, 32, 16, 8):
        if r % b == 0:
            return b
    return r


def _pair_add(g, recv, cj_idx, name):
    _, r, cc = g.shape
    br = _row_block(r)

    def body(cj_ref, g_ref, r_ref, own_ref, pb_ref):
        s = g_ref[...] + r_ref[...]
        pb_ref[...] = s.astype(BF16)

        @pl.when(pl.program_id(1) == cj_ref[1])
        def _():
            own_ref[...] = s[0]

    grid_spec = pltpu.PrefetchScalarGridSpec(
        num_scalar_prefetch=1, grid=(r // br, N_CHIP),
        in_specs=[pl.BlockSpec((1, br, cc), lambda i, j, cj_ref: (2 * j + cj_ref[0], i, 0)),
                  pl.BlockSpec((1, br, cc), lambda i, j, cj_ref: (j, i, 0))],
        out_specs=[pl.BlockSpec((br, cc), lambda i, j, cj_ref: (i, 0)),
                   pl.BlockSpec((1, br, cc), lambda i, j, cj_ref: (j, i, 0))])
    return _pc(body, name=name, grid_spec=grid_spec,
               out_shape=[_sds((r, cc)), _sds((N_CHIP, r, cc), BF16)],
               compiler_params=_params("arbitrary", "arbitrary"))(cj_idx, g, recv)


def _chip_add(p, qs, cj_idx, name):
    r, cc = p.shape
    nq = len(qs)
    br = _row_block(r // nq)
    nb = r // nq // br

    def body(cj_ref, p_ref, *refs):
        o_ref = refs[-1]
        if nq == 2:
            top = pl.program_id(0) < nb
            q = [jnp.where(top, refs[0][k], refs[1][k]).astype(F32) for k in range(3)]
        else:
            q = [refs[0][k].astype(F32) for k in range(3)]
        o_ref[0] = ((p_ref[...] + q[0]) + q[1]) + q[2]

    q_specs = [pl.BlockSpec((3, br, cc), lambda i, cj_ref, h=h: (0, jnp.clip(i - h * nb, 0, nb - 1), 0))
               for h in range(nq)]
    grid_spec = pltpu.PrefetchScalarGridSpec(
        num_scalar_prefetch=1, grid=(r // br,),
        in_specs=[pl.BlockSpec((br, cc), lambda i, cj_ref: (i, 0))] + q_specs,
        out_specs=pl.BlockSpec((1, br, cc), lambda i, cj_ref: (cj_ref[0], i, 0)))
    return _pc(body, name=name, grid_spec=grid_spec, out_shape=_sds((2, r, cc)),
               compiler_params=_params("arbitrary"))(cj_idx, p, *qs)


def _shard_of(both):
    return both.reshape((2 * both.shape[1],) + both.shape[2:])


def _adamw(w, g, m, v, name):
    r, cc = w.shape
    br = _row_block(r)
    if r * cc * 4 <= (1 << 20):
        br = r
    elif br * cc * 4 > (1 << 20) and br > 8:
        br = max(8, (1 << 20) // (cc * 4) // 8 * 8)
        while r % br:
            br -= 8
    c1 = 1.0 - ADAM_B1 ** ADAM_STEP
    c2 = 1.0 - ADAM_B2 ** ADAM_STEP

    def body(w_ref, g_ref, m_ref, v_ref, go_ref, d_ref, mo_ref, vo_ref):
        gg = g_ref[...]
        go_ref[...] = gg
        mn = ADAM_B1 * m_ref[...] + (1.0 - ADAM_B1) * gg
        vn = ADAM_B2 * v_ref[...] + (1.0 - ADAM_B2) * (gg * gg)
        mh = mn / c1
        vh = vn / c2
        d_ref[...] = -ADAM_LR * (mh / (jnp.sqrt(vh) + ADAM_EPS) + ADAM_WD * w_ref[...])
        mo_ref[...] = mn
        vo_ref[...] = vn

    spec = pl.BlockSpec((br, cc), lambda i: (i, 0))
    return _pc(body, name=name, grid=(r // br,), in_specs=[spec] * 4, out_specs=[spec] * 4,
               out_shape=[_sds((r, cc))] * 4, compiler_params=_params("arbitrary"))(w, g, m, v)


def _head(w_half, c8, small, w_ada, b_shard, c_ctx, ret_decay):
    ada_n = w_ada.shape[1]
    mod_sds = _sds((16, ada_n))
    ag_w, ag_c, ag_m = _AllGather([w_half]), _AllGather([c8, small]), _AllGather([mod_sds])
    n_w, n_c, n_m = len(ag_w.scratch), len(ag_c.scratch), len(ag_m.scratch)

    def body(w_ref, c_ref, s_ref, wada_ref, b_ref, cc_ref, rd_ref,
             gw_ref, call_ref, sall_ref, a_ref, modp_ref, mall_ref, lg_ref, sg_ref, *scr):
        scr_w, scr_c, scr_m = scr[:n_w], scr[n_w:n_w + n_c], scr[n_w + n_c:n_w + n_c + n_m]
        c_v, w_v, m_v, sems = scr[n_w + n_c + n_m:]
        ag_w.start((w_ref,), (gw_ref,), scr_w)
        ag_c.start((c_ref, s_ref), (call_ref, sall_ref), scr_c)
        load_w = pltpu.make_async_copy(wada_ref, w_v, sems.at[0])
        load_w.start()
        rd = rd_ref[...]
        lg_ref[...] = -_softplus(-rd)
        sg_ref[...] = _sigmoid(-rd)
        ag_c.finish((c_ref, s_ref), (call_ref, sall_ref), scr_c)
        load_c = pltpu.make_async_copy(call_ref, c_v, sems.at[1])
        load_c.start()
        load_c.wait()
        a_ref[...] = jnp.zeros_like(a_ref)
        for d in range(N_DEV):
            cd = c_v[d, 0:1, :]
            a_ref[d:d + 1, :] = cd * _sigmoid(cd)
        cc = cc_ref[...]
        a_ref[N_DEV:N_DEV + 1, :] = cc * _sigmoid(cc)
        load_w.wait()
        m_v[...] = jnp.dot(a_ref[...], w_v[...], preferred_element_type=F32,
                           precision=lax.Precision.HIGHEST) + b_ref[...]
        put = pltpu.make_async_copy(m_v, modp_ref, sems.at[2])
        put.start()
        put.wait()
        ag_m.start((modp_ref,), (mall_ref,), scr_m)
        ag_m.finish((modp_ref,), (mall_ref,), scr_m)
        ag_w.finish((w_ref,), (gw_ref,), scr_w)

    rd = jnp.broadcast_to(ret_decay.reshape(2, HEADS).T[:, :, None], (HEADS, 2, LANES))
    lane = _full((HEADS, 2, LANES))
    outs = _pc(
        body, name="head",
        in_specs=[ANY, ANY, ANY, ANY, _full((1, ada_n)), _full((1, D_MODEL)), lane],
        out_specs=[ANY, ANY, ANY, _full((16, D_MODEL)), ANY, ANY, lane, lane],
        out_shape=ag_w.out_shapes + ag_c.out_shapes + [_sds((16, D_MODEL)), mod_sds] + ag_m.out_shapes
        + [_sds((HEADS, 2, LANES))] * 2,
        scratch_shapes=ag_w.scratch + ag_c.scratch + ag_m.scratch
        + [pltpu.VMEM((N_DEV,) + c8.shape, F32), pltpu.VMEM(w_ada.shape, F32), pltpu.VMEM((16, ada_n), F32),
           pltpu.SemaphoreType.DMA((3,))],
        compiler_params=_params(),
    )(w_half, c8, small, w_ada, b_shard, c_ctx.reshape(1, D_MODEL), rd)
    gw, c_all, small_all, a16, _, mod_all, lgv, sgv = outs
    return gw, c_all, small_all, a16, mod_all, lgv, sgv


def _ada_grad(at, b):
    n = b.shape[1]
    bn = 512

    def body(a_ref, b_ref, o_ref):
        o_ref[...] = jnp.dot(a_ref[...], b_ref[...], preferred_element_type=F32, precision=lax.Precision.HIGHEST)

    return _pc(body, name="ada_grad", grid=(n // bn,),
               in_specs=[_full((D_MODEL, LANES)), pl.BlockSpec((LANES, bn), lambda i: (0, i))],
               out_specs=pl.BlockSpec((D_MODEL, bn), lambda i: (0, i)), out_shape=_sds((D_MODEL, n)),
               compiler_params=_params("arbitrary"))(at, b)


def _cctx_partial(dmc8, w_ada):
    n = w_ada.shape[1]
    bn = 512

    def body(d_ref, w_ref, o_ref):
        @pl.when(pl.program_id(0) == 0)
        def _():
            o_ref[...] = jnp.zeros_like(o_ref)
        o_ref[...] += lax.dot_general(d_ref[...], w_ref[...], (((1,), (1,)), ((), ())),
                                      preferred_element_type=F32, precision=lax.Precision.HIGHEST)

    return _pc(body, name="cctx_partial", grid=(n // bn,),
               in_specs=[pl.BlockSpec((8, bn), lambda i: (0, i)), pl.BlockSpec((D_MODEL, bn), lambda i: (0, i))],
               out_specs=_full((8, D_MODEL)), out_shape=_sds((8, D_MODEL)),
               compiler_params=_params("arbitrary"))(dmc8, w_ada)


def _cctx_final(parts, c_ctx, m, v):
    c1 = 1.0 - ADAM_B1 ** ADAM_STEP
    c2 = 1.0 - ADAM_B2 ** ADAM_STEP

    def body(p_ref, c_ref, m_ref, v_ref, g_ref, d_ref, mo_ref, vo_ref):
        s = ((p_ref[0, 0:1, :] + p_ref[2, 0:1, :]) + p_ref[4, 0:1, :]) + p_ref[6, 0:1, :]
        z = c_ref[...]
        sg = _sigmoid(z)
        gg = s * (sg * (1.0 + z * (1.0 - sg)))
        g_ref[...] = gg
        mn = ADAM_B1 * m_ref[...] + (1.0 - ADAM_B1) * gg
        vn = ADAM_B2 * v_ref[...] + (1.0 - ADAM_B2) * (gg * gg)
        d_ref[...] = -ADAM_LR * ((mn / c1) / (jnp.sqrt(vn / c2) + ADAM_EPS) + ADAM_WD * z)
        mo_ref[...] = mn
        vo_ref[...] = vn

    row = _full((1, D_MODEL))
    return _pc(body, name="cctx_final", out_shape=[_sds((1, D_MODEL))] * 4,
               in_specs=[_full(parts.shape), row, row, row], out_specs=[row] * 4,
               compiler_params=_params())(parts, c_ctx.reshape(1, D_MODEL), m.reshape(1, D_MODEL), v.reshape(1, D_MODEL))


def _rotary_tables(t_len):
    rows = t_len // GRID_W
    row = jnp.repeat(jnp.arange(rows, dtype=F32), GRID_W)
    col = jnp.tile(jnp.arange(GRID_W, dtype=F32), rows)
    n_freq = DH // 4
    inv = ROPE_BASE ** (-jnp.arange(n_freq, dtype=F32) / n_freq)
    ang = jnp.concatenate([row[:, None] * inv, col[:, None] * inv], axis=-1)
    cos, sin = jnp.cos(ang), jnp.sin(ang)
    return jnp.concatenate([cos, cos], axis=-1), jnp.concatenate([-sin, sin], axis=-1)


def _inproj_fwd(x, gn, sh, sc, w4, cos2, sin2, name, comms=()):
    t = x.shape[0]
    tm = _tile(t, True)
    nc = IN_COLS // N_CHIP

    def body(x_ref, gn_ref, sh_ref, sc_ref, w_ref, c_ref, s_ref, p_ref, xr_ref, hb_ref, p_s):
        xh, _ = _rms(x_ref[...])
        h = xh * gn_ref[...] * (1.0 + sc_ref[...]) + sh_ref[...]
        hb = h.astype(BF16)
        hb_ref[...] = hb
        for j in range(N_CHIP):
            p_s[:, nc * j:nc * (j + 1)] = _dot(hb, w_ref[j])
        cc = c_ref[...]
        ss = s_ref[...]
        for hh in range(2 * HEADS):
            blk = p_s[:, DH * hh:DH * (hh + 1)]
            rot = blk * cc + pltpu.roll(blk, DH // 2, 1) * ss
            if hh >= HEADS:
                rot = rot * K_SCALE
            p_ref[:, DH * hh:DH * (hh + 1)] = rot.astype(BF16)
        p_ref[:, 2 * RET_W:] = p_s[:, 2 * RET_W:].astype(BF16)
        xr_ref[...] = p_s[:, 4 * RET_W:4 * RET_W + LRU_W]

    row = _full((1, D_MODEL))
    outs, couts = _call(
        body, name=name, grid=(t // tm,),
        in_specs=[pl.BlockSpec((tm, D_MODEL), lambda i: (i, 0)), row, row, row, _full(w4.shape),
                  pl.BlockSpec((tm, DH), lambda i: (i, 0)), pl.BlockSpec((tm, DH), lambda i: (i, 0))],
        out_specs=[pl.BlockSpec((tm, IN_COLS), lambda i: (i, 0)), pl.BlockSpec((tm, LRU_W), lambda i: (i, 0)),
                   pl.BlockSpec((tm, D_MODEL), lambda i: (i, 0))],
        out_shape=[_sds((t, IN_COLS), BF16), _sds((t, LRU_W)), _sds((t, D_MODEL), BF16)],
        scratch_shapes=[pltpu.VMEM((tm, IN_COLS), F32)], sem=("arbitrary",),
        args=(x, gn, sh, sc, w4, cos2, sin2), comms=comms)
    return (outs, couts) if comms else outs


def _inproj_bwd(x, gn, sh, sc, w4, cos2, sin2, pieces, dres, name):
    t = x.shape[0]
    tm = _tile(t)
    nc = IN_COLS // N_CHIP

    def body(x_ref, gn_ref, sh_ref, sc_ref, w_ref, c_ref, s_ref, dqf, dqb, dkf, dkb, dvf, dvb, dg, dxr, dgt, dres_ref,
             dx_ref, dpb_ref, dgn_ref, dsh_ref, dsc_ref):
        cc = c_ref[...]
        ss = s_ref[...]
        dq = dqf[...].astype(F32) + dqb[...].astype(F32)
        dk = dkf[...].astype(F32) + dkb[...].astype(F32)
        for hh in range(HEADS):
            sl = slice(DH * hh, DH * (hh + 1))
            b = dq[:, sl]
            dpb_ref[:, sl] = (b * cc + pltpu.roll(b * ss, DH // 2, 1)).astype(BF16)
            b = dk[:, sl]
            dpb_ref[:, RET_W + DH * hh:RET_W + DH * (hh + 1)] = (
                (b * cc + pltpu.roll(b * ss, DH // 2, 1)) * K_SCALE).astype(BF16)
        dpb_ref[:, 2 * RET_W:3 * RET_W] = (dvf[...].astype(F32) + dvb[...].astype(F32)).astype(BF16)
        dpb_ref[:, 3 * RET_W:4 * RET_W] = dg[...].astype(BF16)
        dpb_ref[:, 4 * RET_W:4 * RET_W + LRU_W] = dxr[...].astype(BF16)
        dpb_ref[:, 4 * RET_W + LRU_W:IN_COLS] = dgt[...].astype(BF16)
        dh = _dot_nt(dpb_ref[:, 0:nc], w_ref[0])
        for j in range(1, N_CHIP):
            dh = dh + _dot_nt(dpb_ref[:, nc * j:nc * (j + 1)], w_ref[j])
        dx, dgn_t, dsh_t, dsc_t = _norm_mod_bwd(x_ref[...], gn_ref[...], sc_ref[...], dh)
        dx_ref[...] = dres_ref[...] + dx

        @pl.when(pl.program_id(0) == 0)
        def _():
            dgn_ref[...] = jnp.zeros_like(dgn_ref)
            dsh_ref[...] = jnp.zeros_like(dsh_ref)
            dsc_ref[...] = jnp.zeros_like(dsc_ref)
        dgn_ref[...] += dgn_t
        dsh_ref[...] += dsh_t
        dsc_ref[...] += dsc_t

    row = _full((1, D_MODEL))
    pc = pl.BlockSpec((tm, RET_W), lambda i: (i, 0))
    big = pl.BlockSpec((tm, D_MODEL), lambda i: (i, 0))
    return _pc(body, name=name, grid=(t // tm,),
               in_specs=[big, row, row, row, _full(w4.shape),
                         pl.BlockSpec((tm, DH), lambda i: (i, 0)), pl.BlockSpec((tm, DH), lambda i: (i, 0))]
               + [pc] * 9 + [big],
               out_specs=[big, pl.BlockSpec((tm, IN_COLS), lambda i: (i, 0)), row, row, row],
               out_shape=[_sds((t, D_MODEL)), _sds((t, IN_COLS), BF16), _sds((1, D_MODEL)), _sds((1, D_MODEL)),
                          _sds((1, D_MODEL))],
               compiler_params=_params("arbitrary"))(x, gn, sh, sc, w4, cos2, sin2, *pieces, dres)


def _halo_specs(t, tm):
    n8 = tm // SUBLANES
    last8 = t // SUBLANES - 1
    prev = pl.BlockSpec((SUBLANES, LRU_W), lambda i: (jnp.maximum(i * n8 - 1, 0), 0))
    main = pl.BlockSpec((tm, LRU_W), lambda i: (i, 0))
    nxt = pl.BlockSpec((SUBLANES, LRU_W), lambda i: (jnp.minimum((i + 1) * n8, last8), 0))
    return prev, main, nxt


def _with_halo(prev_ref, main_ref, next_ref, i, nt):
    prev = jnp.where(i > 0, prev_ref[...], 0.0)
    nxt = jnp.where(i < nt - 1, next_ref[...], 0.0)
    return jnp.concatenate([prev, main_ref[...], nxt], axis=0)


def _conv_fwd(xr, cw, cb, name):
    t = xr.shape[0]
    tm = _tile(t, True)
    nt = t // tm
    n = tm + 2 * SUBLANES
    mid = slice(SUBLANES, SUBLANES + tm)

    def body(p_ref, m_ref, n_ref, w_ref, b_ref, o_ref):
        xp = _with_halo(p_ref, m_ref, n_ref, pl.program_id(0), nt)
        acc = b_ref[...] + pltpu.roll(xp, 1, 0)[mid] * w_ref[0:1, :]
        acc = acc + xp[mid] * w_ref[1:2, :]
        acc = acc + pltpu.roll(xp, n - 1, 0)[mid] * w_ref[2:3, :]
        acc = acc + pltpu.roll(xp, n - 2, 0)[mid] * w_ref[3:4, :]
        o_ref[...] = acc

    return _pc(body, name=name, grid=(nt,),
               in_specs=[*_halo_specs(t, tm), _full((4, LRU_W)), _full((1, LRU_W))],
               out_specs=pl.BlockSpec((tm, LRU_W), lambda i: (i, 0)), out_shape=_sds((t, LRU_W)),
               compiler_params=_params("arbitrary"))(xr, xr, xr, cw, cb)


def _conv_bwd(dxc_a, dxc_b, xr, cw, name):
    t = xr.shape[0]
    tm = _tile(t, True)
    nt = t // tm
    n = tm + 2 * SUBLANES
    mid = slice(SUBLANES, SUBLANES + tm)

    def body(ap_ref, am_ref, an_ref, bp_ref, bm_ref, bn_ref, xp_ref, xm_ref, xn_ref, w_ref, dx_ref, dw_ref, db_ref):
        i = pl.program_id(0)
        dp = _with_halo(ap_ref, am_ref, an_ref, i, nt) + _with_halo(bp_ref, bm_ref, bn_ref, i, nt)
        xp = _with_halo(xp_ref, xm_ref, xn_ref, i, nt)
        dx = pltpu.roll(dp, n - 1, 0)[mid] * w_ref[0:1, :]
        dx = dx + dp[mid] * w_ref[1:2, :]
        dx = dx + pltpu.roll(dp, 1, 0)[mid] * w_ref[2:3, :]
        dx = dx + pltpu.roll(dp, 2, 0)[mid] * w_ref[3:4, :]
        dx_ref[...] = dx.astype(BF16)
        d = dp[mid]

        @pl.when(i == 0)
        def _():
            dw_ref[...] = jnp.zeros_like(dw_ref)
            db_ref[...] = jnp.zeros_like(db_ref)
        dw_ref[0:1, :] += _sum0(d * pltpu.roll(xp, 1, 0)[mid])
        dw_ref[1:2, :] += _sum0(d * xp[mid])
        dw_ref[2:3, :] += _sum0(d * pltpu.roll(xp, n - 1, 0)[mid])
        dw_ref[3:4, :] += _sum0(d * pltpu.roll(xp, n - 2, 0)[mid])
        db_ref[...] += _sum0(d)

    return _pc(body, name=name, grid=(nt,),
               in_specs=[*_halo_specs(t, tm), *_halo_specs(t, tm), *_halo_specs(t, tm), _full((4, LRU_W))],
               out_specs=[pl.BlockSpec((tm, LRU_W), lambda i: (i, 0)), _full((4, LRU_W)), _full((1, LRU_W))],
               out_shape=[_sds((t, LRU_W), BF16), _sds((4, LRU_W)), _sds((1, LRU_W))],
               compiler_params=_params("arbitrary"))(dxc_a, dxc_a, dxc_a, dxc_b, dxc_b, dxc_b, xr, xr, xr, cw)


def _local_scan(a, b, reverse):
    n = a.shape[0]
    row = lax.broadcasted_iota(jnp.int32, a.shape, 0) & (SUBLANES - 1)
    for s in (1, 2, 4):
        if reverse:
            a_s, b_s, ok = pltpu.roll(a, n - s, 0), pltpu.roll(b, n - s, 0), row < SUBLANES - s
        else:
            a_s, b_s, ok = pltpu.roll(a, s, 0), pltpu.roll(b, s, 0), row >= s
        b = a * jnp.where(ok, b_s, 0.0) + b
        a = a * jnp.where(ok, a_s, 1.0)
    return a, b


def _carry_scan(a_s, b_s, out_ref, carry, reverse):
    ng = a_s.shape[0] // SUBLANES
    shape = carry.shape

    def step(g, cr):
        gg = (ng - 1 - g) if reverse else g
        off = pl.multiple_of(gg * SUBLANES, SUBLANES)
        h = a_s[pl.ds(off, SUBLANES), :] * cr + b_s[pl.ds(off, SUBLANES), :]
        out_ref[pl.ds(off, SUBLANES), :] = h
        edge = h[0:1, :] if reverse else h[SUBLANES - 1:SUBLANES, :]
        return jnp.broadcast_to(edge, shape)

    return lax.fori_loop(0, ng, step, carry)


def _lru_gates(xc, wa_ref, wx_ref, ba, bx, lam):
    xb = xc.astype(BF16)
    r = _sigmoid(_dot(xb, wa_ref[...]) + ba)
    ig = _sigmoid(_dot(xb, wx_ref[...]) + bx)
    sp = _softplus(-lam)
    la = -LRU_C * r * sp
    a = jnp.exp(la)
    mult = jnp.sqrt(_one_minus_sq(la, a))
    return r, ig, sp, a, mult


def _lru_fwd(xc, wa, wx, ba, bx, lam, h0, reverse, name, comms=()):
    t = xc.shape[0]
    tm = _tile(t, True)
    nt = t // tm
    tidx = (lambda i: (nt - 1 - i, 0)) if reverse else (lambda i: (i, 0))

    def body(x_ref, wa_ref, wx_ref, ba_ref, bx_ref, lam_ref, h0_ref, h_ref, a_s, b_s, c_s):
        @pl.when(pl.program_id(0) == 0)
        def _():
            c_s[...] = jnp.broadcast_to(h0_ref[...], c_s.shape)
        xv = x_ref[...]
        _, ig, _, a, mult = _lru_gates(xv, wa_ref, wx_ref, ba_ref[...], bx_ref[...], lam_ref[...])
        al, bl = _local_scan(a, mult * (ig * xv), reverse)
        a_s[...] = al
        b_s[...] = bl
        c_s[...] = _carry_scan(a_s, b_s, h_ref, c_s[...], reverse)

    vec = _full((1, LRU_W))
    mat = _full((LRU_W, LRU_W))
    (h,), couts = _call(body, name=name, grid=(nt,),
                        in_specs=[pl.BlockSpec((tm, LRU_W), tidx), mat, mat, vec, vec, vec, vec],
                        out_specs=[pl.BlockSpec((tm, LRU_W), tidx)], out_shape=[_sds((t, LRU_W))],
                        scratch_shapes=[pltpu.VMEM((tm, LRU_W), F32), pltpu.VMEM((tm, LRU_W), F32),
                                        pltpu.VMEM((SUBLANES, LRU_W), F32)],
                        sem=("arbitrary",), args=(xc, wa, wx, ba, bx, lam, h0), comms=comms)
    return (h, couts) if comms else h


def _lru_bwd(xc, wa, wx, ba, bx, lam, h, h0, dh, reverse, name, comms=()):
    t = xc.shape[0]
    tm = _tile(t, True)
    nt = t // tm
    n8 = tm // SUBLANES
    last8 = t // SUBLANES - 1
    tidx = (lambda i: (i, 0)) if reverse else (lambda i: (nt - 1 - i, 0))
    if reverse:
        halo = pl.BlockSpec((SUBLANES, LRU_W), lambda i: (jnp.minimum((i + 1) * n8, last8), 0))
    else:
        halo = pl.BlockSpec((SUBLANES, LRU_W), lambda i: (jnp.maximum((nt - 1 - i) * n8 - 1, 0), 0))

    def body(x_ref, wa_ref, wx_ref, ba_ref, bx_ref, lam_ref, h_ref, halo_ref, h0_ref, dh_ref,
             dx_ref, dpre_ref, dba_ref, dbx_ref, dlam_ref, dh0_ref, a_s, b_s, l_s, c_s, e_s):
        i = pl.program_id(0)

        @pl.when(i == 0)
        def _():
            c_s[...] = jnp.zeros_like(c_s)
            e_s[...] = jnp.zeros_like(e_s)
            dba_ref[...] = jnp.zeros_like(dba_ref)
            dbx_ref[...] = jnp.zeros_like(dbx_ref)
            dlam_ref[...] = jnp.zeros_like(dlam_ref)
        xv = x_ref[...]
        lam = lam_ref[...]
        r, ig, sp, a, mult = _lru_gates(xv, wa_ref, wx_ref, ba_ref[...], bx_ref[...], lam)
        hv = h_ref[...]
        rowi = lax.broadcasted_iota(jnp.int32, (tm, LRU_W), 0)
        edge_a = jnp.broadcast_to(e_s[0:1, :], (tm, LRU_W))
        h0b = jnp.broadcast_to(h0_ref[...], (tm, LRU_W))
        if reverse:
            a_sh = jnp.where(rowi == 0, edge_a, pltpu.roll(a, 1, 0))
            hin_edge = jnp.where(i == nt - 1, h0b, jnp.broadcast_to(halo_ref[0:1, :], (tm, LRU_W)))
            h_in = jnp.where(rowi == tm - 1, hin_edge, pltpu.roll(hv, tm - 1, 0))
        else:
            a_sh = jnp.where(rowi == tm - 1, edge_a, pltpu.roll(a, tm - 1, 0))
            hin_edge = jnp.where(i == nt - 1, h0b, jnp.broadcast_to(halo_ref[SUBLANES - 1:SUBLANES, :], (tm, LRU_W)))
            h_in = jnp.where(rowi == 0, hin_edge, pltpu.roll(hv, 1, 0))
        al, bl = _local_scan(a_sh, dh_ref[...], not reverse)
        a_s[...] = al
        b_s[...] = bl
        c_s[...] = _carry_scan(a_s, b_s, l_s, c_s[...], not reverse)
        e_s[...] = jnp.broadcast_to(a[tm - 1:tm, :] if reverse else a[0:1, :], e_s.shape)
        lmb = l_s[...]
        da = lmb * h_in
        ixc = ig * xv
        dmult = lmb * ixc
        dixc = lmb * mult
        dla = da * a - dmult * (a * a) / mult
        dpr = dla * (-LRU_C * sp) * r * (1.0 - r)
        dpi = dixc * xv * ig * (1.0 - ig)
        dprb = dpr.astype(BF16)
        dpib = dpi.astype(BF16)
        dpre_ref[:, 0:LRU_W] = dprb
        dpre_ref[:, LRU_W:2 * LRU_W] = dpib
        dx_ref[...] = dixc * ig + _dot_nt(dprb, wa_ref[...]) + _dot_nt(dpib, wx_ref[...])
        dba_ref[...] += _sum0(dpr)
        dbx_ref[...] += _sum0(dpi)
        dlam_ref[...] += _sum0(dla * (-LRU_C * r)) * (-_sigmoid(-lam))

        @pl.when(i == nt - 1)
        def _():
            al0 = a * lmb
            dh0_ref[...] = al0[tm - 1:tm, :] if reverse else al0[0:1, :]

    vec = _full((1, LRU_W))
    mat = _full((LRU_W, LRU_W))
    tile = pl.BlockSpec((tm, LRU_W), tidx)
    return _call(body, name=name, grid=(nt,),
                 in_specs=[tile, mat, mat, vec, vec, vec, tile, halo, vec, tile],
                 out_specs=[tile, pl.BlockSpec((tm, 2 * LRU_W), tidx), vec, vec, vec, vec],
                 out_shape=[_sds((t, LRU_W)), _sds((t, 2 * LRU_W), BF16), _sds((1, LRU_W)), _sds((1, LRU_W)),
                            _sds((1, LRU_W)), _sds((1, LRU_W))],
                 scratch_shapes=[pltpu.VMEM((tm, LRU_W), F32), pltpu.VMEM((tm, LRU_W), F32),
                                 pltpu.VMEM((tm, LRU_W), F32), pltpu.VMEM((SUBLANES, LRU_W), F32),
                                 pltpu.VMEM((SUBLANES, LRU_W), F32)],
                 sem=("arbitrary",), args=(xc, wa, wx, ba, bx, lam, h, h, h0, dh), comms=comms)


def _decay_tables(lg, reverse):
    ci = lax.broadcasted_iota(jnp.int32, (CHUNK, CHUNK), 0).astype(F32)
    mi = lax.broadcasted_iota(jnp.int32, (CHUNK, CHUNK), 1).astype(F32)
    rel = (mi - ci) if reverse else (ci - mi)
    relc = jnp.maximum(rel, 0.0)
    lg_c = jnp.concatenate([lg] * (CHUNK // LANES), axis=1)
    dm = jnp.where(rel >= 0, jnp.exp(lg_c * relc), 0.0)
    cd = lax.broadcasted_iota(jnp.int32, (CHUNK, DH), 0).astype(F32)
    pq, ps = (CHUNK - cd, cd) if reverse else (cd + 1.0, CHUNK - 1.0 - cd)
    return relc, dm, jnp.exp(lg * pq), jnp.exp(lg * ps), jnp.exp(lg * float(CHUNK)), pq, ps


def _ret_fwd(proj, lgv, s0f, s0b, comms=()):
    t = proj.shape[0]
    n = t // CHUNK

    def one(q, k, v, lg, s_s, hh, o_ref, sp_ref, reverse):
        _, dm, wq, ws, g, _, _ = _decay_tables(lg, reverse)
        vb = v.astype(BF16)
        p = _dot_nt(q.astype(BF16), k.astype(BF16)) * dm
        s = s_s[hh]
        sp_ref[hh, 0] = s
        o_ref[:, DH * hh:DH * (hh + 1)] = _dot(p.astype(BF16), vb) + _dot((q * wq).astype(BF16), s.astype(BF16))
        s_s[hh] = g * s + _dot_tn((k * ws).astype(BF16), vb)

    def body(qf, kf, vf, qb, kb, vb, lg_ref, s0f_ref, s0b_ref, of_ref, ob_ref, spf_ref, spb_ref, sf_s, sb_s):
        @pl.when(pl.program_id(0) == 0)
        def _():
            sf_s[...] = s0f_ref[...]
            sb_s[...] = s0b_ref[...]
        for hh in range(HEADS):
            sl = slice(DH * hh, DH * (hh + 1))
            one(qf[:, sl].astype(F32), kf[:, sl].astype(F32), vf[:, sl], lg_ref[hh, 0:1, :], sf_s, hh, of_ref, spf_ref,
                False)
            one(qb[:, sl].astype(F32), kb[:, sl].astype(F32), vb[:, sl], lg_ref[hh, 1:2, :], sb_s, hh, ob_ref, spb_ref,
                True)

    blk = (CHUNK, RET_W)
    fw = [pl.BlockSpec(blk, lambda i, o=o: (i, o)) for o in range(3)]
    bw = [pl.BlockSpec(blk, lambda i, o=o: (n - 1 - i, o)) for o in range(3)]
    st = _full((HEADS, DH, DH))
    return _call(body, name="ret_fwd", grid=(n,),
                 in_specs=fw + bw + [_full((HEADS, 2, LANES)), st, st],
                 out_specs=[pl.BlockSpec(blk, lambda i: (i, 0)), pl.BlockSpec(blk, lambda i: (n - 1 - i, 0)),
                            pl.BlockSpec((HEADS, 1, DH, DH), lambda i: (0, i, 0, 0)),
                            pl.BlockSpec((HEADS, 1, DH, DH), lambda i: (0, n - 1 - i, 0, 0))],
                 out_shape=[_sds((t, RET_W)), _sds((t, RET_W)), _sds((HEADS, n, DH, DH)), _sds((HEADS, n, DH, DH))],
                 scratch_shapes=[pltpu.VMEM((HEADS, DH, DH), F32), pltpu.VMEM((HEADS, DH, DH), F32)],
                 sem=("arbitrary",), args=(proj, proj, proj, proj, proj, proj, lgv, s0f, s0b), comms=comms)


def _ret_bwd(proj, lgv, sgv, sprev, do, reverse, name, comms=()):
    t = proj.shape[0]
    n = t // CHUNK
    d = 1 if reverse else 0
    cidx = (lambda i: i) if reverse else (lambda i: n - 1 - i)

    def body(q_ref, k_ref, v_ref, lg_ref, sg_ref, s_ref, do_ref, dq_ref, dk_ref, dv_ref, ds0_ref, drd_ref, ds_s, acc_s):
        i = pl.program_id(0)

        @pl.when(i == 0)
        def _():
            ds_s[...] = jnp.zeros_like(ds_s)
            acc_s[...] = jnp.zeros_like(acc_s)
        for hh in range(HEADS):
            sl = slice(DH * hh, DH * (hh + 1))
            relc, dm, wq, ws, g, pq, ps = _decay_tables(lg_ref[hh, d:d + 1, :], reverse)
            qb, kb, vb = q_ref[:, sl], k_ref[:, sl], v_ref[:, sl]
            q, k = qb.astype(F32), kb.astype(F32)
            p = _dot_nt(qb, kb) * dm
            s = s_ref[hh, 0]
            dob = do_ref[:, sl].astype(BF16)
            dsn = ds_s[hh]
            dsb = dsn.astype(BF16)
            dv_ref[:, sl] = (_dot_tn(p.astype(BF16), dob) + _dot((k * ws).astype(BF16), dsb)).astype(BF16)
            dp = _dot_nt(dob, vb)
            dab = (dp * dm).astype(BF16)
            xq = _dot_nt(dob, s.astype(BF16))
            yk = _dot_nt(vb, dsb)
            dq_ref[:, sl] = (_dot(dab, kb) + xq * wq).astype(BF16)
            dk_ref[:, sl] = (_dot_tn(dab, qb) + yk * ws).astype(BF16)
            ds_s[hh] = g * dsn + _dot_tn((q * wq).astype(BF16), dob)
            s_mask = _sum0(dp * p * relc)
            part = (sum(s_mask[:, LANES * u:LANES * (u + 1)] for u in range(CHUNK // LANES))
                    + _sum0(xq * q * wq * pq) + _sum0(yk * k * ws * ps) + _sum0(dsn * s) * g * float(CHUNK))
            acc_s[hh] += jnp.broadcast_to(part, (SUBLANES, LANES))

        @pl.when(i == n - 1)
        def _():
            ds0_ref[...] = ds_s[...]
            for hh in range(HEADS):
                tot = jnp.sum(acc_s[hh, 0:1, :], axis=1, keepdims=True)
                drd_ref[hh] = jnp.broadcast_to(tot, (SUBLANES, LANES)) * sg_ref[hh, d:d + 1, :]

    blk = (CHUNK, RET_W)
    qkv = [pl.BlockSpec(blk, lambda i, o=o: (cidx(i), o)) for o in range(3)]
    hc = pl.BlockSpec(blk, lambda i: (cidx(i), 0))
    lane = _full((HEADS, 2, LANES))
    return _call(body, name=name, grid=(n,),
                 in_specs=qkv + [lane, lane, pl.BlockSpec((HEADS, 1, DH, DH), lambda i: (0, cidx(i), 0, 0)), hc],
                 out_specs=[hc, hc, hc, _full((HEADS, DH, DH)), _full((HEADS, SUBLANES, LANES))],
                 out_shape=[_sds((t, RET_W), BF16)] * 3 + [_sds((HEADS, DH, DH)), _sds((HEADS, SUBLANES, LANES))],
                 scratch_shapes=[pltpu.VMEM((HEADS, DH, DH), F32), pltpu.VMEM((HEADS, SUBLANES, LANES), F32)],
                 sem=("arbitrary",), args=(proj, proj, proj, lgv, sgv, sprev, do), comms=comms)


def _ctx_weights(lg, l_len, reverse):
    pos = lax.broadcasted_iota(jnp.int32, (l_len, DH), 0).astype(F32)
    steps = pos if reverse else (l_len - 1.0 - pos)
    return jnp.exp(lg * steps), steps


def _ctx_state_fwd(projc, lgv):
    l_len = projc.shape[0]

    def body(k_ref, v_ref, lg_ref, sf_ref, sb_ref):
        k = k_ref[...]
        vb = v_ref[...].astype(BF16)
        for d, o_ref in ((0, sf_ref), (1, sb_ref)):
            w, _ = _ctx_weights(lg_ref[0, d:d + 1, :], l_len, d == 1)
            o_ref[0] = _dot_tn((k * w).astype(BF16), vb)

    st = pl.BlockSpec((1, DH, DH), lambda h: (h, 0, 0))
    return _pc(body, name="ctx_state_fwd", grid=(HEADS,),
               in_specs=[pl.BlockSpec((l_len, DH), lambda h: (0, HEADS + h)),
                         pl.BlockSpec((l_len, DH), lambda h: (0, 2 * HEADS + h)),
                         pl.BlockSpec((1, 2, LANES), lambda h: (h, 0, 0))],
               out_specs=[st, st], out_shape=[_sds((HEADS, DH, DH))] * 2,
               compiler_params=_params("arbitrary"))(projc, projc, lgv)


def _ctx_state_bwd(projc, lgv, sgv, dsf, dsb):
    l_len = projc.shape[0]

    def body(k_ref, v_ref, lg_ref, sg_ref, dsf_ref, dsb_ref, dk_ref, dv_ref, drd_ref):
        k = k_ref[...]
        vb = v_ref[...].astype(BF16)
        dk = jnp.zeros((l_len, DH), F32)
        dv = jnp.zeros((l_len, DH), F32)
        rows = []
        for d, ds_ref in ((0, dsf_ref), (1, dsb_ref)):
            w, steps = _ctx_weights(lg_ref[0, d:d + 1, :], l_len, d == 1)
            dsb16 = ds_ref[0].astype(BF16)
            dkw = _dot_nt(vb, dsb16)
            dk = dk + dkw * w
            dv = dv + _dot((k * w).astype(BF16), dsb16)
            tot = jnp.sum(_sum0(dkw * k * w * steps), axis=1, keepdims=True)
            rows.append(jnp.broadcast_to(tot, (1, LANES)) * sg_ref[0, d:d + 1, :])
        dk_ref[...] = dk.astype(BF16)
        dv_ref[...] = dv.astype(BF16)
        rid = lax.broadcasted_iota(jnp.int32, (SUBLANES, LANES), 0)
        drd_ref[0] = jnp.where(rid == 0, rows[0], jnp.where(rid == 1, rows[1], 0.0))

    st = pl.BlockSpec((1, DH, DH), lambda h: (h, 0, 0))
    lane = pl.BlockSpec((1, 2, LANES), lambda h: (h, 0, 0))
    hc = pl.BlockSpec((l_len, DH), lambda h: (0, h))
    return _pc(body, name="ctx_state_bwd", grid=(HEADS,),
               in_specs=[pl.BlockSpec((l_len, DH), lambda h: (0, HEADS + h)),
                         pl.BlockSpec((l_len, DH), lambda h: (0, 2 * HEADS + h)), lane, lane, st, st],
               out_specs=[hc, hc, pl.BlockSpec((1, SUBLANES, LANES), lambda h: (h, 0, 0))],
               out_shape=[_sds((l_len, RET_W), BF16), _sds((l_len, RET_W), BF16), _sds((HEADS, SUBLANES, LANES))],
               compiler_params=_params("arbitrary"))(projc, projc, lgv, sgv, dsf, dsb)


G_BLOCK = (3 * RET_W) // RET_W
GATE_BLOCK = (4 * RET_W + LRU_W) // LRU_W


def _head_norm(y):
    yc = y - jnp.mean(y, axis=-1, keepdims=True)
    rs = lax.rsqrt(jnp.mean(yc * yc, axis=-1, keepdims=True) + EPS)
    return yc * rs, rs


def _gelu_parts(z):
    th = jnp.tanh(GELU_K * (z + GELU_C * z * z * z))
    return 0.5 * z * (1.0 + th), th


def _mix_fwd(o_f, o_b, proj, hf, hb, w_out, x, g1):
    t = x.shape[0]
    tm = _tile(t, True)

    def body(of_ref, ob_ref, g_ref, gt_ref, hf_ref, hb_ref, w_ref, x_ref, g1_ref, x1_ref, cat_ref):
        o = of_ref[...] + ob_ref[...]
        g = g_ref[...].astype(F32)
        for hh in range(HEADS):
            sl = slice(DH * hh, DH * (hh + 1))
            nrm, _ = _head_norm(o[:, sl])
            gh = g[:, sl]
            cat_ref[:, sl] = (gh * _sigmoid(gh) * nrm).astype(BF16)
        gel, _ = _gelu_parts(gt_ref[...].astype(F32))
        cat_ref[:, RET_W:] = ((hf_ref[...] + hb_ref[...]) * gel).astype(BF16)
        x1_ref[...] = x_ref[...] + g1_ref[...] * _dot(cat_ref[...], w_ref[...])

    half = pl.BlockSpec((tm, RET_W), lambda i: (i, 0))
    big = pl.BlockSpec((tm, D_MODEL), lambda i: (i, 0))
    return _pc(body, name="mix_fwd", grid=(t // tm,),
               in_specs=[half, half, pl.BlockSpec((tm, RET_W), lambda i: (i, G_BLOCK)),
                         pl.BlockSpec((tm, LRU_W), lambda i: (i, GATE_BLOCK)), half, half,
                         _full((D_MODEL, D_MODEL)), big, _full((1, D_MODEL))],
               out_specs=[big, big], out_shape=[_sds((t, D_MODEL)), _sds((t, D_MODEL), BF16)],
               compiler_params=_params("arbitrary"))(o_f, o_b, proj, proj, hf, hb, w_out, x, g1)


def _mix_bwd(o_f, o_b, proj, hf, hb, w_out, cat, dx1, g1, comms=()):
    t = dx1.shape[0]
    tm = _tile(t, True)

    def body(of_ref, ob_ref, g_ref, gt_ref, hf_ref, hb_ref, w_ref, cat_ref, dx1_ref, g1_ref,
             do_ref, dhs_ref, dg_ref, dgt_ref, dyb_ref, dg1_ref):
        dx1v = dx1_ref[...]
        y = _dot(cat_ref[...], w_ref[...])

        @pl.when(pl.program_id(0) == 0)
        def _():
            dg1_ref[...] = jnp.zeros_like(dg1_ref)
        dg1_ref[...] += _sum0(dx1v * y)
        dyb = (g1_ref[...] * dx1v).astype(BF16)
        dyb_ref[...] = dyb
        dcat = _dot_nt(dyb, w_ref[...])
        o = of_ref[...] + ob_ref[...]
        g = g_ref[...].astype(F32)
        for hh in range(HEADS):
            sl = slice(DH * hh, DH * (hh + 1))
            nrm, rs = _head_norm(o[:, sl])
            gh = g[:, sl]
            sg = _sigmoid(gh)
            dret = dcat[:, sl]
            dg_ref[:, sl] = (dret * nrm * (sg * (1.0 + gh * (1.0 - sg)))).astype(BF16)
            dn = dret * (gh * sg)
            dyc = rs * (dn - nrm * jnp.mean(dn * nrm, axis=-1, keepdims=True))
            do_ref[:, sl] = (dyc - jnp.mean(dyc, axis=-1, keepdims=True)).astype(BF16)
        z = gt_ref[...].astype(F32)
        gel, th = _gelu_parts(z)
        dlru = dcat[:, RET_W:]
        dhs_ref[...] = dlru * gel
        dgel = 0.5 * (1.0 + th) + 0.5 * z * (1.0 - th * th) * GELU_K * (1.0 + 3.0 * GELU_C * z * z)
        dgt_ref[...] = (dlru * (hf_ref[...] + hb_ref[...]) * dgel).astype(BF16)

    half = pl.BlockSpec((tm, RET_W), lambda i: (i, 0))
    big = pl.BlockSpec((tm, D_MODEL), lambda i: (i, 0))
    return _call(body, name="mix_bwd", grid=(t // tm,),
                 in_specs=[half, half, pl.BlockSpec((tm, RET_W), lambda i: (i, G_BLOCK)),
                           pl.BlockSpec((tm, LRU_W), lambda i: (i, GATE_BLOCK)), half, half,
                           _full((D_MODEL, D_MODEL)), big, big, _full((1, D_MODEL))],
                 out_specs=[half, half, half, half, big, _full((1, D_MODEL))],
                 out_shape=[_sds((t, RET_W), BF16), _sds((t, RET_W)), _sds((t, RET_W), BF16), _sds((t, RET_W), BF16),
                            _sds((t, D_MODEL), BF16), _sds((1, D_MODEL))],
                 scratch_shapes=[], sem=("arbitrary",), args=(o_f, o_b, proj, proj, hf, hb, w_out, cat, dx1, g1),
                 comms=comms)


def _mlp(x1, n2g, sh2, sc2, g2, fg, w1_parts, w2_parts, tgt):
    t = x1.shape[0]
    tm = _tile(t)
    hb_ = MLP_H // N_CHIP
    q_rows = hb_ // 4
    n_cp = 4 * N_DEV

    def body(x1_ref, n2g_ref, sh2_ref, sc2_ref, g2_ref, fg_ref, w1a, w1b, w2a, w2b, tgt_ref,
             dx1_ref, h2b_ref, ab_ref, dub_ref, dmb_ref, dsc_ref, dsh_ref, dg2_ref, dn2_ref, dfg_ref, loss_ref,
             w1_s, w2_s, r_s, sems):
        @pl.when(pl.program_id(0) == 0)
        def _():
            cps = []
            for p, parts in enumerate(((w1a, w2a), (w1b, w2b))):
                for d in range(N_DEV):
                    rows = pl.ds(2 * q_rows * (d % 2) + q_rows * p, q_rows)
                    for src, dst in zip(parts, (w1_s, w2_s)):
                        cps.append(pltpu.make_async_copy(src.at[d], dst.at[d // 2, rows], sems.at[len(cps)]))
            for cp in cps:
                cp.start()
            for r in (dsc_ref, dsh_ref, dg2_ref, dn2_ref, dfg_ref, loss_ref):
                r[...] = jnp.zeros_like(r)
            for cp in cps:
                cp.wait()
        x1v = x1_ref[...]
        n2g, sc2, g2, fg = n2g_ref[...], sc2_ref[...], g2_ref[...], fg_ref[...]
        xh, _ = _rms(x1v)
        h2b = (xh * n2g * (1.0 + sc2) + sh2_ref[...]).astype(BF16)
        h2b_ref[...] = h2b
        m = jnp.zeros((tm, D_MODEL), F32)
        for j in range(N_CHIP):
            sl = slice(hb_ * j, hb_ * (j + 1))
            r = jnp.maximum(_dot(h2b, w1_s[j]), 0.0)
            r_s[:, sl] = r
            ab = (r * r).astype(BF16)
            ab_ref[:, sl] = ab
            m = m + _dot(ab, w2_s[j])
        x2 = x1v + g2 * m
        x2h, r2 = _rms(x2)
        err = x2h * fg - tgt_ref[...]
        loss_ref[...] += _sum0(err * err)
        dout = err * (1.0 / D_MODEL)
        dfg_ref[...] += _sum0(dout * x2h)
        dxh = dout * fg
        dx2 = r2 * (dxh - x2h * jnp.mean(dxh * x2h, axis=-1, keepdims=True))
        dg2_ref[...] += _sum0(dx2 * m)
        dmb = (g2 * dx2).astype(BF16)
        dmb_ref[...] = dmb
        dh2 = jnp.zeros((tm, D_MODEL), F32)
        for j in range(N_CHIP):
            sl = slice(hb_ * j, hb_ * (j + 1))
            dub = (_dot_nt(dmb, w2_s[j]) * (2.0 * r_s[:, sl])).astype(BF16)
            dub_ref[:, sl] = dub
            dh2 = dh2 + _dot_nt(dub, w1_s[j])
        dx, dn2_t, dsh_t, dsc_t = _norm_mod_bwd(x1v, n2g, sc2, dh2)
        dx1_ref[...] = dx2 + dx
        dn2_ref[...] += dn2_t
        dsh_ref[...] += dsh_t
        dsc_ref[...] += dsc_t

        @pl.when(pl.program_id(0) == t // tm - 1)
        def _():
            tot = jnp.sum(loss_ref[...], axis=1, keepdims=True) * (0.5 / D_MODEL)
            loss_ref[...] = jnp.broadcast_to(tot, loss_ref.shape)

    row = _full((1, D_MODEL))
    big = pl.BlockSpec((tm, D_MODEL), lambda i: (i, 0))
    wide = pl.BlockSpec((tm, MLP_H), lambda i: (i, 0))
    return _pc(body, name="mlp", grid=(t // tm,),
               in_specs=[big, row, row, row, row, row, ANY, ANY, ANY, ANY, big],
               out_specs=[big, big, wide, wide, big, row, row, row, row, row, row],
               out_shape=[_sds((t, D_MODEL)), _sds((t, D_MODEL), BF16), _sds((t, MLP_H), BF16), _sds((t, MLP_H), BF16),
                          _sds((t, D_MODEL), BF16)] + [_sds((1, D_MODEL))] * 6,
               scratch_shapes=[pltpu.VMEM((N_CHIP, D_MODEL, hb_), BF16), pltpu.VMEM((N_CHIP, hb_, D_MODEL), BF16),
                               pltpu.VMEM((tm, MLP_H), F32), pltpu.SemaphoreType.DMA((n_cp,))],
               compiler_params=_params("arbitrary"))(x1, n2g, sh2, sc2, g2, fg, *w1_parts, *w2_parts, tgt)


def _tn(a, b, nj, a_blocked, b_blocked, name, extra=None, comms=()):
    t = a.shape[0]
    m = a.shape[1] // (nj if a_blocked else 1)
    n = b.shape[1] // (nj if b_blocked else 1)
    bk = next((b for b in (2048, 1024, 512) if t % b == 0), t)
    nk = t // bk
    a_col = (lambda j: j) if a_blocked else (lambda j: 0)
    b_col = (lambda j: j) if b_blocked else (lambda j: 0)
    in_specs = [pl.BlockSpec((bk, m), lambda j, k: (k, a_col(j))), pl.BlockSpec((bk, n), lambda j, k: (k, b_col(j)))]
    args = [a, b]
    if extra is not None:
        a2, b2 = extra
        t2 = a2.shape[0]
        in_specs += [pl.BlockSpec((t2, m), lambda j, k: (0, a_col(j))),
                     pl.BlockSpec((t2, n), lambda j, k: (0, b_col(j)))]
        args += [a2, b2]

    def body(*refs):
        a_ref, b_ref = refs[0], refs[1]
        o_ref, acc = refs[-2], refs[-1]
        k = pl.program_id(1)

        @pl.when(k == 0)
        def _():
            acc[...] = jnp.zeros_like(acc)
        acc[...] += _dot_tn(a_ref[...].astype(BF16), b_ref[...].astype(BF16))

        @pl.when(k == nk - 1)
        def _():
            if extra is not None:
                acc[...] += _dot_tn(refs[2][...].astype(BF16), refs[3][...].astype(BF16))
            o_ref[0] = acc[...]

    (out,), couts = _call(body, name=name, grid=(nj, nk), in_specs=in_specs,
                          out_specs=[pl.BlockSpec((1, m, n), lambda j, k: (j, 0, 0))], out_shape=[_sds((nj, m, n))],
                          scratch_shapes=[pltpu.VMEM((m, n), F32)], sem=("arbitrary", "arbitrary"), args=args,
                          comms=comms)
    return (out, couts) if comms else out


ROW_LOSS = 0
ROW_DMOD = 1
ROW_DMODC = 7
ROW_N1, ROW_N2, ROW_FG, ROW_CB = 9, 10, 11, 12
ROW_BA, ROW_BX, ROW_LAM = 13, 15, 17
ROW_CW = 20
ROW_RD = 24
SLAB_ROWS = 32
SEG = D_MODEL // 2


def _pack_small(rows, drd, cw2, cb2, lru2, gates):
    n_rows, n_lru = len(rows), len(lru2)

    def body(*refs):
        r = refs[:n_rows]
        drd_f, drd_b, drd_c, cw_a, cw_b, cb_a, cb_b = refs[n_rows:n_rows + 7]
        lru = refs[n_rows + 7:n_rows + 7 + n_lru]
        gf_ref, gb_ref, slab, ga, gx = refs[n_rows + 7 + n_lru:]
        slab[...] = jnp.zeros_like(slab)
        slab[ROW_LOSS:ROW_LOSS + 1, :] = r[0][...]
        for k in range(N_MOD):
            slab[ROW_DMOD + k:ROW_DMOD + k + 1, :] = r[1 + k][...]
        slab[ROW_DMODC:ROW_DMODC + 1, :] = r[7][...]
        slab[ROW_DMODC + 1:ROW_DMODC + 2, :] = r[8][...]
        slab[ROW_N1:ROW_N1 + 1, :] = r[9][...] + r[10][...]
        slab[ROW_N2:ROW_N2 + 1, :] = r[11][...]
        slab[ROW_FG:ROW_FG + 1, :] = r[12][...]
        slab[ROW_CB:ROW_CB + 1, 0:LRU_W] = cb_a[...] + cb_b[...]
        for k, row in enumerate((ROW_BA, ROW_BA + 1, ROW_BX, ROW_BX + 1, ROW_LAM, ROW_LAM + 1)):
            slab[row:row + 1, 0:LRU_W] = lru[2 * k][...] + lru[2 * k + 1][...]
        slab[ROW_CW:ROW_CW + 4, 0:LRU_W] = cw_a[...] + cw_b[...]
        for h in range(HEADS):
            slab[ROW_RD + h:ROW_RD + h + 1, 0:LANES] = drd_f[h, 0:1, :] + drd_c[h, 0:1, :]
            slab[ROW_RD + HEADS + h:ROW_RD + HEADS + h + 1, 0:LANES] = drd_b[h, 0:1, :] + drd_c[h, 1:2, :]
        for d, g_ref in enumerate((gf_ref, gb_ref)):
            for n in range(LRU_BLOCKS):
                blk = slice(LRU_BD * n, LRU_BD * (n + 1))
                ga[blk, LRU_BD * d:LRU_BD * (d + 1)] = g_ref[0, blk, blk].astype(BF16)
                gx[blk, LRU_BD * d:LRU_BD * (d + 1)] = g_ref[1, blk, blk].astype(BF16)

    args = list(rows) + list(drd) + list(cw2) + list(cb2) + list(lru2) + list(gates)
    gate_shape = (LRU_W, 2 * LRU_BD)
    return _pc(body, name="pack_small", in_specs=[_full(a.shape) for a in args],
               out_specs=[_full((SLAB_ROWS, D_MODEL)), _full(gate_shape), _full(gate_shape)],
               out_shape=[_sds((SLAB_ROWS, D_MODEL)), _sds(gate_shape, BF16), _sds(gate_shape, BF16)],
               compiler_params=_params())(*args)


def _adam_math(w, g, m, v):
    mn = ADAM_B1 * m + (1.0 - ADAM_B1) * g
    vn = ADAM_B2 * v + (1.0 - ADAM_B2) * (g * g)
    mh = mn / (1.0 - ADAM_B1 ** ADAM_STEP)
    vh = vn / (1.0 - ADAM_B2 ** ADAM_STEP)
    return -ADAM_LR * (mh / (jnp.sqrt(vh) + ADAM_EPS) + ADAM_WD * w), mn, vn


SMALL_PARAMS = ("b_ada", "norm1_g", "norm2_g", "final_g", "ret_decay", "conv_w", "conv_b", "lru_wa", "lru_ba", "lru_wx",
                "lru_bx", "lru_lambda")


def _finalize_small(chip_idx, slab_all, ga_all, gx_all, wmv):
    n_p = len(SMALL_PARAMS)
    flat = [a for nm in SMALL_PARAMS for a in wmv[nm]]
    ada_n = N_MOD * D_MODEL // N_CHIP

    def body(c_ref, slab_ref, ga_ref, gx_ref, *refs):
        prm = {nm: refs[3 * k:3 * k + 3] for k, nm in enumerate(SMALL_PARAMS)}
        outs = {nm: refs[3 * n_p + 4 * k:3 * n_p + 4 * k + 4] for k, nm in enumerate(SMALL_PARAMS)}
        b128_ref, dmc_ref, loss_ref = refs[3 * n_p + 4 * n_p:]
        chip = c_ref[0]

        def pick(fn):
            acc = fn(0)
            for j in range(1, N_CHIP):
                acc = jnp.where(chip == j, fn(j), acc)
            return acc

        tot = slab_ref[0]
        for d in range(1, N_DEV):
            tot = tot + slab_ref[d]

        def update(nm, g, sl=None, rows=None):
            w_ref, m_ref, v_ref = prm[nm]
            g_ref, d_ref, mo_ref, vo_ref = outs[nm]
            ix = (slice(None) if rows is None else rows, slice(None) if sl is None else sl)
            dl, mn, vn = _adam_math(w_ref[ix], g, m_ref[ix], v_ref[ix])
            g_ref[ix] = g
            d_ref[ix] = dl
            mo_ref[ix] = mn
            vo_ref[ix] = vn

        loss_ref[...] = jnp.broadcast_to(tot[ROW_LOSS:ROW_LOSS + 1, 0:LANES], (SUBLANES, LANES))
        for k in range(N_MOD):
            g = tot[ROW_DMOD + k:ROW_DMOD + k + 1, :]
            if k < 2:
                g = g + tot[ROW_DMODC + k:ROW_DMODC + k + 1, :]
            update("b_ada", g, slice(D_MODEL * k, D_MODEL * (k + 1)))
        update("norm1_g", tot[ROW_N1:ROW_N1 + 1, :])
        update("norm2_g", tot[ROW_N2:ROW_N2 + 1, :])
        update("final_g", tot[ROW_FG:ROW_FG + 1, :])
        update("ret_decay", tot[ROW_RD:ROW_RD + SUBLANES, 0:LANES])
        update("conv_b", tot[ROW_CB:ROW_CB + 1, 0:LRU_W])
        update("conv_w", pick(lambda j: tot[ROW_CW:ROW_CW + 4, LANES * j:LANES * (j + 1)]))
        for nm, row in (("lru_ba", ROW_BA), ("lru_bx", ROW_BX), ("lru_lambda", ROW_LAM)):
            update(nm, pick(lambda j, row=row: tot[row:row + 2, LANES * j:LANES * (j + 1)]))
        for nm, g_all in (("lru_wa", ga_ref), ("lru_wx", gx_ref)):
            for dr in range(2):
                lanes = slice(LRU_BD * dr, LRU_BD * (dr + 1))
                g = g_all[0, :, lanes].astype(F32)
                for d in range(1, N_DEV):
                    g = g + g_all[d, :, lanes].astype(F32)
                update(nm, g, rows=slice(LRU_W * dr, LRU_W * (dr + 1)))

        def seg(rows6, s):
            return rows6[s // 2][:, SEG * (s % 2):SEG * (s % 2 + 1)]

        b128_ref[...] = jnp.zeros_like(b128_ref)
        dmc_ref[...] = jnp.zeros_like(dmc_ref)
        zero = jnp.zeros((1, D_MODEL), F32)
        ctx6 = [tot[ROW_DMODC:ROW_DMODC + 1, :], tot[ROW_DMODC + 1:ROW_DMODC + 2, :]] + [zero] * (N_MOD - 2)
        for q in range(ada_n // SEG):
            cols = slice(SEG * q, SEG * (q + 1))
            for d in range(N_DEV):
                rows6 = [slab_ref[d, ROW_DMOD + k:ROW_DMOD + k + 1, :] for k in range(N_MOD)]
                b128_ref[d:d + 1, cols] = pick(lambda j, rows6=rows6: seg(rows6, 3 * j + q))
            c = pick(lambda j: seg(ctx6, 3 * j + q))
            b128_ref[N_DEV:N_DEV + 1, cols] = c
            dmc_ref[0:1, cols] = c

    out_shape = []
    for nm in SMALL_PARAMS:
        out_shape += [_sds(wmv[nm][0].shape)] * 4
    out_shape += [_sds((LANES, ada_n)), _sds((SUBLANES, ada_n)), _sds((SUBLANES, LANES))]
    args = [slab_all, ga_all, gx_all] + flat
    grid_spec = pltpu.PrefetchScalarGridSpec(
        num_scalar_prefetch=1, grid=(1,), in_specs=[_full(a.shape) for a in args],
        out_specs=[_full(s.shape) for s in out_shape])
    outs = _pc(body, name="finalize_small", grid_spec=grid_spec, out_shape=out_shape,
               compiler_params=_params("arbitrary"))(chip_idx, *args)
    res = {nm: tuple(outs[4 * k:4 * k + 4]) for k, nm in enumerate(SMALL_PARAMS)}
    return res, outs[4 * n_p], outs[4 * n_p + 1], outs[4 * n_p + 2]


def _block_diag(w):
    eye = jnp.eye(LRU_BLOCKS, dtype=F32)
    return (w[:, :, None, :] * eye[:, None, :, None]).reshape(LRU_W, LRU_W).astype(BF16)


def _lane_rep(v8):
    return jnp.broadcast_to(v8.reshape(SUBLANES, 1), (SUBLANES, LANES))


def kernel(x, c, ctx, c_ctx, w_ada, b_ada, norm1_g, norm2_g, w_in, ret_decay, conv_w, conv_b, lru_wa, lru_ba, lru_wx, lru_bx, lru_lambda, w_out, w_mlp1, w_mlp2, final_g, loss_target, m_c_ctx, m_w_ada, m_b_ada, m_norm1_g, m_norm2_g, m_w_in, m_ret_decay, m_conv_w, m_conv_b, m_lru_wa, m_lru_ba, m_lru_wx, m_lru_bx, m_lru_lambda, m_w_out, m_w_mlp1, m_w_mlp2, m_final_g, v_c_ctx, v_w_ada, v_b_ada, v_norm1_g, v_norm2_g, v_w_in, v_ret_decay, v_conv_w, v_conv_b, v_lru_wa, v_lru_ba, v_lru_wx, v_lru_bx, v_lru_lambda, v_w_out, v_w_mlp1, v_w_mlp2, v_final_g):
    ax, ay, ac = lax.axis_index("x"), lax.axis_index("y"), lax.axis_index("c")
    chip = 2 * ax + ay
    dev = 4 * ax + 2 * ay + ac
    c_idx = jnp.stack([ac, chip]).astype(jnp.int32)
    j_idx = chip.reshape(1).astype(jnp.int32)

    xt = x[0]
    t_len = xt.shape[0]
    ctxt = ctx[0]
    l_len = ctxt.shape[0]
    tgt = loss_target[0]
    ada_n = w_ada.shape[2]

    def my_half(w2d):
        r = w2d.shape[0] // 2
        return lax.dynamic_slice_in_dim(w2d, ac * r, r, axis=0).astype(BF16)

    pad8 = lambda a: jnp.pad(a, ((0, SUBLANES - a.shape[0]), (0, 0)))
    small = jnp.concatenate([pad8(conv_w[0]), pad8(lru_ba[0]), pad8(lru_bx[0]), pad8(lru_lambda[0])], axis=0)
    b_shard = lax.dynamic_slice_in_dim(b_ada, chip * ada_n, ada_n, axis=1)
    gw_in, _, small_all, a16, mod_parts, lgv, sgv = _head(
        my_half(w_in[0]), pad8(c), small, w_ada[0], b_shard, c_ctx, ret_decay[0])
    w4 = gw_in.reshape(N_CHIP, D_MODEL, IN_COLS // N_CHIP)

    mod_all = mod_parts[0::2].transpose(1, 0, 2).reshape(16, N_CHIP * ada_n)
    mod_me = lax.dynamic_slice_in_dim(mod_all, dev, 1, axis=0)
    sh1, sc1, g1, sh2, sc2, g2 = [mod_me[:, D_MODEL * k:D_MODEL * (k + 1)] for k in range(N_MOD)]
    csh1, csc1 = mod_all[8:9, 0:D_MODEL], mod_all[8:9, D_MODEL:2 * D_MODEL]

    cos2, sin2 = _rotary_tables(t_len)
    cos_c, sin_c = jnp.ones((l_len, DH), F32), jnp.zeros((l_len, DH), F32)
    n1g, n2g = norm1_g, norm2_g
    fg = final_g.reshape(1, D_MODEL)

    small_full = small_all[0::2].transpose(1, 0, 2).reshape(4 * SUBLANES, LRU_W)
    cw = small_full[0:4]
    cb = conv_b
    ba_f, ba_b = small_full[8:9], small_full[9:10]
    bx_f, bx_b = small_full[16:17], small_full[17:18]
    lam_f, lam_b = small_full[24:25], small_full[25:26]
    wa_f, wa_b = _block_diag(lru_wa[0, 0]), _block_diag(lru_wa[0, 1])
    wx_f, wx_b = _block_diag(lru_wx[0, 0]), _block_diag(lru_wx[0, 1])
    zero_h = jnp.zeros((1, LRU_W), F32)

    projc, xrc, hcb16 = _inproj_fwd(ctxt, n1g, csh1, csc1, w4, cos_c, sin_c, "inproj_fwd_ctx")
    s_f, s_b = _ctx_state_fwd(projc, lgv)
    xcc = _conv_fwd(xrc, cw, cb, "conv_fwd_ctx")
    hcf = _lru_fwd(xcc, wa_f, wx_f, ba_f, bx_f, lam_f, zero_h, False, "lru_fwd_ctx_f")
    hcbk = _lru_fwd(xcc, wa_b, wx_b, ba_b, bx_b, lam_b, zero_h, True, "lru_fwd_ctx_b")
    lru_sf, lru_sb = hcf[l_len - 1:l_len], hcbk[0:1]

    h1, h2 = my_half(w_mlp1[0]), my_half(w_mlp2[0])
    q = h1.shape[0] // 2
    (proj, xrl, hb16), ((gw_1a,),) = _inproj_fwd(xt, n1g, sh1, sc1, w4, cos2, sin2, "inproj_fwd",
                                           comms=(_AllGather([h1[:q]]),))
    (o_f, o_b, spf, spb), ((gw_1b, gw_out),) = _ret_fwd(proj, lgv, s_f, s_b,
                                                       comms=(_AllGather([h1[q:], my_half(w_out[0])]),))
    xcl = _conv_fwd(xrl, cw, cb, "conv_fwd")
    hf, ((gw_2a,),) = _lru_fwd(xcl, wa_f, wx_f, ba_f, bx_f, lam_f, lru_sf, False, "lru_fwd_f",
                              comms=(_AllGather([h2[:q]]),))
    hbk, ((gw_2b,),) = _lru_fwd(xcl, wa_b, wx_b, ba_b, bx_b, lam_b, lru_sb, True, "lru_fwd_b",
                               comms=(_AllGather([h2[q:]]),))
    wo = gw_out.reshape(D_MODEL, D_MODEL)
    x1, cat = _mix_fwd(o_f, o_b, proj, hf, hbk, wo, xt, g1)

    (dx1, h2b, ab, dub, dmb, dsc2, dsh2, dg2, dn2g, dfg, lossv) = _mlp(
        x1, n2g, sh2, sc2, g2, fg, (gw_1a, gw_1b), (gw_2a, gw_2b), tgt)
    gw_mlp1 = _tn(h2b, dub, N_CHIP, False, True, "grad_w_mlp1")
    b_1 = gw_mlp1.reshape(N_DEV, D_MODEL // 2, MLP_H // N_CHIP)
    gw_mlp2, ((r_1,),) = _tn(ab, dmb, N_CHIP, True, False, "grad_w_mlp2", comms=(_pair_exchange([b_1]),))

    jc_idx = c_idx
    half = D_MODEL // 4
    top, bot = (0, half), (half, half)
    b_2 = gw_mlp2.reshape(N_DEV, MLP_H // N_DEV, D_MODEL)
    p_1, pb_1 = _pair_add(b_1, r_1, c_idx, "rs_pair_add_w_mlp1")
    (do, dhs, dg, dgate, dyb, dg1), ((q_1a,), (r_2,)) = _mix_bwd(
        o_f, o_b, proj, hf, hbk, wo, cat, dx1, g1, comms=(_chip_exchange([pb_1], top), _pair_exchange([b_2])))
    gw_o = _tn(cat, dyb, 1, False, False, "grad_w_out")
    b_o = gw_o.reshape(N_DEV, D_MODEL // N_DEV, D_MODEL)
    p_2, pb_2 = _pair_add(b_2, r_2, c_idx, "rs_pair_add_w_mlp2")

    (dq_f, dk_f, dv_f, ds_f, drd_f), ((q_1b,), (r_o,)) = _ret_bwd(
        proj, lgv, sgv, spf, do, False, "ret_bwd_f", comms=(_chip_exchange([pb_1], bot), _pair_exchange([b_o])))
    p_o, pb_o = _pair_add(b_o, r_o, c_idx, "rs_pair_add_w_out")
    h_1 = _chip_add(p_1, (q_1a, q_1b), jc_idx, "rs_chip_add_w_mlp1")

    (dq_b, dk_b, dv_b, ds_b, drd_b), ((q_2a,), (f_1,)) = _ret_bwd(
        proj, lgv, sgv, spb, do, True, "ret_bwd_b", comms=(_chip_exchange([pb_2], top), _pair_gather([h_1])))

    (dxc_f, dpre_f, dba_f, dbx_f, dlam_f, dh0_f), ((q_2b,),) = _lru_bwd(
        xcl, wa_f, wx_f, ba_f, bx_f, lam_f, hf, lru_sf, dhs, False, "lru_bwd_f",
        comms=(_chip_exchange([pb_2], bot),))
    h_2 = _chip_add(p_2, (q_2a, q_2b), jc_idx, "rs_chip_add_w_mlp2")
    (dxc_b, dpre_b, dba_b, dbx_b, dlam_b, dh0_b), ((q_o,), (f_2,)) = _lru_bwd(
        xcl, wa_b, wx_b, ba_b, bx_b, lam_b, hbk, lru_sb, dhs, True, "lru_bwd_b",
        comms=(_chip_exchange([pb_o]), _pair_gather([h_2])))
    h_o = _chip_add(p_o, (q_o,), jc_idx, "rs_chip_add_w_out")
    dxr, dcw, dcb = _conv_bwd(dxc_f, dxc_b, xrl, cw, "conv_bwd")
    grad_x, dpb, dn1g, dsh1, dsc1 = _inproj_bwd(
        xt, n1g, sh1, sc1, w4, cos2, sin2, [dq_f, dq_b, dk_f, dk_b, dv_f, dv_b, dg, dxr, dgate], dx1, "inproj_bwd")

    dkc, dvc, drd_c = _ctx_state_bwd(projc, lgv, sgv, ds_f, ds_b)
    zc = jnp.zeros((l_len, LRU_W), F32)
    dhc_f = lax.dynamic_update_slice(zc, dh0_f, (l_len - 1, 0))
    dhc_b = lax.dynamic_update_slice(zc, dh0_b, (0, 0))
    (dxcc_f, dprec_f, dbac_f, dbxc_f, dlamc_f, _), _ = _lru_bwd(
        xcc, wa_f, wx_f, ba_f, bx_f, lam_f, hcf, zero_h, dhc_f, False, "lru_bwd_ctx_f")
    (dxcc_b, dprec_b, dbac_b, dbxc_b, dlamc_b, _), _ = _lru_bwd(
        xcc, wa_b, wx_b, ba_b, bx_b, lam_b, hcbk, zero_h, dhc_b, True, "lru_bwd_ctx_b")
    dxrc, dcw_c, dcb_c = _conv_bwd(dxcc_f, dxcc_b, xrc, cw, "conv_bwd_ctx")
    zr = jnp.zeros((l_len, RET_W), BF16)
    _, dpbc, dn1g_c, dcsh1, dcsc1 = _inproj_bwd(
        ctxt, n1g, csh1, csc1, w4, cos_c, sin_c, [zr, zr, dkc, zr, dvc, zr, zr, dxrc, zr],
        jnp.zeros((l_len, D_MODEL), F32), "inproj_bwd_ctx")

    gw_i = _tn(hb16, dpb, N_CHIP, False, True, "grad_w_in", extra=(hcb16, dpbc))
    b_i = gw_i.reshape(N_DEV, D_MODEL // 2, IN_COLS // N_CHIP)
    gwa_f, ((r_i,), (f_o,)) = _tn(xcl, dpre_f, 2, False, True, "grad_lru_gates_f", extra=(xcc, dprec_f),
                                  comms=(_pair_exchange([b_i]), _pair_gather([h_o])))
    p_i, pb_i = _pair_add(b_i, r_i, c_idx, "rs_pair_add_w_in")
    gwa_b, ((q_i,),) = _tn(xcl, dpre_b, 2, False, True, "grad_lru_gates_b", extra=(xcc, dprec_b),
                           comms=(_chip_exchange([pb_i]),))
    (f_i,) = _run_comm(_pair_gather([_chip_add(p_i, (q_i,), jc_idx, "rs_chip_add_w_in")]), "rs_pair_gather_w_in")
    g_in, g_out, g_1, g_2 = _shard_of(f_i), _shard_of(f_o), _shard_of(f_1), _shard_of(f_2)
    big = {}
    for nm, w, g, m, v in (("w_in", w_in, g_in, m_w_in, v_w_in), ("w_out", w_out, g_out, m_w_out, v_w_out),
                           ("w_mlp1", w_mlp1, g_1, m_w_mlp1, v_w_mlp1), ("w_mlp2", w_mlp2, g_2, m_w_mlp2, v_w_mlp2)):
        go, d_, mn, vn = _adamw(w[0], g, m[0], v[0], "adamw_" + nm)
        big[nm] = (go[None], d_[None], mn[None], vn[None])

    slab, ga, gx = _pack_small(
        [lossv, dsh1, dsc1, dg1, dsh2, dsc2, dg2, dcsh1, dcsc1, dn1g, dn1g_c, dn2g, dfg],
        (drd_f, drd_b, drd_c), (dcw, dcw_c), (dcb, dcb_c),
        (dba_f, dbac_f, dba_b, dbac_b, dbx_f, dbxc_f, dbx_b, dbxc_b, dlam_f, dlamc_f, dlam_b, dlamc_b),
        (gwa_f, gwa_b))
    slab_all, ga_all, gx_all = _all_gather([slab, ga, gx], "gather_small_grads")
    params = {
        "b_ada": (b_ada, m_b_ada, v_b_ada), "norm1_g": (norm1_g, m_norm1_g, v_norm1_g),
        "norm2_g": (norm2_g, m_norm2_g, v_norm2_g), "final_g": (final_g, m_final_g, v_final_g),
        "ret_decay": (ret_decay, m_ret_decay, v_ret_decay), "conv_w": (conv_w, m_conv_w, v_conv_w),
        "conv_b": (conv_b, m_conv_b, v_conv_b), "lru_wa": (lru_wa, m_lru_wa, v_lru_wa),
        "lru_ba": (lru_ba, m_lru_ba, v_lru_ba), "lru_wx": (lru_wx, m_lru_wx, v_lru_wx),
        "lru_bx": (lru_bx, m_lru_bx, v_lru_bx), "lru_lambda": (lru_lambda, m_lru_lambda, v_lru_lambda),
    }
    as2d = {
        "b_ada": lambda a: a, "norm1_g": lambda a: a, "norm2_g": lambda a: a, "conv_b": lambda a: a,
        "final_g": lambda a: a.reshape(1, D_MODEL), "ret_decay": lambda a: _lane_rep(a.reshape(-1)),
        "conv_w": lambda a: a[0], "lru_ba": lambda a: a[0], "lru_bx": lambda a: a[0], "lru_lambda": lambda a: a[0],
        "lru_wa": lambda a: a.reshape(2 * LRU_W, LRU_BD), "lru_wx": lambda a: a.reshape(2 * LRU_W, LRU_BD),
    }
    res, b128, dmc8, loss8 = _finalize_small(
        j_idx, slab_all, ga_all, gx_all, {nm: tuple(as2d[nm](a) for a in params[nm]) for nm in SMALL_PARAMS})
    loss = loss8[0, 0]
    small_out = {}
    for nm in SMALL_PARAMS:
        shp = params[nm][0].shape
        if nm == "ret_decay":
            small_out[nm] = tuple(o[:, 0].reshape(shp) for o in res[nm])
        else:
            small_out[nm] = tuple(o.reshape(shp) for o in res[nm])

    g_ada = _ada_grad(jnp.pad(a16.T, ((0, 0), (0, LANES - 16))), b128)
    g_ada, d_ada, m_ada, v_ada = _adamw(w_ada[0], g_ada, m_w_ada[0], v_w_ada[0], "adamw_w_ada")

    (cparts,) = _all_gather([_cctx_partial(dmc8, w_ada[0])], "gather_cctx")
    g_cc, d_cc, m_cc, v_cc = _cctx_final(cparts, c_ctx, m_c_ctx, v_c_ctx)
    small_out["c_ctx"] = tuple(a.reshape(D_MODEL) for a in (g_cc, d_cc, m_cc, v_cc))
    small_out["w_ada"] = (g_ada[None], d_ada[None], m_ada[None], v_ada[None])
    small_out.update(big)

    order = ["c_ctx", "w_ada", "b_ada", "norm1_g", "norm2_g", "w_in", "ret_decay", "conv_w", "conv_b", "lru_wa", "lru_ba",
             "lru_wx", "lru_bx", "lru_lambda", "w_out", "w_mlp1", "w_mlp2", "final_g"]
    outs = [loss, grad_x[None]]
    for k in range(4):
        outs += [small_out[nm][k] for nm in order]
    return tuple(outs)
```

```python
import math

import jax
import jax.numpy as jnp
from jax import lax
from jax.experimental import pallas as pl
from jax.experimental.pallas import tpu as pltpu

F32 = jnp.float32
BF16 = jnp.bfloat16

D_MODEL = 1024
HEADS = 4
DH = 128
CHUNK = 256
RET_W = HEADS * DH
LRU_W = 512
LRU_BLOCKS = 8
LRU_BD = LRU_W // LRU_BLOCKS
LRU_C = 8.0
IN_COLS = 4 * RET_W + 2 * LRU_W
MLP_H = 4 * D_MODEL
N_MOD = 6
GRID_W = 64
ROPE_BASE = 10000.0
K_SCALE = DH ** -0.5
EPS = 1e-6
GELU_K = math.sqrt(2.0 / math.pi)
GELU_C = 0.044715

ADAM_LR = 0.001
ADAM_B1 = 0.9
ADAM_B2 = 0.999
ADAM_EPS = 1e-08
ADAM_WD = 0.01
ADAM_STEP = 10

N_DEV = 8
N_CHIP = 4
SUBLANES = 8
LANES = 128
VMEM_LIMIT_V7X = 56 * 1024 * 1024
MESH = pl.DeviceIdType.MESH
ANY = pl.BlockSpec(memory_space=pl.ANY)


def _pc(body, **kw):
    return pl.pallas_call(body, **kw)


def _params(*sem):
    return pltpu.CompilerParams(dimension_semantics=sem if sem else None, vmem_limit_bytes=VMEM_LIMIT_V7X)


def _tile(t, big=False):
    if big and t >= 1024:
        return 512
    return 256 if t >= 256 else t


def _sds(shape, dtype=F32):
    return jax.ShapeDtypeStruct(tuple(shape), dtype)


def _full(shape):
    nd = len(shape)
    return pl.BlockSpec(tuple(shape), lambda *_: (0,) * nd)


def _sigmoid(x):
    return 1.0 / (1.0 + jnp.exp(-x))


def _log1p_pos(y):
    s = y * (1.0 - y * (0.5 - y * (1.0 / 3.0 - y * (0.25 - y * (0.2 - y / 6.0)))))
    return jnp.where(y < 0.03, s, jnp.log(1.0 + y))


def _softplus(z):
    return jnp.maximum(z, 0.0) + _log1p_pos(jnp.exp(-jnp.abs(z)))


def _one_minus_sq(la, a):
    t = la * (1.0 + la * (0.5 + la * (1.0 / 6.0 + la * (1.0 / 24.0 + la * (1.0 / 120.0)))))
    return jnp.where(la > -0.125, -t, 1.0 - a) * (1.0 + a)


def _rms(x):
    r = lax.rsqrt(jnp.mean(x * x, axis=-1, keepdims=True) + EPS)
    return x * r, r


def _dot(a, b):
    return jnp.dot(a, b, preferred_element_type=F32)


def _dot_nt(a, b):
    return lax.dot_general(a, b, (((1,), (1,)), ((), ())), preferred_element_type=F32)


def _dot_tn(a, b):
    return lax.dot_general(a, b, (((0,), (0,)), ((), ())), preferred_element_type=F32)


def _sum0(x):
    return jnp.sum(x, axis=0, keepdims=True)


def _norm_mod_bwd(x, g, sc, dh):
    xh, r = _rms(x)
    hn = xh * g
    dhn = dh * (1.0 + sc)
    dxh = dhn * g
    dx = r * (dxh - xh * jnp.mean(dxh * xh, axis=-1, keepdims=True))
    return dx, _sum0(dhn * xh), _sum0(dh), _sum0(dh * hn)


def _dev_index(p):
    return 4 * p[0] + 2 * p[1] + p[2]


def _mesh_pos():
    return lax.axis_index("x"), lax.axis_index("y"), lax.axis_index("c")


class _AllGather:
    def __init__(self, arrs):
        n = len(arrs)
        self.arrays = list(arrs)
        self.out_shapes = [_sds((N_DEV,) + a.shape, a.dtype) for a in arrs]
        self.scratch = ([pltpu.VMEM(a.shape, a.dtype) for a in arrs]
                        + [pltpu.SemaphoreType.DMA((7 * n,)), pltpu.SemaphoreType.DMA((7 * n,)),
                           pltpu.SemaphoreType.DMA((n,))])
        self.aliases = {}

    def _parts(self, ins, outs, scr):
        n = len(self.arrays)
        stage = scr[:n]
        send_sems, recv_sems, local_sems = scr[n:]
        x, y, c = _mesh_pos()
        me, sib = (x, y, c), (x, y, 1 - c)
        chips = [(1 - x, y), (x, 1 - y), (1 - x, 1 - y)]

        def copy(t, k, block, to, own=False):
            dst = outs[t].at[_dev_index(block)]
            return pltpu.make_async_remote_copy(
                src_ref=ins[t] if own else dst, dst_ref=dst,
                send_sem=send_sems.at[7 * t + k], recv_sem=recv_sems.at[7 * t + k],
                device_id=to, device_id_type=MESH)

        first = []
        for t in range(n):
            first.append(copy(t, 0, me, sib, own=True))
            for j, ch in enumerate(chips):
                first.append(copy(t, 1 + j, me, (*ch, c), own=True))
        stage_in = [pltpu.make_async_copy(ins[t], stage[t], local_sems.at[t]) for t in range(n)]
        mine = [pltpu.make_async_copy(stage[t], outs[t].at[_dev_index(me)], local_sems.at[t]) for t in range(n)]
        return n, c, me, sib, chips, copy, first, stage_in, mine

    def start(self, ins, outs, scr):
        n, _, _, _, _, _, first, stage_in, mine = self._parts(ins, outs, scr)
        for cp in stage_in:
            cp.start()
        for cp in first:
            cp.start()
        for t in range(n):
            stage_in[t].wait()
            mine[t].start()

    def finish(self, ins, outs, scr):
        n, c, me, sib, chips, copy, first, _, mine = self._parts(ins, outs, scr)
        passed = []
        for j, ch in enumerate(chips):
            for t in range(n):
                copy(t, 1 + j, (*ch, c), me).wait_recv()
                p = copy(t, 4 + j, (*ch, c), sib)
                p.start()
                passed.append(p)
        for t in range(n):
            copy(t, 0, sib, me).wait_recv()
            for j, ch in enumerate(chips):
                copy(t, 4 + j, (*ch, 1 - c), me).wait_recv()
        for cp in first + passed:
            cp.wait_send()
        for cp in mine:
            cp.wait()


class _Exchange:
    def __init__(self, arrays, out_shapes, plan, n_copies, aliases=None):
        self.arrays = list(arrays)
        self.out_shapes = list(out_shapes)
        self.plan = plan
        self.scratch = [pltpu.SemaphoreType.DMA((n_copies,)), pltpu.SemaphoreType.DMA((n_copies,))]
        self.aliases = aliases or {}

    def _copies(self, ins, outs, scr):
        send_sems, recv_sems = scr
        snd, rcv = [], []
        for i, (src, dst, peer, lands) in enumerate(self.plan(ins, outs, _mesh_pos())):
            kw = dict(send_sem=send_sems.at[i], recv_sem=recv_sems.at[i], device_id=peer, device_id_type=MESH)
            snd.append(pltpu.make_async_remote_copy(src_ref=src, dst_ref=dst, **kw))
            rcv.append(pltpu.make_async_remote_copy(src_ref=src, dst_ref=lands, **kw))
        return snd, rcv

    def start(self, ins, outs, scr):
        for cp in self._copies(ins, outs, scr)[0]:
            cp.start()

    def finish(self, ins, outs, scr):
        snd, rcv = self._copies(ins, outs, scr)
        for cp in rcv:
            cp.wait_recv()
        for cp in snd:
            cp.wait_send()


def _pair_exchange(grads):
    n = len(grads)

    def plan(ins, outs, pos):
        x, y, c = pos
        return [(ins[t].at[2 * j + (1 - c)], outs[t].at[j], (x, y, 1 - c), outs[t].at[j])
                for t in range(n) for j in range(N_CHIP)]

    return _Exchange(grads, [_sds((N_CHIP,) + g.shape[1:], g.dtype) for g in grads], plan, N_CHIP * n)


def _chip_exchange(parts, rows=None):
    n = len(parts)

    def plan(ins, outs, pos):
        x, y, c = pos
        chips = [(1 - x, y), (x, 1 - y), (1 - x, 1 - y)]

        def src(t, ch):
            blk = ins[t].at[2 * ch[0] + ch[1]]
            return blk if rows is None else blk.at[pl.ds(rows[0], rows[1])]

        return [(src(t, ch), outs[t].at[k], (*ch, c), outs[t].at[k]) for t in range(n) for k, ch in enumerate(chips)]

    shapes = [_sds((3, p.shape[1] if rows is None else rows[1]) + p.shape[2:], p.dtype) for p in parts]
    return _Exchange(parts, shapes, plan, 3 * n)


def _pair_gather(bufs):
    n = len(bufs)

    def plan(ins, outs, pos):
        x, y, c = pos
        return [(ins[t].at[c], outs[t].at[c], (x, y, 1 - c), outs[t].at[1 - c]) for t in range(n)]

    return _Exchange(bufs, [_sds(b.shape, b.dtype) for b in bufs], plan, n, aliases={t: t for t in range(n)})


def _run_comm(comm, name):
    n_in, n_out = len(comm.arrays), len(comm.out_shapes)

    def body(*refs):
        ins, outs, scr = refs[:n_in], refs[n_in:n_in + n_out], refs[n_in + n_out:]
        comm.start(ins, outs, scr)
        comm.finish(ins, outs, scr)

    outs = _pc(body, name=name, out_shape=comm.out_shapes, in_specs=[ANY] * n_in, out_specs=[ANY] * n_out,
               input_output_aliases=dict(comm.aliases), scratch_shapes=comm.scratch,
               compiler_params=_params())(*comm.arrays)
    return list(outs)


def _all_gather(arrs, name):
    return _run_comm(_AllGather(arrs), name)


def _call(body, *, name, grid, in_specs, out_specs, out_shape, scratch_shapes, sem, args, comms=()):
    n_in, n_out, n_scr = len(in_specs), len(out_specs), len(scratch_shapes)
    c_in = [len(cm.arrays) for cm in comms]
    c_out = [len(cm.out_shapes) for cm in comms]
    c_scr = [len(cm.scratch) for cm in comms]
    aliases = {}
    for k, cm in enumerate(comms):
        for a, b in cm.aliases.items():
            aliases[n_in + sum(c_in[:k]) + a] = n_out + sum(c_out[:k]) + b

    def split(refs, counts):
        out, pos = [], 0
        for cnt in counts:
            out.append(refs[pos:pos + cnt])
            pos += cnt
        return out

    def wrapped(*refs):
        ins = refs[:n_in + sum(c_in)]
        outs = refs[len(ins):len(ins) + n_out + sum(c_out)]
        scr = refs[len(ins) + len(outs):]
        cins, couts, cscr = split(ins[n_in:], c_in), split(outs[n_out:], c_out), split(scr[n_scr:], c_scr)
        if comms:
            first = pl.program_id(0) == 0
            last = pl.program_id(0) == grid[0] - 1
            for k in range(1, len(grid)):
                first = jnp.logical_and(first, pl.program_id(k) == 0)
                last = jnp.logical_and(last, pl.program_id(k) == grid[k] - 1)

            @pl.when(first)
            def _():
                for k, cm in enumerate(comms):
                    cm.start(cins[k], couts[k], cscr[k])
        body(*ins[:n_in], *outs[:n_out], *scr[:n_scr])
        if comms:
            @pl.when(last)
            def _():
                for k, cm in enumerate(comms):
                    cm.finish(cins[k], couts[k], cscr[k])

    outs = _pc(wrapped, name=name, grid=grid,
               in_specs=list(in_specs) + [ANY] * sum(c_in), out_specs=list(out_specs) + [ANY] * sum(c_out),
               out_shape=list(out_shape) + [s for cm in comms for s in cm.out_shapes],
               scratch_shapes=list(scratch_shapes) + [s for cm in comms for s in cm.scratch],
               input_output_aliases=aliases, compiler_params=_params(*sem),
               )(*args, *[a for cm in comms for a in cm.arrays])
    outs = list(outs)
    return outs[:n_out], split(outs[n_out:], c_out)


def _row_block(r):
    for b in (512, 256, 128, 64, 32, 16, 8):
        if r % b == 0:
            return b
    return r


def _pair_add(g, recv, cj_idx, name):
    _, r, cc = g.shape
    br = _row_block(r)

    def body(cj_ref, g_ref, r_ref, own_ref, pb_ref):
        s = g_ref[...] + r_ref[...]
        pb_ref[...] = s.astype(BF16)

        @pl.when(pl.program_id(1) == cj_ref[1])
        def _():
            own_ref[...] = s[0]

    grid_spec = pltpu.PrefetchScalarGridSpec(
        num_scalar_prefetch=1, grid=(r // br, N_CHIP),
        in_specs=[pl.BlockSpec((1, br, cc), lambda i, j, cj_ref: (2 * j + cj_ref[0], i, 0)),
                  pl.BlockSpec((1, br, cc), lambda i, j, cj_ref: (j, i, 0))],
        out_specs=[pl.BlockSpec((br, cc), lambda i, j, cj_ref: (i, 0)),
                   pl.BlockSpec((1, br, cc), lambda i, j, cj_ref: (j, i, 0))])
    return _pc(body, name=name, grid_spec=grid_spec,
               out_shape=[_sds((r, cc)), _sds((N_CHIP, r, cc), BF16)],
               compiler_params=_params("arbitrary", "arbitrary"))(cj_idx, g, recv)


def _chip_add(p, qs, cj_idx, name):
    r, cc = p.shape
    nq = len(qs)
    br = _row_block(r // nq)
    nb = r // nq // br

    def body(cj_ref, p_ref, *refs):
        o_ref = refs[-1]
        if nq == 2:
            top = pl.program_id(0) < nb
            q = [jnp.where(top, refs[0][k], refs[1][k]).astype(F32) for k in range(3)]
        else:
            q = [refs[0][k].astype(F32) for k in range(3)]
        o_ref[0] = ((p_ref[...] + q[0]) + q[1]) + q[2]

    q_specs = [pl.BlockSpec((3, br, cc), lambda i, cj_ref, h=h: (0, jnp.clip(i - h * nb, 0, nb - 1), 0))
               for h in range(nq)]
    grid_spec = pltpu.PrefetchScalarGridSpec(
        num_scalar_prefetch=1, grid=(r // br,),
        in_specs=[pl.BlockSpec((br, cc), lambda i, cj_ref: (i, 0))] + q_specs,
        out_specs=pl.BlockSpec((1, br, cc), lambda i, cj_ref: (cj_ref[0], i, 0)))
    return _pc(body, name=name, grid_spec=grid_spec, out_shape=_sds((2, r, cc)),
               compiler_params=_params("arbitrary"))(cj_idx, p, *qs)


def _shard_of(both):
    return both.reshape((2 * both.shape[1],) + both.shape[2:])


def _adamw(w, g, m, v, name):
    r, cc = w.shape
    br = _row_block(r)
    if r * cc * 4 <= (1 << 20):
        br = r
    elif br * cc * 4 > (1 << 20) and br > 8:
        br = max(8, (1 << 20) // (cc * 4) // 8 * 8)
        while r % br:
            br -= 8
    c1 = 1.0 - ADAM_B1 ** ADAM_STEP
    c2 = 1.0 - ADAM_B2 ** ADAM_STEP

    def body(w_ref, g_ref, m_ref, v_ref, go_ref, d_ref, mo_ref, vo_ref):
        gg = g_ref[...]
        go_ref[...] = gg
        mn = ADAM_B1 * m_ref[...] + (1.0 - ADAM_B1) * gg
        vn = ADAM_B2 * v_ref[...] + (1.0 - ADAM_B2) * (gg * gg)
        mh = mn / c1
        vh = vn / c2
        d_ref[...] = -ADAM_LR * (mh / (jnp.sqrt(vh) + ADAM_EPS) + ADAM_WD * w_ref[...])
        mo_ref[...] = mn
        vo_ref[...] = vn

    spec = pl.BlockSpec((br, cc), lambda i: (i, 0))
    return _pc(body, name=name, grid=(r // br,), in_specs=[spec] * 4, out_specs=[spec] * 4,
               out_shape=[_sds((r, cc))] * 4, compiler_params=_params("arbitrary"))(w, g, m, v)


def _head(w_half, c8, small, w_ada, b_shard, c_ctx, ret_decay):
    ada_n = w_ada.shape[1]
    mod_sds = _sds((16, ada_n))
    ag_w, ag_c, ag_m = _AllGather([w_half]), _AllGather([c8, small]), _AllGather([mod_sds])
    n_w, n_c, n_m = len(ag_w.scratch), len(ag_c.scratch), len(ag_m.scratch)

    def body(w_ref, c_ref, s_ref, wada_ref, b_ref, cc_ref, rd_ref,
             gw_ref, call_ref, sall_ref, a_ref, modp_ref, mall_ref, lg_ref, sg_ref, *scr):
        scr_w, scr_c, scr_m = scr[:n_w], scr[n_w:n_w + n_c], scr[n_w + n_c:n_w + n_c + n_m]
        c_v, w_v, m_v, sems = scr[n_w + n_c + n_m:]
        ag_w.start((w_ref,), (gw_ref,), scr_w)
        ag_c.start((c_ref, s_ref), (call_ref, sall_ref), scr_c)
        load_w = pltpu.make_async_copy(wada_ref, w_v, sems.at[0])
        load_w.start()
        rd = rd_ref[...]
        lg_ref[...] = -_softplus(-rd)
        sg_ref[...] = _sigmoid(-rd)
        ag_c.finish((c_ref, s_ref), (call_ref, sall_ref), scr_c)
        load_c = pltpu.make_async_copy(call_ref, c_v, sems.at[1])
        load_c.start()
        load_c.wait()
        a_ref[...] = jnp.zeros_like(a_ref)
        for d in range(N_DEV):
            cd = c_v[d, 0:1, :]
            a_ref[d:d + 1, :] = cd * _sigmoid(cd)
        cc = cc_ref[...]
        a_ref[N_DEV:N_DEV + 1, :] = cc * _sigmoid(cc)
        load_w.wait()
        m_v[...] = jnp.dot(a_ref[...], w_v[...], preferred_element_type=F32,
                           precision=lax.Precision.HIGHEST) + b_ref[...]
        put = pltpu.make_async_copy(m_v, modp_ref, sems.at[2])
        put.start()
        put.wait()
        ag_m.start((modp_ref,), (mall_ref,), scr_m)
        ag_m.finish((modp_ref,), (mall_ref,), scr_m)
        ag_w.finish((w_ref,), (gw_ref,), scr_w)

    rd = jnp.broadcast_to(ret_decay.reshape(2, HEADS).T[:, :, None], (HEADS, 2, LANES))
    lane = _full((HEADS, 2, LANES))
    outs = _pc(
        body, name="head",
        in_specs=[ANY, ANY, ANY, ANY, _full((1, ada_n)), _full((1, D_MODEL)), lane],
        out_specs=[ANY, ANY, ANY, _full((16, D_MODEL)), ANY, ANY, lane, lane],
        out_shape=ag_w.out_shapes + ag_c.out_shapes + [_sds((16, D_MODEL)), mod_sds] + ag_m.out_shapes
        + [_sds((HEADS, 2, LANES))] * 2,
        scratch_shapes=ag_w.scratch + ag_c.scratch + ag_m.scratch
        + [pltpu.VMEM((N_DEV,) + c8.shape, F32), pltpu.VMEM(w_ada.shape, F32), pltpu.VMEM((16, ada_n), F32),
           pltpu.SemaphoreType.DMA((3,))],
        compiler_params=_params(),
    )(w_half, c8, small, w_ada, b_shard, c_ctx.reshape(1, D_MODEL), rd)
    gw, c_all, small_all, a16, _, mod_all, lgv, sgv = outs
    return gw, c_all, small_all, a16, mod_all, lgv, sgv


def _ada_grad(at, b):
    n = b.shape[1]
    bn = 512

    def body(a_ref, b_ref, o_ref):
        o_ref[...] = jnp.dot(a_ref[...], b_ref[...], preferred_element_type=F32, precision=lax.Precision.HIGHEST)

    return _pc(body, name="ada_grad", grid=(n // bn,),
               in_specs=[_full((D_MODEL, LANES)), pl.BlockSpec((LANES, bn), lambda i: (0, i))],
               out_specs=pl.BlockSpec((D_MODEL, bn), lambda i: (0, i)), out_shape=_sds((D_MODEL, n)),
               compiler_params=_params("arbitrary"))(at, b)


def _cctx_partial(dmc8, w_ada):
    n = w_ada.shape[1]
    bn = 512

    def body(d_ref, w_ref, o_ref):
        @pl.when(pl.program_id(0) == 0)
        def _():
            o_ref[...] = jnp.zeros_like(o_ref)
        o_ref[...] += lax.dot_general(d_ref[...], w_ref[...], (((1,), (1,)), ((), ())),
                                      preferred_element_type=F32, precision=lax.Precision.HIGHEST)

    return _pc(body, name="cctx_partial", grid=(n // bn,),
               in_specs=[pl.BlockSpec((8, bn), lambda i: (0, i)), pl.BlockSpec((D_MODEL, bn), lambda i: (0, i))],
               out_specs=_full((8, D_MODEL)), out_shape=_sds((8, D_MODEL)),
               compiler_params=_params("arbitrary"))(dmc8, w_ada)


def _cctx_final(parts, c_ctx, m, v):
    c1 = 1.0 - ADAM_B1 ** ADAM_STEP
    c2 = 1.0 - ADAM_B2 ** ADAM_STEP

    def body(p_ref, c_ref, m_ref, v_ref, g_ref, d_ref, mo_ref, vo_ref):
        s = ((p_ref[0, 0:1, :] + p_ref[2, 0:1, :]) + p_ref[4, 0:1, :]) + p_ref[6, 0:1, :]
        z = c_ref[...]
        sg = _sigmoid(z)
        gg = s * (sg * (1.0 + z * (1.0 - sg)))
        g_ref[...] = gg
        mn = ADAM_B1 * m_ref[...] + (1.0 - ADAM_B1) * gg
        vn = ADAM_B2 * v_ref[...] + (1.0 - ADAM_B2) * (gg * gg)
        d_ref[...] = -ADAM_LR * ((mn / c1) / (jnp.sqrt(vn / c2) + ADAM_EPS) + ADAM_WD * z)
        mo_ref[...] = mn
        vo_ref[...] = vn

    row = _full((1, D_MODEL))
    return _pc(body, name="cctx_final", out_shape=[_sds((1, D_MODEL))] * 4,
               in_specs=[_full(parts.shape), row, row, row], out_specs=[row] * 4,
               compiler_params=_params())(parts, c_ctx.reshape(1, D_MODEL), m.reshape(1, D_MODEL), v.reshape(1, D_MODEL))


def _rotary_tables(t_len):
    rows = t_len // GRID_W
    n_freq = DH // 4
    inv = ROPE_BASE ** (-jnp.arange(n_freq, dtype=F32) / n_freq)
    row_ang = jnp.arange(rows, dtype=F32)[:, None] * inv
    col_ang = jnp.arange(GRID_W, dtype=F32)[:, None] * inv

    def spread(fn):
        return jnp.concatenate([jnp.repeat(fn(row_ang), GRID_W, axis=0), jnp.tile(fn(col_ang), (rows, 1))], axis=-1)

    cos, sin = spread(jnp.cos), spread(jnp.sin)
    return jnp.concatenate([cos, cos], axis=-1), jnp.concatenate([-sin, sin], axis=-1)


def _inproj_fwd(x, gn, sh, sc, w4, cos2, sin2, name, comms=()):
    t = x.shape[0]
    tm = _tile(t, True)
    nc = IN_COLS // N_CHIP

    def body(x_ref, gn_ref, sh_ref, sc_ref, w_ref, c_ref, s_ref, p_ref, xr_ref, hb_ref, p_s):
        xh, _ = _rms(x_ref[...])
        h = xh * gn_ref[...] * (1.0 + sc_ref[...]) + sh_ref[...]
        hb = h.astype(BF16)
        hb_ref[...] = hb
        for j in range(N_CHIP):
            p_s[:, nc * j:nc * (j + 1)] = _dot(hb, w_ref[j])
        cc = c_ref[...]
        ss = s_ref[...]
        for hh in range(2 * HEADS):
            blk = p_s[:, DH * hh:DH * (hh + 1)]
            rot = blk * cc + pltpu.roll(blk, DH // 2, 1) * ss
            if hh >= HEADS:
                rot = rot * K_SCALE
            p_ref[:, DH * hh:DH * (hh + 1)] = rot.astype(BF16)
        p_ref[:, 2 * RET_W:] = p_s[:, 2 * RET_W:].astype(BF16)
        xr_ref[...] = p_s[:, 4 * RET_W:4 * RET_W + LRU_W]

    row = _full((1, D_MODEL))
    outs, couts = _call(
        body, name=name, grid=(t // tm,),
        in_specs=[pl.BlockSpec((tm, D_MODEL), lambda i: (i, 0)), row, row, row, _full(w4.shape),
                  pl.BlockSpec((tm, DH), lambda i: (i, 0)), pl.BlockSpec((tm, DH), lambda i: (i, 0))],
        out_specs=[pl.BlockSpec((tm, IN_COLS), lambda i: (i, 0)), pl.BlockSpec((tm, LRU_W), lambda i: (i, 0)),
                   pl.BlockSpec((tm, D_MODEL), lambda i: (i, 0))],
        out_shape=[_sds((t, IN_COLS), BF16), _sds((t, LRU_W)), _sds((t, D_MODEL), BF16)],
        scratch_shapes=[pltpu.VMEM((tm, IN_COLS), F32)], sem=("arbitrary",),
        args=(x, gn, sh, sc, w4, cos2, sin2), comms=comms)
    return (outs, couts) if comms else outs


def _inproj_bwd(x, gn, sh, sc, w4, cos2, sin2, pieces, dres, name):
    t = x.shape[0]
    tm = _tile(t)
    nc = IN_COLS // N_CHIP

    def body(x_ref, gn_ref, sh_ref, sc_ref, w_ref, c_ref, s_ref, dqf, dqb, dkf, dkb, dvf, dvb, dg, dxr, dgt, dres_ref,
             dx_ref, dpb_ref, dgn_ref, dsh_ref, dsc_ref):
        cc = c_ref[...]
        ss = s_ref[...]
        dq = dqf[...].astype(F32) + dqb[...].astype(F32)
        dk = dkf[...].astype(F32) + dkb[...].astype(F32)
        for hh in range(HEADS):
            sl = slice(DH * hh, DH * (hh + 1))
            b = dq[:, sl]
            dpb_ref[:, sl] = (b * cc + pltpu.roll(b * ss, DH // 2, 1)).astype(BF16)
            b = dk[:, sl]
            dpb_ref[:, RET_W + DH * hh:RET_W + DH * (hh + 1)] = (
                (b * cc + pltpu.roll(b * ss, DH // 2, 1)) * K_SCALE).astype(BF16)
        dpb_ref[:, 2 * RET_W:3 * RET_W] = (dvf[...].astype(F32) + dvb[...].astype(F32)).astype(BF16)
        dpb_ref[:, 3 * RET_W:4 * RET_W] = dg[...].astype(BF16)
        dpb_ref[:, 4 * RET_W:4 * RET_W + LRU_W] = dxr[...].astype(BF16)
        dpb_ref[:, 4 * RET_W + LRU_W:IN_COLS] = dgt[...].astype(BF16)
        dh = _dot_nt(dpb_ref[:, 0:nc], w_ref[0])
        for j in range(1, N_CHIP):
            dh = dh + _dot_nt(dpb_ref[:, nc * j:nc * (j + 1)], w_ref[j])
        dx, dgn_t, dsh_t, dsc_t = _norm_mod_bwd(x_ref[...], gn_ref[...], sc_ref[...], dh)
        dx_ref[...] = dres_ref[...] + dx

        @pl.when(pl.program_id(0) == 0)
        def _():
            dgn_ref[...] = jnp.zeros_like(dgn_ref)
            dsh_ref[...] = jnp.zeros_like(dsh_ref)
            dsc_ref[...] = jnp.zeros_like(dsc_ref)
        dgn_ref[...] += dgn_t
        dsh_ref[...] += dsh_t
        dsc_ref[...] += dsc_t

    row = _full((1, D_MODEL))
    pc = pl.BlockSpec((tm, RET_W), lambda i: (i, 0))
    big = pl.BlockSpec((tm, D_MODEL), lambda i: (i, 0))
    return _pc(body, name=name, grid=(t // tm,),
               in_specs=[big, row, row, row, _full(w4.shape),
                         pl.BlockSpec((tm, DH), lambda i: (i, 0)), pl.BlockSpec((tm, DH), lambda i: (i, 0))]
               + [pc] * 9 + [big],
               out_specs=[big, pl.BlockSpec((tm, IN_COLS), lambda i: (i, 0)), row, row, row],
               out_shape=[_sds((t, D_MODEL)), _sds((t, IN_COLS), BF16), _sds((1, D_MODEL)), _sds((1, D_MODEL)),
                          _sds((1, D_MODEL))],
               compiler_params=_params("arbitrary"))(x, gn, sh, sc, w4, cos2, sin2, *pieces, dres)


def _halo_specs(t, tm):
    n8 = tm // SUBLANES
    last8 = t // SUBLANES - 1
    prev = pl.BlockSpec((SUBLANES, LRU_W), lambda i: (jnp.maximum(i * n8 - 1, 0), 0))
    main = pl.BlockSpec((tm, LRU_W), lambda i: (i, 0))
    nxt = pl.BlockSpec((SUBLANES, LRU_W), lambda i: (jnp.minimum((i + 1) * n8, last8), 0))
    return prev, main, nxt


def _with_halo(prev_ref, main_ref, next_ref, i, nt):
    prev = jnp.where(i > 0, prev_ref[...], 0.0)
    nxt = jnp.where(i < nt - 1, next_ref[...], 0.0)
    return jnp.concatenate([prev, main_ref[...], nxt], axis=0)


def _conv_fwd(xr, cw, cb, name):
    t = xr.shape[0]
    tm = _tile(t, True)
    nt = t // tm
    n = tm + 2 * SUBLANES
    mid = slice(SUBLANES, SUBLANES + tm)

    def body(p_ref, m_ref, n_ref, w_ref, b_ref, o_ref):
        xp = _with_halo(p_ref, m_ref, n_ref, pl.program_id(0), nt)
        acc = b_ref[...] + pltpu.roll(xp, 1, 0)[mid] * w_ref[0:1, :]
        acc = acc + xp[mid] * w_ref[1:2, :]
        acc = acc + pltpu.roll(xp, n - 1, 0)[mid] * w_ref[2:3, :]
        acc = acc + pltpu.roll(xp, n - 2, 0)[mid] * w_ref[3:4, :]
        o_ref[...] = acc

    return _pc(body, name=name, grid=(nt,),
               in_specs=[*_halo_specs(t, tm), _full((4, LRU_W)), _full((1, LRU_W))],
               out_specs=pl.BlockSpec((tm, LRU_W), lambda i: (i, 0)), out_shape=_sds((t, LRU_W)),
               compiler_params=_params("arbitrary"))(xr, xr, xr, cw, cb)


def _conv_bwd(dxc_a, dxc_b, xr, cw, name):
    t = xr.shape[0]
    tm = _tile(t, True)
    nt = t // tm
    n = tm + 2 * SUBLANES
    mid = slice(SUBLANES, SUBLANES + tm)

    def body(ap_ref, am_ref, an_ref, bp_ref, bm_ref, bn_ref, xp_ref, xm_ref, xn_ref, w_ref, dx_ref, dw_ref, db_ref):
        i = pl.program_id(0)
        dp = _with_halo(ap_ref, am_ref, an_ref, i, nt) + _with_halo(bp_ref, bm_ref, bn_ref, i, nt)
        xp = _with_halo(xp_ref, xm_ref, xn_ref, i, nt)
        dx = pltpu.roll(dp, n - 1, 0)[mid] * w_ref[0:1, :]
        dx = dx + dp[mid] * w_ref[1:2, :]
        dx = dx + pltpu.roll(dp, 1, 0)[mid] * w_ref[2:3, :]
        dx = dx + pltpu.roll(dp, 2, 0)[mid] * w_ref[3:4, :]
        dx_ref[...] = dx.astype(BF16)
        d = dp[mid]

        @pl.when(i == 0)
        def _():
            dw_ref[...] = jnp.zeros_like(dw_ref)
            db_ref[...] = jnp.zeros_like(db_ref)
        dw_ref[0:1, :] += _sum0(d * pltpu.roll(xp, 1, 0)[mid])
        dw_ref[1:2, :] += _sum0(d * xp[mid])
        dw_ref[2:3, :] += _sum0(d * pltpu.roll(xp, n - 1, 0)[mid])
        dw_ref[3:4, :] += _sum0(d * pltpu.roll(xp, n - 2, 0)[mid])
        db_ref[...] += _sum0(d)

    return _pc(body, name=name, grid=(nt,),
               in_specs=[*_halo_specs(t, tm), *_halo_specs(t, tm), *_halo_specs(t, tm), _full((4, LRU_W))],
               out_specs=[pl.BlockSpec((tm, LRU_W), lambda i: (i, 0)), _full((4, LRU_W)), _full((1, LRU_W))],
               out_shape=[_sds((t, LRU_W), BF16), _sds((4, LRU_W)), _sds((1, LRU_W))],
               compiler_params=_params("arbitrary"))(dxc_a, dxc_a, dxc_a, dxc_b, dxc_b, dxc_b, xr, xr, xr, cw)


def _local_scan(a, b, reverse):
    n = a.shape[0]
    row = lax.broadcasted_iota(jnp.int32, a.shape, 0) & (SUBLANES - 1)
    for s in (1, 2, 4):
        if reverse:
            a_s, b_s, ok = pltpu.roll(a, n - s, 0), pltpu.roll(b, n - s, 0), row < SUBLANES - s
        else:
            a_s, b_s, ok = pltpu.roll(a, s, 0), pltpu.roll(b, s, 0), row >= s
        b = a * jnp.where(ok, b_s, 0.0) + b
        a = a * jnp.where(ok, a_s, 1.0)
    return a, b


def _carry_scan(a_s, b_s, out_ref, carry, reverse):
    ng = a_s.shape[0] // SUBLANES
    shape = carry.shape

    def step(g, cr):
        gg = (ng - 1 - g) if reverse else g
        off = pl.multiple_of(gg * SUBLANES, SUBLANES)
        h = a_s[pl.ds(off, SUBLANES), :] * cr + b_s[pl.ds(off, SUBLANES), :]
        out_ref[pl.ds(off, SUBLANES), :] = h
        edge = h[0:1, :] if reverse else h[SUBLANES - 1:SUBLANES, :]
        return jnp.broadcast_to(edge, shape)

    return lax.fori_loop(0, ng, step, carry)


def _lru_gates(xc, wa_ref, wx_ref, ba, bx, lam):
    xb = xc.astype(BF16)
    r = _sigmoid(_dot(xb, wa_ref[...]) + ba)
    ig = _sigmoid(_dot(xb, wx_ref[...]) + bx)
    sp = _softplus(-lam)
    la = -LRU_C * r * sp
    a = jnp.exp(la)
    mult = jnp.sqrt(_one_minus_sq(la, a))
    return r, ig, sp, a, mult


def _lru_fwd(xc, wa, wx, ba, bx, lam, h0, reverse, name, comms=()):
    t = xc.shape[0]
    tm = _tile(t, True)
    nt = t // tm
    tidx = (lambda i: (nt - 1 - i, 0)) if reverse else (lambda i: (i, 0))

    def body(x_ref, wa_ref, wx_ref, ba_ref, bx_ref, lam_ref, h0_ref, h_ref, a_s, b_s, c_s):
        @pl.when(pl.program_id(0) == 0)
        def _():
            c_s[...] = jnp.broadcast_to(h0_ref[...], c_s.shape)
        xv = x_ref[...]
        _, ig, _, a, mult = _lru_gates(xv, wa_ref, wx_ref, ba_ref[...], bx_ref[...], lam_ref[...])
        al, bl = _local_scan(a, mult * (ig * xv), reverse)
        a_s[...] = al
        b_s[...] = bl
        c_s[...] = _carry_scan(a_s, b_s, h_ref, c_s[...], reverse)

    vec = _full((1, LRU_W))
    mat = _full((LRU_W, LRU_W))
    (h,), couts = _call(body, name=name, grid=(nt,),
                        in_specs=[pl.BlockSpec((tm, LRU_W), tidx), mat, mat, vec, vec, vec, vec],
                        out_specs=[pl.BlockSpec((tm, LRU_W), tidx)], out_shape=[_sds((t, LRU_W))],
                        scratch_shapes=[pltpu.VMEM((tm, LRU_W), F32), pltpu.VMEM((tm, LRU_W), F32),
                                        pltpu.VMEM((SUBLANES, LRU_W), F32)],
                        sem=("arbitrary",), args=(xc, wa, wx, ba, bx, lam, h0), comms=comms)
    return (h, couts) if comms else h


def _lru_bwd(xc, wa, wx, ba, bx, lam, h, h0, dh, reverse, name, comms=()):
    t = xc.shape[0]
    tm = _tile(t, True)
    nt = t // tm
    n8 = tm // SUBLANES
    last8 = t // SUBLANES - 1
    tidx = (lambda i: (i, 0)) if reverse else (lambda i: (nt - 1 - i, 0))
    if reverse:
        halo = pl.BlockSpec((SUBLANES, LRU_W), lambda i: (jnp.minimum((i + 1) * n8, last8), 0))
    else:
        halo = pl.BlockSpec((SUBLANES, LRU_W), lambda i: (jnp.maximum((nt - 1 - i) * n8 - 1, 0), 0))

    def body(x_ref, wa_ref, wx_ref, ba_ref, bx_ref, lam_ref, h_ref, halo_ref, h0_ref, dh_ref,
             dx_ref, dpre_ref, dba_ref, dbx_ref, dlam_ref, dh0_ref, a_s, b_s, l_s, c_s, e_s):
        i = pl.program_id(0)

        @pl.when(i == 0)
        def _():
            c_s[...] = jnp.zeros_like(c_s)
            e_s[...] = jnp.zeros_like(e_s)
            dba_ref[...] = jnp.zeros_like(dba_ref)
            dbx_ref[...] = jnp.zeros_like(dbx_ref)
            dlam_ref[...] = jnp.zeros_like(dlam_ref)
        xv = x_ref[...]
        lam = lam_ref[...]
        r, ig, sp, a, mult = _lru_gates(xv, wa_ref, wx_ref, ba_ref[...], bx_ref[...], lam)
        hv = h_ref[...]
        rowi = lax.broadcasted_iota(jnp.int32, (tm, LRU_W), 0)
        edge_a = jnp.broadcast_to(e_s[0:1, :], (tm, LRU_W))
        h0b = jnp.broadcast_to(h0_ref[...], (tm, LRU_W))
        if reverse:
            a_sh = jnp.where(rowi == 0, edge_a, pltpu.roll(a, 1, 0))
            hin_edge = jnp.where(i == nt - 1, h0b, jnp.broadcast_to(halo_ref[0:1, :], (tm, LRU_W)))
            h_in = jnp.where(rowi == tm - 1, hin_edge, pltpu.roll(hv, tm - 1, 0))
        else:
            a_sh = jnp.where(rowi == tm - 1, edge_a, pltpu.roll(a, tm - 1, 0))
            hin_edge = jnp.where(i == nt - 1, h0b, jnp.broadcast_to(halo_ref[SUBLANES - 1:SUBLANES, :], (tm, LRU_W)))
            h_in = jnp.where(rowi == 0, hin_edge, pltpu.roll(hv, 1, 0))
        al, bl = _local_scan(a_sh, dh_ref[...], not reverse)
        a_s[...] = al
        b_s[...] = bl
        c_s[...] = _carry_scan(a_s, b_s, l_s, c_s[...], not reverse)
        e_s[...] = jnp.broadcast_to(a[tm - 1:tm, :] if reverse else a[0:1, :], e_s.shape)
        lmb = l_s[...]
        da = lmb * h_in
        ixc = ig * xv
        dmult = lmb * ixc
        dixc = lmb * mult
        dla = da * a - dmult * (a * a) / mult
        dpr = dla * (-LRU_C * sp) * r * (1.0 - r)
        dpi = dixc * xv * ig * (1.0 - ig)
        dprb = dpr.astype(BF16)
        dpib = dpi.astype(BF16)
        dpre_ref[:, 0:LRU_W] = dprb
        dpre_ref[:, LRU_W:2 * LRU_W] = dpib
        dx_ref[...] = dixc * ig + _dot_nt(dprb, wa_ref[...]) + _dot_nt(dpib, wx_ref[...])
        dba_ref[...] += _sum0(dpr)
        dbx_ref[...] += _sum0(dpi)
        dlam_ref[...] += _sum0(dla * (-LRU_C * r)) * (-_sigmoid(-lam))

        @pl.when(i == nt - 1)
        def _():
            al0 = a * lmb
            dh0_ref[...] = al0[tm - 1:tm, :] if reverse else al0[0:1, :]

    vec = _full((1, LRU_W))
    mat = _full((LRU_W, LRU_W))
    tile = pl.BlockSpec((tm, LRU_W), tidx)
    return _call(body, name=name, grid=(nt,),
                 in_specs=[tile, mat, mat, vec, vec, vec, tile, halo, vec, tile],
                 out_specs=[tile, pl.BlockSpec((tm, 2 * LRU_W), tidx), vec, vec, vec, vec],
                 out_shape=[_sds((t, LRU_W)), _sds((t, 2 * LRU_W), BF16), _sds((1, LRU_W)), _sds((1, LRU_W)),
                            _sds((1, LRU_W)), _sds((1, LRU_W))],
                 scratch_shapes=[pltpu.VMEM((tm, LRU_W), F32), pltpu.VMEM((tm, LRU_W), F32),
                                 pltpu.VMEM((tm, LRU_W), F32), pltpu.VMEM((SUBLANES, LRU_W), F32),
                                 pltpu.VMEM((SUBLANES, LRU_W), F32)],
                 sem=("arbitrary",), args=(xc, wa, wx, ba, bx, lam, h, h, h0, dh), comms=comms)


def _decay_tables(lg, reverse):
    ci = lax.broadcasted_iota(jnp.int32, (CHUNK, CHUNK), 0).astype(F32)
    mi = lax.broadcasted_iota(jnp.int32, (CHUNK, CHUNK), 1).astype(F32)
    rel = (mi - ci) if reverse else (ci - mi)
    relc = jnp.maximum(rel, 0.0)
    lg_c = jnp.concatenate([lg] * (CHUNK // LANES), axis=1)
    dm = jnp.where(rel >= 0, jnp.exp(lg_c * relc), 0.0)
    cd = lax.broadcasted_iota(jnp.int32, (CHUNK, DH), 0).astype(F32)
    pq, ps = (CHUNK - cd, cd) if reverse else (cd + 1.0, CHUNK - 1.0 - cd)
    return relc, dm, jnp.exp(lg * pq), jnp.exp(lg * ps), jnp.exp(lg * float(CHUNK)), pq, ps


def _ret_fwd(proj, lgv, s0f, s0b, comms=()):
    t = proj.shape[0]
    n = t // CHUNK

    def one(q, k, v, lg, s_s, hh, o_ref, sp_ref, reverse):
        _, dm, wq, ws, g, _, _ = _decay_tables(lg, reverse)
        vb = v.astype(BF16)
        p = _dot_nt(q.astype(BF16), k.astype(BF16)) * dm
        s = s_s[hh]
        sp_ref[hh, 0] = s
        o_ref[:, DH * hh:DH * (hh + 1)] = _dot(p.astype(BF16), vb) + _dot((q * wq).astype(BF16), s.astype(BF16))
        s_s[hh] = g * s + _dot_tn((k * ws).astype(BF16), vb)

    def body(qf, kf, vf, qb, kb, vb, lg_ref, s0f_ref, s0b_ref, of_ref, ob_ref, spf_ref, spb_ref, sf_s, sb_s):
        @pl.when(pl.program_id(0) == 0)
        def _():
            sf_s[...] = s0f_ref[...]
            sb_s[...] = s0b_ref[...]
        for hh in range(HEADS):
            sl = slice(DH * hh, DH * (hh + 1))
            one(qf[:, sl].astype(F32), kf[:, sl].astype(F32), vf[:, sl], lg_ref[hh, 0:1, :], sf_s, hh, of_ref, spf_ref,
                False)
            one(qb[:, sl].astype(F32), kb[:, sl].astype(F32), vb[:, sl], lg_ref[hh, 1:2, :], sb_s, hh, ob_ref, spb_ref,
                True)

    blk = (CHUNK, RET_W)
    fw = [pl.BlockSpec(blk, lambda i, o=o: (i, o)) for o in range(3)]
    bw = [pl.BlockSpec(blk, lambda i, o=o: (n - 1 - i, o)) for o in range(3)]
    st = _full((HEADS, DH, DH))
    return _call(body, name="ret_fwd", grid=(n,),
                 in_specs=fw + bw + [_full((HEADS, 2, LANES)), st, st],
                 out_specs=[pl.BlockSpec(blk, lambda i: (i, 0)), pl.BlockSpec(blk, lambda i: (n - 1 - i, 0)),
                            pl.BlockSpec((HEADS, 1, DH, DH), lambda i: (0, i, 0, 0)),
                            pl.BlockSpec((HEADS, 1, DH, DH), lambda i: (0, n - 1 - i, 0, 0))],
                 out_shape=[_sds((t, RET_W)), _sds((t, RET_W)), _sds((HEADS, n, DH, DH)), _sds((HEADS, n, DH, DH))],
                 scratch_shapes=[pltpu.VMEM((HEADS, DH, DH), F32), pltpu.VMEM((HEADS, DH, DH), F32)],
                 sem=("arbitrary",), args=(proj, proj, proj, proj, proj, proj, lgv, s0f, s0b), comms=comms)


def _ret_bwd(proj, lgv, sgv, sprev, do, reverse, name, comms=()):
    t = proj.shape[0]
    n = t // CHUNK
    d = 1 if reverse else 0
    cidx = (lambda i: i) if reverse else (lambda i: n - 1 - i)

    def body(q_ref, k_ref, v_ref, lg_ref, sg_ref, s_ref, do_ref, dq_ref, dk_ref, dv_ref, ds0_ref, drd_ref, ds_s, acc_s):
        i = pl.program_id(0)

        @pl.when(i == 0)
        def _():
            ds_s[...] = jnp.zeros_like(ds_s)
            acc_s[...] = jnp.zeros_like(acc_s)
        for hh in range(HEADS):
            sl = slice(DH * hh, DH * (hh + 1))
            relc, dm, wq, ws, g, pq, ps = _decay_tables(lg_ref[hh, d:d + 1, :], reverse)
            qb, kb, vb = q_ref[:, sl], k_ref[:, sl], v_ref[:, sl]
            q, k = qb.astype(F32), kb.astype(F32)
            p = _dot_nt(qb, kb) * dm
            s = s_ref[hh, 0]
            dob = do_ref[:, sl].astype(BF16)
            dsn = ds_s[hh]
            dsb = dsn.astype(BF16)
            dv_ref[:, sl] = (_dot_tn(p.astype(BF16), dob) + _dot((k * ws).astype(BF16), dsb)).astype(BF16)
            dp = _dot_nt(dob, vb)
            dab = (dp * dm).astype(BF16)
            xq = _dot_nt(dob, s.astype(BF16))
            yk = _dot_nt(vb, dsb)
            dq_ref[:, sl] = (_dot(dab, kb) + xq * wq).astype(BF16)
            dk_ref[:, sl] = (_dot_tn(dab, qb) + yk * ws).astype(BF16)
            ds_s[hh] = g * dsn + _dot_tn((q * wq).astype(BF16), dob)
            s_mask = _sum0(dp * p * relc)
            part = (sum(s_mask[:, LANES * u:LANES * (u + 1)] for u in range(CHUNK // LANES))
                    + _sum0(xq * q * wq * pq) + _sum0(yk * k * ws * ps) + _sum0(dsn * s) * g * float(CHUNK))
            acc_s[hh] += jnp.broadcast_to(part, (SUBLANES, LANES))

        @pl.when(i == n - 1)
        def _():
            ds0_ref[...] = ds_s[...]
            for hh in range(HEADS):
                tot = jnp.sum(acc_s[hh, 0:1, :], axis=1, keepdims=True)
                drd_ref[hh] = jnp.broadcast_to(tot, (SUBLANES, LANES)) * sg_ref[hh, d:d + 1, :]

    blk = (CHUNK, RET_W)
    qkv = [pl.BlockSpec(blk, lambda i, o=o: (cidx(i), o)) for o in range(3)]
    hc = pl.BlockSpec(blk, lambda i: (cidx(i), 0))
    lane = _full((HEADS, 2, LANES))
    return _call(body, name=name, grid=(n,),
                 in_specs=qkv + [lane, lane, pl.BlockSpec((HEADS, 1, DH, DH), lambda i: (0, cidx(i), 0, 0)), hc],
                 out_specs=[hc, hc, hc, _full((HEADS, DH, DH)), _full((HEADS, SUBLANES, LANES))],
                 out_shape=[_sds((t, RET_W), BF16)] * 3 + [_sds((HEADS, DH, DH)), _sds((HEADS, SUBLANES, LANES))],
                 scratch_shapes=[pltpu.VMEM((HEADS, DH, DH), F32), pltpu.VMEM((HEADS, SUBLANES, LANES), F32)],
                 sem=("arbitrary",), args=(proj, proj, proj, lgv, sgv, sprev, do), comms=comms)


def _ctx_weights(lg, l_len, reverse):
    pos = lax.broadcasted_iota(jnp.int32, (l_len, DH), 0).astype(F32)
    steps = pos if reverse else (l_len - 1.0 - pos)
    return jnp.exp(lg * steps), steps


def _ctx_state_fwd(projc, lgv):
    l_len = projc.shape[0]

    def body(k_ref, v_ref, lg_ref, sf_ref, sb_ref):
        k = k_ref[...]
        vb = v_ref[...].astype(BF16)
        for d, o_ref in ((0, sf_ref), (1, sb_ref)):
            w, _ = _ctx_weights(lg_ref[0, d:d + 1, :], l_len, d == 1)
            o_ref[0] = _dot_tn((k * w).astype(BF16), vb)

    st = pl.BlockSpec((1, DH, DH), lambda h: (h, 0, 0))
    return _pc(body, name="ctx_state_fwd", grid=(HEADS,),
               in_specs=[pl.BlockSpec((l_len, DH), lambda h: (0, HEADS + h)),
                         pl.BlockSpec((l_len, DH), lambda h: (0, 2 * HEADS + h)),
                         pl.BlockSpec((1, 2, LANES), lambda h: (h, 0, 0))],
               out_specs=[st, st], out_shape=[_sds((HEADS, DH, DH))] * 2,
               compiler_params=_params("arbitrary"))(projc, projc, lgv)


def _ctx_state_bwd(projc, lgv, sgv, dsf, dsb):
    l_len = projc.shape[0]

    def body(k_ref, v_ref, lg_ref, sg_ref, dsf_ref, dsb_ref, dk_ref, dv_ref, drd_ref):
        k = k_ref[...]
        vb = v_ref[...].astype(BF16)
        dk = jnp.zeros((l_len, DH), F32)
        dv = jnp.zeros((l_len, DH), F32)
        rows = []
        for d, ds_ref in ((0, dsf_ref), (1, dsb_ref)):
            w, steps = _ctx_weights(lg_ref[0, d:d + 1, :], l_len, d == 1)
            dsb16 = ds_ref[0].astype(BF16)
            dkw = _dot_nt(vb, dsb16)
            dk = dk + dkw * w
            dv = dv + _dot((k * w).astype(BF16), dsb16)
            tot = jnp.sum(_sum0(dkw * k * w * steps), axis=1, keepdims=True)
            rows.append(jnp.broadcast_to(tot, (1, LANES)) * sg_ref[0, d:d + 1, :])
        dk_ref[...] = dk.astype(BF16)
        dv_ref[...] = dv.astype(BF16)
        rid = lax.broadcasted_iota(jnp.int32, (SUBLANES, LANES), 0)
        drd_ref[0] = jnp.where(rid == 0, rows[0], jnp.where(rid == 1, rows[1], 0.0))

    st = pl.BlockSpec((1, DH, DH), lambda h: (h, 0, 0))
    lane = pl.BlockSpec((1, 2, LANES), lambda h: (h, 0, 0))
    hc = pl.BlockSpec((l_len, DH), lambda h: (0, h))
    return _pc(body, name="ctx_state_bwd", grid=(HEADS,),
               in_specs=[pl.BlockSpec((l_len, DH), lambda h: (0, HEADS + h)),
                         pl.BlockSpec((l_len, DH), lambda h: (0, 2 * HEADS + h)), lane, lane, st, st],
               out_specs=[hc, hc, pl.BlockSpec((1, SUBLANES, LANES), lambda h: (h, 0, 0))],
               out_shape=[_sds((l_len, RET_W), BF16), _sds((l_len, RET_W), BF16), _sds((HEADS, SUBLANES, LANES))],
               compiler_params=_params("arbitrary"))(projc, projc, lgv, sgv, dsf, dsb)


G_BLOCK = (3 * RET_W) // RET_W
GATE_BLOCK = (4 * RET_W + LRU_W) // LRU_W


def _head_norm(y):
    yc = y - jnp.mean(y, axis=-1, keepdims=True)
    rs = lax.rsqrt(jnp.mean(yc * yc, axis=-1, keepdims=True) + EPS)
    return yc * rs, rs


def _gelu_parts(z):
    th = jnp.tanh(GELU_K * (z + GELU_C * z * z * z))
    return 0.5 * z * (1.0 + th), th


def _mix_fwd(o_f, o_b, proj, hf, hb, w_out, x, g1):
    t = x.shape[0]
    tm = _tile(t, True)

    def body(of_ref, ob_ref, g_ref, gt_ref, hf_ref, hb_ref, w_ref, x_ref, g1_ref, x1_ref, cat_ref):
        o = of_ref[...] + ob_ref[...]
        g = g_ref[...].astype(F32)
        for hh in range(HEADS):
            sl = slice(DH * hh, DH * (hh + 1))
            nrm, _ = _head_norm(o[:, sl])
            gh = g[:, sl]
            cat_ref[:, sl] = (gh * _sigmoid(gh) * nrm).astype(BF16)
        gel, _ = _gelu_parts(gt_ref[...].astype(F32))
        cat_ref[:, RET_W:] = ((hf_ref[...] + hb_ref[...]) * gel).astype(BF16)
        x1_ref[...] = x_ref[...] + g1_ref[...] * _dot(cat_ref[...], w_ref[...])

    half = pl.BlockSpec((tm, RET_W), lambda i: (i, 0))
    big = pl.BlockSpec((tm, D_MODEL), lambda i: (i, 0))
    return _pc(body, name="mix_fwd", grid=(t // tm,),
               in_specs=[half, half, pl.BlockSpec((tm, RET_W), lambda i: (i, G_BLOCK)),
                         pl.BlockSpec((tm, LRU_W), lambda i: (i, GATE_BLOCK)), half, half,
                         _full((D_MODEL, D_MODEL)), big, _full((1, D_MODEL))],
               out_specs=[big, big], out_shape=[_sds((t, D_MODEL)), _sds((t, D_MODEL), BF16)],
               compiler_params=_params("arbitrary"))(o_f, o_b, proj, proj, hf, hb, w_out, x, g1)


def _mix_bwd(o_f, o_b, proj, hf, hb, w_out, cat, dx1, g1, comms=()):
    t = dx1.shape[0]
    tm = _tile(t, True)

    def body(of_ref, ob_ref, g_ref, gt_ref, hf_ref, hb_ref, w_ref, cat_ref, dx1_ref, g1_ref,
             do_ref, dhs_ref, dg_ref, dgt_ref, dyb_ref, dg1_ref):
        dx1v = dx1_ref[...]
        y = _dot(cat_ref[...], w_ref[...])

        @pl.when(pl.program_id(0) == 0)
        def _():
            dg1_ref[...] = jnp.zeros_like(dg1_ref)
        dg1_ref[...] += _sum0(dx1v * y)
        dyb = (g1_ref[...] * dx1v).astype(BF16)
        dyb_ref[...] = dyb
        dcat = _dot_nt(dyb, w_ref[...])
        o = of_ref[...] + ob_ref[...]
        g = g_ref[...].astype(F32)
        for hh in range(HEADS):
            sl = slice(DH * hh, DH * (hh + 1))
            nrm, rs = _head_norm(o[:, sl])
            gh = g[:, sl]
            sg = _sigmoid(gh)
            dret = dcat[:, sl]
            dg_ref[:, sl] = (dret * nrm * (sg * (1.0 + gh * (1.0 - sg)))).astype(BF16)
            dn = dret * (gh * sg)
            dyc = rs * (dn - nrm * jnp.mean(dn * nrm, axis=-1, keepdims=True))
            do_ref[:, sl] = (dyc - jnp.mean(dyc, axis=-1, keepdims=True)).astype(BF16)
        z = gt_ref[...].astype(F32)
        gel, th = _gelu_parts(z)
        dlru = dcat[:, RET_W:]
        dhs_ref[...] = dlru * gel
        dgel = 0.5 * (1.0 + th) + 0.5 * z * (1.0 - th * th) * GELU_K * (1.0 + 3.0 * GELU_C * z * z)
        dgt_ref[...] = (dlru * (hf_ref[...] + hb_ref[...]) * dgel).astype(BF16)

    half = pl.BlockSpec((tm, RET_W), lambda i: (i, 0))
    big = pl.BlockSpec((tm, D_MODEL), lambda i: (i, 0))
    return _call(body, name="mix_bwd", grid=(t // tm,),
                 in_specs=[half, half, pl.BlockSpec((tm, RET_W), lambda i: (i, G_BLOCK)),
                           pl.BlockSpec((tm, LRU_W), lambda i: (i, GATE_BLOCK)), half, half,
                           _full((D_MODEL, D_MODEL)), big, big, _full((1, D_MODEL))],
                 out_specs=[half, half, half, half, big, _full((1, D_MODEL))],
                 out_shape=[_sds((t, RET_W), BF16), _sds((t, RET_W)), _sds((t, RET_W), BF16), _sds((t, RET_W), BF16),
                            _sds((t, D_MODEL), BF16), _sds((1, D_MODEL))],
                 scratch_shapes=[], sem=("arbitrary",), args=(o_f, o_b, proj, proj, hf, hb, w_out, cat, dx1, g1),
                 comms=comms)


def _mlp(x1, n2g, sh2, sc2, g2, fg, w1_parts, w2_parts, tgt):
    t = x1.shape[0]
    tm = _tile(t)
    hb_ = MLP_H // N_CHIP
    q_rows = hb_ // 4
    n_cp = 4 * N_DEV

    def body(x1_ref, n2g_ref, sh2_ref, sc2_ref, g2_ref, fg_ref, w1a, w1b, w2a, w2b, tgt_ref,
             dx1_ref, h2b_ref, ab_ref, dub_ref, dmb_ref, dsc_ref, dsh_ref, dg2_ref, dn2_ref, dfg_ref, loss_ref,
             w1_s, w2_s, r_s, sems):
        @pl.when(pl.program_id(0) == 0)
        def _():
            cps = []
            for p, parts in enumerate(((w1a, w2a), (w1b, w2b))):
                for d in range(N_DEV):
                    rows = pl.ds(2 * q_rows * (d % 2) + q_rows * p, q_rows)
                    for src, dst in zip(parts, (w1_s, w2_s)):
                        cps.append(pltpu.make_async_copy(src.at[d], dst.at[d // 2, rows], sems.at[len(cps)]))
            for cp in cps:
                cp.start()
            for r in (dsc_ref, dsh_ref, dg2_ref, dn2_ref, dfg_ref, loss_ref):
                r[...] = jnp.zeros_like(r)
            for cp in cps:
                cp.wait()
        x1v = x1_ref[...]
        n2g, sc2, g2, fg = n2g_ref[...], sc2_ref[...], g2_ref[...], fg_ref[...]
        xh, _ = _rms(x1v)
        h2b = (xh * n2g * (1.0 + sc2) + sh2_ref[...]).astype(BF16)
        h2b_ref[...] = h2b
        m = jnp.zeros((tm, D_MODEL), F32)
        for j in range(N_CHIP):
            sl = slice(hb_ * j, hb_ * (j + 1))
            r = jnp.maximum(_dot(h2b, w1_s[j]), 0.0)
            r_s[:, sl] = r
            ab = (r * r).astype(BF16)
            ab_ref[:, sl] = ab
            m = m + _dot(ab, w2_s[j])
        x2 = x1v + g2 * m
        x2h, r2 = _rms(x2)
        err = x2h * fg - tgt_ref[...]
        loss_ref[...] += _sum0(err * err)
        dout = err * (1.0 / D_MODEL)
        dfg_ref[...] += _sum0(dout * x2h)
        dxh = dout * fg
        dx2 = r2 * (dxh - x2h * jnp.mean(dxh * x2h, axis=-1, keepdims=True))
        dg2_ref[...] += _sum0(dx2 * m)
        dmb = (g2 * dx2).astype(BF16)
        dmb_ref[...] = dmb
        dh2 = jnp.zeros((tm, D_MODEL), F32)
        for j in range(N_CHIP):
            sl = slice(hb_ * j, hb_ * (j + 1))
            dub = (_dot_nt(dmb, w2_s[j]) * (2.0 * r_s[:, sl])).astype(BF16)
            dub_ref[:, sl] = dub
            dh2 = dh2 + _dot_nt(dub, w1_s[j])
        dx, dn2_t, dsh_t, dsc_t = _norm_mod_bwd(x1v, n2g, sc2, dh2)
        dx1_ref[...] = dx2 + dx
        dn2_ref[...] += dn2_t
        dsh_ref[...] += dsh_t
        dsc_ref[...] += dsc_t

        @pl.when(pl.program_id(0) == t // tm - 1)
        def _():
            tot = jnp.sum(loss_ref[...], axis=1, keepdims=True) * (0.5 / D_MODEL)
            loss_ref[...] = jnp.broadcast_to(tot, loss_ref.shape)

    row = _full((1, D_MODEL))
    big = pl.BlockSpec((tm, D_MODEL), lambda i: (i, 0))
    wide = pl.BlockSpec((tm, MLP_H), lambda i: (i, 0))
    return _pc(body, name="mlp", grid=(t // tm,),
               in_specs=[big, row, row, row, row, row, ANY, ANY, ANY, ANY, big],
               out_specs=[big, big, wide, wide, big, row, row, row, row, row, row],
               out_shape=[_sds((t, D_MODEL)), _sds((t, D_MODEL), BF16), _sds((t, MLP_H), BF16), _sds((t, MLP_H), BF16),
                          _sds((t, D_MODEL), BF16)] + [_sds((1, D_MODEL))] * 6,
               scratch_shapes=[pltpu.VMEM((N_CHIP, D_MODEL, hb_), BF16), pltpu.VMEM((N_CHIP, hb_, D_MODEL), BF16),
                               pltpu.VMEM((tm, MLP_H), F32), pltpu.SemaphoreType.DMA((n_cp,))],
               compiler_params=_params("arbitrary"))(x1, n2g, sh2, sc2, g2, fg, *w1_parts, *w2_parts, tgt)


def _tn(a, b, nj, a_blocked, b_blocked, name, extra=None, comms=()):
    t = a.shape[0]
    m = a.shape[1] // (nj if a_blocked else 1)
    n = b.shape[1] // (nj if b_blocked else 1)
    bk = next((b for b in (2048, 1024, 512) if t % b == 0), t)
    nk = t // bk
    a_col = (lambda j: j) if a_blocked else (lambda j: 0)
    b_col = (lambda j: j) if b_blocked else (lambda j: 0)
    in_specs = [pl.BlockSpec((bk, m), lambda j, k: (k, a_col(j))), pl.BlockSpec((bk, n), lambda j, k: (k, b_col(j)))]
    args = [a, b]
    if extra is not None:
        a2, b2 = extra
        t2 = a2.shape[0]
        in_specs += [pl.BlockSpec((t2, m), lambda j, k: (0, a_col(j))),
                     pl.BlockSpec((t2, n), lambda j, k: (0, b_col(j)))]
        args += [a2, b2]

    def body(*refs):
        a_ref, b_ref = refs[0], refs[1]
        o_ref, acc = refs[-2], refs[-1]
        k = pl.program_id(1)

        @pl.when(k == 0)
        def _():
            acc[...] = jnp.zeros_like(acc)
        acc[...] += _dot_tn(a_ref[...].astype(BF16), b_ref[...].astype(BF16))

        @pl.when(k == nk - 1)
        def _():
            if extra is not None:
                acc[...] += _dot_tn(refs[2][...].astype(BF16), refs[3][...].astype(BF16))
            o_ref[0] = acc[...]

    (out,), couts = _call(body, name=name, grid=(nj, nk), in_specs=in_specs,
                          out_specs=[pl.BlockSpec((1, m, n), lambda j, k: (j, 0, 0))], out_shape=[_sds((nj, m, n))],
                          scratch_shapes=[pltpu.VMEM((m, n), F32)], sem=("arbitrary", "arbitrary"), args=args,
                          comms=comms)
    return (out, couts) if comms else out


ROW_LOSS = 0
ROW_DMOD = 1
ROW_DMODC = 7
ROW_N1, ROW_N2, ROW_FG, ROW_CB = 9, 10, 11, 12
ROW_BA, ROW_BX, ROW_LAM = 13, 15, 17
ROW_CW = 20
ROW_RD = 24
SLAB_ROWS = 32
SEG = D_MODEL // 2


def _pack_small(rows, drd, cw2, cb2, lru2, gates):
    n_rows, n_lru = len(rows), len(lru2)

    def body(*refs):
        r = refs[:n_rows]
        drd_f, drd_b, drd_c, cw_a, cw_b, cb_a, cb_b = refs[n_rows:n_rows + 7]
        lru = refs[n_rows + 7:n_rows + 7 + n_lru]
        gf_ref, gb_ref, slab, ga, gx = refs[n_rows + 7 + n_lru:]
        slab[...] = jnp.zeros_like(slab)
        slab[ROW_LOSS:ROW_LOSS + 1, :] = r[0][...]
        for k in range(N_MOD):
            slab[ROW_DMOD + k:ROW_DMOD + k + 1, :] = r[1 + k][...]
        slab[ROW_DMODC:ROW_DMODC + 1, :] = r[7][...]
        slab[ROW_DMODC + 1:ROW_DMODC + 2, :] = r[8][...]
        slab[ROW_N1:ROW_N1 + 1, :] = r[9][...] + r[10][...]
        slab[ROW_N2:ROW_N2 + 1, :] = r[11][...]
        slab[ROW_FG:ROW_FG + 1, :] = r[12][...]
        slab[ROW_CB:ROW_CB + 1, 0:LRU_W] = cb_a[...] + cb_b[...]
        for k, row in enumerate((ROW_BA, ROW_BA + 1, ROW_BX, ROW_BX + 1, ROW_LAM, ROW_LAM + 1)):
            slab[row:row + 1, 0:LRU_W] = lru[2 * k][...] + lru[2 * k + 1][...]
        slab[ROW_CW:ROW_CW + 4, 0:LRU_W] = cw_a[...] + cw_b[...]
        for h in range(HEADS):
            slab[ROW_RD + h:ROW_RD + h + 1, 0:LANES] = drd_f[h, 0:1, :] + drd_c[h, 0:1, :]
            slab[ROW_RD + HEADS + h:ROW_RD + HEADS + h + 1, 0:LANES] = drd_b[h, 0:1, :] + drd_c[h, 1:2, :]
        for d, g_ref in enumerate((gf_ref, gb_ref)):
            for n in range(LRU_BLOCKS):
                blk = slice(LRU_BD * n, LRU_BD * (n + 1))
                ga[blk, LRU_BD * d:LRU_BD * (d + 1)] = g_ref[0, blk, blk].astype(BF16)
                gx[blk, LRU_BD * d:LRU_BD * (d + 1)] = g_ref[1, blk, blk].astype(BF16)

    args = list(rows) + list(drd) + list(cw2) + list(cb2) + list(lru2) + list(gates)
    gate_shape = (LRU_W, 2 * LRU_BD)
    return _pc(body, name="pack_small", in_specs=[_full(a.shape) for a in args],
               out_specs=[_full((SLAB_ROWS, D_MODEL)), _full(gate_shape), _full(gate_shape)],
               out_shape=[_sds((SLAB_ROWS, D_MODEL)), _sds(gate_shape, BF16), _sds(gate_shape, BF16)],
               compiler_params=_params())(*args)


def _adam_math(w, g, m, v):
    mn = ADAM_B1 * m + (1.0 - ADAM_B1) * g
    vn = ADAM_B2 * v + (1.0 - ADAM_B2) * (g * g)
    mh = mn / (1.0 - ADAM_B1 ** ADAM_STEP)
    vh = vn / (1.0 - ADAM_B2 ** ADAM_STEP)
    return -ADAM_LR * (mh / (jnp.sqrt(vh) + ADAM_EPS) + ADAM_WD * w), mn, vn


SMALL_PARAMS = ("b_ada", "norm1_g", "norm2_g", "final_g", "ret_decay", "conv_w", "conv_b", "lru_wa", "lru_ba", "lru_wx",
                "lru_bx", "lru_lambda")


def _finalize_small(chip_idx, slab_all, ga_all, gx_all, wmv):
    n_p = len(SMALL_PARAMS)
    flat = [a for nm in SMALL_PARAMS for a in wmv[nm]]
    ada_n = N_MOD * D_MODEL // N_CHIP

    def body(c_ref, slab_ref, ga_ref, gx_ref, *refs):
        prm = {nm: refs[3 * k:3 * k + 3] for k, nm in enumerate(SMALL_PARAMS)}
        outs = {nm: refs[3 * n_p + 4 * k:3 * n_p + 4 * k + 4] for k, nm in enumerate(SMALL_PARAMS)}
        b128_ref, dmc_ref, loss_ref = refs[3 * n_p + 4 * n_p:]
        chip = c_ref[0]

        def pick(fn):
            acc = fn(0)
            for j in range(1, N_CHIP):
                acc = jnp.where(chip == j, fn(j), acc)
            return acc

        tot = slab_ref[0]
        for d in range(1, N_DEV):
            tot = tot + slab_ref[d]

        def update(nm, g, sl=None, rows=None):
            w_ref, m_ref, v_ref = prm[nm]
            g_ref, d_ref, mo_ref, vo_ref = outs[nm]
            ix = (slice(None) if rows is None else rows, slice(None) if sl is None else sl)
            dl, mn, vn = _adam_math(w_ref[ix], g, m_ref[ix], v_ref[ix])
            g_ref[ix] = g
            d_ref[ix] = dl
            mo_ref[ix] = mn
            vo_ref[ix] = vn

        loss_ref[...] = jnp.broadcast_to(tot[ROW_LOSS:ROW_LOSS + 1, 0:LANES], (SUBLANES, LANES))
        for k in range(N_MOD):
            g = tot[ROW_DMOD + k:ROW_DMOD + k + 1, :]
            if k < 2:
                g = g + tot[ROW_DMODC + k:ROW_DMODC + k + 1, :]
            update("b_ada", g, slice(D_MODEL * k, D_MODEL * (k + 1)))
        update("norm1_g", tot[ROW_N1:ROW_N1 + 1, :])
        update("norm2_g", tot[ROW_N2:ROW_N2 + 1, :])
        update("final_g", tot[ROW_FG:ROW_FG + 1, :])
        update("ret_decay", tot[ROW_RD:ROW_RD + SUBLANES, 0:LANES])
        update("conv_b", tot[ROW_CB:ROW_CB + 1, 0:LRU_W])
        update("conv_w", pick(lambda j: tot[ROW_CW:ROW_CW + 4, LANES * j:LANES * (j + 1)]))
        for nm, row in (("lru_ba", ROW_BA), ("lru_bx", ROW_BX), ("lru_lambda", ROW_LAM)):
            update(nm, pick(lambda j, row=row: tot[row:row + 2, LANES * j:LANES * (j + 1)]))
        for nm, g_all in (("lru_wa", ga_ref), ("lru_wx", gx_ref)):
            for dr in range(2):
                lanes = slice(LRU_BD * dr, LRU_BD * (dr + 1))
                g = g_all[0, :, lanes].astype(F32)
                for d in range(1, N_DEV):
                    g = g + g_all[d, :, lanes].astype(F32)
                update(nm, g, rows=slice(LRU_W * dr, LRU_W * (dr + 1)))

        def seg(rows6, s):
            return rows6[s // 2][:, SEG * (s % 2):SEG * (s % 2 + 1)]

        b128_ref[...] = jnp.zeros_like(b128_ref)
        dmc_ref[...] = jnp.zeros_like(dmc_ref)
        zero = jnp.zeros((1, D_MODEL), F32)
        ctx6 = [tot[ROW_DMODC:ROW_DMODC + 1, :], tot[ROW_DMODC + 1:ROW_DMODC + 2, :]] + [zero] * (N_MOD - 2)
        for q in range(ada_n // SEG):
            cols = slice(SEG * q, SEG * (q + 1))
            for d in range(N_DEV):
                rows6 = [slab_ref[d, ROW_DMOD + k:ROW_DMOD + k + 1, :] for k in range(N_MOD)]
                b128_ref[d:d + 1, cols] = pick(lambda j, rows6=rows6: seg(rows6, 3 * j + q))
            c = pick(lambda j: seg(ctx6, 3 * j + q))
            b128_ref[N_DEV:N_DEV + 1, cols] = c
            dmc_ref[0:1, cols] = c

    out_shape = []
    for nm in SMALL_PARAMS:
        out_shape += [_sds(wmv[nm][0].shape)] * 4
    out_shape += [_sds((LANES, ada_n)), _sds((SUBLANES, ada_n)), _sds((SUBLANES, LANES))]
    args = [slab_all, ga_all, gx_all] + flat
    grid_spec = pltpu.PrefetchScalarGridSpec(
        num_scalar_prefetch=1, grid=(1,), in_specs=[_full(a.shape) for a in args],
        out_specs=[_full(s.shape) for s in out_shape])
    outs = _pc(body, name="finalize_small", grid_spec=grid_spec, out_shape=out_shape,
               compiler_params=_params("arbitrary"))(chip_idx, *args)
    res = {nm: tuple(outs[4 * k:4 * k + 4]) for k, nm in enumerate(SMALL_PARAMS)}
    return res, outs[4 * n_p], outs[4 * n_p + 1], outs[4 * n_p + 2]


def _block_diag(w):
    eye = jnp.eye(LRU_BLOCKS, dtype=F32)
    return (w[:, :, None, :] * eye[:, None, :, None]).reshape(LRU_W, LRU_W).astype(BF16)


def _lane_rep(v8):
    return jnp.broadcast_to(v8.reshape(SUBLANES, 1), (SUBLANES, LANES))


def kernel(x, c, ctx, c_ctx, w_ada, b_ada, norm1_g, norm2_g, w_in, ret_decay, conv_w, conv_b, lru_wa, lru_ba, lru_wx, lru_bx, lru_lambda, w_out, w_mlp1, w_mlp2, final_g, loss_target, m_c_ctx, m_w_ada, m_b_ada, m_norm1_g, m_norm2_g, m_w_in, m_ret_decay, m_conv_w, m_conv_b, m_lru_wa, m_lru_ba, m_lru_wx, m_lru_bx, m_lru_lambda, m_w_out, m_w_mlp1, m_w_mlp2, m_final_g, v_c_ctx, v_w_ada, v_b_ada, v_norm1_g, v_norm2_g, v_w_in, v_ret_decay, v_conv_w, v_conv_b, v_lru_wa, v_lru_ba, v_lru_wx, v_lru_bx, v_lru_lambda, v_w_out, v_w_mlp1, v_w_mlp2, v_final_g):
    ax, ay, ac = lax.axis_index("x"), lax.axis_index("y"), lax.axis_index("c")
    chip = 2 * ax + ay
    dev = 4 * ax + 2 * ay + ac
    c_idx = jnp.stack([ac, chip]).astype(jnp.int32)
    j_idx = chip.reshape(1).astype(jnp.int32)

    xt = x[0]
    t_len = xt.shape[0]
    ctxt = ctx[0]
    l_len = ctxt.shape[0]
    tgt = loss_target[0]
    ada_n = w_ada.shape[2]

    def my_half(w2d):
        r = w2d.shape[0] // 2
        return lax.dynamic_slice_in_dim(w2d, ac * r, r, axis=0).astype(BF16)

    pad8 = lambda a: jnp.pad(a, ((0, SUBLANES - a.shape[0]), (0, 0)))
    small = jnp.concatenate([pad8(conv_w[0]), pad8(lru_ba[0]), pad8(lru_bx[0]), pad8(lru_lambda[0])], axis=0)
    b_shard = lax.dynamic_slice_in_dim(b_ada, chip * ada_n, ada_n, axis=1)
    gw_in, _, small_all, a16, mod_parts, lgv, sgv = _head(
        my_half(w_in[0]), pad8(c), small, w_ada[0], b_shard, c_ctx, ret_decay[0])
    w4 = gw_in.reshape(N_CHIP, D_MODEL, IN_COLS // N_CHIP)

    mod_all = mod_parts[0::2].transpose(1, 0, 2).reshape(16, N_CHIP * ada_n)
    mod_me = lax.dynamic_slice_in_dim(mod_all, dev, 1, axis=0)
    sh1, sc1, g1, sh2, sc2, g2 = [mod_me[:, D_MODEL * k:D_MODEL * (k + 1)] for k in range(N_MOD)]
    csh1, csc1 = mod_all[8:9, 0:D_MODEL], mod_all[8:9, D_MODEL:2 * D_MODEL]

    cos2, sin2 = _rotary_tables(t_len)
    cos_c, sin_c = jnp.ones((l_len, DH), F32), jnp.zeros((l_len, DH), F32)
    n1g, n2g = norm1_g, norm2_g
    fg = final_g.reshape(1, D_MODEL)

    small_full = small_all[0::2].transpose(1, 0, 2).reshape(4 * SUBLANES, LRU_W)
    cw = small_full[0:4]
    cb = conv_b
    ba_f, ba_b = small_full[8:9], small_full[9:10]
    bx_f, bx_b = small_full[16:17], small_full[17:18]
    lam_f, lam_b = small_full[24:25], small_full[25:26]
    wa_f, wa_b = _block_diag(lru_wa[0, 0]), _block_diag(lru_wa[0, 1])
    wx_f, wx_b = _block_diag(lru_wx[0, 0]), _block_diag(lru_wx[0, 1])
    zero_h = jnp.zeros((1, LRU_W), F32)

    projc, xrc, hcb16 = _inproj_fwd(ctxt, n1g, csh1, csc1, w4, cos_c, sin_c, "inproj_fwd_ctx")
    s_f, s_b = _ctx_state_fwd(projc, lgv)
    xcc = _conv_fwd(xrc, cw, cb, "conv_fwd_ctx")
    hcf = _lru_fwd(xcc, wa_f, wx_f, ba_f, bx_f, lam_f, zero_h, False, "lru_fwd_ctx_f")
    hcbk = _lru_fwd(xcc, wa_b, wx_b, ba_b, bx_b, lam_b, zero_h, True, "lru_fwd_ctx_b")
    lru_sf, lru_sb = hcf[l_len - 1:l_len], hcbk[0:1]

    h1, h2 = my_half(w_mlp1[0]), my_half(w_mlp2[0])
    q = h1.shape[0] // 2
    (proj, xrl, hb16), ((gw_1a,),) = _inproj_fwd(xt, n1g, sh1, sc1, w4, cos2, sin2, "inproj_fwd",
                                           comms=(_AllGather([h1[:q]]),))
    (o_f, o_b, spf, spb), ((gw_1b, gw_out),) = _ret_fwd(proj, lgv, s_f, s_b,
                                                       comms=(_AllGather([h1[q:], my_half(w_out[0])]),))
    xcl = _conv_fwd(xrl, cw, cb, "conv_fwd")
    hf, ((gw_2a,),) = _lru_fwd(xcl, wa_f, wx_f, ba_f, bx_f, lam_f, lru_sf, False, "lru_fwd_f",
                              comms=(_AllGather([h2[:q]]),))
    hbk, ((gw_2b,),) = _lru_fwd(xcl, wa_b, wx_b, ba_b, bx_b, lam_b, lru_sb, True, "lru_fwd_b",
                               comms=(_AllGather([h2[q:]]),))
    wo = gw_out.reshape(D_MODEL, D_MODEL)
    x1, cat = _mix_fwd(o_f, o_b, proj, hf, hbk, wo, xt, g1)

    (dx1, h2b, ab, dub, dmb, dsc2, dsh2, dg2, dn2g, dfg, lossv) = _mlp(
        x1, n2g, sh2, sc2, g2, fg, (gw_1a, gw_1b), (gw_2a, gw_2b), tgt)
    gw_mlp1 = _tn(h2b, dub, N_CHIP, False, True, "grad_w_mlp1")
    b_1 = gw_mlp1.reshape(N_DEV, D_MODEL // 2, MLP_H // N_CHIP)
    gw_mlp2, ((r_1,),) = _tn(ab, dmb, N_CHIP, True, False, "grad_w_mlp2", comms=(_pair_exchange([b_1]),))

    jc_idx = c_idx
    half = D_MODEL // 4
    top, bot = (0, half), (half, half)
    b_2 = gw_mlp2.reshape(N_DEV, MLP_H // N_DEV, D_MODEL)
    p_1, pb_1 = _pair_add(b_1, r_1, c_idx, "rs_pair_add_w_mlp1")
    (do, dhs, dg, dgate, dyb, dg1), ((q_1a,), (r_2,)) = _mix_bwd(
        o_f, o_b, proj, hf, hbk, wo, cat, dx1, g1, comms=(_chip_exchange([pb_1], top), _pair_exchange([b_2])))
    gw_o = _tn(cat, dyb, 1, False, False, "grad_w_out")
    b_o = gw_o.reshape(N_DEV, D_MODEL // N_DEV, D_MODEL)
    p_2, pb_2 = _pair_add(b_2, r_2, c_idx, "rs_pair_add_w_mlp2")

    (dq_f, dk_f, dv_f, ds_f, drd_f), ((q_1b,), (r_o,)) = _ret_bwd(
        proj, lgv, sgv, spf, do, False, "ret_bwd_f", comms=(_chip_exchange([pb_1], bot), _pair_exchange([b_o])))
    p_o, pb_o = _pair_add(b_o, r_o, c_idx, "rs_pair_add_w_out")
    h_1 = _chip_add(p_1, (q_1a, q_1b), jc_idx, "rs_chip_add_w_mlp1")

    (dq_b, dk_b, dv_b, ds_b, drd_b), ((q_2a,), (f_1,)) = _ret_bwd(
        proj, lgv, sgv, spb, do, True, "ret_bwd_b", comms=(_chip_exchange([pb_2], top), _pair_gather([h_1])))

    (dxc_f, dpre_f, dba_f, dbx_f, dlam_f, dh0_f), ((q_2b,),) = _lru_bwd(
        xcl, wa_f, wx_f, ba_f, bx_f, lam_f, hf, lru_sf, dhs, False, "lru_bwd_f",
        comms=(_chip_exchange([pb_2], bot),))
    h_2 = _chip_add(p_2, (q_2a, q_2b), jc_idx, "rs_chip_add_w_mlp2")
    (dxc_b, dpre_b, dba_b, dbx_b, dlam_b, dh0_b), ((q_o,), (f_2,)) = _lru_bwd(
        xcl, wa_b, wx_b, ba_b, bx_b, lam_b, hbk, lru_sb, dhs, True, "lru_bwd_b",
        comms=(_chip_exchange([pb_o]), _pair_gather([h_2])))
    h_o = _chip_add(p_o, (q_o,), jc_idx, "rs_chip_add_w_out")
    dxr, dcw, dcb = _conv_bwd(dxc_f, dxc_b, xrl, cw, "conv_bwd")
    grad_x, dpb, dn1g, dsh1, dsc1 = _inproj_bwd(
        xt, n1g, sh1, sc1, w4, cos2, sin2, [dq_f, dq_b, dk_f, dk_b, dv_f, dv_b, dg, dxr, dgate], dx1, "inproj_bwd")

    dkc, dvc, drd_c = _ctx_state_bwd(projc, lgv, sgv, ds_f, ds_b)
    zc = jnp.zeros((l_len, LRU_W), F32)
    dhc_f = lax.dynamic_update_slice(zc, dh0_f, (l_len - 1, 0))
    dhc_b = lax.dynamic_update_slice(zc, dh0_b, (0, 0))
    (dxcc_f, dprec_f, dbac_f, dbxc_f, dlamc_f, _), _ = _lru_bwd(
        xcc, wa_f, wx_f, ba_f, bx_f, lam_f, hcf, zero_h, dhc_f, False, "lru_bwd_ctx_f")
    (dxcc_b, dprec_b, dbac_b, dbxc_b, dlamc_b, _), _ = _lru_bwd(
        xcc, wa_b, wx_b, ba_b, bx_b, lam_b, hcbk, zero_h, dhc_b, True, "lru_bwd_ctx_b")
    dxrc, dcw_c, dcb_c = _conv_bwd(dxcc_f, dxcc_b, xrc, cw, "conv_bwd_ctx")
    zr = jnp.zeros((l_len, RET_W), BF16)
    _, dpbc, dn1g_c, dcsh1, dcsc1 = _inproj_bwd(
        ctxt, n1g, csh1, csc1, w4, cos_c, sin_c, [zr, zr, dkc, zr, dvc, zr, zr, dxrc, zr],
        jnp.zeros((l_len, D_MODEL), F32), "inproj_bwd_ctx")

    gw_i = _tn(hb16, dpb, N_CHIP, False, True, "grad_w_in", extra=(hcb16, dpbc))
    b_i = gw_i.reshape(N_DEV, D_MODEL // 2, IN_COLS // N_CHIP)
    (r_i,) = _run_comm(_pair_exchange([b_i]), "rs_pair_exchange_w_in")
    p_i, pb_i = _pair_add(b_i, r_i, c_idx, "rs_pair_add_w_in")
    gwa_f, ((q_ia,), (f_o,)) = _tn(xcl, dpre_f, 2, False, True, "grad_lru_gates_f", extra=(xcc, dprec_f),
                                   comms=(_chip_exchange([pb_i], top), _pair_gather([h_o])))
    gwa_b, ((q_ib,),) = _tn(xcl, dpre_b, 2, False, True, "grad_lru_gates_b", extra=(xcc, dprec_b),
                            comms=(_chip_exchange([pb_i], bot),))
    (f_i,) = _run_comm(_pair_gather([_chip_add(p_i, (q_ia, q_ib), jc_idx, "rs_chip_add_w_in")]),
                       "rs_pair_gather_w_in")
    g_in, g_out, g_1, g_2 = _shard_of(f_i), _shard_of(f_o), _shard_of(f_1), _shard_of(f_2)
    big = {}
    for nm, w, g, m, v in (("w_in", w_in, g_in, m_w_in, v_w_in), ("w_out", w_out, g_out, m_w_out, v_w_out),
                           ("w_mlp1", w_mlp1, g_1, m_w_mlp1, v_w_mlp1), ("w_mlp2", w_mlp2, g_2, m_w_mlp2, v_w_mlp2)):
        go, d_, mn, vn = _adamw(w[0], g, m[0], v[0], "adamw_" + nm)
        big[nm] = (go[None], d_[None], mn[None], vn[None])

    slab, ga, gx = _pack_small(
        [lossv, dsh1, dsc1, dg1, dsh2, dsc2, dg2, dcsh1, dcsc1, dn1g, dn1g_c, dn2g, dfg],
        (drd_f, drd_b, drd_c), (dcw, dcw_c), (dcb, dcb_c),
        (dba_f, dbac_f, dba_b, dbac_b, dbx_f, dbxc_f, dbx_b, dbxc_b, dlam_f, dlamc_f, dlam_b, dlamc_b),
        (gwa_f, gwa_b))
    slab_all, ga_all, gx_all = _all_gather([slab, ga, gx], "gather_small_grads")
    params = {
        "b_ada": (b_ada, m_b_ada, v_b_ada), "norm1_g": (norm1_g, m_norm1_g, v_norm1_g),
        "norm2_g": (norm2_g, m_norm2_g, v_norm2_g), "final_g": (final_g, m_final_g, v_final_g),
        "ret_decay": (ret_decay, m_ret_decay, v_ret_decay), "conv_w": (conv_w, m_conv_w, v_conv_w),
        "conv_b": (conv_b, m_conv_b, v_conv_b), "lru_wa": (lru_wa, m_lru_wa, v_lru_wa),
        "lru_ba": (lru_ba, m_lru_ba, v_lru_ba), "lru_wx": (lru_wx, m_lru_wx, v_lru_wx),
        "lru_bx": (lru_bx, m_lru_bx, v_lru_bx), "lru_lambda": (lru_lambda, m_lru_lambda, v_lru_lambda),
    }
    as2d = {
        "b_ada": lambda a: a, "norm1_g": lambda a: a, "norm2_g": lambda a: a, "conv_b": lambda a: a,
        "final_g": lambda a: a.reshape(1, D_MODEL), "ret_decay": lambda a: _lane_rep(a.reshape(-1)),
        "conv_w": lambda a: a[0], "lru_ba": lambda a: a[0], "lru_bx": lambda a: a[0], "lru_lambda": lambda a: a[0],
        "lru_wa": lambda a: a.reshape(2 * LRU_W, LRU_BD), "lru_wx": lambda a: a.reshape(2 * LRU_W, LRU_BD),
    }
    res, b128, dmc8, loss8 = _finalize_small(
        j_idx, slab_all, ga_all, gx_all, {nm: tuple(as2d[nm](a) for a in params[nm]) for nm in SMALL_PARAMS})
    loss = loss8[0, 0]
    small_out = {}
    for nm in SMALL_PARAMS:
        shp = params[nm][0].shape
        if nm == "ret_decay":
            small_out[nm] = tuple(o[:, 0].reshape(shp) for o in res[nm])
        else:
            small_out[nm] = tuple(o.reshape(shp) for o in res[nm])

    g_ada = _ada_grad(jnp.pad(a16.T, ((0, 0), (0, LANES - 16))), b128)
    g_ada, d_ada, m_ada, v_ada = _adamw(w_ada[0], g_ada, m_w_ada[0], v_w_ada[0], "adamw_w_ada")

    (cparts,) = _all_gather([_cctx_partial(dmc8, w_ada[0])], "gather_cctx")
    g_cc, d_cc, m_cc, v_cc = _cctx_final(cparts, c_ctx, m_c_ctx, v_c_ctx)
    small_out["c_ctx"] = tuple(a.reshape(D_MODEL) for a in (g_cc, d_cc, m_cc, v_cc))
    small_out["w_ada"] = (g_ada[None], d_ada[None], m_ada[None], v_ada[None])
    small_out.update(big)

    order = ["c_ctx", "w_ada", "b_ada", "norm1_g", "norm2_g", "w_in", "ret_decay", "conv_w", "conv_b", "lru_wa", "lru_ba",
             "lru_wx", "lru_bx", "lru_lambda", "w_out", "w_mlp1", "w_mlp2", "final_g"]
    outs = [loss, grad_x[None]]
    for k in range(4):
        outs += [small_out[nm][k] for nm in order]
    return tuple(outs)
```

```python
import math

import jax
import jax.numpy as jnp
from jax import lax
from jax.experimental import pallas as pl
from jax.experimental.pallas import tpu as pltpu

F32 = jnp.float32
BF16 = jnp.bfloat16

D_MODEL = 1024
HEADS = 4
DH = 128
CHUNK = 256
RET_W = HEADS * DH
LRU_W = 512
LRU_BLOCKS = 8
LRU_BD = LRU_W // LRU_BLOCKS
LRU_C = 8.0
IN_COLS = 4 * RET_W + 2 * LRU_W
MLP_H = 4 * D_MODEL
N_MOD = 6
GRID_W = 64
ROPE_BASE = 10000.0
K_SCALE = DH ** -0.5
EPS = 1e-6
GELU_K = math.sqrt(2.0 / math.pi)
GELU_C = 0.044715

ADAM_LR = 0.001
ADAM_B1 = 0.9
ADAM_B2 = 0.999
ADAM_EPS = 1e-08
ADAM_WD = 0.01
ADAM_STEP = 10

N_DEV = 8
N_CHIP = 4
SUBLANES = 8
LANES = 128
VMEM_LIMIT_V7X = 56 * 1024 * 1024
MESH = pl.DeviceIdType.MESH
ANY = pl.BlockSpec(memory_space=pl.ANY)


def _pc(body, **kw):
    return pl.pallas_call(body, **kw)


def _params(*sem):
    return pltpu.CompilerParams(dimension_semantics=sem if sem else None, vmem_limit_bytes=VMEM_LIMIT_V7X)


def _tile(t, big=False):
    if big and t >= 1024:
        return 512
    return 256 if t >= 256 else t


def _sds(shape, dtype=F32):
    return jax.ShapeDtypeStruct(tuple(shape), dtype)


def _full(shape):
    nd = len(shape)
    return pl.BlockSpec(tuple(shape), lambda *_: (0,) * nd)


def _sigmoid(x):
    return 1.0 / (1.0 + jnp.exp(-x))


def _log1p_pos(y):
    s = y * (1.0 - y * (0.5 - y * (1.0 / 3.0 - y * (0.25 - y * (0.2 - y / 6.0)))))
    return jnp.where(y < 0.03, s, jnp.log(1.0 + y))


def _softplus(z):
    return jnp.maximum(z, 0.0) + _log1p_pos(jnp.exp(-jnp.abs(z)))


def _one_minus_sq(la, a):
    t = la * (1.0 + la * (0.5 + la * (1.0 / 6.0 + la * (1.0 / 24.0 + la * (1.0 / 120.0)))))
    return jnp.where(la > -0.125, -t, 1.0 - a) * (1.0 + a)


def _rms(x):
    r = lax.rsqrt(jnp.mean(x * x, axis=-1, keepdims=True) + EPS)
    return x * r, r


def _dot(a, b):
    return jnp.dot(a, b, preferred_element_type=F32)


def _dot_nt(a, b):
    return lax.dot_general(a, b, (((1,), (1,)), ((), ())), preferred_element_type=F32)


def _dot_tn(a, b):
    return lax.dot_general(a, b, (((0,), (0,)), ((), ())), preferred_element_type=F32)


def _sum0(x):
    return jnp.sum(x, axis=0, keepdims=True)


def _norm_mod_bwd(x, g, sc, dh):
    xh, r = _rms(x)
    hn = xh * g
    dhn = dh * (1.0 + sc)
    dxh = dhn * g
    dx = r * (dxh - xh * jnp.mean(dxh * xh, axis=-1, keepdims=True))
    return dx, _sum0(dhn * xh), _sum0(dh), _sum0(dh * hn)


def _dev_index(p):
    return 4 * p[0] + 2 * p[1] + p[2]


def _mesh_pos():
    return lax.axis_index("x"), lax.axis_index("y"), lax.axis_index("c")


class _AllGather:
    def __init__(self, arrs):
        n = len(arrs)
        self.arrays = list(arrs)
        self.out_shapes = [_sds((N_DEV,) + a.shape, a.dtype) for a in arrs]
        self.scratch = ([pltpu.VMEM(a.shape, a.dtype) for a in arrs]
                        + [pltpu.SemaphoreType.DMA((7 * n,)), pltpu.SemaphoreType.DMA((7 * n,)),
                           pltpu.SemaphoreType.DMA((n,))])
        self.aliases = {}

    def _parts(self, ins, outs, scr):
        n = len(self.arrays)
        stage = scr[:n]
        send_sems, recv_sems, local_sems = scr[n:]
        x, y, c = _mesh_pos()
        me, sib = (x, y, c), (x, y, 1 - c)
        chips = [(1 - x, y), (x, 1 - y), (1 - x, 1 - y)]

        def copy(t, k, block, to, own=False):
            dst = outs[t].at[_dev_index(block)]
            return pltpu.make_async_remote_copy(
                src_ref=ins[t] if own else dst, dst_ref=dst,
                send_sem=send_sems.at[7 * t + k], recv_sem=recv_sems.at[7 * t + k],
                device_id=to, device_id_type=MESH)

        first = []
        for t in range(n):
            first.append(copy(t, 0, me, sib, own=True))
            for j, ch in enumerate(chips):
                first.append(copy(t, 1 + j, me, (*ch, c), own=True))
        stage_in = [pltpu.make_async_copy(ins[t], stage[t], local_sems.at[t]) for t in range(n)]
        mine = [pltpu.make_async_copy(stage[t], outs[t].at[_dev_index(me)], local_sems.at[t]) for t in range(n)]
        return n, c, me, sib, chips, copy, first, stage_in, mine

    def start(self, ins, outs, scr):
        n, _, _, _, _, _, first, stage_in, mine = self._parts(ins, outs, scr)
        for cp in stage_in:
            cp.start()
        for cp in first:
            cp.start()
        for t in range(n):
            stage_in[t].wait()
            mine[t].start()

    def relay(self, ins, outs, scr):
        n, c, me, sib, chips, copy, _, _, _ = self._parts(ins, outs, scr)
        for j, ch in enumerate(chips):
            for t in range(n):
                copy(t, 1 + j, (*ch, c), me).wait_recv()
                copy(t, 4 + j, (*ch, c), sib).start()

    def finish(self, ins, outs, scr):
        n, c, me, sib, chips, copy, first, _, mine = self._parts(ins, outs, scr)
        passed = [copy(t, 4 + j, (*ch, c), sib) for j, ch in enumerate(chips) for t in range(n)]
        for t in range(n):
            copy(t, 0, sib, me).wait_recv()
            for j, ch in enumerate(chips):
                copy(t, 4 + j, (*ch, 1 - c), me).wait_recv()
        for cp in first + passed:
            cp.wait_send()
        for cp in mine:
            cp.wait()


class _Exchange:
    def __init__(self, arrays, out_shapes, plan, n_copies, aliases=None):
        self.arrays = list(arrays)
        self.out_shapes = list(out_shapes)
        self.plan = plan
        self.scratch = [pltpu.SemaphoreType.DMA((n_copies,)), pltpu.SemaphoreType.DMA((n_copies,))]
        self.aliases = aliases or {}

    def _copies(self, ins, outs, scr):
        send_sems, recv_sems = scr
        snd, rcv = [], []
        for i, (src, dst, peer, lands) in enumerate(self.plan(ins, outs, _mesh_pos())):
            kw = dict(send_sem=send_sems.at[i], recv_sem=recv_sems.at[i], device_id=peer, device_id_type=MESH)
            snd.append(pltpu.make_async_remote_copy(src_ref=src, dst_ref=dst, **kw))
            rcv.append(pltpu.make_async_remote_copy(src_ref=src, dst_ref=lands, **kw))
        return snd, rcv

    def start(self, ins, outs, scr):
        for cp in self._copies(ins, outs, scr)[0]:
            cp.start()

    def relay(self, ins, outs, scr):
        pass

    def finish(self, ins, outs, scr):
        snd, rcv = self._copies(ins, outs, scr)
        for cp in rcv:
            cp.wait_recv()
        for cp in snd:
            cp.wait_send()


def _pair_exchange(grads):
    n = len(grads)

    def plan(ins, outs, pos):
        x, y, c = pos
        return [(ins[t].at[2 * j + (1 - c)], outs[t].at[j], (x, y, 1 - c), outs[t].at[j])
                for t in range(n) for j in range(N_CHIP)]

    return _Exchange(grads, [_sds((N_CHIP,) + g.shape[1:], g.dtype) for g in grads], plan, N_CHIP * n)


def _chip_exchange(parts, rows=None):
    n = len(parts)

    def plan(ins, outs, pos):
        x, y, c = pos
        chips = [(1 - x, y), (x, 1 - y), (1 - x, 1 - y)]

        def src(t, ch):
            blk = ins[t].at[2 * ch[0] + ch[1]]
            return blk if rows is None else blk.at[pl.ds(rows[0], rows[1])]

        return [(src(t, ch), outs[t].at[k], (*ch, c), outs[t].at[k]) for t in range(n) for k, ch in enumerate(chips)]

    shapes = [_sds((3, p.shape[1] if rows is None else rows[1]) + p.shape[2:], p.dtype) for p in parts]
    return _Exchange(parts, shapes, plan, 3 * n)


def _pair_gather(bufs):
    n = len(bufs)

    def plan(ins, outs, pos):
        x, y, c = pos
        return [(ins[t].at[c], outs[t].at[c], (x, y, 1 - c), outs[t].at[1 - c]) for t in range(n)]

    return _Exchange(bufs, [_sds(b.shape, b.dtype) for b in bufs], plan, n, aliases={t: t for t in range(n)})


def _run_comm(comm, name):
    n_in, n_out = len(comm.arrays), len(comm.out_shapes)

    def body(*refs):
        ins, outs, scr = refs[:n_in], refs[n_in:n_in + n_out], refs[n_in + n_out:]
        comm.start(ins, outs, scr)
        comm.relay(ins, outs, scr)
        comm.finish(ins, outs, scr)

    outs = _pc(body, name=name, out_shape=comm.out_shapes, in_specs=[ANY] * n_in, out_specs=[ANY] * n_out,
               input_output_aliases=dict(comm.aliases), scratch_shapes=comm.scratch,
               compiler_params=_params())(*comm.arrays)
    return list(outs)


def _all_gather(arrs, name):
    return _run_comm(_AllGather(arrs), name)


def _call(body, *, name, grid, in_specs, out_specs, out_shape, scratch_shapes, sem, args, comms=()):
    n_in, n_out, n_scr = len(in_specs), len(out_specs), len(scratch_shapes)
    c_in = [len(cm.arrays) for cm in comms]
    c_out = [len(cm.out_shapes) for cm in comms]
    c_scr = [len(cm.scratch) for cm in comms]
    aliases = {}
    for k, cm in enumerate(comms):
        for a, b in cm.aliases.items():
            aliases[n_in + sum(c_in[:k]) + a] = n_out + sum(c_out[:k]) + b

    def split(refs, counts):
        out, pos = [], 0
        for cnt in counts:
            out.append(refs[pos:pos + cnt])
            pos += cnt
        return out

    def wrapped(*refs):
        ins = refs[:n_in + sum(c_in)]
        outs = refs[len(ins):len(ins) + n_out + sum(c_out)]
        scr = refs[len(ins) + len(outs):]
        cins, couts, cscr = split(ins[n_in:], c_in), split(outs[n_out:], c_out), split(scr[n_scr:], c_scr)
        if comms:
            first = pl.program_id(0) == 0
            last = pl.program_id(0) == grid[0] - 1
            for k in range(1, len(grid)):
                first = jnp.logical_and(first, pl.program_id(k) == 0)
                last = jnp.logical_and(last, pl.program_id(k) == grid[k] - 1)

            @pl.when(first)
            def _():
                for k, cm in enumerate(comms):
                    cm.start(cins[k], couts[k], cscr[k])
        body(*ins[:n_in], *outs[:n_out], *scr[:n_scr])
        if comms:
            relay_early = len(grid) == 1 and grid[0] >= 4
            if relay_early:
                @pl.when(pl.program_id(0) == (3 * grid[0]) // 4 - 1)
                def _():
                    for k, cm in enumerate(comms):
                        cm.relay(cins[k], couts[k], cscr[k])

            @pl.when(last)
            def _():
                for k, cm in enumerate(comms):
                    if not relay_early:
                        cm.relay(cins[k], couts[k], cscr[k])
                    cm.finish(cins[k], couts[k], cscr[k])

    outs = _pc(wrapped, name=name, grid=grid,
               in_specs=list(in_specs) + [ANY] * sum(c_in), out_specs=list(out_specs) + [ANY] * sum(c_out),
               out_shape=list(out_shape) + [s for cm in comms for s in cm.out_shapes],
               scratch_shapes=list(scratch_shapes) + [s for cm in comms for s in cm.scratch],
               input_output_aliases=aliases, compiler_params=_params(*sem),
               )(*args, *[a for cm in comms for a in cm.arrays])
    outs = list(outs)
    return outs[:n_out], split(outs[n_out:], c_out)


def _row_block(r):
    for b in (512, 256, 128, 64, 32, 16, 8):
        if r % b == 0:
            return b
    return r


def _pair_add(g, recv, cj_idx, name):
    _, r, cc = g.shape
    br = _row_block(r)

    def body(cj_ref, g_ref, r_ref, own_ref, pb_ref):
        s = g_ref[...] + r_ref[...]
        pb_ref[...] = s.astype(BF16)

        @pl.when(pl.program_id(1) == cj_ref[1])
        def _():
            own_ref[...] = s[0]

    grid_spec = pltpu.PrefetchScalarGridSpec(
        num_scalar_prefetch=1, grid=(r // br, N_CHIP),
        in_specs=[pl.BlockSpec((1, br, cc), lambda i, j, cj_ref: (2 * j + cj_ref[0], i, 0)),
                  pl.BlockSpec((1, br, cc), lambda i, j, cj_ref: (j, i, 0))],
        out_specs=[pl.BlockSpec((br, cc), lambda i, j, cj_ref: (i, 0)),
                   pl.BlockSpec((1, br, cc), lambda i, j, cj_ref: (j, i, 0))])
    return _pc(body, name=name, grid_spec=grid_spec,
               out_shape=[_sds((r, cc)), _sds((N_CHIP, r, cc), BF16)],
               compiler_params=_params("arbitrary", "arbitrary"))(cj_idx, g, recv)


def _chip_add(p, qs, cj_idx, name):
    r, cc = p.shape
    nq = len(qs)
    br = _row_block(r // nq)
    nb = r // nq // br

    def body(cj_ref, p_ref, *refs):
        o_ref = refs[-1]
        if nq == 2:
            top = pl.program_id(0) < nb
            q = [jnp.where(top, refs[0][k], refs[1][k]).astype(F32) for k in range(3)]
        else:
            q = [refs[0][k].astype(F32) for k in range(3)]
        o_ref[0] = ((p_ref[...] + q[0]) + q[1]) + q[2]

    q_specs = [pl.BlockSpec((3, br, cc), lambda i, cj_ref, h=h: (0, jnp.clip(i - h * nb, 0, nb - 1), 0))
               for h in range(nq)]
    grid_spec = pltpu.PrefetchScalarGridSpec(
        num_scalar_prefetch=1, grid=(r // br,),
        in_specs=[pl.BlockSpec((br, cc), lambda i, cj_ref: (i, 0))] + q_specs,
        out_specs=pl.BlockSpec((1, br, cc), lambda i, cj_ref: (cj_ref[0], i, 0)))
    return _pc(body, name=name, grid_spec=grid_spec, out_shape=_sds((2, r, cc)),
               compiler_params=_params("arbitrary"))(cj_idx, p, *qs)


def _shard_of(both):
    return both.reshape((2 * both.shape[1],) + both.shape[2:])


def _adamw(w, g, m, v, name):
    r, cc = w.shape
    br = _row_block(r)
    if r * cc * 4 <= (1 << 20):
        br = r
    elif br * cc * 4 > (1 << 20) and br > 8:
        br = max(8, (1 << 20) // (cc * 4) // 8 * 8)
        while r % br:
            br -= 8
    c1 = 1.0 - ADAM_B1 ** ADAM_STEP
    c2 = 1.0 - ADAM_B2 ** ADAM_STEP

    def body(w_ref, g_ref, m_ref, v_ref, go_ref, d_ref, mo_ref, vo_ref):
        gg = g_ref[...]
        go_ref[...] = gg
        mn = ADAM_B1 * m_ref[...] + (1.0 - ADAM_B1) * gg
        vn = ADAM_B2 * v_ref[...] + (1.0 - ADAM_B2) * (gg * gg)
        mh = mn / c1
        vh = vn / c2
        d_ref[...] = -ADAM_LR * (mh / (jnp.sqrt(vh) + ADAM_EPS) + ADAM_WD * w_ref[...])
        mo_ref[...] = mn
        vo_ref[...] = vn

    spec = pl.BlockSpec((br, cc), lambda i: (i, 0))
    return _pc(body, name=name, grid=(r // br,), in_specs=[spec] * 4, out_specs=[spec] * 4,
               out_shape=[_sds((r, cc))] * 4, compiler_params=_params("arbitrary"))(w, g, m, v)


def _head(w_half, c8, small, w_ada, b_shard, c_ctx, ret_decay):
    ada_n = w_ada.shape[1]
    mod_sds = _sds((16, ada_n))
    ag_w, ag_c, ag_m = _AllGather([w_half]), _AllGather([c8, small]), _AllGather([mod_sds])
    n_w, n_c, n_m = len(ag_w.scratch), len(ag_c.scratch), len(ag_m.scratch)

    def body(w_ref, c_ref, s_ref, wada_ref, b_ref, cc_ref, rd_ref,
             gw_ref, call_ref, sall_ref, a_ref, modp_ref, mall_ref, lg_ref, sg_ref, *scr):
        scr_w, scr_c, scr_m = scr[:n_w], scr[n_w:n_w + n_c], scr[n_w + n_c:n_w + n_c + n_m]
        c_v, w_v, m_v, sems = scr[n_w + n_c + n_m:]
        ag_w.start((w_ref,), (gw_ref,), scr_w)
        ag_c.start((c_ref, s_ref), (call_ref, sall_ref), scr_c)
        load_w = pltpu.make_async_copy(wada_ref, w_v, sems.at[0])
        load_w.start()
        rd = rd_ref[...]
        lg_ref[...] = -_softplus(-rd)
        sg_ref[...] = _sigmoid(-rd)
        ag_c.relay((c_ref, s_ref), (call_ref, sall_ref), scr_c)
        ag_c.finish((c_ref, s_ref), (call_ref, sall_ref), scr_c)
        load_c = pltpu.make_async_copy(call_ref, c_v, sems.at[1])
        load_c.start()
        load_c.wait()
        a_ref[...] = jnp.zeros_like(a_ref)
        for d in range(N_DEV):
            cd = c_v[d, 0:1, :]
            a_ref[d:d + 1, :] = cd * _sigmoid(cd)
        cc = cc_ref[...]
        a_ref[N_DEV:N_DEV + 1, :] = cc * _sigmoid(cc)
        load_w.wait()
        m_v[...] = jnp.dot(a_ref[...], w_v[...], preferred_element_type=F32,
                           precision=lax.Precision.HIGHEST) + b_ref[...]
        put = pltpu.make_async_copy(m_v, modp_ref, sems.at[2])
        put.start()
        put.wait()
        ag_m.start((modp_ref,), (mall_ref,), scr_m)
        ag_m.relay((modp_ref,), (mall_ref,), scr_m)
        ag_m.finish((modp_ref,), (mall_ref,), scr_m)
        ag_w.relay((w_ref,), (gw_ref,), scr_w)
        ag_w.finish((w_ref,), (gw_ref,), scr_w)

    rd = jnp.broadcast_to(ret_decay.reshape(2, HEADS).T[:, :, None], (HEADS, 2, LANES))
    lane = _full((HEADS, 2, LANES))
    outs = _pc(
        body, name="head",
        in_specs=[ANY, ANY, ANY, ANY, _full((1, ada_n)), _full((1, D_MODEL)), lane],
        out_specs=[ANY, ANY, ANY, _full((16, D_MODEL)), ANY, ANY, lane, lane],
        out_shape=ag_w.out_shapes + ag_c.out_shapes + [_sds((16, D_MODEL)), mod_sds] + ag_m.out_shapes
        + [_sds((HEADS, 2, LANES))] * 2,
        scratch_shapes=ag_w.scratch + ag_c.scratch + ag_m.scratch
        + [pltpu.VMEM((N_DEV,) + c8.shape, F32), pltpu.VMEM(w_ada.shape, F32), pltpu.VMEM((16, ada_n), F32),
           pltpu.SemaphoreType.DMA((3,))],
        compiler_params=_params(),
    )(w_half, c8, small, w_ada, b_shard, c_ctx.reshape(1, D_MODEL), rd)
    gw, c_all, small_all, a16, _, mod_all, lgv, sgv = outs
    return gw, c_all, small_all, a16, mod_all, lgv, sgv


def _ada_grad(at, b):
    n = b.shape[1]
    bn = 512

    def body(a_ref, b_ref, o_ref):
        o_ref[...] = jnp.dot(a_ref[...], b_ref[...], preferred_element_type=F32, precision=lax.Precision.HIGHEST)

    return _pc(body, name="ada_grad", grid=(n // bn,),
               in_specs=[_full((D_MODEL, LANES)), pl.BlockSpec((LANES, bn), lambda i: (0, i))],
               out_specs=pl.BlockSpec((D_MODEL, bn), lambda i: (0, i)), out_shape=_sds((D_MODEL, n)),
               compiler_params=_params("arbitrary"))(at, b)


def _cctx_partial(dmc8, w_ada):
    n = w_ada.shape[1]
    bn = 512

    def body(d_ref, w_ref, o_ref):
        @pl.when(pl.program_id(0) == 0)
        def _():
            o_ref[...] = jnp.zeros_like(o_ref)
        o_ref[...] += lax.dot_general(d_ref[...], w_ref[...], (((1,), (1,)), ((), ())),
                                      preferred_element_type=F32, precision=lax.Precision.HIGHEST)

    return _pc(body, name="cctx_partial", grid=(n // bn,),
               in_specs=[pl.BlockSpec((8, bn), lambda i: (0, i)), pl.BlockSpec((D_MODEL, bn), lambda i: (0, i))],
               out_specs=_full((8, D_MODEL)), out_shape=_sds((8, D_MODEL)),
               compiler_params=_params("arbitrary"))(dmc8, w_ada)


def _cctx_final(parts, c_ctx, m, v):
    c1 = 1.0 - ADAM_B1 ** ADAM_STEP
    c2 = 1.0 - ADAM_B2 ** ADAM_STEP

    def body(p_ref, c_ref, m_ref, v_ref, g_ref, d_ref, mo_ref, vo_ref):
        s = ((p_ref[0, 0:1, :] + p_ref[2, 0:1, :]) + p_ref[4, 0:1, :]) + p_ref[6, 0:1, :]
        z = c_ref[...]
        sg = _sigmoid(z)
        gg = s * (sg * (1.0 + z * (1.0 - sg)))
        g_ref[...] = gg
        mn = ADAM_B1 * m_ref[...] + (1.0 - ADAM_B1) * gg
        vn = ADAM_B2 * v_ref[...] + (1.0 - ADAM_B2) * (gg * gg)
        d_ref[...] = -ADAM_LR * ((mn / c1) / (jnp.sqrt(vn / c2) + ADAM_EPS) + ADAM_WD * z)
        mo_ref[...] = mn
        vo_ref[...] = vn

    row = _full((1, D_MODEL))
    return _pc(body, name="cctx_final", out_shape=[_sds((1, D_MODEL))] * 4,
               in_specs=[_full(parts.shape), row, row, row], out_specs=[row] * 4,
               compiler_params=_params())(parts, c_ctx.reshape(1, D_MODEL), m.reshape(1, D_MODEL), v.reshape(1, D_MODEL))


def _rotary_tables(t_len):
    rows = t_len // GRID_W
    n_freq = DH // 4
    inv = ROPE_BASE ** (-jnp.arange(n_freq, dtype=F32) / n_freq)
    row_ang = jnp.arange(rows, dtype=F32)[:, None] * inv
    col_ang = jnp.arange(GRID_W, dtype=F32)[:, None] * inv

    def spread(fn):
        return jnp.concatenate([jnp.repeat(fn(row_ang), GRID_W, axis=0), jnp.tile(fn(col_ang), (rows, 1))], axis=-1)

    cos, sin = spread(jnp.cos), spread(jnp.sin)
    return jnp.concatenate([cos, cos], axis=-1), jnp.concatenate([-sin, sin], axis=-1)


def _inproj_fwd(x, gn, sh, sc, w4, cos2, sin2, name, comms=()):
    t = x.shape[0]
    tm = _tile(t, True)
    nc = IN_COLS // N_CHIP

    def body(x_ref, gn_ref, sh_ref, sc_ref, w_ref, c_ref, s_ref, p_ref, xr_ref, hb_ref, p_s):
        xh, _ = _rms(x_ref[...])
        h = xh * gn_ref[...] * (1.0 + sc_ref[...]) + sh_ref[...]
        hb = h.astype(BF16)
        hb_ref[...] = hb
        for j in range(N_CHIP):
            p_s[:, nc * j:nc * (j + 1)] = _dot(hb, w_ref[j])
        cc = c_ref[...]
        ss = s_ref[...]
        for hh in range(2 * HEADS):
            blk = p_s[:, DH * hh:DH * (hh + 1)]
            rot = blk * cc + pltpu.roll(blk, DH // 2, 1) * ss
            if hh >= HEADS:
                rot = rot * K_SCALE
            p_ref[:, DH * hh:DH * (hh + 1)] = rot.astype(BF16)
        p_ref[:, 2 * RET_W:] = p_s[:, 2 * RET_W:].astype(BF16)
        xr_ref[...] = p_s[:, 4 * RET_W:4 * RET_W + LRU_W]

    row = _full((1, D_MODEL))
    outs, couts = _call(
        body, name=name, grid=(t // tm,),
        in_specs=[pl.BlockSpec((tm, D_MODEL), lambda i: (i, 0)), row, row, row, _full(w4.shape),
                  pl.BlockSpec((tm, DH), lambda i: (i, 0)), pl.BlockSpec((tm, DH), lambda i: (i, 0))],
        out_specs=[pl.BlockSpec((tm, IN_COLS), lambda i: (i, 0)), pl.BlockSpec((tm, LRU_W), lambda i: (i, 0)),
                   pl.BlockSpec((tm, D_MODEL), lambda i: (i, 0))],
        out_shape=[_sds((t, IN_COLS), BF16), _sds((t, LRU_W)), _sds((t, D_MODEL), BF16)],
        scratch_shapes=[pltpu.VMEM((tm, IN_COLS), F32)], sem=("arbitrary",),
        args=(x, gn, sh, sc, w4, cos2, sin2), comms=comms)
    return (outs, couts) if comms else outs


def _inproj_bwd(x, gn, sh, sc, w4, cos2, sin2, pieces, dres, name):
    t = x.shape[0]
    tm = _tile(t)
    nc = IN_COLS // N_CHIP

    def body(x_ref, gn_ref, sh_ref, sc_ref, w_ref, c_ref, s_ref, dqf, dqb, dkf, dkb, dvf, dvb, dg, dxr, dgt, dres_ref,
             dx_ref, dpb_ref, dgn_ref, dsh_ref, dsc_ref):
        cc = c_ref[...]
        ss = s_ref[...]
        dq = dqf[...].astype(F32) + dqb[...].astype(F32)
        dk = dkf[...].astype(F32) + dkb[...].astype(F32)
        for hh in range(HEADS):
            sl = slice(DH * hh, DH * (hh + 1))
            b = dq[:, sl]
            dpb_ref[:, sl] = (b * cc + pltpu.roll(b * ss, DH // 2, 1)).astype(BF16)
            b = dk[:, sl]
            dpb_ref[:, RET_W + DH * hh:RET_W + DH * (hh + 1)] = (
                (b * cc + pltpu.roll(b * ss, DH // 2, 1)) * K_SCALE).astype(BF16)
        dpb_ref[:, 2 * RET_W:3 * RET_W] = (dvf[...].astype(F32) + dvb[...].astype(F32)).astype(BF16)
        dpb_ref[:, 3 * RET_W:4 * RET_W] = dg[...].astype(BF16)
        dpb_ref[:, 4 * RET_W:4 * RET_W + LRU_W] = dxr[...].astype(BF16)
        dpb_ref[:, 4 * RET_W + LRU_W:IN_COLS] = dgt[...].astype(BF16)
        dh = _dot_nt(dpb_ref[:, 0:nc], w_ref[0])
        for j in range(1, N_CHIP):
            dh = dh + _dot_nt(dpb_ref[:, nc * j:nc * (j + 1)], w_ref[j])
        dx, dgn_t, dsh_t, dsc_t = _norm_mod_bwd(x_ref[...], gn_ref[...], sc_ref[...], dh)
        dx_ref[...] = dres_ref[...] + dx

        @pl.when(pl.program_id(0) == 0)
        def _():
            dgn_ref[...] = jnp.zeros_like(dgn_ref)
            dsh_ref[...] = jnp.zeros_like(dsh_ref)
            dsc_ref[...] = jnp.zeros_like(dsc_ref)
        dgn_ref[...] += dgn_t
        dsh_ref[...] += dsh_t
        dsc_ref[...] += dsc_t

    row = _full((1, D_MODEL))
    pc = pl.BlockSpec((tm, RET_W), lambda i: (i, 0))
    big = pl.BlockSpec((tm, D_MODEL), lambda i: (i, 0))
    return _pc(body, name=name, grid=(t // tm,),
               in_specs=[big, row, row, row, _full(w4.shape),
                         pl.BlockSpec((tm, DH), lambda i: (i, 0)), pl.BlockSpec((tm, DH), lambda i: (i, 0))]
               + [pc] * 9 + [big],
               out_specs=[big, pl.BlockSpec((tm, IN_COLS), lambda i: (i, 0)), row, row, row],
               out_shape=[_sds((t, D_MODEL)), _sds((t, IN_COLS), BF16), _sds((1, D_MODEL)), _sds((1, D_MODEL)),
                          _sds((1, D_MODEL))],
               compiler_params=_params("arbitrary"))(x, gn, sh, sc, w4, cos2, sin2, *pieces, dres)


def _halo_specs(t, tm):
    n8 = tm // SUBLANES
    last8 = t // SUBLANES - 1
    prev = pl.BlockSpec((SUBLANES, LRU_W), lambda i: (jnp.maximum(i * n8 - 1, 0), 0))
    main = pl.BlockSpec((tm, LRU_W), lambda i: (i, 0))
    nxt = pl.BlockSpec((SUBLANES, LRU_W), lambda i: (jnp.minimum((i + 1) * n8, last8), 0))
    return prev, main, nxt


def _with_halo(prev_ref, main_ref, next_ref, i, nt):
    prev = jnp.where(i > 0, prev_ref[...], 0.0)
    nxt = jnp.where(i < nt - 1, next_ref[...], 0.0)
    return jnp.concatenate([prev, main_ref[...], nxt], axis=0)


def _conv_fwd(xr, cw, cb, name):
    t = xr.shape[0]
    tm = _tile(t, True)
    nt = t // tm
    n = tm + 2 * SUBLANES
    mid = slice(SUBLANES, SUBLANES + tm)

    def body(p_ref, m_ref, n_ref, w_ref, b_ref, o_ref):
        xp = _with_halo(p_ref, m_ref, n_ref, pl.program_id(0), nt)
        acc = b_ref[...] + pltpu.roll(xp, 1, 0)[mid] * w_ref[0:1, :]
        acc = acc + xp[mid] * w_ref[1:2, :]
        acc = acc + pltpu.roll(xp, n - 1, 0)[mid] * w_ref[2:3, :]
        acc = acc + pltpu.roll(xp, n - 2, 0)[mid] * w_ref[3:4, :]
        o_ref[...] = acc

    return _pc(body, name=name, grid=(nt,),
               in_specs=[*_halo_specs(t, tm), _full((4, LRU_W)), _full((1, LRU_W))],
               out_specs=pl.BlockSpec((tm, LRU_W), lambda i: (i, 0)), out_shape=_sds((t, LRU_W)),
               compiler_params=_params("arbitrary"))(xr, xr, xr, cw, cb)


def _conv_bwd(dxc_a, dxc_b, xr, cw, name):
    t = xr.shape[0]
    tm = _tile(t, True)
    nt = t // tm
    n = tm + 2 * SUBLANES
    mid = slice(SUBLANES, SUBLANES + tm)

    def body(ap_ref, am_ref, an_ref, bp_ref, bm_ref, bn_ref, xp_ref, xm_ref, xn_ref, w_ref, dx_ref, dw_ref, db_ref):
        i = pl.program_id(0)
        dp = _with_halo(ap_ref, am_ref, an_ref, i, nt) + _with_halo(bp_ref, bm_ref, bn_ref, i, nt)
        xp = _with_halo(xp_ref, xm_ref, xn_ref, i, nt)
        dx = pltpu.roll(dp, n - 1, 0)[mid] * w_ref[0:1, :]
        dx = dx + dp[mid] * w_ref[1:2, :]
        dx = dx + pltpu.roll(dp, 1, 0)[mid] * w_ref[2:3, :]
        dx = dx + pltpu.roll(dp, 2, 0)[mid] * w_ref[3:4, :]
        dx_ref[...] = dx.astype(BF16)
        d = dp[mid]

        @pl.when(i == 0)
        def _():
            dw_ref[...] = jnp.zeros_like(dw_ref)
            db_ref[...] = jnp.zeros_like(db_ref)
        dw_ref[0:1, :] += _sum0(d * pltpu.roll(xp, 1, 0)[mid])
        dw_ref[1:2, :] += _sum0(d * xp[mid])
        dw_ref[2:3, :] += _sum0(d * pltpu.roll(xp, n - 1, 0)[mid])
        dw_ref[3:4, :] += _sum0(d * pltpu.roll(xp, n - 2, 0)[mid])
        db_ref[...] += _sum0(d)

    return _pc(body, name=name, grid=(nt,),
               in_specs=[*_halo_specs(t, tm), *_halo_specs(t, tm), *_halo_specs(t, tm), _full((4, LRU_W))],
               out_specs=[pl.BlockSpec((tm, LRU_W), lambda i: (i, 0)), _full((4, LRU_W)), _full((1, LRU_W))],
               out_shape=[_sds((t, LRU_W), BF16), _sds((4, LRU_W)), _sds((1, LRU_W))],
               compiler_params=_params("arbitrary"))(dxc_a, dxc_a, dxc_a, dxc_b, dxc_b, dxc_b, xr, xr, xr, cw)


def _local_scan(a, b, reverse):
    n = a.shape[0]
    row = lax.broadcasted_iota(jnp.int32, a.shape, 0) & (SUBLANES - 1)
    for s in (1, 2, 4):
        if reverse:
            a_s, b_s, ok = pltpu.roll(a, n - s, 0), pltpu.roll(b, n - s, 0), row < SUBLANES - s
        else:
            a_s, b_s, ok = pltpu.roll(a, s, 0), pltpu.roll(b, s, 0), row >= s
        b = a * jnp.where(ok, b_s, 0.0) + b
        a = a * jnp.where(ok, a_s, 1.0)
    return a, b


def _carry_scan(a_s, b_s, out_ref, carry, reverse):
    ng = a_s.shape[0] // SUBLANES
    shape = carry.shape

    def step(g, cr):
        gg = (ng - 1 - g) if reverse else g
        off = pl.multiple_of(gg * SUBLANES, SUBLANES)
        h = a_s[pl.ds(off, SUBLANES), :] * cr + b_s[pl.ds(off, SUBLANES), :]
        out_ref[pl.ds(off, SUBLANES), :] = h
        edge = h[0:1, :] if reverse else h[SUBLANES - 1:SUBLANES, :]
        return jnp.broadcast_to(edge, shape)

    return lax.fori_loop(0, ng, step, carry)


def _lru_gates(xc, wa_ref, wx_ref, ba, bx, lam):
    xb = xc.astype(BF16)
    r = _sigmoid(_dot(xb, wa_ref[...]) + ba)
    ig = _sigmoid(_dot(xb, wx_ref[...]) + bx)
    sp = _softplus(-lam)
    la = -LRU_C * r * sp
    a = jnp.exp(la)
    mult = jnp.sqrt(_one_minus_sq(la, a))
    return r, ig, sp, a, mult


def _lru_fwd(xc, wa, wx, ba, bx, lam, h0, reverse, name, comms=()):
    t = xc.shape[0]
    tm = _tile(t, True)
    nt = t // tm
    tidx = (lambda i: (nt - 1 - i, 0)) if reverse else (lambda i: (i, 0))

    def body(x_ref, wa_ref, wx_ref, ba_ref, bx_ref, lam_ref, h0_ref, h_ref, a_s, b_s, c_s):
        @pl.when(pl.program_id(0) == 0)
        def _():
            c_s[...] = jnp.broadcast_to(h0_ref[...], c_s.shape)
        xv = x_ref[...]
        _, ig, _, a, mult = _lru_gates(xv, wa_ref, wx_ref, ba_ref[...], bx_ref[...], lam_ref[...])
        al, bl = _local_scan(a, mult * (ig * xv), reverse)
        a_s[...] = al
        b_s[...] = bl
        c_s[...] = _carry_scan(a_s, b_s, h_ref, c_s[...], reverse)

    vec = _full((1, LRU_W))
    mat = _full((LRU_W, LRU_W))
    (h,), couts = _call(body, name=name, grid=(nt,),
                        in_specs=[pl.BlockSpec((tm, LRU_W), tidx), mat, mat, vec, vec, vec, vec],
                        out_specs=[pl.BlockSpec((tm, LRU_W), tidx)], out_shape=[_sds((t, LRU_W))],
                        scratch_shapes=[pltpu.VMEM((tm, LRU_W), F32), pltpu.VMEM((tm, LRU_W), F32),
                                        pltpu.VMEM((SUBLANES, LRU_W), F32)],
                        sem=("arbitrary",), args=(xc, wa, wx, ba, bx, lam, h0), comms=comms)
    return (h, couts) if comms else h


def _lru_bwd(xc, wa, wx, ba, bx, lam, h, h0, dh, reverse, name, comms=()):
    t = xc.shape[0]
    tm = _tile(t, True)
    nt = t // tm
    n8 = tm // SUBLANES
    last8 = t // SUBLANES - 1
    tidx = (lambda i: (i, 0)) if reverse else (lambda i: (nt - 1 - i, 0))
    if reverse:
        halo = pl.BlockSpec((SUBLANES, LRU_W), lambda i: (jnp.minimum((i + 1) * n8, last8), 0))
    else:
        halo = pl.BlockSpec((SUBLANES, LRU_W), lambda i: (jnp.maximum((nt - 1 - i) * n8 - 1, 0), 0))

    def body(x_ref, wa_ref, wx_ref, ba_ref, bx_ref, lam_ref, h_ref, halo_ref, h0_ref, dh_ref,
             dx_ref, dpre_ref, dba_ref, dbx_ref, dlam_ref, dh0_ref, a_s, b_s, l_s, c_s, e_s):
        i = pl.program_id(0)

        @pl.when(i == 0)
        def _():
            c_s[...] = jnp.zeros_like(c_s)
            e_s[...] = jnp.zeros_like(e_s)
            dba_ref[...] = jnp.zeros_like(dba_ref)
            dbx_ref[...] = jnp.zeros_like(dbx_ref)
            dlam_ref[...] = jnp.zeros_like(dlam_ref)
        xv = x_ref[...]
        lam = lam_ref[...]
        r, ig, sp, a, mult = _lru_gates(xv, wa_ref, wx_ref, ba_ref[...], bx_ref[...], lam)
        hv = h_ref[...]
        rowi = lax.broadcasted_iota(jnp.int32, (tm, LRU_W), 0)
        edge_a = jnp.broadcast_to(e_s[0:1, :], (tm, LRU_W))
        h0b = jnp.broadcast_to(h0_ref[...], (tm, LRU_W))
        if reverse:
            a_sh = jnp.where(rowi == 0, edge_a, pltpu.roll(a, 1, 0))
            hin_edge = jnp.where(i == nt - 1, h0b, jnp.broadcast_to(halo_ref[0:1, :], (tm, LRU_W)))
            h_in = jnp.where(rowi == tm - 1, hin_edge, pltpu.roll(hv, tm - 1, 0))
        else:
            a_sh = jnp.where(rowi == tm - 1, edge_a, pltpu.roll(a, tm - 1, 0))
            hin_edge = jnp.where(i == nt - 1, h0b, jnp.broadcast_to(halo_ref[SUBLANES - 1:SUBLANES, :], (tm, LRU_W)))
            h_in = jnp.where(rowi == 0, hin_edge, pltpu.roll(hv, 1, 0))
        al, bl = _local_scan(a_sh, dh_ref[...], not reverse)
        a_s[...] = al
        b_s[...] = bl
        c_s[...] = _carry_scan(a_s, b_s, l_s, c_s[...], not reverse)
        e_s[...] = jnp.broadcast_to(a[tm - 1:tm, :] if reverse else a[0:1, :], e_s.shape)
        lmb = l_s[...]
        da = lmb * h_in
        ixc = ig * xv
        dmult = lmb * ixc
        dixc = lmb * mult
        dla = da * a - dmult * (a * a) / mult
        dpr = dla * (-LRU_C * sp) * r * (1.0 - r)
        dpi = dixc * xv * ig * (1.0 - ig)
        dprb = dpr.astype(BF16)
        dpib = dpi.astype(BF16)
        dpre_ref[:, 0:LRU_W] = dprb
        dpre_ref[:, LRU_W:2 * LRU_W] = dpib
        dx_ref[...] = dixc * ig + _dot_nt(dprb, wa_ref[...]) + _dot_nt(dpib, wx_ref[...])
        dba_ref[...] += _sum0(dpr)
        dbx_ref[...] += _sum0(dpi)
        dlam_ref[...] += _sum0(dla * (-LRU_C * r)) * (-_sigmoid(-lam))

        @pl.when(i == nt - 1)
        def _():
            al0 = a * lmb
            dh0_ref[...] = al0[tm - 1:tm, :] if reverse else al0[0:1, :]

    vec = _full((1, LRU_W))
    mat = _full((LRU_W, LRU_W))
    tile = pl.BlockSpec((tm, LRU_W), tidx)
    return _call(body, name=name, grid=(nt,),
                 in_specs=[tile, mat, mat, vec, vec, vec, tile, halo, vec, tile],
                 out_specs=[tile, pl.BlockSpec((tm, 2 * LRU_W), tidx), vec, vec, vec, vec],
                 out_shape=[_sds((t, LRU_W)), _sds((t, 2 * LRU_W), BF16), _sds((1, LRU_W)), _sds((1, LRU_W)),
                            _sds((1, LRU_W)), _sds((1, LRU_W))],
                 scratch_shapes=[pltpu.VMEM((tm, LRU_W), F32), pltpu.VMEM((tm, LRU_W), F32),
                                 pltpu.VMEM((tm, LRU_W), F32), pltpu.VMEM((SUBLANES, LRU_W), F32),
                                 pltpu.VMEM((SUBLANES, LRU_W), F32)],
                 sem=("arbitrary",), args=(xc, wa, wx, ba, bx, lam, h, h, h0, dh), comms=comms)


def _decay_tables(lg, reverse):
    ci = lax.broadcasted_iota(jnp.int32, (CHUNK, CHUNK), 0).astype(F32)
    mi = lax.broadcasted_iota(jnp.int32, (CHUNK, CHUNK), 1).astype(F32)
    rel = (mi - ci) if reverse else (ci - mi)
    relc = jnp.maximum(rel, 0.0)
    lg_c = jnp.concatenate([lg] * (CHUNK // LANES), axis=1)
    dm = jnp.where(rel >= 0, jnp.exp(lg_c * relc), 0.0)
    cd = lax.broadcasted_iota(jnp.int32, (CHUNK, DH), 0).astype(F32)
    pq, ps = (CHUNK - cd, cd) if reverse else (cd + 1.0, CHUNK - 1.0 - cd)
    return relc, dm, jnp.exp(lg * pq), jnp.exp(lg * ps), jnp.exp(lg * float(CHUNK)), pq, ps


def _ret_fwd(proj, lgv, s0f, s0b, comms=()):
    t = proj.shape[0]
    n = t // CHUNK

    def one(q, k, v, lg, s_s, hh, o_ref, sp_ref, reverse):
        _, dm, wq, ws, g, _, _ = _decay_tables(lg, reverse)
        vb = v.astype(BF16)
        p = _dot_nt(q.astype(BF16), k.astype(BF16)) * dm
        s = s_s[hh]
        sp_ref[hh, 0] = s
        o_ref[:, DH * hh:DH * (hh + 1)] = _dot(p.astype(BF16), vb) + _dot((q * wq).astype(BF16), s.astype(BF16))
        s_s[hh] = g * s + _dot_tn((k * ws).astype(BF16), vb)

    def body(qf, kf, vf, qb, kb, vb, lg_ref, s0f_ref, s0b_ref, of_ref, ob_ref, spf_ref, spb_ref, sf_s, sb_s):
        @pl.when(pl.program_id(0) == 0)
        def _():
            sf_s[...] = s0f_ref[...]
            sb_s[...] = s0b_ref[...]
        for hh in range(HEADS):
            sl = slice(DH * hh, DH * (hh + 1))
            one(qf[:, sl].astype(F32), kf[:, sl].astype(F32), vf[:, sl], lg_ref[hh, 0:1, :], sf_s, hh, of_ref, spf_ref,
                False)
            one(qb[:, sl].astype(F32), kb[:, sl].astype(F32), vb[:, sl], lg_ref[hh, 1:2, :], sb_s, hh, ob_ref, spb_ref,
                True)

    blk = (CHUNK, RET_W)
    fw = [pl.BlockSpec(blk, lambda i, o=o: (i, o)) for o in range(3)]
    bw = [pl.BlockSpec(blk, lambda i, o=o: (n - 1 - i, o)) for o in range(3)]
    st = _full((HEADS, DH, DH))
    return _call(body, name="ret_fwd", grid=(n,),
                 in_specs=fw + bw + [_full((HEADS, 2, LANES)), st, st],
                 out_specs=[pl.BlockSpec(blk, lambda i: (i, 0)), pl.BlockSpec(blk, lambda i: (n - 1 - i, 0)),
                            pl.BlockSpec((HEADS, 1, DH, DH), lambda i: (0, i, 0, 0)),
                            pl.BlockSpec((HEADS, 1, DH, DH), lambda i: (0, n - 1 - i, 0, 0))],
                 out_shape=[_sds((t, RET_W)), _sds((t, RET_W)), _sds((HEADS, n, DH, DH)), _sds((HEADS, n, DH, DH))],
                 scratch_shapes=[pltpu.VMEM((HEADS, DH, DH), F32), pltpu.VMEM((HEADS, DH, DH), F32)],
                 sem=("arbitrary",), args=(proj, proj, proj, proj, proj, proj, lgv, s0f, s0b), comms=comms)


def _ret_bwd(proj, lgv, sgv, sprev, do, reverse, name, comms=()):
    t = proj.shape[0]
    n = t // CHUNK
    d = 1 if reverse else 0
    cidx = (lambda i: i) if reverse else (lambda i: n - 1 - i)

    def body(q_ref, k_ref, v_ref, lg_ref, sg_ref, s_ref, do_ref, dq_ref, dk_ref, dv_ref, ds0_ref, drd_ref, ds_s, acc_s):
        i = pl.program_id(0)

        @pl.when(i == 0)
        def _():
            ds_s[...] = jnp.zeros_like(ds_s)
            acc_s[...] = jnp.zeros_like(acc_s)
        for hh in range(HEADS):
            sl = slice(DH * hh, DH * (hh + 1))
            relc, dm, wq, ws, g, pq, ps = _decay_tables(lg_ref[hh, d:d + 1, :], reverse)
            qb, kb, vb = q_ref[:, sl], k_ref[:, sl], v_ref[:, sl]
            q, k = qb.astype(F32), kb.astype(F32)
            p = _dot_nt(qb, kb) * dm
            s = s_ref[hh, 0]
            dob = do_ref[:, sl].astype(BF16)
            dsn = ds_s[hh]
            dsb = dsn.astype(BF16)
            dv_ref[:, sl] = (_dot_tn(p.astype(BF16), dob) + _dot((k * ws).astype(BF16), dsb)).astype(BF16)
            dp = _dot_nt(dob, vb)
            dab = (dp * dm).astype(BF16)
            xq = _dot_nt(dob, s.astype(BF16))
            yk = _dot_nt(vb, dsb)
            dq_ref[:, sl] = (_dot(dab, kb) + xq * wq).astype(BF16)
            dk_ref[:, sl] = (_dot_tn(dab, qb) + yk * ws).astype(BF16)
            ds_s[hh] = g * dsn + _dot_tn((q * wq).astype(BF16), dob)
            s_mask = _sum0(dp * p * relc)
            part = (sum(s_mask[:, LANES * u:LANES * (u + 1)] for u in range(CHUNK // LANES))
                    + _sum0(xq * q * wq * pq) + _sum0(yk * k * ws * ps) + _sum0(dsn * s) * g * float(CHUNK))
            acc_s[hh] += jnp.broadcast_to(part, (SUBLANES, LANES))

        @pl.when(i == n - 1)
        def _():
            ds0_ref[...] = ds_s[...]
            for hh in range(HEADS):
                tot = jnp.sum(acc_s[hh, 0:1, :], axis=1, keepdims=True)
                drd_ref[hh] = jnp.broadcast_to(tot, (SUBLANES, LANES)) * sg_ref[hh, d:d + 1, :]

    blk = (CHUNK, RET_W)
    qkv = [pl.BlockSpec(blk, lambda i, o=o: (cidx(i), o)) for o in range(3)]
    hc = pl.BlockSpec(blk, lambda i: (cidx(i), 0))
    lane = _full((HEADS, 2, LANES))
    return _call(body, name=name, grid=(n,),
                 in_specs=qkv + [lane, lane, pl.BlockSpec((HEADS, 1, DH, DH), lambda i: (0, cidx(i), 0, 0)), hc],
                 out_specs=[hc, hc, hc, _full((HEADS, DH, DH)), _full((HEADS, SUBLANES, LANES))],
                 out_shape=[_sds((t, RET_W), BF16)] * 3 + [_sds((HEADS, DH, DH)), _sds((HEADS, SUBLANES, LANES))],
                 scratch_shapes=[pltpu.VMEM((HEADS, DH, DH), F32), pltpu.VMEM((HEADS, SUBLANES, LANES), F32)],
                 sem=("arbitrary",), args=(proj, proj, proj, lgv, sgv, sprev, do), comms=comms)


def _ctx_weights(lg, l_len, reverse):
    pos = lax.broadcasted_iota(jnp.int32, (l_len, DH), 0).astype(F32)
    steps = pos if reverse else (l_len - 1.0 - pos)
    return jnp.exp(lg * steps), steps


def _ctx_state_fwd(projc, lgv):
    l_len = projc.shape[0]

    def body(k_ref, v_ref, lg_ref, sf_ref, sb_ref):
        k = k_ref[...]
        vb = v_ref[...].astype(BF16)
        for d, o_ref in ((0, sf_ref), (1, sb_ref)):
            w, _ = _ctx_weights(lg_ref[0, d:d + 1, :], l_len, d == 1)
            o_ref[0] = _dot_tn((k * w).astype(BF16), vb)

    st = pl.BlockSpec((1, DH, DH), lambda h: (h, 0, 0))
    return _pc(body, name="ctx_state_fwd", grid=(HEADS,),
               in_specs=[pl.BlockSpec((l_len, DH), lambda h: (0, HEADS + h)),
                         pl.BlockSpec((l_len, DH), lambda h: (0, 2 * HEADS + h)),
                         pl.BlockSpec((1, 2, LANES), lambda h: (h, 0, 0))],
               out_specs=[st, st], out_shape=[_sds((HEADS, DH, DH))] * 2,
               compiler_params=_params("arbitrary"))(projc, projc, lgv)


def _ctx_state_bwd(projc, lgv, sgv, dsf, dsb):
    l_len = projc.shape[0]

    def body(k_ref, v_ref, lg_ref, sg_ref, dsf_ref, dsb_ref, dk_ref, dv_ref, drd_ref):
        k = k_ref[...]
        vb = v_ref[...].astype(BF16)
        dk = jnp.zeros((l_len, DH), F32)
        dv = jnp.zeros((l_len, DH), F32)
        rows = []
        for d, ds_ref in ((0, dsf_ref), (1, dsb_ref)):
            w, steps = _ctx_weights(lg_ref[0, d:d + 1, :], l_len, d == 1)
            dsb16 = ds_ref[0].astype(BF16)
            dkw = _dot_nt(vb, dsb16)
            dk = dk + dkw * w
            dv = dv + _dot((k * w).astype(BF16), dsb16)
            tot = jnp.sum(_sum0(dkw * k * w * steps), axis=1, keepdims=True)
            rows.append(jnp.broadcast_to(tot, (1, LANES)) * sg_ref[0, d:d + 1, :])
        dk_ref[...] = dk.astype(BF16)
        dv_ref[...] = dv.astype(BF16)
        rid = lax.broadcasted_iota(jnp.int32, (SUBLANES, LANES), 0)
        drd_ref[0] = jnp.where(rid == 0, rows[0], jnp.where(rid == 1, rows[1], 0.0))

    st = pl.BlockSpec((1, DH, DH), lambda h: (h, 0, 0))
    lane = pl.BlockSpec((1, 2, LANES), lambda h: (h, 0, 0))
    hc = pl.BlockSpec((l_len, DH), lambda h: (0, h))
    return _pc(body, name="ctx_state_bwd", grid=(HEADS,),
               in_specs=[pl.BlockSpec((l_len, DH), lambda h: (0, HEADS + h)),
                         pl.BlockSpec((l_len, DH), lambda h: (0, 2 * HEADS + h)), lane, lane, st, st],
               out_specs=[hc, hc, pl.BlockSpec((1, SUBLANES, LANES), lambda h: (h, 0, 0))],
               out_shape=[_sds((l_len, RET_W), BF16), _sds((l_len, RET_W), BF16), _sds((HEADS, SUBLANES, LANES))],
               compiler_params=_params("arbitrary"))(projc, projc, lgv, sgv, dsf, dsb)


G_BLOCK = (3 * RET_W) // RET_W
GATE_BLOCK = (4 * RET_W + LRU_W) // LRU_W


def _head_norm(y):
    yc = y - jnp.mean(y, axis=-1, keepdims=True)
    rs = lax.rsqrt(jnp.mean(yc * yc, axis=-1, keepdims=True) + EPS)
    return yc * rs, rs


def _gelu_parts(z):
    th = jnp.tanh(GELU_K * (z + GELU_C * z * z * z))
    return 0.5 * z * (1.0 + th), th


def _mix_fwd(o_f, o_b, proj, hf, hb, w_out, x, g1):
    t = x.shape[0]
    tm = _tile(t, True)

    def body(of_ref, ob_ref, g_ref, gt_ref, hf_ref, hb_ref, w_ref, x_ref, g1_ref, x1_ref, cat_ref):
        o = of_ref[...] + ob_ref[...]
        g = g_ref[...].astype(F32)
        for hh in range(HEADS):
            sl = slice(DH * hh, DH * (hh + 1))
            nrm, _ = _head_norm(o[:, sl])
            gh = g[:, sl]
            cat_ref[:, sl] = (gh * _sigmoid(gh) * nrm).astype(BF16)
        gel, _ = _gelu_parts(gt_ref[...].astype(F32))
        cat_ref[:, RET_W:] = ((hf_ref[...] + hb_ref[...]) * gel).astype(BF16)
        x1_ref[...] = x_ref[...] + g1_ref[...] * _dot(cat_ref[...], w_ref[...])

    half = pl.BlockSpec((tm, RET_W), lambda i: (i, 0))
    big = pl.BlockSpec((tm, D_MODEL), lambda i: (i, 0))
    return _pc(body, name="mix_fwd", grid=(t // tm,),
               in_specs=[half, half, pl.BlockSpec((tm, RET_W), lambda i: (i, G_BLOCK)),
                         pl.BlockSpec((tm, LRU_W), lambda i: (i, GATE_BLOCK)), half, half,
                         _full((D_MODEL, D_MODEL)), big, _full((1, D_MODEL))],
               out_specs=[big, big], out_shape=[_sds((t, D_MODEL)), _sds((t, D_MODEL), BF16)],
               compiler_params=_params("arbitrary"))(o_f, o_b, proj, proj, hf, hb, w_out, x, g1)


def _mix_bwd(o_f, o_b, proj, hf, hb, w_out, cat, dx1, g1, comms=()):
    t = dx1.shape[0]
    tm = _tile(t, True)

    def body(of_ref, ob_ref, g_ref, gt_ref, hf_ref, hb_ref, w_ref, cat_ref, dx1_ref, g1_ref,
             do_ref, dhs_ref, dg_ref, dgt_ref, dyb_ref, dg1_ref):
        dx1v = dx1_ref[...]
        y = _dot(cat_ref[...], w_ref[...])

        @pl.when(pl.program_id(0) == 0)
        def _():
            dg1_ref[...] = jnp.zeros_like(dg1_ref)
        dg1_ref[...] += _sum0(dx1v * y)
        dyb = (g1_ref[...] * dx1v).astype(BF16)
        dyb_ref[...] = dyb
        dcat = _dot_nt(dyb, w_ref[...])
        o = of_ref[...] + ob_ref[...]
        g = g_ref[...].astype(F32)
        for hh in range(HEADS):
            sl = slice(DH * hh, DH * (hh + 1))
            nrm, rs = _head_norm(o[:, sl])
            gh = g[:, sl]
            sg = _sigmoid(gh)
            dret = dcat[:, sl]
            dg_ref[:, sl] = (dret * nrm * (sg * (1.0 + gh * (1.0 - sg)))).astype(BF16)
            dn = dret * (gh * sg)
            dyc = rs * (dn - nrm * jnp.mean(dn * nrm, axis=-1, keepdims=True))
            do_ref[:, sl] = (dyc - jnp.mean(dyc, axis=-1, keepdims=True)).astype(BF16)
        z = gt_ref[...].astype(F32)
        gel, th = _gelu_parts(z)
        dlru = dcat[:, RET_W:]
        dhs_ref[...] = dlru * gel
        dgel = 0.5 * (1.0 + th) + 0.5 * z * (1.0 - th * th) * GELU_K * (1.0 + 3.0 * GELU_C * z * z)
        dgt_ref[...] = (dlru * (hf_ref[...] + hb_ref[...]) * dgel).astype(BF16)

    half = pl.BlockSpec((tm, RET_W), lambda i: (i, 0))
    big = pl.BlockSpec((tm, D_MODEL), lambda i: (i, 0))
    return _call(body, name="mix_bwd", grid=(t // tm,),
                 in_specs=[half, half, pl.BlockSpec((tm, RET_W), lambda i: (i, G_BLOCK)),
                           pl.BlockSpec((tm, LRU_W), lambda i: (i, GATE_BLOCK)), half, half,
                           _full((D_MODEL, D_MODEL)), big, big, _full((1, D_MODEL))],
                 out_specs=[half, half, half, half, big, _full((1, D_MODEL))],
                 out_shape=[_sds((t, RET_W), BF16), _sds((t, RET_W)), _sds((t, RET_W), BF16), _sds((t, RET_W), BF16),
                            _sds((t, D_MODEL), BF16), _sds((1, D_MODEL))],
                 scratch_shapes=[], sem=("arbitrary",), args=(o_f, o_b, proj, proj, hf, hb, w_out, cat, dx1, g1),
                 comms=comms)


def _mlp(x1, n2g, sh2, sc2, g2, fg, w1_parts, w2_parts, tgt):
    t = x1.shape[0]
    tm = _tile(t)
    hb_ = MLP_H // N_CHIP
    q_rows = hb_ // 4
    n_cp = 4 * N_DEV

    def body(x1_ref, n2g_ref, sh2_ref, sc2_ref, g2_ref, fg_ref, w1a, w1b, w2a, w2b, tgt_ref,
             dx1_ref, h2b_ref, ab_ref, dub_ref, dmb_ref, dsc_ref, dsh_ref, dg2_ref, dn2_ref, dfg_ref, loss_ref,
             w1_s, w2_s, r_s, sems):
        @pl.when(pl.program_id(0) == 0)
        def _():
            cps = []
            for p, parts in enumerate(((w1a, w2a), (w1b, w2b))):
                for d in range(N_DEV):
                    rows = pl.ds(2 * q_rows * (d % 2) + q_rows * p, q_rows)
                    for src, dst in zip(parts, (w1_s, w2_s)):
                        cps.append(pltpu.make_async_copy(src.at[d], dst.at[d // 2, rows], sems.at[len(cps)]))
            for cp in cps:
                cp.start()
            for r in (dsc_ref, dsh_ref, dg2_ref, dn2_ref, dfg_ref, loss_ref):
                r[...] = jnp.zeros_like(r)
            for cp in cps:
                cp.wait()
        x1v = x1_ref[...]
        n2g, sc2, g2, fg = n2g_ref[...], sc2_ref[...], g2_ref[...], fg_ref[...]
        xh, _ = _rms(x1v)
        h2b = (xh * n2g * (1.0 + sc2) + sh2_ref[...]).astype(BF16)
        h2b_ref[...] = h2b
        m = jnp.zeros((tm, D_MODEL), F32)
        for j in range(N_CHIP):
            sl = slice(hb_ * j, hb_ * (j + 1))
            r = jnp.maximum(_dot(h2b, w1_s[j]), 0.0)
            r_s[:, sl] = r
            ab = (r * r).astype(BF16)
            ab_ref[:, sl] = ab
            m = m + _dot(ab, w2_s[j])
        x2 = x1v + g2 * m
        x2h, r2 = _rms(x2)
        err = x2h * fg - tgt_ref[...]
        loss_ref[...] += _sum0(err * err)
        dout = err * (1.0 / D_MODEL)
        dfg_ref[...] += _sum0(dout * x2h)
        dxh = dout * fg
        dx2 = r2 * (dxh - x2h * jnp.mean(dxh * x2h, axis=-1, keepdims=True))
        dg2_ref[...] += _sum0(dx2 * m)
        dmb = (g2 * dx2).astype(BF16)
        dmb_ref[...] = dmb
        dh2 = jnp.zeros((tm, D_MODEL), F32)
        for j in range(N_CHIP):
            sl = slice(hb_ * j, hb_ * (j + 1))
            dub = (_dot_nt(dmb, w2_s[j]) * (2.0 * r_s[:, sl])).astype(BF16)
            dub_ref[:, sl] = dub
            dh2 = dh2 + _dot_nt(dub, w1_s[j])
        dx, dn2_t, dsh_t, dsc_t = _norm_mod_bwd(x1v, n2g, sc2, dh2)
        dx1_ref[...] = dx2 + dx
        dn2_ref[...] += dn2_t
        dsh_ref[...] += dsh_t
        dsc_ref[...] += dsc_t

        @pl.when(pl.program_id(0) == t // tm - 1)
        def _():
            tot = jnp.sum(loss_ref[...], axis=1, keepdims=True) * (0.5 / D_MODEL)
            loss_ref[...] = jnp.broadcast_to(tot, loss_ref.shape)

    row = _full((1, D_MODEL))
    big = pl.BlockSpec((tm, D_MODEL), lambda i: (i, 0))
    wide = pl.BlockSpec((tm, MLP_H), lambda i: (i, 0))
    return _pc(body, name="mlp", grid=(t // tm,),
               in_specs=[big, row, row, row, row, row, ANY, ANY, ANY, ANY, big],
               out_specs=[big, big, wide, wide, big, row, row, row, row, row, row],
               out_shape=[_sds((t, D_MODEL)), _sds((t, D_MODEL), BF16), _sds((t, MLP_H), BF16), _sds((t, MLP_H), BF16),
                          _sds((t, D_MODEL), BF16)] + [_sds((1, D_MODEL))] * 6,
               scratch_shapes=[pltpu.VMEM((N_CHIP, D_MODEL, hb_), BF16), pltpu.VMEM((N_CHIP, hb_, D_MODEL), BF16),
                               pltpu.VMEM((tm, MLP_H), F32), pltpu.SemaphoreType.DMA((n_cp,))],
               compiler_params=_params("arbitrary"))(x1, n2g, sh2, sc2, g2, fg, *w1_parts, *w2_parts, tgt)


def _tn(a, b, nj, a_blocked, b_blocked, name, extra=None, comms=()):
    t = a.shape[0]
    m = a.shape[1] // (nj if a_blocked else 1)
    n = b.shape[1] // (nj if b_blocked else 1)
    bk = next((b for b in (2048, 1024, 512) if t % b == 0), t)
    nk = t // bk
    a_col = (lambda j: j) if a_blocked else (lambda j: 0)
    b_col = (lambda j: j) if b_blocked else (lambda j: 0)
    in_specs = [pl.BlockSpec((bk, m), lambda j, k: (k, a_col(j))), pl.BlockSpec((bk, n), lambda j, k: (k, b_col(j)))]
    args = [a, b]
    if extra is not None:
        a2, b2 = extra
        t2 = a2.shape[0]
        in_specs += [pl.BlockSpec((t2, m), lambda j, k: (0, a_col(j))),
                     pl.BlockSpec((t2, n), lambda j, k: (0, b_col(j)))]
        args += [a2, b2]

    def body(*refs):
        a_ref, b_ref = refs[0], refs[1]
        o_ref, acc = refs[-2], refs[-1]
        k = pl.program_id(1)

        @pl.when(k == 0)
        def _():
            acc[...] = jnp.zeros_like(acc)
        acc[...] += _dot_tn(a_ref[...].astype(BF16), b_ref[...].astype(BF16))

        @pl.when(k == nk - 1)
        def _():
            if extra is not None:
                acc[...] += _dot_tn(refs[2][...].astype(BF16), refs[3][...].astype(BF16))
            o_ref[0] = acc[...]

    (out,), couts = _call(body, name=name, grid=(nj, nk), in_specs=in_specs,
                          out_specs=[pl.BlockSpec((1, m, n), lambda j, k: (j, 0, 0))], out_shape=[_sds((nj, m, n))],
                          scratch_shapes=[pltpu.VMEM((m, n), F32)], sem=("arbitrary", "arbitrary"), args=args,
                          comms=comms)
    return (out, couts) if comms else out


ROW_LOSS = 0
ROW_DMOD = 1
ROW_DMODC = 7
ROW_N1, ROW_N2, ROW_FG, ROW_CB = 9, 10, 11, 12
ROW_BA, ROW_BX, ROW_LAM = 13, 15, 17
ROW_CW = 20
ROW_RD = 24
SLAB_ROWS = 32
SEG = D_MODEL // 2


def _pack_small(rows, drd, cw2, cb2, lru2, gates):
    n_rows, n_lru = len(rows), len(lru2)

    def body(*refs):
        r = refs[:n_rows]
        drd_f, drd_b, drd_c, cw_a, cw_b, cb_a, cb_b = refs[n_rows:n_rows + 7]
        lru = refs[n_rows + 7:n_rows + 7 + n_lru]
        gf_ref, gb_ref, slab, ga, gx = refs[n_rows + 7 + n_lru:]
        slab[...] = jnp.zeros_like(slab)
        slab[ROW_LOSS:ROW_LOSS + 1, :] = r[0][...]
        for k in range(N_MOD):
            slab[ROW_DMOD + k:ROW_DMOD + k + 1, :] = r[1 + k][...]
        slab[ROW_DMODC:ROW_DMODC + 1, :] = r[7][...]
        slab[ROW_DMODC + 1:ROW_DMODC + 2, :] = r[8][...]
        slab[ROW_N1:ROW_N1 + 1, :] = r[9][...] + r[10][...]
        slab[ROW_N2:ROW_N2 + 1, :] = r[11][...]
        slab[ROW_FG:ROW_FG + 1, :] = r[12][...]
        slab[ROW_CB:ROW_CB + 1, 0:LRU_W] = cb_a[...] + cb_b[...]
        for k, row in enumerate((ROW_BA, ROW_BA + 1, ROW_BX, ROW_BX + 1, ROW_LAM, ROW_LAM + 1)):
            slab[row:row + 1, 0:LRU_W] = lru[2 * k][...] + lru[2 * k + 1][...]
        slab[ROW_CW:ROW_CW + 4, 0:LRU_W] = cw_a[...] + cw_b[...]
        for h in range(HEADS):
            slab[ROW_RD + h:ROW_RD + h + 1, 0:LANES] = drd_f[h, 0:1, :] + drd_c[h, 0:1, :]
            slab[ROW_RD + HEADS + h:ROW_RD + HEADS + h + 1, 0:LANES] = drd_b[h, 0:1, :] + drd_c[h, 1:2, :]
        for d, g_ref in enumerate((gf_ref, gb_ref)):
            for n in range(LRU_BLOCKS):
                blk = slice(LRU_BD * n, LRU_BD * (n + 1))
                ga[blk, LRU_BD * d:LRU_BD * (d + 1)] = g_ref[0, blk, blk].astype(BF16)
                gx[blk, LRU_BD * d:LRU_BD * (d + 1)] = g_ref[1, blk, blk].astype(BF16)

    args = list(rows) + list(drd) + list(cw2) + list(cb2) + list(lru2) + list(gates)
    gate_shape = (LRU_W, 2 * LRU_BD)
    return _pc(body, name="pack_small", in_specs=[_full(a.shape) for a in args],
               out_specs=[_full((SLAB_ROWS, D_MODEL)), _full(gate_shape), _full(gate_shape)],
               out_shape=[_sds((SLAB_ROWS, D_MODEL)), _sds(gate_shape, BF16), _sds(gate_shape, BF16)],
               compiler_params=_params())(*args)


def _adam_math(w, g, m, v):
    mn = ADAM_B1 * m + (1.0 - ADAM_B1) * g
    vn = ADAM_B2 * v + (1.0 - ADAM_B2) * (g * g)
    mh = mn / (1.0 - ADAM_B1 ** ADAM_STEP)
    vh = vn / (1.0 - ADAM_B2 ** ADAM_STEP)
    return -ADAM_LR * (mh / (jnp.sqrt(vh) + ADAM_EPS) + ADAM_WD * w), mn, vn


SMALL_PARAMS = ("b_ada", "norm1_g", "norm2_g", "final_g", "ret_decay", "conv_w", "conv_b", "lru_wa", "lru_ba", "lru_wx",
                "lru_bx", "lru_lambda")


def _finalize_small(chip_idx, slab_all, ga_all, gx_all, wmv):
    n_p = len(SMALL_PARAMS)
    flat = [a for nm in SMALL_PARAMS for a in wmv[nm]]
    ada_n = N_MOD * D_MODEL // N_CHIP

    def body(c_ref, slab_ref, ga_ref, gx_ref, *refs):
        prm = {nm: refs[3 * k:3 * k + 3] for k, nm in enumerate(SMALL_PARAMS)}
        outs = {nm: refs[3 * n_p + 4 * k:3 * n_p + 4 * k + 4] for k, nm in enumerate(SMALL_PARAMS)}
        b128_ref, dmc_ref, loss_ref = refs[3 * n_p + 4 * n_p:]
        chip = c_ref[0]

        def pick(fn):
            acc = fn(0)
            for j in range(1, N_CHIP):
                acc = jnp.where(chip == j, fn(j), acc)
            return acc

        tot = slab_ref[0]
        for d in range(1, N_DEV):
            tot = tot + slab_ref[d]

        def update(nm, g, sl=None, rows=None):
            w_ref, m_ref, v_ref = prm[nm]
            g_ref, d_ref, mo_ref, vo_ref = outs[nm]
            ix = (slice(None) if rows is None else rows, slice(None) if sl is None else sl)
            dl, mn, vn = _adam_math(w_ref[ix], g, m_ref[ix], v_ref[ix])
            g_ref[ix] = g
            d_ref[ix] = dl
            mo_ref[ix] = mn
            vo_ref[ix] = vn

        loss_ref[...] = jnp.broadcast_to(tot[ROW_LOSS:ROW_LOSS + 1, 0:LANES], (SUBLANES, LANES))
        for k in range(N_MOD):
            g = tot[ROW_DMOD + k:ROW_DMOD + k + 1, :]
            if k < 2:
                g = g + tot[ROW_DMODC + k:ROW_DMODC + k + 1, :]
            update("b_ada", g, slice(D_MODEL * k, D_MODEL * (k + 1)))
        update("norm1_g", tot[ROW_N1:ROW_N1 + 1, :])
        update("norm2_g", tot[ROW_N2:ROW_N2 + 1, :])
        update("final_g", tot[ROW_FG:ROW_FG + 1, :])
        update("ret_decay", tot[ROW_RD:ROW_RD + SUBLANES, 0:LANES])
        update("conv_b", tot[ROW_CB:ROW_CB + 1, 0:LRU_W])
        update("conv_w", pick(lambda j: tot[ROW_CW:ROW_CW + 4, LANES * j:LANES * (j + 1)]))
        for nm, row in (("lru_ba", ROW_BA), ("lru_bx", ROW_BX), ("lru_lambda", ROW_LAM)):
            update(nm, pick(lambda j, row=row: tot[row:row + 2, LANES * j:LANES * (j + 1)]))
        for nm, g_all in (("lru_wa", ga_ref), ("lru_wx", gx_ref)):
            for dr in range(2):
                lanes = slice(LRU_BD * dr, LRU_BD * (dr + 1))
                g = g_all[0, :, lanes].astype(F32)
                for d in range(1, N_DEV):
                    g = g + g_all[d, :, lanes].astype(F32)
                update(nm, g, rows=slice(LRU_W * dr, LRU_W * (dr + 1)))

        def seg(rows6, s):
            return rows6[s // 2][:, SEG * (s % 2):SEG * (s % 2 + 1)]

        b128_ref[...] = jnp.zeros_like(b128_ref)
        dmc_ref[...] = jnp.zeros_like(dmc_ref)
        zero = jnp.zeros((1, D_MODEL), F32)
        ctx6 = [tot[ROW_DMODC:ROW_DMODC + 1, :], tot[ROW_DMODC + 1:ROW_DMODC + 2, :]] + [zero] * (N_MOD - 2)
        for q in range(ada_n // SEG):
            cols = slice(SEG * q, SEG * (q + 1))
            for d in range(N_DEV):
                rows6 = [slab_ref[d, ROW_DMOD + k:ROW_DMOD + k + 1, :] for k in range(N_MOD)]
                b128_ref[d:d + 1, cols] = pick(lambda j, rows6=rows6: seg(rows6, 3 * j + q))
            c = pick(lambda j: seg(ctx6, 3 * j + q))
            b128_ref[N_DEV:N_DEV + 1, cols] = c
            dmc_ref[0:1, cols] = c

    out_shape = []
    for nm in SMALL_PARAMS:
        out_shape += [_sds(wmv[nm][0].shape)] * 4
    out_shape += [_sds((LANES, ada_n)), _sds((SUBLANES, ada_n)), _sds((SUBLANES, LANES))]
    args = [slab_all, ga_all, gx_all] + flat
    grid_spec = pltpu.PrefetchScalarGridSpec(
        num_scalar_prefetch=1, grid=(1,), in_specs=[_full(a.shape) for a in args],
        out_specs=[_full(s.shape) for s in out_shape])
    outs = _pc(body, name="finalize_small", grid_spec=grid_spec, out_shape=out_shape,
               compiler_params=_params("arbitrary"))(chip_idx, *args)
    res = {nm: tuple(outs[4 * k:4 * k + 4]) for k, nm in enumerate(SMALL_PARAMS)}
    return res, outs[4 * n_p], outs[4 * n_p + 1], outs[4 * n_p + 2]


def _block_diag(w):
    eye = jnp.eye(LRU_BLOCKS, dtype=F32)
    return (w[:, :, None, :] * eye[:, None, :, None]).reshape(LRU_W, LRU_W).astype(BF16)


def _lane_rep(v8):
    return jnp.broadcast_to(v8.reshape(SUBLANES, 1), (SUBLANES, LANES))


def kernel(x, c, ctx, c_ctx, w_ada, b_ada, norm1_g, norm2_g, w_in, ret_decay, conv_w, conv_b, lru_wa, lru_ba, lru_wx, lru_bx, lru_lambda, w_out, w_mlp1, w_mlp2, final_g, loss_target, m_c_ctx, m_w_ada, m_b_ada, m_norm1_g, m_norm2_g, m_w_in, m_ret_decay, m_conv_w, m_conv_b, m_lru_wa, m_lru_ba, m_lru_wx, m_lru_bx, m_lru_lambda, m_w_out, m_w_mlp1, m_w_mlp2, m_final_g, v_c_ctx, v_w_ada, v_b_ada, v_norm1_g, v_norm2_g, v_w_in, v_ret_decay, v_conv_w, v_conv_b, v_lru_wa, v_lru_ba, v_lru_wx, v_lru_bx, v_lru_lambda, v_w_out, v_w_mlp1, v_w_mlp2, v_final_g):
    ax, ay, ac = lax.axis_index("x"), lax.axis_index("y"), lax.axis_index("c")
    chip = 2 * ax + ay
    dev = 4 * ax + 2 * ay + ac
    c_idx = jnp.stack([ac, chip]).astype(jnp.int32)
    j_idx = chip.reshape(1).astype(jnp.int32)

    xt = x[0]
    t_len = xt.shape[0]
    ctxt = ctx[0]
    l_len = ctxt.shape[0]
    tgt = loss_target[0]
    ada_n = w_ada.shape[2]

    def my_half(w2d):
        r = w2d.shape[0] // 2
        return lax.dynamic_slice_in_dim(w2d, ac * r, r, axis=0).astype(BF16)

    pad8 = lambda a: jnp.pad(a, ((0, SUBLANES - a.shape[0]), (0, 0)))
    small = jnp.concatenate([pad8(conv_w[0]), pad8(lru_ba[0]), pad8(lru_bx[0]), pad8(lru_lambda[0])], axis=0)
    b_shard = lax.dynamic_slice_in_dim(b_ada, chip * ada_n, ada_n, axis=1)
    gw_in, _, small_all, a16, mod_parts, lgv, sgv = _head(
        my_half(w_in[0]), pad8(c), small, w_ada[0], b_shard, c_ctx, ret_decay[0])
    w4 = gw_in.reshape(N_CHIP, D_MODEL, IN_COLS // N_CHIP)

    mod_all = mod_parts[0::2].transpose(1, 0, 2).reshape(16, N_CHIP * ada_n)
    mod_me = lax.dynamic_slice_in_dim(mod_all, dev, 1, axis=0)
    sh1, sc1, g1, sh2, sc2, g2 = [mod_me[:, D_MODEL * k:D_MODEL * (k + 1)] for k in range(N_MOD)]
    csh1, csc1 = mod_all[8:9, 0:D_MODEL], mod_all[8:9, D_MODEL:2 * D_MODEL]

    cos2, sin2 = _rotary_tables(t_len)
    cos_c, sin_c = jnp.ones((l_len, DH), F32), jnp.zeros((l_len, DH), F32)
    n1g, n2g = norm1_g, norm2_g
    fg = final_g.reshape(1, D_MODEL)

    small_full = small_all[0::2].transpose(1, 0, 2).reshape(4 * SUBLANES, LRU_W)
    cw = small_full[0:4]
    cb = conv_b
    ba_f, ba_b = small_full[8:9], small_full[9:10]
    bx_f, bx_b = small_full[16:17], small_full[17:18]
    lam_f, lam_b = small_full[24:25], small_full[25:26]
    wa_f, wa_b = _block_diag(lru_wa[0, 0]), _block_diag(lru_wa[0, 1])
    wx_f, wx_b = _block_diag(lru_wx[0, 0]), _block_diag(lru_wx[0, 1])
    zero_h = jnp.zeros((1, LRU_W), F32)

    projc, xrc, hcb16 = _inproj_fwd(ctxt, n1g, csh1, csc1, w4, cos_c, sin_c, "inproj_fwd_ctx")
    s_f, s_b = _ctx_state_fwd(projc, lgv)
    xcc = _conv_fwd(xrc, cw, cb, "conv_fwd_ctx")
    hcf = _lru_fwd(xcc, wa_f, wx_f, ba_f, bx_f, lam_f, zero_h, False, "lru_fwd_ctx_f")
    hcbk = _lru_fwd(xcc, wa_b, wx_b, ba_b, bx_b, lam_b, zero_h, True, "lru_fwd_ctx_b")
    lru_sf, lru_sb = hcf[l_len - 1:l_len], hcbk[0:1]

    h1, h2 = my_half(w_mlp1[0]), my_half(w_mlp2[0])
    q = h1.shape[0] // 2
    (proj, xrl, hb16), ((gw_1a,),) = _inproj_fwd(xt, n1g, sh1, sc1, w4, cos2, sin2, "inproj_fwd",
                                           comms=(_AllGather([h1[:q]]),))
    (o_f, o_b, spf, spb), ((gw_1b, gw_out),) = _ret_fwd(proj, lgv, s_f, s_b,
                                                       comms=(_AllGather([h1[q:], my_half(w_out[0])]),))
    xcl = _conv_fwd(xrl, cw, cb, "conv_fwd")
    hf, ((gw_2a,),) = _lru_fwd(xcl, wa_f, wx_f, ba_f, bx_f, lam_f, lru_sf, False, "lru_fwd_f",
                              comms=(_AllGather([h2[:q]]),))
    hbk, ((gw_2b,),) = _lru_fwd(xcl, wa_b, wx_b, ba_b, bx_b, lam_b, lru_sb, True, "lru_fwd_b",
                               comms=(_AllGather([h2[q:]]),))
    wo = gw_out.reshape(D_MODEL, D_MODEL)
    x1, cat = _mix_fwd(o_f, o_b, proj, hf, hbk, wo, xt, g1)

    (dx1, h2b, ab, dub, dmb, dsc2, dsh2, dg2, dn2g, dfg, lossv) = _mlp(
        x1, n2g, sh2, sc2, g2, fg, (gw_1a, gw_1b), (gw_2a, gw_2b), tgt)
    gw_mlp1 = _tn(h2b, dub, N_CHIP, False, True, "grad_w_mlp1")
    b_1 = gw_mlp1.reshape(N_DEV, D_MODEL // 2, MLP_H // N_CHIP)
    gw_mlp2, ((r_1,),) = _tn(ab, dmb, N_CHIP, True, False, "grad_w_mlp2", comms=(_pair_exchange([b_1]),))

    half = D_MODEL // 4
    top, bot = (0, half), (half, half)
    b_2 = gw_mlp2.reshape(N_DEV, MLP_H // N_DEV, D_MODEL)
    p_1, pb_1 = _pair_add(b_1, r_1, c_idx, "rs_pair_add_w_mlp1")
    (do, dhs, dg, dgate, dyb, dg1), ((q_1a,), (r_2,)) = _mix_bwd(
        o_f, o_b, proj, hf, hbk, wo, cat, dx1, g1, comms=(_chip_exchange([pb_1], top), _pair_exchange([b_2])))
    gw_o = _tn(cat, dyb, 1, False, False, "grad_w_out")
    b_o = gw_o.reshape(N_DEV, D_MODEL // N_DEV, D_MODEL)
    p_2, pb_2 = _pair_add(b_2, r_2, c_idx, "rs_pair_add_w_mlp2")

    (dq_f, dk_f, dv_f, ds_f, drd_f), ((q_1b,), (r_o,)) = _ret_bwd(
        proj, lgv, sgv, spf, do, False, "ret_bwd_f", comms=(_chip_exchange([pb_1], bot), _pair_exchange([b_o])))
    p_o, pb_o = _pair_add(b_o, r_o, c_idx, "rs_pair_add_w_out")
    h_1 = _chip_add(p_1, (q_1a, q_1b), c_idx, "rs_chip_add_w_mlp1")

    (dq_b, dk_b, dv_b, ds_b, drd_b), ((q_2a,), (f_1,)) = _ret_bwd(
        proj, lgv, sgv, spb, do, True, "ret_bwd_b", comms=(_chip_exchange([pb_2], top), _pair_gather([h_1])))

    (dxc_f, dpre_f, dba_f, dbx_f, dlam_f, dh0_f), ((q_2b,),) = _lru_bwd(
        xcl, wa_f, wx_f, ba_f, bx_f, lam_f, hf, lru_sf, dhs, False, "lru_bwd_f",
        comms=(_chip_exchange([pb_2], bot),))
    h_2 = _chip_add(p_2, (q_2a, q_2b), c_idx, "rs_chip_add_w_mlp2")
    (dxc_b, dpre_b, dba_b, dbx_b, dlam_b, dh0_b), ((q_o,), (f_2,)) = _lru_bwd(
        xcl, wa_b, wx_b, ba_b, bx_b, lam_b, hbk, lru_sb, dhs, True, "lru_bwd_b",
        comms=(_chip_exchange([pb_o]), _pair_gather([h_2])))
    h_o = _chip_add(p_o, (q_o,), c_idx, "rs_chip_add_w_out")
    dxr, dcw, dcb = _conv_bwd(dxc_f, dxc_b, xrl, cw, "conv_bwd")
    grad_x, dpb, dn1g, dsh1, dsc1 = _inproj_bwd(
        xt, n1g, sh1, sc1, w4, cos2, sin2, [dq_f, dq_b, dk_f, dk_b, dv_f, dv_b, dg, dxr, dgate], dx1, "inproj_bwd")

    dkc, dvc, drd_c = _ctx_state_bwd(projc, lgv, sgv, ds_f, ds_b)
    zc = jnp.zeros((l_len, LRU_W), F32)
    dhc_f = lax.dynamic_update_slice(zc, dh0_f, (l_len - 1, 0))
    dhc_b = lax.dynamic_update_slice(zc, dh0_b, (0, 0))
    (dxcc_f, dprec_f, dbac_f, dbxc_f, dlamc_f, _), _ = _lru_bwd(
        xcc, wa_f, wx_f, ba_f, bx_f, lam_f, hcf, zero_h, dhc_f, False, "lru_bwd_ctx_f")
    (dxcc_b, dprec_b, dbac_b, dbxc_b, dlamc_b, _), _ = _lru_bwd(
        xcc, wa_b, wx_b, ba_b, bx_b, lam_b, hcbk, zero_h, dhc_b, True, "lru_bwd_ctx_b")
    dxrc, dcw_c, dcb_c = _conv_bwd(dxcc_f, dxcc_b, xrc, cw, "conv_bwd_ctx")
    zr = jnp.zeros((l_len, RET_W), BF16)
    _, dpbc, dn1g_c, dcsh1, dcsc1 = _inproj_bwd(
        ctxt, n1g, csh1, csc1, w4, cos_c, sin_c, [zr, zr, dkc, zr, dvc, zr, zr, dxrc, zr],
        jnp.zeros((l_len, D_MODEL), F32), "inproj_bwd_ctx")

    gw_i = _tn(hb16, dpb, N_CHIP, False, True, "grad_w_in", extra=(hcb16, dpbc))
    b_i = gw_i.reshape(N_DEV, D_MODEL // 2, IN_COLS // N_CHIP)
    gwa_f, ((r_i,), (f_o,)) = _tn(xcl, dpre_f, 2, False, True, "grad_lru_gates_f", extra=(xcc, dprec_f),
                                  comms=(_pair_exchange([b_i]), _pair_gather([h_o])))
    p_i, pb_i = _pair_add(b_i, r_i, c_idx, "rs_pair_add_w_in")
    gwa_b, ((q_i,),) = _tn(xcl, dpre_b, 2, False, True, "grad_lru_gates_b", extra=(xcc, dprec_b),
                           comms=(_chip_exchange([pb_i]),))
    (f_i,) = _run_comm(_pair_gather([_chip_add(p_i, (q_i,), c_idx, "rs_chip_add_w_in")]), "rs_pair_gather_w_in")
    g_in, g_out, g_1, g_2 = _shard_of(f_i), _shard_of(f_o), _shard_of(f_1), _shard_of(f_2)
    big = {}
    for nm, w, g, m, v in (("w_in", w_in, g_in, m_w_in, v_w_in), ("w_out", w_out, g_out, m_w_out, v_w_out),
                           ("w_mlp1", w_mlp1, g_1, m_w_mlp1, v_w_mlp1), ("w_mlp2", w_mlp2, g_2, m_w_mlp2, v_w_mlp2)):
        go, d_, mn, vn = _adamw(w[0], g, m[0], v[0], "adamw_" + nm)
        big[nm] = (go[None], d_[None], mn[None], vn[None])

    slab, ga, gx = _pack_small(
        [lossv, dsh1, dsc1, dg1, dsh2, dsc2, dg2, dcsh1, dcsc1, dn1g, dn1g_c, dn2g, dfg],
        (drd_f, drd_b, drd_c), (dcw, dcw_c), (dcb, dcb_c),
        (dba_f, dbac_f, dba_b, dbac_b, dbx_f, dbxc_f, dbx_b, dbxc_b, dlam_f, dlamc_f, dlam_b, dlamc_b),
        (gwa_f, gwa_b))
    slab_all, ga_all, gx_all = _all_gather([slab, ga, gx], "gather_small_grads")
    params = {
        "b_ada": (b_ada, m_b_ada, v_b_ada), "norm1_g": (norm1_g, m_norm1_g, v_norm1_g),
        "norm2_g": (norm2_g, m_norm2_g, v_norm2_g), "final_g": (final_g, m_final_g, v_final_g),
        "ret_decay": (ret_decay, m_ret_decay, v_ret_decay), "conv_w": (conv_w, m_conv_w, v_conv_w),
        "conv_b": (conv_b, m_conv_b, v_conv_b), "lru_wa": (lru_wa, m_lru_wa, v_lru_wa),
        "lru_ba": (lru_ba, m_lru_ba, v_lru_ba), "lru_wx": (lru_wx, m_lru_wx, v_lru_wx),
        "lru_bx": (lru_bx, m_lru_bx, v_lru_bx), "lru_lambda": (lru_lambda, m_lru_lambda, v_lru_lambda),
    }
    as2d = {
        "b_ada": lambda a: a, "norm1_g": lambda a: a, "norm2_g": lambda a: a, "conv_b": lambda a: a,
        "final_g": lambda a: a.reshape(1, D_MODEL), "ret_decay": lambda a: _lane_rep(a.reshape(-1)),
        "conv_w": lambda a: a[0], "lru_ba": lambda a: a[0], "lru_bx": lambda a: a[0], "lru_lambda": lambda a: a[0],
        "lru_wa": lambda a: a.reshape(2 * LRU_W, LRU_BD), "lru_wx": lambda a: a.reshape(2 * LRU_W, LRU_BD),
    }
    res, b128, dmc8, loss8 = _finalize_small(
        j_idx, slab_all, ga_all, gx_all, {nm: tuple(as2d[nm](a) for a in params[nm]) for nm in SMALL_PARAMS})
    loss = loss8[0, 0]
    small_out = {}
    for nm in SMALL_PARAMS:
        shp = params[nm][0].shape
        if nm == "ret_decay":
            small_out[nm] = tuple(o[:, 0].reshape(shp) for o in res[nm])
        else:
            small_out[nm] = tuple(o.reshape(shp) for o in res[nm])

    g_ada = _ada_grad(jnp.pad(a16.T, ((0, 0), (0, LANES - 16))), b128)
    g_ada, d_ada, m_ada, v_ada = _adamw(w_ada[0], g_ada, m_w_ada[0], v_w_ada[0], "adamw_w_ada")

    (cparts,) = _all_gather([_cctx_partial(dmc8, w_ada[0])], "gather_cctx")
    g_cc, d_cc, m_cc, v_cc = _cctx_final(cparts, c_ctx, m_c_ctx, v_c_ctx)
    small_out["c_ctx"] = tuple(a.reshape(D_MODEL) for a in (g_cc, d_cc, m_cc, v_cc))
    small_out["w_ada"] = (g_ada[None], d_ada[None], m_ada[None], v_ada[None])
    small_out.update(big)

    order = ["c_ctx", "w_ada", "b_ada", "norm1_g", "norm2_g", "w_in", "ret_decay", "conv_w", "conv_b", "lru_wa", "lru_ba",
             "lru_wx", "lru_bx", "lru_lambda", "w_out", "w_mlp1", "w_mlp2", "final_g"]
    outs = [loss, grad_x[None]]
    for k in range(4):
        outs += [small_out[nm][k] for nm in order]
    return tuple(outs)
```

```python
import math

import jax
import jax.numpy as jnp
from jax import lax
from jax.experimental import pallas as pl
from jax.experimental.pallas import tpu as pltpu

F32 = jnp.float32
BF16 = jnp.bfloat16

D_MODEL = 1024
HEADS = 4
DH = 128
CHUNK = 256
RET_W = HEADS * DH
LRU_W = 512
LRU_BLOCKS = 8
LRU_BD = LRU_W // LRU_BLOCKS
LRU_C = 8.0
IN_COLS = 4 * RET_W + 2 * LRU_W
MLP_H = 4 * D_MODEL
N_MOD = 6
GRID_W = 64
ROPE_BASE = 10000.0
K_SCALE = DH ** -0.5
EPS = 1e-6
GELU_K = math.sqrt(2.0 / math.pi)
GELU_C = 0.044715

ADAM_LR = 0.001
ADAM_B1 = 0.9
ADAM_B2 = 0.999
ADAM_EPS = 1e-08
ADAM_WD = 0.01
ADAM_STEP = 10

N_DEV = 8
N_CHIP = 4
SUBLANES = 8
LANES = 128
VMEM_LIMIT_V7X = 56 * 1024 * 1024
MESH = pl.DeviceIdType.MESH
ANY = pl.BlockSpec(memory_space=pl.ANY)


def _pc(body, **kw):
    return pl.pallas_call(body, **kw)


def _params(*sem):
    return pltpu.CompilerParams(dimension_semantics=sem if sem else None, vmem_limit_bytes=VMEM_LIMIT_V7X)


def _tile(t, big=False):
    if big and t >= 1024:
        return 512
    return 256 if t >= 256 else t


def _sds(shape, dtype=F32):
    return jax.ShapeDtypeStruct(tuple(shape), dtype)


def _full(shape):
    nd = len(shape)
    return pl.BlockSpec(tuple(shape), lambda *_: (0,) * nd)


def _sigmoid(x):
    return 1.0 / (1.0 + jnp.exp(-x))


def _log1p_pos(y):
    s = y * (1.0 - y * (0.5 - y * (1.0 / 3.0 - y * (0.25 - y * (0.2 - y / 6.0)))))
    return jnp.where(y < 0.03, s, jnp.log(1.0 + y))


def _softplus(z):
    return jnp.maximum(z, 0.0) + _log1p_pos(jnp.exp(-jnp.abs(z)))


def _one_minus_sq(la, a):
    t = la * (1.0 + la * (0.5 + la * (1.0 / 6.0 + la * (1.0 / 24.0 + la * (1.0 / 120.0)))))
    return jnp.where(la > -0.125, -t, 1.0 - a) * (1.0 + a)


def _rms(x):
    r = lax.rsqrt(jnp.mean(x * x, axis=-1, keepdims=True) + EPS)
    return x * r, r


def _dot(a, b):
    return jnp.dot(a, b, preferred_element_type=F32)


def _dot_nt(a, b):
    return lax.dot_general(a, b, (((1,), (1,)), ((), ())), preferred_element_type=F32)


def _dot_tn(a, b):
    return lax.dot_general(a, b, (((0,), (0,)), ((), ())), preferred_element_type=F32)


def _sum0(x):
    return jnp.sum(x, axis=0, keepdims=True)


def _norm_mod_bwd(x, g, sc, dh):
    xh, r = _rms(x)
    hn = xh * g
    dhn = dh * (1.0 + sc)
    dxh = dhn * g
    dx = r * (dxh - xh * jnp.mean(dxh * xh, axis=-1, keepdims=True))
    return dx, _sum0(dhn * xh), _sum0(dh), _sum0(dh * hn)


def _dev_index(p):
    return 4 * p[0] + 2 * p[1] + p[2]


def _mesh_pos():
    return lax.axis_index("x"), lax.axis_index("y"), lax.axis_index("c")


class _AllGather:
    def __init__(self, arrs):
        n = len(arrs)
        self.arrays = list(arrs)
        self.out_shapes = [_sds((N_DEV,) + a.shape, a.dtype) for a in arrs]
        self.scratch = ([pltpu.VMEM(a.shape, a.dtype) for a in arrs]
                        + [pltpu.SemaphoreType.DMA((7 * n,)), pltpu.SemaphoreType.DMA((7 * n,)),
                           pltpu.SemaphoreType.DMA((n,))])
        self.aliases = {}

    def _parts(self, ins, outs, scr):
        n = len(self.arrays)
        stage = scr[:n]
        send_sems, recv_sems, local_sems = scr[n:]
        x, y, c = _mesh_pos()
        me, sib = (x, y, c), (x, y, 1 - c)
        chips = [(1 - x, y), (x, 1 - y), (1 - x, 1 - y)]

        def copy(t, k, block, to, own=False):
            dst = outs[t].at[_dev_index(block)]
            return pltpu.make_async_remote_copy(
                src_ref=ins[t] if own else dst, dst_ref=dst,
                send_sem=send_sems.at[7 * t + k], recv_sem=recv_sems.at[7 * t + k],
                device_id=to, device_id_type=MESH)

        first = []
        for t in range(n):
            first.append(copy(t, 0, me, sib, own=True))
            for j, ch in enumerate(chips):
                first.append(copy(t, 1 + j, me, (*ch, c), own=True))
        stage_in = [pltpu.make_async_copy(ins[t], stage[t], local_sems.at[t]) for t in range(n)]
        mine = [pltpu.make_async_copy(stage[t], outs[t].at[_dev_index(me)], local_sems.at[t]) for t in range(n)]
        return n, c, me, sib, chips, copy, first, stage_in, mine

    def start(self, ins, outs, scr):
        n, _, _, _, _, _, first, stage_in, mine = self._parts(ins, outs, scr)
        for cp in stage_in:
            cp.start()
        for cp in first:
            cp.start()
        for t in range(n):
            stage_in[t].wait()
            mine[t].start()

    def relay(self, ins, outs, scr):
        n, c, me, sib, chips, copy, _, _, _ = self._parts(ins, outs, scr)
        for j, ch in enumerate(chips):
            for t in range(n):
                copy(t, 1 + j, (*ch, c), me).wait_recv()
                copy(t, 4 + j, (*ch, c), sib).start()

    def finish(self, ins, outs, scr):
        n, c, me, sib, chips, copy, first, _, mine = self._parts(ins, outs, scr)
        passed = [copy(t, 4 + j, (*ch, c), sib) for j, ch in enumerate(chips) for t in range(n)]
        for t in range(n):
            copy(t, 0, sib, me).wait_recv()
            for j, ch in enumerate(chips):
                copy(t, 4 + j, (*ch, 1 - c), me).wait_recv()
        for cp in first + passed:
            cp.wait_send()
        for cp in mine:
            cp.wait()


class _Exchange:
    def __init__(self, arrays, out_shapes, plan, n_copies, aliases=None):
        self.arrays = list(arrays)
        self.out_shapes = list(out_shapes)
        self.plan = plan
        self.scratch = [pltpu.SemaphoreType.DMA((n_copies,)), pltpu.SemaphoreType.DMA((n_copies,))]
        self.aliases = aliases or {}

    def _copies(self, ins, outs, scr):
        send_sems, recv_sems = scr
        snd, rcv = [], []
        for i, (src, dst, peer, lands) in enumerate(self.plan(ins, outs, _mesh_pos())):
            kw = dict(send_sem=send_sems.at[i], recv_sem=recv_sems.at[i], device_id=peer, device_id_type=MESH)
            snd.append(pltpu.make_async_remote_copy(src_ref=src, dst_ref=dst, **kw))
            rcv.append(pltpu.make_async_remote_copy(src_ref=src, dst_ref=lands, **kw))
        return snd, rcv

    def start(self, ins, outs, scr):
        for cp in self._copies(ins, outs, scr)[0]:
            cp.start()

    def relay(self, ins, outs, scr):
        pass

    def finish(self, ins, outs, scr):
        snd, rcv = self._copies(ins, outs, scr)
        for cp in rcv:
            cp.wait_recv()
        for cp in snd:
            cp.wait_send()


def _pair_exchange(grads):
    n = len(grads)

    def plan(ins, outs, pos):
        x, y, c = pos
        return [(ins[t].at[2 * j + (1 - c)], outs[t].at[j], (x, y, 1 - c), outs[t].at[j])
                for t in range(n) for j in range(N_CHIP)]

    return _Exchange(grads, [_sds((N_CHIP,) + g.shape[1:], g.dtype) for g in grads], plan, N_CHIP * n)


def _chip_exchange(parts, rows=None):
    n = len(parts)

    def plan(ins, outs, pos):
        x, y, c = pos
        chips = [(1 - x, y), (x, 1 - y), (1 - x, 1 - y)]

        def src(t, ch):
            blk = ins[t].at[2 * ch[0] + ch[1]]
            return blk if rows is None else blk.at[pl.ds(rows[0], rows[1])]

        return [(src(t, ch), outs[t].at[k], (*ch, c), outs[t].at[k]) for t in range(n) for k, ch in enumerate(chips)]

    shapes = [_sds((3, p.shape[1] if rows is None else rows[1]) + p.shape[2:], p.dtype) for p in parts]
    return _Exchange(parts, shapes, plan, 3 * n)


def _pair_gather(bufs):
    n = len(bufs)

    def plan(ins, outs, pos):
        x, y, c = pos
        return [(ins[t].at[c], outs[t].at[c], (x, y, 1 - c), outs[t].at[1 - c]) for t in range(n)]

    return _Exchange(bufs, [_sds(b.shape, b.dtype) for b in bufs], plan, n, aliases={t: t for t in range(n)})


def _run_comms(comms, name):
    c_in = [len(cm.arrays) for cm in comms]
    c_out = [len(cm.out_shapes) for cm in comms]
    c_scr = [len(cm.scratch) for cm in comms]
    aliases = {}
    for k, cm in enumerate(comms):
        for a, b in cm.aliases.items():
            aliases[sum(c_in[:k]) + a] = sum(c_out[:k]) + b

    def split(refs, counts):
        out, pos = [], 0
        for cnt in counts:
            out.append(refs[pos:pos + cnt])
            pos += cnt
        return out

    def body(*refs):
        ins = split(refs[:sum(c_in)], c_in)
        outs = split(refs[sum(c_in):sum(c_in) + sum(c_out)], c_out)
        scr = split(refs[sum(c_in) + sum(c_out):], c_scr)
        for phase in ("start", "relay", "finish"):
            for k, cm in enumerate(comms):
                getattr(cm, phase)(ins[k], outs[k], scr[k])

    outs = _pc(body, name=name, out_shape=[s for cm in comms for s in cm.out_shapes],
               in_specs=[ANY] * sum(c_in), out_specs=[ANY] * sum(c_out), input_output_aliases=aliases,
               scratch_shapes=[s for cm in comms for s in cm.scratch],
               compiler_params=_params())(*[a for cm in comms for a in cm.arrays])
    return split(list(outs), c_out)


def _all_gather(arrs, name):
    return _run_comms([_AllGather(arrs)], name)[0]


def _call(body, *, name, grid, in_specs, out_specs, out_shape, scratch_shapes, sem, args, comms=()):
    n_in, n_out, n_scr = len(in_specs), len(out_specs), len(scratch_shapes)
    c_in = [len(cm.arrays) for cm in comms]
    c_out = [len(cm.out_shapes) for cm in comms]
    c_scr = [len(cm.scratch) for cm in comms]
    aliases = {}
    for k, cm in enumerate(comms):
        for a, b in cm.aliases.items():
            aliases[n_in + sum(c_in[:k]) + a] = n_out + sum(c_out[:k]) + b

    def split(refs, counts):
        out, pos = [], 0
        for cnt in counts:
            out.append(refs[pos:pos + cnt])
            pos += cnt
        return out

    def wrapped(*refs):
        ins = refs[:n_in + sum(c_in)]
        outs = refs[len(ins):len(ins) + n_out + sum(c_out)]
        scr = refs[len(ins) + len(outs):]
        cins, couts, cscr = split(ins[n_in:], c_in), split(outs[n_out:], c_out), split(scr[n_scr:], c_scr)
        if comms:
            first = pl.program_id(0) == 0
            last = pl.program_id(0) == grid[0] - 1
            for k in range(1, len(grid)):
                first = jnp.logical_and(first, pl.program_id(k) == 0)
                last = jnp.logical_and(last, pl.program_id(k) == grid[k] - 1)

            @pl.when(first)
            def _():
                for k, cm in enumerate(comms):
                    cm.start(cins[k], couts[k], cscr[k])
        body(*ins[:n_in], *outs[:n_out], *scr[:n_scr])
        if comms:
            relay_early = len(grid) == 1 and grid[0] >= 4
            if relay_early:
                @pl.when(pl.program_id(0) == (7 * grid[0]) // 8 - 1)
                def _():
                    for k, cm in enumerate(comms):
                        cm.relay(cins[k], couts[k], cscr[k])

            @pl.when(last)
            def _():
                for k, cm in enumerate(comms):
                    if not relay_early:
                        cm.relay(cins[k], couts[k], cscr[k])
                    cm.finish(cins[k], couts[k], cscr[k])

    outs = _pc(wrapped, name=name, grid=grid,
               in_specs=list(in_specs) + [ANY] * sum(c_in), out_specs=list(out_specs) + [ANY] * sum(c_out),
               out_shape=list(out_shape) + [s for cm in comms for s in cm.out_shapes],
               scratch_shapes=list(scratch_shapes) + [s for cm in comms for s in cm.scratch],
               input_output_aliases=aliases, compiler_params=_params(*sem),
               )(*args, *[a for cm in comms for a in cm.arrays])
    outs = list(outs)
    return outs[:n_out], split(outs[n_out:], c_out)


def _row_block(r):
    for b in (512, 256, 128, 64, 32, 16, 8):
        if r % b == 0:
            return b
    return r


def _pair_add(g, recv, cj_idx, name):
    _, r, cc = g.shape
    br = _row_block(r)

    def body(cj_ref, g_ref, r_ref, own_ref, pb_ref):
        s = g_ref[...] + r_ref[...]
        pb_ref[...] = s.astype(BF16)

        @pl.when(pl.program_id(1) == cj_ref[1])
        def _():
            own_ref[...] = s[0]

    grid_spec = pltpu.PrefetchScalarGridSpec(
        num_scalar_prefetch=1, grid=(r // br, N_CHIP),
        in_specs=[pl.BlockSpec((1, br, cc), lambda i, j, cj_ref: (2 * j + cj_ref[0], i, 0)),
                  pl.BlockSpec((1, br, cc), lambda i, j, cj_ref: (j, i, 0))],
        out_specs=[pl.BlockSpec((br, cc), lambda i, j, cj_ref: (i, 0)),
                   pl.BlockSpec((1, br, cc), lambda i, j, cj_ref: (j, i, 0))])
    return _pc(body, name=name, grid_spec=grid_spec,
               out_shape=[_sds((r, cc)), _sds((N_CHIP, r, cc), BF16)],
               compiler_params=_params("arbitrary", "arbitrary"))(cj_idx, g, recv)


def _chip_add(p, qs, cj_idx, name):
    r, cc = p.shape
    nq = len(qs)
    br = _row_block(r // nq)
    nb = r // nq // br

    def body(cj_ref, p_ref, *refs):
        o_ref = refs[-1]
        if nq == 2:
            top = pl.program_id(0) < nb
            q = [jnp.where(top, refs[0][k], refs[1][k]).astype(F32) for k in range(3)]
        else:
            q = [refs[0][k].astype(F32) for k in range(3)]
        o_ref[0] = ((p_ref[...] + q[0]) + q[1]) + q[2]

    q_specs = [pl.BlockSpec((3, br, cc), lambda i, cj_ref, h=h: (0, jnp.clip(i - h * nb, 0, nb - 1), 0))
               for h in range(nq)]
    grid_spec = pltpu.PrefetchScalarGridSpec(
        num_scalar_prefetch=1, grid=(r // br,),
        in_specs=[pl.BlockSpec((br, cc), lambda i, cj_ref: (i, 0))] + q_specs,
        out_specs=pl.BlockSpec((1, br, cc), lambda i, cj_ref: (cj_ref[0], i, 0)))
    return _pc(body, name=name, grid_spec=grid_spec, out_shape=_sds((2, r, cc)),
               compiler_params=_params("arbitrary"))(cj_idx, p, *qs)


def _shard_of(both):
    return both.reshape((2 * both.shape[1],) + both.shape[2:])


def _adamw(w, g, m, v, name):
    r, cc = w.shape
    br = _row_block(r)
    if r * cc * 4 <= (1 << 20):
        br = r
    elif br * cc * 4 > (1 << 20) and br > 8:
        br = max(8, (1 << 20) // (cc * 4) // 8 * 8)
        while r % br:
            br -= 8
    c1 = 1.0 - ADAM_B1 ** ADAM_STEP
    c2 = 1.0 - ADAM_B2 ** ADAM_STEP

    def body(w_ref, g_ref, m_ref, v_ref, go_ref, d_ref, mo_ref, vo_ref):
        gg = g_ref[...]
        go_ref[...] = gg
        mn = ADAM_B1 * m_ref[...] + (1.0 - ADAM_B1) * gg
        vn = ADAM_B2 * v_ref[...] + (1.0 - ADAM_B2) * (gg * gg)
        mh = mn / c1
        vh = vn / c2
        d_ref[...] = -ADAM_LR * (mh / (jnp.sqrt(vh) + ADAM_EPS) + ADAM_WD * w_ref[...])
        mo_ref[...] = mn
        vo_ref[...] = vn

    spec = pl.BlockSpec((br, cc), lambda i: (i, 0))
    return _pc(body, name=name, grid=(r // br,), in_specs=[spec] * 4, out_specs=[spec] * 4,
               out_shape=[_sds((r, cc))] * 4, compiler_params=_params("arbitrary"))(w, g, m, v)


def _head(w_half, c8, small, w_ada, b_shard, c_ctx, ret_decay):
    ada_n = w_ada.shape[1]
    mod_sds = _sds((16, ada_n))
    ag_w, ag_c, ag_m = _AllGather([w_half]), _AllGather([c8, small]), _AllGather([mod_sds])
    n_w, n_c, n_m = len(ag_w.scratch), len(ag_c.scratch), len(ag_m.scratch)

    def body(w_ref, c_ref, s_ref, wada_ref, b_ref, cc_ref, rd_ref,
             gw_ref, call_ref, sall_ref, a_ref, modp_ref, mall_ref, lg_ref, sg_ref, *scr):
        scr_w, scr_c, scr_m = scr[:n_w], scr[n_w:n_w + n_c], scr[n_w + n_c:n_w + n_c + n_m]
        c_v, w_v, m_v, sems = scr[n_w + n_c + n_m:]
        ag_w.start((w_ref,), (gw_ref,), scr_w)
        ag_c.start((c_ref, s_ref), (call_ref, sall_ref), scr_c)
        load_w = pltpu.make_async_copy(wada_ref, w_v, sems.at[0])
        load_w.start()
        rd = rd_ref[...]
        lg_ref[...] = -_softplus(-rd)
        sg_ref[...] = _sigmoid(-rd)
        ag_c.relay((c_ref, s_ref), (call_ref, sall_ref), scr_c)
        ag_c.finish((c_ref, s_ref), (call_ref, sall_ref), scr_c)
        load_c = pltpu.make_async_copy(call_ref, c_v, sems.at[1])
        load_c.start()
        load_c.wait()
        a_ref[...] = jnp.zeros_like(a_ref)
        for d in range(N_DEV):
            cd = c_v[d, 0:1, :]
            a_ref[d:d + 1, :] = cd * _sigmoid(cd)
        cc = cc_ref[...]
        a_ref[N_DEV:N_DEV + 1, :] = cc * _sigmoid(cc)
        load_w.wait()
        m_v[...] = jnp.dot(a_ref[...], w_v[...], preferred_element_type=F32,
                           precision=lax.Precision.HIGHEST) + b_ref[...]
        put = pltpu.make_async_copy(m_v, modp_ref, sems.at[2])
        put.start()
        put.wait()
        ag_m.start((modp_ref,), (mall_ref,), scr_m)
        ag_m.relay((modp_ref,), (mall_ref,), scr_m)
        ag_m.finish((modp_ref,), (mall_ref,), scr_m)
        ag_w.relay((w_ref,), (gw_ref,), scr_w)
        ag_w.finish((w_ref,), (gw_ref,), scr_w)

    rd = jnp.broadcast_to(ret_decay.reshape(2, HEADS).T[:, :, None], (HEADS, 2, LANES))
    lane = _full((HEADS, 2, LANES))
    outs = _pc(
        body, name="head",
        in_specs=[ANY, ANY, ANY, ANY, _full((1, ada_n)), _full((1, D_MODEL)), lane],
        out_specs=[ANY, ANY, ANY, _full((16, D_MODEL)), ANY, ANY, lane, lane],
        out_shape=ag_w.out_shapes + ag_c.out_shapes + [_sds((16, D_MODEL)), mod_sds] + ag_m.out_shapes
        + [_sds((HEADS, 2, LANES))] * 2,
        scratch_shapes=ag_w.scratch + ag_c.scratch + ag_m.scratch
        + [pltpu.VMEM((N_DEV,) + c8.shape, F32), pltpu.VMEM(w_ada.shape, F32), pltpu.VMEM((16, ada_n), F32),
           pltpu.SemaphoreType.DMA((3,))],
        compiler_params=_params(),
    )(w_half, c8, small, w_ada, b_shard, c_ctx.reshape(1, D_MODEL), rd)
    gw, c_all, small_all, a16, _, mod_all, lgv, sgv = outs
    return gw, c_all, small_all, a16, mod_all, lgv, sgv


def _ada_grad(at, b):
    n = b.shape[1]
    bn = 512

    def body(a_ref, b_ref, o_ref):
        o_ref[...] = jnp.dot(a_ref[...], b_ref[...], preferred_element_type=F32, precision=lax.Precision.HIGHEST)

    return _pc(body, name="ada_grad", grid=(n // bn,),
               in_specs=[_full((D_MODEL, LANES)), pl.BlockSpec((LANES, bn), lambda i: (0, i))],
               out_specs=pl.BlockSpec((D_MODEL, bn), lambda i: (0, i)), out_shape=_sds((D_MODEL, n)),
               compiler_params=_params("arbitrary"))(at, b)


def _cctx_partial(dmc8, w_ada):
    n = w_ada.shape[1]
    bn = 512

    def body(d_ref, w_ref, o_ref):
        @pl.when(pl.program_id(0) == 0)
        def _():
            o_ref[...] = jnp.zeros_like(o_ref)
        o_ref[...] += lax.dot_general(d_ref[...], w_ref[...], (((1,), (1,)), ((), ())),
                                      preferred_element_type=F32, precision=lax.Precision.HIGHEST)

    return _pc(body, name="cctx_partial", grid=(n // bn,),
               in_specs=[pl.BlockSpec((8, bn), lambda i: (0, i)), pl.BlockSpec((D_MODEL, bn), lambda i: (0, i))],
               out_specs=_full((8, D_MODEL)), out_shape=_sds((8, D_MODEL)),
               compiler_params=_params("arbitrary"))(dmc8, w_ada)


def _cctx_final(parts, c_ctx, m, v):
    c1 = 1.0 - ADAM_B1 ** ADAM_STEP
    c2 = 1.0 - ADAM_B2 ** ADAM_STEP

    def body(p_ref, c_ref, m_ref, v_ref, g_ref, d_ref, mo_ref, vo_ref):
        s = ((p_ref[0, 0:1, :] + p_ref[2, 0:1, :]) + p_ref[4, 0:1, :]) + p_ref[6, 0:1, :]
        z = c_ref[...]
        sg = _sigmoid(z)
        gg = s * (sg * (1.0 + z * (1.0 - sg)))
        g_ref[...] = gg
        mn = ADAM_B1 * m_ref[...] + (1.0 - ADAM_B1) * gg
        vn = ADAM_B2 * v_ref[...] + (1.0 - ADAM_B2) * (gg * gg)
        d_ref[...] = -ADAM_LR * ((mn / c1) / (jnp.sqrt(vn / c2) + ADAM_EPS) + ADAM_WD * z)
        mo_ref[...] = mn
        vo_ref[...] = vn

    row = _full((1, D_MODEL))
    return _pc(body, name="cctx_final", out_shape=[_sds((1, D_MODEL))] * 4,
               in_specs=[_full(parts.shape), row, row, row], out_specs=[row] * 4,
               compiler_params=_params())(parts, c_ctx.reshape(1, D_MODEL), m.reshape(1, D_MODEL), v.reshape(1, D_MODEL))


def _rotary_tables(t_len):
    rows = t_len // GRID_W
    n_freq = DH // 4
    inv = ROPE_BASE ** (-jnp.arange(n_freq, dtype=F32) / n_freq)
    row_ang = jnp.arange(rows, dtype=F32)[:, None] * inv
    col_ang = jnp.arange(GRID_W, dtype=F32)[:, None] * inv

    def spread(fn):
        return jnp.concatenate([jnp.repeat(fn(row_ang), GRID_W, axis=0), jnp.tile(fn(col_ang), (rows, 1))], axis=-1)

    cos, sin = spread(jnp.cos), spread(jnp.sin)
    return jnp.concatenate([cos, cos], axis=-1), jnp.concatenate([-sin, sin], axis=-1)


def _inproj_fwd(x, gn, sh, sc, w4, cos2, sin2, name, comms=()):
    t = x.shape[0]
    tm = _tile(t, True)
    nc = IN_COLS // N_CHIP

    def body(x_ref, gn_ref, sh_ref, sc_ref, w_ref, c_ref, s_ref, p_ref, xr_ref, hb_ref, p_s):
        xh, _ = _rms(x_ref[...])
        h = xh * gn_ref[...] * (1.0 + sc_ref[...]) + sh_ref[...]
        hb = h.astype(BF16)
        hb_ref[...] = hb
        for j in range(N_CHIP):
            p_s[:, nc * j:nc * (j + 1)] = _dot(hb, w_ref[j])
        cc = c_ref[...]
        ss = s_ref[...]
        for hh in range(2 * HEADS):
            blk = p_s[:, DH * hh:DH * (hh + 1)]
            rot = blk * cc + pltpu.roll(blk, DH // 2, 1) * ss
            if hh >= HEADS:
                rot = rot * K_SCALE
            p_ref[:, DH * hh:DH * (hh + 1)] = rot.astype(BF16)
        p_ref[:, 2 * RET_W:] = p_s[:, 2 * RET_W:].astype(BF16)
        xr_ref[...] = p_s[:, 4 * RET_W:4 * RET_W + LRU_W]

    row = _full((1, D_MODEL))
    outs, couts = _call(
        body, name=name, grid=(t // tm,),
        in_specs=[pl.BlockSpec((tm, D_MODEL), lambda i: (i, 0)), row, row, row, _full(w4.shape),
                  pl.BlockSpec((tm, DH), lambda i: (i, 0)), pl.BlockSpec((tm, DH), lambda i: (i, 0))],
        out_specs=[pl.BlockSpec((tm, IN_COLS), lambda i: (i, 0)), pl.BlockSpec((tm, LRU_W), lambda i: (i, 0)),
                   pl.BlockSpec((tm, D_MODEL), lambda i: (i, 0))],
        out_shape=[_sds((t, IN_COLS), BF16), _sds((t, LRU_W)), _sds((t, D_MODEL), BF16)],
        scratch_shapes=[pltpu.VMEM((tm, IN_COLS), F32)], sem=("arbitrary",),
        args=(x, gn, sh, sc, w4, cos2, sin2), comms=comms)
    return (outs, couts) if comms else outs


def _inproj_bwd(x, gn, sh, sc, w4, cos2, sin2, pieces, dres, name):
    t = x.shape[0]
    tm = _tile(t)
    nc = IN_COLS // N_CHIP

    def body(x_ref, gn_ref, sh_ref, sc_ref, w_ref, c_ref, s_ref, dqf, dqb, dkf, dkb, dvf, dvb, dg, dxr, dgt, dres_ref,
             dx_ref, dpb_ref, dgn_ref, dsh_ref, dsc_ref):
        cc = c_ref[...]
        ss = s_ref[...]
        dq = dqf[...].astype(F32) + dqb[...].astype(F32)
        dk = dkf[...].astype(F32) + dkb[...].astype(F32)
        for hh in range(HEADS):
            sl = slice(DH * hh, DH * (hh + 1))
            b = dq[:, sl]
            dpb_ref[:, sl] = (b * cc + pltpu.roll(b * ss, DH // 2, 1)).astype(BF16)
            b = dk[:, sl]
            dpb_ref[:, RET_W + DH * hh:RET_W + DH * (hh + 1)] = (
                (b * cc + pltpu.roll(b * ss, DH // 2, 1)) * K_SCALE).astype(BF16)
        dpb_ref[:, 2 * RET_W:3 * RET_W] = (dvf[...].astype(F32) + dvb[...].astype(F32)).astype(BF16)
        dpb_ref[:, 3 * RET_W:4 * RET_W] = dg[...].astype(BF16)
        dpb_ref[:, 4 * RET_W:4 * RET_W + LRU_W] = dxr[...].astype(BF16)
        dpb_ref[:, 4 * RET_W + LRU_W:IN_COLS] = dgt[...].astype(BF16)
        dh = _dot_nt(dpb_ref[:, 0:nc], w_ref[0])
        for j in range(1, N_CHIP):
            dh = dh + _dot_nt(dpb_ref[:, nc * j:nc * (j + 1)], w_ref[j])
        dx, dgn_t, dsh_t, dsc_t = _norm_mod_bwd(x_ref[...], gn_ref[...], sc_ref[...], dh)
        dx_ref[...] = dres_ref[...] + dx

        @pl.when(pl.program_id(0) == 0)
        def _():
            dgn_ref[...] = jnp.zeros_like(dgn_ref)
            dsh_ref[...] = jnp.zeros_like(dsh_ref)
            dsc_ref[...] = jnp.zeros_like(dsc_ref)
        dgn_ref[...] += dgn_t
        dsh_ref[...] += dsh_t
        dsc_ref[...] += dsc_t

    row = _full((1, D_MODEL))
    pc = pl.BlockSpec((tm, RET_W), lambda i: (i, 0))
    big = pl.BlockSpec((tm, D_MODEL), lambda i: (i, 0))
    return _pc(body, name=name, grid=(t // tm,),
               in_specs=[big, row, row, row, _full(w4.shape),
                         pl.BlockSpec((tm, DH), lambda i: (i, 0)), pl.BlockSpec((tm, DH), lambda i: (i, 0))]
               + [pc] * 9 + [big],
               out_specs=[big, pl.BlockSpec((tm, IN_COLS), lambda i: (i, 0)), row, row, row],
               out_shape=[_sds((t, D_MODEL)), _sds((t, IN_COLS), BF16), _sds((1, D_MODEL)), _sds((1, D_MODEL)),
                          _sds((1, D_MODEL))],
               compiler_params=_params("arbitrary"))(x, gn, sh, sc, w4, cos2, sin2, *pieces, dres)


def _halo_specs(t, tm):
    n8 = tm // SUBLANES
    last8 = t // SUBLANES - 1
    prev = pl.BlockSpec((SUBLANES, LRU_W), lambda i: (jnp.maximum(i * n8 - 1, 0), 0))
    main = pl.BlockSpec((tm, LRU_W), lambda i: (i, 0))
    nxt = pl.BlockSpec((SUBLANES, LRU_W), lambda i: (jnp.minimum((i + 1) * n8, last8), 0))
    return prev, main, nxt


def _with_halo(prev_ref, main_ref, next_ref, i, nt):
    prev = jnp.where(i > 0, prev_ref[...], 0.0)
    nxt = jnp.where(i < nt - 1, next_ref[...], 0.0)
    return jnp.concatenate([prev, main_ref[...], nxt], axis=0)


def _conv_fwd(xr, cw, cb, name):
    t = xr.shape[0]
    tm = _tile(t, True)
    nt = t // tm
    n = tm + 2 * SUBLANES
    mid = slice(SUBLANES, SUBLANES + tm)

    def body(p_ref, m_ref, n_ref, w_ref, b_ref, o_ref):
        xp = _with_halo(p_ref, m_ref, n_ref, pl.program_id(0), nt)
        acc = b_ref[...] + pltpu.roll(xp, 1, 0)[mid] * w_ref[0:1, :]
        acc = acc + xp[mid] * w_ref[1:2, :]
        acc = acc + pltpu.roll(xp, n - 1, 0)[mid] * w_ref[2:3, :]
        acc = acc + pltpu.roll(xp, n - 2, 0)[mid] * w_ref[3:4, :]
        o_ref[...] = acc

    return _pc(body, name=name, grid=(nt,),
               in_specs=[*_halo_specs(t, tm), _full((4, LRU_W)), _full((1, LRU_W))],
               out_specs=pl.BlockSpec((tm, LRU_W), lambda i: (i, 0)), out_shape=_sds((t, LRU_W)),
               compiler_params=_params("arbitrary"))(xr, xr, xr, cw, cb)


def _conv_bwd(dxc_a, dxc_b, xr, cw, name):
    t = xr.shape[0]
    tm = _tile(t, True)
    nt = t // tm
    n = tm + 2 * SUBLANES
    mid = slice(SUBLANES, SUBLANES + tm)

    def body(ap_ref, am_ref, an_ref, bp_ref, bm_ref, bn_ref, xp_ref, xm_ref, xn_ref, w_ref, dx_ref, dw_ref, db_ref):
        i = pl.program_id(0)
        dp = _with_halo(ap_ref, am_ref, an_ref, i, nt) + _with_halo(bp_ref, bm_ref, bn_ref, i, nt)
        xp = _with_halo(xp_ref, xm_ref, xn_ref, i, nt)
        dx = pltpu.roll(dp, n - 1, 0)[mid] * w_ref[0:1, :]
        dx = dx + dp[mid] * w_ref[1:2, :]
        dx = dx + pltpu.roll(dp, 1, 0)[mid] * w_ref[2:3, :]
        dx = dx + pltpu.roll(dp, 2, 0)[mid] * w_ref[3:4, :]
        dx_ref[...] = dx.astype(BF16)
        d = dp[mid]

        @pl.when(i == 0)
        def _():
            dw_ref[...] = jnp.zeros_like(dw_ref)
            db_ref[...] = jnp.zeros_like(db_ref)
        dw_ref[0:1, :] += _sum0(d * pltpu.roll(xp, 1, 0)[mid])
        dw_ref[1:2, :] += _sum0(d * xp[mid])
        dw_ref[2:3, :] += _sum0(d * pltpu.roll(xp, n - 1, 0)[mid])
        dw_ref[3:4, :] += _sum0(d * pltpu.roll(xp, n - 2, 0)[mid])
        db_ref[...] += _sum0(d)

    return _pc(body, name=name, grid=(nt,),
               in_specs=[*_halo_specs(t, tm), *_halo_specs(t, tm), *_halo_specs(t, tm), _full((4, LRU_W))],
               out_specs=[pl.BlockSpec((tm, LRU_W), lambda i: (i, 0)), _full((4, LRU_W)), _full((1, LRU_W))],
               out_shape=[_sds((t, LRU_W), BF16), _sds((4, LRU_W)), _sds((1, LRU_W))],
               compiler_params=_params("arbitrary"))(dxc_a, dxc_a, dxc_a, dxc_b, dxc_b, dxc_b, xr, xr, xr, cw)


def _local_scan(a, b, reverse):
    n = a.shape[0]
    row = lax.broadcasted_iota(jnp.int32, a.shape, 0) & (SUBLANES - 1)
    for s in (1, 2, 4):
        if reverse:
            a_s, b_s, ok = pltpu.roll(a, n - s, 0), pltpu.roll(b, n - s, 0), row < SUBLANES - s
        else:
            a_s, b_s, ok = pltpu.roll(a, s, 0), pltpu.roll(b, s, 0), row >= s
        b = a * jnp.where(ok, b_s, 0.0) + b
        a = a * jnp.where(ok, a_s, 1.0)
    return a, b


def _carry_scan(a_s, b_s, out_ref, carry, reverse):
    ng = a_s.shape[0] // SUBLANES
    shape = carry.shape

    def step(g, cr):
        gg = (ng - 1 - g) if reverse else g
        off = pl.multiple_of(gg * SUBLANES, SUBLANES)
        h = a_s[pl.ds(off, SUBLANES), :] * cr + b_s[pl.ds(off, SUBLANES), :]
        out_ref[pl.ds(off, SUBLANES), :] = h
        edge = h[0:1, :] if reverse else h[SUBLANES - 1:SUBLANES, :]
        return jnp.broadcast_to(edge, shape)

    return lax.fori_loop(0, ng, step, carry)


def _lru_gates(xc, wa_ref, wx_ref, ba, bx, lam):
    xb = xc.astype(BF16)
    r = _sigmoid(_dot(xb, wa_ref[...]) + ba)
    ig = _sigmoid(_dot(xb, wx_ref[...]) + bx)
    sp = _softplus(-lam)
    la = -LRU_C * r * sp
    a = jnp.exp(la)
    mult = jnp.sqrt(_one_minus_sq(la, a))
    return r, ig, sp, a, mult


def _lru_fwd(xc, wa, wx, ba, bx, lam, h0, reverse, name, comms=()):
    t = xc.shape[0]
    tm = _tile(t, True)
    nt = t // tm
    tidx = (lambda i: (nt - 1 - i, 0)) if reverse else (lambda i: (i, 0))

    def body(x_ref, wa_ref, wx_ref, ba_ref, bx_ref, lam_ref, h0_ref, h_ref, a_s, b_s, c_s):
        @pl.when(pl.program_id(0) == 0)
        def _():
            c_s[...] = jnp.broadcast_to(h0_ref[...], c_s.shape)
        xv = x_ref[...]
        _, ig, _, a, mult = _lru_gates(xv, wa_ref, wx_ref, ba_ref[...], bx_ref[...], lam_ref[...])
        al, bl = _local_scan(a, mult * (ig * xv), reverse)
        a_s[...] = al
        b_s[...] = bl
        c_s[...] = _carry_scan(a_s, b_s, h_ref, c_s[...], reverse)

    vec = _full((1, LRU_W))
    mat = _full((LRU_W, LRU_W))
    (h,), couts = _call(body, name=name, grid=(nt,),
                        in_specs=[pl.BlockSpec((tm, LRU_W), tidx), mat, mat, vec, vec, vec, vec],
                        out_specs=[pl.BlockSpec((tm, LRU_W), tidx)], out_shape=[_sds((t, LRU_W))],
                        scratch_shapes=[pltpu.VMEM((tm, LRU_W), F32), pltpu.VMEM((tm, LRU_W), F32),
                                        pltpu.VMEM((SUBLANES, LRU_W), F32)],
                        sem=("arbitrary",), args=(xc, wa, wx, ba, bx, lam, h0), comms=comms)
    return (h, couts) if comms else h


def _lru_bwd(xc, wa, wx, ba, bx, lam, h, h0, dh, reverse, name, comms=()):
    t = xc.shape[0]
    tm = _tile(t, True)
    nt = t // tm
    n8 = tm // SUBLANES
    last8 = t // SUBLANES - 1
    tidx = (lambda i: (i, 0)) if reverse else (lambda i: (nt - 1 - i, 0))
    if reverse:
        halo = pl.BlockSpec((SUBLANES, LRU_W), lambda i: (jnp.minimum((i + 1) * n8, last8), 0))
    else:
        halo = pl.BlockSpec((SUBLANES, LRU_W), lambda i: (jnp.maximum((nt - 1 - i) * n8 - 1, 0), 0))

    def body(x_ref, wa_ref, wx_ref, ba_ref, bx_ref, lam_ref, h_ref, halo_ref, h0_ref, dh_ref,
             dx_ref, dpre_ref, dba_ref, dbx_ref, dlam_ref, dh0_ref, a_s, b_s, l_s, c_s, e_s):
        i = pl.program_id(0)

        @pl.when(i == 0)
        def _():
            c_s[...] = jnp.zeros_like(c_s)
            e_s[...] = jnp.zeros_like(e_s)
            dba_ref[...] = jnp.zeros_like(dba_ref)
            dbx_ref[...] = jnp.zeros_like(dbx_ref)
            dlam_ref[...] = jnp.zeros_like(dlam_ref)
        xv = x_ref[...]
        lam = lam_ref[...]
        r, ig, sp, a, mult = _lru_gates(xv, wa_ref, wx_ref, ba_ref[...], bx_ref[...], lam)
        hv = h_ref[...]
        rowi = lax.broadcasted_iota(jnp.int32, (tm, LRU_W), 0)
        edge_a = jnp.broadcast_to(e_s[0:1, :], (tm, LRU_W))
        h0b = jnp.broadcast_to(h0_ref[...], (tm, LRU_W))
        if reverse:
            a_sh = jnp.where(rowi == 0, edge_a, pltpu.roll(a, 1, 0))
            hin_edge = jnp.where(i == nt - 1, h0b, jnp.broadcast_to(halo_ref[0:1, :], (tm, LRU_W)))
            h_in = jnp.where(rowi == tm - 1, hin_edge, pltpu.roll(hv, tm - 1, 0))
        else:
            a_sh = jnp.where(rowi == tm - 1, edge_a, pltpu.roll(a, tm - 1, 0))
            hin_edge = jnp.where(i == nt - 1, h0b, jnp.broadcast_to(halo_ref[SUBLANES - 1:SUBLANES, :], (tm, LRU_W)))
            h_in = jnp.where(rowi == 0, hin_edge, pltpu.roll(hv, 1, 0))
        al, bl = _local_scan(a_sh, dh_ref[...], not reverse)
        a_s[...] = al
        b_s[...] = bl
        c_s[...] = _carry_scan(a_s, b_s, l_s, c_s[...], not reverse)
        e_s[...] = jnp.broadcast_to(a[tm - 1:tm, :] if reverse else a[0:1, :], e_s.shape)
        lmb = l_s[...]
        da = lmb * h_in
        ixc = ig * xv
        dmult = lmb * ixc
        dixc = lmb * mult
        dla = da * a - dmult * (a * a) / mult
        dpr = dla * (-LRU_C * sp) * r * (1.0 - r)
        dpi = dixc * xv * ig * (1.0 - ig)
        dprb = dpr.astype(BF16)
        dpib = dpi.astype(BF16)
        dpre_ref[:, 0:LRU_W] = dprb
        dpre_ref[:, LRU_W:2 * LRU_W] = dpib
        dx_ref[...] = dixc * ig + _dot_nt(dprb, wa_ref[...]) + _dot_nt(dpib, wx_ref[...])
        dba_ref[...] += _sum0(dpr)
        dbx_ref[...] += _sum0(dpi)
        dlam_ref[...] += _sum0(dla * (-LRU_C * r)) * (-_sigmoid(-lam))

        @pl.when(i == nt - 1)
        def _():
            al0 = a * lmb
            dh0_ref[...] = al0[tm - 1:tm, :] if reverse else al0[0:1, :]

    vec = _full((1, LRU_W))
    mat = _full((LRU_W, LRU_W))
    tile = pl.BlockSpec((tm, LRU_W), tidx)
    return _call(body, name=name, grid=(nt,),
                 in_specs=[tile, mat, mat, vec, vec, vec, tile, halo, vec, tile],
                 out_specs=[tile, pl.BlockSpec((tm, 2 * LRU_W), tidx), vec, vec, vec, vec],
                 out_shape=[_sds((t, LRU_W)), _sds((t, 2 * LRU_W), BF16), _sds((1, LRU_W)), _sds((1, LRU_W)),
                            _sds((1, LRU_W)), _sds((1, LRU_W))],
                 scratch_shapes=[pltpu.VMEM((tm, LRU_W), F32), pltpu.VMEM((tm, LRU_W), F32),
                                 pltpu.VMEM((tm, LRU_W), F32), pltpu.VMEM((SUBLANES, LRU_W), F32),
                                 pltpu.VMEM((SUBLANES, LRU_W), F32)],
                 sem=("arbitrary",), args=(xc, wa, wx, ba, bx, lam, h, h, h0, dh), comms=comms)


def _decay_tables(lg, reverse):
    ci = lax.broadcasted_iota(jnp.int32, (CHUNK, CHUNK), 0).astype(F32)
    mi = lax.broadcasted_iota(jnp.int32, (CHUNK, CHUNK), 1).astype(F32)
    rel = (mi - ci) if reverse else (ci - mi)
    relc = jnp.maximum(rel, 0.0)
    lg_c = jnp.concatenate([lg] * (CHUNK // LANES), axis=1)
    dm = jnp.where(rel >= 0, jnp.exp(lg_c * relc), 0.0)
    cd = lax.broadcasted_iota(jnp.int32, (CHUNK, DH), 0).astype(F32)
    pq, ps = (CHUNK - cd, cd) if reverse else (cd + 1.0, CHUNK - 1.0 - cd)
    return relc, dm, jnp.exp(lg * pq), jnp.exp(lg * ps), jnp.exp(lg * float(CHUNK)), pq, ps


def _ret_fwd(proj, lgv, s0f, s0b, comms=()):
    t = proj.shape[0]
    n = t // CHUNK

    def one(q, k, v, lg, s_s, hh, o_ref, sp_ref, reverse):
        _, dm, wq, ws, g, _, _ = _decay_tables(lg, reverse)
        vb = v.astype(BF16)
        p = _dot_nt(q.astype(BF16), k.astype(BF16)) * dm
        s = s_s[hh]
        sp_ref[hh, 0] = s
        o_ref[:, DH * hh:DH * (hh + 1)] = _dot(p.astype(BF16), vb) + _dot((q * wq).astype(BF16), s.astype(BF16))
        s_s[hh] = g * s + _dot_tn((k * ws).astype(BF16), vb)

    def body(qf, kf, vf, qb, kb, vb, lg_ref, s0f_ref, s0b_ref, of_ref, ob_ref, spf_ref, spb_ref, sf_s, sb_s):
        @pl.when(pl.program_id(0) == 0)
        def _():
            sf_s[...] = s0f_ref[...]
            sb_s[...] = s0b_ref[...]
        for hh in range(HEADS):
            sl = slice(DH * hh, DH * (hh + 1))
            one(qf[:, sl].astype(F32), kf[:, sl].astype(F32), vf[:, sl], lg_ref[hh, 0:1, :], sf_s, hh, of_ref, spf_ref,
                False)
            one(qb[:, sl].astype(F32), kb[:, sl].astype(F32), vb[:, sl], lg_ref[hh, 1:2, :], sb_s, hh, ob_ref, spb_ref,
                True)

    blk = (CHUNK, RET_W)
    fw = [pl.BlockSpec(blk, lambda i, o=o: (i, o)) for o in range(3)]
    bw = [pl.BlockSpec(blk, lambda i, o=o: (n - 1 - i, o)) for o in range(3)]
    st = _full((HEADS, DH, DH))
    return _call(body, name="ret_fwd", grid=(n,),
                 in_specs=fw + bw + [_full((HEADS, 2, LANES)), st, st],
                 out_specs=[pl.BlockSpec(blk, lambda i: (i, 0)), pl.BlockSpec(blk, lambda i: (n - 1 - i, 0)),
                            pl.BlockSpec((HEADS, 1, DH, DH), lambda i: (0, i, 0, 0)),
                            pl.BlockSpec((HEADS, 1, DH, DH), lambda i: (0, n - 1 - i, 0, 0))],
                 out_shape=[_sds((t, RET_W)), _sds((t, RET_W)), _sds((HEADS, n, DH, DH)), _sds((HEADS, n, DH, DH))],
                 scratch_shapes=[pltpu.VMEM((HEADS, DH, DH), F32), pltpu.VMEM((HEADS, DH, DH), F32)],
                 sem=("arbitrary",), args=(proj, proj, proj, proj, proj, proj, lgv, s0f, s0b), comms=comms)


def _ret_bwd(proj, lgv, sgv, sprev, do, reverse, name, comms=()):
    t = proj.shape[0]
    n = t // CHUNK
    d = 1 if reverse else 0
    cidx = (lambda i: i) if reverse else (lambda i: n - 1 - i)

    def body(q_ref, k_ref, v_ref, lg_ref, sg_ref, s_ref, do_ref, dq_ref, dk_ref, dv_ref, ds0_ref, drd_ref, ds_s, acc_s):
        i = pl.program_id(0)

        @pl.when(i == 0)
        def _():
            ds_s[...] = jnp.zeros_like(ds_s)
            acc_s[...] = jnp.zeros_like(acc_s)
        for hh in range(HEADS):
            sl = slice(DH * hh, DH * (hh + 1))
            relc, dm, wq, ws, g, pq, ps = _decay_tables(lg_ref[hh, d:d + 1, :], reverse)
            qb, kb, vb = q_ref[:, sl], k_ref[:, sl], v_ref[:, sl]
            q, k = qb.astype(F32), kb.astype(F32)
            p = _dot_nt(qb, kb) * dm
            s = s_ref[hh, 0]
            dob = do_ref[:, sl].astype(BF16)
            dsn = ds_s[hh]
            dsb = dsn.astype(BF16)
            dv_ref[:, sl] = (_dot_tn(p.astype(BF16), dob) + _dot((k * ws).astype(BF16), dsb)).astype(BF16)
            dp = _dot_nt(dob, vb)
            dab = (dp * dm).astype(BF16)
            xq = _dot_nt(dob, s.astype(BF16))
            yk = _dot_nt(vb, dsb)
            dq_ref[:, sl] = (_dot(dab, kb) + xq * wq).astype(BF16)
            dk_ref[:, sl] = (_dot_tn(dab, qb) + yk * ws).astype(BF16)
            ds_s[hh] = g * dsn + _dot_tn((q * wq).astype(BF16), dob)
            s_mask = _sum0(dp * p * relc)
            part = (sum(s_mask[:, LANES * u:LANES * (u + 1)] for u in range(CHUNK // LANES))
                    + _sum0(xq * q * wq * pq) + _sum0(yk * k * ws * ps) + _sum0(dsn * s) * g * float(CHUNK))
            acc_s[hh] += jnp.broadcast_to(part, (SUBLANES, LANES))

        @pl.when(i == n - 1)
        def _():
            ds0_ref[...] = ds_s[...]
            for hh in range(HEADS):
                tot = jnp.sum(acc_s[hh, 0:1, :], axis=1, keepdims=True)
                drd_ref[hh] = jnp.broadcast_to(tot, (SUBLANES, LANES)) * sg_ref[hh, d:d + 1, :]

    blk = (CHUNK, RET_W)
    qkv = [pl.BlockSpec(blk, lambda i, o=o: (cidx(i), o)) for o in range(3)]
    hc = pl.BlockSpec(blk, lambda i: (cidx(i), 0))
    lane = _full((HEADS, 2, LANES))
    return _call(body, name=name, grid=(n,),
                 in_specs=qkv + [lane, lane, pl.BlockSpec((HEADS, 1, DH, DH), lambda i: (0, cidx(i), 0, 0)), hc],
                 out_specs=[hc, hc, hc, _full((HEADS, DH, DH)), _full((HEADS, SUBLANES, LANES))],
                 out_shape=[_sds((t, RET_W), BF16)] * 3 + [_sds((HEADS, DH, DH)), _sds((HEADS, SUBLANES, LANES))],
                 scratch_shapes=[pltpu.VMEM((HEADS, DH, DH), F32), pltpu.VMEM((HEADS, SUBLANES, LANES), F32)],
                 sem=("arbitrary",), args=(proj, proj, proj, lgv, sgv, sprev, do), comms=comms)


def _ctx_weights(lg, l_len, reverse):
    pos = lax.broadcasted_iota(jnp.int32, (l_len, DH), 0).astype(F32)
    steps = pos if reverse else (l_len - 1.0 - pos)
    return jnp.exp(lg * steps), steps


def _ctx_state_fwd(projc, lgv):
    l_len = projc.shape[0]

    def body(k_ref, v_ref, lg_ref, sf_ref, sb_ref):
        k = k_ref[...]
        vb = v_ref[...].astype(BF16)
        for d, o_ref in ((0, sf_ref), (1, sb_ref)):
            w, _ = _ctx_weights(lg_ref[0, d:d + 1, :], l_len, d == 1)
            o_ref[0] = _dot_tn((k * w).astype(BF16), vb)

    st = pl.BlockSpec((1, DH, DH), lambda h: (h, 0, 0))
    return _pc(body, name="ctx_state_fwd", grid=(HEADS,),
               in_specs=[pl.BlockSpec((l_len, DH), lambda h: (0, HEADS + h)),
                         pl.BlockSpec((l_len, DH), lambda h: (0, 2 * HEADS + h)),
                         pl.BlockSpec((1, 2, LANES), lambda h: (h, 0, 0))],
               out_specs=[st, st], out_shape=[_sds((HEADS, DH, DH))] * 2,
               compiler_params=_params("arbitrary"))(projc, projc, lgv)


def _ctx_state_bwd(projc, lgv, sgv, dsf, dsb):
    l_len = projc.shape[0]

    def body(k_ref, v_ref, lg_ref, sg_ref, dsf_ref, dsb_ref, dk_ref, dv_ref, drd_ref):
        k = k_ref[...]
        vb = v_ref[...].astype(BF16)
        dk = jnp.zeros((l_len, DH), F32)
        dv = jnp.zeros((l_len, DH), F32)
        rows = []
        for d, ds_ref in ((0, dsf_ref), (1, dsb_ref)):
            w, steps = _ctx_weights(lg_ref[0, d:d + 1, :], l_len, d == 1)
            dsb16 = ds_ref[0].astype(BF16)
            dkw = _dot_nt(vb, dsb16)
            dk = dk + dkw * w
            dv = dv + _dot((k * w).astype(BF16), dsb16)
            tot = jnp.sum(_sum0(dkw * k * w * steps), axis=1, keepdims=True)
            rows.append(jnp.broadcast_to(tot, (1, LANES)) * sg_ref[0, d:d + 1, :])
        dk_ref[...] = dk.astype(BF16)
        dv_ref[...] = dv.astype(BF16)
        rid = lax.broadcasted_iota(jnp.int32, (SUBLANES, LANES), 0)
        drd_ref[0] = jnp.where(rid == 0, rows[0], jnp.where(rid == 1, rows[1], 0.0))

    st = pl.BlockSpec((1, DH, DH), lambda h: (h, 0, 0))
    lane = pl.BlockSpec((1, 2, LANES), lambda h: (h, 0, 0))
    hc = pl.BlockSpec((l_len, DH), lambda h: (0, h))
    return _pc(body, name="ctx_state_bwd", grid=(HEADS,),
               in_specs=[pl.BlockSpec((l_len, DH), lambda h: (0, HEADS + h)),
                         pl.BlockSpec((l_len, DH), lambda h: (0, 2 * HEADS + h)), lane, lane, st, st],
               out_specs=[hc, hc, pl.BlockSpec((1, SUBLANES, LANES), lambda h: (h, 0, 0))],
               out_shape=[_sds((l_len, RET_W), BF16), _sds((l_len, RET_W), BF16), _sds((HEADS, SUBLANES, LANES))],
               compiler_params=_params("arbitrary"))(projc, projc, lgv, sgv, dsf, dsb)


G_BLOCK = (3 * RET_W) // RET_W
GATE_BLOCK = (4 * RET_W + LRU_W) // LRU_W


def _head_norm(y):
    yc = y - jnp.mean(y, axis=-1, keepdims=True)
    rs = lax.rsqrt(jnp.mean(yc * yc, axis=-1, keepdims=True) + EPS)
    return yc * rs, rs


def _gelu_parts(z):
    th = jnp.tanh(GELU_K * (z + GELU_C * z * z * z))
    return 0.5 * z * (1.0 + th), th


def _mix_fwd(o_f, o_b, proj, hf, hb, w_out, x, g1):
    t = x.shape[0]
    tm = _tile(t, True)

    def body(of_ref, ob_ref, g_ref, gt_ref, hf_ref, hb_ref, w_ref, x_ref, g1_ref, x1_ref, cat_ref):
        o = of_ref[...] + ob_ref[...]
        g = g_ref[...].astype(F32)
        for hh in range(HEADS):
            sl = slice(DH * hh, DH * (hh + 1))
            nrm, _ = _head_norm(o[:, sl])
            gh = g[:, sl]
            cat_ref[:, sl] = (gh * _sigmoid(gh) * nrm).astype(BF16)
        gel, _ = _gelu_parts(gt_ref[...].astype(F32))
        cat_ref[:, RET_W:] = ((hf_ref[...] + hb_ref[...]) * gel).astype(BF16)
        x1_ref[...] = x_ref[...] + g1_ref[...] * _dot(cat_ref[...], w_ref[...])

    half = pl.BlockSpec((tm, RET_W), lambda i: (i, 0))
    big = pl.BlockSpec((tm, D_MODEL), lambda i: (i, 0))
    return _pc(body, name="mix_fwd", grid=(t // tm,),
               in_specs=[half, half, pl.BlockSpec((tm, RET_W), lambda i: (i, G_BLOCK)),
                         pl.BlockSpec((tm, LRU_W), lambda i: (i, GATE_BLOCK)), half, half,
                         _full((D_MODEL, D_MODEL)), big, _full((1, D_MODEL))],
               out_specs=[big, big], out_shape=[_sds((t, D_MODEL)), _sds((t, D_MODEL), BF16)],
               compiler_params=_params("arbitrary"))(o_f, o_b, proj, proj, hf, hb, w_out, x, g1)


def _mix_bwd(o_f, o_b, proj, hf, hb, w_out, cat, dx1, g1, comms=()):
    t = dx1.shape[0]
    tm = _tile(t, True)

    def body(of_ref, ob_ref, g_ref, gt_ref, hf_ref, hb_ref, w_ref, cat_ref, dx1_ref, g1_ref,
             do_ref, dhs_ref, dg_ref, dgt_ref, dyb_ref, dg1_ref):
        dx1v = dx1_ref[...]
        y = _dot(cat_ref[...], w_ref[...])

        @pl.when(pl.program_id(0) == 0)
        def _():
            dg1_ref[...] = jnp.zeros_like(dg1_ref)
        dg1_ref[...] += _sum0(dx1v * y)
        dyb = (g1_ref[...] * dx1v).astype(BF16)
        dyb_ref[...] = dyb
        dcat = _dot_nt(dyb, w_ref[...])
        o = of_ref[...] + ob_ref[...]
        g = g_ref[...].astype(F32)
        for hh in range(HEADS):
            sl = slice(DH * hh, DH * (hh + 1))
            nrm, rs = _head_norm(o[:, sl])
            gh = g[:, sl]
            sg = _sigmoid(gh)
            dret = dcat[:, sl]
            dg_ref[:, sl] = (dret * nrm * (sg * (1.0 + gh * (1.0 - sg)))).astype(BF16)
            dn = dret * (gh * sg)
            dyc = rs * (dn - nrm * jnp.mean(dn * nrm, axis=-1, keepdims=True))
            do_ref[:, sl] = (dyc - jnp.mean(dyc, axis=-1, keepdims=True)).astype(BF16)
        z = gt_ref[...].astype(F32)
        gel, th = _gelu_parts(z)
        dlru = dcat[:, RET_W:]
        dhs_ref[...] = dlru * gel
        dgel = 0.5 * (1.0 + th) + 0.5 * z * (1.0 - th * th) * GELU_K * (1.0 + 3.0 * GELU_C * z * z)
        dgt_ref[...] = (dlru * (hf_ref[...] + hb_ref[...]) * dgel).astype(BF16)

    half = pl.BlockSpec((tm, RET_W), lambda i: (i, 0))
    big = pl.BlockSpec((tm, D_MODEL), lambda i: (i, 0))
    return _call(body, name="mix_bwd", grid=(t // tm,),
                 in_specs=[half, half, pl.BlockSpec((tm, RET_W), lambda i: (i, G_BLOCK)),
                           pl.BlockSpec((tm, LRU_W), lambda i: (i, GATE_BLOCK)), half, half,
                           _full((D_MODEL, D_MODEL)), big, big, _full((1, D_MODEL))],
                 out_specs=[half, half, half, half, big, _full((1, D_MODEL))],
                 out_shape=[_sds((t, RET_W), BF16), _sds((t, RET_W)), _sds((t, RET_W), BF16), _sds((t, RET_W), BF16),
                            _sds((t, D_MODEL), BF16), _sds((1, D_MODEL))],
                 scratch_shapes=[], sem=("arbitrary",), args=(o_f, o_b, proj, proj, hf, hb, w_out, cat, dx1, g1),
                 comms=comms)


def _mlp(x1, n2g, sh2, sc2, g2, fg, w1_parts, w2_parts, tgt):
    t = x1.shape[0]
    tm = _tile(t)
    hb_ = MLP_H // N_CHIP
    q_rows = hb_ // 4
    n_cp = 4 * N_DEV

    def body(x1_ref, n2g_ref, sh2_ref, sc2_ref, g2_ref, fg_ref, w1a, w1b, w2a, w2b, tgt_ref,
             dx1_ref, h2b_ref, ab_ref, dub_ref, dmb_ref, dsc_ref, dsh_ref, dg2_ref, dn2_ref, dfg_ref, loss_ref,
             w1_s, w2_s, r_s, sems):
        @pl.when(pl.program_id(0) == 0)
        def _():
            cps = []
            for p, parts in enumerate(((w1a, w2a), (w1b, w2b))):
                for d in range(N_DEV):
                    rows = pl.ds(2 * q_rows * (d % 2) + q_rows * p, q_rows)
                    for src, dst in zip(parts, (w1_s, w2_s)):
                        cps.append(pltpu.make_async_copy(src.at[d], dst.at[d // 2, rows], sems.at[len(cps)]))
            for cp in cps:
                cp.start()
            for r in (dsc_ref, dsh_ref, dg2_ref, dn2_ref, dfg_ref, loss_ref):
                r[...] = jnp.zeros_like(r)
            for cp in cps:
                cp.wait()
        x1v = x1_ref[...]
        n2g, sc2, g2, fg = n2g_ref[...], sc2_ref[...], g2_ref[...], fg_ref[...]
        xh, _ = _rms(x1v)
        h2b = (xh * n2g * (1.0 + sc2) + sh2_ref[...]).astype(BF16)
        h2b_ref[...] = h2b
        m = jnp.zeros((tm, D_MODEL), F32)
        for j in range(N_CHIP):
            sl = slice(hb_ * j, hb_ * (j + 1))
            r = jnp.maximum(_dot(h2b, w1_s[j]), 0.0)
            r_s[:, sl] = r
            ab = (r * r).astype(BF16)
            ab_ref[:, sl] = ab
            m = m + _dot(ab, w2_s[j])
        x2 = x1v + g2 * m
        x2h, r2 = _rms(x2)
        err = x2h * fg - tgt_ref[...]
        loss_ref[...] += _sum0(err * err)
        dout = err * (1.0 / D_MODEL)
        dfg_ref[...] += _sum0(dout * x2h)
        dxh = dout * fg
        dx2 = r2 * (dxh - x2h * jnp.mean(dxh * x2h, axis=-1, keepdims=True))
        dg2_ref[...] += _sum0(dx2 * m)
        dmb = (g2 * dx2).astype(BF16)
        dmb_ref[...] = dmb
        dh2 = jnp.zeros((tm, D_MODEL), F32)
        for j in range(N_CHIP):
            sl = slice(hb_ * j, hb_ * (j + 1))
            dub = (_dot_nt(dmb, w2_s[j]) * (2.0 * r_s[:, sl])).astype(BF16)
            dub_ref[:, sl] = dub
            dh2 = dh2 + _dot_nt(dub, w1_s[j])
        dx, dn2_t, dsh_t, dsc_t = _norm_mod_bwd(x1v, n2g, sc2, dh2)
        dx1_ref[...] = dx2 + dx
        dn2_ref[...] += dn2_t
        dsh_ref[...] += dsh_t
        dsc_ref[...] += dsc_t

        @pl.when(pl.program_id(0) == t // tm - 1)
        def _():
            tot = jnp.sum(loss_ref[...], axis=1, keepdims=True) * (0.5 / D_MODEL)
            loss_ref[...] = jnp.broadcast_to(tot, loss_ref.shape)

    row = _full((1, D_MODEL))
    big = pl.BlockSpec((tm, D_MODEL), lambda i: (i, 0))
    wide = pl.BlockSpec((tm, MLP_H), lambda i: (i, 0))
    return _pc(body, name="mlp", grid=(t // tm,),
               in_specs=[big, row, row, row, row, row, ANY, ANY, ANY, ANY, big],
               out_specs=[big, big, wide, wide, big, row, row, row, row, row, row],
               out_shape=[_sds((t, D_MODEL)), _sds((t, D_MODEL), BF16), _sds((t, MLP_H), BF16), _sds((t, MLP_H), BF16),
                          _sds((t, D_MODEL), BF16)] + [_sds((1, D_MODEL))] * 6,
               scratch_shapes=[pltpu.VMEM((N_CHIP, D_MODEL, hb_), BF16), pltpu.VMEM((N_CHIP, hb_, D_MODEL), BF16),
                               pltpu.VMEM((tm, MLP_H), F32), pltpu.SemaphoreType.DMA((n_cp,))],
               compiler_params=_params("arbitrary"))(x1, n2g, sh2, sc2, g2, fg, *w1_parts, *w2_parts, tgt)


def _tn(a, b, nj, a_blocked, b_blocked, name, extra=None, comms=()):
    t = a.shape[0]
    m = a.shape[1] // (nj if a_blocked else 1)
    n = b.shape[1] // (nj if b_blocked else 1)
    bk = next((b for b in (2048, 1024, 512) if t % b == 0), t)
    nk = t // bk
    a_col = (lambda j: j) if a_blocked else (lambda j: 0)
    b_col = (lambda j: j) if b_blocked else (lambda j: 0)
    in_specs = [pl.BlockSpec((bk, m), lambda j, k: (k, a_col(j))), pl.BlockSpec((bk, n), lambda j, k: (k, b_col(j)))]
    args = [a, b]
    if extra is not None:
        a2, b2 = extra
        t2 = a2.shape[0]
        in_specs += [pl.BlockSpec((t2, m), lambda j, k: (0, a_col(j))),
                     pl.BlockSpec((t2, n), lambda j, k: (0, b_col(j)))]
        args += [a2, b2]

    def body(*refs):
        a_ref, b_ref = refs[0], refs[1]
        o_ref, acc = refs[-2], refs[-1]
        k = pl.program_id(1)

        @pl.when(k == 0)
        def _():
            acc[...] = jnp.zeros_like(acc)
        acc[...] += _dot_tn(a_ref[...].astype(BF16), b_ref[...].astype(BF16))

        @pl.when(k == nk - 1)
        def _():
            if extra is not None:
                acc[...] += _dot_tn(refs[2][...].astype(BF16), refs[3][...].astype(BF16))
            o_ref[0] = acc[...]

    (out,), couts = _call(body, name=name, grid=(nj, nk), in_specs=in_specs,
                          out_specs=[pl.BlockSpec((1, m, n), lambda j, k: (j, 0, 0))], out_shape=[_sds((nj, m, n))],
                          scratch_shapes=[pltpu.VMEM((m, n), F32)], sem=("arbitrary", "arbitrary"), args=args,
                          comms=comms)
    return (out, couts) if comms else out


ROW_LOSS = 0
ROW_DMOD = 1
ROW_DMODC = 7
ROW_N1, ROW_N2, ROW_FG, ROW_CB = 9, 10, 11, 12
ROW_BA, ROW_BX, ROW_LAM = 13, 15, 17
ROW_CW = 20
ROW_RD = 24
SLAB_ROWS = 32
SEG = D_MODEL // 2


def _pack_small(rows, drd, cw2, cb2, lru2, gates):
    n_rows, n_lru = len(rows), len(lru2)

    def body(*refs):
        r = refs[:n_rows]
        drd_f, drd_b, drd_c, cw_a, cw_b, cb_a, cb_b = refs[n_rows:n_rows + 7]
        lru = refs[n_rows + 7:n_rows + 7 + n_lru]
        gf_ref, gb_ref, slab, ga, gx = refs[n_rows + 7 + n_lru:]
        slab[...] = jnp.zeros_like(slab)
        slab[ROW_LOSS:ROW_LOSS + 1, :] = r[0][...]
        for k in range(N_MOD):
            slab[ROW_DMOD + k:ROW_DMOD + k + 1, :] = r[1 + k][...]
        slab[ROW_DMODC:ROW_DMODC + 1, :] = r[7][...]
        slab[ROW_DMODC + 1:ROW_DMODC + 2, :] = r[8][...]
        slab[ROW_N1:ROW_N1 + 1, :] = r[9][...] + r[10][...]
        slab[ROW_N2:ROW_N2 + 1, :] = r[11][...]
        slab[ROW_FG:ROW_FG + 1, :] = r[12][...]
        slab[ROW_CB:ROW_CB + 1, 0:LRU_W] = cb_a[...] + cb_b[...]
        for k, row in enumerate((ROW_BA, ROW_BA + 1, ROW_BX, ROW_BX + 1, ROW_LAM, ROW_LAM + 1)):
            slab[row:row + 1, 0:LRU_W] = lru[2 * k][...] + lru[2 * k + 1][...]
        slab[ROW_CW:ROW_CW + 4, 0:LRU_W] = cw_a[...] + cw_b[...]
        for h in range(HEADS):
            slab[ROW_RD + h:ROW_RD + h + 1, 0:LANES] = drd_f[h, 0:1, :] + drd_c[h, 0:1, :]
            slab[ROW_RD + HEADS + h:ROW_RD + HEADS + h + 1, 0:LANES] = drd_b[h, 0:1, :] + drd_c[h, 1:2, :]
        for d, g_ref in enumerate((gf_ref, gb_ref)):
            for n in range(LRU_BLOCKS):
                blk = slice(LRU_BD * n, LRU_BD * (n + 1))
                ga[blk, LRU_BD * d:LRU_BD * (d + 1)] = g_ref[0, blk, blk].astype(BF16)
                gx[blk, LRU_BD * d:LRU_BD * (d + 1)] = g_ref[1, blk, blk].astype(BF16)

    args = list(rows) + list(drd) + list(cw2) + list(cb2) + list(lru2) + list(gates)
    gate_shape = (LRU_W, 2 * LRU_BD)
    return _pc(body, name="pack_small", in_specs=[_full(a.shape) for a in args],
               out_specs=[_full((SLAB_ROWS, D_MODEL)), _full(gate_shape), _full(gate_shape)],
               out_shape=[_sds((SLAB_ROWS, D_MODEL)), _sds(gate_shape, BF16), _sds(gate_shape, BF16)],
               compiler_params=_params())(*args)


def _adam_math(w, g, m, v):
    mn = ADAM_B1 * m + (1.0 - ADAM_B1) * g
    vn = ADAM_B2 * v + (1.0 - ADAM_B2) * (g * g)
    mh = mn / (1.0 - ADAM_B1 ** ADAM_STEP)
    vh = vn / (1.0 - ADAM_B2 ** ADAM_STEP)
    return -ADAM_LR * (mh / (jnp.sqrt(vh) + ADAM_EPS) + ADAM_WD * w), mn, vn


SMALL_PARAMS = ("b_ada", "norm1_g", "norm2_g", "final_g", "ret_decay", "conv_w", "conv_b", "lru_wa", "lru_ba", "lru_wx",
                "lru_bx", "lru_lambda")


def _finalize_small(chip_idx, slab_all, ga_all, gx_all, wmv):
    n_p = len(SMALL_PARAMS)
    flat = [a for nm in SMALL_PARAMS for a in wmv[nm]]
    ada_n = N_MOD * D_MODEL // N_CHIP

    def body(c_ref, slab_ref, ga_ref, gx_ref, *refs):
        prm = {nm: refs[3 * k:3 * k + 3] for k, nm in enumerate(SMALL_PARAMS)}
        outs = {nm: refs[3 * n_p + 4 * k:3 * n_p + 4 * k + 4] for k, nm in enumerate(SMALL_PARAMS)}
        b128_ref, dmc_ref, loss_ref = refs[3 * n_p + 4 * n_p:]
        chip = c_ref[0]

        def pick(fn):
            acc = fn(0)
            for j in range(1, N_CHIP):
                acc = jnp.where(chip == j, fn(j), acc)
            return acc

        tot = slab_ref[0]
        for d in range(1, N_DEV):
            tot = tot + slab_ref[d]

        def update(nm, g, sl=None, rows=None):
            w_ref, m_ref, v_ref = prm[nm]
            g_ref, d_ref, mo_ref, vo_ref = outs[nm]
            ix = (slice(None) if rows is None else rows, slice(None) if sl is None else sl)
            dl, mn, vn = _adam_math(w_ref[ix], g, m_ref[ix], v_ref[ix])
            g_ref[ix] = g
            d_ref[ix] = dl
            mo_ref[ix] = mn
            vo_ref[ix] = vn

        loss_ref[...] = jnp.broadcast_to(tot[ROW_LOSS:ROW_LOSS + 1, 0:LANES], (SUBLANES, LANES))
        for k in range(N_MOD):
            g = tot[ROW_DMOD + k:ROW_DMOD + k + 1, :]
            if k < 2:
                g = g + tot[ROW_DMODC + k:ROW_DMODC + k + 1, :]
            update("b_ada", g, slice(D_MODEL * k, D_MODEL * (k + 1)))
        update("norm1_g", tot[ROW_N1:ROW_N1 + 1, :])
        update("norm2_g", tot[ROW_N2:ROW_N2 + 1, :])
        update("final_g", tot[ROW_FG:ROW_FG + 1, :])
        update("ret_decay", tot[ROW_RD:ROW_RD + SUBLANES, 0:LANES])
        update("conv_b", tot[ROW_CB:ROW_CB + 1, 0:LRU_W])
        update("conv_w", pick(lambda j: tot[ROW_CW:ROW_CW + 4, LANES * j:LANES * (j + 1)]))
        for nm, row in (("lru_ba", ROW_BA), ("lru_bx", ROW_BX), ("lru_lambda", ROW_LAM)):
            update(nm, pick(lambda j, row=row: tot[row:row + 2, LANES * j:LANES * (j + 1)]))
        for nm, g_all in (("lru_wa", ga_ref), ("lru_wx", gx_ref)):
            for dr in range(2):
                lanes = slice(LRU_BD * dr, LRU_BD * (dr + 1))
                g = g_all[0, :, lanes].astype(F32)
                for d in range(1, N_DEV):
                    g = g + g_all[d, :, lanes].astype(F32)
                update(nm, g, rows=slice(LRU_W * dr, LRU_W * (dr + 1)))

        def seg(rows6, s):
            return rows6[s // 2][:, SEG * (s % 2):SEG * (s % 2 + 1)]

        b128_ref[...] = jnp.zeros_like(b128_ref)
        dmc_ref[...] = jnp.zeros_like(dmc_ref)
        zero = jnp.zeros((1, D_MODEL), F32)
        ctx6 = [tot[ROW_DMODC:ROW_DMODC + 1, :], tot[ROW_DMODC + 1:ROW_DMODC + 2, :]] + [zero] * (N_MOD - 2)
        for q in range(ada_n // SEG):
            cols = slice(SEG * q, SEG * (q + 1))
            for d in range(N_DEV):
                rows6 = [slab_ref[d, ROW_DMOD + k:ROW_DMOD + k + 1, :] for k in range(N_MOD)]
                b128_ref[d:d + 1, cols] = pick(lambda j, rows6=rows6: seg(rows6, 3 * j + q))
            c = pick(lambda j: seg(ctx6, 3 * j + q))
            b128_ref[N_DEV:N_DEV + 1, cols] = c
            dmc_ref[0:1, cols] = c

    out_shape = []
    for nm in SMALL_PARAMS:
        out_shape += [_sds(wmv[nm][0].shape)] * 4
    out_shape += [_sds((LANES, ada_n)), _sds((SUBLANES, ada_n)), _sds((SUBLANES, LANES))]
    args = [slab_all, ga_all, gx_all] + flat
    grid_spec = pltpu.PrefetchScalarGridSpec(
        num_scalar_prefetch=1, grid=(1,), in_specs=[_full(a.shape) for a in args],
        out_specs=[_full(s.shape) for s in out_shape])
    outs = _pc(body, name="finalize_small", grid_spec=grid_spec, out_shape=out_shape,
               compiler_params=_params("arbitrary"))(chip_idx, *args)
    res = {nm: tuple(outs[4 * k:4 * k + 4]) for k, nm in enumerate(SMALL_PARAMS)}
    return res, outs[4 * n_p], outs[4 * n_p + 1], outs[4 * n_p + 2]


def _block_diag(w):
    eye = jnp.eye(LRU_BLOCKS, dtype=F32)
    return (w[:, :, None, :] * eye[:, None, :, None]).reshape(LRU_W, LRU_W).astype(BF16)


def _lane_rep(v8):
    return jnp.broadcast_to(v8.reshape(SUBLANES, 1), (SUBLANES, LANES))


def kernel(x, c, ctx, c_ctx, w_ada, b_ada, norm1_g, norm2_g, w_in, ret_decay, conv_w, conv_b, lru_wa, lru_ba, lru_wx, lru_bx, lru_lambda, w_out, w_mlp1, w_mlp2, final_g, loss_target, m_c_ctx, m_w_ada, m_b_ada, m_norm1_g, m_norm2_g, m_w_in, m_ret_decay, m_conv_w, m_conv_b, m_lru_wa, m_lru_ba, m_lru_wx, m_lru_bx, m_lru_lambda, m_w_out, m_w_mlp1, m_w_mlp2, m_final_g, v_c_ctx, v_w_ada, v_b_ada, v_norm1_g, v_norm2_g, v_w_in, v_ret_decay, v_conv_w, v_conv_b, v_lru_wa, v_lru_ba, v_lru_wx, v_lru_bx, v_lru_lambda, v_w_out, v_w_mlp1, v_w_mlp2, v_final_g):
    ax, ay, ac = lax.axis_index("x"), lax.axis_index("y"), lax.axis_index("c")
    chip = 2 * ax + ay
    dev = 4 * ax + 2 * ay + ac
    c_idx = jnp.stack([ac, chip]).astype(jnp.int32)
    j_idx = chip.reshape(1).astype(jnp.int32)

    xt = x[0]
    t_len = xt.shape[0]
    ctxt = ctx[0]
    l_len = ctxt.shape[0]
    tgt = loss_target[0]
    ada_n = w_ada.shape[2]

    def my_half(w2d):
        r = w2d.shape[0] // 2
        return lax.dynamic_slice_in_dim(w2d, ac * r, r, axis=0).astype(BF16)

    pad8 = lambda a: jnp.pad(a, ((0, SUBLANES - a.shape[0]), (0, 0)))
    small = jnp.concatenate([pad8(conv_w[0]), pad8(lru_ba[0]), pad8(lru_bx[0]), pad8(lru_lambda[0])], axis=0)
    b_shard = lax.dynamic_slice_in_dim(b_ada, chip * ada_n, ada_n, axis=1)
    gw_in, _, small_all, a16, mod_parts, lgv, sgv = _head(
        my_half(w_in[0]), pad8(c), small, w_ada[0], b_shard, c_ctx, ret_decay[0])
    w4 = gw_in.reshape(N_CHIP, D_MODEL, IN_COLS // N_CHIP)

    mod_all = mod_parts[0::2].transpose(1, 0, 2).reshape(16, N_CHIP * ada_n)
    mod_me = lax.dynamic_slice_in_dim(mod_all, dev, 1, axis=0)
    sh1, sc1, g1, sh2, sc2, g2 = [mod_me[:, D_MODEL * k:D_MODEL * (k + 1)] for k in range(N_MOD)]
    csh1, csc1 = mod_all[8:9, 0:D_MODEL], mod_all[8:9, D_MODEL:2 * D_MODEL]

    cos2, sin2 = _rotary_tables(t_len)
    cos_c, sin_c = jnp.ones((l_len, DH), F32), jnp.zeros((l_len, DH), F32)
    n1g, n2g = norm1_g, norm2_g
    fg = final_g.reshape(1, D_MODEL)

    small_full = small_all[0::2].transpose(1, 0, 2).reshape(4 * SUBLANES, LRU_W)
    cw = small_full[0:4]
    cb = conv_b
    ba_f, ba_b = small_full[8:9], small_full[9:10]
    bx_f, bx_b = small_full[16:17], small_full[17:18]
    lam_f, lam_b = small_full[24:25], small_full[25:26]
    wa_f, wa_b = _block_diag(lru_wa[0, 0]), _block_diag(lru_wa[0, 1])
    wx_f, wx_b = _block_diag(lru_wx[0, 0]), _block_diag(lru_wx[0, 1])
    zero_h = jnp.zeros((1, LRU_W), F32)

    projc, xrc, hcb16 = _inproj_fwd(ctxt, n1g, csh1, csc1, w4, cos_c, sin_c, "inproj_fwd_ctx")
    s_f, s_b = _ctx_state_fwd(projc, lgv)
    xcc = _conv_fwd(xrc, cw, cb, "conv_fwd_ctx")
    hcf = _lru_fwd(xcc, wa_f, wx_f, ba_f, bx_f, lam_f, zero_h, False, "lru_fwd_ctx_f")
    hcbk = _lru_fwd(xcc, wa_b, wx_b, ba_b, bx_b, lam_b, zero_h, True, "lru_fwd_ctx_b")
    lru_sf, lru_sb = hcf[l_len - 1:l_len], hcbk[0:1]

    h1, h2 = my_half(w_mlp1[0]), my_half(w_mlp2[0])
    q = h1.shape[0] // 2
    (proj, xrl, hb16), ((gw_1a,),) = _inproj_fwd(xt, n1g, sh1, sc1, w4, cos2, sin2, "inproj_fwd",
                                           comms=(_AllGather([h1[:q]]),))
    (o_f, o_b, spf, spb), ((gw_1b, gw_out),) = _ret_fwd(proj, lgv, s_f, s_b,
                                                       comms=(_AllGather([h1[q:], my_half(w_out[0])]),))
    xcl = _conv_fwd(xrl, cw, cb, "conv_fwd")
    hf, ((gw_2a,),) = _lru_fwd(xcl, wa_f, wx_f, ba_f, bx_f, lam_f, lru_sf, False, "lru_fwd_f",
                              comms=(_AllGather([h2[:q]]),))
    hbk, ((gw_2b,),) = _lru_fwd(xcl, wa_b, wx_b, ba_b, bx_b, lam_b, lru_sb, True, "lru_fwd_b",
                               comms=(_AllGather([h2[q:]]),))
    wo = gw_out.reshape(D_MODEL, D_MODEL)
    x1, cat = _mix_fwd(o_f, o_b, proj, hf, hbk, wo, xt, g1)

    (dx1, h2b, ab, dub, dmb, dsc2, dsh2, dg2, dn2g, dfg, lossv) = _mlp(
        x1, n2g, sh2, sc2, g2, fg, (gw_1a, gw_1b), (gw_2a, gw_2b), tgt)
    gw_mlp1 = _tn(h2b, dub, N_CHIP, False, True, "grad_w_mlp1")
    b_1 = gw_mlp1.reshape(N_DEV, D_MODEL // 2, MLP_H // N_CHIP)
    gw_mlp2, ((r_1,),) = _tn(ab, dmb, N_CHIP, True, False, "grad_w_mlp2", comms=(_pair_exchange([b_1]),))

    half = D_MODEL // 4
    top, bot = (0, half), (half, half)
    b_2 = gw_mlp2.reshape(N_DEV, MLP_H // N_DEV, D_MODEL)
    p_1, pb_1 = _pair_add(b_1, r_1, c_idx, "rs_pair_add_w_mlp1")
    (do, dhs, dg, dgate, dyb, dg1), ((q_1a,), (r_2,)) = _mix_bwd(
        o_f, o_b, proj, hf, hbk, wo, cat, dx1, g1, comms=(_chip_exchange([pb_1], top), _pair_exchange([b_2])))
    gw_o = _tn(cat, dyb, 1, False, False, "grad_w_out")
    b_o = gw_o.reshape(N_DEV, D_MODEL // N_DEV, D_MODEL)
    p_2, pb_2 = _pair_add(b_2, r_2, c_idx, "rs_pair_add_w_mlp2")

    (dq_f, dk_f, dv_f, ds_f, drd_f), ((q_1b,), (r_o,)) = _ret_bwd(
        proj, lgv, sgv, spf, do, False, "ret_bwd_f", comms=(_chip_exchange([pb_1], bot), _pair_exchange([b_o])))
    p_o, pb_o = _pair_add(b_o, r_o, c_idx, "rs_pair_add_w_out")
    h_1 = _chip_add(p_1, (q_1a, q_1b), c_idx, "rs_chip_add_w_mlp1")

    (dq_b, dk_b, dv_b, ds_b, drd_b), ((q_2a,), (f_1,)) = _ret_bwd(
        proj, lgv, sgv, spb, do, True, "ret_bwd_b", comms=(_chip_exchange([pb_2], top), _pair_gather([h_1])))

    (dxc_f, dpre_f, dba_f, dbx_f, dlam_f, dh0_f), ((q_2b,),) = _lru_bwd(
        xcl, wa_f, wx_f, ba_f, bx_f, lam_f, hf, lru_sf, dhs, False, "lru_bwd_f",
        comms=(_chip_exchange([pb_2], bot),))
    h_2 = _chip_add(p_2, (q_2a, q_2b), c_idx, "rs_chip_add_w_mlp2")
    (dxc_b, dpre_b, dba_b, dbx_b, dlam_b, dh0_b), ((q_o,), (f_2,)) = _lru_bwd(
        xcl, wa_b, wx_b, ba_b, bx_b, lam_b, hbk, lru_sb, dhs, True, "lru_bwd_b",
        comms=(_chip_exchange([pb_o]), _pair_gather([h_2])))
    h_o = _chip_add(p_o, (q_o,), c_idx, "rs_chip_add_w_out")
    dxr, dcw, dcb = _conv_bwd(dxc_f, dxc_b, xrl, cw, "conv_bwd")
    grad_x, dpb, dn1g, dsh1, dsc1 = _inproj_bwd(
        xt, n1g, sh1, sc1, w4, cos2, sin2, [dq_f, dq_b, dk_f, dk_b, dv_f, dv_b, dg, dxr, dgate], dx1, "inproj_bwd")

    dkc, dvc, drd_c = _ctx_state_bwd(projc, lgv, sgv, ds_f, ds_b)
    zc = jnp.zeros((l_len, LRU_W), F32)
    dhc_f = lax.dynamic_update_slice(zc, dh0_f, (l_len - 1, 0))
    dhc_b = lax.dynamic_update_slice(zc, dh0_b, (0, 0))
    (dxcc_f, dprec_f, dbac_f, dbxc_f, dlamc_f, _), _ = _lru_bwd(
        xcc, wa_f, wx_f, ba_f, bx_f, lam_f, hcf, zero_h, dhc_f, False, "lru_bwd_ctx_f")
    (dxcc_b, dprec_b, dbac_b, dbxc_b, dlamc_b, _), _ = _lru_bwd(
        xcc, wa_b, wx_b, ba_b, bx_b, lam_b, hcbk, zero_h, dhc_b, True, "lru_bwd_ctx_b")
    dxrc, dcw_c, dcb_c = _conv_bwd(dxcc_f, dxcc_b, xrc, cw, "conv_bwd_ctx")
    zr = jnp.zeros((l_len, RET_W), BF16)
    _, dpbc, dn1g_c, dcsh1, dcsc1 = _inproj_bwd(
        ctxt, n1g, csh1, csc1, w4, cos_c, sin_c, [zr, zr, dkc, zr, dvc, zr, zr, dxrc, zr],
        jnp.zeros((l_len, D_MODEL), F32), "inproj_bwd_ctx")

    gw_i = _tn(hb16, dpb, N_CHIP, False, True, "grad_w_in", extra=(hcb16, dpbc))
    b_i = gw_i.reshape(N_DEV, D_MODEL // 2, IN_COLS // N_CHIP)
    gwa_f, ((r_i,), (f_o,)) = _tn(xcl, dpre_f, 2, False, True, "grad_lru_gates_f", extra=(xcc, dprec_f),
                                  comms=(_pair_exchange([b_i]), _pair_gather([h_o])))
    p_i, pb_i = _pair_add(b_i, r_i, c_idx, "rs_pair_add_w_in")
    gwa_b, ((q_i,),) = _tn(xcl, dpre_b, 2, False, True, "grad_lru_gates_b", extra=(xcc, dprec_b),
                           comms=(_chip_exchange([pb_i]),))
    slab, ga, gx = _pack_small(
        [lossv, dsh1, dsc1, dg1, dsh2, dsc2, dg2, dcsh1, dcsc1, dn1g, dn1g_c, dn2g, dfg],
        (drd_f, drd_b, drd_c), (dcw, dcw_c), (dcb, dcb_c),
        (dba_f, dbac_f, dba_b, dbac_b, dbx_f, dbxc_f, dbx_b, dbxc_b, dlam_f, dlamc_f, dlam_b, dlamc_b),
        (gwa_f, gwa_b))
    (f_i,), (slab_all, ga_all, gx_all) = _run_comms(
        [_pair_gather([_chip_add(p_i, (q_i,), c_idx, "rs_chip_add_w_in")]), _AllGather([slab, ga, gx])],
        "tail_exchanges")
    g_in, g_out, g_1, g_2 = _shard_of(f_i), _shard_of(f_o), _shard_of(f_1), _shard_of(f_2)
    big = {}
    for nm, w, g, m, v in (("w_in", w_in, g_in, m_w_in, v_w_in), ("w_out", w_out, g_out, m_w_out, v_w_out),
                           ("w_mlp1", w_mlp1, g_1, m_w_mlp1, v_w_mlp1), ("w_mlp2", w_mlp2, g_2, m_w_mlp2, v_w_mlp2)):
        go, d_, mn, vn = _adamw(w[0], g, m[0], v[0], "adamw_" + nm)
        big[nm] = (go[None], d_[None], mn[None], vn[None])
    params = {
        "b_ada": (b_ada, m_b_ada, v_b_ada), "norm1_g": (norm1_g, m_norm1_g, v_norm1_g),
        "norm2_g": (norm2_g, m_norm2_g, v_norm2_g), "final_g": (final_g, m_final_g, v_final_g),
        "ret_decay": (ret_decay, m_ret_decay, v_ret_decay), "conv_w": (conv_w, m_conv_w, v_conv_w),
        "conv_b": (conv_b, m_conv_b, v_conv_b), "lru_wa": (lru_wa, m_lru_wa, v_lru_wa),
        "lru_ba": (lru_ba, m_lru_ba, v_lru_ba), "lru_wx": (lru_wx, m_lru_wx, v_lru_wx),
        "lru_bx": (lru_bx, m_lru_bx, v_lru_bx), "lru_lambda": (lru_lambda, m_lru_lambda, v_lru_lambda),
    }
    as2d = {
        "b_ada": lambda a: a, "norm1_g": lambda a: a, "norm2_g": lambda a: a, "conv_b": lambda a: a,
        "final_g": lambda a: a.reshape(1, D_MODEL), "ret_decay": lambda a: _lane_rep(a.reshape(-1)),
        "conv_w": lambda a: a[0], "lru_ba": lambda a: a[0], "lru_bx": lambda a: a[0], "lru_lambda": lambda a: a[0],
        "lru_wa": lambda a: a.reshape(2 * LRU_W, LRU_BD), "lru_wx": lambda a: a.reshape(2 * LRU_W, LRU_BD),
    }
    res, b128, dmc8, loss8 = _finalize_small(
        j_idx, slab_all, ga_all, gx_all, {nm: tuple(as2d[nm](a) for a in params[nm]) for nm in SMALL_PARAMS})
    loss = loss8[0, 0]
    small_out = {}
    for nm in SMALL_PARAMS:
        shp = params[nm][0].shape
        if nm == "ret_decay":
            small_out[nm] = tuple(o[:, 0].reshape(shp) for o in res[nm])
        else:
            small_out[nm] = tuple(o.reshape(shp) for o in res[nm])

    g_ada = _ada_grad(jnp.pad(a16.T, ((0, 0), (0, LANES - 16))), b128)
    g_ada, d_ada, m_ada, v_ada = _adamw(w_ada[0], g_ada, m_w_ada[0], v_w_ada[0], "adamw_w_ada")

    (cparts,) = _all_gather([_cctx_partial(dmc8, w_ada[0])], "gather_cctx")
    g_cc, d_cc, m_cc, v_cc = _cctx_final(cparts, c_ctx, m_c_ctx, v_c_ctx)
    small_out["c_ctx"] = tuple(a.reshape(D_MODEL) for a in (g_cc, d_cc, m_cc, v_cc))
    small_out["w_ada"] = (g_ada[None], d_ada[None], m_ada[None], v_ada[None])
    small_out.update(big)

    order = ["c_ctx", "w_ada", "b_ada", "norm1_g", "norm2_g", "w_in", "ret_decay", "conv_w", "conv_b", "lru_wa", "lru_ba",
             "lru_wx", "lru_bx", "lru_lambda", "w_out", "w_mlp1", "w_mlp2", "final_g"]
    outs = [loss, grad_x[None]]
    for k in range(4):
        outs += [small_out[nm][k] for nm in order]
    return tuple(outs)
```

```python
import math

import jax
import jax.numpy as jnp
from jax import lax
from jax.experimental import pallas as pl
from jax.experimental.pallas import tpu as pltpu

F32 = jnp.float32
BF16 = jnp.bfloat16

D_MODEL = 1024
HEADS = 4
DH = 128
CHUNK = 256
RET_W = HEADS * DH
LRU_W = 512
LRU_BLOCKS = 8
LRU_BD = LRU_W // LRU_BLOCKS
LRU_C = 8.0
IN_COLS = 4 * RET_W + 2 * LRU_W
MLP_H = 4 * D_MODEL
N_MOD = 6
GRID_W = 64
ROPE_BASE = 10000.0
K_SCALE = DH ** -0.5
EPS = 1e-6
GELU_K = math.sqrt(2.0 / math.pi)
GELU_C = 0.044715

ADAM_LR = 0.001
ADAM_B1 = 0.9
ADAM_B2 = 0.999
ADAM_EPS = 1e-08
ADAM_WD = 0.01
ADAM_STEP = 10

N_DEV = 8
N_CHIP = 4
SUBLANES = 8
LANES = 128
VMEM_LIMIT_V7X = 56 * 1024 * 1024
MESH = pl.DeviceIdType.MESH
ANY = pl.BlockSpec(memory_space=pl.ANY)


def _pc(body, **kw):
    return pl.pallas_call(body, **kw)


def _params(*sem):
    return pltpu.CompilerParams(dimension_semantics=sem if sem else None, vmem_limit_bytes=VMEM_LIMIT_V7X)


def _tile(t, big=False):
    if big and t >= 1024:
        return 512
    return 256 if t >= 256 else t


def _sds(shape, dtype=F32):
    return jax.ShapeDtypeStruct(tuple(shape), dtype)


def _full(shape):
    nd = len(shape)
    return pl.BlockSpec(tuple(shape), lambda *_: (0,) * nd)


def _sigmoid(x):
    return 1.0 / (1.0 + jnp.exp(-x))


def _log1p_pos(y):
    s = y * (1.0 - y * (0.5 - y * (1.0 / 3.0 - y * (0.25 - y * (0.2 - y / 6.0)))))
    return jnp.where(y < 0.03, s, jnp.log(1.0 + y))


def _softplus(z):
    return jnp.maximum(z, 0.0) + _log1p_pos(jnp.exp(-jnp.abs(z)))


def _one_minus_sq(la, a):
    t = la * (1.0 + la * (0.5 + la * (1.0 / 6.0 + la * (1.0 / 24.0 + la * (1.0 / 120.0)))))
    return jnp.where(la > -0.125, -t, 1.0 - a) * (1.0 + a)


def _rms(x):
    r = lax.rsqrt(jnp.mean(x * x, axis=-1, keepdims=True) + EPS)
    return x * r, r


def _dot(a, b):
    return jnp.dot(a, b, preferred_element_type=F32)


def _dot_nt(a, b):
    return lax.dot_general(a, b, (((1,), (1,)), ((), ())), preferred_element_type=F32)


def _dot_tn(a, b):
    return lax.dot_general(a, b, (((0,), (0,)), ((), ())), preferred_element_type=F32)


def _sum0(x):
    return jnp.sum(x, axis=0, keepdims=True)


def _norm_mod_bwd(x, g, sc, dh):
    xh, r = _rms(x)
    hn = xh * g
    dhn = dh * (1.0 + sc)
    dxh = dhn * g
    dx = r * (dxh - xh * jnp.mean(dxh * xh, axis=-1, keepdims=True))
    return dx, _sum0(dhn * xh), _sum0(dh), _sum0(dh * hn)


def _dev_index(p):
    return 4 * p[0] + 2 * p[1] + p[2]


def _mesh_pos():
    return lax.axis_index("x"), lax.axis_index("y"), lax.axis_index("c")


class _AllGather:
    def __init__(self, arrs):
        n = len(arrs)
        self.arrays = list(arrs)
        self.out_shapes = [_sds((N_DEV,) + a.shape, a.dtype) for a in arrs]
        self.scratch = ([pltpu.VMEM(a.shape, a.dtype) for a in arrs]
                        + [pltpu.SemaphoreType.DMA((7 * n,)), pltpu.SemaphoreType.DMA((7 * n,)),
                           pltpu.SemaphoreType.DMA((n,))])
        self.aliases = {}

    def _parts(self, ins, outs, scr):
        n = len(self.arrays)
        stage = scr[:n]
        send_sems, recv_sems, local_sems = scr[n:]
        x, y, c = _mesh_pos()
        me, sib = (x, y, c), (x, y, 1 - c)
        chips = [(1 - x, y), (x, 1 - y), (1 - x, 1 - y)]

        def copy(t, k, block, to, own=False):
            dst = outs[t].at[_dev_index(block)]
            return pltpu.make_async_remote_copy(
                src_ref=ins[t] if own else dst, dst_ref=dst,
                send_sem=send_sems.at[7 * t + k], recv_sem=recv_sems.at[7 * t + k],
                device_id=to, device_id_type=MESH)

        first = []
        for t in range(n):
            first.append(copy(t, 0, me, sib, own=True))
            for j, ch in enumerate(chips):
                first.append(copy(t, 1 + j, me, (*ch, c), own=True))
        stage_in = [pltpu.make_async_copy(ins[t], stage[t], local_sems.at[t]) for t in range(n)]
        mine = [pltpu.make_async_copy(stage[t], outs[t].at[_dev_index(me)], local_sems.at[t]) for t in range(n)]
        return n, c, me, sib, chips, copy, first, stage_in, mine

    def start(self, ins, outs, scr):
        n, _, _, _, _, _, first, stage_in, mine = self._parts(ins, outs, scr)
        for cp in stage_in:
            cp.start()
        for cp in first:
            cp.start()
        for t in range(n):
            stage_in[t].wait()
            mine[t].start()

    def relay(self, ins, outs, scr):
        n, c, me, sib, chips, copy, _, _, _ = self._parts(ins, outs, scr)
        for j, ch in enumerate(chips):
            for t in range(n):
                copy(t, 1 + j, (*ch, c), me).wait_recv()
                copy(t, 4 + j, (*ch, c), sib).start()

    def finish(self, ins, outs, scr):
        n, c, me, sib, chips, copy, first, _, mine = self._parts(ins, outs, scr)
        passed = [copy(t, 4 + j, (*ch, c), sib) for j, ch in enumerate(chips) for t in range(n)]
        for t in range(n):
            copy(t, 0, sib, me).wait_recv()
            for j, ch in enumerate(chips):
                copy(t, 4 + j, (*ch, 1 - c), me).wait_recv()
        for cp in first + passed:
            cp.wait_send()
        for cp in mine:
            cp.wait()


class _Exchange:
    def __init__(self, arrays, out_shapes, plan, n_copies, aliases=None):
        self.arrays = list(arrays)
        self.out_shapes = list(out_shapes)
        self.plan = plan
        self.scratch = [pltpu.SemaphoreType.DMA((n_copies,)), pltpu.SemaphoreType.DMA((n_copies,))]
        self.aliases = aliases or {}

    def _copies(self, ins, outs, scr):
        send_sems, recv_sems = scr
        snd, rcv = [], []
        for i, (src, dst, peer, lands) in enumerate(self.plan(ins, outs, _mesh_pos())):
            kw = dict(send_sem=send_sems.at[i], recv_sem=recv_sems.at[i], device_id=peer, device_id_type=MESH)
            snd.append(pltpu.make_async_remote_copy(src_ref=src, dst_ref=dst, **kw))
            rcv.append(pltpu.make_async_remote_copy(src_ref=src, dst_ref=lands, **kw))
        return snd, rcv

    def start(self, ins, outs, scr):
        for cp in self._copies(ins, outs, scr)[0]:
            cp.start()

    def relay(self, ins, outs, scr):
        pass

    def finish(self, ins, outs, scr):
        snd, rcv = self._copies(ins, outs, scr)
        for cp in rcv:
            cp.wait_recv()
        for cp in snd:
            cp.wait_send()


def _pair_exchange(grads):
    n = len(grads)

    def plan(ins, outs, pos):
        x, y, c = pos
        return [(ins[t].at[2 * j + (1 - c)], outs[t].at[j], (x, y, 1 - c), outs[t].at[j])
                for t in range(n) for j in range(N_CHIP)]

    return _Exchange(grads, [_sds((N_CHIP,) + g.shape[1:], g.dtype) for g in grads], plan, N_CHIP * n)


def _chip_exchange(parts, rows=None):
    n = len(parts)

    def plan(ins, outs, pos):
        x, y, c = pos
        chips = [(1 - x, y), (x, 1 - y), (1 - x, 1 - y)]

        def src(t, ch):
            blk = ins[t].at[2 * ch[0] + ch[1]]
            return blk if rows is None else blk.at[pl.ds(rows[0], rows[1])]

        return [(src(t, ch), outs[t].at[k], (*ch, c), outs[t].at[k]) for t in range(n) for k, ch in enumerate(chips)]

    shapes = [_sds((3, p.shape[1] if rows is None else rows[1]) + p.shape[2:], p.dtype) for p in parts]
    return _Exchange(parts, shapes, plan, 3 * n)


def _pair_gather(bufs):
    n = len(bufs)

    def plan(ins, outs, pos):
        x, y, c = pos
        return [(ins[t].at[c], outs[t].at[c], (x, y, 1 - c), outs[t].at[1 - c]) for t in range(n)]

    return _Exchange(bufs, [_sds(b.shape, b.dtype) for b in bufs], plan, n, aliases={t: t for t in range(n)})


def _run_comms(comms, name):
    c_in = [len(cm.arrays) for cm in comms]
    c_out = [len(cm.out_shapes) for cm in comms]
    c_scr = [len(cm.scratch) for cm in comms]
    aliases = {}
    for k, cm in enumerate(comms):
        for a, b in cm.aliases.items():
            aliases[sum(c_in[:k]) + a] = sum(c_out[:k]) + b

    def split(refs, counts):
        out, pos = [], 0
        for cnt in counts:
            out.append(refs[pos:pos + cnt])
            pos += cnt
        return out

    def body(*refs):
        ins = split(refs[:sum(c_in)], c_in)
        outs = split(refs[sum(c_in):sum(c_in) + sum(c_out)], c_out)
        scr = split(refs[sum(c_in) + sum(c_out):], c_scr)
        for phase in ("start", "relay", "finish"):
            for k, cm in enumerate(comms):
                getattr(cm, phase)(ins[k], outs[k], scr[k])

    outs = _pc(body, name=name, out_shape=[s for cm in comms for s in cm.out_shapes],
               in_specs=[ANY] * sum(c_in), out_specs=[ANY] * sum(c_out), input_output_aliases=aliases,
               scratch_shapes=[s for cm in comms for s in cm.scratch],
               compiler_params=_params())(*[a for cm in comms for a in cm.arrays])
    return split(list(outs), c_out)


def _all_gather(arrs, name):
    return _run_comms([_AllGather(arrs)], name)[0]


def _call(body, *, name, grid, in_specs, out_specs, out_shape, scratch_shapes, sem, args, comms=()):
    n_in, n_out, n_scr = len(in_specs), len(out_specs), len(scratch_shapes)
    c_in = [len(cm.arrays) for cm in comms]
    c_out = [len(cm.out_shapes) for cm in comms]
    c_scr = [len(cm.scratch) for cm in comms]
    aliases = {}
    for k, cm in enumerate(comms):
        for a, b in cm.aliases.items():
            aliases[n_in + sum(c_in[:k]) + a] = n_out + sum(c_out[:k]) + b

    def split(refs, counts):
        out, pos = [], 0
        for cnt in counts:
            out.append(refs[pos:pos + cnt])
            pos += cnt
        return out

    def wrapped(*refs):
        ins = refs[:n_in + sum(c_in)]
        outs = refs[len(ins):len(ins) + n_out + sum(c_out)]
        scr = refs[len(ins) + len(outs):]
        cins, couts, cscr = split(ins[n_in:], c_in), split(outs[n_out:], c_out), split(scr[n_scr:], c_scr)
        if comms:
            first = pl.program_id(0) == 0
            last = pl.program_id(0) == grid[0] - 1
            for k in range(1, len(grid)):
                first = jnp.logical_and(first, pl.program_id(k) == 0)
                last = jnp.logical_and(last, pl.program_id(k) == grid[k] - 1)

            @pl.when(first)
            def _():
                for k, cm in enumerate(comms):
                    cm.start(cins[k], couts[k], cscr[k])
        body(*ins[:n_in], *outs[:n_out], *scr[:n_scr])
        if comms:
            relay_early = len(grid) == 1 and grid[0] >= 4
            if relay_early:
                @pl.when(pl.program_id(0) == (7 * grid[0]) // 8 - 1)
                def _():
                    for k, cm in enumerate(comms):
                        cm.relay(cins[k], couts[k], cscr[k])

            @pl.when(last)
            def _():
                for k, cm in enumerate(comms):
                    if not relay_early:
                        cm.relay(cins[k], couts[k], cscr[k])
                    cm.finish(cins[k], couts[k], cscr[k])

    outs = _pc(wrapped, name=name, grid=grid,
               in_specs=list(in_specs) + [ANY] * sum(c_in), out_specs=list(out_specs) + [ANY] * sum(c_out),
               out_shape=list(out_shape) + [s for cm in comms for s in cm.out_shapes],
               scratch_shapes=list(scratch_shapes) + [s for cm in comms for s in cm.scratch],
               input_output_aliases=aliases, compiler_params=_params(*sem),
               )(*args, *[a for cm in comms for a in cm.arrays])
    outs = list(outs)
    return outs[:n_out], split(outs[n_out:], c_out)


def _row_block(r):
    for b in (512, 256, 128, 64, 32, 16, 8):
        if r % b == 0:
            return b
    return r


def _pair_add(g, recv, cj_idx, name):
    _, r, cc = g.shape
    br = _row_block(r)

    def body(cj_ref, g_ref, r_ref, own_ref, pb_ref):
        s = g_ref[...] + r_ref[...]
        pb_ref[...] = s.astype(BF16)

        @pl.when(pl.program_id(1) == cj_ref[1])
        def _():
            own_ref[...] = s[0]

    grid_spec = pltpu.PrefetchScalarGridSpec(
        num_scalar_prefetch=1, grid=(r // br, N_CHIP),
        in_specs=[pl.BlockSpec((1, br, cc), lambda i, j, cj_ref: (2 * j + cj_ref[0], i, 0)),
                  pl.BlockSpec((1, br, cc), lambda i, j, cj_ref: (j, i, 0))],
        out_specs=[pl.BlockSpec((br, cc), lambda i, j, cj_ref: (i, 0)),
                   pl.BlockSpec((1, br, cc), lambda i, j, cj_ref: (j, i, 0))])
    return _pc(body, name=name, grid_spec=grid_spec,
               out_shape=[_sds((r, cc)), _sds((N_CHIP, r, cc), BF16)],
               compiler_params=_params("arbitrary", "arbitrary"))(cj_idx, g, recv)


def _chip_add(p, qs, cj_idx, name):
    r, cc = p.shape
    nq = len(qs)
    br = _row_block(r // nq)
    nb = r // nq // br

    def body(cj_ref, p_ref, *refs):
        o_ref = refs[-1]
        if nq == 2:
            top = pl.program_id(0) < nb
            q = [jnp.where(top, refs[0][k], refs[1][k]).astype(F32) for k in range(3)]
        else:
            q = [refs[0][k].astype(F32) for k in range(3)]
        o_ref[0] = ((p_ref[...] + q[0]) + q[1]) + q[2]

    q_specs = [pl.BlockSpec((3, br, cc), lambda i, cj_ref, h=h: (0, jnp.clip(i - h * nb, 0, nb - 1), 0))
               for h in range(nq)]
    grid_spec = pltpu.PrefetchScalarGridSpec(
        num_scalar_prefetch=1, grid=(r // br,),
        in_specs=[pl.BlockSpec((br, cc), lambda i, cj_ref: (i, 0))] + q_specs,
        out_specs=pl.BlockSpec((1, br, cc), lambda i, cj_ref: (cj_ref[0], i, 0)))
    return _pc(body, name=name, grid_spec=grid_spec, out_shape=_sds((2, r, cc)),
               compiler_params=_params("arbitrary"))(cj_idx, p, *qs)


def _shard_of(both):
    return both.reshape((2 * both.shape[1],) + both.shape[2:])


def _adamw(w, g, m, v, name):
    r, cc = w.shape
    br = _row_block(r)
    if r * cc * 4 <= (1 << 20):
        br = r
    elif br * cc * 4 > (1 << 20) and br > 8:
        br = max(8, (1 << 20) // (cc * 4) // 8 * 8)
        while r % br:
            br -= 8
    c1 = 1.0 - ADAM_B1 ** ADAM_STEP
    c2 = 1.0 - ADAM_B2 ** ADAM_STEP

    def body(w_ref, g_ref, m_ref, v_ref, go_ref, d_ref, mo_ref, vo_ref):
        gg = g_ref[...]
        go_ref[...] = gg
        mn = ADAM_B1 * m_ref[...] + (1.0 - ADAM_B1) * gg
        vn = ADAM_B2 * v_ref[...] + (1.0 - ADAM_B2) * (gg * gg)
        mh = mn / c1
        vh = vn / c2
        d_ref[...] = -ADAM_LR * (mh / (jnp.sqrt(vh) + ADAM_EPS) + ADAM_WD * w_ref[...])
        mo_ref[...] = mn
        vo_ref[...] = vn

    spec = pl.BlockSpec((br, cc), lambda i: (i, 0))
    return _pc(body, name=name, grid=(r // br,), in_specs=[spec] * 4, out_specs=[spec] * 4,
               out_shape=[_sds((r, cc))] * 4, compiler_params=_params("arbitrary"))(w, g, m, v)


def _head(w_half, c8, small, w_ada, b_shard, c_ctx, ret_decay):
    ada_n = w_ada.shape[1]
    mod_sds = _sds((16, ada_n))
    ag_w, ag_c, ag_m = _AllGather([w_half]), _AllGather([c8, small]), _AllGather([mod_sds])
    n_w, n_c, n_m = len(ag_w.scratch), len(ag_c.scratch), len(ag_m.scratch)

    def body(w_ref, c_ref, s_ref, wada_ref, b_ref, cc_ref, rd_ref,
             gw_ref, call_ref, sall_ref, a_ref, modp_ref, mall_ref, lg_ref, sg_ref, *scr):
        scr_w, scr_c, scr_m = scr[:n_w], scr[n_w:n_w + n_c], scr[n_w + n_c:n_w + n_c + n_m]
        c_v, w_v, m_v, sems = scr[n_w + n_c + n_m:]
        ag_w.start((w_ref,), (gw_ref,), scr_w)
        ag_c.start((c_ref, s_ref), (call_ref, sall_ref), scr_c)
        load_w = pltpu.make_async_copy(wada_ref, w_v, sems.at[0])
        load_w.start()
        rd = rd_ref[...]
        lg_ref[...] = -_softplus(-rd)
        sg_ref[...] = _sigmoid(-rd)
        ag_c.relay((c_ref, s_ref), (call_ref, sall_ref), scr_c)
        ag_c.finish((c_ref, s_ref), (call_ref, sall_ref), scr_c)
        load_c = pltpu.make_async_copy(call_ref, c_v, sems.at[1])
        load_c.start()
        load_c.wait()
        a_ref[...] = jnp.zeros_like(a_ref)
        for d in range(N_DEV):
            cd = c_v[d, 0:1, :]
            a_ref[d:d + 1, :] = cd * _sigmoid(cd)
        cc = cc_ref[...]
        a_ref[N_DEV:N_DEV + 1, :] = cc * _sigmoid(cc)
        load_w.wait()
        m_v[...] = jnp.dot(a_ref[...], w_v[...], preferred_element_type=F32,
                           precision=lax.Precision.HIGHEST) + b_ref[...]
        put = pltpu.make_async_copy(m_v, modp_ref, sems.at[2])
        put.start()
        put.wait()
        ag_m.start((modp_ref,), (mall_ref,), scr_m)
        ag_m.relay((modp_ref,), (mall_ref,), scr_m)
        ag_m.finish((modp_ref,), (mall_ref,), scr_m)
        ag_w.relay((w_ref,), (gw_ref,), scr_w)
        ag_w.finish((w_ref,), (gw_ref,), scr_w)

    rd = jnp.broadcast_to(ret_decay.reshape(2, HEADS).T[:, :, None], (HEADS, 2, LANES))
    lane = _full((HEADS, 2, LANES))
    outs = _pc(
        body, name="head",
        in_specs=[ANY, ANY, ANY, ANY, _full((1, ada_n)), _full((1, D_MODEL)), lane],
        out_specs=[ANY, ANY, ANY, _full((16, D_MODEL)), ANY, ANY, lane, lane],
        out_shape=ag_w.out_shapes + ag_c.out_shapes + [_sds((16, D_MODEL)), mod_sds] + ag_m.out_shapes
        + [_sds((HEADS, 2, LANES))] * 2,
        scratch_shapes=ag_w.scratch + ag_c.scratch + ag_m.scratch
        + [pltpu.VMEM((N_DEV,) + c8.shape, F32), pltpu.VMEM(w_ada.shape, F32), pltpu.VMEM((16, ada_n), F32),
           pltpu.SemaphoreType.DMA((3,))],
        compiler_params=_params(),
    )(w_half, c8, small, w_ada, b_shard, c_ctx.reshape(1, D_MODEL), rd)
    gw, c_all, small_all, a16, _, mod_all, lgv, sgv = outs
    return gw, c_all, small_all, a16, mod_all, lgv, sgv


def _ada_grad(at, b):
    n = b.shape[1]
    bn = 512

    def body(a_ref, b_ref, o_ref):
        o_ref[...] = jnp.dot(a_ref[...], b_ref[...], preferred_element_type=F32, precision=lax.Precision.HIGHEST)

    return _pc(body, name="ada_grad", grid=(n // bn,),
               in_specs=[_full((D_MODEL, LANES)), pl.BlockSpec((LANES, bn), lambda i: (0, i))],
               out_specs=pl.BlockSpec((D_MODEL, bn), lambda i: (0, i)), out_shape=_sds((D_MODEL, n)),
               compiler_params=_params("arbitrary"))(at, b)


def _cctx_partial(dmc8, w_ada):
    n = w_ada.shape[1]
    bn = 512

    def body(d_ref, w_ref, o_ref):
        @pl.when(pl.program_id(0) == 0)
        def _():
            o_ref[...] = jnp.zeros_like(o_ref)
        o_ref[...] += lax.dot_general(d_ref[...], w_ref[...], (((1,), (1,)), ((), ())),
                                      preferred_element_type=F32, precision=lax.Precision.HIGHEST)

    return _pc(body, name="cctx_partial", grid=(n // bn,),
               in_specs=[pl.BlockSpec((8, bn), lambda i: (0, i)), pl.BlockSpec((D_MODEL, bn), lambda i: (0, i))],
               out_specs=_full((8, D_MODEL)), out_shape=_sds((8, D_MODEL)),
               compiler_params=_params("arbitrary"))(dmc8, w_ada)


def _cctx_final(parts, c_ctx, m, v):
    c1 = 1.0 - ADAM_B1 ** ADAM_STEP
    c2 = 1.0 - ADAM_B2 ** ADAM_STEP

    def body(p_ref, c_ref, m_ref, v_ref, g_ref, d_ref, mo_ref, vo_ref):
        s = ((p_ref[0, 0:1, :] + p_ref[2, 0:1, :]) + p_ref[4, 0:1, :]) + p_ref[6, 0:1, :]
        z = c_ref[...]
        sg = _sigmoid(z)
        gg = s * (sg * (1.0 + z * (1.0 - sg)))
        g_ref[...] = gg
        mn = ADAM_B1 * m_ref[...] + (1.0 - ADAM_B1) * gg
        vn = ADAM_B2 * v_ref[...] + (1.0 - ADAM_B2) * (gg * gg)
        d_ref[...] = -ADAM_LR * ((mn / c1) / (jnp.sqrt(vn / c2) + ADAM_EPS) + ADAM_WD * z)
        mo_ref[...] = mn
        vo_ref[...] = vn

    row = _full((1, D_MODEL))
    return _pc(body, name="cctx_final", out_shape=[_sds((1, D_MODEL))] * 4,
               in_specs=[_full(parts.shape), row, row, row], out_specs=[row] * 4,
               compiler_params=_params())(parts, c_ctx.reshape(1, D_MODEL), m.reshape(1, D_MODEL), v.reshape(1, D_MODEL))


def _rotary_tables(t_len):
    rows = t_len // GRID_W
    n_freq = DH // 4
    inv = ROPE_BASE ** (-jnp.arange(n_freq, dtype=F32) / n_freq)
    row_ang = jnp.arange(rows, dtype=F32)[:, None] * inv
    col_ang = jnp.arange(GRID_W, dtype=F32)[:, None] * inv

    def spread(fn):
        return jnp.concatenate([jnp.repeat(fn(row_ang), GRID_W, axis=0), jnp.tile(fn(col_ang), (rows, 1))], axis=-1)

    cos, sin = spread(jnp.cos), spread(jnp.sin)
    return jnp.concatenate([cos, cos], axis=-1), jnp.concatenate([-sin, sin], axis=-1)


def _inproj_fwd(x, gn, sh, sc, w4, cos2, sin2, name, comms=()):
    t = x.shape[0]
    tm = _tile(t, True)
    nc = IN_COLS // N_CHIP

    def body(x_ref, gn_ref, sh_ref, sc_ref, w_ref, c_ref, s_ref, p_ref, xr_ref, hb_ref, p_s):
        xh, _ = _rms(x_ref[...])
        h = xh * gn_ref[...] * (1.0 + sc_ref[...]) + sh_ref[...]
        hb = h.astype(BF16)
        hb_ref[...] = hb
        for j in range(N_CHIP):
            p_s[:, nc * j:nc * (j + 1)] = _dot(hb, w_ref[j])
        cc = c_ref[...]
        ss = s_ref[...]
        for hh in range(2 * HEADS):
            blk = p_s[:, DH * hh:DH * (hh + 1)]
            rot = blk * cc + pltpu.roll(blk, DH // 2, 1) * ss
            if hh >= HEADS:
                rot = rot * K_SCALE
            p_ref[:, DH * hh:DH * (hh + 1)] = rot.astype(BF16)
        p_ref[:, 2 * RET_W:] = p_s[:, 2 * RET_W:].astype(BF16)
        xr_ref[...] = p_s[:, 4 * RET_W:4 * RET_W + LRU_W]

    row = _full((1, D_MODEL))
    outs, couts = _call(
        body, name=name, grid=(t // tm,),
        in_specs=[pl.BlockSpec((tm, D_MODEL), lambda i: (i, 0)), row, row, row, _full(w4.shape),
                  pl.BlockSpec((tm, DH), lambda i: (i, 0)), pl.BlockSpec((tm, DH), lambda i: (i, 0))],
        out_specs=[pl.BlockSpec((tm, IN_COLS), lambda i: (i, 0)), pl.BlockSpec((tm, LRU_W), lambda i: (i, 0)),
                   pl.BlockSpec((tm, D_MODEL), lambda i: (i, 0))],
        out_shape=[_sds((t, IN_COLS), BF16), _sds((t, LRU_W)), _sds((t, D_MODEL), BF16)],
        scratch_shapes=[pltpu.VMEM((tm, IN_COLS), F32)], sem=("arbitrary",),
        args=(x, gn, sh, sc, w4, cos2, sin2), comms=comms)
    return (outs, couts) if comms else outs


def _inproj_bwd(x, gn, sh, sc, w4, cos2, sin2, pieces, dres, name):
    t = x.shape[0]
    tm = _tile(t)
    nc = IN_COLS // N_CHIP

    def body(x_ref, gn_ref, sh_ref, sc_ref, w_ref, c_ref, s_ref, dqf, dqb, dkf, dkb, dvf, dvb, dg, dxr, dgt, dres_ref,
             dx_ref, dpb_ref, dgn_ref, dsh_ref, dsc_ref):
        cc = c_ref[...]
        ss = s_ref[...]
        dq = dqf[...].astype(F32) + dqb[...].astype(F32)
        dk = dkf[...].astype(F32) + dkb[...].astype(F32)
        for hh in range(HEADS):
            sl = slice(DH * hh, DH * (hh + 1))
            b = dq[:, sl]
            dpb_ref[:, sl] = (b * cc + pltpu.roll(b * ss, DH // 2, 1)).astype(BF16)
            b = dk[:, sl]
            dpb_ref[:, RET_W + DH * hh:RET_W + DH * (hh + 1)] = (
                (b * cc + pltpu.roll(b * ss, DH // 2, 1)) * K_SCALE).astype(BF16)
        dpb_ref[:, 2 * RET_W:3 * RET_W] = (dvf[...].astype(F32) + dvb[...].astype(F32)).astype(BF16)
        dpb_ref[:, 3 * RET_W:4 * RET_W] = dg[...].astype(BF16)
        dpb_ref[:, 4 * RET_W:4 * RET_W + LRU_W] = dxr[...].astype(BF16)
        dpb_ref[:, 4 * RET_W + LRU_W:IN_COLS] = dgt[...].astype(BF16)
        dh = _dot_nt(dpb_ref[:, 0:nc], w_ref[0])
        for j in range(1, N_CHIP):
            dh = dh + _dot_nt(dpb_ref[:, nc * j:nc * (j + 1)], w_ref[j])
        dx, dgn_t, dsh_t, dsc_t = _norm_mod_bwd(x_ref[...], gn_ref[...], sc_ref[...], dh)
        dx_ref[...] = dres_ref[...] + dx

        @pl.when(pl.program_id(0) == 0)
        def _():
            dgn_ref[...] = jnp.zeros_like(dgn_ref)
            dsh_ref[...] = jnp.zeros_like(dsh_ref)
            dsc_ref[...] = jnp.zeros_like(dsc_ref)
        dgn_ref[...] += dgn_t
        dsh_ref[...] += dsh_t
        dsc_ref[...] += dsc_t

    row = _full((1, D_MODEL))
    pc = pl.BlockSpec((tm, RET_W), lambda i: (i, 0))
    big = pl.BlockSpec((tm, D_MODEL), lambda i: (i, 0))
    return _pc(body, name=name, grid=(t // tm,),
               in_specs=[big, row, row, row, _full(w4.shape),
                         pl.BlockSpec((tm, DH), lambda i: (i, 0)), pl.BlockSpec((tm, DH), lambda i: (i, 0))]
               + [pc] * 9 + [big],
               out_specs=[big, pl.BlockSpec((tm, IN_COLS), lambda i: (i, 0)), row, row, row],
               out_shape=[_sds((t, D_MODEL)), _sds((t, IN_COLS), BF16), _sds((1, D_MODEL)), _sds((1, D_MODEL)),
                          _sds((1, D_MODEL))],
               compiler_params=_params("arbitrary"))(x, gn, sh, sc, w4, cos2, sin2, *pieces, dres)


def _halo_specs(t, tm):
    n8 = tm // SUBLANES
    last8 = t // SUBLANES - 1
    prev = pl.BlockSpec((SUBLANES, LRU_W), lambda i: (jnp.maximum(i * n8 - 1, 0), 0))
    main = pl.BlockSpec((tm, LRU_W), lambda i: (i, 0))
    nxt = pl.BlockSpec((SUBLANES, LRU_W), lambda i: (jnp.minimum((i + 1) * n8, last8), 0))
    return prev, main, nxt


def _with_halo(prev_ref, main_ref, next_ref, i, nt):
    prev = jnp.where(i > 0, prev_ref[...], 0.0)
    nxt = jnp.where(i < nt - 1, next_ref[...], 0.0)
    return jnp.concatenate([prev, main_ref[...], nxt], axis=0)


def _conv_fwd(xr, cw, cb, name):
    t = xr.shape[0]
    tm = _tile(t, True)
    nt = t // tm
    n = tm + 2 * SUBLANES
    mid = slice(SUBLANES, SUBLANES + tm)

    def body(p_ref, m_ref, n_ref, w_ref, b_ref, o_ref):
        xp = _with_halo(p_ref, m_ref, n_ref, pl.program_id(0), nt)
        acc = b_ref[...] + pltpu.roll(xp, 1, 0)[mid] * w_ref[0:1, :]
        acc = acc + xp[mid] * w_ref[1:2, :]
        acc = acc + pltpu.roll(xp, n - 1, 0)[mid] * w_ref[2:3, :]
        acc = acc + pltpu.roll(xp, n - 2, 0)[mid] * w_ref[3:4, :]
        o_ref[...] = acc

    return _pc(body, name=name, grid=(nt,),
               in_specs=[*_halo_specs(t, tm), _full((4, LRU_W)), _full((1, LRU_W))],
               out_specs=pl.BlockSpec((tm, LRU_W), lambda i: (i, 0)), out_shape=_sds((t, LRU_W)),
               compiler_params=_params("arbitrary"))(xr, xr, xr, cw, cb)


def _conv_bwd(dxc_a, dxc_b, xr, cw, name):
    t = xr.shape[0]
    tm = _tile(t, True)
    nt = t // tm
    n = tm + 2 * SUBLANES
    mid = slice(SUBLANES, SUBLANES + tm)

    def body(ap_ref, am_ref, an_ref, bp_ref, bm_ref, bn_ref, xp_ref, xm_ref, xn_ref, w_ref, dx_ref, dw_ref, db_ref):
        i = pl.program_id(0)
        dp = _with_halo(ap_ref, am_ref, an_ref, i, nt) + _with_halo(bp_ref, bm_ref, bn_ref, i, nt)
        xp = _with_halo(xp_ref, xm_ref, xn_ref, i, nt)
        dx = pltpu.roll(dp, n - 1, 0)[mid] * w_ref[0:1, :]
        dx = dx + dp[mid] * w_ref[1:2, :]
        dx = dx + pltpu.roll(dp, 1, 0)[mid] * w_ref[2:3, :]
        dx = dx + pltpu.roll(dp, 2, 0)[mid] * w_ref[3:4, :]
        dx_ref[...] = dx.astype(BF16)
        d = dp[mid]

        @pl.when(i == 0)
        def _():
            dw_ref[...] = jnp.zeros_like(dw_ref)
            db_ref[...] = jnp.zeros_like(db_ref)
        dw_ref[0:1, :] += _sum0(d * pltpu.roll(xp, 1, 0)[mid])
        dw_ref[1:2, :] += _sum0(d * xp[mid])
        dw_ref[2:3, :] += _sum0(d * pltpu.roll(xp, n - 1, 0)[mid])
        dw_ref[3:4, :] += _sum0(d * pltpu.roll(xp, n - 2, 0)[mid])
        db_ref[...] += _sum0(d)

    return _pc(body, name=name, grid=(nt,),
               in_specs=[*_halo_specs(t, tm), *_halo_specs(t, tm), *_halo_specs(t, tm), _full((4, LRU_W))],
               out_specs=[pl.BlockSpec((tm, LRU_W), lambda i: (i, 0)), _full((4, LRU_W)), _full((1, LRU_W))],
               out_shape=[_sds((t, LRU_W), BF16), _sds((4, LRU_W)), _sds((1, LRU_W))],
               compiler_params=_params("arbitrary"))(dxc_a, dxc_a, dxc_a, dxc_b, dxc_b, dxc_b, xr, xr, xr, cw)


def _local_scan(a, b, reverse):
    n = a.shape[0]
    row = lax.broadcasted_iota(jnp.int32, a.shape, 0) & (SUBLANES - 1)
    for s in (1, 2, 4):
        if reverse:
            a_s, b_s, ok = pltpu.roll(a, n - s, 0), pltpu.roll(b, n - s, 0), row < SUBLANES - s
        else:
            a_s, b_s, ok = pltpu.roll(a, s, 0), pltpu.roll(b, s, 0), row >= s
        b = a * jnp.where(ok, b_s, 0.0) + b
        a = a * jnp.where(ok, a_s, 1.0)
    return a, b


def _carry_scan(a_s, b_s, out_ref, carry, reverse):
    ng = a_s.shape[0] // SUBLANES
    shape = carry.shape

    def step(g, cr):
        gg = (ng - 1 - g) if reverse else g
        off = pl.multiple_of(gg * SUBLANES, SUBLANES)
        h = a_s[pl.ds(off, SUBLANES), :] * cr + b_s[pl.ds(off, SUBLANES), :]
        out_ref[pl.ds(off, SUBLANES), :] = h
        edge = h[0:1, :] if reverse else h[SUBLANES - 1:SUBLANES, :]
        return jnp.broadcast_to(edge, shape)

    return lax.fori_loop(0, ng, step, carry)


def _lru_gates(xc, wa_ref, wx_ref, ba, bx, lam):
    xb = xc.astype(BF16)
    r = _sigmoid(_dot(xb, wa_ref[...]) + ba)
    ig = _sigmoid(_dot(xb, wx_ref[...]) + bx)
    sp = _softplus(-lam)
    la = -LRU_C * r * sp
    a = jnp.exp(la)
    mult = jnp.sqrt(_one_minus_sq(la, a))
    return r, ig, sp, a, mult


def _lru_fwd(xc, wa, wx, ba, bx, lam, h0, reverse, name, comms=()):
    t = xc.shape[0]
    tm = _tile(t, True)
    nt = t // tm
    tidx = (lambda i: (nt - 1 - i, 0)) if reverse else (lambda i: (i, 0))

    def body(x_ref, wa_ref, wx_ref, ba_ref, bx_ref, lam_ref, h0_ref, h_ref, a_s, b_s, c_s):
        @pl.when(pl.program_id(0) == 0)
        def _():
            c_s[...] = jnp.broadcast_to(h0_ref[...], c_s.shape)
        xv = x_ref[...]
        _, ig, _, a, mult = _lru_gates(xv, wa_ref, wx_ref, ba_ref[...], bx_ref[...], lam_ref[...])
        al, bl = _local_scan(a, mult * (ig * xv), reverse)
        a_s[...] = al
        b_s[...] = bl
        c_s[...] = _carry_scan(a_s, b_s, h_ref, c_s[...], reverse)

    vec = _full((1, LRU_W))
    mat = _full((LRU_W, LRU_W))
    (h,), couts = _call(body, name=name, grid=(nt,),
                        in_specs=[pl.BlockSpec((tm, LRU_W), tidx), mat, mat, vec, vec, vec, vec],
                        out_specs=[pl.BlockSpec((tm, LRU_W), tidx)], out_shape=[_sds((t, LRU_W))],
                        scratch_shapes=[pltpu.VMEM((tm, LRU_W), F32), pltpu.VMEM((tm, LRU_W), F32),
                                        pltpu.VMEM((SUBLANES, LRU_W), F32)],
                        sem=("arbitrary",), args=(xc, wa, wx, ba, bx, lam, h0), comms=comms)
    return (h, couts) if comms else h


def _lru_bwd(xc, wa, wx, ba, bx, lam, h, h0, dh, reverse, name, comms=()):
    t = xc.shape[0]
    tm = _tile(t, True)
    nt = t // tm
    n8 = tm // SUBLANES
    last8 = t // SUBLANES - 1
    tidx = (lambda i: (i, 0)) if reverse else (lambda i: (nt - 1 - i, 0))
    if reverse:
        halo = pl.BlockSpec((SUBLANES, LRU_W), lambda i: (jnp.minimum((i + 1) * n8, last8), 0))
    else:
        halo = pl.BlockSpec((SUBLANES, LRU_W), lambda i: (jnp.maximum((nt - 1 - i) * n8 - 1, 0), 0))

    def body(x_ref, wa_ref, wx_ref, ba_ref, bx_ref, lam_ref, h_ref, halo_ref, h0_ref, dh_ref,
             dx_ref, dpre_ref, dba_ref, dbx_ref, dlam_ref, dh0_ref, a_s, b_s, l_s, c_s, e_s):
        i = pl.program_id(0)

        @pl.when(i == 0)
        def _():
            c_s[...] = jnp.zeros_like(c_s)
            e_s[...] = jnp.zeros_like(e_s)
            dba_ref[...] = jnp.zeros_like(dba_ref)
            dbx_ref[...] = jnp.zeros_like(dbx_ref)
            dlam_ref[...] = jnp.zeros_like(dlam_ref)
        xv = x_ref[...]
        lam = lam_ref[...]
        r, ig, sp, a, mult = _lru_gates(xv, wa_ref, wx_ref, ba_ref[...], bx_ref[...], lam)
        hv = h_ref[...]
        rowi = lax.broadcasted_iota(jnp.int32, (tm, LRU_W), 0)
        edge_a = jnp.broadcast_to(e_s[0:1, :], (tm, LRU_W))
        h0b = jnp.broadcast_to(h0_ref[...], (tm, LRU_W))
        if reverse:
            a_sh = jnp.where(rowi == 0, edge_a, pltpu.roll(a, 1, 0))
            hin_edge = jnp.where(i == nt - 1, h0b, jnp.broadcast_to(halo_ref[0:1, :], (tm, LRU_W)))
            h_in = jnp.where(rowi == tm - 1, hin_edge, pltpu.roll(hv, tm - 1, 0))
        else:
            a_sh = jnp.where(rowi == tm - 1, edge_a, pltpu.roll(a, tm - 1, 0))
            hin_edge = jnp.where(i == nt - 1, h0b, jnp.broadcast_to(halo_ref[SUBLANES - 1:SUBLANES, :], (tm, LRU_W)))
            h_in = jnp.where(rowi == 0, hin_edge, pltpu.roll(hv, 1, 0))
        al, bl = _local_scan(a_sh, dh_ref[...], not reverse)
        a_s[...] = al
        b_s[...] = bl
        c_s[...] = _carry_scan(a_s, b_s, l_s, c_s[...], not reverse)
        e_s[...] = jnp.broadcast_to(a[tm - 1:tm, :] if reverse else a[0:1, :], e_s.shape)
        lmb = l_s[...]
        da = lmb * h_in
        ixc = ig * xv
        dmult = lmb * ixc
        dixc = lmb * mult
        dla = da * a - dmult * (a * a) / mult
        dpr = dla * (-LRU_C * sp) * r * (1.0 - r)
        dpi = dixc * xv * ig * (1.0 - ig)
        dprb = dpr.astype(BF16)
        dpib = dpi.astype(BF16)
        dpre_ref[:, 0:LRU_W] = dprb
        dpre_ref[:, LRU_W:2 * LRU_W] = dpib
        dx_ref[...] = dixc * ig + _dot_nt(dprb, wa_ref[...]) + _dot_nt(dpib, wx_ref[...])
        dba_ref[...] += _sum0(dpr)
        dbx_ref[...] += _sum0(dpi)
        dlam_ref[...] += _sum0(dla * (-LRU_C * r)) * (-_sigmoid(-lam))

        @pl.when(i == nt - 1)
        def _():
            al0 = a * lmb
            dh0_ref[...] = al0[tm - 1:tm, :] if reverse else al0[0:1, :]

    vec = _full((1, LRU_W))
    mat = _full((LRU_W, LRU_W))
    tile = pl.BlockSpec((tm, LRU_W), tidx)
    return _call(body, name=name, grid=(nt,),
                 in_specs=[tile, mat, mat, vec, vec, vec, tile, halo, vec, tile],
                 out_specs=[tile, pl.BlockSpec((tm, 2 * LRU_W), tidx), vec, vec, vec, vec],
                 out_shape=[_sds((t, LRU_W)), _sds((t, 2 * LRU_W), BF16), _sds((1, LRU_W)), _sds((1, LRU_W)),
                            _sds((1, LRU_W)), _sds((1, LRU_W))],
                 scratch_shapes=[pltpu.VMEM((tm, LRU_W), F32), pltpu.VMEM((tm, LRU_W), F32),
                                 pltpu.VMEM((tm, LRU_W), F32), pltpu.VMEM((SUBLANES, LRU_W), F32),
                                 pltpu.VMEM((SUBLANES, LRU_W), F32)],
                 sem=("arbitrary",), args=(xc, wa, wx, ba, bx, lam, h, h, h0, dh), comms=comms)


def _decay_tables(lg, reverse):
    ci = lax.broadcasted_iota(jnp.int32, (CHUNK, CHUNK), 0).astype(F32)
    mi = lax.broadcasted_iota(jnp.int32, (CHUNK, CHUNK), 1).astype(F32)
    rel = (mi - ci) if reverse else (ci - mi)
    relc = jnp.maximum(rel, 0.0)
    lg_c = jnp.concatenate([lg] * (CHUNK // LANES), axis=1)
    dm = jnp.where(rel >= 0, jnp.exp(lg_c * relc), 0.0)
    cd = lax.broadcasted_iota(jnp.int32, (CHUNK, DH), 0).astype(F32)
    pq, ps = (CHUNK - cd, cd) if reverse else (cd + 1.0, CHUNK - 1.0 - cd)
    return relc, dm, jnp.exp(lg * pq), jnp.exp(lg * ps), jnp.exp(lg * float(CHUNK)), pq, ps


def _ret_fwd(proj, lgv, s0f, s0b, comms=()):
    t = proj.shape[0]
    n = t // CHUNK

    def one(q, k, v, lg, s_s, hh, o_ref, sp_ref, reverse):
        _, dm, wq, ws, g, _, _ = _decay_tables(lg, reverse)
        vb = v.astype(BF16)
        p = _dot_nt(q.astype(BF16), k.astype(BF16)) * dm
        s = s_s[hh]
        sp_ref[hh, 0] = s
        o_ref[:, DH * hh:DH * (hh + 1)] = _dot(p.astype(BF16), vb) + _dot((q * wq).astype(BF16), s.astype(BF16))
        s_s[hh] = g * s + _dot_tn((k * ws).astype(BF16), vb)

    def body(qf, kf, vf, qb, kb, vb, lg_ref, s0f_ref, s0b_ref, of_ref, ob_ref, spf_ref, spb_ref, sf_s, sb_s):
        @pl.when(pl.program_id(0) == 0)
        def _():
            sf_s[...] = s0f_ref[...]
            sb_s[...] = s0b_ref[...]
        for hh in range(HEADS):
            sl = slice(DH * hh, DH * (hh + 1))
            one(qf[:, sl].astype(F32), kf[:, sl].astype(F32), vf[:, sl], lg_ref[hh, 0:1, :], sf_s, hh, of_ref, spf_ref,
                False)
            one(qb[:, sl].astype(F32), kb[:, sl].astype(F32), vb[:, sl], lg_ref[hh, 1:2, :], sb_s, hh, ob_ref, spb_ref,
                True)

    blk = (CHUNK, RET_W)
    fw = [pl.BlockSpec(blk, lambda i, o=o: (i, o)) for o in range(3)]
    bw = [pl.BlockSpec(blk, lambda i, o=o: (n - 1 - i, o)) for o in range(3)]
    st = _full((HEADS, DH, DH))
    return _call(body, name="ret_fwd", grid=(n,),
                 in_specs=fw + bw + [_full((HEADS, 2, LANES)), st, st],
                 out_specs=[pl.BlockSpec(blk, lambda i: (i, 0)), pl.BlockSpec(blk, lambda i: (n - 1 - i, 0)),
                            pl.BlockSpec((HEADS, 1, DH, DH), lambda i: (0, i, 0, 0)),
                            pl.BlockSpec((HEADS, 1, DH, DH), lambda i: (0, n - 1 - i, 0, 0))],
                 out_shape=[_sds((t, RET_W)), _sds((t, RET_W)), _sds((HEADS, n, DH, DH)), _sds((HEADS, n, DH, DH))],
                 scratch_shapes=[pltpu.VMEM((HEADS, DH, DH), F32), pltpu.VMEM((HEADS, DH, DH), F32)],
                 sem=("arbitrary",), args=(proj, proj, proj, proj, proj, proj, lgv, s0f, s0b), comms=comms)


def _ret_bwd(proj, lgv, sgv, sprev, do, reverse, name, comms=()):
    t = proj.shape[0]
    n = t // CHUNK
    d = 1 if reverse else 0
    cidx = (lambda i: i) if reverse else (lambda i: n - 1 - i)

    def body(q_ref, k_ref, v_ref, lg_ref, sg_ref, s_ref, do_ref, dq_ref, dk_ref, dv_ref, ds0_ref, drd_ref, ds_s, acc_s):
        i = pl.program_id(0)

        @pl.when(i == 0)
        def _():
            ds_s[...] = jnp.zeros_like(ds_s)
            acc_s[...] = jnp.zeros_like(acc_s)
        for hh in range(HEADS):
            sl = slice(DH * hh, DH * (hh + 1))
            relc, dm, wq, ws, g, pq, ps = _decay_tables(lg_ref[hh, d:d + 1, :], reverse)
            qb, kb, vb = q_ref[:, sl], k_ref[:, sl], v_ref[:, sl]
            q, k = qb.astype(F32), kb.astype(F32)
            p = _dot_nt(qb, kb) * dm
            s = s_ref[hh, 0]
            dob = do_ref[:, sl].astype(BF16)
            dsn = ds_s[hh]
            dsb = dsn.astype(BF16)
            dv_ref[:, sl] = (_dot_tn(p.astype(BF16), dob) + _dot((k * ws).astype(BF16), dsb)).astype(BF16)
            dp = _dot_nt(dob, vb)
            dab = (dp * dm).astype(BF16)
            xq = _dot_nt(dob, s.astype(BF16))
            yk = _dot_nt(vb, dsb)
            dq_ref[:, sl] = (_dot(dab, kb) + xq * wq).astype(BF16)
            dk_ref[:, sl] = (_dot_tn(dab, qb) + yk * ws).astype(BF16)
            ds_s[hh] = g * dsn + _dot_tn((q * wq).astype(BF16), dob)
            s_mask = _sum0(dp * p * relc)
            part = (sum(s_mask[:, LANES * u:LANES * (u + 1)] for u in range(CHUNK // LANES))
                    + _sum0(xq * q * wq * pq) + _sum0(yk * k * ws * ps) + _sum0(dsn * s) * g * float(CHUNK))
            acc_s[hh] += jnp.broadcast_to(part, (SUBLANES, LANES))

        @pl.when(i == n - 1)
        def _():
            ds0_ref[...] = ds_s[...]
            for hh in range(HEADS):
                tot = jnp.sum(acc_s[hh, 0:1, :], axis=1, keepdims=True)
                drd_ref[hh] = jnp.broadcast_to(tot, (SUBLANES, LANES)) * sg_ref[hh, d:d + 1, :]

    blk = (CHUNK, RET_W)
    qkv = [pl.BlockSpec(blk, lambda i, o=o: (cidx(i), o)) for o in range(3)]
    hc = pl.BlockSpec(blk, lambda i: (cidx(i), 0))
    lane = _full((HEADS, 2, LANES))
    return _call(body, name=name, grid=(n,),
                 in_specs=qkv + [lane, lane, pl.BlockSpec((HEADS, 1, DH, DH), lambda i: (0, cidx(i), 0, 0)), hc],
                 out_specs=[hc, hc, hc, _full((HEADS, DH, DH)), _full((HEADS, SUBLANES, LANES))],
                 out_shape=[_sds((t, RET_W), BF16)] * 3 + [_sds((HEADS, DH, DH)), _sds((HEADS, SUBLANES, LANES))],
                 scratch_shapes=[pltpu.VMEM((HEADS, DH, DH), F32), pltpu.VMEM((HEADS, SUBLANES, LANES), F32)],
                 sem=("arbitrary",), args=(proj, proj, proj, lgv, sgv, sprev, do), comms=comms)


def _ctx_weights(lg, l_len, reverse):
    pos = lax.broadcasted_iota(jnp.int32, (l_len, DH), 0).astype(F32)
    steps = pos if reverse else (l_len - 1.0 - pos)
    return jnp.exp(lg * steps), steps


def _ctx_state_fwd(projc, lgv):
    l_len = projc.shape[0]

    def body(k_ref, v_ref, lg_ref, sf_ref, sb_ref):
        k = k_ref[...]
        vb = v_ref[...].astype(BF16)
        for d, o_ref in ((0, sf_ref), (1, sb_ref)):
            w, _ = _ctx_weights(lg_ref[0, d:d + 1, :], l_len, d == 1)
            o_ref[0] = _dot_tn((k * w).astype(BF16), vb)

    st = pl.BlockSpec((1, DH, DH), lambda h: (h, 0, 0))
    return _pc(body, name="ctx_state_fwd", grid=(HEADS,),
               in_specs=[pl.BlockSpec((l_len, DH), lambda h: (0, HEADS + h)),
                         pl.BlockSpec((l_len, DH), lambda h: (0, 2 * HEADS + h)),
                         pl.BlockSpec((1, 2, LANES), lambda h: (h, 0, 0))],
               out_specs=[st, st], out_shape=[_sds((HEADS, DH, DH))] * 2,
               compiler_params=_params("arbitrary"))(projc, projc, lgv)


def _ctx_state_bwd(projc, lgv, sgv, dsf, dsb):
    l_len = projc.shape[0]

    def body(k_ref, v_ref, lg_ref, sg_ref, dsf_ref, dsb_ref, dk_ref, dv_ref, drd_ref):
        k = k_ref[...]
        vb = v_ref[...].astype(BF16)
        dk = jnp.zeros((l_len, DH), F32)
        dv = jnp.zeros((l_len, DH), F32)
        rows = []
        for d, ds_ref in ((0, dsf_ref), (1, dsb_ref)):
            w, steps = _ctx_weights(lg_ref[0, d:d + 1, :], l_len, d == 1)
            dsb16 = ds_ref[0].astype(BF16)
            dkw = _dot_nt(vb, dsb16)
            dk = dk + dkw * w
            dv = dv + _dot((k * w).astype(BF16), dsb16)
            tot = jnp.sum(_sum0(dkw * k * w * steps), axis=1, keepdims=True)
            rows.append(jnp.broadcast_to(tot, (1, LANES)) * sg_ref[0, d:d + 1, :])
        dk_ref[...] = dk.astype(BF16)
        dv_ref[...] = dv.astype(BF16)
        rid = lax.broadcasted_iota(jnp.int32, (SUBLANES, LANES), 0)
        drd_ref[0] = jnp.where(rid == 0, rows[0], jnp.where(rid == 1, rows[1], 0.0))

    st = pl.BlockSpec((1, DH, DH), lambda h: (h, 0, 0))
    lane = pl.BlockSpec((1, 2, LANES), lambda h: (h, 0, 0))
    hc = pl.BlockSpec((l_len, DH), lambda h: (0, h))
    return _pc(body, name="ctx_state_bwd", grid=(HEADS,),
               in_specs=[pl.BlockSpec((l_len, DH), lambda h: (0, HEADS + h)),
                         pl.BlockSpec((l_len, DH), lambda h: (0, 2 * HEADS + h)), lane, lane, st, st],
               out_specs=[hc, hc, pl.BlockSpec((1, SUBLANES, LANES), lambda h: (h, 0, 0))],
               out_shape=[_sds((l_len, RET_W), BF16), _sds((l_len, RET_W), BF16), _sds((HEADS, SUBLANES, LANES))],
               compiler_params=_params("arbitrary"))(projc, projc, lgv, sgv, dsf, dsb)


G_BLOCK = (3 * RET_W) // RET_W
GATE_BLOCK = (4 * RET_W + LRU_W) // LRU_W


def _head_norm(y):
    yc = y - jnp.mean(y, axis=-1, keepdims=True)
    rs = lax.rsqrt(jnp.mean(yc * yc, axis=-1, keepdims=True) + EPS)
    return yc * rs, rs


def _gelu_parts(z):
    th = jnp.tanh(GELU_K * (z + GELU_C * z * z * z))
    return 0.5 * z * (1.0 + th), th


def _mix_fwd(o_f, o_b, proj, hf, hb, w_out, x, g1):
    t = x.shape[0]
    tm = _tile(t, True)

    def body(of_ref, ob_ref, g_ref, gt_ref, hf_ref, hb_ref, w_ref, x_ref, g1_ref, x1_ref, cat_ref):
        o = of_ref[...] + ob_ref[...]
        g = g_ref[...].astype(F32)
        for hh in range(HEADS):
            sl = slice(DH * hh, DH * (hh + 1))
            nrm, _ = _head_norm(o[:, sl])
            gh = g[:, sl]
            cat_ref[:, sl] = (gh * _sigmoid(gh) * nrm).astype(BF16)
        gel, _ = _gelu_parts(gt_ref[...].astype(F32))
        cat_ref[:, RET_W:] = ((hf_ref[...] + hb_ref[...]) * gel).astype(BF16)
        x1_ref[...] = x_ref[...] + g1_ref[...] * _dot(cat_ref[...], w_ref[...])

    half = pl.BlockSpec((tm, RET_W), lambda i: (i, 0))
    big = pl.BlockSpec((tm, D_MODEL), lambda i: (i, 0))
    return _pc(body, name="mix_fwd", grid=(t // tm,),
               in_specs=[half, half, pl.BlockSpec((tm, RET_W), lambda i: (i, G_BLOCK)),
                         pl.BlockSpec((tm, LRU_W), lambda i: (i, GATE_BLOCK)), half, half,
                         _full((D_MODEL, D_MODEL)), big, _full((1, D_MODEL))],
               out_specs=[big, big], out_shape=[_sds((t, D_MODEL)), _sds((t, D_MODEL), BF16)],
               compiler_params=_params("arbitrary"))(o_f, o_b, proj, proj, hf, hb, w_out, x, g1)


def _mix_bwd(o_f, o_b, proj, hf, hb, w_out, cat, dx1, g1, comms=()):
    t = dx1.shape[0]
    tm = _tile(t, True)

    def body(of_ref, ob_ref, g_ref, gt_ref, hf_ref, hb_ref, w_ref, cat_ref, dx1_ref, g1_ref,
             do_ref, dhs_ref, dg_ref, dgt_ref, dyb_ref, dg1_ref):
        dx1v = dx1_ref[...]
        y = _dot(cat_ref[...], w_ref[...])

        @pl.when(pl.program_id(0) == 0)
        def _():
            dg1_ref[...] = jnp.zeros_like(dg1_ref)
        dg1_ref[...] += _sum0(dx1v * y)
        dyb = (g1_ref[...] * dx1v).astype(BF16)
        dyb_ref[...] = dyb
        dcat = _dot_nt(dyb, w_ref[...])
        o = of_ref[...] + ob_ref[...]
        g = g_ref[...].astype(F32)
        for hh in range(HEADS):
            sl = slice(DH * hh, DH * (hh + 1))
            nrm, rs = _head_norm(o[:, sl])
            gh = g[:, sl]
            sg = _sigmoid(gh)
            dret = dcat[:, sl]
            dg_ref[:, sl] = (dret * nrm * (sg * (1.0 + gh * (1.0 - sg)))).astype(BF16)
            dn = dret * (gh * sg)
            dyc = rs * (dn - nrm * jnp.mean(dn * nrm, axis=-1, keepdims=True))
            do_ref[:, sl] = (dyc - jnp.mean(dyc, axis=-1, keepdims=True)).astype(BF16)
        z = gt_ref[...].astype(F32)
        gel, th = _gelu_parts(z)
        dlru = dcat[:, RET_W:]
        dhs_ref[...] = dlru * gel
        dgel = 0.5 * (1.0 + th) + 0.5 * z * (1.0 - th * th) * GELU_K * (1.0 + 3.0 * GELU_C * z * z)
        dgt_ref[...] = (dlru * (hf_ref[...] + hb_ref[...]) * dgel).astype(BF16)

    half = pl.BlockSpec((tm, RET_W), lambda i: (i, 0))
    big = pl.BlockSpec((tm, D_MODEL), lambda i: (i, 0))
    return _call(body, name="mix_bwd", grid=(t // tm,),
                 in_specs=[half, half, pl.BlockSpec((tm, RET_W), lambda i: (i, G_BLOCK)),
                           pl.BlockSpec((tm, LRU_W), lambda i: (i, GATE_BLOCK)), half, half,
                           _full((D_MODEL, D_MODEL)), big, big, _full((1, D_MODEL))],
                 out_specs=[half, half, half, half, big, _full((1, D_MODEL))],
                 out_shape=[_sds((t, RET_W), BF16), _sds((t, RET_W)), _sds((t, RET_W), BF16), _sds((t, RET_W), BF16),
                            _sds((t, D_MODEL), BF16), _sds((1, D_MODEL))],
                 scratch_shapes=[], sem=("arbitrary",), args=(o_f, o_b, proj, proj, hf, hb, w_out, cat, dx1, g1),
                 comms=comms)


def _mlp(x1, n2g, sh2, sc2, g2, fg, w1_parts, w2_parts, tgt):
    t = x1.shape[0]
    tm = _tile(t, True)
    hb_ = MLP_H // N_CHIP
    q_rows = hb_ // 4
    n_cp = 4 * N_DEV

    def body(x1_ref, n2g_ref, sh2_ref, sc2_ref, g2_ref, fg_ref, w1a, w1b, w2a, w2b, tgt_ref,
             dx1_ref, h2b_ref, ab_ref, dub_ref, dmb_ref, dsc_ref, dsh_ref, dg2_ref, dn2_ref, dfg_ref, loss_ref,
             w1_s, w2_s, r_s, stage, sems, osems):
        rows = pl.ds(pl.multiple_of(pl.program_id(0) * tm, tm), tm)
        puts = []

        def put(val, out_ref, j):
            slot = len(puts) % 2
            if len(puts) >= 2:
                puts[-2].wait()
            stage[slot] = val
            puts.append(pltpu.make_async_copy(stage.at[slot], out_ref.at[rows, pl.ds(hb_ * j, hb_)], osems.at[slot]))
            puts[-1].start()

        @pl.when(pl.program_id(0) == 0)
        def _():
            cps = []
            for p, parts in enumerate(((w1a, w2a), (w1b, w2b))):
                for d in range(N_DEV):
                    rows = pl.ds(2 * q_rows * (d % 2) + q_rows * p, q_rows)
                    for src, dst in zip(parts, (w1_s, w2_s)):
                        cps.append(pltpu.make_async_copy(src.at[d], dst.at[d // 2, rows], sems.at[len(cps)]))
            for cp in cps:
                cp.start()
            for r in (dsc_ref, dsh_ref, dg2_ref, dn2_ref, dfg_ref, loss_ref):
                r[...] = jnp.zeros_like(r)
            for cp in cps:
                cp.wait()
        x1v = x1_ref[...]
        n2g, sc2, g2, fg = n2g_ref[...], sc2_ref[...], g2_ref[...], fg_ref[...]
        xh, _ = _rms(x1v)
        h2b = (xh * n2g * (1.0 + sc2) + sh2_ref[...]).astype(BF16)
        h2b_ref[...] = h2b
        m = jnp.zeros((tm, D_MODEL), F32)
        for j in range(N_CHIP):
            sl = slice(hb_ * j, hb_ * (j + 1))
            r = jnp.maximum(_dot(h2b, w1_s[j]), 0.0)
            r_s[:, sl] = r
            ab = (r * r).astype(BF16)
            put(ab, ab_ref, j)
            m = m + _dot(ab, w2_s[j])
        x2 = x1v + g2 * m
        x2h, r2 = _rms(x2)
        err = x2h * fg - tgt_ref[...]
        loss_ref[...] += _sum0(err * err)
        dout = err * (1.0 / D_MODEL)
        dfg_ref[...] += _sum0(dout * x2h)
        dxh = dout * fg
        dx2 = r2 * (dxh - x2h * jnp.mean(dxh * x2h, axis=-1, keepdims=True))
        dg2_ref[...] += _sum0(dx2 * m)
        dmb = (g2 * dx2).astype(BF16)
        dmb_ref[...] = dmb
        dh2 = jnp.zeros((tm, D_MODEL), F32)
        for j in range(N_CHIP):
            sl = slice(hb_ * j, hb_ * (j + 1))
            dub = (_dot_nt(dmb, w2_s[j]) * (2.0 * r_s[:, sl])).astype(BF16)
            put(dub, dub_ref, j)
            dh2 = dh2 + _dot_nt(dub, w1_s[j])
        for cp in puts[-2:]:
            cp.wait()
        dx, dn2_t, dsh_t, dsc_t = _norm_mod_bwd(x1v, n2g, sc2, dh2)
        dx1_ref[...] = dx2 + dx
        dn2_ref[...] += dn2_t
        dsh_ref[...] += dsh_t
        dsc_ref[...] += dsc_t

        @pl.when(pl.program_id(0) == t // tm - 1)
        def _():
            tot = jnp.sum(loss_ref[...], axis=1, keepdims=True) * (0.5 / D_MODEL)
            loss_ref[...] = jnp.broadcast_to(tot, loss_ref.shape)

    row = _full((1, D_MODEL))
    big = pl.BlockSpec((tm, D_MODEL), lambda i: (i, 0))
    return _pc(body, name="mlp", grid=(t // tm,),
               in_specs=[big, row, row, row, row, row, ANY, ANY, ANY, ANY, big],
               out_specs=[big, big, ANY, ANY, big, row, row, row, row, row, row],
               out_shape=[_sds((t, D_MODEL)), _sds((t, D_MODEL), BF16), _sds((t, MLP_H), BF16), _sds((t, MLP_H), BF16),
                          _sds((t, D_MODEL), BF16)] + [_sds((1, D_MODEL))] * 6,
               scratch_shapes=[pltpu.VMEM((N_CHIP, D_MODEL, hb_), BF16), pltpu.VMEM((N_CHIP, hb_, D_MODEL), BF16),
                               pltpu.VMEM((tm, MLP_H), F32), pltpu.VMEM((2, tm, hb_), BF16),
                               pltpu.SemaphoreType.DMA((n_cp,)), pltpu.SemaphoreType.DMA((2,))],
               compiler_params=_params("arbitrary"))(x1, n2g, sh2, sc2, g2, fg, *w1_parts, *w2_parts, tgt)


def _tn(a, b, nj, a_blocked, b_blocked, name, extra=None, comms=()):
    t = a.shape[0]
    m = a.shape[1] // (nj if a_blocked else 1)
    n = b.shape[1] // (nj if b_blocked else 1)
    bk = next((b for b in (2048, 1024, 512) if t % b == 0), t)
    nk = t // bk
    a_col = (lambda j: j) if a_blocked else (lambda j: 0)
    b_col = (lambda j: j) if b_blocked else (lambda j: 0)
    in_specs = [pl.BlockSpec((bk, m), lambda j, k: (k, a_col(j))), pl.BlockSpec((bk, n), lambda j, k: (k, b_col(j)))]
    args = [a, b]
    if extra is not None:
        a2, b2 = extra
        t2 = a2.shape[0]
        in_specs += [pl.BlockSpec((t2, m), lambda j, k: (0, a_col(j))),
                     pl.BlockSpec((t2, n), lambda j, k: (0, b_col(j)))]
        args += [a2, b2]

    def body(*refs):
        a_ref, b_ref = refs[0], refs[1]
        o_ref, acc = refs[-2], refs[-1]
        k = pl.program_id(1)

        @pl.when(k == 0)
        def _():
            acc[...] = jnp.zeros_like(acc)
        acc[...] += _dot_tn(a_ref[...].astype(BF16), b_ref[...].astype(BF16))

        @pl.when(k == nk - 1)
        def _():
            if extra is not None:
                acc[...] += _dot_tn(refs[2][...].astype(BF16), refs[3][...].astype(BF16))
            o_ref[0] = acc[...]

    (out,), couts = _call(body, name=name, grid=(nj, nk), in_specs=in_specs,
                          out_specs=[pl.BlockSpec((1, m, n), lambda j, k: (j, 0, 0))], out_shape=[_sds((nj, m, n))],
                          scratch_shapes=[pltpu.VMEM((m, n), F32)], sem=("arbitrary", "arbitrary"), args=args,
                          comms=comms)
    return (out, couts) if comms else out


ROW_LOSS = 0
ROW_DMOD = 1
ROW_DMODC = 7
ROW_N1, ROW_N2, ROW_FG, ROW_CB = 9, 10, 11, 12
ROW_BA, ROW_BX, ROW_LAM = 13, 15, 17
ROW_CW = 20
ROW_RD = 24
SLAB_ROWS = 32
SEG = D_MODEL // 2


def _pack_small(rows, drd, cw2, cb2, lru2, gates):
    n_rows, n_lru = len(rows), len(lru2)

    def body(*refs):
        r = refs[:n_rows]
        drd_f, drd_b, drd_c, cw_a, cw_b, cb_a, cb_b = refs[n_rows:n_rows + 7]
        lru = refs[n_rows + 7:n_rows + 7 + n_lru]
        gf_ref, gb_ref, slab, ga, gx = refs[n_rows + 7 + n_lru:]
        slab[...] = jnp.zeros_like(slab)
        slab[ROW_LOSS:ROW_LOSS + 1, :] = r[0][...]
        for k in range(N_MOD):
            slab[ROW_DMOD + k:ROW_DMOD + k + 1, :] = r[1 + k][...]
        slab[ROW_DMODC:ROW_DMODC + 1, :] = r[7][...]
        slab[ROW_DMODC + 1:ROW_DMODC + 2, :] = r[8][...]
        slab[ROW_N1:ROW_N1 + 1, :] = r[9][...] + r[10][...]
        slab[ROW_N2:ROW_N2 + 1, :] = r[11][...]
        slab[ROW_FG:ROW_FG + 1, :] = r[12][...]
        slab[ROW_CB:ROW_CB + 1, 0:LRU_W] = cb_a[...] + cb_b[...]
        for k, row in enumerate((ROW_BA, ROW_BA + 1, ROW_BX, ROW_BX + 1, ROW_LAM, ROW_LAM + 1)):
            slab[row:row + 1, 0:LRU_W] = lru[2 * k][...] + lru[2 * k + 1][...]
        slab[ROW_CW:ROW_CW + 4, 0:LRU_W] = cw_a[...] + cw_b[...]
        for h in range(HEADS):
            slab[ROW_RD + h:ROW_RD + h + 1, 0:LANES] = drd_f[h, 0:1, :] + drd_c[h, 0:1, :]
            slab[ROW_RD + HEADS + h:ROW_RD + HEADS + h + 1, 0:LANES] = drd_b[h, 0:1, :] + drd_c[h, 1:2, :]
        for d, g_ref in enumerate((gf_ref, gb_ref)):
            for n in range(LRU_BLOCKS):
                blk = slice(LRU_BD * n, LRU_BD * (n + 1))
                ga[blk, LRU_BD * d:LRU_BD * (d + 1)] = g_ref[0, blk, blk].astype(BF16)
                gx[blk, LRU_BD * d:LRU_BD * (d + 1)] = g_ref[1, blk, blk].astype(BF16)

    args = list(rows) + list(drd) + list(cw2) + list(cb2) + list(lru2) + list(gates)
    gate_shape = (LRU_W, 2 * LRU_BD)
    return _pc(body, name="pack_small", in_specs=[_full(a.shape) for a in args],
               out_specs=[_full((SLAB_ROWS, D_MODEL)), _full(gate_shape), _full(gate_shape)],
               out_shape=[_sds((SLAB_ROWS, D_MODEL)), _sds(gate_shape, BF16), _sds(gate_shape, BF16)],
               compiler_params=_params())(*args)


def _adam_math(w, g, m, v):
    mn = ADAM_B1 * m + (1.0 - ADAM_B1) * g
    vn = ADAM_B2 * v + (1.0 - ADAM_B2) * (g * g)
    mh = mn / (1.0 - ADAM_B1 ** ADAM_STEP)
    vh = vn / (1.0 - ADAM_B2 ** ADAM_STEP)
    return -ADAM_LR * (mh / (jnp.sqrt(vh) + ADAM_EPS) + ADAM_WD * w), mn, vn


SMALL_PARAMS = ("b_ada", "norm1_g", "norm2_g", "final_g", "ret_decay", "conv_w", "conv_b", "lru_wa", "lru_ba", "lru_wx",
                "lru_bx", "lru_lambda")


def _finalize_small(chip_idx, slab_all, ga_all, gx_all, wmv):
    n_p = len(SMALL_PARAMS)
    flat = [a for nm in SMALL_PARAMS for a in wmv[nm]]
    ada_n = N_MOD * D_MODEL // N_CHIP

    def body(c_ref, slab_ref, ga_ref, gx_ref, *refs):
        prm = {nm: refs[3 * k:3 * k + 3] for k, nm in enumerate(SMALL_PARAMS)}
        outs = {nm: refs[3 * n_p + 4 * k:3 * n_p + 4 * k + 4] for k, nm in enumerate(SMALL_PARAMS)}
        b128_ref, dmc_ref, loss_ref = refs[3 * n_p + 4 * n_p:]
        chip = c_ref[0]

        def pick(fn):
            acc = fn(0)
            for j in range(1, N_CHIP):
                acc = jnp.where(chip == j, fn(j), acc)
            return acc

        tot = slab_ref[0]
        for d in range(1, N_DEV):
            tot = tot + slab_ref[d]

        def update(nm, g, sl=None, rows=None):
            w_ref, m_ref, v_ref = prm[nm]
            g_ref, d_ref, mo_ref, vo_ref = outs[nm]
            ix = (slice(None) if rows is None else rows, slice(None) if sl is None else sl)
            dl, mn, vn = _adam_math(w_ref[ix], g, m_ref[ix], v_ref[ix])
            g_ref[ix] = g
            d_ref[ix] = dl
            mo_ref[ix] = mn
            vo_ref[ix] = vn

        loss_ref[...] = jnp.broadcast_to(tot[ROW_LOSS:ROW_LOSS + 1, 0:LANES], (SUBLANES, LANES))
        for k in range(N_MOD):
            g = tot[ROW_DMOD + k:ROW_DMOD + k + 1, :]
            if k < 2:
                g = g + tot[ROW_DMODC + k:ROW_DMODC + k + 1, :]
            update("b_ada", g, slice(D_MODEL * k, D_MODEL * (k + 1)))
        update("norm1_g", tot[ROW_N1:ROW_N1 + 1, :])
        update("norm2_g", tot[ROW_N2:ROW_N2 + 1, :])
        update("final_g", tot[ROW_FG:ROW_FG + 1, :])
        update("ret_decay", tot[ROW_RD:ROW_RD + SUBLANES, 0:LANES])
        update("conv_b", tot[ROW_CB:ROW_CB + 1, 0:LRU_W])
        update("conv_w", pick(lambda j: tot[ROW_CW:ROW_CW + 4, LANES * j:LANES * (j + 1)]))
        for nm, row in (("lru_ba", ROW_BA), ("lru_bx", ROW_BX), ("lru_lambda", ROW_LAM)):
            update(nm, pick(lambda j, row=row: tot[row:row + 2, LANES * j:LANES * (j + 1)]))
        for nm, g_all in (("lru_wa", ga_ref), ("lru_wx", gx_ref)):
            for dr in range(2):
                lanes = slice(LRU_BD * dr, LRU_BD * (dr + 1))
                g = g_all[0, :, lanes].astype(F32)
                for d in range(1, N_DEV):
                    g = g + g_all[d, :, lanes].astype(F32)
                update(nm, g, rows=slice(LRU_W * dr, LRU_W * (dr + 1)))

        def seg(rows6, s):
            return rows6[s // 2][:, SEG * (s % 2):SEG * (s % 2 + 1)]

        b128_ref[...] = jnp.zeros_like(b128_ref)
        dmc_ref[...] = jnp.zeros_like(dmc_ref)
        zero = jnp.zeros((1, D_MODEL), F32)
        ctx6 = [tot[ROW_DMODC:ROW_DMODC + 1, :], tot[ROW_DMODC + 1:ROW_DMODC + 2, :]] + [zero] * (N_MOD - 2)
        for q in range(ada_n // SEG):
            cols = slice(SEG * q, SEG * (q + 1))
            for d in range(N_DEV):
                rows6 = [slab_ref[d, ROW_DMOD + k:ROW_DMOD + k + 1, :] for k in range(N_MOD)]
                b128_ref[d:d + 1, cols] = pick(lambda j, rows6=rows6: seg(rows6, 3 * j + q))
            c = pick(lambda j: seg(ctx6, 3 * j + q))
            b128_ref[N_DEV:N_DEV + 1, cols] = c
            dmc_ref[0:1, cols] = c

    out_shape = []
    for nm in SMALL_PARAMS:
        out_shape += [_sds(wmv[nm][0].shape)] * 4
    out_shape += [_sds((LANES, ada_n)), _sds((SUBLANES, ada_n)), _sds((SUBLANES, LANES))]
    args = [slab_all, ga_all, gx_all] + flat
    grid_spec = pltpu.PrefetchScalarGridSpec(
        num_scalar_prefetch=1, grid=(1,), in_specs=[_full(a.shape) for a in args],
        out_specs=[_full(s.shape) for s in out_shape])
    outs = _pc(body, name="finalize_small", grid_spec=grid_spec, out_shape=out_shape,
               compiler_params=_params("arbitrary"))(chip_idx, *args)
    res = {nm: tuple(outs[4 * k:4 * k + 4]) for k, nm in enumerate(SMALL_PARAMS)}
    return res, outs[4 * n_p], outs[4 * n_p + 1], outs[4 * n_p + 2]


def _block_diag(w):
    eye = jnp.eye(LRU_BLOCKS, dtype=F32)
    return (w[:, :, None, :] * eye[:, None, :, None]).reshape(LRU_W, LRU_W).astype(BF16)


def _lane_rep(v8):
    return jnp.broadcast_to(v8.reshape(SUBLANES, 1), (SUBLANES, LANES))


def kernel(x, c, ctx, c_ctx, w_ada, b_ada, norm1_g, norm2_g, w_in, ret_decay, conv_w, conv_b, lru_wa, lru_ba, lru_wx, lru_bx, lru_lambda, w_out, w_mlp1, w_mlp2, final_g, loss_target, m_c_ctx, m_w_ada, m_b_ada, m_norm1_g, m_norm2_g, m_w_in, m_ret_decay, m_conv_w, m_conv_b, m_lru_wa, m_lru_ba, m_lru_wx, m_lru_bx, m_lru_lambda, m_w_out, m_w_mlp1, m_w_mlp2, m_final_g, v_c_ctx, v_w_ada, v_b_ada, v_norm1_g, v_norm2_g, v_w_in, v_ret_decay, v_conv_w, v_conv_b, v_lru_wa, v_lru_ba, v_lru_wx, v_lru_bx, v_lru_lambda, v_w_out, v_w_mlp1, v_w_mlp2, v_final_g):
    ax, ay, ac = lax.axis_index("x"), lax.axis_index("y"), lax.axis_index("c")
    chip = 2 * ax + ay
    dev = 4 * ax + 2 * ay + ac
    c_idx = jnp.stack([ac, chip]).astype(jnp.int32)
    j_idx = chip.reshape(1).astype(jnp.int32)

    xt = x[0]
    t_len = xt.shape[0]
    ctxt = ctx[0]
    l_len = ctxt.shape[0]
    tgt = loss_target[0]
    ada_n = w_ada.shape[2]

    def my_half(w2d):
        r = w2d.shape[0] // 2
        return lax.dynamic_slice_in_dim(w2d, ac * r, r, axis=0).astype(BF16)

    pad8 = lambda a: jnp.pad(a, ((0, SUBLANES - a.shape[0]), (0, 0)))
    small = jnp.concatenate([pad8(conv_w[0]), pad8(lru_ba[0]), pad8(lru_bx[0]), pad8(lru_lambda[0])], axis=0)
    b_shard = lax.dynamic_slice_in_dim(b_ada, chip * ada_n, ada_n, axis=1)
    gw_in, _, small_all, a16, mod_parts, lgv, sgv = _head(
        my_half(w_in[0]), pad8(c), small, w_ada[0], b_shard, c_ctx, ret_decay[0])
    w4 = gw_in.reshape(N_CHIP, D_MODEL, IN_COLS // N_CHIP)

    mod_all = mod_parts[0::2].transpose(1, 0, 2).reshape(16, N_CHIP * ada_n)
    mod_me = lax.dynamic_slice_in_dim(mod_all, dev, 1, axis=0)
    sh1, sc1, g1, sh2, sc2, g2 = [mod_me[:, D_MODEL * k:D_MODEL * (k + 1)] for k in range(N_MOD)]
    csh1, csc1 = mod_all[8:9, 0:D_MODEL], mod_all[8:9, D_MODEL:2 * D_MODEL]

    cos2, sin2 = _rotary_tables(t_len)
    cos_c, sin_c = jnp.ones((l_len, DH), F32), jnp.zeros((l_len, DH), F32)
    n1g, n2g = norm1_g, norm2_g
    fg = final_g.reshape(1, D_MODEL)

    small_full = small_all[0::2].transpose(1, 0, 2).reshape(4 * SUBLANES, LRU_W)
    cw = small_full[0:4]
    cb = conv_b
    ba_f, ba_b = small_full[8:9], small_full[9:10]
    bx_f, bx_b = small_full[16:17], small_full[17:18]
    lam_f, lam_b = small_full[24:25], small_full[25:26]
    wa_f, wa_b = _block_diag(lru_wa[0, 0]), _block_diag(lru_wa[0, 1])
    wx_f, wx_b = _block_diag(lru_wx[0, 0]), _block_diag(lru_wx[0, 1])
    zero_h = jnp.zeros((1, LRU_W), F32)

    projc, xrc, hcb16 = _inproj_fwd(ctxt, n1g, csh1, csc1, w4, cos_c, sin_c, "inproj_fwd_ctx")
    s_f, s_b = _ctx_state_fwd(projc, lgv)
    xcc = _conv_fwd(xrc, cw, cb, "conv_fwd_ctx")
    hcf = _lru_fwd(xcc, wa_f, wx_f, ba_f, bx_f, lam_f, zero_h, False, "lru_fwd_ctx_f")
    hcbk = _lru_fwd(xcc, wa_b, wx_b, ba_b, bx_b, lam_b, zero_h, True, "lru_fwd_ctx_b")
    lru_sf, lru_sb = hcf[l_len - 1:l_len], hcbk[0:1]

    h1, h2 = my_half(w_mlp1[0]), my_half(w_mlp2[0])
    q = h1.shape[0] // 2
    (proj, xrl, hb16), ((gw_1a,),) = _inproj_fwd(xt, n1g, sh1, sc1, w4, cos2, sin2, "inproj_fwd",
                                           comms=(_AllGather([h1[:q]]),))
    (o_f, o_b, spf, spb), ((gw_1b, gw_out),) = _ret_fwd(proj, lgv, s_f, s_b,
                                                       comms=(_AllGather([h1[q:], my_half(w_out[0])]),))
    xcl = _conv_fwd(xrl, cw, cb, "conv_fwd")
    hf, ((gw_2a,),) = _lru_fwd(xcl, wa_f, wx_f, ba_f, bx_f, lam_f, lru_sf, False, "lru_fwd_f",
                              comms=(_AllGather([h2[:q]]),))
    hbk, ((gw_2b,),) = _lru_fwd(xcl, wa_b, wx_b, ba_b, bx_b, lam_b, lru_sb, True, "lru_fwd_b",
                               comms=(_AllGather([h2[q:]]),))
    wo = gw_out.reshape(D_MODEL, D_MODEL)
    x1, cat = _mix_fwd(o_f, o_b, proj, hf, hbk, wo, xt, g1)

    (dx1, h2b, ab, dub, dmb, dsc2, dsh2, dg2, dn2g, dfg, lossv) = _mlp(
        x1, n2g, sh2, sc2, g2, fg, (gw_1a, gw_1b), (gw_2a, gw_2b), tgt)
    gw_mlp1 = _tn(h2b, dub, N_CHIP, False, True, "grad_w_mlp1")
    b_1 = gw_mlp1.reshape(N_DEV, D_MODEL // 2, MLP_H // N_CHIP)
    gw_mlp2, ((r_1,),) = _tn(ab, dmb, N_CHIP, True, False, "grad_w_mlp2", comms=(_pair_exchange([b_1]),))

    half = D_MODEL // 4
    top, bot = (0, half), (half, half)
    b_2 = gw_mlp2.reshape(N_DEV, MLP_H // N_DEV, D_MODEL)
    p_1, pb_1 = _pair_add(b_1, r_1, c_idx, "rs_pair_add_w_mlp1")
    (do, dhs, dg, dgate, dyb, dg1), ((q_1a,), (r_2,)) = _mix_bwd(
        o_f, o_b, proj, hf, hbk, wo, cat, dx1, g1, comms=(_chip_exchange([pb_1], top), _pair_exchange([b_2])))
    gw_o = _tn(cat, dyb, 1, False, False, "grad_w_out")
    b_o = gw_o.reshape(N_DEV, D_MODEL // N_DEV, D_MODEL)
    p_2, pb_2 = _pair_add(b_2, r_2, c_idx, "rs_pair_add_w_mlp2")

    (dq_f, dk_f, dv_f, ds_f, drd_f), ((q_1b,), (r_o,)) = _ret_bwd(
        proj, lgv, sgv, spf, do, False, "ret_bwd_f", comms=(_chip_exchange([pb_1], bot), _pair_exchange([b_o])))
    p_o, pb_o = _pair_add(b_o, r_o, c_idx, "rs_pair_add_w_out")
    h_1 = _chip_add(p_1, (q_1a, q_1b), c_idx, "rs_chip_add_w_mlp1")

    (dq_b, dk_b, dv_b, ds_b, drd_b), ((q_2a,), (f_1,)) = _ret_bwd(
        proj, lgv, sgv, spb, do, True, "ret_bwd_b", comms=(_chip_exchange([pb_2], top), _pair_gather([h_1])))

    (dxc_f, dpre_f, dba_f, dbx_f, dlam_f, dh0_f), ((q_2b,),) = _lru_bwd(
        xcl, wa_f, wx_f, ba_f, bx_f, lam_f, hf, lru_sf, dhs, False, "lru_bwd_f",
        comms=(_chip_exchange([pb_2], bot),))
    h_2 = _chip_add(p_2, (q_2a, q_2b), c_idx, "rs_chip_add_w_mlp2")
    (dxc_b, dpre_b, dba_b, dbx_b, dlam_b, dh0_b), ((q_o,), (f_2,)) = _lru_bwd(
        xcl, wa_b, wx_b, ba_b, bx_b, lam_b, hbk, lru_sb, dhs, True, "lru_bwd_b",
        comms=(_chip_exchange([pb_o]), _pair_gather([h_2])))
    h_o = _chip_add(p_o, (q_o,), c_idx, "rs_chip_add_w_out")
    dxr, dcw, dcb = _conv_bwd(dxc_f, dxc_b, xrl, cw, "conv_bwd")
    grad_x, dpb, dn1g, dsh1, dsc1 = _inproj_bwd(
        xt, n1g, sh1, sc1, w4, cos2, sin2, [dq_f, dq_b, dk_f, dk_b, dv_f, dv_b, dg, dxr, dgate], dx1, "inproj_bwd")

    dkc, dvc, drd_c = _ctx_state_bwd(projc, lgv, sgv, ds_f, ds_b)
    zc = jnp.zeros((l_len, LRU_W), F32)
    dhc_f = lax.dynamic_update_slice(zc, dh0_f, (l_len - 1, 0))
    dhc_b = lax.dynamic_update_slice(zc, dh0_b, (0, 0))
    (dxcc_f, dprec_f, dbac_f, dbxc_f, dlamc_f, _), _ = _lru_bwd(
        xcc, wa_f, wx_f, ba_f, bx_f, lam_f, hcf, zero_h, dhc_f, False, "lru_bwd_ctx_f")
    (dxcc_b, dprec_b, dbac_b, dbxc_b, dlamc_b, _), _ = _lru_bwd(
        xcc, wa_b, wx_b, ba_b, bx_b, lam_b, hcbk, zero_h, dhc_b, True, "lru_bwd_ctx_b")
    dxrc, dcw_c, dcb_c = _conv_bwd(dxcc_f, dxcc_b, xrc, cw, "conv_bwd_ctx")
    zr = jnp.zeros((l_len, RET_W), BF16)
    _, dpbc, dn1g_c, dcsh1, dcsc1 = _inproj_bwd(
        ctxt, n1g, csh1, csc1, w4, cos_c, sin_c, [zr, zr, dkc, zr, dvc, zr, zr, dxrc, zr],
        jnp.zeros((l_len, D_MODEL), F32), "inproj_bwd_ctx")

    gw_i = _tn(hb16, dpb, N_CHIP, False, True, "grad_w_in", extra=(hcb16, dpbc))
    b_i = gw_i.reshape(N_DEV, D_MODEL // 2, IN_COLS // N_CHIP)
    gwa_f, ((r_i,), (f_o,)) = _tn(xcl, dpre_f, 2, False, True, "grad_lru_gates_f", extra=(xcc, dprec_f),
                                  comms=(_pair_exchange([b_i]), _pair_gather([h_o])))
    p_i, pb_i = _pair_add(b_i, r_i, c_idx, "rs_pair_add_w_in")
    gwa_b, ((q_i,),) = _tn(xcl, dpre_b, 2, False, True, "grad_lru_gates_b", extra=(xcc, dprec_b),
                           comms=(_chip_exchange([pb_i]),))
    slab, ga, gx = _pack_small(
        [lossv, dsh1, dsc1, dg1, dsh2, dsc2, dg2, dcsh1, dcsc1, dn1g, dn1g_c, dn2g, dfg],
        (drd_f, drd_b, drd_c), (dcw, dcw_c), (dcb, dcb_c),
        (dba_f, dbac_f, dba_b, dbac_b, dbx_f, dbxc_f, dbx_b, dbxc_b, dlam_f, dlamc_f, dlam_b, dlamc_b),
        (gwa_f, gwa_b))
    (f_i,), (slab_all, ga_all, gx_all) = _run_comms(
        [_pair_gather([_chip_add(p_i, (q_i,), c_idx, "rs_chip_add_w_in")]), _AllGather([slab, ga, gx])],
        "tail_exchanges")
    g_in, g_out, g_1, g_2 = _shard_of(f_i), _shard_of(f_o), _shard_of(f_1), _shard_of(f_2)
    big = {}
    for nm, w, g, m, v in (("w_in", w_in, g_in, m_w_in, v_w_in), ("w_out", w_out, g_out, m_w_out, v_w_out),
                           ("w_mlp1", w_mlp1, g_1, m_w_mlp1, v_w_mlp1), ("w_mlp2", w_mlp2, g_2, m_w_mlp2, v_w_mlp2)):
        go, d_, mn, vn = _adamw(w[0], g, m[0], v[0], "adamw_" + nm)
        big[nm] = (go[None], d_[None], mn[None], vn[None])
    params = {
        "b_ada": (b_ada, m_b_ada, v_b_ada), "norm1_g": (norm1_g, m_norm1_g, v_norm1_g),
        "norm2_g": (norm2_g, m_norm2_g, v_norm2_g), "final_g": (final_g, m_final_g, v_final_g),
        "ret_decay": (ret_decay, m_ret_decay, v_ret_decay), "conv_w": (conv_w, m_conv_w, v_conv_w),
        "conv_b": (conv_b, m_conv_b, v_conv_b), "lru_wa": (lru_wa, m_lru_wa, v_lru_wa),
        "lru_ba": (lru_ba, m_lru_ba, v_lru_ba), "lru_wx": (lru_wx, m_lru_wx, v_lru_wx),
        "lru_bx": (lru_bx, m_lru_bx, v_lru_bx), "lru_lambda": (lru_lambda, m_lru_lambda, v_lru_lambda),
    }
    as2d = {
        "b_ada": lambda a: a, "norm1_g": lambda a: a, "norm2_g": lambda a: a, "conv_b": lambda a: a,
        "final_g": lambda a: a.reshape(1, D_MODEL), "ret_decay": lambda a: _lane_rep(a.reshape(-1)),
        "conv_w": lambda a: a[0], "lru_ba": lambda a: a[0], "lru_bx": lambda a: a[0], "lru_lambda": lambda a: a[0],
        "lru_wa": lambda a: a.reshape(2 * LRU_W, LRU_BD), "lru_wx": lambda a: a.reshape(2 * LRU_W, LRU_BD),
    }
    res, b128, dmc8, loss8 = _finalize_small(
        j_idx, slab_all, ga_all, gx_all, {nm: tuple(as2d[nm](a) for a in params[nm]) for nm in SMALL_PARAMS})
    loss = loss8[0, 0]
    small_out = {}
    for nm in SMALL_PARAMS:
        shp = params[nm][0].shape
        if nm == "ret_decay":
            small_out[nm] = tuple(o[:, 0].reshape(shp) for o in res[nm])
        else:
            small_out[nm] = tuple(o.reshape(shp) for o in res[nm])

    g_ada = _ada_grad(jnp.pad(a16.T, ((0, 0), (0, LANES - 16))), b128)
    g_ada, d_ada, m_ada, v_ada = _adamw(w_ada[0], g_ada, m_w_ada[0], v_w_ada[0], "adamw_w_ada")

    (cparts,) = _all_gather([_cctx_partial(dmc8, w_ada[0])], "gather_cctx")
    g_cc, d_cc, m_cc, v_cc = _cctx_final(cparts, c_ctx, m_c_ctx, v_c_ctx)
    small_out["c_ctx"] = tuple(a.reshape(D_MODEL) for a in (g_cc, d_cc, m_cc, v_cc))
    small_out["w_ada"] = (g_ada[None], d_ada[None], m_ada[None], v_ada[None])
    small_out.update(big)

    order = ["c_ctx", "w_ada", "b_ada", "norm1_g", "norm2_g", "w_in", "ret_decay", "conv_w", "conv_b", "lru_wa", "lru_ba",
             "lru_wx", "lru_bx", "lru_lambda", "w_out", "w_mlp1", "w_mlp2", "final_g"]
    outs = [loss, grad_x[None]]
    for k in range(4):
        outs += [small_out[nm][k] for nm in order]
    return tuple(outs)
```

```python
import math

import jax
import jax.numpy as jnp
from jax import lax
from jax.experimental import pallas as pl
from jax.experimental.pallas import tpu as pltpu

F32 = jnp.float32
BF16 = jnp.bfloat16

D_MODEL = 1024
HEADS = 4
DH = 128
CHUNK = 256
RET_W = HEADS * DH
LRU_W = 512
LRU_BLOCKS = 8
LRU_BD = LRU_W // LRU_BLOCKS
LRU_C = 8.0
IN_COLS = 4 * RET_W + 2 * LRU_W
MLP_H = 4 * D_MODEL
N_MOD = 6
GRID_W = 64
ROPE_BASE = 10000.0
K_SCALE = DH ** -0.5
EPS = 1e-6
GELU_K = math.sqrt(2.0 / math.pi)
GELU_C = 0.044715

ADAM_LR = 0.001
ADAM_B1 = 0.9
ADAM_B2 = 0.999
ADAM_EPS = 1e-08
ADAM_WD = 0.01
ADAM_STEP = 10

N_DEV = 8
N_CHIP = 4
SUBLANES = 8
LANES = 128
VMEM_LIMIT_V7X = 56 * 1024 * 1024
MESH = pl.DeviceIdType.MESH
ANY = pl.BlockSpec(memory_space=pl.ANY)


def _pc(body, **kw):
    return pl.pallas_call(body, **kw)


def _params(*sem):
    return pltpu.CompilerParams(dimension_semantics=sem if sem else None, vmem_limit_bytes=VMEM_LIMIT_V7X)


def _tile(t, big=False):
    if big and t >= 1024:
        return 512
    return 256 if t >= 256 else t


def _sds(shape, dtype=F32):
    return jax.ShapeDtypeStruct(tuple(shape), dtype)


def _full(shape):
    nd = len(shape)
    return pl.BlockSpec(tuple(shape), lambda *_: (0,) * nd)


def _sigmoid(x):
    return 1.0 / (1.0 + jnp.exp(-x))


def _log1p_pos(y):
    s = y * (1.0 - y * (0.5 - y * (1.0 / 3.0 - y * (0.25 - y * (0.2 - y / 6.0)))))
    return jnp.where(y < 0.03, s, jnp.log(1.0 + y))


def _softplus(z):
    return jnp.maximum(z, 0.0) + _log1p_pos(jnp.exp(-jnp.abs(z)))


def _one_minus_sq(la, a):
    t = la * (1.0 + la * (0.5 + la * (1.0 / 6.0 + la * (1.0 / 24.0 + la * (1.0 / 120.0)))))
    return jnp.where(la > -0.125, -t, 1.0 - a) * (1.0 + a)


def _rms(x):
    r = lax.rsqrt(jnp.mean(x * x, axis=-1, keepdims=True) + EPS)
    return x * r, r


def _dot(a, b):
    return jnp.dot(a, b, preferred_element_type=F32)


def _dot_nt(a, b):
    return lax.dot_general(a, b, (((1,), (1,)), ((), ())), preferred_element_type=F32)


def _dot_tn(a, b):
    return lax.dot_general(a, b, (((0,), (0,)), ((), ())), preferred_element_type=F32)


def _sum0(x):
    return jnp.sum(x, axis=0, keepdims=True)


def _norm_mod_bwd(x, g, sc, dh):
    xh, r = _rms(x)
    hn = xh * g
    dhn = dh * (1.0 + sc)
    dxh = dhn * g
    dx = r * (dxh - xh * jnp.mean(dxh * xh, axis=-1, keepdims=True))
    return dx, _sum0(dhn * xh), _sum0(dh), _sum0(dh * hn)


def _dev_index(p):
    return 4 * p[0] + 2 * p[1] + p[2]


def _mesh_pos():
    return lax.axis_index("x"), lax.axis_index("y"), lax.axis_index("c")


class _AllGather:
    def __init__(self, arrs):
        n = len(arrs)
        self.arrays = list(arrs)
        self.out_shapes = [_sds((N_DEV,) + a.shape, a.dtype) for a in arrs]
        self.scratch = ([pltpu.VMEM(a.shape, a.dtype) for a in arrs]
                        + [pltpu.SemaphoreType.DMA((7 * n,)), pltpu.SemaphoreType.DMA((7 * n,)),
                           pltpu.SemaphoreType.DMA((n,))])
        self.aliases = {}

    def _parts(self, ins, outs, scr):
        n = len(self.arrays)
        stage = scr[:n]
        send_sems, recv_sems, local_sems = scr[n:]
        x, y, c = _mesh_pos()
        me, sib = (x, y, c), (x, y, 1 - c)
        chips = [(1 - x, y), (x, 1 - y), (1 - x, 1 - y)]

        def copy(t, k, block, to, own=False):
            dst = outs[t].at[_dev_index(block)]
            return pltpu.make_async_remote_copy(
                src_ref=ins[t] if own else dst, dst_ref=dst,
                send_sem=send_sems.at[7 * t + k], recv_sem=recv_sems.at[7 * t + k],
                device_id=to, device_id_type=MESH)

        first = []
        for t in range(n):
            first.append(copy(t, 0, me, sib, own=True))
            for j, ch in enumerate(chips):
                first.append(copy(t, 1 + j, me, (*ch, c), own=True))
        stage_in = [pltpu.make_async_copy(ins[t], stage[t], local_sems.at[t]) for t in range(n)]
        mine = [pltpu.make_async_copy(stage[t], outs[t].at[_dev_index(me)], local_sems.at[t]) for t in range(n)]
        return n, c, me, sib, chips, copy, first, stage_in, mine

    def start(self, ins, outs, scr):
        n, _, _, _, _, _, first, stage_in, mine = self._parts(ins, outs, scr)
        for cp in stage_in:
            cp.start()
        for cp in first:
            cp.start()
        for t in range(n):
            stage_in[t].wait()
            mine[t].start()

    def relay(self, ins, outs, scr):
        n, c, me, sib, chips, copy, _, _, _ = self._parts(ins, outs, scr)
        for j, ch in enumerate(chips):
            for t in range(n):
                copy(t, 1 + j, (*ch, c), me).wait_recv()
                copy(t, 4 + j, (*ch, c), sib).start()

    def finish(self, ins, outs, scr):
        n, c, me, sib, chips, copy, first, _, mine = self._parts(ins, outs, scr)
        passed = [copy(t, 4 + j, (*ch, c), sib) for j, ch in enumerate(chips) for t in range(n)]
        for t in range(n):
            copy(t, 0, sib, me).wait_recv()
            for j, ch in enumerate(chips):
                copy(t, 4 + j, (*ch, 1 - c), me).wait_recv()
        for cp in first + passed:
            cp.wait_send()
        for cp in mine:
            cp.wait()


class _Exchange:
    def __init__(self, arrays, out_shapes, plan, n_copies, aliases=None):
        self.arrays = list(arrays)
        self.out_shapes = list(out_shapes)
        self.plan = plan
        self.scratch = [pltpu.SemaphoreType.DMA((n_copies,)), pltpu.SemaphoreType.DMA((n_copies,))]
        self.aliases = aliases or {}

    def _copies(self, ins, outs, scr):
        send_sems, recv_sems = scr
        snd, rcv = [], []
        for i, (src, dst, peer, lands) in enumerate(self.plan(ins, outs, _mesh_pos())):
            kw = dict(send_sem=send_sems.at[i], recv_sem=recv_sems.at[i], device_id=peer, device_id_type=MESH)
            snd.append(pltpu.make_async_remote_copy(src_ref=src, dst_ref=dst, **kw))
            rcv.append(pltpu.make_async_remote_copy(src_ref=src, dst_ref=lands, **kw))
        return snd, rcv

    def start(self, ins, outs, scr):
        for cp in self._copies(ins, outs, scr)[0]:
            cp.start()

    def relay(self, ins, outs, scr):
        pass

    def finish(self, ins, outs, scr):
        snd, rcv = self._copies(ins, outs, scr)
        for cp in rcv:
            cp.wait_recv()
        for cp in snd:
            cp.wait_send()


def _pair_exchange(grads):
    n = len(grads)

    def plan(ins, outs, pos):
        x, y, c = pos
        return [(ins[t].at[2 * j + (1 - c)], outs[t].at[j], (x, y, 1 - c), outs[t].at[j])
                for t in range(n) for j in range(N_CHIP)]

    return _Exchange(grads, [_sds((N_CHIP,) + g.shape[1:], g.dtype) for g in grads], plan, N_CHIP * n)


def _chip_exchange(parts, rows=None):
    n = len(parts)

    def plan(ins, outs, pos):
        x, y, c = pos
        chips = [(1 - x, y), (x, 1 - y), (1 - x, 1 - y)]

        def src(t, ch):
            blk = ins[t].at[2 * ch[0] + ch[1]]
            return blk if rows is None else blk.at[pl.ds(rows[0], rows[1])]

        return [(src(t, ch), outs[t].at[k], (*ch, c), outs[t].at[k]) for t in range(n) for k, ch in enumerate(chips)]

    shapes = [_sds((3, p.shape[1] if rows is None else rows[1]) + p.shape[2:], p.dtype) for p in parts]
    return _Exchange(parts, shapes, plan, 3 * n)


def _pair_gather(bufs):
    n = len(bufs)

    def plan(ins, outs, pos):
        x, y, c = pos
        return [(ins[t].at[c], outs[t].at[c], (x, y, 1 - c), outs[t].at[1 - c]) for t in range(n)]

    return _Exchange(bufs, [_sds(b.shape, b.dtype) for b in bufs], plan, n, aliases={t: t for t in range(n)})


def _run_comms(comms, name):
    c_in = [len(cm.arrays) for cm in comms]
    c_out = [len(cm.out_shapes) for cm in comms]
    c_scr = [len(cm.scratch) for cm in comms]
    aliases = {}
    for k, cm in enumerate(comms):
        for a, b in cm.aliases.items():
            aliases[sum(c_in[:k]) + a] = sum(c_out[:k]) + b

    def split(refs, counts):
        out, pos = [], 0
        for cnt in counts:
            out.append(refs[pos:pos + cnt])
            pos += cnt
        return out

    def body(*refs):
        ins = split(refs[:sum(c_in)], c_in)
        outs = split(refs[sum(c_in):sum(c_in) + sum(c_out)], c_out)
        scr = split(refs[sum(c_in) + sum(c_out):], c_scr)
        for phase in ("start", "relay", "finish"):
            for k, cm in enumerate(comms):
                getattr(cm, phase)(ins[k], outs[k], scr[k])

    outs = _pc(body, name=name, out_shape=[s for cm in comms for s in cm.out_shapes],
               in_specs=[ANY] * sum(c_in), out_specs=[ANY] * sum(c_out), input_output_aliases=aliases,
               scratch_shapes=[s for cm in comms for s in cm.scratch],
               compiler_params=_params())(*[a for cm in comms for a in cm.arrays])
    return split(list(outs), c_out)


def _all_gather(arrs, name):
    return _run_comms([_AllGather(arrs)], name)[0]


def _call(body, *, name, grid, in_specs, out_specs, out_shape, scratch_shapes, sem, args, comms=()):
    n_in, n_out, n_scr = len(in_specs), len(out_specs), len(scratch_shapes)
    c_in = [len(cm.arrays) for cm in comms]
    c_out = [len(cm.out_shapes) for cm in comms]
    c_scr = [len(cm.scratch) for cm in comms]
    aliases = {}
    for k, cm in enumerate(comms):
        for a, b in cm.aliases.items():
            aliases[n_in + sum(c_in[:k]) + a] = n_out + sum(c_out[:k]) + b

    def split(refs, counts):
        out, pos = [], 0
        for cnt in counts:
            out.append(refs[pos:pos + cnt])
            pos += cnt
        return out

    def wrapped(*refs):
        ins = refs[:n_in + sum(c_in)]
        outs = refs[len(ins):len(ins) + n_out + sum(c_out)]
        scr = refs[len(ins) + len(outs):]
        cins, couts, cscr = split(ins[n_in:], c_in), split(outs[n_out:], c_out), split(scr[n_scr:], c_scr)
        if comms:
            first = pl.program_id(0) == 0
            last = pl.program_id(0) == grid[0] - 1
            for k in range(1, len(grid)):
                first = jnp.logical_and(first, pl.program_id(k) == 0)
                last = jnp.logical_and(last, pl.program_id(k) == grid[k] - 1)

            @pl.when(first)
            def _():
                for k, cm in enumerate(comms):
                    cm.start(cins[k], couts[k], cscr[k])
        body(*ins[:n_in], *outs[:n_out], *scr[:n_scr])
        if comms:
            relay_early = len(grid) == 1 and grid[0] >= 4
            if relay_early:
                @pl.when(pl.program_id(0) == (7 * grid[0]) // 8 - 1)
                def _():
                    for k, cm in enumerate(comms):
                        cm.relay(cins[k], couts[k], cscr[k])

            @pl.when(last)
            def _():
                for k, cm in enumerate(comms):
                    if not relay_early:
                        cm.relay(cins[k], couts[k], cscr[k])
                    cm.finish(cins[k], couts[k], cscr[k])

    outs = _pc(wrapped, name=name, grid=grid,
               in_specs=list(in_specs) + [ANY] * sum(c_in), out_specs=list(out_specs) + [ANY] * sum(c_out),
               out_shape=list(out_shape) + [s for cm in comms for s in cm.out_shapes],
               scratch_shapes=list(scratch_shapes) + [s for cm in comms for s in cm.scratch],
               input_output_aliases=aliases, compiler_params=_params(*sem),
               )(*args, *[a for cm in comms for a in cm.arrays])
    outs = list(outs)
    return outs[:n_out], split(outs[n_out:], c_out)


def _row_block(r):
    for b in (512, 256, 128, 64, 32, 16, 8):
        if r % b == 0:
            return b
    return r


def _pair_add(g, recv, cj_idx, name):
    _, r, cc = g.shape
    br = _row_block(r)

    def body(cj_ref, g_ref, r_ref, own_ref, pb_ref):
        s = g_ref[...] + r_ref[...]
        pb_ref[...] = s.astype(BF16)

        @pl.when(pl.program_id(1) == cj_ref[1])
        def _():
            own_ref[...] = s[0]

    grid_spec = pltpu.PrefetchScalarGridSpec(
        num_scalar_prefetch=1, grid=(r // br, N_CHIP),
        in_specs=[pl.BlockSpec((1, br, cc), lambda i, j, cj_ref: (2 * j + cj_ref[0], i, 0)),
                  pl.BlockSpec((1, br, cc), lambda i, j, cj_ref: (j, i, 0))],
        out_specs=[pl.BlockSpec((br, cc), lambda i, j, cj_ref: (i, 0)),
                   pl.BlockSpec((1, br, cc), lambda i, j, cj_ref: (j, i, 0))])
    return _pc(body, name=name, grid_spec=grid_spec,
               out_shape=[_sds((r, cc)), _sds((N_CHIP, r, cc), BF16)],
               compiler_params=_params("arbitrary", "arbitrary"))(cj_idx, g, recv)


def _chip_add(p, qs, cj_idx, name):
    r, cc = p.shape
    nq = len(qs)
    br = _row_block(r // nq)
    nb = r // nq // br

    def body(cj_ref, p_ref, *refs):
        o_ref = refs[-1]
        if nq == 2:
            top = pl.program_id(0) < nb
            q = [jnp.where(top, refs[0][k], refs[1][k]).astype(F32) for k in range(3)]
        else:
            q = [refs[0][k].astype(F32) for k in range(3)]
        o_ref[0] = ((p_ref[...] + q[0]) + q[1]) + q[2]

    q_specs = [pl.BlockSpec((3, br, cc), lambda i, cj_ref, h=h: (0, jnp.clip(i - h * nb, 0, nb - 1), 0))
               for h in range(nq)]
    grid_spec = pltpu.PrefetchScalarGridSpec(
        num_scalar_prefetch=1, grid=(r // br,),
        in_specs=[pl.BlockSpec((br, cc), lambda i, cj_ref: (i, 0))] + q_specs,
        out_specs=pl.BlockSpec((1, br, cc), lambda i, cj_ref: (cj_ref[0], i, 0)))
    return _pc(body, name=name, grid_spec=grid_spec, out_shape=_sds((2, r, cc)),
               compiler_params=_params("arbitrary"))(cj_idx, p, *qs)


def _shard_of(both):
    return both.reshape((2 * both.shape[1],) + both.shape[2:])


def _adamw(w, g, m, v, name):
    r, cc = w.shape
    br = _row_block(r)
    if r * cc * 4 <= (1 << 20):
        br = r
    elif br * cc * 4 > (1 << 20) and br > 8:
        br = max(8, (1 << 20) // (cc * 4) // 8 * 8)
        while r % br:
            br -= 8
    c1 = 1.0 - ADAM_B1 ** ADAM_STEP
    c2 = 1.0 - ADAM_B2 ** ADAM_STEP

    def body(w_ref, g_ref, m_ref, v_ref, go_ref, d_ref, mo_ref, vo_ref):
        gg = g_ref[...]
        go_ref[...] = gg
        mn = ADAM_B1 * m_ref[...] + (1.0 - ADAM_B1) * gg
        vn = ADAM_B2 * v_ref[...] + (1.0 - ADAM_B2) * (gg * gg)
        mh = mn / c1
        vh = vn / c2
        d_ref[...] = -ADAM_LR * (mh / (jnp.sqrt(vh) + ADAM_EPS) + ADAM_WD * w_ref[...])
        mo_ref[...] = mn
        vo_ref[...] = vn

    spec = pl.BlockSpec((br, cc), lambda i: (i, 0))
    return _pc(body, name=name, grid=(r // br,), in_specs=[spec] * 4, out_specs=[spec] * 4,
               out_shape=[_sds((r, cc))] * 4, compiler_params=_params("arbitrary"))(w, g, m, v)


def _head(w_half, c8, small, w_ada, b_shard, c_ctx, ret_decay):
    ada_n = w_ada.shape[1]
    mod_sds = _sds((16, ada_n))
    ag_w, ag_c, ag_m = _AllGather([w_half]), _AllGather([c8, small]), _AllGather([mod_sds])
    n_w, n_c, n_m = len(ag_w.scratch), len(ag_c.scratch), len(ag_m.scratch)

    def body(w_ref, c_ref, s_ref, wada_ref, b_ref, cc_ref, rd_ref,
             gw_ref, call_ref, sall_ref, a_ref, modp_ref, mall_ref, lg_ref, sg_ref, *scr):
        scr_w, scr_c, scr_m = scr[:n_w], scr[n_w:n_w + n_c], scr[n_w + n_c:n_w + n_c + n_m]
        c_v, w_v, m_v, sems = scr[n_w + n_c + n_m:]
        ag_w.start((w_ref,), (gw_ref,), scr_w)
        ag_c.start((c_ref, s_ref), (call_ref, sall_ref), scr_c)
        load_w = pltpu.make_async_copy(wada_ref, w_v, sems.at[0])
        load_w.start()
        rd = rd_ref[...]
        lg_ref[...] = -_softplus(-rd)
        sg_ref[...] = _sigmoid(-rd)
        ag_c.relay((c_ref, s_ref), (call_ref, sall_ref), scr_c)
        ag_c.finish((c_ref, s_ref), (call_ref, sall_ref), scr_c)
        load_c = pltpu.make_async_copy(call_ref, c_v, sems.at[1])
        load_c.start()
        load_c.wait()
        a_ref[...] = jnp.zeros_like(a_ref)
        for d in range(N_DEV):
            cd = c_v[d, 0:1, :]
            a_ref[d:d + 1, :] = cd * _sigmoid(cd)
        cc = cc_ref[...]
        a_ref[N_DEV:N_DEV + 1, :] = cc * _sigmoid(cc)
        load_w.wait()
        m_v[...] = jnp.dot(a_ref[...], w_v[...], preferred_element_type=F32,
                           precision=lax.Precision.HIGHEST) + b_ref[...]
        put = pltpu.make_async_copy(m_v, modp_ref, sems.at[2])
        put.start()
        put.wait()
        ag_m.start((modp_ref,), (mall_ref,), scr_m)
        ag_m.relay((modp_ref,), (mall_ref,), scr_m)
        ag_m.finish((modp_ref,), (mall_ref,), scr_m)
        ag_w.relay((w_ref,), (gw_ref,), scr_w)
        ag_w.finish((w_ref,), (gw_ref,), scr_w)

    rd = jnp.broadcast_to(ret_decay.reshape(2, HEADS).T[:, :, None], (HEADS, 2, LANES))
    lane = _full((HEADS, 2, LANES))
    outs = _pc(
        body, name="head",
        in_specs=[ANY, ANY, ANY, ANY, _full((1, ada_n)), _full((1, D_MODEL)), lane],
        out_specs=[ANY, ANY, ANY, _full((16, D_MODEL)), ANY, ANY, lane, lane],
        out_shape=ag_w.out_shapes + ag_c.out_shapes + [_sds((16, D_MODEL)), mod_sds] + ag_m.out_shapes
        + [_sds((HEADS, 2, LANES))] * 2,
        scratch_shapes=ag_w.scratch + ag_c.scratch + ag_m.scratch
        + [pltpu.VMEM((N_DEV,) + c8.shape, F32), pltpu.VMEM(w_ada.shape, F32), pltpu.VMEM((16, ada_n), F32),
           pltpu.SemaphoreType.DMA((3,))],
        compiler_params=_params(),
    )(w_half, c8, small, w_ada, b_shard, c_ctx.reshape(1, D_MODEL), rd)
    gw, c_all, small_all, a16, _, mod_all, lgv, sgv = outs
    return gw, c_all, small_all, a16, mod_all, lgv, sgv


def _ada_grad(at, b):
    n = b.shape[1]
    bn = 512

    def body(a_ref, b_ref, o_ref):
        o_ref[...] = jnp.dot(a_ref[...], b_ref[...], preferred_element_type=F32, precision=lax.Precision.HIGHEST)

    return _pc(body, name="ada_grad", grid=(n // bn,),
               in_specs=[_full((D_MODEL, LANES)), pl.BlockSpec((LANES, bn), lambda i: (0, i))],
               out_specs=pl.BlockSpec((D_MODEL, bn), lambda i: (0, i)), out_shape=_sds((D_MODEL, n)),
               compiler_params=_params("arbitrary"))(at, b)


def _cctx_partial(dmc8, w_ada):
    n = w_ada.shape[1]
    bn = 512

    def body(d_ref, w_ref, o_ref):
        @pl.when(pl.program_id(0) == 0)
        def _():
            o_ref[...] = jnp.zeros_like(o_ref)
        o_ref[...] += lax.dot_general(d_ref[...], w_ref[...], (((1,), (1,)), ((), ())),
                                      preferred_element_type=F32, precision=lax.Precision.HIGHEST)

    return _pc(body, name="cctx_partial", grid=(n // bn,),
               in_specs=[pl.BlockSpec((8, bn), lambda i: (0, i)), pl.BlockSpec((D_MODEL, bn), lambda i: (0, i))],
               out_specs=_full((8, D_MODEL)), out_shape=_sds((8, D_MODEL)),
               compiler_params=_params("arbitrary"))(dmc8, w_ada)


def _cctx_final(parts, c_ctx, m, v):
    c1 = 1.0 - ADAM_B1 ** ADAM_STEP
    c2 = 1.0 - ADAM_B2 ** ADAM_STEP

    def body(p_ref, c_ref, m_ref, v_ref, g_ref, d_ref, mo_ref, vo_ref):
        s = ((p_ref[0, 0:1, :] + p_ref[2, 0:1, :]) + p_ref[4, 0:1, :]) + p_ref[6, 0:1, :]
        z = c_ref[...]
        sg = _sigmoid(z)
        gg = s * (sg * (1.0 + z * (1.0 - sg)))
        g_ref[...] = gg
        mn = ADAM_B1 * m_ref[...] + (1.0 - ADAM_B1) * gg
        vn = ADAM_B2 * v_ref[...] + (1.0 - ADAM_B2) * (gg * gg)
        d_ref[...] = -ADAM_LR * ((mn / c1) / (jnp.sqrt(vn / c2) + ADAM_EPS) + ADAM_WD * z)
        mo_ref[...] = mn
        vo_ref[...] = vn

    row = _full((1, D_MODEL))
    return _pc(body, name="cctx_final", out_shape=[_sds((1, D_MODEL))] * 4,
               in_specs=[_full(parts.shape), row, row, row], out_specs=[row] * 4,
               compiler_params=_params())(parts, c_ctx.reshape(1, D_MODEL), m.reshape(1, D_MODEL), v.reshape(1, D_MODEL))


def _rotary_tables(t_len):
    rows = t_len // GRID_W
    n_freq = DH // 4
    inv = ROPE_BASE ** (-jnp.arange(n_freq, dtype=F32) / n_freq)
    row_ang = jnp.arange(rows, dtype=F32)[:, None] * inv
    col_ang = jnp.arange(GRID_W, dtype=F32)[:, None] * inv

    def spread(fn):
        return jnp.concatenate([jnp.repeat(fn(row_ang), GRID_W, axis=0), jnp.tile(fn(col_ang), (rows, 1))], axis=-1)

    cos, sin = spread(jnp.cos), spread(jnp.sin)
    return jnp.concatenate([cos, cos], axis=-1), jnp.concatenate([-sin, sin], axis=-1)


def _inproj_fwd(x, gn, sh, sc, w4, cos2, sin2, name, comms=()):
    t = x.shape[0]
    tm = _tile(t, True)
    nc = IN_COLS // N_CHIP

    def body(x_ref, gn_ref, sh_ref, sc_ref, w_ref, c_ref, s_ref, p_ref, xr_ref, hb_ref, p_s):
        xh, _ = _rms(x_ref[...])
        h = xh * gn_ref[...] * (1.0 + sc_ref[...]) + sh_ref[...]
        hb = h.astype(BF16)
        hb_ref[...] = hb
        for j in range(N_CHIP):
            p_s[:, nc * j:nc * (j + 1)] = _dot(hb, w_ref[j])
        cc = c_ref[...]
        ss = s_ref[...]
        for hh in range(2 * HEADS):
            blk = p_s[:, DH * hh:DH * (hh + 1)]
            rot = blk * cc + pltpu.roll(blk, DH // 2, 1) * ss
            if hh >= HEADS:
                rot = rot * K_SCALE
            p_ref[:, DH * hh:DH * (hh + 1)] = rot.astype(BF16)
        p_ref[:, 2 * RET_W:] = p_s[:, 2 * RET_W:].astype(BF16)
        xr_ref[...] = p_s[:, 4 * RET_W:4 * RET_W + LRU_W]

    row = _full((1, D_MODEL))
    outs, couts = _call(
        body, name=name, grid=(t // tm,),
        in_specs=[pl.BlockSpec((tm, D_MODEL), lambda i: (i, 0)), row, row, row, _full(w4.shape),
                  pl.BlockSpec((tm, DH), lambda i: (i, 0)), pl.BlockSpec((tm, DH), lambda i: (i, 0))],
        out_specs=[pl.BlockSpec((tm, IN_COLS), lambda i: (i, 0)), pl.BlockSpec((tm, LRU_W), lambda i: (i, 0)),
                   pl.BlockSpec((tm, D_MODEL), lambda i: (i, 0))],
        out_shape=[_sds((t, IN_COLS), BF16), _sds((t, LRU_W)), _sds((t, D_MODEL), BF16)],
        scratch_shapes=[pltpu.VMEM((tm, IN_COLS), F32)], sem=("arbitrary",),
        args=(x, gn, sh, sc, w4, cos2, sin2), comms=comms)
    return (outs, couts) if comms else outs


def _inproj_bwd(x, gn, sh, sc, w4, cos2, sin2, pieces, dres, name):
    t = x.shape[0]
    tm = _tile(t)
    nc = IN_COLS // N_CHIP

    def body(x_ref, gn_ref, sh_ref, sc_ref, w_ref, c_ref, s_ref, dqf, dqb, dkf, dkb, dvf, dvb, dg, dxr, dgt, dres_ref,
             dx_ref, dpb_ref, dgn_ref, dsh_ref, dsc_ref):
        cc = c_ref[...]
        ss = s_ref[...]
        dq = dqf[...].astype(F32) + dqb[...].astype(F32)
        dk = dkf[...].astype(F32) + dkb[...].astype(F32)
        for hh in range(HEADS):
            sl = slice(DH * hh, DH * (hh + 1))
            b = dq[:, sl]
            dpb_ref[:, sl] = (b * cc + pltpu.roll(b * ss, DH // 2, 1)).astype(BF16)
            b = dk[:, sl]
            dpb_ref[:, RET_W + DH * hh:RET_W + DH * (hh + 1)] = (
                (b * cc + pltpu.roll(b * ss, DH // 2, 1)) * K_SCALE).astype(BF16)
        dpb_ref[:, 2 * RET_W:3 * RET_W] = (dvf[...].astype(F32) + dvb[...].astype(F32)).astype(BF16)
        dpb_ref[:, 3 * RET_W:4 * RET_W] = dg[...].astype(BF16)
        dpb_ref[:, 4 * RET_W:4 * RET_W + LRU_W] = dxr[...].astype(BF16)
        dpb_ref[:, 4 * RET_W + LRU_W:IN_COLS] = dgt[...].astype(BF16)
        dh = _dot_nt(dpb_ref[:, 0:nc], w_ref[0])
        for j in range(1, N_CHIP):
            dh = dh + _dot_nt(dpb_ref[:, nc * j:nc * (j + 1)], w_ref[j])
        dx, dgn_t, dsh_t, dsc_t = _norm_mod_bwd(x_ref[...], gn_ref[...], sc_ref[...], dh)
        dx_ref[...] = dres_ref[...] + dx

        @pl.when(pl.program_id(0) == 0)
        def _():
            dgn_ref[...] = jnp.zeros_like(dgn_ref)
            dsh_ref[...] = jnp.zeros_like(dsh_ref)
            dsc_ref[...] = jnp.zeros_like(dsc_ref)
        dgn_ref[...] += dgn_t
        dsh_ref[...] += dsh_t
        dsc_ref[...] += dsc_t

    row = _full((1, D_MODEL))
    pc = pl.BlockSpec((tm, RET_W), lambda i: (i, 0))
    big = pl.BlockSpec((tm, D_MODEL), lambda i: (i, 0))
    return _pc(body, name=name, grid=(t // tm,),
               in_specs=[big, row, row, row, _full(w4.shape),
                         pl.BlockSpec((tm, DH), lambda i: (i, 0)), pl.BlockSpec((tm, DH), lambda i: (i, 0))]
               + [pc] * 9 + [big],
               out_specs=[big, pl.BlockSpec((tm, IN_COLS), lambda i: (i, 0)), row, row, row],
               out_shape=[_sds((t, D_MODEL)), _sds((t, IN_COLS), BF16), _sds((1, D_MODEL)), _sds((1, D_MODEL)),
                          _sds((1, D_MODEL))],
               compiler_params=_params("arbitrary"))(x, gn, sh, sc, w4, cos2, sin2, *pieces, dres)


def _halo_specs(t, tm):
    n8 = tm // SUBLANES
    last8 = t // SUBLANES - 1
    prev = pl.BlockSpec((SUBLANES, LRU_W), lambda i: (jnp.maximum(i * n8 - 1, 0), 0))
    main = pl.BlockSpec((tm, LRU_W), lambda i: (i, 0))
    nxt = pl.BlockSpec((SUBLANES, LRU_W), lambda i: (jnp.minimum((i + 1) * n8, last8), 0))
    return prev, main, nxt


def _with_halo(prev_ref, main_ref, next_ref, i, nt):
    prev = jnp.where(i > 0, prev_ref[...], 0.0)
    nxt = jnp.where(i < nt - 1, next_ref[...], 0.0)
    return jnp.concatenate([prev, main_ref[...], nxt], axis=0)


def _conv_fwd(xr, cw, cb, name):
    t = xr.shape[0]
    tm = _tile(t, True)
    nt = t // tm
    n = tm + 2 * SUBLANES
    mid = slice(SUBLANES, SUBLANES + tm)

    def body(p_ref, m_ref, n_ref, w_ref, b_ref, o_ref):
        xp = _with_halo(p_ref, m_ref, n_ref, pl.program_id(0), nt)
        acc = b_ref[...] + pltpu.roll(xp, 1, 0)[mid] * w_ref[0:1, :]
        acc = acc + xp[mid] * w_ref[1:2, :]
        acc = acc + pltpu.roll(xp, n - 1, 0)[mid] * w_ref[2:3, :]
        acc = acc + pltpu.roll(xp, n - 2, 0)[mid] * w_ref[3:4, :]
        o_ref[...] = acc

    return _pc(body, name=name, grid=(nt,),
               in_specs=[*_halo_specs(t, tm), _full((4, LRU_W)), _full((1, LRU_W))],
               out_specs=pl.BlockSpec((tm, LRU_W), lambda i: (i, 0)), out_shape=_sds((t, LRU_W)),
               compiler_params=_params("arbitrary"))(xr, xr, xr, cw, cb)


def _conv_bwd(dxc_a, dxc_b, xr, cw, name):
    t = xr.shape[0]
    tm = _tile(t, True)
    nt = t // tm
    n = tm + 2 * SUBLANES
    mid = slice(SUBLANES, SUBLANES + tm)

    def body(ap_ref, am_ref, an_ref, bp_ref, bm_ref, bn_ref, xp_ref, xm_ref, xn_ref, w_ref, dx_ref, dw_ref, db_ref):
        i = pl.program_id(0)
        dp = _with_halo(ap_ref, am_ref, an_ref, i, nt) + _with_halo(bp_ref, bm_ref, bn_ref, i, nt)
        xp = _with_halo(xp_ref, xm_ref, xn_ref, i, nt)
        dx = pltpu.roll(dp, n - 1, 0)[mid] * w_ref[0:1, :]
        dx = dx + dp[mid] * w_ref[1:2, :]
        dx = dx + pltpu.roll(dp, 1, 0)[mid] * w_ref[2:3, :]
        dx = dx + pltpu.roll(dp, 2, 0)[mid] * w_ref[3:4, :]
        dx_ref[...] = dx.astype(BF16)
        d = dp[mid]

        @pl.when(i == 0)
        def _():
            dw_ref[...] = jnp.zeros_like(dw_ref)
            db_ref[...] = jnp.zeros_like(db_ref)
        dw_ref[0:1, :] += _sum0(d * pltpu.roll(xp, 1, 0)[mid])
        dw_ref[1:2, :] += _sum0(d * xp[mid])
        dw_ref[2:3, :] += _sum0(d * pltpu.roll(xp, n - 1, 0)[mid])
        dw_ref[3:4, :] += _sum0(d * pltpu.roll(xp, n - 2, 0)[mid])
        db_ref[...] += _sum0(d)

    return _pc(body, name=name, grid=(nt,),
               in_specs=[*_halo_specs(t, tm), *_halo_specs(t, tm), *_halo_specs(t, tm), _full((4, LRU_W))],
               out_specs=[pl.BlockSpec((tm, LRU_W), lambda i: (i, 0)), _full((4, LRU_W)), _full((1, LRU_W))],
               out_shape=[_sds((t, LRU_W), BF16), _sds((4, LRU_W)), _sds((1, LRU_W))],
               compiler_params=_params("arbitrary"))(dxc_a, dxc_a, dxc_a, dxc_b, dxc_b, dxc_b, xr, xr, xr, cw)


def _local_scan(a, b, reverse):
    n = a.shape[0]
    row = lax.broadcasted_iota(jnp.int32, a.shape, 0) & (SUBLANES - 1)
    for s in (1, 2, 4):
        if reverse:
            a_s, b_s, ok = pltpu.roll(a, n - s, 0), pltpu.roll(b, n - s, 0), row < SUBLANES - s
        else:
            a_s, b_s, ok = pltpu.roll(a, s, 0), pltpu.roll(b, s, 0), row >= s
        b = a * jnp.where(ok, b_s, 0.0) + b
        a = a * jnp.where(ok, a_s, 1.0)
    return a, b


def _carry_scan(a_s, b_s, out_ref, carry, reverse):
    ng = a_s.shape[0] // SUBLANES
    shape = carry.shape

    def step(g, cr):
        gg = (ng - 1 - g) if reverse else g
        off = pl.multiple_of(gg * SUBLANES, SUBLANES)
        h = a_s[pl.ds(off, SUBLANES), :] * cr + b_s[pl.ds(off, SUBLANES), :]
        out_ref[pl.ds(off, SUBLANES), :] = h
        edge = h[0:1, :] if reverse else h[SUBLANES - 1:SUBLANES, :]
        return jnp.broadcast_to(edge, shape)

    return lax.fori_loop(0, ng, step, carry)


def _lru_gates(xc, wa_ref, wx_ref, ba, bx, lam):
    xb = xc.astype(BF16)
    r = _sigmoid(_dot(xb, wa_ref[...]) + ba)
    ig = _sigmoid(_dot(xb, wx_ref[...]) + bx)
    sp = _softplus(-lam)
    la = -LRU_C * r * sp
    a = jnp.exp(la)
    mult = jnp.sqrt(_one_minus_sq(la, a))
    return r, ig, sp, a, mult


def _lru_fwd(xc, wa, wx, ba, bx, lam, h0, reverse, name, comms=()):
    t = xc.shape[0]
    tm = _tile(t, True)
    nt = t // tm
    tidx = (lambda i: (nt - 1 - i, 0)) if reverse else (lambda i: (i, 0))

    def body(x_ref, wa_ref, wx_ref, ba_ref, bx_ref, lam_ref, h0_ref, h_ref, a_s, b_s, c_s):
        @pl.when(pl.program_id(0) == 0)
        def _():
            c_s[...] = jnp.broadcast_to(h0_ref[...], c_s.shape)
        xv = x_ref[...]
        _, ig, _, a, mult = _lru_gates(xv, wa_ref, wx_ref, ba_ref[...], bx_ref[...], lam_ref[...])
        al, bl = _local_scan(a, mult * (ig * xv), reverse)
        a_s[...] = al
        b_s[...] = bl
        c_s[...] = _carry_scan(a_s, b_s, h_ref, c_s[...], reverse)

    vec = _full((1, LRU_W))
    mat = _full((LRU_W, LRU_W))
    (h,), couts = _call(body, name=name, grid=(nt,),
                        in_specs=[pl.BlockSpec((tm, LRU_W), tidx), mat, mat, vec, vec, vec, vec],
                        out_specs=[pl.BlockSpec((tm, LRU_W), tidx)], out_shape=[_sds((t, LRU_W))],
                        scratch_shapes=[pltpu.VMEM((tm, LRU_W), F32), pltpu.VMEM((tm, LRU_W), F32),
                                        pltpu.VMEM((SUBLANES, LRU_W), F32)],
                        sem=("arbitrary",), args=(xc, wa, wx, ba, bx, lam, h0), comms=comms)
    return (h, couts) if comms else h


def _lru_bwd(xc, wa, wx, ba, bx, lam, h, h0, dh, reverse, name, comms=()):
    t = xc.shape[0]
    tm = _tile(t, True)
    nt = t // tm
    n8 = tm // SUBLANES
    last8 = t // SUBLANES - 1
    tidx = (lambda i: (i, 0)) if reverse else (lambda i: (nt - 1 - i, 0))
    if reverse:
        halo = pl.BlockSpec((SUBLANES, LRU_W), lambda i: (jnp.minimum((i + 1) * n8, last8), 0))
    else:
        halo = pl.BlockSpec((SUBLANES, LRU_W), lambda i: (jnp.maximum((nt - 1 - i) * n8 - 1, 0), 0))

    def body(x_ref, wa_ref, wx_ref, ba_ref, bx_ref, lam_ref, h_ref, halo_ref, h0_ref, dh_ref,
             dx_ref, dpre_ref, dba_ref, dbx_ref, dlam_ref, dh0_ref, a_s, b_s, l_s, c_s, e_s):
        i = pl.program_id(0)

        @pl.when(i == 0)
        def _():
            c_s[...] = jnp.zeros_like(c_s)
            e_s[...] = jnp.zeros_like(e_s)
            dba_ref[...] = jnp.zeros_like(dba_ref)
            dbx_ref[...] = jnp.zeros_like(dbx_ref)
            dlam_ref[...] = jnp.zeros_like(dlam_ref)
        xv = x_ref[...]
        lam = lam_ref[...]
        r, ig, sp, a, mult = _lru_gates(xv, wa_ref, wx_ref, ba_ref[...], bx_ref[...], lam)
        hv = h_ref[...]
        rowi = lax.broadcasted_iota(jnp.int32, (tm, LRU_W), 0)
        edge_a = jnp.broadcast_to(e_s[0:1, :], (tm, LRU_W))
        h0b = jnp.broadcast_to(h0_ref[...], (tm, LRU_W))
        if reverse:
            a_sh = jnp.where(rowi == 0, edge_a, pltpu.roll(a, 1, 0))
            hin_edge = jnp.where(i == nt - 1, h0b, jnp.broadcast_to(halo_ref[0:1, :], (tm, LRU_W)))
            h_in = jnp.where(rowi == tm - 1, hin_edge, pltpu.roll(hv, tm - 1, 0))
        else:
            a_sh = jnp.where(rowi == tm - 1, edge_a, pltpu.roll(a, tm - 1, 0))
            hin_edge = jnp.where(i == nt - 1, h0b, jnp.broadcast_to(halo_ref[SUBLANES - 1:SUBLANES, :], (tm, LRU_W)))
            h_in = jnp.where(rowi == 0, hin_edge, pltpu.roll(hv, 1, 0))
        al, bl = _local_scan(a_sh, dh_ref[...], not reverse)
        a_s[...] = al
        b_s[...] = bl
        c_s[...] = _carry_scan(a_s, b_s, l_s, c_s[...], not reverse)
        e_s[...] = jnp.broadcast_to(a[tm - 1:tm, :] if reverse else a[0:1, :], e_s.shape)
        lmb = l_s[...]
        da = lmb * h_in
        ixc = ig * xv
        dmult = lmb * ixc
        dixc = lmb * mult
        dla = da * a - dmult * (a * a) / mult
        dpr = dla * (-LRU_C * sp) * r * (1.0 - r)
        dpi = dixc * xv * ig * (1.0 - ig)
        dprb = dpr.astype(BF16)
        dpib = dpi.astype(BF16)
        dpre_ref[:, 0:LRU_W] = dprb
        dpre_ref[:, LRU_W:2 * LRU_W] = dpib
        dx_ref[...] = dixc * ig + _dot_nt(dprb, wa_ref[...]) + _dot_nt(dpib, wx_ref[...])
        dba_ref[...] += _sum0(dpr)
        dbx_ref[...] += _sum0(dpi)
        dlam_ref[...] += _sum0(dla * (-LRU_C * r)) * (-_sigmoid(-lam))

        @pl.when(i == nt - 1)
        def _():
            al0 = a * lmb
            dh0_ref[...] = al0[tm - 1:tm, :] if reverse else al0[0:1, :]

    vec = _full((1, LRU_W))
    mat = _full((LRU_W, LRU_W))
    tile = pl.BlockSpec((tm, LRU_W), tidx)
    return _call(body, name=name, grid=(nt,),
                 in_specs=[tile, mat, mat, vec, vec, vec, tile, halo, vec, tile],
                 out_specs=[tile, pl.BlockSpec((tm, 2 * LRU_W), tidx), vec, vec, vec, vec],
                 out_shape=[_sds((t, LRU_W)), _sds((t, 2 * LRU_W), BF16), _sds((1, LRU_W)), _sds((1, LRU_W)),
                            _sds((1, LRU_W)), _sds((1, LRU_W))],
                 scratch_shapes=[pltpu.VMEM((tm, LRU_W), F32), pltpu.VMEM((tm, LRU_W), F32),
                                 pltpu.VMEM((tm, LRU_W), F32), pltpu.VMEM((SUBLANES, LRU_W), F32),
                                 pltpu.VMEM((SUBLANES, LRU_W), F32)],
                 sem=("arbitrary",), args=(xc, wa, wx, ba, bx, lam, h, h, h0, dh), comms=comms)


def _decay_tables(lg, reverse):
    ci = lax.broadcasted_iota(jnp.int32, (CHUNK, CHUNK), 0).astype(F32)
    mi = lax.broadcasted_iota(jnp.int32, (CHUNK, CHUNK), 1).astype(F32)
    rel = (mi - ci) if reverse else (ci - mi)
    relc = jnp.maximum(rel, 0.0)
    lg_c = jnp.concatenate([lg] * (CHUNK // LANES), axis=1)
    dm = jnp.where(rel >= 0, jnp.exp(lg_c * relc), 0.0)
    cd = lax.broadcasted_iota(jnp.int32, (CHUNK, DH), 0).astype(F32)
    pq, ps = (CHUNK - cd, cd) if reverse else (cd + 1.0, CHUNK - 1.0 - cd)
    return relc, dm, jnp.exp(lg * pq), jnp.exp(lg * ps), jnp.exp(lg * float(CHUNK)), pq, ps


def _ret_fwd(proj, lgv, s0f, s0b, comms=()):
    t = proj.shape[0]
    n = t // CHUNK

    def one(q, k, v, lg, s_s, hh, o_ref, sp_ref, reverse):
        _, dm, wq, ws, g, _, _ = _decay_tables(lg, reverse)
        vb = v.astype(BF16)
        p = _dot_nt(q.astype(BF16), k.astype(BF16)) * dm
        s = s_s[hh]
        sp_ref[hh, 0] = s
        o_ref[:, DH * hh:DH * (hh + 1)] = _dot(p.astype(BF16), vb) + _dot((q * wq).astype(BF16), s.astype(BF16))
        s_s[hh] = g * s + _dot_tn((k * ws).astype(BF16), vb)

    def body(qf, kf, vf, qb, kb, vb, lg_ref, s0f_ref, s0b_ref, of_ref, ob_ref, spf_ref, spb_ref, sf_s, sb_s):
        @pl.when(pl.program_id(0) == 0)
        def _():
            sf_s[...] = s0f_ref[...]
            sb_s[...] = s0b_ref[...]
        for hh in range(HEADS):
            sl = slice(DH * hh, DH * (hh + 1))
            one(qf[:, sl].astype(F32), kf[:, sl].astype(F32), vf[:, sl], lg_ref[hh, 0:1, :], sf_s, hh, of_ref, spf_ref,
                False)
            one(qb[:, sl].astype(F32), kb[:, sl].astype(F32), vb[:, sl], lg_ref[hh, 1:2, :], sb_s, hh, ob_ref, spb_ref,
                True)

    blk = (CHUNK, RET_W)
    fw = [pl.BlockSpec(blk, lambda i, o=o: (i, o)) for o in range(3)]
    bw = [pl.BlockSpec(blk, lambda i, o=o: (n - 1 - i, o)) for o in range(3)]
    st = _full((HEADS, DH, DH))
    return _call(body, name="ret_fwd", grid=(n,),
                 in_specs=fw + bw + [_full((HEADS, 2, LANES)), st, st],
                 out_specs=[pl.BlockSpec(blk, lambda i: (i, 0)), pl.BlockSpec(blk, lambda i: (n - 1 - i, 0)),
                            pl.BlockSpec((HEADS, 1, DH, DH), lambda i: (0, i, 0, 0)),
                            pl.BlockSpec((HEADS, 1, DH, DH), lambda i: (0, n - 1 - i, 0, 0))],
                 out_shape=[_sds((t, RET_W)), _sds((t, RET_W)), _sds((HEADS, n, DH, DH)), _sds((HEADS, n, DH, DH))],
                 scratch_shapes=[pltpu.VMEM((HEADS, DH, DH), F32), pltpu.VMEM((HEADS, DH, DH), F32)],
                 sem=("arbitrary",), args=(proj, proj, proj, proj, proj, proj, lgv, s0f, s0b), comms=comms)


def _ret_bwd(proj, lgv, sgv, spf, spb, do, comms=()):
    t = proj.shape[0]
    n = t // CHUNK

    def one(q_ref, k_ref, v_ref, lg_ref, s_ref, do_ref, dq_ref, dk_ref, dv_ref, ds_s, acc_s, reverse):
        d = 1 if reverse else 0
        for hh in range(HEADS):
            sl = slice(DH * hh, DH * (hh + 1))
            relc, dm, wq, ws, g, pq, ps = _decay_tables(lg_ref[hh, d:d + 1, :], reverse)
            qb, kb, vb = q_ref[:, sl], k_ref[:, sl], v_ref[:, sl]
            q, k = qb.astype(F32), kb.astype(F32)
            p = _dot_nt(qb, kb) * dm
            s = s_ref[hh, 0]
            dob = do_ref[:, sl].astype(BF16)
            dsn = ds_s[hh]
            dsb = dsn.astype(BF16)
            dv_ref[:, sl] = (_dot_tn(p.astype(BF16), dob) + _dot((k * ws).astype(BF16), dsb)).astype(BF16)
            dp = _dot_nt(dob, vb)
            dab = (dp * dm).astype(BF16)
            xq = _dot_nt(dob, s.astype(BF16))
            yk = _dot_nt(vb, dsb)
            dq_ref[:, sl] = (_dot(dab, kb) + xq * wq).astype(BF16)
            dk_ref[:, sl] = (_dot_tn(dab, qb) + yk * ws).astype(BF16)
            ds_s[hh] = g * dsn + _dot_tn((q * wq).astype(BF16), dob)
            s_mask = _sum0(dp * p * relc)
            part = (sum(s_mask[:, LANES * u:LANES * (u + 1)] for u in range(CHUNK // LANES))
                    + _sum0(xq * q * wq * pq) + _sum0(yk * k * ws * ps) + _sum0(dsn * s) * g * float(CHUNK))
            acc_s[hh] += jnp.broadcast_to(part, (SUBLANES, LANES))

    def body(qf, kf, vf, qb, kb, vb, lg_ref, sg_ref, sf_ref, sb_ref, dof_ref, dob_ref,
             dqf, dkf, dvf, dqb, dkb, dvb, ds0f_ref, ds0b_ref, drdf_ref, drdb_ref, dsf_s, dsb_s, accf_s, accb_s):
        i = pl.program_id(0)

        @pl.when(i == 0)
        def _():
            for r in (dsf_s, dsb_s, accf_s, accb_s):
                r[...] = jnp.zeros_like(r)
        one(qf, kf, vf, lg_ref, sf_ref, dof_ref, dqf, dkf, dvf, dsf_s, accf_s, False)
        one(qb, kb, vb, lg_ref, sb_ref, dob_ref, dqb, dkb, dvb, dsb_s, accb_s, True)

        @pl.when(i == n - 1)
        def _():
            ds0f_ref[...] = dsf_s[...]
            ds0b_ref[...] = dsb_s[...]
            for d, (acc_s, drd_ref) in enumerate(((accf_s, drdf_ref), (accb_s, drdb_ref))):
                for hh in range(HEADS):
                    tot = jnp.sum(acc_s[hh, 0:1, :], axis=1, keepdims=True)
                    drd_ref[hh] = jnp.broadcast_to(tot, (SUBLANES, LANES)) * sg_ref[hh, d:d + 1, :]

    blk = (CHUNK, RET_W)
    fw = lambda o: pl.BlockSpec(blk, lambda i, o=o: (n - 1 - i, o))
    bw = lambda o: pl.BlockSpec(blk, lambda i, o=o: (i, o))
    lane = _full((HEADS, 2, LANES))
    st = _full((HEADS, DH, DH))
    rd = _full((HEADS, SUBLANES, LANES))
    outs, couts = _call(
        body, name="ret_bwd", grid=(n,),
        in_specs=[fw(0), fw(1), fw(2), bw(0), bw(1), bw(2), lane, lane,
                  pl.BlockSpec((HEADS, 1, DH, DH), lambda i: (0, n - 1 - i, 0, 0)),
                  pl.BlockSpec((HEADS, 1, DH, DH), lambda i: (0, i, 0, 0)), fw(0), bw(0)],
        out_specs=[fw(0), fw(0), fw(0), bw(0), bw(0), bw(0), st, st, rd, rd],
        out_shape=[_sds((t, RET_W), BF16)] * 6 + [_sds((HEADS, DH, DH))] * 2 + [_sds((HEADS, SUBLANES, LANES))] * 2,
        scratch_shapes=[pltpu.VMEM((HEADS, DH, DH), F32)] * 2 + [pltpu.VMEM((HEADS, SUBLANES, LANES), F32)] * 2,
        sem=("arbitrary",), args=(proj, proj, proj, proj, proj, proj, lgv, sgv, spf, spb, do, do), comms=comms)
    dqf, dkf, dvf, dqb, dkb, dvb, ds0f, ds0b, drdf, drdb = outs
    return ((dqf, dkf, dvf, ds0f, drdf), (dqb, dkb, dvb, ds0b, drdb)), couts


def _ctx_weights(lg, l_len, reverse):
    pos = lax.broadcasted_iota(jnp.int32, (l_len, DH), 0).astype(F32)
    steps = pos if reverse else (l_len - 1.0 - pos)
    return jnp.exp(lg * steps), steps


def _ctx_state_fwd(projc, lgv):
    l_len = projc.shape[0]

    def body(k_ref, v_ref, lg_ref, sf_ref, sb_ref):
        k = k_ref[...]
        vb = v_ref[...].astype(BF16)
        for d, o_ref in ((0, sf_ref), (1, sb_ref)):
            w, _ = _ctx_weights(lg_ref[0, d:d + 1, :], l_len, d == 1)
            o_ref[0] = _dot_tn((k * w).astype(BF16), vb)

    st = pl.BlockSpec((1, DH, DH), lambda h: (h, 0, 0))
    return _pc(body, name="ctx_state_fwd", grid=(HEADS,),
               in_specs=[pl.BlockSpec((l_len, DH), lambda h: (0, HEADS + h)),
                         pl.BlockSpec((l_len, DH), lambda h: (0, 2 * HEADS + h)),
                         pl.BlockSpec((1, 2, LANES), lambda h: (h, 0, 0))],
               out_specs=[st, st], out_shape=[_sds((HEADS, DH, DH))] * 2,
               compiler_params=_params("arbitrary"))(projc, projc, lgv)


def _ctx_state_bwd(projc, lgv, sgv, dsf, dsb):
    l_len = projc.shape[0]

    def body(k_ref, v_ref, lg_ref, sg_ref, dsf_ref, dsb_ref, dk_ref, dv_ref, drd_ref):
        k = k_ref[...]
        vb = v_ref[...].astype(BF16)
        dk = jnp.zeros((l_len, DH), F32)
        dv = jnp.zeros((l_len, DH), F32)
        rows = []
        for d, ds_ref in ((0, dsf_ref), (1, dsb_ref)):
            w, steps = _ctx_weights(lg_ref[0, d:d + 1, :], l_len, d == 1)
            dsb16 = ds_ref[0].astype(BF16)
            dkw = _dot_nt(vb, dsb16)
            dk = dk + dkw * w
            dv = dv + _dot((k * w).astype(BF16), dsb16)
            tot = jnp.sum(_sum0(dkw * k * w * steps), axis=1, keepdims=True)
            rows.append(jnp.broadcast_to(tot, (1, LANES)) * sg_ref[0, d:d + 1, :])
        dk_ref[...] = dk.astype(BF16)
        dv_ref[...] = dv.astype(BF16)
        rid = lax.broadcasted_iota(jnp.int32, (SUBLANES, LANES), 0)
        drd_ref[0] = jnp.where(rid == 0, rows[0], jnp.where(rid == 1, rows[1], 0.0))

    st = pl.BlockSpec((1, DH, DH), lambda h: (h, 0, 0))
    lane = pl.BlockSpec((1, 2, LANES), lambda h: (h, 0, 0))
    hc = pl.BlockSpec((l_len, DH), lambda h: (0, h))
    return _pc(body, name="ctx_state_bwd", grid=(HEADS,),
               in_specs=[pl.BlockSpec((l_len, DH), lambda h: (0, HEADS + h)),
                         pl.BlockSpec((l_len, DH), lambda h: (0, 2 * HEADS + h)), lane, lane, st, st],
               out_specs=[hc, hc, pl.BlockSpec((1, SUBLANES, LANES), lambda h: (h, 0, 0))],
               out_shape=[_sds((l_len, RET_W), BF16), _sds((l_len, RET_W), BF16), _sds((HEADS, SUBLANES, LANES))],
               compiler_params=_params("arbitrary"))(projc, projc, lgv, sgv, dsf, dsb)


G_BLOCK = (3 * RET_W) // RET_W
GATE_BLOCK = (4 * RET_W + LRU_W) // LRU_W


def _head_norm(y):
    yc = y - jnp.mean(y, axis=-1, keepdims=True)
    rs = lax.rsqrt(jnp.mean(yc * yc, axis=-1, keepdims=True) + EPS)
    return yc * rs, rs


def _gelu_parts(z):
    th = jnp.tanh(GELU_K * (z + GELU_C * z * z * z))
    return 0.5 * z * (1.0 + th), th


def _mix_fwd(o_f, o_b, proj, hf, hb, w_out, x, g1):
    t = x.shape[0]
    tm = _tile(t, True)

    def body(of_ref, ob_ref, g_ref, gt_ref, hf_ref, hb_ref, w_ref, x_ref, g1_ref, x1_ref, cat_ref):
        o = of_ref[...] + ob_ref[...]
        g = g_ref[...].astype(F32)
        for hh in range(HEADS):
            sl = slice(DH * hh, DH * (hh + 1))
            nrm, _ = _head_norm(o[:, sl])
            gh = g[:, sl]
            cat_ref[:, sl] = (gh * _sigmoid(gh) * nrm).astype(BF16)
        gel, _ = _gelu_parts(gt_ref[...].astype(F32))
        cat_ref[:, RET_W:] = ((hf_ref[...] + hb_ref[...]) * gel).astype(BF16)
        x1_ref[...] = x_ref[...] + g1_ref[...] * _dot(cat_ref[...], w_ref[...])

    half = pl.BlockSpec((tm, RET_W), lambda i: (i, 0))
    big = pl.BlockSpec((tm, D_MODEL), lambda i: (i, 0))
    return _pc(body, name="mix_fwd", grid=(t // tm,),
               in_specs=[half, half, pl.BlockSpec((tm, RET_W), lambda i: (i, G_BLOCK)),
                         pl.BlockSpec((tm, LRU_W), lambda i: (i, GATE_BLOCK)), half, half,
                         _full((D_MODEL, D_MODEL)), big, _full((1, D_MODEL))],
               out_specs=[big, big], out_shape=[_sds((t, D_MODEL)), _sds((t, D_MODEL), BF16)],
               compiler_params=_params("arbitrary"))(o_f, o_b, proj, proj, hf, hb, w_out, x, g1)


def _mix_bwd(o_f, o_b, proj, hf, hb, w_out, cat, dx1, g1, comms=()):
    t = dx1.shape[0]
    tm = _tile(t, True)

    def body(of_ref, ob_ref, g_ref, gt_ref, hf_ref, hb_ref, w_ref, cat_ref, dx1_ref, g1_ref,
             do_ref, dhs_ref, dg_ref, dgt_ref, dyb_ref, dg1_ref):
        dx1v = dx1_ref[...]
        y = _dot(cat_ref[...], w_ref[...])

        @pl.when(pl.program_id(0) == 0)
        def _():
            dg1_ref[...] = jnp.zeros_like(dg1_ref)
        dg1_ref[...] += _sum0(dx1v * y)
        dyb = (g1_ref[...] * dx1v).astype(BF16)
        dyb_ref[...] = dyb
        dcat = _dot_nt(dyb, w_ref[...])
        o = of_ref[...] + ob_ref[...]
        g = g_ref[...].astype(F32)
        for hh in range(HEADS):
            sl = slice(DH * hh, DH * (hh + 1))
            nrm, rs = _head_norm(o[:, sl])
            gh = g[:, sl]
            sg = _sigmoid(gh)
            dret = dcat[:, sl]
            dg_ref[:, sl] = (dret * nrm * (sg * (1.0 + gh * (1.0 - sg)))).astype(BF16)
            dn = dret * (gh * sg)
            dyc = rs * (dn - nrm * jnp.mean(dn * nrm, axis=-1, keepdims=True))
            do_ref[:, sl] = (dyc - jnp.mean(dyc, axis=-1, keepdims=True)).astype(BF16)
        z = gt_ref[...].astype(F32)
        gel, th = _gelu_parts(z)
        dlru = dcat[:, RET_W:]
        dhs_ref[...] = dlru * gel
        dgel = 0.5 * (1.0 + th) + 0.5 * z * (1.0 - th * th) * GELU_K * (1.0 + 3.0 * GELU_C * z * z)
        dgt_ref[...] = (dlru * (hf_ref[...] + hb_ref[...]) * dgel).astype(BF16)

    half = pl.BlockSpec((tm, RET_W), lambda i: (i, 0))
    big = pl.BlockSpec((tm, D_MODEL), lambda i: (i, 0))
    return _call(body, name="mix_bwd", grid=(t // tm,),
                 in_specs=[half, half, pl.BlockSpec((tm, RET_W), lambda i: (i, G_BLOCK)),
                           pl.BlockSpec((tm, LRU_W), lambda i: (i, GATE_BLOCK)), half, half,
                           _full((D_MODEL, D_MODEL)), big, big, _full((1, D_MODEL))],
                 out_specs=[half, half, half, half, big, _full((1, D_MODEL))],
                 out_shape=[_sds((t, RET_W), BF16), _sds((t, RET_W)), _sds((t, RET_W), BF16), _sds((t, RET_W), BF16),
                            _sds((t, D_MODEL), BF16), _sds((1, D_MODEL))],
                 scratch_shapes=[], sem=("arbitrary",), args=(o_f, o_b, proj, proj, hf, hb, w_out, cat, dx1, g1),
                 comms=comms)


def _mlp(x1, n2g, sh2, sc2, g2, fg, w1_parts, w2_parts, tgt):
    t = x1.shape[0]
    tm = _tile(t)
    hb_ = MLP_H // N_CHIP
    q_rows = hb_ // 4
    n_cp = 4 * N_DEV

    def body(x1_ref, n2g_ref, sh2_ref, sc2_ref, g2_ref, fg_ref, w1a, w1b, w2a, w2b, tgt_ref,
             dx1_ref, h2b_ref, ab_ref, dub_ref, dmb_ref, dsc_ref, dsh_ref, dg2_ref, dn2_ref, dfg_ref, loss_ref,
             w1_s, w2_s, r_s, sems):
        @pl.when(pl.program_id(0) == 0)
        def _():
            cps = []
            for p, parts in enumerate(((w1a, w2a), (w1b, w2b))):
                for d in range(N_DEV):
                    rows = pl.ds(2 * q_rows * (d % 2) + q_rows * p, q_rows)
                    for src, dst in zip(parts, (w1_s, w2_s)):
                        cps.append(pltpu.make_async_copy(src.at[d], dst.at[d // 2, rows], sems.at[len(cps)]))
            for cp in cps:
                cp.start()
            for r in (dsc_ref, dsh_ref, dg2_ref, dn2_ref, dfg_ref, loss_ref):
                r[...] = jnp.zeros_like(r)
            for cp in cps:
                cp.wait()
        x1v = x1_ref[...]
        n2g, sc2, g2, fg = n2g_ref[...], sc2_ref[...], g2_ref[...], fg_ref[...]
        xh, _ = _rms(x1v)
        h2b = (xh * n2g * (1.0 + sc2) + sh2_ref[...]).astype(BF16)
        h2b_ref[...] = h2b
        m = jnp.zeros((tm, D_MODEL), F32)
        for j in range(N_CHIP):
            sl = slice(hb_ * j, hb_ * (j + 1))
            r = jnp.maximum(_dot(h2b, w1_s[j]), 0.0)
            r_s[:, sl] = r
            ab = (r * r).astype(BF16)
            ab_ref[:, sl] = ab
            m = m + _dot(ab, w2_s[j])
        x2 = x1v + g2 * m
        x2h, r2 = _rms(x2)
        err = x2h * fg - tgt_ref[...]
        loss_ref[...] += _sum0(err * err)
        dout = err * (1.0 / D_MODEL)
        dfg_ref[...] += _sum0(dout * x2h)
        dxh = dout * fg
        dx2 = r2 * (dxh - x2h * jnp.mean(dxh * x2h, axis=-1, keepdims=True))
        dg2_ref[...] += _sum0(dx2 * m)
        dmb = (g2 * dx2).astype(BF16)
        dmb_ref[...] = dmb
        dh2 = jnp.zeros((tm, D_MODEL), F32)
        for j in range(N_CHIP):
            sl = slice(hb_ * j, hb_ * (j + 1))
            dub = (_dot_nt(dmb, w2_s[j]) * (2.0 * r_s[:, sl])).astype(BF16)
            dub_ref[:, sl] = dub
            dh2 = dh2 + _dot_nt(dub, w1_s[j])
        dx, dn2_t, dsh_t, dsc_t = _norm_mod_bwd(x1v, n2g, sc2, dh2)
        dx1_ref[...] = dx2 + dx
        dn2_ref[...] += dn2_t
        dsh_ref[...] += dsh_t
        dsc_ref[...] += dsc_t

        @pl.when(pl.program_id(0) == t // tm - 1)
        def _():
            tot = jnp.sum(loss_ref[...], axis=1, keepdims=True) * (0.5 / D_MODEL)
            loss_ref[...] = jnp.broadcast_to(tot, loss_ref.shape)

    row = _full((1, D_MODEL))
    big = pl.BlockSpec((tm, D_MODEL), lambda i: (i, 0))
    wide = pl.BlockSpec((tm, MLP_H), lambda i: (i, 0))
    return _pc(body, name="mlp", grid=(t // tm,),
               in_specs=[big, row, row, row, row, row, ANY, ANY, ANY, ANY, big],
               out_specs=[big, big, wide, wide, big, row, row, row, row, row, row],
               out_shape=[_sds((t, D_MODEL)), _sds((t, D_MODEL), BF16), _sds((t, MLP_H), BF16), _sds((t, MLP_H), BF16),
                          _sds((t, D_MODEL), BF16)] + [_sds((1, D_MODEL))] * 6,
               scratch_shapes=[pltpu.VMEM((N_CHIP, D_MODEL, hb_), BF16), pltpu.VMEM((N_CHIP, hb_, D_MODEL), BF16),
                               pltpu.VMEM((tm, MLP_H), F32), pltpu.SemaphoreType.DMA((n_cp,))],
               compiler_params=_params("arbitrary"))(x1, n2g, sh2, sc2, g2, fg, *w1_parts, *w2_parts, tgt)


def _tn(a, b, nj, a_blocked, b_blocked, name, extra=None, comms=()):
    t = a.shape[0]
    m = a.shape[1] // (nj if a_blocked else 1)
    n = b.shape[1] // (nj if b_blocked else 1)
    bk = next((b for b in (2048, 1024, 512) if t % b == 0), t)
    nk = t // bk
    a_col = (lambda j: j) if a_blocked else (lambda j: 0)
    b_col = (lambda j: j) if b_blocked else (lambda j: 0)
    in_specs = [pl.BlockSpec((bk, m), lambda j, k: (k, a_col(j))), pl.BlockSpec((bk, n), lambda j, k: (k, b_col(j)))]
    args = [a, b]
    if extra is not None:
        a2, b2 = extra
        t2 = a2.shape[0]
        in_specs += [pl.BlockSpec((t2, m), lambda j, k: (0, a_col(j))),
                     pl.BlockSpec((t2, n), lambda j, k: (0, b_col(j)))]
        args += [a2, b2]

    def body(*refs):
        a_ref, b_ref = refs[0], refs[1]
        o_ref, acc = refs[-2], refs[-1]
        k = pl.program_id(1)

        @pl.when(k == 0)
        def _():
            acc[...] = jnp.zeros_like(acc)
        acc[...] += _dot_tn(a_ref[...].astype(BF16), b_ref[...].astype(BF16))

        @pl.when(k == nk - 1)
        def _():
            if extra is not None:
                acc[...] += _dot_tn(refs[2][...].astype(BF16), refs[3][...].astype(BF16))
            o_ref[0] = acc[...]

    (out,), couts = _call(body, name=name, grid=(nj, nk), in_specs=in_specs,
                          out_specs=[pl.BlockSpec((1, m, n), lambda j, k: (j, 0, 0))], out_shape=[_sds((nj, m, n))],
                          scratch_shapes=[pltpu.VMEM((m, n), F32)], sem=("arbitrary", "arbitrary"), args=args,
                          comms=comms)
    return (out, couts) if comms else out


ROW_LOSS = 0
ROW_DMOD = 1
ROW_DMODC = 7
ROW_N1, ROW_N2, ROW_FG, ROW_CB = 9, 10, 11, 12
ROW_BA, ROW_BX, ROW_LAM = 13, 15, 17
ROW_CW = 20
ROW_RD = 24
SLAB_ROWS = 32
SEG = D_MODEL // 2


def _pack_small(rows, drd, cw2, cb2, lru2, gates):
    n_rows, n_lru = len(rows), len(lru2)

    def body(*refs):
        r = refs[:n_rows]
        drd_f, drd_b, drd_c, cw_a, cw_b, cb_a, cb_b = refs[n_rows:n_rows + 7]
        lru = refs[n_rows + 7:n_rows + 7 + n_lru]
        gf_ref, gb_ref, slab, ga, gx = refs[n_rows + 7 + n_lru:]
        slab[...] = jnp.zeros_like(slab)
        slab[ROW_LOSS:ROW_LOSS + 1, :] = r[0][...]
        for k in range(N_MOD):
            slab[ROW_DMOD + k:ROW_DMOD + k + 1, :] = r[1 + k][...]
        slab[ROW_DMODC:ROW_DMODC + 1, :] = r[7][...]
        slab[ROW_DMODC + 1:ROW_DMODC + 2, :] = r[8][...]
        slab[ROW_N1:ROW_N1 + 1, :] = r[9][...] + r[10][...]
        slab[ROW_N2:ROW_N2 + 1, :] = r[11][...]
        slab[ROW_FG:ROW_FG + 1, :] = r[12][...]
        slab[ROW_CB:ROW_CB + 1, 0:LRU_W] = cb_a[...] + cb_b[...]
        for k, row in enumerate((ROW_BA, ROW_BA + 1, ROW_BX, ROW_BX + 1, ROW_LAM, ROW_LAM + 1)):
            slab[row:row + 1, 0:LRU_W] = lru[2 * k][...] + lru[2 * k + 1][...]
        slab[ROW_CW:ROW_CW + 4, 0:LRU_W] = cw_a[...] + cw_b[...]
        for h in range(HEADS):
            slab[ROW_RD + h:ROW_RD + h + 1, 0:LANES] = drd_f[h, 0:1, :] + drd_c[h, 0:1, :]
            slab[ROW_RD + HEADS + h:ROW_RD + HEADS + h + 1, 0:LANES] = drd_b[h, 0:1, :] + drd_c[h, 1:2, :]
        for d, g_ref in enumerate((gf_ref, gb_ref)):
            for n in range(LRU_BLOCKS):
                blk = slice(LRU_BD * n, LRU_BD * (n + 1))
                ga[blk, LRU_BD * d:LRU_BD * (d + 1)] = g_ref[0, blk, blk].astype(BF16)
                gx[blk, LRU_BD * d:LRU_BD * (d + 1)] = g_ref[1, blk, blk].astype(BF16)

    args = list(rows) + list(drd) + list(cw2) + list(cb2) + list(lru2) + list(gates)
    gate_shape = (LRU_W, 2 * LRU_BD)
    return _pc(body, name="pack_small", in_specs=[_full(a.shape) for a in args],
               out_specs=[_full((SLAB_ROWS, D_MODEL)), _full(gate_shape), _full(gate_shape)],
               out_shape=[_sds((SLAB_ROWS, D_MODEL)), _sds(gate_shape, BF16), _sds(gate_shape, BF16)],
               compiler_params=_params())(*args)


def _adam_math(w, g, m, v):
    mn = ADAM_B1 * m + (1.0 - ADAM_B1) * g
    vn = ADAM_B2 * v + (1.0 - ADAM_B2) * (g * g)
    mh = mn / (1.0 - ADAM_B1 ** ADAM_STEP)
    vh = vn / (1.0 - ADAM_B2 ** ADAM_STEP)
    return -ADAM_LR * (mh / (jnp.sqrt(vh) + ADAM_EPS) + ADAM_WD * w), mn, vn


SMALL_PARAMS = ("b_ada", "norm1_g", "norm2_g", "final_g", "ret_decay", "conv_w", "conv_b", "lru_wa", "lru_ba", "lru_wx",
                "lru_bx", "lru_lambda")


def _finalize_small(chip_idx, slab_all, ga_all, gx_all, wmv):
    n_p = len(SMALL_PARAMS)
    flat = [a for nm in SMALL_PARAMS for a in wmv[nm]]
    ada_n = N_MOD * D_MODEL // N_CHIP

    def body(c_ref, slab_ref, ga_ref, gx_ref, *refs):
        prm = {nm: refs[3 * k:3 * k + 3] for k, nm in enumerate(SMALL_PARAMS)}
        outs = {nm: refs[3 * n_p + 4 * k:3 * n_p + 4 * k + 4] for k, nm in enumerate(SMALL_PARAMS)}
        b128_ref, dmc_ref, loss_ref = refs[3 * n_p + 4 * n_p:]
        chip = c_ref[0]

        def pick(fn):
            acc = fn(0)
            for j in range(1, N_CHIP):
                acc = jnp.where(chip == j, fn(j), acc)
            return acc

        tot = slab_ref[0]
        for d in range(1, N_DEV):
            tot = tot + slab_ref[d]

        def update(nm, g, sl=None, rows=None):
            w_ref, m_ref, v_ref = prm[nm]
            g_ref, d_ref, mo_ref, vo_ref = outs[nm]
            ix = (slice(None) if rows is None else rows, slice(None) if sl is None else sl)
            dl, mn, vn = _adam_math(w_ref[ix], g, m_ref[ix], v_ref[ix])
            g_ref[ix] = g
            d_ref[ix] = dl
            mo_ref[ix] = mn
            vo_ref[ix] = vn

        loss_ref[...] = jnp.broadcast_to(tot[ROW_LOSS:ROW_LOSS + 1, 0:LANES], (SUBLANES, LANES))
        for k in range(N_MOD):
            g = tot[ROW_DMOD + k:ROW_DMOD + k + 1, :]
            if k < 2:
                g = g + tot[ROW_DMODC + k:ROW_DMODC + k + 1, :]
            update("b_ada", g, slice(D_MODEL * k, D_MODEL * (k + 1)))
        update("norm1_g", tot[ROW_N1:ROW_N1 + 1, :])
        update("norm2_g", tot[ROW_N2:ROW_N2 + 1, :])
        update("final_g", tot[ROW_FG:ROW_FG + 1, :])
        update("ret_decay", tot[ROW_RD:ROW_RD + SUBLANES, 0:LANES])
        update("conv_b", tot[ROW_CB:ROW_CB + 1, 0:LRU_W])
        update("conv_w", pick(lambda j: tot[ROW_CW:ROW_CW + 4, LANES * j:LANES * (j + 1)]))
        for nm, row in (("lru_ba", ROW_BA), ("lru_bx", ROW_BX), ("lru_lambda", ROW_LAM)):
            update(nm, pick(lambda j, row=row: tot[row:row + 2, LANES * j:LANES * (j + 1)]))
        for nm, g_all in (("lru_wa", ga_ref), ("lru_wx", gx_ref)):
            for dr in range(2):
                lanes = slice(LRU_BD * dr, LRU_BD * (dr + 1))
                g = g_all[0, :, lanes].astype(F32)
                for d in range(1, N_DEV):
                    g = g + g_all[d, :, lanes].astype(F32)
                update(nm, g, rows=slice(LRU_W * dr, LRU_W * (dr + 1)))

        def seg(rows6, s):
            return rows6[s // 2][:, SEG * (s % 2):SEG * (s % 2 + 1)]

        b128_ref[...] = jnp.zeros_like(b128_ref)
        dmc_ref[...] = jnp.zeros_like(dmc_ref)
        zero = jnp.zeros((1, D_MODEL), F32)
        ctx6 = [tot[ROW_DMODC:ROW_DMODC + 1, :], tot[ROW_DMODC + 1:ROW_DMODC + 2, :]] + [zero] * (N_MOD - 2)
        for q in range(ada_n // SEG):
            cols = slice(SEG * q, SEG * (q + 1))
            for d in range(N_DEV):
                rows6 = [slab_ref[d, ROW_DMOD + k:ROW_DMOD + k + 1, :] for k in range(N_MOD)]
                b128_ref[d:d + 1, cols] = pick(lambda j, rows6=rows6: seg(rows6, 3 * j + q))
            c = pick(lambda j: seg(ctx6, 3 * j + q))
            b128_ref[N_DEV:N_DEV + 1, cols] = c
            dmc_ref[0:1, cols] = c

    out_shape = []
    for nm in SMALL_PARAMS:
        out_shape += [_sds(wmv[nm][0].shape)] * 4
    out_shape += [_sds((LANES, ada_n)), _sds((SUBLANES, ada_n)), _sds((SUBLANES, LANES))]
    args = [slab_all, ga_all, gx_all] + flat
    grid_spec = pltpu.PrefetchScalarGridSpec(
        num_scalar_prefetch=1, grid=(1,), in_specs=[_full(a.shape) for a in args],
        out_specs=[_full(s.shape) for s in out_shape])
    outs = _pc(body, name="finalize_small", grid_spec=grid_spec, out_shape=out_shape,
               compiler_params=_params("arbitrary"))(chip_idx, *args)
    res = {nm: tuple(outs[4 * k:4 * k + 4]) for k, nm in enumerate(SMALL_PARAMS)}
    return res, outs[4 * n_p], outs[4 * n_p + 1], outs[4 * n_p + 2]


def _block_diag(w):
    eye = jnp.eye(LRU_BLOCKS, dtype=F32)
    return (w[:, :, None, :] * eye[:, None, :, None]).reshape(LRU_W, LRU_W).astype(BF16)


def _lane_rep(v8):
    return jnp.broadcast_to(v8.reshape(SUBLANES, 1), (SUBLANES, LANES))


def kernel(x, c, ctx, c_ctx, w_ada, b_ada, norm1_g, norm2_g, w_in, ret_decay, conv_w, conv_b, lru_wa, lru_ba, lru_wx, lru_bx, lru_lambda, w_out, w_mlp1, w_mlp2, final_g, loss_target, m_c_ctx, m_w_ada, m_b_ada, m_norm1_g, m_norm2_g, m_w_in, m_ret_decay, m_conv_w, m_conv_b, m_lru_wa, m_lru_ba, m_lru_wx, m_lru_bx, m_lru_lambda, m_w_out, m_w_mlp1, m_w_mlp2, m_final_g, v_c_ctx, v_w_ada, v_b_ada, v_norm1_g, v_norm2_g, v_w_in, v_ret_decay, v_conv_w, v_conv_b, v_lru_wa, v_lru_ba, v_lru_wx, v_lru_bx, v_lru_lambda, v_w_out, v_w_mlp1, v_w_mlp2, v_final_g):
    ax, ay, ac = lax.axis_index("x"), lax.axis_index("y"), lax.axis_index("c")
    chip = 2 * ax + ay
    dev = 4 * ax + 2 * ay + ac
    c_idx = jnp.stack([ac, chip]).astype(jnp.int32)
    j_idx = chip.reshape(1).astype(jnp.int32)

    xt = x[0]
    t_len = xt.shape[0]
    ctxt = ctx[0]
    l_len = ctxt.shape[0]
    tgt = loss_target[0]
    ada_n = w_ada.shape[2]

    def my_half(w2d):
        r = w2d.shape[0] // 2
        return lax.dynamic_slice_in_dim(w2d, ac * r, r, axis=0).astype(BF16)

    pad8 = lambda a: jnp.pad(a, ((0, SUBLANES - a.shape[0]), (0, 0)))
    small = jnp.concatenate([pad8(conv_w[0]), pad8(lru_ba[0]), pad8(lru_bx[0]), pad8(lru_lambda[0])], axis=0)
    b_shard = lax.dynamic_slice_in_dim(b_ada, chip * ada_n, ada_n, axis=1)
    gw_in, _, small_all, a16, mod_parts, lgv, sgv = _head(
        my_half(w_in[0]), pad8(c), small, w_ada[0], b_shard, c_ctx, ret_decay[0])
    w4 = gw_in.reshape(N_CHIP, D_MODEL, IN_COLS // N_CHIP)

    mod_all = mod_parts[0::2].transpose(1, 0, 2).reshape(16, N_CHIP * ada_n)
    mod_me = lax.dynamic_slice_in_dim(mod_all, dev, 1, axis=0)
    sh1, sc1, g1, sh2, sc2, g2 = [mod_me[:, D_MODEL * k:D_MODEL * (k + 1)] for k in range(N_MOD)]
    csh1, csc1 = mod_all[8:9, 0:D_MODEL], mod_all[8:9, D_MODEL:2 * D_MODEL]

    cos2, sin2 = _rotary_tables(t_len)
    cos_c, sin_c = jnp.ones((l_len, DH), F32), jnp.zeros((l_len, DH), F32)
    n1g, n2g = norm1_g, norm2_g
    fg = final_g.reshape(1, D_MODEL)

    small_full = small_all[0::2].transpose(1, 0, 2).reshape(4 * SUBLANES, LRU_W)
    cw = small_full[0:4]
    cb = conv_b
    ba_f, ba_b = small_full[8:9], small_full[9:10]
    bx_f, bx_b = small_full[16:17], small_full[17:18]
    lam_f, lam_b = small_full[24:25], small_full[25:26]
    wa_f, wa_b = _block_diag(lru_wa[0, 0]), _block_diag(lru_wa[0, 1])
    wx_f, wx_b = _block_diag(lru_wx[0, 0]), _block_diag(lru_wx[0, 1])
    zero_h = jnp.zeros((1, LRU_W), F32)

    projc, xrc, hcb16 = _inproj_fwd(ctxt, n1g, csh1, csc1, w4, cos_c, sin_c, "inproj_fwd_ctx")
    s_f, s_b = _ctx_state_fwd(projc, lgv)
    xcc = _conv_fwd(xrc, cw, cb, "conv_fwd_ctx")
    hcf = _lru_fwd(xcc, wa_f, wx_f, ba_f, bx_f, lam_f, zero_h, False, "lru_fwd_ctx_f")
    hcbk = _lru_fwd(xcc, wa_b, wx_b, ba_b, bx_b, lam_b, zero_h, True, "lru_fwd_ctx_b")
    lru_sf, lru_sb = hcf[l_len - 1:l_len], hcbk[0:1]

    h1, h2 = my_half(w_mlp1[0]), my_half(w_mlp2[0])
    q = h1.shape[0] // 2
    (proj, xrl, hb16), ((gw_1a,),) = _inproj_fwd(xt, n1g, sh1, sc1, w4, cos2, sin2, "inproj_fwd",
                                           comms=(_AllGather([h1[:q]]),))
    (o_f, o_b, spf, spb), ((gw_1b, gw_out),) = _ret_fwd(proj, lgv, s_f, s_b,
                                                       comms=(_AllGather([h1[q:], my_half(w_out[0])]),))
    xcl = _conv_fwd(xrl, cw, cb, "conv_fwd")
    hf, ((gw_2a,),) = _lru_fwd(xcl, wa_f, wx_f, ba_f, bx_f, lam_f, lru_sf, False, "lru_fwd_f",
                              comms=(_AllGather([h2[:q]]),))
    hbk, ((gw_2b,),) = _lru_fwd(xcl, wa_b, wx_b, ba_b, bx_b, lam_b, lru_sb, True, "lru_fwd_b",
                               comms=(_AllGather([h2[q:]]),))
    wo = gw_out.reshape(D_MODEL, D_MODEL)
    x1, cat = _mix_fwd(o_f, o_b, proj, hf, hbk, wo, xt, g1)

    (dx1, h2b, ab, dub, dmb, dsc2, dsh2, dg2, dn2g, dfg, lossv) = _mlp(
        x1, n2g, sh2, sc2, g2, fg, (gw_1a, gw_1b), (gw_2a, gw_2b), tgt)
    gw_mlp1 = _tn(h2b, dub, N_CHIP, False, True, "grad_w_mlp1")
    b_1 = gw_mlp1.reshape(N_DEV, D_MODEL // 2, MLP_H // N_CHIP)
    gw_mlp2, ((r_1,),) = _tn(ab, dmb, N_CHIP, True, False, "grad_w_mlp2", comms=(_pair_exchange([b_1]),))

    half = D_MODEL // 4
    top, bot = (0, half), (half, half)
    b_2 = gw_mlp2.reshape(N_DEV, MLP_H // N_DEV, D_MODEL)
    p_1, pb_1 = _pair_add(b_1, r_1, c_idx, "rs_pair_add_w_mlp1")
    (do, dhs, dg, dgate, dyb, dg1), ((q_1a,), (r_2,)) = _mix_bwd(
        o_f, o_b, proj, hf, hbk, wo, cat, dx1, g1, comms=(_chip_exchange([pb_1], top), _pair_exchange([b_2])))
    gw_o = _tn(cat, dyb, 1, False, False, "grad_w_out")
    b_o = gw_o.reshape(N_DEV, D_MODEL // N_DEV, D_MODEL)
    p_2, pb_2 = _pair_add(b_2, r_2, c_idx, "rs_pair_add_w_mlp2")

    ((dq_f, dk_f, dv_f, ds_f, drd_f), (dq_b, dk_b, dv_b, ds_b, drd_b)), ((q_1b,), (q_2a,), (r_o,)) = _ret_bwd(
        proj, lgv, sgv, spf, spb, do,
        comms=(_chip_exchange([pb_1], bot), _chip_exchange([pb_2], top), _pair_exchange([b_o])))
    p_o, pb_o = _pair_add(b_o, r_o, c_idx, "rs_pair_add_w_out")
    h_1 = _chip_add(p_1, (q_1a, q_1b), c_idx, "rs_chip_add_w_mlp1")

    (dxc_f, dpre_f, dba_f, dbx_f, dlam_f, dh0_f), ((q_2b,), (f_1,)) = _lru_bwd(
        xcl, wa_f, wx_f, ba_f, bx_f, lam_f, hf, lru_sf, dhs, False, "lru_bwd_f",
        comms=(_chip_exchange([pb_2], bot), _pair_gather([h_1])))
    h_2 = _chip_add(p_2, (q_2a, q_2b), c_idx, "rs_chip_add_w_mlp2")
    (dxc_b, dpre_b, dba_b, dbx_b, dlam_b, dh0_b), ((q_o,), (f_2,)) = _lru_bwd(
        xcl, wa_b, wx_b, ba_b, bx_b, lam_b, hbk, lru_sb, dhs, True, "lru_bwd_b",
        comms=(_chip_exchange([pb_o]), _pair_gather([h_2])))
    h_o = _chip_add(p_o, (q_o,), c_idx, "rs_chip_add_w_out")
    dxr, dcw, dcb = _conv_bwd(dxc_f, dxc_b, xrl, cw, "conv_bwd")
    grad_x, dpb, dn1g, dsh1, dsc1 = _inproj_bwd(
        xt, n1g, sh1, sc1, w4, cos2, sin2, [dq_f, dq_b, dk_f, dk_b, dv_f, dv_b, dg, dxr, dgate], dx1, "inproj_bwd")

    dkc, dvc, drd_c = _ctx_state_bwd(projc, lgv, sgv, ds_f, ds_b)
    zc = jnp.zeros((l_len, LRU_W), F32)
    dhc_f = lax.dynamic_update_slice(zc, dh0_f, (l_len - 1, 0))
    dhc_b = lax.dynamic_update_slice(zc, dh0_b, (0, 0))
    (dxcc_f, dprec_f, dbac_f, dbxc_f, dlamc_f, _), _ = _lru_bwd(
        xcc, wa_f, wx_f, ba_f, bx_f, lam_f, hcf, zero_h, dhc_f, False, "lru_bwd_ctx_f")
    (dxcc_b, dprec_b, dbac_b, dbxc_b, dlamc_b, _), _ = _lru_bwd(
        xcc, wa_b, wx_b, ba_b, bx_b, lam_b, hcbk, zero_h, dhc_b, True, "lru_bwd_ctx_b")
    dxrc, dcw_c, dcb_c = _conv_bwd(dxcc_f, dxcc_b, xrc, cw, "conv_bwd_ctx")
    zr = jnp.zeros((l_len, RET_W), BF16)
    _, dpbc, dn1g_c, dcsh1, dcsc1 = _inproj_bwd(
        ctxt, n1g, csh1, csc1, w4, cos_c, sin_c, [zr, zr, dkc, zr, dvc, zr, zr, dxrc, zr],
        jnp.zeros((l_len, D_MODEL), F32), "inproj_bwd_ctx")

    gw_i = _tn(hb16, dpb, N_CHIP, False, True, "grad_w_in", extra=(hcb16, dpbc))
    b_i = gw_i.reshape(N_DEV, D_MODEL // 2, IN_COLS // N_CHIP)
    gwa_f, ((r_i,), (f_o,)) = _tn(xcl, dpre_f, 2, False, True, "grad_lru_gates_f", extra=(xcc, dprec_f),
                                  comms=(_pair_exchange([b_i]), _pair_gather([h_o])))
    p_i, pb_i = _pair_add(b_i, r_i, c_idx, "rs_pair_add_w_in")
    gwa_b, ((q_i,),) = _tn(xcl, dpre_b, 2, False, True, "grad_lru_gates_b", extra=(xcc, dprec_b),
                           comms=(_chip_exchange([pb_i]),))
    slab, ga, gx = _pack_small(
        [lossv, dsh1, dsc1, dg1, dsh2, dsc2, dg2, dcsh1, dcsc1, dn1g, dn1g_c, dn2g, dfg],
        (drd_f, drd_b, drd_c), (dcw, dcw_c), (dcb, dcb_c),
        (dba_f, dbac_f, dba_b, dbac_b, dbx_f, dbxc_f, dbx_b, dbxc_b, dlam_f, dlamc_f, dlam_b, dlamc_b),
        (gwa_f, gwa_b))
    (f_i,), (slab_all, ga_all, gx_all) = _run_comms(
        [_pair_gather([_chip_add(p_i, (q_i,), c_idx, "rs_chip_add_w_in")]), _AllGather([slab, ga, gx])],
        "tail_exchanges")
    g_in, g_out, g_1, g_2 = _shard_of(f_i), _shard_of(f_o), _shard_of(f_1), _shard_of(f_2)
    big = {}
    for nm, w, g, m, v in (("w_in", w_in, g_in, m_w_in, v_w_in), ("w_out", w_out, g_out, m_w_out, v_w_out),
                           ("w_mlp1", w_mlp1, g_1, m_w_mlp1, v_w_mlp1), ("w_mlp2", w_mlp2, g_2, m_w_mlp2, v_w_mlp2)):
        go, d_, mn, vn = _adamw(w[0], g, m[0], v[0], "adamw_" + nm)
        big[nm] = (go[None], d_[None], mn[None], vn[None])
    params = {
        "b_ada": (b_ada, m_b_ada, v_b_ada), "norm1_g": (norm1_g, m_norm1_g, v_norm1_g),
        "norm2_g": (norm2_g, m_norm2_g, v_norm2_g), "final_g": (final_g, m_final_g, v_final_g),
        "ret_decay": (ret_decay, m_ret_decay, v_ret_decay), "conv_w": (conv_w, m_conv_w, v_conv_w),
        "conv_b": (conv_b, m_conv_b, v_conv_b), "lru_wa": (lru_wa, m_lru_wa, v_lru_wa),
        "lru_ba": (lru_ba, m_lru_ba, v_lru_ba), "lru_wx": (lru_wx, m_lru_wx, v_lru_wx),
        "lru_bx": (lru_bx, m_lru_bx, v_lru_bx), "lru_lambda": (lru_lambda, m_lru_lambda, v_lru_lambda),
    }
    as2d = {
        "b_ada": lambda a: a, "norm1_g": lambda a: a, "norm2_g": lambda a: a, "conv_b": lambda a: a,
        "final_g": lambda a: a.reshape(1, D_MODEL), "ret_decay": lambda a: _lane_rep(a.reshape(-1)),
        "conv_w": lambda a: a[0], "lru_ba": lambda a: a[0], "lru_bx": lambda a: a[0], "lru_lambda": lambda a: a[0],
        "lru_wa": lambda a: a.reshape(2 * LRU_W, LRU_BD), "lru_wx": lambda a: a.reshape(2 * LRU_W, LRU_BD),
    }
    res, b128, dmc8, loss8 = _finalize_small(
        j_idx, slab_all, ga_all, gx_all, {nm: tuple(as2d[nm](a) for a in params[nm]) for nm in SMALL_PARAMS})
    loss = loss8[0, 0]
    small_out = {}
    for nm in SMALL_PARAMS:
        shp = params[nm][0].shape
        if nm == "ret_decay":
            small_out[nm] = tuple(o[:, 0].reshape(shp) for o in res[nm])
        else:
            small_out[nm] = tuple(o.reshape(shp) for o in res[nm])

    g_ada = _ada_grad(jnp.pad(a16.T, ((0, 0), (0, LANES - 16))), b128)
    g_ada, d_ada, m_ada, v_ada = _adamw(w_ada[0], g_ada, m_w_ada[0], v_w_ada[0], "adamw_w_ada")

    (cparts,) = _all_gather([_cctx_partial(dmc8, w_ada[0])], "gather_cctx")
    g_cc, d_cc, m_cc, v_cc = _cctx_final(cparts, c_ctx, m_c_ctx, v_c_ctx)
    small_out["c_ctx"] = tuple(a.reshape(D_MODEL) for a in (g_cc, d_cc, m_cc, v_cc))
    small_out["w_ada"] = (g_ada[None], d_ada[None], m_ada[None], v_ada[None])
    small_out.update(big)

    order = ["c_ctx", "w_ada", "b_ada", "norm1_g", "norm2_g", "w_in", "ret_decay", "conv_w", "conv_b", "lru_wa", "lru_ba",
             "lru_wx", "lru_bx", "lru_lambda", "w_out", "w_mlp1", "w_mlp2", "final_g"]
    outs = [loss, grad_x[None]]
    for k in range(4):
        outs += [small_out[nm][k] for nm in order]
    return tuple(outs)
```

```python
import math

import jax
import jax.numpy as jnp
from jax import lax
from jax.experimental import pallas as pl
from jax.experimental.pallas import tpu as pltpu

F32 = jnp.float32
BF16 = jnp.bfloat16

D_MODEL = 1024
HEADS = 4
DH = 128
CHUNK = 256
RET_W = HEADS * DH
LRU_W = 512
LRU_BLOCKS = 8
LRU_BD = LRU_W // LRU_BLOCKS
LRU_C = 8.0
IN_COLS = 4 * RET_W + 2 * LRU_W
MLP_H = 4 * D_MODEL
N_MOD = 6
GRID_W = 64
ROPE_BASE = 10000.0
K_SCALE = DH ** -0.5
EPS = 1e-6
GELU_K = math.sqrt(2.0 / math.pi)
GELU_C = 0.044715

ADAM_LR = 0.001
ADAM_B1 = 0.9
ADAM_B2 = 0.999
ADAM_EPS = 1e-08
ADAM_WD = 0.01
ADAM_STEP = 10

N_DEV = 8
N_CHIP = 4
SUBLANES = 8
LANES = 128
VMEM_LIMIT_V7X = 56 * 1024 * 1024
MESH = pl.DeviceIdType.MESH
ANY = pl.BlockSpec(memory_space=pl.ANY)


def _pc(body, **kw):
    return pl.pallas_call(body, **kw)


def _params(*sem):
    return pltpu.CompilerParams(dimension_semantics=sem if sem else None, vmem_limit_bytes=VMEM_LIMIT_V7X)


def _tile(t, big=False):
    if big and t >= 1024:
        return 512
    return 256 if t >= 256 else t


def _sds(shape, dtype=F32):
    return jax.ShapeDtypeStruct(tuple(shape), dtype)


def _full(shape):
    nd = len(shape)
    return pl.BlockSpec(tuple(shape), lambda *_: (0,) * nd)


def _sigmoid(x):
    return 1.0 / (1.0 + jnp.exp(-x))


def _log1p_pos(y):
    s = y * (1.0 - y * (0.5 - y * (1.0 / 3.0 - y * (0.25 - y * (0.2 - y / 6.0)))))
    return jnp.where(y < 0.03, s, jnp.log(1.0 + y))


def _softplus(z):
    return jnp.maximum(z, 0.0) + _log1p_pos(jnp.exp(-jnp.abs(z)))


def _one_minus_sq(la, a):
    t = la * (1.0 + la * (0.5 + la * (1.0 / 6.0 + la * (1.0 / 24.0 + la * (1.0 / 120.0)))))
    return jnp.where(la > -0.125, -t, 1.0 - a) * (1.0 + a)


def _rms(x):
    r = lax.rsqrt(jnp.mean(x * x, axis=-1, keepdims=True) + EPS)
    return x * r, r


def _dot(a, b):
    return jnp.dot(a, b, preferred_element_type=F32)


def _dot_nt(a, b):
    return lax.dot_general(a, b, (((1,), (1,)), ((), ())), preferred_element_type=F32)


def _dot_tn(a, b):
    return lax.dot_general(a, b, (((0,), (0,)), ((), ())), preferred_element_type=F32)


def _sum0(x):
    return jnp.sum(x, axis=0, keepdims=True)


def _norm_mod_bwd(x, g, sc, dh):
    xh, r = _rms(x)
    hn = xh * g
    dhn = dh * (1.0 + sc)
    dxh = dhn * g
    dx = r * (dxh - xh * jnp.mean(dxh * xh, axis=-1, keepdims=True))
    return dx, _sum0(dhn * xh), _sum0(dh), _sum0(dh * hn)


def _dev_index(p):
    return 4 * p[0] + 2 * p[1] + p[2]


def _mesh_pos():
    return lax.axis_index("x"), lax.axis_index("y"), lax.axis_index("c")


class _AllGather:
    def __init__(self, arrs):
        n = len(arrs)
        self.arrays = list(arrs)
        self.out_shapes = [_sds((N_DEV,) + a.shape, a.dtype) for a in arrs]
        self.scratch = ([pltpu.VMEM(a.shape, a.dtype) for a in arrs]
                        + [pltpu.SemaphoreType.DMA((7 * n,)), pltpu.SemaphoreType.DMA((7 * n,)),
                           pltpu.SemaphoreType.DMA((n,))])
        self.aliases = {}

    def _parts(self, ins, outs, scr):
        n = len(self.arrays)
        stage = scr[:n]
        send_sems, recv_sems, local_sems = scr[n:]
        x, y, c = _mesh_pos()
        me, sib = (x, y, c), (x, y, 1 - c)
        chips = [(1 - x, y), (x, 1 - y), (1 - x, 1 - y)]

        def copy(t, k, block, to, own=False):
            dst = outs[t].at[_dev_index(block)]
            return pltpu.make_async_remote_copy(
                src_ref=ins[t] if own else dst, dst_ref=dst,
                send_sem=send_sems.at[7 * t + k], recv_sem=recv_sems.at[7 * t + k],
                device_id=to, device_id_type=MESH)

        first = []
        for t in range(n):
            first.append(copy(t, 0, me, sib, own=True))
            for j, ch in enumerate(chips):
                first.append(copy(t, 1 + j, me, (*ch, c), own=True))
        stage_in = [pltpu.make_async_copy(ins[t], stage[t], local_sems.at[t]) for t in range(n)]
        mine = [pltpu.make_async_copy(stage[t], outs[t].at[_dev_index(me)], local_sems.at[t]) for t in range(n)]
        return n, c, me, sib, chips, copy, first, stage_in, mine

    def start(self, ins, outs, scr):
        n, _, _, _, _, _, first, stage_in, mine = self._parts(ins, outs, scr)
        for cp in stage_in:
            cp.start()
        for cp in first:
            cp.start()
        for t in range(n):
            stage_in[t].wait()
            mine[t].start()

    def relay(self, ins, outs, scr):
        n, c, me, sib, chips, copy, _, _, _ = self._parts(ins, outs, scr)
        for j, ch in enumerate(chips):
            for t in range(n):
                copy(t, 1 + j, (*ch, c), me).wait_recv()
                copy(t, 4 + j, (*ch, c), sib).start()

    def finish(self, ins, outs, scr):
        n, c, me, sib, chips, copy, first, _, mine = self._parts(ins, outs, scr)
        passed = [copy(t, 4 + j, (*ch, c), sib) for j, ch in enumerate(chips) for t in range(n)]
        for t in range(n):
            copy(t, 0, sib, me).wait_recv()
            for j, ch in enumerate(chips):
                copy(t, 4 + j, (*ch, 1 - c), me).wait_recv()
        for cp in first + passed:
            cp.wait_send()
        for cp in mine:
            cp.wait()


class _Exchange:
    def __init__(self, arrays, out_shapes, plan, n_copies, aliases=None):
        self.arrays = list(arrays)
        self.out_shapes = list(out_shapes)
        self.plan = plan
        self.scratch = [pltpu.SemaphoreType.DMA((n_copies,)), pltpu.SemaphoreType.DMA((n_copies,))]
        self.aliases = aliases or {}

    def _copies(self, ins, outs, scr):
        send_sems, recv_sems = scr
        snd, rcv = [], []
        for i, (src, dst, peer, lands) in enumerate(self.plan(ins, outs, _mesh_pos())):
            kw = dict(send_sem=send_sems.at[i], recv_sem=recv_sems.at[i], device_id=peer, device_id_type=MESH)
            snd.append(pltpu.make_async_remote_copy(src_ref=src, dst_ref=dst, **kw))
            rcv.append(pltpu.make_async_remote_copy(src_ref=src, dst_ref=lands, **kw))
        return snd, rcv

    def start(self, ins, outs, scr):
        for cp in self._copies(ins, outs, scr)[0]:
            cp.start()

    def relay(self, ins, outs, scr):
        pass

    def finish(self, ins, outs, scr):
        snd, rcv = self._copies(ins, outs, scr)
        for cp in rcv:
            cp.wait_recv()
        for cp in snd:
            cp.wait_send()


def _pair_exchange(grads):
    n = len(grads)

    def plan(ins, outs, pos):
        x, y, c = pos
        return [(ins[t].at[2 * j + (1 - c)], outs[t].at[j], (x, y, 1 - c), outs[t].at[j])
                for t in range(n) for j in range(N_CHIP)]

    return _Exchange(grads, [_sds((N_CHIP,) + g.shape[1:], g.dtype) for g in grads], plan, N_CHIP * n)


def _chip_exchange(parts, rows=None):
    n = len(parts)

    def plan(ins, outs, pos):
        x, y, c = pos
        chips = [(1 - x, y), (x, 1 - y), (1 - x, 1 - y)]

        def src(t, ch):
            blk = ins[t].at[2 * ch[0] + ch[1]]
            return blk if rows is None else blk.at[pl.ds(rows[0], rows[1])]

        return [(src(t, ch), outs[t].at[k], (*ch, c), outs[t].at[k]) for t in range(n) for k, ch in enumerate(chips)]

    shapes = [_sds((3, p.shape[1] if rows is None else rows[1]) + p.shape[2:], p.dtype) for p in parts]
    return _Exchange(parts, shapes, plan, 3 * n)


def _pair_gather(bufs):
    n = len(bufs)

    def plan(ins, outs, pos):
        x, y, c = pos
        return [(ins[t].at[c], outs[t].at[c], (x, y, 1 - c), outs[t].at[1 - c]) for t in range(n)]

    return _Exchange(bufs, [_sds(b.shape, b.dtype) for b in bufs], plan, n, aliases={t: t for t in range(n)})


def _run_comms(comms, name):
    c_in = [len(cm.arrays) for cm in comms]
    c_out = [len(cm.out_shapes) for cm in comms]
    c_scr = [len(cm.scratch) for cm in comms]
    aliases = {}
    for k, cm in enumerate(comms):
        for a, b in cm.aliases.items():
            aliases[sum(c_in[:k]) + a] = sum(c_out[:k]) + b

    def split(refs, counts):
        out, pos = [], 0
        for cnt in counts:
            out.append(refs[pos:pos + cnt])
            pos += cnt
        return out

    def body(*refs):
        ins = split(refs[:sum(c_in)], c_in)
        outs = split(refs[sum(c_in):sum(c_in) + sum(c_out)], c_out)
        scr = split(refs[sum(c_in) + sum(c_out):], c_scr)
        for phase in ("start", "relay", "finish"):
            for k, cm in enumerate(comms):
                getattr(cm, phase)(ins[k], outs[k], scr[k])

    outs = _pc(body, name=name, out_shape=[s for cm in comms for s in cm.out_shapes],
               in_specs=[ANY] * sum(c_in), out_specs=[ANY] * sum(c_out), input_output_aliases=aliases,
               scratch_shapes=[s for cm in comms for s in cm.scratch],
               compiler_params=_params())(*[a for cm in comms for a in cm.arrays])
    return split(list(outs), c_out)


def _all_gather(arrs, name):
    return _run_comms([_AllGather(arrs)], name)[0]


def _call(body, *, name, grid, in_specs, out_specs, out_shape, scratch_shapes, sem, args, comms=()):
    n_in, n_out, n_scr = len(in_specs), len(out_specs), len(scratch_shapes)
    c_in = [len(cm.arrays) for cm in comms]
    c_out = [len(cm.out_shapes) for cm in comms]
    c_scr = [len(cm.scratch) for cm in comms]
    aliases = {}
    for k, cm in enumerate(comms):
        for a, b in cm.aliases.items():
            aliases[n_in + sum(c_in[:k]) + a] = n_out + sum(c_out[:k]) + b

    def split(refs, counts):
        out, pos = [], 0
        for cnt in counts:
            out.append(refs[pos:pos + cnt])
            pos += cnt
        return out

    def wrapped(*refs):
        ins = refs[:n_in + sum(c_in)]
        outs = refs[len(ins):len(ins) + n_out + sum(c_out)]
        scr = refs[len(ins) + len(outs):]
        cins, couts, cscr = split(ins[n_in:], c_in), split(outs[n_out:], c_out), split(scr[n_scr:], c_scr)
        if comms:
            first = pl.program_id(0) == 0
            last = pl.program_id(0) == grid[0] - 1
            for k in range(1, len(grid)):
                first = jnp.logical_and(first, pl.program_id(k) == 0)
                last = jnp.logical_and(last, pl.program_id(k) == grid[k] - 1)

            @pl.when(first)
            def _():
                for k, cm in enumerate(comms):
                    cm.start(cins[k], couts[k], cscr[k])
        body(*ins[:n_in], *outs[:n_out], *scr[:n_scr])
        if comms:
            relay_early = len(grid) == 1 and grid[0] >= 4
            if relay_early:
                @pl.when(pl.program_id(0) == (7 * grid[0]) // 8 - 1)
                def _():
                    for k, cm in enumerate(comms):
                        cm.relay(cins[k], couts[k], cscr[k])

            @pl.when(last)
            def _():
                for k, cm in enumerate(comms):
                    if not relay_early:
                        cm.relay(cins[k], couts[k], cscr[k])
                    cm.finish(cins[k], couts[k], cscr[k])

    outs = _pc(wrapped, name=name, grid=grid,
               in_specs=list(in_specs) + [ANY] * sum(c_in), out_specs=list(out_specs) + [ANY] * sum(c_out),
               out_shape=list(out_shape) + [s for cm in comms for s in cm.out_shapes],
               scratch_shapes=list(scratch_shapes) + [s for cm in comms for s in cm.scratch],
               input_output_aliases=aliases, compiler_params=_params(*sem),
               )(*args, *[a for cm in comms for a in cm.arrays])
    outs = list(outs)
    return outs[:n_out], split(outs[n_out:], c_out)


def _row_block(r):
    for b in (512, 256, 128, 64, 32, 16, 8):
        if r % b == 0:
            return b
    return r


def _pair_add(g, recv, cj_idx, name):
    _, r, cc = g.shape
    br = _row_block(r)

    def body(cj_ref, g_ref, r_ref, own_ref, pb_ref):
        s = g_ref[...] + r_ref[...]
        pb_ref[...] = s.astype(BF16)

        @pl.when(pl.program_id(1) == cj_ref[1])
        def _():
            own_ref[...] = s[0]

    grid_spec = pltpu.PrefetchScalarGridSpec(
        num_scalar_prefetch=1, grid=(r // br, N_CHIP),
        in_specs=[pl.BlockSpec((1, br, cc), lambda i, j, cj_ref: (2 * j + cj_ref[0], i, 0)),
                  pl.BlockSpec((1, br, cc), lambda i, j, cj_ref: (j, i, 0))],
        out_specs=[pl.BlockSpec((br, cc), lambda i, j, cj_ref: (i, 0)),
                   pl.BlockSpec((1, br, cc), lambda i, j, cj_ref: (j, i, 0))])
    return _pc(body, name=name, grid_spec=grid_spec,
               out_shape=[_sds((r, cc)), _sds((N_CHIP, r, cc), BF16)],
               compiler_params=_params("arbitrary", "arbitrary"))(cj_idx, g, recv)


def _chip_add(p, qs, cj_idx, name):
    r, cc = p.shape
    nq = len(qs)
    br = _row_block(r // nq)
    nb = r // nq // br

    def body(cj_ref, p_ref, *refs):
        o_ref = refs[-1]
        if nq == 2:
            top = pl.program_id(0) < nb
            q = [jnp.where(top, refs[0][k], refs[1][k]).astype(F32) for k in range(3)]
        else:
            q = [refs[0][k].astype(F32) for k in range(3)]
        o_ref[0] = ((p_ref[...] + q[0]) + q[1]) + q[2]

    q_specs = [pl.BlockSpec((3, br, cc), lambda i, cj_ref, h=h: (0, jnp.clip(i - h * nb, 0, nb - 1), 0))
               for h in range(nq)]
    grid_spec = pltpu.PrefetchScalarGridSpec(
        num_scalar_prefetch=1, grid=(r // br,),
        in_specs=[pl.BlockSpec((br, cc), lambda i, cj_ref: (i, 0))] + q_specs,
        out_specs=pl.BlockSpec((1, br, cc), lambda i, cj_ref: (cj_ref[0], i, 0)))
    return _pc(body, name=name, grid_spec=grid_spec, out_shape=_sds((2, r, cc)),
               compiler_params=_params("arbitrary"))(cj_idx, p, *qs)


def _shard_of(both):
    return both.reshape((2 * both.shape[1],) + both.shape[2:])


def _adamw(w, g, m, v, name):
    r, cc = w.shape
    br = _row_block(r)
    if r * cc * 4 <= (1 << 20):
        br = r
    elif br * cc * 4 > (1 << 20) and br > 8:
        br = max(8, (1 << 20) // (cc * 4) // 8 * 8)
        while r % br:
            br -= 8
    c1 = 1.0 - ADAM_B1 ** ADAM_STEP
    c2 = 1.0 - ADAM_B2 ** ADAM_STEP

    def body(w_ref, g_ref, m_ref, v_ref, go_ref, d_ref, mo_ref, vo_ref):
        gg = g_ref[...]
        go_ref[...] = gg
        mn = ADAM_B1 * m_ref[...] + (1.0 - ADAM_B1) * gg
        vn = ADAM_B2 * v_ref[...] + (1.0 - ADAM_B2) * (gg * gg)
        mh = mn / c1
        vh = vn / c2
        d_ref[...] = -ADAM_LR * (mh / (jnp.sqrt(vh) + ADAM_EPS) + ADAM_WD * w_ref[...])
        mo_ref[...] = mn
        vo_ref[...] = vn

    spec = pl.BlockSpec((br, cc), lambda i: (i, 0))
    return _pc(body, name=name, grid=(r // br,), in_specs=[spec] * 4, out_specs=[spec] * 4,
               out_shape=[_sds((r, cc))] * 4, compiler_params=_params("arbitrary"))(w, g, m, v)


def _head(w_half, c8, small, w_ada, b_shard, c_ctx, ret_decay):
    ada_n = w_ada.shape[1]
    mod_sds = _sds((16, ada_n))
    ag_w, ag_c, ag_m = _AllGather([w_half]), _AllGather([c8, small]), _AllGather([mod_sds])
    n_w, n_c, n_m = len(ag_w.scratch), len(ag_c.scratch), len(ag_m.scratch)

    def body(w_ref, c_ref, s_ref, wada_ref, b_ref, cc_ref, rd_ref,
             gw_ref, call_ref, sall_ref, a_ref, modp_ref, mall_ref, lg_ref, sg_ref, *scr):
        scr_w, scr_c, scr_m = scr[:n_w], scr[n_w:n_w + n_c], scr[n_w + n_c:n_w + n_c + n_m]
        c_v, w_v, m_v, sems = scr[n_w + n_c + n_m:]
        ag_w.start((w_ref,), (gw_ref,), scr_w)
        ag_c.start((c_ref, s_ref), (call_ref, sall_ref), scr_c)
        load_w = pltpu.make_async_copy(wada_ref, w_v, sems.at[0])
        load_w.start()
        rd = rd_ref[...]
        lg_ref[...] = -_softplus(-rd)
        sg_ref[...] = _sigmoid(-rd)
        ag_c.relay((c_ref, s_ref), (call_ref, sall_ref), scr_c)
        ag_c.finish((c_ref, s_ref), (call_ref, sall_ref), scr_c)
        load_c = pltpu.make_async_copy(call_ref, c_v, sems.at[1])
        load_c.start()
        load_c.wait()
        a_ref[...] = jnp.zeros_like(a_ref)
        for d in range(N_DEV):
            cd = c_v[d, 0:1, :]
            a_ref[d:d + 1, :] = cd * _sigmoid(cd)
        cc = cc_ref[...]
        a_ref[N_DEV:N_DEV + 1, :] = cc * _sigmoid(cc)
        load_w.wait()
        m_v[...] = jnp.dot(a_ref[...], w_v[...], preferred_element_type=F32,
                           precision=lax.Precision.HIGHEST) + b_ref[...]
        put = pltpu.make_async_copy(m_v, modp_ref, sems.at[2])
        put.start()
        put.wait()
        ag_m.start((modp_ref,), (mall_ref,), scr_m)
        ag_m.relay((modp_ref,), (mall_ref,), scr_m)
        ag_m.finish((modp_ref,), (mall_ref,), scr_m)
        ag_w.relay((w_ref,), (gw_ref,), scr_w)
        ag_w.finish((w_ref,), (gw_ref,), scr_w)

    rd = jnp.broadcast_to(ret_decay.reshape(2, HEADS).T[:, :, None], (HEADS, 2, LANES))
    lane = _full((HEADS, 2, LANES))
    outs = _pc(
        body, name="head",
        in_specs=[ANY, ANY, ANY, ANY, _full((1, ada_n)), _full((1, D_MODEL)), lane],
        out_specs=[ANY, ANY, ANY, _full((16, D_MODEL)), ANY, ANY, lane, lane],
        out_shape=ag_w.out_shapes + ag_c.out_shapes + [_sds((16, D_MODEL)), mod_sds] + ag_m.out_shapes
        + [_sds((HEADS, 2, LANES))] * 2,
        scratch_shapes=ag_w.scratch + ag_c.scratch + ag_m.scratch
        + [pltpu.VMEM((N_DEV,) + c8.shape, F32), pltpu.VMEM(w_ada.shape, F32), pltpu.VMEM((16, ada_n), F32),
           pltpu.SemaphoreType.DMA((3,))],
        compiler_params=_params(),
    )(w_half, c8, small, w_ada, b_shard, c_ctx.reshape(1, D_MODEL), rd)
    gw, c_all, small_all, a16, _, mod_all, lgv, sgv = outs
    return gw, c_all, small_all, a16, mod_all, lgv, sgv


def _ada_grad(at, b):
    n = b.shape[1]
    bn = 512

    def body(a_ref, b_ref, o_ref):
        o_ref[...] = jnp.dot(a_ref[...], b_ref[...], preferred_element_type=F32, precision=lax.Precision.HIGHEST)

    return _pc(body, name="ada_grad", grid=(n // bn,),
               in_specs=[_full((D_MODEL, LANES)), pl.BlockSpec((LANES, bn), lambda i: (0, i))],
               out_specs=pl.BlockSpec((D_MODEL, bn), lambda i: (0, i)), out_shape=_sds((D_MODEL, n)),
               compiler_params=_params("arbitrary"))(at, b)


def _cctx_partial(dmc8, w_ada):
    n = w_ada.shape[1]
    bn = 512

    def body(d_ref, w_ref, o_ref):
        @pl.when(pl.program_id(0) == 0)
        def _():
            o_ref[...] = jnp.zeros_like(o_ref)
        o_ref[...] += lax.dot_general(d_ref[...], w_ref[...], (((1,), (1,)), ((), ())),
                                      preferred_element_type=F32, precision=lax.Precision.HIGHEST)

    return _pc(body, name="cctx_partial", grid=(n // bn,),
               in_specs=[pl.BlockSpec((8, bn), lambda i: (0, i)), pl.BlockSpec((D_MODEL, bn), lambda i: (0, i))],
               out_specs=_full((8, D_MODEL)), out_shape=_sds((8, D_MODEL)),
               compiler_params=_params("arbitrary"))(dmc8, w_ada)


def _cctx_final(parts, c_ctx, m, v):
    c1 = 1.0 - ADAM_B1 ** ADAM_STEP
    c2 = 1.0 - ADAM_B2 ** ADAM_STEP

    def body(p_ref, c_ref, m_ref, v_ref, g_ref, d_ref, mo_ref, vo_ref):
        s = ((p_ref[0, 0:1, :] + p_ref[2, 0:1, :]) + p_ref[4, 0:1, :]) + p_ref[6, 0:1, :]
        z = c_ref[...]
        sg = _sigmoid(z)
        gg = s * (sg * (1.0 + z * (1.0 - sg)))
        g_ref[...] = gg
        mn = ADAM_B1 * m_ref[...] + (1.0 - ADAM_B1) * gg
        vn = ADAM_B2 * v_ref[...] + (1.0 - ADAM_B2) * (gg * gg)
        d_ref[...] = -ADAM_LR * ((mn / c1) / (jnp.sqrt(vn / c2) + ADAM_EPS) + ADAM_WD * z)
        mo_ref[...] = mn
        vo_ref[...] = vn

    row = _full((1, D_MODEL))
    return _pc(body, name="cctx_final", out_shape=[_sds((1, D_MODEL))] * 4,
               in_specs=[_full(parts.shape), row, row, row], out_specs=[row] * 4,
               compiler_params=_params())(parts, c_ctx.reshape(1, D_MODEL), m.reshape(1, D_MODEL), v.reshape(1, D_MODEL))


def _rotary_tables(t_len):
    rows = t_len // GRID_W
    n_freq = DH // 4
    inv = ROPE_BASE ** (-jnp.arange(n_freq, dtype=F32) / n_freq)
    row_ang = jnp.arange(rows, dtype=F32)[:, None] * inv
    col_ang = jnp.arange(GRID_W, dtype=F32)[:, None] * inv

    def spread(fn):
        return jnp.concatenate([jnp.repeat(fn(row_ang), GRID_W, axis=0), jnp.tile(fn(col_ang), (rows, 1))], axis=-1)

    cos, sin = spread(jnp.cos), spread(jnp.sin)
    return jnp.concatenate([cos, cos], axis=-1), jnp.concatenate([-sin, sin], axis=-1)


def _inproj_fwd(x, gn, sh, sc, w4, cos2, sin2, name, comms=()):
    t = x.shape[0]
    tm = _tile(t, True)
    nc = IN_COLS // N_CHIP

    def body(x_ref, gn_ref, sh_ref, sc_ref, w_ref, c_ref, s_ref, p_ref, xr_ref, hb_ref, p_s):
        xh, _ = _rms(x_ref[...])
        h = xh * gn_ref[...] * (1.0 + sc_ref[...]) + sh_ref[...]
        hb = h.astype(BF16)
        hb_ref[...] = hb
        for j in range(N_CHIP):
            p_s[:, nc * j:nc * (j + 1)] = _dot(hb, w_ref[j])
        cc = c_ref[...]
        ss = s_ref[...]
        for hh in range(2 * HEADS):
            blk = p_s[:, DH * hh:DH * (hh + 1)]
            rot = blk * cc + pltpu.roll(blk, DH // 2, 1) * ss
            if hh >= HEADS:
                rot = rot * K_SCALE
            p_ref[:, DH * hh:DH * (hh + 1)] = rot.astype(BF16)
        p_ref[:, 2 * RET_W:] = p_s[:, 2 * RET_W:].astype(BF16)
        xr_ref[...] = p_s[:, 4 * RET_W:4 * RET_W + LRU_W]

    row = _full((1, D_MODEL))
    outs, couts = _call(
        body, name=name, grid=(t // tm,),
        in_specs=[pl.BlockSpec((tm, D_MODEL), lambda i: (i, 0)), row, row, row, _full(w4.shape),
                  pl.BlockSpec((tm, DH), lambda i: (i, 0)), pl.BlockSpec((tm, DH), lambda i: (i, 0))],
        out_specs=[pl.BlockSpec((tm, IN_COLS), lambda i: (i, 0)), pl.BlockSpec((tm, LRU_W), lambda i: (i, 0)),
                   pl.BlockSpec((tm, D_MODEL), lambda i: (i, 0))],
        out_shape=[_sds((t, IN_COLS), BF16), _sds((t, LRU_W)), _sds((t, D_MODEL), BF16)],
        scratch_shapes=[pltpu.VMEM((tm, IN_COLS), F32)], sem=("arbitrary",),
        args=(x, gn, sh, sc, w4, cos2, sin2), comms=comms)
    return (outs, couts) if comms else outs


def _inproj_bwd(x, gn, sh, sc, w4, cos2, sin2, pieces, dres, name):
    t = x.shape[0]
    tm = _tile(t)
    nc = IN_COLS // N_CHIP

    def body(x_ref, gn_ref, sh_ref, sc_ref, w_ref, c_ref, s_ref, dqf, dqb, dkf, dkb, dvf, dvb, dg, dxr, dgt, dres_ref,
             dx_ref, dpb_ref, dgn_ref, dsh_ref, dsc_ref):
        cc = c_ref[...]
        ss = s_ref[...]
        dq = dqf[...].astype(F32) + dqb[...].astype(F32)
        dk = dkf[...].astype(F32) + dkb[...].astype(F32)
        for hh in range(HEADS):
            sl = slice(DH * hh, DH * (hh + 1))
            b = dq[:, sl]
            dpb_ref[:, sl] = (b * cc + pltpu.roll(b * ss, DH // 2, 1)).astype(BF16)
            b = dk[:, sl]
            dpb_ref[:, RET_W + DH * hh:RET_W + DH * (hh + 1)] = (
                (b * cc + pltpu.roll(b * ss, DH // 2, 1)) * K_SCALE).astype(BF16)
        dpb_ref[:, 2 * RET_W:3 * RET_W] = (dvf[...].astype(F32) + dvb[...].astype(F32)).astype(BF16)
        dpb_ref[:, 3 * RET_W:4 * RET_W] = dg[...].astype(BF16)
        dpb_ref[:, 4 * RET_W:4 * RET_W + LRU_W] = dxr[...].astype(BF16)
        dpb_ref[:, 4 * RET_W + LRU_W:IN_COLS] = dgt[...].astype(BF16)
        dh = _dot_nt(dpb_ref[:, 0:nc], w_ref[0])
        for j in range(1, N_CHIP):
            dh = dh + _dot_nt(dpb_ref[:, nc * j:nc * (j + 1)], w_ref[j])
        dx, dgn_t, dsh_t, dsc_t = _norm_mod_bwd(x_ref[...], gn_ref[...], sc_ref[...], dh)
        dx_ref[...] = dres_ref[...] + dx

        @pl.when(pl.program_id(0) == 0)
        def _():
            dgn_ref[...] = jnp.zeros_like(dgn_ref)
            dsh_ref[...] = jnp.zeros_like(dsh_ref)
            dsc_ref[...] = jnp.zeros_like(dsc_ref)
        dgn_ref[...] += dgn_t
        dsh_ref[...] += dsh_t
        dsc_ref[...] += dsc_t

    row = _full((1, D_MODEL))
    pc = pl.BlockSpec((tm, RET_W), lambda i: (i, 0))
    big = pl.BlockSpec((tm, D_MODEL), lambda i: (i, 0))
    return _pc(body, name=name, grid=(t // tm,),
               in_specs=[big, row, row, row, _full(w4.shape),
                         pl.BlockSpec((tm, DH), lambda i: (i, 0)), pl.BlockSpec((tm, DH), lambda i: (i, 0))]
               + [pc] * 9 + [big],
               out_specs=[big, pl.BlockSpec((tm, IN_COLS), lambda i: (i, 0)), row, row, row],
               out_shape=[_sds((t, D_MODEL)), _sds((t, IN_COLS), BF16), _sds((1, D_MODEL)), _sds((1, D_MODEL)),
                          _sds((1, D_MODEL))],
               compiler_params=_params("arbitrary"))(x, gn, sh, sc, w4, cos2, sin2, *pieces, dres)


def _halo_specs(t, tm):
    n8 = tm // SUBLANES
    last8 = t // SUBLANES - 1
    prev = pl.BlockSpec((SUBLANES, LRU_W), lambda i: (jnp.maximum(i * n8 - 1, 0), 0))
    main = pl.BlockSpec((tm, LRU_W), lambda i: (i, 0))
    nxt = pl.BlockSpec((SUBLANES, LRU_W), lambda i: (jnp.minimum((i + 1) * n8, last8), 0))
    return prev, main, nxt


def _with_halo(prev_ref, main_ref, next_ref, i, nt):
    prev = jnp.where(i > 0, prev_ref[...], 0.0)
    nxt = jnp.where(i < nt - 1, next_ref[...], 0.0)
    return jnp.concatenate([prev, main_ref[...], nxt], axis=0)


def _conv_fwd(xr, cw, cb, name):
    t = xr.shape[0]
    tm = _tile(t, True)
    nt = t // tm
    n = tm + 2 * SUBLANES
    mid = slice(SUBLANES, SUBLANES + tm)

    def body(p_ref, m_ref, n_ref, w_ref, b_ref, o_ref):
        xp = _with_halo(p_ref, m_ref, n_ref, pl.program_id(0), nt)
        acc = b_ref[...] + pltpu.roll(xp, 1, 0)[mid] * w_ref[0:1, :]
        acc = acc + xp[mid] * w_ref[1:2, :]
        acc = acc + pltpu.roll(xp, n - 1, 0)[mid] * w_ref[2:3, :]
        acc = acc + pltpu.roll(xp, n - 2, 0)[mid] * w_ref[3:4, :]
        o_ref[...] = acc

    return _pc(body, name=name, grid=(nt,),
               in_specs=[*_halo_specs(t, tm), _full((4, LRU_W)), _full((1, LRU_W))],
               out_specs=pl.BlockSpec((tm, LRU_W), lambda i: (i, 0)), out_shape=_sds((t, LRU_W)),
               compiler_params=_params("arbitrary"))(xr, xr, xr, cw, cb)


def _conv_bwd(dxc_a, dxc_b, xr, cw, name):
    t = xr.shape[0]
    tm = _tile(t, True)
    nt = t // tm
    n = tm + 2 * SUBLANES
    mid = slice(SUBLANES, SUBLANES + tm)

    def body(ap_ref, am_ref, an_ref, bp_ref, bm_ref, bn_ref, xp_ref, xm_ref, xn_ref, w_ref, dx_ref, dw_ref, db_ref):
        i = pl.program_id(0)
        dp = _with_halo(ap_ref, am_ref, an_ref, i, nt) + _with_halo(bp_ref, bm_ref, bn_ref, i, nt)
        xp = _with_halo(xp_ref, xm_ref, xn_ref, i, nt)
        dx = pltpu.roll(dp, n - 1, 0)[mid] * w_ref[0:1, :]
        dx = dx + dp[mid] * w_ref[1:2, :]
        dx = dx + pltpu.roll(dp, 1, 0)[mid] * w_ref[2:3, :]
        dx = dx + pltpu.roll(dp, 2, 0)[mid] * w_ref[3:4, :]
        dx_ref[...] = dx.astype(BF16)
        d = dp[mid]

        @pl.when(i == 0)
        def _():
            dw_ref[...] = jnp.zeros_like(dw_ref)
            db_ref[...] = jnp.zeros_like(db_ref)
        dw_ref[0:1, :] += _sum0(d * pltpu.roll(xp, 1, 0)[mid])
        dw_ref[1:2, :] += _sum0(d * xp[mid])
        dw_ref[2:3, :] += _sum0(d * pltpu.roll(xp, n - 1, 0)[mid])
        dw_ref[3:4, :] += _sum0(d * pltpu.roll(xp, n - 2, 0)[mid])
        db_ref[...] += _sum0(d)

    return _pc(body, name=name, grid=(nt,),
               in_specs=[*_halo_specs(t, tm), *_halo_specs(t, tm), *_halo_specs(t, tm), _full((4, LRU_W))],
               out_specs=[pl.BlockSpec((tm, LRU_W), lambda i: (i, 0)), _full((4, LRU_W)), _full((1, LRU_W))],
               out_shape=[_sds((t, LRU_W), BF16), _sds((4, LRU_W)), _sds((1, LRU_W))],
               compiler_params=_params("arbitrary"))(dxc_a, dxc_a, dxc_a, dxc_b, dxc_b, dxc_b, xr, xr, xr, cw)


def _local_scan(a, b, reverse):
    n = a.shape[0]
    row = lax.broadcasted_iota(jnp.int32, a.shape, 0) & (SUBLANES - 1)
    for s in (1, 2, 4):
        if reverse:
            a_s, b_s, ok = pltpu.roll(a, n - s, 0), pltpu.roll(b, n - s, 0), row < SUBLANES - s
        else:
            a_s, b_s, ok = pltpu.roll(a, s, 0), pltpu.roll(b, s, 0), row >= s
        b = a * jnp.where(ok, b_s, 0.0) + b
        a = a * jnp.where(ok, a_s, 1.0)
    return a, b


def _carry_scan(a_s, b_s, out_ref, carry, reverse):
    ng = a_s.shape[0] // SUBLANES
    shape = carry.shape

    def step(g, cr):
        gg = (ng - 1 - g) if reverse else g
        off = pl.multiple_of(gg * SUBLANES, SUBLANES)
        h = a_s[pl.ds(off, SUBLANES), :] * cr + b_s[pl.ds(off, SUBLANES), :]
        out_ref[pl.ds(off, SUBLANES), :] = h
        edge = h[0:1, :] if reverse else h[SUBLANES - 1:SUBLANES, :]
        return jnp.broadcast_to(edge, shape)

    return lax.fori_loop(0, ng, step, carry)


def _lru_gates(xc, wa_ref, wx_ref, ba, bx, lam):
    xb = xc.astype(BF16)
    r = _sigmoid(_dot(xb, wa_ref[...]) + ba)
    ig = _sigmoid(_dot(xb, wx_ref[...]) + bx)
    sp = _softplus(-lam)
    la = -LRU_C * r * sp
    a = jnp.exp(la)
    mult = jnp.sqrt(_one_minus_sq(la, a))
    return r, ig, sp, a, mult


def _lru_fwd(xc, par_f, par_b, h0_f, h0_b, name, comms=()):
    t = xc.shape[0]
    tm = _tile(t, True)
    nt = t // tm

    def one(x_ref, prm, h0_ref, h_ref, a_s, b_s, c_s, reverse):
        wa_ref, wx_ref, ba_ref, bx_ref, lam_ref = prm

        @pl.when(pl.program_id(0) == 0)
        def _():
            c_s[...] = jnp.broadcast_to(h0_ref[...], c_s.shape)
        xv = x_ref[...]
        _, ig, _, a, mult = _lru_gates(xv, wa_ref, wx_ref, ba_ref[...], bx_ref[...], lam_ref[...])
        al, bl = _local_scan(a, mult * (ig * xv), reverse)
        a_s[...] = al
        b_s[...] = bl
        c_s[...] = _carry_scan(a_s, b_s, h_ref, c_s[...], reverse)

    def body(xf_ref, xb_ref, *refs):
        prm_f, prm_b = refs[0:5], refs[5:10]
        h0f_ref, h0b_ref, hf_ref, hb_ref = refs[10:14]
        af_s, bf_s, cf_s, ab_s, bb_s, cb_s = refs[14:]
        one(xf_ref, prm_f, h0f_ref, hf_ref, af_s, bf_s, cf_s, False)
        one(xb_ref, prm_b, h0b_ref, hb_ref, ab_s, bb_s, cb_s, True)

    vec = _full((1, LRU_W))
    mat = _full((LRU_W, LRU_W))
    fw = pl.BlockSpec((tm, LRU_W), lambda i: (i, 0))
    bw = pl.BlockSpec((tm, LRU_W), lambda i: (nt - 1 - i, 0))
    tile_s = [pltpu.VMEM((tm, LRU_W), F32), pltpu.VMEM((tm, LRU_W), F32), pltpu.VMEM((SUBLANES, LRU_W), F32)]
    (hf, hb), couts = _call(
        body, name=name, grid=(nt,),
        in_specs=[fw, bw] + [mat, mat, vec, vec, vec] * 2 + [vec, vec],
        out_specs=[pl.BlockSpec((tm, LRU_W), lambda i: (i, 0)), pl.BlockSpec((tm, LRU_W), lambda i: (nt - 1 - i, 0))],
        out_shape=[_sds((t, LRU_W))] * 2, scratch_shapes=tile_s + tile_s, sem=("arbitrary",),
        args=(xc, xc, *par_f, *par_b, h0_f, h0_b), comms=comms)
    return ((hf, hb), couts) if comms else (hf, hb)


def _lru_bwd(xc, par_f, par_b, h_f, h_b, h0_f, h0_b, dh_f, dh_b, name, comms=()):
    t = xc.shape[0]
    tm = _tile(t, True)
    nt = t // tm
    n8 = tm // SUBLANES
    last8 = t // SUBLANES - 1
    tile_f = lambda w: pl.BlockSpec((tm, w), lambda i: (nt - 1 - i, 0))
    tile_b = lambda w: pl.BlockSpec((tm, w), lambda i: (i, 0))
    halo_f = pl.BlockSpec((SUBLANES, LRU_W), lambda i: (jnp.maximum((nt - 1 - i) * n8 - 1, 0), 0))
    halo_b = pl.BlockSpec((SUBLANES, LRU_W), lambda i: (jnp.minimum((i + 1) * n8, last8), 0))

    def one(refs_in, refs_out, refs_scr, reverse):
        x_ref, wa_ref, wx_ref, ba_ref, bx_ref, lam_ref, h_ref, halo_ref, h0_ref, dh_ref = refs_in
        dx_ref, dpre_ref, dba_ref, dbx_ref, dlam_ref, dh0_ref = refs_out
        a_s, b_s, l_s, c_s, e_s = refs_scr
        i = pl.program_id(0)

        @pl.when(i == 0)
        def _():
            c_s[...] = jnp.zeros_like(c_s)
            e_s[...] = jnp.zeros_like(e_s)
            dba_ref[...] = jnp.zeros_like(dba_ref)
            dbx_ref[...] = jnp.zeros_like(dbx_ref)
            dlam_ref[...] = jnp.zeros_like(dlam_ref)
        xv = x_ref[...]
        lam = lam_ref[...]
        r, ig, sp, a, mult = _lru_gates(xv, wa_ref, wx_ref, ba_ref[...], bx_ref[...], lam)
        hv = h_ref[...]
        rowi = lax.broadcasted_iota(jnp.int32, (tm, LRU_W), 0)
        edge_a = jnp.broadcast_to(e_s[0:1, :], (tm, LRU_W))
        h0b = jnp.broadcast_to(h0_ref[...], (tm, LRU_W))
        if reverse:
            a_sh = jnp.where(rowi == 0, edge_a, pltpu.roll(a, 1, 0))
            hin_edge = jnp.where(i == nt - 1, h0b, jnp.broadcast_to(halo_ref[0:1, :], (tm, LRU_W)))
            h_in = jnp.where(rowi == tm - 1, hin_edge, pltpu.roll(hv, tm - 1, 0))
        else:
            a_sh = jnp.where(rowi == tm - 1, edge_a, pltpu.roll(a, tm - 1, 0))
            hin_edge = jnp.where(i == nt - 1, h0b, jnp.broadcast_to(halo_ref[SUBLANES - 1:SUBLANES, :], (tm, LRU_W)))
            h_in = jnp.where(rowi == 0, hin_edge, pltpu.roll(hv, 1, 0))
        al, bl = _local_scan(a_sh, dh_ref[...], not reverse)
        a_s[...] = al
        b_s[...] = bl
        c_s[...] = _carry_scan(a_s, b_s, l_s, c_s[...], not reverse)
        e_s[...] = jnp.broadcast_to(a[tm - 1:tm, :] if reverse else a[0:1, :], e_s.shape)
        lmb = l_s[...]
        da = lmb * h_in
        ixc = ig * xv
        dmult = lmb * ixc
        dixc = lmb * mult
        dla = da * a - dmult * (a * a) / mult
        dpr = dla * (-LRU_C * sp) * r * (1.0 - r)
        dpi = dixc * xv * ig * (1.0 - ig)
        dprb = dpr.astype(BF16)
        dpib = dpi.astype(BF16)
        dpre_ref[:, 0:LRU_W] = dprb
        dpre_ref[:, LRU_W:2 * LRU_W] = dpib
        dx_ref[...] = dixc * ig + _dot_nt(dprb, wa_ref[...]) + _dot_nt(dpib, wx_ref[...])
        dba_ref[...] += _sum0(dpr)
        dbx_ref[...] += _sum0(dpi)
        dlam_ref[...] += _sum0(dla * (-LRU_C * r)) * (-_sigmoid(-lam))

        @pl.when(i == nt - 1)
        def _():
            al0 = a * lmb
            dh0_ref[...] = al0[tm - 1:tm, :] if reverse else al0[0:1, :]

    def body(*refs):
        one(refs[0:10], refs[20:26], refs[32:37], False)
        one(refs[10:20], refs[26:32], refs[37:42], True)

    vec = _full((1, LRU_W))
    mat = _full((LRU_W, LRU_W))

    def in_specs(tile, halo):
        return [tile(LRU_W), mat, mat, vec, vec, vec, tile(LRU_W), halo, vec, tile(LRU_W)]

    def out_specs(tile):
        return [tile(LRU_W), tile(2 * LRU_W), vec, vec, vec, vec]

    out_one = [_sds((t, LRU_W)), _sds((t, 2 * LRU_W), BF16)] + [_sds((1, LRU_W))] * 4
    scr_one = [pltpu.VMEM((tm, LRU_W), F32)] * 3 + [pltpu.VMEM((SUBLANES, LRU_W), F32)] * 2
    outs, couts = _call(
        body, name=name, grid=(nt,), in_specs=in_specs(tile_f, halo_f) + in_specs(tile_b, halo_b),
        out_specs=out_specs(tile_f) + out_specs(tile_b), out_shape=out_one + out_one,
        scratch_shapes=scr_one + scr_one, sem=("arbitrary",),
        args=(xc, *par_f, h_f, h_f, h0_f, dh_f, xc, *par_b, h_b, h_b, h0_b, dh_b), comms=comms)
    return (tuple(outs[0:6]), tuple(outs[6:12])), couts


def _decay_tables(lg, reverse):
    ci = lax.broadcasted_iota(jnp.int32, (CHUNK, CHUNK), 0).astype(F32)
    mi = lax.broadcasted_iota(jnp.int32, (CHUNK, CHUNK), 1).astype(F32)
    rel = (mi - ci) if reverse else (ci - mi)
    relc = jnp.maximum(rel, 0.0)
    lg_c = jnp.concatenate([lg] * (CHUNK // LANES), axis=1)
    dm = jnp.where(rel >= 0, jnp.exp(lg_c * relc), 0.0)
    cd = lax.broadcasted_iota(jnp.int32, (CHUNK, DH), 0).astype(F32)
    pq, ps = (CHUNK - cd, cd) if reverse else (cd + 1.0, CHUNK - 1.0 - cd)
    return relc, dm, jnp.exp(lg * pq), jnp.exp(lg * ps), jnp.exp(lg * float(CHUNK)), pq, ps


def _ret_fwd(proj, lgv, s0f, s0b, comms=()):
    t = proj.shape[0]
    n = t // CHUNK

    def one(q, k, v, lg, s_s, hh, o_ref, sp_ref, reverse):
        _, dm, wq, ws, g, _, _ = _decay_tables(lg, reverse)
        vb = v.astype(BF16)
        p = _dot_nt(q.astype(BF16), k.astype(BF16)) * dm
        s = s_s[hh]
        sp_ref[hh, 0] = s
        o_ref[:, DH * hh:DH * (hh + 1)] = _dot(p.astype(BF16), vb) + _dot((q * wq).astype(BF16), s.astype(BF16))
        s_s[hh] = g * s + _dot_tn((k * ws).astype(BF16), vb)

    def body(qf, kf, vf, qb, kb, vb, lg_ref, s0f_ref, s0b_ref, of_ref, ob_ref, spf_ref, spb_ref, sf_s, sb_s):
        @pl.when(pl.program_id(0) == 0)
        def _():
            sf_s[...] = s0f_ref[...]
            sb_s[...] = s0b_ref[...]
        for hh in range(HEADS):
            sl = slice(DH * hh, DH * (hh + 1))
            one(qf[:, sl].astype(F32), kf[:, sl].astype(F32), vf[:, sl], lg_ref[hh, 0:1, :], sf_s, hh, of_ref, spf_ref,
                False)
            one(qb[:, sl].astype(F32), kb[:, sl].astype(F32), vb[:, sl], lg_ref[hh, 1:2, :], sb_s, hh, ob_ref, spb_ref,
                True)

    blk = (CHUNK, RET_W)
    fw = [pl.BlockSpec(blk, lambda i, o=o: (i, o)) for o in range(3)]
    bw = [pl.BlockSpec(blk, lambda i, o=o: (n - 1 - i, o)) for o in range(3)]
    st = _full((HEADS, DH, DH))
    return _call(body, name="ret_fwd", grid=(n,),
                 in_specs=fw + bw + [_full((HEADS, 2, LANES)), st, st],
                 out_specs=[pl.BlockSpec(blk, lambda i: (i, 0)), pl.BlockSpec(blk, lambda i: (n - 1 - i, 0)),
                            pl.BlockSpec((HEADS, 1, DH, DH), lambda i: (0, i, 0, 0)),
                            pl.BlockSpec((HEADS, 1, DH, DH), lambda i: (0, n - 1 - i, 0, 0))],
                 out_shape=[_sds((t, RET_W)), _sds((t, RET_W)), _sds((HEADS, n, DH, DH)), _sds((HEADS, n, DH, DH))],
                 scratch_shapes=[pltpu.VMEM((HEADS, DH, DH), F32), pltpu.VMEM((HEADS, DH, DH), F32)],
                 sem=("arbitrary",), args=(proj, proj, proj, proj, proj, proj, lgv, s0f, s0b), comms=comms)


def _ret_bwd(proj, lgv, sgv, spf, spb, do, comms=()):
    t = proj.shape[0]
    n = t // CHUNK

    def one(q_ref, k_ref, v_ref, lg_ref, s_ref, do_ref, dq_ref, dk_ref, dv_ref, ds_s, acc_s, reverse):
        d = 1 if reverse else 0
        for hh in range(HEADS):
            sl = slice(DH * hh, DH * (hh + 1))
            relc, dm, wq, ws, g, pq, ps = _decay_tables(lg_ref[hh, d:d + 1, :], reverse)
            qb, kb, vb = q_ref[:, sl], k_ref[:, sl], v_ref[:, sl]
            q, k = qb.astype(F32), kb.astype(F32)
            p = _dot_nt(qb, kb) * dm
            s = s_ref[hh, 0]
            dob = do_ref[:, sl].astype(BF16)
            dsn = ds_s[hh]
            dsb = dsn.astype(BF16)
            dv_ref[:, sl] = (_dot_tn(p.astype(BF16), dob) + _dot((k * ws).astype(BF16), dsb)).astype(BF16)
            dp = _dot_nt(dob, vb)
            dab = (dp * dm).astype(BF16)
            xq = _dot_nt(dob, s.astype(BF16))
            yk = _dot_nt(vb, dsb)
            dq_ref[:, sl] = (_dot(dab, kb) + xq * wq).astype(BF16)
            dk_ref[:, sl] = (_dot_tn(dab, qb) + yk * ws).astype(BF16)
            ds_s[hh] = g * dsn + _dot_tn((q * wq).astype(BF16), dob)
            s_mask = _sum0(dp * p * relc)
            part = (sum(s_mask[:, LANES * u:LANES * (u + 1)] for u in range(CHUNK // LANES))
                    + _sum0(xq * q * wq * pq) + _sum0(yk * k * ws * ps) + _sum0(dsn * s) * g * float(CHUNK))
            acc_s[hh] += jnp.broadcast_to(part, (SUBLANES, LANES))

    def body(qf, kf, vf, qb, kb, vb, lg_ref, sg_ref, sf_ref, sb_ref, dof_ref, dob_ref,
             dqf, dkf, dvf, dqb, dkb, dvb, ds0f_ref, ds0b_ref, drdf_ref, drdb_ref, dsf_s, dsb_s, accf_s, accb_s):
        i = pl.program_id(0)

        @pl.when(i == 0)
        def _():
            for r in (dsf_s, dsb_s, accf_s, accb_s):
                r[...] = jnp.zeros_like(r)
        one(qf, kf, vf, lg_ref, sf_ref, dof_ref, dqf, dkf, dvf, dsf_s, accf_s, False)
        one(qb, kb, vb, lg_ref, sb_ref, dob_ref, dqb, dkb, dvb, dsb_s, accb_s, True)

        @pl.when(i == n - 1)
        def _():
            ds0f_ref[...] = dsf_s[...]
            ds0b_ref[...] = dsb_s[...]
            for d, (acc_s, drd_ref) in enumerate(((accf_s, drdf_ref), (accb_s, drdb_ref))):
                for hh in range(HEADS):
                    tot = jnp.sum(acc_s[hh, 0:1, :], axis=1, keepdims=True)
                    drd_ref[hh] = jnp.broadcast_to(tot, (SUBLANES, LANES)) * sg_ref[hh, d:d + 1, :]

    blk = (CHUNK, RET_W)
    fw = lambda o: pl.BlockSpec(blk, lambda i, o=o: (n - 1 - i, o))
    bw = lambda o: pl.BlockSpec(blk, lambda i, o=o: (i, o))
    lane = _full((HEADS, 2, LANES))
    st = _full((HEADS, DH, DH))
    rd = _full((HEADS, SUBLANES, LANES))
    outs, couts = _call(
        body, name="ret_bwd", grid=(n,),
        in_specs=[fw(0), fw(1), fw(2), bw(0), bw(1), bw(2), lane, lane,
                  pl.BlockSpec((HEADS, 1, DH, DH), lambda i: (0, n - 1 - i, 0, 0)),
                  pl.BlockSpec((HEADS, 1, DH, DH), lambda i: (0, i, 0, 0)), fw(0), bw(0)],
        out_specs=[fw(0), fw(0), fw(0), bw(0), bw(0), bw(0), st, st, rd, rd],
        out_shape=[_sds((t, RET_W), BF16)] * 6 + [_sds((HEADS, DH, DH))] * 2 + [_sds((HEADS, SUBLANES, LANES))] * 2,
        scratch_shapes=[pltpu.VMEM((HEADS, DH, DH), F32)] * 2 + [pltpu.VMEM((HEADS, SUBLANES, LANES), F32)] * 2,
        sem=("arbitrary",), args=(proj, proj, proj, proj, proj, proj, lgv, sgv, spf, spb, do, do), comms=comms)
    dqf, dkf, dvf, dqb, dkb, dvb, ds0f, ds0b, drdf, drdb = outs
    return ((dqf, dkf, dvf, ds0f, drdf), (dqb, dkb, dvb, ds0b, drdb)), couts


def _ctx_weights(lg, l_len, reverse):
    pos = lax.broadcasted_iota(jnp.int32, (l_len, DH), 0).astype(F32)
    steps = pos if reverse else (l_len - 1.0 - pos)
    return jnp.exp(lg * steps), steps


def _ctx_state_fwd(projc, lgv):
    l_len = projc.shape[0]

    def body(k_ref, v_ref, lg_ref, sf_ref, sb_ref):
        k = k_ref[...]
        vb = v_ref[...].astype(BF16)
        for d, o_ref in ((0, sf_ref), (1, sb_ref)):
            w, _ = _ctx_weights(lg_ref[0, d:d + 1, :], l_len, d == 1)
            o_ref[0] = _dot_tn((k * w).astype(BF16), vb)

    st = pl.BlockSpec((1, DH, DH), lambda h: (h, 0, 0))
    return _pc(body, name="ctx_state_fwd", grid=(HEADS,),
               in_specs=[pl.BlockSpec((l_len, DH), lambda h: (0, HEADS + h)),
                         pl.BlockSpec((l_len, DH), lambda h: (0, 2 * HEADS + h)),
                         pl.BlockSpec((1, 2, LANES), lambda h: (h, 0, 0))],
               out_specs=[st, st], out_shape=[_sds((HEADS, DH, DH))] * 2,
               compiler_params=_params("arbitrary"))(projc, projc, lgv)


def _ctx_state_bwd(projc, lgv, sgv, dsf, dsb):
    l_len = projc.shape[0]

    def body(k_ref, v_ref, lg_ref, sg_ref, dsf_ref, dsb_ref, dk_ref, dv_ref, drd_ref):
        k = k_ref[...]
        vb = v_ref[...].astype(BF16)
        dk = jnp.zeros((l_len, DH), F32)
        dv = jnp.zeros((l_len, DH), F32)
        rows = []
        for d, ds_ref in ((0, dsf_ref), (1, dsb_ref)):
            w, steps = _ctx_weights(lg_ref[0, d:d + 1, :], l_len, d == 1)
            dsb16 = ds_ref[0].astype(BF16)
            dkw = _dot_nt(vb, dsb16)
            dk = dk + dkw * w
            dv = dv + _dot((k * w).astype(BF16), dsb16)
            tot = jnp.sum(_sum0(dkw * k * w * steps), axis=1, keepdims=True)
            rows.append(jnp.broadcast_to(tot, (1, LANES)) * sg_ref[0, d:d + 1, :])
        dk_ref[...] = dk.astype(BF16)
        dv_ref[...] = dv.astype(BF16)
        rid = lax.broadcasted_iota(jnp.int32, (SUBLANES, LANES), 0)
        drd_ref[0] = jnp.where(rid == 0, rows[0], jnp.where(rid == 1, rows[1], 0.0))

    st = pl.BlockSpec((1, DH, DH), lambda h: (h, 0, 0))
    lane = pl.BlockSpec((1, 2, LANES), lambda h: (h, 0, 0))
    hc = pl.BlockSpec((l_len, DH), lambda h: (0, h))
    return _pc(body, name="ctx_state_bwd", grid=(HEADS,),
               in_specs=[pl.BlockSpec((l_len, DH), lambda h: (0, HEADS + h)),
                         pl.BlockSpec((l_len, DH), lambda h: (0, 2 * HEADS + h)), lane, lane, st, st],
               out_specs=[hc, hc, pl.BlockSpec((1, SUBLANES, LANES), lambda h: (h, 0, 0))],
               out_shape=[_sds((l_len, RET_W), BF16), _sds((l_len, RET_W), BF16), _sds((HEADS, SUBLANES, LANES))],
               compiler_params=_params("arbitrary"))(projc, projc, lgv, sgv, dsf, dsb)


G_BLOCK = (3 * RET_W) // RET_W
GATE_BLOCK = (4 * RET_W + LRU_W) // LRU_W


def _head_norm(y):
    yc = y - jnp.mean(y, axis=-1, keepdims=True)
    rs = lax.rsqrt(jnp.mean(yc * yc, axis=-1, keepdims=True) + EPS)
    return yc * rs, rs


def _gelu_parts(z):
    th = jnp.tanh(GELU_K * (z + GELU_C * z * z * z))
    return 0.5 * z * (1.0 + th), th


def _mix_fwd(o_f, o_b, proj, hf, hb, w_out, x, g1):
    t = x.shape[0]
    tm = _tile(t, True)

    def body(of_ref, ob_ref, g_ref, gt_ref, hf_ref, hb_ref, w_ref, x_ref, g1_ref, x1_ref, cat_ref):
        o = of_ref[...] + ob_ref[...]
        g = g_ref[...].astype(F32)
        for hh in range(HEADS):
            sl = slice(DH * hh, DH * (hh + 1))
            nrm, _ = _head_norm(o[:, sl])
            gh = g[:, sl]
            cat_ref[:, sl] = (gh * _sigmoid(gh) * nrm).astype(BF16)
        gel, _ = _gelu_parts(gt_ref[...].astype(F32))
        cat_ref[:, RET_W:] = ((hf_ref[...] + hb_ref[...]) * gel).astype(BF16)
        x1_ref[...] = x_ref[...] + g1_ref[...] * _dot(cat_ref[...], w_ref[...])

    half = pl.BlockSpec((tm, RET_W), lambda i: (i, 0))
    big = pl.BlockSpec((tm, D_MODEL), lambda i: (i, 0))
    return _pc(body, name="mix_fwd", grid=(t // tm,),
               in_specs=[half, half, pl.BlockSpec((tm, RET_W), lambda i: (i, G_BLOCK)),
                         pl.BlockSpec((tm, LRU_W), lambda i: (i, GATE_BLOCK)), half, half,
                         _full((D_MODEL, D_MODEL)), big, _full((1, D_MODEL))],
               out_specs=[big, big], out_shape=[_sds((t, D_MODEL)), _sds((t, D_MODEL), BF16)],
               compiler_params=_params("arbitrary"))(o_f, o_b, proj, proj, hf, hb, w_out, x, g1)


def _mix_bwd(o_f, o_b, proj, hf, hb, w_out, cat, dx1, g1, comms=()):
    t = dx1.shape[0]
    tm = _tile(t, True)

    def body(of_ref, ob_ref, g_ref, gt_ref, hf_ref, hb_ref, w_ref, cat_ref, dx1_ref, g1_ref,
             do_ref, dhs_ref, dg_ref, dgt_ref, dyb_ref, dg1_ref):
        dx1v = dx1_ref[...]
        y = _dot(cat_ref[...], w_ref[...])

        @pl.when(pl.program_id(0) == 0)
        def _():
            dg1_ref[...] = jnp.zeros_like(dg1_ref)
        dg1_ref[...] += _sum0(dx1v * y)
        dyb = (g1_ref[...] * dx1v).astype(BF16)
        dyb_ref[...] = dyb
        dcat = _dot_nt(dyb, w_ref[...])
        o = of_ref[...] + ob_ref[...]
        g = g_ref[...].astype(F32)
        for hh in range(HEADS):
            sl = slice(DH * hh, DH * (hh + 1))
            nrm, rs = _head_norm(o[:, sl])
            gh = g[:, sl]
            sg = _sigmoid(gh)
            dret = dcat[:, sl]
            dg_ref[:, sl] = (dret * nrm * (sg * (1.0 + gh * (1.0 - sg)))).astype(BF16)
            dn = dret * (gh * sg)
            dyc = rs * (dn - nrm * jnp.mean(dn * nrm, axis=-1, keepdims=True))
            do_ref[:, sl] = (dyc - jnp.mean(dyc, axis=-1, keepdims=True)).astype(BF16)
        z = gt_ref[...].astype(F32)
        gel, th = _gelu_parts(z)
        dlru = dcat[:, RET_W:]
        dhs_ref[...] = dlru * gel
        dgel = 0.5 * (1.0 + th) + 0.5 * z * (1.0 - th * th) * GELU_K * (1.0 + 3.0 * GELU_C * z * z)
        dgt_ref[...] = (dlru * (hf_ref[...] + hb_ref[...]) * dgel).astype(BF16)

    half = pl.BlockSpec((tm, RET_W), lambda i: (i, 0))
    big = pl.BlockSpec((tm, D_MODEL), lambda i: (i, 0))
    return _call(body, name="mix_bwd", grid=(t // tm,),
                 in_specs=[half, half, pl.BlockSpec((tm, RET_W), lambda i: (i, G_BLOCK)),
                           pl.BlockSpec((tm, LRU_W), lambda i: (i, GATE_BLOCK)), half, half,
                           _full((D_MODEL, D_MODEL)), big, big, _full((1, D_MODEL))],
                 out_specs=[half, half, half, half, big, _full((1, D_MODEL))],
                 out_shape=[_sds((t, RET_W), BF16), _sds((t, RET_W)), _sds((t, RET_W), BF16), _sds((t, RET_W), BF16),
                            _sds((t, D_MODEL), BF16), _sds((1, D_MODEL))],
                 scratch_shapes=[], sem=("arbitrary",), args=(o_f, o_b, proj, proj, hf, hb, w_out, cat, dx1, g1),
                 comms=comms)


def _mlp(x1, n2g, sh2, sc2, g2, fg, w1_parts, w2_parts, tgt):
    t = x1.shape[0]
    tm = _tile(t)
    hb_ = MLP_H // N_CHIP
    q_rows = hb_ // 4
    n_cp = 4 * N_DEV

    def body(x1_ref, n2g_ref, sh2_ref, sc2_ref, g2_ref, fg_ref, w1a, w1b, w2a, w2b, tgt_ref,
             dx1_ref, h2b_ref, ab_ref, dub_ref, dmb_ref, dsc_ref, dsh_ref, dg2_ref, dn2_ref, dfg_ref, loss_ref,
             w1_s, w2_s, r_s, sems):
        @pl.when(pl.program_id(0) == 0)
        def _():
            cps = []
            for p, parts in enumerate(((w1a, w2a), (w1b, w2b))):
                for d in range(N_DEV):
                    rows = pl.ds(2 * q_rows * (d % 2) + q_rows * p, q_rows)
                    for src, dst in zip(parts, (w1_s, w2_s)):
                        cps.append(pltpu.make_async_copy(src.at[d], dst.at[d // 2, rows], sems.at[len(cps)]))
            for cp in cps:
                cp.start()
            for r in (dsc_ref, dsh_ref, dg2_ref, dn2_ref, dfg_ref, loss_ref):
                r[...] = jnp.zeros_like(r)
            for cp in cps:
                cp.wait()
        x1v = x1_ref[...]
        n2g, sc2, g2, fg = n2g_ref[...], sc2_ref[...], g2_ref[...], fg_ref[...]
        xh, _ = _rms(x1v)
        h2b = (xh * n2g * (1.0 + sc2) + sh2_ref[...]).astype(BF16)
        h2b_ref[...] = h2b
        m = jnp.zeros((tm, D_MODEL), F32)
        for j in range(N_CHIP):
            sl = slice(hb_ * j, hb_ * (j + 1))
            r = jnp.maximum(_dot(h2b, w1_s[j]), 0.0)
            r_s[:, sl] = r
            ab = (r * r).astype(BF16)
            ab_ref[:, sl] = ab
            m = m + _dot(ab, w2_s[j])
        x2 = x1v + g2 * m
        x2h, r2 = _rms(x2)
        err = x2h * fg - tgt_ref[...]
        loss_ref[...] += _sum0(err * err)
        dout = err * (1.0 / D_MODEL)
        dfg_ref[...] += _sum0(dout * x2h)
        dxh = dout * fg
        dx2 = r2 * (dxh - x2h * jnp.mean(dxh * x2h, axis=-1, keepdims=True))
        dg2_ref[...] += _sum0(dx2 * m)
        dmb = (g2 * dx2).astype(BF16)
        dmb_ref[...] = dmb
        dh2 = jnp.zeros((tm, D_MODEL), F32)
        for j in range(N_CHIP):
            sl = slice(hb_ * j, hb_ * (j + 1))
            dub = (_dot_nt(dmb, w2_s[j]) * (2.0 * r_s[:, sl])).astype(BF16)
            dub_ref[:, sl] = dub
            dh2 = dh2 + _dot_nt(dub, w1_s[j])
        dx, dn2_t, dsh_t, dsc_t = _norm_mod_bwd(x1v, n2g, sc2, dh2)
        dx1_ref[...] = dx2 + dx
        dn2_ref[...] += dn2_t
        dsh_ref[...] += dsh_t
        dsc_ref[...] += dsc_t

        @pl.when(pl.program_id(0) == t // tm - 1)
        def _():
            tot = jnp.sum(loss_ref[...], axis=1, keepdims=True) * (0.5 / D_MODEL)
            loss_ref[...] = jnp.broadcast_to(tot, loss_ref.shape)

    row = _full((1, D_MODEL))
    big = pl.BlockSpec((tm, D_MODEL), lambda i: (i, 0))
    wide = pl.BlockSpec((tm, MLP_H), lambda i: (i, 0))
    return _pc(body, name="mlp", grid=(t // tm,),
               in_specs=[big, row, row, row, row, row, ANY, ANY, ANY, ANY, big],
               out_specs=[big, big, wide, wide, big, row, row, row, row, row, row],
               out_shape=[_sds((t, D_MODEL)), _sds((t, D_MODEL), BF16), _sds((t, MLP_H), BF16), _sds((t, MLP_H), BF16),
                          _sds((t, D_MODEL), BF16)] + [_sds((1, D_MODEL))] * 6,
               scratch_shapes=[pltpu.VMEM((N_CHIP, D_MODEL, hb_), BF16), pltpu.VMEM((N_CHIP, hb_, D_MODEL), BF16),
                               pltpu.VMEM((tm, MLP_H), F32), pltpu.SemaphoreType.DMA((n_cp,))],
               compiler_params=_params("arbitrary"))(x1, n2g, sh2, sc2, g2, fg, *w1_parts, *w2_parts, tgt)


def _tn(a, b, nj, a_blocked, b_blocked, name, extra=None, comms=()):
    t = a.shape[0]
    m = a.shape[1] // (nj if a_blocked else 1)
    n = b.shape[1] // (nj if b_blocked else 1)
    bk = next((b for b in (2048, 1024, 512) if t % b == 0), t)
    nk = t // bk
    a_col = (lambda j: j) if a_blocked else (lambda j: 0)
    b_col = (lambda j: j) if b_blocked else (lambda j: 0)
    in_specs = [pl.BlockSpec((bk, m), lambda j, k: (k, a_col(j))), pl.BlockSpec((bk, n), lambda j, k: (k, b_col(j)))]
    args = [a, b]
    if extra is not None:
        a2, b2 = extra
        t2 = a2.shape[0]
        in_specs += [pl.BlockSpec((t2, m), lambda j, k: (0, a_col(j))),
                     pl.BlockSpec((t2, n), lambda j, k: (0, b_col(j)))]
        args += [a2, b2]

    def body(*refs):
        a_ref, b_ref = refs[0], refs[1]
        o_ref, acc = refs[-2], refs[-1]
        k = pl.program_id(1)

        @pl.when(k == 0)
        def _():
            acc[...] = jnp.zeros_like(acc)
        acc[...] += _dot_tn(a_ref[...].astype(BF16), b_ref[...].astype(BF16))

        @pl.when(k == nk - 1)
        def _():
            if extra is not None:
                acc[...] += _dot_tn(refs[2][...].astype(BF16), refs[3][...].astype(BF16))
            o_ref[0] = acc[...]

    (out,), couts = _call(body, name=name, grid=(nj, nk), in_specs=in_specs,
                          out_specs=[pl.BlockSpec((1, m, n), lambda j, k: (j, 0, 0))], out_shape=[_sds((nj, m, n))],
                          scratch_shapes=[pltpu.VMEM((m, n), F32)], sem=("arbitrary", "arbitrary"), args=args,
                          comms=comms)
    return (out, couts) if comms else out


ROW_LOSS = 0
ROW_DMOD = 1
ROW_DMODC = 7
ROW_N1, ROW_N2, ROW_FG, ROW_CB = 9, 10, 11, 12
ROW_BA, ROW_BX, ROW_LAM = 13, 15, 17
ROW_CW = 20
ROW_RD = 24
SLAB_ROWS = 32
SEG = D_MODEL // 2


def _pack_small(rows, drd, cw2, cb2, lru2, gates):
    n_rows, n_lru = len(rows), len(lru2)

    def body(*refs):
        r = refs[:n_rows]
        drd_f, drd_b, drd_c, cw_a, cw_b, cb_a, cb_b = refs[n_rows:n_rows + 7]
        lru = refs[n_rows + 7:n_rows + 7 + n_lru]
        gf_ref, gb_ref, slab, ga, gx = refs[n_rows + 7 + n_lru:]
        slab[...] = jnp.zeros_like(slab)
        slab[ROW_LOSS:ROW_LOSS + 1, :] = r[0][...]
        for k in range(N_MOD):
            slab[ROW_DMOD + k:ROW_DMOD + k + 1, :] = r[1 + k][...]
        slab[ROW_DMODC:ROW_DMODC + 1, :] = r[7][...]
        slab[ROW_DMODC + 1:ROW_DMODC + 2, :] = r[8][...]
        slab[ROW_N1:ROW_N1 + 1, :] = r[9][...] + r[10][...]
        slab[ROW_N2:ROW_N2 + 1, :] = r[11][...]
        slab[ROW_FG:ROW_FG + 1, :] = r[12][...]
        slab[ROW_CB:ROW_CB + 1, 0:LRU_W] = cb_a[...] + cb_b[...]
        for k, row in enumerate((ROW_BA, ROW_BA + 1, ROW_BX, ROW_BX + 1, ROW_LAM, ROW_LAM + 1)):
            slab[row:row + 1, 0:LRU_W] = lru[2 * k][...] + lru[2 * k + 1][...]
        slab[ROW_CW:ROW_CW + 4, 0:LRU_W] = cw_a[...] + cw_b[...]
        for h in range(HEADS):
            slab[ROW_RD + h:ROW_RD + h + 1, 0:LANES] = drd_f[h, 0:1, :] + drd_c[h, 0:1, :]
            slab[ROW_RD + HEADS + h:ROW_RD + HEADS + h + 1, 0:LANES] = drd_b[h, 0:1, :] + drd_c[h, 1:2, :]
        for d, g_ref in enumerate((gf_ref, gb_ref)):
            for n in range(LRU_BLOCKS):
                blk = slice(LRU_BD * n, LRU_BD * (n + 1))
                ga[blk, LRU_BD * d:LRU_BD * (d + 1)] = g_ref[0, blk, blk].astype(BF16)
                gx[blk, LRU_BD * d:LRU_BD * (d + 1)] = g_ref[1, blk, blk].astype(BF16)

    args = list(rows) + list(drd) + list(cw2) + list(cb2) + list(lru2) + list(gates)
    gate_shape = (LRU_W, 2 * LRU_BD)
    return _pc(body, name="pack_small", in_specs=[_full(a.shape) for a in args],
               out_specs=[_full((SLAB_ROWS, D_MODEL)), _full(gate_shape), _full(gate_shape)],
               out_shape=[_sds((SLAB_ROWS, D_MODEL)), _sds(gate_shape, BF16), _sds(gate_shape, BF16)],
               compiler_params=_params())(*args)


def _adam_math(w, g, m, v):
    mn = ADAM_B1 * m + (1.0 - ADAM_B1) * g
    vn = ADAM_B2 * v + (1.0 - ADAM_B2) * (g * g)
    mh = mn / (1.0 - ADAM_B1 ** ADAM_STEP)
    vh = vn / (1.0 - ADAM_B2 ** ADAM_STEP)
    return -ADAM_LR * (mh / (jnp.sqrt(vh) + ADAM_EPS) + ADAM_WD * w), mn, vn


SMALL_PARAMS = ("b_ada", "norm1_g", "norm2_g", "final_g", "ret_decay", "conv_w", "conv_b", "lru_wa", "lru_ba", "lru_wx",
                "lru_bx", "lru_lambda")


def _finalize_small(chip_idx, slab_all, ga_all, gx_all, wmv):
    n_p = len(SMALL_PARAMS)
    flat = [a for nm in SMALL_PARAMS for a in wmv[nm]]
    ada_n = N_MOD * D_MODEL // N_CHIP

    def body(c_ref, slab_ref, ga_ref, gx_ref, *refs):
        prm = {nm: refs[3 * k:3 * k + 3] for k, nm in enumerate(SMALL_PARAMS)}
        outs = {nm: refs[3 * n_p + 4 * k:3 * n_p + 4 * k + 4] for k, nm in enumerate(SMALL_PARAMS)}
        b128_ref, dmc_ref, loss_ref = refs[3 * n_p + 4 * n_p:]
        chip = c_ref[0]

        def pick(fn):
            acc = fn(0)
            for j in range(1, N_CHIP):
                acc = jnp.where(chip == j, fn(j), acc)
            return acc

        tot = slab_ref[0]
        for d in range(1, N_DEV):
            tot = tot + slab_ref[d]

        def update(nm, g, sl=None, rows=None):
            w_ref, m_ref, v_ref = prm[nm]
            g_ref, d_ref, mo_ref, vo_ref = outs[nm]
            ix = (slice(None) if rows is None else rows, slice(None) if sl is None else sl)
            dl, mn, vn = _adam_math(w_ref[ix], g, m_ref[ix], v_ref[ix])
            g_ref[ix] = g
            d_ref[ix] = dl
            mo_ref[ix] = mn
            vo_ref[ix] = vn

        loss_ref[...] = jnp.broadcast_to(tot[ROW_LOSS:ROW_LOSS + 1, 0:LANES], (SUBLANES, LANES))
        for k in range(N_MOD):
            g = tot[ROW_DMOD + k:ROW_DMOD + k + 1, :]
            if k < 2:
                g = g + tot[ROW_DMODC + k:ROW_DMODC + k + 1, :]
            update("b_ada", g, slice(D_MODEL * k, D_MODEL * (k + 1)))
        update("norm1_g", tot[ROW_N1:ROW_N1 + 1, :])
        update("norm2_g", tot[ROW_N2:ROW_N2 + 1, :])
        update("final_g", tot[ROW_FG:ROW_FG + 1, :])
        update("ret_decay", tot[ROW_RD:ROW_RD + SUBLANES, 0:LANES])
        update("conv_b", tot[ROW_CB:ROW_CB + 1, 0:LRU_W])
        update("conv_w", pick(lambda j: tot[ROW_CW:ROW_CW + 4, LANES * j:LANES * (j + 1)]))
        for nm, row in (("lru_ba", ROW_BA), ("lru_bx", ROW_BX), ("lru_lambda", ROW_LAM)):
            update(nm, pick(lambda j, row=row: tot[row:row + 2, LANES * j:LANES * (j + 1)]))
        for nm, g_all in (("lru_wa", ga_ref), ("lru_wx", gx_ref)):
            for dr in range(2):
                lanes = slice(LRU_BD * dr, LRU_BD * (dr + 1))
                g = g_all[0, :, lanes].astype(F32)
                for d in range(1, N_DEV):
                    g = g + g_all[d, :, lanes].astype(F32)
                update(nm, g, rows=slice(LRU_W * dr, LRU_W * (dr + 1)))

        def seg(rows6, s):
            return rows6[s // 2][:, SEG * (s % 2):SEG * (s % 2 + 1)]

        b128_ref[...] = jnp.zeros_like(b128_ref)
        dmc_ref[...] = jnp.zeros_like(dmc_ref)
        zero = jnp.zeros((1, D_MODEL), F32)
        ctx6 = [tot[ROW_DMODC:ROW_DMODC + 1, :], tot[ROW_DMODC + 1:ROW_DMODC + 2, :]] + [zero] * (N_MOD - 2)
        for q in range(ada_n // SEG):
            cols = slice(SEG * q, SEG * (q + 1))
            for d in range(N_DEV):
                rows6 = [slab_ref[d, ROW_DMOD + k:ROW_DMOD + k + 1, :] for k in range(N_MOD)]
                b128_ref[d:d + 1, cols] = pick(lambda j, rows6=rows6: seg(rows6, 3 * j + q))
            c = pick(lambda j: seg(ctx6, 3 * j + q))
            b128_ref[N_DEV:N_DEV + 1, cols] = c
            dmc_ref[0:1, cols] = c

    out_shape = []
    for nm in SMALL_PARAMS:
        out_shape += [_sds(wmv[nm][0].shape)] * 4
    out_shape += [_sds((LANES, ada_n)), _sds((SUBLANES, ada_n)), _sds((SUBLANES, LANES))]
    args = [slab_all, ga_all, gx_all] + flat
    grid_spec = pltpu.PrefetchScalarGridSpec(
        num_scalar_prefetch=1, grid=(1,), in_specs=[_full(a.shape) for a in args],
        out_specs=[_full(s.shape) for s in out_shape])
    outs = _pc(body, name="finalize_small", grid_spec=grid_spec, out_shape=out_shape,
               compiler_params=_params("arbitrary"))(chip_idx, *args)
    res = {nm: tuple(outs[4 * k:4 * k + 4]) for k, nm in enumerate(SMALL_PARAMS)}
    return res, outs[4 * n_p], outs[4 * n_p + 1], outs[4 * n_p + 2]


def _block_diag(w):
    eye = jnp.eye(LRU_BLOCKS, dtype=F32)
    return (w[:, :, None, :] * eye[:, None, :, None]).reshape(LRU_W, LRU_W).astype(BF16)


def _lane_rep(v8):
    return jnp.broadcast_to(v8.reshape(SUBLANES, 1), (SUBLANES, LANES))


def kernel(x, c, ctx, c_ctx, w_ada, b_ada, norm1_g, norm2_g, w_in, ret_decay, conv_w, conv_b, lru_wa, lru_ba, lru_wx, lru_bx, lru_lambda, w_out, w_mlp1, w_mlp2, final_g, loss_target, m_c_ctx, m_w_ada, m_b_ada, m_norm1_g, m_norm2_g, m_w_in, m_ret_decay, m_conv_w, m_conv_b, m_lru_wa, m_lru_ba, m_lru_wx, m_lru_bx, m_lru_lambda, m_w_out, m_w_mlp1, m_w_mlp2, m_final_g, v_c_ctx, v_w_ada, v_b_ada, v_norm1_g, v_norm2_g, v_w_in, v_ret_decay, v_conv_w, v_conv_b, v_lru_wa, v_lru_ba, v_lru_wx, v_lru_bx, v_lru_lambda, v_w_out, v_w_mlp1, v_w_mlp2, v_final_g):
    ax, ay, ac = lax.axis_index("x"), lax.axis_index("y"), lax.axis_index("c")
    chip = 2 * ax + ay
    dev = 4 * ax + 2 * ay + ac
    c_idx = jnp.stack([ac, chip]).astype(jnp.int32)
    j_idx = chip.reshape(1).astype(jnp.int32)

    xt = x[0]
    t_len = xt.shape[0]
    ctxt = ctx[0]
    l_len = ctxt.shape[0]
    tgt = loss_target[0]
    ada_n = w_ada.shape[2]

    def my_half(w2d):
        r = w2d.shape[0] // 2
        return lax.dynamic_slice_in_dim(w2d, ac * r, r, axis=0).astype(BF16)

    pad8 = lambda a: jnp.pad(a, ((0, SUBLANES - a.shape[0]), (0, 0)))
    small = jnp.concatenate([pad8(conv_w[0]), pad8(lru_ba[0]), pad8(lru_bx[0]), pad8(lru_lambda[0])], axis=0)
    b_shard = lax.dynamic_slice_in_dim(b_ada, chip * ada_n, ada_n, axis=1)
    gw_in, _, small_all, a16, mod_parts, lgv, sgv = _head(
        my_half(w_in[0]), pad8(c), small, w_ada[0], b_shard, c_ctx, ret_decay[0])
    w4 = gw_in.reshape(N_CHIP, D_MODEL, IN_COLS // N_CHIP)

    mod_all = mod_parts[0::2].transpose(1, 0, 2).reshape(16, N_CHIP * ada_n)
    mod_me = lax.dynamic_slice_in_dim(mod_all, dev, 1, axis=0)
    sh1, sc1, g1, sh2, sc2, g2 = [mod_me[:, D_MODEL * k:D_MODEL * (k + 1)] for k in range(N_MOD)]
    csh1, csc1 = mod_all[8:9, 0:D_MODEL], mod_all[8:9, D_MODEL:2 * D_MODEL]

    cos2, sin2 = _rotary_tables(t_len)
    cos_c, sin_c = jnp.ones((l_len, DH), F32), jnp.zeros((l_len, DH), F32)
    n1g, n2g = norm1_g, norm2_g
    fg = final_g.reshape(1, D_MODEL)

    small_full = small_all[0::2].transpose(1, 0, 2).reshape(4 * SUBLANES, LRU_W)
    cw = small_full[0:4]
    cb = conv_b
    ba_f, ba_b = small_full[8:9], small_full[9:10]
    bx_f, bx_b = small_full[16:17], small_full[17:18]
    lam_f, lam_b = small_full[24:25], small_full[25:26]
    wa_f, wa_b = _block_diag(lru_wa[0, 0]), _block_diag(lru_wa[0, 1])
    wx_f, wx_b = _block_diag(lru_wx[0, 0]), _block_diag(lru_wx[0, 1])
    zero_h = jnp.zeros((1, LRU_W), F32)

    projc, xrc, hcb16 = _inproj_fwd(ctxt, n1g, csh1, csc1, w4, cos_c, sin_c, "inproj_fwd_ctx")
    s_f, s_b = _ctx_state_fwd(projc, lgv)
    xcc = _conv_fwd(xrc, cw, cb, "conv_fwd_ctx")
    par_f, par_b = (wa_f, wx_f, ba_f, bx_f, lam_f), (wa_b, wx_b, ba_b, bx_b, lam_b)
    hcf, hcbk = _lru_fwd(xcc, par_f, par_b, zero_h, zero_h, "lru_fwd_ctx")
    lru_sf, lru_sb = hcf[l_len - 1:l_len], hcbk[0:1]

    h1, h2 = my_half(w_mlp1[0]), my_half(w_mlp2[0])
    q = h1.shape[0] // 2
    (proj, xrl, hb16), ((gw_1a,),) = _inproj_fwd(xt, n1g, sh1, sc1, w4, cos2, sin2, "inproj_fwd",
                                           comms=(_AllGather([h1[:q]]),))
    (o_f, o_b, spf, spb), ((gw_1b, gw_out),) = _ret_fwd(proj, lgv, s_f, s_b,
                                                       comms=(_AllGather([h1[q:], my_half(w_out[0])]),))
    xcl = _conv_fwd(xrl, cw, cb, "conv_fwd")
    (hf, hbk), ((gw_2a, gw_2b),) = _lru_fwd(xcl, par_f, par_b, lru_sf, lru_sb, "lru_fwd",
                                           comms=(_AllGather([h2[:q], h2[q:]]),))
    wo = gw_out.reshape(D_MODEL, D_MODEL)
    x1, cat = _mix_fwd(o_f, o_b, proj, hf, hbk, wo, xt, g1)

    (dx1, h2b, ab, dub, dmb, dsc2, dsh2, dg2, dn2g, dfg, lossv) = _mlp(
        x1, n2g, sh2, sc2, g2, fg, (gw_1a, gw_1b), (gw_2a, gw_2b), tgt)
    gw_mlp1 = _tn(h2b, dub, N_CHIP, False, True, "grad_w_mlp1")
    b_1 = gw_mlp1.reshape(N_DEV, D_MODEL // 2, MLP_H // N_CHIP)
    gw_mlp2, ((r_1,),) = _tn(ab, dmb, N_CHIP, True, False, "grad_w_mlp2", comms=(_pair_exchange([b_1]),))

    half = D_MODEL // 4
    top, bot = (0, half), (half, half)
    b_2 = gw_mlp2.reshape(N_DEV, MLP_H // N_DEV, D_MODEL)
    p_1, pb_1 = _pair_add(b_1, r_1, c_idx, "rs_pair_add_w_mlp1")
    (do, dhs, dg, dgate, dyb, dg1), ((q_1a,), (r_2,)) = _mix_bwd(
        o_f, o_b, proj, hf, hbk, wo, cat, dx1, g1, comms=(_chip_exchange([pb_1], top), _pair_exchange([b_2])))
    gw_o = _tn(cat, dyb, 1, False, False, "grad_w_out")
    b_o = gw_o.reshape(N_DEV, D_MODEL // N_DEV, D_MODEL)
    p_2, pb_2 = _pair_add(b_2, r_2, c_idx, "rs_pair_add_w_mlp2")

    ((dq_f, dk_f, dv_f, ds_f, drd_f), (dq_b, dk_b, dv_b, ds_b, drd_b)), ((q_1b,), (q_2a,), (r_o,)) = _ret_bwd(
        proj, lgv, sgv, spf, spb, do,
        comms=(_chip_exchange([pb_1], bot), _chip_exchange([pb_2], top), _pair_exchange([b_o])))
    p_o, pb_o = _pair_add(b_o, r_o, c_idx, "rs_pair_add_w_out")
    h_1 = _chip_add(p_1, (q_1a, q_1b), c_idx, "rs_chip_add_w_mlp1")

    ((dxc_f, dpre_f, dba_f, dbx_f, dlam_f, dh0_f), (dxc_b, dpre_b, dba_b, dbx_b, dlam_b, dh0_b)), (
        (q_2b,), (q_o,), (f_1,)) = _lru_bwd(
        xcl, par_f, par_b, hf, hbk, lru_sf, lru_sb, dhs, dhs, "lru_bwd",
        comms=(_chip_exchange([pb_2], bot), _chip_exchange([pb_o]), _pair_gather([h_1])))
    h_2 = _chip_add(p_2, (q_2a, q_2b), c_idx, "rs_chip_add_w_mlp2")
    h_o = _chip_add(p_o, (q_o,), c_idx, "rs_chip_add_w_out")
    dxr, dcw, dcb = _conv_bwd(dxc_f, dxc_b, xrl, cw, "conv_bwd")
    grad_x, dpb, dn1g, dsh1, dsc1 = _inproj_bwd(
        xt, n1g, sh1, sc1, w4, cos2, sin2, [dq_f, dq_b, dk_f, dk_b, dv_f, dv_b, dg, dxr, dgate], dx1, "inproj_bwd")

    dkc, dvc, drd_c = _ctx_state_bwd(projc, lgv, sgv, ds_f, ds_b)
    zc = jnp.zeros((l_len, LRU_W), F32)
    dhc_f = lax.dynamic_update_slice(zc, dh0_f, (l_len - 1, 0))
    dhc_b = lax.dynamic_update_slice(zc, dh0_b, (0, 0))
    ((dxcc_f, dprec_f, dbac_f, dbxc_f, dlamc_f, _), (dxcc_b, dprec_b, dbac_b, dbxc_b, dlamc_b, _)), _ = _lru_bwd(
        xcc, par_f, par_b, hcf, hcbk, zero_h, zero_h, dhc_f, dhc_b, "lru_bwd_ctx")
    dxrc, dcw_c, dcb_c = _conv_bwd(dxcc_f, dxcc_b, xrc, cw, "conv_bwd_ctx")
    zr = jnp.zeros((l_len, RET_W), BF16)
    _, dpbc, dn1g_c, dcsh1, dcsc1 = _inproj_bwd(
        ctxt, n1g, csh1, csc1, w4, cos_c, sin_c, [zr, zr, dkc, zr, dvc, zr, zr, dxrc, zr],
        jnp.zeros((l_len, D_MODEL), F32), "inproj_bwd_ctx")

    gw_i = _tn(hb16, dpb, N_CHIP, False, True, "grad_w_in", extra=(hcb16, dpbc))
    b_i = gw_i.reshape(N_DEV, D_MODEL // 2, IN_COLS // N_CHIP)
    gwa_f, ((r_i,), (f_2,), (f_o,)) = _tn(xcl, dpre_f, 2, False, True, "grad_lru_gates_f", extra=(xcc, dprec_f),
                                          comms=(_pair_exchange([b_i]), _pair_gather([h_2]), _pair_gather([h_o])))
    p_i, pb_i = _pair_add(b_i, r_i, c_idx, "rs_pair_add_w_in")
    gwa_b, ((q_i,),) = _tn(xcl, dpre_b, 2, False, True, "grad_lru_gates_b", extra=(xcc, dprec_b),
                           comms=(_chip_exchange([pb_i]),))
    slab, ga, gx = _pack_small(
        [lossv, dsh1, dsc1, dg1, dsh2, dsc2, dg2, dcsh1, dcsc1, dn1g, dn1g_c, dn2g, dfg],
        (drd_f, drd_b, drd_c), (dcw, dcw_c), (dcb, dcb_c),
        (dba_f, dbac_f, dba_b, dbac_b, dbx_f, dbxc_f, dbx_b, dbxc_b, dlam_f, dlamc_f, dlam_b, dlamc_b),
        (gwa_f, gwa_b))
    (f_i,), (slab_all, ga_all, gx_all) = _run_comms(
        [_pair_gather([_chip_add(p_i, (q_i,), c_idx, "rs_chip_add_w_in")]), _AllGather([slab, ga, gx])],
        "tail_exchanges")
    g_in, g_out, g_1, g_2 = _shard_of(f_i), _shard_of(f_o), _shard_of(f_1), _shard_of(f_2)
    big = {}
    for nm, w, g, m, v in (("w_in", w_in, g_in, m_w_in, v_w_in), ("w_out", w_out, g_out, m_w_out, v_w_out),
                           ("w_mlp1", w_mlp1, g_1, m_w_mlp1, v_w_mlp1), ("w_mlp2", w_mlp2, g_2, m_w_mlp2, v_w_mlp2)):
        go, d_, mn, vn = _adamw(w[0], g, m[0], v[0], "adamw_" + nm)
        big[nm] = (go[None], d_[None], mn[None], vn[None])
    params = {
        "b_ada": (b_ada, m_b_ada, v_b_ada), "norm1_g": (norm1_g, m_norm1_g, v_norm1_g),
        "norm2_g": (norm2_g, m_norm2_g, v_norm2_g), "final_g": (final_g, m_final_g, v_final_g),
        "ret_decay": (ret_decay, m_ret_decay, v_ret_decay), "conv_w": (conv_w, m_conv_w, v_conv_w),
        "conv_b": (conv_b, m_conv_b, v_conv_b), "lru_wa": (lru_wa, m_lru_wa, v_lru_wa),
        "lru_ba": (lru_ba, m_lru_ba, v_lru_ba), "lru_wx": (lru_wx, m_lru_wx, v_lru_wx),
        "lru_bx": (lru_bx, m_lru_bx, v_lru_bx), "lru_lambda": (lru_lambda, m_lru_lambda, v_lru_lambda),
    }
    as2d = {
        "b_ada": lambda a: a, "norm1_g": lambda a: a, "norm2_g": lambda a: a, "conv_b": lambda a: a,
        "final_g": lambda a: a.reshape(1, D_MODEL), "ret_decay": lambda a: _lane_rep(a.reshape(-1)),
        "conv_w": lambda a: a[0], "lru_ba": lambda a: a[0], "lru_bx": lambda a: a[0], "lru_lambda": lambda a: a[0],
        "lru_wa": lambda a: a.reshape(2 * LRU_W, LRU_BD), "lru_wx": lambda a: a.reshape(2 * LRU_W, LRU_BD),
    }
    res, b128, dmc8, loss8 = _finalize_small(
        j_idx, slab_all, ga_all, gx_all, {nm: tuple(as2d[nm](a) for a in params[nm]) for nm in SMALL_PARAMS})
    loss = loss8[0, 0]
    small_out = {}
    for nm in SMALL_PARAMS:
        shp = params[nm][0].shape
        if nm == "ret_decay":
            small_out[nm] = tuple(o[:, 0].reshape(shp) for o in res[nm])
        else:
            small_out[nm] = tuple(o.reshape(shp) for o in res[nm])

    g_ada = _ada_grad(jnp.pad(a16.T, ((0, 0), (0, LANES - 16))), b128)
    g_ada, d_ada, m_ada, v_ada = _adamw(w_ada[0], g_ada, m_w_ada[0], v_w_ada[0], "adamw_w_ada")

    (cparts,) = _all_gather([_cctx_partial(dmc8, w_ada[0])], "gather_cctx")
    g_cc, d_cc, m_cc, v_cc = _cctx_final(cparts, c_ctx, m_c_ctx, v_c_ctx)
    small_out["c_ctx"] = tuple(a.reshape(D_MODEL) for a in (g_cc, d_cc, m_cc, v_cc))
    small_out["w_ada"] = (g_ada[None], d_ada[None], m_ada[None], v_ada[None])
    small_out.update(big)

    order = ["c_ctx", "w_ada", "b_ada", "norm1_g", "norm2_g", "w_in", "ret_decay", "conv_w", "conv_b", "lru_wa", "lru_ba",
             "lru_wx", "lru_bx", "lru_lambda", "w_out", "w_mlp1", "w_mlp2", "final_g"]
    outs = [loss, grad_x[None]]
    for k in range(4):
        outs += [small_out[nm][k] for nm in order]
    return tuple(outs)
```

```python
import math

import jax
import jax.numpy as jnp
from jax import lax
from jax.experimental import pallas as pl
from jax.experimental.pallas import tpu as pltpu

F32 = jnp.float32
BF16 = jnp.bfloat16

D_MODEL = 1024
HEADS = 4
DH = 128
CHUNK = 256
RET_W = HEADS * DH
LRU_W = 512
LRU_BLOCKS = 8
LRU_BD = LRU_W // LRU_BLOCKS
LRU_C = 8.0
IN_COLS = 4 * RET_W + 2 * LRU_W
MLP_H = 4 * D_MODEL
N_MOD = 6
GRID_W = 64
ROPE_BASE = 10000.0
K_SCALE = DH ** -0.5
EPS = 1e-6
GELU_K = math.sqrt(2.0 / math.pi)
GELU_C = 0.044715

ADAM_LR = 0.001
ADAM_B1 = 0.9
ADAM_B2 = 0.999
ADAM_EPS = 1e-08
ADAM_WD = 0.01
ADAM_STEP = 10

N_DEV = 8
N_CHIP = 4
SUBLANES = 8
LANES = 128
VMEM_LIMIT_V7X = 56 * 1024 * 1024
MESH = pl.DeviceIdType.MESH
ANY = pl.BlockSpec(memory_space=pl.ANY)


def _pc(body, **kw):
    return pl.pallas_call(body, **kw)


def _params(*sem):
    return pltpu.CompilerParams(dimension_semantics=sem if sem else None, vmem_limit_bytes=VMEM_LIMIT_V7X)


def _tile(t, big=False):
    if big and t >= 1024:
        return 512
    return 256 if t >= 256 else t


def _sds(shape, dtype=F32):
    return jax.ShapeDtypeStruct(tuple(shape), dtype)


def _full(shape):
    nd = len(shape)
    return pl.BlockSpec(tuple(shape), lambda *_: (0,) * nd)


def _sigmoid(x):
    return 1.0 / (1.0 + jnp.exp(-x))


def _log1p_pos(y):
    s = y * (1.0 - y * (0.5 - y * (1.0 / 3.0 - y * (0.25 - y * (0.2 - y / 6.0)))))
    return jnp.where(y < 0.03, s, jnp.log(1.0 + y))


def _softplus(z):
    return jnp.maximum(z, 0.0) + _log1p_pos(jnp.exp(-jnp.abs(z)))


def _one_minus_sq(la, a):
    t = la * (1.0 + la * (0.5 + la * (1.0 / 6.0 + la * (1.0 / 24.0 + la * (1.0 / 120.0)))))
    return jnp.where(la > -0.125, -t, 1.0 - a) * (1.0 + a)


def _rms(x):
    r = lax.rsqrt(jnp.mean(x * x, axis=-1, keepdims=True) + EPS)
    return x * r, r


def _dot(a, b):
    return jnp.dot(a, b, preferred_element_type=F32)


def _dot_nt(a, b):
    return lax.dot_general(a, b, (((1,), (1,)), ((), ())), preferred_element_type=F32)


def _dot_tn(a, b):
    return lax.dot_general(a, b, (((0,), (0,)), ((), ())), preferred_element_type=F32)


def _sum0(x):
    return jnp.sum(x, axis=0, keepdims=True)


def _norm_mod_bwd(x, g, sc, dh):
    xh, r = _rms(x)
    hn = xh * g
    dhn = dh * (1.0 + sc)
    dxh = dhn * g
    dx = r * (dxh - xh * jnp.mean(dxh * xh, axis=-1, keepdims=True))
    return dx, _sum0(dhn * xh), _sum0(dh), _sum0(dh * hn)


def _dev_index(p):
    return 4 * p[0] + 2 * p[1] + p[2]


def _mesh_pos():
    return lax.axis_index("x"), lax.axis_index("y"), lax.axis_index("c")


class _AllGather:
    def __init__(self, arrs):
        n = len(arrs)
        self.arrays = list(arrs)
        self.out_shapes = [_sds((N_DEV,) + a.shape, a.dtype) for a in arrs]
        self.scratch = ([pltpu.VMEM(a.shape, a.dtype) for a in arrs]
                        + [pltpu.SemaphoreType.DMA((7 * n,)), pltpu.SemaphoreType.DMA((7 * n,)),
                           pltpu.SemaphoreType.DMA((n,))])
        self.aliases = {}

    def _parts(self, ins, outs, scr):
        n = len(self.arrays)
        stage = scr[:n]
        send_sems, recv_sems, local_sems = scr[n:]
        x, y, c = _mesh_pos()
        me, sib = (x, y, c), (x, y, 1 - c)
        chips = [(1 - x, y), (x, 1 - y), (1 - x, 1 - y)]

        def copy(t, k, block, to, own=False):
            dst = outs[t].at[_dev_index(block)]
            return pltpu.make_async_remote_copy(
                src_ref=ins[t] if own else dst, dst_ref=dst,
                send_sem=send_sems.at[7 * t + k], recv_sem=recv_sems.at[7 * t + k],
                device_id=to, device_id_type=MESH)

        first = []
        for t in range(n):
            first.append(copy(t, 0, me, sib, own=True))
            for j, ch in enumerate(chips):
                first.append(copy(t, 1 + j, me, (*ch, c), own=True))
        stage_in = [pltpu.make_async_copy(ins[t], stage[t], local_sems.at[t]) for t in range(n)]
        mine = [pltpu.make_async_copy(stage[t], outs[t].at[_dev_index(me)], local_sems.at[t]) for t in range(n)]
        return n, c, me, sib, chips, copy, first, stage_in, mine

    def start(self, ins, outs, scr):
        n, _, _, _, _, _, first, stage_in, mine = self._parts(ins, outs, scr)
        for cp in stage_in:
            cp.start()
        for cp in first:
            cp.start()
        for t in range(n):
            stage_in[t].wait()
            mine[t].start()

    def relay(self, ins, outs, scr):
        n, c, me, sib, chips, copy, _, _, _ = self._parts(ins, outs, scr)
        for j, ch in enumerate(chips):
            for t in range(n):
                copy(t, 1 + j, (*ch, c), me).wait_recv()
                copy(t, 4 + j, (*ch, c), sib).start()

    def finish(self, ins, outs, scr):
        n, c, me, sib, chips, copy, first, _, mine = self._parts(ins, outs, scr)
        passed = [copy(t, 4 + j, (*ch, c), sib) for j, ch in enumerate(chips) for t in range(n)]
        for t in range(n):
            copy(t, 0, sib, me).wait_recv()
            for j, ch in enumerate(chips):
                copy(t, 4 + j, (*ch, 1 - c), me).wait_recv()
        for cp in first + passed:
            cp.wait_send()
        for cp in mine:
            cp.wait()


class _Exchange:
    def __init__(self, arrays, out_shapes, plan, n_copies, aliases=None):
        self.arrays = list(arrays)
        self.out_shapes = list(out_shapes)
        self.plan = plan
        self.scratch = [pltpu.SemaphoreType.DMA((n_copies,)), pltpu.SemaphoreType.DMA((n_copies,))]
        self.aliases = aliases or {}

    def _copies(self, ins, outs, scr):
        send_sems, recv_sems = scr
        snd, rcv = [], []
        for i, (src, dst, peer, lands) in enumerate(self.plan(ins, outs, _mesh_pos())):
            kw = dict(send_sem=send_sems.at[i], recv_sem=recv_sems.at[i], device_id=peer, device_id_type=MESH)
            snd.append(pltpu.make_async_remote_copy(src_ref=src, dst_ref=dst, **kw))
            rcv.append(pltpu.make_async_remote_copy(src_ref=src, dst_ref=lands, **kw))
        return snd, rcv

    def start(self, ins, outs, scr):
        for cp in self._copies(ins, outs, scr)[0]:
            cp.start()

    def relay(self, ins, outs, scr):
        pass

    def finish(self, ins, outs, scr):
        snd, rcv = self._copies(ins, outs, scr)
        for cp in rcv:
            cp.wait_recv()
        for cp in snd:
            cp.wait_send()


def _pair_exchange(grads):
    n = len(grads)

    def plan(ins, outs, pos):
        x, y, c = pos
        return [(ins[t].at[2 * j + (1 - c)], outs[t].at[j], (x, y, 1 - c), outs[t].at[j])
                for t in range(n) for j in range(N_CHIP)]

    return _Exchange(grads, [_sds((N_CHIP,) + g.shape[1:], g.dtype) for g in grads], plan, N_CHIP * n)


def _chip_exchange(parts, rows=None):
    n = len(parts)

    def plan(ins, outs, pos):
        x, y, c = pos
        chips = [(1 - x, y), (x, 1 - y), (1 - x, 1 - y)]

        def src(t, ch):
            blk = ins[t].at[2 * ch[0] + ch[1]]
            return blk if rows is None else blk.at[pl.ds(rows[0], rows[1])]

        return [(src(t, ch), outs[t].at[k], (*ch, c), outs[t].at[k]) for t in range(n) for k, ch in enumerate(chips)]

    shapes = [_sds((3, p.shape[1] if rows is None else rows[1]) + p.shape[2:], p.dtype) for p in parts]
    return _Exchange(parts, shapes, plan, 3 * n)


def _pair_gather(bufs):
    n = len(bufs)

    def plan(ins, outs, pos):
        x, y, c = pos
        return [(ins[t].at[c], outs[t].at[c], (x, y, 1 - c), outs[t].at[1 - c]) for t in range(n)]

    return _Exchange(bufs, [_sds(b.shape, b.dtype) for b in bufs], plan, n, aliases={t: t for t in range(n)})


def _run_comms(comms, name):
    c_in = [len(cm.arrays) for cm in comms]
    c_out = [len(cm.out_shapes) for cm in comms]
    c_scr = [len(cm.scratch) for cm in comms]
    aliases = {}
    for k, cm in enumerate(comms):
        for a, b in cm.aliases.items():
            aliases[sum(c_in[:k]) + a] = sum(c_out[:k]) + b

    def split(refs, counts):
        out, pos = [], 0
        for cnt in counts:
            out.append(refs[pos:pos + cnt])
            pos += cnt
        return out

    def body(*refs):
        ins = split(refs[:sum(c_in)], c_in)
        outs = split(refs[sum(c_in):sum(c_in) + sum(c_out)], c_out)
        scr = split(refs[sum(c_in) + sum(c_out):], c_scr)
        for phase in ("start", "relay", "finish"):
            for k, cm in enumerate(comms):
                getattr(cm, phase)(ins[k], outs[k], scr[k])

    outs = _pc(body, name=name, out_shape=[s for cm in comms for s in cm.out_shapes],
               in_specs=[ANY] * sum(c_in), out_specs=[ANY] * sum(c_out), input_output_aliases=aliases,
               scratch_shapes=[s for cm in comms for s in cm.scratch],
               compiler_params=_params())(*[a for cm in comms for a in cm.arrays])
    return split(list(outs), c_out)


def _all_gather(arrs, name):
    return _run_comms([_AllGather(arrs)], name)[0]


def _call(body, *, name, grid, in_specs, out_specs, out_shape, scratch_shapes, sem, args, comms=()):
    n_in, n_out, n_scr = len(in_specs), len(out_specs), len(scratch_shapes)
    c_in = [len(cm.arrays) for cm in comms]
    c_out = [len(cm.out_shapes) for cm in comms]
    c_scr = [len(cm.scratch) for cm in comms]
    aliases = {}
    for k, cm in enumerate(comms):
        for a, b in cm.aliases.items():
            aliases[n_in + sum(c_in[:k]) + a] = n_out + sum(c_out[:k]) + b

    def split(refs, counts):
        out, pos = [], 0
        for cnt in counts:
            out.append(refs[pos:pos + cnt])
            pos += cnt
        return out

    def wrapped(*refs):
        ins = refs[:n_in + sum(c_in)]
        outs = refs[len(ins):len(ins) + n_out + sum(c_out)]
        scr = refs[len(ins) + len(outs):]
        cins, couts, cscr = split(ins[n_in:], c_in), split(outs[n_out:], c_out), split(scr[n_scr:], c_scr)
        if comms:
            first = pl.program_id(0) == 0
            last = pl.program_id(0) == grid[0] - 1
            for k in range(1, len(grid)):
                first = jnp.logical_and(first, pl.program_id(k) == 0)
                last = jnp.logical_and(last, pl.program_id(k) == grid[k] - 1)

            @pl.when(first)
            def _():
                for k, cm in enumerate(comms):
                    cm.start(cins[k], couts[k], cscr[k])
        body(*ins[:n_in], *outs[:n_out], *scr[:n_scr])
        if comms:
            relay_early = len(grid) == 1 and grid[0] >= 4
            if relay_early:
                @pl.when(pl.program_id(0) == (7 * grid[0]) // 8 - 1)
                def _():
                    for k, cm in enumerate(comms):
                        cm.relay(cins[k], couts[k], cscr[k])

            @pl.when(last)
            def _():
                for k, cm in enumerate(comms):
                    if not relay_early:
                        cm.relay(cins[k], couts[k], cscr[k])
                    cm.finish(cins[k], couts[k], cscr[k])

    outs = _pc(wrapped, name=name, grid=grid,
               in_specs=list(in_specs) + [ANY] * sum(c_in), out_specs=list(out_specs) + [ANY] * sum(c_out),
               out_shape=list(out_shape) + [s for cm in comms for s in cm.out_shapes],
               scratch_shapes=list(scratch_shapes) + [s for cm in comms for s in cm.scratch],
               input_output_aliases=aliases, compiler_params=_params(*sem),
               )(*args, *[a for cm in comms for a in cm.arrays])
    outs = list(outs)
    return outs[:n_out], split(outs[n_out:], c_out)


def _row_block(r):
    for b in (512, 256, 128, 64, 32, 16, 8):
        if r % b == 0:
            return b
    return r


def _pair_add(g, recv, cj_idx, name):
    _, r, cc = g.shape
    br = _row_block(r)

    def body(cj_ref, g_ref, r_ref, own_ref, pb_ref):
        s = g_ref[...] + r_ref[...]
        pb_ref[...] = s.astype(BF16)

        @pl.when(pl.program_id(1) == cj_ref[1])
        def _():
            own_ref[...] = s[0]

    grid_spec = pltpu.PrefetchScalarGridSpec(
        num_scalar_prefetch=1, grid=(r // br, N_CHIP),
        in_specs=[pl.BlockSpec((1, br, cc), lambda i, j, cj_ref: (2 * j + cj_ref[0], i, 0)),
                  pl.BlockSpec((1, br, cc), lambda i, j, cj_ref: (j, i, 0))],
        out_specs=[pl.BlockSpec((br, cc), lambda i, j, cj_ref: (i, 0)),
                   pl.BlockSpec((1, br, cc), lambda i, j, cj_ref: (j, i, 0))])
    return _pc(body, name=name, grid_spec=grid_spec,
               out_shape=[_sds((r, cc)), _sds((N_CHIP, r, cc), BF16)],
               compiler_params=_params("arbitrary", "arbitrary"))(cj_idx, g, recv)


def _chip_add(p, qs, cj_idx, name):
    r, cc = p.shape
    nq = len(qs)
    br = _row_block(r // nq)
    nb = r // nq // br

    def body(cj_ref, p_ref, *refs):
        o_ref = refs[-1]
        if nq == 2:
            top = pl.program_id(0) < nb
            q = [jnp.where(top, refs[0][k], refs[1][k]).astype(F32) for k in range(3)]
        else:
            q = [refs[0][k].astype(F32) for k in range(3)]
        o_ref[0] = ((p_ref[...] + q[0]) + q[1]) + q[2]

    q_specs = [pl.BlockSpec((3, br, cc), lambda i, cj_ref, h=h: (0, jnp.clip(i - h * nb, 0, nb - 1), 0))
               for h in range(nq)]
    grid_spec = pltpu.PrefetchScalarGridSpec(
        num_scalar_prefetch=1, grid=(r // br,),
        in_specs=[pl.BlockSpec((br, cc), lambda i, cj_ref: (i, 0))] + q_specs,
        out_specs=pl.BlockSpec((1, br, cc), lambda i, cj_ref: (cj_ref[0], i, 0)))
    return _pc(body, name=name, grid_spec=grid_spec, out_shape=_sds((2, r, cc)),
               compiler_params=_params("arbitrary"))(cj_idx, p, *qs)


def _shard_of(both):
    return both.reshape((2 * both.shape[1],) + both.shape[2:])


ADAMW_BLOCK_BYTES = 256 * 1024


def _adamw(w, g, m, v, name):
    r, cc = w.shape
    br = max(8, ADAMW_BLOCK_BYTES // (cc * 4) // 8 * 8)
    while r % br:
        br -= 8
    c1 = 1.0 - ADAM_B1 ** ADAM_STEP
    c2 = 1.0 - ADAM_B2 ** ADAM_STEP

    def body(w_ref, g_ref, m_ref, v_ref, go_ref, d_ref, mo_ref, vo_ref):
        gg = g_ref[...]
        go_ref[...] = gg
        mn = ADAM_B1 * m_ref[...] + (1.0 - ADAM_B1) * gg
        vn = ADAM_B2 * v_ref[...] + (1.0 - ADAM_B2) * (gg * gg)
        mh = mn / c1
        vh = vn / c2
        d_ref[...] = -ADAM_LR * (mh / (jnp.sqrt(vh) + ADAM_EPS) + ADAM_WD * w_ref[...])
        mo_ref[...] = mn
        vo_ref[...] = vn

    spec = pl.BlockSpec((br, cc), lambda i: (i, 0))
    return _pc(body, name=name, grid=(r // br,), in_specs=[spec] * 4, out_specs=[spec] * 4,
               out_shape=[_sds((r, cc))] * 4, compiler_params=_params("arbitrary"))(w, g, m, v)


def _head(w_half, c8, small, w_ada, b_shard, c_ctx, ret_decay):
    ada_n = w_ada.shape[1]
    mod_sds = _sds((16, ada_n))
    ag_w, ag_c, ag_m = _AllGather([w_half]), _AllGather([c8, small]), _AllGather([mod_sds])
    n_w, n_c, n_m = len(ag_w.scratch), len(ag_c.scratch), len(ag_m.scratch)

    def body(w_ref, c_ref, s_ref, wada_ref, b_ref, cc_ref, rd_ref,
             gw_ref, call_ref, sall_ref, a_ref, modp_ref, mall_ref, lg_ref, sg_ref, *scr):
        scr_w, scr_c, scr_m = scr[:n_w], scr[n_w:n_w + n_c], scr[n_w + n_c:n_w + n_c + n_m]
        c_v, w_v, m_v, sems = scr[n_w + n_c + n_m:]
        ag_w.start((w_ref,), (gw_ref,), scr_w)
        ag_c.start((c_ref, s_ref), (call_ref, sall_ref), scr_c)
        load_w = pltpu.make_async_copy(wada_ref, w_v, sems.at[0])
        load_w.start()
        rd = rd_ref[...]
        lg_ref[...] = -_softplus(-rd)
        sg_ref[...] = _sigmoid(-rd)
        ag_c.relay((c_ref, s_ref), (call_ref, sall_ref), scr_c)
        ag_c.finish((c_ref, s_ref), (call_ref, sall_ref), scr_c)
        load_c = pltpu.make_async_copy(call_ref, c_v, sems.at[1])
        load_c.start()
        load_c.wait()
        a_ref[...] = jnp.zeros_like(a_ref)
        for d in range(N_DEV):
            cd = c_v[d, 0:1, :]
            a_ref[d:d + 1, :] = cd * _sigmoid(cd)
        cc = cc_ref[...]
        a_ref[N_DEV:N_DEV + 1, :] = cc * _sigmoid(cc)
        load_w.wait()
        m_v[...] = jnp.dot(a_ref[...], w_v[...], preferred_element_type=F32,
                           precision=lax.Precision.HIGHEST) + b_ref[...]
        put = pltpu.make_async_copy(m_v, modp_ref, sems.at[2])
        put.start()
        put.wait()
        ag_m.start((modp_ref,), (mall_ref,), scr_m)
        ag_m.relay((modp_ref,), (mall_ref,), scr_m)
        ag_m.finish((modp_ref,), (mall_ref,), scr_m)
        ag_w.relay((w_ref,), (gw_ref,), scr_w)
        ag_w.finish((w_ref,), (gw_ref,), scr_w)

    rd = jnp.broadcast_to(ret_decay.reshape(2, HEADS).T[:, :, None], (HEADS, 2, LANES))
    lane = _full((HEADS, 2, LANES))
    outs = _pc(
        body, name="head",
        in_specs=[ANY, ANY, ANY, ANY, _full((1, ada_n)), _full((1, D_MODEL)), lane],
        out_specs=[ANY, ANY, ANY, _full((16, D_MODEL)), ANY, ANY, lane, lane],
        out_shape=ag_w.out_shapes + ag_c.out_shapes + [_sds((16, D_MODEL)), mod_sds] + ag_m.out_shapes
        + [_sds((HEADS, 2, LANES))] * 2,
        scratch_shapes=ag_w.scratch + ag_c.scratch + ag_m.scratch
        + [pltpu.VMEM((N_DEV,) + c8.shape, F32), pltpu.VMEM(w_ada.shape, F32), pltpu.VMEM((16, ada_n), F32),
           pltpu.SemaphoreType.DMA((3,))],
        compiler_params=_params(),
    )(w_half, c8, small, w_ada, b_shard, c_ctx.reshape(1, D_MODEL), rd)
    gw, c_all, small_all, a16, _, mod_all, lgv, sgv = outs
    return gw, c_all, small_all, a16, mod_all, lgv, sgv


def _ada_grad(at, b):
    n = b.shape[1]
    bn = 512

    def body(a_ref, b_ref, o_ref):
        o_ref[...] = jnp.dot(a_ref[...], b_ref[...], preferred_element_type=F32, precision=lax.Precision.HIGHEST)

    return _pc(body, name="ada_grad", grid=(n // bn,),
               in_specs=[_full((D_MODEL, LANES)), pl.BlockSpec((LANES, bn), lambda i: (0, i))],
               out_specs=pl.BlockSpec((D_MODEL, bn), lambda i: (0, i)), out_shape=_sds((D_MODEL, n)),
               compiler_params=_params("arbitrary"))(at, b)


def _cctx_partial(dmc8, w_ada):
    n = w_ada.shape[1]
    bn = 512

    def body(d_ref, w_ref, o_ref):
        @pl.when(pl.program_id(0) == 0)
        def _():
            o_ref[...] = jnp.zeros_like(o_ref)
        o_ref[...] += lax.dot_general(d_ref[...], w_ref[...], (((1,), (1,)), ((), ())),
                                      preferred_element_type=F32, precision=lax.Precision.HIGHEST)

    return _pc(body, name="cctx_partial", grid=(n // bn,),
               in_specs=[pl.BlockSpec((8, bn), lambda i: (0, i)), pl.BlockSpec((D_MODEL, bn), lambda i: (0, i))],
               out_specs=_full((8, D_MODEL)), out_shape=_sds((8, D_MODEL)),
               compiler_params=_params("arbitrary"))(dmc8, w_ada)


def _cctx_final(parts, c_ctx, m, v):
    c1 = 1.0 - ADAM_B1 ** ADAM_STEP
    c2 = 1.0 - ADAM_B2 ** ADAM_STEP

    def body(p_ref, c_ref, m_ref, v_ref, g_ref, d_ref, mo_ref, vo_ref):
        s = ((p_ref[0, 0:1, :] + p_ref[2, 0:1, :]) + p_ref[4, 0:1, :]) + p_ref[6, 0:1, :]
        z = c_ref[...]
        sg = _sigmoid(z)
        gg = s * (sg * (1.0 + z * (1.0 - sg)))
        g_ref[...] = gg
        mn = ADAM_B1 * m_ref[...] + (1.0 - ADAM_B1) * gg
        vn = ADAM_B2 * v_ref[...] + (1.0 - ADAM_B2) * (gg * gg)
        d_ref[...] = -ADAM_LR * ((mn / c1) / (jnp.sqrt(vn / c2) + ADAM_EPS) + ADAM_WD * z)
        mo_ref[...] = mn
        vo_ref[...] = vn

    row = _full((1, D_MODEL))
    return _pc(body, name="cctx_final", out_shape=[_sds((1, D_MODEL))] * 4,
               in_specs=[_full(parts.shape), row, row, row], out_specs=[row] * 4,
               compiler_params=_params())(parts, c_ctx.reshape(1, D_MODEL), m.reshape(1, D_MODEL), v.reshape(1, D_MODEL))


def _rotary_tables(t_len):
    rows = t_len // GRID_W
    n_freq = DH // 4
    inv = ROPE_BASE ** (-jnp.arange(n_freq, dtype=F32) / n_freq)
    row_ang = jnp.arange(rows, dtype=F32)[:, None] * inv
    col_ang = jnp.arange(GRID_W, dtype=F32)[:, None] * inv

    def spread(fn):
        return jnp.concatenate([jnp.repeat(fn(row_ang), GRID_W, axis=0), jnp.tile(fn(col_ang), (rows, 1))], axis=-1)

    cos, sin = spread(jnp.cos), spread(jnp.sin)
    return jnp.concatenate([cos, cos], axis=-1), jnp.concatenate([-sin, sin], axis=-1)


def _inproj_fwd(x, gn, sh, sc, w4, cos2, sin2, name, comms=()):
    t = x.shape[0]
    tm = _tile(t, True)
    nc = IN_COLS // N_CHIP

    def body(x_ref, gn_ref, sh_ref, sc_ref, w_ref, c_ref, s_ref, p_ref, xr_ref, hb_ref, p_s):
        xh, _ = _rms(x_ref[...])
        h = xh * gn_ref[...] * (1.0 + sc_ref[...]) + sh_ref[...]
        hb = h.astype(BF16)
        hb_ref[...] = hb
        for j in range(N_CHIP):
            p_s[:, nc * j:nc * (j + 1)] = _dot(hb, w_ref[j])
        cc = c_ref[...]
        ss = s_ref[...]
        for hh in range(2 * HEADS):
            blk = p_s[:, DH * hh:DH * (hh + 1)]
            rot = blk * cc + pltpu.roll(blk, DH // 2, 1) * ss
            if hh >= HEADS:
                rot = rot * K_SCALE
            p_ref[:, DH * hh:DH * (hh + 1)] = rot.astype(BF16)
        p_ref[:, 2 * RET_W:] = p_s[:, 2 * RET_W:].astype(BF16)
        xr_ref[...] = p_s[:, 4 * RET_W:4 * RET_W + LRU_W]

    row = _full((1, D_MODEL))
    outs, couts = _call(
        body, name=name, grid=(t // tm,),
        in_specs=[pl.BlockSpec((tm, D_MODEL), lambda i: (i, 0)), row, row, row, _full(w4.shape),
                  pl.BlockSpec((tm, DH), lambda i: (i, 0)), pl.BlockSpec((tm, DH), lambda i: (i, 0))],
        out_specs=[pl.BlockSpec((tm, IN_COLS), lambda i: (i, 0)), pl.BlockSpec((tm, LRU_W), lambda i: (i, 0)),
                   pl.BlockSpec((tm, D_MODEL), lambda i: (i, 0))],
        out_shape=[_sds((t, IN_COLS), BF16), _sds((t, LRU_W)), _sds((t, D_MODEL), BF16)],
        scratch_shapes=[pltpu.VMEM((tm, IN_COLS), F32)], sem=("arbitrary",),
        args=(x, gn, sh, sc, w4, cos2, sin2), comms=comms)
    return (outs, couts) if comms else outs


def _inproj_bwd(x, gn, sh, sc, w4, cos2, sin2, pieces, dres, name):
    t = x.shape[0]
    tm = _tile(t)
    nc = IN_COLS // N_CHIP

    def body(x_ref, gn_ref, sh_ref, sc_ref, w_ref, c_ref, s_ref, dqf, dqb, dkf, dkb, dvf, dvb, dg, dxr, dgt, dres_ref,
             dx_ref, dpb_ref, dgn_ref, dsh_ref, dsc_ref):
        cc = c_ref[...]
        ss = s_ref[...]
        dq = dqf[...].astype(F32) + dqb[...].astype(F32)
        dk = dkf[...].astype(F32) + dkb[...].astype(F32)
        for hh in range(HEADS):
            sl = slice(DH * hh, DH * (hh + 1))
            b = dq[:, sl]
            dpb_ref[:, sl] = (b * cc + pltpu.roll(b * ss, DH // 2, 1)).astype(BF16)
            b = dk[:, sl]
            dpb_ref[:, RET_W + DH * hh:RET_W + DH * (hh + 1)] = (
                (b * cc + pltpu.roll(b * ss, DH // 2, 1)) * K_SCALE).astype(BF16)
        dpb_ref[:, 2 * RET_W:3 * RET_W] = (dvf[...].astype(F32) + dvb[...].astype(F32)).astype(BF16)
        dpb_ref[:, 3 * RET_W:4 * RET_W] = dg[...].astype(BF16)
        dpb_ref[:, 4 * RET_W:4 * RET_W + LRU_W] = dxr[...].astype(BF16)
        dpb_ref[:, 4 * RET_W + LRU_W:IN_COLS] = dgt[...].astype(BF16)
        dh = _dot_nt(dpb_ref[:, 0:nc], w_ref[0])
        for j in range(1, N_CHIP):
            dh = dh + _dot_nt(dpb_ref[:, nc * j:nc * (j + 1)], w_ref[j])
        dx, dgn_t, dsh_t, dsc_t = _norm_mod_bwd(x_ref[...], gn_ref[...], sc_ref[...], dh)
        dx_ref[...] = dres_ref[...] + dx

        @pl.when(pl.program_id(0) == 0)
        def _():
            dgn_ref[...] = jnp.zeros_like(dgn_ref)
            dsh_ref[...] = jnp.zeros_like(dsh_ref)
            dsc_ref[...] = jnp.zeros_like(dsc_ref)
        dgn_ref[...] += dgn_t
        dsh_ref[...] += dsh_t
        dsc_ref[...] += dsc_t

    row = _full((1, D_MODEL))
    pc = pl.BlockSpec((tm, RET_W), lambda i: (i, 0))
    big = pl.BlockSpec((tm, D_MODEL), lambda i: (i, 0))
    return _pc(body, name=name, grid=(t // tm,),
               in_specs=[big, row, row, row, _full(w4.shape),
                         pl.BlockSpec((tm, DH), lambda i: (i, 0)), pl.BlockSpec((tm, DH), lambda i: (i, 0))]
               + [pc] * 9 + [big],
               out_specs=[big, pl.BlockSpec((tm, IN_COLS), lambda i: (i, 0)), row, row, row],
               out_shape=[_sds((t, D_MODEL)), _sds((t, IN_COLS), BF16), _sds((1, D_MODEL)), _sds((1, D_MODEL)),
                          _sds((1, D_MODEL))],
               compiler_params=_params("arbitrary"))(x, gn, sh, sc, w4, cos2, sin2, *pieces, dres)


def _halo_specs(t, tm):
    n8 = tm // SUBLANES
    last8 = t // SUBLANES - 1
    prev = pl.BlockSpec((SUBLANES, LRU_W), lambda i: (jnp.maximum(i * n8 - 1, 0), 0))
    main = pl.BlockSpec((tm, LRU_W), lambda i: (i, 0))
    nxt = pl.BlockSpec((SUBLANES, LRU_W), lambda i: (jnp.minimum((i + 1) * n8, last8), 0))
    return prev, main, nxt


def _with_halo(prev_ref, main_ref, next_ref, i, nt):
    prev = jnp.where(i > 0, prev_ref[...], 0.0)
    nxt = jnp.where(i < nt - 1, next_ref[...], 0.0)
    return jnp.concatenate([prev, main_ref[...], nxt], axis=0)


def _conv_fwd(xr, cw, cb, name):
    t = xr.shape[0]
    tm = _tile(t, True)
    nt = t // tm
    n = tm + 2 * SUBLANES
    mid = slice(SUBLANES, SUBLANES + tm)

    def body(p_ref, m_ref, n_ref, w_ref, b_ref, o_ref):
        xp = _with_halo(p_ref, m_ref, n_ref, pl.program_id(0), nt)
        acc = b_ref[...] + pltpu.roll(xp, 1, 0)[mid] * w_ref[0:1, :]
        acc = acc + xp[mid] * w_ref[1:2, :]
        acc = acc + pltpu.roll(xp, n - 1, 0)[mid] * w_ref[2:3, :]
        acc = acc + pltpu.roll(xp, n - 2, 0)[mid] * w_ref[3:4, :]
        o_ref[...] = acc

    return _pc(body, name=name, grid=(nt,),
               in_specs=[*_halo_specs(t, tm), _full((4, LRU_W)), _full((1, LRU_W))],
               out_specs=pl.BlockSpec((tm, LRU_W), lambda i: (i, 0)), out_shape=_sds((t, LRU_W)),
               compiler_params=_params("arbitrary"))(xr, xr, xr, cw, cb)


def _conv_bwd(dxc_a, dxc_b, xr, cw, name):
    t = xr.shape[0]
    tm = _tile(t, True)
    nt = t // tm
    n = tm + 2 * SUBLANES
    mid = slice(SUBLANES, SUBLANES + tm)

    def body(ap_ref, am_ref, an_ref, bp_ref, bm_ref, bn_ref, xp_ref, xm_ref, xn_ref, w_ref, dx_ref, dw_ref, db_ref):
        i = pl.program_id(0)
        dp = _with_halo(ap_ref, am_ref, an_ref, i, nt) + _with_halo(bp_ref, bm_ref, bn_ref, i, nt)
        xp = _with_halo(xp_ref, xm_ref, xn_ref, i, nt)
        dx = pltpu.roll(dp, n - 1, 0)[mid] * w_ref[0:1, :]
        dx = dx + dp[mid] * w_ref[1:2, :]
        dx = dx + pltpu.roll(dp, 1, 0)[mid] * w_ref[2:3, :]
        dx = dx + pltpu.roll(dp, 2, 0)[mid] * w_ref[3:4, :]
        dx_ref[...] = dx.astype(BF16)
        d = dp[mid]

        @pl.when(i == 0)
        def _():
            dw_ref[...] = jnp.zeros_like(dw_ref)
            db_ref[...] = jnp.zeros_like(db_ref)
        dw_ref[0:1, :] += _sum0(d * pltpu.roll(xp, 1, 0)[mid])
        dw_ref[1:2, :] += _sum0(d * xp[mid])
        dw_ref[2:3, :] += _sum0(d * pltpu.roll(xp, n - 1, 0)[mid])
        dw_ref[3:4, :] += _sum0(d * pltpu.roll(xp, n - 2, 0)[mid])
        db_ref[...] += _sum0(d)

    return _pc(body, name=name, grid=(nt,),
               in_specs=[*_halo_specs(t, tm), *_halo_specs(t, tm), *_halo_specs(t, tm), _full((4, LRU_W))],
               out_specs=[pl.BlockSpec((tm, LRU_W), lambda i: (i, 0)), _full((4, LRU_W)), _full((1, LRU_W))],
               out_shape=[_sds((t, LRU_W), BF16), _sds((4, LRU_W)), _sds((1, LRU_W))],
               compiler_params=_params("arbitrary"))(dxc_a, dxc_a, dxc_a, dxc_b, dxc_b, dxc_b, xr, xr, xr, cw)


def _local_scan(a, b, reverse):
    n = a.shape[0]
    row = lax.broadcasted_iota(jnp.int32, a.shape, 0) & (SUBLANES - 1)
    for s in (1, 2, 4):
        if reverse:
            a_s, b_s, ok = pltpu.roll(a, n - s, 0), pltpu.roll(b, n - s, 0), row < SUBLANES - s
        else:
            a_s, b_s, ok = pltpu.roll(a, s, 0), pltpu.roll(b, s, 0), row >= s
        b = a * jnp.where(ok, b_s, 0.0) + b
        a = a * jnp.where(ok, a_s, 1.0)
    return a, b


def _carry_scan(a_s, b_s, out_ref, carry, reverse):
    ng = a_s.shape[0] // SUBLANES
    shape = carry.shape

    def step(g, cr):
        gg = (ng - 1 - g) if reverse else g
        off = pl.multiple_of(gg * SUBLANES, SUBLANES)
        h = a_s[pl.ds(off, SUBLANES), :] * cr + b_s[pl.ds(off, SUBLANES), :]
        out_ref[pl.ds(off, SUBLANES), :] = h
        edge = h[0:1, :] if reverse else h[SUBLANES - 1:SUBLANES, :]
        return jnp.broadcast_to(edge, shape)

    return lax.fori_loop(0, ng, step, carry)


def _lru_gates(xc, wa_ref, wx_ref, ba, bx, lam):
    xb = xc.astype(BF16)
    r = _sigmoid(_dot(xb, wa_ref[...]) + ba)
    ig = _sigmoid(_dot(xb, wx_ref[...]) + bx)
    sp = _softplus(-lam)
    la = -LRU_C * r * sp
    a = jnp.exp(la)
    mult = jnp.sqrt(_one_minus_sq(la, a))
    return r, ig, sp, a, mult


def _lru_fwd(xc, par_f, par_b, h0_f, h0_b, name, comms=()):
    t = xc.shape[0]
    tm = _tile(t, True)
    nt = t // tm

    def one(x_ref, prm, h0_ref, h_ref, a_s, b_s, c_s, reverse):
        wa_ref, wx_ref, ba_ref, bx_ref, lam_ref = prm

        @pl.when(pl.program_id(0) == 0)
        def _():
            c_s[...] = jnp.broadcast_to(h0_ref[...], c_s.shape)
        xv = x_ref[...]
        _, ig, _, a, mult = _lru_gates(xv, wa_ref, wx_ref, ba_ref[...], bx_ref[...], lam_ref[...])
        al, bl = _local_scan(a, mult * (ig * xv), reverse)
        a_s[...] = al
        b_s[...] = bl
        c_s[...] = _carry_scan(a_s, b_s, h_ref, c_s[...], reverse)

    def body(xf_ref, xb_ref, *refs):
        prm_f, prm_b = refs[0:5], refs[5:10]
        h0f_ref, h0b_ref, hf_ref, hb_ref = refs[10:14]
        af_s, bf_s, cf_s, ab_s, bb_s, cb_s = refs[14:]
        one(xf_ref, prm_f, h0f_ref, hf_ref, af_s, bf_s, cf_s, False)
        one(xb_ref, prm_b, h0b_ref, hb_ref, ab_s, bb_s, cb_s, True)

    vec = _full((1, LRU_W))
    mat = _full((LRU_W, LRU_W))
    fw = pl.BlockSpec((tm, LRU_W), lambda i: (i, 0))
    bw = pl.BlockSpec((tm, LRU_W), lambda i: (nt - 1 - i, 0))
    tile_s = [pltpu.VMEM((tm, LRU_W), F32), pltpu.VMEM((tm, LRU_W), F32), pltpu.VMEM((SUBLANES, LRU_W), F32)]
    (hf, hb), couts = _call(
        body, name=name, grid=(nt,),
        in_specs=[fw, bw] + [mat, mat, vec, vec, vec] * 2 + [vec, vec],
        out_specs=[pl.BlockSpec((tm, LRU_W), lambda i: (i, 0)), pl.BlockSpec((tm, LRU_W), lambda i: (nt - 1 - i, 0))],
        out_shape=[_sds((t, LRU_W))] * 2, scratch_shapes=tile_s + tile_s, sem=("arbitrary",),
        args=(xc, xc, *par_f, *par_b, h0_f, h0_b), comms=comms)
    return ((hf, hb), couts) if comms else (hf, hb)


def _lru_bwd(xc, par_f, par_b, h_f, h_b, h0_f, h0_b, dh_f, dh_b, name, comms=()):
    t = xc.shape[0]
    tm = _tile(t, True)
    nt = t // tm
    n8 = tm // SUBLANES
    last8 = t // SUBLANES - 1
    tile_f = lambda w: pl.BlockSpec((tm, w), lambda i: (nt - 1 - i, 0))
    tile_b = lambda w: pl.BlockSpec((tm, w), lambda i: (i, 0))
    halo_f = pl.BlockSpec((SUBLANES, LRU_W), lambda i: (jnp.maximum((nt - 1 - i) * n8 - 1, 0), 0))
    halo_b = pl.BlockSpec((SUBLANES, LRU_W), lambda i: (jnp.minimum((i + 1) * n8, last8), 0))

    def one(refs_in, refs_out, refs_scr, reverse):
        x_ref, wa_ref, wx_ref, ba_ref, bx_ref, lam_ref, h_ref, halo_ref, h0_ref, dh_ref = refs_in
        dx_ref, dpre_ref, dba_ref, dbx_ref, dlam_ref, dh0_ref = refs_out
        a_s, b_s, l_s, c_s, e_s = refs_scr
        i = pl.program_id(0)

        @pl.when(i == 0)
        def _():
            c_s[...] = jnp.zeros_like(c_s)
            e_s[...] = jnp.zeros_like(e_s)
            dba_ref[...] = jnp.zeros_like(dba_ref)
            dbx_ref[...] = jnp.zeros_like(dbx_ref)
            dlam_ref[...] = jnp.zeros_like(dlam_ref)
        xv = x_ref[...]
        lam = lam_ref[...]
        r, ig, sp, a, mult = _lru_gates(xv, wa_ref, wx_ref, ba_ref[...], bx_ref[...], lam)
        hv = h_ref[...]
        rowi = lax.broadcasted_iota(jnp.int32, (tm, LRU_W), 0)
        edge_a = jnp.broadcast_to(e_s[0:1, :], (tm, LRU_W))
        h0b = jnp.broadcast_to(h0_ref[...], (tm, LRU_W))
        if reverse:
            a_sh = jnp.where(rowi == 0, edge_a, pltpu.roll(a, 1, 0))
            hin_edge = jnp.where(i == nt - 1, h0b, jnp.broadcast_to(halo_ref[0:1, :], (tm, LRU_W)))
            h_in = jnp.where(rowi == tm - 1, hin_edge, pltpu.roll(hv, tm - 1, 0))
        else:
            a_sh = jnp.where(rowi == tm - 1, edge_a, pltpu.roll(a, tm - 1, 0))
            hin_edge = jnp.where(i == nt - 1, h0b, jnp.broadcast_to(halo_ref[SUBLANES - 1:SUBLANES, :], (tm, LRU_W)))
            h_in = jnp.where(rowi == 0, hin_edge, pltpu.roll(hv, 1, 0))
        al, bl = _local_scan(a_sh, dh_ref[...], not reverse)
        a_s[...] = al
        b_s[...] = bl
        c_s[...] = _carry_scan(a_s, b_s, l_s, c_s[...], not reverse)
        e_s[...] = jnp.broadcast_to(a[tm - 1:tm, :] if reverse else a[0:1, :], e_s.shape)
        lmb = l_s[...]
        da = lmb * h_in
        ixc = ig * xv
        dmult = lmb * ixc
        dixc = lmb * mult
        dla = da * a - dmult * (a * a) / mult
        dpr = dla * (-LRU_C * sp) * r * (1.0 - r)
        dpi = dixc * xv * ig * (1.0 - ig)
        dprb = dpr.astype(BF16)
        dpib = dpi.astype(BF16)
        dpre_ref[:, 0:LRU_W] = dprb
        dpre_ref[:, LRU_W:2 * LRU_W] = dpib
        dx_ref[...] = dixc * ig + _dot_nt(dprb, wa_ref[...]) + _dot_nt(dpib, wx_ref[...])
        dba_ref[...] += _sum0(dpr)
        dbx_ref[...] += _sum0(dpi)
        dlam_ref[...] += _sum0(dla * (-LRU_C * r)) * (-_sigmoid(-lam))

        @pl.when(i == nt - 1)
        def _():
            al0 = a * lmb
            dh0_ref[...] = al0[tm - 1:tm, :] if reverse else al0[0:1, :]

    def body(*refs):
        one(refs[0:10], refs[20:26], refs[32:37], False)
        one(refs[10:20], refs[26:32], refs[37:42], True)

    vec = _full((1, LRU_W))
    mat = _full((LRU_W, LRU_W))

    def in_specs(tile, halo):
        return [tile(LRU_W), mat, mat, vec, vec, vec, tile(LRU_W), halo, vec, tile(LRU_W)]

    def out_specs(tile):
        return [tile(LRU_W), tile(2 * LRU_W), vec, vec, vec, vec]

    out_one = [_sds((t, LRU_W)), _sds((t, 2 * LRU_W), BF16)] + [_sds((1, LRU_W))] * 4
    scr_one = [pltpu.VMEM((tm, LRU_W), F32)] * 3 + [pltpu.VMEM((SUBLANES, LRU_W), F32)] * 2
    outs, couts = _call(
        body, name=name, grid=(nt,), in_specs=in_specs(tile_f, halo_f) + in_specs(tile_b, halo_b),
        out_specs=out_specs(tile_f) + out_specs(tile_b), out_shape=out_one + out_one,
        scratch_shapes=scr_one + scr_one, sem=("arbitrary",),
        args=(xc, *par_f, h_f, h_f, h0_f, dh_f, xc, *par_b, h_b, h_b, h0_b, dh_b), comms=comms)
    return (tuple(outs[0:6]), tuple(outs[6:12])), couts


def _decay_tables(lg, reverse):
    ci = lax.broadcasted_iota(jnp.int32, (CHUNK, CHUNK), 0).astype(F32)
    mi = lax.broadcasted_iota(jnp.int32, (CHUNK, CHUNK), 1).astype(F32)
    rel = (mi - ci) if reverse else (ci - mi)
    relc = jnp.maximum(rel, 0.0)
    lg_c = jnp.concatenate([lg] * (CHUNK // LANES), axis=1)
    dm = jnp.where(rel >= 0, jnp.exp(lg_c * relc), 0.0)
    cd = lax.broadcasted_iota(jnp.int32, (CHUNK, DH), 0).astype(F32)
    pq, ps = (CHUNK - cd, cd) if reverse else (cd + 1.0, CHUNK - 1.0 - cd)
    return relc, dm, jnp.exp(lg * pq), jnp.exp(lg * ps), jnp.exp(lg * float(CHUNK)), pq, ps


def _ret_fwd(proj, lgv, s0f, s0b, comms=()):
    t = proj.shape[0]
    n = t // CHUNK

    def one(q, k, v, lg, s_s, hh, o_ref, sp_ref, reverse):
        _, dm, wq, ws, g, _, _ = _decay_tables(lg, reverse)
        vb = v.astype(BF16)
        p = _dot_nt(q.astype(BF16), k.astype(BF16)) * dm
        s = s_s[hh]
        sp_ref[hh, 0] = s
        o_ref[:, DH * hh:DH * (hh + 1)] = _dot(p.astype(BF16), vb) + _dot((q * wq).astype(BF16), s.astype(BF16))
        s_s[hh] = g * s + _dot_tn((k * ws).astype(BF16), vb)

    def body(qf, kf, vf, qb, kb, vb, lg_ref, s0f_ref, s0b_ref, of_ref, ob_ref, spf_ref, spb_ref, sf_s, sb_s):
        @pl.when(pl.program_id(0) == 0)
        def _():
            sf_s[...] = s0f_ref[...]
            sb_s[...] = s0b_ref[...]
        for hh in range(HEADS):
            sl = slice(DH * hh, DH * (hh + 1))
            one(qf[:, sl].astype(F32), kf[:, sl].astype(F32), vf[:, sl], lg_ref[hh, 0:1, :], sf_s, hh, of_ref, spf_ref,
                False)
            one(qb[:, sl].astype(F32), kb[:, sl].astype(F32), vb[:, sl], lg_ref[hh, 1:2, :], sb_s, hh, ob_ref, spb_ref,
                True)

    blk = (CHUNK, RET_W)
    fw = [pl.BlockSpec(blk, lambda i, o=o: (i, o)) for o in range(3)]
    bw = [pl.BlockSpec(blk, lambda i, o=o: (n - 1 - i, o)) for o in range(3)]
    st = _full((HEADS, DH, DH))
    return _call(body, name="ret_fwd", grid=(n,),
                 in_specs=fw + bw + [_full((HEADS, 2, LANES)), st, st],
                 out_specs=[pl.BlockSpec(blk, lambda i: (i, 0)), pl.BlockSpec(blk, lambda i: (n - 1 - i, 0)),
                            pl.BlockSpec((HEADS, 1, DH, DH), lambda i: (0, i, 0, 0)),
                            pl.BlockSpec((HEADS, 1, DH, DH), lambda i: (0, n - 1 - i, 0, 0))],
                 out_shape=[_sds((t, RET_W)), _sds((t, RET_W)), _sds((HEADS, n, DH, DH)), _sds((HEADS, n, DH, DH))],
                 scratch_shapes=[pltpu.VMEM((HEADS, DH, DH), F32), pltpu.VMEM((HEADS, DH, DH), F32)],
                 sem=("arbitrary",), args=(proj, proj, proj, proj, proj, proj, lgv, s0f, s0b), comms=comms)


def _ret_bwd(proj, lgv, sgv, spf, spb, do, comms=()):
    t = proj.shape[0]
    n = t // CHUNK

    def one(q_ref, k_ref, v_ref, lg_ref, s_ref, do_ref, dq_ref, dk_ref, dv_ref, ds_s, acc_s, reverse):
        d = 1 if reverse else 0
        for hh in range(HEADS):
            sl = slice(DH * hh, DH * (hh + 1))
            relc, dm, wq, ws, g, pq, ps = _decay_tables(lg_ref[hh, d:d + 1, :], reverse)
            qb, kb, vb = q_ref[:, sl], k_ref[:, sl], v_ref[:, sl]
            q, k = qb.astype(F32), kb.astype(F32)
            p = _dot_nt(qb, kb) * dm
            s = s_ref[hh, 0]
            dob = do_ref[:, sl].astype(BF16)
            dsn = ds_s[hh]
            dsb = dsn.astype(BF16)
            dv_ref[:, sl] = (_dot_tn(p.astype(BF16), dob) + _dot((k * ws).astype(BF16), dsb)).astype(BF16)
            dp = _dot_nt(dob, vb)
            dab = (dp * dm).astype(BF16)
            xq = _dot_nt(dob, s.astype(BF16))
            yk = _dot_nt(vb, dsb)
            dq_ref[:, sl] = (_dot(dab, kb) + xq * wq).astype(BF16)
            dk_ref[:, sl] = (_dot_tn(dab, qb) + yk * ws).astype(BF16)
            ds_s[hh] = g * dsn + _dot_tn((q * wq).astype(BF16), dob)
            s_mask = _sum0(dp * p * relc)
            part = (sum(s_mask[:, LANES * u:LANES * (u + 1)] for u in range(CHUNK // LANES))
                    + _sum0(xq * q * wq * pq) + _sum0(yk * k * ws * ps) + _sum0(dsn * s) * g * float(CHUNK))
            acc_s[hh] += jnp.broadcast_to(part, (SUBLANES, LANES))

    def body(qf, kf, vf, qb, kb, vb, lg_ref, sg_ref, sf_ref, sb_ref, dof_ref, dob_ref,
             dqf, dkf, dvf, dqb, dkb, dvb, ds0f_ref, ds0b_ref, drdf_ref, drdb_ref, dsf_s, dsb_s, accf_s, accb_s):
        i = pl.program_id(0)

        @pl.when(i == 0)
        def _():
            for r in (dsf_s, dsb_s, accf_s, accb_s):
                r[...] = jnp.zeros_like(r)
        one(qf, kf, vf, lg_ref, sf_ref, dof_ref, dqf, dkf, dvf, dsf_s, accf_s, False)
        one(qb, kb, vb, lg_ref, sb_ref, dob_ref, dqb, dkb, dvb, dsb_s, accb_s, True)

        @pl.when(i == n - 1)
        def _():
            ds0f_ref[...] = dsf_s[...]
            ds0b_ref[...] = dsb_s[...]
            for d, (acc_s, drd_ref) in enumerate(((accf_s, drdf_ref), (accb_s, drdb_ref))):
                for hh in range(HEADS):
                    tot = jnp.sum(acc_s[hh, 0:1, :], axis=1, keepdims=True)
                    drd_ref[hh] = jnp.broadcast_to(tot, (SUBLANES, LANES)) * sg_ref[hh, d:d + 1, :]

    blk = (CHUNK, RET_W)
    fw = lambda o: pl.BlockSpec(blk, lambda i, o=o: (n - 1 - i, o))
    bw = lambda o: pl.BlockSpec(blk, lambda i, o=o: (i, o))
    lane = _full((HEADS, 2, LANES))
    st = _full((HEADS, DH, DH))
    rd = _full((HEADS, SUBLANES, LANES))
    outs, couts = _call(
        body, name="ret_bwd", grid=(n,),
        in_specs=[fw(0), fw(1), fw(2), bw(0), bw(1), bw(2), lane, lane,
                  pl.BlockSpec((HEADS, 1, DH, DH), lambda i: (0, n - 1 - i, 0, 0)),
                  pl.BlockSpec((HEADS, 1, DH, DH), lambda i: (0, i, 0, 0)), fw(0), bw(0)],
        out_specs=[fw(0), fw(0), fw(0), bw(0), bw(0), bw(0), st, st, rd, rd],
        out_shape=[_sds((t, RET_W), BF16)] * 6 + [_sds((HEADS, DH, DH))] * 2 + [_sds((HEADS, SUBLANES, LANES))] * 2,
        scratch_shapes=[pltpu.VMEM((HEADS, DH, DH), F32)] * 2 + [pltpu.VMEM((HEADS, SUBLANES, LANES), F32)] * 2,
        sem=("arbitrary",), args=(proj, proj, proj, proj, proj, proj, lgv, sgv, spf, spb, do, do), comms=comms)
    dqf, dkf, dvf, dqb, dkb, dvb, ds0f, ds0b, drdf, drdb = outs
    return ((dqf, dkf, dvf, ds0f, drdf), (dqb, dkb, dvb, ds0b, drdb)), couts


def _ctx_weights(lg, l_len, reverse):
    pos = lax.broadcasted_iota(jnp.int32, (l_len, DH), 0).astype(F32)
    steps = pos if reverse else (l_len - 1.0 - pos)
    return jnp.exp(lg * steps), steps


def _ctx_state_fwd(projc, lgv):
    l_len = projc.shape[0]

    def body(k_ref, v_ref, lg_ref, sf_ref, sb_ref):
        k = k_ref[...]
        vb = v_ref[...].astype(BF16)
        for d, o_ref in ((0, sf_ref), (1, sb_ref)):
            w, _ = _ctx_weights(lg_ref[0, d:d + 1, :], l_len, d == 1)
            o_ref[0] = _dot_tn((k * w).astype(BF16), vb)

    st = pl.BlockSpec((1, DH, DH), lambda h: (h, 0, 0))
    return _pc(body, name="ctx_state_fwd", grid=(HEADS,),
               in_specs=[pl.BlockSpec((l_len, DH), lambda h: (0, HEADS + h)),
                         pl.BlockSpec((l_len, DH), lambda h: (0, 2 * HEADS + h)),
                         pl.BlockSpec((1, 2, LANES), lambda h: (h, 0, 0))],
               out_specs=[st, st], out_shape=[_sds((HEADS, DH, DH))] * 2,
               compiler_params=_params("arbitrary"))(projc, projc, lgv)


def _ctx_state_bwd(projc, lgv, sgv, dsf, dsb):
    l_len = projc.shape[0]

    def body(k_ref, v_ref, lg_ref, sg_ref, dsf_ref, dsb_ref, dk_ref, dv_ref, drd_ref):
        k = k_ref[...]
        vb = v_ref[...].astype(BF16)
        dk = jnp.zeros((l_len, DH), F32)
        dv = jnp.zeros((l_len, DH), F32)
        rows = []
        for d, ds_ref in ((0, dsf_ref), (1, dsb_ref)):
            w, steps = _ctx_weights(lg_ref[0, d:d + 1, :], l_len, d == 1)
            dsb16 = ds_ref[0].astype(BF16)
            dkw = _dot_nt(vb, dsb16)
            dk = dk + dkw * w
            dv = dv + _dot((k * w).astype(BF16), dsb16)
            tot = jnp.sum(_sum0(dkw * k * w * steps), axis=1, keepdims=True)
            rows.append(jnp.broadcast_to(tot, (1, LANES)) * sg_ref[0, d:d + 1, :])
        dk_ref[...] = dk.astype(BF16)
        dv_ref[...] = dv.astype(BF16)
        rid = lax.broadcasted_iota(jnp.int32, (SUBLANES, LANES), 0)
        drd_ref[0] = jnp.where(rid == 0, rows[0], jnp.where(rid == 1, rows[1], 0.0))

    st = pl.BlockSpec((1, DH, DH), lambda h: (h, 0, 0))
    lane = pl.BlockSpec((1, 2, LANES), lambda h: (h, 0, 0))
    hc = pl.BlockSpec((l_len, DH), lambda h: (0, h))
    return _pc(body, name="ctx_state_bwd", grid=(HEADS,),
               in_specs=[pl.BlockSpec((l_len, DH), lambda h: (0, HEADS + h)),
                         pl.BlockSpec((l_len, DH), lambda h: (0, 2 * HEADS + h)), lane, lane, st, st],
               out_specs=[hc, hc, pl.BlockSpec((1, SUBLANES, LANES), lambda h: (h, 0, 0))],
               out_shape=[_sds((l_len, RET_W), BF16), _sds((l_len, RET_W), BF16), _sds((HEADS, SUBLANES, LANES))],
               compiler_params=_params("arbitrary"))(projc, projc, lgv, sgv, dsf, dsb)


G_BLOCK = (3 * RET_W) // RET_W
GATE_BLOCK = (4 * RET_W + LRU_W) // LRU_W


def _head_norm(y):
    yc = y - jnp.mean(y, axis=-1, keepdims=True)
    rs = lax.rsqrt(jnp.mean(yc * yc, axis=-1, keepdims=True) + EPS)
    return yc * rs, rs


def _gelu_parts(z):
    th = jnp.tanh(GELU_K * (z + GELU_C * z * z * z))
    return 0.5 * z * (1.0 + th), th


def _mix_fwd(o_f, o_b, proj, hf, hb, w_out, x, g1):
    t = x.shape[0]
    tm = _tile(t, True)

    def body(of_ref, ob_ref, g_ref, gt_ref, hf_ref, hb_ref, w_ref, x_ref, g1_ref, x1_ref, cat_ref):
        o = of_ref[...] + ob_ref[...]
        g = g_ref[...].astype(F32)
        for hh in range(HEADS):
            sl = slice(DH * hh, DH * (hh + 1))
            nrm, _ = _head_norm(o[:, sl])
            gh = g[:, sl]
            cat_ref[:, sl] = (gh * _sigmoid(gh) * nrm).astype(BF16)
        gel, _ = _gelu_parts(gt_ref[...].astype(F32))
        cat_ref[:, RET_W:] = ((hf_ref[...] + hb_ref[...]) * gel).astype(BF16)
        x1_ref[...] = x_ref[...] + g1_ref[...] * _dot(cat_ref[...], w_ref[...])

    half = pl.BlockSpec((tm, RET_W), lambda i: (i, 0))
    big = pl.BlockSpec((tm, D_MODEL), lambda i: (i, 0))
    return _pc(body, name="mix_fwd", grid=(t // tm,),
               in_specs=[half, half, pl.BlockSpec((tm, RET_W), lambda i: (i, G_BLOCK)),
                         pl.BlockSpec((tm, LRU_W), lambda i: (i, GATE_BLOCK)), half, half,
                         _full((D_MODEL, D_MODEL)), big, _full((1, D_MODEL))],
               out_specs=[big, big], out_shape=[_sds((t, D_MODEL)), _sds((t, D_MODEL), BF16)],
               compiler_params=_params("arbitrary"))(o_f, o_b, proj, proj, hf, hb, w_out, x, g1)


def _mix_bwd(o_f, o_b, proj, hf, hb, w_out, cat, dx1, g1, comms=()):
    t = dx1.shape[0]
    tm = _tile(t, True)

    def body(of_ref, ob_ref, g_ref, gt_ref, hf_ref, hb_ref, w_ref, cat_ref, dx1_ref, g1_ref,
             do_ref, dhs_ref, dg_ref, dgt_ref, dyb_ref, dg1_ref):
        dx1v = dx1_ref[...]
        y = _dot(cat_ref[...], w_ref[...])

        @pl.when(pl.program_id(0) == 0)
        def _():
            dg1_ref[...] = jnp.zeros_like(dg1_ref)
        dg1_ref[...] += _sum0(dx1v * y)
        dyb = (g1_ref[...] * dx1v).astype(BF16)
        dyb_ref[...] = dyb
        dcat = _dot_nt(dyb, w_ref[...])
        o = of_ref[...] + ob_ref[...]
        g = g_ref[...].astype(F32)
        for hh in range(HEADS):
            sl = slice(DH * hh, DH * (hh + 1))
            nrm, rs = _head_norm(o[:, sl])
            gh = g[:, sl]
            sg = _sigmoid(gh)
            dret = dcat[:, sl]
            dg_ref[:, sl] = (dret * nrm * (sg * (1.0 + gh * (1.0 - sg)))).astype(BF16)
            dn = dret * (gh * sg)
            dyc = rs * (dn - nrm * jnp.mean(dn * nrm, axis=-1, keepdims=True))
            do_ref[:, sl] = (dyc - jnp.mean(dyc, axis=-1, keepdims=True)).astype(BF16)
        z = gt_ref[...].astype(F32)
        gel, th = _gelu_parts(z)
        dlru = dcat[:, RET_W:]
        dhs_ref[...] = dlru * gel
        dgel = 0.5 * (1.0 + th) + 0.5 * z * (1.0 - th * th) * GELU_K * (1.0 + 3.0 * GELU_C * z * z)
        dgt_ref[...] = (dlru * (hf_ref[...] + hb_ref[...]) * dgel).astype(BF16)

    half = pl.BlockSpec((tm, RET_W), lambda i: (i, 0))
    big = pl.BlockSpec((tm, D_MODEL), lambda i: (i, 0))
    return _call(body, name="mix_bwd", grid=(t // tm,),
                 in_specs=[half, half, pl.BlockSpec((tm, RET_W), lambda i: (i, G_BLOCK)),
                           pl.BlockSpec((tm, LRU_W), lambda i: (i, GATE_BLOCK)), half, half,
                           _full((D_MODEL, D_MODEL)), big, big, _full((1, D_MODEL))],
                 out_specs=[half, half, half, half, big, _full((1, D_MODEL))],
                 out_shape=[_sds((t, RET_W), BF16), _sds((t, RET_W)), _sds((t, RET_W), BF16), _sds((t, RET_W), BF16),
                            _sds((t, D_MODEL), BF16), _sds((1, D_MODEL))],
                 scratch_shapes=[], sem=("arbitrary",), args=(o_f, o_b, proj, proj, hf, hb, w_out, cat, dx1, g1),
                 comms=comms)


def _mlp(x1, n2g, sh2, sc2, g2, fg, w1_parts, w2_parts, tgt):
    t = x1.shape[0]
    tm = _tile(t)
    hb_ = MLP_H // N_CHIP
    q_rows = hb_ // 4
    n_cp = 4 * N_DEV

    def body(x1_ref, n2g_ref, sh2_ref, sc2_ref, g2_ref, fg_ref, w1a, w1b, w2a, w2b, tgt_ref,
             dx1_ref, h2b_ref, ab_ref, dub_ref, dmb_ref, dsc_ref, dsh_ref, dg2_ref, dn2_ref, dfg_ref, loss_ref,
             w1_s, w2_s, r_s, sems):
        @pl.when(pl.program_id(0) == 0)
        def _():
            cps = []
            for p, parts in enumerate(((w1a, w2a), (w1b, w2b))):
                for d in range(N_DEV):
                    rows = pl.ds(2 * q_rows * (d % 2) + q_rows * p, q_rows)
                    for src, dst in zip(parts, (w1_s, w2_s)):
                        cps.append(pltpu.make_async_copy(src.at[d], dst.at[d // 2, rows], sems.at[len(cps)]))
            for cp in cps:
                cp.start()
            for r in (dsc_ref, dsh_ref, dg2_ref, dn2_ref, dfg_ref, loss_ref):
                r[...] = jnp.zeros_like(r)
            for cp in cps:
                cp.wait()
        x1v = x1_ref[...]
        n2g, sc2, g2, fg = n2g_ref[...], sc2_ref[...], g2_ref[...], fg_ref[...]
        xh, _ = _rms(x1v)
        h2b = (xh * n2g * (1.0 + sc2) + sh2_ref[...]).astype(BF16)
        h2b_ref[...] = h2b
        m = jnp.zeros((tm, D_MODEL), F32)
        for j in range(N_CHIP):
            sl = slice(hb_ * j, hb_ * (j + 1))
            r = jnp.maximum(_dot(h2b, w1_s[j]), 0.0)
            r_s[:, sl] = r
            ab = (r * r).astype(BF16)
            ab_ref[:, sl] = ab
            m = m + _dot(ab, w2_s[j])
        x2 = x1v + g2 * m
        x2h, r2 = _rms(x2)
        err = x2h * fg - tgt_ref[...]
        loss_ref[...] += _sum0(err * err)
        dout = err * (1.0 / D_MODEL)
        dfg_ref[...] += _sum0(dout * x2h)
        dxh = dout * fg
        dx2 = r2 * (dxh - x2h * jnp.mean(dxh * x2h, axis=-1, keepdims=True))
        dg2_ref[...] += _sum0(dx2 * m)
        dmb = (g2 * dx2).astype(BF16)
        dmb_ref[...] = dmb
        dh2 = jnp.zeros((tm, D_MODEL), F32)
        for j in range(N_CHIP):
            sl = slice(hb_ * j, hb_ * (j + 1))
            dub = (_dot_nt(dmb, w2_s[j]) * (2.0 * r_s[:, sl])).astype(BF16)
            dub_ref[:, sl] = dub
            dh2 = dh2 + _dot_nt(dub, w1_s[j])
        dx, dn2_t, dsh_t, dsc_t = _norm_mod_bwd(x1v, n2g, sc2, dh2)
        dx1_ref[...] = dx2 + dx
        dn2_ref[...] += dn2_t
        dsh_ref[...] += dsh_t
        dsc_ref[...] += dsc_t

        @pl.when(pl.program_id(0) == t // tm - 1)
        def _():
            tot = jnp.sum(loss_ref[...], axis=1, keepdims=True) * (0.5 / D_MODEL)
            loss_ref[...] = jnp.broadcast_to(tot, loss_ref.shape)

    row = _full((1, D_MODEL))
    big = pl.BlockSpec((tm, D_MODEL), lambda i: (i, 0))
    wide = pl.BlockSpec((tm, MLP_H), lambda i: (i, 0))
    return _pc(body, name="mlp", grid=(t // tm,),
               in_specs=[big, row, row, row, row, row, ANY, ANY, ANY, ANY, big],
               out_specs=[big, big, wide, wide, big, row, row, row, row, row, row],
               out_shape=[_sds((t, D_MODEL)), _sds((t, D_MODEL), BF16), _sds((t, MLP_H), BF16), _sds((t, MLP_H), BF16),
                          _sds((t, D_MODEL), BF16)] + [_sds((1, D_MODEL))] * 6,
               scratch_shapes=[pltpu.VMEM((N_CHIP, D_MODEL, hb_), BF16), pltpu.VMEM((N_CHIP, hb_, D_MODEL), BF16),
                               pltpu.VMEM((tm, MLP_H), F32), pltpu.SemaphoreType.DMA((n_cp,))],
               compiler_params=_params("arbitrary"))(x1, n2g, sh2, sc2, g2, fg, *w1_parts, *w2_parts, tgt)


def _tn(a, b, nj, a_blocked, b_blocked, name, extra=None, comms=()):
    t = a.shape[0]
    m = a.shape[1] // (nj if a_blocked else 1)
    n = b.shape[1] // (nj if b_blocked else 1)
    bk = next((b for b in (2048, 1024, 512) if t % b == 0), t)
    nk = t // bk
    a_col = (lambda j: j) if a_blocked else (lambda j: 0)
    b_col = (lambda j: j) if b_blocked else (lambda j: 0)
    in_specs = [pl.BlockSpec((bk, m), lambda j, k: (k, a_col(j))), pl.BlockSpec((bk, n), lambda j, k: (k, b_col(j)))]
    args = [a, b]
    if extra is not None:
        a2, b2 = extra
        t2 = a2.shape[0]
        in_specs += [pl.BlockSpec((t2, m), lambda j, k: (0, a_col(j))),
                     pl.BlockSpec((t2, n), lambda j, k: (0, b_col(j)))]
        args += [a2, b2]

    def body(*refs):
        a_ref, b_ref = refs[0], refs[1]
        o_ref, acc = refs[-2], refs[-1]
        k = pl.program_id(1)

        @pl.when(k == 0)
        def _():
            acc[...] = jnp.zeros_like(acc)
        acc[...] += _dot_tn(a_ref[...].astype(BF16), b_ref[...].astype(BF16))

        @pl.when(k == nk - 1)
        def _():
            if extra is not None:
                acc[...] += _dot_tn(refs[2][...].astype(BF16), refs[3][...].astype(BF16))
            o_ref[0] = acc[...]

    (out,), couts = _call(body, name=name, grid=(nj, nk), in_specs=in_specs,
                          out_specs=[pl.BlockSpec((1, m, n), lambda j, k: (j, 0, 0))], out_shape=[_sds((nj, m, n))],
                          scratch_shapes=[pltpu.VMEM((m, n), F32)], sem=("arbitrary", "arbitrary"), args=args,
                          comms=comms)
    return (out, couts) if comms else out


ROW_LOSS = 0
ROW_DMOD = 1
ROW_DMODC = 7
ROW_N1, ROW_N2, ROW_FG, ROW_CB = 9, 10, 11, 12
ROW_BA, ROW_BX, ROW_LAM = 13, 15, 17
ROW_CW = 20
ROW_RD = 24
SLAB_ROWS = 32
SEG = D_MODEL // 2


def _pack_small(rows, drd, cw2, cb2, lru2, gates):
    n_rows, n_lru = len(rows), len(lru2)

    def body(*refs):
        r = refs[:n_rows]
        drd_f, drd_b, drd_c, cw_a, cw_b, cb_a, cb_b = refs[n_rows:n_rows + 7]
        lru = refs[n_rows + 7:n_rows + 7 + n_lru]
        gf_ref, gb_ref, slab, ga, gx = refs[n_rows + 7 + n_lru:]
        slab[...] = jnp.zeros_like(slab)
        slab[ROW_LOSS:ROW_LOSS + 1, :] = r[0][...]
        for k in range(N_MOD):
            slab[ROW_DMOD + k:ROW_DMOD + k + 1, :] = r[1 + k][...]
        slab[ROW_DMODC:ROW_DMODC + 1, :] = r[7][...]
        slab[ROW_DMODC + 1:ROW_DMODC + 2, :] = r[8][...]
        slab[ROW_N1:ROW_N1 + 1, :] = r[9][...] + r[10][...]
        slab[ROW_N2:ROW_N2 + 1, :] = r[11][...]
        slab[ROW_FG:ROW_FG + 1, :] = r[12][...]
        slab[ROW_CB:ROW_CB + 1, 0:LRU_W] = cb_a[...] + cb_b[...]
        for k, row in enumerate((ROW_BA, ROW_BA + 1, ROW_BX, ROW_BX + 1, ROW_LAM, ROW_LAM + 1)):
            slab[row:row + 1, 0:LRU_W] = lru[2 * k][...] + lru[2 * k + 1][...]
        slab[ROW_CW:ROW_CW + 4, 0:LRU_W] = cw_a[...] + cw_b[...]
        for h in range(HEADS):
            slab[ROW_RD + h:ROW_RD + h + 1, 0:LANES] = drd_f[h, 0:1, :] + drd_c[h, 0:1, :]
            slab[ROW_RD + HEADS + h:ROW_RD + HEADS + h + 1, 0:LANES] = drd_b[h, 0:1, :] + drd_c[h, 1:2, :]
        for d, g_ref in enumerate((gf_ref, gb_ref)):
            for n in range(LRU_BLOCKS):
                blk = slice(LRU_BD * n, LRU_BD * (n + 1))
                ga[blk, LRU_BD * d:LRU_BD * (d + 1)] = g_ref[0, blk, blk].astype(BF16)
                gx[blk, LRU_BD * d:LRU_BD * (d + 1)] = g_ref[1, blk, blk].astype(BF16)

    args = list(rows) + list(drd) + list(cw2) + list(cb2) + list(lru2) + list(gates)
    gate_shape = (LRU_W, 2 * LRU_BD)
    return _pc(body, name="pack_small", in_specs=[_full(a.shape) for a in args],
               out_specs=[_full((SLAB_ROWS, D_MODEL)), _full(gate_shape), _full(gate_shape)],
               out_shape=[_sds((SLAB_ROWS, D_MODEL)), _sds(gate_shape, BF16), _sds(gate_shape, BF16)],
               compiler_params=_params())(*args)


def _adam_math(w, g, m, v):
    mn = ADAM_B1 * m + (1.0 - ADAM_B1) * g
    vn = ADAM_B2 * v + (1.0 - ADAM_B2) * (g * g)
    mh = mn / (1.0 - ADAM_B1 ** ADAM_STEP)
    vh = vn / (1.0 - ADAM_B2 ** ADAM_STEP)
    return -ADAM_LR * (mh / (jnp.sqrt(vh) + ADAM_EPS) + ADAM_WD * w), mn, vn


SMALL_PARAMS = ("b_ada", "norm1_g", "norm2_g", "final_g", "ret_decay", "conv_w", "conv_b", "lru_wa", "lru_ba", "lru_wx",
                "lru_bx", "lru_lambda")


def _finalize_small(chip_idx, slab_all, ga_all, gx_all, wmv):
    n_p = len(SMALL_PARAMS)
    flat = [a for nm in SMALL_PARAMS for a in wmv[nm]]
    ada_n = N_MOD * D_MODEL // N_CHIP

    def body(c_ref, slab_ref, ga_ref, gx_ref, *refs):
        prm = {nm: refs[3 * k:3 * k + 3] for k, nm in enumerate(SMALL_PARAMS)}
        outs = {nm: refs[3 * n_p + 4 * k:3 * n_p + 4 * k + 4] for k, nm in enumerate(SMALL_PARAMS)}
        b128_ref, dmc_ref, loss_ref = refs[3 * n_p + 4 * n_p:]
        chip = c_ref[0]

        def pick(fn):
            acc = fn(0)
            for j in range(1, N_CHIP):
                acc = jnp.where(chip == j, fn(j), acc)
            return acc

        tot = slab_ref[0]
        for d in range(1, N_DEV):
            tot = tot + slab_ref[d]

        def update(nm, g, sl=None, rows=None):
            w_ref, m_ref, v_ref = prm[nm]
            g_ref, d_ref, mo_ref, vo_ref = outs[nm]
            ix = (slice(None) if rows is None else rows, slice(None) if sl is None else sl)
            dl, mn, vn = _adam_math(w_ref[ix], g, m_ref[ix], v_ref[ix])
            g_ref[ix] = g
            d_ref[ix] = dl
            mo_ref[ix] = mn
            vo_ref[ix] = vn

        loss_ref[...] = jnp.broadcast_to(tot[ROW_LOSS:ROW_LOSS + 1, 0:LANES], (SUBLANES, LANES))
        for k in range(N_MOD):
            g = tot[ROW_DMOD + k:ROW_DMOD + k + 1, :]
            if k < 2:
                g = g + tot[ROW_DMODC + k:ROW_DMODC + k + 1, :]
            update("b_ada", g, slice(D_MODEL * k, D_MODEL * (k + 1)))
        update("norm1_g", tot[ROW_N1:ROW_N1 + 1, :])
        update("norm2_g", tot[ROW_N2:ROW_N2 + 1, :])
        update("final_g", tot[ROW_FG:ROW_FG + 1, :])
        update("ret_decay", tot[ROW_RD:ROW_RD + SUBLANES, 0:LANES])
        update("conv_b", tot[ROW_CB:ROW_CB + 1, 0:LRU_W])
        update("conv_w", pick(lambda j: tot[ROW_CW:ROW_CW + 4, LANES * j:LANES * (j + 1)]))
        for nm, row in (("lru_ba", ROW_BA), ("lru_bx", ROW_BX), ("lru_lambda", ROW_LAM)):
            update(nm, pick(lambda j, row=row: tot[row:row + 2, LANES * j:LANES * (j + 1)]))
        for nm, g_all in (("lru_wa", ga_ref), ("lru_wx", gx_ref)):
            for dr in range(2):
                lanes = slice(LRU_BD * dr, LRU_BD * (dr + 1))
                g = g_all[0, :, lanes].astype(F32)
                for d in range(1, N_DEV):
                    g = g + g_all[d, :, lanes].astype(F32)
                update(nm, g, rows=slice(LRU_W * dr, LRU_W * (dr + 1)))

        def seg(rows6, s):
            return rows6[s // 2][:, SEG * (s % 2):SEG * (s % 2 + 1)]

        b128_ref[...] = jnp.zeros_like(b128_ref)
        dmc_ref[...] = jnp.zeros_like(dmc_ref)
        zero = jnp.zeros((1, D_MODEL), F32)
        ctx6 = [tot[ROW_DMODC:ROW_DMODC + 1, :], tot[ROW_DMODC + 1:ROW_DMODC + 2, :]] + [zero] * (N_MOD - 2)
        for q in range(ada_n // SEG):
            cols = slice(SEG * q, SEG * (q + 1))
            for d in range(N_DEV):
                rows6 = [slab_ref[d, ROW_DMOD + k:ROW_DMOD + k + 1, :] for k in range(N_MOD)]
                b128_ref[d:d + 1, cols] = pick(lambda j, rows6=rows6: seg(rows6, 3 * j + q))
            c = pick(lambda j: seg(ctx6, 3 * j + q))
            b128_ref[N_DEV:N_DEV + 1, cols] = c
            dmc_ref[0:1, cols] = c

    out_shape = []
    for nm in SMALL_PARAMS:
        out_shape += [_sds(wmv[nm][0].shape)] * 4
    out_shape += [_sds((LANES, ada_n)), _sds((SUBLANES, ada_n)), _sds((SUBLANES, LANES))]
    args = [slab_all, ga_all, gx_all] + flat
    grid_spec = pltpu.PrefetchScalarGridSpec(
        num_scalar_prefetch=1, grid=(1,), in_specs=[_full(a.shape) for a in args],
        out_specs=[_full(s.shape) for s in out_shape])
    outs = _pc(body, name="finalize_small", grid_spec=grid_spec, out_shape=out_shape,
               compiler_params=_params("arbitrary"))(chip_idx, *args)
    res = {nm: tuple(outs[4 * k:4 * k + 4]) for k, nm in enumerate(SMALL_PARAMS)}
    return res, outs[4 * n_p], outs[4 * n_p + 1], outs[4 * n_p + 2]


def _block_diag(w):
    eye = jnp.eye(LRU_BLOCKS, dtype=F32)
    return (w[:, :, None, :] * eye[:, None, :, None]).reshape(LRU_W, LRU_W).astype(BF16)


def _lane_rep(v8):
    return jnp.broadcast_to(v8.reshape(SUBLANES, 1), (SUBLANES, LANES))


def kernel(x, c, ctx, c_ctx, w_ada, b_ada, norm1_g, norm2_g, w_in, ret_decay, conv_w, conv_b, lru_wa, lru_ba, lru_wx, lru_bx, lru_lambda, w_out, w_mlp1, w_mlp2, final_g, loss_target, m_c_ctx, m_w_ada, m_b_ada, m_norm1_g, m_norm2_g, m_w_in, m_ret_decay, m_conv_w, m_conv_b, m_lru_wa, m_lru_ba, m_lru_wx, m_lru_bx, m_lru_lambda, m_w_out, m_w_mlp1, m_w_mlp2, m_final_g, v_c_ctx, v_w_ada, v_b_ada, v_norm1_g, v_norm2_g, v_w_in, v_ret_decay, v_conv_w, v_conv_b, v_lru_wa, v_lru_ba, v_lru_wx, v_lru_bx, v_lru_lambda, v_w_out, v_w_mlp1, v_w_mlp2, v_final_g):
    ax, ay, ac = lax.axis_index("x"), lax.axis_index("y"), lax.axis_index("c")
    chip = 2 * ax + ay
    dev = 4 * ax + 2 * ay + ac
    c_idx = jnp.stack([ac, chip]).astype(jnp.int32)
    j_idx = chip.reshape(1).astype(jnp.int32)

    xt = x[0]
    t_len = xt.shape[0]
    ctxt = ctx[0]
    l_len = ctxt.shape[0]
    tgt = loss_target[0]
    ada_n = w_ada.shape[2]

    def my_half(w2d):
        r = w2d.shape[0] // 2
        return lax.dynamic_slice_in_dim(w2d, ac * r, r, axis=0).astype(BF16)

    pad8 = lambda a: jnp.pad(a, ((0, SUBLANES - a.shape[0]), (0, 0)))
    small = jnp.concatenate([pad8(conv_w[0]), pad8(lru_ba[0]), pad8(lru_bx[0]), pad8(lru_lambda[0])], axis=0)
    b_shard = lax.dynamic_slice_in_dim(b_ada, chip * ada_n, ada_n, axis=1)
    gw_in, _, small_all, a16, mod_parts, lgv, sgv = _head(
        my_half(w_in[0]), pad8(c), small, w_ada[0], b_shard, c_ctx, ret_decay[0])
    w4 = gw_in.reshape(N_CHIP, D_MODEL, IN_COLS // N_CHIP)

    mod_all = mod_parts[0::2].transpose(1, 0, 2).reshape(16, N_CHIP * ada_n)
    mod_me = lax.dynamic_slice_in_dim(mod_all, dev, 1, axis=0)
    sh1, sc1, g1, sh2, sc2, g2 = [mod_me[:, D_MODEL * k:D_MODEL * (k + 1)] for k in range(N_MOD)]
    csh1, csc1 = mod_all[8:9, 0:D_MODEL], mod_all[8:9, D_MODEL:2 * D_MODEL]

    cos2, sin2 = _rotary_tables(t_len)
    cos_c, sin_c = jnp.ones((l_len, DH), F32), jnp.zeros((l_len, DH), F32)
    n1g, n2g = norm1_g, norm2_g
    fg = final_g.reshape(1, D_MODEL)

    small_full = small_all[0::2].transpose(1, 0, 2).reshape(4 * SUBLANES, LRU_W)
    cw = small_full[0:4]
    cb = conv_b
    ba_f, ba_b = small_full[8:9], small_full[9:10]
    bx_f, bx_b = small_full[16:17], small_full[17:18]
    lam_f, lam_b = small_full[24:25], small_full[25:26]
    wa_f, wa_b = _block_diag(lru_wa[0, 0]), _block_diag(lru_wa[0, 1])
    wx_f, wx_b = _block_diag(lru_wx[0, 0]), _block_diag(lru_wx[0, 1])
    zero_h = jnp.zeros((1, LRU_W), F32)

    projc, xrc, hcb16 = _inproj_fwd(ctxt, n1g, csh1, csc1, w4, cos_c, sin_c, "inproj_fwd_ctx")
    s_f, s_b = _ctx_state_fwd(projc, lgv)
    xcc = _conv_fwd(xrc, cw, cb, "conv_fwd_ctx")
    par_f, par_b = (wa_f, wx_f, ba_f, bx_f, lam_f), (wa_b, wx_b, ba_b, bx_b, lam_b)
    hcf, hcbk = _lru_fwd(xcc, par_f, par_b, zero_h, zero_h, "lru_fwd_ctx")
    lru_sf, lru_sb = hcf[l_len - 1:l_len], hcbk[0:1]

    h1, h2 = my_half(w_mlp1[0]), my_half(w_mlp2[0])
    q = h1.shape[0] // 2
    (proj, xrl, hb16), ((gw_1a,),) = _inproj_fwd(xt, n1g, sh1, sc1, w4, cos2, sin2, "inproj_fwd",
                                           comms=(_AllGather([h1[:q]]),))
    (o_f, o_b, spf, spb), ((gw_1b, gw_out),) = _ret_fwd(proj, lgv, s_f, s_b,
                                                       comms=(_AllGather([h1[q:], my_half(w_out[0])]),))
    xcl = _conv_fwd(xrl, cw, cb, "conv_fwd")
    (hf, hbk), ((gw_2a, gw_2b),) = _lru_fwd(xcl, par_f, par_b, lru_sf, lru_sb, "lru_fwd",
                                           comms=(_AllGather([h2[:q], h2[q:]]),))
    wo = gw_out.reshape(D_MODEL, D_MODEL)
    x1, cat = _mix_fwd(o_f, o_b, proj, hf, hbk, wo, xt, g1)

    (dx1, h2b, ab, dub, dmb, dsc2, dsh2, dg2, dn2g, dfg, lossv) = _mlp(
        x1, n2g, sh2, sc2, g2, fg, (gw_1a, gw_1b), (gw_2a, gw_2b), tgt)
    gw_mlp1 = _tn(h2b, dub, N_CHIP, False, True, "grad_w_mlp1")
    b_1 = gw_mlp1.reshape(N_DEV, D_MODEL // 2, MLP_H // N_CHIP)
    gw_mlp2, ((r_1,),) = _tn(ab, dmb, N_CHIP, True, False, "grad_w_mlp2", comms=(_pair_exchange([b_1]),))

    half = D_MODEL // 4
    top, bot = (0, half), (half, half)
    b_2 = gw_mlp2.reshape(N_DEV, MLP_H // N_DEV, D_MODEL)
    p_1, pb_1 = _pair_add(b_1, r_1, c_idx, "rs_pair_add_w_mlp1")
    (do, dhs, dg, dgate, dyb, dg1), ((q_1a,), (r_2,)) = _mix_bwd(
        o_f, o_b, proj, hf, hbk, wo, cat, dx1, g1, comms=(_chip_exchange([pb_1], top), _pair_exchange([b_2])))
    gw_o = _tn(cat, dyb, 1, False, False, "grad_w_out")
    b_o = gw_o.reshape(N_DEV, D_MODEL // N_DEV, D_MODEL)
    p_2, pb_2 = _pair_add(b_2, r_2, c_idx, "rs_pair_add_w_mlp2")

    ((dq_f, dk_f, dv_f, ds_f, drd_f), (dq_b, dk_b, dv_b, ds_b, drd_b)), ((q_1b,), (q_2a,), (r_o,)) = _ret_bwd(
        proj, lgv, sgv, spf, spb, do,
        comms=(_chip_exchange([pb_1], bot), _chip_exchange([pb_2], top), _pair_exchange([b_o])))
    p_o, pb_o = _pair_add(b_o, r_o, c_idx, "rs_pair_add_w_out")
    h_1 = _chip_add(p_1, (q_1a, q_1b), c_idx, "rs_chip_add_w_mlp1")

    ((dxc_f, dpre_f, dba_f, dbx_f, dlam_f, dh0_f), (dxc_b, dpre_b, dba_b, dbx_b, dlam_b, dh0_b)), (
        (q_2b,), (q_o,), (f_1,)) = _lru_bwd(
        xcl, par_f, par_b, hf, hbk, lru_sf, lru_sb, dhs, dhs, "lru_bwd",
        comms=(_chip_exchange([pb_2], bot), _chip_exchange([pb_o]), _pair_gather([h_1])))
    h_2 = _chip_add(p_2, (q_2a, q_2b), c_idx, "rs_chip_add_w_mlp2")
    h_o = _chip_add(p_o, (q_o,), c_idx, "rs_chip_add_w_out")
    dxr, dcw, dcb = _conv_bwd(dxc_f, dxc_b, xrl, cw, "conv_bwd")
    grad_x, dpb, dn1g, dsh1, dsc1 = _inproj_bwd(
        xt, n1g, sh1, sc1, w4, cos2, sin2, [dq_f, dq_b, dk_f, dk_b, dv_f, dv_b, dg, dxr, dgate], dx1, "inproj_bwd")

    dkc, dvc, drd_c = _ctx_state_bwd(projc, lgv, sgv, ds_f, ds_b)
    zc = jnp.zeros((l_len, LRU_W), F32)
    dhc_f = lax.dynamic_update_slice(zc, dh0_f, (l_len - 1, 0))
    dhc_b = lax.dynamic_update_slice(zc, dh0_b, (0, 0))
    ((dxcc_f, dprec_f, dbac_f, dbxc_f, dlamc_f, _), (dxcc_b, dprec_b, dbac_b, dbxc_b, dlamc_b, _)), _ = _lru_bwd(
        xcc, par_f, par_b, hcf, hcbk, zero_h, zero_h, dhc_f, dhc_b, "lru_bwd_ctx")
    dxrc, dcw_c, dcb_c = _conv_bwd(dxcc_f, dxcc_b, xrc, cw, "conv_bwd_ctx")
    zr = jnp.zeros((l_len, RET_W), BF16)
    _, dpbc, dn1g_c, dcsh1, dcsc1 = _inproj_bwd(
        ctxt, n1g, csh1, csc1, w4, cos_c, sin_c, [zr, zr, dkc, zr, dvc, zr, zr, dxrc, zr],
        jnp.zeros((l_len, D_MODEL), F32), "inproj_bwd_ctx")

    gw_i = _tn(hb16, dpb, N_CHIP, False, True, "grad_w_in", extra=(hcb16, dpbc))
    b_i = gw_i.reshape(N_DEV, D_MODEL // 2, IN_COLS // N_CHIP)
    gwa_f, ((r_i,), (f_2,), (f_o,)) = _tn(xcl, dpre_f, 2, False, True, "grad_lru_gates_f", extra=(xcc, dprec_f),
                                          comms=(_pair_exchange([b_i]), _pair_gather([h_2]), _pair_gather([h_o])))
    p_i, pb_i = _pair_add(b_i, r_i, c_idx, "rs_pair_add_w_in")
    gwa_b, ((q_i,),) = _tn(xcl, dpre_b, 2, False, True, "grad_lru_gates_b", extra=(xcc, dprec_b),
                           comms=(_chip_exchange([pb_i]),))
    slab, ga, gx = _pack_small(
        [lossv, dsh1, dsc1, dg1, dsh2, dsc2, dg2, dcsh1, dcsc1, dn1g, dn1g_c, dn2g, dfg],
        (drd_f, drd_b, drd_c), (dcw, dcw_c), (dcb, dcb_c),
        (dba_f, dbac_f, dba_b, dbac_b, dbx_f, dbxc_f, dbx_b, dbxc_b, dlam_f, dlamc_f, dlam_b, dlamc_b),
        (gwa_f, gwa_b))
    (f_i,), (slab_all, ga_all, gx_all) = _run_comms(
        [_pair_gather([_chip_add(p_i, (q_i,), c_idx, "rs_chip_add_w_in")]), _AllGather([slab, ga, gx])],
        "tail_exchanges")
    g_in, g_out, g_1, g_2 = _shard_of(f_i), _shard_of(f_o), _shard_of(f_1), _shard_of(f_2)
    big = {}
    for nm, w, g, m, v in (("w_in", w_in, g_in, m_w_in, v_w_in), ("w_out", w_out, g_out, m_w_out, v_w_out),
                           ("w_mlp1", w_mlp1, g_1, m_w_mlp1, v_w_mlp1), ("w_mlp2", w_mlp2, g_2, m_w_mlp2, v_w_mlp2)):
        go, d_, mn, vn = _adamw(w[0], g, m[0], v[0], "adamw_" + nm)
        big[nm] = (go[None], d_[None], mn[None], vn[None])
    params = {
        "b_ada": (b_ada, m_b_ada, v_b_ada), "norm1_g": (norm1_g, m_norm1_g, v_norm1_g),
        "norm2_g": (norm2_g, m_norm2_g, v_norm2_g), "final_g": (final_g, m_final_g, v_final_g),
        "ret_decay": (ret_decay, m_ret_decay, v_ret_decay), "conv_w": (conv_w, m_conv_w, v_conv_w),
        "conv_b": (conv_b, m_conv_b, v_conv_b), "lru_wa": (lru_wa, m_lru_wa, v_lru_wa),
        "lru_ba": (lru_ba, m_lru_ba, v_lru_ba), "lru_wx": (lru_wx, m_lru_wx, v_lru_wx),
        "lru_bx": (lru_bx, m_lru_bx, v_lru_bx), "lru_lambda": (lru_lambda, m_lru_lambda, v_lru_lambda),
    }
    as2d = {
        "b_ada": lambda a: a, "norm1_g": lambda a: a, "norm2_g": lambda a: a, "conv_b": lambda a: a,
        "final_g": lambda a: a.reshape(1, D_MODEL), "ret_decay": lambda a: _lane_rep(a.reshape(-1)),
        "conv_w": lambda a: a[0], "lru_ba": lambda a: a[0], "lru_bx": lambda a: a[0], "lru_lambda": lambda a: a[0],
        "lru_wa": lambda a: a.reshape(2 * LRU_W, LRU_BD), "lru_wx": lambda a: a.reshape(2 * LRU_W, LRU_BD),
    }
    res, b128, dmc8, loss8 = _finalize_small(
        j_idx, slab_all, ga_all, gx_all, {nm: tuple(as2d[nm](a) for a in params[nm]) for nm in SMALL_PARAMS})
    loss = loss8[0, 0]
    small_out = {}
    for nm in SMALL_PARAMS:
        shp = params[nm][0].shape
        if nm == "ret_decay":
            small_out[nm] = tuple(o[:, 0].reshape(shp) for o in res[nm])
        else:
            small_out[nm] = tuple(o.reshape(shp) for o in res[nm])

    g_ada = _ada_grad(jnp.pad(a16.T, ((0, 0), (0, LANES - 16))), b128)
    g_ada, d_ada, m_ada, v_ada = _adamw(w_ada[0], g_ada, m_w_ada[0], v_w_ada[0], "adamw_w_ada")

    (cparts,) = _all_gather([_cctx_partial(dmc8, w_ada[0])], "gather_cctx")
    g_cc, d_cc, m_cc, v_cc = _cctx_final(cparts, c_ctx, m_c_ctx, v_c_ctx)
    small_out["c_ctx"] = tuple(a.reshape(D_MODEL) for a in (g_cc, d_cc, m_cc, v_cc))
    small_out["w_ada"] = (g_ada[None], d_ada[None], m_ada[None], v_ada[None])
    small_out.update(big)

    order = ["c_ctx", "w_ada", "b_ada", "norm1_g", "norm2_g", "w_in", "ret_decay", "conv_w", "conv_b", "lru_wa", "lru_ba",
             "lru_wx", "lru_bx", "lru_lambda", "w_out", "w_mlp1", "w_mlp2", "final_g"]
    outs = [loss, grad_x[None]]
    for k in range(4):
        outs += [small_out[nm][k] for nm in order]
    return tuple(outs)
```

```python
import math

import jax
import jax.numpy as jnp
from jax import lax
from jax.experimental import pallas as pl
from jax.experimental.pallas import tpu as pltpu

F32 = jnp.float32
BF16 = jnp.bfloat16

D_MODEL = 1024
HEADS = 4
DH = 128
CHUNK = 256
RET_W = HEADS * DH
LRU_W = 512
LRU_BLOCKS = 8
LRU_BD = LRU_W // LRU_BLOCKS
LRU_C = 8.0
IN_COLS = 4 * RET_W + 2 * LRU_W
MLP_H = 4 * D_MODEL
N_MOD = 6
GRID_W = 64
ROPE_BASE = 10000.0
K_SCALE = DH ** -0.5
EPS = 1e-6
GELU_K = math.sqrt(2.0 / math.pi)
GELU_C = 0.044715

ADAM_LR = 0.001
ADAM_B1 = 0.9
ADAM_B2 = 0.999
ADAM_EPS = 1e-08
ADAM_WD = 0.01
ADAM_STEP = 10

N_DEV = 8
N_CHIP = 4
SUBLANES = 8
LANES = 128
VMEM_LIMIT_V7X = 56 * 1024 * 1024
MESH = pl.DeviceIdType.MESH
ANY = pl.BlockSpec(memory_space=pl.ANY)


def _pc(body, **kw):
    return pl.pallas_call(body, **kw)


def _params(*sem):
    return pltpu.CompilerParams(dimension_semantics=sem if sem else None, vmem_limit_bytes=VMEM_LIMIT_V7X)


def _tile(t, big=False):
    if big and t >= 1024:
        return 512
    return 256 if t >= 256 else t


def _sds(shape, dtype=F32):
    return jax.ShapeDtypeStruct(tuple(shape), dtype)


def _full(shape):
    nd = len(shape)
    return pl.BlockSpec(tuple(shape), lambda *_: (0,) * nd)


def _sigmoid(x):
    return 1.0 / (1.0 + jnp.exp(-x))


def _log1p_pos(y):
    s = y * (1.0 - y * (0.5 - y * (1.0 / 3.0 - y * (0.25 - y * (0.2 - y / 6.0)))))
    return jnp.where(y < 0.03, s, jnp.log(1.0 + y))


def _softplus(z):
    return jnp.maximum(z, 0.0) + _log1p_pos(jnp.exp(-jnp.abs(z)))


def _one_minus_sq(la, a):
    t = la * (1.0 + la * (0.5 + la * (1.0 / 6.0 + la * (1.0 / 24.0 + la * (1.0 / 120.0)))))
    return jnp.where(la > -0.125, -t, 1.0 - a) * (1.0 + a)


def _rms(x):
    r = lax.rsqrt(jnp.mean(x * x, axis=-1, keepdims=True) + EPS)
    return x * r, r


def _dot(a, b):
    return jnp.dot(a, b, preferred_element_type=F32)


def _dot_nt(a, b):
    return lax.dot_general(a, b, (((1,), (1,)), ((), ())), preferred_element_type=F32)


def _dot_tn(a, b):
    return lax.dot_general(a, b, (((0,), (0,)), ((), ())), preferred_element_type=F32)


def _sum0(x):
    return jnp.sum(x, axis=0, keepdims=True)


def _norm_mod_bwd(x, g, sc, dh):
    xh, r = _rms(x)
    hn = xh * g
    dhn = dh * (1.0 + sc)
    dxh = dhn * g
    dx = r * (dxh - xh * jnp.mean(dxh * xh, axis=-1, keepdims=True))
    return dx, _sum0(dhn * xh), _sum0(dh), _sum0(dh * hn)


def _dev_index(p):
    return 4 * p[0] + 2 * p[1] + p[2]


def _mesh_pos():
    return lax.axis_index("x"), lax.axis_index("y"), lax.axis_index("c")


class _AllGather:
    def __init__(self, arrs):
        n = len(arrs)
        self.arrays = list(arrs)
        self.out_shapes = [_sds((N_DEV,) + a.shape, a.dtype) for a in arrs]
        self.scratch = ([pltpu.VMEM(a.shape, a.dtype) for a in arrs]
                        + [pltpu.SemaphoreType.DMA((7 * n,)), pltpu.SemaphoreType.DMA((7 * n,)),
                           pltpu.SemaphoreType.DMA((n,))])
        self.aliases = {}

    def _parts(self, ins, outs, scr):
        n = len(self.arrays)
        stage = scr[:n]
        send_sems, recv_sems, local_sems = scr[n:]
        x, y, c = _mesh_pos()
        me, sib = (x, y, c), (x, y, 1 - c)
        chips = [(1 - x, y), (x, 1 - y), (1 - x, 1 - y)]

        def copy(t, k, block, to, own=False):
            dst = outs[t].at[_dev_index(block)]
            return pltpu.make_async_remote_copy(
                src_ref=ins[t] if own else dst, dst_ref=dst,
                send_sem=send_sems.at[7 * t + k], recv_sem=recv_sems.at[7 * t + k],
                device_id=to, device_id_type=MESH)

        first = []
        for t in range(n):
            first.append(copy(t, 0, me, sib, own=True))
            for j, ch in enumerate(chips):
                first.append(copy(t, 1 + j, me, (*ch, c), own=True))
        stage_in = [pltpu.make_async_copy(ins[t], stage[t], local_sems.at[t]) for t in range(n)]
        mine = [pltpu.make_async_copy(stage[t], outs[t].at[_dev_index(me)], local_sems.at[t]) for t in range(n)]
        return n, c, me, sib, chips, copy, first, stage_in, mine

    def start(self, ins, outs, scr):
        n, _, _, _, _, _, first, stage_in, mine = self._parts(ins, outs, scr)
        for cp in stage_in:
            cp.start()
        for cp in first:
            cp.start()
        for t in range(n):
            stage_in[t].wait()
            mine[t].start()

    def relay(self, ins, outs, scr):
        n, c, me, sib, chips, copy, _, _, _ = self._parts(ins, outs, scr)
        for j, ch in enumerate(chips):
            for t in range(n):
                copy(t, 1 + j, (*ch, c), me).wait_recv()
                copy(t, 4 + j, (*ch, c), sib).start()

    def finish(self, ins, outs, scr):
        n, c, me, sib, chips, copy, first, _, mine = self._parts(ins, outs, scr)
        passed = [copy(t, 4 + j, (*ch, c), sib) for j, ch in enumerate(chips) for t in range(n)]
        for t in range(n):
            copy(t, 0, sib, me).wait_recv()
            for j, ch in enumerate(chips):
                copy(t, 4 + j, (*ch, 1 - c), me).wait_recv()
        for cp in first + passed:
            cp.wait_send()
        for cp in mine:
            cp.wait()


class _Exchange:
    def __init__(self, arrays, out_shapes, plan, n_copies, aliases=None):
        self.arrays = list(arrays)
        self.out_shapes = list(out_shapes)
        self.plan = plan
        self.scratch = [pltpu.SemaphoreType.DMA((n_copies,)), pltpu.SemaphoreType.DMA((n_copies,))]
        self.aliases = aliases or {}

    def _copies(self, ins, outs, scr):
        send_sems, recv_sems = scr
        snd, rcv = [], []
        for i, (src, dst, peer, lands) in enumerate(self.plan(ins, outs, _mesh_pos())):
            kw = dict(send_sem=send_sems.at[i], recv_sem=recv_sems.at[i], device_id=peer, device_id_type=MESH)
            snd.append(pltpu.make_async_remote_copy(src_ref=src, dst_ref=dst, **kw))
            rcv.append(pltpu.make_async_remote_copy(src_ref=src, dst_ref=lands, **kw))
        return snd, rcv

    def start(self, ins, outs, scr):
        for cp in self._copies(ins, outs, scr)[0]:
            cp.start()

    def relay(self, ins, outs, scr):
        pass

    def finish(self, ins, outs, scr):
        snd, rcv = self._copies(ins, outs, scr)
        for cp in rcv:
            cp.wait_recv()
        for cp in snd:
            cp.wait_send()


def _pair_exchange(grads):
    n = len(grads)

    def plan(ins, outs, pos):
        x, y, c = pos
        return [(ins[t].at[2 * j + (1 - c)], outs[t].at[j], (x, y, 1 - c), outs[t].at[j])
                for t in range(n) for j in range(N_CHIP)]

    return _Exchange(grads, [_sds((N_CHIP,) + g.shape[1:], g.dtype) for g in grads], plan, N_CHIP * n)


def _chip_exchange(parts, rows=None):
    n = len(parts)

    def plan(ins, outs, pos):
        x, y, c = pos
        chips = [(1 - x, y), (x, 1 - y), (1 - x, 1 - y)]

        def src(t, ch):
            blk = ins[t].at[2 * ch[0] + ch[1]]
            return blk if rows is None else blk.at[pl.ds(rows[0], rows[1])]

        return [(src(t, ch), outs[t].at[k], (*ch, c), outs[t].at[k]) for t in range(n) for k, ch in enumerate(chips)]

    shapes = [_sds((3, p.shape[1] if rows is None else rows[1]) + p.shape[2:], p.dtype) for p in parts]
    return _Exchange(parts, shapes, plan, 3 * n)


def _pair_gather(bufs):
    n = len(bufs)

    def plan(ins, outs, pos):
        x, y, c = pos
        return [(ins[t].at[c], outs[t].at[c], (x, y, 1 - c), outs[t].at[1 - c]) for t in range(n)]

    return _Exchange(bufs, [_sds(b.shape, b.dtype) for b in bufs], plan, n, aliases={t: t for t in range(n)})


def _run_comms(comms, name):
    c_in = [len(cm.arrays) for cm in comms]
    c_out = [len(cm.out_shapes) for cm in comms]
    c_scr = [len(cm.scratch) for cm in comms]
    aliases = {}
    for k, cm in enumerate(comms):
        for a, b in cm.aliases.items():
            aliases[sum(c_in[:k]) + a] = sum(c_out[:k]) + b

    def split(refs, counts):
        out, pos = [], 0
        for cnt in counts:
            out.append(refs[pos:pos + cnt])
            pos += cnt
        return out

    def body(*refs):
        ins = split(refs[:sum(c_in)], c_in)
        outs = split(refs[sum(c_in):sum(c_in) + sum(c_out)], c_out)
        scr = split(refs[sum(c_in) + sum(c_out):], c_scr)
        for phase in ("start", "relay", "finish"):
            for k, cm in enumerate(comms):
                getattr(cm, phase)(ins[k], outs[k], scr[k])

    outs = _pc(body, name=name, out_shape=[s for cm in comms for s in cm.out_shapes],
               in_specs=[ANY] * sum(c_in), out_specs=[ANY] * sum(c_out), input_output_aliases=aliases,
               scratch_shapes=[s for cm in comms for s in cm.scratch],
               compiler_params=_params())(*[a for cm in comms for a in cm.arrays])
    return split(list(outs), c_out)


def _all_gather(arrs, name):
    return _run_comms([_AllGather(arrs)], name)[0]


def _call(body, *, name, grid, in_specs, out_specs, out_shape, scratch_shapes, sem, args, comms=()):
    n_in, n_out, n_scr = len(in_specs), len(out_specs), len(scratch_shapes)
    c_in = [len(cm.arrays) for cm in comms]
    c_out = [len(cm.out_shapes) for cm in comms]
    c_scr = [len(cm.scratch) for cm in comms]
    aliases = {}
    for k, cm in enumerate(comms):
        for a, b in cm.aliases.items():
            aliases[n_in + sum(c_in[:k]) + a] = n_out + sum(c_out[:k]) + b

    def split(refs, counts):
        out, pos = [], 0
        for cnt in counts:
            out.append(refs[pos:pos + cnt])
            pos += cnt
        return out

    def wrapped(*refs):
        ins = refs[:n_in + sum(c_in)]
        outs = refs[len(ins):len(ins) + n_out + sum(c_out)]
        scr = refs[len(ins) + len(outs):]
        cins, couts, cscr = split(ins[n_in:], c_in), split(outs[n_out:], c_out), split(scr[n_scr:], c_scr)
        if comms:
            first = pl.program_id(0) == 0
            last = pl.program_id(0) == grid[0] - 1
            for k in range(1, len(grid)):
                first = jnp.logical_and(first, pl.program_id(k) == 0)
                last = jnp.logical_and(last, pl.program_id(k) == grid[k] - 1)

            @pl.when(first)
            def _():
                for k, cm in enumerate(comms):
                    cm.start(cins[k], couts[k], cscr[k])
        body(*ins[:n_in], *outs[:n_out], *scr[:n_scr])
        if comms:
            relay_early = len(grid) == 1 and grid[0] >= 4
            if relay_early:
                @pl.when(pl.program_id(0) == (7 * grid[0]) // 8 - 1)
                def _():
                    for k, cm in enumerate(comms):
                        cm.relay(cins[k], couts[k], cscr[k])

            @pl.when(last)
            def _():
                for k, cm in enumerate(comms):
                    if not relay_early:
                        cm.relay(cins[k], couts[k], cscr[k])
                    cm.finish(cins[k], couts[k], cscr[k])

    outs = _pc(wrapped, name=name, grid=grid,
               in_specs=list(in_specs) + [ANY] * sum(c_in), out_specs=list(out_specs) + [ANY] * sum(c_out),
               out_shape=list(out_shape) + [s for cm in comms for s in cm.out_shapes],
               scratch_shapes=list(scratch_shapes) + [s for cm in comms for s in cm.scratch],
               input_output_aliases=aliases, compiler_params=_params(*sem),
               )(*args, *[a for cm in comms for a in cm.arrays])
    outs = list(outs)
    return outs[:n_out], split(outs[n_out:], c_out)


def _row_block(r):
    for b in (512, 256, 128, 64, 32, 16, 8):
        if r % b == 0:
            return b
    return r


def _pair_add(g, recv, cj_idx, name):
    _, r, cc = g.shape
    br = _row_block(r)

    def body(cj_ref, g_ref, r_ref, own_ref, pb_ref):
        s = g_ref[...] + r_ref[...]
        pb_ref[...] = s.astype(BF16)

        @pl.when(pl.program_id(1) == cj_ref[1])
        def _():
            own_ref[...] = s[0]

    grid_spec = pltpu.PrefetchScalarGridSpec(
        num_scalar_prefetch=1, grid=(r // br, N_CHIP),
        in_specs=[pl.BlockSpec((1, br, cc), lambda i, j, cj_ref: (2 * j + cj_ref[0], i, 0)),
                  pl.BlockSpec((1, br, cc), lambda i, j, cj_ref: (j, i, 0))],
        out_specs=[pl.BlockSpec((br, cc), lambda i, j, cj_ref: (i, 0)),
                   pl.BlockSpec((1, br, cc), lambda i, j, cj_ref: (j, i, 0))])
    return _pc(body, name=name, grid_spec=grid_spec,
               out_shape=[_sds((r, cc)), _sds((N_CHIP, r, cc), BF16)],
               compiler_params=_params("arbitrary", "arbitrary"))(cj_idx, g, recv)


def _chip_add(p, qs, cj_idx, name):
    r, cc = p.shape
    nq = len(qs)
    br = _row_block(r // nq)
    nb = r // nq // br

    def body(cj_ref, p_ref, *refs):
        o_ref = refs[-1]
        if nq == 2:
            top = pl.program_id(0) < nb
            q = [jnp.where(top, refs[0][k], refs[1][k]).astype(F32) for k in range(3)]
        else:
            q = [refs[0][k].astype(F32) for k in range(3)]
        o_ref[0] = ((p_ref[...] + q[0]) + q[1]) + q[2]

    q_specs = [pl.BlockSpec((3, br, cc), lambda i, cj_ref, h=h: (0, jnp.clip(i - h * nb, 0, nb - 1), 0))
               for h in range(nq)]
    grid_spec = pltpu.PrefetchScalarGridSpec(
        num_scalar_prefetch=1, grid=(r // br,),
        in_specs=[pl.BlockSpec((br, cc), lambda i, cj_ref: (i, 0))] + q_specs,
        out_specs=pl.BlockSpec((1, br, cc), lambda i, cj_ref: (cj_ref[0], i, 0)))
    return _pc(body, name=name, grid_spec=grid_spec, out_shape=_sds((2, r, cc)),
               compiler_params=_params("arbitrary"))(cj_idx, p, *qs)


def _shard_of(both):
    return both.reshape((2 * both.shape[1],) + both.shape[2:])


ADAMW_CHUNKS = 4


def _adamw(w, g, m, v, name):
    r, cc = w.shape
    rows = r // ADAMW_CHUNKS
    assert rows * ADAMW_CHUNKS == r and rows % SUBLANES == 0
    c1 = 1.0 - ADAM_B1 ** ADAM_STEP
    c2 = 1.0 - ADAM_B2 ** ADAM_STEP

    def body(w_hbm, g_hbm, m_hbm, v_hbm, go_hbm, d_hbm, mo_hbm, vo_hbm, wb, gb, mb, vb, sem_in, sem_out):
        srcs, bufs, dsts = (w_hbm, g_hbm, m_hbm, v_hbm), (wb, gb, mb, vb), (d_hbm, go_hbm, mo_hbm, vo_hbm)

        def load(a, k):
            sl = pl.ds(k * rows, rows)
            return pltpu.make_async_copy(srcs[a].at[sl], bufs[a].at[sl], sem_in.at[a, k])

        def store(a, k):
            sl = pl.ds(k * rows, rows)
            return pltpu.make_async_copy(bufs[a].at[sl], dsts[a].at[sl], sem_out.at[a, k])

        for k in range(ADAMW_CHUNKS):
            for a in range(4):
                load(a, k).start()
        for k in range(ADAMW_CHUNKS):
            for a in range(4):
                load(a, k).wait()
            store(1, k).start()
            sl = pl.ds(k * rows, rows)
            gg = gb[sl]
            mn = ADAM_B1 * mb[sl] + (1.0 - ADAM_B1) * gg
            vn = ADAM_B2 * vb[sl] + (1.0 - ADAM_B2) * (gg * gg)
            mh = mn / c1
            vh = vn / c2
            wb[sl] = -ADAM_LR * (mh / (jnp.sqrt(vh) + ADAM_EPS) + ADAM_WD * wb[sl])
            mb[sl] = mn
            vb[sl] = vn
            for a in (0, 2, 3):
                store(a, k).start()
        for k in range(ADAMW_CHUNKS):
            for a in range(4):
                store(a, k).wait()

    go, d, mo, vo = _pc(body, name=name, in_specs=[ANY] * 4, out_specs=[ANY] * 4, out_shape=[_sds((r, cc))] * 4,
                        scratch_shapes=[pltpu.VMEM((r, cc), F32)] * 4 + [pltpu.SemaphoreType.DMA((4, ADAMW_CHUNKS))] * 2,
                        compiler_params=_params())(w, g, m, v)
    return go, d, mo, vo


def _head(w_half, c8, small, w_ada, b_shard, c_ctx, ret_decay):
    ada_n = w_ada.shape[1]
    mod_sds = _sds((16, ada_n))
    ag_w, ag_c, ag_m = _AllGather([w_half]), _AllGather([c8, small]), _AllGather([mod_sds])
    n_w, n_c, n_m = len(ag_w.scratch), len(ag_c.scratch), len(ag_m.scratch)

    def body(w_ref, c_ref, s_ref, wada_ref, b_ref, cc_ref, rd_ref,
             gw_ref, call_ref, sall_ref, a_ref, modp_ref, mall_ref, lg_ref, sg_ref, *scr):
        scr_w, scr_c, scr_m = scr[:n_w], scr[n_w:n_w + n_c], scr[n_w + n_c:n_w + n_c + n_m]
        c_v, w_v, m_v, sems = scr[n_w + n_c + n_m:]
        ag_w.start((w_ref,), (gw_ref,), scr_w)
        ag_c.start((c_ref, s_ref), (call_ref, sall_ref), scr_c)
        load_w = pltpu.make_async_copy(wada_ref, w_v, sems.at[0])
        load_w.start()
        rd = rd_ref[...]
        lg_ref[...] = -_softplus(-rd)
        sg_ref[...] = _sigmoid(-rd)
        ag_c.relay((c_ref, s_ref), (call_ref, sall_ref), scr_c)
        ag_c.finish((c_ref, s_ref), (call_ref, sall_ref), scr_c)
        load_c = pltpu.make_async_copy(call_ref, c_v, sems.at[1])
        load_c.start()
        load_c.wait()
        a_ref[...] = jnp.zeros_like(a_ref)
        for d in range(N_DEV):
            cd = c_v[d, 0:1, :]
            a_ref[d:d + 1, :] = cd * _sigmoid(cd)
        cc = cc_ref[...]
        a_ref[N_DEV:N_DEV + 1, :] = cc * _sigmoid(cc)
        load_w.wait()
        m_v[...] = jnp.dot(a_ref[...], w_v[...], preferred_element_type=F32,
                           precision=lax.Precision.HIGHEST) + b_ref[...]
        put = pltpu.make_async_copy(m_v, modp_ref, sems.at[2])
        put.start()
        put.wait()
        ag_m.start((modp_ref,), (mall_ref,), scr_m)
        ag_m.relay((modp_ref,), (mall_ref,), scr_m)
        ag_m.finish((modp_ref,), (mall_ref,), scr_m)
        ag_w.relay((w_ref,), (gw_ref,), scr_w)
        ag_w.finish((w_ref,), (gw_ref,), scr_w)

    rd = jnp.broadcast_to(ret_decay.reshape(2, HEADS).T[:, :, None], (HEADS, 2, LANES))
    lane = _full((HEADS, 2, LANES))
    outs = _pc(
        body, name="head",
        in_specs=[ANY, ANY, ANY, ANY, _full((1, ada_n)), _full((1, D_MODEL)), lane],
        out_specs=[ANY, ANY, ANY, _full((16, D_MODEL)), ANY, ANY, lane, lane],
        out_shape=ag_w.out_shapes + ag_c.out_shapes + [_sds((16, D_MODEL)), mod_sds] + ag_m.out_shapes
        + [_sds((HEADS, 2, LANES))] * 2,
        scratch_shapes=ag_w.scratch + ag_c.scratch + ag_m.scratch
        + [pltpu.VMEM((N_DEV,) + c8.shape, F32), pltpu.VMEM(w_ada.shape, F32), pltpu.VMEM((16, ada_n), F32),
           pltpu.SemaphoreType.DMA((3,))],
        compiler_params=_params(),
    )(w_half, c8, small, w_ada, b_shard, c_ctx.reshape(1, D_MODEL), rd)
    gw, c_all, small_all, a16, _, mod_all, lgv, sgv = outs
    return gw, c_all, small_all, a16, mod_all, lgv, sgv


def _ada_grad(at, b):
    n = b.shape[1]
    bn = 512

    def body(a_ref, b_ref, o_ref):
        o_ref[...] = jnp.dot(a_ref[...], b_ref[...], preferred_element_type=F32, precision=lax.Precision.HIGHEST)

    return _pc(body, name="ada_grad", grid=(n // bn,),
               in_specs=[_full((D_MODEL, LANES)), pl.BlockSpec((LANES, bn), lambda i: (0, i))],
               out_specs=pl.BlockSpec((D_MODEL, bn), lambda i: (0, i)), out_shape=_sds((D_MODEL, n)),
               compiler_params=_params("arbitrary"))(at, b)


def _cctx_partial(dmc8, w_ada):
    n = w_ada.shape[1]
    bn = 512

    def body(d_ref, w_ref, o_ref):
        @pl.when(pl.program_id(0) == 0)
        def _():
            o_ref[...] = jnp.zeros_like(o_ref)
        o_ref[...] += lax.dot_general(d_ref[...], w_ref[...], (((1,), (1,)), ((), ())),
                                      preferred_element_type=F32, precision=lax.Precision.HIGHEST)

    return _pc(body, name="cctx_partial", grid=(n // bn,),
               in_specs=[pl.BlockSpec((8, bn), lambda i: (0, i)), pl.BlockSpec((D_MODEL, bn), lambda i: (0, i))],
               out_specs=_full((8, D_MODEL)), out_shape=_sds((8, D_MODEL)),
               compiler_params=_params("arbitrary"))(dmc8, w_ada)


def _cctx_final(parts, c_ctx, m, v):
    c1 = 1.0 - ADAM_B1 ** ADAM_STEP
    c2 = 1.0 - ADAM_B2 ** ADAM_STEP

    def body(p_ref, c_ref, m_ref, v_ref, g_ref, d_ref, mo_ref, vo_ref):
        s = ((p_ref[0, 0:1, :] + p_ref[2, 0:1, :]) + p_ref[4, 0:1, :]) + p_ref[6, 0:1, :]
        z = c_ref[...]
        sg = _sigmoid(z)
        gg = s * (sg * (1.0 + z * (1.0 - sg)))
        g_ref[...] = gg
        mn = ADAM_B1 * m_ref[...] + (1.0 - ADAM_B1) * gg
        vn = ADAM_B2 * v_ref[...] + (1.0 - ADAM_B2) * (gg * gg)
        d_ref[...] = -ADAM_LR * ((mn / c1) / (jnp.sqrt(vn / c2) + ADAM_EPS) + ADAM_WD * z)
        mo_ref[...] = mn
        vo_ref[...] = vn

    row = _full((1, D_MODEL))
    return _pc(body, name="cctx_final", out_shape=[_sds((1, D_MODEL))] * 4,
               in_specs=[_full(parts.shape), row, row, row], out_specs=[row] * 4,
               compiler_params=_params())(parts, c_ctx.reshape(1, D_MODEL), m.reshape(1, D_MODEL), v.reshape(1, D_MODEL))


def _rotary_tables(t_len):
    rows = t_len // GRID_W
    n_freq = DH // 4
    inv = ROPE_BASE ** (-jnp.arange(n_freq, dtype=F32) / n_freq)
    row_ang = jnp.arange(rows, dtype=F32)[:, None] * inv
    col_ang = jnp.arange(GRID_W, dtype=F32)[:, None] * inv

    def spread(fn):
        return jnp.concatenate([jnp.repeat(fn(row_ang), GRID_W, axis=0), jnp.tile(fn(col_ang), (rows, 1))], axis=-1)

    cos, sin = spread(jnp.cos), spread(jnp.sin)
    return jnp.concatenate([cos, cos], axis=-1), jnp.concatenate([-sin, sin], axis=-1)


def _inproj_fwd(x, gn, sh, sc, w4, cos2, sin2, name, comms=()):
    t = x.shape[0]
    tm = _tile(t, True)
    nc = IN_COLS // N_CHIP

    def body(x_ref, gn_ref, sh_ref, sc_ref, w_ref, c_ref, s_ref, p_ref, xr_ref, hb_ref, p_s):
        xh, _ = _rms(x_ref[...])
        h = xh * gn_ref[...] * (1.0 + sc_ref[...]) + sh_ref[...]
        hb = h.astype(BF16)
        hb_ref[...] = hb
        for j in range(N_CHIP):
            p_s[:, nc * j:nc * (j + 1)] = _dot(hb, w_ref[j])
        cc = c_ref[...]
        ss = s_ref[...]
        for hh in range(2 * HEADS):
            blk = p_s[:, DH * hh:DH * (hh + 1)]
            rot = blk * cc + pltpu.roll(blk, DH // 2, 1) * ss
            if hh >= HEADS:
                rot = rot * K_SCALE
            p_ref[:, DH * hh:DH * (hh + 1)] = rot.astype(BF16)
        p_ref[:, 2 * RET_W:] = p_s[:, 2 * RET_W:].astype(BF16)
        xr_ref[...] = p_s[:, 4 * RET_W:4 * RET_W + LRU_W]

    row = _full((1, D_MODEL))
    outs, couts = _call(
        body, name=name, grid=(t // tm,),
        in_specs=[pl.BlockSpec((tm, D_MODEL), lambda i: (i, 0)), row, row, row, _full(w4.shape),
                  pl.BlockSpec((tm, DH), lambda i: (i, 0)), pl.BlockSpec((tm, DH), lambda i: (i, 0))],
        out_specs=[pl.BlockSpec((tm, IN_COLS), lambda i: (i, 0)), pl.BlockSpec((tm, LRU_W), lambda i: (i, 0)),
                   pl.BlockSpec((tm, D_MODEL), lambda i: (i, 0))],
        out_shape=[_sds((t, IN_COLS), BF16), _sds((t, LRU_W)), _sds((t, D_MODEL), BF16)],
        scratch_shapes=[pltpu.VMEM((tm, IN_COLS), F32)], sem=("arbitrary",),
        args=(x, gn, sh, sc, w4, cos2, sin2), comms=comms)
    return (outs, couts) if comms else outs


def _inproj_bwd(x, gn, sh, sc, w4, cos2, sin2, pieces, dres, name):
    t = x.shape[0]
    tm = _tile(t)
    nc = IN_COLS // N_CHIP

    def body(x_ref, gn_ref, sh_ref, sc_ref, w_ref, c_ref, s_ref, dqf, dqb, dkf, dkb, dvf, dvb, dg, dxr, dgt, dres_ref,
             dx_ref, dpb_ref, dgn_ref, dsh_ref, dsc_ref):
        cc = c_ref[...]
        ss = s_ref[...]
        dq = dqf[...].astype(F32) + dqb[...].astype(F32)
        dk = dkf[...].astype(F32) + dkb[...].astype(F32)
        for hh in range(HEADS):
            sl = slice(DH * hh, DH * (hh + 1))
            b = dq[:, sl]
            dpb_ref[:, sl] = (b * cc + pltpu.roll(b * ss, DH // 2, 1)).astype(BF16)
            b = dk[:, sl]
            dpb_ref[:, RET_W + DH * hh:RET_W + DH * (hh + 1)] = (
                (b * cc + pltpu.roll(b * ss, DH // 2, 1)) * K_SCALE).astype(BF16)
        dpb_ref[:, 2 * RET_W:3 * RET_W] = (dvf[...].astype(F32) + dvb[...].astype(F32)).astype(BF16)
        dpb_ref[:, 3 * RET_W:4 * RET_W] = dg[...].astype(BF16)
        dpb_ref[:, 4 * RET_W:4 * RET_W + LRU_W] = dxr[...].astype(BF16)
        dpb_ref[:, 4 * RET_W + LRU_W:IN_COLS] = dgt[...].astype(BF16)
        dh = _dot_nt(dpb_ref[:, 0:nc], w_ref[0])
        for j in range(1, N_CHIP):
            dh = dh + _dot_nt(dpb_ref[:, nc * j:nc * (j + 1)], w_ref[j])
        dx, dgn_t, dsh_t, dsc_t = _norm_mod_bwd(x_ref[...], gn_ref[...], sc_ref[...], dh)
        dx_ref[...] = dres_ref[...] + dx

        @pl.when(pl.program_id(0) == 0)
        def _():
            dgn_ref[...] = jnp.zeros_like(dgn_ref)
            dsh_ref[...] = jnp.zeros_like(dsh_ref)
            dsc_ref[...] = jnp.zeros_like(dsc_ref)
        dgn_ref[...] += dgn_t
        dsh_ref[...] += dsh_t
        dsc_ref[...] += dsc_t

    row = _full((1, D_MODEL))
    pc = pl.BlockSpec((tm, RET_W), lambda i: (i, 0))
    big = pl.BlockSpec((tm, D_MODEL), lambda i: (i, 0))
    return _pc(body, name=name, grid=(t // tm,),
               in_specs=[big, row, row, row, _full(w4.shape),
                         pl.BlockSpec((tm, DH), lambda i: (i, 0)), pl.BlockSpec((tm, DH), lambda i: (i, 0))]
               + [pc] * 9 + [big],
               out_specs=[big, pl.BlockSpec((tm, IN_COLS), lambda i: (i, 0)), row, row, row],
               out_shape=[_sds((t, D_MODEL)), _sds((t, IN_COLS), BF16), _sds((1, D_MODEL)), _sds((1, D_MODEL)),
                          _sds((1, D_MODEL))],
               compiler_params=_params("arbitrary"))(x, gn, sh, sc, w4, cos2, sin2, *pieces, dres)


def _halo_specs(t, tm):
    n8 = tm // SUBLANES
    last8 = t // SUBLANES - 1
    prev = pl.BlockSpec((SUBLANES, LRU_W), lambda i: (jnp.maximum(i * n8 - 1, 0), 0))
    main = pl.BlockSpec((tm, LRU_W), lambda i: (i, 0))
    nxt = pl.BlockSpec((SUBLANES, LRU_W), lambda i: (jnp.minimum((i + 1) * n8, last8), 0))
    return prev, main, nxt


def _with_halo(prev_ref, main_ref, next_ref, i, nt):
    prev = jnp.where(i > 0, prev_ref[...], 0.0)
    nxt = jnp.where(i < nt - 1, next_ref[...], 0.0)
    return jnp.concatenate([prev, main_ref[...], nxt], axis=0)


def _conv_fwd(xr, cw, cb, name):
    t = xr.shape[0]
    tm = _tile(t, True)
    nt = t // tm
    n = tm + 2 * SUBLANES
    mid = slice(SUBLANES, SUBLANES + tm)

    def body(p_ref, m_ref, n_ref, w_ref, b_ref, o_ref):
        xp = _with_halo(p_ref, m_ref, n_ref, pl.program_id(0), nt)
        acc = b_ref[...] + pltpu.roll(xp, 1, 0)[mid] * w_ref[0:1, :]
        acc = acc + xp[mid] * w_ref[1:2, :]
        acc = acc + pltpu.roll(xp, n - 1, 0)[mid] * w_ref[2:3, :]
        acc = acc + pltpu.roll(xp, n - 2, 0)[mid] * w_ref[3:4, :]
        o_ref[...] = acc

    return _pc(body, name=name, grid=(nt,),
               in_specs=[*_halo_specs(t, tm), _full((4, LRU_W)), _full((1, LRU_W))],
               out_specs=pl.BlockSpec((tm, LRU_W), lambda i: (i, 0)), out_shape=_sds((t, LRU_W)),
               compiler_params=_params("arbitrary"))(xr, xr, xr, cw, cb)


def _conv_bwd(dxc_a, dxc_b, xr, cw, name):
    t = xr.shape[0]
    tm = _tile(t, True)
    nt = t // tm
    n = tm + 2 * SUBLANES
    mid = slice(SUBLANES, SUBLANES + tm)

    def body(ap_ref, am_ref, an_ref, bp_ref, bm_ref, bn_ref, xp_ref, xm_ref, xn_ref, w_ref, dx_ref, dw_ref, db_ref):
        i = pl.program_id(0)
        dp = _with_halo(ap_ref, am_ref, an_ref, i, nt) + _with_halo(bp_ref, bm_ref, bn_ref, i, nt)
        xp = _with_halo(xp_ref, xm_ref, xn_ref, i, nt)
        dx = pltpu.roll(dp, n - 1, 0)[mid] * w_ref[0:1, :]
        dx = dx + dp[mid] * w_ref[1:2, :]
        dx = dx + pltpu.roll(dp, 1, 0)[mid] * w_ref[2:3, :]
        dx = dx + pltpu.roll(dp, 2, 0)[mid] * w_ref[3:4, :]
        dx_ref[...] = dx.astype(BF16)
        d = dp[mid]

        @pl.when(i == 0)
        def _():
            dw_ref[...] = jnp.zeros_like(dw_ref)
            db_ref[...] = jnp.zeros_like(db_ref)
        dw_ref[0:1, :] += _sum0(d * pltpu.roll(xp, 1, 0)[mid])
        dw_ref[1:2, :] += _sum0(d * xp[mid])
        dw_ref[2:3, :] += _sum0(d * pltpu.roll(xp, n - 1, 0)[mid])
        dw_ref[3:4, :] += _sum0(d * pltpu.roll(xp, n - 2, 0)[mid])
        db_ref[...] += _sum0(d)

    return _pc(body, name=name, grid=(nt,),
               in_specs=[*_halo_specs(t, tm), *_halo_specs(t, tm), *_halo_specs(t, tm), _full((4, LRU_W))],
               out_specs=[pl.BlockSpec((tm, LRU_W), lambda i: (i, 0)), _full((4, LRU_W)), _full((1, LRU_W))],
               out_shape=[_sds((t, LRU_W), BF16), _sds((4, LRU_W)), _sds((1, LRU_W))],
               compiler_params=_params("arbitrary"))(dxc_a, dxc_a, dxc_a, dxc_b, dxc_b, dxc_b, xr, xr, xr, cw)


def _local_scan(a, b, reverse):
    n = a.shape[0]
    row = lax.broadcasted_iota(jnp.int32, a.shape, 0) & (SUBLANES - 1)
    for s in (1, 2, 4):
        if reverse:
            a_s, b_s, ok = pltpu.roll(a, n - s, 0), pltpu.roll(b, n - s, 0), row < SUBLANES - s
        else:
            a_s, b_s, ok = pltpu.roll(a, s, 0), pltpu.roll(b, s, 0), row >= s
        b = a * jnp.where(ok, b_s, 0.0) + b
        a = a * jnp.where(ok, a_s, 1.0)
    return a, b


def _carry_scan(a_s, b_s, out_ref, carry, reverse):
    ng = a_s.shape[0] // SUBLANES
    shape = carry.shape

    def step(g, cr):
        gg = (ng - 1 - g) if reverse else g
        off = pl.multiple_of(gg * SUBLANES, SUBLANES)
        h = a_s[pl.ds(off, SUBLANES), :] * cr + b_s[pl.ds(off, SUBLANES), :]
        out_ref[pl.ds(off, SUBLANES), :] = h
        edge = h[0:1, :] if reverse else h[SUBLANES - 1:SUBLANES, :]
        return jnp.broadcast_to(edge, shape)

    return lax.fori_loop(0, ng, step, carry)


def _lru_gates(xc, wa_ref, wx_ref, ba, bx, lam):
    xb = xc.astype(BF16)
    r = _sigmoid(_dot(xb, wa_ref[...]) + ba)
    ig = _sigmoid(_dot(xb, wx_ref[...]) + bx)
    sp = _softplus(-lam)
    la = -LRU_C * r * sp
    a = jnp.exp(la)
    mult = jnp.sqrt(_one_minus_sq(la, a))
    return r, ig, sp, a, mult


def _lru_fwd(xc, par_f, par_b, h0_f, h0_b, name, comms=()):
    t = xc.shape[0]
    tm = _tile(t, True)
    nt = t // tm

    def one(x_ref, prm, h0_ref, h_ref, a_s, b_s, c_s, reverse):
        wa_ref, wx_ref, ba_ref, bx_ref, lam_ref = prm

        @pl.when(pl.program_id(0) == 0)
        def _():
            c_s[...] = jnp.broadcast_to(h0_ref[...], c_s.shape)
        xv = x_ref[...]
        _, ig, _, a, mult = _lru_gates(xv, wa_ref, wx_ref, ba_ref[...], bx_ref[...], lam_ref[...])
        al, bl = _local_scan(a, mult * (ig * xv), reverse)
        a_s[...] = al
        b_s[...] = bl
        c_s[...] = _carry_scan(a_s, b_s, h_ref, c_s[...], reverse)

    def body(xf_ref, xb_ref, *refs):
        prm_f, prm_b = refs[0:5], refs[5:10]
        h0f_ref, h0b_ref, hf_ref, hb_ref = refs[10:14]
        af_s, bf_s, cf_s, ab_s, bb_s, cb_s = refs[14:]
        one(xf_ref, prm_f, h0f_ref, hf_ref, af_s, bf_s, cf_s, False)
        one(xb_ref, prm_b, h0b_ref, hb_ref, ab_s, bb_s, cb_s, True)

    vec = _full((1, LRU_W))
    mat = _full((LRU_W, LRU_W))
    fw = pl.BlockSpec((tm, LRU_W), lambda i: (i, 0))
    bw = pl.BlockSpec((tm, LRU_W), lambda i: (nt - 1 - i, 0))
    tile_s = [pltpu.VMEM((tm, LRU_W), F32), pltpu.VMEM((tm, LRU_W), F32), pltpu.VMEM((SUBLANES, LRU_W), F32)]
    (hf, hb), couts = _call(
        body, name=name, grid=(nt,),
        in_specs=[fw, bw] + [mat, mat, vec, vec, vec] * 2 + [vec, vec],
        out_specs=[pl.BlockSpec((tm, LRU_W), lambda i: (i, 0)), pl.BlockSpec((tm, LRU_W), lambda i: (nt - 1 - i, 0))],
        out_shape=[_sds((t, LRU_W))] * 2, scratch_shapes=tile_s + tile_s, sem=("arbitrary",),
        args=(xc, xc, *par_f, *par_b, h0_f, h0_b), comms=comms)
    return ((hf, hb), couts) if comms else (hf, hb)


def _lru_bwd(xc, par_f, par_b, h_f, h_b, h0_f, h0_b, dh_f, dh_b, name, comms=()):
    t = xc.shape[0]
    tm = _tile(t, True)
    nt = t // tm
    n8 = tm // SUBLANES
    last8 = t // SUBLANES - 1
    tile_f = lambda w: pl.BlockSpec((tm, w), lambda i: (nt - 1 - i, 0))
    tile_b = lambda w: pl.BlockSpec((tm, w), lambda i: (i, 0))
    halo_f = pl.BlockSpec((SUBLANES, LRU_W), lambda i: (jnp.maximum((nt - 1 - i) * n8 - 1, 0), 0))
    halo_b = pl.BlockSpec((SUBLANES, LRU_W), lambda i: (jnp.minimum((i + 1) * n8, last8), 0))

    def one(refs_in, refs_out, refs_scr, reverse):
        x_ref, wa_ref, wx_ref, ba_ref, bx_ref, lam_ref, h_ref, halo_ref, h0_ref, dh_ref = refs_in
        dx_ref, dpre_ref, dba_ref, dbx_ref, dlam_ref, dh0_ref = refs_out
        a_s, b_s, l_s, c_s, e_s = refs_scr
        i = pl.program_id(0)

        @pl.when(i == 0)
        def _():
            c_s[...] = jnp.zeros_like(c_s)
            e_s[...] = jnp.zeros_like(e_s)
            dba_ref[...] = jnp.zeros_like(dba_ref)
            dbx_ref[...] = jnp.zeros_like(dbx_ref)
            dlam_ref[...] = jnp.zeros_like(dlam_ref)
        xv = x_ref[...]
        lam = lam_ref[...]
        r, ig, sp, a, mult = _lru_gates(xv, wa_ref, wx_ref, ba_ref[...], bx_ref[...], lam)
        hv = h_ref[...]
        rowi = lax.broadcasted_iota(jnp.int32, (tm, LRU_W), 0)
        edge_a = jnp.broadcast_to(e_s[0:1, :], (tm, LRU_W))
        h0b = jnp.broadcast_to(h0_ref[...], (tm, LRU_W))
        if reverse:
            a_sh = jnp.where(rowi == 0, edge_a, pltpu.roll(a, 1, 0))
            hin_edge = jnp.where(i == nt - 1, h0b, jnp.broadcast_to(halo_ref[0:1, :], (tm, LRU_W)))
            h_in = jnp.where(rowi == tm - 1, hin_edge, pltpu.roll(hv, tm - 1, 0))
        else:
            a_sh = jnp.where(rowi == tm - 1, edge_a, pltpu.roll(a, tm - 1, 0))
            hin_edge = jnp.where(i == nt - 1, h0b, jnp.broadcast_to(halo_ref[SUBLANES - 1:SUBLANES, :], (tm, LRU_W)))
            h_in = jnp.where(rowi == 0, hin_edge, pltpu.roll(hv, 1, 0))
        al, bl = _local_scan(a_sh, dh_ref[...], not reverse)
        a_s[...] = al
        b_s[...] = bl
        c_s[...] = _carry_scan(a_s, b_s, l_s, c_s[...], not reverse)
        e_s[...] = jnp.broadcast_to(a[tm - 1:tm, :] if reverse else a[0:1, :], e_s.shape)
        lmb = l_s[...]
        da = lmb * h_in
        ixc = ig * xv
        dmult = lmb * ixc
        dixc = lmb * mult
        dla = da * a - dmult * (a * a) / mult
        dpr = dla * (-LRU_C * sp) * r * (1.0 - r)
        dpi = dixc * xv * ig * (1.0 - ig)
        dprb = dpr.astype(BF16)
        dpib = dpi.astype(BF16)
        dpre_ref[:, 0:LRU_W] = dprb
        dpre_ref[:, LRU_W:2 * LRU_W] = dpib
        dx_ref[...] = dixc * ig + _dot_nt(dprb, wa_ref[...]) + _dot_nt(dpib, wx_ref[...])
        dba_ref[...] += _sum0(dpr)
        dbx_ref[...] += _sum0(dpi)
        dlam_ref[...] += _sum0(dla * (-LRU_C * r)) * (-_sigmoid(-lam))

        @pl.when(i == nt - 1)
        def _():
            al0 = a * lmb
            dh0_ref[...] = al0[tm - 1:tm, :] if reverse else al0[0:1, :]

    def body(*refs):
        one(refs[0:10], refs[20:26], refs[32:37], False)
        one(refs[10:20], refs[26:32], refs[37:42], True)

    vec = _full((1, LRU_W))
    mat = _full((LRU_W, LRU_W))

    def in_specs(tile, halo):
        return [tile(LRU_W), mat, mat, vec, vec, vec, tile(LRU_W), halo, vec, tile(LRU_W)]

    def out_specs(tile):
        return [tile(LRU_W), tile(2 * LRU_W), vec, vec, vec, vec]

    out_one = [_sds((t, LRU_W)), _sds((t, 2 * LRU_W), BF16)] + [_sds((1, LRU_W))] * 4
    scr_one = [pltpu.VMEM((tm, LRU_W), F32)] * 3 + [pltpu.VMEM((SUBLANES, LRU_W), F32)] * 2
    outs, couts = _call(
        body, name=name, grid=(nt,), in_specs=in_specs(tile_f, halo_f) + in_specs(tile_b, halo_b),
        out_specs=out_specs(tile_f) + out_specs(tile_b), out_shape=out_one + out_one,
        scratch_shapes=scr_one + scr_one, sem=("arbitrary",),
        args=(xc, *par_f, h_f, h_f, h0_f, dh_f, xc, *par_b, h_b, h_b, h0_b, dh_b), comms=comms)
    return (tuple(outs[0:6]), tuple(outs[6:12])), couts


def _decay_tables(lg, reverse):
    ci = lax.broadcasted_iota(jnp.int32, (CHUNK, CHUNK), 0).astype(F32)
    mi = lax.broadcasted_iota(jnp.int32, (CHUNK, CHUNK), 1).astype(F32)
    rel = (mi - ci) if reverse else (ci - mi)
    relc = jnp.maximum(rel, 0.0)
    lg_c = jnp.concatenate([lg] * (CHUNK // LANES), axis=1)
    dm = jnp.where(rel >= 0, jnp.exp(lg_c * relc), 0.0)
    cd = lax.broadcasted_iota(jnp.int32, (CHUNK, DH), 0).astype(F32)
    pq, ps = (CHUNK - cd, cd) if reverse else (cd + 1.0, CHUNK - 1.0 - cd)
    return relc, dm, jnp.exp(lg * pq), jnp.exp(lg * ps), jnp.exp(lg * float(CHUNK)), pq, ps


def _ret_fwd(proj, lgv, s0f, s0b, comms=()):
    t = proj.shape[0]
    n = t // CHUNK

    def one(q, k, v, lg, s_s, hh, o_ref, sp_ref, reverse):
        _, dm, wq, ws, g, _, _ = _decay_tables(lg, reverse)
        vb = v.astype(BF16)
        p = _dot_nt(q.astype(BF16), k.astype(BF16)) * dm
        s = s_s[hh]
        sp_ref[hh, 0] = s
        o_ref[:, DH * hh:DH * (hh + 1)] = _dot(p.astype(BF16), vb) + _dot((q * wq).astype(BF16), s.astype(BF16))
        s_s[hh] = g * s + _dot_tn((k * ws).astype(BF16), vb)

    def body(qf, kf, vf, qb, kb, vb, lg_ref, s0f_ref, s0b_ref, of_ref, ob_ref, spf_ref, spb_ref, sf_s, sb_s):
        @pl.when(pl.program_id(0) == 0)
        def _():
            sf_s[...] = s0f_ref[...]
            sb_s[...] = s0b_ref[...]
        for hh in range(HEADS):
            sl = slice(DH * hh, DH * (hh + 1))
            one(qf[:, sl].astype(F32), kf[:, sl].astype(F32), vf[:, sl], lg_ref[hh, 0:1, :], sf_s, hh, of_ref, spf_ref,
                False)
            one(qb[:, sl].astype(F32), kb[:, sl].astype(F32), vb[:, sl], lg_ref[hh, 1:2, :], sb_s, hh, ob_ref, spb_ref,
                True)

    blk = (CHUNK, RET_W)
    fw = [pl.BlockSpec(blk, lambda i, o=o: (i, o)) for o in range(3)]
    bw = [pl.BlockSpec(blk, lambda i, o=o: (n - 1 - i, o)) for o in range(3)]
    st = _full((HEADS, DH, DH))
    return _call(body, name="ret_fwd", grid=(n,),
                 in_specs=fw + bw + [_full((HEADS, 2, LANES)), st, st],
                 out_specs=[pl.BlockSpec(blk, lambda i: (i, 0)), pl.BlockSpec(blk, lambda i: (n - 1 - i, 0)),
                            pl.BlockSpec((HEADS, 1, DH, DH), lambda i: (0, i, 0, 0)),
                            pl.BlockSpec((HEADS, 1, DH, DH), lambda i: (0, n - 1 - i, 0, 0))],
                 out_shape=[_sds((t, RET_W)), _sds((t, RET_W)), _sds((HEADS, n, DH, DH)), _sds((HEADS, n, DH, DH))],
                 scratch_shapes=[pltpu.VMEM((HEADS, DH, DH), F32), pltpu.VMEM((HEADS, DH, DH), F32)],
                 sem=("arbitrary",), args=(proj, proj, proj, proj, proj, proj, lgv, s0f, s0b), comms=comms)


def _ret_bwd(proj, lgv, sgv, spf, spb, do, comms=()):
    t = proj.shape[0]
    n = t // CHUNK

    def one(q_ref, k_ref, v_ref, lg_ref, s_ref, do_ref, dq_ref, dk_ref, dv_ref, ds_s, acc_s, reverse):
        d = 1 if reverse else 0
        for hh in range(HEADS):
            sl = slice(DH * hh, DH * (hh + 1))
            relc, dm, wq, ws, g, pq, ps = _decay_tables(lg_ref[hh, d:d + 1, :], reverse)
            qb, kb, vb = q_ref[:, sl], k_ref[:, sl], v_ref[:, sl]
            q, k = qb.astype(F32), kb.astype(F32)
            p = _dot_nt(qb, kb) * dm
            s = s_ref[hh, 0]
            dob = do_ref[:, sl].astype(BF16)
            dsn = ds_s[hh]
            dsb = dsn.astype(BF16)
            dv_ref[:, sl] = (_dot_tn(p.astype(BF16), dob) + _dot((k * ws).astype(BF16), dsb)).astype(BF16)
            dp = _dot_nt(dob, vb)
            dab = (dp * dm).astype(BF16)
            xq = _dot_nt(dob, s.astype(BF16))
            yk = _dot_nt(vb, dsb)
            dq_ref[:, sl] = (_dot(dab, kb) + xq * wq).astype(BF16)
            dk_ref[:, sl] = (_dot_tn(dab, qb) + yk * ws).astype(BF16)
            ds_s[hh] = g * dsn + _dot_tn((q * wq).astype(BF16), dob)
            s_mask = _sum0(dp * p * relc)
            part = (sum(s_mask[:, LANES * u:LANES * (u + 1)] for u in range(CHUNK // LANES))
                    + _sum0(xq * q * wq * pq) + _sum0(yk * k * ws * ps) + _sum0(dsn * s) * g * float(CHUNK))
            acc_s[hh] += jnp.broadcast_to(part, (SUBLANES, LANES))

    def body(qf, kf, vf, qb, kb, vb, lg_ref, sg_ref, sf_ref, sb_ref, dof_ref, dob_ref,
             dqf, dkf, dvf, dqb, dkb, dvb, ds0f_ref, ds0b_ref, drdf_ref, drdb_ref, dsf_s, dsb_s, accf_s, accb_s):
        i = pl.program_id(0)

        @pl.when(i == 0)
        def _():
            for r in (dsf_s, dsb_s, accf_s, accb_s):
                r[...] = jnp.zeros_like(r)
        one(qf, kf, vf, lg_ref, sf_ref, dof_ref, dqf, dkf, dvf, dsf_s, accf_s, False)
        one(qb, kb, vb, lg_ref, sb_ref, dob_ref, dqb, dkb, dvb, dsb_s, accb_s, True)

        @pl.when(i == n - 1)
        def _():
            ds0f_ref[...] = dsf_s[...]
            ds0b_ref[...] = dsb_s[...]
            for d, (acc_s, drd_ref) in enumerate(((accf_s, drdf_ref), (accb_s, drdb_ref))):
                for hh in range(HEADS):
                    tot = jnp.sum(acc_s[hh, 0:1, :], axis=1, keepdims=True)
                    drd_ref[hh] = jnp.broadcast_to(tot, (SUBLANES, LANES)) * sg_ref[hh, d:d + 1, :]

    blk = (CHUNK, RET_W)
    fw = lambda o: pl.BlockSpec(blk, lambda i, o=o: (n - 1 - i, o))
    bw = lambda o: pl.BlockSpec(blk, lambda i, o=o: (i, o))
    lane = _full((HEADS, 2, LANES))
    st = _full((HEADS, DH, DH))
    rd = _full((HEADS, SUBLANES, LANES))
    outs, couts = _call(
        body, name="ret_bwd", grid=(n,),
        in_specs=[fw(0), fw(1), fw(2), bw(0), bw(1), bw(2), lane, lane,
                  pl.BlockSpec((HEADS, 1, DH, DH), lambda i: (0, n - 1 - i, 0, 0)),
                  pl.BlockSpec((HEADS, 1, DH, DH), lambda i: (0, i, 0, 0)), fw(0), bw(0)],
        out_specs=[fw(0), fw(0), fw(0), bw(0), bw(0), bw(0), st, st, rd, rd],
        out_shape=[_sds((t, RET_W), BF16)] * 6 + [_sds((HEADS, DH, DH))] * 2 + [_sds((HEADS, SUBLANES, LANES))] * 2,
        scratch_shapes=[pltpu.VMEM((HEADS, DH, DH), F32)] * 2 + [pltpu.VMEM((HEADS, SUBLANES, LANES), F32)] * 2,
        sem=("arbitrary",), args=(proj, proj, proj, proj, proj, proj, lgv, sgv, spf, spb, do, do), comms=comms)
    dqf, dkf, dvf, dqb, dkb, dvb, ds0f, ds0b, drdf, drdb = outs
    return ((dqf, dkf, dvf, ds0f, drdf), (dqb, dkb, dvb, ds0b, drdb)), couts


def _ctx_weights(lg, l_len, reverse):
    pos = lax.broadcasted_iota(jnp.int32, (l_len, DH), 0).astype(F32)
    steps = pos if reverse else (l_len - 1.0 - pos)
    return jnp.exp(lg * steps), steps


def _ctx_state_fwd(projc, lgv):
    l_len = projc.shape[0]

    def body(k_ref, v_ref, lg_ref, sf_ref, sb_ref):
        k = k_ref[...]
        vb = v_ref[...].astype(BF16)
        for d, o_ref in ((0, sf_ref), (1, sb_ref)):
            w, _ = _ctx_weights(lg_ref[0, d:d + 1, :], l_len, d == 1)
            o_ref[0] = _dot_tn((k * w).astype(BF16), vb)

    st = pl.BlockSpec((1, DH, DH), lambda h: (h, 0, 0))
    return _pc(body, name="ctx_state_fwd", grid=(HEADS,),
               in_specs=[pl.BlockSpec((l_len, DH), lambda h: (0, HEADS + h)),
                         pl.BlockSpec((l_len, DH), lambda h: (0, 2 * HEADS + h)),
                         pl.BlockSpec((1, 2, LANES), lambda h: (h, 0, 0))],
               out_specs=[st, st], out_shape=[_sds((HEADS, DH, DH))] * 2,
               compiler_params=_params("arbitrary"))(projc, projc, lgv)


def _ctx_state_bwd(projc, lgv, sgv, dsf, dsb):
    l_len = projc.shape[0]

    def body(k_ref, v_ref, lg_ref, sg_ref, dsf_ref, dsb_ref, dk_ref, dv_ref, drd_ref):
        k = k_ref[...]
        vb = v_ref[...].astype(BF16)
        dk = jnp.zeros((l_len, DH), F32)
        dv = jnp.zeros((l_len, DH), F32)
        rows = []
        for d, ds_ref in ((0, dsf_ref), (1, dsb_ref)):
            w, steps = _ctx_weights(lg_ref[0, d:d + 1, :], l_len, d == 1)
            dsb16 = ds_ref[0].astype(BF16)
            dkw = _dot_nt(vb, dsb16)
            dk = dk + dkw * w
            dv = dv + _dot((k * w).astype(BF16), dsb16)
            tot = jnp.sum(_sum0(dkw * k * w * steps), axis=1, keepdims=True)
            rows.append(jnp.broadcast_to(tot, (1, LANES)) * sg_ref[0, d:d + 1, :])
        dk_ref[...] = dk.astype(BF16)
        dv_ref[...] = dv.astype(BF16)
        rid = lax.broadcasted_iota(jnp.int32, (SUBLANES, LANES), 0)
        drd_ref[0] = jnp.where(rid == 0, rows[0], jnp.where(rid == 1, rows[1], 0.0))

    st = pl.BlockSpec((1, DH, DH), lambda h: (h, 0, 0))
    lane = pl.BlockSpec((1, 2, LANES), lambda h: (h, 0, 0))
    hc = pl.BlockSpec((l_len, DH), lambda h: (0, h))
    return _pc(body, name="ctx_state_bwd", grid=(HEADS,),
               in_specs=[pl.BlockSpec((l_len, DH), lambda h: (0, HEADS + h)),
                         pl.BlockSpec((l_len, DH), lambda h: (0, 2 * HEADS + h)), lane, lane, st, st],
               out_specs=[hc, hc, pl.BlockSpec((1, SUBLANES, LANES), lambda h: (h, 0, 0))],
               out_shape=[_sds((l_len, RET_W), BF16), _sds((l_len, RET_W), BF16), _sds((HEADS, SUBLANES, LANES))],
               compiler_params=_params("arbitrary"))(projc, projc, lgv, sgv, dsf, dsb)


G_BLOCK = (3 * RET_W) // RET_W
GATE_BLOCK = (4 * RET_W + LRU_W) // LRU_W


def _head_norm(y):
    yc = y - jnp.mean(y, axis=-1, keepdims=True)
    rs = lax.rsqrt(jnp.mean(yc * yc, axis=-1, keepdims=True) + EPS)
    return yc * rs, rs


def _gelu_parts(z):
    th = jnp.tanh(GELU_K * (z + GELU_C * z * z * z))
    return 0.5 * z * (1.0 + th), th


def _mix_fwd(o_f, o_b, proj, hf, hb, w_out, x, g1):
    t = x.shape[0]
    tm = _tile(t, True)

    def body(of_ref, ob_ref, g_ref, gt_ref, hf_ref, hb_ref, w_ref, x_ref, g1_ref, x1_ref, cat_ref):
        o = of_ref[...] + ob_ref[...]
        g = g_ref[...].astype(F32)
        for hh in range(HEADS):
            sl = slice(DH * hh, DH * (hh + 1))
            nrm, _ = _head_norm(o[:, sl])
            gh = g[:, sl]
            cat_ref[:, sl] = (gh * _sigmoid(gh) * nrm).astype(BF16)
        gel, _ = _gelu_parts(gt_ref[...].astype(F32))
        cat_ref[:, RET_W:] = ((hf_ref[...] + hb_ref[...]) * gel).astype(BF16)
        x1_ref[...] = x_ref[...] + g1_ref[...] * _dot(cat_ref[...], w_ref[...])

    half = pl.BlockSpec((tm, RET_W), lambda i: (i, 0))
    big = pl.BlockSpec((tm, D_MODEL), lambda i: (i, 0))
    return _pc(body, name="mix_fwd", grid=(t // tm,),
               in_specs=[half, half, pl.BlockSpec((tm, RET_W), lambda i: (i, G_BLOCK)),
                         pl.BlockSpec((tm, LRU_W), lambda i: (i, GATE_BLOCK)), half, half,
                         _full((D_MODEL, D_MODEL)), big, _full((1, D_MODEL))],
               out_specs=[big, big], out_shape=[_sds((t, D_MODEL)), _sds((t, D_MODEL), BF16)],
               compiler_params=_params("arbitrary"))(o_f, o_b, proj, proj, hf, hb, w_out, x, g1)


def _mix_bwd(o_f, o_b, proj, hf, hb, w_out, cat, dx1, g1, comms=()):
    t = dx1.shape[0]
    tm = _tile(t, True)

    def body(of_ref, ob_ref, g_ref, gt_ref, hf_ref, hb_ref, w_ref, cat_ref, dx1_ref, g1_ref,
             do_ref, dhs_ref, dg_ref, dgt_ref, dyb_ref, dg1_ref):
        dx1v = dx1_ref[...]
        y = _dot(cat_ref[...], w_ref[...])

        @pl.when(pl.program_id(0) == 0)
        def _():
            dg1_ref[...] = jnp.zeros_like(dg1_ref)
        dg1_ref[...] += _sum0(dx1v * y)
        dyb = (g1_ref[...] * dx1v).astype(BF16)
        dyb_ref[...] = dyb
        dcat = _dot_nt(dyb, w_ref[...])
        o = of_ref[...] + ob_ref[...]
        g = g_ref[...].astype(F32)
        for hh in range(HEADS):
            sl = slice(DH * hh, DH * (hh + 1))
            nrm, rs = _head_norm(o[:, sl])
            gh = g[:, sl]
            sg = _sigmoid(gh)
            dret = dcat[:, sl]
            dg_ref[:, sl] = (dret * nrm * (sg * (1.0 + gh * (1.0 - sg)))).astype(BF16)
            dn = dret * (gh * sg)
            dyc = rs * (dn - nrm * jnp.mean(dn * nrm, axis=-1, keepdims=True))
            do_ref[:, sl] = (dyc - jnp.mean(dyc, axis=-1, keepdims=True)).astype(BF16)
        z = gt_ref[...].astype(F32)
        gel, th = _gelu_parts(z)
        dlru = dcat[:, RET_W:]
        dhs_ref[...] = dlru * gel
        dgel = 0.5 * (1.0 + th) + 0.5 * z * (1.0 - th * th) * GELU_K * (1.0 + 3.0 * GELU_C * z * z)
        dgt_ref[...] = (dlru * (hf_ref[...] + hb_ref[...]) * dgel).astype(BF16)

    half = pl.BlockSpec((tm, RET_W), lambda i: (i, 0))
    big = pl.BlockSpec((tm, D_MODEL), lambda i: (i, 0))
    return _call(body, name="mix_bwd", grid=(t // tm,),
                 in_specs=[half, half, pl.BlockSpec((tm, RET_W), lambda i: (i, G_BLOCK)),
                           pl.BlockSpec((tm, LRU_W), lambda i: (i, GATE_BLOCK)), half, half,
                           _full((D_MODEL, D_MODEL)), big, big, _full((1, D_MODEL))],
                 out_specs=[half, half, half, half, big, _full((1, D_MODEL))],
                 out_shape=[_sds((t, RET_W), BF16), _sds((t, RET_W)), _sds((t, RET_W), BF16), _sds((t, RET_W), BF16),
                            _sds((t, D_MODEL), BF16), _sds((1, D_MODEL))],
                 scratch_shapes=[], sem=("arbitrary",), args=(o_f, o_b, proj, proj, hf, hb, w_out, cat, dx1, g1),
                 comms=comms)


def _mlp(x1, n2g, sh2, sc2, g2, fg, w1_parts, w2_parts, tgt):
    t = x1.shape[0]
    tm = _tile(t)
    hb_ = MLP_H // N_CHIP
    q_rows = hb_ // 4
    n_cp = 4 * N_DEV

    def body(x1_ref, n2g_ref, sh2_ref, sc2_ref, g2_ref, fg_ref, w1a, w1b, w2a, w2b, tgt_ref,
             dx1_ref, h2b_ref, ab_ref, dub_ref, dmb_ref, dsc_ref, dsh_ref, dg2_ref, dn2_ref, dfg_ref, loss_ref,
             w1_s, w2_s, r_s, sems):
        @pl.when(pl.program_id(0) == 0)
        def _():
            cps = []
            for p, parts in enumerate(((w1a, w2a), (w1b, w2b))):
                for d in range(N_DEV):
                    rows = pl.ds(2 * q_rows * (d % 2) + q_rows * p, q_rows)
                    for src, dst in zip(parts, (w1_s, w2_s)):
                        cps.append(pltpu.make_async_copy(src.at[d], dst.at[d // 2, rows], sems.at[len(cps)]))
            for cp in cps:
                cp.start()
            for r in (dsc_ref, dsh_ref, dg2_ref, dn2_ref, dfg_ref, loss_ref):
                r[...] = jnp.zeros_like(r)
            for cp in cps:
                cp.wait()
        x1v = x1_ref[...]
        n2g, sc2, g2, fg = n2g_ref[...], sc2_ref[...], g2_ref[...], fg_ref[...]
        xh, _ = _rms(x1v)
        h2b = (xh * n2g * (1.0 + sc2) + sh2_ref[...]).astype(BF16)
        h2b_ref[...] = h2b
        m = jnp.zeros((tm, D_MODEL), F32)
        for j in range(N_CHIP):
            sl = slice(hb_ * j, hb_ * (j + 1))
            r = jnp.maximum(_dot(h2b, w1_s[j]), 0.0)
            r_s[:, sl] = r
            ab = (r * r).astype(BF16)
            ab_ref[:, sl] = ab
            m = m + _dot(ab, w2_s[j])
        x2 = x1v + g2 * m
        x2h, r2 = _rms(x2)
        err = x2h * fg - tgt_ref[...]
        loss_ref[...] += _sum0(err * err)
        dout = err * (1.0 / D_MODEL)
        dfg_ref[...] += _sum0(dout * x2h)
        dxh = dout * fg
        dx2 = r2 * (dxh - x2h * jnp.mean(dxh * x2h, axis=-1, keepdims=True))
        dg2_ref[...] += _sum0(dx2 * m)
        dmb = (g2 * dx2).astype(BF16)
        dmb_ref[...] = dmb
        dh2 = jnp.zeros((tm, D_MODEL), F32)
        for j in range(N_CHIP):
            sl = slice(hb_ * j, hb_ * (j + 1))
            dub = (_dot_nt(dmb, w2_s[j]) * (2.0 * r_s[:, sl])).astype(BF16)
            dub_ref[:, sl] = dub
            dh2 = dh2 + _dot_nt(dub, w1_s[j])
        dx, dn2_t, dsh_t, dsc_t = _norm_mod_bwd(x1v, n2g, sc2, dh2)
        dx1_ref[...] = dx2 + dx
        dn2_ref[...] += dn2_t
        dsh_ref[...] += dsh_t
        dsc_ref[...] += dsc_t

        @pl.when(pl.program_id(0) == t // tm - 1)
        def _():
            tot = jnp.sum(loss_ref[...], axis=1, keepdims=True) * (0.5 / D_MODEL)
            loss_ref[...] = jnp.broadcast_to(tot, loss_ref.shape)

    row = _full((1, D_MODEL))
    big = pl.BlockSpec((tm, D_MODEL), lambda i: (i, 0))
    wide = pl.BlockSpec((tm, MLP_H), lambda i: (i, 0))
    return _pc(body, name="mlp", grid=(t // tm,),
               in_specs=[big, row, row, row, row, row, ANY, ANY, ANY, ANY, big],
               out_specs=[big, big, wide, wide, big, row, row, row, row, row, row],
               out_shape=[_sds((t, D_MODEL)), _sds((t, D_MODEL), BF16), _sds((t, MLP_H), BF16), _sds((t, MLP_H), BF16),
                          _sds((t, D_MODEL), BF16)] + [_sds((1, D_MODEL))] * 6,
               scratch_shapes=[pltpu.VMEM((N_CHIP, D_MODEL, hb_), BF16), pltpu.VMEM((N_CHIP, hb_, D_MODEL), BF16),
                               pltpu.VMEM((tm, MLP_H), F32), pltpu.SemaphoreType.DMA((n_cp,))],
               compiler_params=_params("arbitrary"))(x1, n2g, sh2, sc2, g2, fg, *w1_parts, *w2_parts, tgt)


def _tn(a, b, nj, a_blocked, b_blocked, name, extra=None, comms=()):
    t = a.shape[0]
    m = a.shape[1] // (nj if a_blocked else 1)
    n = b.shape[1] // (nj if b_blocked else 1)
    bk = next((b for b in (2048, 1024, 512) if t % b == 0), t)
    nk = t // bk
    a_col = (lambda j: j) if a_blocked else (lambda j: 0)
    b_col = (lambda j: j) if b_blocked else (lambda j: 0)
    in_specs = [pl.BlockSpec((bk, m), lambda j, k: (k, a_col(j))), pl.BlockSpec((bk, n), lambda j, k: (k, b_col(j)))]
    args = [a, b]
    if extra is not None:
        a2, b2 = extra
        t2 = a2.shape[0]
        in_specs += [pl.BlockSpec((t2, m), lambda j, k: (0, a_col(j))),
                     pl.BlockSpec((t2, n), lambda j, k: (0, b_col(j)))]
        args += [a2, b2]

    def body(*refs):
        a_ref, b_ref = refs[0], refs[1]
        o_ref, acc = refs[-2], refs[-1]
        k = pl.program_id(1)

        @pl.when(k == 0)
        def _():
            acc[...] = jnp.zeros_like(acc)
        acc[...] += _dot_tn(a_ref[...].astype(BF16), b_ref[...].astype(BF16))

        @pl.when(k == nk - 1)
        def _():
            if extra is not None:
                acc[...] += _dot_tn(refs[2][...].astype(BF16), refs[3][...].astype(BF16))
            o_ref[0] = acc[...]

    (out,), couts = _call(body, name=name, grid=(nj, nk), in_specs=in_specs,
                          out_specs=[pl.BlockSpec((1, m, n), lambda j, k: (j, 0, 0))], out_shape=[_sds((nj, m, n))],
                          scratch_shapes=[pltpu.VMEM((m, n), F32)], sem=("arbitrary", "arbitrary"), args=args,
                          comms=comms)
    return (out, couts) if comms else out


ROW_LOSS = 0
ROW_DMOD = 1
ROW_DMODC = 7
ROW_N1, ROW_N2, ROW_FG, ROW_CB = 9, 10, 11, 12
ROW_BA, ROW_BX, ROW_LAM = 13, 15, 17
ROW_CW = 20
ROW_RD = 24
SLAB_ROWS = 32
SEG = D_MODEL // 2


def _pack_small(rows, drd, cw2, cb2, lru2, gates):
    n_rows, n_lru = len(rows), len(lru2)

    def body(*refs):
        r = refs[:n_rows]
        drd_f, drd_b, drd_c, cw_a, cw_b, cb_a, cb_b = refs[n_rows:n_rows + 7]
        lru = refs[n_rows + 7:n_rows + 7 + n_lru]
        gf_ref, gb_ref, slab, ga, gx = refs[n_rows + 7 + n_lru:]
        slab[...] = jnp.zeros_like(slab)
        slab[ROW_LOSS:ROW_LOSS + 1, :] = r[0][...]
        for k in range(N_MOD):
            slab[ROW_DMOD + k:ROW_DMOD + k + 1, :] = r[1 + k][...]
        slab[ROW_DMODC:ROW_DMODC + 1, :] = r[7][...]
        slab[ROW_DMODC + 1:ROW_DMODC + 2, :] = r[8][...]
        slab[ROW_N1:ROW_N1 + 1, :] = r[9][...] + r[10][...]
        slab[ROW_N2:ROW_N2 + 1, :] = r[11][...]
        slab[ROW_FG:ROW_FG + 1, :] = r[12][...]
        slab[ROW_CB:ROW_CB + 1, 0:LRU_W] = cb_a[...] + cb_b[...]
        for k, row in enumerate((ROW_BA, ROW_BA + 1, ROW_BX, ROW_BX + 1, ROW_LAM, ROW_LAM + 1)):
            slab[row:row + 1, 0:LRU_W] = lru[2 * k][...] + lru[2 * k + 1][...]
        slab[ROW_CW:ROW_CW + 4, 0:LRU_W] = cw_a[...] + cw_b[...]
        for h in range(HEADS):
            slab[ROW_RD + h:ROW_RD + h + 1, 0:LANES] = drd_f[h, 0:1, :] + drd_c[h, 0:1, :]
            slab[ROW_RD + HEADS + h:ROW_RD + HEADS + h + 1, 0:LANES] = drd_b[h, 0:1, :] + drd_c[h, 1:2, :]
        for d, g_ref in enumerate((gf_ref, gb_ref)):
            for n in range(LRU_BLOCKS):
                blk = slice(LRU_BD * n, LRU_BD * (n + 1))
                ga[blk, LRU_BD * d:LRU_BD * (d + 1)] = g_ref[0, blk, blk].astype(BF16)
                gx[blk, LRU_BD * d:LRU_BD * (d + 1)] = g_ref[1, blk, blk].astype(BF16)

    args = list(rows) + list(drd) + list(cw2) + list(cb2) + list(lru2) + list(gates)
    gate_shape = (LRU_W, 2 * LRU_BD)
    return _pc(body, name="pack_small", in_specs=[_full(a.shape) for a in args],
               out_specs=[_full((SLAB_ROWS, D_MODEL)), _full(gate_shape), _full(gate_shape)],
               out_shape=[_sds((SLAB_ROWS, D_MODEL)), _sds(gate_shape, BF16), _sds(gate_shape, BF16)],
               compiler_params=_params())(*args)


def _adam_math(w, g, m, v):
    mn = ADAM_B1 * m + (1.0 - ADAM_B1) * g
    vn = ADAM_B2 * v + (1.0 - ADAM_B2) * (g * g)
    mh = mn / (1.0 - ADAM_B1 ** ADAM_STEP)
    vh = vn / (1.0 - ADAM_B2 ** ADAM_STEP)
    return -ADAM_LR * (mh / (jnp.sqrt(vh) + ADAM_EPS) + ADAM_WD * w), mn, vn


SMALL_PARAMS = ("b_ada", "norm1_g", "norm2_g", "final_g", "ret_decay", "conv_w", "conv_b", "lru_wa", "lru_ba", "lru_wx",
                "lru_bx", "lru_lambda")


def _finalize_small(chip_idx, slab_all, ga_all, gx_all, wmv):
    n_p = len(SMALL_PARAMS)
    flat = [a for nm in SMALL_PARAMS for a in wmv[nm]]
    ada_n = N_MOD * D_MODEL // N_CHIP

    def body(c_ref, slab_ref, ga_ref, gx_ref, *refs):
        prm = {nm: refs[3 * k:3 * k + 3] for k, nm in enumerate(SMALL_PARAMS)}
        outs = {nm: refs[3 * n_p + 4 * k:3 * n_p + 4 * k + 4] for k, nm in enumerate(SMALL_PARAMS)}
        b128_ref, dmc_ref, loss_ref = refs[3 * n_p + 4 * n_p:]
        chip = c_ref[0]

        def pick(fn):
            acc = fn(0)
            for j in range(1, N_CHIP):
                acc = jnp.where(chip == j, fn(j), acc)
            return acc

        tot = slab_ref[0]
        for d in range(1, N_DEV):
            tot = tot + slab_ref[d]

        def update(nm, g, sl=None, rows=None):
            w_ref, m_ref, v_ref = prm[nm]
            g_ref, d_ref, mo_ref, vo_ref = outs[nm]
            ix = (slice(None) if rows is None else rows, slice(None) if sl is None else sl)
            dl, mn, vn = _adam_math(w_ref[ix], g, m_ref[ix], v_ref[ix])
            g_ref[ix] = g
            d_ref[ix] = dl
            mo_ref[ix] = mn
            vo_ref[ix] = vn

        loss_ref[...] = jnp.broadcast_to(tot[ROW_LOSS:ROW_LOSS + 1, 0:LANES], (SUBLANES, LANES))
        for k in range(N_MOD):
            g = tot[ROW_DMOD + k:ROW_DMOD + k + 1, :]
            if k < 2:
                g = g + tot[ROW_DMODC + k:ROW_DMODC + k + 1, :]
            update("b_ada", g, slice(D_MODEL * k, D_MODEL * (k + 1)))
        update("norm1_g", tot[ROW_N1:ROW_N1 + 1, :])
        update("norm2_g", tot[ROW_N2:ROW_N2 + 1, :])
        update("final_g", tot[ROW_FG:ROW_FG + 1, :])
        update("ret_decay", tot[ROW_RD:ROW_RD + SUBLANES, 0:LANES])
        update("conv_b", tot[ROW_CB:ROW_CB + 1, 0:LRU_W])
        update("conv_w", pick(lambda j: tot[ROW_CW:ROW_CW + 4, LANES * j:LANES * (j + 1)]))
        for nm, row in (("lru_ba", ROW_BA), ("lru_bx", ROW_BX), ("lru_lambda", ROW_LAM)):
            update(nm, pick(lambda j, row=row: tot[row:row + 2, LANES * j:LANES * (j + 1)]))
        for nm, g_all in (("lru_wa", ga_ref), ("lru_wx", gx_ref)):
            for dr in range(2):
                lanes = slice(LRU_BD * dr, LRU_BD * (dr + 1))
                g = g_all[0, :, lanes].astype(F32)
                for d in range(1, N_DEV):
                    g = g + g_all[d, :, lanes].astype(F32)
                update(nm, g, rows=slice(LRU_W * dr, LRU_W * (dr + 1)))

        def seg(rows6, s):
            return rows6[s // 2][:, SEG * (s % 2):SEG * (s % 2 + 1)]

        b128_ref[...] = jnp.zeros_like(b128_ref)
        dmc_ref[...] = jnp.zeros_like(dmc_ref)
        zero = jnp.zeros((1, D_MODEL), F32)
        ctx6 = [tot[ROW_DMODC:ROW_DMODC + 1, :], tot[ROW_DMODC + 1:ROW_DMODC + 2, :]] + [zero] * (N_MOD - 2)
        for q in range(ada_n // SEG):
            cols = slice(SEG * q, SEG * (q + 1))
            for d in range(N_DEV):
                rows6 = [slab_ref[d, ROW_DMOD + k:ROW_DMOD + k + 1, :] for k in range(N_MOD)]
                b128_ref[d:d + 1, cols] = pick(lambda j, rows6=rows6: seg(rows6, 3 * j + q))
            c = pick(lambda j: seg(ctx6, 3 * j + q))
            b128_ref[N_DEV:N_DEV + 1, cols] = c
            dmc_ref[0:1, cols] = c

    out_shape = []
    for nm in SMALL_PARAMS:
        out_shape += [_sds(wmv[nm][0].shape)] * 4
    out_shape += [_sds((LANES, ada_n)), _sds((SUBLANES, ada_n)), _sds((SUBLANES, LANES))]
    args = [slab_all, ga_all, gx_all] + flat
    grid_spec = pltpu.PrefetchScalarGridSpec(
        num_scalar_prefetch=1, grid=(1,), in_specs=[_full(a.shape) for a in args],
        out_specs=[_full(s.shape) for s in out_shape])
    outs = _pc(body, name="finalize_small", grid_spec=grid_spec, out_shape=out_shape,
               compiler_params=_params("arbitrary"))(chip_idx, *args)
    res = {nm: tuple(outs[4 * k:4 * k + 4]) for k, nm in enumerate(SMALL_PARAMS)}
    return res, outs[4 * n_p], outs[4 * n_p + 1], outs[4 * n_p + 2]


def _block_diag(w):
    eye = jnp.eye(LRU_BLOCKS, dtype=F32)
    return (w[:, :, None, :] * eye[:, None, :, None]).reshape(LRU_W, LRU_W).astype(BF16)


def _lane_rep(v8):
    return jnp.broadcast_to(v8.reshape(SUBLANES, 1), (SUBLANES, LANES))


def kernel(x, c, ctx, c_ctx, w_ada, b_ada, norm1_g, norm2_g, w_in, ret_decay, conv_w, conv_b, lru_wa, lru_ba, lru_wx, lru_bx, lru_lambda, w_out, w_mlp1, w_mlp2, final_g, loss_target, m_c_ctx, m_w_ada, m_b_ada, m_norm1_g, m_norm2_g, m_w_in, m_ret_decay, m_conv_w, m_conv_b, m_lru_wa, m_lru_ba, m_lru_wx, m_lru_bx, m_lru_lambda, m_w_out, m_w_mlp1, m_w_mlp2, m_final_g, v_c_ctx, v_w_ada, v_b_ada, v_norm1_g, v_norm2_g, v_w_in, v_ret_decay, v_conv_w, v_conv_b, v_lru_wa, v_lru_ba, v_lru_wx, v_lru_bx, v_lru_lambda, v_w_out, v_w_mlp1, v_w_mlp2, v_final_g):
    ax, ay, ac = lax.axis_index("x"), lax.axis_index("y"), lax.axis_index("c")
    chip = 2 * ax + ay
    dev = 4 * ax + 2 * ay + ac
    c_idx = jnp.stack([ac, chip]).astype(jnp.int32)
    j_idx = chip.reshape(1).astype(jnp.int32)

    xt = x[0]
    t_len = xt.shape[0]
    ctxt = ctx[0]
    l_len = ctxt.shape[0]
    tgt = loss_target[0]
    ada_n = w_ada.shape[2]

    def my_half(w2d):
        r = w2d.shape[0] // 2
        return lax.dynamic_slice_in_dim(w2d, ac * r, r, axis=0).astype(BF16)

    pad8 = lambda a: jnp.pad(a, ((0, SUBLANES - a.shape[0]), (0, 0)))
    small = jnp.concatenate([pad8(conv_w[0]), pad8(lru_ba[0]), pad8(lru_bx[0]), pad8(lru_lambda[0])], axis=0)
    b_shard = lax.dynamic_slice_in_dim(b_ada, chip * ada_n, ada_n, axis=1)
    gw_in, _, small_all, a16, mod_parts, lgv, sgv = _head(
        my_half(w_in[0]), pad8(c), small, w_ada[0], b_shard, c_ctx, ret_decay[0])
    w4 = gw_in.reshape(N_CHIP, D_MODEL, IN_COLS // N_CHIP)

    mod_all = mod_parts[0::2].transpose(1, 0, 2).reshape(16, N_CHIP * ada_n)
    mod_me = lax.dynamic_slice_in_dim(mod_all, dev, 1, axis=0)
    sh1, sc1, g1, sh2, sc2, g2 = [mod_me[:, D_MODEL * k:D_MODEL * (k + 1)] for k in range(N_MOD)]
    csh1, csc1 = mod_all[8:9, 0:D_MODEL], mod_all[8:9, D_MODEL:2 * D_MODEL]

    cos2, sin2 = _rotary_tables(t_len)
    cos_c, sin_c = jnp.ones((l_len, DH), F32), jnp.zeros((l_len, DH), F32)
    n1g, n2g = norm1_g, norm2_g
    fg = final_g.reshape(1, D_MODEL)

    small_full = small_all[0::2].transpose(1, 0, 2).reshape(4 * SUBLANES, LRU_W)
    cw = small_full[0:4]
    cb = conv_b
    ba_f, ba_b = small_full[8:9], small_full[9:10]
    bx_f, bx_b = small_full[16:17], small_full[17:18]
    lam_f, lam_b = small_full[24:25], small_full[25:26]
    wa_f, wa_b = _block_diag(lru_wa[0, 0]), _block_diag(lru_wa[0, 1])
    wx_f, wx_b = _block_diag(lru_wx[0, 0]), _block_diag(lru_wx[0, 1])
    zero_h = jnp.zeros((1, LRU_W), F32)

    projc, xrc, hcb16 = _inproj_fwd(ctxt, n1g, csh1, csc1, w4, cos_c, sin_c, "inproj_fwd_ctx")
    s_f, s_b = _ctx_state_fwd(projc, lgv)
    xcc = _conv_fwd(xrc, cw, cb, "conv_fwd_ctx")
    par_f, par_b = (wa_f, wx_f, ba_f, bx_f, lam_f), (wa_b, wx_b, ba_b, bx_b, lam_b)
    hcf, hcbk = _lru_fwd(xcc, par_f, par_b, zero_h, zero_h, "lru_fwd_ctx")
    lru_sf, lru_sb = hcf[l_len - 1:l_len], hcbk[0:1]

    h1, h2 = my_half(w_mlp1[0]), my_half(w_mlp2[0])
    q = h1.shape[0] // 2
    (proj, xrl, hb16), ((gw_1a,),) = _inproj_fwd(xt, n1g, sh1, sc1, w4, cos2, sin2, "inproj_fwd",
                                           comms=(_AllGather([h1[:q]]),))
    (o_f, o_b, spf, spb), ((gw_1b, gw_out),) = _ret_fwd(proj, lgv, s_f, s_b,
                                                       comms=(_AllGather([h1[q:], my_half(w_out[0])]),))
    xcl = _conv_fwd(xrl, cw, cb, "conv_fwd")
    (hf, hbk), ((gw_2a, gw_2b),) = _lru_fwd(xcl, par_f, par_b, lru_sf, lru_sb, "lru_fwd",
                                           comms=(_AllGather([h2[:q], h2[q:]]),))
    wo = gw_out.reshape(D_MODEL, D_MODEL)
    x1, cat = _mix_fwd(o_f, o_b, proj, hf, hbk, wo, xt, g1)

    (dx1, h2b, ab, dub, dmb, dsc2, dsh2, dg2, dn2g, dfg, lossv) = _mlp(
        x1, n2g, sh2, sc2, g2, fg, (gw_1a, gw_1b), (gw_2a, gw_2b), tgt)
    gw_mlp1 = _tn(h2b, dub, N_CHIP, False, True, "grad_w_mlp1")
    b_1 = gw_mlp1.reshape(N_DEV, D_MODEL // 2, MLP_H // N_CHIP)
    gw_mlp2, ((r_1,),) = _tn(ab, dmb, N_CHIP, True, False, "grad_w_mlp2", comms=(_pair_exchange([b_1]),))

    half = D_MODEL // 4
    top, bot = (0, half), (half, half)
    b_2 = gw_mlp2.reshape(N_DEV, MLP_H // N_DEV, D_MODEL)
    p_1, pb_1 = _pair_add(b_1, r_1, c_idx, "rs_pair_add_w_mlp1")
    (do, dhs, dg, dgate, dyb, dg1), ((q_1a,), (r_2,)) = _mix_bwd(
        o_f, o_b, proj, hf, hbk, wo, cat, dx1, g1, comms=(_chip_exchange([pb_1], top), _pair_exchange([b_2])))
    gw_o = _tn(cat, dyb, 1, False, False, "grad_w_out")
    b_o = gw_o.reshape(N_DEV, D_MODEL // N_DEV, D_MODEL)
    p_2, pb_2 = _pair_add(b_2, r_2, c_idx, "rs_pair_add_w_mlp2")

    ((dq_f, dk_f, dv_f, ds_f, drd_f), (dq_b, dk_b, dv_b, ds_b, drd_b)), ((q_1b,), (q_2a,), (r_o,)) = _ret_bwd(
        proj, lgv, sgv, spf, spb, do,
        comms=(_chip_exchange([pb_1], bot), _chip_exchange([pb_2], top), _pair_exchange([b_o])))
    p_o, pb_o = _pair_add(b_o, r_o, c_idx, "rs_pair_add_w_out")
    h_1 = _chip_add(p_1, (q_1a, q_1b), c_idx, "rs_chip_add_w_mlp1")

    ((dxc_f, dpre_f, dba_f, dbx_f, dlam_f, dh0_f), (dxc_b, dpre_b, dba_b, dbx_b, dlam_b, dh0_b)), (
        (q_2b,), (q_o,), (f_1,)) = _lru_bwd(
        xcl, par_f, par_b, hf, hbk, lru_sf, lru_sb, dhs, dhs, "lru_bwd",
        comms=(_chip_exchange([pb_2], bot), _chip_exchange([pb_o]), _pair_gather([h_1])))
    h_2 = _chip_add(p_2, (q_2a, q_2b), c_idx, "rs_chip_add_w_mlp2")
    h_o = _chip_add(p_o, (q_o,), c_idx, "rs_chip_add_w_out")
    dxr, dcw, dcb = _conv_bwd(dxc_f, dxc_b, xrl, cw, "conv_bwd")
    grad_x, dpb, dn1g, dsh1, dsc1 = _inproj_bwd(
        xt, n1g, sh1, sc1, w4, cos2, sin2, [dq_f, dq_b, dk_f, dk_b, dv_f, dv_b, dg, dxr, dgate], dx1, "inproj_bwd")

    dkc, dvc, drd_c = _ctx_state_bwd(projc, lgv, sgv, ds_f, ds_b)
    zc = jnp.zeros((l_len, LRU_W), F32)
    dhc_f = lax.dynamic_update_slice(zc, dh0_f, (l_len - 1, 0))
    dhc_b = lax.dynamic_update_slice(zc, dh0_b, (0, 0))
    ((dxcc_f, dprec_f, dbac_f, dbxc_f, dlamc_f, _), (dxcc_b, dprec_b, dbac_b, dbxc_b, dlamc_b, _)), _ = _lru_bwd(
        xcc, par_f, par_b, hcf, hcbk, zero_h, zero_h, dhc_f, dhc_b, "lru_bwd_ctx")
    dxrc, dcw_c, dcb_c = _conv_bwd(dxcc_f, dxcc_b, xrc, cw, "conv_bwd_ctx")
    zr = jnp.zeros((l_len, RET_W), BF16)
    _, dpbc, dn1g_c, dcsh1, dcsc1 = _inproj_bwd(
        ctxt, n1g, csh1, csc1, w4, cos_c, sin_c, [zr, zr, dkc, zr, dvc, zr, zr, dxrc, zr],
        jnp.zeros((l_len, D_MODEL), F32), "inproj_bwd_ctx")

    gw_i = _tn(hb16, dpb, N_CHIP, False, True, "grad_w_in", extra=(hcb16, dpbc))
    b_i = gw_i.reshape(N_DEV, D_MODEL // 2, IN_COLS // N_CHIP)
    gwa_f, ((r_i,), (f_2,), (f_o,)) = _tn(xcl, dpre_f, 2, False, True, "grad_lru_gates_f", extra=(xcc, dprec_f),
                                          comms=(_pair_exchange([b_i]), _pair_gather([h_2]), _pair_gather([h_o])))
    p_i, pb_i = _pair_add(b_i, r_i, c_idx, "rs_pair_add_w_in")
    gwa_b, ((q_i,),) = _tn(xcl, dpre_b, 2, False, True, "grad_lru_gates_b", extra=(xcc, dprec_b),
                           comms=(_chip_exchange([pb_i]),))
    slab, ga, gx = _pack_small(
        [lossv, dsh1, dsc1, dg1, dsh2, dsc2, dg2, dcsh1, dcsc1, dn1g, dn1g_c, dn2g, dfg],
        (drd_f, drd_b, drd_c), (dcw, dcw_c), (dcb, dcb_c),
        (dba_f, dbac_f, dba_b, dbac_b, dbx_f, dbxc_f, dbx_b, dbxc_b, dlam_f, dlamc_f, dlam_b, dlamc_b),
        (gwa_f, gwa_b))
    (f_i,), (slab_all, ga_all, gx_all) = _run_comms(
        [_pair_gather([_chip_add(p_i, (q_i,), c_idx, "rs_chip_add_w_in")]), _AllGather([slab, ga, gx])],
        "tail_exchanges")
    g_in, g_out, g_1, g_2 = _shard_of(f_i), _shard_of(f_o), _shard_of(f_1), _shard_of(f_2)
    big = {}
    for nm, w, g, m, v in (("w_in", w_in, g_in, m_w_in, v_w_in), ("w_out", w_out, g_out, m_w_out, v_w_out),
                           ("w_mlp1", w_mlp1, g_1, m_w_mlp1, v_w_mlp1), ("w_mlp2", w_mlp2, g_2, m_w_mlp2, v_w_mlp2)):
        go, d_, mn, vn = _adamw(w[0], g, m[0], v[0], "adamw_" + nm)
        big[nm] = (go[None], d_[None], mn[None], vn[None])
    params = {
        "b_ada": (b_ada, m_b_ada, v_b_ada), "norm1_g": (norm1_g, m_norm1_g, v_norm1_g),
        "norm2_g": (norm2_g, m_norm2_g, v_norm2_g), "final_g": (final_g, m_final_g, v_final_g),
        "ret_decay": (ret_decay, m_ret_decay, v_ret_decay), "conv_w": (conv_w, m_conv_w, v_conv_w),
        "conv_b": (conv_b, m_conv_b, v_conv_b), "lru_wa": (lru_wa, m_lru_wa, v_lru_wa),
        "lru_ba": (lru_ba, m_lru_ba, v_lru_ba), "lru_wx": (lru_wx, m_lru_wx, v_lru_wx),
        "lru_bx": (lru_bx, m_lru_bx, v_lru_bx), "lru_lambda": (lru_lambda, m_lru_lambda, v_lru_lambda),
    }
    as2d = {
        "b_ada": lambda a: a, "norm1_g": lambda a: a, "norm2_g": lambda a: a, "conv_b": lambda a: a,
        "final_g": lambda a: a.reshape(1, D_MODEL), "ret_decay": lambda a: _lane_rep(a.reshape(-1)),
        "conv_w": lambda a: a[0], "lru_ba": lambda a: a[0], "lru_bx": lambda a: a[0], "lru_lambda": lambda a: a[0],
        "lru_wa": lambda a: a.reshape(2 * LRU_W, LRU_BD), "lru_wx": lambda a: a.reshape(2 * LRU_W, LRU_BD),
    }
    res, b128, dmc8, loss8 = _finalize_small(
        j_idx, slab_all, ga_all, gx_all, {nm: tuple(as2d[nm](a) for a in params[nm]) for nm in SMALL_PARAMS})
    loss = loss8[0, 0]
    small_out = {}
    for nm in SMALL_PARAMS:
        shp = params[nm][0].shape
        if nm == "ret_decay":
            small_out[nm] = tuple(o[:, 0].reshape(shp) for o in res[nm])
        else:
            small_out[nm] = tuple(o.reshape(shp) for o in res[nm])

    g_ada = _ada_grad(jnp.pad(a16.T, ((0, 0), (0, LANES - 16))), b128)
    g_ada, d_ada, m_ada, v_ada = _adamw(w_ada[0], g_ada, m_w_ada[0], v_w_ada[0], "adamw_w_ada")

    (cparts,) = _all_gather([_cctx_partial(dmc8, w_ada[0])], "gather_cctx")
    g_cc, d_cc, m_cc, v_cc = _cctx_final(cparts, c_ctx, m_c_ctx, v_c_ctx)
    small_out["c_ctx"] = tuple(a.reshape(D_MODEL) for a in (g_cc, d_cc, m_cc, v_cc))
    small_out["w_ada"] = (g_ada[None], d_ada[None], m_ada[None], v_ada[None])
    small_out.update(big)

    order = ["c_ctx", "w_ada", "b_ada", "norm1_g", "norm2_g", "w_in", "ret_decay", "conv_w", "conv_b", "lru_wa", "lru_ba",
             "lru_wx", "lru_bx", "lru_lambda", "w_out", "w_mlp1", "w_mlp2", "final_g"]
    outs = [loss, grad_x[None]]
    for k in range(4):
        outs += [small_out[nm][k] for nm in order]
    return tuple(outs)
```

```python
import math

import jax
import jax.numpy as jnp
from jax import lax
from jax.experimental import pallas as pl
from jax.experimental.pallas import tpu as pltpu

F32 = jnp.float32
BF16 = jnp.bfloat16

D_MODEL = 1024
HEADS = 4
DH = 128
CHUNK = 256
RET_W = HEADS * DH
LRU_W = 512
LRU_BLOCKS = 8
LRU_BD = LRU_W // LRU_BLOCKS
LRU_C = 8.0
IN_COLS = 4 * RET_W + 2 * LRU_W
MLP_H = 4 * D_MODEL
N_MOD = 6
GRID_W = 64
ROPE_BASE = 10000.0
K_SCALE = DH ** -0.5
EPS = 1e-6
GELU_K = math.sqrt(2.0 / math.pi)
GELU_C = 0.044715

ADAM_LR = 0.001
ADAM_B1 = 0.9
ADAM_B2 = 0.999
ADAM_EPS = 1e-08
ADAM_WD = 0.01
ADAM_STEP = 10

N_DEV = 8
N_CHIP = 4
SUBLANES = 8
LANES = 128
VMEM_LIMIT_V7X = 56 * 1024 * 1024
MESH = pl.DeviceIdType.MESH
ANY = pl.BlockSpec(memory_space=pl.ANY)


def _pc(body, **kw):
    return pl.pallas_call(body, **kw)


def _params(*sem):
    return pltpu.CompilerParams(dimension_semantics=sem if sem else None, vmem_limit_bytes=VMEM_LIMIT_V7X)


def _tile(t, big=False):
    if big and t >= 1024:
        return 512
    return 256 if t >= 256 else t


def _sds(shape, dtype=F32):
    return jax.ShapeDtypeStruct(tuple(shape), dtype)


def _full(shape):
    nd = len(shape)
    return pl.BlockSpec(tuple(shape), lambda *_: (0,) * nd)


def _sigmoid(x):
    return 0.5 * jnp.tanh(0.5 * x) + 0.5


def _log1p_pos(y):
    s = y * (1.0 - y * (0.5 - y * (1.0 / 3.0 - y * (0.25 - y * (0.2 - y / 6.0)))))
    return jnp.where(y < 0.03, s, jnp.log(1.0 + y))


def _softplus(z):
    return jnp.maximum(z, 0.0) + _log1p_pos(jnp.exp(-jnp.abs(z)))


def _one_minus_sq(la, a):
    t = la * (1.0 + la * (0.5 + la * (1.0 / 6.0 + la * (1.0 / 24.0 + la * (1.0 / 120.0)))))
    return jnp.where(la > -0.125, -t, 1.0 - a) * (1.0 + a)


def _rms(x):
    r = lax.rsqrt(jnp.mean(x * x, axis=-1, keepdims=True) + EPS)
    return x * r, r


def _dot(a, b):
    return jnp.dot(a, b, preferred_element_type=F32)


def _dot_nt(a, b):
    return lax.dot_general(a, b, (((1,), (1,)), ((), ())), preferred_element_type=F32)


def _dot_tn(a, b):
    return lax.dot_general(a, b, (((0,), (0,)), ((), ())), preferred_element_type=F32)


def _sum0(x):
    return jnp.sum(x, axis=0, keepdims=True)


def _norm_mod_bwd(x, g, sc, dh):
    xh, r = _rms(x)
    hn = xh * g
    dhn = dh * (1.0 + sc)
    dxh = dhn * g
    dx = r * (dxh - xh * jnp.mean(dxh * xh, axis=-1, keepdims=True))
    return dx, _sum0(dhn * xh), _sum0(dh), _sum0(dh * hn)


def _dev_index(p):
    return 4 * p[0] + 2 * p[1] + p[2]


def _mesh_pos():
    return lax.axis_index("x"), lax.axis_index("y"), lax.axis_index("c")


class _AllGather:
    def __init__(self, arrs):
        n = len(arrs)
        self.arrays = list(arrs)
        self.out_shapes = [_sds((N_DEV,) + a.shape, a.dtype) for a in arrs]
        self.scratch = ([pltpu.VMEM(a.shape, a.dtype) for a in arrs]
                        + [pltpu.SemaphoreType.DMA((7 * n,)), pltpu.SemaphoreType.DMA((7 * n,)),
                           pltpu.SemaphoreType.DMA((n,))])
        self.aliases = {}

    def _parts(self, ins, outs, scr):
        n = len(self.arrays)
        stage = scr[:n]
        send_sems, recv_sems, local_sems = scr[n:]
        x, y, c = _mesh_pos()
        me, sib = (x, y, c), (x, y, 1 - c)
        chips = [(1 - x, y), (x, 1 - y), (1 - x, 1 - y)]

        def copy(t, k, block, to, own=False):
            dst = outs[t].at[_dev_index(block)]
            return pltpu.make_async_remote_copy(
                src_ref=ins[t] if own else dst, dst_ref=dst,
                send_sem=send_sems.at[7 * t + k], recv_sem=recv_sems.at[7 * t + k],
                device_id=to, device_id_type=MESH)

        first = []
        for t in range(n):
            first.append(copy(t, 0, me, sib, own=True))
            for j, ch in enumerate(chips):
                first.append(copy(t, 1 + j, me, (*ch, c), own=True))
        stage_in = [pltpu.make_async_copy(ins[t], stage[t], local_sems.at[t]) for t in range(n)]
        mine = [pltpu.make_async_copy(stage[t], outs[t].at[_dev_index(me)], local_sems.at[t]) for t in range(n)]
        return n, c, me, sib, chips, copy, first, stage_in, mine

    def start(self, ins, outs, scr):
        n, _, _, _, _, _, first, stage_in, mine = self._parts(ins, outs, scr)
        for cp in stage_in:
            cp.start()
        for cp in first:
            cp.start()
        for t in range(n):
            stage_in[t].wait()
            mine[t].start()

    def relay(self, ins, outs, scr):
        n, c, me, sib, chips, copy, _, _, _ = self._parts(ins, outs, scr)
        for j, ch in enumerate(chips):
            for t in range(n):
                copy(t, 1 + j, (*ch, c), me).wait_recv()
                copy(t, 4 + j, (*ch, c), sib).start()

    def finish(self, ins, outs, scr):
        n, c, me, sib, chips, copy, first, _, mine = self._parts(ins, outs, scr)
        passed = [copy(t, 4 + j, (*ch, c), sib) for j, ch in enumerate(chips) for t in range(n)]
        for t in range(n):
            copy(t, 0, sib, me).wait_recv()
            for j, ch in enumerate(chips):
                copy(t, 4 + j, (*ch, 1 - c), me).wait_recv()
        for cp in first + passed:
            cp.wait_send()
        for cp in mine:
            cp.wait()


class _Exchange:
    def __init__(self, arrays, out_shapes, plan, n_copies, aliases=None):
        self.arrays = list(arrays)
        self.out_shapes = list(out_shapes)
        self.plan = plan
        self.scratch = [pltpu.SemaphoreType.DMA((n_copies,)), pltpu.SemaphoreType.DMA((n_copies,))]
        self.aliases = aliases or {}

    def _copies(self, ins, outs, scr):
        send_sems, recv_sems = scr
        snd, rcv = [], []
        for i, (src, dst, peer, lands) in enumerate(self.plan(ins, outs, _mesh_pos())):
            kw = dict(send_sem=send_sems.at[i], recv_sem=recv_sems.at[i], device_id=peer, device_id_type=MESH)
            snd.append(pltpu.make_async_remote_copy(src_ref=src, dst_ref=dst, **kw))
            rcv.append(pltpu.make_async_remote_copy(src_ref=src, dst_ref=lands, **kw))
        return snd, rcv

    def start(self, ins, outs, scr):
        for cp in self._copies(ins, outs, scr)[0]:
            cp.start()

    def relay(self, ins, outs, scr):
        pass

    def finish(self, ins, outs, scr):
        snd, rcv = self._copies(ins, outs, scr)
        for cp in rcv:
            cp.wait_recv()
        for cp in snd:
            cp.wait_send()


def _pair_exchange(grads):
    n = len(grads)

    def plan(ins, outs, pos):
        x, y, c = pos
        return [(ins[t].at[2 * j + (1 - c)], outs[t].at[j], (x, y, 1 - c), outs[t].at[j])
                for t in range(n) for j in range(N_CHIP)]

    return _Exchange(grads, [_sds((N_CHIP,) + g.shape[1:], g.dtype) for g in grads], plan, N_CHIP * n)


def _chip_exchange(parts, rows=None):
    n = len(parts)

    def plan(ins, outs, pos):
        x, y, c = pos
        chips = [(1 - x, y), (x, 1 - y), (1 - x, 1 - y)]

        def src(t, ch):
            blk = ins[t].at[2 * ch[0] + ch[1]]
            return blk if rows is None else blk.at[pl.ds(rows[0], rows[1])]

        return [(src(t, ch), outs[t].at[k], (*ch, c), outs[t].at[k]) for t in range(n) for k, ch in enumerate(chips)]

    shapes = [_sds((3, p.shape[1] if rows is None else rows[1]) + p.shape[2:], p.dtype) for p in parts]
    return _Exchange(parts, shapes, plan, 3 * n)


def _pair_gather(bufs):
    n = len(bufs)

    def plan(ins, outs, pos):
        x, y, c = pos
        return [(ins[t].at[c], outs[t].at[c], (x, y, 1 - c), outs[t].at[1 - c]) for t in range(n)]

    return _Exchange(bufs, [_sds(b.shape, b.dtype) for b in bufs], plan, n, aliases={t: t for t in range(n)})


def _run_comms(comms, name):
    c_in = [len(cm.arrays) for cm in comms]
    c_out = [len(cm.out_shapes) for cm in comms]
    c_scr = [len(cm.scratch) for cm in comms]
    aliases = {}
    for k, cm in enumerate(comms):
        for a, b in cm.aliases.items():
            aliases[sum(c_in[:k]) + a] = sum(c_out[:k]) + b

    def split(refs, counts):
        out, pos = [], 0
        for cnt in counts:
            out.append(refs[pos:pos + cnt])
            pos += cnt
        return out

    def body(*refs):
        ins = split(refs[:sum(c_in)], c_in)
        outs = split(refs[sum(c_in):sum(c_in) + sum(c_out)], c_out)
        scr = split(refs[sum(c_in) + sum(c_out):], c_scr)
        for phase in ("start", "relay", "finish"):
            for k, cm in enumerate(comms):
                getattr(cm, phase)(ins[k], outs[k], scr[k])

    outs = _pc(body, name=name, out_shape=[s for cm in comms for s in cm.out_shapes],
               in_specs=[ANY] * sum(c_in), out_specs=[ANY] * sum(c_out), input_output_aliases=aliases,
               scratch_shapes=[s for cm in comms for s in cm.scratch],
               compiler_params=_params())(*[a for cm in comms for a in cm.arrays])
    return split(list(outs), c_out)


def _all_gather(arrs, name):
    return _run_comms([_AllGather(arrs)], name)[0]


def _call(body, *, name, grid, in_specs, out_specs, out_shape, scratch_shapes, sem, args, comms=()):
    n_in, n_out, n_scr = len(in_specs), len(out_specs), len(scratch_shapes)
    c_in = [len(cm.arrays) for cm in comms]
    c_out = [len(cm.out_shapes) for cm in comms]
    c_scr = [len(cm.scratch) for cm in comms]
    aliases = {}
    for k, cm in enumerate(comms):
        for a, b in cm.aliases.items():
            aliases[n_in + sum(c_in[:k]) + a] = n_out + sum(c_out[:k]) + b

    def split(refs, counts):
        out, pos = [], 0
        for cnt in counts:
            out.append(refs[pos:pos + cnt])
            pos += cnt
        return out

    def wrapped(*refs):
        ins = refs[:n_in + sum(c_in)]
        outs = refs[len(ins):len(ins) + n_out + sum(c_out)]
        scr = refs[len(ins) + len(outs):]
        cins, couts, cscr = split(ins[n_in:], c_in), split(outs[n_out:], c_out), split(scr[n_scr:], c_scr)
        if comms:
            first = pl.program_id(0) == 0
            last = pl.program_id(0) == grid[0] - 1
            for k in range(1, len(grid)):
                first = jnp.logical_and(first, pl.program_id(k) == 0)
                last = jnp.logical_and(last, pl.program_id(k) == grid[k] - 1)

            @pl.when(first)
            def _():
                for k, cm in enumerate(comms):
                    cm.start(cins[k], couts[k], cscr[k])
        body(*ins[:n_in], *outs[:n_out], *scr[:n_scr])
        if comms:
            relay_early = len(grid) == 1 and grid[0] >= 4
            if relay_early:
                @pl.when(pl.program_id(0) == (7 * grid[0]) // 8 - 1)
                def _():
                    for k, cm in enumerate(comms):
                        cm.relay(cins[k], couts[k], cscr[k])

            @pl.when(last)
            def _():
                for k, cm in enumerate(comms):
                    if not relay_early:
                        cm.relay(cins[k], couts[k], cscr[k])
                    cm.finish(cins[k], couts[k], cscr[k])

    outs = _pc(wrapped, name=name, grid=grid,
               in_specs=list(in_specs) + [ANY] * sum(c_in), out_specs=list(out_specs) + [ANY] * sum(c_out),
               out_shape=list(out_shape) + [s for cm in comms for s in cm.out_shapes],
               scratch_shapes=list(scratch_shapes) + [s for cm in comms for s in cm.scratch],
               input_output_aliases=aliases, compiler_params=_params(*sem),
               )(*args, *[a for cm in comms for a in cm.arrays])
    outs = list(outs)
    return outs[:n_out], split(outs[n_out:], c_out)


def _row_block(r):
    for b in (512, 256, 128, 64, 32, 16, 8):
        if r % b == 0:
            return b
    return r


def _pair_add(g, recv, cj_idx, name):
    _, r, cc = g.shape
    br = _row_block(r)

    def body(cj_ref, g_ref, r_ref, own_ref, pb_ref):
        s = g_ref[...] + r_ref[...]
        pb_ref[...] = s.astype(BF16)

        @pl.when(pl.program_id(1) == cj_ref[1])
        def _():
            own_ref[...] = s[0]

    grid_spec = pltpu.PrefetchScalarGridSpec(
        num_scalar_prefetch=1, grid=(r // br, N_CHIP),
        in_specs=[pl.BlockSpec((1, br, cc), lambda i, j, cj_ref: (2 * j + cj_ref[0], i, 0)),
                  pl.BlockSpec((1, br, cc), lambda i, j, cj_ref: (j, i, 0))],
        out_specs=[pl.BlockSpec((br, cc), lambda i, j, cj_ref: (i, 0)),
                   pl.BlockSpec((1, br, cc), lambda i, j, cj_ref: (j, i, 0))])
    return _pc(body, name=name, grid_spec=grid_spec,
               out_shape=[_sds((r, cc)), _sds((N_CHIP, r, cc), BF16)],
               compiler_params=_params("arbitrary", "arbitrary"))(cj_idx, g, recv)


def _chip_add(p, qs, cj_idx, name):
    r, cc = p.shape
    nq = len(qs)
    br = _row_block(r // nq)
    nb = r // nq // br

    def body(cj_ref, p_ref, *refs):
        o_ref = refs[-1]
        if nq == 2:
            top = pl.program_id(0) < nb
            q = [jnp.where(top, refs[0][k], refs[1][k]).astype(F32) for k in range(3)]
        else:
            q = [refs[0][k].astype(F32) for k in range(3)]
        o_ref[0] = ((p_ref[...] + q[0]) + q[1]) + q[2]

    q_specs = [pl.BlockSpec((3, br, cc), lambda i, cj_ref, h=h: (0, jnp.clip(i - h * nb, 0, nb - 1), 0))
               for h in range(nq)]
    grid_spec = pltpu.PrefetchScalarGridSpec(
        num_scalar_prefetch=1, grid=(r // br,),
        in_specs=[pl.BlockSpec((br, cc), lambda i, cj_ref: (i, 0))] + q_specs,
        out_specs=pl.BlockSpec((1, br, cc), lambda i, cj_ref: (cj_ref[0], i, 0)))
    return _pc(body, name=name, grid_spec=grid_spec, out_shape=_sds((2, r, cc)),
               compiler_params=_params("arbitrary"))(cj_idx, p, *qs)


def _shard_of(both):
    return both.reshape((2 * both.shape[1],) + both.shape[2:])


ADAMW_CHUNKS = 4


def _adamw(w, g, m, v, name):
    r, cc = w.shape
    rows = r // ADAMW_CHUNKS
    assert rows * ADAMW_CHUNKS == r and rows % SUBLANES == 0
    c1 = 1.0 - ADAM_B1 ** ADAM_STEP
    c2 = 1.0 - ADAM_B2 ** ADAM_STEP

    def body(w_hbm, g_hbm, m_hbm, v_hbm, go_hbm, d_hbm, mo_hbm, vo_hbm, wb, gb, mb, vb, sem_in, sem_out):
        srcs, bufs, dsts = (w_hbm, g_hbm, m_hbm, v_hbm), (wb, gb, mb, vb), (d_hbm, go_hbm, mo_hbm, vo_hbm)

        def load(a, k):
            sl = pl.ds(k * rows, rows)
            return pltpu.make_async_copy(srcs[a].at[sl], bufs[a].at[sl], sem_in.at[a, k])

        def store(a, k):
            sl = pl.ds(k * rows, rows)
            return pltpu.make_async_copy(bufs[a].at[sl], dsts[a].at[sl], sem_out.at[a, k])

        for k in range(ADAMW_CHUNKS):
            for a in range(4):
                load(a, k).start()
        for k in range(ADAMW_CHUNKS):
            for a in range(4):
                load(a, k).wait()
            store(1, k).start()
            sl = pl.ds(k * rows, rows)
            gg = gb[sl]
            mn = ADAM_B1 * mb[sl] + (1.0 - ADAM_B1) * gg
            vn = ADAM_B2 * vb[sl] + (1.0 - ADAM_B2) * (gg * gg)
            mh = mn / c1
            vh = vn / c2
            wb[sl] = -ADAM_LR * (mh / (jnp.sqrt(vh) + ADAM_EPS) + ADAM_WD * wb[sl])
            mb[sl] = mn
            vb[sl] = vn
            for a in (0, 2, 3):
                store(a, k).start()
        for k in range(ADAMW_CHUNKS):
            for a in range(4):
                store(a, k).wait()

    go, d, mo, vo = _pc(body, name=name, in_specs=[ANY] * 4, out_specs=[ANY] * 4, out_shape=[_sds((r, cc))] * 4,
                        scratch_shapes=[pltpu.VMEM((r, cc), F32)] * 4 + [pltpu.SemaphoreType.DMA((4, ADAMW_CHUNKS))] * 2,
                        compiler_params=_params())(w, g, m, v)
    return go, d, mo, vo


def _head(w_half, c8, small, w_ada, b_shard, c_ctx, ret_decay):
    ada_n = w_ada.shape[1]
    mod_sds = _sds((16, ada_n))
    ag_w, ag_c, ag_m = _AllGather([w_half]), _AllGather([c8, small]), _AllGather([mod_sds])
    n_w, n_c, n_m = len(ag_w.scratch), len(ag_c.scratch), len(ag_m.scratch)

    def body(w_ref, c_ref, s_ref, wada_ref, b_ref, cc_ref, rd_ref,
             gw_ref, call_ref, sall_ref, a_ref, modp_ref, mall_ref, lg_ref, sg_ref, *scr):
        scr_w, scr_c, scr_m = scr[:n_w], scr[n_w:n_w + n_c], scr[n_w + n_c:n_w + n_c + n_m]
        c_v, w_v, m_v, sems = scr[n_w + n_c + n_m:]
        ag_w.start((w_ref,), (gw_ref,), scr_w)
        ag_c.start((c_ref, s_ref), (call_ref, sall_ref), scr_c)
        load_w = pltpu.make_async_copy(wada_ref, w_v, sems.at[0])
        load_w.start()
        rd = rd_ref[...]
        lg_ref[...] = -_softplus(-rd)
        sg_ref[...] = _sigmoid(-rd)
        ag_c.relay((c_ref, s_ref), (call_ref, sall_ref), scr_c)
        ag_c.finish((c_ref, s_ref), (call_ref, sall_ref), scr_c)
        load_c = pltpu.make_async_copy(call_ref, c_v, sems.at[1])
        load_c.start()
        load_c.wait()
        a_ref[...] = jnp.zeros_like(a_ref)
        for d in range(N_DEV):
            cd = c_v[d, 0:1, :]
            a_ref[d:d + 1, :] = cd * _sigmoid(cd)
        cc = cc_ref[...]
        a_ref[N_DEV:N_DEV + 1, :] = cc * _sigmoid(cc)
        load_w.wait()
        m_v[...] = jnp.dot(a_ref[...], w_v[...], preferred_element_type=F32,
                           precision=lax.Precision.HIGHEST) + b_ref[...]
        put = pltpu.make_async_copy(m_v, modp_ref, sems.at[2])
        put.start()
        put.wait()
        ag_m.start((modp_ref,), (mall_ref,), scr_m)
        ag_m.relay((modp_ref,), (mall_ref,), scr_m)
        ag_m.finish((modp_ref,), (mall_ref,), scr_m)
        ag_w.relay((w_ref,), (gw_ref,), scr_w)
        ag_w.finish((w_ref,), (gw_ref,), scr_w)

    rd = jnp.broadcast_to(ret_decay.reshape(2, HEADS).T[:, :, None], (HEADS, 2, LANES))
    lane = _full((HEADS, 2, LANES))
    outs = _pc(
        body, name="head",
        in_specs=[ANY, ANY, ANY, ANY, _full((1, ada_n)), _full((1, D_MODEL)), lane],
        out_specs=[ANY, ANY, ANY, _full((16, D_MODEL)), ANY, ANY, lane, lane],
        out_shape=ag_w.out_shapes + ag_c.out_shapes + [_sds((16, D_MODEL)), mod_sds] + ag_m.out_shapes
        + [_sds((HEADS, 2, LANES))] * 2,
        scratch_shapes=ag_w.scratch + ag_c.scratch + ag_m.scratch
        + [pltpu.VMEM((N_DEV,) + c8.shape, F32), pltpu.VMEM(w_ada.shape, F32), pltpu.VMEM((16, ada_n), F32),
           pltpu.SemaphoreType.DMA((3,))],
        compiler_params=_params(),
    )(w_half, c8, small, w_ada, b_shard, c_ctx.reshape(1, D_MODEL), rd)
    gw, c_all, small_all, a16, _, mod_all, lgv, sgv = outs
    return gw, c_all, small_all, a16, mod_all, lgv, sgv


def _ada_grad(at, b):
    n = b.shape[1]
    bn = 512

    def body(a_ref, b_ref, o_ref):
        o_ref[...] = jnp.dot(a_ref[...], b_ref[...], preferred_element_type=F32, precision=lax.Precision.HIGHEST)

    return _pc(body, name="ada_grad", grid=(n // bn,),
               in_specs=[_full((D_MODEL, LANES)), pl.BlockSpec((LANES, bn), lambda i: (0, i))],
               out_specs=pl.BlockSpec((D_MODEL, bn), lambda i: (0, i)), out_shape=_sds((D_MODEL, n)),
               compiler_params=_params("arbitrary"))(at, b)


def _cctx_partial(dmc8, w_ada):
    n = w_ada.shape[1]
    bn = 512

    def body(d_ref, w_ref, o_ref):
        @pl.when(pl.program_id(0) == 0)
        def _():
            o_ref[...] = jnp.zeros_like(o_ref)
        o_ref[...] += lax.dot_general(d_ref[...], w_ref[...], (((1,), (1,)), ((), ())),
                                      preferred_element_type=F32, precision=lax.Precision.HIGHEST)

    return _pc(body, name="cctx_partial", grid=(n // bn,),
               in_specs=[pl.BlockSpec((8, bn), lambda i: (0, i)), pl.BlockSpec((D_MODEL, bn), lambda i: (0, i))],
               out_specs=_full((8, D_MODEL)), out_shape=_sds((8, D_MODEL)),
               compiler_params=_params("arbitrary"))(dmc8, w_ada)


def _cctx_final(parts, c_ctx, m, v):
    c1 = 1.0 - ADAM_B1 ** ADAM_STEP
    c2 = 1.0 - ADAM_B2 ** ADAM_STEP

    def body(p_ref, c_ref, m_ref, v_ref, g_ref, d_ref, mo_ref, vo_ref):
        s = ((p_ref[0, 0:1, :] + p_ref[2, 0:1, :]) + p_ref[4, 0:1, :]) + p_ref[6, 0:1, :]
        z = c_ref[...]
        sg = _sigmoid(z)
        gg = s * (sg * (1.0 + z * (1.0 - sg)))
        g_ref[...] = gg
        mn = ADAM_B1 * m_ref[...] + (1.0 - ADAM_B1) * gg
        vn = ADAM_B2 * v_ref[...] + (1.0 - ADAM_B2) * (gg * gg)
        d_ref[...] = -ADAM_LR * ((mn / c1) / (jnp.sqrt(vn / c2) + ADAM_EPS) + ADAM_WD * z)
        mo_ref[...] = mn
        vo_ref[...] = vn

    row = _full((1, D_MODEL))
    return _pc(body, name="cctx_final", out_shape=[_sds((1, D_MODEL))] * 4,
               in_specs=[_full(parts.shape), row, row, row], out_specs=[row] * 4,
               compiler_params=_params())(parts, c_ctx.reshape(1, D_MODEL), m.reshape(1, D_MODEL), v.reshape(1, D_MODEL))


def _rotary_tables(t_len):
    rows = t_len // GRID_W
    n_freq = DH // 4
    inv = ROPE_BASE ** (-jnp.arange(n_freq, dtype=F32) / n_freq)
    row_ang = jnp.arange(rows, dtype=F32)[:, None] * inv
    col_ang = jnp.arange(GRID_W, dtype=F32)[:, None] * inv

    def spread(fn):
        return jnp.concatenate([jnp.repeat(fn(row_ang), GRID_W, axis=0), jnp.tile(fn(col_ang), (rows, 1))], axis=-1)

    cos, sin = spread(jnp.cos), spread(jnp.sin)
    return jnp.concatenate([cos, cos], axis=-1), jnp.concatenate([-sin, sin], axis=-1)


def _inproj_fwd(x, gn, sh, sc, w4, cos2, sin2, name, comms=()):
    t = x.shape[0]
    tm = _tile(t, True)
    nc = IN_COLS // N_CHIP

    def body(x_ref, gn_ref, sh_ref, sc_ref, w_ref, c_ref, s_ref, p_ref, xr_ref, hb_ref, p_s):
        xh, _ = _rms(x_ref[...])
        h = xh * gn_ref[...] * (1.0 + sc_ref[...]) + sh_ref[...]
        hb = h.astype(BF16)
        hb_ref[...] = hb
        for j in range(N_CHIP):
            p_s[:, nc * j:nc * (j + 1)] = _dot(hb, w_ref[j])
        cc = c_ref[...]
        ss = s_ref[...]
        for hh in range(2 * HEADS):
            blk = p_s[:, DH * hh:DH * (hh + 1)]
            rot = blk * cc + pltpu.roll(blk, DH // 2, 1) * ss
            if hh >= HEADS:
                rot = rot * K_SCALE
            p_ref[:, DH * hh:DH * (hh + 1)] = rot.astype(BF16)
        p_ref[:, 2 * RET_W:] = p_s[:, 2 * RET_W:].astype(BF16)
        xr_ref[...] = p_s[:, 4 * RET_W:4 * RET_W + LRU_W]

    row = _full((1, D_MODEL))
    outs, couts = _call(
        body, name=name, grid=(t // tm,),
        in_specs=[pl.BlockSpec((tm, D_MODEL), lambda i: (i, 0)), row, row, row, _full(w4.shape),
                  pl.BlockSpec((tm, DH), lambda i: (i, 0)), pl.BlockSpec((tm, DH), lambda i: (i, 0))],
        out_specs=[pl.BlockSpec((tm, IN_COLS), lambda i: (i, 0)), pl.BlockSpec((tm, LRU_W), lambda i: (i, 0)),
                   pl.BlockSpec((tm, D_MODEL), lambda i: (i, 0))],
        out_shape=[_sds((t, IN_COLS), BF16), _sds((t, LRU_W)), _sds((t, D_MODEL), BF16)],
        scratch_shapes=[pltpu.VMEM((tm, IN_COLS), F32)], sem=("arbitrary",),
        args=(x, gn, sh, sc, w4, cos2, sin2), comms=comms)
    return (outs, couts) if comms else outs


def _inproj_bwd(x, gn, sh, sc, w4, cos2, sin2, pieces, dres, name):
    t = x.shape[0]
    tm = _tile(t)
    nc = IN_COLS // N_CHIP

    def body(x_ref, gn_ref, sh_ref, sc_ref, w_ref, c_ref, s_ref, dqf, dqb, dkf, dkb, dvf, dvb, dg, dxr, dgt, dres_ref,
             dx_ref, dpb_ref, dgn_ref, dsh_ref, dsc_ref):
        cc = c_ref[...]
        ss = s_ref[...]
        dq = dqf[...].astype(F32) + dqb[...].astype(F32)
        dk = dkf[...].astype(F32) + dkb[...].astype(F32)
        for hh in range(HEADS):
            sl = slice(DH * hh, DH * (hh + 1))
            b = dq[:, sl]
            dpb_ref[:, sl] = (b * cc + pltpu.roll(b * ss, DH // 2, 1)).astype(BF16)
            b = dk[:, sl]
            dpb_ref[:, RET_W + DH * hh:RET_W + DH * (hh + 1)] = (
                (b * cc + pltpu.roll(b * ss, DH // 2, 1)) * K_SCALE).astype(BF16)
        dpb_ref[:, 2 * RET_W:3 * RET_W] = (dvf[...].astype(F32) + dvb[...].astype(F32)).astype(BF16)
        dpb_ref[:, 3 * RET_W:4 * RET_W] = dg[...].astype(BF16)
        dpb_ref[:, 4 * RET_W:4 * RET_W + LRU_W] = dxr[...].astype(BF16)
        dpb_ref[:, 4 * RET_W + LRU_W:IN_COLS] = dgt[...].astype(BF16)
        dh = _dot_nt(dpb_ref[:, 0:nc], w_ref[0])
        for j in range(1, N_CHIP):
            dh = dh + _dot_nt(dpb_ref[:, nc * j:nc * (j + 1)], w_ref[j])
        dx, dgn_t, dsh_t, dsc_t = _norm_mod_bwd(x_ref[...], gn_ref[...], sc_ref[...], dh)
        dx_ref[...] = dres_ref[...] + dx

        @pl.when(pl.program_id(0) == 0)
        def _():
            dgn_ref[...] = jnp.zeros_like(dgn_ref)
            dsh_ref[...] = jnp.zeros_like(dsh_ref)
            dsc_ref[...] = jnp.zeros_like(dsc_ref)
        dgn_ref[...] += dgn_t
        dsh_ref[...] += dsh_t
        dsc_ref[...] += dsc_t

    row = _full((1, D_MODEL))
    pc = pl.BlockSpec((tm, RET_W), lambda i: (i, 0))
    big = pl.BlockSpec((tm, D_MODEL), lambda i: (i, 0))
    return _pc(body, name=name, grid=(t // tm,),
               in_specs=[big, row, row, row, _full(w4.shape),
                         pl.BlockSpec((tm, DH), lambda i: (i, 0)), pl.BlockSpec((tm, DH), lambda i: (i, 0))]
               + [pc] * 9 + [big],
               out_specs=[big, pl.BlockSpec((tm, IN_COLS), lambda i: (i, 0)), row, row, row],
               out_shape=[_sds((t, D_MODEL)), _sds((t, IN_COLS), BF16), _sds((1, D_MODEL)), _sds((1, D_MODEL)),
                          _sds((1, D_MODEL))],
               compiler_params=_params("arbitrary"))(x, gn, sh, sc, w4, cos2, sin2, *pieces, dres)


def _halo_specs(t, tm):
    n8 = tm // SUBLANES
    last8 = t // SUBLANES - 1
    prev = pl.BlockSpec((SUBLANES, LRU_W), lambda i: (jnp.maximum(i * n8 - 1, 0), 0))
    main = pl.BlockSpec((tm, LRU_W), lambda i: (i, 0))
    nxt = pl.BlockSpec((SUBLANES, LRU_W), lambda i: (jnp.minimum((i + 1) * n8, last8), 0))
    return prev, main, nxt


def _with_halo(prev_ref, main_ref, next_ref, i, nt):
    prev = jnp.where(i > 0, prev_ref[...], 0.0)
    nxt = jnp.where(i < nt - 1, next_ref[...], 0.0)
    return jnp.concatenate([prev, main_ref[...], nxt], axis=0)


def _conv_fwd(xr, cw, cb, name):
    t = xr.shape[0]
    tm = _tile(t, True)
    nt = t // tm
    n = tm + 2 * SUBLANES
    mid = slice(SUBLANES, SUBLANES + tm)

    def body(p_ref, m_ref, n_ref, w_ref, b_ref, o_ref):
        xp = _with_halo(p_ref, m_ref, n_ref, pl.program_id(0), nt)
        acc = b_ref[...] + pltpu.roll(xp, 1, 0)[mid] * w_ref[0:1, :]
        acc = acc + xp[mid] * w_ref[1:2, :]
        acc = acc + pltpu.roll(xp, n - 1, 0)[mid] * w_ref[2:3, :]
        acc = acc + pltpu.roll(xp, n - 2, 0)[mid] * w_ref[3:4, :]
        o_ref[...] = acc

    return _pc(body, name=name, grid=(nt,),
               in_specs=[*_halo_specs(t, tm), _full((4, LRU_W)), _full((1, LRU_W))],
               out_specs=pl.BlockSpec((tm, LRU_W), lambda i: (i, 0)), out_shape=_sds((t, LRU_W)),
               compiler_params=_params("arbitrary"))(xr, xr, xr, cw, cb)


def _conv_bwd(dxc_a, dxc_b, xr, cw, name):
    t = xr.shape[0]
    tm = _tile(t, True)
    nt = t // tm
    n = tm + 2 * SUBLANES
    mid = slice(SUBLANES, SUBLANES + tm)

    def body(ap_ref, am_ref, an_ref, bp_ref, bm_ref, bn_ref, xp_ref, xm_ref, xn_ref, w_ref, dx_ref, dw_ref, db_ref):
        i = pl.program_id(0)
        dp = _with_halo(ap_ref, am_ref, an_ref, i, nt) + _with_halo(bp_ref, bm_ref, bn_ref, i, nt)
        xp = _with_halo(xp_ref, xm_ref, xn_ref, i, nt)
        dx = pltpu.roll(dp, n - 1, 0)[mid] * w_ref[0:1, :]
        dx = dx + dp[mid] * w_ref[1:2, :]
        dx = dx + pltpu.roll(dp, 1, 0)[mid] * w_ref[2:3, :]
        dx = dx + pltpu.roll(dp, 2, 0)[mid] * w_ref[3:4, :]
        dx_ref[...] = dx.astype(BF16)
        d = dp[mid]

        @pl.when(i == 0)
        def _():
            dw_ref[...] = jnp.zeros_like(dw_ref)
            db_ref[...] = jnp.zeros_like(db_ref)
        dw_ref[0:1, :] += _sum0(d * pltpu.roll(xp, 1, 0)[mid])
        dw_ref[1:2, :] += _sum0(d * xp[mid])
        dw_ref[2:3, :] += _sum0(d * pltpu.roll(xp, n - 1, 0)[mid])
        dw_ref[3:4, :] += _sum0(d * pltpu.roll(xp, n - 2, 0)[mid])
        db_ref[...] += _sum0(d)

    return _pc(body, name=name, grid=(nt,),
               in_specs=[*_halo_specs(t, tm), *_halo_specs(t, tm), *_halo_specs(t, tm), _full((4, LRU_W))],
               out_specs=[pl.BlockSpec((tm, LRU_W), lambda i: (i, 0)), _full((4, LRU_W)), _full((1, LRU_W))],
               out_shape=[_sds((t, LRU_W), BF16), _sds((4, LRU_W)), _sds((1, LRU_W))],
               compiler_params=_params("arbitrary"))(dxc_a, dxc_a, dxc_a, dxc_b, dxc_b, dxc_b, xr, xr, xr, cw)


def _local_scan(a, b, reverse):
    n = a.shape[0]
    row = lax.broadcasted_iota(jnp.int32, a.shape, 0) & (SUBLANES - 1)
    for s in (1, 2, 4):
        if reverse:
            a_s, b_s, ok = pltpu.roll(a, n - s, 0), pltpu.roll(b, n - s, 0), row < SUBLANES - s
        else:
            a_s, b_s, ok = pltpu.roll(a, s, 0), pltpu.roll(b, s, 0), row >= s
        b = a * jnp.where(ok, b_s, 0.0) + b
        a = a * jnp.where(ok, a_s, 1.0)
    return a, b


def _carry_scan(a_s, b_s, out_ref, carry, reverse):
    ng = a_s.shape[0] // SUBLANES
    shape = carry.shape

    def step(g, cr):
        gg = (ng - 1 - g) if reverse else g
        off = pl.multiple_of(gg * SUBLANES, SUBLANES)
        h = a_s[pl.ds(off, SUBLANES), :] * cr + b_s[pl.ds(off, SUBLANES), :]
        out_ref[pl.ds(off, SUBLANES), :] = h
        edge = h[0:1, :] if reverse else h[SUBLANES - 1:SUBLANES, :]
        return jnp.broadcast_to(edge, shape)

    return lax.fori_loop(0, ng, step, carry)


def _lru_gates(xc, wa_ref, wx_ref, ba, bx, lam):
    xb = xc.astype(BF16)
    r = _sigmoid(_dot(xb, wa_ref[...]) + ba)
    ig = _sigmoid(_dot(xb, wx_ref[...]) + bx)
    sp = _softplus(-lam)
    la = -LRU_C * r * sp
    a = jnp.exp(la)
    mult = jnp.sqrt(_one_minus_sq(la, a))
    return r, ig, sp, a, mult


def _lru_fwd(xc, par_f, par_b, h0_f, h0_b, name, comms=()):
    t = xc.shape[0]
    tm = _tile(t, True)
    nt = t // tm

    def one(x_ref, prm, h0_ref, h_ref, a_s, b_s, c_s, reverse):
        wa_ref, wx_ref, ba_ref, bx_ref, lam_ref = prm

        @pl.when(pl.program_id(0) == 0)
        def _():
            c_s[...] = jnp.broadcast_to(h0_ref[...], c_s.shape)
        xv = x_ref[...]
        _, ig, _, a, mult = _lru_gates(xv, wa_ref, wx_ref, ba_ref[...], bx_ref[...], lam_ref[...])
        al, bl = _local_scan(a, mult * (ig * xv), reverse)
        a_s[...] = al
        b_s[...] = bl
        c_s[...] = _carry_scan(a_s, b_s, h_ref, c_s[...], reverse)

    def body(xf_ref, xb_ref, *refs):
        prm_f, prm_b = refs[0:5], refs[5:10]
        h0f_ref, h0b_ref, hf_ref, hb_ref = refs[10:14]
        af_s, bf_s, cf_s, ab_s, bb_s, cb_s = refs[14:]
        one(xf_ref, prm_f, h0f_ref, hf_ref, af_s, bf_s, cf_s, False)
        one(xb_ref, prm_b, h0b_ref, hb_ref, ab_s, bb_s, cb_s, True)

    vec = _full((1, LRU_W))
    mat = _full((LRU_W, LRU_W))
    fw = pl.BlockSpec((tm, LRU_W), lambda i: (i, 0))
    bw = pl.BlockSpec((tm, LRU_W), lambda i: (nt - 1 - i, 0))
    tile_s = [pltpu.VMEM((tm, LRU_W), F32), pltpu.VMEM((tm, LRU_W), F32), pltpu.VMEM((SUBLANES, LRU_W), F32)]
    (hf, hb), couts = _call(
        body, name=name, grid=(nt,),
        in_specs=[fw, bw] + [mat, mat, vec, vec, vec] * 2 + [vec, vec],
        out_specs=[pl.BlockSpec((tm, LRU_W), lambda i: (i, 0)), pl.BlockSpec((tm, LRU_W), lambda i: (nt - 1 - i, 0))],
        out_shape=[_sds((t, LRU_W))] * 2, scratch_shapes=tile_s + tile_s, sem=("arbitrary",),
        args=(xc, xc, *par_f, *par_b, h0_f, h0_b), comms=comms)
    return ((hf, hb), couts) if comms else (hf, hb)


def _lru_bwd(xc, par_f, par_b, h_f, h_b, h0_f, h0_b, dh_f, dh_b, name, comms=()):
    t = xc.shape[0]
    tm = _tile(t, True)
    nt = t // tm
    n8 = tm // SUBLANES
    last8 = t // SUBLANES - 1
    tile_f = lambda w: pl.BlockSpec((tm, w), lambda i: (nt - 1 - i, 0))
    tile_b = lambda w: pl.BlockSpec((tm, w), lambda i: (i, 0))
    halo_f = pl.BlockSpec((SUBLANES, LRU_W), lambda i: (jnp.maximum((nt - 1 - i) * n8 - 1, 0), 0))
    halo_b = pl.BlockSpec((SUBLANES, LRU_W), lambda i: (jnp.minimum((i + 1) * n8, last8), 0))

    def one(refs_in, refs_out, refs_scr, reverse):
        x_ref, wa_ref, wx_ref, ba_ref, bx_ref, lam_ref, h_ref, halo_ref, h0_ref, dh_ref = refs_in
        dx_ref, dpre_ref, dba_ref, dbx_ref, dlam_ref, dh0_ref = refs_out
        a_s, b_s, l_s, c_s, e_s = refs_scr
        i = pl.program_id(0)

        @pl.when(i == 0)
        def _():
            c_s[...] = jnp.zeros_like(c_s)
            e_s[...] = jnp.zeros_like(e_s)
            dba_ref[...] = jnp.zeros_like(dba_ref)
            dbx_ref[...] = jnp.zeros_like(dbx_ref)
            dlam_ref[...] = jnp.zeros_like(dlam_ref)
        xv = x_ref[...]
        lam = lam_ref[...]
        r, ig, sp, a, mult = _lru_gates(xv, wa_ref, wx_ref, ba_ref[...], bx_ref[...], lam)
        hv = h_ref[...]
        rowi = lax.broadcasted_iota(jnp.int32, (tm, LRU_W), 0)
        edge_a = jnp.broadcast_to(e_s[0:1, :], (tm, LRU_W))
        h0b = jnp.broadcast_to(h0_ref[...], (tm, LRU_W))
        if reverse:
            a_sh = jnp.where(rowi == 0, edge_a, pltpu.roll(a, 1, 0))
            hin_edge = jnp.where(i == nt - 1, h0b, jnp.broadcast_to(halo_ref[0:1, :], (tm, LRU_W)))
            h_in = jnp.where(rowi == tm - 1, hin_edge, pltpu.roll(hv, tm - 1, 0))
        else:
            a_sh = jnp.where(rowi == tm - 1, edge_a, pltpu.roll(a, tm - 1, 0))
            hin_edge = jnp.where(i == nt - 1, h0b, jnp.broadcast_to(halo_ref[SUBLANES - 1:SUBLANES, :], (tm, LRU_W)))
            h_in = jnp.where(rowi == 0, hin_edge, pltpu.roll(hv, 1, 0))
        al, bl = _local_scan(a_sh, dh_ref[...], not reverse)
        a_s[...] = al
        b_s[...] = bl
        c_s[...] = _carry_scan(a_s, b_s, l_s, c_s[...], not reverse)
        e_s[...] = jnp.broadcast_to(a[tm - 1:tm, :] if reverse else a[0:1, :], e_s.shape)
        lmb = l_s[...]
        da = lmb * h_in
        ixc = ig * xv
        dmult = lmb * ixc
        dixc = lmb * mult
        dla = da * a - dmult * (a * a) / mult
        dpr = dla * (-LRU_C * sp) * r * (1.0 - r)
        dpi = dixc * xv * ig * (1.0 - ig)
        dprb = dpr.astype(BF16)
        dpib = dpi.astype(BF16)
        dpre_ref[:, 0:LRU_W] = dprb
        dpre_ref[:, LRU_W:2 * LRU_W] = dpib
        dx_ref[...] = dixc * ig + _dot_nt(dprb, wa_ref[...]) + _dot_nt(dpib, wx_ref[...])
        dba_ref[...] += _sum0(dpr)
        dbx_ref[...] += _sum0(dpi)
        dlam_ref[...] += _sum0(dla * (-LRU_C * r)) * (-_sigmoid(-lam))

        @pl.when(i == nt - 1)
        def _():
            al0 = a * lmb
            dh0_ref[...] = al0[tm - 1:tm, :] if reverse else al0[0:1, :]

    def body(*refs):
        one(refs[0:10], refs[20:26], refs[32:37], False)
        one(refs[10:20], refs[26:32], refs[37:42], True)

    vec = _full((1, LRU_W))
    mat = _full((LRU_W, LRU_W))

    def in_specs(tile, halo):
        return [tile(LRU_W), mat, mat, vec, vec, vec, tile(LRU_W), halo, vec, tile(LRU_W)]

    def out_specs(tile):
        return [tile(LRU_W), tile(2 * LRU_W), vec, vec, vec, vec]

    out_one = [_sds((t, LRU_W)), _sds((t, 2 * LRU_W), BF16)] + [_sds((1, LRU_W))] * 4
    scr_one = [pltpu.VMEM((tm, LRU_W), F32)] * 3 + [pltpu.VMEM((SUBLANES, LRU_W), F32)] * 2
    outs, couts = _call(
        body, name=name, grid=(nt,), in_specs=in_specs(tile_f, halo_f) + in_specs(tile_b, halo_b),
        out_specs=out_specs(tile_f) + out_specs(tile_b), out_shape=out_one + out_one,
        scratch_shapes=scr_one + scr_one, sem=("arbitrary",),
        args=(xc, *par_f, h_f, h_f, h0_f, dh_f, xc, *par_b, h_b, h_b, h0_b, dh_b), comms=comms)
    return (tuple(outs[0:6]), tuple(outs[6:12])), couts


def _decay_tables(lg, reverse):
    ci = lax.broadcasted_iota(jnp.int32, (CHUNK, CHUNK), 0).astype(F32)
    mi = lax.broadcasted_iota(jnp.int32, (CHUNK, CHUNK), 1).astype(F32)
    rel = (mi - ci) if reverse else (ci - mi)
    relc = jnp.maximum(rel, 0.0)
    lg_c = jnp.concatenate([lg] * (CHUNK // LANES), axis=1)
    dm = jnp.where(rel >= 0, jnp.exp(lg_c * relc), 0.0)
    cd = lax.broadcasted_iota(jnp.int32, (CHUNK, DH), 0).astype(F32)
    pq, ps = (CHUNK - cd, cd) if reverse else (cd + 1.0, CHUNK - 1.0 - cd)
    return relc, dm, jnp.exp(lg * pq), jnp.exp(lg * ps), jnp.exp(lg * float(CHUNK)), pq, ps


def _ret_fwd(proj, lgv, s0f, s0b, comms=()):
    t = proj.shape[0]
    n = t // CHUNK

    def one(q, k, v, lg, s_s, hh, o_ref, sp_ref, reverse):
        _, dm, wq, ws, g, _, _ = _decay_tables(lg, reverse)
        vb = v.astype(BF16)
        p = _dot_nt(q.astype(BF16), k.astype(BF16)) * dm
        s = s_s[hh]
        sp_ref[hh, 0] = s
        o_ref[:, DH * hh:DH * (hh + 1)] = _dot(p.astype(BF16), vb) + _dot((q * wq).astype(BF16), s.astype(BF16))
        s_s[hh] = g * s + _dot_tn((k * ws).astype(BF16), vb)

    def body(qf, kf, vf, qb, kb, vb, lg_ref, s0f_ref, s0b_ref, of_ref, ob_ref, spf_ref, spb_ref, sf_s, sb_s):
        @pl.when(pl.program_id(0) == 0)
        def _():
            sf_s[...] = s0f_ref[...]
            sb_s[...] = s0b_ref[...]
        for hh in range(HEADS):
            sl = slice(DH * hh, DH * (hh + 1))
            one(qf[:, sl].astype(F32), kf[:, sl].astype(F32), vf[:, sl], lg_ref[hh, 0:1, :], sf_s, hh, of_ref, spf_ref,
                False)
            one(qb[:, sl].astype(F32), kb[:, sl].astype(F32), vb[:, sl], lg_ref[hh, 1:2, :], sb_s, hh, ob_ref, spb_ref,
                True)

    blk = (CHUNK, RET_W)
    fw = [pl.BlockSpec(blk, lambda i, o=o: (i, o)) for o in range(3)]
    bw = [pl.BlockSpec(blk, lambda i, o=o: (n - 1 - i, o)) for o in range(3)]
    st = _full((HEADS, DH, DH))
    return _call(body, name="ret_fwd", grid=(n,),
                 in_specs=fw + bw + [_full((HEADS, 2, LANES)), st, st],
                 out_specs=[pl.BlockSpec(blk, lambda i: (i, 0)), pl.BlockSpec(blk, lambda i: (n - 1 - i, 0)),
                            pl.BlockSpec((HEADS, 1, DH, DH), lambda i: (0, i, 0, 0)),
                            pl.BlockSpec((HEADS, 1, DH, DH), lambda i: (0, n - 1 - i, 0, 0))],
                 out_shape=[_sds((t, RET_W)), _sds((t, RET_W)), _sds((HEADS, n, DH, DH)), _sds((HEADS, n, DH, DH))],
                 scratch_shapes=[pltpu.VMEM((HEADS, DH, DH), F32), pltpu.VMEM((HEADS, DH, DH), F32)],
                 sem=("arbitrary",), args=(proj, proj, proj, proj, proj, proj, lgv, s0f, s0b), comms=comms)


def _ret_bwd(proj, lgv, sgv, spf, spb, do, comms=()):
    t = proj.shape[0]
    n = t // CHUNK

    def one(q_ref, k_ref, v_ref, lg_ref, s_ref, do_ref, dq_ref, dk_ref, dv_ref, ds_s, acc_s, reverse):
        d = 1 if reverse else 0
        for hh in range(HEADS):
            sl = slice(DH * hh, DH * (hh + 1))
            relc, dm, wq, ws, g, pq, ps = _decay_tables(lg_ref[hh, d:d + 1, :], reverse)
            qb, kb, vb = q_ref[:, sl], k_ref[:, sl], v_ref[:, sl]
            q, k = qb.astype(F32), kb.astype(F32)
            p = _dot_nt(qb, kb) * dm
            s = s_ref[hh, 0]
            dob = do_ref[:, sl].astype(BF16)
            dsn = ds_s[hh]
            dsb = dsn.astype(BF16)
            dv_ref[:, sl] = (_dot_tn(p.astype(BF16), dob) + _dot((k * ws).astype(BF16), dsb)).astype(BF16)
            dp = _dot_nt(dob, vb)
            dab = (dp * dm).astype(BF16)
            xq = _dot_nt(dob, s.astype(BF16))
            yk = _dot_nt(vb, dsb)
            dq_ref[:, sl] = (_dot(dab, kb) + xq * wq).astype(BF16)
            dk_ref[:, sl] = (_dot_tn(dab, qb) + yk * ws).astype(BF16)
            ds_s[hh] = g * dsn + _dot_tn((q * wq).astype(BF16), dob)
            s_mask = _sum0(dp * p * relc)
            part = (sum(s_mask[:, LANES * u:LANES * (u + 1)] for u in range(CHUNK // LANES))
                    + _sum0(xq * q * wq * pq) + _sum0(yk * k * ws * ps) + _sum0(dsn * s) * g * float(CHUNK))
            acc_s[hh] += jnp.broadcast_to(part, (SUBLANES, LANES))

    def body(qf, kf, vf, qb, kb, vb, lg_ref, sg_ref, sf_ref, sb_ref, dof_ref, dob_ref,
             dqf, dkf, dvf, dqb, dkb, dvb, ds0f_ref, ds0b_ref, drdf_ref, drdb_ref, dsf_s, dsb_s, accf_s, accb_s):
        i = pl.program_id(0)

        @pl.when(i == 0)
        def _():
            for r in (dsf_s, dsb_s, accf_s, accb_s):
                r[...] = jnp.zeros_like(r)
        one(qf, kf, vf, lg_ref, sf_ref, dof_ref, dqf, dkf, dvf, dsf_s, accf_s, False)
        one(qb, kb, vb, lg_ref, sb_ref, dob_ref, dqb, dkb, dvb, dsb_s, accb_s, True)

        @pl.when(i == n - 1)
        def _():
            ds0f_ref[...] = dsf_s[...]
            ds0b_ref[...] = dsb_s[...]
            for d, (acc_s, drd_ref) in enumerate(((accf_s, drdf_ref), (accb_s, drdb_ref))):
                for hh in range(HEADS):
                    tot = jnp.sum(acc_s[hh, 0:1, :], axis=1, keepdims=True)
                    drd_ref[hh] = jnp.broadcast_to(tot, (SUBLANES, LANES)) * sg_ref[hh, d:d + 1, :]

    blk = (CHUNK, RET_W)
    fw = lambda o: pl.BlockSpec(blk, lambda i, o=o: (n - 1 - i, o))
    bw = lambda o: pl.BlockSpec(blk, lambda i, o=o: (i, o))
    lane = _full((HEADS, 2, LANES))
    st = _full((HEADS, DH, DH))
    rd = _full((HEADS, SUBLANES, LANES))
    outs, couts = _call(
        body, name="ret_bwd", grid=(n,),
        in_specs=[fw(0), fw(1), fw(2), bw(0), bw(1), bw(2), lane, lane,
                  pl.BlockSpec((HEADS, 1, DH, DH), lambda i: (0, n - 1 - i, 0, 0)),
                  pl.BlockSpec((HEADS, 1, DH, DH), lambda i: (0, i, 0, 0)), fw(0), bw(0)],
        out_specs=[fw(0), fw(0), fw(0), bw(0), bw(0), bw(0), st, st, rd, rd],
        out_shape=[_sds((t, RET_W), BF16)] * 6 + [_sds((HEADS, DH, DH))] * 2 + [_sds((HEADS, SUBLANES, LANES))] * 2,
        scratch_shapes=[pltpu.VMEM((HEADS, DH, DH), F32)] * 2 + [pltpu.VMEM((HEADS, SUBLANES, LANES), F32)] * 2,
        sem=("arbitrary",), args=(proj, proj, proj, proj, proj, proj, lgv, sgv, spf, spb, do, do), comms=comms)
    dqf, dkf, dvf, dqb, dkb, dvb, ds0f, ds0b, drdf, drdb = outs
    return ((dqf, dkf, dvf, ds0f, drdf), (dqb, dkb, dvb, ds0b, drdb)), couts


def _ctx_weights(lg, l_len, reverse):
    pos = lax.broadcasted_iota(jnp.int32, (l_len, DH), 0).astype(F32)
    steps = pos if reverse else (l_len - 1.0 - pos)
    return jnp.exp(lg * steps), steps


def _ctx_state_fwd(projc, lgv):
    l_len = projc.shape[0]

    def body(k_ref, v_ref, lg_ref, sf_ref, sb_ref):
        k = k_ref[...]
        vb = v_ref[...].astype(BF16)
        for d, o_ref in ((0, sf_ref), (1, sb_ref)):
            w, _ = _ctx_weights(lg_ref[0, d:d + 1, :], l_len, d == 1)
            o_ref[0] = _dot_tn((k * w).astype(BF16), vb)

    st = pl.BlockSpec((1, DH, DH), lambda h: (h, 0, 0))
    return _pc(body, name="ctx_state_fwd", grid=(HEADS,),
               in_specs=[pl.BlockSpec((l_len, DH), lambda h: (0, HEADS + h)),
                         pl.BlockSpec((l_len, DH), lambda h: (0, 2 * HEADS + h)),
                         pl.BlockSpec((1, 2, LANES), lambda h: (h, 0, 0))],
               out_specs=[st, st], out_shape=[_sds((HEADS, DH, DH))] * 2,
               compiler_params=_params("arbitrary"))(projc, projc, lgv)


def _ctx_state_bwd(projc, lgv, sgv, dsf, dsb):
    l_len = projc.shape[0]

    def body(k_ref, v_ref, lg_ref, sg_ref, dsf_ref, dsb_ref, dk_ref, dv_ref, drd_ref):
        k = k_ref[...]
        vb = v_ref[...].astype(BF16)
        dk = jnp.zeros((l_len, DH), F32)
        dv = jnp.zeros((l_len, DH), F32)
        rows = []
        for d, ds_ref in ((0, dsf_ref), (1, dsb_ref)):
            w, steps = _ctx_weights(lg_ref[0, d:d + 1, :], l_len, d == 1)
            dsb16 = ds_ref[0].astype(BF16)
            dkw = _dot_nt(vb, dsb16)
            dk = dk + dkw * w
            dv = dv + _dot((k * w).astype(BF16), dsb16)
            tot = jnp.sum(_sum0(dkw * k * w * steps), axis=1, keepdims=True)
            rows.append(jnp.broadcast_to(tot, (1, LANES)) * sg_ref[0, d:d + 1, :])
        dk_ref[...] = dk.astype(BF16)
        dv_ref[...] = dv.astype(BF16)
        rid = lax.broadcasted_iota(jnp.int32, (SUBLANES, LANES), 0)
        drd_ref[0] = jnp.where(rid == 0, rows[0], jnp.where(rid == 1, rows[1], 0.0))

    st = pl.BlockSpec((1, DH, DH), lambda h: (h, 0, 0))
    lane = pl.BlockSpec((1, 2, LANES), lambda h: (h, 0, 0))
    hc = pl.BlockSpec((l_len, DH), lambda h: (0, h))
    return _pc(body, name="ctx_state_bwd", grid=(HEADS,),
               in_specs=[pl.BlockSpec((l_len, DH), lambda h: (0, HEADS + h)),
                         pl.BlockSpec((l_len, DH), lambda h: (0, 2 * HEADS + h)), lane, lane, st, st],
               out_specs=[hc, hc, pl.BlockSpec((1, SUBLANES, LANES), lambda h: (h, 0, 0))],
               out_shape=[_sds((l_len, RET_W), BF16), _sds((l_len, RET_W), BF16), _sds((HEADS, SUBLANES, LANES))],
               compiler_params=_params("arbitrary"))(projc, projc, lgv, sgv, dsf, dsb)


G_BLOCK = (3 * RET_W) // RET_W
GATE_BLOCK = (4 * RET_W + LRU_W) // LRU_W


def _head_norm(y):
    yc = y - jnp.mean(y, axis=-1, keepdims=True)
    rs = lax.rsqrt(jnp.mean(yc * yc, axis=-1, keepdims=True) + EPS)
    return yc * rs, rs


def _gelu_parts(z):
    th = jnp.tanh(GELU_K * (z + GELU_C * z * z * z))
    return 0.5 * z * (1.0 + th), th


def _mix_fwd(o_f, o_b, proj, hf, hb, w_out, x, g1):
    t = x.shape[0]
    tm = _tile(t, True)

    def body(of_ref, ob_ref, g_ref, gt_ref, hf_ref, hb_ref, w_ref, x_ref, g1_ref, x1_ref, cat_ref):
        o = of_ref[...] + ob_ref[...]
        g = g_ref[...].astype(F32)
        for hh in range(HEADS):
            sl = slice(DH * hh, DH * (hh + 1))
            nrm, _ = _head_norm(o[:, sl])
            gh = g[:, sl]
            cat_ref[:, sl] = (gh * _sigmoid(gh) * nrm).astype(BF16)
        gel, _ = _gelu_parts(gt_ref[...].astype(F32))
        cat_ref[:, RET_W:] = ((hf_ref[...] + hb_ref[...]) * gel).astype(BF16)
        x1_ref[...] = x_ref[...] + g1_ref[...] * _dot(cat_ref[...], w_ref[...])

    half = pl.BlockSpec((tm, RET_W), lambda i: (i, 0))
    big = pl.BlockSpec((tm, D_MODEL), lambda i: (i, 0))
    return _pc(body, name="mix_fwd", grid=(t // tm,),
               in_specs=[half, half, pl.BlockSpec((tm, RET_W), lambda i: (i, G_BLOCK)),
                         pl.BlockSpec((tm, LRU_W), lambda i: (i, GATE_BLOCK)), half, half,
                         _full((D_MODEL, D_MODEL)), big, _full((1, D_MODEL))],
               out_specs=[big, big], out_shape=[_sds((t, D_MODEL)), _sds((t, D_MODEL), BF16)],
               compiler_params=_params("arbitrary"))(o_f, o_b, proj, proj, hf, hb, w_out, x, g1)


def _mix_bwd(o_f, o_b, proj, hf, hb, w_out, cat, dx1, g1, comms=()):
    t = dx1.shape[0]
    tm = _tile(t, True)

    def body(of_ref, ob_ref, g_ref, gt_ref, hf_ref, hb_ref, w_ref, cat_ref, dx1_ref, g1_ref,
             do_ref, dhs_ref, dg_ref, dgt_ref, dyb_ref, dg1_ref):
        dx1v = dx1_ref[...]
        y = _dot(cat_ref[...], w_ref[...])

        @pl.when(pl.program_id(0) == 0)
        def _():
            dg1_ref[...] = jnp.zeros_like(dg1_ref)
        dg1_ref[...] += _sum0(dx1v * y)
        dyb = (g1_ref[...] * dx1v).astype(BF16)
        dyb_ref[...] = dyb
        dcat = _dot_nt(dyb, w_ref[...])
        o = of_ref[...] + ob_ref[...]
        g = g_ref[...].astype(F32)
        for hh in range(HEADS):
            sl = slice(DH * hh, DH * (hh + 1))
            nrm, rs = _head_norm(o[:, sl])
            gh = g[:, sl]
            sg = _sigmoid(gh)
            dret = dcat[:, sl]
            dg_ref[:, sl] = (dret * nrm * (sg * (1.0 + gh * (1.0 - sg)))).astype(BF16)
            dn = dret * (gh * sg)
            dyc = rs * (dn - nrm * jnp.mean(dn * nrm, axis=-1, keepdims=True))
            do_ref[:, sl] = (dyc - jnp.mean(dyc, axis=-1, keepdims=True)).astype(BF16)
        z = gt_ref[...].astype(F32)
        gel, th = _gelu_parts(z)
        dlru = dcat[:, RET_W:]
        dhs_ref[...] = dlru * gel
        dgel = 0.5 * (1.0 + th) + 0.5 * z * (1.0 - th * th) * GELU_K * (1.0 + 3.0 * GELU_C * z * z)
        dgt_ref[...] = (dlru * (hf_ref[...] + hb_ref[...]) * dgel).astype(BF16)

    half = pl.BlockSpec((tm, RET_W), lambda i: (i, 0))
    big = pl.BlockSpec((tm, D_MODEL), lambda i: (i, 0))
    return _call(body, name="mix_bwd", grid=(t // tm,),
                 in_specs=[half, half, pl.BlockSpec((tm, RET_W), lambda i: (i, G_BLOCK)),
                           pl.BlockSpec((tm, LRU_W), lambda i: (i, GATE_BLOCK)), half, half,
                           _full((D_MODEL, D_MODEL)), big, big, _full((1, D_MODEL))],
                 out_specs=[half, half, half, half, big, _full((1, D_MODEL))],
                 out_shape=[_sds((t, RET_W), BF16), _sds((t, RET_W)), _sds((t, RET_W), BF16), _sds((t, RET_W), BF16),
                            _sds((t, D_MODEL), BF16), _sds((1, D_MODEL))],
                 scratch_shapes=[], sem=("arbitrary",), args=(o_f, o_b, proj, proj, hf, hb, w_out, cat, dx1, g1),
                 comms=comms)


def _mlp(x1, n2g, sh2, sc2, g2, fg, w1_parts, w2_parts, tgt):
    t = x1.shape[0]
    tm = _tile(t)
    hb_ = MLP_H // N_CHIP
    q_rows = hb_ // 4
    n_cp = 4 * N_DEV

    def body(x1_ref, n2g_ref, sh2_ref, sc2_ref, g2_ref, fg_ref, w1a, w1b, w2a, w2b, tgt_ref,
             dx1_ref, h2b_ref, ab_ref, dub_ref, dmb_ref, dsc_ref, dsh_ref, dg2_ref, dn2_ref, dfg_ref, loss_ref,
             w1_s, w2_s, r_s, sems):
        @pl.when(pl.program_id(0) == 0)
        def _():
            cps = []
            for p, parts in enumerate(((w1a, w2a), (w1b, w2b))):
                for d in range(N_DEV):
                    rows = pl.ds(2 * q_rows * (d % 2) + q_rows * p, q_rows)
                    for src, dst in zip(parts, (w1_s, w2_s)):
                        cps.append(pltpu.make_async_copy(src.at[d], dst.at[d // 2, rows], sems.at[len(cps)]))
            for cp in cps:
                cp.start()
            for r in (dsc_ref, dsh_ref, dg2_ref, dn2_ref, dfg_ref, loss_ref):
                r[...] = jnp.zeros_like(r)
            for cp in cps:
                cp.wait()
        x1v = x1_ref[...]
        n2g, sc2, g2, fg = n2g_ref[...], sc2_ref[...], g2_ref[...], fg_ref[...]
        xh, _ = _rms(x1v)
        h2b = (xh * n2g * (1.0 + sc2) + sh2_ref[...]).astype(BF16)
        h2b_ref[...] = h2b
        m = jnp.zeros((tm, D_MODEL), F32)
        for j in range(N_CHIP):
            sl = slice(hb_ * j, hb_ * (j + 1))
            r = jnp.maximum(_dot(h2b, w1_s[j]), 0.0)
            r_s[:, sl] = r
            ab = (r * r).astype(BF16)
            ab_ref[:, sl] = ab
            m = m + _dot(ab, w2_s[j])
        x2 = x1v + g2 * m
        x2h, r2 = _rms(x2)
        err = x2h * fg - tgt_ref[...]
        loss_ref[...] += _sum0(err * err)
        dout = err * (1.0 / D_MODEL)
        dfg_ref[...] += _sum0(dout * x2h)
        dxh = dout * fg
        dx2 = r2 * (dxh - x2h * jnp.mean(dxh * x2h, axis=-1, keepdims=True))
        dg2_ref[...] += _sum0(dx2 * m)
        dmb = (g2 * dx2).astype(BF16)
        dmb_ref[...] = dmb
        dh2 = jnp.zeros((tm, D_MODEL), F32)
        for j in range(N_CHIP):
            sl = slice(hb_ * j, hb_ * (j + 1))
            dub = (_dot_nt(dmb, w2_s[j]) * (2.0 * r_s[:, sl])).astype(BF16)
            dub_ref[:, sl] = dub
            dh2 = dh2 + _dot_nt(dub, w1_s[j])
        dx, dn2_t, dsh_t, dsc_t = _norm_mod_bwd(x1v, n2g, sc2, dh2)
        dx1_ref[...] = dx2 + dx
        dn2_ref[...] += dn2_t
        dsh_ref[...] += dsh_t
        dsc_ref[...] += dsc_t

        @pl.when(pl.program_id(0) == t // tm - 1)
        def _():
            tot = jnp.sum(loss_ref[...], axis=1, keepdims=True) * (0.5 / D_MODEL)
            loss_ref[...] = jnp.broadcast_to(tot, loss_ref.shape)

    row = _full((1, D_MODEL))
    big = pl.BlockSpec((tm, D_MODEL), lambda i: (i, 0))
    wide = pl.BlockSpec((tm, MLP_H), lambda i: (i, 0))
    return _pc(body, name="mlp", grid=(t // tm,),
               in_specs=[big, row, row, row, row, row, ANY, ANY, ANY, ANY, big],
               out_specs=[big, big, wide, wide, big, row, row, row, row, row, row],
               out_shape=[_sds((t, D_MODEL)), _sds((t, D_MODEL), BF16), _sds((t, MLP_H), BF16), _sds((t, MLP_H), BF16),
                          _sds((t, D_MODEL), BF16)] + [_sds((1, D_MODEL))] * 6,
               scratch_shapes=[pltpu.VMEM((N_CHIP, D_MODEL, hb_), BF16), pltpu.VMEM((N_CHIP, hb_, D_MODEL), BF16),
                               pltpu.VMEM((tm, MLP_H), F32), pltpu.SemaphoreType.DMA((n_cp,))],
               compiler_params=_params("arbitrary"))(x1, n2g, sh2, sc2, g2, fg, *w1_parts, *w2_parts, tgt)


def _tn(a, b, nj, a_blocked, b_blocked, name, extra=None, comms=()):
    t = a.shape[0]
    m = a.shape[1] // (nj if a_blocked else 1)
    n = b.shape[1] // (nj if b_blocked else 1)
    bk = next((b for b in (2048, 1024, 512) if t % b == 0), t)
    nk = t // bk
    a_col = (lambda j: j) if a_blocked else (lambda j: 0)
    b_col = (lambda j: j) if b_blocked else (lambda j: 0)
    in_specs = [pl.BlockSpec((bk, m), lambda j, k: (k, a_col(j))), pl.BlockSpec((bk, n), lambda j, k: (k, b_col(j)))]
    args = [a, b]
    if extra is not None:
        a2, b2 = extra
        t2 = a2.shape[0]
        in_specs += [pl.BlockSpec((t2, m), lambda j, k: (0, a_col(j))),
                     pl.BlockSpec((t2, n), lambda j, k: (0, b_col(j)))]
        args += [a2, b2]

    def body(*refs):
        a_ref, b_ref = refs[0], refs[1]
        o_ref, acc = refs[-2], refs[-1]
        k = pl.program_id(1)

        @pl.when(k == 0)
        def _():
            acc[...] = jnp.zeros_like(acc)
        acc[...] += _dot_tn(a_ref[...].astype(BF16), b_ref[...].astype(BF16))

        @pl.when(k == nk - 1)
        def _():
            if extra is not None:
                acc[...] += _dot_tn(refs[2][...].astype(BF16), refs[3][...].astype(BF16))
            o_ref[0] = acc[...]

    (out,), couts = _call(body, name=name, grid=(nj, nk), in_specs=in_specs,
                          out_specs=[pl.BlockSpec((1, m, n), lambda j, k: (j, 0, 0))], out_shape=[_sds((nj, m, n))],
                          scratch_shapes=[pltpu.VMEM((m, n), F32)], sem=("arbitrary", "arbitrary"), args=args,
                          comms=comms)
    return (out, couts) if comms else out


ROW_LOSS = 0
ROW_DMOD = 1
ROW_DMODC = 7
ROW_N1, ROW_N2, ROW_FG, ROW_CB = 9, 10, 11, 12
ROW_BA, ROW_BX, ROW_LAM = 13, 15, 17
ROW_CW = 20
ROW_RD = 24
SLAB_ROWS = 32
SEG = D_MODEL // 2


def _pack_small(rows, drd, cw2, cb2, lru2, gates):
    n_rows, n_lru = len(rows), len(lru2)

    def body(*refs):
        r = refs[:n_rows]
        drd_f, drd_b, drd_c, cw_a, cw_b, cb_a, cb_b = refs[n_rows:n_rows + 7]
        lru = refs[n_rows + 7:n_rows + 7 + n_lru]
        gf_ref, gb_ref, slab, ga, gx = refs[n_rows + 7 + n_lru:]
        slab[...] = jnp.zeros_like(slab)
        slab[ROW_LOSS:ROW_LOSS + 1, :] = r[0][...]
        for k in range(N_MOD):
            slab[ROW_DMOD + k:ROW_DMOD + k + 1, :] = r[1 + k][...]
        slab[ROW_DMODC:ROW_DMODC + 1, :] = r[7][...]
        slab[ROW_DMODC + 1:ROW_DMODC + 2, :] = r[8][...]
        slab[ROW_N1:ROW_N1 + 1, :] = r[9][...] + r[10][...]
        slab[ROW_N2:ROW_N2 + 1, :] = r[11][...]
        slab[ROW_FG:ROW_FG + 1, :] = r[12][...]
        slab[ROW_CB:ROW_CB + 1, 0:LRU_W] = cb_a[...] + cb_b[...]
        for k, row in enumerate((ROW_BA, ROW_BA + 1, ROW_BX, ROW_BX + 1, ROW_LAM, ROW_LAM + 1)):
            slab[row:row + 1, 0:LRU_W] = lru[2 * k][...] + lru[2 * k + 1][...]
        slab[ROW_CW:ROW_CW + 4, 0:LRU_W] = cw_a[...] + cw_b[...]
        for h in range(HEADS):
            slab[ROW_RD + h:ROW_RD + h + 1, 0:LANES] = drd_f[h, 0:1, :] + drd_c[h, 0:1, :]
            slab[ROW_RD + HEADS + h:ROW_RD + HEADS + h + 1, 0:LANES] = drd_b[h, 0:1, :] + drd_c[h, 1:2, :]
        for d, g_ref in enumerate((gf_ref, gb_ref)):
            for n in range(LRU_BLOCKS):
                blk = slice(LRU_BD * n, LRU_BD * (n + 1))
                ga[blk, LRU_BD * d:LRU_BD * (d + 1)] = g_ref[0, blk, blk].astype(BF16)
                gx[blk, LRU_BD * d:LRU_BD * (d + 1)] = g_ref[1, blk, blk].astype(BF16)

    args = list(rows) + list(drd) + list(cw2) + list(cb2) + list(lru2) + list(gates)
    gate_shape = (LRU_W, 2 * LRU_BD)
    return _pc(body, name="pack_small", in_specs=[_full(a.shape) for a in args],
               out_specs=[_full((SLAB_ROWS, D_MODEL)), _full(gate_shape), _full(gate_shape)],
               out_shape=[_sds((SLAB_ROWS, D_MODEL)), _sds(gate_shape, BF16), _sds(gate_shape, BF16)],
               compiler_params=_params())(*args)


def _adam_math(w, g, m, v):
    mn = ADAM_B1 * m + (1.0 - ADAM_B1) * g
    vn = ADAM_B2 * v + (1.0 - ADAM_B2) * (g * g)
    mh = mn / (1.0 - ADAM_B1 ** ADAM_STEP)
    vh = vn / (1.0 - ADAM_B2 ** ADAM_STEP)
    return -ADAM_LR * (mh / (jnp.sqrt(vh) + ADAM_EPS) + ADAM_WD * w), mn, vn


SMALL_PARAMS = ("b_ada", "norm1_g", "norm2_g", "final_g", "ret_decay", "conv_w", "conv_b", "lru_wa", "lru_ba", "lru_wx",
                "lru_bx", "lru_lambda")


def _finalize_small(chip_idx, slab_all, ga_all, gx_all, wmv):
    n_p = len(SMALL_PARAMS)
    flat = [a for nm in SMALL_PARAMS for a in wmv[nm]]
    ada_n = N_MOD * D_MODEL // N_CHIP

    def body(c_ref, slab_ref, ga_ref, gx_ref, *refs):
        prm = {nm: refs[3 * k:3 * k + 3] for k, nm in enumerate(SMALL_PARAMS)}
        outs = {nm: refs[3 * n_p + 4 * k:3 * n_p + 4 * k + 4] for k, nm in enumerate(SMALL_PARAMS)}
        b128_ref, dmc_ref, loss_ref = refs[3 * n_p + 4 * n_p:]
        chip = c_ref[0]

        def pick(fn):
            acc = fn(0)
            for j in range(1, N_CHIP):
                acc = jnp.where(chip == j, fn(j), acc)
            return acc

        tot = slab_ref[0]
        for d in range(1, N_DEV):
            tot = tot + slab_ref[d]

        def update(nm, g, sl=None, rows=None):
            w_ref, m_ref, v_ref = prm[nm]
            g_ref, d_ref, mo_ref, vo_ref = outs[nm]
            ix = (slice(None) if rows is None else rows, slice(None) if sl is None else sl)
            dl, mn, vn = _adam_math(w_ref[ix], g, m_ref[ix], v_ref[ix])
            g_ref[ix] = g
            d_ref[ix] = dl
            mo_ref[ix] = mn
            vo_ref[ix] = vn

        loss_ref[...] = jnp.broadcast_to(tot[ROW_LOSS:ROW_LOSS + 1, 0:LANES], (SUBLANES, LANES))
        for k in range(N_MOD):
            g = tot[ROW_DMOD + k:ROW_DMOD + k + 1, :]
            if k < 2:
                g = g + tot[ROW_DMODC + k:ROW_DMODC + k + 1, :]
            update("b_ada", g, slice(D_MODEL * k, D_MODEL * (k + 1)))
        update("norm1_g", tot[ROW_N1:ROW_N1 + 1, :])
        update("norm2_g", tot[ROW_N2:ROW_N2 + 1, :])
        update("final_g", tot[ROW_FG:ROW_FG + 1, :])
        update("ret_decay", tot[ROW_RD:ROW_RD + SUBLANES, 0:LANES])
        update("conv_b", tot[ROW_CB:ROW_CB + 1, 0:LRU_W])
        update("conv_w", pick(lambda j: tot[ROW_CW:ROW_CW + 4, LANES * j:LANES * (j + 1)]))
        for nm, row in (("lru_ba", ROW_BA), ("lru_bx", ROW_BX), ("lru_lambda", ROW_LAM)):
            update(nm, pick(lambda j, row=row: tot[row:row + 2, LANES * j:LANES * (j + 1)]))
        for nm, g_all in (("lru_wa", ga_ref), ("lru_wx", gx_ref)):
            for dr in range(2):
                lanes = slice(LRU_BD * dr, LRU_BD * (dr + 1))
                g = g_all[0, :, lanes].astype(F32)
                for d in range(1, N_DEV):
                    g = g + g_all[d, :, lanes].astype(F32)
                update(nm, g, rows=slice(LRU_W * dr, LRU_W * (dr + 1)))

        def seg(rows6, s):
            return rows6[s // 2][:, SEG * (s % 2):SEG * (s % 2 + 1)]

        b128_ref[...] = jnp.zeros_like(b128_ref)
        dmc_ref[...] = jnp.zeros_like(dmc_ref)
        zero = jnp.zeros((1, D_MODEL), F32)
        ctx6 = [tot[ROW_DMODC:ROW_DMODC + 1, :], tot[ROW_DMODC + 1:ROW_DMODC + 2, :]] + [zero] * (N_MOD - 2)
        for q in range(ada_n // SEG):
            cols = slice(SEG * q, SEG * (q + 1))
            for d in range(N_DEV):
                rows6 = [slab_ref[d, ROW_DMOD + k:ROW_DMOD + k + 1, :] for k in range(N_MOD)]
                b128_ref[d:d + 1, cols] = pick(lambda j, rows6=rows6: seg(rows6, 3 * j + q))
            c = pick(lambda j: seg(ctx6, 3 * j + q))
            b128_ref[N_DEV:N_DEV + 1, cols] = c
            dmc_ref[0:1, cols] = c

    out_shape = []
    for nm in SMALL_PARAMS:
        out_shape += [_sds(wmv[nm][0].shape)] * 4
    out_shape += [_sds((LANES, ada_n)), _sds((SUBLANES, ada_n)), _sds((SUBLANES, LANES))]
    args = [slab_all, ga_all, gx_all] + flat
    grid_spec = pltpu.PrefetchScalarGridSpec(
        num_scalar_prefetch=1, grid=(1,), in_specs=[_full(a.shape) for a in args],
        out_specs=[_full(s.shape) for s in out_shape])
    outs = _pc(body, name="finalize_small", grid_spec=grid_spec, out_shape=out_shape,
               compiler_params=_params("arbitrary"))(chip_idx, *args)
    res = {nm: tuple(outs[4 * k:4 * k + 4]) for k, nm in enumerate(SMALL_PARAMS)}
    return res, outs[4 * n_p], outs[4 * n_p + 1], outs[4 * n_p + 2]


def _block_diag(w):
    eye = jnp.eye(LRU_BLOCKS, dtype=F32)
    return (w[:, :, None, :] * eye[:, None, :, None]).reshape(LRU_W, LRU_W).astype(BF16)


def _lane_rep(v8):
    return jnp.broadcast_to(v8.reshape(SUBLANES, 1), (SUBLANES, LANES))


def kernel(x, c, ctx, c_ctx, w_ada, b_ada, norm1_g, norm2_g, w_in, ret_decay, conv_w, conv_b, lru_wa, lru_ba, lru_wx, lru_bx, lru_lambda, w_out, w_mlp1, w_mlp2, final_g, loss_target, m_c_ctx, m_w_ada, m_b_ada, m_norm1_g, m_norm2_g, m_w_in, m_ret_decay, m_conv_w, m_conv_b, m_lru_wa, m_lru_ba, m_lru_wx, m_lru_bx, m_lru_lambda, m_w_out, m_w_mlp1, m_w_mlp2, m_final_g, v_c_ctx, v_w_ada, v_b_ada, v_norm1_g, v_norm2_g, v_w_in, v_ret_decay, v_conv_w, v_conv_b, v_lru_wa, v_lru_ba, v_lru_wx, v_lru_bx, v_lru_lambda, v_w_out, v_w_mlp1, v_w_mlp2, v_final_g):
    ax, ay, ac = lax.axis_index("x"), lax.axis_index("y"), lax.axis_index("c")
    chip = 2 * ax + ay
    dev = 4 * ax + 2 * ay + ac
    c_idx = jnp.stack([ac, chip]).astype(jnp.int32)
    j_idx = chip.reshape(1).astype(jnp.int32)

    xt = x[0]
    t_len = xt.shape[0]
    ctxt = ctx[0]
    l_len = ctxt.shape[0]
    tgt = loss_target[0]
    ada_n = w_ada.shape[2]

    def my_half(w2d):
        r = w2d.shape[0] // 2
        return lax.dynamic_slice_in_dim(w2d, ac * r, r, axis=0).astype(BF16)

    pad8 = lambda a: jnp.pad(a, ((0, SUBLANES - a.shape[0]), (0, 0)))
    small = jnp.concatenate([pad8(conv_w[0]), pad8(lru_ba[0]), pad8(lru_bx[0]), pad8(lru_lambda[0])], axis=0)
    b_shard = lax.dynamic_slice_in_dim(b_ada, chip * ada_n, ada_n, axis=1)
    gw_in, _, small_all, a16, mod_parts, lgv, sgv = _head(
        my_half(w_in[0]), pad8(c), small, w_ada[0], b_shard, c_ctx, ret_decay[0])
    w4 = gw_in.reshape(N_CHIP, D_MODEL, IN_COLS // N_CHIP)

    mod_all = mod_parts[0::2].transpose(1, 0, 2).reshape(16, N_CHIP * ada_n)
    mod_me = lax.dynamic_slice_in_dim(mod_all, dev, 1, axis=0)
    sh1, sc1, g1, sh2, sc2, g2 = [mod_me[:, D_MODEL * k:D_MODEL * (k + 1)] for k in range(N_MOD)]
    csh1, csc1 = mod_all[8:9, 0:D_MODEL], mod_all[8:9, D_MODEL:2 * D_MODEL]

    cos2, sin2 = _rotary_tables(t_len)
    cos_c, sin_c = jnp.ones((l_len, DH), F32), jnp.zeros((l_len, DH), F32)
    n1g, n2g = norm1_g, norm2_g
    fg = final_g.reshape(1, D_MODEL)

    small_full = small_all[0::2].transpose(1, 0, 2).reshape(4 * SUBLANES, LRU_W)
    cw = small_full[0:4]
    cb = conv_b
    ba_f, ba_b = small_full[8:9], small_full[9:10]
    bx_f, bx_b = small_full[16:17], small_full[17:18]
    lam_f, lam_b = small_full[24:25], small_full[25:26]
    wa_f, wa_b = _block_diag(lru_wa[0, 0]), _block_diag(lru_wa[0, 1])
    wx_f, wx_b = _block_diag(lru_wx[0, 0]), _block_diag(lru_wx[0, 1])
    zero_h = jnp.zeros((1, LRU_W), F32)

    projc, xrc, hcb16 = _inproj_fwd(ctxt, n1g, csh1, csc1, w4, cos_c, sin_c, "inproj_fwd_ctx")
    s_f, s_b = _ctx_state_fwd(projc, lgv)
    xcc = _conv_fwd(xrc, cw, cb, "conv_fwd_ctx")
    par_f, par_b = (wa_f, wx_f, ba_f, bx_f, lam_f), (wa_b, wx_b, ba_b, bx_b, lam_b)
    hcf, hcbk = _lru_fwd(xcc, par_f, par_b, zero_h, zero_h, "lru_fwd_ctx")
    lru_sf, lru_sb = hcf[l_len - 1:l_len], hcbk[0:1]

    h1, h2 = my_half(w_mlp1[0]), my_half(w_mlp2[0])
    q = h1.shape[0] // 2
    (proj, xrl, hb16), ((gw_1a,),) = _inproj_fwd(xt, n1g, sh1, sc1, w4, cos2, sin2, "inproj_fwd",
                                           comms=(_AllGather([h1[:q]]),))
    (o_f, o_b, spf, spb), ((gw_1b, gw_out),) = _ret_fwd(proj, lgv, s_f, s_b,
                                                       comms=(_AllGather([h1[q:], my_half(w_out[0])]),))
    xcl = _conv_fwd(xrl, cw, cb, "conv_fwd")
    (hf, hbk), ((gw_2a, gw_2b),) = _lru_fwd(xcl, par_f, par_b, lru_sf, lru_sb, "lru_fwd",
                                           comms=(_AllGather([h2[:q], h2[q:]]),))
    wo = gw_out.reshape(D_MODEL, D_MODEL)
    x1, cat = _mix_fwd(o_f, o_b, proj, hf, hbk, wo, xt, g1)

    (dx1, h2b, ab, dub, dmb, dsc2, dsh2, dg2, dn2g, dfg, lossv) = _mlp(
        x1, n2g, sh2, sc2, g2, fg, (gw_1a, gw_1b), (gw_2a, gw_2b), tgt)
    gw_mlp1 = _tn(h2b, dub, N_CHIP, False, True, "grad_w_mlp1")
    b_1 = gw_mlp1.reshape(N_DEV, D_MODEL // 2, MLP_H // N_CHIP)
    gw_mlp2, ((r_1,),) = _tn(ab, dmb, N_CHIP, True, False, "grad_w_mlp2", comms=(_pair_exchange([b_1]),))

    half = D_MODEL // 4
    top, bot = (0, half), (half, half)
    b_2 = gw_mlp2.reshape(N_DEV, MLP_H // N_DEV, D_MODEL)
    p_1, pb_1 = _pair_add(b_1, r_1, c_idx, "rs_pair_add_w_mlp1")
    (do, dhs, dg, dgate, dyb, dg1), ((q_1a,), (r_2,)) = _mix_bwd(
        o_f, o_b, proj, hf, hbk, wo, cat, dx1, g1, comms=(_chip_exchange([pb_1], top), _pair_exchange([b_2])))
    gw_o = _tn(cat, dyb, 1, False, False, "grad_w_out")
    b_o = gw_o.reshape(N_DEV, D_MODEL // N_DEV, D_MODEL)
    p_2, pb_2 = _pair_add(b_2, r_2, c_idx, "rs_pair_add_w_mlp2")

    ((dq_f, dk_f, dv_f, ds_f, drd_f), (dq_b, dk_b, dv_b, ds_b, drd_b)), ((q_1b,), (q_2a,), (r_o,)) = _ret_bwd(
        proj, lgv, sgv, spf, spb, do,
        comms=(_chip_exchange([pb_1], bot), _chip_exchange([pb_2], top), _pair_exchange([b_o])))
    p_o, pb_o = _pair_add(b_o, r_o, c_idx, "rs_pair_add_w_out")
    h_1 = _chip_add(p_1, (q_1a, q_1b), c_idx, "rs_chip_add_w_mlp1")

    ((dxc_f, dpre_f, dba_f, dbx_f, dlam_f, dh0_f), (dxc_b, dpre_b, dba_b, dbx_b, dlam_b, dh0_b)), (
        (q_2b,), (q_o,), (f_1,)) = _lru_bwd(
        xcl, par_f, par_b, hf, hbk, lru_sf, lru_sb, dhs, dhs, "lru_bwd",
        comms=(_chip_exchange([pb_2], bot), _chip_exchange([pb_o]), _pair_gather([h_1])))
    h_2 = _chip_add(p_2, (q_2a, q_2b), c_idx, "rs_chip_add_w_mlp2")
    h_o = _chip_add(p_o, (q_o,), c_idx, "rs_chip_add_w_out")
    dxr, dcw, dcb = _conv_bwd(dxc_f, dxc_b, xrl, cw, "conv_bwd")
    grad_x, dpb, dn1g, dsh1, dsc1 = _inproj_bwd(
        xt, n1g, sh1, sc1, w4, cos2, sin2, [dq_f, dq_b, dk_f, dk_b, dv_f, dv_b, dg, dxr, dgate], dx1, "inproj_bwd")

    dkc, dvc, drd_c = _ctx_state_bwd(projc, lgv, sgv, ds_f, ds_b)
    zc = jnp.zeros((l_len, LRU_W), F32)
    dhc_f = lax.dynamic_update_slice(zc, dh0_f, (l_len - 1, 0))
    dhc_b = lax.dynamic_update_slice(zc, dh0_b, (0, 0))
    ((dxcc_f, dprec_f, dbac_f, dbxc_f, dlamc_f, _), (dxcc_b, dprec_b, dbac_b, dbxc_b, dlamc_b, _)), _ = _lru_bwd(
        xcc, par_f, par_b, hcf, hcbk, zero_h, zero_h, dhc_f, dhc_b, "lru_bwd_ctx")
    dxrc, dcw_c, dcb_c = _conv_bwd(dxcc_f, dxcc_b, xrc, cw, "conv_bwd_ctx")
    zr = jnp.zeros((l_len, RET_W), BF16)
    _, dpbc, dn1g_c, dcsh1, dcsc1 = _inproj_bwd(
        ctxt, n1g, csh1, csc1, w4, cos_c, sin_c, [zr, zr, dkc, zr, dvc, zr, zr, dxrc, zr],
        jnp.zeros((l_len, D_MODEL), F32), "inproj_bwd_ctx")

    gw_i = _tn(hb16, dpb, N_CHIP, False, True, "grad_w_in", extra=(hcb16, dpbc))
    b_i = gw_i.reshape(N_DEV, D_MODEL // 2, IN_COLS // N_CHIP)
    gwa_f, ((r_i,), (f_2,), (f_o,)) = _tn(xcl, dpre_f, 2, False, True, "grad_lru_gates_f", extra=(xcc, dprec_f),
                                          comms=(_pair_exchange([b_i]), _pair_gather([h_2]), _pair_gather([h_o])))
    p_i, pb_i = _pair_add(b_i, r_i, c_idx, "rs_pair_add_w_in")
    gwa_b, ((q_i,),) = _tn(xcl, dpre_b, 2, False, True, "grad_lru_gates_b", extra=(xcc, dprec_b),
                           comms=(_chip_exchange([pb_i]),))
    slab, ga, gx = _pack_small(
        [lossv, dsh1, dsc1, dg1, dsh2, dsc2, dg2, dcsh1, dcsc1, dn1g, dn1g_c, dn2g, dfg],
        (drd_f, drd_b, drd_c), (dcw, dcw_c), (dcb, dcb_c),
        (dba_f, dbac_f, dba_b, dbac_b, dbx_f, dbxc_f, dbx_b, dbxc_b, dlam_f, dlamc_f, dlam_b, dlamc_b),
        (gwa_f, gwa_b))
    (f_i,), (slab_all, ga_all, gx_all) = _run_comms(
        [_pair_gather([_chip_add(p_i, (q_i,), c_idx, "rs_chip_add_w_in")]), _AllGather([slab, ga, gx])],
        "tail_exchanges")
    g_in, g_out, g_1, g_2 = _shard_of(f_i), _shard_of(f_o), _shard_of(f_1), _shard_of(f_2)
    big = {}
    for nm, w, g, m, v in (("w_in", w_in, g_in, m_w_in, v_w_in), ("w_out", w_out, g_out, m_w_out, v_w_out),
                           ("w_mlp1", w_mlp1, g_1, m_w_mlp1, v_w_mlp1), ("w_mlp2", w_mlp2, g_2, m_w_mlp2, v_w_mlp2)):
        go, d_, mn, vn = _adamw(w[0], g, m[0], v[0], "adamw_" + nm)
        big[nm] = (go[None], d_[None], mn[None], vn[None])
    params = {
        "b_ada": (b_ada, m_b_ada, v_b_ada), "norm1_g": (norm1_g, m_norm1_g, v_norm1_g),
        "norm2_g": (norm2_g, m_norm2_g, v_norm2_g), "final_g": (final_g, m_final_g, v_final_g),
        "ret_decay": (ret_decay, m_ret_decay, v_ret_decay), "conv_w": (conv_w, m_conv_w, v_conv_w),
        "conv_b": (conv_b, m_conv_b, v_conv_b), "lru_wa": (lru_wa, m_lru_wa, v_lru_wa),
        "lru_ba": (lru_ba, m_lru_ba, v_lru_ba), "lru_wx": (lru_wx, m_lru_wx, v_lru_wx),
        "lru_bx": (lru_bx, m_lru_bx, v_lru_bx), "lru_lambda": (lru_lambda, m_lru_lambda, v_lru_lambda),
    }
    as2d = {
        "b_ada": lambda a: a, "norm1_g": lambda a: a, "norm2_g": lambda a: a, "conv_b": lambda a: a,
        "final_g": lambda a: a.reshape(1, D_MODEL), "ret_decay": lambda a: _lane_rep(a.reshape(-1)),
        "conv_w": lambda a: a[0], "lru_ba": lambda a: a[0], "lru_bx": lambda a: a[0], "lru_lambda": lambda a: a[0],
        "lru_wa": lambda a: a.reshape(2 * LRU_W, LRU_BD), "lru_wx": lambda a: a.reshape(2 * LRU_W, LRU_BD),
    }
    res, b128, dmc8, loss8 = _finalize_small(
        j_idx, slab_all, ga_all, gx_all, {nm: tuple(as2d[nm](a) for a in params[nm]) for nm in SMALL_PARAMS})
    loss = loss8[0, 0]
    small_out = {}
    for nm in SMALL_PARAMS:
        shp = params[nm][0].shape
        if nm == "ret_decay":
            small_out[nm] = tuple(o[:, 0].reshape(shp) for o in res[nm])
        else:
            small_out[nm] = tuple(o.reshape(shp) for o in res[nm])

    g_ada = _ada_grad(jnp.pad(a16.T, ((0, 0), (0, LANES - 16))), b128)
    g_ada, d_ada, m_ada, v_ada = _adamw(w_ada[0], g_ada, m_w_ada[0], v_w_ada[0], "adamw_w_ada")

    (cparts,) = _all_gather([_cctx_partial(dmc8, w_ada[0])], "gather_cctx")
    g_cc, d_cc, m_cc, v_cc = _cctx_final(cparts, c_ctx, m_c_ctx, v_c_ctx)
    small_out["c_ctx"] = tuple(a.reshape(D_MODEL) for a in (g_cc, d_cc, m_cc, v_cc))
    small_out["w_ada"] = (g_ada[None], d_ada[None], m_ada[None], v_ada[None])
    small_out.update(big)

    order = ["c_ctx", "w_ada", "b_ada", "norm1_g", "norm2_g", "w_in", "ret_decay", "conv_w", "conv_b", "lru_wa", "lru_ba",
             "lru_wx", "lru_bx", "lru_lambda", "w_out", "w_mlp1", "w_mlp2", "final_g"]
    outs = [loss, grad_x[None]]
    for k in range(4):
        outs += [small_out[nm][k] for nm in order]
    return tuple(outs)
```

```python
import math

import jax
import jax.numpy as jnp
from jax import lax
from jax.experimental import pallas as pl
from jax.experimental.pallas import tpu as pltpu

F32 = jnp.float32
BF16 = jnp.bfloat16

D_MODEL = 1024
HEADS = 4
DH = 128
CHUNK = 256
RET_W = HEADS * DH
LRU_W = 512
LRU_BLOCKS = 8
LRU_BD = LRU_W // LRU_BLOCKS
LRU_C = 8.0
IN_COLS = 4 * RET_W + 2 * LRU_W
MLP_H = 4 * D_MODEL
N_MOD = 6
GRID_W = 64
ROPE_BASE = 10000.0
K_SCALE = DH ** -0.5
EPS = 1e-6
GELU_K = math.sqrt(2.0 / math.pi)
GELU_C = 0.044715

ADAM_LR = 0.001
ADAM_B1 = 0.9
ADAM_B2 = 0.999
ADAM_EPS = 1e-08
ADAM_WD = 0.01
ADAM_STEP = 10

N_DEV = 8
N_CHIP = 4
SUBLANES = 8
LANES = 128
VMEM_LIMIT_V7X = 56 * 1024 * 1024
MESH = pl.DeviceIdType.MESH
ANY = pl.BlockSpec(memory_space=pl.ANY)


def _pc(body, **kw):
    return pl.pallas_call(body, **kw)


def _params(*sem):
    return pltpu.CompilerParams(dimension_semantics=sem if sem else None, vmem_limit_bytes=VMEM_LIMIT_V7X)


def _tile(t, big=False):
    if big and t >= 1024:
        return 512
    return 256 if t >= 256 else t


def _sds(shape, dtype=F32):
    return jax.ShapeDtypeStruct(tuple(shape), dtype)


def _full(shape):
    nd = len(shape)
    return pl.BlockSpec(tuple(shape), lambda *_: (0,) * nd)


def _sigmoid(x):
    return 0.5 * jnp.tanh(0.5 * x) + 0.5


def _log1p_pos(y):
    s = y * (1.0 - y * (0.5 - y * (1.0 / 3.0 - y * (0.25 - y * (0.2 - y / 6.0)))))
    return jnp.where(y < 0.03, s, jnp.log(1.0 + y))


def _softplus(z):
    return jnp.maximum(z, 0.0) + _log1p_pos(jnp.exp(-jnp.abs(z)))


def _one_minus_sq(la, a):
    t = la * (1.0 + la * (0.5 + la * (1.0 / 6.0 + la * (1.0 / 24.0 + la * (1.0 / 120.0)))))
    return jnp.where(la > -0.125, -t, 1.0 - a) * (1.0 + a)


def _rms(x):
    r = lax.rsqrt(jnp.mean(x * x, axis=-1, keepdims=True) + EPS)
    return x * r, r


def _dot(a, b):
    return jnp.dot(a, b, preferred_element_type=F32)


def _dot_nt(a, b):
    return lax.dot_general(a, b, (((1,), (1,)), ((), ())), preferred_element_type=F32)


def _dot_tn(a, b):
    return lax.dot_general(a, b, (((0,), (0,)), ((), ())), preferred_element_type=F32)


def _sum0(x):
    return jnp.sum(x, axis=0, keepdims=True)


def _norm_mod_bwd(x, g, sc, dh):
    xh, r = _rms(x)
    hn = xh * g
    dhn = dh * (1.0 + sc)
    dxh = dhn * g
    dx = r * (dxh - xh * jnp.mean(dxh * xh, axis=-1, keepdims=True))
    return dx, _sum0(dhn * xh), _sum0(dh), _sum0(dh * hn)


def _dev_index(p):
    return 4 * p[0] + 2 * p[1] + p[2]


def _mesh_pos():
    return lax.axis_index("x"), lax.axis_index("y"), lax.axis_index("c")


class _AllGather:
    def __init__(self, arrs):
        n = len(arrs)
        self.arrays = list(arrs)
        self.out_shapes = [_sds((N_DEV,) + a.shape, a.dtype) for a in arrs]
        self.scratch = ([pltpu.VMEM(a.shape, a.dtype) for a in arrs]
                        + [pltpu.SemaphoreType.DMA((7 * n,)), pltpu.SemaphoreType.DMA((7 * n,)),
                           pltpu.SemaphoreType.DMA((n,))])
        self.aliases = {}

    def _parts(self, ins, outs, scr):
        n = len(self.arrays)
        stage = scr[:n]
        send_sems, recv_sems, local_sems = scr[n:]
        x, y, c = _mesh_pos()
        me, sib = (x, y, c), (x, y, 1 - c)
        chips = [(1 - x, y), (x, 1 - y), (1 - x, 1 - y)]

        def copy(t, k, block, to, own=False):
            dst = outs[t].at[_dev_index(block)]
            return pltpu.make_async_remote_copy(
                src_ref=ins[t] if own else dst, dst_ref=dst,
                send_sem=send_sems.at[7 * t + k], recv_sem=recv_sems.at[7 * t + k],
                device_id=to, device_id_type=MESH)

        first = []
        for t in range(n):
            first.append(copy(t, 0, me, sib, own=True))
            for j, ch in enumerate(chips):
                first.append(copy(t, 1 + j, me, (*ch, c), own=True))
        stage_in = [pltpu.make_async_copy(ins[t], stage[t], local_sems.at[t]) for t in range(n)]
        mine = [pltpu.make_async_copy(stage[t], outs[t].at[_dev_index(me)], local_sems.at[t]) for t in range(n)]
        return n, c, me, sib, chips, copy, first, stage_in, mine

    def start(self, ins, outs, scr):
        n, _, _, _, _, _, first, stage_in, mine = self._parts(ins, outs, scr)
        for cp in stage_in:
            cp.start()
        for cp in first:
            cp.start()
        for t in range(n):
            stage_in[t].wait()
            mine[t].start()

    def relay(self, ins, outs, scr):
        n, c, me, sib, chips, copy, _, _, _ = self._parts(ins, outs, scr)
        for j, ch in enumerate(chips):
            for t in range(n):
                copy(t, 1 + j, (*ch, c), me).wait_recv()
                copy(t, 4 + j, (*ch, c), sib).start()

    def finish(self, ins, outs, scr):
        n, c, me, sib, chips, copy, first, _, mine = self._parts(ins, outs, scr)
        passed = [copy(t, 4 + j, (*ch, c), sib) for j, ch in enumerate(chips) for t in range(n)]
        for t in range(n):
            copy(t, 0, sib, me).wait_recv()
            for j, ch in enumerate(chips):
                copy(t, 4 + j, (*ch, 1 - c), me).wait_recv()
        for cp in first + passed:
            cp.wait_send()
        for cp in mine:
            cp.wait()


class _Exchange:
    def __init__(self, arrays, out_shapes, plan, n_copies, aliases=None):
        self.arrays = list(arrays)
        self.out_shapes = list(out_shapes)
        self.plan = plan
        self.scratch = [pltpu.SemaphoreType.DMA((n_copies,)), pltpu.SemaphoreType.DMA((n_copies,))]
        self.aliases = aliases or {}

    def _copies(self, ins, outs, scr):
        send_sems, recv_sems = scr
        snd, rcv = [], []
        for i, (src, dst, peer, lands) in enumerate(self.plan(ins, outs, _mesh_pos())):
            kw = dict(send_sem=send_sems.at[i], recv_sem=recv_sems.at[i], device_id=peer, device_id_type=MESH)
            snd.append(pltpu.make_async_remote_copy(src_ref=src, dst_ref=dst, **kw))
            rcv.append(pltpu.make_async_remote_copy(src_ref=src, dst_ref=lands, **kw))
        return snd, rcv

    def start(self, ins, outs, scr):
        for cp in self._copies(ins, outs, scr)[0]:
            cp.start()

    def relay(self, ins, outs, scr):
        pass

    def finish(self, ins, outs, scr):
        snd, rcv = self._copies(ins, outs, scr)
        for cp in rcv:
            cp.wait_recv()
        for cp in snd:
            cp.wait_send()


def _pair_exchange(grads):
    n = len(grads)

    def plan(ins, outs, pos):
        x, y, c = pos
        return [(ins[t].at[2 * j + (1 - c)], outs[t].at[j], (x, y, 1 - c), outs[t].at[j])
                for t in range(n) for j in range(N_CHIP)]

    return _Exchange(grads, [_sds((N_CHIP,) + g.shape[1:], g.dtype) for g in grads], plan, N_CHIP * n)


def _chip_exchange(parts, rows=None):
    n = len(parts)

    def plan(ins, outs, pos):
        x, y, c = pos
        chips = [(1 - x, y), (x, 1 - y), (1 - x, 1 - y)]

        def src(t, ch):
            blk = ins[t].at[2 * ch[0] + ch[1]]
            return blk if rows is None else blk.at[pl.ds(rows[0], rows[1])]

        return [(src(t, ch), outs[t].at[k], (*ch, c), outs[t].at[k]) for t in range(n) for k, ch in enumerate(chips)]

    shapes = [_sds((3, p.shape[1] if rows is None else rows[1]) + p.shape[2:], p.dtype) for p in parts]
    return _Exchange(parts, shapes, plan, 3 * n)


def _pair_gather(bufs):
    n = len(bufs)

    def plan(ins, outs, pos):
        x, y, c = pos
        return [(ins[t].at[c], outs[t].at[c], (x, y, 1 - c), outs[t].at[1 - c]) for t in range(n)]

    return _Exchange(bufs, [_sds(b.shape, b.dtype) for b in bufs], plan, n, aliases={t: t for t in range(n)})


def _run_comms(comms, name):
    c_in = [len(cm.arrays) for cm in comms]
    c_out = [len(cm.out_shapes) for cm in comms]
    c_scr = [len(cm.scratch) for cm in comms]
    aliases = {}
    for k, cm in enumerate(comms):
        for a, b in cm.aliases.items():
            aliases[sum(c_in[:k]) + a] = sum(c_out[:k]) + b

    def split(refs, counts):
        out, pos = [], 0
        for cnt in counts:
            out.append(refs[pos:pos + cnt])
            pos += cnt
        return out

    def body(*refs):
        ins = split(refs[:sum(c_in)], c_in)
        outs = split(refs[sum(c_in):sum(c_in) + sum(c_out)], c_out)
        scr = split(refs[sum(c_in) + sum(c_out):], c_scr)
        for phase in ("start", "relay", "finish"):
            for k, cm in enumerate(comms):
                getattr(cm, phase)(ins[k], outs[k], scr[k])

    outs = _pc(body, name=name, out_shape=[s for cm in comms for s in cm.out_shapes],
               in_specs=[ANY] * sum(c_in), out_specs=[ANY] * sum(c_out), input_output_aliases=aliases,
               scratch_shapes=[s for cm in comms for s in cm.scratch],
               compiler_params=_params())(*[a for cm in comms for a in cm.arrays])
    return split(list(outs), c_out)


def _all_gather(arrs, name):
    return _run_comms([_AllGather(arrs)], name)[0]


def _call(body, *, name, grid, in_specs, out_specs, out_shape, scratch_shapes, sem, args, comms=()):
    n_in, n_out, n_scr = len(in_specs), len(out_specs), len(scratch_shapes)
    c_in = [len(cm.arrays) for cm in comms]
    c_out = [len(cm.out_shapes) for cm in comms]
    c_scr = [len(cm.scratch) for cm in comms]
    aliases = {}
    for k, cm in enumerate(comms):
        for a, b in cm.aliases.items():
            aliases[n_in + sum(c_in[:k]) + a] = n_out + sum(c_out[:k]) + b

    def split(refs, counts):
        out, pos = [], 0
        for cnt in counts:
            out.append(refs[pos:pos + cnt])
            pos += cnt
        return out

    def wrapped(*refs):
        ins = refs[:n_in + sum(c_in)]
        outs = refs[len(ins):len(ins) + n_out + sum(c_out)]
        scr = refs[len(ins) + len(outs):]
        cins, couts, cscr = split(ins[n_in:], c_in), split(outs[n_out:], c_out), split(scr[n_scr:], c_scr)
        if comms:
            first = pl.program_id(0) == 0
            last = pl.program_id(0) == grid[0] - 1
            for k in range(1, len(grid)):
                first = jnp.logical_and(first, pl.program_id(k) == 0)
                last = jnp.logical_and(last, pl.program_id(k) == grid[k] - 1)

            @pl.when(first)
            def _():
                for k, cm in enumerate(comms):
                    cm.start(cins[k], couts[k], cscr[k])
        body(*ins[:n_in], *outs[:n_out], *scr[:n_scr])
        if comms:
            relay_early = len(grid) == 1 and grid[0] >= 4
            if relay_early:
                @pl.when(pl.program_id(0) == (7 * grid[0]) // 8 - 1)
                def _():
                    for k, cm in enumerate(comms):
                        cm.relay(cins[k], couts[k], cscr[k])

            @pl.when(last)
            def _():
                for k, cm in enumerate(comms):
                    if not relay_early:
                        cm.relay(cins[k], couts[k], cscr[k])
                    cm.finish(cins[k], couts[k], cscr[k])

    outs = _pc(wrapped, name=name, grid=grid,
               in_specs=list(in_specs) + [ANY] * sum(c_in), out_specs=list(out_specs) + [ANY] * sum(c_out),
               out_shape=list(out_shape) + [s for cm in comms for s in cm.out_shapes],
               scratch_shapes=list(scratch_shapes) + [s for cm in comms for s in cm.scratch],
               input_output_aliases=aliases, compiler_params=_params(*sem),
               )(*args, *[a for cm in comms for a in cm.arrays])
    outs = list(outs)
    return outs[:n_out], split(outs[n_out:], c_out)


def _row_block(r):
    for b in (512, 256, 128, 64, 32, 16, 8):
        if r % b == 0:
            return b
    return r


def _pair_add(g, recv, cj_idx, name):
    _, r, cc = g.shape
    br = _row_block(r)

    def body(cj_ref, g_ref, r_ref, own_ref, pb_ref):
        s = g_ref[...] + r_ref[...]
        pb_ref[...] = s.astype(BF16)

        @pl.when(pl.program_id(1) == cj_ref[1])
        def _():
            own_ref[...] = s[0]

    grid_spec = pltpu.PrefetchScalarGridSpec(
        num_scalar_prefetch=1, grid=(r // br, N_CHIP),
        in_specs=[pl.BlockSpec((1, br, cc), lambda i, j, cj_ref: (2 * j + cj_ref[0], i, 0)),
                  pl.BlockSpec((1, br, cc), lambda i, j, cj_ref: (j, i, 0))],
        out_specs=[pl.BlockSpec((br, cc), lambda i, j, cj_ref: (i, 0)),
                   pl.BlockSpec((1, br, cc), lambda i, j, cj_ref: (j, i, 0))])
    return _pc(body, name=name, grid_spec=grid_spec,
               out_shape=[_sds((r, cc)), _sds((N_CHIP, r, cc), BF16)],
               compiler_params=_params("arbitrary", "arbitrary"))(cj_idx, g, recv)


def _chip_add(p, qs, cj_idx, name):
    r, cc = p.shape
    nq = len(qs)
    br = _row_block(r // nq)
    nb = r // nq // br

    def body(cj_ref, p_ref, *refs):
        o_ref = refs[-1]
        if nq == 2:
            top = pl.program_id(0) < nb
            q = [jnp.where(top, refs[0][k], refs[1][k]).astype(F32) for k in range(3)]
        else:
            q = [refs[0][k].astype(F32) for k in range(3)]
        o_ref[0] = ((p_ref[...] + q[0]) + q[1]) + q[2]

    q_specs = [pl.BlockSpec((3, br, cc), lambda i, cj_ref, h=h: (0, jnp.clip(i - h * nb, 0, nb - 1), 0))
               for h in range(nq)]
    grid_spec = pltpu.PrefetchScalarGridSpec(
        num_scalar_prefetch=1, grid=(r // br,),
        in_specs=[pl.BlockSpec((br, cc), lambda i, cj_ref: (i, 0))] + q_specs,
        out_specs=pl.BlockSpec((1, br, cc), lambda i, cj_ref: (cj_ref[0], i, 0)))
    return _pc(body, name=name, grid_spec=grid_spec, out_shape=_sds((2, r, cc)),
               compiler_params=_params("arbitrary"))(cj_idx, p, *qs)


def _shard_of(both):
    return both.reshape((2 * both.shape[1],) + both.shape[2:])


ADAMW_CHUNKS = 4


def _adamw(w, g, m, v, name):
    r, cc = w.shape
    rows = r // ADAMW_CHUNKS
    assert rows * ADAMW_CHUNKS == r and rows % SUBLANES == 0
    c1 = 1.0 - ADAM_B1 ** ADAM_STEP
    c2 = 1.0 - ADAM_B2 ** ADAM_STEP

    def body(w_hbm, g_hbm, m_hbm, v_hbm, go_hbm, d_hbm, mo_hbm, vo_hbm, wb, gb, mb, vb, sem_in, sem_out):
        srcs, bufs, dsts = (w_hbm, g_hbm, m_hbm, v_hbm), (wb, gb, mb, vb), (d_hbm, go_hbm, mo_hbm, vo_hbm)

        def load(a, k):
            sl = pl.ds(k * rows, rows)
            return pltpu.make_async_copy(srcs[a].at[sl], bufs[a].at[sl], sem_in.at[a, k])

        def store(a, k):
            sl = pl.ds(k * rows, rows)
            return pltpu.make_async_copy(bufs[a].at[sl], dsts[a].at[sl], sem_out.at[a, k])

        for k in range(ADAMW_CHUNKS):
            for a in range(4):
                load(a, k).start()
        for k in range(ADAMW_CHUNKS):
            for a in range(4):
                load(a, k).wait()
            store(1, k).start()
            sl = pl.ds(k * rows, rows)
            gg = gb[sl]
            mn = ADAM_B1 * mb[sl] + (1.0 - ADAM_B1) * gg
            vn = ADAM_B2 * vb[sl] + (1.0 - ADAM_B2) * (gg * gg)
            mh = mn / c1
            vh = vn / c2
            wb[sl] = -ADAM_LR * (mh / (jnp.sqrt(vh) + ADAM_EPS) + ADAM_WD * wb[sl])
            mb[sl] = mn
            vb[sl] = vn
            for a in (0, 2, 3):
                store(a, k).start()
        for k in range(ADAMW_CHUNKS):
            for a in range(4):
                store(a, k).wait()

    go, d, mo, vo = _pc(body, name=name, in_specs=[ANY] * 4, out_specs=[ANY] * 4, out_shape=[_sds((r, cc))] * 4,
                        scratch_shapes=[pltpu.VMEM((r, cc), F32)] * 4 + [pltpu.SemaphoreType.DMA((4, ADAMW_CHUNKS))] * 2,
                        compiler_params=_params())(w, g, m, v)
    return go, d, mo, vo


def _head(w_half, c8, small, w_ada, b_shard, c_ctx, ret_decay):
    ada_n = w_ada.shape[1]
    mod_sds = _sds((16, ada_n))
    ag_w, ag_c, ag_m = _AllGather([w_half]), _AllGather([c8, small]), _AllGather([mod_sds])
    n_w, n_c, n_m = len(ag_w.scratch), len(ag_c.scratch), len(ag_m.scratch)

    def body(w_ref, c_ref, s_ref, wada_ref, b_ref, cc_ref, rd_ref,
             gw_ref, call_ref, sall_ref, a_ref, modp_ref, mall_ref, lg_ref, sg_ref, *scr):
        scr_w, scr_c, scr_m = scr[:n_w], scr[n_w:n_w + n_c], scr[n_w + n_c:n_w + n_c + n_m]
        c_v, w_v, m_v, sems = scr[n_w + n_c + n_m:]
        ag_w.start((w_ref,), (gw_ref,), scr_w)
        ag_c.start((c_ref, s_ref), (call_ref, sall_ref), scr_c)
        load_w = pltpu.make_async_copy(wada_ref, w_v, sems.at[0])
        load_w.start()
        rd = rd_ref[...]
        lg_ref[...] = -_softplus(-rd)
        sg_ref[...] = _sigmoid(-rd)
        ag_c.relay((c_ref, s_ref), (call_ref, sall_ref), scr_c)
        ag_c.finish((c_ref, s_ref), (call_ref, sall_ref), scr_c)
        load_c = pltpu.make_async_copy(call_ref, c_v, sems.at[1])
        load_c.start()
        load_c.wait()
        a_ref[...] = jnp.zeros_like(a_ref)
        for d in range(N_DEV):
            cd = c_v[d, 0:1, :]
            a_ref[d:d + 1, :] = cd * _sigmoid(cd)
        cc = cc_ref[...]
        a_ref[N_DEV:N_DEV + 1, :] = cc * _sigmoid(cc)
        load_w.wait()
        m_v[...] = jnp.dot(a_ref[...], w_v[...], preferred_element_type=F32,
                           precision=lax.Precision.HIGHEST) + b_ref[...]
        put = pltpu.make_async_copy(m_v, modp_ref, sems.at[2])
        put.start()
        put.wait()
        ag_m.start((modp_ref,), (mall_ref,), scr_m)
        ag_m.relay((modp_ref,), (mall_ref,), scr_m)
        ag_m.finish((modp_ref,), (mall_ref,), scr_m)
        ag_w.relay((w_ref,), (gw_ref,), scr_w)
        ag_w.finish((w_ref,), (gw_ref,), scr_w)

    rd = jnp.broadcast_to(ret_decay.reshape(2, HEADS).T[:, :, None], (HEADS, 2, LANES))
    lane = _full((HEADS, 2, LANES))
    outs = _pc(
        body, name="head",
        in_specs=[ANY, ANY, ANY, ANY, _full((1, ada_n)), _full((1, D_MODEL)), lane],
        out_specs=[ANY, ANY, ANY, _full((16, D_MODEL)), ANY, ANY, lane, lane],
        out_shape=ag_w.out_shapes + ag_c.out_shapes + [_sds((16, D_MODEL)), mod_sds] + ag_m.out_shapes
        + [_sds((HEADS, 2, LANES))] * 2,
        scratch_shapes=ag_w.scratch + ag_c.scratch + ag_m.scratch
        + [pltpu.VMEM((N_DEV,) + c8.shape, F32), pltpu.VMEM(w_ada.shape, F32), pltpu.VMEM((16, ada_n), F32),
           pltpu.SemaphoreType.DMA((3,))],
        compiler_params=_params(),
    )(w_half, c8, small, w_ada, b_shard, c_ctx.reshape(1, D_MODEL), rd)
    gw, c_all, small_all, a16, _, mod_all, lgv, sgv = outs
    return gw, c_all, small_all, a16, mod_all, lgv, sgv


def _ada_grad(at, b):
    n = b.shape[1]
    bn = 512

    def body(a_ref, b_ref, o_ref):
        o_ref[...] = jnp.dot(a_ref[...], b_ref[...], preferred_element_type=F32, precision=lax.Precision.HIGHEST)

    return _pc(body, name="ada_grad", grid=(n // bn,),
               in_specs=[_full((D_MODEL, LANES)), pl.BlockSpec((LANES, bn), lambda i: (0, i))],
               out_specs=pl.BlockSpec((D_MODEL, bn), lambda i: (0, i)), out_shape=_sds((D_MODEL, n)),
               compiler_params=_params("arbitrary"))(at, b)


def _cctx_partial(dmc8, w_ada):
    n = w_ada.shape[1]
    bn = 512

    def body(d_ref, w_ref, o_ref):
        @pl.when(pl.program_id(0) == 0)
        def _():
            o_ref[...] = jnp.zeros_like(o_ref)
        o_ref[...] += lax.dot_general(d_ref[...], w_ref[...], (((1,), (1,)), ((), ())),
                                      preferred_element_type=F32, precision=lax.Precision.HIGHEST)

    return _pc(body, name="cctx_partial", grid=(n // bn,),
               in_specs=[pl.BlockSpec((8, bn), lambda i: (0, i)), pl.BlockSpec((D_MODEL, bn), lambda i: (0, i))],
               out_specs=_full((8, D_MODEL)), out_shape=_sds((8, D_MODEL)),
               compiler_params=_params("arbitrary"))(dmc8, w_ada)


def _cctx_final(parts, c_ctx, m, v):
    c1 = 1.0 - ADAM_B1 ** ADAM_STEP
    c2 = 1.0 - ADAM_B2 ** ADAM_STEP

    def body(p_ref, c_ref, m_ref, v_ref, g_ref, d_ref, mo_ref, vo_ref):
        s = ((p_ref[0, 0:1, :] + p_ref[2, 0:1, :]) + p_ref[4, 0:1, :]) + p_ref[6, 0:1, :]
        z = c_ref[...]
        sg = _sigmoid(z)
        gg = s * (sg * (1.0 + z * (1.0 - sg)))
        g_ref[...] = gg
        mn = ADAM_B1 * m_ref[...] + (1.0 - ADAM_B1) * gg
        vn = ADAM_B2 * v_ref[...] + (1.0 - ADAM_B2) * (gg * gg)
        d_ref[...] = -ADAM_LR * ((mn / c1) / (jnp.sqrt(vn / c2) + ADAM_EPS) + ADAM_WD * z)
        mo_ref[...] = mn
        vo_ref[...] = vn

    row = _full((1, D_MODEL))
    return _pc(body, name="cctx_final", out_shape=[_sds((1, D_MODEL))] * 4,
               in_specs=[_full(parts.shape), row, row, row], out_specs=[row] * 4,
               compiler_params=_params())(parts, c_ctx.reshape(1, D_MODEL), m.reshape(1, D_MODEL), v.reshape(1, D_MODEL))


def _rotary_tables(t_len):
    rows = t_len // GRID_W
    n_freq = DH // 4
    inv = ROPE_BASE ** (-jnp.arange(n_freq, dtype=F32) / n_freq)
    row_ang = jnp.arange(rows, dtype=F32)[:, None] * inv
    col_ang = jnp.arange(GRID_W, dtype=F32)[:, None] * inv

    def spread(fn):
        return jnp.concatenate([jnp.repeat(fn(row_ang), GRID_W, axis=0), jnp.tile(fn(col_ang), (rows, 1))], axis=-1)

    cos, sin = spread(jnp.cos), spread(jnp.sin)
    return jnp.concatenate([cos, cos], axis=-1), jnp.concatenate([-sin, sin], axis=-1)


def _inproj_fwd(x, gn, sh, sc, w4, cos2, sin2, name, comms=()):
    t = x.shape[0]
    tm = _tile(t, True)
    nc = IN_COLS // N_CHIP

    def body(x_ref, gn_ref, sh_ref, sc_ref, w_ref, c_ref, s_ref, p_ref, xr_ref, hb_ref, p_s):
        xh, _ = _rms(x_ref[...])
        h = xh * gn_ref[...] * (1.0 + sc_ref[...]) + sh_ref[...]
        hb = h.astype(BF16)
        hb_ref[...] = hb
        for j in range(N_CHIP):
            p_s[:, nc * j:nc * (j + 1)] = _dot(hb, w_ref[j])
        cc = c_ref[...]
        ss = s_ref[...]
        for hh in range(2 * HEADS):
            blk = p_s[:, DH * hh:DH * (hh + 1)]
            rot = blk * cc + pltpu.roll(blk, DH // 2, 1) * ss
            if hh >= HEADS:
                rot = rot * K_SCALE
            p_ref[:, DH * hh:DH * (hh + 1)] = rot.astype(BF16)
        p_ref[:, 2 * RET_W:] = p_s[:, 2 * RET_W:].astype(BF16)
        xr_ref[...] = p_s[:, 4 * RET_W:4 * RET_W + LRU_W]

    row = _full((1, D_MODEL))
    outs, couts = _call(
        body, name=name, grid=(t // tm,),
        in_specs=[pl.BlockSpec((tm, D_MODEL), lambda i: (i, 0)), row, row, row, _full(w4.shape),
                  pl.BlockSpec((tm, DH), lambda i: (i, 0)), pl.BlockSpec((tm, DH), lambda i: (i, 0))],
        out_specs=[pl.BlockSpec((tm, IN_COLS), lambda i: (i, 0)), pl.BlockSpec((tm, LRU_W), lambda i: (i, 0)),
                   pl.BlockSpec((tm, D_MODEL), lambda i: (i, 0))],
        out_shape=[_sds((t, IN_COLS), BF16), _sds((t, LRU_W)), _sds((t, D_MODEL), BF16)],
        scratch_shapes=[pltpu.VMEM((tm, IN_COLS), F32)], sem=("arbitrary",),
        args=(x, gn, sh, sc, w4, cos2, sin2), comms=comms)
    return (outs, couts) if comms else outs


def _inproj_bwd(x, gn, sh, sc, w4, cos2, sin2, pieces, dres, name):
    t = x.shape[0]
    tm = _tile(t)
    nc = IN_COLS // N_CHIP

    def body(x_ref, gn_ref, sh_ref, sc_ref, w_ref, c_ref, s_ref, dqf, dqb, dkf, dkb, dvf, dvb, dg, dxr, dgt, dres_ref,
             dx_ref, dpb_ref, dgn_ref, dsh_ref, dsc_ref):
        cc = c_ref[...]
        ss = s_ref[...]
        dq = dqf[...].astype(F32) + dqb[...].astype(F32)
        dk = dkf[...].astype(F32) + dkb[...].astype(F32)
        for hh in range(HEADS):
            sl = slice(DH * hh, DH * (hh + 1))
            b = dq[:, sl]
            dpb_ref[:, sl] = (b * cc + pltpu.roll(b * ss, DH // 2, 1)).astype(BF16)
            b = dk[:, sl]
            dpb_ref[:, RET_W + DH * hh:RET_W + DH * (hh + 1)] = (
                (b * cc + pltpu.roll(b * ss, DH // 2, 1)) * K_SCALE).astype(BF16)
        dpb_ref[:, 2 * RET_W:3 * RET_W] = (dvf[...].astype(F32) + dvb[...].astype(F32)).astype(BF16)
        dpb_ref[:, 3 * RET_W:4 * RET_W] = dg[...].astype(BF16)
        dpb_ref[:, 4 * RET_W:4 * RET_W + LRU_W] = dxr[...].astype(BF16)
        dpb_ref[:, 4 * RET_W + LRU_W:IN_COLS] = dgt[...].astype(BF16)
        dh = _dot_nt(dpb_ref[:, 0:nc], w_ref[0])
        for j in range(1, N_CHIP):
            dh = dh + _dot_nt(dpb_ref[:, nc * j:nc * (j + 1)], w_ref[j])
        dx, dgn_t, dsh_t, dsc_t = _norm_mod_bwd(x_ref[...], gn_ref[...], sc_ref[...], dh)
        dx_ref[...] = dres_ref[...] + dx

        @pl.when(pl.program_id(0) == 0)
        def _():
            dgn_ref[...] = jnp.zeros_like(dgn_ref)
            dsh_ref[...] = jnp.zeros_like(dsh_ref)
            dsc_ref[...] = jnp.zeros_like(dsc_ref)
        dgn_ref[...] += dgn_t
        dsh_ref[...] += dsh_t
        dsc_ref[...] += dsc_t

    row = _full((1, D_MODEL))
    pc = pl.BlockSpec((tm, RET_W), lambda i: (i, 0))
    big = pl.BlockSpec((tm, D_MODEL), lambda i: (i, 0))
    return _pc(body, name=name, grid=(t // tm,),
               in_specs=[big, row, row, row, _full(w4.shape),
                         pl.BlockSpec((tm, DH), lambda i: (i, 0)), pl.BlockSpec((tm, DH), lambda i: (i, 0))]
               + [pc] * 9 + [big],
               out_specs=[big, pl.BlockSpec((tm, IN_COLS), lambda i: (i, 0)), row, row, row],
               out_shape=[_sds((t, D_MODEL)), _sds((t, IN_COLS), BF16), _sds((1, D_MODEL)), _sds((1, D_MODEL)),
                          _sds((1, D_MODEL))],
               compiler_params=_params("arbitrary"))(x, gn, sh, sc, w4, cos2, sin2, *pieces, dres)


def _halo_specs(t, tm):
    n8 = tm // SUBLANES
    last8 = t // SUBLANES - 1
    prev = pl.BlockSpec((SUBLANES, LRU_W), lambda i: (jnp.maximum(i * n8 - 1, 0), 0))
    main = pl.BlockSpec((tm, LRU_W), lambda i: (i, 0))
    nxt = pl.BlockSpec((SUBLANES, LRU_W), lambda i: (jnp.minimum((i + 1) * n8, last8), 0))
    return prev, main, nxt


def _with_halo(prev_ref, main_ref, next_ref, i, nt):
    prev = jnp.where(i > 0, prev_ref[...], 0.0)
    nxt = jnp.where(i < nt - 1, next_ref[...], 0.0)
    return jnp.concatenate([prev, main_ref[...], nxt], axis=0)


def _conv_fwd(xr, cw, cb, name):
    t = xr.shape[0]
    tm = _tile(t, True)
    nt = t // tm
    n = tm + 2 * SUBLANES
    mid = slice(SUBLANES, SUBLANES + tm)

    def body(p_ref, m_ref, n_ref, w_ref, b_ref, o_ref):
        xp = _with_halo(p_ref, m_ref, n_ref, pl.program_id(0), nt)
        acc = b_ref[...] + pltpu.roll(xp, 1, 0)[mid] * w_ref[0:1, :]
        acc = acc + xp[mid] * w_ref[1:2, :]
        acc = acc + pltpu.roll(xp, n - 1, 0)[mid] * w_ref[2:3, :]
        acc = acc + pltpu.roll(xp, n - 2, 0)[mid] * w_ref[3:4, :]
        o_ref[...] = acc

    return _pc(body, name=name, grid=(nt,),
               in_specs=[*_halo_specs(t, tm), _full((4, LRU_W)), _full((1, LRU_W))],
               out_specs=pl.BlockSpec((tm, LRU_W), lambda i: (i, 0)), out_shape=_sds((t, LRU_W)),
               compiler_params=_params("arbitrary"))(xr, xr, xr, cw, cb)


def _conv_bwd(dxc_a, dxc_b, xr, cw, name):
    t = xr.shape[0]
    tm = _tile(t, True)
    nt = t // tm
    n = tm + 2 * SUBLANES
    mid = slice(SUBLANES, SUBLANES + tm)

    def body(ap_ref, am_ref, an_ref, bp_ref, bm_ref, bn_ref, xp_ref, xm_ref, xn_ref, w_ref, dx_ref, dw_ref, db_ref):
        i = pl.program_id(0)
        dp = _with_halo(ap_ref, am_ref, an_ref, i, nt) + _with_halo(bp_ref, bm_ref, bn_ref, i, nt)
        xp = _with_halo(xp_ref, xm_ref, xn_ref, i, nt)
        dx = pltpu.roll(dp, n - 1, 0)[mid] * w_ref[0:1, :]
        dx = dx + dp[mid] * w_ref[1:2, :]
        dx = dx + pltpu.roll(dp, 1, 0)[mid] * w_ref[2:3, :]
        dx = dx + pltpu.roll(dp, 2, 0)[mid] * w_ref[3:4, :]
        dx_ref[...] = dx.astype(BF16)
        d = dp[mid]

        @pl.when(i == 0)
        def _():
            dw_ref[...] = jnp.zeros_like(dw_ref)
            db_ref[...] = jnp.zeros_like(db_ref)
        dw_ref[0:1, :] += _sum0(d * pltpu.roll(xp, 1, 0)[mid])
        dw_ref[1:2, :] += _sum0(d * xp[mid])
        dw_ref[2:3, :] += _sum0(d * pltpu.roll(xp, n - 1, 0)[mid])
        dw_ref[3:4, :] += _sum0(d * pltpu.roll(xp, n - 2, 0)[mid])
        db_ref[...] += _sum0(d)

    return _pc(body, name=name, grid=(nt,),
               in_specs=[*_halo_specs(t, tm), *_halo_specs(t, tm), *_halo_specs(t, tm), _full((4, LRU_W))],
               out_specs=[pl.BlockSpec((tm, LRU_W), lambda i: (i, 0)), _full((4, LRU_W)), _full((1, LRU_W))],
               out_shape=[_sds((t, LRU_W), BF16), _sds((4, LRU_W)), _sds((1, LRU_W))],
               compiler_params=_params("arbitrary"))(dxc_a, dxc_a, dxc_a, dxc_b, dxc_b, dxc_b, xr, xr, xr, cw)


def _scan_scratch(n, c):
    return pltpu.VMEM((c // LANES, n, LANES), F32)


def _to_lane_blocks(ref, val):
    for lb in range(ref.shape[0]):
        ref[lb] = val[:, lb * LANES:(lb + 1) * LANES]


def _group_scan(a_s, b_s, reverse):
    nb, n, _ = a_s.shape
    ng = n // SUBLANES
    order = range(SUBLANES - 1, -1, -1) if reverse else range(SUBLANES)
    for lb in range(nb):
        prev = None
        for r in order:
            rows = pl.ds(r, ng, stride=SUBLANES)
            a_r, b_r = a_s[lb, rows, :], b_s[lb, rows, :]
            if prev is not None:
                b_r = a_r * prev[1] + b_r
                a_r = a_r * prev[0]
                a_s[lb, rows, :] = a_r
                b_s[lb, rows, :] = b_r
            prev = (a_r, b_r)


def _carry_scan(a_s, b_s, out_ref, carry, reverse):
    nb, n, _ = a_s.shape
    ng = n // SUBLANES

    def step(g, crs):
        gg = (ng - 1 - g) if reverse else g
        off = pl.multiple_of(gg * SUBLANES, SUBLANES)
        new = []
        for lb in range(nb):
            h = a_s[lb, pl.ds(off, SUBLANES), :] * crs[lb] + b_s[lb, pl.ds(off, SUBLANES), :]
            out_ref[pl.ds(off, SUBLANES), pl.ds(lb * LANES, LANES)] = h
            edge = h[0:1, :] if reverse else h[SUBLANES - 1:SUBLANES, :]
            new.append(jnp.broadcast_to(edge, (SUBLANES, LANES)))
        return tuple(new)

    crs = lax.fori_loop(0, ng, step, tuple(carry[:, lb * LANES:(lb + 1) * LANES] for lb in range(nb)))
    return jnp.concatenate(crs, axis=1)


def _lru_gates(xc, wa_ref, wx_ref, ba, bx, lam):
    xb = xc.astype(BF16)
    r = _sigmoid(_dot(xb, wa_ref[...]) + ba)
    ig = _sigmoid(_dot(xb, wx_ref[...]) + bx)
    sp = _softplus(-lam)
    la = -LRU_C * r * sp
    a = jnp.exp(la)
    mult = jnp.sqrt(_one_minus_sq(la, a))
    return r, ig, sp, a, mult


def _lru_fwd(xc, par_f, par_b, h0_f, h0_b, name, comms=()):
    t = xc.shape[0]
    tm = _tile(t, True)
    nt = t // tm

    def one(x_ref, prm, h0_ref, h_ref, a_s, b_s, c_s, reverse):
        wa_ref, wx_ref, ba_ref, bx_ref, lam_ref = prm

        @pl.when(pl.program_id(0) == 0)
        def _():
            c_s[...] = jnp.broadcast_to(h0_ref[...], c_s.shape)
        xv = x_ref[...]
        _, ig, _, a, mult = _lru_gates(xv, wa_ref, wx_ref, ba_ref[...], bx_ref[...], lam_ref[...])
        _to_lane_blocks(a_s, a)
        _to_lane_blocks(b_s, mult * (ig * xv))
        _group_scan(a_s, b_s, reverse)
        c_s[...] = _carry_scan(a_s, b_s, h_ref, c_s[...], reverse)

    def body(xf_ref, xb_ref, *refs):
        prm_f, prm_b = refs[0:5], refs[5:10]
        h0f_ref, h0b_ref, hf_ref, hb_ref = refs[10:14]
        af_s, bf_s, cf_s, ab_s, bb_s, cb_s = refs[14:]
        one(xf_ref, prm_f, h0f_ref, hf_ref, af_s, bf_s, cf_s, False)
        one(xb_ref, prm_b, h0b_ref, hb_ref, ab_s, bb_s, cb_s, True)

    vec = _full((1, LRU_W))
    mat = _full((LRU_W, LRU_W))
    fw = pl.BlockSpec((tm, LRU_W), lambda i: (i, 0))
    bw = pl.BlockSpec((tm, LRU_W), lambda i: (nt - 1 - i, 0))
    tile_s = [_scan_scratch(tm, LRU_W), _scan_scratch(tm, LRU_W), pltpu.VMEM((SUBLANES, LRU_W), F32)]
    (hf, hb), couts = _call(
        body, name=name, grid=(nt,),
        in_specs=[fw, bw] + [mat, mat, vec, vec, vec] * 2 + [vec, vec],
        out_specs=[pl.BlockSpec((tm, LRU_W), lambda i: (i, 0)), pl.BlockSpec((tm, LRU_W), lambda i: (nt - 1 - i, 0))],
        out_shape=[_sds((t, LRU_W))] * 2, scratch_shapes=tile_s + tile_s, sem=("arbitrary",),
        args=(xc, xc, *par_f, *par_b, h0_f, h0_b), comms=comms)
    return ((hf, hb), couts) if comms else (hf, hb)


def _lru_bwd(xc, par_f, par_b, h_f, h_b, h0_f, h0_b, dh_f, dh_b, name, comms=()):
    t = xc.shape[0]
    tm = _tile(t, True)
    nt = t // tm
    n8 = tm // SUBLANES
    last8 = t // SUBLANES - 1
    tile_f = lambda w: pl.BlockSpec((tm, w), lambda i: (nt - 1 - i, 0))
    tile_b = lambda w: pl.BlockSpec((tm, w), lambda i: (i, 0))
    halo_f = pl.BlockSpec((SUBLANES, LRU_W), lambda i: (jnp.maximum((nt - 1 - i) * n8 - 1, 0), 0))
    halo_b = pl.BlockSpec((SUBLANES, LRU_W), lambda i: (jnp.minimum((i + 1) * n8, last8), 0))

    def one(refs_in, refs_out, refs_scr, reverse):
        x_ref, wa_ref, wx_ref, ba_ref, bx_ref, lam_ref, h_ref, halo_ref, h0_ref, dh_ref = refs_in
        dx_ref, dpre_ref, dba_ref, dbx_ref, dlam_ref, dh0_ref = refs_out
        a_s, b_s, l_s, c_s, e_s = refs_scr
        i = pl.program_id(0)

        @pl.when(i == 0)
        def _():
            c_s[...] = jnp.zeros_like(c_s)
            e_s[...] = jnp.zeros_like(e_s)
            dba_ref[...] = jnp.zeros_like(dba_ref)
            dbx_ref[...] = jnp.zeros_like(dbx_ref)
            dlam_ref[...] = jnp.zeros_like(dlam_ref)
        xv = x_ref[...]
        lam = lam_ref[...]
        r, ig, sp, a, mult = _lru_gates(xv, wa_ref, wx_ref, ba_ref[...], bx_ref[...], lam)
        hv = h_ref[...]
        rowi = lax.broadcasted_iota(jnp.int32, (tm, LRU_W), 0)
        edge_a = jnp.broadcast_to(e_s[0:1, :], (tm, LRU_W))
        h0b = jnp.broadcast_to(h0_ref[...], (tm, LRU_W))
        if reverse:
            a_sh = jnp.where(rowi == 0, edge_a, pltpu.roll(a, 1, 0))
            hin_edge = jnp.where(i == nt - 1, h0b, jnp.broadcast_to(halo_ref[0:1, :], (tm, LRU_W)))
            h_in = jnp.where(rowi == tm - 1, hin_edge, pltpu.roll(hv, tm - 1, 0))
        else:
            a_sh = jnp.where(rowi == tm - 1, edge_a, pltpu.roll(a, tm - 1, 0))
            hin_edge = jnp.where(i == nt - 1, h0b, jnp.broadcast_to(halo_ref[SUBLANES - 1:SUBLANES, :], (tm, LRU_W)))
            h_in = jnp.where(rowi == 0, hin_edge, pltpu.roll(hv, 1, 0))
        _to_lane_blocks(a_s, a_sh)
        _to_lane_blocks(b_s, dh_ref[...])
        _group_scan(a_s, b_s, not reverse)
        c_s[...] = _carry_scan(a_s, b_s, l_s, c_s[...], not reverse)
        e_s[...] = jnp.broadcast_to(a[tm - 1:tm, :] if reverse else a[0:1, :], e_s.shape)
        lmb = l_s[...]
        da = lmb * h_in
        ixc = ig * xv
        dmult = lmb * ixc
        dixc = lmb * mult
        dla = da * a - dmult * (a * a) / mult
        dpr = dla * (-LRU_C * sp) * r * (1.0 - r)
        dpi = dixc * xv * ig * (1.0 - ig)
        dprb = dpr.astype(BF16)
        dpib = dpi.astype(BF16)
        dpre_ref[:, 0:LRU_W] = dprb
        dpre_ref[:, LRU_W:2 * LRU_W] = dpib
        dx_ref[...] = dixc * ig + _dot_nt(dprb, wa_ref[...]) + _dot_nt(dpib, wx_ref[...])
        dba_ref[...] += _sum0(dpr)
        dbx_ref[...] += _sum0(dpi)
        dlam_ref[...] += _sum0(dla * (-LRU_C * r)) * (-_sigmoid(-lam))

        @pl.when(i == nt - 1)
        def _():
            al0 = a * lmb
            dh0_ref[...] = al0[tm - 1:tm, :] if reverse else al0[0:1, :]

    def body(*refs):
        one(refs[0:10], refs[20:26], refs[32:37], False)
        one(refs[10:20], refs[26:32], refs[37:42], True)

    vec = _full((1, LRU_W))
    mat = _full((LRU_W, LRU_W))

    def in_specs(tile, halo):
        return [tile(LRU_W), mat, mat, vec, vec, vec, tile(LRU_W), halo, vec, tile(LRU_W)]

    def out_specs(tile):
        return [tile(LRU_W), tile(2 * LRU_W), vec, vec, vec, vec]

    out_one = [_sds((t, LRU_W)), _sds((t, 2 * LRU_W), BF16)] + [_sds((1, LRU_W))] * 4
    scr_one = ([_scan_scratch(tm, LRU_W)] * 2 + [pltpu.VMEM((tm, LRU_W), F32)]
               + [pltpu.VMEM((SUBLANES, LRU_W), F32)] * 2)
    outs, couts = _call(
        body, name=name, grid=(nt,), in_specs=in_specs(tile_f, halo_f) + in_specs(tile_b, halo_b),
        out_specs=out_specs(tile_f) + out_specs(tile_b), out_shape=out_one + out_one,
        scratch_shapes=scr_one + scr_one, sem=("arbitrary",),
        args=(xc, *par_f, h_f, h_f, h0_f, dh_f, xc, *par_b, h_b, h_b, h0_b, dh_b), comms=comms)
    return (tuple(outs[0:6]), tuple(outs[6:12])), couts


def _decay_tables(lg, reverse):
    ci = lax.broadcasted_iota(jnp.int32, (CHUNK, CHUNK), 0).astype(F32)
    mi = lax.broadcasted_iota(jnp.int32, (CHUNK, CHUNK), 1).astype(F32)
    rel = (mi - ci) if reverse else (ci - mi)
    relc = jnp.maximum(rel, 0.0)
    lg_c = jnp.concatenate([lg] * (CHUNK // LANES), axis=1)
    dm = jnp.where(rel >= 0, jnp.exp(lg_c * relc), 0.0)
    cd = lax.broadcasted_iota(jnp.int32, (CHUNK, DH), 0).astype(F32)
    pq, ps = (CHUNK - cd, cd) if reverse else (cd + 1.0, CHUNK - 1.0 - cd)
    return relc, dm, jnp.exp(lg * pq), jnp.exp(lg * ps), jnp.exp(lg * float(CHUNK)), pq, ps


def _ret_fwd(proj, lgv, s0f, s0b, comms=()):
    t = proj.shape[0]
    n = t // CHUNK

    def one(q, k, v, lg, s_s, hh, o_ref, sp_ref, reverse):
        _, dm, wq, ws, g, _, _ = _decay_tables(lg, reverse)
        vb = v.astype(BF16)
        p = _dot_nt(q.astype(BF16), k.astype(BF16)) * dm
        s = s_s[hh]
        sp_ref[hh, 0] = s
        o_ref[:, DH * hh:DH * (hh + 1)] = _dot(p.astype(BF16), vb) + _dot((q * wq).astype(BF16), s.astype(BF16))
        s_s[hh] = g * s + _dot_tn((k * ws).astype(BF16), vb)

    def body(qf, kf, vf, qb, kb, vb, lg_ref, s0f_ref, s0b_ref, of_ref, ob_ref, spf_ref, spb_ref, sf_s, sb_s):
        @pl.when(pl.program_id(0) == 0)
        def _():
            sf_s[...] = s0f_ref[...]
            sb_s[...] = s0b_ref[...]
        for hh in range(HEADS):
            sl = slice(DH * hh, DH * (hh + 1))
            one(qf[:, sl].astype(F32), kf[:, sl].astype(F32), vf[:, sl], lg_ref[hh, 0:1, :], sf_s, hh, of_ref, spf_ref,
                False)
            one(qb[:, sl].astype(F32), kb[:, sl].astype(F32), vb[:, sl], lg_ref[hh, 1:2, :], sb_s, hh, ob_ref, spb_ref,
                True)

    blk = (CHUNK, RET_W)
    fw = [pl.BlockSpec(blk, lambda i, o=o: (i, o)) for o in range(3)]
    bw = [pl.BlockSpec(blk, lambda i, o=o: (n - 1 - i, o)) for o in range(3)]
    st = _full((HEADS, DH, DH))
    return _call(body, name="ret_fwd", grid=(n,),
                 in_specs=fw + bw + [_full((HEADS, 2, LANES)), st, st],
                 out_specs=[pl.BlockSpec(blk, lambda i: (i, 0)), pl.BlockSpec(blk, lambda i: (n - 1 - i, 0)),
                            pl.BlockSpec((HEADS, 1, DH, DH), lambda i: (0, i, 0, 0)),
                            pl.BlockSpec((HEADS, 1, DH, DH), lambda i: (0, n - 1 - i, 0, 0))],
                 out_shape=[_sds((t, RET_W)), _sds((t, RET_W)), _sds((HEADS, n, DH, DH)), _sds((HEADS, n, DH, DH))],
                 scratch_shapes=[pltpu.VMEM((HEADS, DH, DH), F32), pltpu.VMEM((HEADS, DH, DH), F32)],
                 sem=("arbitrary",), args=(proj, proj, proj, proj, proj, proj, lgv, s0f, s0b), comms=comms)


def _ret_bwd(proj, lgv, sgv, spf, spb, do, comms=()):
    t = proj.shape[0]
    n = t // CHUNK

    def one(q_ref, k_ref, v_ref, lg_ref, s_ref, do_ref, dq_ref, dk_ref, dv_ref, ds_s, acc_s, reverse):
        d = 1 if reverse else 0
        for hh in range(HEADS):
            sl = slice(DH * hh, DH * (hh + 1))
            relc, dm, wq, ws, g, pq, ps = _decay_tables(lg_ref[hh, d:d + 1, :], reverse)
            qb, kb, vb = q_ref[:, sl], k_ref[:, sl], v_ref[:, sl]
            q, k = qb.astype(F32), kb.astype(F32)
            p = _dot_nt(qb, kb) * dm
            s = s_ref[hh, 0]
            dob = do_ref[:, sl].astype(BF16)
            dsn = ds_s[hh]
            dsb = dsn.astype(BF16)
            dv_ref[:, sl] = (_dot_tn(p.astype(BF16), dob) + _dot((k * ws).astype(BF16), dsb)).astype(BF16)
            dp = _dot_nt(dob, vb)
            dab = (dp * dm).astype(BF16)
            xq = _dot_nt(dob, s.astype(BF16))
            yk = _dot_nt(vb, dsb)
            dq_ref[:, sl] = (_dot(dab, kb) + xq * wq).astype(BF16)
            dk_ref[:, sl] = (_dot_tn(dab, qb) + yk * ws).astype(BF16)
            ds_s[hh] = g * dsn + _dot_tn((q * wq).astype(BF16), dob)
            s_mask = _sum0(dp * p * relc)
            part = (sum(s_mask[:, LANES * u:LANES * (u + 1)] for u in range(CHUNK // LANES))
                    + _sum0(xq * q * wq * pq) + _sum0(yk * k * ws * ps) + _sum0(dsn * s) * g * float(CHUNK))
            acc_s[hh] += jnp.broadcast_to(part, (SUBLANES, LANES))

    def body(qf, kf, vf, qb, kb, vb, lg_ref, sg_ref, sf_ref, sb_ref, dof_ref, dob_ref,
             dqf, dkf, dvf, dqb, dkb, dvb, ds0f_ref, ds0b_ref, drdf_ref, drdb_ref, dsf_s, dsb_s, accf_s, accb_s):
        i = pl.program_id(0)

        @pl.when(i == 0)
        def _():
            for r in (dsf_s, dsb_s, accf_s, accb_s):
                r[...] = jnp.zeros_like(r)
        one(qf, kf, vf, lg_ref, sf_ref, dof_ref, dqf, dkf, dvf, dsf_s, accf_s, False)
        one(qb, kb, vb, lg_ref, sb_ref, dob_ref, dqb, dkb, dvb, dsb_s, accb_s, True)

        @pl.when(i == n - 1)
        def _():
            ds0f_ref[...] = dsf_s[...]
            ds0b_ref[...] = dsb_s[...]
            for d, (acc_s, drd_ref) in enumerate(((accf_s, drdf_ref), (accb_s, drdb_ref))):
                for hh in range(HEADS):
                    tot = jnp.sum(acc_s[hh, 0:1, :], axis=1, keepdims=True)
                    drd_ref[hh] = jnp.broadcast_to(tot, (SUBLANES, LANES)) * sg_ref[hh, d:d + 1, :]

    blk = (CHUNK, RET_W)
    fw = lambda o: pl.BlockSpec(blk, lambda i, o=o: (n - 1 - i, o))
    bw = lambda o: pl.BlockSpec(blk, lambda i, o=o: (i, o))
    lane = _full((HEADS, 2, LANES))
    st = _full((HEADS, DH, DH))
    rd = _full((HEADS, SUBLANES, LANES))
    outs, couts = _call(
        body, name="ret_bwd", grid=(n,),
        in_specs=[fw(0), fw(1), fw(2), bw(0), bw(1), bw(2), lane, lane,
                  pl.BlockSpec((HEADS, 1, DH, DH), lambda i: (0, n - 1 - i, 0, 0)),
                  pl.BlockSpec((HEADS, 1, DH, DH), lambda i: (0, i, 0, 0)), fw(0), bw(0)],
        out_specs=[fw(0), fw(0), fw(0), bw(0), bw(0), bw(0), st, st, rd, rd],
        out_shape=[_sds((t, RET_W), BF16)] * 6 + [_sds((HEADS, DH, DH))] * 2 + [_sds((HEADS, SUBLANES, LANES))] * 2,
        scratch_shapes=[pltpu.VMEM((HEADS, DH, DH), F32)] * 2 + [pltpu.VMEM((HEADS, SUBLANES, LANES), F32)] * 2,
        sem=("arbitrary",), args=(proj, proj, proj, proj, proj, proj, lgv, sgv, spf, spb, do, do), comms=comms)
    dqf, dkf, dvf, dqb, dkb, dvb, ds0f, ds0b, drdf, drdb = outs
    return ((dqf, dkf, dvf, ds0f, drdf), (dqb, dkb, dvb, ds0b, drdb)), couts


def _ctx_weights(lg, l_len, reverse):
    pos = lax.broadcasted_iota(jnp.int32, (l_len, DH), 0).astype(F32)
    steps = pos if reverse else (l_len - 1.0 - pos)
    return jnp.exp(lg * steps), steps


def _ctx_state_fwd(projc, lgv):
    l_len = projc.shape[0]

    def body(k_ref, v_ref, lg_ref, sf_ref, sb_ref):
        k = k_ref[...]
        vb = v_ref[...].astype(BF16)
        for d, o_ref in ((0, sf_ref), (1, sb_ref)):
            w, _ = _ctx_weights(lg_ref[0, d:d + 1, :], l_len, d == 1)
            o_ref[0] = _dot_tn((k * w).astype(BF16), vb)

    st = pl.BlockSpec((1, DH, DH), lambda h: (h, 0, 0))
    return _pc(body, name="ctx_state_fwd", grid=(HEADS,),
               in_specs=[pl.BlockSpec((l_len, DH), lambda h: (0, HEADS + h)),
                         pl.BlockSpec((l_len, DH), lambda h: (0, 2 * HEADS + h)),
                         pl.BlockSpec((1, 2, LANES), lambda h: (h, 0, 0))],
               out_specs=[st, st], out_shape=[_sds((HEADS, DH, DH))] * 2,
               compiler_params=_params("arbitrary"))(projc, projc, lgv)


def _ctx_state_bwd(projc, lgv, sgv, dsf, dsb):
    l_len = projc.shape[0]

    def body(k_ref, v_ref, lg_ref, sg_ref, dsf_ref, dsb_ref, dk_ref, dv_ref, drd_ref):
        k = k_ref[...]
        vb = v_ref[...].astype(BF16)
        dk = jnp.zeros((l_len, DH), F32)
        dv = jnp.zeros((l_len, DH), F32)
        rows = []
        for d, ds_ref in ((0, dsf_ref), (1, dsb_ref)):
            w, steps = _ctx_weights(lg_ref[0, d:d + 1, :], l_len, d == 1)
            dsb16 = ds_ref[0].astype(BF16)
            dkw = _dot_nt(vb, dsb16)
            dk = dk + dkw * w
            dv = dv + _dot((k * w).astype(BF16), dsb16)
            tot = jnp.sum(_sum0(dkw * k * w * steps), axis=1, keepdims=True)
            rows.append(jnp.broadcast_to(tot, (1, LANES)) * sg_ref[0, d:d + 1, :])
        dk_ref[...] = dk.astype(BF16)
        dv_ref[...] = dv.astype(BF16)
        rid = lax.broadcasted_iota(jnp.int32, (SUBLANES, LANES), 0)
        drd_ref[0] = jnp.where(rid == 0, rows[0], jnp.where(rid == 1, rows[1], 0.0))

    st = pl.BlockSpec((1, DH, DH), lambda h: (h, 0, 0))
    lane = pl.BlockSpec((1, 2, LANES), lambda h: (h, 0, 0))
    hc = pl.BlockSpec((l_len, DH), lambda h: (0, h))
    return _pc(body, name="ctx_state_bwd", grid=(HEADS,),
               in_specs=[pl.BlockSpec((l_len, DH), lambda h: (0, HEADS + h)),
                         pl.BlockSpec((l_len, DH), lambda h: (0, 2 * HEADS + h)), lane, lane, st, st],
               out_specs=[hc, hc, pl.BlockSpec((1, SUBLANES, LANES), lambda h: (h, 0, 0))],
               out_shape=[_sds((l_len, RET_W), BF16), _sds((l_len, RET_W), BF16), _sds((HEADS, SUBLANES, LANES))],
               compiler_params=_params("arbitrary"))(projc, projc, lgv, sgv, dsf, dsb)


G_BLOCK = (3 * RET_W) // RET_W
GATE_BLOCK = (4 * RET_W + LRU_W) // LRU_W


def _head_norm(y):
    yc = y - jnp.mean(y, axis=-1, keepdims=True)
    rs = lax.rsqrt(jnp.mean(yc * yc, axis=-1, keepdims=True) + EPS)
    return yc * rs, rs


def _gelu_parts(z):
    th = jnp.tanh(GELU_K * (z + GELU_C * z * z * z))
    return 0.5 * z * (1.0 + th), th


def _mix_fwd(o_f, o_b, proj, hf, hb, w_out, x, g1):
    t = x.shape[0]
    tm = _tile(t, True)

    def body(of_ref, ob_ref, g_ref, gt_ref, hf_ref, hb_ref, w_ref, x_ref, g1_ref, x1_ref, cat_ref):
        o = of_ref[...] + ob_ref[...]
        g = g_ref[...].astype(F32)
        for hh in range(HEADS):
            sl = slice(DH * hh, DH * (hh + 1))
            nrm, _ = _head_norm(o[:, sl])
            gh = g[:, sl]
            cat_ref[:, sl] = (gh * _sigmoid(gh) * nrm).astype(BF16)
        gel, _ = _gelu_parts(gt_ref[...].astype(F32))
        cat_ref[:, RET_W:] = ((hf_ref[...] + hb_ref[...]) * gel).astype(BF16)
        x1_ref[...] = x_ref[...] + g1_ref[...] * _dot(cat_ref[...], w_ref[...])

    half = pl.BlockSpec((tm, RET_W), lambda i: (i, 0))
    big = pl.BlockSpec((tm, D_MODEL), lambda i: (i, 0))
    return _pc(body, name="mix_fwd", grid=(t // tm,),
               in_specs=[half, half, pl.BlockSpec((tm, RET_W), lambda i: (i, G_BLOCK)),
                         pl.BlockSpec((tm, LRU_W), lambda i: (i, GATE_BLOCK)), half, half,
                         _full((D_MODEL, D_MODEL)), big, _full((1, D_MODEL))],
               out_specs=[big, big], out_shape=[_sds((t, D_MODEL)), _sds((t, D_MODEL), BF16)],
               compiler_params=_params("arbitrary"))(o_f, o_b, proj, proj, hf, hb, w_out, x, g1)


def _mix_bwd(o_f, o_b, proj, hf, hb, w_out, cat, dx1, g1, comms=()):
    t = dx1.shape[0]
    tm = _tile(t, True)

    def body(of_ref, ob_ref, g_ref, gt_ref, hf_ref, hb_ref, w_ref, cat_ref, dx1_ref, g1_ref,
             do_ref, dhs_ref, dg_ref, dgt_ref, dyb_ref, dg1_ref):
        dx1v = dx1_ref[...]
        y = _dot(cat_ref[...], w_ref[...])

        @pl.when(pl.program_id(0) == 0)
        def _():
            dg1_ref[...] = jnp.zeros_like(dg1_ref)
        dg1_ref[...] += _sum0(dx1v * y)
        dyb = (g1_ref[...] * dx1v).astype(BF16)
        dyb_ref[...] = dyb
        dcat = _dot_nt(dyb, w_ref[...])
        o = of_ref[...] + ob_ref[...]
        g = g_ref[...].astype(F32)
        for hh in range(HEADS):
            sl = slice(DH * hh, DH * (hh + 1))
            nrm, rs = _head_norm(o[:, sl])
            gh = g[:, sl]
            sg = _sigmoid(gh)
            dret = dcat[:, sl]
            dg_ref[:, sl] = (dret * nrm * (sg * (1.0 + gh * (1.0 - sg)))).astype(BF16)
            dn = dret * (gh * sg)
            dyc = rs * (dn - nrm * jnp.mean(dn * nrm, axis=-1, keepdims=True))
            do_ref[:, sl] = (dyc - jnp.mean(dyc, axis=-1, keepdims=True)).astype(BF16)
        z = gt_ref[...].astype(F32)
        gel, th = _gelu_parts(z)
        dlru = dcat[:, RET_W:]
        dhs_ref[...] = dlru * gel
        dgel = 0.5 * (1.0 + th) + 0.5 * z * (1.0 - th * th) * GELU_K * (1.0 + 3.0 * GELU_C * z * z)
        dgt_ref[...] = (dlru * (hf_ref[...] + hb_ref[...]) * dgel).astype(BF16)

    half = pl.BlockSpec((tm, RET_W), lambda i: (i, 0))
    big = pl.BlockSpec((tm, D_MODEL), lambda i: (i, 0))
    return _call(body, name="mix_bwd", grid=(t // tm,),
                 in_specs=[half, half, pl.BlockSpec((tm, RET_W), lambda i: (i, G_BLOCK)),
                           pl.BlockSpec((tm, LRU_W), lambda i: (i, GATE_BLOCK)), half, half,
                           _full((D_MODEL, D_MODEL)), big, big, _full((1, D_MODEL))],
                 out_specs=[half, half, half, half, big, _full((1, D_MODEL))],
                 out_shape=[_sds((t, RET_W), BF16), _sds((t, RET_W)), _sds((t, RET_W), BF16), _sds((t, RET_W), BF16),
                            _sds((t, D_MODEL), BF16), _sds((1, D_MODEL))],
                 scratch_shapes=[], sem=("arbitrary",), args=(o_f, o_b, proj, proj, hf, hb, w_out, cat, dx1, g1),
                 comms=comms)


def _mlp(x1, n2g, sh2, sc2, g2, fg, w1_parts, w2_parts, tgt):
    t = x1.shape[0]
    tm = _tile(t)
    hb_ = MLP_H // N_CHIP
    q_rows = hb_ // 4
    n_cp = 4 * N_DEV

    def body(x1_ref, n2g_ref, sh2_ref, sc2_ref, g2_ref, fg_ref, w1a, w1b, w2a, w2b, tgt_ref,
             dx1_ref, h2b_ref, ab_ref, dub_ref, dmb_ref, dsc_ref, dsh_ref, dg2_ref, dn2_ref, dfg_ref, loss_ref,
             w1_s, w2_s, r_s, sems):
        @pl.when(pl.program_id(0) == 0)
        def _():
            cps = []
            for p, parts in enumerate(((w1a, w2a), (w1b, w2b))):
                for d in range(N_DEV):
                    rows = pl.ds(2 * q_rows * (d % 2) + q_rows * p, q_rows)
                    for src, dst in zip(parts, (w1_s, w2_s)):
                        cps.append(pltpu.make_async_copy(src.at[d], dst.at[d // 2, rows], sems.at[len(cps)]))
            for cp in cps:
                cp.start()
            for r in (dsc_ref, dsh_ref, dg2_ref, dn2_ref, dfg_ref, loss_ref):
                r[...] = jnp.zeros_like(r)
            for cp in cps:
                cp.wait()
        x1v = x1_ref[...]
        n2g, sc2, g2, fg = n2g_ref[...], sc2_ref[...], g2_ref[...], fg_ref[...]
        xh, _ = _rms(x1v)
        h2b = (xh * n2g * (1.0 + sc2) + sh2_ref[...]).astype(BF16)
        h2b_ref[...] = h2b
        m = jnp.zeros((tm, D_MODEL), F32)
        for j in range(N_CHIP):
            sl = slice(hb_ * j, hb_ * (j + 1))
            r = jnp.maximum(_dot(h2b, w1_s[j]), 0.0)
            r_s[:, sl] = r
            ab = (r * r).astype(BF16)
            ab_ref[:, sl] = ab
            m = m + _dot(ab, w2_s[j])
        x2 = x1v + g2 * m
        x2h, r2 = _rms(x2)
        err = x2h * fg - tgt_ref[...]
        loss_ref[...] += _sum0(err * err)
        dout = err * (1.0 / D_MODEL)
        dfg_ref[...] += _sum0(dout * x2h)
        dxh = dout * fg
        dx2 = r2 * (dxh - x2h * jnp.mean(dxh * x2h, axis=-1, keepdims=True))
        dg2_ref[...] += _sum0(dx2 * m)
        dmb = (g2 * dx2).astype(BF16)
        dmb_ref[...] = dmb
        dh2 = jnp.zeros((tm, D_MODEL), F32)
        for j in range(N_CHIP):
            sl = slice(hb_ * j, hb_ * (j + 1))
            dub = (_dot_nt(dmb, w2_s[j]) * (2.0 * r_s[:, sl])).astype(BF16)
            dub_ref[:, sl] = dub
            dh2 = dh2 + _dot_nt(dub, w1_s[j])
        dx, dn2_t, dsh_t, dsc_t = _norm_mod_bwd(x1v, n2g, sc2, dh2)
        dx1_ref[...] = dx2 + dx
        dn2_ref[...] += dn2_t
        dsh_ref[...] += dsh_t
        dsc_ref[...] += dsc_t

        @pl.when(pl.program_id(0) == t // tm - 1)
        def _():
            tot = jnp.sum(loss_ref[...], axis=1, keepdims=True) * (0.5 / D_MODEL)
            loss_ref[...] = jnp.broadcast_to(tot, loss_ref.shape)

    row = _full((1, D_MODEL))
    big = pl.BlockSpec((tm, D_MODEL), lambda i: (i, 0))
    wide = pl.BlockSpec((tm, MLP_H), lambda i: (i, 0))
    return _pc(body, name="mlp", grid=(t // tm,),
               in_specs=[big, row, row, row, row, row, ANY, ANY, ANY, ANY, big],
               out_specs=[big, big, wide, wide, big, row, row, row, row, row, row],
               out_shape=[_sds((t, D_MODEL)), _sds((t, D_MODEL), BF16), _sds((t, MLP_H), BF16), _sds((t, MLP_H), BF16),
                          _sds((t, D_MODEL), BF16)] + [_sds((1, D_MODEL))] * 6,
               scratch_shapes=[pltpu.VMEM((N_CHIP, D_MODEL, hb_), BF16), pltpu.VMEM((N_CHIP, hb_, D_MODEL), BF16),
                               pltpu.VMEM((tm, MLP_H), F32), pltpu.SemaphoreType.DMA((n_cp,))],
               compiler_params=_params("arbitrary"))(x1, n2g, sh2, sc2, g2, fg, *w1_parts, *w2_parts, tgt)


def _tn(a, b, nj, a_blocked, b_blocked, name, extra=None, comms=()):
    t = a.shape[0]
    m = a.shape[1] // (nj if a_blocked else 1)
    n = b.shape[1] // (nj if b_blocked else 1)
    bk = next((b for b in (2048, 1024, 512) if t % b == 0), t)
    nk = t // bk
    a_col = (lambda j: j) if a_blocked else (lambda j: 0)
    b_col = (lambda j: j) if b_blocked else (lambda j: 0)
    in_specs = [pl.BlockSpec((bk, m), lambda j, k: (k, a_col(j))), pl.BlockSpec((bk, n), lambda j, k: (k, b_col(j)))]
    args = [a, b]
    if extra is not None:
        a2, b2 = extra
        t2 = a2.shape[0]
        in_specs += [pl.BlockSpec((t2, m), lambda j, k: (0, a_col(j))),
                     pl.BlockSpec((t2, n), lambda j, k: (0, b_col(j)))]
        args += [a2, b2]

    def body(*refs):
        a_ref, b_ref = refs[0], refs[1]
        o_ref, acc = refs[-2], refs[-1]
        k = pl.program_id(1)

        @pl.when(k == 0)
        def _():
            acc[...] = jnp.zeros_like(acc)
        acc[...] += _dot_tn(a_ref[...].astype(BF16), b_ref[...].astype(BF16))

        @pl.when(k == nk - 1)
        def _():
            if extra is not None:
                acc[...] += _dot_tn(refs[2][...].astype(BF16), refs[3][...].astype(BF16))
            o_ref[0] = acc[...]

    (out,), couts = _call(body, name=name, grid=(nj, nk), in_specs=in_specs,
                          out_specs=[pl.BlockSpec((1, m, n), lambda j, k: (j, 0, 0))], out_shape=[_sds((nj, m, n))],
                          scratch_shapes=[pltpu.VMEM((m, n), F32)], sem=("arbitrary", "arbitrary"), args=args,
                          comms=comms)
    return (out, couts) if comms else out


ROW_LOSS = 0
ROW_DMOD = 1
ROW_DMODC = 7
ROW_N1, ROW_N2, ROW_FG, ROW_CB = 9, 10, 11, 12
ROW_BA, ROW_BX, ROW_LAM = 13, 15, 17
ROW_CW = 20
ROW_RD = 24
SLAB_ROWS = 32
SEG = D_MODEL // 2


def _pack_small(rows, drd, cw2, cb2, lru2, gates):
    n_rows, n_lru = len(rows), len(lru2)

    def body(*refs):
        r = refs[:n_rows]
        drd_f, drd_b, drd_c, cw_a, cw_b, cb_a, cb_b = refs[n_rows:n_rows + 7]
        lru = refs[n_rows + 7:n_rows + 7 + n_lru]
        gf_ref, gb_ref, slab, ga, gx = refs[n_rows + 7 + n_lru:]
        slab[...] = jnp.zeros_like(slab)
        slab[ROW_LOSS:ROW_LOSS + 1, :] = r[0][...]
        for k in range(N_MOD):
            slab[ROW_DMOD + k:ROW_DMOD + k + 1, :] = r[1 + k][...]
        slab[ROW_DMODC:ROW_DMODC + 1, :] = r[7][...]
        slab[ROW_DMODC + 1:ROW_DMODC + 2, :] = r[8][...]
        slab[ROW_N1:ROW_N1 + 1, :] = r[9][...] + r[10][...]
        slab[ROW_N2:ROW_N2 + 1, :] = r[11][...]
        slab[ROW_FG:ROW_FG + 1, :] = r[12][...]
        slab[ROW_CB:ROW_CB + 1, 0:LRU_W] = cb_a[...] + cb_b[...]
        for k, row in enumerate((ROW_BA, ROW_BA + 1, ROW_BX, ROW_BX + 1, ROW_LAM, ROW_LAM + 1)):
            slab[row:row + 1, 0:LRU_W] = lru[2 * k][...] + lru[2 * k + 1][...]
        slab[ROW_CW:ROW_CW + 4, 0:LRU_W] = cw_a[...] + cw_b[...]
        for h in range(HEADS):
            slab[ROW_RD + h:ROW_RD + h + 1, 0:LANES] = drd_f[h, 0:1, :] + drd_c[h, 0:1, :]
            slab[ROW_RD + HEADS + h:ROW_RD + HEADS + h + 1, 0:LANES] = drd_b[h, 0:1, :] + drd_c[h, 1:2, :]
        for d, g_ref in enumerate((gf_ref, gb_ref)):
            for n in range(LRU_BLOCKS):
                blk = slice(LRU_BD * n, LRU_BD * (n + 1))
                ga[blk, LRU_BD * d:LRU_BD * (d + 1)] = g_ref[0, blk, blk].astype(BF16)
                gx[blk, LRU_BD * d:LRU_BD * (d + 1)] = g_ref[1, blk, blk].astype(BF16)

    args = list(rows) + list(drd) + list(cw2) + list(cb2) + list(lru2) + list(gates)
    gate_shape = (LRU_W, 2 * LRU_BD)
    return _pc(body, name="pack_small", in_specs=[_full(a.shape) for a in args],
               out_specs=[_full((SLAB_ROWS, D_MODEL)), _full(gate_shape), _full(gate_shape)],
               out_shape=[_sds((SLAB_ROWS, D_MODEL)), _sds(gate_shape, BF16), _sds(gate_shape, BF16)],
               compiler_params=_params())(*args)


def _adam_math(w, g, m, v):
    mn = ADAM_B1 * m + (1.0 - ADAM_B1) * g
    vn = ADAM_B2 * v + (1.0 - ADAM_B2) * (g * g)
    mh = mn / (1.0 - ADAM_B1 ** ADAM_STEP)
    vh = vn / (1.0 - ADAM_B2 ** ADAM_STEP)
    return -ADAM_LR * (mh / (jnp.sqrt(vh) + ADAM_EPS) + ADAM_WD * w), mn, vn


SMALL_PARAMS = ("b_ada", "norm1_g", "norm2_g", "final_g", "ret_decay", "conv_w", "conv_b", "lru_wa", "lru_ba", "lru_wx",
                "lru_bx", "lru_lambda")


def _finalize_small(chip_idx, slab_all, ga_all, gx_all, wmv):
    n_p = len(SMALL_PARAMS)
    flat = [a for nm in SMALL_PARAMS for a in wmv[nm]]
    ada_n = N_MOD * D_MODEL // N_CHIP

    def body(c_ref, slab_ref, ga_ref, gx_ref, *refs):
        prm = {nm: refs[3 * k:3 * k + 3] for k, nm in enumerate(SMALL_PARAMS)}
        outs = {nm: refs[3 * n_p + 4 * k:3 * n_p + 4 * k + 4] for k, nm in enumerate(SMALL_PARAMS)}
        b128_ref, dmc_ref, loss_ref = refs[3 * n_p + 4 * n_p:]
        chip = c_ref[0]

        def pick(fn):
            acc = fn(0)
            for j in range(1, N_CHIP):
                acc = jnp.where(chip == j, fn(j), acc)
            return acc

        tot = slab_ref[0]
        for d in range(1, N_DEV):
            tot = tot + slab_ref[d]

        def update(nm, g, sl=None, rows=None):
            w_ref, m_ref, v_ref = prm[nm]
            g_ref, d_ref, mo_ref, vo_ref = outs[nm]
            ix = (slice(None) if rows is None else rows, slice(None) if sl is None else sl)
            dl, mn, vn = _adam_math(w_ref[ix], g, m_ref[ix], v_ref[ix])
            g_ref[ix] = g
            d_ref[ix] = dl
            mo_ref[ix] = mn
            vo_ref[ix] = vn

        loss_ref[...] = jnp.broadcast_to(tot[ROW_LOSS:ROW_LOSS + 1, 0:LANES], (SUBLANES, LANES))
        for k in range(N_MOD):
            g = tot[ROW_DMOD + k:ROW_DMOD + k + 1, :]
            if k < 2:
                g = g + tot[ROW_DMODC + k:ROW_DMODC + k + 1, :]
            update("b_ada", g, slice(D_MODEL * k, D_MODEL * (k + 1)))
        update("norm1_g", tot[ROW_N1:ROW_N1 + 1, :])
        update("norm2_g", tot[ROW_N2:ROW_N2 + 1, :])
        update("final_g", tot[ROW_FG:ROW_FG + 1, :])
        update("ret_decay", tot[ROW_RD:ROW_RD + SUBLANES, 0:LANES])
        update("conv_b", tot[ROW_CB:ROW_CB + 1, 0:LRU_W])
        update("conv_w", pick(lambda j: tot[ROW_CW:ROW_CW + 4, LANES * j:LANES * (j + 1)]))
        for nm, row in (("lru_ba", ROW_BA), ("lru_bx", ROW_BX), ("lru_lambda", ROW_LAM)):
            update(nm, pick(lambda j, row=row: tot[row:row + 2, LANES * j:LANES * (j + 1)]))
        for nm, g_all in (("lru_wa", ga_ref), ("lru_wx", gx_ref)):
            for dr in range(2):
                lanes = slice(LRU_BD * dr, LRU_BD * (dr + 1))
                g = g_all[0, :, lanes].astype(F32)
                for d in range(1, N_DEV):
                    g = g + g_all[d, :, lanes].astype(F32)
                update(nm, g, rows=slice(LRU_W * dr, LRU_W * (dr + 1)))

        def seg(rows6, s):
            return rows6[s // 2][:, SEG * (s % 2):SEG * (s % 2 + 1)]

        b128_ref[...] = jnp.zeros_like(b128_ref)
        dmc_ref[...] = jnp.zeros_like(dmc_ref)
        zero = jnp.zeros((1, D_MODEL), F32)
        ctx6 = [tot[ROW_DMODC:ROW_DMODC + 1, :], tot[ROW_DMODC + 1:ROW_DMODC + 2, :]] + [zero] * (N_MOD - 2)
        for q in range(ada_n // SEG):
            cols = slice(SEG * q, SEG * (q + 1))
            for d in range(N_DEV):
                rows6 = [slab_ref[d, ROW_DMOD + k:ROW_DMOD + k + 1, :] for k in range(N_MOD)]
                b128_ref[d:d + 1, cols] = pick(lambda j, rows6=rows6: seg(rows6, 3 * j + q))
            c = pick(lambda j: seg(ctx6, 3 * j + q))
            b128_ref[N_DEV:N_DEV + 1, cols] = c
            dmc_ref[0:1, cols] = c

    out_shape = []
    for nm in SMALL_PARAMS:
        out_shape += [_sds(wmv[nm][0].shape)] * 4
    out_shape += [_sds((LANES, ada_n)), _sds((SUBLANES, ada_n)), _sds((SUBLANES, LANES))]
    args = [slab_all, ga_all, gx_all] + flat
    grid_spec = pltpu.PrefetchScalarGridSpec(
        num_scalar_prefetch=1, grid=(1,), in_specs=[_full(a.shape) for a in args],
        out_specs=[_full(s.shape) for s in out_shape])
    outs = _pc(body, name="finalize_small", grid_spec=grid_spec, out_shape=out_shape,
               compiler_params=_params("arbitrary"))(chip_idx, *args)
    res = {nm: tuple(outs[4 * k:4 * k + 4]) for k, nm in enumerate(SMALL_PARAMS)}
    return res, outs[4 * n_p], outs[4 * n_p + 1], outs[4 * n_p + 2]


def _block_diag(w):
    eye = jnp.eye(LRU_BLOCKS, dtype=F32)
    return (w[:, :, None, :] * eye[:, None, :, None]).reshape(LRU_W, LRU_W).astype(BF16)


def _lane_rep(v8):
    return jnp.broadcast_to(v8.reshape(SUBLANES, 1), (SUBLANES, LANES))


def kernel(x, c, ctx, c_ctx, w_ada, b_ada, norm1_g, norm2_g, w_in, ret_decay, conv_w, conv_b, lru_wa, lru_ba, lru_wx, lru_bx, lru_lambda, w_out, w_mlp1, w_mlp2, final_g, loss_target, m_c_ctx, m_w_ada, m_b_ada, m_norm1_g, m_norm2_g, m_w_in, m_ret_decay, m_conv_w, m_conv_b, m_lru_wa, m_lru_ba, m_lru_wx, m_lru_bx, m_lru_lambda, m_w_out, m_w_mlp1, m_w_mlp2, m_final_g, v_c_ctx, v_w_ada, v_b_ada, v_norm1_g, v_norm2_g, v_w_in, v_ret_decay, v_conv_w, v_conv_b, v_lru_wa, v_lru_ba, v_lru_wx, v_lru_bx, v_lru_lambda, v_w_out, v_w_mlp1, v_w_mlp2, v_final_g):
    ax, ay, ac = lax.axis_index("x"), lax.axis_index("y"), lax.axis_index("c")
    chip = 2 * ax + ay
    dev = 4 * ax + 2 * ay + ac
    c_idx = jnp.stack([ac, chip]).astype(jnp.int32)
    j_idx = chip.reshape(1).astype(jnp.int32)

    xt = x[0]
    t_len = xt.shape[0]
    ctxt = ctx[0]
    l_len = ctxt.shape[0]
    tgt = loss_target[0]
    ada_n = w_ada.shape[2]

    def my_half(w2d):
        r = w2d.shape[0] // 2
        return lax.dynamic_slice_in_dim(w2d, ac * r, r, axis=0).astype(BF16)

    pad8 = lambda a: jnp.pad(a, ((0, SUBLANES - a.shape[0]), (0, 0)))
    small = jnp.concatenate([pad8(conv_w[0]), pad8(lru_ba[0]), pad8(lru_bx[0]), pad8(lru_lambda[0])], axis=0)
    b_shard = lax.dynamic_slice_in_dim(b_ada, chip * ada_n, ada_n, axis=1)
    gw_in, _, small_all, a16, mod_parts, lgv, sgv = _head(
        my_half(w_in[0]), pad8(c), small, w_ada[0], b_shard, c_ctx, ret_decay[0])
    w4 = gw_in.reshape(N_CHIP, D_MODEL, IN_COLS // N_CHIP)

    mod_all = mod_parts[0::2].transpose(1, 0, 2).reshape(16, N_CHIP * ada_n)
    mod_me = lax.dynamic_slice_in_dim(mod_all, dev, 1, axis=0)
    sh1, sc1, g1, sh2, sc2, g2 = [mod_me[:, D_MODEL * k:D_MODEL * (k + 1)] for k in range(N_MOD)]
    csh1, csc1 = mod_all[8:9, 0:D_MODEL], mod_all[8:9, D_MODEL:2 * D_MODEL]

    cos2, sin2 = _rotary_tables(t_len)
    cos_c, sin_c = jnp.ones((l_len, DH), F32), jnp.zeros((l_len, DH), F32)
    n1g, n2g = norm1_g, norm2_g
    fg = final_g.reshape(1, D_MODEL)

    small_full = small_all[0::2].transpose(1, 0, 2).reshape(4 * SUBLANES, LRU_W)
    cw = small_full[0:4]
    cb = conv_b
    ba_f, ba_b = small_full[8:9], small_full[9:10]
    bx_f, bx_b = small_full[16:17], small_full[17:18]
    lam_f, lam_b = small_full[24:25], small_full[25:26]
    wa_f, wa_b = _block_diag(lru_wa[0, 0]), _block_diag(lru_wa[0, 1])
    wx_f, wx_b = _block_diag(lru_wx[0, 0]), _block_diag(lru_wx[0, 1])
    zero_h = jnp.zeros((1, LRU_W), F32)

    projc, xrc, hcb16 = _inproj_fwd(ctxt, n1g, csh1, csc1, w4, cos_c, sin_c, "inproj_fwd_ctx")
    s_f, s_b = _ctx_state_fwd(projc, lgv)
    xcc = _conv_fwd(xrc, cw, cb, "conv_fwd_ctx")
    par_f, par_b = (wa_f, wx_f, ba_f, bx_f, lam_f), (wa_b, wx_b, ba_b, bx_b, lam_b)
    hcf, hcbk = _lru_fwd(xcc, par_f, par_b, zero_h, zero_h, "lru_fwd_ctx")
    lru_sf, lru_sb = hcf[l_len - 1:l_len], hcbk[0:1]

    h1, h2 = my_half(w_mlp1[0]), my_half(w_mlp2[0])
    q = h1.shape[0] // 2
    (proj, xrl, hb16), ((gw_1a,),) = _inproj_fwd(xt, n1g, sh1, sc1, w4, cos2, sin2, "inproj_fwd",
                                           comms=(_AllGather([h1[:q]]),))
    (o_f, o_b, spf, spb), ((gw_1b, gw_out),) = _ret_fwd(proj, lgv, s_f, s_b,
                                                       comms=(_AllGather([h1[q:], my_half(w_out[0])]),))
    xcl = _conv_fwd(xrl, cw, cb, "conv_fwd")
    (hf, hbk), ((gw_2a, gw_2b),) = _lru_fwd(xcl, par_f, par_b, lru_sf, lru_sb, "lru_fwd",
                                           comms=(_AllGather([h2[:q], h2[q:]]),))
    wo = gw_out.reshape(D_MODEL, D_MODEL)
    x1, cat = _mix_fwd(o_f, o_b, proj, hf, hbk, wo, xt, g1)

    (dx1, h2b, ab, dub, dmb, dsc2, dsh2, dg2, dn2g, dfg, lossv) = _mlp(
        x1, n2g, sh2, sc2, g2, fg, (gw_1a, gw_1b), (gw_2a, gw_2b), tgt)
    gw_mlp1 = _tn(h2b, dub, N_CHIP, False, True, "grad_w_mlp1")
    b_1 = gw_mlp1.reshape(N_DEV, D_MODEL // 2, MLP_H // N_CHIP)
    gw_mlp2, ((r_1,),) = _tn(ab, dmb, N_CHIP, True, False, "grad_w_mlp2", comms=(_pair_exchange([b_1]),))

    half = D_MODEL // 4
    top, bot = (0, half), (half, half)
    b_2 = gw_mlp2.reshape(N_DEV, MLP_H // N_DEV, D_MODEL)
    p_1, pb_1 = _pair_add(b_1, r_1, c_idx, "rs_pair_add_w_mlp1")
    (do, dhs, dg, dgate, dyb, dg1), ((q_1a,), (r_2,)) = _mix_bwd(
        o_f, o_b, proj, hf, hbk, wo, cat, dx1, g1, comms=(_chip_exchange([pb_1], top), _pair_exchange([b_2])))
    gw_o = _tn(cat, dyb, 1, False, False, "grad_w_out")
    b_o = gw_o.reshape(N_DEV, D_MODEL // N_DEV, D_MODEL)
    p_2, pb_2 = _pair_add(b_2, r_2, c_idx, "rs_pair_add_w_mlp2")

    ((dq_f, dk_f, dv_f, ds_f, drd_f), (dq_b, dk_b, dv_b, ds_b, drd_b)), ((q_1b,), (q_2a,), (r_o,)) = _ret_bwd(
        proj, lgv, sgv, spf, spb, do,
        comms=(_chip_exchange([pb_1], bot), _chip_exchange([pb_2], top), _pair_exchange([b_o])))
    p_o, pb_o = _pair_add(b_o, r_o, c_idx, "rs_pair_add_w_out")
    h_1 = _chip_add(p_1, (q_1a, q_1b), c_idx, "rs_chip_add_w_mlp1")

    ((dxc_f, dpre_f, dba_f, dbx_f, dlam_f, dh0_f), (dxc_b, dpre_b, dba_b, dbx_b, dlam_b, dh0_b)), (
        (q_2b,), (q_o,), (f_1,)) = _lru_bwd(
        xcl, par_f, par_b, hf, hbk, lru_sf, lru_sb, dhs, dhs, "lru_bwd",
        comms=(_chip_exchange([pb_2], bot), _chip_exchange([pb_o]), _pair_gather([h_1])))
    h_2 = _chip_add(p_2, (q_2a, q_2b), c_idx, "rs_chip_add_w_mlp2")
    h_o = _chip_add(p_o, (q_o,), c_idx, "rs_chip_add_w_out")
    dxr, dcw, dcb = _conv_bwd(dxc_f, dxc_b, xrl, cw, "conv_bwd")
    grad_x, dpb, dn1g, dsh1, dsc1 = _inproj_bwd(
        xt, n1g, sh1, sc1, w4, cos2, sin2, [dq_f, dq_b, dk_f, dk_b, dv_f, dv_b, dg, dxr, dgate], dx1, "inproj_bwd")

    dkc, dvc, drd_c = _ctx_state_bwd(projc, lgv, sgv, ds_f, ds_b)
    zc = jnp.zeros((l_len, LRU_W), F32)
    dhc_f = lax.dynamic_update_slice(zc, dh0_f, (l_len - 1, 0))
    dhc_b = lax.dynamic_update_slice(zc, dh0_b, (0, 0))
    ((dxcc_f, dprec_f, dbac_f, dbxc_f, dlamc_f, _), (dxcc_b, dprec_b, dbac_b, dbxc_b, dlamc_b, _)), _ = _lru_bwd(
        xcc, par_f, par_b, hcf, hcbk, zero_h, zero_h, dhc_f, dhc_b, "lru_bwd_ctx")
    dxrc, dcw_c, dcb_c = _conv_bwd(dxcc_f, dxcc_b, xrc, cw, "conv_bwd_ctx")
    zr = jnp.zeros((l_len, RET_W), BF16)
    _, dpbc, dn1g_c, dcsh1, dcsc1 = _inproj_bwd(
        ctxt, n1g, csh1, csc1, w4, cos_c, sin_c, [zr, zr, dkc, zr, dvc, zr, zr, dxrc, zr],
        jnp.zeros((l_len, D_MODEL), F32), "inproj_bwd_ctx")

    gw_i = _tn(hb16, dpb, N_CHIP, False, True, "grad_w_in", extra=(hcb16, dpbc))
    b_i = gw_i.reshape(N_DEV, D_MODEL // 2, IN_COLS // N_CHIP)
    gwa_f, ((r_i,), (f_2,), (f_o,)) = _tn(xcl, dpre_f, 2, False, True, "grad_lru_gates_f", extra=(xcc, dprec_f),
                                          comms=(_pair_exchange([b_i]), _pair_gather([h_2]), _pair_gather([h_o])))
    p_i, pb_i = _pair_add(b_i, r_i, c_idx, "rs_pair_add_w_in")
    gwa_b, ((q_i,),) = _tn(xcl, dpre_b, 2, False, True, "grad_lru_gates_b", extra=(xcc, dprec_b),
                           comms=(_chip_exchange([pb_i]),))
    slab, ga, gx = _pack_small(
        [lossv, dsh1, dsc1, dg1, dsh2, dsc2, dg2, dcsh1, dcsc1, dn1g, dn1g_c, dn2g, dfg],
        (drd_f, drd_b, drd_c), (dcw, dcw_c), (dcb, dcb_c),
        (dba_f, dbac_f, dba_b, dbac_b, dbx_f, dbxc_f, dbx_b, dbxc_b, dlam_f, dlamc_f, dlam_b, dlamc_b),
        (gwa_f, gwa_b))
    (f_i,), (slab_all, ga_all, gx_all) = _run_comms(
        [_pair_gather([_chip_add(p_i, (q_i,), c_idx, "rs_chip_add_w_in")]), _AllGather([slab, ga, gx])],
        "tail_exchanges")
    g_in, g_out, g_1, g_2 = _shard_of(f_i), _shard_of(f_o), _shard_of(f_1), _shard_of(f_2)
    big = {}
    for nm, w, g, m, v in (("w_in", w_in, g_in, m_w_in, v_w_in), ("w_out", w_out, g_out, m_w_out, v_w_out),
                           ("w_mlp1", w_mlp1, g_1, m_w_mlp1, v_w_mlp1), ("w_mlp2", w_mlp2, g_2, m_w_mlp2, v_w_mlp2)):
        go, d_, mn, vn = _adamw(w[0], g, m[0], v[0], "adamw_" + nm)
        big[nm] = (go[None], d_[None], mn[None], vn[None])
    params = {
        "b_ada": (b_ada, m_b_ada, v_b_ada), "norm1_g": (norm1_g, m_norm1_g, v_norm1_g),
        "norm2_g": (norm2_g, m_norm2_g, v_norm2_g), "final_g": (final_g, m_final_g, v_final_g),
        "ret_decay": (ret_decay, m_ret_decay, v_ret_decay), "conv_w": (conv_w, m_conv_w, v_conv_w),
        "conv_b": (conv_b, m_conv_b, v_conv_b), "lru_wa": (lru_wa, m_lru_wa, v_lru_wa),
        "lru_ba": (lru_ba, m_lru_ba, v_lru_ba), "lru_wx": (lru_wx, m_lru_wx, v_lru_wx),
        "lru_bx": (lru_bx, m_lru_bx, v_lru_bx), "lru_lambda": (lru_lambda, m_lru_lambda, v_lru_lambda),
    }
    as2d = {
        "b_ada": lambda a: a, "norm1_g": lambda a: a, "norm2_g": lambda a: a, "conv_b": lambda a: a,
        "final_g": lambda a: a.reshape(1, D_MODEL), "ret_decay": lambda a: _lane_rep(a.reshape(-1)),
        "conv_w": lambda a: a[0], "lru_ba": lambda a: a[0], "lru_bx": lambda a: a[0], "lru_lambda": lambda a: a[0],
        "lru_wa": lambda a: a.reshape(2 * LRU_W, LRU_BD), "lru_wx": lambda a: a.reshape(2 * LRU_W, LRU_BD),
    }
    res, b128, dmc8, loss8 = _finalize_small(
        j_idx, slab_all, ga_all, gx_all, {nm: tuple(as2d[nm](a) for a in params[nm]) for nm in SMALL_PARAMS})
    loss = loss8[0, 0]
    small_out = {}
    for nm in SMALL_PARAMS:
        shp = params[nm][0].shape
        if nm == "ret_decay":
            small_out[nm] = tuple(o[:, 0].reshape(shp) for o in res[nm])
        else:
            small_out[nm] = tuple(o.reshape(shp) for o in res[nm])

    g_ada = _ada_grad(jnp.pad(a16.T, ((0, 0), (0, LANES - 16))), b128)
    g_ada, d_ada, m_ada, v_ada = _adamw(w_ada[0], g_ada, m_w_ada[0], v_w_ada[0], "adamw_w_ada")

    (cparts,) = _all_gather([_cctx_partial(dmc8, w_ada[0])], "gather_cctx")
    g_cc, d_cc, m_cc, v_cc = _cctx_final(cparts, c_ctx, m_c_ctx, v_c_ctx)
    small_out["c_ctx"] = tuple(a.reshape(D_MODEL) for a in (g_cc, d_cc, m_cc, v_cc))
    small_out["w_ada"] = (g_ada[None], d_ada[None], m_ada[None], v_ada[None])
    small_out.update(big)

    order = ["c_ctx", "w_ada", "b_ada", "norm1_g", "norm2_g", "w_in", "ret_decay", "conv_w", "conv_b", "lru_wa", "lru_ba",
             "lru_wx", "lru_bx", "lru_lambda", "w_out", "w_mlp1", "w_mlp2", "final_g"]
    outs = [loss, grad_x[None]]
    for k in range(4):
        outs += [small_out[nm][k] for nm in order]
    return tuple(outs)
```

```python
import math

import jax
import jax.numpy as jnp
from jax import lax
from jax.experimental import pallas as pl
from jax.experimental.pallas import tpu as pltpu

F32 = jnp.float32
BF16 = jnp.bfloat16

D_MODEL = 1024
HEADS = 4
DH = 128
CHUNK = 256
RET_W = HEADS * DH
LRU_W = 512
LRU_BLOCKS = 8
LRU_BD = LRU_W // LRU_BLOCKS
LRU_C = 8.0
IN_COLS = 4 * RET_W + 2 * LRU_W
MLP_H = 4 * D_MODEL
N_MOD = 6
GRID_W = 64
ROPE_BASE = 10000.0
K_SCALE = DH ** -0.5
EPS = 1e-6
GELU_K = math.sqrt(2.0 / math.pi)
GELU_C = 0.044715

ADAM_LR = 0.001
ADAM_B1 = 0.9
ADAM_B2 = 0.999
ADAM_EPS = 1e-08
ADAM_WD = 0.01
ADAM_STEP = 10

N_DEV = 8
N_CHIP = 4
SUBLANES = 8
LANES = 128
VMEM_LIMIT_V7X = 56 * 1024 * 1024
MESH = pl.DeviceIdType.MESH
ANY = pl.BlockSpec(memory_space=pl.ANY)


def _pc(body, **kw):
    return pl.pallas_call(body, **kw)


def _params(*sem):
    return pltpu.CompilerParams(dimension_semantics=sem if sem else None, vmem_limit_bytes=VMEM_LIMIT_V7X)


def _tile(t, big=False):
    if big and t >= 1024:
        return 512
    return 256 if t >= 256 else t


def _sds(shape, dtype=F32):
    return jax.ShapeDtypeStruct(tuple(shape), dtype)


def _full(shape):
    nd = len(shape)
    return pl.BlockSpec(tuple(shape), lambda *_: (0,) * nd)


def _sigmoid(x):
    return 0.5 * jnp.tanh(0.5 * x) + 0.5


def _log1p_pos(y):
    s = y * (1.0 - y * (0.5 - y * (1.0 / 3.0 - y * (0.25 - y * (0.2 - y / 6.0)))))
    return jnp.where(y < 0.03, s, jnp.log(1.0 + y))


def _softplus(z):
    return jnp.maximum(z, 0.0) + _log1p_pos(jnp.exp(-jnp.abs(z)))


def _one_minus_sq(la, a):
    return -jnp.tanh(la) * (1.0 + a * a)


def _rms(x):
    r = lax.rsqrt(jnp.mean(x * x, axis=-1, keepdims=True) + EPS)
    return x * r, r


def _dot(a, b):
    return jnp.dot(a, b, preferred_element_type=F32)


def _dot_nt(a, b):
    return lax.dot_general(a, b, (((1,), (1,)), ((), ())), preferred_element_type=F32)


def _dot_tn(a, b):
    return lax.dot_general(a, b, (((0,), (0,)), ((), ())), preferred_element_type=F32)


def _sum0(x):
    return jnp.sum(x, axis=0, keepdims=True)


def _norm_mod_bwd(x, g, sc, dh):
    xh, r = _rms(x)
    hn = xh * g
    dhn = dh * (1.0 + sc)
    dxh = dhn * g
    dx = r * (dxh - xh * jnp.mean(dxh * xh, axis=-1, keepdims=True))
    return dx, _sum0(dhn * xh), _sum0(dh), _sum0(dh * hn)


def _dev_index(p):
    return 4 * p[0] + 2 * p[1] + p[2]


def _mesh_pos():
    return lax.axis_index("x"), lax.axis_index("y"), lax.axis_index("c")


class _AllGather:
    def __init__(self, arrs):
        n = len(arrs)
        self.arrays = list(arrs)
        self.out_shapes = [_sds((N_DEV,) + a.shape, a.dtype) for a in arrs]
        self.scratch = ([pltpu.VMEM(a.shape, a.dtype) for a in arrs]
                        + [pltpu.SemaphoreType.DMA((7 * n,)), pltpu.SemaphoreType.DMA((7 * n,)),
                           pltpu.SemaphoreType.DMA((n,))])
        self.aliases = {}

    def _parts(self, ins, outs, scr):
        n = len(self.arrays)
        stage = scr[:n]
        send_sems, recv_sems, local_sems = scr[n:]
        x, y, c = _mesh_pos()
        me, sib = (x, y, c), (x, y, 1 - c)
        chips = [(1 - x, y), (x, 1 - y), (1 - x, 1 - y)]

        def copy(t, k, block, to, own=False):
            dst = outs[t].at[_dev_index(block)]
            return pltpu.make_async_remote_copy(
                src_ref=ins[t] if own else dst, dst_ref=dst,
                send_sem=send_sems.at[7 * t + k], recv_sem=recv_sems.at[7 * t + k],
                device_id=to, device_id_type=MESH)

        first = []
        for t in range(n):
            first.append(copy(t, 0, me, sib, own=True))
            for j, ch in enumerate(chips):
                first.append(copy(t, 1 + j, me, (*ch, c), own=True))
        stage_in = [pltpu.make_async_copy(ins[t], stage[t], local_sems.at[t]) for t in range(n)]
        mine = [pltpu.make_async_copy(stage[t], outs[t].at[_dev_index(me)], local_sems.at[t]) for t in range(n)]
        return n, c, me, sib, chips, copy, first, stage_in, mine

    def start(self, ins, outs, scr):
        n, _, _, _, _, _, first, stage_in, mine = self._parts(ins, outs, scr)
        for cp in stage_in:
            cp.start()
        for cp in first:
            cp.start()
        for t in range(n):
            stage_in[t].wait()
            mine[t].start()

    def relay(self, ins, outs, scr):
        n, c, me, sib, chips, copy, _, _, _ = self._parts(ins, outs, scr)
        for j, ch in enumerate(chips):
            for t in range(n):
                copy(t, 1 + j, (*ch, c), me).wait_recv()
                copy(t, 4 + j, (*ch, c), sib).start()

    def finish(self, ins, outs, scr):
        n, c, me, sib, chips, copy, first, _, mine = self._parts(ins, outs, scr)
        passed = [copy(t, 4 + j, (*ch, c), sib) for j, ch in enumerate(chips) for t in range(n)]
        for t in range(n):
            copy(t, 0, sib, me).wait_recv()
            for j, ch in enumerate(chips):
                copy(t, 4 + j, (*ch, 1 - c), me).wait_recv()
        for cp in first + passed:
            cp.wait_send()
        for cp in mine:
            cp.wait()


class _Exchange:
    def __init__(self, arrays, out_shapes, plan, n_copies, aliases=None):
        self.arrays = list(arrays)
        self.out_shapes = list(out_shapes)
        self.plan = plan
        self.scratch = [pltpu.SemaphoreType.DMA((n_copies,)), pltpu.SemaphoreType.DMA((n_copies,))]
        self.aliases = aliases or {}

    def _copies(self, ins, outs, scr):
        send_sems, recv_sems = scr
        snd, rcv = [], []
        for i, (src, dst, peer, lands) in enumerate(self.plan(ins, outs, _mesh_pos())):
            kw = dict(send_sem=send_sems.at[i], recv_sem=recv_sems.at[i], device_id=peer, device_id_type=MESH)
            snd.append(pltpu.make_async_remote_copy(src_ref=src, dst_ref=dst, **kw))
            rcv.append(pltpu.make_async_remote_copy(src_ref=src, dst_ref=lands, **kw))
        return snd, rcv

    def start(self, ins, outs, scr):
        for cp in self._copies(ins, outs, scr)[0]:
            cp.start()

    def relay(self, ins, outs, scr):
        pass

    def finish(self, ins, outs, scr):
        snd, rcv = self._copies(ins, outs, scr)
        for cp in rcv:
            cp.wait_recv()
        for cp in snd:
            cp.wait_send()


def _pair_exchange(grads):
    n = len(grads)

    def plan(ins, outs, pos):
        x, y, c = pos
        return [(ins[t].at[2 * j + (1 - c)], outs[t].at[j], (x, y, 1 - c), outs[t].at[j])
                for t in range(n) for j in range(N_CHIP)]

    return _Exchange(grads, [_sds((N_CHIP,) + g.shape[1:], g.dtype) for g in grads], plan, N_CHIP * n)


def _chip_exchange(parts, rows=None):
    n = len(parts)

    def plan(ins, outs, pos):
        x, y, c = pos
        chips = [(1 - x, y), (x, 1 - y), (1 - x, 1 - y)]

        def src(t, ch):
            blk = ins[t].at[2 * ch[0] + ch[1]]
            return blk if rows is None else blk.at[pl.ds(rows[0], rows[1])]

        return [(src(t, ch), outs[t].at[k], (*ch, c), outs[t].at[k]) for t in range(n) for k, ch in enumerate(chips)]

    shapes = [_sds((3, p.shape[1] if rows is None else rows[1]) + p.shape[2:], p.dtype) for p in parts]
    return _Exchange(parts, shapes, plan, 3 * n)


def _pair_gather(bufs):
    n = len(bufs)

    def plan(ins, outs, pos):
        x, y, c = pos
        return [(ins[t].at[c], outs[t].at[c], (x, y, 1 - c), outs[t].at[1 - c]) for t in range(n)]

    return _Exchange(bufs, [_sds(b.shape, b.dtype) for b in bufs], plan, n, aliases={t: t for t in range(n)})


def _run_comms(comms, name):
    c_in = [len(cm.arrays) for cm in comms]
    c_out = [len(cm.out_shapes) for cm in comms]
    c_scr = [len(cm.scratch) for cm in comms]
    aliases = {}
    for k, cm in enumerate(comms):
        for a, b in cm.aliases.items():
            aliases[sum(c_in[:k]) + a] = sum(c_out[:k]) + b

    def split(refs, counts):
        out, pos = [], 0
        for cnt in counts:
            out.append(refs[pos:pos + cnt])
            pos += cnt
        return out

    def body(*refs):
        ins = split(refs[:sum(c_in)], c_in)
        outs = split(refs[sum(c_in):sum(c_in) + sum(c_out)], c_out)
        scr = split(refs[sum(c_in) + sum(c_out):], c_scr)
        for phase in ("start", "relay", "finish"):
            for k, cm in enumerate(comms):
                getattr(cm, phase)(ins[k], outs[k], scr[k])

    outs = _pc(body, name=name, out_shape=[s for cm in comms for s in cm.out_shapes],
               in_specs=[ANY] * sum(c_in), out_specs=[ANY] * sum(c_out), input_output_aliases=aliases,
               scratch_shapes=[s for cm in comms for s in cm.scratch],
               compiler_params=_params())(*[a for cm in comms for a in cm.arrays])
    return split(list(outs), c_out)


def _all_gather(arrs, name):
    return _run_comms([_AllGather(arrs)], name)[0]


def _call(body, *, name, grid, in_specs, out_specs, out_shape, scratch_shapes, sem, args, comms=()):
    n_in, n_out, n_scr = len(in_specs), len(out_specs), len(scratch_shapes)
    c_in = [len(cm.arrays) for cm in comms]
    c_out = [len(cm.out_shapes) for cm in comms]
    c_scr = [len(cm.scratch) for cm in comms]
    aliases = {}
    for k, cm in enumerate(comms):
        for a, b in cm.aliases.items():
            aliases[n_in + sum(c_in[:k]) + a] = n_out + sum(c_out[:k]) + b

    def split(refs, counts):
        out, pos = [], 0
        for cnt in counts:
            out.append(refs[pos:pos + cnt])
            pos += cnt
        return out

    def wrapped(*refs):
        ins = refs[:n_in + sum(c_in)]
        outs = refs[len(ins):len(ins) + n_out + sum(c_out)]
        scr = refs[len(ins) + len(outs):]
        cins, couts, cscr = split(ins[n_in:], c_in), split(outs[n_out:], c_out), split(scr[n_scr:], c_scr)
        if comms:
            first = pl.program_id(0) == 0
            last = pl.program_id(0) == grid[0] - 1
            for k in range(1, len(grid)):
                first = jnp.logical_and(first, pl.program_id(k) == 0)
                last = jnp.logical_and(last, pl.program_id(k) == grid[k] - 1)

            @pl.when(first)
            def _():
                for k, cm in enumerate(comms):
                    cm.start(cins[k], couts[k], cscr[k])
        body(*ins[:n_in], *outs[:n_out], *scr[:n_scr])
        if comms:
            relay_early = len(grid) == 1 and grid[0] >= 4
            if relay_early:
                @pl.when(pl.program_id(0) == (7 * grid[0]) // 8 - 1)
                def _():
                    for k, cm in enumerate(comms):
                        cm.relay(cins[k], couts[k], cscr[k])

            @pl.when(last)
            def _():
                for k, cm in enumerate(comms):
                    if not relay_early:
                        cm.relay(cins[k], couts[k], cscr[k])
                    cm.finish(cins[k], couts[k], cscr[k])

    outs = _pc(wrapped, name=name, grid=grid,
               in_specs=list(in_specs) + [ANY] * sum(c_in), out_specs=list(out_specs) + [ANY] * sum(c_out),
               out_shape=list(out_shape) + [s for cm in comms for s in cm.out_shapes],
               scratch_shapes=list(scratch_shapes) + [s for cm in comms for s in cm.scratch],
               input_output_aliases=aliases, compiler_params=_params(*sem),
               )(*args, *[a for cm in comms for a in cm.arrays])
    outs = list(outs)
    return outs[:n_out], split(outs[n_out:], c_out)


def _row_block(r):
    for b in (512, 256, 128, 64, 32, 16, 8):
        if r % b == 0:
            return b
    return r


def _pair_add(g, recv, cj_idx, name):
    _, r, cc = g.shape
    br = _row_block(r)

    def body(cj_ref, g_ref, r_ref, own_ref, pb_ref):
        s = g_ref[...] + r_ref[...]
        pb_ref[...] = s.astype(BF16)

        @pl.when(pl.program_id(1) == cj_ref[1])
        def _():
            own_ref[...] = s[0]

    grid_spec = pltpu.PrefetchScalarGridSpec(
        num_scalar_prefetch=1, grid=(r // br, N_CHIP),
        in_specs=[pl.BlockSpec((1, br, cc), lambda i, j, cj_ref: (2 * j + cj_ref[0], i, 0)),
                  pl.BlockSpec((1, br, cc), lambda i, j, cj_ref: (j, i, 0))],
        out_specs=[pl.BlockSpec((br, cc), lambda i, j, cj_ref: (i, 0)),
                   pl.BlockSpec((1, br, cc), lambda i, j, cj_ref: (j, i, 0))])
    return _pc(body, name=name, grid_spec=grid_spec,
               out_shape=[_sds((r, cc)), _sds((N_CHIP, r, cc), BF16)],
               compiler_params=_params("arbitrary", "arbitrary"))(cj_idx, g, recv)


def _chip_add(p, qs, cj_idx, name):
    r, cc = p.shape
    nq = len(qs)
    br = _row_block(r // nq)
    nb = r // nq // br

    def body(cj_ref, p_ref, *refs):
        o_ref = refs[-1]
        if nq == 2:
            top = pl.program_id(0) < nb
            q = [jnp.where(top, refs[0][k], refs[1][k]).astype(F32) for k in range(3)]
        else:
            q = [refs[0][k].astype(F32) for k in range(3)]
        o_ref[0] = ((p_ref[...] + q[0]) + q[1]) + q[2]

    q_specs = [pl.BlockSpec((3, br, cc), lambda i, cj_ref, h=h: (0, jnp.clip(i - h * nb, 0, nb - 1), 0))
               for h in range(nq)]
    grid_spec = pltpu.PrefetchScalarGridSpec(
        num_scalar_prefetch=1, grid=(r // br,),
        in_specs=[pl.BlockSpec((br, cc), lambda i, cj_ref: (i, 0))] + q_specs,
        out_specs=pl.BlockSpec((1, br, cc), lambda i, cj_ref: (cj_ref[0], i, 0)))
    return _pc(body, name=name, grid_spec=grid_spec, out_shape=_sds((2, r, cc)),
               compiler_params=_params("arbitrary"))(cj_idx, p, *qs)


def _shard_of(both):
    return both.reshape((2 * both.shape[1],) + both.shape[2:])


ADAMW_CHUNKS = 4


def _adamw(w, g, m, v, name):
    r, cc = w.shape
    rows = r // ADAMW_CHUNKS
    assert rows * ADAMW_CHUNKS == r and rows % SUBLANES == 0
    c1 = 1.0 - ADAM_B1 ** ADAM_STEP
    c2 = 1.0 - ADAM_B2 ** ADAM_STEP

    def body(w_hbm, g_hbm, m_hbm, v_hbm, go_hbm, d_hbm, mo_hbm, vo_hbm, wb, gb, mb, vb, sem_in, sem_out):
        srcs, bufs, dsts = (w_hbm, g_hbm, m_hbm, v_hbm), (wb, gb, mb, vb), (d_hbm, go_hbm, mo_hbm, vo_hbm)

        def load(a, k):
            sl = pl.ds(k * rows, rows)
            return pltpu.make_async_copy(srcs[a].at[sl], bufs[a].at[sl], sem_in.at[a, k])

        def store(a, k):
            sl = pl.ds(k * rows, rows)
            return pltpu.make_async_copy(bufs[a].at[sl], dsts[a].at[sl], sem_out.at[a, k])

        for k in range(ADAMW_CHUNKS):
            for a in range(4):
                load(a, k).start()
        for k in range(ADAMW_CHUNKS):
            for a in range(4):
                load(a, k).wait()
            store(1, k).start()
            sl = pl.ds(k * rows, rows)
            gg = gb[sl]
            mn = ADAM_B1 * mb[sl] + (1.0 - ADAM_B1) * gg
            vn = ADAM_B2 * vb[sl] + (1.0 - ADAM_B2) * (gg * gg)
            mh = mn / c1
            vh = vn / c2
            wb[sl] = -ADAM_LR * (mh / (jnp.sqrt(vh) + ADAM_EPS) + ADAM_WD * wb[sl])
            mb[sl] = mn
            vb[sl] = vn
            for a in (0, 2, 3):
                store(a, k).start()
        for k in range(ADAMW_CHUNKS):
            for a in range(4):
                store(a, k).wait()

    go, d, mo, vo = _pc(body, name=name, in_specs=[ANY] * 4, out_specs=[ANY] * 4, out_shape=[_sds((r, cc))] * 4,
                        scratch_shapes=[pltpu.VMEM((r, cc), F32)] * 4 + [pltpu.SemaphoreType.DMA((4, ADAMW_CHUNKS))] * 2,
                        compiler_params=_params())(w, g, m, v)
    return go, d, mo, vo


def _head(w_half, c8, small, w_ada, b_shard, c_ctx, ret_decay):
    ada_n = w_ada.shape[1]
    mod_sds = _sds((16, ada_n))
    ag_w, ag_c, ag_m = _AllGather([w_half]), _AllGather([c8, small]), _AllGather([mod_sds])
    n_w, n_c, n_m = len(ag_w.scratch), len(ag_c.scratch), len(ag_m.scratch)

    def body(w_ref, c_ref, s_ref, wada_ref, b_ref, cc_ref, rd_ref,
             gw_ref, call_ref, sall_ref, a_ref, modp_ref, mall_ref, lg_ref, sg_ref, *scr):
        scr_w, scr_c, scr_m = scr[:n_w], scr[n_w:n_w + n_c], scr[n_w + n_c:n_w + n_c + n_m]
        c_v, w_v, m_v, sems = scr[n_w + n_c + n_m:]
        ag_w.start((w_ref,), (gw_ref,), scr_w)
        ag_c.start((c_ref, s_ref), (call_ref, sall_ref), scr_c)
        load_w = pltpu.make_async_copy(wada_ref, w_v, sems.at[0])
        load_w.start()
        rd = rd_ref[...]
        lg_ref[...] = -_softplus(-rd)
        sg_ref[...] = _sigmoid(-rd)
        ag_c.relay((c_ref, s_ref), (call_ref, sall_ref), scr_c)
        ag_c.finish((c_ref, s_ref), (call_ref, sall_ref), scr_c)
        load_c = pltpu.make_async_copy(call_ref, c_v, sems.at[1])
        load_c.start()
        load_c.wait()
        a_ref[...] = jnp.zeros_like(a_ref)
        for d in range(N_DEV):
            cd = c_v[d, 0:1, :]
            a_ref[d:d + 1, :] = cd * _sigmoid(cd)
        cc = cc_ref[...]
        a_ref[N_DEV:N_DEV + 1, :] = cc * _sigmoid(cc)
        load_w.wait()
        m_v[...] = jnp.dot(a_ref[...], w_v[...], preferred_element_type=F32,
                           precision=lax.Precision.HIGHEST) + b_ref[...]
        put = pltpu.make_async_copy(m_v, modp_ref, sems.at[2])
        put.start()
        put.wait()
        ag_m.start((modp_ref,), (mall_ref,), scr_m)
        ag_m.relay((modp_ref,), (mall_ref,), scr_m)
        ag_m.finish((modp_ref,), (mall_ref,), scr_m)
        ag_w.relay((w_ref,), (gw_ref,), scr_w)
        ag_w.finish((w_ref,), (gw_ref,), scr_w)

    rd = jnp.broadcast_to(ret_decay.reshape(2, HEADS).T[:, :, None], (HEADS, 2, LANES))
    lane = _full((HEADS, 2, LANES))
    outs = _pc(
        body, name="head",
        in_specs=[ANY, ANY, ANY, ANY, _full((1, ada_n)), _full((1, D_MODEL)), lane],
        out_specs=[ANY, ANY, ANY, _full((16, D_MODEL)), ANY, ANY, lane, lane],
        out_shape=ag_w.out_shapes + ag_c.out_shapes + [_sds((16, D_MODEL)), mod_sds] + ag_m.out_shapes
        + [_sds((HEADS, 2, LANES))] * 2,
        scratch_shapes=ag_w.scratch + ag_c.scratch + ag_m.scratch
        + [pltpu.VMEM((N_DEV,) + c8.shape, F32), pltpu.VMEM(w_ada.shape, F32), pltpu.VMEM((16, ada_n), F32),
           pltpu.SemaphoreType.DMA((3,))],
        compiler_params=_params(),
    )(w_half, c8, small, w_ada, b_shard, c_ctx.reshape(1, D_MODEL), rd)
    gw, c_all, small_all, a16, _, mod_all, lgv, sgv = outs
    return gw, c_all, small_all, a16, mod_all, lgv, sgv


def _ada_grad(at, b):
    n = b.shape[1]
    bn = 512

    def body(a_ref, b_ref, o_ref):
        o_ref[...] = jnp.dot(a_ref[...], b_ref[...], preferred_element_type=F32, precision=lax.Precision.HIGHEST)

    return _pc(body, name="ada_grad", grid=(n // bn,),
               in_specs=[_full((D_MODEL, LANES)), pl.BlockSpec((LANES, bn), lambda i: (0, i))],
               out_specs=pl.BlockSpec((D_MODEL, bn), lambda i: (0, i)), out_shape=_sds((D_MODEL, n)),
               compiler_params=_params("arbitrary"))(at, b)


def _cctx_partial(dmc8, w_ada):
    n = w_ada.shape[1]
    bn = 512

    def body(d_ref, w_ref, o_ref):
        @pl.when(pl.program_id(0) == 0)
        def _():
            o_ref[...] = jnp.zeros_like(o_ref)
        o_ref[...] += lax.dot_general(d_ref[...], w_ref[...], (((1,), (1,)), ((), ())),
                                      preferred_element_type=F32, precision=lax.Precision.HIGHEST)

    return _pc(body, name="cctx_partial", grid=(n // bn,),
               in_specs=[pl.BlockSpec((8, bn), lambda i: (0, i)), pl.BlockSpec((D_MODEL, bn), lambda i: (0, i))],
               out_specs=_full((8, D_MODEL)), out_shape=_sds((8, D_MODEL)),
               compiler_params=_params("arbitrary"))(dmc8, w_ada)


def _cctx_final(parts, c_ctx, m, v):
    c1 = 1.0 - ADAM_B1 ** ADAM_STEP
    c2 = 1.0 - ADAM_B2 ** ADAM_STEP

    def body(p_ref, c_ref, m_ref, v_ref, g_ref, d_ref, mo_ref, vo_ref):
        s = ((p_ref[0, 0:1, :] + p_ref[2, 0:1, :]) + p_ref[4, 0:1, :]) + p_ref[6, 0:1, :]
        z = c_ref[...]
        sg = _sigmoid(z)
        gg = s * (sg * (1.0 + z * (1.0 - sg)))
        g_ref[...] = gg
        mn = ADAM_B1 * m_ref[...] + (1.0 - ADAM_B1) * gg
        vn = ADAM_B2 * v_ref[...] + (1.0 - ADAM_B2) * (gg * gg)
        d_ref[...] = -ADAM_LR * ((mn / c1) / (jnp.sqrt(vn / c2) + ADAM_EPS) + ADAM_WD * z)
        mo_ref[...] = mn
        vo_ref[...] = vn

    row = _full((1, D_MODEL))
    return _pc(body, name="cctx_final", out_shape=[_sds((1, D_MODEL))] * 4,
               in_specs=[_full(parts.shape), row, row, row], out_specs=[row] * 4,
               compiler_params=_params())(parts, c_ctx.reshape(1, D_MODEL), m.reshape(1, D_MODEL), v.reshape(1, D_MODEL))


def _rotary_tables(t_len):
    rows = t_len // GRID_W
    n_freq = DH // 4
    inv = ROPE_BASE ** (-jnp.arange(n_freq, dtype=F32) / n_freq)
    row_ang = jnp.arange(rows, dtype=F32)[:, None] * inv
    col_ang = jnp.arange(GRID_W, dtype=F32)[:, None] * inv

    def spread(fn):
        return jnp.concatenate([jnp.repeat(fn(row_ang), GRID_W, axis=0), jnp.tile(fn(col_ang), (rows, 1))], axis=-1)

    cos, sin = spread(jnp.cos), spread(jnp.sin)
    return jnp.concatenate([cos, cos], axis=-1), jnp.concatenate([-sin, sin], axis=-1)


def _inproj_fwd(x, gn, sh, sc, w4, cos2, sin2, name, comms=()):
    t = x.shape[0]
    tm = _tile(t, True)
    nc = IN_COLS // N_CHIP

    def body(x_ref, gn_ref, sh_ref, sc_ref, w_ref, c_ref, s_ref, p_ref, xr_ref, hb_ref, p_s):
        xh, _ = _rms(x_ref[...])
        h = xh * gn_ref[...] * (1.0 + sc_ref[...]) + sh_ref[...]
        hb = h.astype(BF16)
        hb_ref[...] = hb
        for j in range(N_CHIP):
            p_s[:, nc * j:nc * (j + 1)] = _dot(hb, w_ref[j])
        cc = c_ref[...]
        ss = s_ref[...]
        for hh in range(2 * HEADS):
            blk = p_s[:, DH * hh:DH * (hh + 1)]
            rot = blk * cc + pltpu.roll(blk, DH // 2, 1) * ss
            if hh >= HEADS:
                rot = rot * K_SCALE
            p_ref[:, DH * hh:DH * (hh + 1)] = rot.astype(BF16)
        p_ref[:, 2 * RET_W:] = p_s[:, 2 * RET_W:].astype(BF16)
        xr_ref[...] = p_s[:, 4 * RET_W:4 * RET_W + LRU_W]

    row = _full((1, D_MODEL))
    outs, couts = _call(
        body, name=name, grid=(t // tm,),
        in_specs=[pl.BlockSpec((tm, D_MODEL), lambda i: (i, 0)), row, row, row, _full(w4.shape),
                  pl.BlockSpec((tm, DH), lambda i: (i, 0)), pl.BlockSpec((tm, DH), lambda i: (i, 0))],
        out_specs=[pl.BlockSpec((tm, IN_COLS), lambda i: (i, 0)), pl.BlockSpec((tm, LRU_W), lambda i: (i, 0)),
                   pl.BlockSpec((tm, D_MODEL), lambda i: (i, 0))],
        out_shape=[_sds((t, IN_COLS), BF16), _sds((t, LRU_W)), _sds((t, D_MODEL), BF16)],
        scratch_shapes=[pltpu.VMEM((tm, IN_COLS), F32)], sem=("arbitrary",),
        args=(x, gn, sh, sc, w4, cos2, sin2), comms=comms)
    return (outs, couts) if comms else outs


def _inproj_bwd(x, gn, sh, sc, w4, cos2, sin2, pieces, dres, name):
    t = x.shape[0]
    tm = _tile(t)
    nc = IN_COLS // N_CHIP

    def body(x_ref, gn_ref, sh_ref, sc_ref, w_ref, c_ref, s_ref, dqf, dqb, dkf, dkb, dvf, dvb, dg, dxr, dgt, dres_ref,
             dx_ref, dpb_ref, dgn_ref, dsh_ref, dsc_ref):
        cc = c_ref[...]
        ss = s_ref[...]
        dq = dqf[...].astype(F32) + dqb[...].astype(F32)
        dk = dkf[...].astype(F32) + dkb[...].astype(F32)
        for hh in range(HEADS):
            sl = slice(DH * hh, DH * (hh + 1))
            b = dq[:, sl]
            dpb_ref[:, sl] = (b * cc + pltpu.roll(b * ss, DH // 2, 1)).astype(BF16)
            b = dk[:, sl]
            dpb_ref[:, RET_W + DH * hh:RET_W + DH * (hh + 1)] = (
                (b * cc + pltpu.roll(b * ss, DH // 2, 1)) * K_SCALE).astype(BF16)
        dpb_ref[:, 2 * RET_W:3 * RET_W] = (dvf[...].astype(F32) + dvb[...].astype(F32)).astype(BF16)
        dpb_ref[:, 3 * RET_W:4 * RET_W] = dg[...].astype(BF16)
        dpb_ref[:, 4 * RET_W:4 * RET_W + LRU_W] = dxr[...].astype(BF16)
        dpb_ref[:, 4 * RET_W + LRU_W:IN_COLS] = dgt[...].astype(BF16)
        dh = _dot_nt(dpb_ref[:, 0:nc], w_ref[0])
        for j in range(1, N_CHIP):
            dh = dh + _dot_nt(dpb_ref[:, nc * j:nc * (j + 1)], w_ref[j])
        dx, dgn_t, dsh_t, dsc_t = _norm_mod_bwd(x_ref[...], gn_ref[...], sc_ref[...], dh)
        dx_ref[...] = dres_ref[...] + dx

        @pl.when(pl.program_id(0) == 0)
        def _():
            dgn_ref[...] = jnp.zeros_like(dgn_ref)
            dsh_ref[...] = jnp.zeros_like(dsh_ref)
            dsc_ref[...] = jnp.zeros_like(dsc_ref)
        dgn_ref[...] += dgn_t
        dsh_ref[...] += dsh_t
        dsc_ref[...] += dsc_t

    row = _full((1, D_MODEL))
    pc = pl.BlockSpec((tm, RET_W), lambda i: (i, 0))
    big = pl.BlockSpec((tm, D_MODEL), lambda i: (i, 0))
    return _pc(body, name=name, grid=(t // tm,),
               in_specs=[big, row, row, row, _full(w4.shape),
                         pl.BlockSpec((tm, DH), lambda i: (i, 0)), pl.BlockSpec((tm, DH), lambda i: (i, 0))]
               + [pc] * 9 + [big],
               out_specs=[big, pl.BlockSpec((tm, IN_COLS), lambda i: (i, 0)), row, row, row],
               out_shape=[_sds((t, D_MODEL)), _sds((t, IN_COLS), BF16), _sds((1, D_MODEL)), _sds((1, D_MODEL)),
                          _sds((1, D_MODEL))],
               compiler_params=_params("arbitrary"))(x, gn, sh, sc, w4, cos2, sin2, *pieces, dres)


def _halo_specs(t, tm):
    n8 = tm // SUBLANES
    last8 = t // SUBLANES - 1
    prev = pl.BlockSpec((SUBLANES, LRU_W), lambda i: (jnp.maximum(i * n8 - 1, 0), 0))
    main = pl.BlockSpec((tm, LRU_W), lambda i: (i, 0))
    nxt = pl.BlockSpec((SUBLANES, LRU_W), lambda i: (jnp.minimum((i + 1) * n8, last8), 0))
    return prev, main, nxt


def _with_halo(prev_ref, main_ref, next_ref, i, nt):
    prev = jnp.where(i > 0, prev_ref[...], 0.0)
    nxt = jnp.where(i < nt - 1, next_ref[...], 0.0)
    return jnp.concatenate([prev, main_ref[...], nxt], axis=0)


def _conv_fwd(xr, cw, cb, name):
    t = xr.shape[0]
    tm = _tile(t, True)
    nt = t // tm
    n = tm + 2 * SUBLANES
    mid = slice(SUBLANES, SUBLANES + tm)

    def body(p_ref, m_ref, n_ref, w_ref, b_ref, o_ref):
        xp = _with_halo(p_ref, m_ref, n_ref, pl.program_id(0), nt)
        acc = b_ref[...] + pltpu.roll(xp, 1, 0)[mid] * w_ref[0:1, :]
        acc = acc + xp[mid] * w_ref[1:2, :]
        acc = acc + pltpu.roll(xp, n - 1, 0)[mid] * w_ref[2:3, :]
        acc = acc + pltpu.roll(xp, n - 2, 0)[mid] * w_ref[3:4, :]
        o_ref[...] = acc

    return _pc(body, name=name, grid=(nt,),
               in_specs=[*_halo_specs(t, tm), _full((4, LRU_W)), _full((1, LRU_W))],
               out_specs=pl.BlockSpec((tm, LRU_W), lambda i: (i, 0)), out_shape=_sds((t, LRU_W)),
               compiler_params=_params("arbitrary"))(xr, xr, xr, cw, cb)


def _conv_bwd(dxc_a, dxc_b, xr, cw, name):
    t = xr.shape[0]
    tm = _tile(t, True)
    nt = t // tm
    n = tm + 2 * SUBLANES
    mid = slice(SUBLANES, SUBLANES + tm)

    def body(ap_ref, am_ref, an_ref, bp_ref, bm_ref, bn_ref, xp_ref, xm_ref, xn_ref, w_ref, dx_ref, dw_ref, db_ref):
        i = pl.program_id(0)
        dp = _with_halo(ap_ref, am_ref, an_ref, i, nt) + _with_halo(bp_ref, bm_ref, bn_ref, i, nt)
        xp = _with_halo(xp_ref, xm_ref, xn_ref, i, nt)
        dx = pltpu.roll(dp, n - 1, 0)[mid] * w_ref[0:1, :]
        dx = dx + dp[mid] * w_ref[1:2, :]
        dx = dx + pltpu.roll(dp, 1, 0)[mid] * w_ref[2:3, :]
        dx = dx + pltpu.roll(dp, 2, 0)[mid] * w_ref[3:4, :]
        dx_ref[...] = dx.astype(BF16)
        d = dp[mid]

        @pl.when(i == 0)
        def _():
            dw_ref[...] = jnp.zeros_like(dw_ref)
            db_ref[...] = jnp.zeros_like(db_ref)
        dw_ref[0:1, :] += _sum0(d * pltpu.roll(xp, 1, 0)[mid])
        dw_ref[1:2, :] += _sum0(d * xp[mid])
        dw_ref[2:3, :] += _sum0(d * pltpu.roll(xp, n - 1, 0)[mid])
        dw_ref[3:4, :] += _sum0(d * pltpu.roll(xp, n - 2, 0)[mid])
        db_ref[...] += _sum0(d)

    return _pc(body, name=name, grid=(nt,),
               in_specs=[*_halo_specs(t, tm), *_halo_specs(t, tm), *_halo_specs(t, tm), _full((4, LRU_W))],
               out_specs=[pl.BlockSpec((tm, LRU_W), lambda i: (i, 0)), _full((4, LRU_W)), _full((1, LRU_W))],
               out_shape=[_sds((t, LRU_W), BF16), _sds((4, LRU_W)), _sds((1, LRU_W))],
               compiler_params=_params("arbitrary"))(dxc_a, dxc_a, dxc_a, dxc_b, dxc_b, dxc_b, xr, xr, xr, cw)


def _scan_scratch(n, c):
    return pltpu.VMEM((c // LANES, n, LANES), F32)


def _to_lane_blocks(ref, val):
    for lb in range(ref.shape[0]):
        ref[lb] = val[:, lb * LANES:(lb + 1) * LANES]


def _group_scan(a_s, b_s, reverse):
    nb, n, _ = a_s.shape
    ng = n // SUBLANES
    order = range(SUBLANES - 1, -1, -1) if reverse else range(SUBLANES)
    for lb in range(nb):
        prev = None
        for r in order:
            rows = pl.ds(r, ng, stride=SUBLANES)
            a_r, b_r = a_s[lb, rows, :], b_s[lb, rows, :]
            if prev is not None:
                b_r = a_r * prev[1] + b_r
                a_r = a_r * prev[0]
                a_s[lb, rows, :] = a_r
                b_s[lb, rows, :] = b_r
            prev = (a_r, b_r)


def _carry_scan(a_s, b_s, out_ref, carry, reverse):
    nb, n, _ = a_s.shape
    ng = n // SUBLANES

    def step(g, crs):
        gg = (ng - 1 - g) if reverse else g
        off = pl.multiple_of(gg * SUBLANES, SUBLANES)
        new = []
        for lb in range(nb):
            h = a_s[lb, pl.ds(off, SUBLANES), :] * crs[lb] + b_s[lb, pl.ds(off, SUBLANES), :]
            out_ref[pl.ds(off, SUBLANES), pl.ds(lb * LANES, LANES)] = h
            edge = h[0:1, :] if reverse else h[SUBLANES - 1:SUBLANES, :]
            new.append(jnp.broadcast_to(edge, (SUBLANES, LANES)))
        return tuple(new)

    crs = lax.fori_loop(0, ng, step, tuple(carry[:, lb * LANES:(lb + 1) * LANES] for lb in range(nb)))
    return jnp.concatenate(crs, axis=1)


def _lru_gates(xc, wa_ref, wx_ref, ba, bx, lam):
    xb = xc.astype(BF16)
    r = _sigmoid(_dot(xb, wa_ref[...]) + ba)
    ig = _sigmoid(_dot(xb, wx_ref[...]) + bx)
    sp = _softplus(-lam)
    la = -LRU_C * r * sp
    a = jnp.exp(la)
    mult = jnp.sqrt(_one_minus_sq(la, a))
    return r, ig, sp, a, mult


def _lru_fwd(xc, par_f, par_b, h0_f, h0_b, name, comms=()):
    t = xc.shape[0]
    tm = _tile(t, True)
    nt = t // tm

    def one(x_ref, prm, h0_ref, h_ref, a_s, b_s, c_s, reverse):
        wa_ref, wx_ref, ba_ref, bx_ref, lam_ref = prm

        @pl.when(pl.program_id(0) == 0)
        def _():
            c_s[...] = jnp.broadcast_to(h0_ref[...], c_s.shape)
        xv = x_ref[...]
        _, ig, _, a, mult = _lru_gates(xv, wa_ref, wx_ref, ba_ref[...], bx_ref[...], lam_ref[...])
        _to_lane_blocks(a_s, a)
        _to_lane_blocks(b_s, mult * (ig * xv))
        _group_scan(a_s, b_s, reverse)
        c_s[...] = _carry_scan(a_s, b_s, h_ref, c_s[...], reverse)

    def body(xf_ref, xb_ref, *refs):
        prm_f, prm_b = refs[0:5], refs[5:10]
        h0f_ref, h0b_ref, hf_ref, hb_ref = refs[10:14]
        af_s, bf_s, cf_s, ab_s, bb_s, cb_s = refs[14:]
        one(xf_ref, prm_f, h0f_ref, hf_ref, af_s, bf_s, cf_s, False)
        one(xb_ref, prm_b, h0b_ref, hb_ref, ab_s, bb_s, cb_s, True)

    vec = _full((1, LRU_W))
    mat = _full((LRU_W, LRU_W))
    fw = pl.BlockSpec((tm, LRU_W), lambda i: (i, 0))
    bw = pl.BlockSpec((tm, LRU_W), lambda i: (nt - 1 - i, 0))
    tile_s = [_scan_scratch(tm, LRU_W), _scan_scratch(tm, LRU_W), pltpu.VMEM((SUBLANES, LRU_W), F32)]
    (hf, hb), couts = _call(
        body, name=name, grid=(nt,),
        in_specs=[fw, bw] + [mat, mat, vec, vec, vec] * 2 + [vec, vec],
        out_specs=[pl.BlockSpec((tm, LRU_W), lambda i: (i, 0)), pl.BlockSpec((tm, LRU_W), lambda i: (nt - 1 - i, 0))],
        out_shape=[_sds((t, LRU_W))] * 2, scratch_shapes=tile_s + tile_s, sem=("arbitrary",),
        args=(xc, xc, *par_f, *par_b, h0_f, h0_b), comms=comms)
    return ((hf, hb), couts) if comms else (hf, hb)


def _lru_bwd(xc, par_f, par_b, h_f, h_b, h0_f, h0_b, dh_f, dh_b, name, comms=()):
    t = xc.shape[0]
    tm = _tile(t, True)
    nt = t // tm
    n8 = tm // SUBLANES
    last8 = t // SUBLANES - 1
    tile_f = lambda w: pl.BlockSpec((tm, w), lambda i: (nt - 1 - i, 0))
    tile_b = lambda w: pl.BlockSpec((tm, w), lambda i: (i, 0))
    halo_f = pl.BlockSpec((SUBLANES, LRU_W), lambda i: (jnp.maximum((nt - 1 - i) * n8 - 1, 0), 0))
    halo_b = pl.BlockSpec((SUBLANES, LRU_W), lambda i: (jnp.minimum((i + 1) * n8, last8), 0))

    def one(refs_in, refs_out, refs_scr, reverse):
        x_ref, wa_ref, wx_ref, ba_ref, bx_ref, lam_ref, h_ref, halo_ref, h0_ref, dh_ref = refs_in
        dx_ref, dpre_ref, dba_ref, dbx_ref, dlam_ref, dh0_ref = refs_out
        a_s, b_s, l_s, c_s, e_s = refs_scr
        i = pl.program_id(0)

        @pl.when(i == 0)
        def _():
            c_s[...] = jnp.zeros_like(c_s)
            e_s[...] = jnp.zeros_like(e_s)
            dba_ref[...] = jnp.zeros_like(dba_ref)
            dbx_ref[...] = jnp.zeros_like(dbx_ref)
            dlam_ref[...] = jnp.zeros_like(dlam_ref)
        xv = x_ref[...]
        lam = lam_ref[...]
        r, ig, sp, a, mult = _lru_gates(xv, wa_ref, wx_ref, ba_ref[...], bx_ref[...], lam)
        hv = h_ref[...]
        rowi = lax.broadcasted_iota(jnp.int32, (tm, LRU_W), 0)
        edge_a = jnp.broadcast_to(e_s[0:1, :], (tm, LRU_W))
        h0b = jnp.broadcast_to(h0_ref[...], (tm, LRU_W))
        if reverse:
            a_sh = jnp.where(rowi == 0, edge_a, pltpu.roll(a, 1, 0))
            hin_edge = jnp.where(i == nt - 1, h0b, jnp.broadcast_to(halo_ref[0:1, :], (tm, LRU_W)))
            h_in = jnp.where(rowi == tm - 1, hin_edge, pltpu.roll(hv, tm - 1, 0))
        else:
            a_sh = jnp.where(rowi == tm - 1, edge_a, pltpu.roll(a, tm - 1, 0))
            hin_edge = jnp.where(i == nt - 1, h0b, jnp.broadcast_to(halo_ref[SUBLANES - 1:SUBLANES, :], (tm, LRU_W)))
            h_in = jnp.where(rowi == 0, hin_edge, pltpu.roll(hv, 1, 0))
        _to_lane_blocks(a_s, a_sh)
        _to_lane_blocks(b_s, dh_ref[...])
        _group_scan(a_s, b_s, not reverse)
        c_s[...] = _carry_scan(a_s, b_s, l_s, c_s[...], not reverse)
        e_s[...] = jnp.broadcast_to(a[tm - 1:tm, :] if reverse else a[0:1, :], e_s.shape)
        lmb = l_s[...]
        da = lmb * h_in
        ixc = ig * xv
        dmult = lmb * ixc
        dixc = lmb * mult
        dla = da * a - dmult * (a * a) / mult
        dpr = dla * (-LRU_C * sp) * r * (1.0 - r)
        dpi = dixc * xv * ig * (1.0 - ig)
        dprb = dpr.astype(BF16)
        dpib = dpi.astype(BF16)
        dpre_ref[:, 0:LRU_W] = dprb
        dpre_ref[:, LRU_W:2 * LRU_W] = dpib
        dx_ref[...] = dixc * ig + _dot_nt(dprb, wa_ref[...]) + _dot_nt(dpib, wx_ref[...])
        dba_ref[...] += _sum0(dpr)
        dbx_ref[...] += _sum0(dpi)
        dlam_ref[...] += _sum0(dla * (-LRU_C * r)) * (-_sigmoid(-lam))

        @pl.when(i == nt - 1)
        def _():
            al0 = a * lmb
            dh0_ref[...] = al0[tm - 1:tm, :] if reverse else al0[0:1, :]

    def body(*refs):
        one(refs[0:10], refs[20:26], refs[32:37], False)
        one(refs[10:20], refs[26:32], refs[37:42], True)

    vec = _full((1, LRU_W))
    mat = _full((LRU_W, LRU_W))

    def in_specs(tile, halo):
        return [tile(LRU_W), mat, mat, vec, vec, vec, tile(LRU_W), halo, vec, tile(LRU_W)]

    def out_specs(tile):
        return [tile(LRU_W), tile(2 * LRU_W), vec, vec, vec, vec]

    out_one = [_sds((t, LRU_W)), _sds((t, 2 * LRU_W), BF16)] + [_sds((1, LRU_W))] * 4
    scr_one = ([_scan_scratch(tm, LRU_W)] * 2 + [pltpu.VMEM((tm, LRU_W), F32)]
               + [pltpu.VMEM((SUBLANES, LRU_W), F32)] * 2)
    outs, couts = _call(
        body, name=name, grid=(nt,), in_specs=in_specs(tile_f, halo_f) + in_specs(tile_b, halo_b),
        out_specs=out_specs(tile_f) + out_specs(tile_b), out_shape=out_one + out_one,
        scratch_shapes=scr_one + scr_one, sem=("arbitrary",),
        args=(xc, *par_f, h_f, h_f, h0_f, dh_f, xc, *par_b, h_b, h_b, h0_b, dh_b), comms=comms)
    return (tuple(outs[0:6]), tuple(outs[6:12])), couts


def _decay_tables(lg, reverse):
    ci = lax.broadcasted_iota(jnp.int32, (CHUNK, CHUNK), 0).astype(F32)
    mi = lax.broadcasted_iota(jnp.int32, (CHUNK, CHUNK), 1).astype(F32)
    rel = (mi - ci) if reverse else (ci - mi)
    relc = jnp.maximum(rel, 0.0)
    lg_c = jnp.concatenate([lg] * (CHUNK // LANES), axis=1)
    dm = jnp.where(rel >= 0, jnp.exp(lg_c * relc), 0.0)
    cd = lax.broadcasted_iota(jnp.int32, (CHUNK, DH), 0).astype(F32)
    pq, ps = (CHUNK - cd, cd) if reverse else (cd + 1.0, CHUNK - 1.0 - cd)
    return relc, dm, jnp.exp(lg * pq), jnp.exp(lg * ps), jnp.exp(lg * float(CHUNK)), pq, ps


def _ret_fwd(proj, lgv, s0f, s0b, comms=()):
    t = proj.shape[0]
    n = t // CHUNK

    def one(q, k, v, lg, s_s, hh, o_ref, sp_ref, reverse):
        _, dm, wq, ws, g, _, _ = _decay_tables(lg, reverse)
        vb = v.astype(BF16)
        p = _dot_nt(q.astype(BF16), k.astype(BF16)) * dm
        s = s_s[hh]
        sp_ref[hh, 0] = s
        o_ref[:, DH * hh:DH * (hh + 1)] = _dot(p.astype(BF16), vb) + _dot((q * wq).astype(BF16), s.astype(BF16))
        s_s[hh] = g * s + _dot_tn((k * ws).astype(BF16), vb)

    def body(qf, kf, vf, qb, kb, vb, lg_ref, s0f_ref, s0b_ref, of_ref, ob_ref, spf_ref, spb_ref, sf_s, sb_s):
        @pl.when(pl.program_id(0) == 0)
        def _():
            sf_s[...] = s0f_ref[...]
            sb_s[...] = s0b_ref[...]
        for hh in range(HEADS):
            sl = slice(DH * hh, DH * (hh + 1))
            one(qf[:, sl].astype(F32), kf[:, sl].astype(F32), vf[:, sl], lg_ref[hh, 0:1, :], sf_s, hh, of_ref, spf_ref,
                False)
            one(qb[:, sl].astype(F32), kb[:, sl].astype(F32), vb[:, sl], lg_ref[hh, 1:2, :], sb_s, hh, ob_ref, spb_ref,
                True)

    blk = (CHUNK, RET_W)
    fw = [pl.BlockSpec(blk, lambda i, o=o: (i, o)) for o in range(3)]
    bw = [pl.BlockSpec(blk, lambda i, o=o: (n - 1 - i, o)) for o in range(3)]
    st = _full((HEADS, DH, DH))
    return _call(body, name="ret_fwd", grid=(n,),
                 in_specs=fw + bw + [_full((HEADS, 2, LANES)), st, st],
                 out_specs=[pl.BlockSpec(blk, lambda i: (i, 0)), pl.BlockSpec(blk, lambda i: (n - 1 - i, 0)),
                            pl.BlockSpec((HEADS, 1, DH, DH), lambda i: (0, i, 0, 0)),
                            pl.BlockSpec((HEADS, 1, DH, DH), lambda i: (0, n - 1 - i, 0, 0))],
                 out_shape=[_sds((t, RET_W)), _sds((t, RET_W)), _sds((HEADS, n, DH, DH)), _sds((HEADS, n, DH, DH))],
                 scratch_shapes=[pltpu.VMEM((HEADS, DH, DH), F32), pltpu.VMEM((HEADS, DH, DH), F32)],
                 sem=("arbitrary",), args=(proj, proj, proj, proj, proj, proj, lgv, s0f, s0b), comms=comms)


def _ret_bwd(proj, lgv, sgv, spf, spb, do, comms=()):
    t = proj.shape[0]
    n = t // CHUNK

    def one(q_ref, k_ref, v_ref, lg_ref, s_ref, do_ref, dq_ref, dk_ref, dv_ref, ds_s, acc_s, reverse):
        d = 1 if reverse else 0
        for hh in range(HEADS):
            sl = slice(DH * hh, DH * (hh + 1))
            relc, dm, wq, ws, g, pq, ps = _decay_tables(lg_ref[hh, d:d + 1, :], reverse)
            qb, kb, vb = q_ref[:, sl], k_ref[:, sl], v_ref[:, sl]
            q, k = qb.astype(F32), kb.astype(F32)
            p = _dot_nt(qb, kb) * dm
            s = s_ref[hh, 0]
            dob = do_ref[:, sl].astype(BF16)
            dsn = ds_s[hh]
            dsb = dsn.astype(BF16)
            dv_ref[:, sl] = (_dot_tn(p.astype(BF16), dob) + _dot((k * ws).astype(BF16), dsb)).astype(BF16)
            dp = _dot_nt(dob, vb)
            dab = (dp * dm).astype(BF16)
            xq = _dot_nt(dob, s.astype(BF16))
            yk = _dot_nt(vb, dsb)
            dq_ref[:, sl] = (_dot(dab, kb) + xq * wq).astype(BF16)
            dk_ref[:, sl] = (_dot_tn(dab, qb) + yk * ws).astype(BF16)
            ds_s[hh] = g * dsn + _dot_tn((q * wq).astype(BF16), dob)
            s_mask = _sum0(dp * p * relc)
            part = (sum(s_mask[:, LANES * u:LANES * (u + 1)] for u in range(CHUNK // LANES))
                    + _sum0(xq * q * wq * pq) + _sum0(yk * k * ws * ps) + _sum0(dsn * s) * g * float(CHUNK))
            acc_s[hh] += jnp.broadcast_to(part, (SUBLANES, LANES))

    def body(qf, kf, vf, qb, kb, vb, lg_ref, sg_ref, sf_ref, sb_ref, dof_ref, dob_ref,
             dqf, dkf, dvf, dqb, dkb, dvb, ds0f_ref, ds0b_ref, drdf_ref, drdb_ref, dsf_s, dsb_s, accf_s, accb_s):
        i = pl.program_id(0)

        @pl.when(i == 0)
        def _():
            for r in (dsf_s, dsb_s, accf_s, accb_s):
                r[...] = jnp.zeros_like(r)
        one(qf, kf, vf, lg_ref, sf_ref, dof_ref, dqf, dkf, dvf, dsf_s, accf_s, False)
        one(qb, kb, vb, lg_ref, sb_ref, dob_ref, dqb, dkb, dvb, dsb_s, accb_s, True)

        @pl.when(i == n - 1)
        def _():
            ds0f_ref[...] = dsf_s[...]
            ds0b_ref[...] = dsb_s[...]
            for d, (acc_s, drd_ref) in enumerate(((accf_s, drdf_ref), (accb_s, drdb_ref))):
                for hh in range(HEADS):
                    tot = jnp.sum(acc_s[hh, 0:1, :], axis=1, keepdims=True)
                    drd_ref[hh] = jnp.broadcast_to(tot, (SUBLANES, LANES)) * sg_ref[hh, d:d + 1, :]

    blk = (CHUNK, RET_W)
    fw = lambda o: pl.BlockSpec(blk, lambda i, o=o: (n - 1 - i, o))
    bw = lambda o: pl.BlockSpec(blk, lambda i, o=o: (i, o))
    lane = _full((HEADS, 2, LANES))
    st = _full((HEADS, DH, DH))
    rd = _full((HEADS, SUBLANES, LANES))
    outs, couts = _call(
        body, name="ret_bwd", grid=(n,),
        in_specs=[fw(0), fw(1), fw(2), bw(0), bw(1), bw(2), lane, lane,
                  pl.BlockSpec((HEADS, 1, DH, DH), lambda i: (0, n - 1 - i, 0, 0)),
                  pl.BlockSpec((HEADS, 1, DH, DH), lambda i: (0, i, 0, 0)), fw(0), bw(0)],
        out_specs=[fw(0), fw(0), fw(0), bw(0), bw(0), bw(0), st, st, rd, rd],
        out_shape=[_sds((t, RET_W), BF16)] * 6 + [_sds((HEADS, DH, DH))] * 2 + [_sds((HEADS, SUBLANES, LANES))] * 2,
        scratch_shapes=[pltpu.VMEM((HEADS, DH, DH), F32)] * 2 + [pltpu.VMEM((HEADS, SUBLANES, LANES), F32)] * 2,
        sem=("arbitrary",), args=(proj, proj, proj, proj, proj, proj, lgv, sgv, spf, spb, do, do), comms=comms)
    dqf, dkf, dvf, dqb, dkb, dvb, ds0f, ds0b, drdf, drdb = outs
    return ((dqf, dkf, dvf, ds0f, drdf), (dqb, dkb, dvb, ds0b, drdb)), couts


def _ctx_weights(lg, l_len, reverse):
    pos = lax.broadcasted_iota(jnp.int32, (l_len, DH), 0).astype(F32)
    steps = pos if reverse else (l_len - 1.0 - pos)
    return jnp.exp(lg * steps), steps


def _ctx_state_fwd(projc, lgv):
    l_len = projc.shape[0]

    def body(k_ref, v_ref, lg_ref, sf_ref, sb_ref):
        k = k_ref[...]
        vb = v_ref[...].astype(BF16)
        for d, o_ref in ((0, sf_ref), (1, sb_ref)):
            w, _ = _ctx_weights(lg_ref[0, d:d + 1, :], l_len, d == 1)
            o_ref[0] = _dot_tn((k * w).astype(BF16), vb)

    st = pl.BlockSpec((1, DH, DH), lambda h: (h, 0, 0))
    return _pc(body, name="ctx_state_fwd", grid=(HEADS,),
               in_specs=[pl.BlockSpec((l_len, DH), lambda h: (0, HEADS + h)),
                         pl.BlockSpec((l_len, DH), lambda h: (0, 2 * HEADS + h)),
                         pl.BlockSpec((1, 2, LANES), lambda h: (h, 0, 0))],
               out_specs=[st, st], out_shape=[_sds((HEADS, DH, DH))] * 2,
               compiler_params=_params("arbitrary"))(projc, projc, lgv)


def _ctx_state_bwd(projc, lgv, sgv, dsf, dsb):
    l_len = projc.shape[0]

    def body(k_ref, v_ref, lg_ref, sg_ref, dsf_ref, dsb_ref, dk_ref, dv_ref, drd_ref):
        k = k_ref[...]
        vb = v_ref[...].astype(BF16)
        dk = jnp.zeros((l_len, DH), F32)
        dv = jnp.zeros((l_len, DH), F32)
        rows = []
        for d, ds_ref in ((0, dsf_ref), (1, dsb_ref)):
            w, steps = _ctx_weights(lg_ref[0, d:d + 1, :], l_len, d == 1)
            dsb16 = ds_ref[0].astype(BF16)
            dkw = _dot_nt(vb, dsb16)
            dk = dk + dkw * w
            dv = dv + _dot((k * w).astype(BF16), dsb16)
            tot = jnp.sum(_sum0(dkw * k * w * steps), axis=1, keepdims=True)
            rows.append(jnp.broadcast_to(tot, (1, LANES)) * sg_ref[0, d:d + 1, :])
        dk_ref[...] = dk.astype(BF16)
        dv_ref[...] = dv.astype(BF16)
        rid = lax.broadcasted_iota(jnp.int32, (SUBLANES, LANES), 0)
        drd_ref[0] = jnp.where(rid == 0, rows[0], jnp.where(rid == 1, rows[1], 0.0))

    st = pl.BlockSpec((1, DH, DH), lambda h: (h, 0, 0))
    lane = pl.BlockSpec((1, 2, LANES), lambda h: (h, 0, 0))
    hc = pl.BlockSpec((l_len, DH), lambda h: (0, h))
    return _pc(body, name="ctx_state_bwd", grid=(HEADS,),
               in_specs=[pl.BlockSpec((l_len, DH), lambda h: (0, HEADS + h)),
                         pl.BlockSpec((l_len, DH), lambda h: (0, 2 * HEADS + h)), lane, lane, st, st],
               out_specs=[hc, hc, pl.BlockSpec((1, SUBLANES, LANES), lambda h: (h, 0, 0))],
               out_shape=[_sds((l_len, RET_W), BF16), _sds((l_len, RET_W), BF16), _sds((HEADS, SUBLANES, LANES))],
               compiler_params=_params("arbitrary"))(projc, projc, lgv, sgv, dsf, dsb)


G_BLOCK = (3 * RET_W) // RET_W
GATE_BLOCK = (4 * RET_W + LRU_W) // LRU_W


def _head_norm(y):
    yc = y - jnp.mean(y, axis=-1, keepdims=True)
    rs = lax.rsqrt(jnp.mean(yc * yc, axis=-1, keepdims=True) + EPS)
    return yc * rs, rs


def _gelu_parts(z):
    th = jnp.tanh(GELU_K * (z + GELU_C * z * z * z))
    return 0.5 * z * (1.0 + th), th


def _mix_fwd(o_f, o_b, proj, hf, hb, w_out, x, g1, comms):
    t = x.shape[0]
    tm = _tile(t, True)

    def body(of_ref, ob_ref, g_ref, gt_ref, hf_ref, hb_ref, w_ref, x_ref, g1_ref, x1_ref, cat_ref):
        o = of_ref[...] + ob_ref[...]
        g = g_ref[...].astype(F32)
        for hh in range(HEADS):
            sl = slice(DH * hh, DH * (hh + 1))
            nrm, _ = _head_norm(o[:, sl])
            gh = g[:, sl]
            cat_ref[:, sl] = (gh * _sigmoid(gh) * nrm).astype(BF16)
        gel, _ = _gelu_parts(gt_ref[...].astype(F32))
        cat_ref[:, RET_W:] = ((hf_ref[...] + hb_ref[...]) * gel).astype(BF16)
        x1_ref[...] = x_ref[...] + g1_ref[...] * _dot(cat_ref[...], w_ref[...])

    half = pl.BlockSpec((tm, RET_W), lambda i: (i, 0))
    big = pl.BlockSpec((tm, D_MODEL), lambda i: (i, 0))
    return _call(body, name="mix_fwd", grid=(t // tm,),
                 in_specs=[half, half, pl.BlockSpec((tm, RET_W), lambda i: (i, G_BLOCK)),
                           pl.BlockSpec((tm, LRU_W), lambda i: (i, GATE_BLOCK)), half, half,
                           _full((D_MODEL, D_MODEL)), big, _full((1, D_MODEL))],
                 out_specs=[big, big], out_shape=[_sds((t, D_MODEL)), _sds((t, D_MODEL), BF16)],
                 scratch_shapes=[], sem=("arbitrary",), args=(o_f, o_b, proj, proj, hf, hb, w_out, x, g1),
                 comms=comms)


def _mix_bwd(o_f, o_b, proj, hf, hb, w_out, cat, dx1, g1, comms=()):
    t = dx1.shape[0]
    tm = _tile(t, True)

    def body(of_ref, ob_ref, g_ref, gt_ref, hf_ref, hb_ref, w_ref, cat_ref, dx1_ref, g1_ref,
             do_ref, dhs_ref, dg_ref, dgt_ref, dyb_ref, dg1_ref):
        dx1v = dx1_ref[...]
        y = _dot(cat_ref[...], w_ref[...])

        @pl.when(pl.program_id(0) == 0)
        def _():
            dg1_ref[...] = jnp.zeros_like(dg1_ref)
        dg1_ref[...] += _sum0(dx1v * y)
        dyb = (g1_ref[...] * dx1v).astype(BF16)
        dyb_ref[...] = dyb
        dcat = _dot_nt(dyb, w_ref[...])
        o = of_ref[...] + ob_ref[...]
        g = g_ref[...].astype(F32)
        for hh in range(HEADS):
            sl = slice(DH * hh, DH * (hh + 1))
            nrm, rs = _head_norm(o[:, sl])
            gh = g[:, sl]
            sg = _sigmoid(gh)
            dret = dcat[:, sl]
            dg_ref[:, sl] = (dret * nrm * (sg * (1.0 + gh * (1.0 - sg)))).astype(BF16)
            dn = dret * (gh * sg)
            dyc = rs * (dn - nrm * jnp.mean(dn * nrm, axis=-1, keepdims=True))
            do_ref[:, sl] = (dyc - jnp.mean(dyc, axis=-1, keepdims=True)).astype(BF16)
        z = gt_ref[...].astype(F32)
        gel, th = _gelu_parts(z)
        dlru = dcat[:, RET_W:]
        dhs_ref[...] = dlru * gel
        dgel = 0.5 * (1.0 + th) + 0.5 * z * (1.0 - th * th) * GELU_K * (1.0 + 3.0 * GELU_C * z * z)
        dgt_ref[...] = (dlru * (hf_ref[...] + hb_ref[...]) * dgel).astype(BF16)

    half = pl.BlockSpec((tm, RET_W), lambda i: (i, 0))
    big = pl.BlockSpec((tm, D_MODEL), lambda i: (i, 0))
    return _call(body, name="mix_bwd", grid=(t // tm,),
                 in_specs=[half, half, pl.BlockSpec((tm, RET_W), lambda i: (i, G_BLOCK)),
                           pl.BlockSpec((tm, LRU_W), lambda i: (i, GATE_BLOCK)), half, half,
                           _full((D_MODEL, D_MODEL)), big, big, _full((1, D_MODEL))],
                 out_specs=[half, half, half, half, big, _full((1, D_MODEL))],
                 out_shape=[_sds((t, RET_W), BF16), _sds((t, RET_W)), _sds((t, RET_W), BF16), _sds((t, RET_W), BF16),
                            _sds((t, D_MODEL), BF16), _sds((1, D_MODEL))],
                 scratch_shapes=[], sem=("arbitrary",), args=(o_f, o_b, proj, proj, hf, hb, w_out, cat, dx1, g1),
                 comms=comms)


def _mlp(x1, n2g, sh2, sc2, g2, fg, w1_parts, w2_parts, tgt):
    t = x1.shape[0]
    tm = _tile(t)
    hb_ = MLP_H // N_CHIP
    q_rows = hb_ // 4
    n_cp = 4 * N_DEV

    def body(x1_ref, n2g_ref, sh2_ref, sc2_ref, g2_ref, fg_ref, w1a, w1b, w2a, w2b, tgt_ref,
             dx1_ref, h2b_ref, ab_ref, dub_ref, dmb_ref, dsc_ref, dsh_ref, dg2_ref, dn2_ref, dfg_ref, loss_ref,
             w1_s, w2_s, r_s, sems):
        @pl.when(pl.program_id(0) == 0)
        def _():
            cps = []
            for p, parts in enumerate(((w1a, w2a), (w1b, w2b))):
                for d in range(N_DEV):
                    rows = pl.ds(2 * q_rows * (d % 2) + q_rows * p, q_rows)
                    for src, dst in zip(parts, (w1_s, w2_s)):
                        cps.append(pltpu.make_async_copy(src.at[d], dst.at[d // 2, rows], sems.at[len(cps)]))
            for cp in cps:
                cp.start()
            for r in (dsc_ref, dsh_ref, dg2_ref, dn2_ref, dfg_ref, loss_ref):
                r[...] = jnp.zeros_like(r)
            for cp in cps:
                cp.wait()
        x1v = x1_ref[...]
        n2g, sc2, g2, fg = n2g_ref[...], sc2_ref[...], g2_ref[...], fg_ref[...]
        xh, _ = _rms(x1v)
        h2b = (xh * n2g * (1.0 + sc2) + sh2_ref[...]).astype(BF16)
        h2b_ref[...] = h2b
        m = jnp.zeros((tm, D_MODEL), F32)
        for j in range(N_CHIP):
            sl = slice(hb_ * j, hb_ * (j + 1))
            r = jnp.maximum(_dot(h2b, w1_s[j]), 0.0)
            r_s[:, sl] = r
            ab = (r * r).astype(BF16)
            ab_ref[:, sl] = ab
            m = m + _dot(ab, w2_s[j])
        x2 = x1v + g2 * m
        x2h, r2 = _rms(x2)
        err = x2h * fg - tgt_ref[...]
        loss_ref[...] += _sum0(err * err)
        dout = err * (1.0 / D_MODEL)
        dfg_ref[...] += _sum0(dout * x2h)
        dxh = dout * fg
        dx2 = r2 * (dxh - x2h * jnp.mean(dxh * x2h, axis=-1, keepdims=True))
        dg2_ref[...] += _sum0(dx2 * m)
        dmb = (g2 * dx2).astype(BF16)
        dmb_ref[...] = dmb
        dh2 = jnp.zeros((tm, D_MODEL), F32)
        for j in range(N_CHIP):
            sl = slice(hb_ * j, hb_ * (j + 1))
            dub = (_dot_nt(dmb, w2_s[j]) * (2.0 * r_s[:, sl])).astype(BF16)
            dub_ref[:, sl] = dub
            dh2 = dh2 + _dot_nt(dub, w1_s[j])
        dx, dn2_t, dsh_t, dsc_t = _norm_mod_bwd(x1v, n2g, sc2, dh2)
        dx1_ref[...] = dx2 + dx
        dn2_ref[...] += dn2_t
        dsh_ref[...] += dsh_t
        dsc_ref[...] += dsc_t

        @pl.when(pl.program_id(0) == t // tm - 1)
        def _():
            tot = jnp.sum(loss_ref[...], axis=1, keepdims=True) * (0.5 / D_MODEL)
            loss_ref[...] = jnp.broadcast_to(tot, loss_ref.shape)

    row = _full((1, D_MODEL))
    big = pl.BlockSpec((tm, D_MODEL), lambda i: (i, 0))
    wide = pl.BlockSpec((tm, MLP_H), lambda i: (i, 0))
    return _pc(body, name="mlp", grid=(t // tm,),
               in_specs=[big, row, row, row, row, row, ANY, ANY, ANY, ANY, big],
               out_specs=[big, big, wide, wide, big, row, row, row, row, row, row],
               out_shape=[_sds((t, D_MODEL)), _sds((t, D_MODEL), BF16), _sds((t, MLP_H), BF16), _sds((t, MLP_H), BF16),
                          _sds((t, D_MODEL), BF16)] + [_sds((1, D_MODEL))] * 6,
               scratch_shapes=[pltpu.VMEM((N_CHIP, D_MODEL, hb_), BF16), pltpu.VMEM((N_CHIP, hb_, D_MODEL), BF16),
                               pltpu.VMEM((tm, MLP_H), F32), pltpu.SemaphoreType.DMA((n_cp,))],
               compiler_params=_params("arbitrary"))(x1, n2g, sh2, sc2, g2, fg, *w1_parts, *w2_parts, tgt)


def _tn(a, b, nj, a_blocked, b_blocked, name, extra=None, comms=()):
    t = a.shape[0]
    m = a.shape[1] // (nj if a_blocked else 1)
    n = b.shape[1] // (nj if b_blocked else 1)
    bk = next((b for b in (2048, 1024, 512) if t % b == 0), t)
    nk = t // bk
    a_col = (lambda j: j) if a_blocked else (lambda j: 0)
    b_col = (lambda j: j) if b_blocked else (lambda j: 0)
    in_specs = [pl.BlockSpec((bk, m), lambda j, k: (k, a_col(j))), pl.BlockSpec((bk, n), lambda j, k: (k, b_col(j)))]
    args = [a, b]
    if extra is not None:
        a2, b2 = extra
        t2 = a2.shape[0]
        in_specs += [pl.BlockSpec((t2, m), lambda j, k: (0, a_col(j))),
                     pl.BlockSpec((t2, n), lambda j, k: (0, b_col(j)))]
        args += [a2, b2]

    def body(*refs):
        a_ref, b_ref = refs[0], refs[1]
        o_ref, acc = refs[-2], refs[-1]
        k = pl.program_id(1)

        @pl.when(k == 0)
        def _():
            acc[...] = jnp.zeros_like(acc)
        acc[...] += _dot_tn(a_ref[...].astype(BF16), b_ref[...].astype(BF16))

        @pl.when(k == nk - 1)
        def _():
            if extra is not None:
                acc[...] += _dot_tn(refs[2][...].astype(BF16), refs[3][...].astype(BF16))
            o_ref[0] = acc[...]

    (out,), couts = _call(body, name=name, grid=(nj, nk), in_specs=in_specs,
                          out_specs=[pl.BlockSpec((1, m, n), lambda j, k: (j, 0, 0))], out_shape=[_sds((nj, m, n))],
                          scratch_shapes=[pltpu.VMEM((m, n), F32)], sem=("arbitrary", "arbitrary"), args=args,
                          comms=comms)
    return (out, couts) if comms else out


ROW_LOSS = 0
ROW_DMOD = 1
ROW_DMODC = 7
ROW_N1, ROW_N2, ROW_FG, ROW_CB = 9, 10, 11, 12
ROW_BA, ROW_BX, ROW_LAM = 13, 15, 17
ROW_CW = 20
ROW_RD = 24
SLAB_ROWS = 32
SEG = D_MODEL // 2


def _pack_small(rows, drd, cw2, cb2, lru2, gates):
    n_rows, n_lru = len(rows), len(lru2)

    def body(*refs):
        r = refs[:n_rows]
        drd_f, drd_b, drd_c, cw_a, cw_b, cb_a, cb_b = refs[n_rows:n_rows + 7]
        lru = refs[n_rows + 7:n_rows + 7 + n_lru]
        gf_ref, gb_ref, slab, ga, gx = refs[n_rows + 7 + n_lru:]
        slab[...] = jnp.zeros_like(slab)
        slab[ROW_LOSS:ROW_LOSS + 1, :] = r[0][...]
        for k in range(N_MOD):
            slab[ROW_DMOD + k:ROW_DMOD + k + 1, :] = r[1 + k][...]
        slab[ROW_DMODC:ROW_DMODC + 1, :] = r[7][...]
        slab[ROW_DMODC + 1:ROW_DMODC + 2, :] = r[8][...]
        slab[ROW_N1:ROW_N1 + 1, :] = r[9][...] + r[10][...]
        slab[ROW_N2:ROW_N2 + 1, :] = r[11][...]
        slab[ROW_FG:ROW_FG + 1, :] = r[12][...]
        slab[ROW_CB:ROW_CB + 1, 0:LRU_W] = cb_a[...] + cb_b[...]
        for k, row in enumerate((ROW_BA, ROW_BA + 1, ROW_BX, ROW_BX + 1, ROW_LAM, ROW_LAM + 1)):
            slab[row:row + 1, 0:LRU_W] = lru[2 * k][...] + lru[2 * k + 1][...]
        slab[ROW_CW:ROW_CW + 4, 0:LRU_W] = cw_a[...] + cw_b[...]
        for h in range(HEADS):
            slab[ROW_RD + h:ROW_RD + h + 1, 0:LANES] = drd_f[h, 0:1, :] + drd_c[h, 0:1, :]
            slab[ROW_RD + HEADS + h:ROW_RD + HEADS + h + 1, 0:LANES] = drd_b[h, 0:1, :] + drd_c[h, 1:2, :]
        for d, g_ref in enumerate((gf_ref, gb_ref)):
            for n in range(LRU_BLOCKS):
                blk = slice(LRU_BD * n, LRU_BD * (n + 1))
                ga[blk, LRU_BD * d:LRU_BD * (d + 1)] = g_ref[0, blk, blk].astype(BF16)
                gx[blk, LRU_BD * d:LRU_BD * (d + 1)] = g_ref[1, blk, blk].astype(BF16)

    args = list(rows) + list(drd) + list(cw2) + list(cb2) + list(lru2) + list(gates)
    gate_shape = (LRU_W, 2 * LRU_BD)
    return _pc(body, name="pack_small", in_specs=[_full(a.shape) for a in args],
               out_specs=[_full((SLAB_ROWS, D_MODEL)), _full(gate_shape), _full(gate_shape)],
               out_shape=[_sds((SLAB_ROWS, D_MODEL)), _sds(gate_shape, BF16), _sds(gate_shape, BF16)],
               compiler_params=_params())(*args)


def _adam_math(w, g, m, v):
    mn = ADAM_B1 * m + (1.0 - ADAM_B1) * g
    vn = ADAM_B2 * v + (1.0 - ADAM_B2) * (g * g)
    mh = mn / (1.0 - ADAM_B1 ** ADAM_STEP)
    vh = vn / (1.0 - ADAM_B2 ** ADAM_STEP)
    return -ADAM_LR * (mh / (jnp.sqrt(vh) + ADAM_EPS) + ADAM_WD * w), mn, vn


SMALL_PARAMS = ("b_ada", "norm1_g", "norm2_g", "final_g", "ret_decay", "conv_w", "conv_b", "lru_wa", "lru_ba", "lru_wx",
                "lru_bx", "lru_lambda")


def _finalize_small(chip_idx, slab_all, ga_all, gx_all, wmv):
    n_p = len(SMALL_PARAMS)
    flat = [a for nm in SMALL_PARAMS for a in wmv[nm]]
    ada_n = N_MOD * D_MODEL // N_CHIP

    def body(c_ref, slab_ref, ga_ref, gx_ref, *refs):
        prm = {nm: refs[3 * k:3 * k + 3] for k, nm in enumerate(SMALL_PARAMS)}
        outs = {nm: refs[3 * n_p + 4 * k:3 * n_p + 4 * k + 4] for k, nm in enumerate(SMALL_PARAMS)}
        b128_ref, dmc_ref, loss_ref = refs[3 * n_p + 4 * n_p:]
        chip = c_ref[0]

        def pick(fn):
            acc = fn(0)
            for j in range(1, N_CHIP):
                acc = jnp.where(chip == j, fn(j), acc)
            return acc

        tot = slab_ref[0]
        for d in range(1, N_DEV):
            tot = tot + slab_ref[d]

        def update(nm, g, sl=None, rows=None):
            w_ref, m_ref, v_ref = prm[nm]
            g_ref, d_ref, mo_ref, vo_ref = outs[nm]
            ix = (slice(None) if rows is None else rows, slice(None) if sl is None else sl)
            dl, mn, vn = _adam_math(w_ref[ix], g, m_ref[ix], v_ref[ix])
            g_ref[ix] = g
            d_ref[ix] = dl
            mo_ref[ix] = mn
            vo_ref[ix] = vn

        loss_ref[...] = jnp.broadcast_to(tot[ROW_LOSS:ROW_LOSS + 1, 0:LANES], (SUBLANES, LANES))
        for k in range(N_MOD):
            g = tot[ROW_DMOD + k:ROW_DMOD + k + 1, :]
            if k < 2:
                g = g + tot[ROW_DMODC + k:ROW_DMODC + k + 1, :]
            update("b_ada", g, slice(D_MODEL * k, D_MODEL * (k + 1)))
        update("norm1_g", tot[ROW_N1:ROW_N1 + 1, :])
        update("norm2_g", tot[ROW_N2:ROW_N2 + 1, :])
        update("final_g", tot[ROW_FG:ROW_FG + 1, :])
        update("ret_decay", tot[ROW_RD:ROW_RD + SUBLANES, 0:LANES])
        update("conv_b", tot[ROW_CB:ROW_CB + 1, 0:LRU_W])
        update("conv_w", pick(lambda j: tot[ROW_CW:ROW_CW + 4, LANES * j:LANES * (j + 1)]))
        for nm, row in (("lru_ba", ROW_BA), ("lru_bx", ROW_BX), ("lru_lambda", ROW_LAM)):
            update(nm, pick(lambda j, row=row: tot[row:row + 2, LANES * j:LANES * (j + 1)]))
        for nm, g_all in (("lru_wa", ga_ref), ("lru_wx", gx_ref)):
            for dr in range(2):
                lanes = slice(LRU_BD * dr, LRU_BD * (dr + 1))
                g = g_all[0, :, lanes].astype(F32)
                for d in range(1, N_DEV):
                    g = g + g_all[d, :, lanes].astype(F32)
                update(nm, g, rows=slice(LRU_W * dr, LRU_W * (dr + 1)))

        def seg(rows6, s):
            return rows6[s // 2][:, SEG * (s % 2):SEG * (s % 2 + 1)]

        b128_ref[...] = jnp.zeros_like(b128_ref)
        dmc_ref[...] = jnp.zeros_like(dmc_ref)
        zero = jnp.zeros((1, D_MODEL), F32)
        ctx6 = [tot[ROW_DMODC:ROW_DMODC + 1, :], tot[ROW_DMODC + 1:ROW_DMODC + 2, :]] + [zero] * (N_MOD - 2)
        for q in range(ada_n // SEG):
            cols = slice(SEG * q, SEG * (q + 1))
            for d in range(N_DEV):
                rows6 = [slab_ref[d, ROW_DMOD + k:ROW_DMOD + k + 1, :] for k in range(N_MOD)]
                b128_ref[d:d + 1, cols] = pick(lambda j, rows6=rows6: seg(rows6, 3 * j + q))
            c = pick(lambda j: seg(ctx6, 3 * j + q))
            b128_ref[N_DEV:N_DEV + 1, cols] = c
            dmc_ref[0:1, cols] = c

    out_shape = []
    for nm in SMALL_PARAMS:
        out_shape += [_sds(wmv[nm][0].shape)] * 4
    out_shape += [_sds((LANES, ada_n)), _sds((SUBLANES, ada_n)), _sds((SUBLANES, LANES))]
    args = [slab_all, ga_all, gx_all] + flat
    grid_spec = pltpu.PrefetchScalarGridSpec(
        num_scalar_prefetch=1, grid=(1,), in_specs=[_full(a.shape) for a in args],
        out_specs=[_full(s.shape) for s in out_shape])
    outs = _pc(body, name="finalize_small", grid_spec=grid_spec, out_shape=out_shape,
               compiler_params=_params("arbitrary"))(chip_idx, *args)
    res = {nm: tuple(outs[4 * k:4 * k + 4]) for k, nm in enumerate(SMALL_PARAMS)}
    return res, outs[4 * n_p], outs[4 * n_p + 1], outs[4 * n_p + 2]


def _block_diag(w):
    eye = jnp.eye(LRU_BLOCKS, dtype=F32)
    return (w[:, :, None, :] * eye[:, None, :, None]).reshape(LRU_W, LRU_W).astype(BF16)


def _lane_rep(v8):
    return jnp.broadcast_to(v8.reshape(SUBLANES, 1), (SUBLANES, LANES))


def kernel(x, c, ctx, c_ctx, w_ada, b_ada, norm1_g, norm2_g, w_in, ret_decay, conv_w, conv_b, lru_wa, lru_ba, lru_wx, lru_bx, lru_lambda, w_out, w_mlp1, w_mlp2, final_g, loss_target, m_c_ctx, m_w_ada, m_b_ada, m_norm1_g, m_norm2_g, m_w_in, m_ret_decay, m_conv_w, m_conv_b, m_lru_wa, m_lru_ba, m_lru_wx, m_lru_bx, m_lru_lambda, m_w_out, m_w_mlp1, m_w_mlp2, m_final_g, v_c_ctx, v_w_ada, v_b_ada, v_norm1_g, v_norm2_g, v_w_in, v_ret_decay, v_conv_w, v_conv_b, v_lru_wa, v_lru_ba, v_lru_wx, v_lru_bx, v_lru_lambda, v_w_out, v_w_mlp1, v_w_mlp2, v_final_g):
    ax, ay, ac = lax.axis_index("x"), lax.axis_index("y"), lax.axis_index("c")
    chip = 2 * ax + ay
    dev = 4 * ax + 2 * ay + ac
    c_idx = jnp.stack([ac, chip]).astype(jnp.int32)
    j_idx = chip.reshape(1).astype(jnp.int32)

    xt = x[0]
    t_len = xt.shape[0]
    ctxt = ctx[0]
    l_len = ctxt.shape[0]
    tgt = loss_target[0]
    ada_n = w_ada.shape[2]

    def my_half(w2d):
        r = w2d.shape[0] // 2
        return lax.dynamic_slice_in_dim(w2d, ac * r, r, axis=0).astype(BF16)

    pad8 = lambda a: jnp.pad(a, ((0, SUBLANES - a.shape[0]), (0, 0)))
    small = jnp.concatenate([pad8(conv_w[0]), pad8(lru_ba[0]), pad8(lru_bx[0]), pad8(lru_lambda[0])], axis=0)
    b_shard = lax.dynamic_slice_in_dim(b_ada, chip * ada_n, ada_n, axis=1)
    gw_in, _, small_all, a16, mod_parts, lgv, sgv = _head(
        my_half(w_in[0]), pad8(c), small, w_ada[0], b_shard, c_ctx, ret_decay[0])
    w4 = gw_in.reshape(N_CHIP, D_MODEL, IN_COLS // N_CHIP)

    mod_all = mod_parts[0::2].transpose(1, 0, 2).reshape(16, N_CHIP * ada_n)
    mod_me = lax.dynamic_slice_in_dim(mod_all, dev, 1, axis=0)
    sh1, sc1, g1, sh2, sc2, g2 = [mod_me[:, D_MODEL * k:D_MODEL * (k + 1)] for k in range(N_MOD)]
    csh1, csc1 = mod_all[8:9, 0:D_MODEL], mod_all[8:9, D_MODEL:2 * D_MODEL]

    cos2, sin2 = _rotary_tables(t_len)
    cos_c, sin_c = jnp.ones((l_len, DH), F32), jnp.zeros((l_len, DH), F32)
    n1g, n2g = norm1_g, norm2_g
    fg = final_g.reshape(1, D_MODEL)

    small_full = small_all[0::2].transpose(1, 0, 2).reshape(4 * SUBLANES, LRU_W)
    cw = small_full[0:4]
    cb = conv_b
    ba_f, ba_b = small_full[8:9], small_full[9:10]
    bx_f, bx_b = small_full[16:17], small_full[17:18]
    lam_f, lam_b = small_full[24:25], small_full[25:26]
    wa_f, wa_b = _block_diag(lru_wa[0, 0]), _block_diag(lru_wa[0, 1])
    wx_f, wx_b = _block_diag(lru_wx[0, 0]), _block_diag(lru_wx[0, 1])
    zero_h = jnp.zeros((1, LRU_W), F32)

    projc, xrc, hcb16 = _inproj_fwd(ctxt, n1g, csh1, csc1, w4, cos_c, sin_c, "inproj_fwd_ctx")
    s_f, s_b = _ctx_state_fwd(projc, lgv)
    xcc = _conv_fwd(xrc, cw, cb, "conv_fwd_ctx")
    par_f, par_b = (wa_f, wx_f, ba_f, bx_f, lam_f), (wa_b, wx_b, ba_b, bx_b, lam_b)
    hcf, hcbk = _lru_fwd(xcc, par_f, par_b, zero_h, zero_h, "lru_fwd_ctx")
    lru_sf, lru_sb = hcf[l_len - 1:l_len], hcbk[0:1]

    h1, h2 = my_half(w_mlp1[0]), my_half(w_mlp2[0])
    q = h1.shape[0] // 2
    (proj, xrl, hb16), ((gw_1a,),) = _inproj_fwd(xt, n1g, sh1, sc1, w4, cos2, sin2, "inproj_fwd",
                                           comms=(_AllGather([h1[:q]]),))
    (o_f, o_b, spf, spb), ((gw_1b, gw_out),) = _ret_fwd(proj, lgv, s_f, s_b,
                                                       comms=(_AllGather([h1[q:], my_half(w_out[0])]),))
    xcl = _conv_fwd(xrl, cw, cb, "conv_fwd")
    (hf, hbk), ((gw_2a,),) = _lru_fwd(xcl, par_f, par_b, lru_sf, lru_sb, "lru_fwd", comms=(_AllGather([h2[:q]]),))
    wo = gw_out.reshape(D_MODEL, D_MODEL)
    (x1, cat), ((gw_2b,),) = _mix_fwd(o_f, o_b, proj, hf, hbk, wo, xt, g1, comms=(_AllGather([h2[q:]]),))

    (dx1, h2b, ab, dub, dmb, dsc2, dsh2, dg2, dn2g, dfg, lossv) = _mlp(
        x1, n2g, sh2, sc2, g2, fg, (gw_1a, gw_1b), (gw_2a, gw_2b), tgt)
    gw_mlp1 = _tn(h2b, dub, N_CHIP, False, True, "grad_w_mlp1")
    b_1 = gw_mlp1.reshape(N_DEV, D_MODEL // 2, MLP_H // N_CHIP)
    gw_mlp2, ((r_1,),) = _tn(ab, dmb, N_CHIP, True, False, "grad_w_mlp2", comms=(_pair_exchange([b_1]),))

    half = D_MODEL // 4
    top, bot = (0, half), (half, half)
    b_2 = gw_mlp2.reshape(N_DEV, MLP_H // N_DEV, D_MODEL)
    p_1, pb_1 = _pair_add(b_1, r_1, c_idx, "rs_pair_add_w_mlp1")
    (do, dhs, dg, dgate, dyb, dg1), ((q_1a,), (r_2,)) = _mix_bwd(
        o_f, o_b, proj, hf, hbk, wo, cat, dx1, g1, comms=(_chip_exchange([pb_1], top), _pair_exchange([b_2])))
    gw_o = _tn(cat, dyb, 1, False, False, "grad_w_out")
    b_o = gw_o.reshape(N_DEV, D_MODEL // N_DEV, D_MODEL)
    p_2, pb_2 = _pair_add(b_2, r_2, c_idx, "rs_pair_add_w_mlp2")

    ((dq_f, dk_f, dv_f, ds_f, drd_f), (dq_b, dk_b, dv_b, ds_b, drd_b)), ((q_1b,), (q_2a,), (r_o,)) = _ret_bwd(
        proj, lgv, sgv, spf, spb, do,
        comms=(_chip_exchange([pb_1], bot), _chip_exchange([pb_2], top), _pair_exchange([b_o])))
    p_o, pb_o = _pair_add(b_o, r_o, c_idx, "rs_pair_add_w_out")
    h_1 = _chip_add(p_1, (q_1a, q_1b), c_idx, "rs_chip_add_w_mlp1")

    ((dxc_f, dpre_f, dba_f, dbx_f, dlam_f, dh0_f), (dxc_b, dpre_b, dba_b, dbx_b, dlam_b, dh0_b)), (
        (q_2b,), (q_o,), (f_1,)) = _lru_bwd(
        xcl, par_f, par_b, hf, hbk, lru_sf, lru_sb, dhs, dhs, "lru_bwd",
        comms=(_chip_exchange([pb_2], bot), _chip_exchange([pb_o]), _pair_gather([h_1])))
    h_2 = _chip_add(p_2, (q_2a, q_2b), c_idx, "rs_chip_add_w_mlp2")
    h_o = _chip_add(p_o, (q_o,), c_idx, "rs_chip_add_w_out")
    dxr, dcw, dcb = _conv_bwd(dxc_f, dxc_b, xrl, cw, "conv_bwd")
    grad_x, dpb, dn1g, dsh1, dsc1 = _inproj_bwd(
        xt, n1g, sh1, sc1, w4, cos2, sin2, [dq_f, dq_b, dk_f, dk_b, dv_f, dv_b, dg, dxr, dgate], dx1, "inproj_bwd")

    dkc, dvc, drd_c = _ctx_state_bwd(projc, lgv, sgv, ds_f, ds_b)
    zc = jnp.zeros((l_len, LRU_W), F32)
    dhc_f = lax.dynamic_update_slice(zc, dh0_f, (l_len - 1, 0))
    dhc_b = lax.dynamic_update_slice(zc, dh0_b, (0, 0))
    ((dxcc_f, dprec_f, dbac_f, dbxc_f, dlamc_f, _), (dxcc_b, dprec_b, dbac_b, dbxc_b, dlamc_b, _)), _ = _lru_bwd(
        xcc, par_f, par_b, hcf, hcbk, zero_h, zero_h, dhc_f, dhc_b, "lru_bwd_ctx")
    dxrc, dcw_c, dcb_c = _conv_bwd(dxcc_f, dxcc_b, xrc, cw, "conv_bwd_ctx")
    zr = jnp.zeros((l_len, RET_W), BF16)
    _, dpbc, dn1g_c, dcsh1, dcsc1 = _inproj_bwd(
        ctxt, n1g, csh1, csc1, w4, cos_c, sin_c, [zr, zr, dkc, zr, dvc, zr, zr, dxrc, zr],
        jnp.zeros((l_len, D_MODEL), F32), "inproj_bwd_ctx")

    gw_i = _tn(hb16, dpb, N_CHIP, False, True, "grad_w_in", extra=(hcb16, dpbc))
    b_i = gw_i.reshape(N_DEV, D_MODEL // 2, IN_COLS // N_CHIP)
    gwa_f, ((r_i,), (f_2,), (f_o,)) = _tn(xcl, dpre_f, 2, False, True, "grad_lru_gates_f", extra=(xcc, dprec_f),
                                          comms=(_pair_exchange([b_i]), _pair_gather([h_2]), _pair_gather([h_o])))
    p_i, pb_i = _pair_add(b_i, r_i, c_idx, "rs_pair_add_w_in")
    gwa_b, ((q_i,),) = _tn(xcl, dpre_b, 2, False, True, "grad_lru_gates_b", extra=(xcc, dprec_b),
                           comms=(_chip_exchange([pb_i]),))
    slab, ga, gx = _pack_small(
        [lossv, dsh1, dsc1, dg1, dsh2, dsc2, dg2, dcsh1, dcsc1, dn1g, dn1g_c, dn2g, dfg],
        (drd_f, drd_b, drd_c), (dcw, dcw_c), (dcb, dcb_c),
        (dba_f, dbac_f, dba_b, dbac_b, dbx_f, dbxc_f, dbx_b, dbxc_b, dlam_f, dlamc_f, dlam_b, dlamc_b),
        (gwa_f, gwa_b))
    (f_i,), (slab_all, ga_all, gx_all) = _run_comms(
        [_pair_gather([_chip_add(p_i, (q_i,), c_idx, "rs_chip_add_w_in")]), _AllGather([slab, ga, gx])],
        "tail_exchanges")
    g_in, g_out, g_1, g_2 = _shard_of(f_i), _shard_of(f_o), _shard_of(f_1), _shard_of(f_2)
    big = {}
    for nm, w, g, m, v in (("w_in", w_in, g_in, m_w_in, v_w_in), ("w_out", w_out, g_out, m_w_out, v_w_out),
                           ("w_mlp1", w_mlp1, g_1, m_w_mlp1, v_w_mlp1), ("w_mlp2", w_mlp2, g_2, m_w_mlp2, v_w_mlp2)):
        go, d_, mn, vn = _adamw(w[0], g, m[0], v[0], "adamw_" + nm)
        big[nm] = (go[None], d_[None], mn[None], vn[None])
    params = {
        "b_ada": (b_ada, m_b_ada, v_b_ada), "norm1_g": (norm1_g, m_norm1_g, v_norm1_g),
        "norm2_g": (norm2_g, m_norm2_g, v_norm2_g), "final_g": (final_g, m_final_g, v_final_g),
        "ret_decay": (ret_decay, m_ret_decay, v_ret_decay), "conv_w": (conv_w, m_conv_w, v_conv_w),
        "conv_b": (conv_b, m_conv_b, v_conv_b), "lru_wa": (lru_wa, m_lru_wa, v_lru_wa),
        "lru_ba": (lru_ba, m_lru_ba, v_lru_ba), "lru_wx": (lru_wx, m_lru_wx, v_lru_wx),
        "lru_bx": (lru_bx, m_lru_bx, v_lru_bx), "lru_lambda": (lru_lambda, m_lru_lambda, v_lru_lambda),
    }
    as2d = {
        "b_ada": lambda a: a, "norm1_g": lambda a: a, "norm2_g": lambda a: a, "conv_b": lambda a: a,
        "final_g": lambda a: a.reshape(1, D_MODEL), "ret_decay": lambda a: _lane_rep(a.reshape(-1)),
        "conv_w": lambda a: a[0], "lru_ba": lambda a: a[0], "lru_bx": lambda a: a[0], "lru_lambda": lambda a: a[0],
        "lru_wa": lambda a: a.reshape(2 * LRU_W, LRU_BD), "lru_wx": lambda a: a.reshape(2 * LRU_W, LRU_BD),
    }
    res, b128, dmc8, loss8 = _finalize_small(
        j_idx, slab_all, ga_all, gx_all, {nm: tuple(as2d[nm](a) for a in params[nm]) for nm in SMALL_PARAMS})
    loss = loss8[0, 0]
    small_out = {}
    for nm in SMALL_PARAMS:
        shp = params[nm][0].shape
        if nm == "ret_decay":
            small_out[nm] = tuple(o[:, 0].reshape(shp) for o in res[nm])
        else:
            small_out[nm] = tuple(o.reshape(shp) for o in res[nm])

    g_ada = _ada_grad(jnp.pad(a16.T, ((0, 0), (0, LANES - 16))), b128)
    g_ada, d_ada, m_ada, v_ada = _adamw(w_ada[0], g_ada, m_w_ada[0], v_w_ada[0], "adamw_w_ada")

    (cparts,) = _all_gather([_cctx_partial(dmc8, w_ada[0])], "gather_cctx")
    g_cc, d_cc, m_cc, v_cc = _cctx_final(cparts, c_ctx, m_c_ctx, v_c_ctx)
    small_out["c_ctx"] = tuple(a.reshape(D_MODEL) for a in (g_cc, d_cc, m_cc, v_cc))
    small_out["w_ada"] = (g_ada[None], d_ada[None], m_ada[None], v_ada[None])
    small_out.update(big)

    order = ["c_ctx", "w_ada", "b_ada", "norm1_g", "norm2_g", "w_in", "ret_decay", "conv_w", "conv_b", "lru_wa", "lru_ba",
             "lru_wx", "lru_bx", "lru_lambda", "w_out", "w_mlp1", "w_mlp2", "final_g"]
    outs = [loss, grad_x[None]]
    for k in range(4):
        outs += [small_out[nm][k] for nm in order]
    return tuple(outs)
```

```python
import math

import jax
import jax.numpy as jnp
from jax import lax
from jax.experimental import pallas as pl
from jax.experimental.pallas import tpu as pltpu

F32 = jnp.float32
BF16 = jnp.bfloat16

D_MODEL = 1024
HEADS = 4
DH = 128
CHUNK = 256
RET_W = HEADS * DH
LRU_W = 512
LRU_BLOCKS = 8
LRU_BD = LRU_W // LRU_BLOCKS
LRU_C = 8.0
IN_COLS = 4 * RET_W + 2 * LRU_W
MLP_H = 4 * D_MODEL
N_MOD = 6
GRID_W = 64
ROPE_BASE = 10000.0
K_SCALE = DH ** -0.5
EPS = 1e-6
GELU_K = math.sqrt(2.0 / math.pi)
GELU_C = 0.044715

ADAM_LR = 0.001
ADAM_B1 = 0.9
ADAM_B2 = 0.999
ADAM_EPS = 1e-08
ADAM_WD = 0.01
ADAM_STEP = 10

N_DEV = 8
N_CHIP = 4
SUBLANES = 8
LANES = 128
VMEM_LIMIT_V7X = 56 * 1024 * 1024
MESH = pl.DeviceIdType.MESH
ANY = pl.BlockSpec(memory_space=pl.ANY)


def _pc(body, **kw):
    return pl.pallas_call(body, **kw)


def _params(*sem):
    return pltpu.CompilerParams(dimension_semantics=sem if sem else None, vmem_limit_bytes=VMEM_LIMIT_V7X)


def _tile(t, big=False):
    if big and t >= 1024:
        return 512
    return 256 if t >= 256 else t


def _sds(shape, dtype=F32):
    return jax.ShapeDtypeStruct(tuple(shape), dtype)


def _full(shape):
    nd = len(shape)
    return pl.BlockSpec(tuple(shape), lambda *_: (0,) * nd)


def _sigmoid(x):
    return 0.5 * jnp.tanh(0.5 * x) + 0.5


def _log1p_pos(y):
    s = y * (1.0 - y * (0.5 - y * (1.0 / 3.0 - y * (0.25 - y * (0.2 - y / 6.0)))))
    return jnp.where(y < 0.03, s, jnp.log(1.0 + y))


def _softplus(z):
    return jnp.maximum(z, 0.0) + _log1p_pos(jnp.exp(-jnp.abs(z)))


def _one_minus_sq(la, a):
    return -jnp.tanh(la) * (1.0 + a * a)


def _rms(x):
    r = lax.rsqrt(jnp.mean(x * x, axis=-1, keepdims=True) + EPS)
    return x * r, r


def _dot(a, b):
    return jnp.dot(a, b, preferred_element_type=F32)


def _dot_nt(a, b):
    return lax.dot_general(a, b, (((1,), (1,)), ((), ())), preferred_element_type=F32)


def _dot_tn(a, b):
    return lax.dot_general(a, b, (((0,), (0,)), ((), ())), preferred_element_type=F32)


def _sum0(x):
    return jnp.sum(x, axis=0, keepdims=True)


def _norm_mod_bwd(x, g, sc, dh):
    xh, r = _rms(x)
    hn = xh * g
    dhn = dh * (1.0 + sc)
    dxh = dhn * g
    dx = r * (dxh - xh * jnp.mean(dxh * xh, axis=-1, keepdims=True))
    return dx, _sum0(dhn * xh), _sum0(dh), _sum0(dh * hn)


def _dev_index(p):
    return 4 * p[0] + 2 * p[1] + p[2]


def _mesh_pos():
    return lax.axis_index("x"), lax.axis_index("y"), lax.axis_index("c")


class _AllGather:
    def __init__(self, arrs):
        n = len(arrs)
        self.arrays = list(arrs)
        self.out_shapes = [_sds((N_DEV,) + a.shape, a.dtype) for a in arrs]
        self.scratch = ([pltpu.VMEM(a.shape, a.dtype) for a in arrs]
                        + [pltpu.SemaphoreType.DMA((7 * n,)), pltpu.SemaphoreType.DMA((7 * n,)),
                           pltpu.SemaphoreType.DMA((n,))])
        self.aliases = {}

    def _parts(self, ins, outs, scr):
        n = len(self.arrays)
        stage = scr[:n]
        send_sems, recv_sems, local_sems = scr[n:]
        x, y, c = _mesh_pos()
        me, sib = (x, y, c), (x, y, 1 - c)
        chips = [(1 - x, y), (x, 1 - y), (1 - x, 1 - y)]

        def copy(t, k, block, to, own=False):
            dst = outs[t].at[_dev_index(block)]
            return pltpu.make_async_remote_copy(
                src_ref=ins[t] if own else dst, dst_ref=dst,
                send_sem=send_sems.at[7 * t + k], recv_sem=recv_sems.at[7 * t + k],
                device_id=to, device_id_type=MESH)

        first = []
        for t in range(n):
            first.append(copy(t, 0, me, sib, own=True))
            for j, ch in enumerate(chips):
                first.append(copy(t, 1 + j, me, (*ch, c), own=True))
        stage_in = [pltpu.make_async_copy(ins[t], stage[t], local_sems.at[t]) for t in range(n)]
        mine = [pltpu.make_async_copy(stage[t], outs[t].at[_dev_index(me)], local_sems.at[t]) for t in range(n)]
        return n, c, me, sib, chips, copy, first, stage_in, mine

    def start(self, ins, outs, scr):
        n, _, _, _, _, _, first, stage_in, mine = self._parts(ins, outs, scr)
        for cp in stage_in:
            cp.start()
        for cp in first:
            cp.start()
        for t in range(n):
            stage_in[t].wait()
            mine[t].start()

    def relay(self, ins, outs, scr):
        n, c, me, sib, chips, copy, _, _, _ = self._parts(ins, outs, scr)
        for j, ch in enumerate(chips):
            for t in range(n):
                copy(t, 1 + j, (*ch, c), me).wait_recv()
                copy(t, 4 + j, (*ch, c), sib).start()

    def finish(self, ins, outs, scr):
        n, c, me, sib, chips, copy, first, _, mine = self._parts(ins, outs, scr)
        passed = [copy(t, 4 + j, (*ch, c), sib) for j, ch in enumerate(chips) for t in range(n)]
        for t in range(n):
            copy(t, 0, sib, me).wait_recv()
            for j, ch in enumerate(chips):
                copy(t, 4 + j, (*ch, 1 - c), me).wait_recv()
        for cp in first + passed:
            cp.wait_send()
        for cp in mine:
            cp.wait()


class _Exchange:
    def __init__(self, arrays, out_shapes, plan, n_copies, aliases=None):
        self.arrays = list(arrays)
        self.out_shapes = list(out_shapes)
        self.plan = plan
        self.scratch = [pltpu.SemaphoreType.DMA((n_copies,)), pltpu.SemaphoreType.DMA((n_copies,))]
        self.aliases = aliases or {}

    def _copies(self, ins, outs, scr):
        send_sems, recv_sems = scr
        snd, rcv = [], []
        for i, (src, dst, peer, lands) in enumerate(self.plan(ins, outs, _mesh_pos())):
            kw = dict(send_sem=send_sems.at[i], recv_sem=recv_sems.at[i], device_id=peer, device_id_type=MESH)
            snd.append(pltpu.make_async_remote_copy(src_ref=src, dst_ref=dst, **kw))
            rcv.append(pltpu.make_async_remote_copy(src_ref=src, dst_ref=lands, **kw))
        return snd, rcv

    def start(self, ins, outs, scr):
        for cp in self._copies(ins, outs, scr)[0]:
            cp.start()

    def relay(self, ins, outs, scr):
        pass

    def finish(self, ins, outs, scr):
        snd, rcv = self._copies(ins, outs, scr)
        for cp in rcv:
            cp.wait_recv()
        for cp in snd:
            cp.wait_send()


def _pair_exchange(grads):
    n = len(grads)

    def plan(ins, outs, pos):
        x, y, c = pos
        return [(ins[t].at[2 * j + (1 - c)], outs[t].at[j], (x, y, 1 - c), outs[t].at[j])
                for t in range(n) for j in range(N_CHIP)]

    return _Exchange(grads, [_sds((N_CHIP,) + g.shape[1:], g.dtype) for g in grads], plan, N_CHIP * n)


def _chip_exchange(parts, rows=None):
    n = len(parts)

    def plan(ins, outs, pos):
        x, y, c = pos
        chips = [(1 - x, y), (x, 1 - y), (1 - x, 1 - y)]

        def src(t, ch):
            blk = ins[t].at[2 * ch[0] + ch[1]]
            return blk if rows is None else blk.at[pl.ds(rows[0], rows[1])]

        return [(src(t, ch), outs[t].at[k], (*ch, c), outs[t].at[k]) for t in range(n) for k, ch in enumerate(chips)]

    shapes = [_sds((3, p.shape[1] if rows is None else rows[1]) + p.shape[2:], p.dtype) for p in parts]
    return _Exchange(parts, shapes, plan, 3 * n)


def _pair_gather(bufs):
    n = len(bufs)

    def plan(ins, outs, pos):
        x, y, c = pos
        return [(ins[t].at[c], outs[t].at[c], (x, y, 1 - c), outs[t].at[1 - c]) for t in range(n)]

    return _Exchange(bufs, [_sds(b.shape, b.dtype) for b in bufs], plan, n, aliases={t: t for t in range(n)})


def _run_comms(comms, name):
    c_in = [len(cm.arrays) for cm in comms]
    c_out = [len(cm.out_shapes) for cm in comms]
    c_scr = [len(cm.scratch) for cm in comms]
    aliases = {}
    for k, cm in enumerate(comms):
        for a, b in cm.aliases.items():
            aliases[sum(c_in[:k]) + a] = sum(c_out[:k]) + b

    def split(refs, counts):
        out, pos = [], 0
        for cnt in counts:
            out.append(refs[pos:pos + cnt])
            pos += cnt
        return out

    def body(*refs):
        ins = split(refs[:sum(c_in)], c_in)
        outs = split(refs[sum(c_in):sum(c_in) + sum(c_out)], c_out)
        scr = split(refs[sum(c_in) + sum(c_out):], c_scr)
        for phase in ("start", "relay", "finish"):
            for k, cm in enumerate(comms):
                getattr(cm, phase)(ins[k], outs[k], scr[k])

    outs = _pc(body, name=name, out_shape=[s for cm in comms for s in cm.out_shapes],
               in_specs=[ANY] * sum(c_in), out_specs=[ANY] * sum(c_out), input_output_aliases=aliases,
               scratch_shapes=[s for cm in comms for s in cm.scratch],
               compiler_params=_params())(*[a for cm in comms for a in cm.arrays])
    return split(list(outs), c_out)


def _all_gather(arrs, name):
    return _run_comms([_AllGather(arrs)], name)[0]


def _call(body, *, name, grid, in_specs, out_specs, out_shape, scratch_shapes, sem, args, comms=()):
    n_in, n_out, n_scr = len(in_specs), len(out_specs), len(scratch_shapes)
    c_in = [len(cm.arrays) for cm in comms]
    c_out = [len(cm.out_shapes) for cm in comms]
    c_scr = [len(cm.scratch) for cm in comms]
    aliases = {}
    for k, cm in enumerate(comms):
        for a, b in cm.aliases.items():
            aliases[n_in + sum(c_in[:k]) + a] = n_out + sum(c_out[:k]) + b

    def split(refs, counts):
        out, pos = [], 0
        for cnt in counts:
            out.append(refs[pos:pos + cnt])
            pos += cnt
        return out

    def wrapped(*refs):
        ins = refs[:n_in + sum(c_in)]
        outs = refs[len(ins):len(ins) + n_out + sum(c_out)]
        scr = refs[len(ins) + len(outs):]
        cins, couts, cscr = split(ins[n_in:], c_in), split(outs[n_out:], c_out), split(scr[n_scr:], c_scr)
        if comms:
            first = pl.program_id(0) == 0
            last = pl.program_id(0) == grid[0] - 1
            for k in range(1, len(grid)):
                first = jnp.logical_and(first, pl.program_id(k) == 0)
                last = jnp.logical_and(last, pl.program_id(k) == grid[k] - 1)

            @pl.when(first)
            def _():
                for k, cm in enumerate(comms):
                    cm.start(cins[k], couts[k], cscr[k])
        body(*ins[:n_in], *outs[:n_out], *scr[:n_scr])
        if comms:
            relay_early = len(grid) == 1 and grid[0] >= 4
            if relay_early:
                @pl.when(pl.program_id(0) == (7 * grid[0]) // 8 - 1)
                def _():
                    for k, cm in enumerate(comms):
                        cm.relay(cins[k], couts[k], cscr[k])

            @pl.when(last)
            def _():
                for k, cm in enumerate(comms):
                    if not relay_early:
                        cm.relay(cins[k], couts[k], cscr[k])
                    cm.finish(cins[k], couts[k], cscr[k])

    outs = _pc(wrapped, name=name, grid=grid,
               in_specs=list(in_specs) + [ANY] * sum(c_in), out_specs=list(out_specs) + [ANY] * sum(c_out),
               out_shape=list(out_shape) + [s for cm in comms for s in cm.out_shapes],
               scratch_shapes=list(scratch_shapes) + [s for cm in comms for s in cm.scratch],
               input_output_aliases=aliases, compiler_params=_params(*sem),
               )(*args, *[a for cm in comms for a in cm.arrays])
    outs = list(outs)
    return outs[:n_out], split(outs[n_out:], c_out)


def _row_block(r):
    for b in (512, 256, 128, 64, 32, 16, 8):
        if r % b == 0:
            return b
    return r


def _pair_add(g, recv, cj_idx, name):
    _, r, cc = g.shape
    br = _row_block(r)

    def body(cj_ref, g_ref, r_ref, own_ref, pb_ref):
        s = g_ref[...] + r_ref[...]
        pb_ref[...] = s.astype(BF16)

        @pl.when(pl.program_id(1) == cj_ref[1])
        def _():
            own_ref[...] = s[0]

    grid_spec = pltpu.PrefetchScalarGridSpec(
        num_scalar_prefetch=1, grid=(r // br, N_CHIP),
        in_specs=[pl.BlockSpec((1, br, cc), lambda i, j, cj_ref: (2 * j + cj_ref[0], i, 0)),
                  pl.BlockSpec((1, br, cc), lambda i, j, cj_ref: (j, i, 0))],
        out_specs=[pl.BlockSpec((br, cc), lambda i, j, cj_ref: (i, 0)),
                   pl.BlockSpec((1, br, cc), lambda i, j, cj_ref: (j, i, 0))])
    return _pc(body, name=name, grid_spec=grid_spec,
               out_shape=[_sds((r, cc)), _sds((N_CHIP, r, cc), BF16)],
               compiler_params=_params("arbitrary", "arbitrary"))(cj_idx, g, recv)


def _chip_add(p, qs, cj_idx, name):
    r, cc = p.shape
    nq = len(qs)
    br = _row_block(r // nq)
    nb = r // nq // br

    def body(cj_ref, p_ref, *refs):
        o_ref = refs[-1]
        if nq == 2:
            top = pl.program_id(0) < nb
            q = [jnp.where(top, refs[0][k], refs[1][k]).astype(F32) for k in range(3)]
        else:
            q = [refs[0][k].astype(F32) for k in range(3)]
        o_ref[0] = ((p_ref[...] + q[0]) + q[1]) + q[2]

    q_specs = [pl.BlockSpec((3, br, cc), lambda i, cj_ref, h=h: (0, jnp.clip(i - h * nb, 0, nb - 1), 0))
               for h in range(nq)]
    grid_spec = pltpu.PrefetchScalarGridSpec(
        num_scalar_prefetch=1, grid=(r // br,),
        in_specs=[pl.BlockSpec((br, cc), lambda i, cj_ref: (i, 0))] + q_specs,
        out_specs=pl.BlockSpec((1, br, cc), lambda i, cj_ref: (cj_ref[0], i, 0)))
    return _pc(body, name=name, grid_spec=grid_spec, out_shape=_sds((2, r, cc)),
               compiler_params=_params("arbitrary"))(cj_idx, p, *qs)


def _shard_of(both):
    return both.reshape((2 * both.shape[1],) + both.shape[2:])


ADAMW_CHUNKS = 4


def _adamw(w, g, m, v, name):
    r, cc = w.shape
    rows = r // ADAMW_CHUNKS
    assert rows * ADAMW_CHUNKS == r and rows % SUBLANES == 0
    c1 = 1.0 - ADAM_B1 ** ADAM_STEP
    c2 = 1.0 - ADAM_B2 ** ADAM_STEP

    def body(w_hbm, g_hbm, m_hbm, v_hbm, go_hbm, d_hbm, mo_hbm, vo_hbm, wb, gb, mb, vb, sem_in, sem_out):
        srcs, bufs, dsts = (w_hbm, g_hbm, m_hbm, v_hbm), (wb, gb, mb, vb), (d_hbm, go_hbm, mo_hbm, vo_hbm)

        def load(a, k):
            sl = pl.ds(k * rows, rows)
            return pltpu.make_async_copy(srcs[a].at[sl], bufs[a].at[sl], sem_in.at[a, k])

        def store(a, k):
            sl = pl.ds(k * rows, rows)
            return pltpu.make_async_copy(bufs[a].at[sl], dsts[a].at[sl], sem_out.at[a, k])

        for k in range(ADAMW_CHUNKS):
            for a in range(4):
                load(a, k).start()
        for k in range(ADAMW_CHUNKS):
            for a in range(4):
                load(a, k).wait()
            store(1, k).start()
            sl = pl.ds(k * rows, rows)
            gg = gb[sl]
            mn = ADAM_B1 * mb[sl] + (1.0 - ADAM_B1) * gg
            vn = ADAM_B2 * vb[sl] + (1.0 - ADAM_B2) * (gg * gg)
            mh = mn / c1
            vh = vn / c2
            wb[sl] = -ADAM_LR * (mh / (jnp.sqrt(vh) + ADAM_EPS) + ADAM_WD * wb[sl])
            mb[sl] = mn
            vb[sl] = vn
            for a in (0, 2, 3):
                store(a, k).start()
        for k in range(ADAMW_CHUNKS):
            for a in range(4):
                store(a, k).wait()

    go, d, mo, vo = _pc(body, name=name, in_specs=[ANY] * 4, out_specs=[ANY] * 4, out_shape=[_sds((r, cc))] * 4,
                        scratch_shapes=[pltpu.VMEM((r, cc), F32)] * 4 + [pltpu.SemaphoreType.DMA((4, ADAMW_CHUNKS))] * 2,
                        compiler_params=_params())(w, g, m, v)
    return go, d, mo, vo


def _head(w_half, c8, small, w_ada, b_shard, c_ctx, ret_decay):
    ada_n = w_ada.shape[1]
    mod_sds = _sds((16, ada_n))
    ag_w, ag_c, ag_m = _AllGather([w_half]), _AllGather([c8, small]), _AllGather([mod_sds])
    n_w, n_c, n_m = len(ag_w.scratch), len(ag_c.scratch), len(ag_m.scratch)

    def body(w_ref, c_ref, s_ref, wada_ref, b_ref, cc_ref, rd_ref,
             gw_ref, call_ref, sall_ref, a_ref, modp_ref, mall_ref, lg_ref, sg_ref, *scr):
        scr_w, scr_c, scr_m = scr[:n_w], scr[n_w:n_w + n_c], scr[n_w + n_c:n_w + n_c + n_m]
        c_v, w_v, m_v, sems = scr[n_w + n_c + n_m:]
        ag_w.start((w_ref,), (gw_ref,), scr_w)
        ag_c.start((c_ref, s_ref), (call_ref, sall_ref), scr_c)
        load_w = pltpu.make_async_copy(wada_ref, w_v, sems.at[0])
        load_w.start()
        rd = rd_ref[...]
        lg_ref[...] = -_softplus(-rd)
        sg_ref[...] = _sigmoid(-rd)
        ag_c.relay((c_ref, s_ref), (call_ref, sall_ref), scr_c)
        ag_c.finish((c_ref, s_ref), (call_ref, sall_ref), scr_c)
        load_c = pltpu.make_async_copy(call_ref, c_v, sems.at[1])
        load_c.start()
        load_c.wait()
        a_ref[...] = jnp.zeros_like(a_ref)
        for d in range(N_DEV):
            cd = c_v[d, 0:1, :]
            a_ref[d:d + 1, :] = cd * _sigmoid(cd)
        cc = cc_ref[...]
        a_ref[N_DEV:N_DEV + 1, :] = cc * _sigmoid(cc)
        load_w.wait()
        m_v[...] = jnp.dot(a_ref[...], w_v[...], preferred_element_type=F32,
                           precision=lax.Precision.HIGHEST) + b_ref[...]
        put = pltpu.make_async_copy(m_v, modp_ref, sems.at[2])
        put.start()
        put.wait()
        ag_m.start((modp_ref,), (mall_ref,), scr_m)
        ag_m.relay((modp_ref,), (mall_ref,), scr_m)
        ag_m.finish((modp_ref,), (mall_ref,), scr_m)
        ag_w.relay((w_ref,), (gw_ref,), scr_w)
        ag_w.finish((w_ref,), (gw_ref,), scr_w)

    rd = jnp.broadcast_to(ret_decay.reshape(2, HEADS).T[:, :, None], (HEADS, 2, LANES))
    lane = _full((HEADS, 2, LANES))
    outs = _pc(
        body, name="head",
        in_specs=[ANY, ANY, ANY, ANY, _full((1, ada_n)), _full((1, D_MODEL)), lane],
        out_specs=[ANY, ANY, ANY, _full((16, D_MODEL)), ANY, ANY, lane, lane],
        out_shape=ag_w.out_shapes + ag_c.out_shapes + [_sds((16, D_MODEL)), mod_sds] + ag_m.out_shapes
        + [_sds((HEADS, 2, LANES))] * 2,
        scratch_shapes=ag_w.scratch + ag_c.scratch + ag_m.scratch
        + [pltpu.VMEM((N_DEV,) + c8.shape, F32), pltpu.VMEM(w_ada.shape, F32), pltpu.VMEM((16, ada_n), F32),
           pltpu.SemaphoreType.DMA((3,))],
        compiler_params=_params(),
    )(w_half, c8, small, w_ada, b_shard, c_ctx.reshape(1, D_MODEL), rd)
    gw, c_all, small_all, a16, _, mod_all, lgv, sgv = outs
    return gw, c_all, small_all, a16, mod_all, lgv, sgv


def _ada_grad(at, b):
    n = b.shape[1]
    bn = 512

    def body(a_ref, b_ref, o_ref):
        o_ref[...] = jnp.dot(a_ref[...], b_ref[...], preferred_element_type=F32, precision=lax.Precision.HIGHEST)

    return _pc(body, name="ada_grad", grid=(n // bn,),
               in_specs=[_full((D_MODEL, LANES)), pl.BlockSpec((LANES, bn), lambda i: (0, i))],
               out_specs=pl.BlockSpec((D_MODEL, bn), lambda i: (0, i)), out_shape=_sds((D_MODEL, n)),
               compiler_params=_params("arbitrary"))(at, b)


def _cctx_partial(dmc8, w_ada):
    n = w_ada.shape[1]
    bn = 512

    def body(d_ref, w_ref, o_ref):
        @pl.when(pl.program_id(0) == 0)
        def _():
            o_ref[...] = jnp.zeros_like(o_ref)
        o_ref[...] += lax.dot_general(d_ref[...], w_ref[...], (((1,), (1,)), ((), ())),
                                      preferred_element_type=F32, precision=lax.Precision.HIGHEST)

    return _pc(body, name="cctx_partial", grid=(n // bn,),
               in_specs=[pl.BlockSpec((8, bn), lambda i: (0, i)), pl.BlockSpec((D_MODEL, bn), lambda i: (0, i))],
               out_specs=_full((8, D_MODEL)), out_shape=_sds((8, D_MODEL)),
               compiler_params=_params("arbitrary"))(dmc8, w_ada)


def _cctx_final(parts, c_ctx, m, v):
    c1 = 1.0 - ADAM_B1 ** ADAM_STEP
    c2 = 1.0 - ADAM_B2 ** ADAM_STEP

    def body(p_ref, c_ref, m_ref, v_ref, g_ref, d_ref, mo_ref, vo_ref):
        s = ((p_ref[0, 0:1, :] + p_ref[2, 0:1, :]) + p_ref[4, 0:1, :]) + p_ref[6, 0:1, :]
        z = c_ref[...]
        sg = _sigmoid(z)
        gg = s * (sg * (1.0 + z * (1.0 - sg)))
        g_ref[...] = gg
        mn = ADAM_B1 * m_ref[...] + (1.0 - ADAM_B1) * gg
        vn = ADAM_B2 * v_ref[...] + (1.0 - ADAM_B2) * (gg * gg)
        d_ref[...] = -ADAM_LR * ((mn / c1) / (jnp.sqrt(vn / c2) + ADAM_EPS) + ADAM_WD * z)
        mo_ref[...] = mn
        vo_ref[...] = vn

    row = _full((1, D_MODEL))
    return _pc(body, name="cctx_final", out_shape=[_sds((1, D_MODEL))] * 4,
               in_specs=[_full(parts.shape), row, row, row], out_specs=[row] * 4,
               compiler_params=_params())(parts, c_ctx.reshape(1, D_MODEL), m.reshape(1, D_MODEL), v.reshape(1, D_MODEL))


def _rotary_tables(t_len):
    rows = t_len // GRID_W
    n_freq = DH // 4
    inv = ROPE_BASE ** (-jnp.arange(n_freq, dtype=F32) / n_freq)
    row_ang = jnp.arange(rows, dtype=F32)[:, None] * inv
    col_ang = jnp.arange(GRID_W, dtype=F32)[:, None] * inv

    def spread(fn):
        return jnp.concatenate([jnp.repeat(fn(row_ang), GRID_W, axis=0), jnp.tile(fn(col_ang), (rows, 1))], axis=-1)

    cos, sin = spread(jnp.cos), spread(jnp.sin)
    return jnp.concatenate([cos, cos], axis=-1), jnp.concatenate([-sin, sin], axis=-1)


def _inproj_fwd(x, gn, sh, sc, w4, cos2, sin2, name, comms=()):
    t = x.shape[0]
    tm = _tile(t, True)
    nc = IN_COLS // N_CHIP

    def body(x_ref, gn_ref, sh_ref, sc_ref, w_ref, c_ref, s_ref, p_ref, xr_ref, hb_ref, p_s):
        xh, _ = _rms(x_ref[...])
        h = xh * gn_ref[...] * (1.0 + sc_ref[...]) + sh_ref[...]
        hb = h.astype(BF16)
        hb_ref[...] = hb
        for j in range(N_CHIP):
            p_s[:, nc * j:nc * (j + 1)] = _dot(hb, w_ref[j])
        cc = c_ref[...]
        ss = s_ref[...]
        for hh in range(2 * HEADS):
            blk = p_s[:, DH * hh:DH * (hh + 1)]
            rot = blk * cc + pltpu.roll(blk, DH // 2, 1) * ss
            if hh >= HEADS:
                rot = rot * K_SCALE
            p_ref[:, DH * hh:DH * (hh + 1)] = rot.astype(BF16)
        p_ref[:, 2 * RET_W:] = p_s[:, 2 * RET_W:].astype(BF16)
        xr_ref[...] = p_s[:, 4 * RET_W:4 * RET_W + LRU_W]

    row = _full((1, D_MODEL))
    outs, couts = _call(
        body, name=name, grid=(t // tm,),
        in_specs=[pl.BlockSpec((tm, D_MODEL), lambda i: (i, 0)), row, row, row, _full(w4.shape),
                  pl.BlockSpec((tm, DH), lambda i: (i, 0)), pl.BlockSpec((tm, DH), lambda i: (i, 0))],
        out_specs=[pl.BlockSpec((tm, IN_COLS), lambda i: (i, 0)), pl.BlockSpec((tm, LRU_W), lambda i: (i, 0)),
                   pl.BlockSpec((tm, D_MODEL), lambda i: (i, 0))],
        out_shape=[_sds((t, IN_COLS), BF16), _sds((t, LRU_W)), _sds((t, D_MODEL), BF16)],
        scratch_shapes=[pltpu.VMEM((tm, IN_COLS), F32)], sem=("arbitrary",),
        args=(x, gn, sh, sc, w4, cos2, sin2), comms=comms)
    return (outs, couts) if comms else outs


def _inproj_bwd(x, gn, sh, sc, w4, cos2, sin2, pieces, dres, name):
    t = x.shape[0]
    tm = _tile(t)
    nc = IN_COLS // N_CHIP

    def body(x_ref, gn_ref, sh_ref, sc_ref, w_ref, c_ref, s_ref, dqf, dqb, dkf, dkb, dvf, dvb, dg, dxr, dgt, dres_ref,
             dx_ref, dpb_ref, dgn_ref, dsh_ref, dsc_ref):
        cc = c_ref[...]
        ss = s_ref[...]
        dq = dqf[...].astype(F32) + dqb[...].astype(F32)
        dk = dkf[...].astype(F32) + dkb[...].astype(F32)
        for hh in range(HEADS):
            sl = slice(DH * hh, DH * (hh + 1))
            b = dq[:, sl]
            dpb_ref[:, sl] = (b * cc + pltpu.roll(b * ss, DH // 2, 1)).astype(BF16)
            b = dk[:, sl]
            dpb_ref[:, RET_W + DH * hh:RET_W + DH * (hh + 1)] = (
                (b * cc + pltpu.roll(b * ss, DH // 2, 1)) * K_SCALE).astype(BF16)
        dpb_ref[:, 2 * RET_W:3 * RET_W] = (dvf[...].astype(F32) + dvb[...].astype(F32)).astype(BF16)
        dpb_ref[:, 3 * RET_W:4 * RET_W] = dg[...].astype(BF16)
        dpb_ref[:, 4 * RET_W:4 * RET_W + LRU_W] = dxr[...].astype(BF16)
        dpb_ref[:, 4 * RET_W + LRU_W:IN_COLS] = dgt[...].astype(BF16)
        dh = _dot_nt(dpb_ref[:, 0:nc], w_ref[0])
        for j in range(1, N_CHIP):
            dh = dh + _dot_nt(dpb_ref[:, nc * j:nc * (j + 1)], w_ref[j])
        dx, dgn_t, dsh_t, dsc_t = _norm_mod_bwd(x_ref[...], gn_ref[...], sc_ref[...], dh)
        dx_ref[...] = dres_ref[...] + dx

        @pl.when(pl.program_id(0) == 0)
        def _():
            dgn_ref[...] = jnp.zeros_like(dgn_ref)
            dsh_ref[...] = jnp.zeros_like(dsh_ref)
            dsc_ref[...] = jnp.zeros_like(dsc_ref)
        dgn_ref[...] += dgn_t
        dsh_ref[...] += dsh_t
        dsc_ref[...] += dsc_t

    row = _full((1, D_MODEL))
    pc = pl.BlockSpec((tm, RET_W), lambda i: (i, 0))
    big = pl.BlockSpec((tm, D_MODEL), lambda i: (i, 0))
    return _pc(body, name=name, grid=(t // tm,),
               in_specs=[big, row, row, row, _full(w4.shape),
                         pl.BlockSpec((tm, DH), lambda i: (i, 0)), pl.BlockSpec((tm, DH), lambda i: (i, 0))]
               + [pc] * 9 + [big],
               out_specs=[big, pl.BlockSpec((tm, IN_COLS), lambda i: (i, 0)), row, row, row],
               out_shape=[_sds((t, D_MODEL)), _sds((t, IN_COLS), BF16), _sds((1, D_MODEL)), _sds((1, D_MODEL)),
                          _sds((1, D_MODEL))],
               compiler_params=_params("arbitrary"))(x, gn, sh, sc, w4, cos2, sin2, *pieces, dres)


def _halo_specs(t, tm):
    n8 = tm // SUBLANES
    last8 = t // SUBLANES - 1
    prev = pl.BlockSpec((SUBLANES, LRU_W), lambda i: (jnp.maximum(i * n8 - 1, 0), 0))
    main = pl.BlockSpec((tm, LRU_W), lambda i: (i, 0))
    nxt = pl.BlockSpec((SUBLANES, LRU_W), lambda i: (jnp.minimum((i + 1) * n8, last8), 0))
    return prev, main, nxt


def _with_halo(prev_ref, main_ref, next_ref, i, nt):
    prev = jnp.where(i > 0, prev_ref[...], 0.0)
    nxt = jnp.where(i < nt - 1, next_ref[...], 0.0)
    return jnp.concatenate([prev, main_ref[...], nxt], axis=0)


def _conv_fwd(xr, cw, cb, name):
    t = xr.shape[0]
    tm = _tile(t, True)
    nt = t // tm
    n = tm + 2 * SUBLANES
    mid = slice(SUBLANES, SUBLANES + tm)

    def body(p_ref, m_ref, n_ref, w_ref, b_ref, o_ref):
        xp = _with_halo(p_ref, m_ref, n_ref, pl.program_id(0), nt)
        acc = b_ref[...] + pltpu.roll(xp, 1, 0)[mid] * w_ref[0:1, :]
        acc = acc + xp[mid] * w_ref[1:2, :]
        acc = acc + pltpu.roll(xp, n - 1, 0)[mid] * w_ref[2:3, :]
        acc = acc + pltpu.roll(xp, n - 2, 0)[mid] * w_ref[3:4, :]
        o_ref[...] = acc

    return _pc(body, name=name, grid=(nt,),
               in_specs=[*_halo_specs(t, tm), _full((4, LRU_W)), _full((1, LRU_W))],
               out_specs=pl.BlockSpec((tm, LRU_W), lambda i: (i, 0)), out_shape=_sds((t, LRU_W)),
               compiler_params=_params("arbitrary"))(xr, xr, xr, cw, cb)


def _conv_bwd(dxc_a, dxc_b, xr, cw, name):
    t = xr.shape[0]
    tm = _tile(t, True)
    nt = t // tm
    n = tm + 2 * SUBLANES
    mid = slice(SUBLANES, SUBLANES + tm)

    def body(ap_ref, am_ref, an_ref, bp_ref, bm_ref, bn_ref, xp_ref, xm_ref, xn_ref, w_ref, dx_ref, dw_ref, db_ref):
        i = pl.program_id(0)
        dp = _with_halo(ap_ref, am_ref, an_ref, i, nt) + _with_halo(bp_ref, bm_ref, bn_ref, i, nt)
        xp = _with_halo(xp_ref, xm_ref, xn_ref, i, nt)
        dx = pltpu.roll(dp, n - 1, 0)[mid] * w_ref[0:1, :]
        dx = dx + dp[mid] * w_ref[1:2, :]
        dx = dx + pltpu.roll(dp, 1, 0)[mid] * w_ref[2:3, :]
        dx = dx + pltpu.roll(dp, 2, 0)[mid] * w_ref[3:4, :]
        dx_ref[...] = dx.astype(BF16)
        d = dp[mid]

        @pl.when(i == 0)
        def _():
            dw_ref[...] = jnp.zeros_like(dw_ref)
            db_ref[...] = jnp.zeros_like(db_ref)
        dw_ref[0:1, :] += _sum0(d * pltpu.roll(xp, 1, 0)[mid])
        dw_ref[1:2, :] += _sum0(d * xp[mid])
        dw_ref[2:3, :] += _sum0(d * pltpu.roll(xp, n - 1, 0)[mid])
        dw_ref[3:4, :] += _sum0(d * pltpu.roll(xp, n - 2, 0)[mid])
        db_ref[...] += _sum0(d)

    return _pc(body, name=name, grid=(nt,),
               in_specs=[*_halo_specs(t, tm), *_halo_specs(t, tm), *_halo_specs(t, tm), _full((4, LRU_W))],
               out_specs=[pl.BlockSpec((tm, LRU_W), lambda i: (i, 0)), _full((4, LRU_W)), _full((1, LRU_W))],
               out_shape=[_sds((t, LRU_W), BF16), _sds((4, LRU_W)), _sds((1, LRU_W))],
               compiler_params=_params("arbitrary"))(dxc_a, dxc_a, dxc_a, dxc_b, dxc_b, dxc_b, xr, xr, xr, cw)


def _scan_scratch(n, c):
    return pltpu.VMEM((c // LANES, n, LANES), F32)


def _to_lane_blocks(ref, val):
    for lb in range(ref.shape[0]):
        ref[lb] = val[:, lb * LANES:(lb + 1) * LANES]


def _group_scan(a_s, b_s, reverse):
    nb, n, _ = a_s.shape
    ng = n // SUBLANES
    order = range(SUBLANES - 1, -1, -1) if reverse else range(SUBLANES)
    for lb in range(nb):
        prev = None
        for r in order:
            rows = pl.ds(r, ng, stride=SUBLANES)
            a_r, b_r = a_s[lb, rows, :], b_s[lb, rows, :]
            if prev is not None:
                b_r = a_r * prev[1] + b_r
                a_r = a_r * prev[0]
                a_s[lb, rows, :] = a_r
                b_s[lb, rows, :] = b_r
            prev = (a_r, b_r)


def _carry_scans(jobs):
    nb, n, _ = jobs[0][0].shape
    ng = n // SUBLANES

    def step(g, all_crs):
        res = []
        for (a_s, b_s, out_ref, _, reverse), crs in zip(jobs, all_crs):
            gg = (ng - 1 - g) if reverse else g
            off = pl.multiple_of(gg * SUBLANES, SUBLANES)
            new = []
            for lb in range(nb):
                h = a_s[lb, pl.ds(off, SUBLANES), :] * crs[lb] + b_s[lb, pl.ds(off, SUBLANES), :]
                out_ref[pl.ds(off, SUBLANES), pl.ds(lb * LANES, LANES)] = h
                edge = h[0:1, :] if reverse else h[SUBLANES - 1:SUBLANES, :]
                new.append(jnp.broadcast_to(edge, (SUBLANES, LANES)))
            res.append(tuple(new))
        return tuple(res)

    init = tuple(tuple(job[3][:, lb * LANES:(lb + 1) * LANES] for lb in range(nb)) for job in jobs)
    return [jnp.concatenate(crs, axis=1) for crs in lax.fori_loop(0, ng, step, init)]


def _lru_gates(xc, wa_ref, wx_ref, ba, bx, lam):
    xb = xc.astype(BF16)
    r = _sigmoid(_dot(xb, wa_ref[...]) + ba)
    ig = _sigmoid(_dot(xb, wx_ref[...]) + bx)
    sp = _softplus(-lam)
    la = -LRU_C * r * sp
    a = jnp.exp(la)
    mult = jnp.sqrt(_one_minus_sq(la, a))
    return r, ig, sp, a, mult


def _lru_fwd(xc, par_f, par_b, h0_f, h0_b, name, comms=()):
    t = xc.shape[0]
    tm = _tile(t, True)
    nt = t // tm

    def one(x_ref, prm, h0_ref, a_s, b_s, c_s, reverse):
        wa_ref, wx_ref, ba_ref, bx_ref, lam_ref = prm

        @pl.when(pl.program_id(0) == 0)
        def _():
            c_s[...] = jnp.broadcast_to(h0_ref[...], c_s.shape)
        xv = x_ref[...]
        _, ig, _, a, mult = _lru_gates(xv, wa_ref, wx_ref, ba_ref[...], bx_ref[...], lam_ref[...])
        _to_lane_blocks(a_s, a)
        _to_lane_blocks(b_s, mult * (ig * xv))
        _group_scan(a_s, b_s, reverse)

    def body(xf_ref, xb_ref, *refs):
        prm_f, prm_b = refs[0:5], refs[5:10]
        h0f_ref, h0b_ref, hf_ref, hb_ref = refs[10:14]
        af_s, bf_s, cf_s, ab_s, bb_s, cb_s = refs[14:]
        one(xf_ref, prm_f, h0f_ref, af_s, bf_s, cf_s, False)
        one(xb_ref, prm_b, h0b_ref, ab_s, bb_s, cb_s, True)
        cf_s[...], cb_s[...] = _carry_scans([(af_s, bf_s, hf_ref, cf_s[...], False),
                                             (ab_s, bb_s, hb_ref, cb_s[...], True)])

    vec = _full((1, LRU_W))
    mat = _full((LRU_W, LRU_W))
    fw = pl.BlockSpec((tm, LRU_W), lambda i: (i, 0))
    bw = pl.BlockSpec((tm, LRU_W), lambda i: (nt - 1 - i, 0))
    tile_s = [_scan_scratch(tm, LRU_W), _scan_scratch(tm, LRU_W), pltpu.VMEM((SUBLANES, LRU_W), F32)]
    (hf, hb), couts = _call(
        body, name=name, grid=(nt,),
        in_specs=[fw, bw] + [mat, mat, vec, vec, vec] * 2 + [vec, vec],
        out_specs=[pl.BlockSpec((tm, LRU_W), lambda i: (i, 0)), pl.BlockSpec((tm, LRU_W), lambda i: (nt - 1 - i, 0))],
        out_shape=[_sds((t, LRU_W))] * 2, scratch_shapes=tile_s + tile_s, sem=("arbitrary",),
        args=(xc, xc, *par_f, *par_b, h0_f, h0_b), comms=comms)
    return ((hf, hb), couts) if comms else (hf, hb)


def _lru_bwd(xc, par_f, par_b, h_f, h_b, h0_f, h0_b, dh_f, dh_b, name, comms=()):
    t = xc.shape[0]
    tm = _tile(t, True)
    nt = t // tm
    n8 = tm // SUBLANES
    last8 = t // SUBLANES - 1
    tile_f = lambda w: pl.BlockSpec((tm, w), lambda i: (nt - 1 - i, 0))
    tile_b = lambda w: pl.BlockSpec((tm, w), lambda i: (i, 0))
    halo_f = pl.BlockSpec((SUBLANES, LRU_W), lambda i: (jnp.maximum((nt - 1 - i) * n8 - 1, 0), 0))
    halo_b = pl.BlockSpec((SUBLANES, LRU_W), lambda i: (jnp.minimum((i + 1) * n8, last8), 0))

    def one(refs_in, refs_out, refs_scr, reverse):
        x_ref, wa_ref, wx_ref, ba_ref, bx_ref, lam_ref, h_ref, halo_ref, h0_ref, dh_ref = refs_in
        dx_ref, dpre_ref, dba_ref, dbx_ref, dlam_ref, dh0_ref = refs_out
        a_s, b_s, l_s, c_s, e_s = refs_scr
        i = pl.program_id(0)

        @pl.when(i == 0)
        def _():
            c_s[...] = jnp.zeros_like(c_s)
            e_s[...] = jnp.zeros_like(e_s)
            dba_ref[...] = jnp.zeros_like(dba_ref)
            dbx_ref[...] = jnp.zeros_like(dbx_ref)
            dlam_ref[...] = jnp.zeros_like(dlam_ref)
        xv = x_ref[...]
        lam = lam_ref[...]
        r, ig, sp, a, mult = _lru_gates(xv, wa_ref, wx_ref, ba_ref[...], bx_ref[...], lam)
        hv = h_ref[...]
        rowi = lax.broadcasted_iota(jnp.int32, (tm, LRU_W), 0)
        edge_a = jnp.broadcast_to(e_s[0:1, :], (tm, LRU_W))
        h0b = jnp.broadcast_to(h0_ref[...], (tm, LRU_W))
        if reverse:
            a_sh = jnp.where(rowi == 0, edge_a, pltpu.roll(a, 1, 0))
            hin_edge = jnp.where(i == nt - 1, h0b, jnp.broadcast_to(halo_ref[0:1, :], (tm, LRU_W)))
            h_in = jnp.where(rowi == tm - 1, hin_edge, pltpu.roll(hv, tm - 1, 0))
        else:
            a_sh = jnp.where(rowi == tm - 1, edge_a, pltpu.roll(a, tm - 1, 0))
            hin_edge = jnp.where(i == nt - 1, h0b, jnp.broadcast_to(halo_ref[SUBLANES - 1:SUBLANES, :], (tm, LRU_W)))
            h_in = jnp.where(rowi == 0, hin_edge, pltpu.roll(hv, 1, 0))
        _to_lane_blocks(a_s, a_sh)
        _to_lane_blocks(b_s, dh_ref[...])
        _group_scan(a_s, b_s, not reverse)
        e_s[...] = jnp.broadcast_to(a[tm - 1:tm, :] if reverse else a[0:1, :], e_s.shape)
        return lam, r, ig, sp, a, mult, h_in

    def post(vals, refs_in, refs_out, refs_scr, reverse):
        lam, r, ig, sp, a, mult, h_in = vals
        xv = refs_in[0][...]
        wa_ref, wx_ref = refs_in[1:3]
        dx_ref, dpre_ref, dba_ref, dbx_ref, dlam_ref, dh0_ref = refs_out
        l_s = refs_scr[2]
        i = pl.program_id(0)
        lmb = l_s[...]
        da = lmb * h_in
        ixc = ig * xv
        dmult = lmb * ixc
        dixc = lmb * mult
        dla = da * a - dmult * (a * a) / mult
        dpr = dla * (-LRU_C * sp) * r * (1.0 - r)
        dpi = dixc * xv * ig * (1.0 - ig)
        dprb = dpr.astype(BF16)
        dpib = dpi.astype(BF16)
        dpre_ref[:, 0:LRU_W] = dprb
        dpre_ref[:, LRU_W:2 * LRU_W] = dpib
        dx_ref[...] = dixc * ig + _dot_nt(dprb, wa_ref[...]) + _dot_nt(dpib, wx_ref[...])
        dba_ref[...] += _sum0(dpr)
        dbx_ref[...] += _sum0(dpi)
        dlam_ref[...] += _sum0(dla * (-LRU_C * r)) * (-_sigmoid(-lam))

        @pl.when(i == nt - 1)
        def _():
            al0 = a * lmb
            dh0_ref[...] = al0[tm - 1:tm, :] if reverse else al0[0:1, :]

    def body(*refs):
        jobs = ((refs[0:10], refs[20:26], refs[32:37], False), (refs[10:20], refs[26:32], refs[37:42], True))
        vals = [one(*job) for job in jobs]
        carries = _carry_scans([(scr[0], scr[1], scr[2], scr[3][...], not rev) for _, _, scr, rev in jobs])
        for (_, _, scr, _), carry in zip(jobs, carries):
            scr[3][...] = carry
        for v, job in zip(vals, jobs):
            post(v, *job)

    vec = _full((1, LRU_W))
    mat = _full((LRU_W, LRU_W))

    def in_specs(tile, halo):
        return [tile(LRU_W), mat, mat, vec, vec, vec, tile(LRU_W), halo, vec, tile(LRU_W)]

    def out_specs(tile):
        return [tile(LRU_W), tile(2 * LRU_W), vec, vec, vec, vec]

    out_one = [_sds((t, LRU_W)), _sds((t, 2 * LRU_W), BF16)] + [_sds((1, LRU_W))] * 4
    scr_one = ([_scan_scratch(tm, LRU_W)] * 2 + [pltpu.VMEM((tm, LRU_W), F32)]
               + [pltpu.VMEM((SUBLANES, LRU_W), F32)] * 2)
    outs, couts = _call(
        body, name=name, grid=(nt,), in_specs=in_specs(tile_f, halo_f) + in_specs(tile_b, halo_b),
        out_specs=out_specs(tile_f) + out_specs(tile_b), out_shape=out_one + out_one,
        scratch_shapes=scr_one + scr_one, sem=("arbitrary",),
        args=(xc, *par_f, h_f, h_f, h0_f, dh_f, xc, *par_b, h_b, h_b, h0_b, dh_b), comms=comms)
    return (tuple(outs[0:6]), tuple(outs[6:12])), couts


def _decay_tables(lg, reverse):
    ci = lax.broadcasted_iota(jnp.int32, (CHUNK, CHUNK), 0).astype(F32)
    mi = lax.broadcasted_iota(jnp.int32, (CHUNK, CHUNK), 1).astype(F32)
    rel = (mi - ci) if reverse else (ci - mi)
    relc = jnp.maximum(rel, 0.0)
    lg_c = jnp.concatenate([lg] * (CHUNK // LANES), axis=1)
    dm = jnp.where(rel >= 0, jnp.exp(lg_c * relc), 0.0)
    cd = lax.broadcasted_iota(jnp.int32, (CHUNK, DH), 0).astype(F32)
    pq, ps = (CHUNK - cd, cd) if reverse else (cd + 1.0, CHUNK - 1.0 - cd)
    return relc, dm, jnp.exp(lg * pq), jnp.exp(lg * ps), jnp.exp(lg * float(CHUNK)), pq, ps


def _ret_fwd(proj, lgv, s0f, s0b, comms=()):
    t = proj.shape[0]
    n = t // CHUNK

    def one(q, k, v, lg, s_s, hh, o_ref, sp_ref, reverse):
        _, dm, wq, ws, g, _, _ = _decay_tables(lg, reverse)
        vb = v.astype(BF16)
        p = _dot_nt(q.astype(BF16), k.astype(BF16)) * dm
        s = s_s[hh]
        sp_ref[hh, 0] = s
        o_ref[:, DH * hh:DH * (hh + 1)] = _dot(p.astype(BF16), vb) + _dot((q * wq).astype(BF16), s.astype(BF16))
        s_s[hh] = g * s + _dot_tn((k * ws).astype(BF16), vb)

    def body(qf, kf, vf, qb, kb, vb, lg_ref, s0f_ref, s0b_ref, of_ref, ob_ref, spf_ref, spb_ref, sf_s, sb_s):
        @pl.when(pl.program_id(0) == 0)
        def _():
            sf_s[...] = s0f_ref[...]
            sb_s[...] = s0b_ref[...]
        for hh in range(HEADS):
            sl = slice(DH * hh, DH * (hh + 1))
            one(qf[:, sl].astype(F32), kf[:, sl].astype(F32), vf[:, sl], lg_ref[hh, 0:1, :], sf_s, hh, of_ref, spf_ref,
                False)
            one(qb[:, sl].astype(F32), kb[:, sl].astype(F32), vb[:, sl], lg_ref[hh, 1:2, :], sb_s, hh, ob_ref, spb_ref,
                True)

    blk = (CHUNK, RET_W)
    fw = [pl.BlockSpec(blk, lambda i, o=o: (i, o)) for o in range(3)]
    bw = [pl.BlockSpec(blk, lambda i, o=o: (n - 1 - i, o)) for o in range(3)]
    st = _full((HEADS, DH, DH))
    return _call(body, name="ret_fwd", grid=(n,),
                 in_specs=fw + bw + [_full((HEADS, 2, LANES)), st, st],
                 out_specs=[pl.BlockSpec(blk, lambda i: (i, 0)), pl.BlockSpec(blk, lambda i: (n - 1 - i, 0)),
                            pl.BlockSpec((HEADS, 1, DH, DH), lambda i: (0, i, 0, 0)),
                            pl.BlockSpec((HEADS, 1, DH, DH), lambda i: (0, n - 1 - i, 0, 0))],
                 out_shape=[_sds((t, RET_W)), _sds((t, RET_W)), _sds((HEADS, n, DH, DH)), _sds((HEADS, n, DH, DH))],
                 scratch_shapes=[pltpu.VMEM((HEADS, DH, DH), F32), pltpu.VMEM((HEADS, DH, DH), F32)],
                 sem=("arbitrary",), args=(proj, proj, proj, proj, proj, proj, lgv, s0f, s0b), comms=comms)


def _ret_bwd(proj, lgv, sgv, spf, spb, do, comms=()):
    t = proj.shape[0]
    n = t // CHUNK

    def one(q_ref, k_ref, v_ref, lg_ref, s_ref, do_ref, dq_ref, dk_ref, dv_ref, ds_s, acc_s, reverse):
        d = 1 if reverse else 0
        for hh in range(HEADS):
            sl = slice(DH * hh, DH * (hh + 1))
            relc, dm, wq, ws, g, pq, ps = _decay_tables(lg_ref[hh, d:d + 1, :], reverse)
            qb, kb, vb = q_ref[:, sl], k_ref[:, sl], v_ref[:, sl]
            q, k = qb.astype(F32), kb.astype(F32)
            p = _dot_nt(qb, kb) * dm
            s = s_ref[hh, 0]
            dob = do_ref[:, sl].astype(BF16)
            dsn = ds_s[hh]
            dsb = dsn.astype(BF16)
            dv_ref[:, sl] = (_dot_tn(p.astype(BF16), dob) + _dot((k * ws).astype(BF16), dsb)).astype(BF16)
            dp = _dot_nt(dob, vb)
            dab = (dp * dm).astype(BF16)
            xq = _dot_nt(dob, s.astype(BF16))
            yk = _dot_nt(vb, dsb)
            dq_ref[:, sl] = (_dot(dab, kb) + xq * wq).astype(BF16)
            dk_ref[:, sl] = (_dot_tn(dab, qb) + yk * ws).astype(BF16)
            ds_s[hh] = g * dsn + _dot_tn((q * wq).astype(BF16), dob)
            s_mask = _sum0(dp * p * relc)
            part = (sum(s_mask[:, LANES * u:LANES * (u + 1)] for u in range(CHUNK // LANES))
                    + _sum0(xq * q * wq * pq) + _sum0(yk * k * ws * ps) + _sum0(dsn * s) * g * float(CHUNK))
            acc_s[hh] += jnp.broadcast_to(part, (SUBLANES, LANES))

    def body(qf, kf, vf, qb, kb, vb, lg_ref, sg_ref, sf_ref, sb_ref, dof_ref, dob_ref,
             dqf, dkf, dvf, dqb, dkb, dvb, ds0f_ref, ds0b_ref, drdf_ref, drdb_ref, dsf_s, dsb_s, accf_s, accb_s):
        i = pl.program_id(0)

        @pl.when(i == 0)
        def _():
            for r in (dsf_s, dsb_s, accf_s, accb_s):
                r[...] = jnp.zeros_like(r)
        one(qf, kf, vf, lg_ref, sf_ref, dof_ref, dqf, dkf, dvf, dsf_s, accf_s, False)
        one(qb, kb, vb, lg_ref, sb_ref, dob_ref, dqb, dkb, dvb, dsb_s, accb_s, True)

        @pl.when(i == n - 1)
        def _():
            ds0f_ref[...] = dsf_s[...]
            ds0b_ref[...] = dsb_s[...]
            for d, (acc_s, drd_ref) in enumerate(((accf_s, drdf_ref), (accb_s, drdb_ref))):
                for hh in range(HEADS):
                    tot = jnp.sum(acc_s[hh, 0:1, :], axis=1, keepdims=True)
                    drd_ref[hh] = jnp.broadcast_to(tot, (SUBLANES, LANES)) * sg_ref[hh, d:d + 1, :]

    blk = (CHUNK, RET_W)
    fw = lambda o: pl.BlockSpec(blk, lambda i, o=o: (n - 1 - i, o))
    bw = lambda o: pl.BlockSpec(blk, lambda i, o=o: (i, o))
    lane = _full((HEADS, 2, LANES))
    st = _full((HEADS, DH, DH))
    rd = _full((HEADS, SUBLANES, LANES))
    outs, couts = _call(
        body, name="ret_bwd", grid=(n,),
        in_specs=[fw(0), fw(1), fw(2), bw(0), bw(1), bw(2), lane, lane,
                  pl.BlockSpec((HEADS, 1, DH, DH), lambda i: (0, n - 1 - i, 0, 0)),
                  pl.BlockSpec((HEADS, 1, DH, DH), lambda i: (0, i, 0, 0)), fw(0), bw(0)],
        out_specs=[fw(0), fw(0), fw(0), bw(0), bw(0), bw(0), st, st, rd, rd],
        out_shape=[_sds((t, RET_W), BF16)] * 6 + [_sds((HEADS, DH, DH))] * 2 + [_sds((HEADS, SUBLANES, LANES))] * 2,
        scratch_shapes=[pltpu.VMEM((HEADS, DH, DH), F32)] * 2 + [pltpu.VMEM((HEADS, SUBLANES, LANES), F32)] * 2,
        sem=("arbitrary",), args=(proj, proj, proj, proj, proj, proj, lgv, sgv, spf, spb, do, do), comms=comms)
    dqf, dkf, dvf, dqb, dkb, dvb, ds0f, ds0b, drdf, drdb = outs
    return ((dqf, dkf, dvf, ds0f, drdf), (dqb, dkb, dvb, ds0b, drdb)), couts


def _ctx_weights(lg, l_len, reverse):
    pos = lax.broadcasted_iota(jnp.int32, (l_len, DH), 0).astype(F32)
    steps = pos if reverse else (l_len - 1.0 - pos)
    return jnp.exp(lg * steps), steps


def _ctx_state_fwd(projc, lgv):
    l_len = projc.shape[0]

    def body(k_ref, v_ref, lg_ref, sf_ref, sb_ref):
        k = k_ref[...]
        vb = v_ref[...].astype(BF16)
        for d, o_ref in ((0, sf_ref), (1, sb_ref)):
            w, _ = _ctx_weights(lg_ref[0, d:d + 1, :], l_len, d == 1)
            o_ref[0] = _dot_tn((k * w).astype(BF16), vb)

    st = pl.BlockSpec((1, DH, DH), lambda h: (h, 0, 0))
    return _pc(body, name="ctx_state_fwd", grid=(HEADS,),
               in_specs=[pl.BlockSpec((l_len, DH), lambda h: (0, HEADS + h)),
                         pl.BlockSpec((l_len, DH), lambda h: (0, 2 * HEADS + h)),
                         pl.BlockSpec((1, 2, LANES), lambda h: (h, 0, 0))],
               out_specs=[st, st], out_shape=[_sds((HEADS, DH, DH))] * 2,
               compiler_params=_params("arbitrary"))(projc, projc, lgv)


def _ctx_state_bwd(projc, lgv, sgv, dsf, dsb):
    l_len = projc.shape[0]

    def body(k_ref, v_ref, lg_ref, sg_ref, dsf_ref, dsb_ref, dk_ref, dv_ref, drd_ref):
        k = k_ref[...]
        vb = v_ref[...].astype(BF16)
        dk = jnp.zeros((l_len, DH), F32)
        dv = jnp.zeros((l_len, DH), F32)
        rows = []
        for d, ds_ref in ((0, dsf_ref), (1, dsb_ref)):
            w, steps = _ctx_weights(lg_ref[0, d:d + 1, :], l_len, d == 1)
            dsb16 = ds_ref[0].astype(BF16)
            dkw = _dot_nt(vb, dsb16)
            dk = dk + dkw * w
            dv = dv + _dot((k * w).astype(BF16), dsb16)
            tot = jnp.sum(_sum0(dkw * k * w * steps), axis=1, keepdims=True)
            rows.append(jnp.broadcast_to(tot, (1, LANES)) * sg_ref[0, d:d + 1, :])
        dk_ref[...] = dk.astype(BF16)
        dv_ref[...] = dv.astype(BF16)
        rid = lax.broadcasted_iota(jnp.int32, (SUBLANES, LANES), 0)
        drd_ref[0] = jnp.where(rid == 0, rows[0], jnp.where(rid == 1, rows[1], 0.0))

    st = pl.BlockSpec((1, DH, DH), lambda h: (h, 0, 0))
    lane = pl.BlockSpec((1, 2, LANES), lambda h: (h, 0, 0))
    hc = pl.BlockSpec((l_len, DH), lambda h: (0, h))
    return _pc(body, name="ctx_state_bwd", grid=(HEADS,),
               in_specs=[pl.BlockSpec((l_len, DH), lambda h: (0, HEADS + h)),
                         pl.BlockSpec((l_len, DH), lambda h: (0, 2 * HEADS + h)), lane, lane, st, st],
               out_specs=[hc, hc, pl.BlockSpec((1, SUBLANES, LANES), lambda h: (h, 0, 0))],
               out_shape=[_sds((l_len, RET_W), BF16), _sds((l_len, RET_W), BF16), _sds((HEADS, SUBLANES, LANES))],
               compiler_params=_params("arbitrary"))(projc, projc, lgv, sgv, dsf, dsb)


G_BLOCK = (3 * RET_W) // RET_W
GATE_BLOCK = (4 * RET_W + LRU_W) // LRU_W


def _head_norm(y):
    yc = y - jnp.mean(y, axis=-1, keepdims=True)
    rs = lax.rsqrt(jnp.mean(yc * yc, axis=-1, keepdims=True) + EPS)
    return yc * rs, rs


def _gelu_parts(z):
    th = jnp.tanh(GELU_K * (z + GELU_C * z * z * z))
    return 0.5 * z * (1.0 + th), th


def _mix_fwd(o_f, o_b, proj, hf, hb, w_out, x, g1, comms):
    t = x.shape[0]
    tm = _tile(t, True)

    def body(of_ref, ob_ref, g_ref, gt_ref, hf_ref, hb_ref, w_ref, x_ref, g1_ref, x1_ref, cat_ref):
        o = of_ref[...] + ob_ref[...]
        g = g_ref[...].astype(F32)
        for hh in range(HEADS):
            sl = slice(DH * hh, DH * (hh + 1))
            nrm, _ = _head_norm(o[:, sl])
            gh = g[:, sl]
            cat_ref[:, sl] = (gh * _sigmoid(gh) * nrm).astype(BF16)
        gel, _ = _gelu_parts(gt_ref[...].astype(F32))
        cat_ref[:, RET_W:] = ((hf_ref[...] + hb_ref[...]) * gel).astype(BF16)
        x1_ref[...] = x_ref[...] + g1_ref[...] * _dot(cat_ref[...], w_ref[...])

    half = pl.BlockSpec((tm, RET_W), lambda i: (i, 0))
    big = pl.BlockSpec((tm, D_MODEL), lambda i: (i, 0))
    return _call(body, name="mix_fwd", grid=(t // tm,),
                 in_specs=[half, half, pl.BlockSpec((tm, RET_W), lambda i: (i, G_BLOCK)),
                           pl.BlockSpec((tm, LRU_W), lambda i: (i, GATE_BLOCK)), half, half,
                           _full((D_MODEL, D_MODEL)), big, _full((1, D_MODEL))],
                 out_specs=[big, big], out_shape=[_sds((t, D_MODEL)), _sds((t, D_MODEL), BF16)],
                 scratch_shapes=[], sem=("arbitrary",), args=(o_f, o_b, proj, proj, hf, hb, w_out, x, g1),
                 comms=comms)


def _mix_bwd(o_f, o_b, proj, hf, hb, w_out, cat, dx1, g1, comms=()):
    t = dx1.shape[0]
    tm = _tile(t, True)

    def body(of_ref, ob_ref, g_ref, gt_ref, hf_ref, hb_ref, w_ref, cat_ref, dx1_ref, g1_ref,
             do_ref, dhs_ref, dg_ref, dgt_ref, dyb_ref, dg1_ref):
        dx1v = dx1_ref[...]
        y = _dot(cat_ref[...], w_ref[...])

        @pl.when(pl.program_id(0) == 0)
        def _():
            dg1_ref[...] = jnp.zeros_like(dg1_ref)
        dg1_ref[...] += _sum0(dx1v * y)
        dyb = (g1_ref[...] * dx1v).astype(BF16)
        dyb_ref[...] = dyb
        dcat = _dot_nt(dyb, w_ref[...])
        o = of_ref[...] + ob_ref[...]
        g = g_ref[...].astype(F32)
        for hh in range(HEADS):
            sl = slice(DH * hh, DH * (hh + 1))
            nrm, rs = _head_norm(o[:, sl])
            gh = g[:, sl]
            sg = _sigmoid(gh)
            dret = dcat[:, sl]
            dg_ref[:, sl] = (dret * nrm * (sg * (1.0 + gh * (1.0 - sg)))).astype(BF16)
            dn = dret * (gh * sg)
            dyc = rs * (dn - nrm * jnp.mean(dn * nrm, axis=-1, keepdims=True))
            do_ref[:, sl] = (dyc - jnp.mean(dyc, axis=-1, keepdims=True)).astype(BF16)
        z = gt_ref[...].astype(F32)
        gel, th = _gelu_parts(z)
        dlru = dcat[:, RET_W:]
        dhs_ref[...] = dlru * gel
        dgel = 0.5 * (1.0 + th) + 0.5 * z * (1.0 - th * th) * GELU_K * (1.0 + 3.0 * GELU_C * z * z)
        dgt_ref[...] = (dlru * (hf_ref[...] + hb_ref[...]) * dgel).astype(BF16)

    half = pl.BlockSpec((tm, RET_W), lambda i: (i, 0))
    big = pl.BlockSpec((tm, D_MODEL), lambda i: (i, 0))
    return _call(body, name="mix_bwd", grid=(t // tm,),
                 in_specs=[half, half, pl.BlockSpec((tm, RET_W), lambda i: (i, G_BLOCK)),
                           pl.BlockSpec((tm, LRU_W), lambda i: (i, GATE_BLOCK)), half, half,
                           _full((D_MODEL, D_MODEL)), big, big, _full((1, D_MODEL))],
                 out_specs=[half, half, half, half, big, _full((1, D_MODEL))],
                 out_shape=[_sds((t, RET_W), BF16), _sds((t, RET_W)), _sds((t, RET_W), BF16), _sds((t, RET_W), BF16),
                            _sds((t, D_MODEL), BF16), _sds((1, D_MODEL))],
                 scratch_shapes=[], sem=("arbitrary",), args=(o_f, o_b, proj, proj, hf, hb, w_out, cat, dx1, g1),
                 comms=comms)


def _mlp(x1, n2g, sh2, sc2, g2, fg, w1_parts, w2_parts, tgt):
    t = x1.shape[0]
    tm = _tile(t)
    hb_ = MLP_H // N_CHIP
    q_rows = hb_ // 4
    n_cp = 4 * N_DEV

    def body(x1_ref, n2g_ref, sh2_ref, sc2_ref, g2_ref, fg_ref, w1a, w1b, w2a, w2b, tgt_ref,
             dx1_ref, h2b_ref, ab_ref, dub_ref, dmb_ref, dsc_ref, dsh_ref, dg2_ref, dn2_ref, dfg_ref, loss_ref,
             w1_s, w2_s, r_s, sems):
        @pl.when(pl.program_id(0) == 0)
        def _():
            cps = []
            for p, parts in enumerate(((w1a, w2a), (w1b, w2b))):
                for d in range(N_DEV):
                    rows = pl.ds(2 * q_rows * (d % 2) + q_rows * p, q_rows)
                    for src, dst in zip(parts, (w1_s, w2_s)):
                        cps.append(pltpu.make_async_copy(src.at[d], dst.at[d // 2, rows], sems.at[len(cps)]))
            for cp in cps:
                cp.start()
            for r in (dsc_ref, dsh_ref, dg2_ref, dn2_ref, dfg_ref, loss_ref):
                r[...] = jnp.zeros_like(r)
            for cp in cps:
                cp.wait()
        x1v = x1_ref[...]
        n2g, sc2, g2, fg = n2g_ref[...], sc2_ref[...], g2_ref[...], fg_ref[...]
        xh, _ = _rms(x1v)
        h2b = (xh * n2g * (1.0 + sc2) + sh2_ref[...]).astype(BF16)
        h2b_ref[...] = h2b
        m = jnp.zeros((tm, D_MODEL), F32)
        for j in range(N_CHIP):
            sl = slice(hb_ * j, hb_ * (j + 1))
            r = jnp.maximum(_dot(h2b, w1_s[j]), 0.0)
            r_s[:, sl] = r
            ab = (r * r).astype(BF16)
            ab_ref[:, sl] = ab
            m = m + _dot(ab, w2_s[j])
        x2 = x1v + g2 * m
        x2h, r2 = _rms(x2)
        err = x2h * fg - tgt_ref[...]
        loss_ref[...] += _sum0(err * err)
        dout = err * (1.0 / D_MODEL)
        dfg_ref[...] += _sum0(dout * x2h)
        dxh = dout * fg
        dx2 = r2 * (dxh - x2h * jnp.mean(dxh * x2h, axis=-1, keepdims=True))
        dg2_ref[...] += _sum0(dx2 * m)
        dmb = (g2 * dx2).astype(BF16)
        dmb_ref[...] = dmb
        dh2 = jnp.zeros((tm, D_MODEL), F32)
        for j in range(N_CHIP):
            sl = slice(hb_ * j, hb_ * (j + 1))
            dub = (_dot_nt(dmb, w2_s[j]) * (2.0 * r_s[:, sl])).astype(BF16)
            dub_ref[:, sl] = dub
            dh2 = dh2 + _dot_nt(dub, w1_s[j])
        dx, dn2_t, dsh_t, dsc_t = _norm_mod_bwd(x1v, n2g, sc2, dh2)
        dx1_ref[...] = dx2 + dx
        dn2_ref[...] += dn2_t
        dsh_ref[...] += dsh_t
        dsc_ref[...] += dsc_t

        @pl.when(pl.program_id(0) == t // tm - 1)
        def _():
            tot = jnp.sum(loss_ref[...], axis=1, keepdims=True) * (0.5 / D_MODEL)
            loss_ref[...] = jnp.broadcast_to(tot, loss_ref.shape)

    row = _full((1, D_MODEL))
    big = pl.BlockSpec((tm, D_MODEL), lambda i: (i, 0))
    wide = pl.BlockSpec((tm, MLP_H), lambda i: (i, 0))
    return _pc(body, name="mlp", grid=(t // tm,),
               in_specs=[big, row, row, row, row, row, ANY, ANY, ANY, ANY, big],
               out_specs=[big, big, wide, wide, big, row, row, row, row, row, row],
               out_shape=[_sds((t, D_MODEL)), _sds((t, D_MODEL), BF16), _sds((t, MLP_H), BF16), _sds((t, MLP_H), BF16),
                          _sds((t, D_MODEL), BF16)] + [_sds((1, D_MODEL))] * 6,
               scratch_shapes=[pltpu.VMEM((N_CHIP, D_MODEL, hb_), BF16), pltpu.VMEM((N_CHIP, hb_, D_MODEL), BF16),
                               pltpu.VMEM((tm, MLP_H), F32), pltpu.SemaphoreType.DMA((n_cp,))],
               compiler_params=_params("arbitrary"))(x1, n2g, sh2, sc2, g2, fg, *w1_parts, *w2_parts, tgt)


def _tn(a, b, nj, a_blocked, b_blocked, name, extra=None, comms=()):
    t = a.shape[0]
    m = a.shape[1] // (nj if a_blocked else 1)
    n = b.shape[1] // (nj if b_blocked else 1)
    bk = next((b for b in (2048, 1024, 512) if t % b == 0), t)
    nk = t // bk
    a_col = (lambda j: j) if a_blocked else (lambda j: 0)
    b_col = (lambda j: j) if b_blocked else (lambda j: 0)
    in_specs = [pl.BlockSpec((bk, m), lambda j, k: (k, a_col(j))), pl.BlockSpec((bk, n), lambda j, k: (k, b_col(j)))]
    args = [a, b]
    if extra is not None:
        a2, b2 = extra
        t2 = a2.shape[0]
        in_specs += [pl.BlockSpec((t2, m), lambda j, k: (0, a_col(j))),
                     pl.BlockSpec((t2, n), lambda j, k: (0, b_col(j)))]
        args += [a2, b2]

    def body(*refs):
        a_ref, b_ref = refs[0], refs[1]
        o_ref, acc = refs[-2], refs[-1]
        k = pl.program_id(1)

        @pl.when(k == 0)
        def _():
            acc[...] = jnp.zeros_like(acc)
        acc[...] += _dot_tn(a_ref[...].astype(BF16), b_ref[...].astype(BF16))

        @pl.when(k == nk - 1)
        def _():
            if extra is not None:
                acc[...] += _dot_tn(refs[2][...].astype(BF16), refs[3][...].astype(BF16))
            o_ref[0] = acc[...]

    (out,), couts = _call(body, name=name, grid=(nj, nk), in_specs=in_specs,
                          out_specs=[pl.BlockSpec((1, m, n), lambda j, k: (j, 0, 0))], out_shape=[_sds((nj, m, n))],
                          scratch_shapes=[pltpu.VMEM((m, n), F32)], sem=("arbitrary", "arbitrary"), args=args,
                          comms=comms)
    return (out, couts) if comms else out


ROW_LOSS = 0
ROW_DMOD = 1
ROW_DMODC = 7
ROW_N1, ROW_N2, ROW_FG, ROW_CB = 9, 10, 11, 12
ROW_BA, ROW_BX, ROW_LAM = 13, 15, 17
ROW_CW = 20
ROW_RD = 24
SLAB_ROWS = 32
SEG = D_MODEL // 2


def _pack_small(rows, drd, cw2, cb2, lru2, gates):
    n_rows, n_lru = len(rows), len(lru2)

    def body(*refs):
        r = refs[:n_rows]
        drd_f, drd_b, drd_c, cw_a, cw_b, cb_a, cb_b = refs[n_rows:n_rows + 7]
        lru = refs[n_rows + 7:n_rows + 7 + n_lru]
        gf_ref, gb_ref, slab, ga, gx = refs[n_rows + 7 + n_lru:]
        slab[...] = jnp.zeros_like(slab)
        slab[ROW_LOSS:ROW_LOSS + 1, :] = r[0][...]
        for k in range(N_MOD):
            slab[ROW_DMOD + k:ROW_DMOD + k + 1, :] = r[1 + k][...]
        slab[ROW_DMODC:ROW_DMODC + 1, :] = r[7][...]
        slab[ROW_DMODC + 1:ROW_DMODC + 2, :] = r[8][...]
        slab[ROW_N1:ROW_N1 + 1, :] = r[9][...] + r[10][...]
        slab[ROW_N2:ROW_N2 + 1, :] = r[11][...]
        slab[ROW_FG:ROW_FG + 1, :] = r[12][...]
        slab[ROW_CB:ROW_CB + 1, 0:LRU_W] = cb_a[...] + cb_b[...]
        for k, row in enumerate((ROW_BA, ROW_BA + 1, ROW_BX, ROW_BX + 1, ROW_LAM, ROW_LAM + 1)):
            slab[row:row + 1, 0:LRU_W] = lru[2 * k][...] + lru[2 * k + 1][...]
        slab[ROW_CW:ROW_CW + 4, 0:LRU_W] = cw_a[...] + cw_b[...]
        for h in range(HEADS):
            slab[ROW_RD + h:ROW_RD + h + 1, 0:LANES] = drd_f[h, 0:1, :] + drd_c[h, 0:1, :]
            slab[ROW_RD + HEADS + h:ROW_RD + HEADS + h + 1, 0:LANES] = drd_b[h, 0:1, :] + drd_c[h, 1:2, :]
        for d, g_ref in enumerate((gf_ref, gb_ref)):
            for n in range(LRU_BLOCKS):
                blk = slice(LRU_BD * n, LRU_BD * (n + 1))
                ga[blk, LRU_BD * d:LRU_BD * (d + 1)] = g_ref[0, blk, blk].astype(BF16)
                gx[blk, LRU_BD * d:LRU_BD * (d + 1)] = g_ref[1, blk, blk].astype(BF16)

    args = list(rows) + list(drd) + list(cw2) + list(cb2) + list(lru2) + list(gates)
    gate_shape = (LRU_W, 2 * LRU_BD)
    return _pc(body, name="pack_small", in_specs=[_full(a.shape) for a in args],
               out_specs=[_full((SLAB_ROWS, D_MODEL)), _full(gate_shape), _full(gate_shape)],
               out_shape=[_sds((SLAB_ROWS, D_MODEL)), _sds(gate_shape, BF16), _sds(gate_shape, BF16)],
               compiler_params=_params())(*args)


def _adam_math(w, g, m, v):
    mn = ADAM_B1 * m + (1.0 - ADAM_B1) * g
    vn = ADAM_B2 * v + (1.0 - ADAM_B2) * (g * g)
    mh = mn / (1.0 - ADAM_B1 ** ADAM_STEP)
    vh = vn / (1.0 - ADAM_B2 ** ADAM_STEP)
    return -ADAM_LR * (mh / (jnp.sqrt(vh) + ADAM_EPS) + ADAM_WD * w), mn, vn


SMALL_PARAMS = ("b_ada", "norm1_g", "norm2_g", "final_g", "ret_decay", "conv_w", "conv_b", "lru_wa", "lru_ba", "lru_wx",
                "lru_bx", "lru_lambda")


def _finalize_small(chip_idx, slab_all, ga_all, gx_all, wmv):
    n_p = len(SMALL_PARAMS)
    flat = [a for nm in SMALL_PARAMS for a in wmv[nm]]
    ada_n = N_MOD * D_MODEL // N_CHIP

    def body(c_ref, slab_ref, ga_ref, gx_ref, *refs):
        prm = {nm: refs[3 * k:3 * k + 3] for k, nm in enumerate(SMALL_PARAMS)}
        outs = {nm: refs[3 * n_p + 4 * k:3 * n_p + 4 * k + 4] for k, nm in enumerate(SMALL_PARAMS)}
        b128_ref, dmc_ref, loss_ref = refs[3 * n_p + 4 * n_p:]
        chip = c_ref[0]

        def pick(fn):
            acc = fn(0)
            for j in range(1, N_CHIP):
                acc = jnp.where(chip == j, fn(j), acc)
            return acc

        tot = slab_ref[0]
        for d in range(1, N_DEV):
            tot = tot + slab_ref[d]

        def update(nm, g, sl=None, rows=None):
            w_ref, m_ref, v_ref = prm[nm]
            g_ref, d_ref, mo_ref, vo_ref = outs[nm]
            ix = (slice(None) if rows is None else rows, slice(None) if sl is None else sl)
            dl, mn, vn = _adam_math(w_ref[ix], g, m_ref[ix], v_ref[ix])
            g_ref[ix] = g
            d_ref[ix] = dl
            mo_ref[ix] = mn
            vo_ref[ix] = vn

        loss_ref[...] = jnp.broadcast_to(tot[ROW_LOSS:ROW_LOSS + 1, 0:LANES], (SUBLANES, LANES))
        for k in range(N_MOD):
            g = tot[ROW_DMOD + k:ROW_DMOD + k + 1, :]
            if k < 2:
                g = g + tot[ROW_DMODC + k:ROW_DMODC + k + 1, :]
            update("b_ada", g, slice(D_MODEL * k, D_MODEL * (k + 1)))
        update("norm1_g", tot[ROW_N1:ROW_N1 + 1, :])
        update("norm2_g", tot[ROW_N2:ROW_N2 + 1, :])
        update("final_g", tot[ROW_FG:ROW_FG + 1, :])
        update("ret_decay", tot[ROW_RD:ROW_RD + SUBLANES, 0:LANES])
        update("conv_b", tot[ROW_CB:ROW_CB + 1, 0:LRU_W])
        update("conv_w", pick(lambda j: tot[ROW_CW:ROW_CW + 4, LANES * j:LANES * (j + 1)]))
        for nm, row in (("lru_ba", ROW_BA), ("lru_bx", ROW_BX), ("lru_lambda", ROW_LAM)):
            update(nm, pick(lambda j, row=row: tot[row:row + 2, LANES * j:LANES * (j + 1)]))
        for nm, g_all in (("lru_wa", ga_ref), ("lru_wx", gx_ref)):
            for dr in range(2):
                lanes = slice(LRU_BD * dr, LRU_BD * (dr + 1))
                g = g_all[0, :, lanes].astype(F32)
                for d in range(1, N_DEV):
                    g = g + g_all[d, :, lanes].astype(F32)
                update(nm, g, rows=slice(LRU_W * dr, LRU_W * (dr + 1)))

        def seg(rows6, s):
            return rows6[s // 2][:, SEG * (s % 2):SEG * (s % 2 + 1)]

        b128_ref[...] = jnp.zeros_like(b128_ref)
        dmc_ref[...] = jnp.zeros_like(dmc_ref)
        zero = jnp.zeros((1, D_MODEL), F32)
        ctx6 = [tot[ROW_DMODC:ROW_DMODC + 1, :], tot[ROW_DMODC + 1:ROW_DMODC + 2, :]] + [zero] * (N_MOD - 2)
        for q in range(ada_n // SEG):
            cols = slice(SEG * q, SEG * (q + 1))
            for d in range(N_DEV):
                rows6 = [slab_ref[d, ROW_DMOD + k:ROW_DMOD + k + 1, :] for k in range(N_MOD)]
                b128_ref[d:d + 1, cols] = pick(lambda j, rows6=rows6: seg(rows6, 3 * j + q))
            c = pick(lambda j: seg(ctx6, 3 * j + q))
            b128_ref[N_DEV:N_DEV + 1, cols] = c
            dmc_ref[0:1, cols] = c

    out_shape = []
    for nm in SMALL_PARAMS:
        out_shape += [_sds(wmv[nm][0].shape)] * 4
    out_shape += [_sds((LANES, ada_n)), _sds((SUBLANES, ada_n)), _sds((SUBLANES, LANES))]
    args = [slab_all, ga_all, gx_all] + flat
    grid_spec = pltpu.PrefetchScalarGridSpec(
        num_scalar_prefetch=1, grid=(1,), in_specs=[_full(a.shape) for a in args],
        out_specs=[_full(s.shape) for s in out_shape])
    outs = _pc(body, name="finalize_small", grid_spec=grid_spec, out_shape=out_shape,
               compiler_params=_params("arbitrary"))(chip_idx, *args)
    res = {nm: tuple(outs[4 * k:4 * k + 4]) for k, nm in enumerate(SMALL_PARAMS)}
    return res, outs[4 * n_p], outs[4 * n_p + 1], outs[4 * n_p + 2]


def _block_diag(w):
    eye = jnp.eye(LRU_BLOCKS, dtype=F32)
    return (w[:, :, None, :] * eye[:, None, :, None]).reshape(LRU_W, LRU_W).astype(BF16)


def _lane_rep(v8):
    return jnp.broadcast_to(v8.reshape(SUBLANES, 1), (SUBLANES, LANES))


def kernel(x, c, ctx, c_ctx, w_ada, b_ada, norm1_g, norm2_g, w_in, ret_decay, conv_w, conv_b, lru_wa, lru_ba, lru_wx, lru_bx, lru_lambda, w_out, w_mlp1, w_mlp2, final_g, loss_target, m_c_ctx, m_w_ada, m_b_ada, m_norm1_g, m_norm2_g, m_w_in, m_ret_decay, m_conv_w, m_conv_b, m_lru_wa, m_lru_ba, m_lru_wx, m_lru_bx, m_lru_lambda, m_w_out, m_w_mlp1, m_w_mlp2, m_final_g, v_c_ctx, v_w_ada, v_b_ada, v_norm1_g, v_norm2_g, v_w_in, v_ret_decay, v_conv_w, v_conv_b, v_lru_wa, v_lru_ba, v_lru_wx, v_lru_bx, v_lru_lambda, v_w_out, v_w_mlp1, v_w_mlp2, v_final_g):
    ax, ay, ac = lax.axis_index("x"), lax.axis_index("y"), lax.axis_index("c")
    chip = 2 * ax + ay
    dev = 4 * ax + 2 * ay + ac
    c_idx = jnp.stack([ac, chip]).astype(jnp.int32)
    j_idx = chip.reshape(1).astype(jnp.int32)

    xt = x[0]
    t_len = xt.shape[0]
    ctxt = ctx[0]
    l_len = ctxt.shape[0]
    tgt = loss_target[0]
    ada_n = w_ada.shape[2]

    def my_half(w2d):
        r = w2d.shape[0] // 2
        return lax.dynamic_slice_in_dim(w2d, ac * r, r, axis=0).astype(BF16)

    pad8 = lambda a: jnp.pad(a, ((0, SUBLANES - a.shape[0]), (0, 0)))
    small = jnp.concatenate([pad8(conv_w[0]), pad8(lru_ba[0]), pad8(lru_bx[0]), pad8(lru_lambda[0])], axis=0)
    b_shard = lax.dynamic_slice_in_dim(b_ada, chip * ada_n, ada_n, axis=1)
    gw_in, _, small_all, a16, mod_parts, lgv, sgv = _head(
        my_half(w_in[0]), pad8(c), small, w_ada[0], b_shard, c_ctx, ret_decay[0])
    w4 = gw_in.reshape(N_CHIP, D_MODEL, IN_COLS // N_CHIP)

    mod_all = mod_parts[0::2].transpose(1, 0, 2).reshape(16, N_CHIP * ada_n)
    mod_me = lax.dynamic_slice_in_dim(mod_all, dev, 1, axis=0)
    sh1, sc1, g1, sh2, sc2, g2 = [mod_me[:, D_MODEL * k:D_MODEL * (k + 1)] for k in range(N_MOD)]
    csh1, csc1 = mod_all[8:9, 0:D_MODEL], mod_all[8:9, D_MODEL:2 * D_MODEL]

    cos2, sin2 = _rotary_tables(t_len)
    cos_c, sin_c = jnp.ones((l_len, DH), F32), jnp.zeros((l_len, DH), F32)
    n1g, n2g = norm1_g, norm2_g
    fg = final_g.reshape(1, D_MODEL)

    small_full = small_all[0::2].transpose(1, 0, 2).reshape(4 * SUBLANES, LRU_W)
    cw = small_full[0:4]
    cb = conv_b
    ba_f, ba_b = small_full[8:9], small_full[9:10]
    bx_f, bx_b = small_full[16:17], small_full[17:18]
    lam_f, lam_b = small_full[24:25], small_full[25:26]
    wa_f, wa_b = _block_diag(lru_wa[0, 0]), _block_diag(lru_wa[0, 1])
    wx_f, wx_b = _block_diag(lru_wx[0, 0]), _block_diag(lru_wx[0, 1])
    zero_h = jnp.zeros((1, LRU_W), F32)

    projc, xrc, hcb16 = _inproj_fwd(ctxt, n1g, csh1, csc1, w4, cos_c, sin_c, "inproj_fwd_ctx")
    s_f, s_b = _ctx_state_fwd(projc, lgv)
    xcc = _conv_fwd(xrc, cw, cb, "conv_fwd_ctx")
    par_f, par_b = (wa_f, wx_f, ba_f, bx_f, lam_f), (wa_b, wx_b, ba_b, bx_b, lam_b)
    hcf, hcbk = _lru_fwd(xcc, par_f, par_b, zero_h, zero_h, "lru_fwd_ctx")
    lru_sf, lru_sb = hcf[l_len - 1:l_len], hcbk[0:1]

    h1, h2 = my_half(w_mlp1[0]), my_half(w_mlp2[0])
    q = h1.shape[0] // 2
    (proj, xrl, hb16), ((gw_1a,),) = _inproj_fwd(xt, n1g, sh1, sc1, w4, cos2, sin2, "inproj_fwd",
                                           comms=(_AllGather([h1[:q]]),))
    (o_f, o_b, spf, spb), ((gw_1b, gw_out),) = _ret_fwd(proj, lgv, s_f, s_b,
                                                       comms=(_AllGather([h1[q:], my_half(w_out[0])]),))
    xcl = _conv_fwd(xrl, cw, cb, "conv_fwd")
    (hf, hbk), ((gw_2a,),) = _lru_fwd(xcl, par_f, par_b, lru_sf, lru_sb, "lru_fwd", comms=(_AllGather([h2[:q]]),))
    wo = gw_out.reshape(D_MODEL, D_MODEL)
    (x1, cat), ((gw_2b,),) = _mix_fwd(o_f, o_b, proj, hf, hbk, wo, xt, g1, comms=(_AllGather([h2[q:]]),))

    (dx1, h2b, ab, dub, dmb, dsc2, dsh2, dg2, dn2g, dfg, lossv) = _mlp(
        x1, n2g, sh2, sc2, g2, fg, (gw_1a, gw_1b), (gw_2a, gw_2b), tgt)
    gw_mlp1 = _tn(h2b, dub, N_CHIP, False, True, "grad_w_mlp1")
    b_1 = gw_mlp1.reshape(N_DEV, D_MODEL // 2, MLP_H // N_CHIP)
    gw_mlp2, ((r_1,),) = _tn(ab, dmb, N_CHIP, True, False, "grad_w_mlp2", comms=(_pair_exchange([b_1]),))

    half = D_MODEL // 4
    top, bot = (0, half), (half, half)
    b_2 = gw_mlp2.reshape(N_DEV, MLP_H // N_DEV, D_MODEL)
    p_1, pb_1 = _pair_add(b_1, r_1, c_idx, "rs_pair_add_w_mlp1")
    (do, dhs, dg, dgate, dyb, dg1), ((q_1a,), (r_2,)) = _mix_bwd(
        o_f, o_b, proj, hf, hbk, wo, cat, dx1, g1, comms=(_chip_exchange([pb_1], top), _pair_exchange([b_2])))
    gw_o = _tn(cat, dyb, 1, False, False, "grad_w_out")
    b_o = gw_o.reshape(N_DEV, D_MODEL // N_DEV, D_MODEL)
    p_2, pb_2 = _pair_add(b_2, r_2, c_idx, "rs_pair_add_w_mlp2")

    ((dq_f, dk_f, dv_f, ds_f, drd_f), (dq_b, dk_b, dv_b, ds_b, drd_b)), ((q_1b,), (q_2a,), (r_o,)) = _ret_bwd(
        proj, lgv, sgv, spf, spb, do,
        comms=(_chip_exchange([pb_1], bot), _chip_exchange([pb_2], top), _pair_exchange([b_o])))
    p_o, pb_o = _pair_add(b_o, r_o, c_idx, "rs_pair_add_w_out")
    h_1 = _chip_add(p_1, (q_1a, q_1b), c_idx, "rs_chip_add_w_mlp1")

    ((dxc_f, dpre_f, dba_f, dbx_f, dlam_f, dh0_f), (dxc_b, dpre_b, dba_b, dbx_b, dlam_b, dh0_b)), (
        (q_2b,), (q_o,), (f_1,)) = _lru_bwd(
        xcl, par_f, par_b, hf, hbk, lru_sf, lru_sb, dhs, dhs, "lru_bwd",
        comms=(_chip_exchange([pb_2], bot), _chip_exchange([pb_o]), _pair_gather([h_1])))
    h_2 = _chip_add(p_2, (q_2a, q_2b), c_idx, "rs_chip_add_w_mlp2")
    h_o = _chip_add(p_o, (q_o,), c_idx, "rs_chip_add_w_out")
    dxr, dcw, dcb = _conv_bwd(dxc_f, dxc_b, xrl, cw, "conv_bwd")
    grad_x, dpb, dn1g, dsh1, dsc1 = _inproj_bwd(
        xt, n1g, sh1, sc1, w4, cos2, sin2, [dq_f, dq_b, dk_f, dk_b, dv_f, dv_b, dg, dxr, dgate], dx1, "inproj_bwd")

    dkc, dvc, drd_c = _ctx_state_bwd(projc, lgv, sgv, ds_f, ds_b)
    zc = jnp.zeros((l_len, LRU_W), F32)
    dhc_f = lax.dynamic_update_slice(zc, dh0_f, (l_len - 1, 0))
    dhc_b = lax.dynamic_update_slice(zc, dh0_b, (0, 0))
    ((dxcc_f, dprec_f, dbac_f, dbxc_f, dlamc_f, _), (dxcc_b, dprec_b, dbac_b, dbxc_b, dlamc_b, _)), _ = _lru_bwd(
        xcc, par_f, par_b, hcf, hcbk, zero_h, zero_h, dhc_f, dhc_b, "lru_bwd_ctx")
    dxrc, dcw_c, dcb_c = _conv_bwd(dxcc_f, dxcc_b, xrc, cw, "conv_bwd_ctx")
    zr = jnp.zeros((l_len, RET_W), BF16)
    _, dpbc, dn1g_c, dcsh1, dcsc1 = _inproj_bwd(
        ctxt, n1g, csh1, csc1, w4, cos_c, sin_c, [zr, zr, dkc, zr, dvc, zr, zr, dxrc, zr],
        jnp.zeros((l_len, D_MODEL), F32), "inproj_bwd_ctx")

    gw_i = _tn(hb16, dpb, N_CHIP, False, True, "grad_w_in", extra=(hcb16, dpbc))
    b_i = gw_i.reshape(N_DEV, D_MODEL // 2, IN_COLS // N_CHIP)
    gwa_f, ((r_i,), (f_2,), (f_o,)) = _tn(xcl, dpre_f, 2, False, True, "grad_lru_gates_f", extra=(xcc, dprec_f),
                                          comms=(_pair_exchange([b_i]), _pair_gather([h_2]), _pair_gather([h_o])))
    p_i, pb_i = _pair_add(b_i, r_i, c_idx, "rs_pair_add_w_in")
    gwa_b, ((q_i,),) = _tn(xcl, dpre_b, 2, False, True, "grad_lru_gates_b", extra=(xcc, dprec_b),
                           comms=(_chip_exchange([pb_i]),))
    slab, ga, gx = _pack_small(
        [lossv, dsh1, dsc1, dg1, dsh2, dsc2, dg2, dcsh1, dcsc1, dn1g, dn1g_c, dn2g, dfg],
        (drd_f, drd_b, drd_c), (dcw, dcw_c), (dcb, dcb_c),
        (dba_f, dbac_f, dba_b, dbac_b, dbx_f, dbxc_f, dbx_b, dbxc_b, dlam_f, dlamc_f, dlam_b, dlamc_b),
        (gwa_f, gwa_b))
    (f_i,), (slab_all, ga_all, gx_all) = _run_comms(
        [_pair_gather([_chip_add(p_i, (q_i,), c_idx, "rs_chip_add_w_in")]), _AllGather([slab, ga, gx])],
        "tail_exchanges")
    g_in, g_out, g_1, g_2 = _shard_of(f_i), _shard_of(f_o), _shard_of(f_1), _shard_of(f_2)
    big = {}
    for nm, w, g, m, v in (("w_in", w_in, g_in, m_w_in, v_w_in), ("w_out", w_out, g_out, m_w_out, v_w_out),
                           ("w_mlp1", w_mlp1, g_1, m_w_mlp1, v_w_mlp1), ("w_mlp2", w_mlp2, g_2, m_w_mlp2, v_w_mlp2)):
        go, d_, mn, vn = _adamw(w[0], g, m[0], v[0], "adamw_" + nm)
        big[nm] = (go[None], d_[None], mn[None], vn[None])
    params = {
        "b_ada": (b_ada, m_b_ada, v_b_ada), "norm1_g": (norm1_g, m_norm1_g, v_norm1_g),
        "norm2_g": (norm2_g, m_norm2_g, v_norm2_g), "final_g": (final_g, m_final_g, v_final_g),
        "ret_decay": (ret_decay, m_ret_decay, v_ret_decay), "conv_w": (conv_w, m_conv_w, v_conv_w),
        "conv_b": (conv_b, m_conv_b, v_conv_b), "lru_wa": (lru_wa, m_lru_wa, v_lru_wa),
        "lru_ba": (lru_ba, m_lru_ba, v_lru_ba), "lru_wx": (lru_wx, m_lru_wx, v_lru_wx),
        "lru_bx": (lru_bx, m_lru_bx, v_lru_bx), "lru_lambda": (lru_lambda, m_lru_lambda, v_lru_lambda),
    }
    as2d = {
        "b_ada": lambda a: a, "norm1_g": lambda a: a, "norm2_g": lambda a: a, "conv_b": lambda a: a,
        "final_g": lambda a: a.reshape(1, D_MODEL), "ret_decay": lambda a: _lane_rep(a.reshape(-1)),
        "conv_w": lambda a: a[0], "lru_ba": lambda a: a[0], "lru_bx": lambda a: a[0], "lru_lambda": lambda a: a[0],
        "lru_wa": lambda a: a.reshape(2 * LRU_W, LRU_BD), "lru_wx": lambda a: a.reshape(2 * LRU_W, LRU_BD),
    }
    res, b128, dmc8, loss8 = _finalize_small(
        j_idx, slab_all, ga_all, gx_all, {nm: tuple(as2d[nm](a) for a in params[nm]) for nm in SMALL_PARAMS})
    loss = loss8[0, 0]
    small_out = {}
    for nm in SMALL_PARAMS:
        shp = params[nm][0].shape
        if nm == "ret_decay":
            small_out[nm] = tuple(o[:, 0].reshape(shp) for o in res[nm])
        else:
            small_out[nm] = tuple(o.reshape(shp) for o in res[nm])

    g_ada = _ada_grad(jnp.pad(a16.T, ((0, 0), (0, LANES - 16))), b128)
    g_ada, d_ada, m_ada, v_ada = _adamw(w_ada[0], g_ada, m_w_ada[0], v_w_ada[0], "adamw_w_ada")

    (cparts,) = _all_gather([_cctx_partial(dmc8, w_ada[0])], "gather_cctx")
    g_cc, d_cc, m_cc, v_cc = _cctx_final(cparts, c_ctx, m_c_ctx, v_c_ctx)
    small_out["c_ctx"] = tuple(a.reshape(D_MODEL) for a in (g_cc, d_cc, m_cc, v_cc))
    small_out["w_ada"] = (g_ada[None], d_ada[None], m_ada[None], v_ada[None])
    small_out.update(big)

    order = ["c_ctx", "w_ada", "b_ada", "norm1_g", "norm2_g", "w_in", "ret_decay", "conv_w", "conv_b", "lru_wa", "lru_ba",
             "lru_wx", "lru_bx", "lru_lambda", "w_out", "w_mlp1", "w_mlp2", "final_g"]
    outs = [loss, grad_x[None]]
    for k in range(4):
        outs += [small_out[nm][k] for nm in order]
    return tuple(outs)
```

```python
import math

import jax
import jax.numpy as jnp
from jax import lax
from jax.experimental import pallas as pl
from jax.experimental.pallas import tpu as pltpu

F32 = jnp.float32
BF16 = jnp.bfloat16

D_MODEL = 1024
HEADS = 4
DH = 128
CHUNK = 256
RET_W = HEADS * DH
LRU_W = 512
LRU_BLOCKS = 8
LRU_BD = LRU_W // LRU_BLOCKS
LRU_C = 8.0
IN_COLS = 4 * RET_W + 2 * LRU_W
MLP_H = 4 * D_MODEL
N_MOD = 6
GRID_W = 64
ROPE_BASE = 10000.0
K_SCALE = DH ** -0.5
EPS = 1e-6
GELU_K = math.sqrt(2.0 / math.pi)
GELU_C = 0.044715

ADAM_LR = 0.001
ADAM_B1 = 0.9
ADAM_B2 = 0.999
ADAM_EPS = 1e-08
ADAM_WD = 0.01
ADAM_STEP = 10

N_DEV = 8
N_CHIP = 4
SUBLANES = 8
LANES = 128
VMEM_LIMIT_V7X = 56 * 1024 * 1024
MESH = pl.DeviceIdType.MESH
ANY = pl.BlockSpec(memory_space=pl.ANY)


def _pc(body, **kw):
    return pl.pallas_call(body, **kw)


def _params(*sem):
    return pltpu.CompilerParams(dimension_semantics=sem if sem else None, vmem_limit_bytes=VMEM_LIMIT_V7X)


def _tile(t, big=False):
    if big and t >= 1024:
        return 512
    return 256 if t >= 256 else t


def _sds(shape, dtype=F32):
    return jax.ShapeDtypeStruct(tuple(shape), dtype)


def _full(shape):
    nd = len(shape)
    return pl.BlockSpec(tuple(shape), lambda *_: (0,) * nd)


def _sigmoid(x):
    return 0.5 * jnp.tanh(0.5 * x) + 0.5


def _log1p_pos(y):
    s = y * (1.0 - y * (0.5 - y * (1.0 / 3.0 - y * (0.25 - y * (0.2 - y / 6.0)))))
    return jnp.where(y < 0.03, s, jnp.log(1.0 + y))


def _softplus(z):
    return jnp.maximum(z, 0.0) + _log1p_pos(jnp.exp(-jnp.abs(z)))


def _one_minus_sq(la, a):
    return -jnp.tanh(la) * (1.0 + a * a)


def _rms(x):
    r = lax.rsqrt(jnp.mean(x * x, axis=-1, keepdims=True) + EPS)
    return x * r, r


def _dot(a, b):
    return jnp.dot(a, b, preferred_element_type=F32)


def _dot_nt(a, b):
    return lax.dot_general(a, b, (((1,), (1,)), ((), ())), preferred_element_type=F32)


def _dot_tn(a, b):
    return lax.dot_general(a, b, (((0,), (0,)), ((), ())), preferred_element_type=F32)


def _sum0(x):
    return jnp.sum(x, axis=0, keepdims=True)


def _norm_mod_bwd(x, g, sc, dh):
    xh, r = _rms(x)
    hn = xh * g
    dhn = dh * (1.0 + sc)
    dxh = dhn * g
    dx = r * (dxh - xh * jnp.mean(dxh * xh, axis=-1, keepdims=True))
    return dx, _sum0(dhn * xh), _sum0(dh), _sum0(dh * hn)


def _dev_index(p):
    return 4 * p[0] + 2 * p[1] + p[2]


def _mesh_pos():
    return lax.axis_index("x"), lax.axis_index("y"), lax.axis_index("c")


class _AllGather:
    def __init__(self, arrs):
        n = len(arrs)
        self.arrays = list(arrs)
        self.out_shapes = [_sds((N_DEV,) + a.shape, a.dtype) for a in arrs]
        self.scratch = ([pltpu.VMEM(a.shape, a.dtype) for a in arrs]
                        + [pltpu.SemaphoreType.DMA((7 * n,)), pltpu.SemaphoreType.DMA((7 * n,)),
                           pltpu.SemaphoreType.DMA((n,))])
        self.aliases = {}

    def _parts(self, ins, outs, scr):
        n = len(self.arrays)
        stage = scr[:n]
        send_sems, recv_sems, local_sems = scr[n:]
        x, y, c = _mesh_pos()
        me, sib = (x, y, c), (x, y, 1 - c)
        chips = [(1 - x, y), (x, 1 - y), (1 - x, 1 - y)]

        def copy(t, k, block, to, own=False):
            dst = outs[t].at[_dev_index(block)]
            return pltpu.make_async_remote_copy(
                src_ref=ins[t] if own else dst, dst_ref=dst,
                send_sem=send_sems.at[7 * t + k], recv_sem=recv_sems.at[7 * t + k],
                device_id=to, device_id_type=MESH)

        first = []
        for t in range(n):
            first.append(copy(t, 0, me, sib, own=True))
            for j, ch in enumerate(chips):
                first.append(copy(t, 1 + j, me, (*ch, c), own=True))
        stage_in = [pltpu.make_async_copy(ins[t], stage[t], local_sems.at[t]) for t in range(n)]
        mine = [pltpu.make_async_copy(stage[t], outs[t].at[_dev_index(me)], local_sems.at[t]) for t in range(n)]
        return n, c, me, sib, chips, copy, first, stage_in, mine

    def start(self, ins, outs, scr):
        n, _, _, _, _, _, first, stage_in, mine = self._parts(ins, outs, scr)
        for cp in stage_in:
            cp.start()
        for cp in first:
            cp.start()
        for t in range(n):
            stage_in[t].wait()
            mine[t].start()

    def relay(self, ins, outs, scr):
        n, c, me, sib, chips, copy, _, _, _ = self._parts(ins, outs, scr)
        for j, ch in enumerate(chips):
            for t in range(n):
                copy(t, 1 + j, (*ch, c), me).wait_recv()
                copy(t, 4 + j, (*ch, c), sib).start()

    def finish(self, ins, outs, scr):
        n, c, me, sib, chips, copy, first, _, mine = self._parts(ins, outs, scr)
        passed = [copy(t, 4 + j, (*ch, c), sib) for j, ch in enumerate(chips) for t in range(n)]
        for t in range(n):
            copy(t, 0, sib, me).wait_recv()
            for j, ch in enumerate(chips):
                copy(t, 4 + j, (*ch, 1 - c), me).wait_recv()
        for cp in first + passed:
            cp.wait_send()
        for cp in mine:
            cp.wait()


class _Exchange:
    def __init__(self, arrays, out_shapes, plan, n_copies, aliases=None):
        self.arrays = list(arrays)
        self.out_shapes = list(out_shapes)
        self.plan = plan
        self.scratch = [pltpu.SemaphoreType.DMA((n_copies,)), pltpu.SemaphoreType.DMA((n_copies,))]
        self.aliases = aliases or {}

    def _copies(self, ins, outs, scr):
        send_sems, recv_sems = scr
        snd, rcv = [], []
        for i, (src, dst, peer, lands) in enumerate(self.plan(ins, outs, _mesh_pos())):
            kw = dict(send_sem=send_sems.at[i], recv_sem=recv_sems.at[i], device_id=peer, device_id_type=MESH)
            snd.append(pltpu.make_async_remote_copy(src_ref=src, dst_ref=dst, **kw))
            rcv.append(pltpu.make_async_remote_copy(src_ref=src, dst_ref=lands, **kw))
        return snd, rcv

    def start(self, ins, outs, scr):
        for cp in self._copies(ins, outs, scr)[0]:
            cp.start()

    def relay(self, ins, outs, scr):
        pass

    def finish(self, ins, outs, scr):
        snd, rcv = self._copies(ins, outs, scr)
        for cp in rcv:
            cp.wait_recv()
        for cp in snd:
            cp.wait_send()


def _pair_exchange(grads):
    n = len(grads)

    def plan(ins, outs, pos):
        x, y, c = pos
        return [(ins[t].at[2 * j + (1 - c)], outs[t].at[j], (x, y, 1 - c), outs[t].at[j])
                for t in range(n) for j in range(N_CHIP)]

    return _Exchange(grads, [_sds((N_CHIP,) + g.shape[1:], g.dtype) for g in grads], plan, N_CHIP * n)


def _chip_exchange(parts, rows=None):
    n = len(parts)

    def plan(ins, outs, pos):
        x, y, c = pos
        chips = [(1 - x, y), (x, 1 - y), (1 - x, 1 - y)]

        def src(t, ch):
            blk = ins[t].at[2 * ch[0] + ch[1]]
            return blk if rows is None else blk.at[pl.ds(rows[0], rows[1])]

        return [(src(t, ch), outs[t].at[k], (*ch, c), outs[t].at[k]) for t in range(n) for k, ch in enumerate(chips)]

    shapes = [_sds((3, p.shape[1] if rows is None else rows[1]) + p.shape[2:], p.dtype) for p in parts]
    return _Exchange(parts, shapes, plan, 3 * n)


def _pair_gather(bufs):
    n = len(bufs)

    def plan(ins, outs, pos):
        x, y, c = pos
        return [(ins[t].at[c], outs[t].at[c], (x, y, 1 - c), outs[t].at[1 - c]) for t in range(n)]

    return _Exchange(bufs, [_sds(b.shape, b.dtype) for b in bufs], plan, n, aliases={t: t for t in range(n)})


def _run_comms(comms, name):
    c_in = [len(cm.arrays) for cm in comms]
    c_out = [len(cm.out_shapes) for cm in comms]
    c_scr = [len(cm.scratch) for cm in comms]
    aliases = {}
    for k, cm in enumerate(comms):
        for a, b in cm.aliases.items():
            aliases[sum(c_in[:k]) + a] = sum(c_out[:k]) + b

    def split(refs, counts):
        out, pos = [], 0
        for cnt in counts:
            out.append(refs[pos:pos + cnt])
            pos += cnt
        return out

    def body(*refs):
        ins = split(refs[:sum(c_in)], c_in)
        outs = split(refs[sum(c_in):sum(c_in) + sum(c_out)], c_out)
        scr = split(refs[sum(c_in) + sum(c_out):], c_scr)
        for phase in ("start", "relay", "finish"):
            for k, cm in enumerate(comms):
                getattr(cm, phase)(ins[k], outs[k], scr[k])

    outs = _pc(body, name=name, out_shape=[s for cm in comms for s in cm.out_shapes],
               in_specs=[ANY] * sum(c_in), out_specs=[ANY] * sum(c_out), input_output_aliases=aliases,
               scratch_shapes=[s for cm in comms for s in cm.scratch],
               compiler_params=_params())(*[a for cm in comms for a in cm.arrays])
    return split(list(outs), c_out)


def _all_gather(arrs, name):
    return _run_comms([_AllGather(arrs)], name)[0]


def _call(body, *, name, grid, in_specs, out_specs, out_shape, scratch_shapes, sem, args, comms=()):
    n_in, n_out, n_scr = len(in_specs), len(out_specs), len(scratch_shapes)
    c_in = [len(cm.arrays) for cm in comms]
    c_out = [len(cm.out_shapes) for cm in comms]
    c_scr = [len(cm.scratch) for cm in comms]
    aliases = {}
    for k, cm in enumerate(comms):
        for a, b in cm.aliases.items():
            aliases[n_in + sum(c_in[:k]) + a] = n_out + sum(c_out[:k]) + b

    def split(refs, counts):
        out, pos = [], 0
        for cnt in counts:
            out.append(refs[pos:pos + cnt])
            pos += cnt
        return out

    def wrapped(*refs):
        ins = refs[:n_in + sum(c_in)]
        outs = refs[len(ins):len(ins) + n_out + sum(c_out)]
        scr = refs[len(ins) + len(outs):]
        cins, couts, cscr = split(ins[n_in:], c_in), split(outs[n_out:], c_out), split(scr[n_scr:], c_scr)
        if comms:
            first = pl.program_id(0) == 0
            last = pl.program_id(0) == grid[0] - 1
            for k in range(1, len(grid)):
                first = jnp.logical_and(first, pl.program_id(k) == 0)
                last = jnp.logical_and(last, pl.program_id(k) == grid[k] - 1)

            @pl.when(first)
            def _():
                for k, cm in enumerate(comms):
                    cm.start(cins[k], couts[k], cscr[k])
        body(*ins[:n_in], *outs[:n_out], *scr[:n_scr])
        if comms:
            relay_early = len(grid) == 1 and grid[0] >= 4
            if relay_early:
                @pl.when(pl.program_id(0) == (7 * grid[0]) // 8 - 1)
                def _():
                    for k, cm in enumerate(comms):
                        cm.relay(cins[k], couts[k], cscr[k])

            @pl.when(last)
            def _():
                for k, cm in enumerate(comms):
                    if not relay_early:
                        cm.relay(cins[k], couts[k], cscr[k])
                    cm.finish(cins[k], couts[k], cscr[k])

    outs = _pc(wrapped, name=name, grid=grid,
               in_specs=list(in_specs) + [ANY] * sum(c_in), out_specs=list(out_specs) + [ANY] * sum(c_out),
               out_shape=list(out_shape) + [s for cm in comms for s in cm.out_shapes],
               scratch_shapes=list(scratch_shapes) + [s for cm in comms for s in cm.scratch],
               input_output_aliases=aliases, compiler_params=_params(*sem),
               )(*args, *[a for cm in comms for a in cm.arrays])
    outs = list(outs)
    return outs[:n_out], split(outs[n_out:], c_out)


def _row_block(r):
    for b in (512, 256, 128, 64, 32, 16, 8):
        if r % b == 0:
            return b
    return r


def _pair_add(g, recv, cj_idx, name):
    _, r, cc = g.shape
    br = _row_block(r)

    def body(cj_ref, g_ref, r_ref, own_ref, pb_ref):
        s = g_ref[...] + r_ref[...]
        pb_ref[...] = s.astype(BF16)

        @pl.when(pl.program_id(1) == cj_ref[1])
        def _():
            own_ref[...] = s[0]

    grid_spec = pltpu.PrefetchScalarGridSpec(
        num_scalar_prefetch=1, grid=(r // br, N_CHIP),
        in_specs=[pl.BlockSpec((1, br, cc), lambda i, j, cj_ref: (2 * j + cj_ref[0], i, 0)),
                  pl.BlockSpec((1, br, cc), lambda i, j, cj_ref: (j, i, 0))],
        out_specs=[pl.BlockSpec((br, cc), lambda i, j, cj_ref: (i, 0)),
                   pl.BlockSpec((1, br, cc), lambda i, j, cj_ref: (j, i, 0))])
    return _pc(body, name=name, grid_spec=grid_spec,
               out_shape=[_sds((r, cc)), _sds((N_CHIP, r, cc), BF16)],
               compiler_params=_params("arbitrary", "arbitrary"))(cj_idx, g, recv)


def _chip_add(p, qs, cj_idx, name):
    r, cc = p.shape
    nq = len(qs)
    br = _row_block(r // nq)
    nb = r // nq // br

    def body(cj_ref, p_ref, *refs):
        o_ref = refs[-1]
        if nq == 2:
            top = pl.program_id(0) < nb
            q = [jnp.where(top, refs[0][k], refs[1][k]).astype(F32) for k in range(3)]
        else:
            q = [refs[0][k].astype(F32) for k in range(3)]
        o_ref[0] = ((p_ref[...] + q[0]) + q[1]) + q[2]

    q_specs = [pl.BlockSpec((3, br, cc), lambda i, cj_ref, h=h: (0, jnp.clip(i - h * nb, 0, nb - 1), 0))
               for h in range(nq)]
    grid_spec = pltpu.PrefetchScalarGridSpec(
        num_scalar_prefetch=1, grid=(r // br,),
        in_specs=[pl.BlockSpec((br, cc), lambda i, cj_ref: (i, 0))] + q_specs,
        out_specs=pl.BlockSpec((1, br, cc), lambda i, cj_ref: (cj_ref[0], i, 0)))
    return _pc(body, name=name, grid_spec=grid_spec, out_shape=_sds((2, r, cc)),
               compiler_params=_params("arbitrary"))(cj_idx, p, *qs)


def _shard_of(both):
    return both.reshape((2 * both.shape[1],) + both.shape[2:])


ADAMW_CHUNKS = 4


def _adamw(w, g, m, v, name):
    r, cc = w.shape
    rows = r // ADAMW_CHUNKS
    assert rows * ADAMW_CHUNKS == r and rows % SUBLANES == 0
    c1 = 1.0 - ADAM_B1 ** ADAM_STEP
    c2 = 1.0 - ADAM_B2 ** ADAM_STEP

    def body(w_hbm, g_hbm, m_hbm, v_hbm, go_hbm, d_hbm, mo_hbm, vo_hbm, wb, gb, mb, vb, sem_in, sem_out):
        srcs, bufs, dsts = (w_hbm, g_hbm, m_hbm, v_hbm), (wb, gb, mb, vb), (d_hbm, go_hbm, mo_hbm, vo_hbm)

        def load(a, k):
            sl = pl.ds(k * rows, rows)
            return pltpu.make_async_copy(srcs[a].at[sl], bufs[a].at[sl], sem_in.at[a, k])

        def store(a, k):
            sl = pl.ds(k * rows, rows)
            return pltpu.make_async_copy(bufs[a].at[sl], dsts[a].at[sl], sem_out.at[a, k])

        for k in range(ADAMW_CHUNKS):
            for a in range(4):
                load(a, k).start()
        for k in range(ADAMW_CHUNKS):
            for a in range(4):
                load(a, k).wait()
            store(1, k).start()
            sl = pl.ds(k * rows, rows)
            gg = gb[sl]
            mn = ADAM_B1 * mb[sl] + (1.0 - ADAM_B1) * gg
            vn = ADAM_B2 * vb[sl] + (1.0 - ADAM_B2) * (gg * gg)
            mh = mn / c1
            vh = vn / c2
            wb[sl] = -ADAM_LR * (mh / (jnp.sqrt(vh) + ADAM_EPS) + ADAM_WD * wb[sl])
            mb[sl] = mn
            vb[sl] = vn
            for a in (0, 2, 3):
                store(a, k).start()
        for k in range(ADAMW_CHUNKS):
            for a in range(4):
                store(a, k).wait()

    go, d, mo, vo = _pc(body, name=name, in_specs=[ANY] * 4, out_specs=[ANY] * 4, out_shape=[_sds((r, cc))] * 4,
                        scratch_shapes=[pltpu.VMEM((r, cc), F32)] * 4 + [pltpu.SemaphoreType.DMA((4, ADAMW_CHUNKS))] * 2,
                        compiler_params=_params())(w, g, m, v)
    return go, d, mo, vo


def _head(w_half, c8, small, w_ada, b_shard, c_ctx, ret_decay):
    ada_n = w_ada.shape[1]
    mod_sds = _sds((16, ada_n))
    ag_w, ag_c, ag_m = _AllGather([w_half]), _AllGather([c8, small]), _AllGather([mod_sds])
    n_w, n_c, n_m = len(ag_w.scratch), len(ag_c.scratch), len(ag_m.scratch)

    def body(w_ref, c_ref, s_ref, wada_ref, b_ref, cc_ref, rd_ref,
             gw_ref, call_ref, sall_ref, a_ref, modp_ref, mall_ref, lg_ref, sg_ref, *scr):
        scr_w, scr_c, scr_m = scr[:n_w], scr[n_w:n_w + n_c], scr[n_w + n_c:n_w + n_c + n_m]
        c_v, w_v, m_v, sems = scr[n_w + n_c + n_m:]
        ag_w.start((w_ref,), (gw_ref,), scr_w)
        ag_c.start((c_ref, s_ref), (call_ref, sall_ref), scr_c)
        load_w = pltpu.make_async_copy(wada_ref, w_v, sems.at[0])
        load_w.start()
        rd = rd_ref[...]
        lg_ref[...] = -_softplus(-rd)
        sg_ref[...] = _sigmoid(-rd)
        ag_c.relay((c_ref, s_ref), (call_ref, sall_ref), scr_c)
        ag_c.finish((c_ref, s_ref), (call_ref, sall_ref), scr_c)
        load_c = pltpu.make_async_copy(call_ref, c_v, sems.at[1])
        load_c.start()
        load_c.wait()
        a_ref[...] = jnp.zeros_like(a_ref)
        for d in range(N_DEV):
            cd = c_v[d, 0:1, :]
            a_ref[d:d + 1, :] = cd * _sigmoid(cd)
        cc = cc_ref[...]
        a_ref[N_DEV:N_DEV + 1, :] = cc * _sigmoid(cc)
        load_w.wait()
        m_v[...] = jnp.dot(a_ref[...], w_v[...], preferred_element_type=F32,
                           precision=lax.Precision.HIGHEST) + b_ref[...]
        put = pltpu.make_async_copy(m_v, modp_ref, sems.at[2])
        put.start()
        put.wait()
        ag_m.start((modp_ref,), (mall_ref,), scr_m)
        ag_m.relay((modp_ref,), (mall_ref,), scr_m)
        ag_m.finish((modp_ref,), (mall_ref,), scr_m)
        ag_w.relay((w_ref,), (gw_ref,), scr_w)
        ag_w.finish((w_ref,), (gw_ref,), scr_w)

    rd = jnp.broadcast_to(ret_decay.reshape(2, HEADS).T[:, :, None], (HEADS, 2, LANES))
    lane = _full((HEADS, 2, LANES))
    outs = _pc(
        body, name="head",
        in_specs=[ANY, ANY, ANY, ANY, _full((1, ada_n)), _full((1, D_MODEL)), lane],
        out_specs=[ANY, ANY, ANY, _full((16, D_MODEL)), ANY, ANY, lane, lane],
        out_shape=ag_w.out_shapes + ag_c.out_shapes + [_sds((16, D_MODEL)), mod_sds] + ag_m.out_shapes
        + [_sds((HEADS, 2, LANES))] * 2,
        scratch_shapes=ag_w.scratch + ag_c.scratch + ag_m.scratch
        + [pltpu.VMEM((N_DEV,) + c8.shape, F32), pltpu.VMEM(w_ada.shape, F32), pltpu.VMEM((16, ada_n), F32),
           pltpu.SemaphoreType.DMA((3,))],
        compiler_params=_params(),
    )(w_half, c8, small, w_ada, b_shard, c_ctx.reshape(1, D_MODEL), rd)
    gw, c_all, small_all, a16, _, mod_all, lgv, sgv = outs
    return gw, c_all, small_all, a16, mod_all, lgv, sgv


def _ada_grad(at, b):
    n = b.shape[1]
    bn = 512

    def body(a_ref, b_ref, o_ref):
        o_ref[...] = jnp.dot(a_ref[...], b_ref[...], preferred_element_type=F32, precision=lax.Precision.HIGHEST)

    return _pc(body, name="ada_grad", grid=(n // bn,),
               in_specs=[_full((D_MODEL, LANES)), pl.BlockSpec((LANES, bn), lambda i: (0, i))],
               out_specs=pl.BlockSpec((D_MODEL, bn), lambda i: (0, i)), out_shape=_sds((D_MODEL, n)),
               compiler_params=_params("arbitrary"))(at, b)


def _cctx_partial(dmc8, w_ada):
    n = w_ada.shape[1]
    bn = 512

    def body(d_ref, w_ref, o_ref):
        @pl.when(pl.program_id(0) == 0)
        def _():
            o_ref[...] = jnp.zeros_like(o_ref)
        o_ref[...] += lax.dot_general(d_ref[...], w_ref[...], (((1,), (1,)), ((), ())),
                                      preferred_element_type=F32, precision=lax.Precision.HIGHEST)

    return _pc(body, name="cctx_partial", grid=(n // bn,),
               in_specs=[pl.BlockSpec((8, bn), lambda i: (0, i)), pl.BlockSpec((D_MODEL, bn), lambda i: (0, i))],
               out_specs=_full((8, D_MODEL)), out_shape=_sds((8, D_MODEL)),
               compiler_params=_params("arbitrary"))(dmc8, w_ada)


def _cctx_final(parts, c_ctx, m, v):
    c1 = 1.0 - ADAM_B1 ** ADAM_STEP
    c2 = 1.0 - ADAM_B2 ** ADAM_STEP

    def body(p_ref, c_ref, m_ref, v_ref, g_ref, d_ref, mo_ref, vo_ref):
        s = ((p_ref[0, 0:1, :] + p_ref[2, 0:1, :]) + p_ref[4, 0:1, :]) + p_ref[6, 0:1, :]
        z = c_ref[...]
        sg = _sigmoid(z)
        gg = s * (sg * (1.0 + z * (1.0 - sg)))
        g_ref[...] = gg
        mn = ADAM_B1 * m_ref[...] + (1.0 - ADAM_B1) * gg
        vn = ADAM_B2 * v_ref[...] + (1.0 - ADAM_B2) * (gg * gg)
        d_ref[...] = -ADAM_LR * ((mn / c1) / (jnp.sqrt(vn / c2) + ADAM_EPS) + ADAM_WD * z)
        mo_ref[...] = mn
        vo_ref[...] = vn

    row = _full((1, D_MODEL))
    return _pc(body, name="cctx_final", out_shape=[_sds((1, D_MODEL))] * 4,
               in_specs=[_full(parts.shape), row, row, row], out_specs=[row] * 4,
               compiler_params=_params())(parts, c_ctx.reshape(1, D_MODEL), m.reshape(1, D_MODEL), v.reshape(1, D_MODEL))


def _rotary_tables(t_len):
    rows = t_len // GRID_W
    n_freq = DH // 4
    inv = ROPE_BASE ** (-jnp.arange(n_freq, dtype=F32) / n_freq)
    row_ang = jnp.arange(rows, dtype=F32)[:, None] * inv
    col_ang = jnp.arange(GRID_W, dtype=F32)[:, None] * inv

    def spread(fn):
        return jnp.concatenate([jnp.repeat(fn(row_ang), GRID_W, axis=0), jnp.tile(fn(col_ang), (rows, 1))], axis=-1)

    cos, sin = spread(jnp.cos), spread(jnp.sin)
    return jnp.concatenate([cos, cos], axis=-1), jnp.concatenate([-sin, sin], axis=-1)


def _inproj_fwd(x, gn, sh, sc, w4, cos2, sin2, name, comms=()):
    t = x.shape[0]
    tm = _tile(t, True)
    nc = IN_COLS // N_CHIP

    def body(x_ref, gn_ref, sh_ref, sc_ref, w_ref, c_ref, s_ref, p_ref, xr_ref, hb_ref, p_s):
        xh, _ = _rms(x_ref[...])
        h = xh * gn_ref[...] * (1.0 + sc_ref[...]) + sh_ref[...]
        hb = h.astype(BF16)
        hb_ref[...] = hb
        for j in range(N_CHIP):
            p_s[:, nc * j:nc * (j + 1)] = _dot(hb, w_ref[j])
        cc = c_ref[...]
        ss = s_ref[...]
        for hh in range(2 * HEADS):
            blk = p_s[:, DH * hh:DH * (hh + 1)]
            rot = blk * cc + pltpu.roll(blk, DH // 2, 1) * ss
            if hh >= HEADS:
                rot = rot * K_SCALE
            p_ref[:, DH * hh:DH * (hh + 1)] = rot.astype(BF16)
        p_ref[:, 2 * RET_W:] = p_s[:, 2 * RET_W:].astype(BF16)
        xr_ref[...] = p_s[:, 4 * RET_W:4 * RET_W + LRU_W]

    row = _full((1, D_MODEL))
    outs, couts = _call(
        body, name=name, grid=(t // tm,),
        in_specs=[pl.BlockSpec((tm, D_MODEL), lambda i: (i, 0)), row, row, row, _full(w4.shape),
                  pl.BlockSpec((tm, DH), lambda i: (i, 0)), pl.BlockSpec((tm, DH), lambda i: (i, 0))],
        out_specs=[pl.BlockSpec((tm, IN_COLS), lambda i: (i, 0)), pl.BlockSpec((tm, LRU_W), lambda i: (i, 0)),
                   pl.BlockSpec((tm, D_MODEL), lambda i: (i, 0))],
        out_shape=[_sds((t, IN_COLS), BF16), _sds((t, LRU_W)), _sds((t, D_MODEL), BF16)],
        scratch_shapes=[pltpu.VMEM((tm, IN_COLS), F32)], sem=("arbitrary",),
        args=(x, gn, sh, sc, w4, cos2, sin2), comms=comms)
    return (outs, couts) if comms else outs


def _inproj_bwd(x, gn, sh, sc, w4, cos2, sin2, pieces, dres, name):
    t = x.shape[0]
    tm = _tile(t)
    nc = IN_COLS // N_CHIP

    def body(x_ref, gn_ref, sh_ref, sc_ref, w_ref, c_ref, s_ref, dqf, dqb, dkf, dkb, dvf, dvb, dg, dxr, dgt, dres_ref,
             dx_ref, dpb_ref, dgn_ref, dsh_ref, dsc_ref):
        cc = c_ref[...]
        ss = s_ref[...]
        dq = dqf[...].astype(F32) + dqb[...].astype(F32)
        dk = dkf[...].astype(F32) + dkb[...].astype(F32)
        for hh in range(HEADS):
            sl = slice(DH * hh, DH * (hh + 1))
            b = dq[:, sl]
            dpb_ref[:, sl] = (b * cc + pltpu.roll(b * ss, DH // 2, 1)).astype(BF16)
            b = dk[:, sl]
            dpb_ref[:, RET_W + DH * hh:RET_W + DH * (hh + 1)] = (
                (b * cc + pltpu.roll(b * ss, DH // 2, 1)) * K_SCALE).astype(BF16)
        dpb_ref[:, 2 * RET_W:3 * RET_W] = (dvf[...].astype(F32) + dvb[...].astype(F32)).astype(BF16)
        dpb_ref[:, 3 * RET_W:4 * RET_W] = dg[...].astype(BF16)
        dpb_ref[:, 4 * RET_W:4 * RET_W + LRU_W] = dxr[...].astype(BF16)
        dpb_ref[:, 4 * RET_W + LRU_W:IN_COLS] = dgt[...].astype(BF16)
        dh = _dot_nt(dpb_ref[:, 0:nc], w_ref[0])
        for j in range(1, N_CHIP):
            dh = dh + _dot_nt(dpb_ref[:, nc * j:nc * (j + 1)], w_ref[j])
        dx, dgn_t, dsh_t, dsc_t = _norm_mod_bwd(x_ref[...], gn_ref[...], sc_ref[...], dh)
        dx_ref[...] = dres_ref[...] + dx

        @pl.when(pl.program_id(0) == 0)
        def _():
            dgn_ref[...] = jnp.zeros_like(dgn_ref)
            dsh_ref[...] = jnp.zeros_like(dsh_ref)
            dsc_ref[...] = jnp.zeros_like(dsc_ref)
        dgn_ref[...] += dgn_t
        dsh_ref[...] += dsh_t
        dsc_ref[...] += dsc_t

    row = _full((1, D_MODEL))
    pc = pl.BlockSpec((tm, RET_W), lambda i: (i, 0))
    big = pl.BlockSpec((tm, D_MODEL), lambda i: (i, 0))
    return _pc(body, name=name, grid=(t // tm,),
               in_specs=[big, row, row, row, _full(w4.shape),
                         pl.BlockSpec((tm, DH), lambda i: (i, 0)), pl.BlockSpec((tm, DH), lambda i: (i, 0))]
               + [pc] * 9 + [big],
               out_specs=[big, pl.BlockSpec((tm, IN_COLS), lambda i: (i, 0)), row, row, row],
               out_shape=[_sds((t, D_MODEL)), _sds((t, IN_COLS), BF16), _sds((1, D_MODEL)), _sds((1, D_MODEL)),
                          _sds((1, D_MODEL))],
               compiler_params=_params("arbitrary"))(x, gn, sh, sc, w4, cos2, sin2, *pieces, dres)


def _halo_specs(t, tm):
    n8 = tm // SUBLANES
    last8 = t // SUBLANES - 1
    prev = pl.BlockSpec((SUBLANES, LRU_W), lambda i: (jnp.maximum(i * n8 - 1, 0), 0))
    main = pl.BlockSpec((tm, LRU_W), lambda i: (i, 0))
    nxt = pl.BlockSpec((SUBLANES, LRU_W), lambda i: (jnp.minimum((i + 1) * n8, last8), 0))
    return prev, main, nxt


def _with_halo(prev_ref, main_ref, next_ref, i, nt):
    prev = jnp.where(i > 0, prev_ref[...], 0.0)
    nxt = jnp.where(i < nt - 1, next_ref[...], 0.0)
    return jnp.concatenate([prev, main_ref[...], nxt], axis=0)


def _conv_fwd(xr, cw, cb, name):
    t = xr.shape[0]
    tm = _tile(t, True)
    nt = t // tm
    n = tm + 2 * SUBLANES
    mid = slice(SUBLANES, SUBLANES + tm)

    def body(p_ref, m_ref, n_ref, w_ref, b_ref, o_ref):
        xp = _with_halo(p_ref, m_ref, n_ref, pl.program_id(0), nt)
        acc = b_ref[...] + pltpu.roll(xp, 1, 0)[mid] * w_ref[0:1, :]
        acc = acc + xp[mid] * w_ref[1:2, :]
        acc = acc + pltpu.roll(xp, n - 1, 0)[mid] * w_ref[2:3, :]
        acc = acc + pltpu.roll(xp, n - 2, 0)[mid] * w_ref[3:4, :]
        o_ref[...] = acc

    return _pc(body, name=name, grid=(nt,),
               in_specs=[*_halo_specs(t, tm), _full((4, LRU_W)), _full((1, LRU_W))],
               out_specs=pl.BlockSpec((tm, LRU_W), lambda i: (i, 0)), out_shape=_sds((t, LRU_W)),
               compiler_params=_params("arbitrary"))(xr, xr, xr, cw, cb)


def _conv_bwd(dxc_a, dxc_b, xr, cw, name):
    t = xr.shape[0]
    tm = _tile(t, True)
    nt = t // tm
    n = tm + 2 * SUBLANES
    mid = slice(SUBLANES, SUBLANES + tm)

    def body(ap_ref, am_ref, an_ref, bp_ref, bm_ref, bn_ref, xp_ref, xm_ref, xn_ref, w_ref, dx_ref, dw_ref, db_ref):
        i = pl.program_id(0)
        dp = _with_halo(ap_ref, am_ref, an_ref, i, nt) + _with_halo(bp_ref, bm_ref, bn_ref, i, nt)
        xp = _with_halo(xp_ref, xm_ref, xn_ref, i, nt)
        dx = pltpu.roll(dp, n - 1, 0)[mid] * w_ref[0:1, :]
        dx = dx + dp[mid] * w_ref[1:2, :]
        dx = dx + pltpu.roll(dp, 1, 0)[mid] * w_ref[2:3, :]
        dx = dx + pltpu.roll(dp, 2, 0)[mid] * w_ref[3:4, :]
        dx_ref[...] = dx.astype(BF16)
        d = dp[mid]

        @pl.when(i == 0)
        def _():
            dw_ref[...] = jnp.zeros_like(dw_ref)
            db_ref[...] = jnp.zeros_like(db_ref)
        dw_ref[0:1, :] += _sum0(d * pltpu.roll(xp, 1, 0)[mid])
        dw_ref[1:2, :] += _sum0(d * xp[mid])
        dw_ref[2:3, :] += _sum0(d * pltpu.roll(xp, n - 1, 0)[mid])
        dw_ref[3:4, :] += _sum0(d * pltpu.roll(xp, n - 2, 0)[mid])
        db_ref[...] += _sum0(d)

    return _pc(body, name=name, grid=(nt,),
               in_specs=[*_halo_specs(t, tm), *_halo_specs(t, tm), *_halo_specs(t, tm), _full((4, LRU_W))],
               out_specs=[pl.BlockSpec((tm, LRU_W), lambda i: (i, 0)), _full((4, LRU_W)), _full((1, LRU_W))],
               out_shape=[_sds((t, LRU_W), BF16), _sds((4, LRU_W)), _sds((1, LRU_W))],
               compiler_params=_params("arbitrary"))(dxc_a, dxc_a, dxc_a, dxc_b, dxc_b, dxc_b, xr, xr, xr, cw)


def _scan_scratch(n, c):
    return pltpu.VMEM((c // LANES, n, LANES), F32)


def _to_lane_blocks(ref, val):
    for lb in range(ref.shape[0]):
        ref[lb] = val[:, lb * LANES:(lb + 1) * LANES]


def _group_scan(a_s, b_s, reverse):
    nb, n, _ = a_s.shape
    ng = n // SUBLANES
    order = range(SUBLANES - 1, -1, -1) if reverse else range(SUBLANES)
    for lb in range(nb):
        prev = None
        for r in order:
            rows = pl.ds(r, ng, stride=SUBLANES)
            a_r, b_r = a_s[lb, rows, :], b_s[lb, rows, :]
            if prev is not None:
                b_r = a_r * prev[1] + b_r
                a_r = a_r * prev[0]
                a_s[lb, rows, :] = a_r
                b_s[lb, rows, :] = b_r
            prev = (a_r, b_r)


def _carry_scans(jobs):
    nb, n, _ = jobs[0][0].shape
    ng = n // SUBLANES

    def step(g, all_crs):
        res = []
        for (a_s, b_s, out_ref, _, reverse), crs in zip(jobs, all_crs):
            gg = (ng - 1 - g) if reverse else g
            off = pl.multiple_of(gg * SUBLANES, SUBLANES)
            new = []
            for lb in range(nb):
                h = a_s[lb, pl.ds(off, SUBLANES), :] * crs[lb] + b_s[lb, pl.ds(off, SUBLANES), :]
                out_ref[pl.ds(off, SUBLANES), pl.ds(lb * LANES, LANES)] = h
                edge = h[0:1, :] if reverse else h[SUBLANES - 1:SUBLANES, :]
                new.append(jnp.broadcast_to(edge, (SUBLANES, LANES)))
            res.append(tuple(new))
        return tuple(res)

    init = tuple(tuple(job[3][:, lb * LANES:(lb + 1) * LANES] for lb in range(nb)) for job in jobs)
    return [jnp.concatenate(crs, axis=1) for crs in lax.fori_loop(0, ng, step, init)]


def _lru_gates(xc, wa_ref, wx_ref, ba, bx, lam):
    xb = xc.astype(BF16)
    r = _sigmoid(_dot(xb, wa_ref[...]) + ba)
    ig = _sigmoid(_dot(xb, wx_ref[...]) + bx)
    sp = _softplus(-lam)
    la = -LRU_C * r * sp
    a = jnp.exp(la)
    return r, ig, sp, a, _one_minus_sq(la, a)


def _lru_fwd(xc, par_f, par_b, h0_f, h0_b, name, comms=()):
    t = xc.shape[0]
    tm = _tile(t, True)
    nt = t // tm

    def one(x_ref, prm, h0_ref, a_s, b_s, c_s, reverse):
        wa_ref, wx_ref, ba_ref, bx_ref, lam_ref = prm

        @pl.when(pl.program_id(0) == 0)
        def _():
            c_s[...] = jnp.broadcast_to(h0_ref[...], c_s.shape)
        xv = x_ref[...]
        _, ig, _, a, q = _lru_gates(xv, wa_ref, wx_ref, ba_ref[...], bx_ref[...], lam_ref[...])
        _to_lane_blocks(a_s, a)
        _to_lane_blocks(b_s, jnp.sqrt(q) * (ig * xv))
        _group_scan(a_s, b_s, reverse)

    def body(xf_ref, xb_ref, *refs):
        prm_f, prm_b = refs[0:5], refs[5:10]
        h0f_ref, h0b_ref, hf_ref, hb_ref = refs[10:14]
        af_s, bf_s, cf_s, ab_s, bb_s, cb_s = refs[14:]
        one(xf_ref, prm_f, h0f_ref, af_s, bf_s, cf_s, False)
        one(xb_ref, prm_b, h0b_ref, ab_s, bb_s, cb_s, True)
        cf_s[...], cb_s[...] = _carry_scans([(af_s, bf_s, hf_ref, cf_s[...], False),
                                             (ab_s, bb_s, hb_ref, cb_s[...], True)])

    vec = _full((1, LRU_W))
    mat = _full((LRU_W, LRU_W))
    fw = pl.BlockSpec((tm, LRU_W), lambda i: (i, 0))
    bw = pl.BlockSpec((tm, LRU_W), lambda i: (nt - 1 - i, 0))
    tile_s = [_scan_scratch(tm, LRU_W), _scan_scratch(tm, LRU_W), pltpu.VMEM((SUBLANES, LRU_W), F32)]
    (hf, hb), couts = _call(
        body, name=name, grid=(nt,),
        in_specs=[fw, bw] + [mat, mat, vec, vec, vec] * 2 + [vec, vec],
        out_specs=[pl.BlockSpec((tm, LRU_W), lambda i: (i, 0)), pl.BlockSpec((tm, LRU_W), lambda i: (nt - 1 - i, 0))],
        out_shape=[_sds((t, LRU_W))] * 2, scratch_shapes=tile_s + tile_s, sem=("arbitrary",),
        args=(xc, xc, *par_f, *par_b, h0_f, h0_b), comms=comms)
    return ((hf, hb), couts) if comms else (hf, hb)


def _lru_bwd(xc, par_f, par_b, h_f, h_b, h0_f, h0_b, dh_f, dh_b, name, comms=()):
    t = xc.shape[0]
    tm = _tile(t, True)
    nt = t // tm
    n8 = tm // SUBLANES
    last8 = t // SUBLANES - 1
    tile_f = lambda w: pl.BlockSpec((tm, w), lambda i: (nt - 1 - i, 0))
    tile_b = lambda w: pl.BlockSpec((tm, w), lambda i: (i, 0))
    halo_f = pl.BlockSpec((SUBLANES, LRU_W), lambda i: (jnp.maximum((nt - 1 - i) * n8 - 1, 0), 0))
    halo_b = pl.BlockSpec((SUBLANES, LRU_W), lambda i: (jnp.minimum((i + 1) * n8, last8), 0))

    def one(refs_in, refs_out, refs_scr, reverse):
        x_ref, wa_ref, wx_ref, ba_ref, bx_ref, lam_ref, h_ref, halo_ref, h0_ref, dh_ref = refs_in
        dx_ref, dpre_ref, dba_ref, dbx_ref, dlam_ref, dh0_ref = refs_out
        a_s, b_s, l_s, c_s, e_s = refs_scr
        i = pl.program_id(0)

        @pl.when(i == 0)
        def _():
            c_s[...] = jnp.zeros_like(c_s)
            e_s[...] = jnp.zeros_like(e_s)
            dba_ref[...] = jnp.zeros_like(dba_ref)
            dbx_ref[...] = jnp.zeros_like(dbx_ref)
            dlam_ref[...] = jnp.zeros_like(dlam_ref)
        xv = x_ref[...]
        lam = lam_ref[...]
        r, ig, sp, a, q = _lru_gates(xv, wa_ref, wx_ref, ba_ref[...], bx_ref[...], lam)
        rs = lax.rsqrt(q)
        hv = h_ref[...]
        rowi = lax.broadcasted_iota(jnp.int32, (tm, LRU_W), 0)
        edge_a = jnp.broadcast_to(e_s[0:1, :], (tm, LRU_W))
        h0b = jnp.broadcast_to(h0_ref[...], (tm, LRU_W))
        if reverse:
            a_sh = jnp.where(rowi == 0, edge_a, pltpu.roll(a, 1, 0))
            hin_edge = jnp.where(i == nt - 1, h0b, jnp.broadcast_to(halo_ref[0:1, :], (tm, LRU_W)))
            h_in = jnp.where(rowi == tm - 1, hin_edge, pltpu.roll(hv, tm - 1, 0))
        else:
            a_sh = jnp.where(rowi == tm - 1, edge_a, pltpu.roll(a, tm - 1, 0))
            hin_edge = jnp.where(i == nt - 1, h0b, jnp.broadcast_to(halo_ref[SUBLANES - 1:SUBLANES, :], (tm, LRU_W)))
            h_in = jnp.where(rowi == 0, hin_edge, pltpu.roll(hv, 1, 0))
        _to_lane_blocks(a_s, a_sh)
        _to_lane_blocks(b_s, dh_ref[...])
        _group_scan(a_s, b_s, not reverse)
        e_s[...] = jnp.broadcast_to(a[tm - 1:tm, :] if reverse else a[0:1, :], e_s.shape)
        return lam, r, ig, sp, a, q * rs, rs, h_in

    def post(vals, refs_in, refs_out, refs_scr, reverse):
        lam, r, ig, sp, a, mult, rs, h_in = vals
        xv = refs_in[0][...]
        wa_ref, wx_ref = refs_in[1:3]
        dx_ref, dpre_ref, dba_ref, dbx_ref, dlam_ref, dh0_ref = refs_out
        l_s = refs_scr[2]
        i = pl.program_id(0)
        lmb = l_s[...]
        da = lmb * h_in
        ixc = ig * xv
        dmult = lmb * ixc
        dixc = lmb * mult
        dla = da * a - dmult * (a * a) * rs
        dpr = dla * (-LRU_C * sp) * r * (1.0 - r)
        dpi = dixc * xv * ig * (1.0 - ig)
        dprb = dpr.astype(BF16)
        dpib = dpi.astype(BF16)
        dpre_ref[:, 0:LRU_W] = dprb
        dpre_ref[:, LRU_W:2 * LRU_W] = dpib
        dx_ref[...] = dixc * ig + _dot_nt(dprb, wa_ref[...]) + _dot_nt(dpib, wx_ref[...])
        dba_ref[...] += _sum0(dpr)
        dbx_ref[...] += _sum0(dpi)
        dlam_ref[...] += _sum0(dla * (-LRU_C * r)) * (-_sigmoid(-lam))

        @pl.when(i == nt - 1)
        def _():
            al0 = a * lmb
            dh0_ref[...] = al0[tm - 1:tm, :] if reverse else al0[0:1, :]

    def body(*refs):
        jobs = ((refs[0:10], refs[20:26], refs[32:37], False), (refs[10:20], refs[26:32], refs[37:42], True))
        vals = [one(*job) for job in jobs]
        carries = _carry_scans([(scr[0], scr[1], scr[2], scr[3][...], not rev) for _, _, scr, rev in jobs])
        for (_, _, scr, _), carry in zip(jobs, carries):
            scr[3][...] = carry
        for v, job in zip(vals, jobs):
            post(v, *job)

    vec = _full((1, LRU_W))
    mat = _full((LRU_W, LRU_W))

    def in_specs(tile, halo):
        return [tile(LRU_W), mat, mat, vec, vec, vec, tile(LRU_W), halo, vec, tile(LRU_W)]

    def out_specs(tile):
        return [tile(LRU_W), tile(2 * LRU_W), vec, vec, vec, vec]

    out_one = [_sds((t, LRU_W)), _sds((t, 2 * LRU_W), BF16)] + [_sds((1, LRU_W))] * 4
    scr_one = ([_scan_scratch(tm, LRU_W)] * 2 + [pltpu.VMEM((tm, LRU_W), F32)]
               + [pltpu.VMEM((SUBLANES, LRU_W), F32)] * 2)
    outs, couts = _call(
        body, name=name, grid=(nt,), in_specs=in_specs(tile_f, halo_f) + in_specs(tile_b, halo_b),
        out_specs=out_specs(tile_f) + out_specs(tile_b), out_shape=out_one + out_one,
        scratch_shapes=scr_one + scr_one, sem=("arbitrary",),
        args=(xc, *par_f, h_f, h_f, h0_f, dh_f, xc, *par_b, h_b, h_b, h0_b, dh_b), comms=comms)
    return (tuple(outs[0:6]), tuple(outs[6:12])), couts


def _decay_tables(lg, reverse):
    ci = lax.broadcasted_iota(jnp.int32, (CHUNK, CHUNK), 0).astype(F32)
    mi = lax.broadcasted_iota(jnp.int32, (CHUNK, CHUNK), 1).astype(F32)
    rel = (mi - ci) if reverse else (ci - mi)
    relc = jnp.maximum(rel, 0.0)
    lg_c = jnp.concatenate([lg] * (CHUNK // LANES), axis=1)
    dm = jnp.where(rel >= 0, jnp.exp(lg_c * relc), 0.0)
    cd = lax.broadcasted_iota(jnp.int32, (CHUNK, DH), 0).astype(F32)
    pq, ps = (CHUNK - cd, cd) if reverse else (cd + 1.0, CHUNK - 1.0 - cd)
    return relc, dm, jnp.exp(lg * pq), jnp.exp(lg * ps), jnp.exp(lg * float(CHUNK)), pq, ps


def _ret_fwd(proj, lgv, s0f, s0b, comms=()):
    t = proj.shape[0]
    n = t // CHUNK

    def one(q, k, v, lg, s_s, hh, o_ref, sp_ref, reverse):
        _, dm, wq, ws, g, _, _ = _decay_tables(lg, reverse)
        vb = v.astype(BF16)
        p = _dot_nt(q.astype(BF16), k.astype(BF16)) * dm
        s = s_s[hh]
        sp_ref[hh, 0] = s
        o_ref[:, DH * hh:DH * (hh + 1)] = _dot(p.astype(BF16), vb) + _dot((q * wq).astype(BF16), s.astype(BF16))
        s_s[hh] = g * s + _dot_tn((k * ws).astype(BF16), vb)

    def body(qf, kf, vf, qb, kb, vb, lg_ref, s0f_ref, s0b_ref, of_ref, ob_ref, spf_ref, spb_ref, sf_s, sb_s):
        @pl.when(pl.program_id(0) == 0)
        def _():
            sf_s[...] = s0f_ref[...]
            sb_s[...] = s0b_ref[...]
        for hh in range(HEADS):
            sl = slice(DH * hh, DH * (hh + 1))
            one(qf[:, sl].astype(F32), kf[:, sl].astype(F32), vf[:, sl], lg_ref[hh, 0:1, :], sf_s, hh, of_ref, spf_ref,
                False)
            one(qb[:, sl].astype(F32), kb[:, sl].astype(F32), vb[:, sl], lg_ref[hh, 1:2, :], sb_s, hh, ob_ref, spb_ref,
                True)

    blk = (CHUNK, RET_W)
    fw = [pl.BlockSpec(blk, lambda i, o=o: (i, o)) for o in range(3)]
    bw = [pl.BlockSpec(blk, lambda i, o=o: (n - 1 - i, o)) for o in range(3)]
    st = _full((HEADS, DH, DH))
    return _call(body, name="ret_fwd", grid=(n,),
                 in_specs=fw + bw + [_full((HEADS, 2, LANES)), st, st],
                 out_specs=[pl.BlockSpec(blk, lambda i: (i, 0)), pl.BlockSpec(blk, lambda i: (n - 1 - i, 0)),
                            pl.BlockSpec((HEADS, 1, DH, DH), lambda i: (0, i, 0, 0)),
                            pl.BlockSpec((HEADS, 1, DH, DH), lambda i: (0, n - 1 - i, 0, 0))],
                 out_shape=[_sds((t, RET_W)), _sds((t, RET_W)), _sds((HEADS, n, DH, DH)), _sds((HEADS, n, DH, DH))],
                 scratch_shapes=[pltpu.VMEM((HEADS, DH, DH), F32), pltpu.VMEM((HEADS, DH, DH), F32)],
                 sem=("arbitrary",), args=(proj, proj, proj, proj, proj, proj, lgv, s0f, s0b), comms=comms)


def _ret_bwd(proj, lgv, sgv, spf, spb, do, comms=()):
    t = proj.shape[0]
    n = t // CHUNK

    def one(q_ref, k_ref, v_ref, lg_ref, s_ref, do_ref, dq_ref, dk_ref, dv_ref, ds_s, acc_s, reverse):
        d = 1 if reverse else 0
        for hh in range(HEADS):
            sl = slice(DH * hh, DH * (hh + 1))
            relc, dm, wq, ws, g, pq, ps = _decay_tables(lg_ref[hh, d:d + 1, :], reverse)
            qb, kb, vb = q_ref[:, sl], k_ref[:, sl], v_ref[:, sl]
            q, k = qb.astype(F32), kb.astype(F32)
            p = _dot_nt(qb, kb) * dm
            s = s_ref[hh, 0]
            dob = do_ref[:, sl].astype(BF16)
            dsn = ds_s[hh]
            dsb = dsn.astype(BF16)
            dv_ref[:, sl] = (_dot_tn(p.astype(BF16), dob) + _dot((k * ws).astype(BF16), dsb)).astype(BF16)
            dp = _dot_nt(dob, vb)
            dab = (dp * dm).astype(BF16)
            xq = _dot_nt(dob, s.astype(BF16))
            yk = _dot_nt(vb, dsb)
            dq_ref[:, sl] = (_dot(dab, kb) + xq * wq).astype(BF16)
            dk_ref[:, sl] = (_dot_tn(dab, qb) + yk * ws).astype(BF16)
            ds_s[hh] = g * dsn + _dot_tn((q * wq).astype(BF16), dob)
            s_mask = _sum0(dp * p * relc)
            part = (sum(s_mask[:, LANES * u:LANES * (u + 1)] for u in range(CHUNK // LANES))
                    + _sum0(xq * q * wq * pq) + _sum0(yk * k * ws * ps) + _sum0(dsn * s) * g * float(CHUNK))
            acc_s[hh] += jnp.broadcast_to(part, (SUBLANES, LANES))

    def body(qf, kf, vf, qb, kb, vb, lg_ref, sg_ref, sf_ref, sb_ref, dof_ref, dob_ref,
             dqf, dkf, dvf, dqb, dkb, dvb, ds0f_ref, ds0b_ref, drdf_ref, drdb_ref, dsf_s, dsb_s, accf_s, accb_s):
        i = pl.program_id(0)

        @pl.when(i == 0)
        def _():
            for r in (dsf_s, dsb_s, accf_s, accb_s):
                r[...] = jnp.zeros_like(r)
        one(qf, kf, vf, lg_ref, sf_ref, dof_ref, dqf, dkf, dvf, dsf_s, accf_s, False)
        one(qb, kb, vb, lg_ref, sb_ref, dob_ref, dqb, dkb, dvb, dsb_s, accb_s, True)

        @pl.when(i == n - 1)
        def _():
            ds0f_ref[...] = dsf_s[...]
            ds0b_ref[...] = dsb_s[...]
            for d, (acc_s, drd_ref) in enumerate(((accf_s, drdf_ref), (accb_s, drdb_ref))):
                for hh in range(HEADS):
                    tot = jnp.sum(acc_s[hh, 0:1, :], axis=1, keepdims=True)
                    drd_ref[hh] = jnp.broadcast_to(tot, (SUBLANES, LANES)) * sg_ref[hh, d:d + 1, :]

    blk = (CHUNK, RET_W)
    fw = lambda o: pl.BlockSpec(blk, lambda i, o=o: (n - 1 - i, o))
    bw = lambda o: pl.BlockSpec(blk, lambda i, o=o: (i, o))
    lane = _full((HEADS, 2, LANES))
    st = _full((HEADS, DH, DH))
    rd = _full((HEADS, SUBLANES, LANES))
    outs, couts = _call(
        body, name="ret_bwd", grid=(n,),
        in_specs=[fw(0), fw(1), fw(2), bw(0), bw(1), bw(2), lane, lane,
                  pl.BlockSpec((HEADS, 1, DH, DH), lambda i: (0, n - 1 - i, 0, 0)),
                  pl.BlockSpec((HEADS, 1, DH, DH), lambda i: (0, i, 0, 0)), fw(0), bw(0)],
        out_specs=[fw(0), fw(0), fw(0), bw(0), bw(0), bw(0), st, st, rd, rd],
        out_shape=[_sds((t, RET_W), BF16)] * 6 + [_sds((HEADS, DH, DH))] * 2 + [_sds((HEADS, SUBLANES, LANES))] * 2,
        scratch_shapes=[pltpu.VMEM((HEADS, DH, DH), F32)] * 2 + [pltpu.VMEM((HEADS, SUBLANES, LANES), F32)] * 2,
        sem=("arbitrary",), args=(proj, proj, proj, proj, proj, proj, lgv, sgv, spf, spb, do, do), comms=comms)
    dqf, dkf, dvf, dqb, dkb, dvb, ds0f, ds0b, drdf, drdb = outs
    return ((dqf, dkf, dvf, ds0f, drdf), (dqb, dkb, dvb, ds0b, drdb)), couts


def _ctx_weights(lg, l_len, reverse):
    pos = lax.broadcasted_iota(jnp.int32, (l_len, DH), 0).astype(F32)
    steps = pos if reverse else (l_len - 1.0 - pos)
    return jnp.exp(lg * steps), steps


def _ctx_state_fwd(projc, lgv):
    l_len = projc.shape[0]

    def body(k_ref, v_ref, lg_ref, sf_ref, sb_ref):
        k = k_ref[...]
        vb = v_ref[...].astype(BF16)
        for d, o_ref in ((0, sf_ref), (1, sb_ref)):
            w, _ = _ctx_weights(lg_ref[0, d:d + 1, :], l_len, d == 1)
            o_ref[0] = _dot_tn((k * w).astype(BF16), vb)

    st = pl.BlockSpec((1, DH, DH), lambda h: (h, 0, 0))
    return _pc(body, name="ctx_state_fwd", grid=(HEADS,),
               in_specs=[pl.BlockSpec((l_len, DH), lambda h: (0, HEADS + h)),
                         pl.BlockSpec((l_len, DH), lambda h: (0, 2 * HEADS + h)),
                         pl.BlockSpec((1, 2, LANES), lambda h: (h, 0, 0))],
               out_specs=[st, st], out_shape=[_sds((HEADS, DH, DH))] * 2,
               compiler_params=_params("arbitrary"))(projc, projc, lgv)


def _ctx_state_bwd(projc, lgv, sgv, dsf, dsb):
    l_len = projc.shape[0]

    def body(k_ref, v_ref, lg_ref, sg_ref, dsf_ref, dsb_ref, dk_ref, dv_ref, drd_ref):
        k = k_ref[...]
        vb = v_ref[...].astype(BF16)
        dk = jnp.zeros((l_len, DH), F32)
        dv = jnp.zeros((l_len, DH), F32)
        rows = []
        for d, ds_ref in ((0, dsf_ref), (1, dsb_ref)):
            w, steps = _ctx_weights(lg_ref[0, d:d + 1, :], l_len, d == 1)
            dsb16 = ds_ref[0].astype(BF16)
            dkw = _dot_nt(vb, dsb16)
            dk = dk + dkw * w
            dv = dv + _dot((k * w).astype(BF16), dsb16)
            tot = jnp.sum(_sum0(dkw * k * w * steps), axis=1, keepdims=True)
            rows.append(jnp.broadcast_to(tot, (1, LANES)) * sg_ref[0, d:d + 1, :])
        dk_ref[...] = dk.astype(BF16)
        dv_ref[...] = dv.astype(BF16)
        rid = lax.broadcasted_iota(jnp.int32, (SUBLANES, LANES), 0)
        drd_ref[0] = jnp.where(rid == 0, rows[0], jnp.where(rid == 1, rows[1], 0.0))

    st = pl.BlockSpec((1, DH, DH), lambda h: (h, 0, 0))
    lane = pl.BlockSpec((1, 2, LANES), lambda h: (h, 0, 0))
    hc = pl.BlockSpec((l_len, DH), lambda h: (0, h))
    return _pc(body, name="ctx_state_bwd", grid=(HEADS,),
               in_specs=[pl.BlockSpec((l_len, DH), lambda h: (0, HEADS + h)),
                         pl.BlockSpec((l_len, DH), lambda h: (0, 2 * HEADS + h)), lane, lane, st, st],
               out_specs=[hc, hc, pl.BlockSpec((1, SUBLANES, LANES), lambda h: (h, 0, 0))],
               out_shape=[_sds((l_len, RET_W), BF16), _sds((l_len, RET_W), BF16), _sds((HEADS, SUBLANES, LANES))],
               compiler_params=_params("arbitrary"))(projc, projc, lgv, sgv, dsf, dsb)


G_BLOCK = (3 * RET_W) // RET_W
GATE_BLOCK = (4 * RET_W + LRU_W) // LRU_W


def _head_norm(y):
    yc = y - jnp.mean(y, axis=-1, keepdims=True)
    rs = lax.rsqrt(jnp.mean(yc * yc, axis=-1, keepdims=True) + EPS)
    return yc * rs, rs


def _gelu_parts(z):
    th = jnp.tanh(GELU_K * (z + GELU_C * z * z * z))
    return 0.5 * z * (1.0 + th), th


def _mix_fwd(o_f, o_b, proj, hf, hb, w_out, x, g1, comms):
    t = x.shape[0]
    tm = _tile(t, True)

    def body(of_ref, ob_ref, g_ref, gt_ref, hf_ref, hb_ref, w_ref, x_ref, g1_ref, x1_ref, cat_ref):
        o = of_ref[...] + ob_ref[...]
        g = g_ref[...].astype(F32)
        for hh in range(HEADS):
            sl = slice(DH * hh, DH * (hh + 1))
            nrm, _ = _head_norm(o[:, sl])
            gh = g[:, sl]
            cat_ref[:, sl] = (gh * _sigmoid(gh) * nrm).astype(BF16)
        gel, _ = _gelu_parts(gt_ref[...].astype(F32))
        cat_ref[:, RET_W:] = ((hf_ref[...] + hb_ref[...]) * gel).astype(BF16)
        x1_ref[...] = x_ref[...] + g1_ref[...] * _dot(cat_ref[...], w_ref[...])

    half = pl.BlockSpec((tm, RET_W), lambda i: (i, 0))
    big = pl.BlockSpec((tm, D_MODEL), lambda i: (i, 0))
    return _call(body, name="mix_fwd", grid=(t // tm,),
                 in_specs=[half, half, pl.BlockSpec((tm, RET_W), lambda i: (i, G_BLOCK)),
                           pl.BlockSpec((tm, LRU_W), lambda i: (i, GATE_BLOCK)), half, half,
                           _full((D_MODEL, D_MODEL)), big, _full((1, D_MODEL))],
                 out_specs=[big, big], out_shape=[_sds((t, D_MODEL)), _sds((t, D_MODEL), BF16)],
                 scratch_shapes=[], sem=("arbitrary",), args=(o_f, o_b, proj, proj, hf, hb, w_out, x, g1),
                 comms=comms)


def _mix_bwd(o_f, o_b, proj, hf, hb, w_out, cat, dx1, g1, comms=()):
    t = dx1.shape[0]
    tm = _tile(t, True)

    def body(of_ref, ob_ref, g_ref, gt_ref, hf_ref, hb_ref, w_ref, cat_ref, dx1_ref, g1_ref,
             do_ref, dhs_ref, dg_ref, dgt_ref, dyb_ref, dg1_ref):
        dx1v = dx1_ref[...]
        y = _dot(cat_ref[...], w_ref[...])

        @pl.when(pl.program_id(0) == 0)
        def _():
            dg1_ref[...] = jnp.zeros_like(dg1_ref)
        dg1_ref[...] += _sum0(dx1v * y)
        dyb = (g1_ref[...] * dx1v).astype(BF16)
        dyb_ref[...] = dyb
        dcat = _dot_nt(dyb, w_ref[...])
        o = of_ref[...] + ob_ref[...]
        g = g_ref[...].astype(F32)
        for hh in range(HEADS):
            sl = slice(DH * hh, DH * (hh + 1))
            nrm, rs = _head_norm(o[:, sl])
            gh = g[:, sl]
            sg = _sigmoid(gh)
            dret = dcat[:, sl]
            dg_ref[:, sl] = (dret * nrm * (sg * (1.0 + gh * (1.0 - sg)))).astype(BF16)
            dn = dret * (gh * sg)
            dyc = rs * (dn - nrm * jnp.mean(dn * nrm, axis=-1, keepdims=True))
            do_ref[:, sl] = (dyc - jnp.mean(dyc, axis=-1, keepdims=True)).astype(BF16)
        z = gt_ref[...].astype(F32)
        gel, th = _gelu_parts(z)
        dlru = dcat[:, RET_W:]
        dhs_ref[...] = dlru * gel
        dgel = 0.5 * (1.0 + th) + 0.5 * z * (1.0 - th * th) * GELU_K * (1.0 + 3.0 * GELU_C * z * z)
        dgt_ref[...] = (dlru * (hf_ref[...] + hb_ref[...]) * dgel).astype(BF16)

    half = pl.BlockSpec((tm, RET_W), lambda i: (i, 0))
    big = pl.BlockSpec((tm, D_MODEL), lambda i: (i, 0))
    return _call(body, name="mix_bwd", grid=(t // tm,),
                 in_specs=[half, half, pl.BlockSpec((tm, RET_W), lambda i: (i, G_BLOCK)),
                           pl.BlockSpec((tm, LRU_W), lambda i: (i, GATE_BLOCK)), half, half,
                           _full((D_MODEL, D_MODEL)), big, big, _full((1, D_MODEL))],
                 out_specs=[half, half, half, half, big, _full((1, D_MODEL))],
                 out_shape=[_sds((t, RET_W), BF16), _sds((t, RET_W)), _sds((t, RET_W), BF16), _sds((t, RET_W), BF16),
                            _sds((t, D_MODEL), BF16), _sds((1, D_MODEL))],
                 scratch_shapes=[], sem=("arbitrary",), args=(o_f, o_b, proj, proj, hf, hb, w_out, cat, dx1, g1),
                 comms=comms)


def _mlp(x1, n2g, sh2, sc2, g2, fg, w1_parts, w2_parts, tgt):
    t = x1.shape[0]
    tm = _tile(t)
    hb_ = MLP_H // N_CHIP
    q_rows = hb_ // 4
    n_cp = 4 * N_DEV

    def body(x1_ref, n2g_ref, sh2_ref, sc2_ref, g2_ref, fg_ref, w1a, w1b, w2a, w2b, tgt_ref,
             dx1_ref, h2b_ref, ab_ref, dub_ref, dmb_ref, dsc_ref, dsh_ref, dg2_ref, dn2_ref, dfg_ref, loss_ref,
             w1_s, w2_s, r_s, sems):
        @pl.when(pl.program_id(0) == 0)
        def _():
            cps = []
            for p, parts in enumerate(((w1a, w2a), (w1b, w2b))):
                for d in range(N_DEV):
                    rows = pl.ds(2 * q_rows * (d % 2) + q_rows * p, q_rows)
                    for src, dst in zip(parts, (w1_s, w2_s)):
                        cps.append(pltpu.make_async_copy(src.at[d], dst.at[d // 2, rows], sems.at[len(cps)]))
            for cp in cps:
                cp.start()
            for r in (dsc_ref, dsh_ref, dg2_ref, dn2_ref, dfg_ref, loss_ref):
                r[...] = jnp.zeros_like(r)
            for cp in cps:
                cp.wait()
        x1v = x1_ref[...]
        n2g, sc2, g2, fg = n2g_ref[...], sc2_ref[...], g2_ref[...], fg_ref[...]
        xh, _ = _rms(x1v)
        h2b = (xh * n2g * (1.0 + sc2) + sh2_ref[...]).astype(BF16)
        h2b_ref[...] = h2b
        m = jnp.zeros((tm, D_MODEL), F32)
        for j in range(N_CHIP):
            sl = slice(hb_ * j, hb_ * (j + 1))
            r = jnp.maximum(_dot(h2b, w1_s[j]), 0.0)
            r_s[:, sl] = r
            ab = (r * r).astype(BF16)
            ab_ref[:, sl] = ab
            m = m + _dot(ab, w2_s[j])
        x2 = x1v + g2 * m
        x2h, r2 = _rms(x2)
        err = x2h * fg - tgt_ref[...]
        loss_ref[...] += _sum0(err * err)
        dout = err * (1.0 / D_MODEL)
        dfg_ref[...] += _sum0(dout * x2h)
        dxh = dout * fg
        dx2 = r2 * (dxh - x2h * jnp.mean(dxh * x2h, axis=-1, keepdims=True))
        dg2_ref[...] += _sum0(dx2 * m)
        dmb = (g2 * dx2).astype(BF16)
        dmb_ref[...] = dmb
        dh2 = jnp.zeros((tm, D_MODEL), F32)
        for j in range(N_CHIP):
            sl = slice(hb_ * j, hb_ * (j + 1))
            dub = (_dot_nt(dmb, w2_s[j]) * (2.0 * r_s[:, sl])).astype(BF16)
            dub_ref[:, sl] = dub
            dh2 = dh2 + _dot_nt(dub, w1_s[j])
        dx, dn2_t, dsh_t, dsc_t = _norm_mod_bwd(x1v, n2g, sc2, dh2)
        dx1_ref[...] = dx2 + dx
        dn2_ref[...] += dn2_t
        dsh_ref[...] += dsh_t
        dsc_ref[...] += dsc_t

        @pl.when(pl.program_id(0) == t // tm - 1)
        def _():
            tot = jnp.sum(loss_ref[...], axis=1, keepdims=True) * (0.5 / D_MODEL)
            loss_ref[...] = jnp.broadcast_to(tot, loss_ref.shape)

    row = _full((1, D_MODEL))
    big = pl.BlockSpec((tm, D_MODEL), lambda i: (i, 0))
    wide = pl.BlockSpec((tm, MLP_H), lambda i: (i, 0))
    return _pc(body, name="mlp", grid=(t // tm,),
               in_specs=[big, row, row, row, row, row, ANY, ANY, ANY, ANY, big],
               out_specs=[big, big, wide, wide, big, row, row, row, row, row, row],
               out_shape=[_sds((t, D_MODEL)), _sds((t, D_MODEL), BF16), _sds((t, MLP_H), BF16), _sds((t, MLP_H), BF16),
                          _sds((t, D_MODEL), BF16)] + [_sds((1, D_MODEL))] * 6,
               scratch_shapes=[pltpu.VMEM((N_CHIP, D_MODEL, hb_), BF16), pltpu.VMEM((N_CHIP, hb_, D_MODEL), BF16),
                               pltpu.VMEM((tm, MLP_H), F32), pltpu.SemaphoreType.DMA((n_cp,))],
               compiler_params=_params("arbitrary"))(x1, n2g, sh2, sc2, g2, fg, *w1_parts, *w2_parts, tgt)


def _tn(a, b, nj, a_blocked, b_blocked, name, extra=None, comms=()):
    t = a.shape[0]
    m = a.shape[1] // (nj if a_blocked else 1)
    n = b.shape[1] // (nj if b_blocked else 1)
    bk = next((b for b in (2048, 1024, 512) if t % b == 0), t)
    nk = t // bk
    a_col = (lambda j: j) if a_blocked else (lambda j: 0)
    b_col = (lambda j: j) if b_blocked else (lambda j: 0)
    in_specs = [pl.BlockSpec((bk, m), lambda j, k: (k, a_col(j))), pl.BlockSpec((bk, n), lambda j, k: (k, b_col(j)))]
    args = [a, b]
    if extra is not None:
        a2, b2 = extra
        t2 = a2.shape[0]
        in_specs += [pl.BlockSpec((t2, m), lambda j, k: (0, a_col(j))),
                     pl.BlockSpec((t2, n), lambda j, k: (0, b_col(j)))]
        args += [a2, b2]

    def body(*refs):
        a_ref, b_ref = refs[0], refs[1]
        o_ref, acc = refs[-2], refs[-1]
        k = pl.program_id(1)

        @pl.when(k == 0)
        def _():
            acc[...] = jnp.zeros_like(acc)
        acc[...] += _dot_tn(a_ref[...].astype(BF16), b_ref[...].astype(BF16))

        @pl.when(k == nk - 1)
        def _():
            if extra is not None:
                acc[...] += _dot_tn(refs[2][...].astype(BF16), refs[3][...].astype(BF16))
            o_ref[0] = acc[...]

    (out,), couts = _call(body, name=name, grid=(nj, nk), in_specs=in_specs,
                          out_specs=[pl.BlockSpec((1, m, n), lambda j, k: (j, 0, 0))], out_shape=[_sds((nj, m, n))],
                          scratch_shapes=[pltpu.VMEM((m, n), F32)], sem=("arbitrary", "arbitrary"), args=args,
                          comms=comms)
    return (out, couts) if comms else out


ROW_LOSS = 0
ROW_DMOD = 1
ROW_DMODC = 7
ROW_N1, ROW_N2, ROW_FG, ROW_CB = 9, 10, 11, 12
ROW_BA, ROW_BX, ROW_LAM = 13, 15, 17
ROW_CW = 20
ROW_RD = 24
SLAB_ROWS = 32
SEG = D_MODEL // 2


def _pack_small(rows, drd, cw2, cb2, lru2, gates):
    n_rows, n_lru = len(rows), len(lru2)

    def body(*refs):
        r = refs[:n_rows]
        drd_f, drd_b, drd_c, cw_a, cw_b, cb_a, cb_b = refs[n_rows:n_rows + 7]
        lru = refs[n_rows + 7:n_rows + 7 + n_lru]
        gf_ref, gb_ref, slab, ga, gx = refs[n_rows + 7 + n_lru:]
        slab[...] = jnp.zeros_like(slab)
        slab[ROW_LOSS:ROW_LOSS + 1, :] = r[0][...]
        for k in range(N_MOD):
            slab[ROW_DMOD + k:ROW_DMOD + k + 1, :] = r[1 + k][...]
        slab[ROW_DMODC:ROW_DMODC + 1, :] = r[7][...]
        slab[ROW_DMODC + 1:ROW_DMODC + 2, :] = r[8][...]
        slab[ROW_N1:ROW_N1 + 1, :] = r[9][...] + r[10][...]
        slab[ROW_N2:ROW_N2 + 1, :] = r[11][...]
        slab[ROW_FG:ROW_FG + 1, :] = r[12][...]
        slab[ROW_CB:ROW_CB + 1, 0:LRU_W] = cb_a[...] + cb_b[...]
        for k, row in enumerate((ROW_BA, ROW_BA + 1, ROW_BX, ROW_BX + 1, ROW_LAM, ROW_LAM + 1)):
            slab[row:row + 1, 0:LRU_W] = lru[2 * k][...] + lru[2 * k + 1][...]
        slab[ROW_CW:ROW_CW + 4, 0:LRU_W] = cw_a[...] + cw_b[...]
        for h in range(HEADS):
            slab[ROW_RD + h:ROW_RD + h + 1, 0:LANES] = drd_f[h, 0:1, :] + drd_c[h, 0:1, :]
            slab[ROW_RD + HEADS + h:ROW_RD + HEADS + h + 1, 0:LANES] = drd_b[h, 0:1, :] + drd_c[h, 1:2, :]
        for d, g_ref in enumerate((gf_ref, gb_ref)):
            for n in range(LRU_BLOCKS):
                blk = slice(LRU_BD * n, LRU_BD * (n + 1))
                ga[blk, LRU_BD * d:LRU_BD * (d + 1)] = g_ref[0, blk, blk].astype(BF16)
                gx[blk, LRU_BD * d:LRU_BD * (d + 1)] = g_ref[1, blk, blk].astype(BF16)

    args = list(rows) + list(drd) + list(cw2) + list(cb2) + list(lru2) + list(gates)
    gate_shape = (LRU_W, 2 * LRU_BD)
    return _pc(body, name="pack_small", in_specs=[_full(a.shape) for a in args],
               out_specs=[_full((SLAB_ROWS, D_MODEL)), _full(gate_shape), _full(gate_shape)],
               out_shape=[_sds((SLAB_ROWS, D_MODEL)), _sds(gate_shape, BF16), _sds(gate_shape, BF16)],
               compiler_params=_params())(*args)


def _adam_math(w, g, m, v):
    mn = ADAM_B1 * m + (1.0 - ADAM_B1) * g
    vn = ADAM_B2 * v + (1.0 - ADAM_B2) * (g * g)
    mh = mn / (1.0 - ADAM_B1 ** ADAM_STEP)
    vh = vn / (1.0 - ADAM_B2 ** ADAM_STEP)
    return -ADAM_LR * (mh / (jnp.sqrt(vh) + ADAM_EPS) + ADAM_WD * w), mn, vn


SMALL_PARAMS = ("b_ada", "norm1_g", "norm2_g", "final_g", "ret_decay", "conv_w", "conv_b", "lru_wa", "lru_ba", "lru_wx",
                "lru_bx", "lru_lambda")


def _finalize_small(chip_idx, slab_all, ga_all, gx_all, wmv):
    n_p = len(SMALL_PARAMS)
    flat = [a for nm in SMALL_PARAMS for a in wmv[nm]]
    ada_n = N_MOD * D_MODEL // N_CHIP

    def body(c_ref, slab_ref, ga_ref, gx_ref, *refs):
        prm = {nm: refs[3 * k:3 * k + 3] for k, nm in enumerate(SMALL_PARAMS)}
        outs = {nm: refs[3 * n_p + 4 * k:3 * n_p + 4 * k + 4] for k, nm in enumerate(SMALL_PARAMS)}
        b128_ref, dmc_ref, loss_ref = refs[3 * n_p + 4 * n_p:]
        chip = c_ref[0]

        def pick(fn):
            acc = fn(0)
            for j in range(1, N_CHIP):
                acc = jnp.where(chip == j, fn(j), acc)
            return acc

        tot = slab_ref[0]
        for d in range(1, N_DEV):
            tot = tot + slab_ref[d]

        def update(nm, g, sl=None, rows=None):
            w_ref, m_ref, v_ref = prm[nm]
            g_ref, d_ref, mo_ref, vo_ref = outs[nm]
            ix = (slice(None) if rows is None else rows, slice(None) if sl is None else sl)
            dl, mn, vn = _adam_math(w_ref[ix], g, m_ref[ix], v_ref[ix])
            g_ref[ix] = g
            d_ref[ix] = dl
            mo_ref[ix] = mn
            vo_ref[ix] = vn

        loss_ref[...] = jnp.broadcast_to(tot[ROW_LOSS:ROW_LOSS + 1, 0:LANES], (SUBLANES, LANES))
        for k in range(N_MOD):
            g = tot[ROW_DMOD + k:ROW_DMOD + k + 1, :]
            if k < 2:
                g = g + tot[ROW_DMODC + k:ROW_DMODC + k + 1, :]
            update("b_ada", g, slice(D_MODEL * k, D_MODEL * (k + 1)))
        update("norm1_g", tot[ROW_N1:ROW_N1 + 1, :])
        update("norm2_g", tot[ROW_N2:ROW_N2 + 1, :])
        update("final_g", tot[ROW_FG:ROW_FG + 1, :])
        update("ret_decay", tot[ROW_RD:ROW_RD + SUBLANES, 0:LANES])
        update("conv_b", tot[ROW_CB:ROW_CB + 1, 0:LRU_W])
        update("conv_w", pick(lambda j: tot[ROW_CW:ROW_CW + 4, LANES * j:LANES * (j + 1)]))
        for nm, row in (("lru_ba", ROW_BA), ("lru_bx", ROW_BX), ("lru_lambda", ROW_LAM)):
            update(nm, pick(lambda j, row=row: tot[row:row + 2, LANES * j:LANES * (j + 1)]))
        for nm, g_all in (("lru_wa", ga_ref), ("lru_wx", gx_ref)):
            for dr in range(2):
                lanes = slice(LRU_BD * dr, LRU_BD * (dr + 1))
                g = g_all[0, :, lanes].astype(F32)
                for d in range(1, N_DEV):
                    g = g + g_all[d, :, lanes].astype(F32)
                update(nm, g, rows=slice(LRU_W * dr, LRU_W * (dr + 1)))

        def seg(rows6, s):
            return rows6[s // 2][:, SEG * (s % 2):SEG * (s % 2 + 1)]

        b128_ref[...] = jnp.zeros_like(b128_ref)
        dmc_ref[...] = jnp.zeros_like(dmc_ref)
        zero = jnp.zeros((1, D_MODEL), F32)
        ctx6 = [tot[ROW_DMODC:ROW_DMODC + 1, :], tot[ROW_DMODC + 1:ROW_DMODC + 2, :]] + [zero] * (N_MOD - 2)
        for q in range(ada_n // SEG):
            cols = slice(SEG * q, SEG * (q + 1))
            for d in range(N_DEV):
                rows6 = [slab_ref[d, ROW_DMOD + k:ROW_DMOD + k + 1, :] for k in range(N_MOD)]
                b128_ref[d:d + 1, cols] = pick(lambda j, rows6=rows6: seg(rows6, 3 * j + q))
            c = pick(lambda j: seg(ctx6, 3 * j + q))
            b128_ref[N_DEV:N_DEV + 1, cols] = c
            dmc_ref[0:1, cols] = c

    out_shape = []
    for nm in SMALL_PARAMS:
        out_shape += [_sds(wmv[nm][0].shape)] * 4
    out_shape += [_sds((LANES, ada_n)), _sds((SUBLANES, ada_n)), _sds((SUBLANES, LANES))]
    args = [slab_all, ga_all, gx_all] + flat
    grid_spec = pltpu.PrefetchScalarGridSpec(
        num_scalar_prefetch=1, grid=(1,), in_specs=[_full(a.shape) for a in args],
        out_specs=[_full(s.shape) for s in out_shape])
    outs = _pc(body, name="finalize_small", grid_spec=grid_spec, out_shape=out_shape,
               compiler_params=_params("arbitrary"))(chip_idx, *args)
    res = {nm: tuple(outs[4 * k:4 * k + 4]) for k, nm in enumerate(SMALL_PARAMS)}
    return res, outs[4 * n_p], outs[4 * n_p + 1], outs[4 * n_p + 2]


def _block_diag(w):
    eye = jnp.eye(LRU_BLOCKS, dtype=F32)
    return (w[:, :, None, :] * eye[:, None, :, None]).reshape(LRU_W, LRU_W).astype(BF16)


def _lane_rep(v8):
    return jnp.broadcast_to(v8.reshape(SUBLANES, 1), (SUBLANES, LANES))


def kernel(x, c, ctx, c_ctx, w_ada, b_ada, norm1_g, norm2_g, w_in, ret_decay, conv_w, conv_b, lru_wa, lru_ba, lru_wx, lru_bx, lru_lambda, w_out, w_mlp1, w_mlp2, final_g, loss_target, m_c_ctx, m_w_ada, m_b_ada, m_norm1_g, m_norm2_g, m_w_in, m_ret_decay, m_conv_w, m_conv_b, m_lru_wa, m_lru_ba, m_lru_wx, m_lru_bx, m_lru_lambda, m_w_out, m_w_mlp1, m_w_mlp2, m_final_g, v_c_ctx, v_w_ada, v_b_ada, v_norm1_g, v_norm2_g, v_w_in, v_ret_decay, v_conv_w, v_conv_b, v_lru_wa, v_lru_ba, v_lru_wx, v_lru_bx, v_lru_lambda, v_w_out, v_w_mlp1, v_w_mlp2, v_final_g):
    ax, ay, ac = lax.axis_index("x"), lax.axis_index("y"), lax.axis_index("c")
    chip = 2 * ax + ay
    dev = 4 * ax + 2 * ay + ac
    c_idx = jnp.stack([ac, chip]).astype(jnp.int32)
    j_idx = chip.reshape(1).astype(jnp.int32)

    xt = x[0]
    t_len = xt.shape[0]
    ctxt = ctx[0]
    l_len = ctxt.shape[0]
    tgt = loss_target[0]
    ada_n = w_ada.shape[2]

    def my_half(w2d):
        r = w2d.shape[0] // 2
        return lax.dynamic_slice_in_dim(w2d, ac * r, r, axis=0).astype(BF16)

    pad8 = lambda a: jnp.pad(a, ((0, SUBLANES - a.shape[0]), (0, 0)))
    small = jnp.concatenate([pad8(conv_w[0]), pad8(lru_ba[0]), pad8(lru_bx[0]), pad8(lru_lambda[0])], axis=0)
    b_shard = lax.dynamic_slice_in_dim(b_ada, chip * ada_n, ada_n, axis=1)
    gw_in, _, small_all, a16, mod_parts, lgv, sgv = _head(
        my_half(w_in[0]), pad8(c), small, w_ada[0], b_shard, c_ctx, ret_decay[0])
    w4 = gw_in.reshape(N_CHIP, D_MODEL, IN_COLS // N_CHIP)

    mod_all = mod_parts[0::2].transpose(1, 0, 2).reshape(16, N_CHIP * ada_n)
    mod_me = lax.dynamic_slice_in_dim(mod_all, dev, 1, axis=0)
    sh1, sc1, g1, sh2, sc2, g2 = [mod_me[:, D_MODEL * k:D_MODEL * (k + 1)] for k in range(N_MOD)]
    csh1, csc1 = mod_all[8:9, 0:D_MODEL], mod_all[8:9, D_MODEL:2 * D_MODEL]

    cos2, sin2 = _rotary_tables(t_len)
    cos_c, sin_c = jnp.ones((l_len, DH), F32), jnp.zeros((l_len, DH), F32)
    n1g, n2g = norm1_g, norm2_g
    fg = final_g.reshape(1, D_MODEL)

    small_full = small_all[0::2].transpose(1, 0, 2).reshape(4 * SUBLANES, LRU_W)
    cw = small_full[0:4]
    cb = conv_b
    ba_f, ba_b = small_full[8:9], small_full[9:10]
    bx_f, bx_b = small_full[16:17], small_full[17:18]
    lam_f, lam_b = small_full[24:25], small_full[25:26]
    wa_f, wa_b = _block_diag(lru_wa[0, 0]), _block_diag(lru_wa[0, 1])
    wx_f, wx_b = _block_diag(lru_wx[0, 0]), _block_diag(lru_wx[0, 1])
    zero_h = jnp.zeros((1, LRU_W), F32)

    projc, xrc, hcb16 = _inproj_fwd(ctxt, n1g, csh1, csc1, w4, cos_c, sin_c, "inproj_fwd_ctx")
    s_f, s_b = _ctx_state_fwd(projc, lgv)
    xcc = _conv_fwd(xrc, cw, cb, "conv_fwd_ctx")
    par_f, par_b = (wa_f, wx_f, ba_f, bx_f, lam_f), (wa_b, wx_b, ba_b, bx_b, lam_b)
    hcf, hcbk = _lru_fwd(xcc, par_f, par_b, zero_h, zero_h, "lru_fwd_ctx")
    lru_sf, lru_sb = hcf[l_len - 1:l_len], hcbk[0:1]

    h1, h2 = my_half(w_mlp1[0]), my_half(w_mlp2[0])
    q = h1.shape[0] // 2
    (proj, xrl, hb16), ((gw_1a,),) = _inproj_fwd(xt, n1g, sh1, sc1, w4, cos2, sin2, "inproj_fwd",
                                           comms=(_AllGather([h1[:q]]),))
    (o_f, o_b, spf, spb), ((gw_1b, gw_out),) = _ret_fwd(proj, lgv, s_f, s_b,
                                                       comms=(_AllGather([h1[q:], my_half(w_out[0])]),))
    xcl = _conv_fwd(xrl, cw, cb, "conv_fwd")
    (hf, hbk), ((gw_2a,),) = _lru_fwd(xcl, par_f, par_b, lru_sf, lru_sb, "lru_fwd", comms=(_AllGather([h2[:q]]),))
    wo = gw_out.reshape(D_MODEL, D_MODEL)
    (x1, cat), ((gw_2b,),) = _mix_fwd(o_f, o_b, proj, hf, hbk, wo, xt, g1, comms=(_AllGather([h2[q:]]),))

    (dx1, h2b, ab, dub, dmb, dsc2, dsh2, dg2, dn2g, dfg, lossv) = _mlp(
        x1, n2g, sh2, sc2, g2, fg, (gw_1a, gw_1b), (gw_2a, gw_2b), tgt)
    gw_mlp1 = _tn(h2b, dub, N_CHIP, False, True, "grad_w_mlp1")
    b_1 = gw_mlp1.reshape(N_DEV, D_MODEL // 2, MLP_H // N_CHIP)
    gw_mlp2, ((r_1,),) = _tn(ab, dmb, N_CHIP, True, False, "grad_w_mlp2", comms=(_pair_exchange([b_1]),))

    half = D_MODEL // 4
    top, bot = (0, half), (half, half)
    b_2 = gw_mlp2.reshape(N_DEV, MLP_H // N_DEV, D_MODEL)
    p_1, pb_1 = _pair_add(b_1, r_1, c_idx, "rs_pair_add_w_mlp1")
    (do, dhs, dg, dgate, dyb, dg1), ((q_1a,), (r_2,)) = _mix_bwd(
        o_f, o_b, proj, hf, hbk, wo, cat, dx1, g1, comms=(_chip_exchange([pb_1], top), _pair_exchange([b_2])))
    gw_o = _tn(cat, dyb, 1, False, False, "grad_w_out")
    b_o = gw_o.reshape(N_DEV, D_MODEL // N_DEV, D_MODEL)
    p_2, pb_2 = _pair_add(b_2, r_2, c_idx, "rs_pair_add_w_mlp2")

    ((dq_f, dk_f, dv_f, ds_f, drd_f), (dq_b, dk_b, dv_b, ds_b, drd_b)), ((q_1b,), (q_2a,), (r_o,)) = _ret_bwd(
        proj, lgv, sgv, spf, spb, do,
        comms=(_chip_exchange([pb_1], bot), _chip_exchange([pb_2], top), _pair_exchange([b_o])))
    p_o, pb_o = _pair_add(b_o, r_o, c_idx, "rs_pair_add_w_out")
    h_1 = _chip_add(p_1, (q_1a, q_1b), c_idx, "rs_chip_add_w_mlp1")

    ((dxc_f, dpre_f, dba_f, dbx_f, dlam_f, dh0_f), (dxc_b, dpre_b, dba_b, dbx_b, dlam_b, dh0_b)), (
        (q_2b,), (q_o,), (f_1,)) = _lru_bwd(
        xcl, par_f, par_b, hf, hbk, lru_sf, lru_sb, dhs, dhs, "lru_bwd",
        comms=(_chip_exchange([pb_2], bot), _chip_exchange([pb_o]), _pair_gather([h_1])))
    h_2 = _chip_add(p_2, (q_2a, q_2b), c_idx, "rs_chip_add_w_mlp2")
    h_o = _chip_add(p_o, (q_o,), c_idx, "rs_chip_add_w_out")
    dxr, dcw, dcb = _conv_bwd(dxc_f, dxc_b, xrl, cw, "conv_bwd")
    grad_x, dpb, dn1g, dsh1, dsc1 = _inproj_bwd(
        xt, n1g, sh1, sc1, w4, cos2, sin2, [dq_f, dq_b, dk_f, dk_b, dv_f, dv_b, dg, dxr, dgate], dx1, "inproj_bwd")

    dkc, dvc, drd_c = _ctx_state_bwd(projc, lgv, sgv, ds_f, ds_b)
    zc = jnp.zeros((l_len, LRU_W), F32)
    dhc_f = lax.dynamic_update_slice(zc, dh0_f, (l_len - 1, 0))
    dhc_b = lax.dynamic_update_slice(zc, dh0_b, (0, 0))
    ((dxcc_f, dprec_f, dbac_f, dbxc_f, dlamc_f, _), (dxcc_b, dprec_b, dbac_b, dbxc_b, dlamc_b, _)), _ = _lru_bwd(
        xcc, par_f, par_b, hcf, hcbk, zero_h, zero_h, dhc_f, dhc_b, "lru_bwd_ctx")
    dxrc, dcw_c, dcb_c = _conv_bwd(dxcc_f, dxcc_b, xrc, cw, "conv_bwd_ctx")
    zr = jnp.zeros((l_len, RET_W), BF16)
    _, dpbc, dn1g_c, dcsh1, dcsc1 = _inproj_bwd(
        ctxt, n1g, csh1, csc1, w4, cos_c, sin_c, [zr, zr, dkc, zr, dvc, zr, zr, dxrc, zr],
        jnp.zeros((l_len, D_MODEL), F32), "inproj_bwd_ctx")

    gw_i = _tn(hb16, dpb, N_CHIP, False, True, "grad_w_in", extra=(hcb16, dpbc))
    b_i = gw_i.reshape(N_DEV, D_MODEL // 2, IN_COLS // N_CHIP)
    gwa_f, ((r_i,), (f_2,), (f_o,)) = _tn(xcl, dpre_f, 2, False, True, "grad_lru_gates_f", extra=(xcc, dprec_f),
                                          comms=(_pair_exchange([b_i]), _pair_gather([h_2]), _pair_gather([h_o])))
    p_i, pb_i = _pair_add(b_i, r_i, c_idx, "rs_pair_add_w_in")
    gwa_b, ((q_i,),) = _tn(xcl, dpre_b, 2, False, True, "grad_lru_gates_b", extra=(xcc, dprec_b),
                           comms=(_chip_exchange([pb_i]),))
    slab, ga, gx = _pack_small(
        [lossv, dsh1, dsc1, dg1, dsh2, dsc2, dg2, dcsh1, dcsc1, dn1g, dn1g_c, dn2g, dfg],
        (drd_f, drd_b, drd_c), (dcw, dcw_c), (dcb, dcb_c),
        (dba_f, dbac_f, dba_b, dbac_b, dbx_f, dbxc_f, dbx_b, dbxc_b, dlam_f, dlamc_f, dlam_b, dlamc_b),
        (gwa_f, gwa_b))
    (f_i,), (slab_all, ga_all, gx_all) = _run_comms(
        [_pair_gather([_chip_add(p_i, (q_i,), c_idx, "rs_chip_add_w_in")]), _AllGather([slab, ga, gx])],
        "tail_exchanges")
    g_in, g_out, g_1, g_2 = _shard_of(f_i), _shard_of(f_o), _shard_of(f_1), _shard_of(f_2)
    big = {}
    for nm, w, g, m, v in (("w_in", w_in, g_in, m_w_in, v_w_in), ("w_out", w_out, g_out, m_w_out, v_w_out),
                           ("w_mlp1", w_mlp1, g_1, m_w_mlp1, v_w_mlp1), ("w_mlp2", w_mlp2, g_2, m_w_mlp2, v_w_mlp2)):
        go, d_, mn, vn = _adamw(w[0], g, m[0], v[0], "adamw_" + nm)
        big[nm] = (go[None], d_[None], mn[None], vn[None])
    params = {
        "b_ada": (b_ada, m_b_ada, v_b_ada), "norm1_g": (norm1_g, m_norm1_g, v_norm1_g),
        "norm2_g": (norm2_g, m_norm2_g, v_norm2_g), "final_g": (final_g, m_final_g, v_final_g),
        "ret_decay": (ret_decay, m_ret_decay, v_ret_decay), "conv_w": (conv_w, m_conv_w, v_conv_w),
        "conv_b": (conv_b, m_conv_b, v_conv_b), "lru_wa": (lru_wa, m_lru_wa, v_lru_wa),
        "lru_ba": (lru_ba, m_lru_ba, v_lru_ba), "lru_wx": (lru_wx, m_lru_wx, v_lru_wx),
        "lru_bx": (lru_bx, m_lru_bx, v_lru_bx), "lru_lambda": (lru_lambda, m_lru_lambda, v_lru_lambda),
    }
    as2d = {
        "b_ada": lambda a: a, "norm1_g": lambda a: a, "norm2_g": lambda a: a, "conv_b": lambda a: a,
        "final_g": lambda a: a.reshape(1, D_MODEL), "ret_decay": lambda a: _lane_rep(a.reshape(-1)),
        "conv_w": lambda a: a[0], "lru_ba": lambda a: a[0], "lru_bx": lambda a: a[0], "lru_lambda": lambda a: a[0],
        "lru_wa": lambda a: a.reshape(2 * LRU_W, LRU_BD), "lru_wx": lambda a: a.reshape(2 * LRU_W, LRU_BD),
    }
    res, b128, dmc8, loss8 = _finalize_small(
        j_idx, slab_all, ga_all, gx_all, {nm: tuple(as2d[nm](a) for a in params[nm]) for nm in SMALL_PARAMS})
    loss = loss8[0, 0]
    small_out = {}
    for nm in SMALL_PARAMS:
        shp = params[nm][0].shape
        if nm == "ret_decay":
            small_out[nm] = tuple(o[:, 0].reshape(shp) for o in res[nm])
        else:
            small_out[nm] = tuple(o.reshape(shp) for o in res[nm])

    g_ada = _ada_grad(jnp.pad(a16.T, ((0, 0), (0, LANES - 16))), b128)
    g_ada, d_ada, m_ada, v_ada = _adamw(w_ada[0], g_ada, m_w_ada[0], v_w_ada[0], "adamw_w_ada")

    (cparts,) = _all_gather([_cctx_partial(dmc8, w_ada[0])], "gather_cctx")
    g_cc, d_cc, m_cc, v_cc = _cctx_final(cparts, c_ctx, m_c_ctx, v_c_ctx)
    small_out["c_ctx"] = tuple(a.reshape(D_MODEL) for a in (g_cc, d_cc, m_cc, v_cc))
    small_out["w_ada"] = (g_ada[None], d_ada[None], m_ada[None], v_ada[None])
    small_out.update(big)

    order = ["c_ctx", "w_ada", "b_ada", "norm1_g", "norm2_g", "w_in", "ret_decay", "conv_w", "conv_b", "lru_wa", "lru_ba",
             "lru_wx", "lru_bx", "lru_lambda", "w_out", "w_mlp1", "w_mlp2", "final_g"]
    outs = [loss, grad_x[None]]
    for k in range(4):
        outs += [small_out[nm][k] for nm in order]
    return tuple(outs)
```

```python
import math

import jax
import jax.numpy as jnp
from jax import lax
from jax.experimental import pallas as pl
from jax.experimental.pallas import tpu as pltpu

F32 = jnp.float32
BF16 = jnp.bfloat16

D_MODEL = 1024
HEADS = 4
DH = 128
CHUNK = 256
RET_W = HEADS * DH
LRU_W = 512
LRU_BLOCKS = 8
LRU_BD = LRU_W // LRU_BLOCKS
LRU_C = 8.0
IN_COLS = 4 * RET_W + 2 * LRU_W
MLP_H = 4 * D_MODEL
N_MOD = 6
GRID_W = 64
ROPE_BASE = 10000.0
K_SCALE = DH ** -0.5
EPS = 1e-6
GELU_K = math.sqrt(2.0 / math.pi)
GELU_C = 0.044715

ADAM_LR = 0.001
ADAM_B1 = 0.9
ADAM_B2 = 0.999
ADAM_EPS = 1e-08
ADAM_WD = 0.01
ADAM_STEP = 10

N_DEV = 8
N_CHIP = 4
SUBLANES = 8
LANES = 128
VMEM_LIMIT_V7X = 56 * 1024 * 1024
MESH = pl.DeviceIdType.MESH
ANY = pl.BlockSpec(memory_space=pl.ANY)


def _pc(body, **kw):
    return pl.pallas_call(body, **kw)


def _params(*sem):
    return pltpu.CompilerParams(dimension_semantics=sem if sem else None, vmem_limit_bytes=VMEM_LIMIT_V7X)


def _tile(t, big=False):
    if big and t >= 1024:
        return 512
    return 256 if t >= 256 else t


def _sds(shape, dtype=F32):
    return jax.ShapeDtypeStruct(tuple(shape), dtype)


def _full(shape):
    nd = len(shape)
    return pl.BlockSpec(tuple(shape), lambda *_: (0,) * nd)


def _sigmoid(x):
    return 0.5 * jnp.tanh(0.5 * x) + 0.5


def _log1p_pos(y):
    s = y * (1.0 - y * (0.5 - y * (1.0 / 3.0 - y * (0.25 - y * (0.2 - y / 6.0)))))
    return jnp.where(y < 0.03, s, jnp.log(1.0 + y))


def _softplus(z):
    return jnp.maximum(z, 0.0) + _log1p_pos(jnp.exp(-jnp.abs(z)))


def _one_minus_sq(la, a):
    return -jnp.tanh(la) * (1.0 + a * a)


def _rms(x):
    r = lax.rsqrt(jnp.mean(x * x, axis=-1, keepdims=True) + EPS)
    return x * r, r


def _dot(a, b):
    return jnp.dot(a, b, preferred_element_type=F32)


def _dot_nt(a, b):
    return lax.dot_general(a, b, (((1,), (1,)), ((), ())), preferred_element_type=F32)


def _dot_tn(a, b):
    return lax.dot_general(a, b, (((0,), (0,)), ((), ())), preferred_element_type=F32)


def _sum0(x):
    return jnp.sum(x, axis=0, keepdims=True)


def _norm_mod_bwd(x, g, sc, dh):
    xh, r = _rms(x)
    hn = xh * g
    dhn = dh * (1.0 + sc)
    dxh = dhn * g
    dx = r * (dxh - xh * jnp.mean(dxh * xh, axis=-1, keepdims=True))
    return dx, _sum0(dhn * xh), _sum0(dh), _sum0(dh * hn)


def _dev_index(p):
    return 4 * p[0] + 2 * p[1] + p[2]


def _mesh_pos():
    return lax.axis_index("x"), lax.axis_index("y"), lax.axis_index("c")


class _AllGather:
    def __init__(self, arrs):
        n = len(arrs)
        self.arrays = list(arrs)
        self.out_shapes = [_sds((N_DEV,) + a.shape, a.dtype) for a in arrs]
        self.scratch = ([pltpu.VMEM(a.shape, a.dtype) for a in arrs]
                        + [pltpu.SemaphoreType.DMA((7 * n,)), pltpu.SemaphoreType.DMA((7 * n,)),
                           pltpu.SemaphoreType.DMA((n,))])
        self.aliases = {}

    def _parts(self, ins, outs, scr):
        n = len(self.arrays)
        stage = scr[:n]
        send_sems, recv_sems, local_sems = scr[n:]
        x, y, c = _mesh_pos()
        me, sib = (x, y, c), (x, y, 1 - c)
        chips = [(1 - x, y), (x, 1 - y), (1 - x, 1 - y)]

        def copy(t, k, block, to, own=False):
            dst = outs[t].at[_dev_index(block)]
            return pltpu.make_async_remote_copy(
                src_ref=ins[t] if own else dst, dst_ref=dst,
                send_sem=send_sems.at[7 * t + k], recv_sem=recv_sems.at[7 * t + k],
                device_id=to, device_id_type=MESH)

        first = []
        for t in range(n):
            first.append(copy(t, 0, me, sib, own=True))
            for j, ch in enumerate(chips):
                first.append(copy(t, 1 + j, me, (*ch, c), own=True))
        stage_in = [pltpu.make_async_copy(ins[t], stage[t], local_sems.at[t]) for t in range(n)]
        mine = [pltpu.make_async_copy(stage[t], outs[t].at[_dev_index(me)], local_sems.at[t]) for t in range(n)]
        return n, c, me, sib, chips, copy, first, stage_in, mine

    def start(self, ins, outs, scr):
        n, _, _, _, _, _, first, stage_in, mine = self._parts(ins, outs, scr)
        for cp in stage_in:
            cp.start()
        for cp in first:
            cp.start()
        for t in range(n):
            stage_in[t].wait()
            mine[t].start()

    def relay(self, ins, outs, scr):
        n, c, me, sib, chips, copy, _, _, _ = self._parts(ins, outs, scr)
        for j, ch in enumerate(chips):
            for t in range(n):
                copy(t, 1 + j, (*ch, c), me).wait_recv()
                copy(t, 4 + j, (*ch, c), sib).start()

    def finish(self, ins, outs, scr):
        n, c, me, sib, chips, copy, first, _, mine = self._parts(ins, outs, scr)
        passed = [copy(t, 4 + j, (*ch, c), sib) for j, ch in enumerate(chips) for t in range(n)]
        for t in range(n):
            copy(t, 0, sib, me).wait_recv()
            for j, ch in enumerate(chips):
                copy(t, 4 + j, (*ch, 1 - c), me).wait_recv()
        for cp in first + passed:
            cp.wait_send()
        for cp in mine:
            cp.wait()


class _Exchange:
    def __init__(self, arrays, out_shapes, plan, n_copies, aliases=None):
        self.arrays = list(arrays)
        self.out_shapes = list(out_shapes)
        self.plan = plan
        self.scratch = [pltpu.SemaphoreType.DMA((n_copies,)), pltpu.SemaphoreType.DMA((n_copies,))]
        self.aliases = aliases or {}

    def _copies(self, ins, outs, scr):
        send_sems, recv_sems = scr
        snd, rcv = [], []
        for i, (src, dst, peer, lands) in enumerate(self.plan(ins, outs, _mesh_pos())):
            kw = dict(send_sem=send_sems.at[i], recv_sem=recv_sems.at[i], device_id=peer, device_id_type=MESH)
            snd.append(pltpu.make_async_remote_copy(src_ref=src, dst_ref=dst, **kw))
            rcv.append(pltpu.make_async_remote_copy(src_ref=src, dst_ref=lands, **kw))
        return snd, rcv

    def start(self, ins, outs, scr):
        for cp in self._copies(ins, outs, scr)[0]:
            cp.start()

    def relay(self, ins, outs, scr):
        pass

    def finish(self, ins, outs, scr):
        snd, rcv = self._copies(ins, outs, scr)
        for cp in rcv:
            cp.wait_recv()
        for cp in snd:
            cp.wait_send()


def _pair_exchange(grads):
    n = len(grads)

    def plan(ins, outs, pos):
        x, y, c = pos
        return [(ins[t].at[2 * j + (1 - c)], outs[t].at[j], (x, y, 1 - c), outs[t].at[j])
                for t in range(n) for j in range(N_CHIP)]

    return _Exchange(grads, [_sds((N_CHIP,) + g.shape[1:], g.dtype) for g in grads], plan, N_CHIP * n)


def _chip_exchange(parts, rows=None):
    n = len(parts)

    def plan(ins, outs, pos):
        x, y, c = pos
        chips = [(1 - x, y), (x, 1 - y), (1 - x, 1 - y)]

        def src(t, ch):
            blk = ins[t].at[2 * ch[0] + ch[1]]
            return blk if rows is None else blk.at[pl.ds(rows[0], rows[1])]

        return [(src(t, ch), outs[t].at[k], (*ch, c), outs[t].at[k]) for t in range(n) for k, ch in enumerate(chips)]

    shapes = [_sds((3, p.shape[1] if rows is None else rows[1]) + p.shape[2:], p.dtype) for p in parts]
    return _Exchange(parts, shapes, plan, 3 * n)


def _pair_gather(bufs):
    n = len(bufs)

    def plan(ins, outs, pos):
        x, y, c = pos
        return [(ins[t].at[c], outs[t].at[c], (x, y, 1 - c), outs[t].at[1 - c]) for t in range(n)]

    return _Exchange(bufs, [_sds(b.shape, b.dtype) for b in bufs], plan, n, aliases={t: t for t in range(n)})


def _run_comms(comms, name):
    c_in = [len(cm.arrays) for cm in comms]
    c_out = [len(cm.out_shapes) for cm in comms]
    c_scr = [len(cm.scratch) for cm in comms]
    aliases = {}
    for k, cm in enumerate(comms):
        for a, b in cm.aliases.items():
            aliases[sum(c_in[:k]) + a] = sum(c_out[:k]) + b

    def split(refs, counts):
        out, pos = [], 0
        for cnt in counts:
            out.append(refs[pos:pos + cnt])
            pos += cnt
        return out

    def body(*refs):
        ins = split(refs[:sum(c_in)], c_in)
        outs = split(refs[sum(c_in):sum(c_in) + sum(c_out)], c_out)
        scr = split(refs[sum(c_in) + sum(c_out):], c_scr)
        for phase in ("start", "relay", "finish"):
            for k, cm in enumerate(comms):
                getattr(cm, phase)(ins[k], outs[k], scr[k])

    outs = _pc(body, name=name, out_shape=[s for cm in comms for s in cm.out_shapes],
               in_specs=[ANY] * sum(c_in), out_specs=[ANY] * sum(c_out), input_output_aliases=aliases,
               scratch_shapes=[s for cm in comms for s in cm.scratch],
               compiler_params=_params())(*[a for cm in comms for a in cm.arrays])
    return split(list(outs), c_out)


def _all_gather(arrs, name):
    return _run_comms([_AllGather(arrs)], name)[0]


def _call(body, *, name, grid, in_specs, out_specs, out_shape, scratch_shapes, sem, args, comms=()):
    n_in, n_out, n_scr = len(in_specs), len(out_specs), len(scratch_shapes)
    c_in = [len(cm.arrays) for cm in comms]
    c_out = [len(cm.out_shapes) for cm in comms]
    c_scr = [len(cm.scratch) for cm in comms]
    aliases = {}
    for k, cm in enumerate(comms):
        for a, b in cm.aliases.items():
            aliases[n_in + sum(c_in[:k]) + a] = n_out + sum(c_out[:k]) + b

    def split(refs, counts):
        out, pos = [], 0
        for cnt in counts:
            out.append(refs[pos:pos + cnt])
            pos += cnt
        return out

    def wrapped(*refs):
        ins = refs[:n_in + sum(c_in)]
        outs = refs[len(ins):len(ins) + n_out + sum(c_out)]
        scr = refs[len(ins) + len(outs):]
        cins, couts, cscr = split(ins[n_in:], c_in), split(outs[n_out:], c_out), split(scr[n_scr:], c_scr)
        if comms:
            first = pl.program_id(0) == 0
            last = pl.program_id(0) == grid[0] - 1
            for k in range(1, len(grid)):
                first = jnp.logical_and(first, pl.program_id(k) == 0)
                last = jnp.logical_and(last, pl.program_id(k) == grid[k] - 1)

            @pl.when(first)
            def _():
                for k, cm in enumerate(comms):
                    cm.start(cins[k], couts[k], cscr[k])
        body(*ins[:n_in], *outs[:n_out], *scr[:n_scr])
        if comms:
            relay_early = len(grid) == 1 and grid[0] >= 4
            if relay_early:
                @pl.when(pl.program_id(0) == (7 * grid[0]) // 8 - 1)
                def _():
                    for k, cm in enumerate(comms):
                        cm.relay(cins[k], couts[k], cscr[k])

            @pl.when(last)
            def _():
                for k, cm in enumerate(comms):
                    if not relay_early:
                        cm.relay(cins[k], couts[k], cscr[k])
                    cm.finish(cins[k], couts[k], cscr[k])

    outs = _pc(wrapped, name=name, grid=grid,
               in_specs=list(in_specs) + [ANY] * sum(c_in), out_specs=list(out_specs) + [ANY] * sum(c_out),
               out_shape=list(out_shape) + [s for cm in comms for s in cm.out_shapes],
               scratch_shapes=list(scratch_shapes) + [s for cm in comms for s in cm.scratch],
               input_output_aliases=aliases, compiler_params=_params(*sem),
               )(*args, *[a for cm in comms for a in cm.arrays])
    outs = list(outs)
    return outs[:n_out], split(outs[n_out:], c_out)


def _row_block(r):
    for b in (512, 256, 128, 64, 32, 16, 8):
        if r % b == 0:
            return b
    return r


def _pair_add(g, recv, cj_idx, name):
    _, r, cc = g.shape
    br = _row_block(r)

    def body(cj_ref, g_ref, r_ref, own_ref, pb_ref):
        s = g_ref[...] + r_ref[...]
        pb_ref[...] = s.astype(BF16)

        @pl.when(pl.program_id(1) == cj_ref[1])
        def _():
            own_ref[...] = s[0]

    grid_spec = pltpu.PrefetchScalarGridSpec(
        num_scalar_prefetch=1, grid=(r // br, N_CHIP),
        in_specs=[pl.BlockSpec((1, br, cc), lambda i, j, cj_ref: (2 * j + cj_ref[0], i, 0)),
                  pl.BlockSpec((1, br, cc), lambda i, j, cj_ref: (j, i, 0))],
        out_specs=[pl.BlockSpec((br, cc), lambda i, j, cj_ref: (i, 0)),
                   pl.BlockSpec((1, br, cc), lambda i, j, cj_ref: (j, i, 0))])
    return _pc(body, name=name, grid_spec=grid_spec,
               out_shape=[_sds((r, cc)), _sds((N_CHIP, r, cc), BF16)],
               compiler_params=_params("arbitrary", "arbitrary"))(cj_idx, g, recv)


def _chip_add(p, qs, cj_idx, name):
    r, cc = p.shape
    nq = len(qs)
    br = _row_block(r // nq)
    nb = r // nq // br

    def body(cj_ref, p_ref, *refs):
        o_ref = refs[-1]
        if nq == 2:
            top = pl.program_id(0) < nb
            q = [jnp.where(top, refs[0][k], refs[1][k]).astype(F32) for k in range(3)]
        else:
            q = [refs[0][k].astype(F32) for k in range(3)]
        o_ref[0] = ((p_ref[...] + q[0]) + q[1]) + q[2]

    q_specs = [pl.BlockSpec((3, br, cc), lambda i, cj_ref, h=h: (0, jnp.clip(i - h * nb, 0, nb - 1), 0))
               for h in range(nq)]
    grid_spec = pltpu.PrefetchScalarGridSpec(
        num_scalar_prefetch=1, grid=(r // br,),
        in_specs=[pl.BlockSpec((br, cc), lambda i, cj_ref: (i, 0))] + q_specs,
        out_specs=pl.BlockSpec((1, br, cc), lambda i, cj_ref: (cj_ref[0], i, 0)))
    return _pc(body, name=name, grid_spec=grid_spec, out_shape=_sds((2, r, cc)),
               compiler_params=_params("arbitrary"))(cj_idx, p, *qs)


def _shard_of(both):
    return both.reshape((2 * both.shape[1],) + both.shape[2:])


ADAMW_CHUNKS = 8


def _adamw(w, g, m, v, name):
    r, cc = w.shape
    rows = r // ADAMW_CHUNKS
    assert rows * ADAMW_CHUNKS == r and rows % SUBLANES == 0
    c1 = 1.0 - ADAM_B1 ** ADAM_STEP
    c2 = 1.0 - ADAM_B2 ** ADAM_STEP

    def body(w_hbm, g_hbm, m_hbm, v_hbm, go_hbm, d_hbm, mo_hbm, vo_hbm, wb, gb, mb, vb, sem_in, sem_out):
        srcs, bufs, dsts = (w_hbm, g_hbm, m_hbm, v_hbm), (wb, gb, mb, vb), (d_hbm, go_hbm, mo_hbm, vo_hbm)

        def load(a, k):
            sl = pl.ds(k * rows, rows)
            return pltpu.make_async_copy(srcs[a].at[sl], bufs[a].at[sl], sem_in.at[a, k])

        def store(a, k):
            sl = pl.ds(k * rows, rows)
            return pltpu.make_async_copy(bufs[a].at[sl], dsts[a].at[sl], sem_out.at[a, k])

        for k in range(ADAMW_CHUNKS):
            for a in range(4):
                load(a, k).start()
        for k in range(ADAMW_CHUNKS):
            for a in range(4):
                load(a, k).wait()
            store(1, k).start()
            sl = pl.ds(k * rows, rows)
            gg = gb[sl]
            mn = ADAM_B1 * mb[sl] + (1.0 - ADAM_B1) * gg
            vn = ADAM_B2 * vb[sl] + (1.0 - ADAM_B2) * (gg * gg)
            mh = mn / c1
            vh = vn / c2
            wb[sl] = -ADAM_LR * (mh / (jnp.sqrt(vh) + ADAM_EPS) + ADAM_WD * wb[sl])
            mb[sl] = mn
            vb[sl] = vn
            for a in (0, 2, 3):
                store(a, k).start()
        for k in range(ADAMW_CHUNKS):
            for a in range(4):
                store(a, k).wait()

    go, d, mo, vo = _pc(body, name=name, in_specs=[ANY] * 4, out_specs=[ANY] * 4, out_shape=[_sds((r, cc))] * 4,
                        scratch_shapes=[pltpu.VMEM((r, cc), F32)] * 4 + [pltpu.SemaphoreType.DMA((4, ADAMW_CHUNKS))] * 2,
                        compiler_params=_params())(w, g, m, v)
    return go, d, mo, vo


def _head(w_half, c8, small, w_ada, b_shard, c_ctx, ret_decay):
    ada_n = w_ada.shape[1]
    mod_sds = _sds((16, ada_n))
    ag_w, ag_c, ag_m = _AllGather([w_half]), _AllGather([c8, small]), _AllGather([mod_sds])
    n_w, n_c, n_m = len(ag_w.scratch), len(ag_c.scratch), len(ag_m.scratch)

    def body(w_ref, c_ref, s_ref, wada_ref, b_ref, cc_ref, rd_ref,
             gw_ref, call_ref, sall_ref, a_ref, modp_ref, mall_ref, lg_ref, sg_ref, *scr):
        scr_w, scr_c, scr_m = scr[:n_w], scr[n_w:n_w + n_c], scr[n_w + n_c:n_w + n_c + n_m]
        c_v, w_v, m_v, sems = scr[n_w + n_c + n_m:]
        ag_w.start((w_ref,), (gw_ref,), scr_w)
        ag_c.start((c_ref, s_ref), (call_ref, sall_ref), scr_c)
        load_w = pltpu.make_async_copy(wada_ref, w_v, sems.at[0])
        load_w.start()
        rd = rd_ref[...]
        lg_ref[...] = -_softplus(-rd)
        sg_ref[...] = _sigmoid(-rd)
        ag_c.relay((c_ref, s_ref), (call_ref, sall_ref), scr_c)
        ag_c.finish((c_ref, s_ref), (call_ref, sall_ref), scr_c)
        load_c = pltpu.make_async_copy(call_ref, c_v, sems.at[1])
        load_c.start()
        load_c.wait()
        a_ref[...] = jnp.zeros_like(a_ref)
        for d in range(N_DEV):
            cd = c_v[d, 0:1, :]
            a_ref[d:d + 1, :] = cd * _sigmoid(cd)
        cc = cc_ref[...]
        a_ref[N_DEV:N_DEV + 1, :] = cc * _sigmoid(cc)
        load_w.wait()
        m_v[...] = jnp.dot(a_ref[...], w_v[...], preferred_element_type=F32,
                           precision=lax.Precision.HIGHEST) + b_ref[...]
        put = pltpu.make_async_copy(m_v, modp_ref, sems.at[2])
        put.start()
        put.wait()
        ag_m.start((modp_ref,), (mall_ref,), scr_m)
        ag_m.relay((modp_ref,), (mall_ref,), scr_m)
        ag_m.finish((modp_ref,), (mall_ref,), scr_m)
        ag_w.relay((w_ref,), (gw_ref,), scr_w)
        ag_w.finish((w_ref,), (gw_ref,), scr_w)

    rd = jnp.broadcast_to(ret_decay.reshape(2, HEADS).T[:, :, None], (HEADS, 2, LANES))
    lane = _full((HEADS, 2, LANES))
    outs = _pc(
        body, name="head",
        in_specs=[ANY, ANY, ANY, ANY, _full((1, ada_n)), _full((1, D_MODEL)), lane],
        out_specs=[ANY, ANY, ANY, _full((16, D_MODEL)), ANY, ANY, lane, lane],
        out_shape=ag_w.out_shapes + ag_c.out_shapes + [_sds((16, D_MODEL)), mod_sds] + ag_m.out_shapes
        + [_sds((HEADS, 2, LANES))] * 2,
        scratch_shapes=ag_w.scratch + ag_c.scratch + ag_m.scratch
        + [pltpu.VMEM((N_DEV,) + c8.shape, F32), pltpu.VMEM(w_ada.shape, F32), pltpu.VMEM((16, ada_n), F32),
           pltpu.SemaphoreType.DMA((3,))],
        compiler_params=_params(),
    )(w_half, c8, small, w_ada, b_shard, c_ctx.reshape(1, D_MODEL), rd)
    gw, c_all, small_all, a16, _, mod_all, lgv, sgv = outs
    return gw, c_all, small_all, a16, mod_all, lgv, sgv


def _ada_grad(at, b):
    n = b.shape[1]
    bn = 512

    def body(a_ref, b_ref, o_ref):
        o_ref[...] = jnp.dot(a_ref[...], b_ref[...], preferred_element_type=F32, precision=lax.Precision.HIGHEST)

    return _pc(body, name="ada_grad", grid=(n // bn,),
               in_specs=[_full((D_MODEL, LANES)), pl.BlockSpec((LANES, bn), lambda i: (0, i))],
               out_specs=pl.BlockSpec((D_MODEL, bn), lambda i: (0, i)), out_shape=_sds((D_MODEL, n)),
               compiler_params=_params("arbitrary"))(at, b)


def _cctx_partial(dmc8, w_ada):
    n = w_ada.shape[1]
    bn = 512

    def body(d_ref, w_ref, o_ref):
        @pl.when(pl.program_id(0) == 0)
        def _():
            o_ref[...] = jnp.zeros_like(o_ref)
        o_ref[...] += lax.dot_general(d_ref[...], w_ref[...], (((1,), (1,)), ((), ())),
                                      preferred_element_type=F32, precision=lax.Precision.HIGHEST)

    return _pc(body, name="cctx_partial", grid=(n // bn,),
               in_specs=[pl.BlockSpec((8, bn), lambda i: (0, i)), pl.BlockSpec((D_MODEL, bn), lambda i: (0, i))],
               out_specs=_full((8, D_MODEL)), out_shape=_sds((8, D_MODEL)),
               compiler_params=_params("arbitrary"))(dmc8, w_ada)


def _cctx_final(parts, c_ctx, m, v):
    c1 = 1.0 - ADAM_B1 ** ADAM_STEP
    c2 = 1.0 - ADAM_B2 ** ADAM_STEP

    def body(p_ref, c_ref, m_ref, v_ref, g_ref, d_ref, mo_ref, vo_ref):
        s = ((p_ref[0, 0:1, :] + p_ref[2, 0:1, :]) + p_ref[4, 0:1, :]) + p_ref[6, 0:1, :]
        z = c_ref[...]
        sg = _sigmoid(z)
        gg = s * (sg * (1.0 + z * (1.0 - sg)))
        g_ref[...] = gg
        mn = ADAM_B1 * m_ref[...] + (1.0 - ADAM_B1) * gg
        vn = ADAM_B2 * v_ref[...] + (1.0 - ADAM_B2) * (gg * gg)
        d_ref[...] = -ADAM_LR * ((mn / c1) / (jnp.sqrt(vn / c2) + ADAM_EPS) + ADAM_WD * z)
        mo_ref[...] = mn
        vo_ref[...] = vn

    row = _full((1, D_MODEL))
    return _pc(body, name="cctx_final", out_shape=[_sds((1, D_MODEL))] * 4,
               in_specs=[_full(parts.shape), row, row, row], out_specs=[row] * 4,
               compiler_params=_params())(parts, c_ctx.reshape(1, D_MODEL), m.reshape(1, D_MODEL), v.reshape(1, D_MODEL))


def _rotary_tables(t_len):
    rows = t_len // GRID_W
    n_freq = DH // 4
    inv = ROPE_BASE ** (-jnp.arange(n_freq, dtype=F32) / n_freq)
    row_ang = jnp.arange(rows, dtype=F32)[:, None] * inv
    col_ang = jnp.arange(GRID_W, dtype=F32)[:, None] * inv

    def spread(fn):
        return jnp.concatenate([jnp.repeat(fn(row_ang), GRID_W, axis=0), jnp.tile(fn(col_ang), (rows, 1))], axis=-1)

    cos, sin = spread(jnp.cos), spread(jnp.sin)
    return jnp.concatenate([cos, cos], axis=-1), jnp.concatenate([-sin, sin], axis=-1)


def _inproj_fwd(x, gn, sh, sc, w4, cos2, sin2, name, comms=()):
    t = x.shape[0]
    tm = _tile(t, True)
    nc = IN_COLS // N_CHIP

    def body(x_ref, gn_ref, sh_ref, sc_ref, w_ref, c_ref, s_ref, p_ref, xr_ref, hb_ref, p_s):
        xh, _ = _rms(x_ref[...])
        h = xh * gn_ref[...] * (1.0 + sc_ref[...]) + sh_ref[...]
        hb = h.astype(BF16)
        hb_ref[...] = hb
        for j in range(N_CHIP):
            p_s[:, nc * j:nc * (j + 1)] = _dot(hb, w_ref[j])
        cc = c_ref[...]
        ss = s_ref[...]
        for hh in range(2 * HEADS):
            blk = p_s[:, DH * hh:DH * (hh + 1)]
            rot = blk * cc + pltpu.roll(blk, DH // 2, 1) * ss
            if hh >= HEADS:
                rot = rot * K_SCALE
            p_ref[:, DH * hh:DH * (hh + 1)] = rot.astype(BF16)
        p_ref[:, 2 * RET_W:] = p_s[:, 2 * RET_W:].astype(BF16)
        xr_ref[...] = p_s[:, 4 * RET_W:4 * RET_W + LRU_W]

    row = _full((1, D_MODEL))
    outs, couts = _call(
        body, name=name, grid=(t // tm,),
        in_specs=[pl.BlockSpec((tm, D_MODEL), lambda i: (i, 0)), row, row, row, _full(w4.shape),
                  pl.BlockSpec((tm, DH), lambda i: (i, 0)), pl.BlockSpec((tm, DH), lambda i: (i, 0))],
        out_specs=[pl.BlockSpec((tm, IN_COLS), lambda i: (i, 0)), pl.BlockSpec((tm, LRU_W), lambda i: (i, 0)),
                   pl.BlockSpec((tm, D_MODEL), lambda i: (i, 0))],
        out_shape=[_sds((t, IN_COLS), BF16), _sds((t, LRU_W)), _sds((t, D_MODEL), BF16)],
        scratch_shapes=[pltpu.VMEM((tm, IN_COLS), F32)], sem=("arbitrary",),
        args=(x, gn, sh, sc, w4, cos2, sin2), comms=comms)
    return (outs, couts) if comms else outs


def _inproj_bwd(x, gn, sh, sc, w4, cos2, sin2, pieces, dres, name):
    t = x.shape[0]
    tm = _tile(t)
    nc = IN_COLS // N_CHIP

    def body(x_ref, gn_ref, sh_ref, sc_ref, w_ref, c_ref, s_ref, dqf, dqb, dkf, dkb, dvf, dvb, dg, dxr, dgt, dres_ref,
             dx_ref, dpb_ref, dgn_ref, dsh_ref, dsc_ref):
        cc = c_ref[...]
        ss = s_ref[...]
        dq = dqf[...].astype(F32) + dqb[...].astype(F32)
        dk = dkf[...].astype(F32) + dkb[...].astype(F32)
        for hh in range(HEADS):
            sl = slice(DH * hh, DH * (hh + 1))
            b = dq[:, sl]
            dpb_ref[:, sl] = (b * cc + pltpu.roll(b * ss, DH // 2, 1)).astype(BF16)
            b = dk[:, sl]
            dpb_ref[:, RET_W + DH * hh:RET_W + DH * (hh + 1)] = (
                (b * cc + pltpu.roll(b * ss, DH // 2, 1)) * K_SCALE).astype(BF16)
        dpb_ref[:, 2 * RET_W:3 * RET_W] = (dvf[...].astype(F32) + dvb[...].astype(F32)).astype(BF16)
        dpb_ref[:, 3 * RET_W:4 * RET_W] = dg[...].astype(BF16)
        dpb_ref[:, 4 * RET_W:4 * RET_W + LRU_W] = dxr[...].astype(BF16)
        dpb_ref[:, 4 * RET_W + LRU_W:IN_COLS] = dgt[...].astype(BF16)
        dh = _dot_nt(dpb_ref[:, 0:nc], w_ref[0])
        for j in range(1, N_CHIP):
            dh = dh + _dot_nt(dpb_ref[:, nc * j:nc * (j + 1)], w_ref[j])
        dx, dgn_t, dsh_t, dsc_t = _norm_mod_bwd(x_ref[...], gn_ref[...], sc_ref[...], dh)
        dx_ref[...] = dres_ref[...] + dx

        @pl.when(pl.program_id(0) == 0)
        def _():
            dgn_ref[...] = jnp.zeros_like(dgn_ref)
            dsh_ref[...] = jnp.zeros_like(dsh_ref)
            dsc_ref[...] = jnp.zeros_like(dsc_ref)
        dgn_ref[...] += dgn_t
        dsh_ref[...] += dsh_t
        dsc_ref[...] += dsc_t

    row = _full((1, D_MODEL))
    pc = pl.BlockSpec((tm, RET_W), lambda i: (i, 0))
    big = pl.BlockSpec((tm, D_MODEL), lambda i: (i, 0))
    return _pc(body, name=name, grid=(t // tm,),
               in_specs=[big, row, row, row, _full(w4.shape),
                         pl.BlockSpec((tm, DH), lambda i: (i, 0)), pl.BlockSpec((tm, DH), lambda i: (i, 0))]
               + [pc] * 9 + [big],
               out_specs=[big, pl.BlockSpec((tm, IN_COLS), lambda i: (i, 0)), row, row, row],
               out_shape=[_sds((t, D_MODEL)), _sds((t, IN_COLS), BF16), _sds((1, D_MODEL)), _sds((1, D_MODEL)),
                          _sds((1, D_MODEL))],
               compiler_params=_params("arbitrary"))(x, gn, sh, sc, w4, cos2, sin2, *pieces, dres)


def _halo_specs(t, tm):
    n8 = tm // SUBLANES
    last8 = t // SUBLANES - 1
    prev = pl.BlockSpec((SUBLANES, LRU_W), lambda i: (jnp.maximum(i * n8 - 1, 0), 0))
    main = pl.BlockSpec((tm, LRU_W), lambda i: (i, 0))
    nxt = pl.BlockSpec((SUBLANES, LRU_W), lambda i: (jnp.minimum((i + 1) * n8, last8), 0))
    return prev, main, nxt


def _with_halo(prev_ref, main_ref, next_ref, i, nt):
    prev = jnp.where(i > 0, prev_ref[...], 0.0)
    nxt = jnp.where(i < nt - 1, next_ref[...], 0.0)
    return jnp.concatenate([prev, main_ref[...], nxt], axis=0)


def _conv_fwd(xr, cw, cb, name):
    t = xr.shape[0]
    tm = _tile(t, True)
    nt = t // tm
    n = tm + 2 * SUBLANES
    mid = slice(SUBLANES, SUBLANES + tm)

    def body(p_ref, m_ref, n_ref, w_ref, b_ref, o_ref):
        xp = _with_halo(p_ref, m_ref, n_ref, pl.program_id(0), nt)
        acc = b_ref[...] + pltpu.roll(xp, 1, 0)[mid] * w_ref[0:1, :]
        acc = acc + xp[mid] * w_ref[1:2, :]
        acc = acc + pltpu.roll(xp, n - 1, 0)[mid] * w_ref[2:3, :]
        acc = acc + pltpu.roll(xp, n - 2, 0)[mid] * w_ref[3:4, :]
        o_ref[...] = acc

    return _pc(body, name=name, grid=(nt,),
               in_specs=[*_halo_specs(t, tm), _full((4, LRU_W)), _full((1, LRU_W))],
               out_specs=pl.BlockSpec((tm, LRU_W), lambda i: (i, 0)), out_shape=_sds((t, LRU_W)),
               compiler_params=_params("arbitrary"))(xr, xr, xr, cw, cb)


def _conv_bwd(dxc_a, dxc_b, xr, cw, name):
    t = xr.shape[0]
    tm = _tile(t, True)
    nt = t // tm
    n = tm + 2 * SUBLANES
    mid = slice(SUBLANES, SUBLANES + tm)

    def body(ap_ref, am_ref, an_ref, bp_ref, bm_ref, bn_ref, xp_ref, xm_ref, xn_ref, w_ref, dx_ref, dw_ref, db_ref):
        i = pl.program_id(0)
        dp = _with_halo(ap_ref, am_ref, an_ref, i, nt) + _with_halo(bp_ref, bm_ref, bn_ref, i, nt)
        xp = _with_halo(xp_ref, xm_ref, xn_ref, i, nt)
        dx = pltpu.roll(dp, n - 1, 0)[mid] * w_ref[0:1, :]
        dx = dx + dp[mid] * w_ref[1:2, :]
        dx = dx + pltpu.roll(dp, 1, 0)[mid] * w_ref[2:3, :]
        dx = dx + pltpu.roll(dp, 2, 0)[mid] * w_ref[3:4, :]
        dx_ref[...] = dx.astype(BF16)
        d = dp[mid]

        @pl.when(i == 0)
        def _():
            dw_ref[...] = jnp.zeros_like(dw_ref)
            db_ref[...] = jnp.zeros_like(db_ref)
        dw_ref[0:1, :] += _sum0(d * pltpu.roll(xp, 1, 0)[mid])
        dw_ref[1:2, :] += _sum0(d * xp[mid])
        dw_ref[2:3, :] += _sum0(d * pltpu.roll(xp, n - 1, 0)[mid])
        dw_ref[3:4, :] += _sum0(d * pltpu.roll(xp, n - 2, 0)[mid])
        db_ref[...] += _sum0(d)

    return _pc(body, name=name, grid=(nt,),
               in_specs=[*_halo_specs(t, tm), *_halo_specs(t, tm), *_halo_specs(t, tm), _full((4, LRU_W))],
               out_specs=[pl.BlockSpec((tm, LRU_W), lambda i: (i, 0)), _full((4, LRU_W)), _full((1, LRU_W))],
               out_shape=[_sds((t, LRU_W), BF16), _sds((4, LRU_W)), _sds((1, LRU_W))],
               compiler_params=_params("arbitrary"))(dxc_a, dxc_a, dxc_a, dxc_b, dxc_b, dxc_b, xr, xr, xr, cw)


def _scan_scratch(n, c):
    return pltpu.VMEM((c // LANES, n, LANES), F32)


def _to_lane_blocks(ref, val):
    for lb in range(ref.shape[0]):
        ref[lb] = val[:, lb * LANES:(lb + 1) * LANES]


def _group_scan(a_s, b_s, reverse):
    nb, n, _ = a_s.shape
    ng = n // SUBLANES
    order = range(SUBLANES - 1, -1, -1) if reverse else range(SUBLANES)
    for lb in range(nb):
        prev = None
        for r in order:
            rows = pl.ds(r, ng, stride=SUBLANES)
            a_r, b_r = a_s[lb, rows, :], b_s[lb, rows, :]
            if prev is not None:
                b_r = a_r * prev[1] + b_r
                a_r = a_r * prev[0]
                a_s[lb, rows, :] = a_r
                b_s[lb, rows, :] = b_r
            prev = (a_r, b_r)


def _carry_scans(jobs):
    nb, n, _ = jobs[0][0].shape
    ng = n // SUBLANES

    def step(g, all_crs):
        res = []
        for (a_s, b_s, out_ref, _, reverse), crs in zip(jobs, all_crs):
            gg = (ng - 1 - g) if reverse else g
            off = pl.multiple_of(gg * SUBLANES, SUBLANES)
            new = []
            for lb in range(nb):
                h = a_s[lb, pl.ds(off, SUBLANES), :] * crs[lb] + b_s[lb, pl.ds(off, SUBLANES), :]
                out_ref[pl.ds(off, SUBLANES), pl.ds(lb * LANES, LANES)] = h
                edge = h[0:1, :] if reverse else h[SUBLANES - 1:SUBLANES, :]
                new.append(jnp.broadcast_to(edge, (SUBLANES, LANES)))
            res.append(tuple(new))
        return tuple(res)

    init = tuple(tuple(job[3][:, lb * LANES:(lb + 1) * LANES] for lb in range(nb)) for job in jobs)
    return [jnp.concatenate(crs, axis=1) for crs in lax.fori_loop(0, ng, step, init)]


def _lru_gates(xc, wa_ref, wx_ref, ba, bx, lam):
    xb = xc.astype(BF16)
    r = _sigmoid(_dot(xb, wa_ref[...]) + ba)
    ig = _sigmoid(_dot(xb, wx_ref[...]) + bx)
    sp = _softplus(-lam)
    la = -LRU_C * r * sp
    a = jnp.exp(la)
    return r, ig, sp, a, _one_minus_sq(la, a)


def _lru_fwd(xc, par_f, par_b, h0_f, h0_b, name, comms=()):
    t = xc.shape[0]
    tm = _tile(t, True)
    nt = t // tm

    def one(x_ref, prm, h0_ref, a_s, b_s, c_s, reverse):
        wa_ref, wx_ref, ba_ref, bx_ref, lam_ref = prm

        @pl.when(pl.program_id(0) == 0)
        def _():
            c_s[...] = jnp.broadcast_to(h0_ref[...], c_s.shape)
        xv = x_ref[...]
        _, ig, _, a, q = _lru_gates(xv, wa_ref, wx_ref, ba_ref[...], bx_ref[...], lam_ref[...])
        _to_lane_blocks(a_s, a)
        _to_lane_blocks(b_s, jnp.sqrt(q) * (ig * xv))
        _group_scan(a_s, b_s, reverse)

    def body(xf_ref, xb_ref, *refs):
        prm_f, prm_b = refs[0:5], refs[5:10]
        h0f_ref, h0b_ref, hf_ref, hb_ref = refs[10:14]
        af_s, bf_s, cf_s, ab_s, bb_s, cb_s = refs[14:]
        one(xf_ref, prm_f, h0f_ref, af_s, bf_s, cf_s, False)
        one(xb_ref, prm_b, h0b_ref, ab_s, bb_s, cb_s, True)
        cf_s[...], cb_s[...] = _carry_scans([(af_s, bf_s, hf_ref, cf_s[...], False),
                                             (ab_s, bb_s, hb_ref, cb_s[...], True)])

    vec = _full((1, LRU_W))
    mat = _full((LRU_W, LRU_W))
    fw = pl.BlockSpec((tm, LRU_W), lambda i: (i, 0))
    bw = pl.BlockSpec((tm, LRU_W), lambda i: (nt - 1 - i, 0))
    tile_s = [_scan_scratch(tm, LRU_W), _scan_scratch(tm, LRU_W), pltpu.VMEM((SUBLANES, LRU_W), F32)]
    (hf, hb), couts = _call(
        body, name=name, grid=(nt,),
        in_specs=[fw, bw] + [mat, mat, vec, vec, vec] * 2 + [vec, vec],
        out_specs=[pl.BlockSpec((tm, LRU_W), lambda i: (i, 0)), pl.BlockSpec((tm, LRU_W), lambda i: (nt - 1 - i, 0))],
        out_shape=[_sds((t, LRU_W))] * 2, scratch_shapes=tile_s + tile_s, sem=("arbitrary",),
        args=(xc, xc, *par_f, *par_b, h0_f, h0_b), comms=comms)
    return ((hf, hb), couts) if comms else (hf, hb)


def _lru_bwd(xc, par_f, par_b, h_f, h_b, h0_f, h0_b, dh_f, dh_b, name, comms=()):
    t = xc.shape[0]
    tm = _tile(t, True)
    nt = t // tm
    n8 = tm // SUBLANES
    last8 = t // SUBLANES - 1
    tile_f = lambda w: pl.BlockSpec((tm, w), lambda i: (nt - 1 - i, 0))
    tile_b = lambda w: pl.BlockSpec((tm, w), lambda i: (i, 0))
    halo_f = pl.BlockSpec((SUBLANES, LRU_W), lambda i: (jnp.maximum((nt - 1 - i) * n8 - 1, 0), 0))
    halo_b = pl.BlockSpec((SUBLANES, LRU_W), lambda i: (jnp.minimum((i + 1) * n8, last8), 0))

    def one(refs_in, refs_out, refs_scr, reverse):
        x_ref, wa_ref, wx_ref, ba_ref, bx_ref, lam_ref, h_ref, halo_ref, h0_ref, dh_ref = refs_in
        dx_ref, dpre_ref, dba_ref, dbx_ref, dlam_ref, dh0_ref = refs_out
        a_s, b_s, l_s, c_s, e_s = refs_scr
        i = pl.program_id(0)

        @pl.when(i == 0)
        def _():
            c_s[...] = jnp.zeros_like(c_s)
            e_s[...] = jnp.zeros_like(e_s)
            dba_ref[...] = jnp.zeros_like(dba_ref)
            dbx_ref[...] = jnp.zeros_like(dbx_ref)
            dlam_ref[...] = jnp.zeros_like(dlam_ref)
        xv = x_ref[...]
        lam = lam_ref[...]
        r, ig, sp, a, q = _lru_gates(xv, wa_ref, wx_ref, ba_ref[...], bx_ref[...], lam)
        rs = lax.rsqrt(q)
        hv = h_ref[...]
        rowi = lax.broadcasted_iota(jnp.int32, (tm, LRU_W), 0)
        edge_a = jnp.broadcast_to(e_s[0:1, :], (tm, LRU_W))
        h0b = jnp.broadcast_to(h0_ref[...], (tm, LRU_W))
        if reverse:
            a_sh = jnp.where(rowi == 0, edge_a, pltpu.roll(a, 1, 0))
            hin_edge = jnp.where(i == nt - 1, h0b, jnp.broadcast_to(halo_ref[0:1, :], (tm, LRU_W)))
            h_in = jnp.where(rowi == tm - 1, hin_edge, pltpu.roll(hv, tm - 1, 0))
        else:
            a_sh = jnp.where(rowi == tm - 1, edge_a, pltpu.roll(a, tm - 1, 0))
            hin_edge = jnp.where(i == nt - 1, h0b, jnp.broadcast_to(halo_ref[SUBLANES - 1:SUBLANES, :], (tm, LRU_W)))
            h_in = jnp.where(rowi == 0, hin_edge, pltpu.roll(hv, 1, 0))
        _to_lane_blocks(a_s, a_sh)
        _to_lane_blocks(b_s, dh_ref[...])
        _group_scan(a_s, b_s, not reverse)
        e_s[...] = jnp.broadcast_to(a[tm - 1:tm, :] if reverse else a[0:1, :], e_s.shape)
        return lam, r, ig, sp, a, q * rs, rs, h_in

    def post(vals, refs_in, refs_out, refs_scr, reverse):
        lam, r, ig, sp, a, mult, rs, h_in = vals
        xv = refs_in[0][...]
        wa_ref, wx_ref = refs_in[1:3]
        dx_ref, dpre_ref, dba_ref, dbx_ref, dlam_ref, dh0_ref = refs_out
        l_s = refs_scr[2]
        i = pl.program_id(0)
        lmb = l_s[...]
        da = lmb * h_in
        ixc = ig * xv
        dmult = lmb * ixc
        dixc = lmb * mult
        dla = da * a - dmult * (a * a) * rs
        dpr = dla * (-LRU_C * sp) * r * (1.0 - r)
        dpi = dixc * xv * ig * (1.0 - ig)
        dprb = dpr.astype(BF16)
        dpib = dpi.astype(BF16)
        dpre_ref[:, 0:LRU_W] = dprb
        dpre_ref[:, LRU_W:2 * LRU_W] = dpib
        dx_ref[...] = dixc * ig + _dot_nt(dprb, wa_ref[...]) + _dot_nt(dpib, wx_ref[...])
        dba_ref[...] += _sum0(dpr)
        dbx_ref[...] += _sum0(dpi)
        dlam_ref[...] += _sum0(dla * (-LRU_C * r)) * (-_sigmoid(-lam))

        @pl.when(i == nt - 1)
        def _():
            al0 = a * lmb
            dh0_ref[...] = al0[tm - 1:tm, :] if reverse else al0[0:1, :]

    def body(*refs):
        jobs = ((refs[0:10], refs[20:26], refs[32:37], False), (refs[10:20], refs[26:32], refs[37:42], True))
        vals = [one(*job) for job in jobs]
        carries = _carry_scans([(scr[0], scr[1], scr[2], scr[3][...], not rev) for _, _, scr, rev in jobs])
        for (_, _, scr, _), carry in zip(jobs, carries):
            scr[3][...] = carry
        for v, job in zip(vals, jobs):
            post(v, *job)

    vec = _full((1, LRU_W))
    mat = _full((LRU_W, LRU_W))

    def in_specs(tile, halo):
        return [tile(LRU_W), mat, mat, vec, vec, vec, tile(LRU_W), halo, vec, tile(LRU_W)]

    def out_specs(tile):
        return [tile(LRU_W), tile(2 * LRU_W), vec, vec, vec, vec]

    out_one = [_sds((t, LRU_W)), _sds((t, 2 * LRU_W), BF16)] + [_sds((1, LRU_W))] * 4
    scr_one = ([_scan_scratch(tm, LRU_W)] * 2 + [pltpu.VMEM((tm, LRU_W), F32)]
               + [pltpu.VMEM((SUBLANES, LRU_W), F32)] * 2)
    outs, couts = _call(
        body, name=name, grid=(nt,), in_specs=in_specs(tile_f, halo_f) + in_specs(tile_b, halo_b),
        out_specs=out_specs(tile_f) + out_specs(tile_b), out_shape=out_one + out_one,
        scratch_shapes=scr_one + scr_one, sem=("arbitrary",),
        args=(xc, *par_f, h_f, h_f, h0_f, dh_f, xc, *par_b, h_b, h_b, h0_b, dh_b), comms=comms)
    return (tuple(outs[0:6]), tuple(outs[6:12])), couts


def _decay_tables(lg, reverse):
    ci = lax.broadcasted_iota(jnp.int32, (CHUNK, CHUNK), 0).astype(F32)
    mi = lax.broadcasted_iota(jnp.int32, (CHUNK, CHUNK), 1).astype(F32)
    rel = (mi - ci) if reverse else (ci - mi)
    relc = jnp.maximum(rel, 0.0)
    lg_c = jnp.concatenate([lg] * (CHUNK // LANES), axis=1)
    dm = jnp.where(rel >= 0, jnp.exp(lg_c * relc), 0.0)
    cd = lax.broadcasted_iota(jnp.int32, (CHUNK, DH), 0).astype(F32)
    pq, ps = (CHUNK - cd, cd) if reverse else (cd + 1.0, CHUNK - 1.0 - cd)
    return relc, dm, jnp.exp(lg * pq), jnp.exp(lg * ps), jnp.exp(lg * float(CHUNK)), pq, ps


def _ret_fwd(proj, lgv, s0f, s0b, comms=()):
    t = proj.shape[0]
    n = t // CHUNK

    def one(q, k, v, lg, s_s, hh, o_ref, sp_ref, reverse):
        _, dm, wq, ws, g, _, _ = _decay_tables(lg, reverse)
        vb = v.astype(BF16)
        p = _dot_nt(q.astype(BF16), k.astype(BF16)) * dm
        s = s_s[hh]
        sp_ref[hh, 0] = s
        o_ref[:, DH * hh:DH * (hh + 1)] = _dot(p.astype(BF16), vb) + _dot((q * wq).astype(BF16), s.astype(BF16))
        s_s[hh] = g * s + _dot_tn((k * ws).astype(BF16), vb)

    def body(qf, kf, vf, qb, kb, vb, lg_ref, s0f_ref, s0b_ref, of_ref, ob_ref, spf_ref, spb_ref, sf_s, sb_s):
        @pl.when(pl.program_id(0) == 0)
        def _():
            sf_s[...] = s0f_ref[...]
            sb_s[...] = s0b_ref[...]
        for hh in range(HEADS):
            sl = slice(DH * hh, DH * (hh + 1))
            one(qf[:, sl].astype(F32), kf[:, sl].astype(F32), vf[:, sl], lg_ref[hh, 0:1, :], sf_s, hh, of_ref, spf_ref,
                False)
            one(qb[:, sl].astype(F32), kb[:, sl].astype(F32), vb[:, sl], lg_ref[hh, 1:2, :], sb_s, hh, ob_ref, spb_ref,
                True)

    blk = (CHUNK, RET_W)
    fw = [pl.BlockSpec(blk, lambda i, o=o: (i, o)) for o in range(3)]
    bw = [pl.BlockSpec(blk, lambda i, o=o: (n - 1 - i, o)) for o in range(3)]
    st = _full((HEADS, DH, DH))
    return _call(body, name="ret_fwd", grid=(n,),
                 in_specs=fw + bw + [_full((HEADS, 2, LANES)), st, st],
                 out_specs=[pl.BlockSpec(blk, lambda i: (i, 0)), pl.BlockSpec(blk, lambda i: (n - 1 - i, 0)),
                            pl.BlockSpec((HEADS, 1, DH, DH), lambda i: (0, i, 0, 0)),
                            pl.BlockSpec((HEADS, 1, DH, DH), lambda i: (0, n - 1 - i, 0, 0))],
                 out_shape=[_sds((t, RET_W)), _sds((t, RET_W)), _sds((HEADS, n, DH, DH)), _sds((HEADS, n, DH, DH))],
                 scratch_shapes=[pltpu.VMEM((HEADS, DH, DH), F32), pltpu.VMEM((HEADS, DH, DH), F32)],
                 sem=("arbitrary",), args=(proj, proj, proj, proj, proj, proj, lgv, s0f, s0b), comms=comms)


def _ret_bwd(proj, lgv, sgv, spf, spb, do, comms=()):
    t = proj.shape[0]
    n = t // CHUNK

    def one(q_ref, k_ref, v_ref, lg_ref, s_ref, do_ref, dq_ref, dk_ref, dv_ref, ds_s, acc_s, reverse):
        d = 1 if reverse else 0
        for hh in range(HEADS):
            sl = slice(DH * hh, DH * (hh + 1))
            relc, dm, wq, ws, g, pq, ps = _decay_tables(lg_ref[hh, d:d + 1, :], reverse)
            qb, kb, vb = q_ref[:, sl], k_ref[:, sl], v_ref[:, sl]
            q, k = qb.astype(F32), kb.astype(F32)
            p = _dot_nt(qb, kb) * dm
            s = s_ref[hh, 0]
            dob = do_ref[:, sl].astype(BF16)
            dsn = ds_s[hh]
            dsb = dsn.astype(BF16)
            dv_ref[:, sl] = (_dot_tn(p.astype(BF16), dob) + _dot((k * ws).astype(BF16), dsb)).astype(BF16)
            dp = _dot_nt(dob, vb)
            dab = (dp * dm).astype(BF16)
            xq = _dot_nt(dob, s.astype(BF16))
            yk = _dot_nt(vb, dsb)
            dq_ref[:, sl] = (_dot(dab, kb) + xq * wq).astype(BF16)
            dk_ref[:, sl] = (_dot_tn(dab, qb) + yk * ws).astype(BF16)
            ds_s[hh] = g * dsn + _dot_tn((q * wq).astype(BF16), dob)
            s_mask = _sum0(dp * p * relc)
            part = (sum(s_mask[:, LANES * u:LANES * (u + 1)] for u in range(CHUNK // LANES))
                    + _sum0(xq * q * wq * pq) + _sum0(yk * k * ws * ps) + _sum0(dsn * s) * g * float(CHUNK))
            acc_s[hh] += jnp.broadcast_to(part, (SUBLANES, LANES))

    def body(qf, kf, vf, qb, kb, vb, lg_ref, sg_ref, sf_ref, sb_ref, dof_ref, dob_ref,
             dqf, dkf, dvf, dqb, dkb, dvb, ds0f_ref, ds0b_ref, drdf_ref, drdb_ref, dsf_s, dsb_s, accf_s, accb_s):
        i = pl.program_id(0)

        @pl.when(i == 0)
        def _():
            for r in (dsf_s, dsb_s, accf_s, accb_s):
                r[...] = jnp.zeros_like(r)
        one(qf, kf, vf, lg_ref, sf_ref, dof_ref, dqf, dkf, dvf, dsf_s, accf_s, False)
        one(qb, kb, vb, lg_ref, sb_ref, dob_ref, dqb, dkb, dvb, dsb_s, accb_s, True)

        @pl.when(i == n - 1)
        def _():
            ds0f_ref[...] = dsf_s[...]
            ds0b_ref[...] = dsb_s[...]
            for d, (acc_s, drd_ref) in enumerate(((accf_s, drdf_ref), (accb_s, drdb_ref))):
                for hh in range(HEADS):
                    tot = jnp.sum(acc_s[hh, 0:1, :], axis=1, keepdims=True)
                    drd_ref[hh] = jnp.broadcast_to(tot, (SUBLANES, LANES)) * sg_ref[hh, d:d + 1, :]

    blk = (CHUNK, RET_W)
    fw = lambda o: pl.BlockSpec(blk, lambda i, o=o: (n - 1 - i, o))
    bw = lambda o: pl.BlockSpec(blk, lambda i, o=o: (i, o))
    lane = _full((HEADS, 2, LANES))
    st = _full((HEADS, DH, DH))
    rd = _full((HEADS, SUBLANES, LANES))
    outs, couts = _call(
        body, name="ret_bwd", grid=(n,),
        in_specs=[fw(0), fw(1), fw(2), bw(0), bw(1), bw(2), lane, lane,
                  pl.BlockSpec((HEADS, 1, DH, DH), lambda i: (0, n - 1 - i, 0, 0)),
                  pl.BlockSpec((HEADS, 1, DH, DH), lambda i: (0, i, 0, 0)), fw(0), bw(0)],
        out_specs=[fw(0), fw(0), fw(0), bw(0), bw(0), bw(0), st, st, rd, rd],
        out_shape=[_sds((t, RET_W), BF16)] * 6 + [_sds((HEADS, DH, DH))] * 2 + [_sds((HEADS, SUBLANES, LANES))] * 2,
        scratch_shapes=[pltpu.VMEM((HEADS, DH, DH), F32)] * 2 + [pltpu.VMEM((HEADS, SUBLANES, LANES), F32)] * 2,
        sem=("arbitrary",), args=(proj, proj, proj, proj, proj, proj, lgv, sgv, spf, spb, do, do), comms=comms)
    dqf, dkf, dvf, dqb, dkb, dvb, ds0f, ds0b, drdf, drdb = outs
    return ((dqf, dkf, dvf, ds0f, drdf), (dqb, dkb, dvb, ds0b, drdb)), couts


def _ctx_weights(lg, l_len, reverse):
    pos = lax.broadcasted_iota(jnp.int32, (l_len, DH), 0).astype(F32)
    steps = pos if reverse else (l_len - 1.0 - pos)
    return jnp.exp(lg * steps), steps


def _ctx_state_fwd(projc, lgv):
    l_len = projc.shape[0]

    def body(k_ref, v_ref, lg_ref, sf_ref, sb_ref):
        k = k_ref[...]
        vb = v_ref[...].astype(BF16)
        for d, o_ref in ((0, sf_ref), (1, sb_ref)):
            w, _ = _ctx_weights(lg_ref[0, d:d + 1, :], l_len, d == 1)
            o_ref[0] = _dot_tn((k * w).astype(BF16), vb)

    st = pl.BlockSpec((1, DH, DH), lambda h: (h, 0, 0))
    return _pc(body, name="ctx_state_fwd", grid=(HEADS,),
               in_specs=[pl.BlockSpec((l_len, DH), lambda h: (0, HEADS + h)),
                         pl.BlockSpec((l_len, DH), lambda h: (0, 2 * HEADS + h)),
                         pl.BlockSpec((1, 2, LANES), lambda h: (h, 0, 0))],
               out_specs=[st, st], out_shape=[_sds((HEADS, DH, DH))] * 2,
               compiler_params=_params("arbitrary"))(projc, projc, lgv)


def _ctx_state_bwd(projc, lgv, sgv, dsf, dsb):
    l_len = projc.shape[0]

    def body(k_ref, v_ref, lg_ref, sg_ref, dsf_ref, dsb_ref, dk_ref, dv_ref, drd_ref):
        k = k_ref[...]
        vb = v_ref[...].astype(BF16)
        dk = jnp.zeros((l_len, DH), F32)
        dv = jnp.zeros((l_len, DH), F32)
        rows = []
        for d, ds_ref in ((0, dsf_ref), (1, dsb_ref)):
            w, steps = _ctx_weights(lg_ref[0, d:d + 1, :], l_len, d == 1)
            dsb16 = ds_ref[0].astype(BF16)
            dkw = _dot_nt(vb, dsb16)
            dk = dk + dkw * w
            dv = dv + _dot((k * w).astype(BF16), dsb16)
            tot = jnp.sum(_sum0(dkw * k * w * steps), axis=1, keepdims=True)
            rows.append(jnp.broadcast_to(tot, (1, LANES)) * sg_ref[0, d:d + 1, :])
        dk_ref[...] = dk.astype(BF16)
        dv_ref[...] = dv.astype(BF16)
        rid = lax.broadcasted_iota(jnp.int32, (SUBLANES, LANES), 0)
        drd_ref[0] = jnp.where(rid == 0, rows[0], jnp.where(rid == 1, rows[1], 0.0))

    st = pl.BlockSpec((1, DH, DH), lambda h: (h, 0, 0))
    lane = pl.BlockSpec((1, 2, LANES), lambda h: (h, 0, 0))
    hc = pl.BlockSpec((l_len, DH), lambda h: (0, h))
    return _pc(body, name="ctx_state_bwd", grid=(HEADS,),
               in_specs=[pl.BlockSpec((l_len, DH), lambda h: (0, HEADS + h)),
                         pl.BlockSpec((l_len, DH), lambda h: (0, 2 * HEADS + h)), lane, lane, st, st],
               out_specs=[hc, hc, pl.BlockSpec((1, SUBLANES, LANES), lambda h: (h, 0, 0))],
               out_shape=[_sds((l_len, RET_W), BF16), _sds((l_len, RET_W), BF16), _sds((HEADS, SUBLANES, LANES))],
               compiler_params=_params("arbitrary"))(projc, projc, lgv, sgv, dsf, dsb)


G_BLOCK = (3 * RET_W) // RET_W
GATE_BLOCK = (4 * RET_W + LRU_W) // LRU_W


def _head_norm(y):
    yc = y - jnp.mean(y, axis=-1, keepdims=True)
    rs = lax.rsqrt(jnp.mean(yc * yc, axis=-1, keepdims=True) + EPS)
    return yc * rs, rs


def _gelu_parts(z):
    th = jnp.tanh(GELU_K * (z + GELU_C * z * z * z))
    return 0.5 * z * (1.0 + th), th


def _mix_fwd(o_f, o_b, proj, hf, hb, w_out, x, g1, comms):
    t = x.shape[0]
    tm = _tile(t, True)

    def body(of_ref, ob_ref, g_ref, gt_ref, hf_ref, hb_ref, w_ref, x_ref, g1_ref, x1_ref, cat_ref):
        o = of_ref[...] + ob_ref[...]
        g = g_ref[...].astype(F32)
        for hh in range(HEADS):
            sl = slice(DH * hh, DH * (hh + 1))
            nrm, _ = _head_norm(o[:, sl])
            gh = g[:, sl]
            cat_ref[:, sl] = (gh * _sigmoid(gh) * nrm).astype(BF16)
        gel, _ = _gelu_parts(gt_ref[...].astype(F32))
        cat_ref[:, RET_W:] = ((hf_ref[...] + hb_ref[...]) * gel).astype(BF16)
        x1_ref[...] = x_ref[...] + g1_ref[...] * _dot(cat_ref[...], w_ref[...])

    half = pl.BlockSpec((tm, RET_W), lambda i: (i, 0))
    big = pl.BlockSpec((tm, D_MODEL), lambda i: (i, 0))
    return _call(body, name="mix_fwd", grid=(t // tm,),
                 in_specs=[half, half, pl.BlockSpec((tm, RET_W), lambda i: (i, G_BLOCK)),
                           pl.BlockSpec((tm, LRU_W), lambda i: (i, GATE_BLOCK)), half, half,
                           _full((D_MODEL, D_MODEL)), big, _full((1, D_MODEL))],
                 out_specs=[big, big], out_shape=[_sds((t, D_MODEL)), _sds((t, D_MODEL), BF16)],
                 scratch_shapes=[], sem=("arbitrary",), args=(o_f, o_b, proj, proj, hf, hb, w_out, x, g1),
                 comms=comms)


def _mix_bwd(o_f, o_b, proj, hf, hb, w_out, cat, dx1, g1, comms=()):
    t = dx1.shape[0]
    tm = _tile(t, True)

    def body(of_ref, ob_ref, g_ref, gt_ref, hf_ref, hb_ref, w_ref, cat_ref, dx1_ref, g1_ref,
             do_ref, dhs_ref, dg_ref, dgt_ref, dyb_ref, dg1_ref):
        dx1v = dx1_ref[...]
        y = _dot(cat_ref[...], w_ref[...])

        @pl.when(pl.program_id(0) == 0)
        def _():
            dg1_ref[...] = jnp.zeros_like(dg1_ref)
        dg1_ref[...] += _sum0(dx1v * y)
        dyb = (g1_ref[...] * dx1v).astype(BF16)
        dyb_ref[...] = dyb
        dcat = _dot_nt(dyb, w_ref[...])
        o = of_ref[...] + ob_ref[...]
        g = g_ref[...].astype(F32)
        for hh in range(HEADS):
            sl = slice(DH * hh, DH * (hh + 1))
            nrm, rs = _head_norm(o[:, sl])
            gh = g[:, sl]
            sg = _sigmoid(gh)
            dret = dcat[:, sl]
            dg_ref[:, sl] = (dret * nrm * (sg * (1.0 + gh * (1.0 - sg)))).astype(BF16)
            dn = dret * (gh * sg)
            dyc = rs * (dn - nrm * jnp.mean(dn * nrm, axis=-1, keepdims=True))
            do_ref[:, sl] = (dyc - jnp.mean(dyc, axis=-1, keepdims=True)).astype(BF16)
        z = gt_ref[...].astype(F32)
        gel, th = _gelu_parts(z)
        dlru = dcat[:, RET_W:]
        dhs_ref[...] = dlru * gel
        dgel = 0.5 * (1.0 + th) + 0.5 * z * (1.0 - th * th) * GELU_K * (1.0 + 3.0 * GELU_C * z * z)
        dgt_ref[...] = (dlru * (hf_ref[...] + hb_ref[...]) * dgel).astype(BF16)

    half = pl.BlockSpec((tm, RET_W), lambda i: (i, 0))
    big = pl.BlockSpec((tm, D_MODEL), lambda i: (i, 0))
    return _call(body, name="mix_bwd", grid=(t // tm,),
                 in_specs=[half, half, pl.BlockSpec((tm, RET_W), lambda i: (i, G_BLOCK)),
                           pl.BlockSpec((tm, LRU_W), lambda i: (i, GATE_BLOCK)), half, half,
                           _full((D_MODEL, D_MODEL)), big, big, _full((1, D_MODEL))],
                 out_specs=[half, half, half, half, big, _full((1, D_MODEL))],
                 out_shape=[_sds((t, RET_W), BF16), _sds((t, RET_W)), _sds((t, RET_W), BF16), _sds((t, RET_W), BF16),
                            _sds((t, D_MODEL), BF16), _sds((1, D_MODEL))],
                 scratch_shapes=[], sem=("arbitrary",), args=(o_f, o_b, proj, proj, hf, hb, w_out, cat, dx1, g1),
                 comms=comms)


def _mlp(x1, n2g, sh2, sc2, g2, fg, w1_parts, w2_parts, tgt):
    t = x1.shape[0]
    tm = _tile(t)
    hb_ = MLP_H // N_CHIP
    q_rows = hb_ // 4
    n_cp = 4 * N_DEV

    def body(x1_ref, n2g_ref, sh2_ref, sc2_ref, g2_ref, fg_ref, w1a, w1b, w2a, w2b, tgt_ref,
             dx1_ref, h2b_ref, ab_ref, dub_ref, dmb_ref, dsc_ref, dsh_ref, dg2_ref, dn2_ref, dfg_ref, loss_ref,
             w1_s, w2_s, r_s, sems):
        @pl.when(pl.program_id(0) == 0)
        def _():
            cps = []
            for p, parts in enumerate(((w1a, w2a), (w1b, w2b))):
                for d in range(N_DEV):
                    rows = pl.ds(2 * q_rows * (d % 2) + q_rows * p, q_rows)
                    for src, dst in zip(parts, (w1_s, w2_s)):
                        cps.append(pltpu.make_async_copy(src.at[d], dst.at[d // 2, rows], sems.at[len(cps)]))
            for cp in cps:
                cp.start()
            for r in (dsc_ref, dsh_ref, dg2_ref, dn2_ref, dfg_ref, loss_ref):
                r[...] = jnp.zeros_like(r)
            for cp in cps:
                cp.wait()
        x1v = x1_ref[...]
        n2g, sc2, g2, fg = n2g_ref[...], sc2_ref[...], g2_ref[...], fg_ref[...]
        xh, _ = _rms(x1v)
        h2b = (xh * n2g * (1.0 + sc2) + sh2_ref[...]).astype(BF16)
        h2b_ref[...] = h2b
        m = jnp.zeros((tm, D_MODEL), F32)
        for j in range(N_CHIP):
            sl = slice(hb_ * j, hb_ * (j + 1))
            r = jnp.maximum(_dot(h2b, w1_s[j]), 0.0)
            r_s[:, sl] = r
            ab = (r * r).astype(BF16)
            ab_ref[:, sl] = ab
            m = m + _dot(ab, w2_s[j])
        x2 = x1v + g2 * m
        x2h, r2 = _rms(x2)
        err = x2h * fg - tgt_ref[...]
        loss_ref[...] += _sum0(err * err)
        dout = err * (1.0 / D_MODEL)
        dfg_ref[...] += _sum0(dout * x2h)
        dxh = dout * fg
        dx2 = r2 * (dxh - x2h * jnp.mean(dxh * x2h, axis=-1, keepdims=True))
        dg2_ref[...] += _sum0(dx2 * m)
        dmb = (g2 * dx2).astype(BF16)
        dmb_ref[...] = dmb
        dh2 = jnp.zeros((tm, D_MODEL), F32)
        for j in range(N_CHIP):
            sl = slice(hb_ * j, hb_ * (j + 1))
            dub = (_dot_nt(dmb, w2_s[j]) * (2.0 * r_s[:, sl])).astype(BF16)
            dub_ref[:, sl] = dub
            dh2 = dh2 + _dot_nt(dub, w1_s[j])
        dx, dn2_t, dsh_t, dsc_t = _norm_mod_bwd(x1v, n2g, sc2, dh2)
        dx1_ref[...] = dx2 + dx
        dn2_ref[...] += dn2_t
        dsh_ref[...] += dsh_t
        dsc_ref[...] += dsc_t

        @pl.when(pl.program_id(0) == t // tm - 1)
        def _():
            tot = jnp.sum(loss_ref[...], axis=1, keepdims=True) * (0.5 / D_MODEL)
            loss_ref[...] = jnp.broadcast_to(tot, loss_ref.shape)

    row = _full((1, D_MODEL))
    big = pl.BlockSpec((tm, D_MODEL), lambda i: (i, 0))
    wide = pl.BlockSpec((tm, MLP_H), lambda i: (i, 0))
    return _pc(body, name="mlp", grid=(t // tm,),
               in_specs=[big, row, row, row, row, row, ANY, ANY, ANY, ANY, big],
               out_specs=[big, big, wide, wide, big, row, row, row, row, row, row],
               out_shape=[_sds((t, D_MODEL)), _sds((t, D_MODEL), BF16), _sds((t, MLP_H), BF16), _sds((t, MLP_H), BF16),
                          _sds((t, D_MODEL), BF16)] + [_sds((1, D_MODEL))] * 6,
               scratch_shapes=[pltpu.VMEM((N_CHIP, D_MODEL, hb_), BF16), pltpu.VMEM((N_CHIP, hb_, D_MODEL), BF16),
                               pltpu.VMEM((tm, MLP_H), F32), pltpu.SemaphoreType.DMA((n_cp,))],
               compiler_params=_params("arbitrary"))(x1, n2g, sh2, sc2, g2, fg, *w1_parts, *w2_parts, tgt)


def _tn(a, b, nj, a_blocked, b_blocked, name, extra=None, comms=()):
    t = a.shape[0]
    m = a.shape[1] // (nj if a_blocked else 1)
    n = b.shape[1] // (nj if b_blocked else 1)
    bk = next((b for b in (2048, 1024, 512) if t % b == 0), t)
    nk = t // bk
    a_col = (lambda j: j) if a_blocked else (lambda j: 0)
    b_col = (lambda j: j) if b_blocked else (lambda j: 0)
    in_specs = [pl.BlockSpec((bk, m), lambda j, k: (k, a_col(j))), pl.BlockSpec((bk, n), lambda j, k: (k, b_col(j)))]
    args = [a, b]
    if extra is not None:
        a2, b2 = extra
        t2 = a2.shape[0]
        in_specs += [pl.BlockSpec((t2, m), lambda j, k: (0, a_col(j))),
                     pl.BlockSpec((t2, n), lambda j, k: (0, b_col(j)))]
        args += [a2, b2]

    def body(*refs):
        a_ref, b_ref = refs[0], refs[1]
        o_ref, acc = refs[-2], refs[-1]
        k = pl.program_id(1)

        @pl.when(k == 0)
        def _():
            acc[...] = jnp.zeros_like(acc)
        acc[...] += _dot_tn(a_ref[...].astype(BF16), b_ref[...].astype(BF16))

        @pl.when(k == nk - 1)
        def _():
            if extra is not None:
                acc[...] += _dot_tn(refs[2][...].astype(BF16), refs[3][...].astype(BF16))
            o_ref[0] = acc[...]

    (out,), couts = _call(body, name=name, grid=(nj, nk), in_specs=in_specs,
                          out_specs=[pl.BlockSpec((1, m, n), lambda j, k: (j, 0, 0))], out_shape=[_sds((nj, m, n))],
                          scratch_shapes=[pltpu.VMEM((m, n), F32)], sem=("arbitrary", "arbitrary"), args=args,
                          comms=comms)
    return (out, couts) if comms else out


ROW_LOSS = 0
ROW_DMOD = 1
ROW_DMODC = 7
ROW_N1, ROW_N2, ROW_FG, ROW_CB = 9, 10, 11, 12
ROW_BA, ROW_BX, ROW_LAM = 13, 15, 17
ROW_CW = 20
ROW_RD = 24
SLAB_ROWS = 32
SEG = D_MODEL // 2


def _pack_small(rows, drd, cw2, cb2, lru2, gates):
    n_rows, n_lru = len(rows), len(lru2)

    def body(*refs):
        r = refs[:n_rows]
        drd_f, drd_b, drd_c, cw_a, cw_b, cb_a, cb_b = refs[n_rows:n_rows + 7]
        lru = refs[n_rows + 7:n_rows + 7 + n_lru]
        gf_ref, gb_ref, slab, ga, gx = refs[n_rows + 7 + n_lru:]
        slab[...] = jnp.zeros_like(slab)
        slab[ROW_LOSS:ROW_LOSS + 1, :] = r[0][...]
        for k in range(N_MOD):
            slab[ROW_DMOD + k:ROW_DMOD + k + 1, :] = r[1 + k][...]
        slab[ROW_DMODC:ROW_DMODC + 1, :] = r[7][...]
        slab[ROW_DMODC + 1:ROW_DMODC + 2, :] = r[8][...]
        slab[ROW_N1:ROW_N1 + 1, :] = r[9][...] + r[10][...]
        slab[ROW_N2:ROW_N2 + 1, :] = r[11][...]
        slab[ROW_FG:ROW_FG + 1, :] = r[12][...]
        slab[ROW_CB:ROW_CB + 1, 0:LRU_W] = cb_a[...] + cb_b[...]
        for k, row in enumerate((ROW_BA, ROW_BA + 1, ROW_BX, ROW_BX + 1, ROW_LAM, ROW_LAM + 1)):
            slab[row:row + 1, 0:LRU_W] = lru[2 * k][...] + lru[2 * k + 1][...]
        slab[ROW_CW:ROW_CW + 4, 0:LRU_W] = cw_a[...] + cw_b[...]
        for h in range(HEADS):
            slab[ROW_RD + h:ROW_RD + h + 1, 0:LANES] = drd_f[h, 0:1, :] + drd_c[h, 0:1, :]
            slab[ROW_RD + HEADS + h:ROW_RD + HEADS + h + 1, 0:LANES] = drd_b[h, 0:1, :] + drd_c[h, 1:2, :]
        for d, g_ref in enumerate((gf_ref, gb_ref)):
            for n in range(LRU_BLOCKS):
                blk = slice(LRU_BD * n, LRU_BD * (n + 1))
                ga[blk, LRU_BD * d:LRU_BD * (d + 1)] = g_ref[0, blk, blk].astype(BF16)
                gx[blk, LRU_BD * d:LRU_BD * (d + 1)] = g_ref[1, blk, blk].astype(BF16)

    args = list(rows) + list(drd) + list(cw2) + list(cb2) + list(lru2) + list(gates)
    gate_shape = (LRU_W, 2 * LRU_BD)
    return _pc(body, name="pack_small", in_specs=[_full(a.shape) for a in args],
               out_specs=[_full((SLAB_ROWS, D_MODEL)), _full(gate_shape), _full(gate_shape)],
               out_shape=[_sds((SLAB_ROWS, D_MODEL)), _sds(gate_shape, BF16), _sds(gate_shape, BF16)],
               compiler_params=_params())(*args)


def _adam_math(w, g, m, v):
    mn = ADAM_B1 * m + (1.0 - ADAM_B1) * g
    vn = ADAM_B2 * v + (1.0 - ADAM_B2) * (g * g)
    mh = mn / (1.0 - ADAM_B1 ** ADAM_STEP)
    vh = vn / (1.0 - ADAM_B2 ** ADAM_STEP)
    return -ADAM_LR * (mh / (jnp.sqrt(vh) + ADAM_EPS) + ADAM_WD * w), mn, vn


SMALL_PARAMS = ("b_ada", "norm1_g", "norm2_g", "final_g", "ret_decay", "conv_w", "conv_b", "lru_wa", "lru_ba", "lru_wx",
                "lru_bx", "lru_lambda")


def _finalize_small(chip_idx, slab_all, ga_all, gx_all, wmv):
    n_p = len(SMALL_PARAMS)
    flat = [a for nm in SMALL_PARAMS for a in wmv[nm]]
    ada_n = N_MOD * D_MODEL // N_CHIP

    def body(c_ref, slab_ref, ga_ref, gx_ref, *refs):
        prm = {nm: refs[3 * k:3 * k + 3] for k, nm in enumerate(SMALL_PARAMS)}
        outs = {nm: refs[3 * n_p + 4 * k:3 * n_p + 4 * k + 4] for k, nm in enumerate(SMALL_PARAMS)}
        b128_ref, dmc_ref, loss_ref = refs[3 * n_p + 4 * n_p:]
        chip = c_ref[0]

        def pick(fn):
            acc = fn(0)
            for j in range(1, N_CHIP):
                acc = jnp.where(chip == j, fn(j), acc)
            return acc

        tot = slab_ref[0]
        for d in range(1, N_DEV):
            tot = tot + slab_ref[d]

        def update(nm, g, sl=None, rows=None):
            w_ref, m_ref, v_ref = prm[nm]
            g_ref, d_ref, mo_ref, vo_ref = outs[nm]
            ix = (slice(None) if rows is None else rows, slice(None) if sl is None else sl)
            dl, mn, vn = _adam_math(w_ref[ix], g, m_ref[ix], v_ref[ix])
            g_ref[ix] = g
            d_ref[ix] = dl
            mo_ref[ix] = mn
            vo_ref[ix] = vn

        loss_ref[...] = jnp.broadcast_to(tot[ROW_LOSS:ROW_LOSS + 1, 0:LANES], (SUBLANES, LANES))
        for k in range(N_MOD):
            g = tot[ROW_DMOD + k:ROW_DMOD + k + 1, :]
            if k < 2:
                g = g + tot[ROW_DMODC + k:ROW_DMODC + k + 1, :]
            update("b_ada", g, slice(D_MODEL * k, D_MODEL * (k + 1)))
        update("norm1_g", tot[ROW_N1:ROW_N1 + 1, :])
        update("norm2_g", tot[ROW_N2:ROW_N2 + 1, :])
        update("final_g", tot[ROW_FG:ROW_FG + 1, :])
        update("ret_decay", tot[ROW_RD:ROW_RD + SUBLANES, 0:LANES])
        update("conv_b", tot[ROW_CB:ROW_CB + 1, 0:LRU_W])
        update("conv_w", pick(lambda j: tot[ROW_CW:ROW_CW + 4, LANES * j:LANES * (j + 1)]))
        for nm, row in (("lru_ba", ROW_BA), ("lru_bx", ROW_BX), ("lru_lambda", ROW_LAM)):
            update(nm, pick(lambda j, row=row: tot[row:row + 2, LANES * j:LANES * (j + 1)]))
        for nm, g_all in (("lru_wa", ga_ref), ("lru_wx", gx_ref)):
            for dr in range(2):
                lanes = slice(LRU_BD * dr, LRU_BD * (dr + 1))
                g = g_all[0, :, lanes].astype(F32)
                for d in range(1, N_DEV):
                    g = g + g_all[d, :, lanes].astype(F32)
                update(nm, g, rows=slice(LRU_W * dr, LRU_W * (dr + 1)))

        def seg(rows6, s):
            return rows6[s // 2][:, SEG * (s % 2):SEG * (s % 2 + 1)]

        b128_ref[...] = jnp.zeros_like(b128_ref)
        dmc_ref[...] = jnp.zeros_like(dmc_ref)
        zero = jnp.zeros((1, D_MODEL), F32)
        ctx6 = [tot[ROW_DMODC:ROW_DMODC + 1, :], tot[ROW_DMODC + 1:ROW_DMODC + 2, :]] + [zero] * (N_MOD - 2)
        for q in range(ada_n // SEG):
            cols = slice(SEG * q, SEG * (q + 1))
            for d in range(N_DEV):
                rows6 = [slab_ref[d, ROW_DMOD + k:ROW_DMOD + k + 1, :] for k in range(N_MOD)]
                b128_ref[d:d + 1, cols] = pick(lambda j, rows6=rows6: seg(rows6, 3 * j + q))
            c = pick(lambda j: seg(ctx6, 3 * j + q))
            b128_ref[N_DEV:N_DEV + 1, cols] = c
            dmc_ref[0:1, cols] = c

    out_shape = []
    for nm in SMALL_PARAMS:
        out_shape += [_sds(wmv[nm][0].shape)] * 4
    out_shape += [_sds((LANES, ada_n)), _sds((SUBLANES, ada_n)), _sds((SUBLANES, LANES))]
    args = [slab_all, ga_all, gx_all] + flat
    grid_spec = pltpu.PrefetchScalarGridSpec(
        num_scalar_prefetch=1, grid=(1,), in_specs=[_full(a.shape) for a in args],
        out_specs=[_full(s.shape) for s in out_shape])
    outs = _pc(body, name="finalize_small", grid_spec=grid_spec, out_shape=out_shape,
               compiler_params=_params("arbitrary"))(chip_idx, *args)
    res = {nm: tuple(outs[4 * k:4 * k + 4]) for k, nm in enumerate(SMALL_PARAMS)}
    return res, outs[4 * n_p], outs[4 * n_p + 1], outs[4 * n_p + 2]


def _block_diag(w):
    eye = jnp.eye(LRU_BLOCKS, dtype=F32)
    return (w[:, :, None, :] * eye[:, None, :, None]).reshape(LRU_W, LRU_W).astype(BF16)


def _lane_rep(v8):
    return jnp.broadcast_to(v8.reshape(SUBLANES, 1), (SUBLANES, LANES))


def kernel(x, c, ctx, c_ctx, w_ada, b_ada, norm1_g, norm2_g, w_in, ret_decay, conv_w, conv_b, lru_wa, lru_ba, lru_wx, lru_bx, lru_lambda, w_out, w_mlp1, w_mlp2, final_g, loss_target, m_c_ctx, m_w_ada, m_b_ada, m_norm1_g, m_norm2_g, m_w_in, m_ret_decay, m_conv_w, m_conv_b, m_lru_wa, m_lru_ba, m_lru_wx, m_lru_bx, m_lru_lambda, m_w_out, m_w_mlp1, m_w_mlp2, m_final_g, v_c_ctx, v_w_ada, v_b_ada, v_norm1_g, v_norm2_g, v_w_in, v_ret_decay, v_conv_w, v_conv_b, v_lru_wa, v_lru_ba, v_lru_wx, v_lru_bx, v_lru_lambda, v_w_out, v_w_mlp1, v_w_mlp2, v_final_g):
    ax, ay, ac = lax.axis_index("x"), lax.axis_index("y"), lax.axis_index("c")
    chip = 2 * ax + ay
    dev = 4 * ax + 2 * ay + ac
    c_idx = jnp.stack([ac, chip]).astype(jnp.int32)
    j_idx = chip.reshape(1).astype(jnp.int32)

    xt = x[0]
    t_len = xt.shape[0]
    ctxt = ctx[0]
    l_len = ctxt.shape[0]
    tgt = loss_target[0]
    ada_n = w_ada.shape[2]

    def my_half(w2d):
        r = w2d.shape[0] // 2
        return lax.dynamic_slice_in_dim(w2d, ac * r, r, axis=0).astype(BF16)

    pad8 = lambda a: jnp.pad(a, ((0, SUBLANES - a.shape[0]), (0, 0)))
    small = jnp.concatenate([pad8(conv_w[0]), pad8(lru_ba[0]), pad8(lru_bx[0]), pad8(lru_lambda[0])], axis=0)
    b_shard = lax.dynamic_slice_in_dim(b_ada, chip * ada_n, ada_n, axis=1)
    gw_in, _, small_all, a16, mod_parts, lgv, sgv = _head(
        my_half(w_in[0]), pad8(c), small, w_ada[0], b_shard, c_ctx, ret_decay[0])
    w4 = gw_in.reshape(N_CHIP, D_MODEL, IN_COLS // N_CHIP)

    mod_all = mod_parts[0::2].transpose(1, 0, 2).reshape(16, N_CHIP * ada_n)
    mod_me = lax.dynamic_slice_in_dim(mod_all, dev, 1, axis=0)
    sh1, sc1, g1, sh2, sc2, g2 = [mod_me[:, D_MODEL * k:D_MODEL * (k + 1)] for k in range(N_MOD)]
    csh1, csc1 = mod_all[8:9, 0:D_MODEL], mod_all[8:9, D_MODEL:2 * D_MODEL]

    cos2, sin2 = _rotary_tables(t_len)
    cos_c, sin_c = jnp.ones((l_len, DH), F32), jnp.zeros((l_len, DH), F32)
    n1g, n2g = norm1_g, norm2_g
    fg = final_g.reshape(1, D_MODEL)

    small_full = small_all[0::2].transpose(1, 0, 2).reshape(4 * SUBLANES, LRU_W)
    cw = small_full[0:4]
    cb = conv_b
    ba_f, ba_b = small_full[8:9], small_full[9:10]
    bx_f, bx_b = small_full[16:17], small_full[17:18]
    lam_f, lam_b = small_full[24:25], small_full[25:26]
    wa_f, wa_b = _block_diag(lru_wa[0, 0]), _block_diag(lru_wa[0, 1])
    wx_f, wx_b = _block_diag(lru_wx[0, 0]), _block_diag(lru_wx[0, 1])
    zero_h = jnp.zeros((1, LRU_W), F32)

    projc, xrc, hcb16 = _inproj_fwd(ctxt, n1g, csh1, csc1, w4, cos_c, sin_c, "inproj_fwd_ctx")
    s_f, s_b = _ctx_state_fwd(projc, lgv)
    xcc = _conv_fwd(xrc, cw, cb, "conv_fwd_ctx")
    par_f, par_b = (wa_f, wx_f, ba_f, bx_f, lam_f), (wa_b, wx_b, ba_b, bx_b, lam_b)
    hcf, hcbk = _lru_fwd(xcc, par_f, par_b, zero_h, zero_h, "lru_fwd_ctx")
    lru_sf, lru_sb = hcf[l_len - 1:l_len], hcbk[0:1]

    h1, h2 = my_half(w_mlp1[0]), my_half(w_mlp2[0])
    q = h1.shape[0] // 2
    (proj, xrl, hb16), ((gw_1a,),) = _inproj_fwd(xt, n1g, sh1, sc1, w4, cos2, sin2, "inproj_fwd",
                                           comms=(_AllGather([h1[:q]]),))
    (o_f, o_b, spf, spb), ((gw_1b, gw_out),) = _ret_fwd(proj, lgv, s_f, s_b,
                                                       comms=(_AllGather([h1[q:], my_half(w_out[0])]),))
    xcl = _conv_fwd(xrl, cw, cb, "conv_fwd")
    (hf, hbk), ((gw_2a,),) = _lru_fwd(xcl, par_f, par_b, lru_sf, lru_sb, "lru_fwd", comms=(_AllGather([h2[:q]]),))
    wo = gw_out.reshape(D_MODEL, D_MODEL)
    (x1, cat), ((gw_2b,),) = _mix_fwd(o_f, o_b, proj, hf, hbk, wo, xt, g1, comms=(_AllGather([h2[q:]]),))

    (dx1, h2b, ab, dub, dmb, dsc2, dsh2, dg2, dn2g, dfg, lossv) = _mlp(
        x1, n2g, sh2, sc2, g2, fg, (gw_1a, gw_1b), (gw_2a, gw_2b), tgt)
    gw_mlp1 = _tn(h2b, dub, N_CHIP, False, True, "grad_w_mlp1")
    b_1 = gw_mlp1.reshape(N_DEV, D_MODEL // 2, MLP_H // N_CHIP)
    gw_mlp2, ((r_1,),) = _tn(ab, dmb, N_CHIP, True, False, "grad_w_mlp2", comms=(_pair_exchange([b_1]),))

    half = D_MODEL // 4
    top, bot = (0, half), (half, half)
    b_2 = gw_mlp2.reshape(N_DEV, MLP_H // N_DEV, D_MODEL)
    p_1, pb_1 = _pair_add(b_1, r_1, c_idx, "rs_pair_add_w_mlp1")
    (do, dhs, dg, dgate, dyb, dg1), ((q_1a,), (r_2,)) = _mix_bwd(
        o_f, o_b, proj, hf, hbk, wo, cat, dx1, g1, comms=(_chip_exchange([pb_1], top), _pair_exchange([b_2])))
    gw_o = _tn(cat, dyb, 1, False, False, "grad_w_out")
    b_o = gw_o.reshape(N_DEV, D_MODEL // N_DEV, D_MODEL)
    p_2, pb_2 = _pair_add(b_2, r_2, c_idx, "rs_pair_add_w_mlp2")

    ((dq_f, dk_f, dv_f, ds_f, drd_f), (dq_b, dk_b, dv_b, ds_b, drd_b)), ((q_1b,), (q_2a,), (r_o,)) = _ret_bwd(
        proj, lgv, sgv, spf, spb, do,
        comms=(_chip_exchange([pb_1], bot), _chip_exchange([pb_2], top), _pair_exchange([b_o])))
    p_o, pb_o = _pair_add(b_o, r_o, c_idx, "rs_pair_add_w_out")
    h_1 = _chip_add(p_1, (q_1a, q_1b), c_idx, "rs_chip_add_w_mlp1")

    ((dxc_f, dpre_f, dba_f, dbx_f, dlam_f, dh0_f), (dxc_b, dpre_b, dba_b, dbx_b, dlam_b, dh0_b)), (
        (q_2b,), (q_o,), (f_1,)) = _lru_bwd(
        xcl, par_f, par_b, hf, hbk, lru_sf, lru_sb, dhs, dhs, "lru_bwd",
        comms=(_chip_exchange([pb_2], bot), _chip_exchange([pb_o]), _pair_gather([h_1])))
    h_2 = _chip_add(p_2, (q_2a, q_2b), c_idx, "rs_chip_add_w_mlp2")
    h_o = _chip_add(p_o, (q_o,), c_idx, "rs_chip_add_w_out")
    dxr, dcw, dcb = _conv_bwd(dxc_f, dxc_b, xrl, cw, "conv_bwd")
    grad_x, dpb, dn1g, dsh1, dsc1 = _inproj_bwd(
        xt, n1g, sh1, sc1, w4, cos2, sin2, [dq_f, dq_b, dk_f, dk_b, dv_f, dv_b, dg, dxr, dgate], dx1, "inproj_bwd")

    dkc, dvc, drd_c = _ctx_state_bwd(projc, lgv, sgv, ds_f, ds_b)
    zc = jnp.zeros((l_len, LRU_W), F32)
    dhc_f = lax.dynamic_update_slice(zc, dh0_f, (l_len - 1, 0))
    dhc_b = lax.dynamic_update_slice(zc, dh0_b, (0, 0))
    ((dxcc_f, dprec_f, dbac_f, dbxc_f, dlamc_f, _), (dxcc_b, dprec_b, dbac_b, dbxc_b, dlamc_b, _)), _ = _lru_bwd(
        xcc, par_f, par_b, hcf, hcbk, zero_h, zero_h, dhc_f, dhc_b, "lru_bwd_ctx")
    dxrc, dcw_c, dcb_c = _conv_bwd(dxcc_f, dxcc_b, xrc, cw, "conv_bwd_ctx")
    zr = jnp.zeros((l_len, RET_W), BF16)
    _, dpbc, dn1g_c, dcsh1, dcsc1 = _inproj_bwd(
        ctxt, n1g, csh1, csc1, w4, cos_c, sin_c, [zr, zr, dkc, zr, dvc, zr, zr, dxrc, zr],
        jnp.zeros((l_len, D_MODEL), F32), "inproj_bwd_ctx")

    gw_i = _tn(hb16, dpb, N_CHIP, False, True, "grad_w_in", extra=(hcb16, dpbc))
    b_i = gw_i.reshape(N_DEV, D_MODEL // 2, IN_COLS // N_CHIP)
    gwa_f, ((r_i,), (f_2,), (f_o,)) = _tn(xcl, dpre_f, 2, False, True, "grad_lru_gates_f", extra=(xcc, dprec_f),
                                          comms=(_pair_exchange([b_i]), _pair_gather([h_2]), _pair_gather([h_o])))
    p_i, pb_i = _pair_add(b_i, r_i, c_idx, "rs_pair_add_w_in")
    gwa_b, ((q_i,),) = _tn(xcl, dpre_b, 2, False, True, "grad_lru_gates_b", extra=(xcc, dprec_b),
                           comms=(_chip_exchange([pb_i]),))
    slab, ga, gx = _pack_small(
        [lossv, dsh1, dsc1, dg1, dsh2, dsc2, dg2, dcsh1, dcsc1, dn1g, dn1g_c, dn2g, dfg],
        (drd_f, drd_b, drd_c), (dcw, dcw_c), (dcb, dcb_c),
        (dba_f, dbac_f, dba_b, dbac_b, dbx_f, dbxc_f, dbx_b, dbxc_b, dlam_f, dlamc_f, dlam_b, dlamc_b),
        (gwa_f, gwa_b))
    (f_i,), (slab_all, ga_all, gx_all) = _run_comms(
        [_pair_gather([_chip_add(p_i, (q_i,), c_idx, "rs_chip_add_w_in")]), _AllGather([slab, ga, gx])],
        "tail_exchanges")
    g_in, g_out, g_1, g_2 = _shard_of(f_i), _shard_of(f_o), _shard_of(f_1), _shard_of(f_2)
    big = {}
    for nm, w, g, m, v in (("w_in", w_in, g_in, m_w_in, v_w_in), ("w_out", w_out, g_out, m_w_out, v_w_out),
                           ("w_mlp1", w_mlp1, g_1, m_w_mlp1, v_w_mlp1), ("w_mlp2", w_mlp2, g_2, m_w_mlp2, v_w_mlp2)):
        go, d_, mn, vn = _adamw(w[0], g, m[0], v[0], "adamw_" + nm)
        big[nm] = (go[None], d_[None], mn[None], vn[None])
    params = {
        "b_ada": (b_ada, m_b_ada, v_b_ada), "norm1_g": (norm1_g, m_norm1_g, v_norm1_g),
        "norm2_g": (norm2_g, m_norm2_g, v_norm2_g), "final_g": (final_g, m_final_g, v_final_g),
        "ret_decay": (ret_decay, m_ret_decay, v_ret_decay), "conv_w": (conv_w, m_conv_w, v_conv_w),
        "conv_b": (conv_b, m_conv_b, v_conv_b), "lru_wa": (lru_wa, m_lru_wa, v_lru_wa),
        "lru_ba": (lru_ba, m_lru_ba, v_lru_ba), "lru_wx": (lru_wx, m_lru_wx, v_lru_wx),
        "lru_bx": (lru_bx, m_lru_bx, v_lru_bx), "lru_lambda": (lru_lambda, m_lru_lambda, v_lru_lambda),
    }
    as2d = {
        "b_ada": lambda a: a, "norm1_g": lambda a: a, "norm2_g": lambda a: a, "conv_b": lambda a: a,
        "final_g": lambda a: a.reshape(1, D_MODEL), "ret_decay": lambda a: _lane_rep(a.reshape(-1)),
        "conv_w": lambda a: a[0], "lru_ba": lambda a: a[0], "lru_bx": lambda a: a[0], "lru_lambda": lambda a: a[0],
        "lru_wa": lambda a: a.reshape(2 * LRU_W, LRU_BD), "lru_wx": lambda a: a.reshape(2 * LRU_W, LRU_BD),
    }
    res, b128, dmc8, loss8 = _finalize_small(
        j_idx, slab_all, ga_all, gx_all, {nm: tuple(as2d[nm](a) for a in params[nm]) for nm in SMALL_PARAMS})
    loss = loss8[0, 0]
    small_out = {}
    for nm in SMALL_PARAMS:
        shp = params[nm][0].shape
        if nm == "ret_decay":
            small_out[nm] = tuple(o[:, 0].reshape(shp) for o in res[nm])
        else:
            small_out[nm] = tuple(o.reshape(shp) for o in res[nm])

    g_ada = _ada_grad(jnp.pad(a16.T, ((0, 0), (0, LANES - 16))), b128)
    g_ada, d_ada, m_ada, v_ada = _adamw(w_ada[0], g_ada, m_w_ada[0], v_w_ada[0], "adamw_w_ada")

    (cparts,) = _all_gather([_cctx_partial(dmc8, w_ada[0])], "gather_cctx")
    g_cc, d_cc, m_cc, v_cc = _cctx_final(cparts, c_ctx, m_c_ctx, v_c_ctx)
    small_out["c_ctx"] = tuple(a.reshape(D_MODEL) for a in (g_cc, d_cc, m_cc, v_cc))
    small_out["w_ada"] = (g_ada[None], d_ada[None], m_ada[None], v_ada[None])
    small_out.update(big)

    order = ["c_ctx", "w_ada", "b_ada", "norm1_g", "norm2_g", "w_in", "ret_decay", "conv_w", "conv_b", "lru_wa", "lru_ba",
             "lru_wx", "lru_bx", "lru_lambda", "w_out", "w_mlp1", "w_mlp2", "final_g"]
    outs = [loss, grad_x[None]]
    for k in range(4):
        outs += [small_out[nm][k] for nm in order]
    return tuple(outs)
```

```python
import math

import jax
import jax.numpy as jnp
from jax import lax
from jax.experimental import pallas as pl
from jax.experimental.pallas import tpu as pltpu

F32 = jnp.float32
BF16 = jnp.bfloat16

D_MODEL = 1024
HEADS = 4
DH = 128
CHUNK = 256
RET_W = HEADS * DH
LRU_W = 512
LRU_BLOCKS = 8
LRU_BD = LRU_W // LRU_BLOCKS
LRU_C = 8.0
IN_COLS = 4 * RET_W + 2 * LRU_W
MLP_H = 4 * D_MODEL
N_MOD = 6
GRID_W = 64
ROPE_BASE = 10000.0
K_SCALE = DH ** -0.5
EPS = 1e-6
GELU_K = math.sqrt(2.0 / math.pi)
GELU_C = 0.044715

ADAM_LR = 0.001
ADAM_B1 = 0.9
ADAM_B2 = 0.999
ADAM_EPS = 1e-08
ADAM_WD = 0.01
ADAM_STEP = 10

N_DEV = 8
N_CHIP = 4
SUBLANES = 8
LANES = 128
VMEM_LIMIT_V7X = 56 * 1024 * 1024
MESH = pl.DeviceIdType.MESH
ANY = pl.BlockSpec(memory_space=pl.ANY)


def _pc(body, **kw):
    return pl.pallas_call(body, **kw)


def _params(*sem):
    return pltpu.CompilerParams(dimension_semantics=sem if sem else None, vmem_limit_bytes=VMEM_LIMIT_V7X)


def _tile(t, big=False):
    if big and t >= 1024:
        return 512
    return 256 if t >= 256 else t


def _sds(shape, dtype=F32):
    return jax.ShapeDtypeStruct(tuple(shape), dtype)


def _full(shape):
    nd = len(shape)
    return pl.BlockSpec(tuple(shape), lambda *_: (0,) * nd)


def _sigmoid(x):
    return 0.5 * jnp.tanh(0.5 * x) + 0.5


def _log1p_pos(y):
    s = y * (1.0 - y * (0.5 - y * (1.0 / 3.0 - y * (0.25 - y * (0.2 - y / 6.0)))))
    return jnp.where(y < 0.03, s, jnp.log(1.0 + y))


def _softplus(z):
    return jnp.maximum(z, 0.0) + _log1p_pos(jnp.exp(-jnp.abs(z)))


def _one_minus_sq(la, a):
    return -jnp.tanh(la) * (1.0 + a * a)


def _rms(x):
    r = lax.rsqrt(jnp.mean(x * x, axis=-1, keepdims=True) + EPS)
    return x * r, r


def _dot(a, b):
    return jnp.dot(a, b, preferred_element_type=F32)


def _dot_nt(a, b):
    return lax.dot_general(a, b, (((1,), (1,)), ((), ())), preferred_element_type=F32)


def _dot_tn(a, b):
    return lax.dot_general(a, b, (((0,), (0,)), ((), ())), preferred_element_type=F32)


def _sum0(x):
    return jnp.sum(x, axis=0, keepdims=True)


def _norm_mod_bwd(x, g, sc, dh):
    xh, r = _rms(x)
    hn = xh * g
    dhn = dh * (1.0 + sc)
    dxh = dhn * g
    dx = r * (dxh - xh * jnp.mean(dxh * xh, axis=-1, keepdims=True))
    return dx, _sum0(dhn * xh), _sum0(dh), _sum0(dh * hn)


def _dev_index(p):
    return 4 * p[0] + 2 * p[1] + p[2]


def _mesh_pos():
    return lax.axis_index("x"), lax.axis_index("y"), lax.axis_index("c")


class _AllGather:
    def __init__(self, arrs):
        n = len(arrs)
        self.arrays = list(arrs)
        self.out_shapes = [_sds((N_DEV,) + a.shape, a.dtype) for a in arrs]
        self.scratch = ([pltpu.VMEM(a.shape, a.dtype) for a in arrs]
                        + [pltpu.SemaphoreType.DMA((7 * n,)), pltpu.SemaphoreType.DMA((7 * n,)),
                           pltpu.SemaphoreType.DMA((n,))])
        self.aliases = {}

    def _parts(self, ins, outs, scr):
        n = len(self.arrays)
        stage = scr[:n]
        send_sems, recv_sems, local_sems = scr[n:]
        x, y, c = _mesh_pos()
        me, sib = (x, y, c), (x, y, 1 - c)
        chips = [(1 - x, y), (x, 1 - y), (1 - x, 1 - y)]

        def copy(t, k, block, to, own=False):
            dst = outs[t].at[_dev_index(block)]
            return pltpu.make_async_remote_copy(
                src_ref=ins[t] if own else dst, dst_ref=dst,
                send_sem=send_sems.at[7 * t + k], recv_sem=recv_sems.at[7 * t + k],
                device_id=to, device_id_type=MESH)

        first = []
        for t in range(n):
            first.append(copy(t, 0, me, sib, own=True))
            for j, ch in enumerate(chips):
                first.append(copy(t, 1 + j, me, (*ch, c), own=True))
        stage_in = [pltpu.make_async_copy(ins[t], stage[t], local_sems.at[t]) for t in range(n)]
        mine = [pltpu.make_async_copy(stage[t], outs[t].at[_dev_index(me)], local_sems.at[t]) for t in range(n)]
        return n, c, me, sib, chips, copy, first, stage_in, mine

    def start(self, ins, outs, scr):
        n, _, _, _, _, _, first, stage_in, mine = self._parts(ins, outs, scr)
        for cp in stage_in:
            cp.start()
        for cp in first:
            cp.start()
        for t in range(n):
            stage_in[t].wait()
            mine[t].start()

    def relay(self, ins, outs, scr):
        n, c, me, sib, chips, copy, _, _, _ = self._parts(ins, outs, scr)
        for j, ch in enumerate(chips):
            for t in range(n):
                copy(t, 1 + j, (*ch, c), me).wait_recv()
                copy(t, 4 + j, (*ch, c), sib).start()

    def finish(self, ins, outs, scr):
        n, c, me, sib, chips, copy, first, _, mine = self._parts(ins, outs, scr)
        passed = [copy(t, 4 + j, (*ch, c), sib) for j, ch in enumerate(chips) for t in range(n)]
        for t in range(n):
            copy(t, 0, sib, me).wait_recv()
            for j, ch in enumerate(chips):
                copy(t, 4 + j, (*ch, 1 - c), me).wait_recv()
        for cp in first + passed:
            cp.wait_send()
        for cp in mine:
            cp.wait()


class _Exchange:
    def __init__(self, arrays, out_shapes, plan, n_copies, aliases=None):
        self.arrays = list(arrays)
        self.out_shapes = list(out_shapes)
        self.plan = plan
        self.scratch = [pltpu.SemaphoreType.DMA((n_copies,)), pltpu.SemaphoreType.DMA((n_copies,))]
        self.aliases = aliases or {}

    def _copies(self, ins, outs, scr):
        send_sems, recv_sems = scr
        snd, rcv = [], []
        for i, (src, dst, peer, lands) in enumerate(self.plan(ins, outs, _mesh_pos())):
            kw = dict(send_sem=send_sems.at[i], recv_sem=recv_sems.at[i], device_id=peer, device_id_type=MESH)
            snd.append(pltpu.make_async_remote_copy(src_ref=src, dst_ref=dst, **kw))
            rcv.append(pltpu.make_async_remote_copy(src_ref=src, dst_ref=lands, **kw))
        return snd, rcv

    def start(self, ins, outs, scr):
        for cp in self._copies(ins, outs, scr)[0]:
            cp.start()

    def relay(self, ins, outs, scr):
        pass

    def finish(self, ins, outs, scr):
        snd, rcv = self._copies(ins, outs, scr)
        for cp in rcv:
            cp.wait_recv()
        for cp in snd:
            cp.wait_send()


def _pair_exchange(grads):
    n = len(grads)

    def plan(ins, outs, pos):
        x, y, c = pos
        return [(ins[t].at[2 * j + (1 - c)], outs[t].at[j], (x, y, 1 - c), outs[t].at[j])
                for t in range(n) for j in range(N_CHIP)]

    return _Exchange(grads, [_sds((N_CHIP,) + g.shape[1:], g.dtype) for g in grads], plan, N_CHIP * n)


def _chip_exchange(parts, rows=None):
    n = len(parts)

    def plan(ins, outs, pos):
        x, y, c = pos
        chips = [(1 - x, y), (x, 1 - y), (1 - x, 1 - y)]

        def src(t, ch):
            blk = ins[t].at[2 * ch[0] + ch[1]]
            return blk if rows is None else blk.at[pl.ds(rows[0], rows[1])]

        return [(src(t, ch), outs[t].at[k], (*ch, c), outs[t].at[k]) for t in range(n) for k, ch in enumerate(chips)]

    shapes = [_sds((3, p.shape[1] if rows is None else rows[1]) + p.shape[2:], p.dtype) for p in parts]
    return _Exchange(parts, shapes, plan, 3 * n)


def _pair_gather(bufs):
    n = len(bufs)

    def plan(ins, outs, pos):
        x, y, c = pos
        return [(ins[t].at[c], outs[t].at[c], (x, y, 1 - c), outs[t].at[1 - c]) for t in range(n)]

    return _Exchange(bufs, [_sds(b.shape, b.dtype) for b in bufs], plan, n, aliases={t: t for t in range(n)})


def _run_comms(comms, name):
    c_in = [len(cm.arrays) for cm in comms]
    c_out = [len(cm.out_shapes) for cm in comms]
    c_scr = [len(cm.scratch) for cm in comms]
    aliases = {}
    for k, cm in enumerate(comms):
        for a, b in cm.aliases.items():
            aliases[sum(c_in[:k]) + a] = sum(c_out[:k]) + b

    def split(refs, counts):
        out, pos = [], 0
        for cnt in counts:
            out.append(refs[pos:pos + cnt])
            pos += cnt
        return out

    def body(*refs):
        ins = split(refs[:sum(c_in)], c_in)
        outs = split(refs[sum(c_in):sum(c_in) + sum(c_out)], c_out)
        scr = split(refs[sum(c_in) + sum(c_out):], c_scr)
        for phase in ("start", "relay", "finish"):
            for k, cm in enumerate(comms):
                getattr(cm, phase)(ins[k], outs[k], scr[k])

    outs = _pc(body, name=name, out_shape=[s for cm in comms for s in cm.out_shapes],
               in_specs=[ANY] * sum(c_in), out_specs=[ANY] * sum(c_out), input_output_aliases=aliases,
               scratch_shapes=[s for cm in comms for s in cm.scratch],
               compiler_params=_params())(*[a for cm in comms for a in cm.arrays])
    return split(list(outs), c_out)


def _all_gather(arrs, name):
    return _run_comms([_AllGather(arrs)], name)[0]


def _call(body, *, name, grid, in_specs, out_specs, out_shape, scratch_shapes, sem, args, comms=()):
    n_in, n_out, n_scr = len(in_specs), len(out_specs), len(scratch_shapes)
    c_in = [len(cm.arrays) for cm in comms]
    c_out = [len(cm.out_shapes) for cm in comms]
    c_scr = [len(cm.scratch) for cm in comms]
    aliases = {}
    for k, cm in enumerate(comms):
        for a, b in cm.aliases.items():
            aliases[n_in + sum(c_in[:k]) + a] = n_out + sum(c_out[:k]) + b

    def split(refs, counts):
        out, pos = [], 0
        for cnt in counts:
            out.append(refs[pos:pos + cnt])
            pos += cnt
        return out

    def wrapped(*refs):
        ins = refs[:n_in + sum(c_in)]
        outs = refs[len(ins):len(ins) + n_out + sum(c_out)]
        scr = refs[len(ins) + len(outs):]
        cins, couts, cscr = split(ins[n_in:], c_in), split(outs[n_out:], c_out), split(scr[n_scr:], c_scr)
        if comms:
            first = pl.program_id(0) == 0
            last = pl.program_id(0) == grid[0] - 1
            for k in range(1, len(grid)):
                first = jnp.logical_and(first, pl.program_id(k) == 0)
                last = jnp.logical_and(last, pl.program_id(k) == grid[k] - 1)

            @pl.when(first)
            def _():
                for k, cm in enumerate(comms):
                    cm.start(cins[k], couts[k], cscr[k])
        body(*ins[:n_in], *outs[:n_out], *scr[:n_scr])
        if comms:
            relay_early = len(grid) == 1 and grid[0] >= 4
            if relay_early:
                @pl.when(pl.program_id(0) == (7 * grid[0]) // 8 - 1)
                def _():
                    for k, cm in enumerate(comms):
                        cm.relay(cins[k], couts[k], cscr[k])

            @pl.when(last)
            def _():
                for k, cm in enumerate(comms):
                    if not relay_early:
                        cm.relay(cins[k], couts[k], cscr[k])
                    cm.finish(cins[k], couts[k], cscr[k])

    outs = _pc(wrapped, name=name, grid=grid,
               in_specs=list(in_specs) + [ANY] * sum(c_in), out_specs=list(out_specs) + [ANY] * sum(c_out),
               out_shape=list(out_shape) + [s for cm in comms for s in cm.out_shapes],
               scratch_shapes=list(scratch_shapes) + [s for cm in comms for s in cm.scratch],
               input_output_aliases=aliases, compiler_params=_params(*sem),
               )(*args, *[a for cm in comms for a in cm.arrays])
    outs = list(outs)
    return outs[:n_out], split(outs[n_out:], c_out)


def _row_block(r):
    for b in (512, 256, 128, 64, 32, 16, 8):
        if r % b == 0:
            return b
    return r


def _pair_add(g, recv, cj_idx, name):
    _, r, cc = g.shape
    br = _row_block(r)

    def body(cj_ref, g_ref, r_ref, own_ref, pb_ref):
        s = g_ref[...] + r_ref[...]
        pb_ref[...] = s.astype(BF16)

        @pl.when(pl.program_id(1) == cj_ref[1])
        def _():
            own_ref[...] = s[0]

    grid_spec = pltpu.PrefetchScalarGridSpec(
        num_scalar_prefetch=1, grid=(r // br, N_CHIP),
        in_specs=[pl.BlockSpec((1, br, cc), lambda i, j, cj_ref: (2 * j + cj_ref[0], i, 0)),
                  pl.BlockSpec((1, br, cc), lambda i, j, cj_ref: (j, i, 0))],
        out_specs=[pl.BlockSpec((br, cc), lambda i, j, cj_ref: (i, 0)),
                   pl.BlockSpec((1, br, cc), lambda i, j, cj_ref: (j, i, 0))])
    return _pc(body, name=name, grid_spec=grid_spec,
               out_shape=[_sds((r, cc)), _sds((N_CHIP, r, cc), BF16)],
               compiler_params=_params("arbitrary", "arbitrary"))(cj_idx, g, recv)


def _chip_add(p, qs, cj_idx, name):
    r, cc = p.shape
    nq = len(qs)
    br = _row_block(r // nq)
    nb = r // nq // br

    def body(cj_ref, p_ref, *refs):
        o_ref = refs[-1]
        if nq == 2:
            top = pl.program_id(0) < nb
            q = [jnp.where(top, refs[0][k], refs[1][k]).astype(F32) for k in range(3)]
        else:
            q = [refs[0][k].astype(F32) for k in range(3)]
        o_ref[0] = ((p_ref[...] + q[0]) + q[1]) + q[2]

    q_specs = [pl.BlockSpec((3, br, cc), lambda i, cj_ref, h=h: (0, jnp.clip(i - h * nb, 0, nb - 1), 0))
               for h in range(nq)]
    grid_spec = pltpu.PrefetchScalarGridSpec(
        num_scalar_prefetch=1, grid=(r // br,),
        in_specs=[pl.BlockSpec((br, cc), lambda i, cj_ref: (i, 0))] + q_specs,
        out_specs=pl.BlockSpec((1, br, cc), lambda i, cj_ref: (cj_ref[0], i, 0)))
    return _pc(body, name=name, grid_spec=grid_spec, out_shape=_sds((2, r, cc)),
               compiler_params=_params("arbitrary"))(cj_idx, p, *qs)


def _shard_of(both):
    return both.reshape((2 * both.shape[1],) + both.shape[2:])


ADAMW_CHUNKS = 4


def _adamw(w, g, m, v, name):
    r, cc = w.shape
    rows = r // ADAMW_CHUNKS
    assert rows * ADAMW_CHUNKS == r and rows % SUBLANES == 0
    c1 = 1.0 - ADAM_B1 ** ADAM_STEP
    c2 = 1.0 - ADAM_B2 ** ADAM_STEP

    def body(w_hbm, g_hbm, m_hbm, v_hbm, go_hbm, d_hbm, mo_hbm, vo_hbm, wb, gb, mb, vb, sem_in, sem_out):
        srcs, bufs, dsts = (w_hbm, g_hbm, m_hbm, v_hbm), (wb, gb, mb, vb), (d_hbm, go_hbm, mo_hbm, vo_hbm)

        def load(a, k):
            sl = pl.ds(k * rows, rows)
            return pltpu.make_async_copy(srcs[a].at[sl], bufs[a].at[sl], sem_in.at[a, k])

        def store(a, k):
            sl = pl.ds(k * rows, rows)
            return pltpu.make_async_copy(bufs[a].at[sl], dsts[a].at[sl], sem_out.at[a, k])

        for k in range(ADAMW_CHUNKS):
            for a in range(4):
                load(a, k).start()
        for k in range(ADAMW_CHUNKS):
            for a in range(4):
                load(a, k).wait()
            store(1, k).start()
            sl = pl.ds(k * rows, rows)
            gg = gb[sl]
            mn = ADAM_B1 * mb[sl] + (1.0 - ADAM_B1) * gg
            vn = ADAM_B2 * vb[sl] + (1.0 - ADAM_B2) * (gg * gg)
            mh = mn / c1
            vh = vn / c2
            wb[sl] = -ADAM_LR * (mh / (jnp.sqrt(vh) + ADAM_EPS) + ADAM_WD * wb[sl])
            mb[sl] = mn
            vb[sl] = vn
            for a in (0, 2, 3):
                store(a, k).start()
        for k in range(ADAMW_CHUNKS):
            for a in range(4):
                store(a, k).wait()

    go, d, mo, vo = _pc(body, name=name, in_specs=[ANY] * 4, out_specs=[ANY] * 4, out_shape=[_sds((r, cc))] * 4,
                        scratch_shapes=[pltpu.VMEM((r, cc), F32)] * 4 + [pltpu.SemaphoreType.DMA((4, ADAMW_CHUNKS))] * 2,
                        compiler_params=_params())(w, g, m, v)
    return go, d, mo, vo


def _head(w_half, c8, small, w_ada, b_shard, c_ctx, ret_decay):
    ada_n = w_ada.shape[1]
    mod_sds = _sds((16, ada_n))
    ag_w, ag_c, ag_m = _AllGather([w_half]), _AllGather([c8, small]), _AllGather([mod_sds])
    n_w, n_c, n_m = len(ag_w.scratch), len(ag_c.scratch), len(ag_m.scratch)

    def body(w_ref, c_ref, s_ref, wada_ref, b_ref, cc_ref, rd_ref,
             gw_ref, call_ref, sall_ref, a_ref, modp_ref, mall_ref, lg_ref, sg_ref, *scr):
        scr_w, scr_c, scr_m = scr[:n_w], scr[n_w:n_w + n_c], scr[n_w + n_c:n_w + n_c + n_m]
        c_v, w_v, m_v, sems = scr[n_w + n_c + n_m:]
        ag_w.start((w_ref,), (gw_ref,), scr_w)
        ag_c.start((c_ref, s_ref), (call_ref, sall_ref), scr_c)
        load_w = pltpu.make_async_copy(wada_ref, w_v, sems.at[0])
        load_w.start()
        rd = rd_ref[...]
        lg_ref[...] = -_softplus(-rd)
        sg_ref[...] = _sigmoid(-rd)
        ag_c.relay((c_ref, s_ref), (call_ref, sall_ref), scr_c)
        ag_c.finish((c_ref, s_ref), (call_ref, sall_ref), scr_c)
        load_c = pltpu.make_async_copy(call_ref, c_v, sems.at[1])
        load_c.start()
        load_c.wait()
        a_ref[...] = jnp.zeros_like(a_ref)
        for d in range(N_DEV):
            cd = c_v[d, 0:1, :]
            a_ref[d:d + 1, :] = cd * _sigmoid(cd)
        cc = cc_ref[...]
        a_ref[N_DEV:N_DEV + 1, :] = cc * _sigmoid(cc)
        load_w.wait()
        m_v[...] = jnp.dot(a_ref[...], w_v[...], preferred_element_type=F32,
                           precision=lax.Precision.HIGHEST) + b_ref[...]
        put = pltpu.make_async_copy(m_v, modp_ref, sems.at[2])
        put.start()
        put.wait()
        ag_m.start((modp_ref,), (mall_ref,), scr_m)
        ag_m.relay((modp_ref,), (mall_ref,), scr_m)
        ag_m.finish((modp_ref,), (mall_ref,), scr_m)
        ag_w.relay((w_ref,), (gw_ref,), scr_w)
        ag_w.finish((w_ref,), (gw_ref,), scr_w)

    rd = jnp.broadcast_to(ret_decay.reshape(2, HEADS).T[:, :, None], (HEADS, 2, LANES))
    lane = _full((HEADS, 2, LANES))
    outs = _pc(
        body, name="head",
        in_specs=[ANY, ANY, ANY, ANY, _full((1, ada_n)), _full((1, D_MODEL)), lane],
        out_specs=[ANY, ANY, ANY, _full((16, D_MODEL)), ANY, ANY, lane, lane],
        out_shape=ag_w.out_shapes + ag_c.out_shapes + [_sds((16, D_MODEL)), mod_sds] + ag_m.out_shapes
        + [_sds((HEADS, 2, LANES))] * 2,
        scratch_shapes=ag_w.scratch + ag_c.scratch + ag_m.scratch
        + [pltpu.VMEM((N_DEV,) + c8.shape, F32), pltpu.VMEM(w_ada.shape, F32), pltpu.VMEM((16, ada_n), F32),
           pltpu.SemaphoreType.DMA((3,))],
        compiler_params=_params(),
    )(w_half, c8, small, w_ada, b_shard, c_ctx.reshape(1, D_MODEL), rd)
    gw, c_all, small_all, a16, _, mod_all, lgv, sgv = outs
    return gw, c_all, small_all, a16, mod_all, lgv, sgv


def _ada_grad(at, b):
    n = b.shape[1]
    bn = 512

    def body(a_ref, b_ref, o_ref):
        o_ref[...] = jnp.dot(a_ref[...], b_ref[...], preferred_element_type=F32, precision=lax.Precision.HIGHEST)

    return _pc(body, name="ada_grad", grid=(n // bn,),
               in_specs=[_full((D_MODEL, LANES)), pl.BlockSpec((LANES, bn), lambda i: (0, i))],
               out_specs=pl.BlockSpec((D_MODEL, bn), lambda i: (0, i)), out_shape=_sds((D_MODEL, n)),
               compiler_params=_params("arbitrary"))(at, b)


def _cctx_partial(dmc8, w_ada):
    n = w_ada.shape[1]
    bn = 512

    def body(d_ref, w_ref, o_ref):
        @pl.when(pl.program_id(0) == 0)
        def _():
            o_ref[...] = jnp.zeros_like(o_ref)
        o_ref[...] += lax.dot_general(d_ref[...], w_ref[...], (((1,), (1,)), ((), ())),
                                      preferred_element_type=F32, precision=lax.Precision.HIGHEST)

    return _pc(body, name="cctx_partial", grid=(n // bn,),
               in_specs=[pl.BlockSpec((8, bn), lambda i: (0, i)), pl.BlockSpec((D_MODEL, bn), lambda i: (0, i))],
               out_specs=_full((8, D_MODEL)), out_shape=_sds((8, D_MODEL)),
               compiler_params=_params("arbitrary"))(dmc8, w_ada)


def _cctx_final(parts, c_ctx, m, v):
    c1 = 1.0 - ADAM_B1 ** ADAM_STEP
    c2 = 1.0 - ADAM_B2 ** ADAM_STEP

    def body(p_ref, c_ref, m_ref, v_ref, g_ref, d_ref, mo_ref, vo_ref):
        s = ((p_ref[0, 0:1, :] + p_ref[2, 0:1, :]) + p_ref[4, 0:1, :]) + p_ref[6, 0:1, :]
        z = c_ref[...]
        sg = _sigmoid(z)
        gg = s * (sg * (1.0 + z * (1.0 - sg)))
        g_ref[...] = gg
        mn = ADAM_B1 * m_ref[...] + (1.0 - ADAM_B1) * gg
        vn = ADAM_B2 * v_ref[...] + (1.0 - ADAM_B2) * (gg * gg)
        d_ref[...] = -ADAM_LR * ((mn / c1) / (jnp.sqrt(vn / c2) + ADAM_EPS) + ADAM_WD * z)
        mo_ref[...] = mn
        vo_ref[...] = vn

    row = _full((1, D_MODEL))
    return _pc(body, name="cctx_final", out_shape=[_sds((1, D_MODEL))] * 4,
               in_specs=[_full(parts.shape), row, row, row], out_specs=[row] * 4,
               compiler_params=_params())(parts, c_ctx.reshape(1, D_MODEL), m.reshape(1, D_MODEL), v.reshape(1, D_MODEL))


def _rotary_tables(t_len):
    rows = t_len // GRID_W
    n_freq = DH // 4
    inv = ROPE_BASE ** (-jnp.arange(n_freq, dtype=F32) / n_freq)
    row_ang = jnp.arange(rows, dtype=F32)[:, None] * inv
    col_ang = jnp.arange(GRID_W, dtype=F32)[:, None] * inv

    def spread(fn):
        return jnp.concatenate([jnp.repeat(fn(row_ang), GRID_W, axis=0), jnp.tile(fn(col_ang), (rows, 1))], axis=-1)

    cos, sin = spread(jnp.cos), spread(jnp.sin)
    return jnp.concatenate([cos, cos], axis=-1), jnp.concatenate([-sin, sin], axis=-1)


def _inproj_fwd(x, gn, sh, sc, w4, cos2, sin2, name, comms=()):
    t = x.shape[0]
    tm = _tile(t, True)
    nc = IN_COLS // N_CHIP

    def body(x_ref, gn_ref, sh_ref, sc_ref, w_ref, c_ref, s_ref, p_ref, xr_ref, hb_ref, p_s):
        xh, _ = _rms(x_ref[...])
        h = xh * gn_ref[...] * (1.0 + sc_ref[...]) + sh_ref[...]
        hb = h.astype(BF16)
        hb_ref[...] = hb
        for j in range(N_CHIP):
            p_s[:, nc * j:nc * (j + 1)] = _dot(hb, w_ref[j])
        cc = c_ref[...]
        ss = s_ref[...]
        for hh in range(2 * HEADS):
            blk = p_s[:, DH * hh:DH * (hh + 1)]
            rot = blk * cc + pltpu.roll(blk, DH // 2, 1) * ss
            if hh >= HEADS:
                rot = rot * K_SCALE
            p_ref[:, DH * hh:DH * (hh + 1)] = rot.astype(BF16)
        p_ref[:, 2 * RET_W:] = p_s[:, 2 * RET_W:].astype(BF16)
        xr_ref[...] = p_s[:, 4 * RET_W:4 * RET_W + LRU_W]

    row = _full((1, D_MODEL))
    outs, couts = _call(
        body, name=name, grid=(t // tm,),
        in_specs=[pl.BlockSpec((tm, D_MODEL), lambda i: (i, 0)), row, row, row, _full(w4.shape),
                  pl.BlockSpec((tm, DH), lambda i: (i, 0)), pl.BlockSpec((tm, DH), lambda i: (i, 0))],
        out_specs=[pl.BlockSpec((tm, IN_COLS), lambda i: (i, 0)), pl.BlockSpec((tm, LRU_W), lambda i: (i, 0)),
                   pl.BlockSpec((tm, D_MODEL), lambda i: (i, 0))],
        out_shape=[_sds((t, IN_COLS), BF16), _sds((t, LRU_W)), _sds((t, D_MODEL), BF16)],
        scratch_shapes=[pltpu.VMEM((tm, IN_COLS), F32)], sem=("arbitrary",),
        args=(x, gn, sh, sc, w4, cos2, sin2), comms=comms)
    return (outs, couts) if comms else outs


def _inproj_bwd(x, gn, sh, sc, w4, cos2, sin2, pieces, dres, name):
    t = x.shape[0]
    tm = _tile(t)
    nc = IN_COLS // N_CHIP

    def body(x_ref, gn_ref, sh_ref, sc_ref, w_ref, c_ref, s_ref, dqf, dqb, dkf, dkb, dvf, dvb, dg, dxr, dgt, dres_ref,
             dx_ref, dpb_ref, dgn_ref, dsh_ref, dsc_ref):
        cc = c_ref[...]
        ss = s_ref[...]
        dq = dqf[...].astype(F32) + dqb[...].astype(F32)
        dk = dkf[...].astype(F32) + dkb[...].astype(F32)
        for hh in range(HEADS):
            sl = slice(DH * hh, DH * (hh + 1))
            b = dq[:, sl]
            dpb_ref[:, sl] = (b * cc + pltpu.roll(b * ss, DH // 2, 1)).astype(BF16)
            b = dk[:, sl]
            dpb_ref[:, RET_W + DH * hh:RET_W + DH * (hh + 1)] = (
                (b * cc + pltpu.roll(b * ss, DH // 2, 1)) * K_SCALE).astype(BF16)
        dpb_ref[:, 2 * RET_W:3 * RET_W] = (dvf[...].astype(F32) + dvb[...].astype(F32)).astype(BF16)
        dpb_ref[:, 3 * RET_W:4 * RET_W] = dg[...].astype(BF16)
        dpb_ref[:, 4 * RET_W:4 * RET_W + LRU_W] = dxr[...].astype(BF16)
        dpb_ref[:, 4 * RET_W + LRU_W:IN_COLS] = dgt[...].astype(BF16)
        dh = _dot_nt(dpb_ref[:, 0:nc], w_ref[0])
        for j in range(1, N_CHIP):
            dh = dh + _dot_nt(dpb_ref[:, nc * j:nc * (j + 1)], w_ref[j])
        dx, dgn_t, dsh_t, dsc_t = _norm_mod_bwd(x_ref[...], gn_ref[...], sc_ref[...], dh)
        dx_ref[...] = dres_ref[...] + dx

        @pl.when(pl.program_id(0) == 0)
        def _():
            dgn_ref[...] = jnp.zeros_like(dgn_ref)
            dsh_ref[...] = jnp.zeros_like(dsh_ref)
            dsc_ref[...] = jnp.zeros_like(dsc_ref)
        dgn_ref[...] += dgn_t
        dsh_ref[...] += dsh_t
        dsc_ref[...] += dsc_t

    row = _full((1, D_MODEL))
    pc = pl.BlockSpec((tm, RET_W), lambda i: (i, 0))
    big = pl.BlockSpec((tm, D_MODEL), lambda i: (i, 0))
    return _pc(body, name=name, grid=(t // tm,),
               in_specs=[big, row, row, row, _full(w4.shape),
                         pl.BlockSpec((tm, DH), lambda i: (i, 0)), pl.BlockSpec((tm, DH), lambda i: (i, 0))]
               + [pc] * 9 + [big],
               out_specs=[big, pl.BlockSpec((tm, IN_COLS), lambda i: (i, 0)), row, row, row],
               out_shape=[_sds((t, D_MODEL)), _sds((t, IN_COLS), BF16), _sds((1, D_MODEL)), _sds((1, D_MODEL)),
                          _sds((1, D_MODEL))],
               compiler_params=_params("arbitrary"))(x, gn, sh, sc, w4, cos2, sin2, *pieces, dres)


def _halo_specs(t, tm):
    n8 = tm // SUBLANES
    last8 = t // SUBLANES - 1
    prev = pl.BlockSpec((SUBLANES, LRU_W), lambda i: (jnp.maximum(i * n8 - 1, 0), 0))
    main = pl.BlockSpec((tm, LRU_W), lambda i: (i, 0))
    nxt = pl.BlockSpec((SUBLANES, LRU_W), lambda i: (jnp.minimum((i + 1) * n8, last8), 0))
    return prev, main, nxt


def _with_halo(prev_ref, main_ref, next_ref, i, nt):
    prev = jnp.where(i > 0, prev_ref[...], 0.0)
    nxt = jnp.where(i < nt - 1, next_ref[...], 0.0)
    return jnp.concatenate([prev, main_ref[...], nxt], axis=0)


def _conv_fwd(xr, cw, cb, name):
    t = xr.shape[0]
    tm = _tile(t, True)
    nt = t // tm
    n = tm + 2 * SUBLANES
    mid = slice(SUBLANES, SUBLANES + tm)

    def body(p_ref, m_ref, n_ref, w_ref, b_ref, o_ref):
        xp = _with_halo(p_ref, m_ref, n_ref, pl.program_id(0), nt)
        acc = b_ref[...] + pltpu.roll(xp, 1, 0)[mid] * w_ref[0:1, :]
        acc = acc + xp[mid] * w_ref[1:2, :]
        acc = acc + pltpu.roll(xp, n - 1, 0)[mid] * w_ref[2:3, :]
        acc = acc + pltpu.roll(xp, n - 2, 0)[mid] * w_ref[3:4, :]
        o_ref[...] = acc

    return _pc(body, name=name, grid=(nt,),
               in_specs=[*_halo_specs(t, tm), _full((4, LRU_W)), _full((1, LRU_W))],
               out_specs=pl.BlockSpec((tm, LRU_W), lambda i: (i, 0)), out_shape=_sds((t, LRU_W)),
               compiler_params=_params("arbitrary"))(xr, xr, xr, cw, cb)


def _conv_bwd(dxc_a, dxc_b, xr, cw, name):
    t = xr.shape[0]
    tm = _tile(t, True)
    nt = t // tm
    n = tm + 2 * SUBLANES
    mid = slice(SUBLANES, SUBLANES + tm)

    def body(ap_ref, am_ref, an_ref, bp_ref, bm_ref, bn_ref, xp_ref, xm_ref, xn_ref, w_ref, dx_ref, dw_ref, db_ref):
        i = pl.program_id(0)
        dp = _with_halo(ap_ref, am_ref, an_ref, i, nt) + _with_halo(bp_ref, bm_ref, bn_ref, i, nt)
        xp = _with_halo(xp_ref, xm_ref, xn_ref, i, nt)
        dx = pltpu.roll(dp, n - 1, 0)[mid] * w_ref[0:1, :]
        dx = dx + dp[mid] * w_ref[1:2, :]
        dx = dx + pltpu.roll(dp, 1, 0)[mid] * w_ref[2:3, :]
        dx = dx + pltpu.roll(dp, 2, 0)[mid] * w_ref[3:4, :]
        dx_ref[...] = dx.astype(BF16)
        d = dp[mid]

        @pl.when(i == 0)
        def _():
            dw_ref[...] = jnp.zeros_like(dw_ref)
            db_ref[...] = jnp.zeros_like(db_ref)
        dw_ref[0:1, :] += _sum0(d * pltpu.roll(xp, 1, 0)[mid])
        dw_ref[1:2, :] += _sum0(d * xp[mid])
        dw_ref[2:3, :] += _sum0(d * pltpu.roll(xp, n - 1, 0)[mid])
        dw_ref[3:4, :] += _sum0(d * pltpu.roll(xp, n - 2, 0)[mid])
        db_ref[...] += _sum0(d)

    return _pc(body, name=name, grid=(nt,),
               in_specs=[*_halo_specs(t, tm), *_halo_specs(t, tm), *_halo_specs(t, tm), _full((4, LRU_W))],
               out_specs=[pl.BlockSpec((tm, LRU_W), lambda i: (i, 0)), _full((4, LRU_W)), _full((1, LRU_W))],
               out_shape=[_sds((t, LRU_W), BF16), _sds((4, LRU_W)), _sds((1, LRU_W))],
               compiler_params=_params("arbitrary"))(dxc_a, dxc_a, dxc_a, dxc_b, dxc_b, dxc_b, xr, xr, xr, cw)


def _scan_scratch(n, c):
    return pltpu.VMEM((c // LANES, n, LANES), F32)


def _to_lane_blocks(ref, val):
    for lb in range(ref.shape[0]):
        ref[lb] = val[:, lb * LANES:(lb + 1) * LANES]


def _group_scan(a_s, b_s, reverse):
    nb, n, _ = a_s.shape
    ng = n // SUBLANES
    order = range(SUBLANES - 1, -1, -1) if reverse else range(SUBLANES)
    for lb in range(nb):
        prev = None
        for r in order:
            rows = pl.ds(r, ng, stride=SUBLANES)
            a_r, b_r = a_s[lb, rows, :], b_s[lb, rows, :]
            if prev is not None:
                b_r = a_r * prev[1] + b_r
                a_r = a_r * prev[0]
                a_s[lb, rows, :] = a_r
                b_s[lb, rows, :] = b_r
            prev = (a_r, b_r)


def _carry_scans(jobs):
    nb, n, _ = jobs[0][0].shape
    ng = n // SUBLANES

    def step(g, all_crs):
        res = []
        for (a_s, b_s, out_ref, _, reverse), crs in zip(jobs, all_crs):
            gg = (ng - 1 - g) if reverse else g
            off = pl.multiple_of(gg * SUBLANES, SUBLANES)
            new = []
            for lb in range(nb):
                h = a_s[lb, pl.ds(off, SUBLANES), :] * crs[lb] + b_s[lb, pl.ds(off, SUBLANES), :]
                out_ref[pl.ds(off, SUBLANES), pl.ds(lb * LANES, LANES)] = h
                edge = h[0:1, :] if reverse else h[SUBLANES - 1:SUBLANES, :]
                new.append(jnp.broadcast_to(edge, (SUBLANES, LANES)))
            res.append(tuple(new))
        return tuple(res)

    init = tuple(tuple(job[3][:, lb * LANES:(lb + 1) * LANES] for lb in range(nb)) for job in jobs)
    return [jnp.concatenate(crs, axis=1) for crs in lax.fori_loop(0, ng, step, init, unroll=4)]


def _lru_gates(xc, wa_ref, wx_ref, ba, bx, lam):
    xb = xc.astype(BF16)
    r = _sigmoid(_dot(xb, wa_ref[...]) + ba)
    ig = _sigmoid(_dot(xb, wx_ref[...]) + bx)
    sp = _softplus(-lam)
    la = -LRU_C * r * sp
    a = jnp.exp(la)
    return r, ig, sp, a, _one_minus_sq(la, a)


def _lru_fwd(xc, par_f, par_b, h0_f, h0_b, name, comms=()):
    t = xc.shape[0]
    tm = _tile(t, True)
    nt = t // tm

    def one(x_ref, prm, h0_ref, a_s, b_s, c_s, reverse):
        wa_ref, wx_ref, ba_ref, bx_ref, lam_ref = prm

        @pl.when(pl.program_id(0) == 0)
        def _():
            c_s[...] = jnp.broadcast_to(h0_ref[...], c_s.shape)
        xv = x_ref[...]
        _, ig, _, a, q = _lru_gates(xv, wa_ref, wx_ref, ba_ref[...], bx_ref[...], lam_ref[...])
        _to_lane_blocks(a_s, a)
        _to_lane_blocks(b_s, jnp.sqrt(q) * (ig * xv))
        _group_scan(a_s, b_s, reverse)

    def body(xf_ref, xb_ref, *refs):
        prm_f, prm_b = refs[0:5], refs[5:10]
        h0f_ref, h0b_ref, hf_ref, hb_ref = refs[10:14]
        af_s, bf_s, cf_s, ab_s, bb_s, cb_s = refs[14:]
        one(xf_ref, prm_f, h0f_ref, af_s, bf_s, cf_s, False)
        one(xb_ref, prm_b, h0b_ref, ab_s, bb_s, cb_s, True)
        cf_s[...], cb_s[...] = _carry_scans([(af_s, bf_s, hf_ref, cf_s[...], False),
                                             (ab_s, bb_s, hb_ref, cb_s[...], True)])

    vec = _full((1, LRU_W))
    mat = _full((LRU_W, LRU_W))
    fw = pl.BlockSpec((tm, LRU_W), lambda i: (i, 0))
    bw = pl.BlockSpec((tm, LRU_W), lambda i: (nt - 1 - i, 0))
    tile_s = [_scan_scratch(tm, LRU_W), _scan_scratch(tm, LRU_W), pltpu.VMEM((SUBLANES, LRU_W), F32)]
    (hf, hb), couts = _call(
        body, name=name, grid=(nt,),
        in_specs=[fw, bw] + [mat, mat, vec, vec, vec] * 2 + [vec, vec],
        out_specs=[pl.BlockSpec((tm, LRU_W), lambda i: (i, 0)), pl.BlockSpec((tm, LRU_W), lambda i: (nt - 1 - i, 0))],
        out_shape=[_sds((t, LRU_W))] * 2, scratch_shapes=tile_s + tile_s, sem=("arbitrary",),
        args=(xc, xc, *par_f, *par_b, h0_f, h0_b), comms=comms)
    return ((hf, hb), couts) if comms else (hf, hb)


def _lru_bwd(xc, par_f, par_b, h_f, h_b, h0_f, h0_b, dh_f, dh_b, name, comms=()):
    t = xc.shape[0]
    tm = _tile(t, True)
    nt = t // tm
    n8 = tm // SUBLANES
    last8 = t // SUBLANES - 1
    tile_f = lambda w: pl.BlockSpec((tm, w), lambda i: (nt - 1 - i, 0))
    tile_b = lambda w: pl.BlockSpec((tm, w), lambda i: (i, 0))
    halo_f = pl.BlockSpec((SUBLANES, LRU_W), lambda i: (jnp.maximum((nt - 1 - i) * n8 - 1, 0), 0))
    halo_b = pl.BlockSpec((SUBLANES, LRU_W), lambda i: (jnp.minimum((i + 1) * n8, last8), 0))

    def one(refs_in, refs_out, refs_scr, reverse):
        x_ref, wa_ref, wx_ref, ba_ref, bx_ref, lam_ref, h_ref, halo_ref, h0_ref, dh_ref = refs_in
        dx_ref, dpre_ref, dba_ref, dbx_ref, dlam_ref, dh0_ref = refs_out
        a_s, b_s, l_s, c_s, e_s = refs_scr
        i = pl.program_id(0)

        @pl.when(i == 0)
        def _():
            c_s[...] = jnp.zeros_like(c_s)
            e_s[...] = jnp.zeros_like(e_s)
            dba_ref[...] = jnp.zeros_like(dba_ref)
            dbx_ref[...] = jnp.zeros_like(dbx_ref)
            dlam_ref[...] = jnp.zeros_like(dlam_ref)
        xv = x_ref[...]
        lam = lam_ref[...]
        r, ig, sp, a, q = _lru_gates(xv, wa_ref, wx_ref, ba_ref[...], bx_ref[...], lam)
        rs = lax.rsqrt(q)
        hv = h_ref[...]
        rowi = lax.broadcasted_iota(jnp.int32, (tm, LRU_W), 0)
        edge_a = jnp.broadcast_to(e_s[0:1, :], (tm, LRU_W))
        h0b = jnp.broadcast_to(h0_ref[...], (tm, LRU_W))
        if reverse:
            a_sh = jnp.where(rowi == 0, edge_a, pltpu.roll(a, 1, 0))
            hin_edge = jnp.where(i == nt - 1, h0b, jnp.broadcast_to(halo_ref[0:1, :], (tm, LRU_W)))
            h_in = jnp.where(rowi == tm - 1, hin_edge, pltpu.roll(hv, tm - 1, 0))
        else:
            a_sh = jnp.where(rowi == tm - 1, edge_a, pltpu.roll(a, tm - 1, 0))
            hin_edge = jnp.where(i == nt - 1, h0b, jnp.broadcast_to(halo_ref[SUBLANES - 1:SUBLANES, :], (tm, LRU_W)))
            h_in = jnp.where(rowi == 0, hin_edge, pltpu.roll(hv, 1, 0))
        _to_lane_blocks(a_s, a_sh)
        _to_lane_blocks(b_s, dh_ref[...])
        _group_scan(a_s, b_s, not reverse)
        e_s[...] = jnp.broadcast_to(a[tm - 1:tm, :] if reverse else a[0:1, :], e_s.shape)
        return lam, r, ig, sp, a, q * rs, rs, h_in

    def post(vals, refs_in, refs_out, refs_scr, reverse):
        lam, r, ig, sp, a, mult, rs, h_in = vals
        xv = refs_in[0][...]
        wa_ref, wx_ref = refs_in[1:3]
        dx_ref, dpre_ref, dba_ref, dbx_ref, dlam_ref, dh0_ref = refs_out
        l_s = refs_scr[2]
        i = pl.program_id(0)
        lmb = l_s[...]
        da = lmb * h_in
        ixc = ig * xv
        dmult = lmb * ixc
        dixc = lmb * mult
        dla = da * a - dmult * (a * a) * rs
        dpr = dla * (-LRU_C * sp) * r * (1.0 - r)
        dpi = dixc * xv * ig * (1.0 - ig)
        dprb = dpr.astype(BF16)
        dpib = dpi.astype(BF16)
        dpre_ref[:, 0:LRU_W] = dprb
        dpre_ref[:, LRU_W:2 * LRU_W] = dpib
        dx_ref[...] = dixc * ig + _dot_nt(dprb, wa_ref[...]) + _dot_nt(dpib, wx_ref[...])
        dba_ref[...] += _sum0(dpr)
        dbx_ref[...] += _sum0(dpi)
        dlam_ref[...] += _sum0(dla * (-LRU_C * r)) * (-_sigmoid(-lam))

        @pl.when(i == nt - 1)
        def _():
            al0 = a * lmb
            dh0_ref[...] = al0[tm - 1:tm, :] if reverse else al0[0:1, :]

    def body(*refs):
        jobs = ((refs[0:10], refs[20:26], refs[32:37], False), (refs[10:20], refs[26:32], refs[37:42], True))
        vals = [one(*job) for job in jobs]
        carries = _carry_scans([(scr[0], scr[1], scr[2], scr[3][...], not rev) for _, _, scr, rev in jobs])
        for (_, _, scr, _), carry in zip(jobs, carries):
            scr[3][...] = carry
        for v, job in zip(vals, jobs):
            post(v, *job)

    vec = _full((1, LRU_W))
    mat = _full((LRU_W, LRU_W))

    def in_specs(tile, halo):
        return [tile(LRU_W), mat, mat, vec, vec, vec, tile(LRU_W), halo, vec, tile(LRU_W)]

    def out_specs(tile):
        return [tile(LRU_W), tile(2 * LRU_W), vec, vec, vec, vec]

    out_one = [_sds((t, LRU_W)), _sds((t, 2 * LRU_W), BF16)] + [_sds((1, LRU_W))] * 4
    scr_one = ([_scan_scratch(tm, LRU_W)] * 2 + [pltpu.VMEM((tm, LRU_W), F32)]
               + [pltpu.VMEM((SUBLANES, LRU_W), F32)] * 2)
    outs, couts = _call(
        body, name=name, grid=(nt,), in_specs=in_specs(tile_f, halo_f) + in_specs(tile_b, halo_b),
        out_specs=out_specs(tile_f) + out_specs(tile_b), out_shape=out_one + out_one,
        scratch_shapes=scr_one + scr_one, sem=("arbitrary",),
        args=(xc, *par_f, h_f, h_f, h0_f, dh_f, xc, *par_b, h_b, h_b, h0_b, dh_b), comms=comms)
    return (tuple(outs[0:6]), tuple(outs[6:12])), couts


def _decay_tables(lg, reverse):
    ci = lax.broadcasted_iota(jnp.int32, (CHUNK, CHUNK), 0).astype(F32)
    mi = lax.broadcasted_iota(jnp.int32, (CHUNK, CHUNK), 1).astype(F32)
    rel = (mi - ci) if reverse else (ci - mi)
    relc = jnp.maximum(rel, 0.0)
    lg_c = jnp.concatenate([lg] * (CHUNK // LANES), axis=1)
    dm = jnp.where(rel >= 0, jnp.exp(lg_c * relc), 0.0)
    cd = lax.broadcasted_iota(jnp.int32, (CHUNK, DH), 0).astype(F32)
    pq, ps = (CHUNK - cd, cd) if reverse else (cd + 1.0, CHUNK - 1.0 - cd)
    return relc, dm, jnp.exp(lg * pq), jnp.exp(lg * ps), jnp.exp(lg * float(CHUNK)), pq, ps


def _ret_fwd(proj, lgv, s0f, s0b, comms=()):
    t = proj.shape[0]
    n = t // CHUNK

    def one(q, k, v, lg, s_s, hh, o_ref, sp_ref, reverse):
        _, dm, wq, ws, g, _, _ = _decay_tables(lg, reverse)
        vb = v.astype(BF16)
        p = _dot_nt(q.astype(BF16), k.astype(BF16)) * dm
        s = s_s[hh]
        sp_ref[hh, 0] = s
        o_ref[:, DH * hh:DH * (hh + 1)] = _dot(p.astype(BF16), vb) + _dot((q * wq).astype(BF16), s.astype(BF16))
        s_s[hh] = g * s + _dot_tn((k * ws).astype(BF16), vb)

    def body(qf, kf, vf, qb, kb, vb, lg_ref, s0f_ref, s0b_ref, of_ref, ob_ref, spf_ref, spb_ref, sf_s, sb_s):
        @pl.when(pl.program_id(0) == 0)
        def _():
            sf_s[...] = s0f_ref[...]
            sb_s[...] = s0b_ref[...]
        for hh in range(HEADS):
            sl = slice(DH * hh, DH * (hh + 1))
            one(qf[:, sl].astype(F32), kf[:, sl].astype(F32), vf[:, sl], lg_ref[hh, 0:1, :], sf_s, hh, of_ref, spf_ref,
                False)
            one(qb[:, sl].astype(F32), kb[:, sl].astype(F32), vb[:, sl], lg_ref[hh, 1:2, :], sb_s, hh, ob_ref, spb_ref,
                True)

    blk = (CHUNK, RET_W)
    fw = [pl.BlockSpec(blk, lambda i, o=o: (i, o)) for o in range(3)]
    bw = [pl.BlockSpec(blk, lambda i, o=o: (n - 1 - i, o)) for o in range(3)]
    st = _full((HEADS, DH, DH))
    return _call(body, name="ret_fwd", grid=(n,),
                 in_specs=fw + bw + [_full((HEADS, 2, LANES)), st, st],
                 out_specs=[pl.BlockSpec(blk, lambda i: (i, 0)), pl.BlockSpec(blk, lambda i: (n - 1 - i, 0)),
                            pl.BlockSpec((HEADS, 1, DH, DH), lambda i: (0, i, 0, 0)),
                            pl.BlockSpec((HEADS, 1, DH, DH), lambda i: (0, n - 1 - i, 0, 0))],
                 out_shape=[_sds((t, RET_W)), _sds((t, RET_W)), _sds((HEADS, n, DH, DH)), _sds((HEADS, n, DH, DH))],
                 scratch_shapes=[pltpu.VMEM((HEADS, DH, DH), F32), pltpu.VMEM((HEADS, DH, DH), F32)],
                 sem=("arbitrary",), args=(proj, proj, proj, proj, proj, proj, lgv, s0f, s0b), comms=comms)


def _ret_bwd(proj, lgv, sgv, spf, spb, do, comms=()):
    t = proj.shape[0]
    n = t // CHUNK

    def one(q_ref, k_ref, v_ref, lg_ref, s_ref, do_ref, dq_ref, dk_ref, dv_ref, ds_s, acc_s, reverse):
        d = 1 if reverse else 0
        for hh in range(HEADS):
            sl = slice(DH * hh, DH * (hh + 1))
            relc, dm, wq, ws, g, pq, ps = _decay_tables(lg_ref[hh, d:d + 1, :], reverse)
            qb, kb, vb = q_ref[:, sl], k_ref[:, sl], v_ref[:, sl]
            q, k = qb.astype(F32), kb.astype(F32)
            p = _dot_nt(qb, kb) * dm
            s = s_ref[hh, 0]
            dob = do_ref[:, sl].astype(BF16)
            dsn = ds_s[hh]
            dsb = dsn.astype(BF16)
            dv_ref[:, sl] = (_dot_tn(p.astype(BF16), dob) + _dot((k * ws).astype(BF16), dsb)).astype(BF16)
            dp = _dot_nt(dob, vb)
            dab = (dp * dm).astype(BF16)
            xq = _dot_nt(dob, s.astype(BF16))
            yk = _dot_nt(vb, dsb)
            dq_ref[:, sl] = (_dot(dab, kb) + xq * wq).astype(BF16)
            dk_ref[:, sl] = (_dot_tn(dab, qb) + yk * ws).astype(BF16)
            ds_s[hh] = g * dsn + _dot_tn((q * wq).astype(BF16), dob)
            s_mask = _sum0(dp * p * relc)
            part = (sum(s_mask[:, LANES * u:LANES * (u + 1)] for u in range(CHUNK // LANES))
                    + _sum0(xq * q * wq * pq) + _sum0(yk * k * ws * ps) + _sum0(dsn * s) * g * float(CHUNK))
            acc_s[hh] += jnp.broadcast_to(part, (SUBLANES, LANES))

    def body(qf, kf, vf, qb, kb, vb, lg_ref, sg_ref, sf_ref, sb_ref, dof_ref, dob_ref,
             dqf, dkf, dvf, dqb, dkb, dvb, ds0f_ref, ds0b_ref, drdf_ref, drdb_ref, dsf_s, dsb_s, accf_s, accb_s):
        i = pl.program_id(0)

        @pl.when(i == 0)
        def _():
            for r in (dsf_s, dsb_s, accf_s, accb_s):
                r[...] = jnp.zeros_like(r)
        one(qf, kf, vf, lg_ref, sf_ref, dof_ref, dqf, dkf, dvf, dsf_s, accf_s, False)
        one(qb, kb, vb, lg_ref, sb_ref, dob_ref, dqb, dkb, dvb, dsb_s, accb_s, True)

        @pl.when(i == n - 1)
        def _():
            ds0f_ref[...] = dsf_s[...]
            ds0b_ref[...] = dsb_s[...]
            for d, (acc_s, drd_ref) in enumerate(((accf_s, drdf_ref), (accb_s, drdb_ref))):
                for hh in range(HEADS):
                    tot = jnp.sum(acc_s[hh, 0:1, :], axis=1, keepdims=True)
                    drd_ref[hh] = jnp.broadcast_to(tot, (SUBLANES, LANES)) * sg_ref[hh, d:d + 1, :]

    blk = (CHUNK, RET_W)
    fw = lambda o: pl.BlockSpec(blk, lambda i, o=o: (n - 1 - i, o))
    bw = lambda o: pl.BlockSpec(blk, lambda i, o=o: (i, o))
    lane = _full((HEADS, 2, LANES))
    st = _full((HEADS, DH, DH))
    rd = _full((HEADS, SUBLANES, LANES))
    outs, couts = _call(
        body, name="ret_bwd", grid=(n,),
        in_specs=[fw(0), fw(1), fw(2), bw(0), bw(1), bw(2), lane, lane,
                  pl.BlockSpec((HEADS, 1, DH, DH), lambda i: (0, n - 1 - i, 0, 0)),
                  pl.BlockSpec((HEADS, 1, DH, DH), lambda i: (0, i, 0, 0)), fw(0), bw(0)],
        out_specs=[fw(0), fw(0), fw(0), bw(0), bw(0), bw(0), st, st, rd, rd],
        out_shape=[_sds((t, RET_W), BF16)] * 6 + [_sds((HEADS, DH, DH))] * 2 + [_sds((HEADS, SUBLANES, LANES))] * 2,
        scratch_shapes=[pltpu.VMEM((HEADS, DH, DH), F32)] * 2 + [pltpu.VMEM((HEADS, SUBLANES, LANES), F32)] * 2,
        sem=("arbitrary",), args=(proj, proj, proj, proj, proj, proj, lgv, sgv, spf, spb, do, do), comms=comms)
    dqf, dkf, dvf, dqb, dkb, dvb, ds0f, ds0b, drdf, drdb = outs
    return ((dqf, dkf, dvf, ds0f, drdf), (dqb, dkb, dvb, ds0b, drdb)), couts


def _ctx_weights(lg, l_len, reverse):
    pos = lax.broadcasted_iota(jnp.int32, (l_len, DH), 0).astype(F32)
    steps = pos if reverse else (l_len - 1.0 - pos)
    return jnp.exp(lg * steps), steps


def _ctx_state_fwd(projc, lgv):
    l_len = projc.shape[0]

    def body(k_ref, v_ref, lg_ref, sf_ref, sb_ref):
        k = k_ref[...]
        vb = v_ref[...].astype(BF16)
        for d, o_ref in ((0, sf_ref), (1, sb_ref)):
            w, _ = _ctx_weights(lg_ref[0, d:d + 1, :], l_len, d == 1)
            o_ref[0] = _dot_tn((k * w).astype(BF16), vb)

    st = pl.BlockSpec((1, DH, DH), lambda h: (h, 0, 0))
    return _pc(body, name="ctx_state_fwd", grid=(HEADS,),
               in_specs=[pl.BlockSpec((l_len, DH), lambda h: (0, HEADS + h)),
                         pl.BlockSpec((l_len, DH), lambda h: (0, 2 * HEADS + h)),
                         pl.BlockSpec((1, 2, LANES), lambda h: (h, 0, 0))],
               out_specs=[st, st], out_shape=[_sds((HEADS, DH, DH))] * 2,
               compiler_params=_params("arbitrary"))(projc, projc, lgv)


def _ctx_state_bwd(projc, lgv, sgv, dsf, dsb):
    l_len = projc.shape[0]

    def body(k_ref, v_ref, lg_ref, sg_ref, dsf_ref, dsb_ref, dk_ref, dv_ref, drd_ref):
        k = k_ref[...]
        vb = v_ref[...].astype(BF16)
        dk = jnp.zeros((l_len, DH), F32)
        dv = jnp.zeros((l_len, DH), F32)
        rows = []
        for d, ds_ref in ((0, dsf_ref), (1, dsb_ref)):
            w, steps = _ctx_weights(lg_ref[0, d:d + 1, :], l_len, d == 1)
            dsb16 = ds_ref[0].astype(BF16)
            dkw = _dot_nt(vb, dsb16)
            dk = dk + dkw * w
            dv = dv + _dot((k * w).astype(BF16), dsb16)
            tot = jnp.sum(_sum0(dkw * k * w * steps), axis=1, keepdims=True)
            rows.append(jnp.broadcast_to(tot, (1, LANES)) * sg_ref[0, d:d + 1, :])
        dk_ref[...] = dk.astype(BF16)
        dv_ref[...] = dv.astype(BF16)
        rid = lax.broadcasted_iota(jnp.int32, (SUBLANES, LANES), 0)
        drd_ref[0] = jnp.where(rid == 0, rows[0], jnp.where(rid == 1, rows[1], 0.0))

    st = pl.BlockSpec((1, DH, DH), lambda h: (h, 0, 0))
    lane = pl.BlockSpec((1, 2, LANES), lambda h: (h, 0, 0))
    hc = pl.BlockSpec((l_len, DH), lambda h: (0, h))
    return _pc(body, name="ctx_state_bwd", grid=(HEADS,),
               in_specs=[pl.BlockSpec((l_len, DH), lambda h: (0, HEADS + h)),
                         pl.BlockSpec((l_len, DH), lambda h: (0, 2 * HEADS + h)), lane, lane, st, st],
               out_specs=[hc, hc, pl.BlockSpec((1, SUBLANES, LANES), lambda h: (h, 0, 0))],
               out_shape=[_sds((l_len, RET_W), BF16), _sds((l_len, RET_W), BF16), _sds((HEADS, SUBLANES, LANES))],
               compiler_params=_params("arbitrary"))(projc, projc, lgv, sgv, dsf, dsb)


G_BLOCK = (3 * RET_W) // RET_W
GATE_BLOCK = (4 * RET_W + LRU_W) // LRU_W


def _head_norm(y):
    yc = y - jnp.mean(y, axis=-1, keepdims=True)
    rs = lax.rsqrt(jnp.mean(yc * yc, axis=-1, keepdims=True) + EPS)
    return yc * rs, rs


def _gelu_parts(z):
    th = jnp.tanh(GELU_K * (z + GELU_C * z * z * z))
    return 0.5 * z * (1.0 + th), th


def _mix_fwd(o_f, o_b, proj, hf, hb, w_out, x, g1, comms):
    t = x.shape[0]
    tm = _tile(t, True)

    def body(of_ref, ob_ref, g_ref, gt_ref, hf_ref, hb_ref, w_ref, x_ref, g1_ref, x1_ref, cat_ref):
        o = of_ref[...] + ob_ref[...]
        g = g_ref[...].astype(F32)
        for hh in range(HEADS):
            sl = slice(DH * hh, DH * (hh + 1))
            nrm, _ = _head_norm(o[:, sl])
            gh = g[:, sl]
            cat_ref[:, sl] = (gh * _sigmoid(gh) * nrm).astype(BF16)
        gel, _ = _gelu_parts(gt_ref[...].astype(F32))
        cat_ref[:, RET_W:] = ((hf_ref[...] + hb_ref[...]) * gel).astype(BF16)
        x1_ref[...] = x_ref[...] + g1_ref[...] * _dot(cat_ref[...], w_ref[...])

    half = pl.BlockSpec((tm, RET_W), lambda i: (i, 0))
    big = pl.BlockSpec((tm, D_MODEL), lambda i: (i, 0))
    return _call(body, name="mix_fwd", grid=(t // tm,),
                 in_specs=[half, half, pl.BlockSpec((tm, RET_W), lambda i: (i, G_BLOCK)),
                           pl.BlockSpec((tm, LRU_W), lambda i: (i, GATE_BLOCK)), half, half,
                           _full((D_MODEL, D_MODEL)), big, _full((1, D_MODEL))],
                 out_specs=[big, big], out_shape=[_sds((t, D_MODEL)), _sds((t, D_MODEL), BF16)],
                 scratch_shapes=[], sem=("arbitrary",), args=(o_f, o_b, proj, proj, hf, hb, w_out, x, g1),
                 comms=comms)


def _mix_bwd(o_f, o_b, proj, hf, hb, w_out, cat, dx1, g1, comms=()):
    t = dx1.shape[0]
    tm = _tile(t, True)

    def body(of_ref, ob_ref, g_ref, gt_ref, hf_ref, hb_ref, w_ref, cat_ref, dx1_ref, g1_ref,
             do_ref, dhs_ref, dg_ref, dgt_ref, dyb_ref, dg1_ref):
        dx1v = dx1_ref[...]
        y = _dot(cat_ref[...], w_ref[...])

        @pl.when(pl.program_id(0) == 0)
        def _():
            dg1_ref[...] = jnp.zeros_like(dg1_ref)
        dg1_ref[...] += _sum0(dx1v * y)
        dyb = (g1_ref[...] * dx1v).astype(BF16)
        dyb_ref[...] = dyb
        dcat = _dot_nt(dyb, w_ref[...])
        o = of_ref[...] + ob_ref[...]
        g = g_ref[...].astype(F32)
        for hh in range(HEADS):
            sl = slice(DH * hh, DH * (hh + 1))
            nrm, rs = _head_norm(o[:, sl])
            gh = g[:, sl]
            sg = _sigmoid(gh)
            dret = dcat[:, sl]
            dg_ref[:, sl] = (dret * nrm * (sg * (1.0 + gh * (1.0 - sg)))).astype(BF16)
            dn = dret * (gh * sg)
            dyc = rs * (dn - nrm * jnp.mean(dn * nrm, axis=-1, keepdims=True))
            do_ref[:, sl] = (dyc - jnp.mean(dyc, axis=-1, keepdims=True)).astype(BF16)
        z = gt_ref[...].astype(F32)
        gel, th = _gelu_parts(z)
        dlru = dcat[:, RET_W:]
        dhs_ref[...] = dlru * gel
        dgel = 0.5 * (1.0 + th) + 0.5 * z * (1.0 - th * th) * GELU_K * (1.0 + 3.0 * GELU_C * z * z)
        dgt_ref[...] = (dlru * (hf_ref[...] + hb_ref[...]) * dgel).astype(BF16)

    half = pl.BlockSpec((tm, RET_W), lambda i: (i, 0))
    big = pl.BlockSpec((tm, D_MODEL), lambda i: (i, 0))
    return _call(body, name="mix_bwd", grid=(t // tm,),
                 in_specs=[half, half, pl.BlockSpec((tm, RET_W), lambda i: (i, G_BLOCK)),
                           pl.BlockSpec((tm, LRU_W), lambda i: (i, GATE_BLOCK)), half, half,
                           _full((D_MODEL, D_MODEL)), big, big, _full((1, D_MODEL))],
                 out_specs=[half, half, half, half, big, _full((1, D_MODEL))],
                 out_shape=[_sds((t, RET_W), BF16), _sds((t, RET_W)), _sds((t, RET_W), BF16), _sds((t, RET_W), BF16),
                            _sds((t, D_MODEL), BF16), _sds((1, D_MODEL))],
                 scratch_shapes=[], sem=("arbitrary",), args=(o_f, o_b, proj, proj, hf, hb, w_out, cat, dx1, g1),
                 comms=comms)


def _mlp(x1, n2g, sh2, sc2, g2, fg, w1_parts, w2_parts, tgt):
    t = x1.shape[0]
    tm = _tile(t)
    hb_ = MLP_H // N_CHIP
    q_rows = hb_ // 4
    n_cp = 4 * N_DEV

    def body(x1_ref, n2g_ref, sh2_ref, sc2_ref, g2_ref, fg_ref, w1a, w1b, w2a, w2b, tgt_ref,
             dx1_ref, h2b_ref, ab_ref, dub_ref, dmb_ref, dsc_ref, dsh_ref, dg2_ref, dn2_ref, dfg_ref, loss_ref,
             w1_s, w2_s, r_s, sems):
        @pl.when(pl.program_id(0) == 0)
        def _():
            cps = []
            for p, parts in enumerate(((w1a, w2a), (w1b, w2b))):
                for d in range(N_DEV):
                    rows = pl.ds(2 * q_rows * (d % 2) + q_rows * p, q_rows)
                    for src, dst in zip(parts, (w1_s, w2_s)):
                        cps.append(pltpu.make_async_copy(src.at[d], dst.at[d // 2, rows], sems.at[len(cps)]))
            for cp in cps:
                cp.start()
            for r in (dsc_ref, dsh_ref, dg2_ref, dn2_ref, dfg_ref, loss_ref):
                r[...] = jnp.zeros_like(r)
            for cp in cps:
                cp.wait()
        x1v = x1_ref[...]
        n2g, sc2, g2, fg = n2g_ref[...], sc2_ref[...], g2_ref[...], fg_ref[...]
        xh, _ = _rms(x1v)
        h2b = (xh * n2g * (1.0 + sc2) + sh2_ref[...]).astype(BF16)
        h2b_ref[...] = h2b
        m = jnp.zeros((tm, D_MODEL), F32)
        for j in range(N_CHIP):
            sl = slice(hb_ * j, hb_ * (j + 1))
            r = jnp.maximum(_dot(h2b, w1_s[j]), 0.0)
            r_s[:, sl] = r
            ab = (r * r).astype(BF16)
            ab_ref[:, sl] = ab
            m = m + _dot(ab, w2_s[j])
        x2 = x1v + g2 * m
        x2h, r2 = _rms(x2)
        err = x2h * fg - tgt_ref[...]
        loss_ref[...] += _sum0(err * err)
        dout = err * (1.0 / D_MODEL)
        dfg_ref[...] += _sum0(dout * x2h)
        dxh = dout * fg
        dx2 = r2 * (dxh - x2h * jnp.mean(dxh * x2h, axis=-1, keepdims=True))
        dg2_ref[...] += _sum0(dx2 * m)
        dmb = (g2 * dx2).astype(BF16)
        dmb_ref[...] = dmb
        dh2 = jnp.zeros((tm, D_MODEL), F32)
        for j in range(N_CHIP):
            sl = slice(hb_ * j, hb_ * (j + 1))
            dub = (_dot_nt(dmb, w2_s[j]) * (2.0 * r_s[:, sl])).astype(BF16)
            dub_ref[:, sl] = dub
            dh2 = dh2 + _dot_nt(dub, w1_s[j])
        dx, dn2_t, dsh_t, dsc_t = _norm_mod_bwd(x1v, n2g, sc2, dh2)
        dx1_ref[...] = dx2 + dx
        dn2_ref[...] += dn2_t
        dsh_ref[...] += dsh_t
        dsc_ref[...] += dsc_t

        @pl.when(pl.program_id(0) == t // tm - 1)
        def _():
            tot = jnp.sum(loss_ref[...], axis=1, keepdims=True) * (0.5 / D_MODEL)
            loss_ref[...] = jnp.broadcast_to(tot, loss_ref.shape)

    row = _full((1, D_MODEL))
    big = pl.BlockSpec((tm, D_MODEL), lambda i: (i, 0))
    wide = pl.BlockSpec((tm, MLP_H), lambda i: (i, 0))
    return _pc(body, name="mlp", grid=(t // tm,),
               in_specs=[big, row, row, row, row, row, ANY, ANY, ANY, ANY, big],
               out_specs=[big, big, wide, wide, big, row, row, row, row, row, row],
               out_shape=[_sds((t, D_MODEL)), _sds((t, D_MODEL), BF16), _sds((t, MLP_H), BF16), _sds((t, MLP_H), BF16),
                          _sds((t, D_MODEL), BF16)] + [_sds((1, D_MODEL))] * 6,
               scratch_shapes=[pltpu.VMEM((N_CHIP, D_MODEL, hb_), BF16), pltpu.VMEM((N_CHIP, hb_, D_MODEL), BF16),
                               pltpu.VMEM((tm, MLP_H), F32), pltpu.SemaphoreType.DMA((n_cp,))],
               compiler_params=_params("arbitrary"))(x1, n2g, sh2, sc2, g2, fg, *w1_parts, *w2_parts, tgt)


def _tn(a, b, nj, a_blocked, b_blocked, name, extra=None, comms=()):
    t = a.shape[0]
    m = a.shape[1] // (nj if a_blocked else 1)
    n = b.shape[1] // (nj if b_blocked else 1)
    bk = next((b for b in (2048, 1024, 512) if t % b == 0), t)
    nk = t // bk
    a_col = (lambda j: j) if a_blocked else (lambda j: 0)
    b_col = (lambda j: j) if b_blocked else (lambda j: 0)
    in_specs = [pl.BlockSpec((bk, m), lambda j, k: (k, a_col(j))), pl.BlockSpec((bk, n), lambda j, k: (k, b_col(j)))]
    args = [a, b]
    if extra is not None:
        a2, b2 = extra
        t2 = a2.shape[0]
        in_specs += [pl.BlockSpec((t2, m), lambda j, k: (0, a_col(j))),
                     pl.BlockSpec((t2, n), lambda j, k: (0, b_col(j)))]
        args += [a2, b2]

    def body(*refs):
        a_ref, b_ref = refs[0], refs[1]
        o_ref, acc = refs[-2], refs[-1]
        k = pl.program_id(1)

        @pl.when(k == 0)
        def _():
            acc[...] = jnp.zeros_like(acc)
        acc[...] += _dot_tn(a_ref[...].astype(BF16), b_ref[...].astype(BF16))

        @pl.when(k == nk - 1)
        def _():
            if extra is not None:
                acc[...] += _dot_tn(refs[2][...].astype(BF16), refs[3][...].astype(BF16))
            o_ref[0] = acc[...]

    (out,), couts = _call(body, name=name, grid=(nj, nk), in_specs=in_specs,
                          out_specs=[pl.BlockSpec((1, m, n), lambda j, k: (j, 0, 0))], out_shape=[_sds((nj, m, n))],
                          scratch_shapes=[pltpu.VMEM((m, n), F32)], sem=("arbitrary", "arbitrary"), args=args,
                          comms=comms)
    return (out, couts) if comms else out


ROW_LOSS = 0
ROW_DMOD = 1
ROW_DMODC = 7
ROW_N1, ROW_N2, ROW_FG, ROW_CB = 9, 10, 11, 12
ROW_BA, ROW_BX, ROW_LAM = 13, 15, 17
ROW_CW = 20
ROW_RD = 24
SLAB_ROWS = 32
SEG = D_MODEL // 2


def _pack_small(rows, drd, cw2, cb2, lru2, gates):
    n_rows, n_lru = len(rows), len(lru2)

    def body(*refs):
        r = refs[:n_rows]
        drd_f, drd_b, drd_c, cw_a, cw_b, cb_a, cb_b = refs[n_rows:n_rows + 7]
        lru = refs[n_rows + 7:n_rows + 7 + n_lru]
        gf_ref, gb_ref, slab, ga, gx = refs[n_rows + 7 + n_lru:]
        slab[...] = jnp.zeros_like(slab)
        slab[ROW_LOSS:ROW_LOSS + 1, :] = r[0][...]
        for k in range(N_MOD):
            slab[ROW_DMOD + k:ROW_DMOD + k + 1, :] = r[1 + k][...]
        slab[ROW_DMODC:ROW_DMODC + 1, :] = r[7][...]
        slab[ROW_DMODC + 1:ROW_DMODC + 2, :] = r[8][...]
        slab[ROW_N1:ROW_N1 + 1, :] = r[9][...] + r[10][...]
        slab[ROW_N2:ROW_N2 + 1, :] = r[11][...]
        slab[ROW_FG:ROW_FG + 1, :] = r[12][...]
        slab[ROW_CB:ROW_CB + 1, 0:LRU_W] = cb_a[...] + cb_b[...]
        for k, row in enumerate((ROW_BA, ROW_BA + 1, ROW_BX, ROW_BX + 1, ROW_LAM, ROW_LAM + 1)):
            slab[row:row + 1, 0:LRU_W] = lru[2 * k][...] + lru[2 * k + 1][...]
        slab[ROW_CW:ROW_CW + 4, 0:LRU_W] = cw_a[...] + cw_b[...]
        for h in range(HEADS):
            slab[ROW_RD + h:ROW_RD + h + 1, 0:LANES] = drd_f[h, 0:1, :] + drd_c[h, 0:1, :]
            slab[ROW_RD + HEADS + h:ROW_RD + HEADS + h + 1, 0:LANES] = drd_b[h, 0:1, :] + drd_c[h, 1:2, :]
        for d, g_ref in enumerate((gf_ref, gb_ref)):
            for n in range(LRU_BLOCKS):
                blk = slice(LRU_BD * n, LRU_BD * (n + 1))
                ga[blk, LRU_BD * d:LRU_BD * (d + 1)] = g_ref[0, blk, blk].astype(BF16)
                gx[blk, LRU_BD * d:LRU_BD * (d + 1)] = g_ref[1, blk, blk].astype(BF16)

    args = list(rows) + list(drd) + list(cw2) + list(cb2) + list(lru2) + list(gates)
    gate_shape = (LRU_W, 2 * LRU_BD)
    return _pc(body, name="pack_small", in_specs=[_full(a.shape) for a in args],
               out_specs=[_full((SLAB_ROWS, D_MODEL)), _full(gate_shape), _full(gate_shape)],
               out_shape=[_sds((SLAB_ROWS, D_MODEL)), _sds(gate_shape, BF16), _sds(gate_shape, BF16)],
               compiler_params=_params())(*args)


def _adam_math(w, g, m, v):
    mn = ADAM_B1 * m + (1.0 - ADAM_B1) * g
    vn = ADAM_B2 * v + (1.0 - ADAM_B2) * (g * g)
    mh = mn / (1.0 - ADAM_B1 ** ADAM_STEP)
    vh = vn / (1.0 - ADAM_B2 ** ADAM_STEP)
    return -ADAM_LR * (mh / (jnp.sqrt(vh) + ADAM_EPS) + ADAM_WD * w), mn, vn


SMALL_PARAMS = ("b_ada", "norm1_g", "norm2_g", "final_g", "ret_decay", "conv_w", "conv_b", "lru_wa", "lru_ba", "lru_wx",
                "lru_bx", "lru_lambda")


def _finalize_small(chip_idx, slab_all, ga_all, gx_all, wmv):
    n_p = len(SMALL_PARAMS)
    flat = [a for nm in SMALL_PARAMS for a in wmv[nm]]
    ada_n = N_MOD * D_MODEL // N_CHIP

    def body(c_ref, slab_ref, ga_ref, gx_ref, *refs):
        prm = {nm: refs[3 * k:3 * k + 3] for k, nm in enumerate(SMALL_PARAMS)}
        outs = {nm: refs[3 * n_p + 4 * k:3 * n_p + 4 * k + 4] for k, nm in enumerate(SMALL_PARAMS)}
        b128_ref, dmc_ref, loss_ref = refs[3 * n_p + 4 * n_p:]
        chip = c_ref[0]

        def pick(fn):
            acc = fn(0)
            for j in range(1, N_CHIP):
                acc = jnp.where(chip == j, fn(j), acc)
            return acc

        tot = slab_ref[0]
        for d in range(1, N_DEV):
            tot = tot + slab_ref[d]

        def update(nm, g, sl=None, rows=None):
            w_ref, m_ref, v_ref = prm[nm]
            g_ref, d_ref, mo_ref, vo_ref = outs[nm]
            ix = (slice(None) if rows is None else rows, slice(None) if sl is None else sl)
            dl, mn, vn = _adam_math(w_ref[ix], g, m_ref[ix], v_ref[ix])
            g_ref[ix] = g
            d_ref[ix] = dl
            mo_ref[ix] = mn
            vo_ref[ix] = vn

        loss_ref[...] = jnp.broadcast_to(tot[ROW_LOSS:ROW_LOSS + 1, 0:LANES], (SUBLANES, LANES))
        for k in range(N_MOD):
            g = tot[ROW_DMOD + k:ROW_DMOD + k + 1, :]
            if k < 2:
                g = g + tot[ROW_DMODC + k:ROW_DMODC + k + 1, :]
            update("b_ada", g, slice(D_MODEL * k, D_MODEL * (k + 1)))
        update("norm1_g", tot[ROW_N1:ROW_N1 + 1, :])
        update("norm2_g", tot[ROW_N2:ROW_N2 + 1, :])
        update("final_g", tot[ROW_FG:ROW_FG + 1, :])
        update("ret_decay", tot[ROW_RD:ROW_RD + SUBLANES, 0:LANES])
        update("conv_b", tot[ROW_CB:ROW_CB + 1, 0:LRU_W])
        update("conv_w", pick(lambda j: tot[ROW_CW:ROW_CW + 4, LANES * j:LANES * (j + 1)]))
        for nm, row in (("lru_ba", ROW_BA), ("lru_bx", ROW_BX), ("lru_lambda", ROW_LAM)):
            update(nm, pick(lambda j, row=row: tot[row:row + 2, LANES * j:LANES * (j + 1)]))
        for nm, g_all in (("lru_wa", ga_ref), ("lru_wx", gx_ref)):
            for dr in range(2):
                lanes = slice(LRU_BD * dr, LRU_BD * (dr + 1))
                g = g_all[0, :, lanes].astype(F32)
                for d in range(1, N_DEV):
                    g = g + g_all[d, :, lanes].astype(F32)
                update(nm, g, rows=slice(LRU_W * dr, LRU_W * (dr + 1)))

        def seg(rows6, s):
            return rows6[s // 2][:, SEG * (s % 2):SEG * (s % 2 + 1)]

        b128_ref[...] = jnp.zeros_like(b128_ref)
        dmc_ref[...] = jnp.zeros_like(dmc_ref)
        zero = jnp.zeros((1, D_MODEL), F32)
        ctx6 = [tot[ROW_DMODC:ROW_DMODC + 1, :], tot[ROW_DMODC + 1:ROW_DMODC + 2, :]] + [zero] * (N_MOD - 2)
        for q in range(ada_n // SEG):
            cols = slice(SEG * q, SEG * (q + 1))
            for d in range(N_DEV):
                rows6 = [slab_ref[d, ROW_DMOD + k:ROW_DMOD + k + 1, :] for k in range(N_MOD)]
                b128_ref[d:d + 1, cols] = pick(lambda j, rows6=rows6: seg(rows6, 3 * j + q))
            c = pick(lambda j: seg(ctx6, 3 * j + q))
            b128_ref[N_DEV:N_DEV + 1, cols] = c
            dmc_ref[0:1, cols] = c

    out_shape = []
    for nm in SMALL_PARAMS:
        out_shape += [_sds(wmv[nm][0].shape)] * 4
    out_shape += [_sds((LANES, ada_n)), _sds((SUBLANES, ada_n)), _sds((SUBLANES, LANES))]
    args = [slab_all, ga_all, gx_all] + flat
    grid_spec = pltpu.PrefetchScalarGridSpec(
        num_scalar_prefetch=1, grid=(1,), in_specs=[_full(a.shape) for a in args],
        out_specs=[_full(s.shape) for s in out_shape])
    outs = _pc(body, name="finalize_small", grid_spec=grid_spec, out_shape=out_shape,
               compiler_params=_params("arbitrary"))(chip_idx, *args)
    res = {nm: tuple(outs[4 * k:4 * k + 4]) for k, nm in enumerate(SMALL_PARAMS)}
    return res, outs[4 * n_p], outs[4 * n_p + 1], outs[4 * n_p + 2]


def _block_diag(w):
    eye = jnp.eye(LRU_BLOCKS, dtype=F32)
    return (w[:, :, None, :] * eye[:, None, :, None]).reshape(LRU_W, LRU_W).astype(BF16)


def _lane_rep(v8):
    return jnp.broadcast_to(v8.reshape(SUBLANES, 1), (SUBLANES, LANES))


def kernel(x, c, ctx, c_ctx, w_ada, b_ada, norm1_g, norm2_g, w_in, ret_decay, conv_w, conv_b, lru_wa, lru_ba, lru_wx, lru_bx, lru_lambda, w_out, w_mlp1, w_mlp2, final_g, loss_target, m_c_ctx, m_w_ada, m_b_ada, m_norm1_g, m_norm2_g, m_w_in, m_ret_decay, m_conv_w, m_conv_b, m_lru_wa, m_lru_ba, m_lru_wx, m_lru_bx, m_lru_lambda, m_w_out, m_w_mlp1, m_w_mlp2, m_final_g, v_c_ctx, v_w_ada, v_b_ada, v_norm1_g, v_norm2_g, v_w_in, v_ret_decay, v_conv_w, v_conv_b, v_lru_wa, v_lru_ba, v_lru_wx, v_lru_bx, v_lru_lambda, v_w_out, v_w_mlp1, v_w_mlp2, v_final_g):
    ax, ay, ac = lax.axis_index("x"), lax.axis_index("y"), lax.axis_index("c")
    chip = 2 * ax + ay
    dev = 4 * ax + 2 * ay + ac
    c_idx = jnp.stack([ac, chip]).astype(jnp.int32)
    j_idx = chip.reshape(1).astype(jnp.int32)

    xt = x[0]
    t_len = xt.shape[0]
    ctxt = ctx[0]
    l_len = ctxt.shape[0]
    tgt = loss_target[0]
    ada_n = w_ada.shape[2]

    def my_half(w2d):
        r = w2d.shape[0] // 2
        return lax.dynamic_slice_in_dim(w2d, ac * r, r, axis=0).astype(BF16)

    pad8 = lambda a: jnp.pad(a, ((0, SUBLANES - a.shape[0]), (0, 0)))
    small = jnp.concatenate([pad8(conv_w[0]), pad8(lru_ba[0]), pad8(lru_bx[0]), pad8(lru_lambda[0])], axis=0)
    b_shard = lax.dynamic_slice_in_dim(b_ada, chip * ada_n, ada_n, axis=1)
    gw_in, _, small_all, a16, mod_parts, lgv, sgv = _head(
        my_half(w_in[0]), pad8(c), small, w_ada[0], b_shard, c_ctx, ret_decay[0])
    w4 = gw_in.reshape(N_CHIP, D_MODEL, IN_COLS // N_CHIP)

    mod_all = mod_parts[0::2].transpose(1, 0, 2).reshape(16, N_CHIP * ada_n)
    mod_me = lax.dynamic_slice_in_dim(mod_all, dev, 1, axis=0)
    sh1, sc1, g1, sh2, sc2, g2 = [mod_me[:, D_MODEL * k:D_MODEL * (k + 1)] for k in range(N_MOD)]
    csh1, csc1 = mod_all[8:9, 0:D_MODEL], mod_all[8:9, D_MODEL:2 * D_MODEL]

    cos2, sin2 = _rotary_tables(t_len)
    cos_c, sin_c = jnp.ones((l_len, DH), F32), jnp.zeros((l_len, DH), F32)
    n1g, n2g = norm1_g, norm2_g
    fg = final_g.reshape(1, D_MODEL)

    small_full = small_all[0::2].transpose(1, 0, 2).reshape(4 * SUBLANES, LRU_W)
    cw = small_full[0:4]
    cb = conv_b
    ba_f, ba_b = small_full[8:9], small_full[9:10]
    bx_f, bx_b = small_full[16:17], small_full[17:18]
    lam_f, lam_b = small_full[24:25], small_full[25:26]
    wa_f, wa_b = _block_diag(lru_wa[0, 0]), _block_diag(lru_wa[0, 1])
    wx_f, wx_b = _block_diag(lru_wx[0, 0]), _block_diag(lru_wx[0, 1])
    zero_h = jnp.zeros((1, LRU_W), F32)

    projc, xrc, hcb16 = _inproj_fwd(ctxt, n1g, csh1, csc1, w4, cos_c, sin_c, "inproj_fwd_ctx")
    s_f, s_b = _ctx_state_fwd(projc, lgv)
    xcc = _conv_fwd(xrc, cw, cb, "conv_fwd_ctx")
    par_f, par_b = (wa_f, wx_f, ba_f, bx_f, lam_f), (wa_b, wx_b, ba_b, bx_b, lam_b)
    hcf, hcbk = _lru_fwd(xcc, par_f, par_b, zero_h, zero_h, "lru_fwd_ctx")
    lru_sf, lru_sb = hcf[l_len - 1:l_len], hcbk[0:1]

    h1, h2 = my_half(w_mlp1[0]), my_half(w_mlp2[0])
    q = h1.shape[0] // 2
    (proj, xrl, hb16), ((gw_1a,),) = _inproj_fwd(xt, n1g, sh1, sc1, w4, cos2, sin2, "inproj_fwd",
                                           comms=(_AllGather([h1[:q]]),))
    (o_f, o_b, spf, spb), ((gw_1b, gw_out),) = _ret_fwd(proj, lgv, s_f, s_b,
                                                       comms=(_AllGather([h1[q:], my_half(w_out[0])]),))
    xcl = _conv_fwd(xrl, cw, cb, "conv_fwd")
    (hf, hbk), ((gw_2a,),) = _lru_fwd(xcl, par_f, par_b, lru_sf, lru_sb, "lru_fwd", comms=(_AllGather([h2[:q]]),))
    wo = gw_out.reshape(D_MODEL, D_MODEL)
    (x1, cat), ((gw_2b,),) = _mix_fwd(o_f, o_b, proj, hf, hbk, wo, xt, g1, comms=(_AllGather([h2[q:]]),))

    (dx1, h2b, ab, dub, dmb, dsc2, dsh2, dg2, dn2g, dfg, lossv) = _mlp(
        x1, n2g, sh2, sc2, g2, fg, (gw_1a, gw_1b), (gw_2a, gw_2b), tgt)
    gw_mlp1 = _tn(h2b, dub, N_CHIP, False, True, "grad_w_mlp1")
    b_1 = gw_mlp1.reshape(N_DEV, D_MODEL // 2, MLP_H // N_CHIP)
    gw_mlp2, ((r_1,),) = _tn(ab, dmb, N_CHIP, True, False, "grad_w_mlp2", comms=(_pair_exchange([b_1]),))

    half = D_MODEL // 4
    top, bot = (0, half), (half, half)
    b_2 = gw_mlp2.reshape(N_DEV, MLP_H // N_DEV, D_MODEL)
    p_1, pb_1 = _pair_add(b_1, r_1, c_idx, "rs_pair_add_w_mlp1")
    (do, dhs, dg, dgate, dyb, dg1), ((q_1a,), (r_2,)) = _mix_bwd(
        o_f, o_b, proj, hf, hbk, wo, cat, dx1, g1, comms=(_chip_exchange([pb_1], top), _pair_exchange([b_2])))
    gw_o = _tn(cat, dyb, 1, False, False, "grad_w_out")
    b_o = gw_o.reshape(N_DEV, D_MODEL // N_DEV, D_MODEL)
    p_2, pb_2 = _pair_add(b_2, r_2, c_idx, "rs_pair_add_w_mlp2")

    ((dq_f, dk_f, dv_f, ds_f, drd_f), (dq_b, dk_b, dv_b, ds_b, drd_b)), ((q_1b,), (q_2a,), (r_o,)) = _ret_bwd(
        proj, lgv, sgv, spf, spb, do,
        comms=(_chip_exchange([pb_1], bot), _chip_exchange([pb_2], top), _pair_exchange([b_o])))
    p_o, pb_o = _pair_add(b_o, r_o, c_idx, "rs_pair_add_w_out")
    h_1 = _chip_add(p_1, (q_1a, q_1b), c_idx, "rs_chip_add_w_mlp1")

    ((dxc_f, dpre_f, dba_f, dbx_f, dlam_f, dh0_f), (dxc_b, dpre_b, dba_b, dbx_b, dlam_b, dh0_b)), (
        (q_2b,), (q_o,), (f_1,)) = _lru_bwd(
        xcl, par_f, par_b, hf, hbk, lru_sf, lru_sb, dhs, dhs, "lru_bwd",
        comms=(_chip_exchange([pb_2], bot), _chip_exchange([pb_o]), _pair_gather([h_1])))
    h_2 = _chip_add(p_2, (q_2a, q_2b), c_idx, "rs_chip_add_w_mlp2")
    h_o = _chip_add(p_o, (q_o,), c_idx, "rs_chip_add_w_out")
    dxr, dcw, dcb = _conv_bwd(dxc_f, dxc_b, xrl, cw, "conv_bwd")
    grad_x, dpb, dn1g, dsh1, dsc1 = _inproj_bwd(
        xt, n1g, sh1, sc1, w4, cos2, sin2, [dq_f, dq_b, dk_f, dk_b, dv_f, dv_b, dg, dxr, dgate], dx1, "inproj_bwd")

    dkc, dvc, drd_c = _ctx_state_bwd(projc, lgv, sgv, ds_f, ds_b)
    zc = jnp.zeros((l_len, LRU_W), F32)
    dhc_f = lax.dynamic_update_slice(zc, dh0_f, (l_len - 1, 0))
    dhc_b = lax.dynamic_update_slice(zc, dh0_b, (0, 0))
    ((dxcc_f, dprec_f, dbac_f, dbxc_f, dlamc_f, _), (dxcc_b, dprec_b, dbac_b, dbxc_b, dlamc_b, _)), _ = _lru_bwd(
        xcc, par_f, par_b, hcf, hcbk, zero_h, zero_h, dhc_f, dhc_b, "lru_bwd_ctx")
    dxrc, dcw_c, dcb_c = _conv_bwd(dxcc_f, dxcc_b, xrc, cw, "conv_bwd_ctx")
    zr = jnp.zeros((l_len, RET_W), BF16)
    _, dpbc, dn1g_c, dcsh1, dcsc1 = _inproj_bwd(
        ctxt, n1g, csh1, csc1, w4, cos_c, sin_c, [zr, zr, dkc, zr, dvc, zr, zr, dxrc, zr],
        jnp.zeros((l_len, D_MODEL), F32), "inproj_bwd_ctx")

    gw_i = _tn(hb16, dpb, N_CHIP, False, True, "grad_w_in", extra=(hcb16, dpbc))
    b_i = gw_i.reshape(N_DEV, D_MODEL // 2, IN_COLS // N_CHIP)
    gwa_f, ((r_i,), (f_2,), (f_o,)) = _tn(xcl, dpre_f, 2, False, True, "grad_lru_gates_f", extra=(xcc, dprec_f),
                                          comms=(_pair_exchange([b_i]), _pair_gather([h_2]), _pair_gather([h_o])))
    p_i, pb_i = _pair_add(b_i, r_i, c_idx, "rs_pair_add_w_in")
    gwa_b, ((q_i,),) = _tn(xcl, dpre_b, 2, False, True, "grad_lru_gates_b", extra=(xcc, dprec_b),
                           comms=(_chip_exchange([pb_i]),))
    slab, ga, gx = _pack_small(
        [lossv, dsh1, dsc1, dg1, dsh2, dsc2, dg2, dcsh1, dcsc1, dn1g, dn1g_c, dn2g, dfg],
        (drd_f, drd_b, drd_c), (dcw, dcw_c), (dcb, dcb_c),
        (dba_f, dbac_f, dba_b, dbac_b, dbx_f, dbxc_f, dbx_b, dbxc_b, dlam_f, dlamc_f, dlam_b, dlamc_b),
        (gwa_f, gwa_b))
    (f_i,), (slab_all, ga_all, gx_all) = _run_comms(
        [_pair_gather([_chip_add(p_i, (q_i,), c_idx, "rs_chip_add_w_in")]), _AllGather([slab, ga, gx])],
        "tail_exchanges")
    g_in, g_out, g_1, g_2 = _shard_of(f_i), _shard_of(f_o), _shard_of(f_1), _shard_of(f_2)
    big = {}
    for nm, w, g, m, v in (("w_in", w_in, g_in, m_w_in, v_w_in), ("w_out", w_out, g_out, m_w_out, v_w_out),
                           ("w_mlp1", w_mlp1, g_1, m_w_mlp1, v_w_mlp1), ("w_mlp2", w_mlp2, g_2, m_w_mlp2, v_w_mlp2)):
        go, d_, mn, vn = _adamw(w[0], g, m[0], v[0], "adamw_" + nm)
        big[nm] = (go[None], d_[None], mn[None], vn[None])
    params = {
        "b_ada": (b_ada, m_b_ada, v_b_ada), "norm1_g": (norm1_g, m_norm1_g, v_norm1_g),
        "norm2_g": (norm2_g, m_norm2_g, v_norm2_g), "final_g": (final_g, m_final_g, v_final_g),
        "ret_decay": (ret_decay, m_ret_decay, v_ret_decay), "conv_w": (conv_w, m_conv_w, v_conv_w),
        "conv_b": (conv_b, m_conv_b, v_conv_b), "lru_wa": (lru_wa, m_lru_wa, v_lru_wa),
        "lru_ba": (lru_ba, m_lru_ba, v_lru_ba), "lru_wx": (lru_wx, m_lru_wx, v_lru_wx),
        "lru_bx": (lru_bx, m_lru_bx, v_lru_bx), "lru_lambda": (lru_lambda, m_lru_lambda, v_lru_lambda),
    }
    as2d = {
        "b_ada": lambda a: a, "norm1_g": lambda a: a, "norm2_g": lambda a: a, "conv_b": lambda a: a,
        "final_g": lambda a: a.reshape(1, D_MODEL), "ret_decay": lambda a: _lane_rep(a.reshape(-1)),
        "conv_w": lambda a: a[0], "lru_ba": lambda a: a[0], "lru_bx": lambda a: a[0], "lru_lambda": lambda a: a[0],
        "lru_wa": lambda a: a.reshape(2 * LRU_W, LRU_BD), "lru_wx": lambda a: a.reshape(2 * LRU_W, LRU_BD),
    }
    res, b128, dmc8, loss8 = _finalize_small(
        j_idx, slab_all, ga_all, gx_all, {nm: tuple(as2d[nm](a) for a in params[nm]) for nm in SMALL_PARAMS})
    loss = loss8[0, 0]
    small_out = {}
    for nm in SMALL_PARAMS:
        shp = params[nm][0].shape
        if nm == "ret_decay":
            small_out[nm] = tuple(o[:, 0].reshape(shp) for o in res[nm])
        else:
            small_out[nm] = tuple(o.reshape(shp) for o in res[nm])

    g_ada = _ada_grad(jnp.pad(a16.T, ((0, 0), (0, LANES - 16))), b128)
    g_ada, d_ada, m_ada, v_ada = _adamw(w_ada[0], g_ada, m_w_ada[0], v_w_ada[0], "adamw_w_ada")

    (cparts,) = _all_gather([_cctx_partial(dmc8, w_ada[0])], "gather_cctx")
    g_cc, d_cc, m_cc, v_cc = _cctx_final(cparts, c_ctx, m_c_ctx, v_c_ctx)
    small_out["c_ctx"] = tuple(a.reshape(D_MODEL) for a in (g_cc, d_cc, m_cc, v_cc))
    small_out["w_ada"] = (g_ada[None], d_ada[None], m_ada[None], v_ada[None])
    small_out.update(big)

    order = ["c_ctx", "w_ada", "b_ada", "norm1_g", "norm2_g", "w_in", "ret_decay", "conv_w", "conv_b", "lru_wa", "lru_ba",
             "lru_wx", "lru_bx", "lru_lambda", "w_out", "w_mlp1", "w_mlp2", "final_g"]
    outs = [loss, grad_x[None]]
    for k in range(4):
        outs += [small_out[nm][k] for nm in order]
    return tuple(outs)
```

```python
import math

import jax
import jax.numpy as jnp
from jax import lax
from jax.experimental import pallas as pl
from jax.experimental.pallas import tpu as pltpu

F32 = jnp.float32
BF16 = jnp.bfloat16

D_MODEL = 1024
HEADS = 4
DH = 128
CHUNK = 256
RET_W = HEADS * DH
LRU_W = 512
LRU_BLOCKS = 8
LRU_BD = LRU_W // LRU_BLOCKS
LRU_C = 8.0
IN_COLS = 4 * RET_W + 2 * LRU_W
MLP_H = 4 * D_MODEL
N_MOD = 6
GRID_W = 64
ROPE_BASE = 10000.0
K_SCALE = DH ** -0.5
EPS = 1e-6
GELU_K = math.sqrt(2.0 / math.pi)
GELU_C = 0.044715

ADAM_LR = 0.001
ADAM_B1 = 0.9
ADAM_B2 = 0.999
ADAM_EPS = 1e-08
ADAM_WD = 0.01
ADAM_STEP = 10

N_DEV = 8
N_CHIP = 4
SUBLANES = 8
LANES = 128
VMEM_LIMIT_V7X = 56 * 1024 * 1024
MESH = pl.DeviceIdType.MESH
ANY = pl.BlockSpec(memory_space=pl.ANY)


def _pc(body, **kw):
    return pl.pallas_call(body, **kw)


def _params(*sem):
    return pltpu.CompilerParams(dimension_semantics=sem if sem else None, vmem_limit_bytes=VMEM_LIMIT_V7X)


def _tile(t, big=False):
    if big and t >= 1024:
        return 512
    return 256 if t >= 256 else t


def _sds(shape, dtype=F32):
    return jax.ShapeDtypeStruct(tuple(shape), dtype)


def _full(shape):
    nd = len(shape)
    return pl.BlockSpec(tuple(shape), lambda *_: (0,) * nd)


def _sigmoid(x):
    return 0.5 * jnp.tanh(0.5 * x) + 0.5


def _log1p_pos(y):
    s = y * (1.0 - y * (0.5 - y * (1.0 / 3.0 - y * (0.25 - y * (0.2 - y / 6.0)))))
    return jnp.where(y < 0.03, s, jnp.log(1.0 + y))


def _softplus(z):
    return jnp.maximum(z, 0.0) + _log1p_pos(jnp.exp(-jnp.abs(z)))


def _one_minus_sq(la, a):
    return -jnp.tanh(la) * (1.0 + a * a)


def _rms(x):
    r = lax.rsqrt(jnp.mean(x * x, axis=-1, keepdims=True) + EPS)
    return x * r, r


def _dot(a, b):
    return jnp.dot(a, b, preferred_element_type=F32)


def _dot_nt(a, b):
    return lax.dot_general(a, b, (((1,), (1,)), ((), ())), preferred_element_type=F32)


def _dot_tn(a, b):
    return lax.dot_general(a, b, (((0,), (0,)), ((), ())), preferred_element_type=F32)


def _sum0(x):
    return jnp.sum(x, axis=0, keepdims=True)


def _norm_mod_bwd(x, g, sc, dh):
    xh, r = _rms(x)
    hn = xh * g
    dhn = dh * (1.0 + sc)
    dxh = dhn * g
    dx = r * (dxh - xh * jnp.mean(dxh * xh, axis=-1, keepdims=True))
    return dx, _sum0(dhn * xh), _sum0(dh), _sum0(dh * hn)


def _dev_index(p):
    return 4 * p[0] + 2 * p[1] + p[2]


def _mesh_pos():
    return lax.axis_index("x"), lax.axis_index("y"), lax.axis_index("c")


class _AllGather:
    def __init__(self, arrs):
        n = len(arrs)
        self.arrays = list(arrs)
        self.out_shapes = [_sds((N_DEV,) + a.shape, a.dtype) for a in arrs]
        self.scratch = ([pltpu.VMEM(a.shape, a.dtype) for a in arrs]
                        + [pltpu.SemaphoreType.DMA((7 * n,)), pltpu.SemaphoreType.DMA((7 * n,)),
                           pltpu.SemaphoreType.DMA((n,))])
        self.aliases = {}

    def _parts(self, ins, outs, scr):
        n = len(self.arrays)
        stage = scr[:n]
        send_sems, recv_sems, local_sems = scr[n:]
        x, y, c = _mesh_pos()
        me, sib = (x, y, c), (x, y, 1 - c)
        chips = [(1 - x, y), (x, 1 - y), (1 - x, 1 - y)]

        def copy(t, k, block, to, own=False):
            dst = outs[t].at[_dev_index(block)]
            return pltpu.make_async_remote_copy(
                src_ref=ins[t] if own else dst, dst_ref=dst,
                send_sem=send_sems.at[7 * t + k], recv_sem=recv_sems.at[7 * t + k],
                device_id=to, device_id_type=MESH)

        first = []
        for t in range(n):
            first.append(copy(t, 0, me, sib, own=True))
            for j, ch in enumerate(chips):
                first.append(copy(t, 1 + j, me, (*ch, c), own=True))
        stage_in = [pltpu.make_async_copy(ins[t], stage[t], local_sems.at[t]) for t in range(n)]
        mine = [pltpu.make_async_copy(stage[t], outs[t].at[_dev_index(me)], local_sems.at[t]) for t in range(n)]
        return n, c, me, sib, chips, copy, first, stage_in, mine

    def start(self, ins, outs, scr):
        n, _, _, _, _, _, first, stage_in, mine = self._parts(ins, outs, scr)
        for cp in stage_in:
            cp.start()
        for cp in first:
            cp.start()
        for t in range(n):
            stage_in[t].wait()
            mine[t].start()

    def relay(self, ins, outs, scr):
        n, c, me, sib, chips, copy, _, _, _ = self._parts(ins, outs, scr)
        for j, ch in enumerate(chips):
            for t in range(n):
                copy(t, 1 + j, (*ch, c), me).wait_recv()
                copy(t, 4 + j, (*ch, c), sib).start()

    def finish(self, ins, outs, scr):
        n, c, me, sib, chips, copy, first, _, mine = self._parts(ins, outs, scr)
        passed = [copy(t, 4 + j, (*ch, c), sib) for j, ch in enumerate(chips) for t in range(n)]
        for t in range(n):
            copy(t, 0, sib, me).wait_recv()
            for j, ch in enumerate(chips):
                copy(t, 4 + j, (*ch, 1 - c), me).wait_recv()
        for cp in first + passed:
            cp.wait_send()
        for cp in mine:
            cp.wait()


class _Exchange:
    def __init__(self, arrays, out_shapes, plan, n_copies, aliases=None):
        self.arrays = list(arrays)
        self.out_shapes = list(out_shapes)
        self.plan = plan
        self.scratch = [pltpu.SemaphoreType.DMA((n_copies,)), pltpu.SemaphoreType.DMA((n_copies,))]
        self.aliases = aliases or {}

    def _copies(self, ins, outs, scr):
        send_sems, recv_sems = scr
        snd, rcv = [], []
        for i, (src, dst, peer, lands) in enumerate(self.plan(ins, outs, _mesh_pos())):
            kw = dict(send_sem=send_sems.at[i], recv_sem=recv_sems.at[i], device_id=peer, device_id_type=MESH)
            snd.append(pltpu.make_async_remote_copy(src_ref=src, dst_ref=dst, **kw))
            rcv.append(pltpu.make_async_remote_copy(src_ref=src, dst_ref=lands, **kw))
        return snd, rcv

    def start(self, ins, outs, scr):
        for cp in self._copies(ins, outs, scr)[0]:
            cp.start()

    def relay(self, ins, outs, scr):
        pass

    def finish(self, ins, outs, scr):
        snd, rcv = self._copies(ins, outs, scr)
        for cp in rcv:
            cp.wait_recv()
        for cp in snd:
            cp.wait_send()


def _pair_exchange(grads):
    n = len(grads)

    def plan(ins, outs, pos):
        x, y, c = pos
        return [(ins[t].at[2 * j + (1 - c)], outs[t].at[j], (x, y, 1 - c), outs[t].at[j])
                for t in range(n) for j in range(N_CHIP)]

    return _Exchange(grads, [_sds((N_CHIP,) + g.shape[1:], g.dtype) for g in grads], plan, N_CHIP * n)


def _chip_exchange(parts, rows=None):
    n = len(parts)

    def plan(ins, outs, pos):
        x, y, c = pos
        chips = [(1 - x, y), (x, 1 - y), (1 - x, 1 - y)]

        def src(t, ch):
            blk = ins[t].at[2 * ch[0] + ch[1]]
            return blk if rows is None else blk.at[pl.ds(rows[0], rows[1])]

        return [(src(t, ch), outs[t].at[k], (*ch, c), outs[t].at[k]) for t in range(n) for k, ch in enumerate(chips)]

    shapes = [_sds((3, p.shape[1] if rows is None else rows[1]) + p.shape[2:], p.dtype) for p in parts]
    return _Exchange(parts, shapes, plan, 3 * n)


def _pair_gather(bufs):
    n = len(bufs)

    def plan(ins, outs, pos):
        x, y, c = pos
        return [(ins[t].at[c], outs[t].at[c], (x, y, 1 - c), outs[t].at[1 - c]) for t in range(n)]

    return _Exchange(bufs, [_sds(b.shape, b.dtype) for b in bufs], plan, n, aliases={t: t for t in range(n)})


def _run_comms(comms, name):
    c_in = [len(cm.arrays) for cm in comms]
    c_out = [len(cm.out_shapes) for cm in comms]
    c_scr = [len(cm.scratch) for cm in comms]
    aliases = {}
    for k, cm in enumerate(comms):
        for a, b in cm.aliases.items():
            aliases[sum(c_in[:k]) + a] = sum(c_out[:k]) + b

    def split(refs, counts):
        out, pos = [], 0
        for cnt in counts:
            out.append(refs[pos:pos + cnt])
            pos += cnt
        return out

    def body(*refs):
        ins = split(refs[:sum(c_in)], c_in)
        outs = split(refs[sum(c_in):sum(c_in) + sum(c_out)], c_out)
        scr = split(refs[sum(c_in) + sum(c_out):], c_scr)
        for phase in ("start", "relay", "finish"):
            for k, cm in enumerate(comms):
                getattr(cm, phase)(ins[k], outs[k], scr[k])

    outs = _pc(body, name=name, out_shape=[s for cm in comms for s in cm.out_shapes],
               in_specs=[ANY] * sum(c_in), out_specs=[ANY] * sum(c_out), input_output_aliases=aliases,
               scratch_shapes=[s for cm in comms for s in cm.scratch],
               compiler_params=_params())(*[a for cm in comms for a in cm.arrays])
    return split(list(outs), c_out)


def _all_gather(arrs, name):
    return _run_comms([_AllGather(arrs)], name)[0]


def _call(body, *, name, grid, in_specs, out_specs, out_shape, scratch_shapes, sem, args, comms=()):
    n_in, n_out, n_scr = len(in_specs), len(out_specs), len(scratch_shapes)
    c_in = [len(cm.arrays) for cm in comms]
    c_out = [len(cm.out_shapes) for cm in comms]
    c_scr = [len(cm.scratch) for cm in comms]
    aliases = {}
    for k, cm in enumerate(comms):
        for a, b in cm.aliases.items():
            aliases[n_in + sum(c_in[:k]) + a] = n_out + sum(c_out[:k]) + b

    def split(refs, counts):
        out, pos = [], 0
        for cnt in counts:
            out.append(refs[pos:pos + cnt])
            pos += cnt
        return out

    def wrapped(*refs):
        ins = refs[:n_in + sum(c_in)]
        outs = refs[len(ins):len(ins) + n_out + sum(c_out)]
        scr = refs[len(ins) + len(outs):]
        cins, couts, cscr = split(ins[n_in:], c_in), split(outs[n_out:], c_out), split(scr[n_scr:], c_scr)
        if comms:
            first = pl.program_id(0) == 0
            last = pl.program_id(0) == grid[0] - 1
            for k in range(1, len(grid)):
                first = jnp.logical_and(first, pl.program_id(k) == 0)
                last = jnp.logical_and(last, pl.program_id(k) == grid[k] - 1)

            @pl.when(first)
            def _():
                for k, cm in enumerate(comms):
                    cm.start(cins[k], couts[k], cscr[k])
        body(*ins[:n_in], *outs[:n_out], *scr[:n_scr])
        if comms:
            relay_early = len(grid) == 1 and grid[0] >= 4
            if relay_early:
                @pl.when(pl.program_id(0) == (7 * grid[0]) // 8 - 1)
                def _():
                    for k, cm in enumerate(comms):
                        cm.relay(cins[k], couts[k], cscr[k])

            @pl.when(last)
            def _():
                for k, cm in enumerate(comms):
                    if not relay_early:
                        cm.relay(cins[k], couts[k], cscr[k])
                    cm.finish(cins[k], couts[k], cscr[k])

    outs = _pc(wrapped, name=name, grid=grid,
               in_specs=list(in_specs) + [ANY] * sum(c_in), out_specs=list(out_specs) + [ANY] * sum(c_out),
               out_shape=list(out_shape) + [s for cm in comms for s in cm.out_shapes],
               scratch_shapes=list(scratch_shapes) + [s for cm in comms for s in cm.scratch],
               input_output_aliases=aliases, compiler_params=_params(*sem),
               )(*args, *[a for cm in comms for a in cm.arrays])
    outs = list(outs)
    return outs[:n_out], split(outs[n_out:], c_out)


def _row_block(r):
    for b in (512, 256, 128, 64, 32, 16, 8):
        if r % b == 0:
            return b
    return r


def _pair_add(g, recv, cj_idx, name):
    _, r, cc = g.shape
    br = _row_block(r)

    def body(cj_ref, g_ref, r_ref, own_ref, pb_ref):
        s = g_ref[...] + r_ref[...]
        pb_ref[...] = s.astype(BF16)

        @pl.when(pl.program_id(1) == cj_ref[1])
        def _():
            own_ref[...] = s[0]

    grid_spec = pltpu.PrefetchScalarGridSpec(
        num_scalar_prefetch=1, grid=(r // br, N_CHIP),
        in_specs=[pl.BlockSpec((1, br, cc), lambda i, j, cj_ref: (2 * j + cj_ref[0], i, 0)),
                  pl.BlockSpec((1, br, cc), lambda i, j, cj_ref: (j, i, 0))],
        out_specs=[pl.BlockSpec((br, cc), lambda i, j, cj_ref: (i, 0)),
                   pl.BlockSpec((1, br, cc), lambda i, j, cj_ref: (j, i, 0))])
    return _pc(body, name=name, grid_spec=grid_spec,
               out_shape=[_sds((r, cc)), _sds((N_CHIP, r, cc), BF16)],
               compiler_params=_params("arbitrary", "arbitrary"))(cj_idx, g, recv)


def _chip_add(p, qs, cj_idx, name):
    r, cc = p.shape
    nq = len(qs)
    br = _row_block(r // nq)
    nb = r // nq // br

    def body(cj_ref, p_ref, *refs):
        o_ref = refs[-1]
        if nq == 2:
            top = pl.program_id(0) < nb
            q = [jnp.where(top, refs[0][k], refs[1][k]).astype(F32) for k in range(3)]
        else:
            q = [refs[0][k].astype(F32) for k in range(3)]
        o_ref[0] = ((p_ref[...] + q[0]) + q[1]) + q[2]

    q_specs = [pl.BlockSpec((3, br, cc), lambda i, cj_ref, h=h: (0, jnp.clip(i - h * nb, 0, nb - 1), 0))
               for h in range(nq)]
    grid_spec = pltpu.PrefetchScalarGridSpec(
        num_scalar_prefetch=1, grid=(r // br,),
        in_specs=[pl.BlockSpec((br, cc), lambda i, cj_ref: (i, 0))] + q_specs,
        out_specs=pl.BlockSpec((1, br, cc), lambda i, cj_ref: (cj_ref[0], i, 0)))
    return _pc(body, name=name, grid_spec=grid_spec, out_shape=_sds((2, r, cc)),
               compiler_params=_params("arbitrary"))(cj_idx, p, *qs)


def _shard_of(both):
    return both.reshape((2 * both.shape[1],) + both.shape[2:])


ADAMW_CHUNKS = 4


def _adamw(w, g, m, v, name):
    r, cc = w.shape
    rows = r // ADAMW_CHUNKS
    assert rows * ADAMW_CHUNKS == r and rows % SUBLANES == 0
    c1 = 1.0 - ADAM_B1 ** ADAM_STEP
    c2 = 1.0 - ADAM_B2 ** ADAM_STEP

    def body(w_hbm, g_hbm, m_hbm, v_hbm, go_hbm, d_hbm, mo_hbm, vo_hbm, wb, gb, mb, vb, sem_in, sem_out):
        srcs, bufs, dsts = (w_hbm, g_hbm, m_hbm, v_hbm), (wb, gb, mb, vb), (d_hbm, go_hbm, mo_hbm, vo_hbm)

        def load(a, k):
            sl = pl.ds(k * rows, rows)
            return pltpu.make_async_copy(srcs[a].at[sl], bufs[a].at[sl], sem_in.at[a, k])

        def store(a, k):
            sl = pl.ds(k * rows, rows)
            return pltpu.make_async_copy(bufs[a].at[sl], dsts[a].at[sl], sem_out.at[a, k])

        for k in range(ADAMW_CHUNKS):
            for a in range(4):
                load(a, k).start(priority=(a + k) % 2)
        for k in range(ADAMW_CHUNKS):
            for a in range(4):
                load(a, k).wait()
            store(1, k).start(priority=(1 + k) % 2)
            sl = pl.ds(k * rows, rows)
            gg = gb[sl]
            mn = ADAM_B1 * mb[sl] + (1.0 - ADAM_B1) * gg
            vn = ADAM_B2 * vb[sl] + (1.0 - ADAM_B2) * (gg * gg)
            mh = mn / c1
            vh = vn / c2
            wb[sl] = -ADAM_LR * (mh / (jnp.sqrt(vh) + ADAM_EPS) + ADAM_WD * wb[sl])
            mb[sl] = mn
            vb[sl] = vn
            for a in (0, 2, 3):
                store(a, k).start(priority=(a + k) % 2)
        for k in range(ADAMW_CHUNKS):
            for a in range(4):
                store(a, k).wait()

    go, d, mo, vo = _pc(body, name=name, in_specs=[ANY] * 4, out_specs=[ANY] * 4, out_shape=[_sds((r, cc))] * 4,
                        scratch_shapes=[pltpu.VMEM((r, cc), F32)] * 4 + [pltpu.SemaphoreType.DMA((4, ADAMW_CHUNKS))] * 2,
                        compiler_params=_params())(w, g, m, v)
    return go, d, mo, vo


def _head(w_half, c8, small, w_ada, b_shard, c_ctx, ret_decay):
    ada_n = w_ada.shape[1]
    mod_sds = _sds((16, ada_n))
    ag_w, ag_c, ag_m = _AllGather([w_half]), _AllGather([c8, small]), _AllGather([mod_sds])
    n_w, n_c, n_m = len(ag_w.scratch), len(ag_c.scratch), len(ag_m.scratch)

    def body(w_ref, c_ref, s_ref, wada_ref, b_ref, cc_ref, rd_ref,
             gw_ref, call_ref, sall_ref, a_ref, modp_ref, mall_ref, lg_ref, sg_ref, *scr):
        scr_w, scr_c, scr_m = scr[:n_w], scr[n_w:n_w + n_c], scr[n_w + n_c:n_w + n_c + n_m]
        c_v, w_v, m_v, sems = scr[n_w + n_c + n_m:]
        ag_w.start((w_ref,), (gw_ref,), scr_w)
        ag_c.start((c_ref, s_ref), (call_ref, sall_ref), scr_c)
        load_w = pltpu.make_async_copy(wada_ref, w_v, sems.at[0])
        load_w.start()
        rd = rd_ref[...]
        lg_ref[...] = -_softplus(-rd)
        sg_ref[...] = _sigmoid(-rd)
        ag_c.relay((c_ref, s_ref), (call_ref, sall_ref), scr_c)
        ag_c.finish((c_ref, s_ref), (call_ref, sall_ref), scr_c)
        load_c = pltpu.make_async_copy(call_ref, c_v, sems.at[1])
        load_c.start()
        load_c.wait()
        a_ref[...] = jnp.zeros_like(a_ref)
        for d in range(N_DEV):
            cd = c_v[d, 0:1, :]
            a_ref[d:d + 1, :] = cd * _sigmoid(cd)
        cc = cc_ref[...]
        a_ref[N_DEV:N_DEV + 1, :] = cc * _sigmoid(cc)
        load_w.wait()
        m_v[...] = jnp.dot(a_ref[...], w_v[...], preferred_element_type=F32,
                           precision=lax.Precision.HIGHEST) + b_ref[...]
        put = pltpu.make_async_copy(m_v, modp_ref, sems.at[2])
        put.start()
        put.wait()
        ag_m.start((modp_ref,), (mall_ref,), scr_m)
        ag_m.relay((modp_ref,), (mall_ref,), scr_m)
        ag_m.finish((modp_ref,), (mall_ref,), scr_m)
        ag_w.relay((w_ref,), (gw_ref,), scr_w)
        ag_w.finish((w_ref,), (gw_ref,), scr_w)

    rd = jnp.broadcast_to(ret_decay.reshape(2, HEADS).T[:, :, None], (HEADS, 2, LANES))
    lane = _full((HEADS, 2, LANES))
    outs = _pc(
        body, name="head",
        in_specs=[ANY, ANY, ANY, ANY, _full((1, ada_n)), _full((1, D_MODEL)), lane],
        out_specs=[ANY, ANY, ANY, _full((16, D_MODEL)), ANY, ANY, lane, lane],
        out_shape=ag_w.out_shapes + ag_c.out_shapes + [_sds((16, D_MODEL)), mod_sds] + ag_m.out_shapes
        + [_sds((HEADS, 2, LANES))] * 2,
        scratch_shapes=ag_w.scratch + ag_c.scratch + ag_m.scratch
        + [pltpu.VMEM((N_DEV,) + c8.shape, F32), pltpu.VMEM(w_ada.shape, F32), pltpu.VMEM((16, ada_n), F32),
           pltpu.SemaphoreType.DMA((3,))],
        compiler_params=_params(),
    )(w_half, c8, small, w_ada, b_shard, c_ctx.reshape(1, D_MODEL), rd)
    gw, c_all, small_all, a16, _, mod_all, lgv, sgv = outs
    return gw, c_all, small_all, a16, mod_all, lgv, sgv


def _ada_grad(at, b):
    n = b.shape[1]
    bn = 512

    def body(a_ref, b_ref, o_ref):
        o_ref[...] = jnp.dot(a_ref[...], b_ref[...], preferred_element_type=F32, precision=lax.Precision.HIGHEST)

    return _pc(body, name="ada_grad", grid=(n // bn,),
               in_specs=[_full((D_MODEL, LANES)), pl.BlockSpec((LANES, bn), lambda i: (0, i))],
               out_specs=pl.BlockSpec((D_MODEL, bn), lambda i: (0, i)), out_shape=_sds((D_MODEL, n)),
               compiler_params=_params("arbitrary"))(at, b)


def _cctx_partial(dmc8, w_ada):
    n = w_ada.shape[1]
    bn = 512

    def body(d_ref, w_ref, o_ref):
        @pl.when(pl.program_id(0) == 0)
        def _():
            o_ref[...] = jnp.zeros_like(o_ref)
        o_ref[...] += lax.dot_general(d_ref[...], w_ref[...], (((1,), (1,)), ((), ())),
                                      preferred_element_type=F32, precision=lax.Precision.HIGHEST)

    return _pc(body, name="cctx_partial", grid=(n // bn,),
               in_specs=[pl.BlockSpec((8, bn), lambda i: (0, i)), pl.BlockSpec((D_MODEL, bn), lambda i: (0, i))],
               out_specs=_full((8, D_MODEL)), out_shape=_sds((8, D_MODEL)),
               compiler_params=_params("arbitrary"))(dmc8, w_ada)


def _cctx_final(parts, c_ctx, m, v):
    c1 = 1.0 - ADAM_B1 ** ADAM_STEP
    c2 = 1.0 - ADAM_B2 ** ADAM_STEP

    def body(p_ref, c_ref, m_ref, v_ref, g_ref, d_ref, mo_ref, vo_ref):
        s = ((p_ref[0, 0:1, :] + p_ref[2, 0:1, :]) + p_ref[4, 0:1, :]) + p_ref[6, 0:1, :]
        z = c_ref[...]
        sg = _sigmoid(z)
        gg = s * (sg * (1.0 + z * (1.0 - sg)))
        g_ref[...] = gg
        mn = ADAM_B1 * m_ref[...] + (1.0 - ADAM_B1) * gg
        vn = ADAM_B2 * v_ref[...] + (1.0 - ADAM_B2) * (gg * gg)
        d_ref[...] = -ADAM_LR * ((mn / c1) / (jnp.sqrt(vn / c2) + ADAM_EPS) + ADAM_WD * z)
        mo_ref[...] = mn
        vo_ref[...] = vn

    row = _full((1, D_MODEL))
    return _pc(body, name="cctx_final", out_shape=[_sds((1, D_MODEL))] * 4,
               in_specs=[_full(parts.shape), row, row, row], out_specs=[row] * 4,
               compiler_params=_params())(parts, c_ctx.reshape(1, D_MODEL), m.reshape(1, D_MODEL), v.reshape(1, D_MODEL))


def _rotary_tables(t_len):
    rows = t_len // GRID_W
    n_freq = DH // 4
    inv = ROPE_BASE ** (-jnp.arange(n_freq, dtype=F32) / n_freq)
    row_ang = jnp.arange(rows, dtype=F32)[:, None] * inv
    col_ang = jnp.arange(GRID_W, dtype=F32)[:, None] * inv

    def spread(fn):
        return jnp.concatenate([jnp.repeat(fn(row_ang), GRID_W, axis=0), jnp.tile(fn(col_ang), (rows, 1))], axis=-1)

    cos, sin = spread(jnp.cos), spread(jnp.sin)
    return jnp.concatenate([cos, cos], axis=-1), jnp.concatenate([-sin, sin], axis=-1)


def _inproj_fwd(x, gn, sh, sc, w4, cos2, sin2, name, comms=()):
    t = x.shape[0]
    tm = _tile(t, True)
    nc = IN_COLS // N_CHIP

    def body(x_ref, gn_ref, sh_ref, sc_ref, w_ref, c_ref, s_ref, p_ref, xr_ref, hb_ref, p_s):
        xh, _ = _rms(x_ref[...])
        h = xh * gn_ref[...] * (1.0 + sc_ref[...]) + sh_ref[...]
        hb = h.astype(BF16)
        hb_ref[...] = hb
        for j in range(N_CHIP):
            p_s[:, nc * j:nc * (j + 1)] = _dot(hb, w_ref[j])
        cc = c_ref[...]
        ss = s_ref[...]
        for hh in range(2 * HEADS):
            blk = p_s[:, DH * hh:DH * (hh + 1)]
            rot = blk * cc + pltpu.roll(blk, DH // 2, 1) * ss
            if hh >= HEADS:
                rot = rot * K_SCALE
            p_ref[:, DH * hh:DH * (hh + 1)] = rot.astype(BF16)
        p_ref[:, 2 * RET_W:] = p_s[:, 2 * RET_W:].astype(BF16)
        xr_ref[...] = p_s[:, 4 * RET_W:4 * RET_W + LRU_W]

    row = _full((1, D_MODEL))
    outs, couts = _call(
        body, name=name, grid=(t // tm,),
        in_specs=[pl.BlockSpec((tm, D_MODEL), lambda i: (i, 0)), row, row, row, _full(w4.shape),
                  pl.BlockSpec((tm, DH), lambda i: (i, 0)), pl.BlockSpec((tm, DH), lambda i: (i, 0))],
        out_specs=[pl.BlockSpec((tm, IN_COLS), lambda i: (i, 0)), pl.BlockSpec((tm, LRU_W), lambda i: (i, 0)),
                   pl.BlockSpec((tm, D_MODEL), lambda i: (i, 0))],
        out_shape=[_sds((t, IN_COLS), BF16), _sds((t, LRU_W)), _sds((t, D_MODEL), BF16)],
        scratch_shapes=[pltpu.VMEM((tm, IN_COLS), F32)], sem=("arbitrary",),
        args=(x, gn, sh, sc, w4, cos2, sin2), comms=comms)
    return (outs, couts) if comms else outs


def _inproj_bwd(x, gn, sh, sc, w4, cos2, sin2, pieces, dres, name):
    t = x.shape[0]
    tm = _tile(t)
    nc = IN_COLS // N_CHIP

    def body(x_ref, gn_ref, sh_ref, sc_ref, w_ref, c_ref, s_ref, dqf, dqb, dkf, dkb, dvf, dvb, dg, dxr, dgt, dres_ref,
             dx_ref, dpb_ref, dgn_ref, dsh_ref, dsc_ref):
        cc = c_ref[...]
        ss = s_ref[...]
        dq = dqf[...].astype(F32) + dqb[...].astype(F32)
        dk = dkf[...].astype(F32) + dkb[...].astype(F32)
        for hh in range(HEADS):
            sl = slice(DH * hh, DH * (hh + 1))
            b = dq[:, sl]
            dpb_ref[:, sl] = (b * cc + pltpu.roll(b * ss, DH // 2, 1)).astype(BF16)
            b = dk[:, sl]
            dpb_ref[:, RET_W + DH * hh:RET_W + DH * (hh + 1)] = (
                (b * cc + pltpu.roll(b * ss, DH // 2, 1)) * K_SCALE).astype(BF16)
        dpb_ref[:, 2 * RET_W:3 * RET_W] = (dvf[...].astype(F32) + dvb[...].astype(F32)).astype(BF16)
        dpb_ref[:, 3 * RET_W:4 * RET_W] = dg[...].astype(BF16)
        dpb_ref[:, 4 * RET_W:4 * RET_W + LRU_W] = dxr[...].astype(BF16)
        dpb_ref[:, 4 * RET_W + LRU_W:IN_COLS] = dgt[...].astype(BF16)
        dh = _dot_nt(dpb_ref[:, 0:nc], w_ref[0])
        for j in range(1, N_CHIP):
            dh = dh + _dot_nt(dpb_ref[:, nc * j:nc * (j + 1)], w_ref[j])
        dx, dgn_t, dsh_t, dsc_t = _norm_mod_bwd(x_ref[...], gn_ref[...], sc_ref[...], dh)
        dx_ref[...] = dres_ref[...] + dx

        @pl.when(pl.program_id(0) == 0)
        def _():
            dgn_ref[...] = jnp.zeros_like(dgn_ref)
            dsh_ref[...] = jnp.zeros_like(dsh_ref)
            dsc_ref[...] = jnp.zeros_like(dsc_ref)
        dgn_ref[...] += dgn_t
        dsh_ref[...] += dsh_t
        dsc_ref[...] += dsc_t

    row = _full((1, D_MODEL))
    pc = pl.BlockSpec((tm, RET_W), lambda i: (i, 0))
    big = pl.BlockSpec((tm, D_MODEL), lambda i: (i, 0))
    return _pc(body, name=name, grid=(t // tm,),
               in_specs=[big, row, row, row, _full(w4.shape),
                         pl.BlockSpec((tm, DH), lambda i: (i, 0)), pl.BlockSpec((tm, DH), lambda i: (i, 0))]
               + [pc] * 9 + [big],
               out_specs=[big, pl.BlockSpec((tm, IN_COLS), lambda i: (i, 0)), row, row, row],
               out_shape=[_sds((t, D_MODEL)), _sds((t, IN_COLS), BF16), _sds((1, D_MODEL)), _sds((1, D_MODEL)),
                          _sds((1, D_MODEL))],
               compiler_params=_params("arbitrary"))(x, gn, sh, sc, w4, cos2, sin2, *pieces, dres)


def _halo_specs(t, tm):
    n8 = tm // SUBLANES
    last8 = t // SUBLANES - 1
    prev = pl.BlockSpec((SUBLANES, LRU_W), lambda i: (jnp.maximum(i * n8 - 1, 0), 0))
    main = pl.BlockSpec((tm, LRU_W), lambda i: (i, 0))
    nxt = pl.BlockSpec((SUBLANES, LRU_W), lambda i: (jnp.minimum((i + 1) * n8, last8), 0))
    return prev, main, nxt


def _with_halo(prev_ref, main_ref, next_ref, i, nt):
    prev = jnp.where(i > 0, prev_ref[...], 0.0)
    nxt = jnp.where(i < nt - 1, next_ref[...], 0.0)
    return jnp.concatenate([prev, main_ref[...], nxt], axis=0)


def _conv_fwd(xr, cw, cb, name):
    t = xr.shape[0]
    tm = _tile(t, True)
    nt = t // tm
    n = tm + 2 * SUBLANES
    mid = slice(SUBLANES, SUBLANES + tm)

    def body(p_ref, m_ref, n_ref, w_ref, b_ref, o_ref):
        xp = _with_halo(p_ref, m_ref, n_ref, pl.program_id(0), nt)
        acc = b_ref[...] + pltpu.roll(xp, 1, 0)[mid] * w_ref[0:1, :]
        acc = acc + xp[mid] * w_ref[1:2, :]
        acc = acc + pltpu.roll(xp, n - 1, 0)[mid] * w_ref[2:3, :]
        acc = acc + pltpu.roll(xp, n - 2, 0)[mid] * w_ref[3:4, :]
        o_ref[...] = acc

    return _pc(body, name=name, grid=(nt,),
               in_specs=[*_halo_specs(t, tm), _full((4, LRU_W)), _full((1, LRU_W))],
               out_specs=pl.BlockSpec((tm, LRU_W), lambda i: (i, 0)), out_shape=_sds((t, LRU_W)),
               compiler_params=_params("arbitrary"))(xr, xr, xr, cw, cb)


def _conv_bwd(dxc_a, dxc_b, xr, cw, name):
    t = xr.shape[0]
    tm = _tile(t, True)
    nt = t // tm
    n = tm + 2 * SUBLANES
    mid = slice(SUBLANES, SUBLANES + tm)

    def body(ap_ref, am_ref, an_ref, bp_ref, bm_ref, bn_ref, xp_ref, xm_ref, xn_ref, w_ref, dx_ref, dw_ref, db_ref):
        i = pl.program_id(0)
        dp = _with_halo(ap_ref, am_ref, an_ref, i, nt) + _with_halo(bp_ref, bm_ref, bn_ref, i, nt)
        xp = _with_halo(xp_ref, xm_ref, xn_ref, i, nt)
        dx = pltpu.roll(dp, n - 1, 0)[mid] * w_ref[0:1, :]
        dx = dx + dp[mid] * w_ref[1:2, :]
        dx = dx + pltpu.roll(dp, 1, 0)[mid] * w_ref[2:3, :]
        dx = dx + pltpu.roll(dp, 2, 0)[mid] * w_ref[3:4, :]
        dx_ref[...] = dx.astype(BF16)
        d = dp[mid]

        @pl.when(i == 0)
        def _():
            dw_ref[...] = jnp.zeros_like(dw_ref)
            db_ref[...] = jnp.zeros_like(db_ref)
        dw_ref[0:1, :] += _sum0(d * pltpu.roll(xp, 1, 0)[mid])
        dw_ref[1:2, :] += _sum0(d * xp[mid])
        dw_ref[2:3, :] += _sum0(d * pltpu.roll(xp, n - 1, 0)[mid])
        dw_ref[3:4, :] += _sum0(d * pltpu.roll(xp, n - 2, 0)[mid])
        db_ref[...] += _sum0(d)

    return _pc(body, name=name, grid=(nt,),
               in_specs=[*_halo_specs(t, tm), *_halo_specs(t, tm), *_halo_specs(t, tm), _full((4, LRU_W))],
               out_specs=[pl.BlockSpec((tm, LRU_W), lambda i: (i, 0)), _full((4, LRU_W)), _full((1, LRU_W))],
               out_shape=[_sds((t, LRU_W), BF16), _sds((4, LRU_W)), _sds((1, LRU_W))],
               compiler_params=_params("arbitrary"))(dxc_a, dxc_a, dxc_a, dxc_b, dxc_b, dxc_b, xr, xr, xr, cw)


def _scan_scratch(n, c):
    return pltpu.VMEM((c // LANES, n, LANES), F32)


def _to_lane_blocks(ref, val):
    for lb in range(ref.shape[0]):
        ref[lb] = val[:, lb * LANES:(lb + 1) * LANES]


def _group_scan(a_s, b_s, reverse):
    nb, n, _ = a_s.shape
    ng = n // SUBLANES
    order = range(SUBLANES - 1, -1, -1) if reverse else range(SUBLANES)
    for lb in range(nb):
        prev = None
        for r in order:
            rows = pl.ds(r, ng, stride=SUBLANES)
            a_r, b_r = a_s[lb, rows, :], b_s[lb, rows, :]
            if prev is not None:
                b_r = a_r * prev[1] + b_r
                a_r = a_r * prev[0]
                a_s[lb, rows, :] = a_r
                b_s[lb, rows, :] = b_r
            prev = (a_r, b_r)


def _carry_scans(jobs):
    nb, n, _ = jobs[0][0].shape
    ng = n // SUBLANES

    def step(g, all_crs):
        res = []
        for (a_s, b_s, out_ref, _, reverse), crs in zip(jobs, all_crs):
            gg = (ng - 1 - g) if reverse else g
            off = pl.multiple_of(gg * SUBLANES, SUBLANES)
            new = []
            for lb in range(nb):
                h = a_s[lb, pl.ds(off, SUBLANES), :] * crs[lb] + b_s[lb, pl.ds(off, SUBLANES), :]
                out_ref[pl.ds(off, SUBLANES), pl.ds(lb * LANES, LANES)] = h
                edge = h[0:1, :] if reverse else h[SUBLANES - 1:SUBLANES, :]
                new.append(jnp.broadcast_to(edge, (SUBLANES, LANES)))
            res.append(tuple(new))
        return tuple(res)

    init = tuple(tuple(job[3][:, lb * LANES:(lb + 1) * LANES] for lb in range(nb)) for job in jobs)
    return [jnp.concatenate(crs, axis=1) for crs in lax.fori_loop(0, ng, step, init)]


def _lru_gates(xc, wa_ref, wx_ref, ba, bx, lam):
    xb = xc.astype(BF16)
    r = _sigmoid(_dot(xb, wa_ref[...]) + ba)
    ig = _sigmoid(_dot(xb, wx_ref[...]) + bx)
    sp = _softplus(-lam)
    la = -LRU_C * r * sp
    a = jnp.exp(la)
    return r, ig, sp, a, _one_minus_sq(la, a)


def _lru_fwd(xc, par_f, par_b, h0_f, h0_b, name, comms=()):
    t = xc.shape[0]
    tm = _tile(t, True)
    nt = t // tm

    def one(x_ref, prm, h0_ref, a_s, b_s, c_s, reverse):
        wa_ref, wx_ref, ba_ref, bx_ref, lam_ref = prm

        @pl.when(pl.program_id(0) == 0)
        def _():
            c_s[...] = jnp.broadcast_to(h0_ref[...], c_s.shape)
        xv = x_ref[...]
        _, ig, _, a, q = _lru_gates(xv, wa_ref, wx_ref, ba_ref[...], bx_ref[...], lam_ref[...])
        _to_lane_blocks(a_s, a)
        _to_lane_blocks(b_s, jnp.sqrt(q) * (ig * xv))
        _group_scan(a_s, b_s, reverse)

    def body(xf_ref, xb_ref, *refs):
        prm_f, prm_b = refs[0:5], refs[5:10]
        h0f_ref, h0b_ref, hf_ref, hb_ref = refs[10:14]
        af_s, bf_s, cf_s, ab_s, bb_s, cb_s = refs[14:]
        one(xf_ref, prm_f, h0f_ref, af_s, bf_s, cf_s, False)
        one(xb_ref, prm_b, h0b_ref, ab_s, bb_s, cb_s, True)
        cf_s[...], cb_s[...] = _carry_scans([(af_s, bf_s, hf_ref, cf_s[...], False),
                                             (ab_s, bb_s, hb_ref, cb_s[...], True)])

    vec = _full((1, LRU_W))
    mat = _full((LRU_W, LRU_W))
    fw = pl.BlockSpec((tm, LRU_W), lambda i: (i, 0))
    bw = pl.BlockSpec((tm, LRU_W), lambda i: (nt - 1 - i, 0))
    tile_s = [_scan_scratch(tm, LRU_W), _scan_scratch(tm, LRU_W), pltpu.VMEM((SUBLANES, LRU_W), F32)]
    (hf, hb), couts = _call(
        body, name=name, grid=(nt,),
        in_specs=[fw, bw] + [mat, mat, vec, vec, vec] * 2 + [vec, vec],
        out_specs=[pl.BlockSpec((tm, LRU_W), lambda i: (i, 0)), pl.BlockSpec((tm, LRU_W), lambda i: (nt - 1 - i, 0))],
        out_shape=[_sds((t, LRU_W))] * 2, scratch_shapes=tile_s + tile_s, sem=("arbitrary",),
        args=(xc, xc, *par_f, *par_b, h0_f, h0_b), comms=comms)
    return ((hf, hb), couts) if comms else (hf, hb)


def _lru_bwd(xc, par_f, par_b, h_f, h_b, h0_f, h0_b, dh_f, dh_b, name, comms=()):
    t = xc.shape[0]
    tm = _tile(t, True)
    nt = t // tm
    n8 = tm // SUBLANES
    last8 = t // SUBLANES - 1
    tile_f = lambda w: pl.BlockSpec((tm, w), lambda i: (nt - 1 - i, 0))
    tile_b = lambda w: pl.BlockSpec((tm, w), lambda i: (i, 0))
    halo_f = pl.BlockSpec((SUBLANES, LRU_W), lambda i: (jnp.maximum((nt - 1 - i) * n8 - 1, 0), 0))
    halo_b = pl.BlockSpec((SUBLANES, LRU_W), lambda i: (jnp.minimum((i + 1) * n8, last8), 0))

    def one(refs_in, refs_out, refs_scr, reverse):
        x_ref, wa_ref, wx_ref, ba_ref, bx_ref, lam_ref, h_ref, halo_ref, h0_ref, dh_ref = refs_in
        dx_ref, dpre_ref, dba_ref, dbx_ref, dlam_ref, dh0_ref = refs_out
        a_s, b_s, l_s, c_s, e_s = refs_scr
        i = pl.program_id(0)

        @pl.when(i == 0)
        def _():
            c_s[...] = jnp.zeros_like(c_s)
            e_s[...] = jnp.zeros_like(e_s)
            dba_ref[...] = jnp.zeros_like(dba_ref)
            dbx_ref[...] = jnp.zeros_like(dbx_ref)
            dlam_ref[...] = jnp.zeros_like(dlam_ref)
        xv = x_ref[...]
        lam = lam_ref[...]
        r, ig, sp, a, q = _lru_gates(xv, wa_ref, wx_ref, ba_ref[...], bx_ref[...], lam)
        rs = lax.rsqrt(q)
        hv = h_ref[...]
        rowi = lax.broadcasted_iota(jnp.int32, (tm, LRU_W), 0)
        edge_a = jnp.broadcast_to(e_s[0:1, :], (tm, LRU_W))
        h0b = jnp.broadcast_to(h0_ref[...], (tm, LRU_W))
        if reverse:
            a_sh = jnp.where(rowi == 0, edge_a, pltpu.roll(a, 1, 0))
            hin_edge = jnp.where(i == nt - 1, h0b, jnp.broadcast_to(halo_ref[0:1, :], (tm, LRU_W)))
            h_in = jnp.where(rowi == tm - 1, hin_edge, pltpu.roll(hv, tm - 1, 0))
        else:
            a_sh = jnp.where(rowi == tm - 1, edge_a, pltpu.roll(a, tm - 1, 0))
            hin_edge = jnp.where(i == nt - 1, h0b, jnp.broadcast_to(halo_ref[SUBLANES - 1:SUBLANES, :], (tm, LRU_W)))
            h_in = jnp.where(rowi == 0, hin_edge, pltpu.roll(hv, 1, 0))
        _to_lane_blocks(a_s, a_sh)
        _to_lane_blocks(b_s, dh_ref[...])
        _group_scan(a_s, b_s, not reverse)
        e_s[...] = jnp.broadcast_to(a[tm - 1:tm, :] if reverse else a[0:1, :], e_s.shape)
        return lam, r, ig, sp, a, q * rs, rs, h_in

    def post(vals, refs_in, refs_out, refs_scr, reverse):
        lam, r, ig, sp, a, mult, rs, h_in = vals
        xv = refs_in[0][...]
        wa_ref, wx_ref = refs_in[1:3]
        dx_ref, dpre_ref, dba_ref, dbx_ref, dlam_ref, dh0_ref = refs_out
        l_s = refs_scr[2]
        i = pl.program_id(0)
        lmb = l_s[...]
        da = lmb * h_in
        ixc = ig * xv
        dmult = lmb * ixc
        dixc = lmb * mult
        dla = da * a - dmult * (a * a) * rs
        dpr = dla * (-LRU_C * sp) * r * (1.0 - r)
        dpi = dixc * xv * ig * (1.0 - ig)
        dprb = dpr.astype(BF16)
        dpib = dpi.astype(BF16)
        dpre_ref[:, 0:LRU_W] = dprb
        dpre_ref[:, LRU_W:2 * LRU_W] = dpib
        dx_ref[...] = dixc * ig + _dot_nt(dprb, wa_ref[...]) + _dot_nt(dpib, wx_ref[...])
        dba_ref[...] += _sum0(dpr)
        dbx_ref[...] += _sum0(dpi)
        dlam_ref[...] += _sum0(dla * (-LRU_C * r)) * (-_sigmoid(-lam))

        @pl.when(i == nt - 1)
        def _():
            al0 = a * lmb
            dh0_ref[...] = al0[tm - 1:tm, :] if reverse else al0[0:1, :]

    def body(*refs):
        jobs = ((refs[0:10], refs[20:26], refs[32:37], False), (refs[10:20], refs[26:32], refs[37:42], True))
        vals = [one(*job) for job in jobs]
        carries = _carry_scans([(scr[0], scr[1], scr[2], scr[3][...], not rev) for _, _, scr, rev in jobs])
        for (_, _, scr, _), carry in zip(jobs, carries):
            scr[3][...] = carry
        for v, job in zip(vals, jobs):
            post(v, *job)

    vec = _full((1, LRU_W))
    mat = _full((LRU_W, LRU_W))

    def in_specs(tile, halo):
        return [tile(LRU_W), mat, mat, vec, vec, vec, tile(LRU_W), halo, vec, tile(LRU_W)]

    def out_specs(tile):
        return [tile(LRU_W), tile(2 * LRU_W), vec, vec, vec, vec]

    out_one = [_sds((t, LRU_W)), _sds((t, 2 * LRU_W), BF16)] + [_sds((1, LRU_W))] * 4
    scr_one = ([_scan_scratch(tm, LRU_W)] * 2 + [pltpu.VMEM((tm, LRU_W), F32)]
               + [pltpu.VMEM((SUBLANES, LRU_W), F32)] * 2)
    outs, couts = _call(
        body, name=name, grid=(nt,), in_specs=in_specs(tile_f, halo_f) + in_specs(tile_b, halo_b),
        out_specs=out_specs(tile_f) + out_specs(tile_b), out_shape=out_one + out_one,
        scratch_shapes=scr_one + scr_one, sem=("arbitrary",),
        args=(xc, *par_f, h_f, h_f, h0_f, dh_f, xc, *par_b, h_b, h_b, h0_b, dh_b), comms=comms)
    return (tuple(outs[0:6]), tuple(outs[6:12])), couts


def _decay_tables(lg, reverse):
    ci = lax.broadcasted_iota(jnp.int32, (CHUNK, CHUNK), 0).astype(F32)
    mi = lax.broadcasted_iota(jnp.int32, (CHUNK, CHUNK), 1).astype(F32)
    rel = (mi - ci) if reverse else (ci - mi)
    relc = jnp.maximum(rel, 0.0)
    lg_c = jnp.concatenate([lg] * (CHUNK // LANES), axis=1)
    dm = jnp.where(rel >= 0, jnp.exp(lg_c * relc), 0.0)
    cd = lax.broadcasted_iota(jnp.int32, (CHUNK, DH), 0).astype(F32)
    pq, ps = (CHUNK - cd, cd) if reverse else (cd + 1.0, CHUNK - 1.0 - cd)
    return relc, dm, jnp.exp(lg * pq), jnp.exp(lg * ps), jnp.exp(lg * float(CHUNK)), pq, ps


def _ret_fwd(proj, lgv, s0f, s0b, comms=()):
    t = proj.shape[0]
    n = t // CHUNK

    def one(q, k, v, lg, s_s, hh, o_ref, sp_ref, reverse):
        _, dm, wq, ws, g, _, _ = _decay_tables(lg, reverse)
        vb = v.astype(BF16)
        p = _dot_nt(q.astype(BF16), k.astype(BF16)) * dm
        s = s_s[hh]
        sp_ref[hh, 0] = s
        o_ref[:, DH * hh:DH * (hh + 1)] = _dot(p.astype(BF16), vb) + _dot((q * wq).astype(BF16), s.astype(BF16))
        s_s[hh] = g * s + _dot_tn((k * ws).astype(BF16), vb)

    def body(qf, kf, vf, qb, kb, vb, lg_ref, s0f_ref, s0b_ref, of_ref, ob_ref, spf_ref, spb_ref, sf_s, sb_s):
        @pl.when(pl.program_id(0) == 0)
        def _():
            sf_s[...] = s0f_ref[...]
            sb_s[...] = s0b_ref[...]
        for hh in range(HEADS):
            sl = slice(DH * hh, DH * (hh + 1))
            one(qf[:, sl].astype(F32), kf[:, sl].astype(F32), vf[:, sl], lg_ref[hh, 0:1, :], sf_s, hh, of_ref, spf_ref,
                False)
            one(qb[:, sl].astype(F32), kb[:, sl].astype(F32), vb[:, sl], lg_ref[hh, 1:2, :], sb_s, hh, ob_ref, spb_ref,
                True)

    blk = (CHUNK, RET_W)
    fw = [pl.BlockSpec(blk, lambda i, o=o: (i, o)) for o in range(3)]
    bw = [pl.BlockSpec(blk, lambda i, o=o: (n - 1 - i, o)) for o in range(3)]
    st = _full((HEADS, DH, DH))
    return _call(body, name="ret_fwd", grid=(n,),
                 in_specs=fw + bw + [_full((HEADS, 2, LANES)), st, st],
                 out_specs=[pl.BlockSpec(blk, lambda i: (i, 0)), pl.BlockSpec(blk, lambda i: (n - 1 - i, 0)),
                            pl.BlockSpec((HEADS, 1, DH, DH), lambda i: (0, i, 0, 0)),
                            pl.BlockSpec((HEADS, 1, DH, DH), lambda i: (0, n - 1 - i, 0, 0))],
                 out_shape=[_sds((t, RET_W)), _sds((t, RET_W)), _sds((HEADS, n, DH, DH)), _sds((HEADS, n, DH, DH))],
                 scratch_shapes=[pltpu.VMEM((HEADS, DH, DH), F32), pltpu.VMEM((HEADS, DH, DH), F32)],
                 sem=("arbitrary",), args=(proj, proj, proj, proj, proj, proj, lgv, s0f, s0b), comms=comms)


def _ret_bwd(proj, lgv, sgv, spf, spb, do, comms=()):
    t = proj.shape[0]
    n = t // CHUNK

    def one(q_ref, k_ref, v_ref, lg_ref, s_ref, do_ref, dq_ref, dk_ref, dv_ref, ds_s, acc_s, reverse):
        d = 1 if reverse else 0
        for hh in range(HEADS):
            sl = slice(DH * hh, DH * (hh + 1))
            relc, dm, wq, ws, g, pq, ps = _decay_tables(lg_ref[hh, d:d + 1, :], reverse)
            qb, kb, vb = q_ref[:, sl], k_ref[:, sl], v_ref[:, sl]
            q, k = qb.astype(F32), kb.astype(F32)
            p = _dot_nt(qb, kb) * dm
            s = s_ref[hh, 0]
            dob = do_ref[:, sl].astype(BF16)
            dsn = ds_s[hh]
            dsb = dsn.astype(BF16)
            dv_ref[:, sl] = (_dot_tn(p.astype(BF16), dob) + _dot((k * ws).astype(BF16), dsb)).astype(BF16)
            dp = _dot_nt(dob, vb)
            dab = (dp * dm).astype(BF16)
            xq = _dot_nt(dob, s.astype(BF16))
            yk = _dot_nt(vb, dsb)
            dq_ref[:, sl] = (_dot(dab, kb) + xq * wq).astype(BF16)
            dk_ref[:, sl] = (_dot_tn(dab, qb) + yk * ws).astype(BF16)
            ds_s[hh] = g * dsn + _dot_tn((q * wq).astype(BF16), dob)
            s_mask = _sum0(dp * p * relc)
            part = (sum(s_mask[:, LANES * u:LANES * (u + 1)] for u in range(CHUNK // LANES))
                    + _sum0(xq * q * wq * pq) + _sum0(yk * k * ws * ps) + _sum0(dsn * s) * g * float(CHUNK))
            acc_s[hh] += jnp.broadcast_to(part, (SUBLANES, LANES))

    def body(qf, kf, vf, qb, kb, vb, lg_ref, sg_ref, sf_ref, sb_ref, dof_ref, dob_ref,
             dqf, dkf, dvf, dqb, dkb, dvb, ds0f_ref, ds0b_ref, drdf_ref, drdb_ref, dsf_s, dsb_s, accf_s, accb_s):
        i = pl.program_id(0)

        @pl.when(i == 0)
        def _():
            for r in (dsf_s, dsb_s, accf_s, accb_s):
                r[...] = jnp.zeros_like(r)
        one(qf, kf, vf, lg_ref, sf_ref, dof_ref, dqf, dkf, dvf, dsf_s, accf_s, False)
        one(qb, kb, vb, lg_ref, sb_ref, dob_ref, dqb, dkb, dvb, dsb_s, accb_s, True)

        @pl.when(i == n - 1)
        def _():
            ds0f_ref[...] = dsf_s[...]
            ds0b_ref[...] = dsb_s[...]
            for d, (acc_s, drd_ref) in enumerate(((accf_s, drdf_ref), (accb_s, drdb_ref))):
                for hh in range(HEADS):
                    tot = jnp.sum(acc_s[hh, 0:1, :], axis=1, keepdims=True)
                    drd_ref[hh] = jnp.broadcast_to(tot, (SUBLANES, LANES)) * sg_ref[hh, d:d + 1, :]

    blk = (CHUNK, RET_W)
    fw = lambda o: pl.BlockSpec(blk, lambda i, o=o: (n - 1 - i, o))
    bw = lambda o: pl.BlockSpec(blk, lambda i, o=o: (i, o))
    lane = _full((HEADS, 2, LANES))
    st = _full((HEADS, DH, DH))
    rd = _full((HEADS, SUBLANES, LANES))
    outs, couts = _call(
        body, name="ret_bwd", grid=(n,),
        in_specs=[fw(0), fw(1), fw(2), bw(0), bw(1), bw(2), lane, lane,
                  pl.BlockSpec((HEADS, 1, DH, DH), lambda i: (0, n - 1 - i, 0, 0)),
                  pl.BlockSpec((HEADS, 1, DH, DH), lambda i: (0, i, 0, 0)), fw(0), bw(0)],
        out_specs=[fw(0), fw(0), fw(0), bw(0), bw(0), bw(0), st, st, rd, rd],
        out_shape=[_sds((t, RET_W), BF16)] * 6 + [_sds((HEADS, DH, DH))] * 2 + [_sds((HEADS, SUBLANES, LANES))] * 2,
        scratch_shapes=[pltpu.VMEM((HEADS, DH, DH), F32)] * 2 + [pltpu.VMEM((HEADS, SUBLANES, LANES), F32)] * 2,
        sem=("arbitrary",), args=(proj, proj, proj, proj, proj, proj, lgv, sgv, spf, spb, do, do), comms=comms)
    dqf, dkf, dvf, dqb, dkb, dvb, ds0f, ds0b, drdf, drdb = outs
    return ((dqf, dkf, dvf, ds0f, drdf), (dqb, dkb, dvb, ds0b, drdb)), couts


def _ctx_weights(lg, l_len, reverse):
    pos = lax.broadcasted_iota(jnp.int32, (l_len, DH), 0).astype(F32)
    steps = pos if reverse else (l_len - 1.0 - pos)
    return jnp.exp(lg * steps), steps


def _ctx_state_fwd(projc, lgv):
    l_len = projc.shape[0]

    def body(k_ref, v_ref, lg_ref, sf_ref, sb_ref):
        k = k_ref[...]
        vb = v_ref[...].astype(BF16)
        for d, o_ref in ((0, sf_ref), (1, sb_ref)):
            w, _ = _ctx_weights(lg_ref[0, d:d + 1, :], l_len, d == 1)
            o_ref[0] = _dot_tn((k * w).astype(BF16), vb)

    st = pl.BlockSpec((1, DH, DH), lambda h: (h, 0, 0))
    return _pc(body, name="ctx_state_fwd", grid=(HEADS,),
               in_specs=[pl.BlockSpec((l_len, DH), lambda h: (0, HEADS + h)),
                         pl.BlockSpec((l_len, DH), lambda h: (0, 2 * HEADS + h)),
                         pl.BlockSpec((1, 2, LANES), lambda h: (h, 0, 0))],
               out_specs=[st, st], out_shape=[_sds((HEADS, DH, DH))] * 2,
               compiler_params=_params("arbitrary"))(projc, projc, lgv)


def _ctx_state_bwd(projc, lgv, sgv, dsf, dsb):
    l_len = projc.shape[0]

    def body(k_ref, v_ref, lg_ref, sg_ref, dsf_ref, dsb_ref, dk_ref, dv_ref, drd_ref):
        k = k_ref[...]
        vb = v_ref[...].astype(BF16)
        dk = jnp.zeros((l_len, DH), F32)
        dv = jnp.zeros((l_len, DH), F32)
        rows = []
        for d, ds_ref in ((0, dsf_ref), (1, dsb_ref)):
            w, steps = _ctx_weights(lg_ref[0, d:d + 1, :], l_len, d == 1)
            dsb16 = ds_ref[0].astype(BF16)
            dkw = _dot_nt(vb, dsb16)
            dk = dk + dkw * w
            dv = dv + _dot((k * w).astype(BF16), dsb16)
            tot = jnp.sum(_sum0(dkw * k * w * steps), axis=1, keepdims=True)
            rows.append(jnp.broadcast_to(tot, (1, LANES)) * sg_ref[0, d:d + 1, :])
        dk_ref[...] = dk.astype(BF16)
        dv_ref[...] = dv.astype(BF16)
        rid = lax.broadcasted_iota(jnp.int32, (SUBLANES, LANES), 0)
        drd_ref[0] = jnp.where(rid == 0, rows[0], jnp.where(rid == 1, rows[1], 0.0))

    st = pl.BlockSpec((1, DH, DH), lambda h: (h, 0, 0))
    lane = pl.BlockSpec((1, 2, LANES), lambda h: (h, 0, 0))
    hc = pl.BlockSpec((l_len, DH), lambda h: (0, h))
    return _pc(body, name="ctx_state_bwd", grid=(HEADS,),
               in_specs=[pl.BlockSpec((l_len, DH), lambda h: (0, HEADS + h)),
                         pl.BlockSpec((l_len, DH), lambda h: (0, 2 * HEADS + h)), lane, lane, st, st],
               out_specs=[hc, hc, pl.BlockSpec((1, SUBLANES, LANES), lambda h: (h, 0, 0))],
               out_shape=[_sds((l_len, RET_W), BF16), _sds((l_len, RET_W), BF16), _sds((HEADS, SUBLANES, LANES))],
               compiler_params=_params("arbitrary"))(projc, projc, lgv, sgv, dsf, dsb)


G_BLOCK = (3 * RET_W) // RET_W
GATE_BLOCK = (4 * RET_W + LRU_W) // LRU_W


def _head_norm(y):
    yc = y - jnp.mean(y, axis=-1, keepdims=True)
    rs = lax.rsqrt(jnp.mean(yc * yc, axis=-1, keepdims=True) + EPS)
    return yc * rs, rs


def _gelu_parts(z):
    th = jnp.tanh(GELU_K * (z + GELU_C * z * z * z))
    return 0.5 * z * (1.0 + th), th


def _mix_fwd(o_f, o_b, proj, hf, hb, w_out, x, g1, comms):
    t = x.shape[0]
    tm = _tile(t, True)

    def body(of_ref, ob_ref, g_ref, gt_ref, hf_ref, hb_ref, w_ref, x_ref, g1_ref, x1_ref, cat_ref):
        o = of_ref[...] + ob_ref[...]
        g = g_ref[...].astype(F32)
        for hh in range(HEADS):
            sl = slice(DH * hh, DH * (hh + 1))
            nrm, _ = _head_norm(o[:, sl])
            gh = g[:, sl]
            cat_ref[:, sl] = (gh * _sigmoid(gh) * nrm).astype(BF16)
        gel, _ = _gelu_parts(gt_ref[...].astype(F32))
        cat_ref[:, RET_W:] = ((hf_ref[...] + hb_ref[...]) * gel).astype(BF16)
        x1_ref[...] = x_ref[...] + g1_ref[...] * _dot(cat_ref[...], w_ref[...])

    half = pl.BlockSpec((tm, RET_W), lambda i: (i, 0))
    big = pl.BlockSpec((tm, D_MODEL), lambda i: (i, 0))
    return _call(body, name="mix_fwd", grid=(t // tm,),
                 in_specs=[half, half, pl.BlockSpec((tm, RET_W), lambda i: (i, G_BLOCK)),
                           pl.BlockSpec((tm, LRU_W), lambda i: (i, GATE_BLOCK)), half, half,
                           _full((D_MODEL, D_MODEL)), big, _full((1, D_MODEL))],
                 out_specs=[big, big], out_shape=[_sds((t, D_MODEL)), _sds((t, D_MODEL), BF16)],
                 scratch_shapes=[], sem=("arbitrary",), args=(o_f, o_b, proj, proj, hf, hb, w_out, x, g1),
                 comms=comms)


def _mix_bwd(o_f, o_b, proj, hf, hb, w_out, cat, dx1, g1, comms=()):
    t = dx1.shape[0]
    tm = _tile(t, True)

    def body(of_ref, ob_ref, g_ref, gt_ref, hf_ref, hb_ref, w_ref, cat_ref, dx1_ref, g1_ref,
             do_ref, dhs_ref, dg_ref, dgt_ref, dyb_ref, dg1_ref):
        dx1v = dx1_ref[...]
        y = _dot(cat_ref[...], w_ref[...])

        @pl.when(pl.program_id(0) == 0)
        def _():
            dg1_ref[...] = jnp.zeros_like(dg1_ref)
        dg1_ref[...] += _sum0(dx1v * y)
        dyb = (g1_ref[...] * dx1v).astype(BF16)
        dyb_ref[...] = dyb
        dcat = _dot_nt(dyb, w_ref[...])
        o = of_ref[...] + ob_ref[...]
        g = g_ref[...].astype(F32)
        for hh in range(HEADS):
            sl = slice(DH * hh, DH * (hh + 1))
            nrm, rs = _head_norm(o[:, sl])
            gh = g[:, sl]
            sg = _sigmoid(gh)
            dret = dcat[:, sl]
            dg_ref[:, sl] = (dret * nrm * (sg * (1.0 + gh * (1.0 - sg)))).astype(BF16)
            dn = dret * (gh * sg)
            dyc = rs * (dn - nrm * jnp.mean(dn * nrm, axis=-1, keepdims=True))
            do_ref[:, sl] = (dyc - jnp.mean(dyc, axis=-1, keepdims=True)).astype(BF16)
        z = gt_ref[...].astype(F32)
        gel, th = _gelu_parts(z)
        dlru = dcat[:, RET_W:]
        dhs_ref[...] = dlru * gel
        dgel = 0.5 * (1.0 + th) + 0.5 * z * (1.0 - th * th) * GELU_K * (1.0 + 3.0 * GELU_C * z * z)
        dgt_ref[...] = (dlru * (hf_ref[...] + hb_ref[...]) * dgel).astype(BF16)

    half = pl.BlockSpec((tm, RET_W), lambda i: (i, 0))
    big = pl.BlockSpec((tm, D_MODEL), lambda i: (i, 0))
    return _call(body, name="mix_bwd", grid=(t // tm,),
                 in_specs=[half, half, pl.BlockSpec((tm, RET_W), lambda i: (i, G_BLOCK)),
                           pl.BlockSpec((tm, LRU_W), lambda i: (i, GATE_BLOCK)), half, half,
                           _full((D_MODEL, D_MODEL)), big, big, _full((1, D_MODEL))],
                 out_specs=[half, half, half, half, big, _full((1, D_MODEL))],
                 out_shape=[_sds((t, RET_W), BF16), _sds((t, RET_W)), _sds((t, RET_W), BF16), _sds((t, RET_W), BF16),
                            _sds((t, D_MODEL), BF16), _sds((1, D_MODEL))],
                 scratch_shapes=[], sem=("arbitrary",), args=(o_f, o_b, proj, proj, hf, hb, w_out, cat, dx1, g1),
                 comms=comms)


def _mlp(x1, n2g, sh2, sc2, g2, fg, w1_parts, w2_parts, tgt):
    t = x1.shape[0]
    tm = _tile(t)
    hb_ = MLP_H // N_CHIP
    q_rows = hb_ // 4
    n_cp = 4 * N_DEV

    def body(x1_ref, n2g_ref, sh2_ref, sc2_ref, g2_ref, fg_ref, w1a, w1b, w2a, w2b, tgt_ref,
             dx1_ref, h2b_ref, ab_ref, dub_ref, dmb_ref, dsc_ref, dsh_ref, dg2_ref, dn2_ref, dfg_ref, loss_ref,
             w1_s, w2_s, r_s, sems):
        @pl.when(pl.program_id(0) == 0)
        def _():
            cps = []
            for p, parts in enumerate(((w1a, w2a), (w1b, w2b))):
                for d in range(N_DEV):
                    rows = pl.ds(2 * q_rows * (d % 2) + q_rows * p, q_rows)
                    for src, dst in zip(parts, (w1_s, w2_s)):
                        cps.append(pltpu.make_async_copy(src.at[d], dst.at[d // 2, rows], sems.at[len(cps)]))
            for n, cp in enumerate(cps):
                cp.start(priority=n % 2)
            for r in (dsc_ref, dsh_ref, dg2_ref, dn2_ref, dfg_ref, loss_ref):
                r[...] = jnp.zeros_like(r)
            for cp in cps:
                cp.wait()
        x1v = x1_ref[...]
        n2g, sc2, g2, fg = n2g_ref[...], sc2_ref[...], g2_ref[...], fg_ref[...]
        xh, _ = _rms(x1v)
        h2b = (xh * n2g * (1.0 + sc2) + sh2_ref[...]).astype(BF16)
        h2b_ref[...] = h2b
        m = jnp.zeros((tm, D_MODEL), F32)
        for j in range(N_CHIP):
            sl = slice(hb_ * j, hb_ * (j + 1))
            r = jnp.maximum(_dot(h2b, w1_s[j]), 0.0)
            r_s[:, sl] = r
            ab = (r * r).astype(BF16)
            ab_ref[:, sl] = ab
            m = m + _dot(ab, w2_s[j])
        x2 = x1v + g2 * m
        x2h, r2 = _rms(x2)
        err = x2h * fg - tgt_ref[...]
        loss_ref[...] += _sum0(err * err)
        dout = err * (1.0 / D_MODEL)
        dfg_ref[...] += _sum0(dout * x2h)
        dxh = dout * fg
        dx2 = r2 * (dxh - x2h * jnp.mean(dxh * x2h, axis=-1, keepdims=True))
        dg2_ref[...] += _sum0(dx2 * m)
        dmb = (g2 * dx2).astype(BF16)
        dmb_ref[...] = dmb
        dh2 = jnp.zeros((tm, D_MODEL), F32)
        for j in range(N_CHIP):
            sl = slice(hb_ * j, hb_ * (j + 1))
            dub = (_dot_nt(dmb, w2_s[j]) * (2.0 * r_s[:, sl])).astype(BF16)
            dub_ref[:, sl] = dub
            dh2 = dh2 + _dot_nt(dub, w1_s[j])
        dx, dn2_t, dsh_t, dsc_t = _norm_mod_bwd(x1v, n2g, sc2, dh2)
        dx1_ref[...] = dx2 + dx
        dn2_ref[...] += dn2_t
        dsh_ref[...] += dsh_t
        dsc_ref[...] += dsc_t

        @pl.when(pl.program_id(0) == t // tm - 1)
        def _():
            tot = jnp.sum(loss_ref[...], axis=1, keepdims=True) * (0.5 / D_MODEL)
            loss_ref[...] = jnp.broadcast_to(tot, loss_ref.shape)

    row = _full((1, D_MODEL))
    big = pl.BlockSpec((tm, D_MODEL), lambda i: (i, 0))
    wide = pl.BlockSpec((tm, MLP_H), lambda i: (i, 0))
    return _pc(body, name="mlp", grid=(t // tm,),
               in_specs=[big, row, row, row, row, row, ANY, ANY, ANY, ANY, big],
               out_specs=[big, big, wide, wide, big, row, row, row, row, row, row],
               out_shape=[_sds((t, D_MODEL)), _sds((t, D_MODEL), BF16), _sds((t, MLP_H), BF16), _sds((t, MLP_H), BF16),
                          _sds((t, D_MODEL), BF16)] + [_sds((1, D_MODEL))] * 6,
               scratch_shapes=[pltpu.VMEM((N_CHIP, D_MODEL, hb_), BF16), pltpu.VMEM((N_CHIP, hb_, D_MODEL), BF16),
                               pltpu.VMEM((tm, MLP_H), F32), pltpu.SemaphoreType.DMA((n_cp,))],
               compiler_params=_params("arbitrary"))(x1, n2g, sh2, sc2, g2, fg, *w1_parts, *w2_parts, tgt)


def _tn(a, b, nj, a_blocked, b_blocked, name, extra=None, comms=()):
    t = a.shape[0]
    m = a.shape[1] // (nj if a_blocked else 1)
    n = b.shape[1] // (nj if b_blocked else 1)
    bk = next((b for b in (2048, 1024, 512) if t % b == 0), t)
    nk = t // bk
    a_col = (lambda j: j) if a_blocked else (lambda j: 0)
    b_col = (lambda j: j) if b_blocked else (lambda j: 0)
    in_specs = [pl.BlockSpec((bk, m), lambda j, k: (k, a_col(j))), pl.BlockSpec((bk, n), lambda j, k: (k, b_col(j)))]
    args = [a, b]
    if extra is not None:
        a2, b2 = extra
        t2 = a2.shape[0]
        in_specs += [pl.BlockSpec((t2, m), lambda j, k: (0, a_col(j))),
                     pl.BlockSpec((t2, n), lambda j, k: (0, b_col(j)))]
        args += [a2, b2]

    def body(*refs):
        a_ref, b_ref = refs[0], refs[1]
        o_ref, acc = refs[-2], refs[-1]
        k = pl.program_id(1)

        @pl.when(k == 0)
        def _():
            acc[...] = jnp.zeros_like(acc)
        acc[...] += _dot_tn(a_ref[...].astype(BF16), b_ref[...].astype(BF16))

        @pl.when(k == nk - 1)
        def _():
            if extra is not None:
                acc[...] += _dot_tn(refs[2][...].astype(BF16), refs[3][...].astype(BF16))
            o_ref[0] = acc[...]

    (out,), couts = _call(body, name=name, grid=(nj, nk), in_specs=in_specs,
                          out_specs=[pl.BlockSpec((1, m, n), lambda j, k: (j, 0, 0))], out_shape=[_sds((nj, m, n))],
                          scratch_shapes=[pltpu.VMEM((m, n), F32)], sem=("arbitrary", "arbitrary"), args=args,
                          comms=comms)
    return (out, couts) if comms else out


ROW_LOSS = 0
ROW_DMOD = 1
ROW_DMODC = 7
ROW_N1, ROW_N2, ROW_FG, ROW_CB = 9, 10, 11, 12
ROW_BA, ROW_BX, ROW_LAM = 13, 15, 17
ROW_CW = 20
ROW_RD = 24
SLAB_ROWS = 32
SEG = D_MODEL // 2


def _pack_small(rows, drd, cw2, cb2, lru2, gates):
    n_rows, n_lru = len(rows), len(lru2)

    def body(*refs):
        r = refs[:n_rows]
        drd_f, drd_b, drd_c, cw_a, cw_b, cb_a, cb_b = refs[n_rows:n_rows + 7]
        lru = refs[n_rows + 7:n_rows + 7 + n_lru]
        gf_ref, gb_ref, slab, ga, gx = refs[n_rows + 7 + n_lru:]
        slab[...] = jnp.zeros_like(slab)
        slab[ROW_LOSS:ROW_LOSS + 1, :] = r[0][...]
        for k in range(N_MOD):
            slab[ROW_DMOD + k:ROW_DMOD + k + 1, :] = r[1 + k][...]
        slab[ROW_DMODC:ROW_DMODC + 1, :] = r[7][...]
        slab[ROW_DMODC + 1:ROW_DMODC + 2, :] = r[8][...]
        slab[ROW_N1:ROW_N1 + 1, :] = r[9][...] + r[10][...]
        slab[ROW_N2:ROW_N2 + 1, :] = r[11][...]
        slab[ROW_FG:ROW_FG + 1, :] = r[12][...]
        slab[ROW_CB:ROW_CB + 1, 0:LRU_W] = cb_a[...] + cb_b[...]
        for k, row in enumerate((ROW_BA, ROW_BA + 1, ROW_BX, ROW_BX + 1, ROW_LAM, ROW_LAM + 1)):
            slab[row:row + 1, 0:LRU_W] = lru[2 * k][...] + lru[2 * k + 1][...]
        slab[ROW_CW:ROW_CW + 4, 0:LRU_W] = cw_a[...] + cw_b[...]
        for h in range(HEADS):
            slab[ROW_RD + h:ROW_RD + h + 1, 0:LANES] = drd_f[h, 0:1, :] + drd_c[h, 0:1, :]
            slab[ROW_RD + HEADS + h:ROW_RD + HEADS + h + 1, 0:LANES] = drd_b[h, 0:1, :] + drd_c[h, 1:2, :]
        for d, g_ref in enumerate((gf_ref, gb_ref)):
            for n in range(LRU_BLOCKS):
                blk = slice(LRU_BD * n, LRU_BD * (n + 1))
                ga[blk, LRU_BD * d:LRU_BD * (d + 1)] = g_ref[0, blk, blk].astype(BF16)
                gx[blk, LRU_BD * d:LRU_BD * (d + 1)] = g_ref[1, blk, blk].astype(BF16)

    args = list(rows) + list(drd) + list(cw2) + list(cb2) + list(lru2) + list(gates)
    gate_shape = (LRU_W, 2 * LRU_BD)
    return _pc(body, name="pack_small", in_specs=[_full(a.shape) for a in args],
               out_specs=[_full((SLAB_ROWS, D_MODEL)), _full(gate_shape), _full(gate_shape)],
               out_shape=[_sds((SLAB_ROWS, D_MODEL)), _sds(gate_shape, BF16), _sds(gate_shape, BF16)],
               compiler_params=_params())(*args)


def _adam_math(w, g, m, v):
    mn = ADAM_B1 * m + (1.0 - ADAM_B1) * g
    vn = ADAM_B2 * v + (1.0 - ADAM_B2) * (g * g)
    mh = mn / (1.0 - ADAM_B1 ** ADAM_STEP)
    vh = vn / (1.0 - ADAM_B2 ** ADAM_STEP)
    return -ADAM_LR * (mh / (jnp.sqrt(vh) + ADAM_EPS) + ADAM_WD * w), mn, vn


SMALL_PARAMS = ("b_ada", "norm1_g", "norm2_g", "final_g", "ret_decay", "conv_w", "conv_b", "lru_wa", "lru_ba", "lru_wx",
                "lru_bx", "lru_lambda")


def _finalize_small(chip_idx, slab_all, ga_all, gx_all, wmv):
    n_p = len(SMALL_PARAMS)
    flat = [a for nm in SMALL_PARAMS for a in wmv[nm]]
    ada_n = N_MOD * D_MODEL // N_CHIP

    def body(c_ref, slab_ref, ga_ref, gx_ref, *refs):
        prm = {nm: refs[3 * k:3 * k + 3] for k, nm in enumerate(SMALL_PARAMS)}
        outs = {nm: refs[3 * n_p + 4 * k:3 * n_p + 4 * k + 4] for k, nm in enumerate(SMALL_PARAMS)}
        b128_ref, dmc_ref, loss_ref = refs[3 * n_p + 4 * n_p:]
        chip = c_ref[0]

        def pick(fn):
            acc = fn(0)
            for j in range(1, N_CHIP):
                acc = jnp.where(chip == j, fn(j), acc)
            return acc

        tot = slab_ref[0]
        for d in range(1, N_DEV):
            tot = tot + slab_ref[d]

        def update(nm, g, sl=None, rows=None):
            w_ref, m_ref, v_ref = prm[nm]
            g_ref, d_ref, mo_ref, vo_ref = outs[nm]
            ix = (slice(None) if rows is None else rows, slice(None) if sl is None else sl)
            dl, mn, vn = _adam_math(w_ref[ix], g, m_ref[ix], v_ref[ix])
            g_ref[ix] = g
            d_ref[ix] = dl
            mo_ref[ix] = mn
            vo_ref[ix] = vn

        loss_ref[...] = jnp.broadcast_to(tot[ROW_LOSS:ROW_LOSS + 1, 0:LANES], (SUBLANES, LANES))
        for k in range(N_MOD):
            g = tot[ROW_DMOD + k:ROW_DMOD + k + 1, :]
            if k < 2:
                g = g + tot[ROW_DMODC + k:ROW_DMODC + k + 1, :]
            update("b_ada", g, slice(D_MODEL * k, D_MODEL * (k + 1)))
        update("norm1_g", tot[ROW_N1:ROW_N1 + 1, :])
        update("norm2_g", tot[ROW_N2:ROW_N2 + 1, :])
        update("final_g", tot[ROW_FG:ROW_FG + 1, :])
        update("ret_decay", tot[ROW_RD:ROW_RD + SUBLANES, 0:LANES])
        update("conv_b", tot[ROW_CB:ROW_CB + 1, 0:LRU_W])
        update("conv_w", pick(lambda j: tot[ROW_CW:ROW_CW + 4, LANES * j:LANES * (j + 1)]))
        for nm, row in (("lru_ba", ROW_BA), ("lru_bx", ROW_BX), ("lru_lambda", ROW_LAM)):
            update(nm, pick(lambda j, row=row: tot[row:row + 2, LANES * j:LANES * (j + 1)]))
        for nm, g_all in (("lru_wa", ga_ref), ("lru_wx", gx_ref)):
            for dr in range(2):
                lanes = slice(LRU_BD * dr, LRU_BD * (dr + 1))
                g = g_all[0, :, lanes].astype(F32)
                for d in range(1, N_DEV):
                    g = g + g_all[d, :, lanes].astype(F32)
                update(nm, g, rows=slice(LRU_W * dr, LRU_W * (dr + 1)))

        def seg(rows6, s):
            return rows6[s // 2][:, SEG * (s % 2):SEG * (s % 2 + 1)]

        b128_ref[...] = jnp.zeros_like(b128_ref)
        dmc_ref[...] = jnp.zeros_like(dmc_ref)
        zero = jnp.zeros((1, D_MODEL), F32)
        ctx6 = [tot[ROW_DMODC:ROW_DMODC + 1, :], tot[ROW_DMODC + 1:ROW_DMODC + 2, :]] + [zero] * (N_MOD - 2)
        for q in range(ada_n // SEG):
            cols = slice(SEG * q, SEG * (q + 1))
            for d in range(N_DEV):
                rows6 = [slab_ref[d, ROW_DMOD + k:ROW_DMOD + k + 1, :] for k in range(N_MOD)]
                b128_ref[d:d + 1, cols] = pick(lambda j, rows6=rows6: seg(rows6, 3 * j + q))
            c = pick(lambda j: seg(ctx6, 3 * j + q))
            b128_ref[N_DEV:N_DEV + 1, cols] = c
            dmc_ref[0:1, cols] = c

    out_shape = []
    for nm in SMALL_PARAMS:
        out_shape += [_sds(wmv[nm][0].shape)] * 4
    out_shape += [_sds((LANES, ada_n)), _sds((SUBLANES, ada_n)), _sds((SUBLANES, LANES))]
    args = [slab_all, ga_all, gx_all] + flat
    grid_spec = pltpu.PrefetchScalarGridSpec(
        num_scalar_prefetch=1, grid=(1,), in_specs=[_full(a.shape) for a in args],
        out_specs=[_full(s.shape) for s in out_shape])
    outs = _pc(body, name="finalize_small", grid_spec=grid_spec, out_shape=out_shape,
               compiler_params=_params("arbitrary"))(chip_idx, *args)
    res = {nm: tuple(outs[4 * k:4 * k + 4]) for k, nm in enumerate(SMALL_PARAMS)}
    return res, outs[4 * n_p], outs[4 * n_p + 1], outs[4 * n_p + 2]


def _block_diag(w):
    eye = jnp.eye(LRU_BLOCKS, dtype=F32)
    return (w[:, :, None, :] * eye[:, None, :, None]).reshape(LRU_W, LRU_W).astype(BF16)


def _lane_rep(v8):
    return jnp.broadcast_to(v8.reshape(SUBLANES, 1), (SUBLANES, LANES))


def kernel(x, c, ctx, c_ctx, w_ada, b_ada, norm1_g, norm2_g, w_in, ret_decay, conv_w, conv_b, lru_wa, lru_ba, lru_wx, lru_bx, lru_lambda, w_out, w_mlp1, w_mlp2, final_g, loss_target, m_c_ctx, m_w_ada, m_b_ada, m_norm1_g, m_norm2_g, m_w_in, m_ret_decay, m_conv_w, m_conv_b, m_lru_wa, m_lru_ba, m_lru_wx, m_lru_bx, m_lru_lambda, m_w_out, m_w_mlp1, m_w_mlp2, m_final_g, v_c_ctx, v_w_ada, v_b_ada, v_norm1_g, v_norm2_g, v_w_in, v_ret_decay, v_conv_w, v_conv_b, v_lru_wa, v_lru_ba, v_lru_wx, v_lru_bx, v_lru_lambda, v_w_out, v_w_mlp1, v_w_mlp2, v_final_g):
    ax, ay, ac = lax.axis_index("x"), lax.axis_index("y"), lax.axis_index("c")
    chip = 2 * ax + ay
    dev = 4 * ax + 2 * ay + ac
    c_idx = jnp.stack([ac, chip]).astype(jnp.int32)
    j_idx = chip.reshape(1).astype(jnp.int32)

    xt = x[0]
    t_len = xt.shape[0]
    ctxt = ctx[0]
    l_len = ctxt.shape[0]
    tgt = loss_target[0]
    ada_n = w_ada.shape[2]

    def my_half(w2d):
        r = w2d.shape[0] // 2
        return lax.dynamic_slice_in_dim(w2d, ac * r, r, axis=0).astype(BF16)

    pad8 = lambda a: jnp.pad(a, ((0, SUBLANES - a.shape[0]), (0, 0)))
    small = jnp.concatenate([pad8(conv_w[0]), pad8(lru_ba[0]), pad8(lru_bx[0]), pad8(lru_lambda[0])], axis=0)
    b_shard = lax.dynamic_slice_in_dim(b_ada, chip * ada_n, ada_n, axis=1)
    gw_in, _, small_all, a16, mod_parts, lgv, sgv = _head(
        my_half(w_in[0]), pad8(c), small, w_ada[0], b_shard, c_ctx, ret_decay[0])
    w4 = gw_in.reshape(N_CHIP, D_MODEL, IN_COLS // N_CHIP)

    mod_all = mod_parts[0::2].transpose(1, 0, 2).reshape(16, N_CHIP * ada_n)
    mod_me = lax.dynamic_slice_in_dim(mod_all, dev, 1, axis=0)
    sh1, sc1, g1, sh2, sc2, g2 = [mod_me[:, D_MODEL * k:D_MODEL * (k + 1)] for k in range(N_MOD)]
    csh1, csc1 = mod_all[8:9, 0:D_MODEL], mod_all[8:9, D_MODEL:2 * D_MODEL]

    cos2, sin2 = _rotary_tables(t_len)
    cos_c, sin_c = jnp.ones((l_len, DH), F32), jnp.zeros((l_len, DH), F32)
    n1g, n2g = norm1_g, norm2_g
    fg = final_g.reshape(1, D_MODEL)

    small_full = small_all[0::2].transpose(1, 0, 2).reshape(4 * SUBLANES, LRU_W)
    cw = small_full[0:4]
    cb = conv_b
    ba_f, ba_b = small_full[8:9], small_full[9:10]
    bx_f, bx_b = small_full[16:17], small_full[17:18]
    lam_f, lam_b = small_full[24:25], small_full[25:26]
    wa_f, wa_b = _block_diag(lru_wa[0, 0]), _block_diag(lru_wa[0, 1])
    wx_f, wx_b = _block_diag(lru_wx[0, 0]), _block_diag(lru_wx[0, 1])
    zero_h = jnp.zeros((1, LRU_W), F32)

    projc, xrc, hcb16 = _inproj_fwd(ctxt, n1g, csh1, csc1, w4, cos_c, sin_c, "inproj_fwd_ctx")
    s_f, s_b = _ctx_state_fwd(projc, lgv)
    xcc = _conv_fwd(xrc, cw, cb, "conv_fwd_ctx")
    par_f, par_b = (wa_f, wx_f, ba_f, bx_f, lam_f), (wa_b, wx_b, ba_b, bx_b, lam_b)
    hcf, hcbk = _lru_fwd(xcc, par_f, par_b, zero_h, zero_h, "lru_fwd_ctx")
    lru_sf, lru_sb = hcf[l_len - 1:l_len], hcbk[0:1]

    h1, h2 = my_half(w_mlp1[0]), my_half(w_mlp2[0])
    q = h1.shape[0] // 2
    (proj, xrl, hb16), ((gw_1a,),) = _inproj_fwd(xt, n1g, sh1, sc1, w4, cos2, sin2, "inproj_fwd",
                                           comms=(_AllGather([h1[:q]]),))
    (o_f, o_b, spf, spb), ((gw_1b, gw_out),) = _ret_fwd(proj, lgv, s_f, s_b,
                                                       comms=(_AllGather([h1[q:], my_half(w_out[0])]),))
    xcl = _conv_fwd(xrl, cw, cb, "conv_fwd")
    (hf, hbk), ((gw_2a,),) = _lru_fwd(xcl, par_f, par_b, lru_sf, lru_sb, "lru_fwd", comms=(_AllGather([h2[:q]]),))
    wo = gw_out.reshape(D_MODEL, D_MODEL)
    (x1, cat), ((gw_2b,),) = _mix_fwd(o_f, o_b, proj, hf, hbk, wo, xt, g1, comms=(_AllGather([h2[q:]]),))

    (dx1, h2b, ab, dub, dmb, dsc2, dsh2, dg2, dn2g, dfg, lossv) = _mlp(
        x1, n2g, sh2, sc2, g2, fg, (gw_1a, gw_1b), (gw_2a, gw_2b), tgt)
    gw_mlp1 = _tn(h2b, dub, N_CHIP, False, True, "grad_w_mlp1")
    b_1 = gw_mlp1.reshape(N_DEV, D_MODEL // 2, MLP_H // N_CHIP)
    gw_mlp2, ((r_1,),) = _tn(ab, dmb, N_CHIP, True, False, "grad_w_mlp2", comms=(_pair_exchange([b_1]),))

    half = D_MODEL // 4
    top, bot = (0, half), (half, half)
    b_2 = gw_mlp2.reshape(N_DEV, MLP_H // N_DEV, D_MODEL)
    p_1, pb_1 = _pair_add(b_1, r_1, c_idx, "rs_pair_add_w_mlp1")
    (do, dhs, dg, dgate, dyb, dg1), ((q_1a,), (r_2,)) = _mix_bwd(
        o_f, o_b, proj, hf, hbk, wo, cat, dx1, g1, comms=(_chip_exchange([pb_1], top), _pair_exchange([b_2])))
    gw_o = _tn(cat, dyb, 1, False, False, "grad_w_out")
    b_o = gw_o.reshape(N_DEV, D_MODEL // N_DEV, D_MODEL)
    p_2, pb_2 = _pair_add(b_2, r_2, c_idx, "rs_pair_add_w_mlp2")

    ((dq_f, dk_f, dv_f, ds_f, drd_f), (dq_b, dk_b, dv_b, ds_b, drd_b)), ((q_1b,), (q_2a,), (r_o,)) = _ret_bwd(
        proj, lgv, sgv, spf, spb, do,
        comms=(_chip_exchange([pb_1], bot), _chip_exchange([pb_2], top), _pair_exchange([b_o])))
    p_o, pb_o = _pair_add(b_o, r_o, c_idx, "rs_pair_add_w_out")
    h_1 = _chip_add(p_1, (q_1a, q_1b), c_idx, "rs_chip_add_w_mlp1")

    ((dxc_f, dpre_f, dba_f, dbx_f, dlam_f, dh0_f), (dxc_b, dpre_b, dba_b, dbx_b, dlam_b, dh0_b)), (
        (q_2b,), (q_o,), (f_1,)) = _lru_bwd(
        xcl, par_f, par_b, hf, hbk, lru_sf, lru_sb, dhs, dhs, "lru_bwd",
        comms=(_chip_exchange([pb_2], bot), _chip_exchange([pb_o]), _pair_gather([h_1])))
    h_2 = _chip_add(p_2, (q_2a, q_2b), c_idx, "rs_chip_add_w_mlp2")
    h_o = _chip_add(p_o, (q_o,), c_idx, "rs_chip_add_w_out")
    dxr, dcw, dcb = _conv_bwd(dxc_f, dxc_b, xrl, cw, "conv_bwd")
    grad_x, dpb, dn1g, dsh1, dsc1 = _inproj_bwd(
        xt, n1g, sh1, sc1, w4, cos2, sin2, [dq_f, dq_b, dk_f, dk_b, dv_f, dv_b, dg, dxr, dgate], dx1, "inproj_bwd")

    dkc, dvc, drd_c = _ctx_state_bwd(projc, lgv, sgv, ds_f, ds_b)
    zc = jnp.zeros((l_len, LRU_W), F32)
    dhc_f = lax.dynamic_update_slice(zc, dh0_f, (l_len - 1, 0))
    dhc_b = lax.dynamic_update_slice(zc, dh0_b, (0, 0))
    ((dxcc_f, dprec_f, dbac_f, dbxc_f, dlamc_f, _), (dxcc_b, dprec_b, dbac_b, dbxc_b, dlamc_b, _)), _ = _lru_bwd(
        xcc, par_f, par_b, hcf, hcbk, zero_h, zero_h, dhc_f, dhc_b, "lru_bwd_ctx")
    dxrc, dcw_c, dcb_c = _conv_bwd(dxcc_f, dxcc_b, xrc, cw, "conv_bwd_ctx")
    zr = jnp.zeros((l_len, RET_W), BF16)
    _, dpbc, dn1g_c, dcsh1, dcsc1 = _inproj_bwd(
        ctxt, n1g, csh1, csc1, w4, cos_c, sin_c, [zr, zr, dkc, zr, dvc, zr, zr, dxrc, zr],
        jnp.zeros((l_len, D_MODEL), F32), "inproj_bwd_ctx")

    gw_i = _tn(hb16, dpb, N_CHIP, False, True, "grad_w_in", extra=(hcb16, dpbc))
    b_i = gw_i.reshape(N_DEV, D_MODEL // 2, IN_COLS // N_CHIP)
    gwa_f, ((r_i,), (f_2,), (f_o,)) = _tn(xcl, dpre_f, 2, False, True, "grad_lru_gates_f", extra=(xcc, dprec_f),
                                          comms=(_pair_exchange([b_i]), _pair_gather([h_2]), _pair_gather([h_o])))
    p_i, pb_i = _pair_add(b_i, r_i, c_idx, "rs_pair_add_w_in")
    gwa_b, ((q_i,),) = _tn(xcl, dpre_b, 2, False, True, "grad_lru_gates_b", extra=(xcc, dprec_b),
                           comms=(_chip_exchange([pb_i]),))
    slab, ga, gx = _pack_small(
        [lossv, dsh1, dsc1, dg1, dsh2, dsc2, dg2, dcsh1, dcsc1, dn1g, dn1g_c, dn2g, dfg],
        (drd_f, drd_b, drd_c), (dcw, dcw_c), (dcb, dcb_c),
        (dba_f, dbac_f, dba_b, dbac_b, dbx_f, dbxc_f, dbx_b, dbxc_b, dlam_f, dlamc_f, dlam_b, dlamc_b),
        (gwa_f, gwa_b))
    (f_i,), (slab_all, ga_all, gx_all) = _run_comms(
        [_pair_gather([_chip_add(p_i, (q_i,), c_idx, "rs_chip_add_w_in")]), _AllGather([slab, ga, gx])],
        "tail_exchanges")
    g_in, g_out, g_1, g_2 = _shard_of(f_i), _shard_of(f_o), _shard_of(f_1), _shard_of(f_2)
    big = {}
    for nm, w, g, m, v in (("w_in", w_in, g_in, m_w_in, v_w_in), ("w_out", w_out, g_out, m_w_out, v_w_out),
                           ("w_mlp1", w_mlp1, g_1, m_w_mlp1, v_w_mlp1), ("w_mlp2", w_mlp2, g_2, m_w_mlp2, v_w_mlp2)):
        go, d_, mn, vn = _adamw(w[0], g, m[0], v[0], "adamw_" + nm)
        big[nm] = (go[None], d_[None], mn[None], vn[None])
    params = {
        "b_ada": (b_ada, m_b_ada, v_b_ada), "norm1_g": (norm1_g, m_norm1_g, v_norm1_g),
        "norm2_g": (norm2_g, m_norm2_g, v_norm2_g), "final_g": (final_g, m_final_g, v_final_g),
        "ret_decay": (ret_decay, m_ret_decay, v_ret_decay), "conv_w": (conv_w, m_conv_w, v_conv_w),
        "conv_b": (conv_b, m_conv_b, v_conv_b), "lru_wa": (lru_wa, m_lru_wa, v_lru_wa),
        "lru_ba": (lru_ba, m_lru_ba, v_lru_ba), "lru_wx": (lru_wx, m_lru_wx, v_lru_wx),
        "lru_bx": (lru_bx, m_lru_bx, v_lru_bx), "lru_lambda": (lru_lambda, m_lru_lambda, v_lru_lambda),
    }
    as2d = {
        "b_ada": lambda a: a, "norm1_g": lambda a: a, "norm2_g": lambda a: a, "conv_b": lambda a: a,
        "final_g": lambda a: a.reshape(1, D_MODEL), "ret_decay": lambda a: _lane_rep(a.reshape(-1)),
        "conv_w": lambda a: a[0], "lru_ba": lambda a: a[0], "lru_bx": lambda a: a[0], "lru_lambda": lambda a: a[0],
        "lru_wa": lambda a: a.reshape(2 * LRU_W, LRU_BD), "lru_wx": lambda a: a.reshape(2 * LRU_W, LRU_BD),
    }
    res, b128, dmc8, loss8 = _finalize_small(
        j_idx, slab_all, ga_all, gx_all, {nm: tuple(as2d[nm](a) for a in params[nm]) for nm in SMALL_PARAMS})
    loss = loss8[0, 0]
    small_out = {}
    for nm in SMALL_PARAMS:
        shp = params[nm][0].shape
        if nm == "ret_decay":
            small_out[nm] = tuple(o[:, 0].reshape(shp) for o in res[nm])
        else:
            small_out[nm] = tuple(o.reshape(shp) for o in res[nm])

    g_ada = _ada_grad(jnp.pad(a16.T, ((0, 0), (0, LANES - 16))), b128)
    g_ada, d_ada, m_ada, v_ada = _adamw(w_ada[0], g_ada, m_w_ada[0], v_w_ada[0], "adamw_w_ada")

    (cparts,) = _all_gather([_cctx_partial(dmc8, w_ada[0])], "gather_cctx")
    g_cc, d_cc, m_cc, v_cc = _cctx_final(cparts, c_ctx, m_c_ctx, v_c_ctx)
    small_out["c_ctx"] = tuple(a.reshape(D_MODEL) for a in (g_cc, d_cc, m_cc, v_cc))
    small_out["w_ada"] = (g_ada[None], d_ada[None], m_ada[None], v_ada[None])
    small_out.update(big)

    order = ["c_ctx", "w_ada", "b_ada", "norm1_g", "norm2_g", "w_in", "ret_decay", "conv_w", "conv_b", "lru_wa", "lru_ba",
             "lru_wx", "lru_bx", "lru_lambda", "w_out", "w_mlp1", "w_mlp2", "final_g"]
    outs = [loss, grad_x[None]]
    for k in range(4):
        outs += [small_out[nm][k] for nm in order]
    return tuple(outs)
```

```python
import math

import jax
import jax.numpy as jnp
from jax import lax
from jax.experimental import pallas as pl
from jax.experimental.pallas import tpu as pltpu

F32 = jnp.float32
BF16 = jnp.bfloat16

D_MODEL = 1024
HEADS = 4
DH = 128
CHUNK = 256
RET_W = HEADS * DH
LRU_W = 512
LRU_BLOCKS = 8
LRU_BD = LRU_W // LRU_BLOCKS
LRU_C = 8.0
IN_COLS = 4 * RET_W + 2 * LRU_W
MLP_H = 4 * D_MODEL
N_MOD = 6
GRID_W = 64
ROPE_BASE = 10000.0
K_SCALE = DH ** -0.5
EPS = 1e-6
GELU_K = math.sqrt(2.0 / math.pi)
GELU_C = 0.044715

ADAM_LR = 0.001
ADAM_B1 = 0.9
ADAM_B2 = 0.999
ADAM_EPS = 1e-08
ADAM_WD = 0.01
ADAM_STEP = 10

N_DEV = 8
N_CHIP = 4
SUBLANES = 8
LANES = 128
VMEM_LIMIT_V7X = 56 * 1024 * 1024
MESH = pl.DeviceIdType.MESH
ANY = pl.BlockSpec(memory_space=pl.ANY)


def _pc(body, **kw):
    return pl.pallas_call(body, **kw)


def _params(*sem):
    return pltpu.CompilerParams(dimension_semantics=sem if sem else None, vmem_limit_bytes=VMEM_LIMIT_V7X)


def _tile(t, big=False):
    if big and t >= 1024:
        return 512
    return 256 if t >= 256 else t


def _sds(shape, dtype=F32):
    return jax.ShapeDtypeStruct(tuple(shape), dtype)


def _full(shape, single=False):
    nd = len(shape)
    kw = dict(pipeline_mode=pl.Buffered(1)) if single else {}
    return pl.BlockSpec(tuple(shape), lambda *_: (0,) * nd, **kw)


def _sigmoid(x):
    return 0.5 * jnp.tanh(0.5 * x) + 0.5


def _log1p_pos(y):
    s = y * (1.0 - y * (0.5 - y * (1.0 / 3.0 - y * (0.25 - y * (0.2 - y / 6.0)))))
    return jnp.where(y < 0.03, s, jnp.log(1.0 + y))


def _softplus(z):
    return jnp.maximum(z, 0.0) + _log1p_pos(jnp.exp(-jnp.abs(z)))


def _one_minus_sq(la, a):
    return -jnp.tanh(la) * (1.0 + a * a)


def _rms(x):
    r = lax.rsqrt(jnp.mean(x * x, axis=-1, keepdims=True) + EPS)
    return x * r, r


def _dot(a, b):
    return jnp.dot(a, b, preferred_element_type=F32)


def _dot_nt(a, b):
    return lax.dot_general(a, b, (((1,), (1,)), ((), ())), preferred_element_type=F32)


def _dot_tn(a, b):
    return lax.dot_general(a, b, (((0,), (0,)), ((), ())), preferred_element_type=F32)


def _sum0(x):
    return jnp.sum(x, axis=0, keepdims=True)


def _norm_mod_bwd(x, g, sc, dh):
    xh, r = _rms(x)
    hn = xh * g
    dhn = dh * (1.0 + sc)
    dxh = dhn * g
    dx = r * (dxh - xh * jnp.mean(dxh * xh, axis=-1, keepdims=True))
    return dx, _sum0(dhn * xh), _sum0(dh), _sum0(dh * hn)


def _dev_index(p):
    return 4 * p[0] + 2 * p[1] + p[2]


def _mesh_pos():
    return lax.axis_index("x"), lax.axis_index("y"), lax.axis_index("c")


class _AllGather:
    def __init__(self, arrs):
        n = len(arrs)
        self.arrays = list(arrs)
        self.out_shapes = [_sds((N_DEV,) + a.shape, a.dtype) for a in arrs]
        self.scratch = ([pltpu.VMEM(a.shape, a.dtype) for a in arrs]
                        + [pltpu.SemaphoreType.DMA((7 * n,)), pltpu.SemaphoreType.DMA((7 * n,)),
                           pltpu.SemaphoreType.DMA((n,))])
        self.aliases = {}

    def _parts(self, ins, outs, scr):
        n = len(self.arrays)
        stage = scr[:n]
        send_sems, recv_sems, local_sems = scr[n:]
        x, y, c = _mesh_pos()
        me, sib = (x, y, c), (x, y, 1 - c)
        chips = [(1 - x, y), (x, 1 - y), (1 - x, 1 - y)]

        def copy(t, k, block, to, own=False):
            dst = outs[t].at[_dev_index(block)]
            return pltpu.make_async_remote_copy(
                src_ref=ins[t] if own else dst, dst_ref=dst,
                send_sem=send_sems.at[7 * t + k], recv_sem=recv_sems.at[7 * t + k],
                device_id=to, device_id_type=MESH)

        first = []
        for t in range(n):
            first.append(copy(t, 0, me, sib, own=True))
            for j, ch in enumerate(chips):
                first.append(copy(t, 1 + j, me, (*ch, c), own=True))
        stage_in = [pltpu.make_async_copy(ins[t], stage[t], local_sems.at[t]) for t in range(n)]
        mine = [pltpu.make_async_copy(stage[t], outs[t].at[_dev_index(me)], local_sems.at[t]) for t in range(n)]
        return n, c, me, sib, chips, copy, first, stage_in, mine

    def start(self, ins, outs, scr):
        n, _, _, _, _, _, first, stage_in, mine = self._parts(ins, outs, scr)
        for cp in stage_in:
            cp.start()
        for cp in first:
            cp.start()
        for t in range(n):
            stage_in[t].wait()
            mine[t].start()

    def relay(self, ins, outs, scr):
        n, c, me, sib, chips, copy, _, _, _ = self._parts(ins, outs, scr)
        for j, ch in enumerate(chips):
            for t in range(n):
                copy(t, 1 + j, (*ch, c), me).wait_recv()
                copy(t, 4 + j, (*ch, c), sib).start()

    def finish(self, ins, outs, scr):
        n, c, me, sib, chips, copy, first, _, mine = self._parts(ins, outs, scr)
        passed = [copy(t, 4 + j, (*ch, c), sib) for j, ch in enumerate(chips) for t in range(n)]
        for t in range(n):
            copy(t, 0, sib, me).wait_recv()
            for j, ch in enumerate(chips):
                copy(t, 4 + j, (*ch, 1 - c), me).wait_recv()
        for cp in first + passed:
            cp.wait_send()
        for cp in mine:
            cp.wait()


class _Exchange:
    def __init__(self, arrays, out_shapes, plan, n_copies, aliases=None):
        self.arrays = list(arrays)
        self.out_shapes = list(out_shapes)
        self.plan = plan
        self.scratch = [pltpu.SemaphoreType.DMA((n_copies,)), pltpu.SemaphoreType.DMA((n_copies,))]
        self.aliases = aliases or {}

    def _copies(self, ins, outs, scr):
        send_sems, recv_sems = scr
        snd, rcv = [], []
        for i, (src, dst, peer, lands) in enumerate(self.plan(ins, outs, _mesh_pos())):
            kw = dict(send_sem=send_sems.at[i], recv_sem=recv_sems.at[i], device_id=peer, device_id_type=MESH)
            snd.append(pltpu.make_async_remote_copy(src_ref=src, dst_ref=dst, **kw))
            rcv.append(pltpu.make_async_remote_copy(src_ref=src, dst_ref=lands, **kw))
        return snd, rcv

    def start(self, ins, outs, scr):
        for cp in self._copies(ins, outs, scr)[0]:
            cp.start()

    def relay(self, ins, outs, scr):
        pass

    def finish(self, ins, outs, scr):
        snd, rcv = self._copies(ins, outs, scr)
        for cp in rcv:
            cp.wait_recv()
        for cp in snd:
            cp.wait_send()


def _pair_exchange(grads):
    n = len(grads)

    def plan(ins, outs, pos):
        x, y, c = pos
        return [(ins[t].at[2 * j + (1 - c)], outs[t].at[j], (x, y, 1 - c), outs[t].at[j])
                for t in range(n) for j in range(N_CHIP)]

    return _Exchange(grads, [_sds((N_CHIP,) + g.shape[1:], g.dtype) for g in grads], plan, N_CHIP * n)


def _chip_exchange(parts, rows=None):
    n = len(parts)

    def plan(ins, outs, pos):
        x, y, c = pos
        chips = [(1 - x, y), (x, 1 - y), (1 - x, 1 - y)]

        def src(t, ch):
            blk = ins[t].at[2 * ch[0] + ch[1]]
            return blk if rows is None else blk.at[pl.ds(rows[0], rows[1])]

        return [(src(t, ch), outs[t].at[k], (*ch, c), outs[t].at[k]) for t in range(n) for k, ch in enumerate(chips)]

    shapes = [_sds((3, p.shape[1] if rows is None else rows[1]) + p.shape[2:], p.dtype) for p in parts]
    return _Exchange(parts, shapes, plan, 3 * n)


def _pair_gather(bufs):
    n = len(bufs)

    def plan(ins, outs, pos):
        x, y, c = pos
        return [(ins[t].at[c], outs[t].at[c], (x, y, 1 - c), outs[t].at[1 - c]) for t in range(n)]

    return _Exchange(bufs, [_sds(b.shape, b.dtype) for b in bufs], plan, n, aliases={t: t for t in range(n)})


def _run_comms(comms, name):
    c_in = [len(cm.arrays) for cm in comms]
    c_out = [len(cm.out_shapes) for cm in comms]
    c_scr = [len(cm.scratch) for cm in comms]
    aliases = {}
    for k, cm in enumerate(comms):
        for a, b in cm.aliases.items():
            aliases[sum(c_in[:k]) + a] = sum(c_out[:k]) + b

    def split(refs, counts):
        out, pos = [], 0
        for cnt in counts:
            out.append(refs[pos:pos + cnt])
            pos += cnt
        return out

    def body(*refs):
        ins = split(refs[:sum(c_in)], c_in)
        outs = split(refs[sum(c_in):sum(c_in) + sum(c_out)], c_out)
        scr = split(refs[sum(c_in) + sum(c_out):], c_scr)
        for phase in ("start", "relay", "finish"):
            for k, cm in enumerate(comms):
                getattr(cm, phase)(ins[k], outs[k], scr[k])

    outs = _pc(body, name=name, out_shape=[s for cm in comms for s in cm.out_shapes],
               in_specs=[ANY] * sum(c_in), out_specs=[ANY] * sum(c_out), input_output_aliases=aliases,
               scratch_shapes=[s for cm in comms for s in cm.scratch],
               compiler_params=_params())(*[a for cm in comms for a in cm.arrays])
    return split(list(outs), c_out)


def _all_gather(arrs, name):
    return _run_comms([_AllGather(arrs)], name)[0]


def _call(body, *, name, grid, in_specs, out_specs, out_shape, scratch_shapes, sem, args, comms=()):
    n_in, n_out, n_scr = len(in_specs), len(out_specs), len(scratch_shapes)
    c_in = [len(cm.arrays) for cm in comms]
    c_out = [len(cm.out_shapes) for cm in comms]
    c_scr = [len(cm.scratch) for cm in comms]
    aliases = {}
    for k, cm in enumerate(comms):
        for a, b in cm.aliases.items():
            aliases[n_in + sum(c_in[:k]) + a] = n_out + sum(c_out[:k]) + b

    def split(refs, counts):
        out, pos = [], 0
        for cnt in counts:
            out.append(refs[pos:pos + cnt])
            pos += cnt
        return out

    def wrapped(*refs):
        ins = refs[:n_in + sum(c_in)]
        outs = refs[len(ins):len(ins) + n_out + sum(c_out)]
        scr = refs[len(ins) + len(outs):]
        cins, couts, cscr = split(ins[n_in:], c_in), split(outs[n_out:], c_out), split(scr[n_scr:], c_scr)
        if comms:
            first = pl.program_id(0) == 0
            last = pl.program_id(0) == grid[0] - 1
            for k in range(1, len(grid)):
                first = jnp.logical_and(first, pl.program_id(k) == 0)
                last = jnp.logical_and(last, pl.program_id(k) == grid[k] - 1)

            @pl.when(first)
            def _():
                for k, cm in enumerate(comms):
                    cm.start(cins[k], couts[k], cscr[k])
        body(*ins[:n_in], *outs[:n_out], *scr[:n_scr])
        if comms:
            relay_early = len(grid) == 1 and grid[0] >= 4
            if relay_early:
                @pl.when(pl.program_id(0) == (7 * grid[0]) // 8 - 1)
                def _():
                    for k, cm in enumerate(comms):
                        cm.relay(cins[k], couts[k], cscr[k])

            @pl.when(last)
            def _():
                for k, cm in enumerate(comms):
                    if not relay_early:
                        cm.relay(cins[k], couts[k], cscr[k])
                    cm.finish(cins[k], couts[k], cscr[k])

    outs = _pc(wrapped, name=name, grid=grid,
               in_specs=list(in_specs) + [ANY] * sum(c_in), out_specs=list(out_specs) + [ANY] * sum(c_out),
               out_shape=list(out_shape) + [s for cm in comms for s in cm.out_shapes],
               scratch_shapes=list(scratch_shapes) + [s for cm in comms for s in cm.scratch],
               input_output_aliases=aliases, compiler_params=_params(*sem),
               )(*args, *[a for cm in comms for a in cm.arrays])
    outs = list(outs)
    return outs[:n_out], split(outs[n_out:], c_out)


def _row_block(r):
    for b in (512, 256, 128, 64, 32, 16, 8):
        if r % b == 0:
            return b
    return r


def _pair_add(g, recv, cj_idx, name):
    _, r, cc = g.shape
    br = _row_block(r)

    def body(cj_ref, g_ref, r_ref, own_ref, pb_ref):
        s = g_ref[...] + r_ref[...]
        pb_ref[...] = s.astype(BF16)

        @pl.when(pl.program_id(1) == cj_ref[1])
        def _():
            own_ref[...] = s[0]

    grid_spec = pltpu.PrefetchScalarGridSpec(
        num_scalar_prefetch=1, grid=(r // br, N_CHIP),
        in_specs=[pl.BlockSpec((1, br, cc), lambda i, j, cj_ref: (2 * j + cj_ref[0], i, 0)),
                  pl.BlockSpec((1, br, cc), lambda i, j, cj_ref: (j, i, 0))],
        out_specs=[pl.BlockSpec((br, cc), lambda i, j, cj_ref: (i, 0)),
                   pl.BlockSpec((1, br, cc), lambda i, j, cj_ref: (j, i, 0))])
    return _pc(body, name=name, grid_spec=grid_spec,
               out_shape=[_sds((r, cc)), _sds((N_CHIP, r, cc), BF16)],
               compiler_params=_params("arbitrary", "arbitrary"))(cj_idx, g, recv)


def _chip_add(p, qs, cj_idx, name):
    r, cc = p.shape
    nq = len(qs)
    br = _row_block(r // nq)
    nb = r // nq // br

    def body(cj_ref, p_ref, *refs):
        o_ref = refs[-1]
        if nq == 2:
            top = pl.program_id(0) < nb
            q = [jnp.where(top, refs[0][k], refs[1][k]).astype(F32) for k in range(3)]
        else:
            q = [refs[0][k].astype(F32) for k in range(3)]
        o_ref[0] = ((p_ref[...] + q[0]) + q[1]) + q[2]

    q_specs = [pl.BlockSpec((3, br, cc), lambda i, cj_ref, h=h: (0, jnp.clip(i - h * nb, 0, nb - 1), 0))
               for h in range(nq)]
    grid_spec = pltpu.PrefetchScalarGridSpec(
        num_scalar_prefetch=1, grid=(r // br,),
        in_specs=[pl.BlockSpec((br, cc), lambda i, cj_ref: (i, 0))] + q_specs,
        out_specs=pl.BlockSpec((1, br, cc), lambda i, cj_ref: (cj_ref[0], i, 0)))
    return _pc(body, name=name, grid_spec=grid_spec, out_shape=_sds((2, r, cc)),
               compiler_params=_params("arbitrary"))(cj_idx, p, *qs)


def _shard_of(both):
    return both.reshape((2 * both.shape[1],) + both.shape[2:])


ADAMW_CHUNKS = 4


def _adamw(w, g, m, v, name):
    r, cc = w.shape
    rows = r // ADAMW_CHUNKS
    assert rows * ADAMW_CHUNKS == r and rows % SUBLANES == 0
    c1 = 1.0 - ADAM_B1 ** ADAM_STEP
    c2 = 1.0 - ADAM_B2 ** ADAM_STEP

    def body(w_hbm, g_hbm, m_hbm, v_hbm, go_hbm, d_hbm, mo_hbm, vo_hbm, wb, gb, mb, vb, sem_in, sem_out):
        srcs, bufs, dsts = (w_hbm, g_hbm, m_hbm, v_hbm), (wb, gb, mb, vb), (d_hbm, go_hbm, mo_hbm, vo_hbm)

        def load(a, k):
            sl = pl.ds(k * rows, rows)
            return pltpu.make_async_copy(srcs[a].at[sl], bufs[a].at[sl], sem_in.at[a, k])

        def store(a, k):
            sl = pl.ds(k * rows, rows)
            return pltpu.make_async_copy(bufs[a].at[sl], dsts[a].at[sl], sem_out.at[a, k])

        for k in range(ADAMW_CHUNKS):
            for a in range(4):
                load(a, k).start(priority=(a + k) % 2)
        for k in range(ADAMW_CHUNKS):
            for a in range(4):
                load(a, k).wait()
            store(1, k).start(priority=(1 + k) % 2)
            sl = pl.ds(k * rows, rows)
            gg = gb[sl]
            mn = ADAM_B1 * mb[sl] + (1.0 - ADAM_B1) * gg
            vn = ADAM_B2 * vb[sl] + (1.0 - ADAM_B2) * (gg * gg)
            mh = mn / c1
            vh = vn / c2
            wb[sl] = -ADAM_LR * (mh / (jnp.sqrt(vh) + ADAM_EPS) + ADAM_WD * wb[sl])
            mb[sl] = mn
            vb[sl] = vn
            for a in (0, 2, 3):
                store(a, k).start(priority=(a + k) % 2)
        for k in range(ADAMW_CHUNKS):
            for a in range(4):
                store(a, k).wait()

    go, d, mo, vo = _pc(body, name=name, in_specs=[ANY] * 4, out_specs=[ANY] * 4, out_shape=[_sds((r, cc))] * 4,
                        scratch_shapes=[pltpu.VMEM((r, cc), F32)] * 4 + [pltpu.SemaphoreType.DMA((4, ADAMW_CHUNKS))] * 2,
                        compiler_params=_params())(w, g, m, v)
    return go, d, mo, vo


def _head(w_half, c8, small, w_ada, b_shard, c_ctx, ret_decay):
    ada_n = w_ada.shape[1]
    mod_sds = _sds((16, ada_n))
    ag_w, ag_c, ag_m = _AllGather([w_half]), _AllGather([c8, small]), _AllGather([mod_sds])
    n_w, n_c, n_m = len(ag_w.scratch), len(ag_c.scratch), len(ag_m.scratch)

    def body(w_ref, c_ref, s_ref, wada_ref, b_ref, cc_ref, rd_ref,
             gw_ref, call_ref, sall_ref, a_ref, modp_ref, mall_ref, lg_ref, sg_ref, *scr):
        scr_w, scr_c, scr_m = scr[:n_w], scr[n_w:n_w + n_c], scr[n_w + n_c:n_w + n_c + n_m]
        c_v, w_v, m_v, sems = scr[n_w + n_c + n_m:]
        ag_w.start((w_ref,), (gw_ref,), scr_w)
        ag_c.start((c_ref, s_ref), (call_ref, sall_ref), scr_c)
        load_w = pltpu.make_async_copy(wada_ref, w_v, sems.at[0])
        load_w.start()
        rd = rd_ref[...]
        lg_ref[...] = -_softplus(-rd)
        sg_ref[...] = _sigmoid(-rd)
        ag_c.relay((c_ref, s_ref), (call_ref, sall_ref), scr_c)
        ag_c.finish((c_ref, s_ref), (call_ref, sall_ref), scr_c)
        load_c = pltpu.make_async_copy(call_ref, c_v, sems.at[1])
        load_c.start()
        load_c.wait()
        a_ref[...] = jnp.zeros_like(a_ref)
        for d in range(N_DEV):
            cd = c_v[d, 0:1, :]
            a_ref[d:d + 1, :] = cd * _sigmoid(cd)
        cc = cc_ref[...]
        a_ref[N_DEV:N_DEV + 1, :] = cc * _sigmoid(cc)
        load_w.wait()
        m_v[...] = jnp.dot(a_ref[...], w_v[...], preferred_element_type=F32,
                           precision=lax.Precision.HIGHEST) + b_ref[...]
        put = pltpu.make_async_copy(m_v, modp_ref, sems.at[2])
        put.start()
        put.wait()
        ag_m.start((modp_ref,), (mall_ref,), scr_m)
        ag_m.relay((modp_ref,), (mall_ref,), scr_m)
        ag_m.finish((modp_ref,), (mall_ref,), scr_m)
        ag_w.relay((w_ref,), (gw_ref,), scr_w)
        ag_w.finish((w_ref,), (gw_ref,), scr_w)

    rd = jnp.broadcast_to(ret_decay.reshape(2, HEADS).T[:, :, None], (HEADS, 2, LANES))
    lane = _full((HEADS, 2, LANES))
    outs = _pc(
        body, name="head",
        in_specs=[ANY, ANY, ANY, ANY, _full((1, ada_n)), _full((1, D_MODEL)), lane],
        out_specs=[ANY, ANY, ANY, _full((16, D_MODEL)), ANY, ANY, lane, lane],
        out_shape=ag_w.out_shapes + ag_c.out_shapes + [_sds((16, D_MODEL)), mod_sds] + ag_m.out_shapes
        + [_sds((HEADS, 2, LANES))] * 2,
        scratch_shapes=ag_w.scratch + ag_c.scratch + ag_m.scratch
        + [pltpu.VMEM((N_DEV,) + c8.shape, F32), pltpu.VMEM(w_ada.shape, F32), pltpu.VMEM((16, ada_n), F32),
           pltpu.SemaphoreType.DMA((3,))],
        compiler_params=_params(),
    )(w_half, c8, small, w_ada, b_shard, c_ctx.reshape(1, D_MODEL), rd)
    gw, c_all, small_all, a16, _, mod_all, lgv, sgv = outs
    return gw, c_all, small_all, a16, mod_all, lgv, sgv


def _ada_grad(at, b):
    n = b.shape[1]
    bn = 512

    def body(a_ref, b_ref, o_ref):
        o_ref[...] = jnp.dot(a_ref[...], b_ref[...], preferred_element_type=F32, precision=lax.Precision.HIGHEST)

    return _pc(body, name="ada_grad", grid=(n // bn,),
               in_specs=[_full((D_MODEL, LANES)), pl.BlockSpec((LANES, bn), lambda i: (0, i))],
               out_specs=pl.BlockSpec((D_MODEL, bn), lambda i: (0, i)), out_shape=_sds((D_MODEL, n)),
               compiler_params=_params("arbitrary"))(at, b)


def _cctx_partial(dmc8, w_ada):
    n = w_ada.shape[1]
    bn = 512

    def body(d_ref, w_ref, o_ref):
        @pl.when(pl.program_id(0) == 0)
        def _():
            o_ref[...] = jnp.zeros_like(o_ref)
        o_ref[...] += lax.dot_general(d_ref[...], w_ref[...], (((1,), (1,)), ((), ())),
                                      preferred_element_type=F32, precision=lax.Precision.HIGHEST)

    return _pc(body, name="cctx_partial", grid=(n // bn,),
               in_specs=[pl.BlockSpec((8, bn), lambda i: (0, i)), pl.BlockSpec((D_MODEL, bn), lambda i: (0, i))],
               out_specs=_full((8, D_MODEL)), out_shape=_sds((8, D_MODEL)),
               compiler_params=_params("arbitrary"))(dmc8, w_ada)


def _cctx_final(parts, c_ctx, m, v):
    c1 = 1.0 - ADAM_B1 ** ADAM_STEP
    c2 = 1.0 - ADAM_B2 ** ADAM_STEP

    def body(p_ref, c_ref, m_ref, v_ref, g_ref, d_ref, mo_ref, vo_ref):
        s = ((p_ref[0, 0:1, :] + p_ref[2, 0:1, :]) + p_ref[4, 0:1, :]) + p_ref[6, 0:1, :]
        z = c_ref[...]
        sg = _sigmoid(z)
        gg = s * (sg * (1.0 + z * (1.0 - sg)))
        g_ref[...] = gg
        mn = ADAM_B1 * m_ref[...] + (1.0 - ADAM_B1) * gg
        vn = ADAM_B2 * v_ref[...] + (1.0 - ADAM_B2) * (gg * gg)
        d_ref[...] = -ADAM_LR * ((mn / c1) / (jnp.sqrt(vn / c2) + ADAM_EPS) + ADAM_WD * z)
        mo_ref[...] = mn
        vo_ref[...] = vn

    row = _full((1, D_MODEL))
    return _pc(body, name="cctx_final", out_shape=[_sds((1, D_MODEL))] * 4,
               in_specs=[_full(parts.shape), row, row, row], out_specs=[row] * 4,
               compiler_params=_params())(parts, c_ctx.reshape(1, D_MODEL), m.reshape(1, D_MODEL), v.reshape(1, D_MODEL))


def _rotary_tables(t_len):
    rows = t_len // GRID_W
    n_freq = DH // 4
    inv = ROPE_BASE ** (-jnp.arange(n_freq, dtype=F32) / n_freq)
    row_ang = jnp.arange(rows, dtype=F32)[:, None] * inv
    col_ang = jnp.arange(GRID_W, dtype=F32)[:, None] * inv

    def spread(fn):
        return jnp.concatenate([jnp.repeat(fn(row_ang), GRID_W, axis=0), jnp.tile(fn(col_ang), (rows, 1))], axis=-1)

    cos, sin = spread(jnp.cos), spread(jnp.sin)
    return jnp.concatenate([cos, cos], axis=-1), jnp.concatenate([-sin, sin], axis=-1)


def _inproj_fwd(x, gn, sh, sc, w4, cos2, sin2, name, comms=()):
    t = x.shape[0]
    tm = _tile(t, True)
    nc = IN_COLS // N_CHIP

    def body(x_ref, gn_ref, sh_ref, sc_ref, w_ref, c_ref, s_ref, p_ref, xr_ref, hb_ref, p_s):
        xh, _ = _rms(x_ref[...])
        h = xh * gn_ref[...] * (1.0 + sc_ref[...]) + sh_ref[...]
        hb = h.astype(BF16)
        hb_ref[...] = hb
        for j in range(N_CHIP):
            p_s[:, nc * j:nc * (j + 1)] = _dot(hb, w_ref[j])
        cc = c_ref[...]
        ss = s_ref[...]
        for hh in range(2 * HEADS):
            blk = p_s[:, DH * hh:DH * (hh + 1)]
            rot = blk * cc + pltpu.roll(blk, DH // 2, 1) * ss
            if hh >= HEADS:
                rot = rot * K_SCALE
            p_ref[:, DH * hh:DH * (hh + 1)] = rot.astype(BF16)
        p_ref[:, 2 * RET_W:] = p_s[:, 2 * RET_W:].astype(BF16)
        xr_ref[...] = p_s[:, 4 * RET_W:4 * RET_W + LRU_W]

    row = _full((1, D_MODEL))
    outs, couts = _call(
        body, name=name, grid=(t // tm,),
        in_specs=[pl.BlockSpec((tm, D_MODEL), lambda i: (i, 0)), row, row, row, _full(w4.shape),
                  pl.BlockSpec((tm, DH), lambda i: (i, 0)), pl.BlockSpec((tm, DH), lambda i: (i, 0))],
        out_specs=[pl.BlockSpec((tm, IN_COLS), lambda i: (i, 0)), pl.BlockSpec((tm, LRU_W), lambda i: (i, 0)),
                   pl.BlockSpec((tm, D_MODEL), lambda i: (i, 0))],
        out_shape=[_sds((t, IN_COLS), BF16), _sds((t, LRU_W)), _sds((t, D_MODEL), BF16)],
        scratch_shapes=[pltpu.VMEM((tm, IN_COLS), F32)], sem=("arbitrary",),
        args=(x, gn, sh, sc, w4, cos2, sin2), comms=comms)
    return (outs, couts) if comms else outs


def _inproj_bwd(x, gn, sh, sc, w4, cos2, sin2, pieces, dres, name):
    t = x.shape[0]
    tm = _tile(t, True)
    nc = IN_COLS // N_CHIP

    def body(x_ref, gn_ref, sh_ref, sc_ref, w_ref, c_ref, s_ref, dqf, dqb, dkf, dkb, dvf, dvb, dg, dxr, dgt, dres_ref,
             dx_ref, dpb_ref, dgn_ref, dsh_ref, dsc_ref):
        cc = c_ref[...]
        ss = s_ref[...]
        dq = dqf[...].astype(F32) + dqb[...].astype(F32)
        dk = dkf[...].astype(F32) + dkb[...].astype(F32)
        for hh in range(HEADS):
            sl = slice(DH * hh, DH * (hh + 1))
            b = dq[:, sl]
            dpb_ref[:, sl] = (b * cc + pltpu.roll(b * ss, DH // 2, 1)).astype(BF16)
            b = dk[:, sl]
            dpb_ref[:, RET_W + DH * hh:RET_W + DH * (hh + 1)] = (
                (b * cc + pltpu.roll(b * ss, DH // 2, 1)) * K_SCALE).astype(BF16)
        dpb_ref[:, 2 * RET_W:3 * RET_W] = (dvf[...].astype(F32) + dvb[...].astype(F32)).astype(BF16)
        dpb_ref[:, 3 * RET_W:4 * RET_W] = dg[...].astype(BF16)
        dpb_ref[:, 4 * RET_W:4 * RET_W + LRU_W] = dxr[...].astype(BF16)
        dpb_ref[:, 4 * RET_W + LRU_W:IN_COLS] = dgt[...].astype(BF16)
        dh = _dot_nt(dpb_ref[:, 0:nc], w_ref[0])
        for j in range(1, N_CHIP):
            dh = dh + _dot_nt(dpb_ref[:, nc * j:nc * (j + 1)], w_ref[j])
        dx, dgn_t, dsh_t, dsc_t = _norm_mod_bwd(x_ref[...], gn_ref[...], sc_ref[...], dh)
        dx_ref[...] = dres_ref[...] + dx

        @pl.when(pl.program_id(0) == 0)
        def _():
            dgn_ref[...] = jnp.zeros_like(dgn_ref)
            dsh_ref[...] = jnp.zeros_like(dsh_ref)
            dsc_ref[...] = jnp.zeros_like(dsc_ref)
        dgn_ref[...] += dgn_t
        dsh_ref[...] += dsh_t
        dsc_ref[...] += dsc_t

    row = _full((1, D_MODEL))
    pc = pl.BlockSpec((tm, RET_W), lambda i: (i, 0))
    big = pl.BlockSpec((tm, D_MODEL), lambda i: (i, 0))
    return _pc(body, name=name, grid=(t // tm,),
               in_specs=[big, row, row, row, _full(w4.shape, single=True),
                         pl.BlockSpec((tm, DH), lambda i: (i, 0)), pl.BlockSpec((tm, DH), lambda i: (i, 0))]
               + [pc] * 9 + [big],
               out_specs=[big, pl.BlockSpec((tm, IN_COLS), lambda i: (i, 0)), row, row, row],
               out_shape=[_sds((t, D_MODEL)), _sds((t, IN_COLS), BF16), _sds((1, D_MODEL)), _sds((1, D_MODEL)),
                          _sds((1, D_MODEL))],
               compiler_params=_params("arbitrary"))(x, gn, sh, sc, w4, cos2, sin2, *pieces, dres)


def _halo_specs(t, tm):
    n8 = tm // SUBLANES
    last8 = t // SUBLANES - 1
    prev = pl.BlockSpec((SUBLANES, LRU_W), lambda i: (jnp.maximum(i * n8 - 1, 0), 0))
    main = pl.BlockSpec((tm, LRU_W), lambda i: (i, 0))
    nxt = pl.BlockSpec((SUBLANES, LRU_W), lambda i: (jnp.minimum((i + 1) * n8, last8), 0))
    return prev, main, nxt


def _with_halo(prev_ref, main_ref, next_ref, i, nt):
    prev = jnp.where(i > 0, prev_ref[...], 0.0)
    nxt = jnp.where(i < nt - 1, next_ref[...], 0.0)
    return jnp.concatenate([prev, main_ref[...], nxt], axis=0)


def _conv_fwd(xr, cw, cb, name):
    t = xr.shape[0]
    tm = _tile(t, True)
    nt = t // tm
    n = tm + 2 * SUBLANES
    mid = slice(SUBLANES, SUBLANES + tm)

    def body(p_ref, m_ref, n_ref, w_ref, b_ref, o_ref):
        xp = _with_halo(p_ref, m_ref, n_ref, pl.program_id(0), nt)
        acc = b_ref[...] + pltpu.roll(xp, 1, 0)[mid] * w_ref[0:1, :]
        acc = acc + xp[mid] * w_ref[1:2, :]
        acc = acc + pltpu.roll(xp, n - 1, 0)[mid] * w_ref[2:3, :]
        acc = acc + pltpu.roll(xp, n - 2, 0)[mid] * w_ref[3:4, :]
        o_ref[...] = acc

    return _pc(body, name=name, grid=(nt,),
               in_specs=[*_halo_specs(t, tm), _full((4, LRU_W)), _full((1, LRU_W))],
               out_specs=pl.BlockSpec((tm, LRU_W), lambda i: (i, 0)), out_shape=_sds((t, LRU_W)),
               compiler_params=_params("arbitrary"))(xr, xr, xr, cw, cb)


def _conv_bwd(dxc_a, dxc_b, xr, cw, name):
    t = xr.shape[0]
    tm = _tile(t, True)
    nt = t // tm
    n = tm + 2 * SUBLANES
    mid = slice(SUBLANES, SUBLANES + tm)

    def body(ap_ref, am_ref, an_ref, bp_ref, bm_ref, bn_ref, xp_ref, xm_ref, xn_ref, w_ref, dx_ref, dw_ref, db_ref):
        i = pl.program_id(0)
        dp = _with_halo(ap_ref, am_ref, an_ref, i, nt) + _with_halo(bp_ref, bm_ref, bn_ref, i, nt)
        xp = _with_halo(xp_ref, xm_ref, xn_ref, i, nt)
        dx = pltpu.roll(dp, n - 1, 0)[mid] * w_ref[0:1, :]
        dx = dx + dp[mid] * w_ref[1:2, :]
        dx = dx + pltpu.roll(dp, 1, 0)[mid] * w_ref[2:3, :]
        dx = dx + pltpu.roll(dp, 2, 0)[mid] * w_ref[3:4, :]
        dx_ref[...] = dx.astype(BF16)
        d = dp[mid]

        @pl.when(i == 0)
        def _():
            dw_ref[...] = jnp.zeros_like(dw_ref)
            db_ref[...] = jnp.zeros_like(db_ref)
        dw_ref[0:1, :] += _sum0(d * pltpu.roll(xp, 1, 0)[mid])
        dw_ref[1:2, :] += _sum0(d * xp[mid])
        dw_ref[2:3, :] += _sum0(d * pltpu.roll(xp, n - 1, 0)[mid])
        dw_ref[3:4, :] += _sum0(d * pltpu.roll(xp, n - 2, 0)[mid])
        db_ref[...] += _sum0(d)

    return _pc(body, name=name, grid=(nt,),
               in_specs=[*_halo_specs(t, tm), *_halo_specs(t, tm), *_halo_specs(t, tm), _full((4, LRU_W))],
               out_specs=[pl.BlockSpec((tm, LRU_W), lambda i: (i, 0)), _full((4, LRU_W)), _full((1, LRU_W))],
               out_shape=[_sds((t, LRU_W), BF16), _sds((4, LRU_W)), _sds((1, LRU_W))],
               compiler_params=_params("arbitrary"))(dxc_a, dxc_a, dxc_a, dxc_b, dxc_b, dxc_b, xr, xr, xr, cw)


def _scan_scratch(n, c):
    return pltpu.VMEM((c // LANES, n, LANES), F32)


def _to_lane_blocks(ref, val):
    for lb in range(ref.shape[0]):
        ref[lb] = val[:, lb * LANES:(lb + 1) * LANES]


def _group_scan(a_s, b_s, reverse):
    nb, n, _ = a_s.shape
    ng = n // SUBLANES
    order = range(SUBLANES - 1, -1, -1) if reverse else range(SUBLANES)
    for lb in range(nb):
        prev = None
        for r in order:
            rows = pl.ds(r, ng, stride=SUBLANES)
            a_r, b_r = a_s[lb, rows, :], b_s[lb, rows, :]
            if prev is not None:
                b_r = a_r * prev[1] + b_r
                a_r = a_r * prev[0]
                a_s[lb, rows, :] = a_r
                b_s[lb, rows, :] = b_r
            prev = (a_r, b_r)


def _carry_scans(jobs):
    nb, n, _ = jobs[0][0].shape
    ng = n // SUBLANES

    def step(g, all_crs):
        res = []
        for (a_s, b_s, out_ref, _, reverse), crs in zip(jobs, all_crs):
            gg = (ng - 1 - g) if reverse else g
            off = pl.multiple_of(gg * SUBLANES, SUBLANES)
            new = []
            for lb in range(nb):
                h = a_s[lb, pl.ds(off, SUBLANES), :] * crs[lb] + b_s[lb, pl.ds(off, SUBLANES), :]
                out_ref[pl.ds(off, SUBLANES), pl.ds(lb * LANES, LANES)] = h
                edge = h[0:1, :] if reverse else h[SUBLANES - 1:SUBLANES, :]
                new.append(jnp.broadcast_to(edge, (SUBLANES, LANES)))
            res.append(tuple(new))
        return tuple(res)

    init = tuple(tuple(job[3][:, lb * LANES:(lb + 1) * LANES] for lb in range(nb)) for job in jobs)
    return [jnp.concatenate(crs, axis=1) for crs in lax.fori_loop(0, ng, step, init)]


def _lru_gates(xc, wa_ref, wx_ref, ba, bx, lam):
    xb = xc.astype(BF16)
    r = _sigmoid(_dot(xb, wa_ref[...]) + ba)
    ig = _sigmoid(_dot(xb, wx_ref[...]) + bx)
    sp = _softplus(-lam)
    la = -LRU_C * r * sp
    a = jnp.exp(la)
    return r, ig, sp, a, _one_minus_sq(la, a)


def _lru_fwd(xc, par_f, par_b, h0_f, h0_b, name, comms=()):
    t = xc.shape[0]
    tm = _tile(t, True)
    nt = t // tm

    def one(x_ref, prm, h0_ref, a_s, b_s, c_s, reverse):
        wa_ref, wx_ref, ba_ref, bx_ref, lam_ref = prm

        @pl.when(pl.program_id(0) == 0)
        def _():
            c_s[...] = jnp.broadcast_to(h0_ref[...], c_s.shape)
        xv = x_ref[...]
        _, ig, _, a, q = _lru_gates(xv, wa_ref, wx_ref, ba_ref[...], bx_ref[...], lam_ref[...])
        _to_lane_blocks(a_s, a)
        _to_lane_blocks(b_s, jnp.sqrt(q) * (ig * xv))
        _group_scan(a_s, b_s, reverse)

    def body(xf_ref, xb_ref, *refs):
        prm_f, prm_b = refs[0:5], refs[5:10]
        h0f_ref, h0b_ref, hf_ref, hb_ref = refs[10:14]
        af_s, bf_s, cf_s, ab_s, bb_s, cb_s = refs[14:]
        one(xf_ref, prm_f, h0f_ref, af_s, bf_s, cf_s, False)
        one(xb_ref, prm_b, h0b_ref, ab_s, bb_s, cb_s, True)
        cf_s[...], cb_s[...] = _carry_scans([(af_s, bf_s, hf_ref, cf_s[...], False),
                                             (ab_s, bb_s, hb_ref, cb_s[...], True)])

    vec = _full((1, LRU_W))
    mat = _full((LRU_W, LRU_W))
    fw = pl.BlockSpec((tm, LRU_W), lambda i: (i, 0))
    bw = pl.BlockSpec((tm, LRU_W), lambda i: (nt - 1 - i, 0))
    tile_s = [_scan_scratch(tm, LRU_W), _scan_scratch(tm, LRU_W), pltpu.VMEM((SUBLANES, LRU_W), F32)]
    (hf, hb), couts = _call(
        body, name=name, grid=(nt,),
        in_specs=[fw, bw] + [mat, mat, vec, vec, vec] * 2 + [vec, vec],
        out_specs=[pl.BlockSpec((tm, LRU_W), lambda i: (i, 0)), pl.BlockSpec((tm, LRU_W), lambda i: (nt - 1 - i, 0))],
        out_shape=[_sds((t, LRU_W))] * 2, scratch_shapes=tile_s + tile_s, sem=("arbitrary",),
        args=(xc, xc, *par_f, *par_b, h0_f, h0_b), comms=comms)
    return ((hf, hb), couts) if comms else (hf, hb)


def _lru_bwd(xc, par_f, par_b, h_f, h_b, h0_f, h0_b, dh_f, dh_b, name, comms=()):
    t = xc.shape[0]
    tm = _tile(t, True)
    nt = t // tm
    n8 = tm // SUBLANES
    last8 = t // SUBLANES - 1
    tile_f = lambda w: pl.BlockSpec((tm, w), lambda i: (nt - 1 - i, 0))
    tile_b = lambda w: pl.BlockSpec((tm, w), lambda i: (i, 0))
    halo_f = pl.BlockSpec((SUBLANES, LRU_W), lambda i: (jnp.maximum((nt - 1 - i) * n8 - 1, 0), 0))
    halo_b = pl.BlockSpec((SUBLANES, LRU_W), lambda i: (jnp.minimum((i + 1) * n8, last8), 0))

    def one(refs_in, refs_out, refs_scr, reverse):
        x_ref, wa_ref, wx_ref, ba_ref, bx_ref, lam_ref, h_ref, halo_ref, h0_ref, dh_ref = refs_in
        dx_ref, dpre_ref, dba_ref, dbx_ref, dlam_ref, dh0_ref = refs_out
        a_s, b_s, l_s, c_s, e_s = refs_scr
        i = pl.program_id(0)

        @pl.when(i == 0)
        def _():
            c_s[...] = jnp.zeros_like(c_s)
            e_s[...] = jnp.zeros_like(e_s)
            dba_ref[...] = jnp.zeros_like(dba_ref)
            dbx_ref[...] = jnp.zeros_like(dbx_ref)
            dlam_ref[...] = jnp.zeros_like(dlam_ref)
        xv = x_ref[...]
        lam = lam_ref[...]
        r, ig, sp, a, q = _lru_gates(xv, wa_ref, wx_ref, ba_ref[...], bx_ref[...], lam)
        rs = lax.rsqrt(q)
        hv = h_ref[...]
        rowi = lax.broadcasted_iota(jnp.int32, (tm, LRU_W), 0)
        edge_a = jnp.broadcast_to(e_s[0:1, :], (tm, LRU_W))
        h0b = jnp.broadcast_to(h0_ref[...], (tm, LRU_W))
        if reverse:
            a_sh = jnp.where(rowi == 0, edge_a, pltpu.roll(a, 1, 0))
            hin_edge = jnp.where(i == nt - 1, h0b, jnp.broadcast_to(halo_ref[0:1, :], (tm, LRU_W)))
            h_in = jnp.where(rowi == tm - 1, hin_edge, pltpu.roll(hv, tm - 1, 0))
        else:
            a_sh = jnp.where(rowi == tm - 1, edge_a, pltpu.roll(a, tm - 1, 0))
            hin_edge = jnp.where(i == nt - 1, h0b, jnp.broadcast_to(halo_ref[SUBLANES - 1:SUBLANES, :], (tm, LRU_W)))
            h_in = jnp.where(rowi == 0, hin_edge, pltpu.roll(hv, 1, 0))
        _to_lane_blocks(a_s, a_sh)
        _to_lane_blocks(b_s, dh_ref[...])
        _group_scan(a_s, b_s, not reverse)
        e_s[...] = jnp.broadcast_to(a[tm - 1:tm, :] if reverse else a[0:1, :], e_s.shape)
        return lam, r, ig, sp, a, q * rs, rs, h_in

    def post(vals, refs_in, refs_out, refs_scr, reverse):
        lam, r, ig, sp, a, mult, rs, h_in = vals
        xv = refs_in[0][...]
        wa_ref, wx_ref = refs_in[1:3]
        dx_ref, dpre_ref, dba_ref, dbx_ref, dlam_ref, dh0_ref = refs_out
        l_s = refs_scr[2]
        i = pl.program_id(0)
        lmb = l_s[...]
        da = lmb * h_in
        ixc = ig * xv
        dmult = lmb * ixc
        dixc = lmb * mult
        dla = da * a - dmult * (a * a) * rs
        dpr = dla * (-LRU_C * sp) * r * (1.0 - r)
        dpi = dixc * xv * ig * (1.0 - ig)
        dprb = dpr.astype(BF16)
        dpib = dpi.astype(BF16)
        dpre_ref[:, 0:LRU_W] = dprb
        dpre_ref[:, LRU_W:2 * LRU_W] = dpib
        dx_ref[...] = dixc * ig + _dot_nt(dprb, wa_ref[...]) + _dot_nt(dpib, wx_ref[...])
        dba_ref[...] += _sum0(dpr)
        dbx_ref[...] += _sum0(dpi)
        dlam_ref[...] += _sum0(dla * (-LRU_C * r)) * (-_sigmoid(-lam))

        @pl.when(i == nt - 1)
        def _():
            al0 = a * lmb
            dh0_ref[...] = al0[tm - 1:tm, :] if reverse else al0[0:1, :]

    def body(*refs):
        jobs = ((refs[0:10], refs[20:26], refs[32:37], False), (refs[10:20], refs[26:32], refs[37:42], True))
        vals = [one(*job) for job in jobs]
        carries = _carry_scans([(scr[0], scr[1], scr[2], scr[3][...], not rev) for _, _, scr, rev in jobs])
        for (_, _, scr, _), carry in zip(jobs, carries):
            scr[3][...] = carry
        for v, job in zip(vals, jobs):
            post(v, *job)

    vec = _full((1, LRU_W))
    mat = _full((LRU_W, LRU_W))

    def in_specs(tile, halo):
        return [tile(LRU_W), mat, mat, vec, vec, vec, tile(LRU_W), halo, vec, tile(LRU_W)]

    def out_specs(tile):
        return [tile(LRU_W), tile(2 * LRU_W), vec, vec, vec, vec]

    out_one = [_sds((t, LRU_W)), _sds((t, 2 * LRU_W), BF16)] + [_sds((1, LRU_W))] * 4
    scr_one = ([_scan_scratch(tm, LRU_W)] * 2 + [pltpu.VMEM((tm, LRU_W), F32)]
               + [pltpu.VMEM((SUBLANES, LRU_W), F32)] * 2)
    outs, couts = _call(
        body, name=name, grid=(nt,), in_specs=in_specs(tile_f, halo_f) + in_specs(tile_b, halo_b),
        out_specs=out_specs(tile_f) + out_specs(tile_b), out_shape=out_one + out_one,
        scratch_shapes=scr_one + scr_one, sem=("arbitrary",),
        args=(xc, *par_f, h_f, h_f, h0_f, dh_f, xc, *par_b, h_b, h_b, h0_b, dh_b), comms=comms)
    return (tuple(outs[0:6]), tuple(outs[6:12])), couts


def _decay_tables(lg, reverse):
    ci = lax.broadcasted_iota(jnp.int32, (CHUNK, CHUNK), 0).astype(F32)
    mi = lax.broadcasted_iota(jnp.int32, (CHUNK, CHUNK), 1).astype(F32)
    rel = (mi - ci) if reverse else (ci - mi)
    relc = jnp.maximum(rel, 0.0)
    lg_c = jnp.concatenate([lg] * (CHUNK // LANES), axis=1)
    dm = jnp.where(rel >= 0, jnp.exp(lg_c * relc), 0.0)
    cd = lax.broadcasted_iota(jnp.int32, (CHUNK, DH), 0).astype(F32)
    pq, ps = (CHUNK - cd, cd) if reverse else (cd + 1.0, CHUNK - 1.0 - cd)
    return relc, dm, jnp.exp(lg * pq), jnp.exp(lg * ps), jnp.exp(lg * float(CHUNK)), pq, ps


def _ret_fwd(proj, lgv, s0f, s0b, comms=()):
    t = proj.shape[0]
    n = t // CHUNK

    def one(q, k, v, lg, s_s, hh, o_ref, sp_ref, reverse):
        _, dm, wq, ws, g, _, _ = _decay_tables(lg, reverse)
        vb = v.astype(BF16)
        p = _dot_nt(q.astype(BF16), k.astype(BF16)) * dm
        s = s_s[hh]
        sp_ref[hh, 0] = s
        o_ref[:, DH * hh:DH * (hh + 1)] = _dot(p.astype(BF16), vb) + _dot((q * wq).astype(BF16), s.astype(BF16))
        s_s[hh] = g * s + _dot_tn((k * ws).astype(BF16), vb)

    def body(qf, kf, vf, qb, kb, vb, lg_ref, s0f_ref, s0b_ref, of_ref, ob_ref, spf_ref, spb_ref, sf_s, sb_s):
        @pl.when(pl.program_id(0) == 0)
        def _():
            sf_s[...] = s0f_ref[...]
            sb_s[...] = s0b_ref[...]
        for hh in range(HEADS):
            sl = slice(DH * hh, DH * (hh + 1))
            one(qf[:, sl].astype(F32), kf[:, sl].astype(F32), vf[:, sl], lg_ref[hh, 0:1, :], sf_s, hh, of_ref, spf_ref,
                False)
            one(qb[:, sl].astype(F32), kb[:, sl].astype(F32), vb[:, sl], lg_ref[hh, 1:2, :], sb_s, hh, ob_ref, spb_ref,
                True)

    blk = (CHUNK, RET_W)
    fw = [pl.BlockSpec(blk, lambda i, o=o: (i, o)) for o in range(3)]
    bw = [pl.BlockSpec(blk, lambda i, o=o: (n - 1 - i, o)) for o in range(3)]
    st = _full((HEADS, DH, DH))
    return _call(body, name="ret_fwd", grid=(n,),
                 in_specs=fw + bw + [_full((HEADS, 2, LANES)), st, st],
                 out_specs=[pl.BlockSpec(blk, lambda i: (i, 0)), pl.BlockSpec(blk, lambda i: (n - 1 - i, 0)),
                            pl.BlockSpec((HEADS, 1, DH, DH), lambda i: (0, i, 0, 0)),
                            pl.BlockSpec((HEADS, 1, DH, DH), lambda i: (0, n - 1 - i, 0, 0))],
                 out_shape=[_sds((t, RET_W)), _sds((t, RET_W)), _sds((HEADS, n, DH, DH)), _sds((HEADS, n, DH, DH))],
                 scratch_shapes=[pltpu.VMEM((HEADS, DH, DH), F32), pltpu.VMEM((HEADS, DH, DH), F32)],
                 sem=("arbitrary",), args=(proj, proj, proj, proj, proj, proj, lgv, s0f, s0b), comms=comms)


def _ret_bwd(proj, lgv, sgv, spf, spb, do, comms=()):
    t = proj.shape[0]
    n = t // CHUNK

    def one(q_ref, k_ref, v_ref, lg_ref, s_ref, do_ref, dq_ref, dk_ref, dv_ref, ds_s, acc_s, reverse):
        d = 1 if reverse else 0
        for hh in range(HEADS):
            sl = slice(DH * hh, DH * (hh + 1))
            relc, dm, wq, ws, g, pq, ps = _decay_tables(lg_ref[hh, d:d + 1, :], reverse)
            qb, kb, vb = q_ref[:, sl], k_ref[:, sl], v_ref[:, sl]
            q, k = qb.astype(F32), kb.astype(F32)
            p = _dot_nt(qb, kb) * dm
            s = s_ref[hh, 0]
            dob = do_ref[:, sl].astype(BF16)
            dsn = ds_s[hh]
            dsb = dsn.astype(BF16)
            dv_ref[:, sl] = (_dot_tn(p.astype(BF16), dob) + _dot((k * ws).astype(BF16), dsb)).astype(BF16)
            dp = _dot_nt(dob, vb)
            dab = (dp * dm).astype(BF16)
            xq = _dot_nt(dob, s.astype(BF16))
            yk = _dot_nt(vb, dsb)
            dq_ref[:, sl] = (_dot(dab, kb) + xq * wq).astype(BF16)
            dk_ref[:, sl] = (_dot_tn(dab, qb) + yk * ws).astype(BF16)
            ds_s[hh] = g * dsn + _dot_tn((q * wq).astype(BF16), dob)
            s_mask = _sum0(dp * p * relc)
            part = (sum(s_mask[:, LANES * u:LANES * (u + 1)] for u in range(CHUNK // LANES))
                    + _sum0(xq * q * wq * pq) + _sum0(yk * k * ws * ps) + _sum0(dsn * s) * g * float(CHUNK))
            acc_s[hh] += jnp.broadcast_to(part, (SUBLANES, LANES))

    def body(qf, kf, vf, qb, kb, vb, lg_ref, sg_ref, sf_ref, sb_ref, dof_ref, dob_ref,
             dqf, dkf, dvf, dqb, dkb, dvb, ds0f_ref, ds0b_ref, drdf_ref, drdb_ref, dsf_s, dsb_s, accf_s, accb_s):
        i = pl.program_id(0)

        @pl.when(i == 0)
        def _():
            for r in (dsf_s, dsb_s, accf_s, accb_s):
                r[...] = jnp.zeros_like(r)
        one(qf, kf, vf, lg_ref, sf_ref, dof_ref, dqf, dkf, dvf, dsf_s, accf_s, False)
        one(qb, kb, vb, lg_ref, sb_ref, dob_ref, dqb, dkb, dvb, dsb_s, accb_s, True)

        @pl.when(i == n - 1)
        def _():
            ds0f_ref[...] = dsf_s[...]
            ds0b_ref[...] = dsb_s[...]
            for d, (acc_s, drd_ref) in enumerate(((accf_s, drdf_ref), (accb_s, drdb_ref))):
                for hh in range(HEADS):
                    tot = jnp.sum(acc_s[hh, 0:1, :], axis=1, keepdims=True)
                    drd_ref[hh] = jnp.broadcast_to(tot, (SUBLANES, LANES)) * sg_ref[hh, d:d + 1, :]

    blk = (CHUNK, RET_W)
    fw = lambda o: pl.BlockSpec(blk, lambda i, o=o: (n - 1 - i, o))
    bw = lambda o: pl.BlockSpec(blk, lambda i, o=o: (i, o))
    lane = _full((HEADS, 2, LANES))
    st = _full((HEADS, DH, DH))
    rd = _full((HEADS, SUBLANES, LANES))
    outs, couts = _call(
        body, name="ret_bwd", grid=(n,),
        in_specs=[fw(0), fw(1), fw(2), bw(0), bw(1), bw(2), lane, lane,
                  pl.BlockSpec((HEADS, 1, DH, DH), lambda i: (0, n - 1 - i, 0, 0)),
                  pl.BlockSpec((HEADS, 1, DH, DH), lambda i: (0, i, 0, 0)), fw(0), bw(0)],
        out_specs=[fw(0), fw(0), fw(0), bw(0), bw(0), bw(0), st, st, rd, rd],
        out_shape=[_sds((t, RET_W), BF16)] * 6 + [_sds((HEADS, DH, DH))] * 2 + [_sds((HEADS, SUBLANES, LANES))] * 2,
        scratch_shapes=[pltpu.VMEM((HEADS, DH, DH), F32)] * 2 + [pltpu.VMEM((HEADS, SUBLANES, LANES), F32)] * 2,
        sem=("arbitrary",), args=(proj, proj, proj, proj, proj, proj, lgv, sgv, spf, spb, do, do), comms=comms)
    dqf, dkf, dvf, dqb, dkb, dvb, ds0f, ds0b, drdf, drdb = outs
    return ((dqf, dkf, dvf, ds0f, drdf), (dqb, dkb, dvb, ds0b, drdb)), couts


def _ctx_weights(lg, l_len, reverse):
    pos = lax.broadcasted_iota(jnp.int32, (l_len, DH), 0).astype(F32)
    steps = pos if reverse else (l_len - 1.0 - pos)
    return jnp.exp(lg * steps), steps


def _ctx_state_fwd(projc, lgv):
    l_len = projc.shape[0]

    def body(k_ref, v_ref, lg_ref, sf_ref, sb_ref):
        k = k_ref[...]
        vb = v_ref[...].astype(BF16)
        for d, o_ref in ((0, sf_ref), (1, sb_ref)):
            w, _ = _ctx_weights(lg_ref[0, d:d + 1, :], l_len, d == 1)
            o_ref[0] = _dot_tn((k * w).astype(BF16), vb)

    st = pl.BlockSpec((1, DH, DH), lambda h: (h, 0, 0))
    return _pc(body, name="ctx_state_fwd", grid=(HEADS,),
               in_specs=[pl.BlockSpec((l_len, DH), lambda h: (0, HEADS + h)),
                         pl.BlockSpec((l_len, DH), lambda h: (0, 2 * HEADS + h)),
                         pl.BlockSpec((1, 2, LANES), lambda h: (h, 0, 0))],
               out_specs=[st, st], out_shape=[_sds((HEADS, DH, DH))] * 2,
               compiler_params=_params("arbitrary"))(projc, projc, lgv)


def _ctx_state_bwd(projc, lgv, sgv, dsf, dsb):
    l_len = projc.shape[0]

    def body(k_ref, v_ref, lg_ref, sg_ref, dsf_ref, dsb_ref, dk_ref, dv_ref, drd_ref):
        k = k_ref[...]
        vb = v_ref[...].astype(BF16)
        dk = jnp.zeros((l_len, DH), F32)
        dv = jnp.zeros((l_len, DH), F32)
        rows = []
        for d, ds_ref in ((0, dsf_ref), (1, dsb_ref)):
            w, steps = _ctx_weights(lg_ref[0, d:d + 1, :], l_len, d == 1)
            dsb16 = ds_ref[0].astype(BF16)
            dkw = _dot_nt(vb, dsb16)
            dk = dk + dkw * w
            dv = dv + _dot((k * w).astype(BF16), dsb16)
            tot = jnp.sum(_sum0(dkw * k * w * steps), axis=1, keepdims=True)
            rows.append(jnp.broadcast_to(tot, (1, LANES)) * sg_ref[0, d:d + 1, :])
        dk_ref[...] = dk.astype(BF16)
        dv_ref[...] = dv.astype(BF16)
        rid = lax.broadcasted_iota(jnp.int32, (SUBLANES, LANES), 0)
        drd_ref[0] = jnp.where(rid == 0, rows[0], jnp.where(rid == 1, rows[1], 0.0))

    st = pl.BlockSpec((1, DH, DH), lambda h: (h, 0, 0))
    lane = pl.BlockSpec((1, 2, LANES), lambda h: (h, 0, 0))
    hc = pl.BlockSpec((l_len, DH), lambda h: (0, h))
    return _pc(body, name="ctx_state_bwd", grid=(HEADS,),
               in_specs=[pl.BlockSpec((l_len, DH), lambda h: (0, HEADS + h)),
                         pl.BlockSpec((l_len, DH), lambda h: (0, 2 * HEADS + h)), lane, lane, st, st],
               out_specs=[hc, hc, pl.BlockSpec((1, SUBLANES, LANES), lambda h: (h, 0, 0))],
               out_shape=[_sds((l_len, RET_W), BF16), _sds((l_len, RET_W), BF16), _sds((HEADS, SUBLANES, LANES))],
               compiler_params=_params("arbitrary"))(projc, projc, lgv, sgv, dsf, dsb)


G_BLOCK = (3 * RET_W) // RET_W
GATE_BLOCK = (4 * RET_W + LRU_W) // LRU_W


def _head_norm(y):
    yc = y - jnp.mean(y, axis=-1, keepdims=True)
    rs = lax.rsqrt(jnp.mean(yc * yc, axis=-1, keepdims=True) + EPS)
    return yc * rs, rs


def _gelu_parts(z):
    th = jnp.tanh(GELU_K * (z + GELU_C * z * z * z))
    return 0.5 * z * (1.0 + th), th


def _mix_fwd(o_f, o_b, proj, hf, hb, w_out, x, g1, comms):
    t = x.shape[0]
    tm = _tile(t, True)

    def body(of_ref, ob_ref, g_ref, gt_ref, hf_ref, hb_ref, w_ref, x_ref, g1_ref, x1_ref, cat_ref):
        o = of_ref[...] + ob_ref[...]
        g = g_ref[...].astype(F32)
        for hh in range(HEADS):
            sl = slice(DH * hh, DH * (hh + 1))
            nrm, _ = _head_norm(o[:, sl])
            gh = g[:, sl]
            cat_ref[:, sl] = (gh * _sigmoid(gh) * nrm).astype(BF16)
        gel, _ = _gelu_parts(gt_ref[...].astype(F32))
        cat_ref[:, RET_W:] = ((hf_ref[...] + hb_ref[...]) * gel).astype(BF16)
        x1_ref[...] = x_ref[...] + g1_ref[...] * _dot(cat_ref[...], w_ref[...])

    half = pl.BlockSpec((tm, RET_W), lambda i: (i, 0))
    big = pl.BlockSpec((tm, D_MODEL), lambda i: (i, 0))
    return _call(body, name="mix_fwd", grid=(t // tm,),
                 in_specs=[half, half, pl.BlockSpec((tm, RET_W), lambda i: (i, G_BLOCK)),
                           pl.BlockSpec((tm, LRU_W), lambda i: (i, GATE_BLOCK)), half, half,
                           _full((D_MODEL, D_MODEL)), big, _full((1, D_MODEL))],
                 out_specs=[big, big], out_shape=[_sds((t, D_MODEL)), _sds((t, D_MODEL), BF16)],
                 scratch_shapes=[], sem=("arbitrary",), args=(o_f, o_b, proj, proj, hf, hb, w_out, x, g1),
                 comms=comms)


def _mix_bwd(o_f, o_b, proj, hf, hb, w_out, cat, dx1, g1, comms=()):
    t = dx1.shape[0]
    tm = _tile(t, True)

    def body(of_ref, ob_ref, g_ref, gt_ref, hf_ref, hb_ref, w_ref, cat_ref, dx1_ref, g1_ref,
             do_ref, dhs_ref, dg_ref, dgt_ref, dyb_ref, dg1_ref):
        dx1v = dx1_ref[...]
        y = _dot(cat_ref[...], w_ref[...])

        @pl.when(pl.program_id(0) == 0)
        def _():
            dg1_ref[...] = jnp.zeros_like(dg1_ref)
        dg1_ref[...] += _sum0(dx1v * y)
        dyb = (g1_ref[...] * dx1v).astype(BF16)
        dyb_ref[...] = dyb
        dcat = _dot_nt(dyb, w_ref[...])
        o = of_ref[...] + ob_ref[...]
        g = g_ref[...].astype(F32)
        for hh in range(HEADS):
            sl = slice(DH * hh, DH * (hh + 1))
            nrm, rs = _head_norm(o[:, sl])
            gh = g[:, sl]
            sg = _sigmoid(gh)
            dret = dcat[:, sl]
            dg_ref[:, sl] = (dret * nrm * (sg * (1.0 + gh * (1.0 - sg)))).astype(BF16)
            dn = dret * (gh * sg)
            dyc = rs * (dn - nrm * jnp.mean(dn * nrm, axis=-1, keepdims=True))
            do_ref[:, sl] = (dyc - jnp.mean(dyc, axis=-1, keepdims=True)).astype(BF16)
        z = gt_ref[...].astype(F32)
        gel, th = _gelu_parts(z)
        dlru = dcat[:, RET_W:]
        dhs_ref[...] = dlru * gel
        dgel = 0.5 * (1.0 + th) + 0.5 * z * (1.0 - th * th) * GELU_K * (1.0 + 3.0 * GELU_C * z * z)
        dgt_ref[...] = (dlru * (hf_ref[...] + hb_ref[...]) * dgel).astype(BF16)

    half = pl.BlockSpec((tm, RET_W), lambda i: (i, 0))
    big = pl.BlockSpec((tm, D_MODEL), lambda i: (i, 0))
    return _call(body, name="mix_bwd", grid=(t // tm,),
                 in_specs=[half, half, pl.BlockSpec((tm, RET_W), lambda i: (i, G_BLOCK)),
                           pl.BlockSpec((tm, LRU_W), lambda i: (i, GATE_BLOCK)), half, half,
                           _full((D_MODEL, D_MODEL)), big, big, _full((1, D_MODEL))],
                 out_specs=[half, half, half, half, big, _full((1, D_MODEL))],
                 out_shape=[_sds((t, RET_W), BF16), _sds((t, RET_W)), _sds((t, RET_W), BF16), _sds((t, RET_W), BF16),
                            _sds((t, D_MODEL), BF16), _sds((1, D_MODEL))],
                 scratch_shapes=[], sem=("arbitrary",), args=(o_f, o_b, proj, proj, hf, hb, w_out, cat, dx1, g1),
                 comms=comms)


def _mlp(x1, n2g, sh2, sc2, g2, fg, w1_parts, w2_parts, tgt):
    t = x1.shape[0]
    tm = _tile(t)
    hb_ = MLP_H // N_CHIP
    q_rows = hb_ // 4
    n_cp = 4 * N_DEV

    def body(x1_ref, n2g_ref, sh2_ref, sc2_ref, g2_ref, fg_ref, w1a, w1b, w2a, w2b, tgt_ref,
             dx1_ref, h2b_ref, ab_ref, dub_ref, dmb_ref, dsc_ref, dsh_ref, dg2_ref, dn2_ref, dfg_ref, loss_ref,
             w1_s, w2_s, r_s, sems):
        @pl.when(pl.program_id(0) == 0)
        def _():
            cps = []
            for p, parts in enumerate(((w1a, w2a), (w1b, w2b))):
                for d in range(N_DEV):
                    rows = pl.ds(2 * q_rows * (d % 2) + q_rows * p, q_rows)
                    for src, dst in zip(parts, (w1_s, w2_s)):
                        cps.append(pltpu.make_async_copy(src.at[d], dst.at[d // 2, rows], sems.at[len(cps)]))
            for n, cp in enumerate(cps):
                cp.start(priority=n % 2)
            for r in (dsc_ref, dsh_ref, dg2_ref, dn2_ref, dfg_ref, loss_ref):
                r[...] = jnp.zeros_like(r)
            for cp in cps:
                cp.wait()
        x1v = x1_ref[...]
        n2g, sc2, g2, fg = n2g_ref[...], sc2_ref[...], g2_ref[...], fg_ref[...]
        xh, _ = _rms(x1v)
        h2b = (xh * n2g * (1.0 + sc2) + sh2_ref[...]).astype(BF16)
        h2b_ref[...] = h2b
        m = jnp.zeros((tm, D_MODEL), F32)
        for j in range(N_CHIP):
            sl = slice(hb_ * j, hb_ * (j + 1))
            r = jnp.maximum(_dot(h2b, w1_s[j]), 0.0)
            r_s[:, sl] = r
            ab = (r * r).astype(BF16)
            ab_ref[:, sl] = ab
            m = m + _dot(ab, w2_s[j])
        x2 = x1v + g2 * m
        x2h, r2 = _rms(x2)
        err = x2h * fg - tgt_ref[...]
        loss_ref[...] += _sum0(err * err)
        dout = err * (1.0 / D_MODEL)
        dfg_ref[...] += _sum0(dout * x2h)
        dxh = dout * fg
        dx2 = r2 * (dxh - x2h * jnp.mean(dxh * x2h, axis=-1, keepdims=True))
        dg2_ref[...] += _sum0(dx2 * m)
        dmb = (g2 * dx2).astype(BF16)
        dmb_ref[...] = dmb
        dh2 = jnp.zeros((tm, D_MODEL), F32)
        for j in range(N_CHIP):
            sl = slice(hb_ * j, hb_ * (j + 1))
            dub = (_dot_nt(dmb, w2_s[j]) * (2.0 * r_s[:, sl])).astype(BF16)
            dub_ref[:, sl] = dub
            dh2 = dh2 + _dot_nt(dub, w1_s[j])
        dx, dn2_t, dsh_t, dsc_t = _norm_mod_bwd(x1v, n2g, sc2, dh2)
        dx1_ref[...] = dx2 + dx
        dn2_ref[...] += dn2_t
        dsh_ref[...] += dsh_t
        dsc_ref[...] += dsc_t

        @pl.when(pl.program_id(0) == t // tm - 1)
        def _():
            tot = jnp.sum(loss_ref[...], axis=1, keepdims=True) * (0.5 / D_MODEL)
            loss_ref[...] = jnp.broadcast_to(tot, loss_ref.shape)

    row = _full((1, D_MODEL))
    big = pl.BlockSpec((tm, D_MODEL), lambda i: (i, 0))
    wide = pl.BlockSpec((tm, MLP_H), lambda i: (i, 0))
    return _pc(body, name="mlp", grid=(t // tm,),
               in_specs=[big, row, row, row, row, row, ANY, ANY, ANY, ANY, big],
               out_specs=[big, big, wide, wide, big, row, row, row, row, row, row],
               out_shape=[_sds((t, D_MODEL)), _sds((t, D_MODEL), BF16), _sds((t, MLP_H), BF16), _sds((t, MLP_H), BF16),
                          _sds((t, D_MODEL), BF16)] + [_sds((1, D_MODEL))] * 6,
               scratch_shapes=[pltpu.VMEM((N_CHIP, D_MODEL, hb_), BF16), pltpu.VMEM((N_CHIP, hb_, D_MODEL), BF16),
                               pltpu.VMEM((tm, MLP_H), F32), pltpu.SemaphoreType.DMA((n_cp,))],
               compiler_params=_params("arbitrary"))(x1, n2g, sh2, sc2, g2, fg, *w1_parts, *w2_parts, tgt)


def _tn(a, b, nj, a_blocked, b_blocked, name, extra=None, comms=()):
    t = a.shape[0]
    m = a.shape[1] // (nj if a_blocked else 1)
    n = b.shape[1] // (nj if b_blocked else 1)
    bk = next((b for b in (2048, 1024, 512) if t % b == 0), t)
    nk = t // bk
    a_col = (lambda j: j) if a_blocked else (lambda j: 0)
    b_col = (lambda j: j) if b_blocked else (lambda j: 0)
    in_specs = [pl.BlockSpec((bk, m), lambda j, k: (k, a_col(j))), pl.BlockSpec((bk, n), lambda j, k: (k, b_col(j)))]
    args = [a, b]
    if extra is not None:
        a2, b2 = extra
        t2 = a2.shape[0]
        in_specs += [pl.BlockSpec((t2, m), lambda j, k: (0, a_col(j))),
                     pl.BlockSpec((t2, n), lambda j, k: (0, b_col(j)))]
        args += [a2, b2]

    def body(*refs):
        a_ref, b_ref = refs[0], refs[1]
        o_ref, acc = refs[-2], refs[-1]
        k = pl.program_id(1)

        @pl.when(k == 0)
        def _():
            acc[...] = jnp.zeros_like(acc)
        acc[...] += _dot_tn(a_ref[...].astype(BF16), b_ref[...].astype(BF16))

        @pl.when(k == nk - 1)
        def _():
            if extra is not None:
                acc[...] += _dot_tn(refs[2][...].astype(BF16), refs[3][...].astype(BF16))
            o_ref[0] = acc[...]

    (out,), couts = _call(body, name=name, grid=(nj, nk), in_specs=in_specs,
                          out_specs=[pl.BlockSpec((1, m, n), lambda j, k: (j, 0, 0))], out_shape=[_sds((nj, m, n))],
                          scratch_shapes=[pltpu.VMEM((m, n), F32)], sem=("arbitrary", "arbitrary"), args=args,
                          comms=comms)
    return (out, couts) if comms else out


ROW_LOSS = 0
ROW_DMOD = 1
ROW_DMODC = 7
ROW_N1, ROW_N2, ROW_FG, ROW_CB = 9, 10, 11, 12
ROW_BA, ROW_BX, ROW_LAM = 13, 15, 17
ROW_CW = 20
ROW_RD = 24
SLAB_ROWS = 32
SEG = D_MODEL // 2


def _pack_small(rows, drd, cw2, cb2, lru2, gates):
    n_rows, n_lru = len(rows), len(lru2)

    def body(*refs):
        r = refs[:n_rows]
        drd_f, drd_b, drd_c, cw_a, cw_b, cb_a, cb_b = refs[n_rows:n_rows + 7]
        lru = refs[n_rows + 7:n_rows + 7 + n_lru]
        gf_ref, gb_ref, slab, ga, gx = refs[n_rows + 7 + n_lru:]
        slab[...] = jnp.zeros_like(slab)
        slab[ROW_LOSS:ROW_LOSS + 1, :] = r[0][...]
        for k in range(N_MOD):
            slab[ROW_DMOD + k:ROW_DMOD + k + 1, :] = r[1 + k][...]
        slab[ROW_DMODC:ROW_DMODC + 1, :] = r[7][...]
        slab[ROW_DMODC + 1:ROW_DMODC + 2, :] = r[8][...]
        slab[ROW_N1:ROW_N1 + 1, :] = r[9][...] + r[10][...]
        slab[ROW_N2:ROW_N2 + 1, :] = r[11][...]
        slab[ROW_FG:ROW_FG + 1, :] = r[12][...]
        slab[ROW_CB:ROW_CB + 1, 0:LRU_W] = cb_a[...] + cb_b[...]
        for k, row in enumerate((ROW_BA, ROW_BA + 1, ROW_BX, ROW_BX + 1, ROW_LAM, ROW_LAM + 1)):
            slab[row:row + 1, 0:LRU_W] = lru[2 * k][...] + lru[2 * k + 1][...]
        slab[ROW_CW:ROW_CW + 4, 0:LRU_W] = cw_a[...] + cw_b[...]
        for h in range(HEADS):
            slab[ROW_RD + h:ROW_RD + h + 1, 0:LANES] = drd_f[h, 0:1, :] + drd_c[h, 0:1, :]
            slab[ROW_RD + HEADS + h:ROW_RD + HEADS + h + 1, 0:LANES] = drd_b[h, 0:1, :] + drd_c[h, 1:2, :]
        for d, g_ref in enumerate((gf_ref, gb_ref)):
            for n in range(LRU_BLOCKS):
                blk = slice(LRU_BD * n, LRU_BD * (n + 1))
                ga[blk, LRU_BD * d:LRU_BD * (d + 1)] = g_ref[0, blk, blk].astype(BF16)
                gx[blk, LRU_BD * d:LRU_BD * (d + 1)] = g_ref[1, blk, blk].astype(BF16)

    args = list(rows) + list(drd) + list(cw2) + list(cb2) + list(lru2) + list(gates)
    gate_shape = (LRU_W, 2 * LRU_BD)
    return _pc(body, name="pack_small", in_specs=[_full(a.shape) for a in args],
               out_specs=[_full((SLAB_ROWS, D_MODEL)), _full(gate_shape), _full(gate_shape)],
               out_shape=[_sds((SLAB_ROWS, D_MODEL)), _sds(gate_shape, BF16), _sds(gate_shape, BF16)],
               compiler_params=_params())(*args)


def _adam_math(w, g, m, v):
    mn = ADAM_B1 * m + (1.0 - ADAM_B1) * g
    vn = ADAM_B2 * v + (1.0 - ADAM_B2) * (g * g)
    mh = mn / (1.0 - ADAM_B1 ** ADAM_STEP)
    vh = vn / (1.0 - ADAM_B2 ** ADAM_STEP)
    return -ADAM_LR * (mh / (jnp.sqrt(vh) + ADAM_EPS) + ADAM_WD * w), mn, vn


SMALL_PARAMS = ("b_ada", "norm1_g", "norm2_g", "final_g", "ret_decay", "conv_w", "conv_b", "lru_wa", "lru_ba", "lru_wx",
                "lru_bx", "lru_lambda")


def _finalize_small(chip_idx, slab_all, ga_all, gx_all, wmv):
    n_p = len(SMALL_PARAMS)
    flat = [a for nm in SMALL_PARAMS for a in wmv[nm]]
    ada_n = N_MOD * D_MODEL // N_CHIP

    def body(c_ref, slab_ref, ga_ref, gx_ref, *refs):
        prm = {nm: refs[3 * k:3 * k + 3] for k, nm in enumerate(SMALL_PARAMS)}
        outs = {nm: refs[3 * n_p + 4 * k:3 * n_p + 4 * k + 4] for k, nm in enumerate(SMALL_PARAMS)}
        b128_ref, dmc_ref, loss_ref = refs[3 * n_p + 4 * n_p:]
        chip = c_ref[0]

        def pick(fn):
            acc = fn(0)
            for j in range(1, N_CHIP):
                acc = jnp.where(chip == j, fn(j), acc)
            return acc

        tot = slab_ref[0]
        for d in range(1, N_DEV):
            tot = tot + slab_ref[d]

        def update(nm, g, sl=None, rows=None):
            w_ref, m_ref, v_ref = prm[nm]
            g_ref, d_ref, mo_ref, vo_ref = outs[nm]
            ix = (slice(None) if rows is None else rows, slice(None) if sl is None else sl)
            dl, mn, vn = _adam_math(w_ref[ix], g, m_ref[ix], v_ref[ix])
            g_ref[ix] = g
            d_ref[ix] = dl
            mo_ref[ix] = mn
            vo_ref[ix] = vn

        loss_ref[...] = jnp.broadcast_to(tot[ROW_LOSS:ROW_LOSS + 1, 0:LANES], (SUBLANES, LANES))
        for k in range(N_MOD):
            g = tot[ROW_DMOD + k:ROW_DMOD + k + 1, :]
            if k < 2:
                g = g + tot[ROW_DMODC + k:ROW_DMODC + k + 1, :]
            update("b_ada", g, slice(D_MODEL * k, D_MODEL * (k + 1)))
        update("norm1_g", tot[ROW_N1:ROW_N1 + 1, :])
        update("norm2_g", tot[ROW_N2:ROW_N2 + 1, :])
        update("final_g", tot[ROW_FG:ROW_FG + 1, :])
        update("ret_decay", tot[ROW_RD:ROW_RD + SUBLANES, 0:LANES])
        update("conv_b", tot[ROW_CB:ROW_CB + 1, 0:LRU_W])
        update("conv_w", pick(lambda j: tot[ROW_CW:ROW_CW + 4, LANES * j:LANES * (j + 1)]))
        for nm, row in (("lru_ba", ROW_BA), ("lru_bx", ROW_BX), ("lru_lambda", ROW_LAM)):
            update(nm, pick(lambda j, row=row: tot[row:row + 2, LANES * j:LANES * (j + 1)]))
        for nm, g_all in (("lru_wa", ga_ref), ("lru_wx", gx_ref)):
            for dr in range(2):
                lanes = slice(LRU_BD * dr, LRU_BD * (dr + 1))
                g = g_all[0, :, lanes].astype(F32)
                for d in range(1, N_DEV):
                    g = g + g_all[d, :, lanes].astype(F32)
                update(nm, g, rows=slice(LRU_W * dr, LRU_W * (dr + 1)))

        def seg(rows6, s):
            return rows6[s // 2][:, SEG * (s % 2):SEG * (s % 2 + 1)]

        b128_ref[...] = jnp.zeros_like(b128_ref)
        dmc_ref[...] = jnp.zeros_like(dmc_ref)
        zero = jnp.zeros((1, D_MODEL), F32)
        ctx6 = [tot[ROW_DMODC:ROW_DMODC + 1, :], tot[ROW_DMODC + 1:ROW_DMODC + 2, :]] + [zero] * (N_MOD - 2)
        for q in range(ada_n // SEG):
            cols = slice(SEG * q, SEG * (q + 1))
            for d in range(N_DEV):
                rows6 = [slab_ref[d, ROW_DMOD + k:ROW_DMOD + k + 1, :] for k in range(N_MOD)]
                b128_ref[d:d + 1, cols] = pick(lambda j, rows6=rows6: seg(rows6, 3 * j + q))
            c = pick(lambda j: seg(ctx6, 3 * j + q))
            b128_ref[N_DEV:N_DEV + 1, cols] = c
            dmc_ref[0:1, cols] = c

    out_shape = []
    for nm in SMALL_PARAMS:
        out_shape += [_sds(wmv[nm][0].shape)] * 4
    out_shape += [_sds((LANES, ada_n)), _sds((SUBLANES, ada_n)), _sds((SUBLANES, LANES))]
    args = [slab_all, ga_all, gx_all] + flat
    grid_spec = pltpu.PrefetchScalarGridSpec(
        num_scalar_prefetch=1, grid=(1,), in_specs=[_full(a.shape) for a in args],
        out_specs=[_full(s.shape) for s in out_shape])
    outs = _pc(body, name="finalize_small", grid_spec=grid_spec, out_shape=out_shape,
               compiler_params=_params("arbitrary"))(chip_idx, *args)
    res = {nm: tuple(outs[4 * k:4 * k + 4]) for k, nm in enumerate(SMALL_PARAMS)}
    return res, outs[4 * n_p], outs[4 * n_p + 1], outs[4 * n_p + 2]


def _block_diag(w):
    eye = jnp.eye(LRU_BLOCKS, dtype=F32)
    return (w[:, :, None, :] * eye[:, None, :, None]).reshape(LRU_W, LRU_W).astype(BF16)


def _lane_rep(v8):
    return jnp.broadcast_to(v8.reshape(SUBLANES, 1), (SUBLANES, LANES))


def kernel(x, c, ctx, c_ctx, w_ada, b_ada, norm1_g, norm2_g, w_in, ret_decay, conv_w, conv_b, lru_wa, lru_ba, lru_wx, lru_bx, lru_lambda, w_out, w_mlp1, w_mlp2, final_g, loss_target, m_c_ctx, m_w_ada, m_b_ada, m_norm1_g, m_norm2_g, m_w_in, m_ret_decay, m_conv_w, m_conv_b, m_lru_wa, m_lru_ba, m_lru_wx, m_lru_bx, m_lru_lambda, m_w_out, m_w_mlp1, m_w_mlp2, m_final_g, v_c_ctx, v_w_ada, v_b_ada, v_norm1_g, v_norm2_g, v_w_in, v_ret_decay, v_conv_w, v_conv_b, v_lru_wa, v_lru_ba, v_lru_wx, v_lru_bx, v_lru_lambda, v_w_out, v_w_mlp1, v_w_mlp2, v_final_g):
    ax, ay, ac = lax.axis_index("x"), lax.axis_index("y"), lax.axis_index("c")
    chip = 2 * ax + ay
    dev = 4 * ax + 2 * ay + ac
    c_idx = jnp.stack([ac, chip]).astype(jnp.int32)
    j_idx = chip.reshape(1).astype(jnp.int32)

    xt = x[0]
    t_len = xt.shape[0]
    ctxt = ctx[0]
    l_len = ctxt.shape[0]
    tgt = loss_target[0]
    ada_n = w_ada.shape[2]

    def my_half(w2d):
        r = w2d.shape[0] // 2
        return lax.dynamic_slice_in_dim(w2d, ac * r, r, axis=0).astype(BF16)

    pad8 = lambda a: jnp.pad(a, ((0, SUBLANES - a.shape[0]), (0, 0)))
    small = jnp.concatenate([pad8(conv_w[0]), pad8(lru_ba[0]), pad8(lru_bx[0]), pad8(lru_lambda[0])], axis=0)
    b_shard = lax.dynamic_slice_in_dim(b_ada, chip * ada_n, ada_n, axis=1)
    gw_in, _, small_all, a16, mod_parts, lgv, sgv = _head(
        my_half(w_in[0]), pad8(c), small, w_ada[0], b_shard, c_ctx, ret_decay[0])
    w4 = gw_in.reshape(N_CHIP, D_MODEL, IN_COLS // N_CHIP)

    mod_all = mod_parts[0::2].transpose(1, 0, 2).reshape(16, N_CHIP * ada_n)
    mod_me = lax.dynamic_slice_in_dim(mod_all, dev, 1, axis=0)
    sh1, sc1, g1, sh2, sc2, g2 = [mod_me[:, D_MODEL * k:D_MODEL * (k + 1)] for k in range(N_MOD)]
    csh1, csc1 = mod_all[8:9, 0:D_MODEL], mod_all[8:9, D_MODEL:2 * D_MODEL]

    cos2, sin2 = _rotary_tables(t_len)
    cos_c, sin_c = jnp.ones((l_len, DH), F32), jnp.zeros((l_len, DH), F32)
    n1g, n2g = norm1_g, norm2_g
    fg = final_g.reshape(1, D_MODEL)

    small_full = small_all[0::2].transpose(1, 0, 2).reshape(4 * SUBLANES, LRU_W)
    cw = small_full[0:4]
    cb = conv_b
    ba_f, ba_b = small_full[8:9], small_full[9:10]
    bx_f, bx_b = small_full[16:17], small_full[17:18]
    lam_f, lam_b = small_full[24:25], small_full[25:26]
    wa_f, wa_b = _block_diag(lru_wa[0, 0]), _block_diag(lru_wa[0, 1])
    wx_f, wx_b = _block_diag(lru_wx[0, 0]), _block_diag(lru_wx[0, 1])
    zero_h = jnp.zeros((1, LRU_W), F32)

    projc, xrc, hcb16 = _inproj_fwd(ctxt, n1g, csh1, csc1, w4, cos_c, sin_c, "inproj_fwd_ctx")
    s_f, s_b = _ctx_state_fwd(projc, lgv)
    xcc = _conv_fwd(xrc, cw, cb, "conv_fwd_ctx")
    par_f, par_b = (wa_f, wx_f, ba_f, bx_f, lam_f), (wa_b, wx_b, ba_b, bx_b, lam_b)
    hcf, hcbk = _lru_fwd(xcc, par_f, par_b, zero_h, zero_h, "lru_fwd_ctx")
    lru_sf, lru_sb = hcf[l_len - 1:l_len], hcbk[0:1]

    h1, h2 = my_half(w_mlp1[0]), my_half(w_mlp2[0])
    q = h1.shape[0] // 2
    (proj, xrl, hb16), ((gw_1a,),) = _inproj_fwd(xt, n1g, sh1, sc1, w4, cos2, sin2, "inproj_fwd",
                                           comms=(_AllGather([h1[:q]]),))
    (o_f, o_b, spf, spb), ((gw_1b, gw_out),) = _ret_fwd(proj, lgv, s_f, s_b,
                                                       comms=(_AllGather([h1[q:], my_half(w_out[0])]),))
    xcl = _conv_fwd(xrl, cw, cb, "conv_fwd")
    (hf, hbk), ((gw_2a,),) = _lru_fwd(xcl, par_f, par_b, lru_sf, lru_sb, "lru_fwd", comms=(_AllGather([h2[:q]]),))
    wo = gw_out.reshape(D_MODEL, D_MODEL)
    (x1, cat), ((gw_2b,),) = _mix_fwd(o_f, o_b, proj, hf, hbk, wo, xt, g1, comms=(_AllGather([h2[q:]]),))

    (dx1, h2b, ab, dub, dmb, dsc2, dsh2, dg2, dn2g, dfg, lossv) = _mlp(
        x1, n2g, sh2, sc2, g2, fg, (gw_1a, gw_1b), (gw_2a, gw_2b), tgt)
    gw_mlp1 = _tn(h2b, dub, N_CHIP, False, True, "grad_w_mlp1")
    b_1 = gw_mlp1.reshape(N_DEV, D_MODEL // 2, MLP_H // N_CHIP)
    gw_mlp2, ((r_1,),) = _tn(ab, dmb, N_CHIP, True, False, "grad_w_mlp2", comms=(_pair_exchange([b_1]),))

    half = D_MODEL // 4
    top, bot = (0, half), (half, half)
    b_2 = gw_mlp2.reshape(N_DEV, MLP_H // N_DEV, D_MODEL)
    p_1, pb_1 = _pair_add(b_1, r_1, c_idx, "rs_pair_add_w_mlp1")
    (do, dhs, dg, dgate, dyb, dg1), ((q_1a,), (r_2,)) = _mix_bwd(
        o_f, o_b, proj, hf, hbk, wo, cat, dx1, g1, comms=(_chip_exchange([pb_1], top), _pair_exchange([b_2])))
    gw_o = _tn(cat, dyb, 1, False, False, "grad_w_out")
    b_o = gw_o.reshape(N_DEV, D_MODEL // N_DEV, D_MODEL)
    p_2, pb_2 = _pair_add(b_2, r_2, c_idx, "rs_pair_add_w_mlp2")

    ((dq_f, dk_f, dv_f, ds_f, drd_f), (dq_b, dk_b, dv_b, ds_b, drd_b)), ((q_1b,), (q_2a,), (r_o,)) = _ret_bwd(
        proj, lgv, sgv, spf, spb, do,
        comms=(_chip_exchange([pb_1], bot), _chip_exchange([pb_2], top), _pair_exchange([b_o])))
    p_o, pb_o = _pair_add(b_o, r_o, c_idx, "rs_pair_add_w_out")
    h_1 = _chip_add(p_1, (q_1a, q_1b), c_idx, "rs_chip_add_w_mlp1")

    ((dxc_f, dpre_f, dba_f, dbx_f, dlam_f, dh0_f), (dxc_b, dpre_b, dba_b, dbx_b, dlam_b, dh0_b)), (
        (q_2b,), (q_o,), (f_1,)) = _lru_bwd(
        xcl, par_f, par_b, hf, hbk, lru_sf, lru_sb, dhs, dhs, "lru_bwd",
        comms=(_chip_exchange([pb_2], bot), _chip_exchange([pb_o]), _pair_gather([h_1])))
    h_2 = _chip_add(p_2, (q_2a, q_2b), c_idx, "rs_chip_add_w_mlp2")
    h_o = _chip_add(p_o, (q_o,), c_idx, "rs_chip_add_w_out")
    dxr, dcw, dcb = _conv_bwd(dxc_f, dxc_b, xrl, cw, "conv_bwd")
    grad_x, dpb, dn1g, dsh1, dsc1 = _inproj_bwd(
        xt, n1g, sh1, sc1, w4, cos2, sin2, [dq_f, dq_b, dk_f, dk_b, dv_f, dv_b, dg, dxr, dgate], dx1, "inproj_bwd")

    dkc, dvc, drd_c = _ctx_state_bwd(projc, lgv, sgv, ds_f, ds_b)
    zc = jnp.zeros((l_len, LRU_W), F32)
    dhc_f = lax.dynamic_update_slice(zc, dh0_f, (l_len - 1, 0))
    dhc_b = lax.dynamic_update_slice(zc, dh0_b, (0, 0))
    ((dxcc_f, dprec_f, dbac_f, dbxc_f, dlamc_f, _), (dxcc_b, dprec_b, dbac_b, dbxc_b, dlamc_b, _)), _ = _lru_bwd(
        xcc, par_f, par_b, hcf, hcbk, zero_h, zero_h, dhc_f, dhc_b, "lru_bwd_ctx")
    dxrc, dcw_c, dcb_c = _conv_bwd(dxcc_f, dxcc_b, xrc, cw, "conv_bwd_ctx")
    zr = jnp.zeros((l_len, RET_W), BF16)
    _, dpbc, dn1g_c, dcsh1, dcsc1 = _inproj_bwd(
        ctxt, n1g, csh1, csc1, w4, cos_c, sin_c, [zr, zr, dkc, zr, dvc, zr, zr, dxrc, zr],
        jnp.zeros((l_len, D_MODEL), F32), "inproj_bwd_ctx")

    gw_i = _tn(hb16, dpb, N_CHIP, False, True, "grad_w_in", extra=(hcb16, dpbc))
    b_i = gw_i.reshape(N_DEV, D_MODEL // 2, IN_COLS // N_CHIP)
    gwa_f, ((r_i,), (f_2,), (f_o,)) = _tn(xcl, dpre_f, 2, False, True, "grad_lru_gates_f", extra=(xcc, dprec_f),
                                          comms=(_pair_exchange([b_i]), _pair_gather([h_2]), _pair_gather([h_o])))
    p_i, pb_i = _pair_add(b_i, r_i, c_idx, "rs_pair_add_w_in")
    gwa_b, ((q_i,),) = _tn(xcl, dpre_b, 2, False, True, "grad_lru_gates_b", extra=(xcc, dprec_b),
                           comms=(_chip_exchange([pb_i]),))
    slab, ga, gx = _pack_small(
        [lossv, dsh1, dsc1, dg1, dsh2, dsc2, dg2, dcsh1, dcsc1, dn1g, dn1g_c, dn2g, dfg],
        (drd_f, drd_b, drd_c), (dcw, dcw_c), (dcb, dcb_c),
        (dba_f, dbac_f, dba_b, dbac_b, dbx_f, dbxc_f, dbx_b, dbxc_b, dlam_f, dlamc_f, dlam_b, dlamc_b),
        (gwa_f, gwa_b))
    (f_i,), (slab_all, ga_all, gx_all) = _run_comms(
        [_pair_gather([_chip_add(p_i, (q_i,), c_idx, "rs_chip_add_w_in")]), _AllGather([slab, ga, gx])],
        "tail_exchanges")
    g_in, g_out, g_1, g_2 = _shard_of(f_i), _shard_of(f_o), _shard_of(f_1), _shard_of(f_2)
    big = {}
    for nm, w, g, m, v in (("w_in", w_in, g_in, m_w_in, v_w_in), ("w_out", w_out, g_out, m_w_out, v_w_out),
                           ("w_mlp1", w_mlp1, g_1, m_w_mlp1, v_w_mlp1), ("w_mlp2", w_mlp2, g_2, m_w_mlp2, v_w_mlp2)):
        go, d_, mn, vn = _adamw(w[0], g, m[0], v[0], "adamw_" + nm)
        big[nm] = (go[None], d_[None], mn[None], vn[None])
    params = {
        "b_ada": (b_ada, m_b_ada, v_b_ada), "norm1_g": (norm1_g, m_norm1_g, v_norm1_g),
        "norm2_g": (norm2_g, m_norm2_g, v_norm2_g), "final_g": (final_g, m_final_g, v_final_g),
        "ret_decay": (ret_decay, m_ret_decay, v_ret_decay), "conv_w": (conv_w, m_conv_w, v_conv_w),
        "conv_b": (conv_b, m_conv_b, v_conv_b), "lru_wa": (lru_wa, m_lru_wa, v_lru_wa),
        "lru_ba": (lru_ba, m_lru_ba, v_lru_ba), "lru_wx": (lru_wx, m_lru_wx, v_lru_wx),
        "lru_bx": (lru_bx, m_lru_bx, v_lru_bx), "lru_lambda": (lru_lambda, m_lru_lambda, v_lru_lambda),
    }
    as2d = {
        "b_ada": lambda a: a, "norm1_g": lambda a: a, "norm2_g": lambda a: a, "conv_b": lambda a: a,
        "final_g": lambda a: a.reshape(1, D_MODEL), "ret_decay": lambda a: _lane_rep(a.reshape(-1)),
        "conv_w": lambda a: a[0], "lru_ba": lambda a: a[0], "lru_bx": lambda a: a[0], "lru_lambda": lambda a: a[0],
        "lru_wa": lambda a: a.reshape(2 * LRU_W, LRU_BD), "lru_wx": lambda a: a.reshape(2 * LRU_W, LRU_BD),
    }
    res, b128, dmc8, loss8 = _finalize_small(
        j_idx, slab_all, ga_all, gx_all, {nm: tuple(as2d[nm](a) for a in params[nm]) for nm in SMALL_PARAMS})
    loss = loss8[0, 0]
    small_out = {}
    for nm in SMALL_PARAMS:
        shp = params[nm][0].shape
        if nm == "ret_decay":
            small_out[nm] = tuple(o[:, 0].reshape(shp) for o in res[nm])
        else:
            small_out[nm] = tuple(o.reshape(shp) for o in res[nm])

    g_ada = _ada_grad(jnp.pad(a16.T, ((0, 0), (0, LANES - 16))), b128)
    g_ada, d_ada, m_ada, v_ada = _adamw(w_ada[0], g_ada, m_w_ada[0], v_w_ada[0], "adamw_w_ada")

    (cparts,) = _all_gather([_cctx_partial(dmc8, w_ada[0])], "gather_cctx")
    g_cc, d_cc, m_cc, v_cc = _cctx_final(cparts, c_ctx, m_c_ctx, v_c_ctx)
    small_out["c_ctx"] = tuple(a.reshape(D_MODEL) for a in (g_cc, d_cc, m_cc, v_cc))
    small_out["w_ada"] = (g_ada[None], d_ada[None], m_ada[None], v_ada[None])
    small_out.update(big)

    order = ["c_ctx", "w_ada", "b_ada", "norm1_g", "norm2_g", "w_in", "ret_decay", "conv_w", "conv_b", "lru_wa", "lru_ba",
             "lru_wx", "lru_bx", "lru_lambda", "w_out", "w_mlp1", "w_mlp2", "final_g"]
    outs = [loss, grad_x[None]]
    for k in range(4):
        outs += [small_out[nm][k] for nm in order]
    return tuple(outs)
```
